```python
import jax, jax.numpy as jnp
from jax import lax
import numpy as np

D_MODEL = 1024
BATCH = 8
SEQ = 4096
DEPTH = 1

N_MEM = 256
MIX_WIDTH = D_MODEL
FOX_HEADS = 8
FOX_HEAD_DIM = 64
FOX_WIDTH = FOX_HEADS * FOX_HEAD_DIM
GMLP_GROUPS = 8
GMLP_GROUP_DIM = 64
GMLP_WIDTH = GMLP_GROUPS * GMLP_GROUP_DIM
CHUNK = 128
Q_BLOCK = 128
Q_OFF = 0
K_OFF = Q_OFF + FOX_WIDTH
V_OFF = K_OFF + FOX_WIDTH
F_OFF = V_OFF + FOX_WIDTH
UV_OFF = F_OFF + FOX_HEADS
IN_COLS = UV_OFF + 2 * GMLP_WIDTH
CA_HEADS = 4
CA_HEAD_DIM = D_MODEL // CA_HEADS
D_FF = 2816
EPS = 1e-6

kernel_name = "hybrid_fox_gmlp_macaron_memxattn"


def rms_norm(x, g):
    xf = x.astype(jnp.float32)
    y = xf * lax.rsqrt(jnp.mean(xf * xf, axis=-1, keepdims=True) + EPS)
    return (y * g.astype(jnp.float32)).astype(x.dtype)


def swiglu(h, w_in, w_out):
    gu = h @ w_in
    g, u = jnp.split(gu, 2, axis=-1)
    return (jax.nn.silu(g) * u) @ w_out


def fox_attention(q, k, v, log_f):
    S = q.shape[2]
    scale = FOX_HEAD_DIM ** -0.5
    c = jnp.cumsum(log_f, axis=-1)
    neg = jnp.finfo(jnp.float32).min
    outs = []
    for i in range(S // Q_BLOCK):
        q0, q1 = i * Q_BLOCK, (i + 1) * Q_BLOCK
        qb, kb, vb = q[:, :, q0:q1], k[:, :, :q1], v[:, :, :q1]
        s = (jnp.einsum('bhqd,bhkd->bhqk', qb, kb).astype(jnp.float32) * scale
             + c[:, :, q0:q1, None] - c[:, :, None, :q1])
        mask = (q0 + jnp.arange(Q_BLOCK))[:, None] >= jnp.arange(q1)[None, :]
        s = jnp.where(mask, s, neg)
        p = jax.nn.softmax(s, axis=-1)
        outs.append(jnp.einsum('bhqk,bhkd->bhqd', p.astype(vb.dtype), vb))
    return jnp.concatenate(outs, axis=2)


def spatial_gating(u, v, w_s, b_s):
    B, S, G, Dg = v.shape
    vc = v.reshape(B, S // CHUNK, CHUNK, G, Dg)
    tril = jnp.tril(jnp.ones((CHUNK, CHUNK), dtype=bool))
    w = jnp.where(tril[None], w_s, jnp.zeros_like(w_s))
    mixed = jnp.einsum('gts,bcsgd->bctgd', w, vc) + b_s.T[None, None, :, :, None]
    return u * mixed.reshape(B, S, G, Dg)


def mem_cross_attention(h, m, w_cq, w_ckv, g_cq, g_ck, w_co):
    B, S, _ = h.shape
    M = m.shape[1]
    q = (h @ w_cq).reshape(B, S, CA_HEADS, CA_HEAD_DIM)
    kv = m @ w_ckv
    k, v = jnp.split(kv, 2, axis=-1)
    k = k.reshape(B, M, CA_HEADS, CA_HEAD_DIM)
    v = v.reshape(B, M, CA_HEADS, CA_HEAD_DIM)
    q = rms_norm(q, g_cq)
    k = rms_norm(k, g_ck)
    s = jnp.einsum('bqhd,bkhd->bhqk', q, k).astype(jnp.float32) * (CA_HEAD_DIM ** -0.5)
    p = jax.nn.softmax(s, axis=-1)
    o = jnp.einsum('bhqk,bkhd->bqhd', p.astype(v.dtype), v).reshape(B, S, D_MODEL)
    return o @ w_co


def _fwd_setup_inputs(seed: int = 0) -> dict:
    key = jax.random.key(seed)
    ks = jax.random.split(key, 32)
    L = DEPTH

    def w(k, shape, fan_in):
        return jax.random.normal(k, shape, jnp.float32) * (fan_in ** -0.5)

    def g(k, shape):
        return 1.0 + 0.02 * jax.random.normal(k, shape, jnp.float32)

    b_f = (jnp.linspace(1.0, 6.0, FOX_HEADS, dtype=jnp.float32)[None, :]
           + 0.1 * jax.random.normal(ks[7], (L, FOX_HEADS), jnp.float32))
    return {
        "x": jax.random.normal(ks[0], (BATCH, SEQ, D_MODEL), jnp.float32),
        "mem": jax.random.normal(ks[1], (BATCH, N_MEM, D_MODEL), jnp.float32),
        "g_ffn1": g(ks[2], (L, D_MODEL)),
        "w_ffn1_in": w(ks[3], (L, D_MODEL, 2 * D_FF), D_MODEL),
        "w_ffn1_out": w(ks[4], (L, D_FF, D_MODEL), D_FF),
        "g_mix": g(ks[5], (L, D_MODEL)),
        "w_in": w(ks[6], (L, D_MODEL, IN_COLS), D_MODEL),
        "b_f": b_f,
        "g_q": g(ks[8], (L, FOX_HEAD_DIM)),
        "g_k": g(ks[9], (L, FOX_HEAD_DIM)),
        "g_sgu": g(ks[10], (L, GMLP_WIDTH)),
        "w_s": w(ks[11], (L, GMLP_GROUPS, CHUNK, CHUNK), CHUNK),
        "b_s": g(ks[12], (L, GMLP_GROUPS, CHUNK)),
        "g_fox_o": g(ks[13], (L, FOX_WIDTH)),
        "g_gmlp_o": g(ks[14], (L, GMLP_WIDTH)),
        "w_out": w(ks[15], (L, MIX_WIDTH, D_MODEL), MIX_WIDTH),
        "g_ca": g(ks[16], (L, D_MODEL)),
        "g_mem": g(ks[17], (L, D_MODEL)),
        "w_cq": w(ks[18], (L, D_MODEL, D_MODEL), D_MODEL),
        "w_ckv": w(ks[19], (L, D_MODEL, 2 * D_MODEL), D_MODEL),
        "g_cq": g(ks[20], (L, CA_HEAD_DIM)),
        "g_ck": g(ks[21], (L, CA_HEAD_DIM)),
        "w_co": w(ks[22], (L, D_MODEL, D_MODEL), D_MODEL),
        "g_ffn2": g(ks[23], (L, D_MODEL)),
        "w_ffn2_in": w(ks[24], (L, D_MODEL, 2 * D_FF), D_MODEL),
        "w_ffn2_out": w(ks[25], (L, D_FF, D_MODEL), D_FF),
    }


def _fwd_reference(x, mem, g_ffn1, w_ffn1_in, w_ffn1_out, g_mix, w_in, b_f, g_q, g_k,
              g_sgu, w_s, b_s, g_fox_o, g_gmlp_o, w_out, g_ca, g_mem, w_cq, w_ckv,
              g_cq, g_ck, w_co, g_ffn2, w_ffn2_in, w_ffn2_out):
    B, S, _ = x.shape
    for l in range(DEPTH):
        x = x + 0.5 * swiglu(rms_norm(x, g_ffn1[l]), w_ffn1_in[l], w_ffn1_out[l])

        h = rms_norm(x, g_mix[l])
        z = h @ w_in[l]
        q = z[..., Q_OFF:K_OFF].reshape(B, S, FOX_HEADS, FOX_HEAD_DIM)
        k = z[..., K_OFF:V_OFF].reshape(B, S, FOX_HEADS, FOX_HEAD_DIM)
        v = z[..., V_OFF:F_OFF].reshape(B, S, FOX_HEADS, FOX_HEAD_DIM)
        f_logit = z[..., F_OFF:UV_OFF]
        uv = z[..., UV_OFF:]

        q = rms_norm(q, g_q[l])
        k = rms_norm(k, g_k[l])
        log_f = jax.nn.log_sigmoid(f_logit.astype(jnp.float32) + b_f[l].astype(jnp.float32))
        attn = fox_attention(q.transpose(0, 2, 1, 3), k.transpose(0, 2, 1, 3),
                             v.transpose(0, 2, 1, 3), log_f.transpose(0, 2, 1))
        attn = attn.transpose(0, 2, 1, 3).reshape(B, S, FOX_WIDTH)

        uv = jax.nn.gelu(uv)
        u, vg = jnp.split(uv, 2, axis=-1)
        vg = rms_norm(vg, g_sgu[l])
        sgu = spatial_gating(u.reshape(B, S, GMLP_GROUPS, GMLP_GROUP_DIM),
                             vg.reshape(B, S, GMLP_GROUPS, GMLP_GROUP_DIM),
                             w_s[l], b_s[l]).reshape(B, S, GMLP_WIDTH)

        y = jnp.concatenate([rms_norm(attn, g_fox_o[l]), rms_norm(sgu, g_gmlp_o[l])], axis=-1)
        x = x + y @ w_out[l]

        x = x + mem_cross_attention(rms_norm(x, g_ca[l]), rms_norm(mem, g_mem[l]),
                                    w_cq[l], w_ckv[l], g_cq[l], g_ck[l], w_co[l])

        x = x + 0.5 * swiglu(rms_norm(x, g_ffn2[l]), w_ffn2_in[l], w_ffn2_out[l])
    return x


import jax as _jax
import jax.numpy as _jnp

TWIN_FORMAT = 'train_step'
FWD_PARAMS = ['x', 'mem', 'g_ffn1', 'w_ffn1_in', 'w_ffn1_out', 'g_mix', 'w_in', 'b_f', 'g_q', 'g_k', 'g_sgu', 'w_s', 'b_s', 'g_fox_o', 'g_gmlp_o', 'w_out', 'g_ca', 'g_mem', 'w_cq', 'w_ckv', 'g_cq', 'g_ck', 'w_co', 'g_ffn2', 'w_ffn2_in', 'w_ffn2_out']
TWIN_WEIGHTS = ['g_ffn1', 'w_ffn1_in', 'w_ffn1_out', 'g_mix', 'w_in', 'b_f', 'g_q', 'g_k', 'g_sgu', 'w_s', 'b_s', 'g_fox_o', 'g_gmlp_o', 'w_out', 'g_ca', 'g_mem', 'w_cq', 'w_ckv', 'g_cq', 'g_ck', 'w_co', 'g_ffn2', 'w_ffn2_in', 'w_ffn2_out']
TWIN_DIFF_INPUT = 'x'
TWIN_INPUTS = ['x', 'mem', 'g_ffn1', 'w_ffn1_in', 'w_ffn1_out', 'g_mix', 'w_in', 'b_f', 'g_q', 'g_k', 'g_sgu', 'w_s', 'b_s', 'g_fox_o', 'g_gmlp_o', 'w_out', 'g_ca', 'g_mem', 'w_cq', 'w_ckv', 'g_cq', 'g_ck', 'w_co', 'g_ffn2', 'w_ffn2_in', 'w_ffn2_out', 'loss_target', 'm_g_ffn1', 'm_w_ffn1_in', 'm_w_ffn1_out', 'm_g_mix', 'm_w_in', 'm_b_f', 'm_g_q', 'm_g_k', 'm_g_sgu', 'm_w_s', 'm_b_s', 'm_g_fox_o', 'm_g_gmlp_o', 'm_w_out', 'm_g_ca', 'm_g_mem', 'm_w_cq', 'm_w_ckv', 'm_g_cq', 'm_g_ck', 'm_w_co', 'm_g_ffn2', 'm_w_ffn2_in', 'm_w_ffn2_out', 'v_g_ffn1', 'v_w_ffn1_in', 'v_w_ffn1_out', 'v_g_mix', 'v_w_in', 'v_b_f', 'v_g_q', 'v_g_k', 'v_g_sgu', 'v_w_s', 'v_b_s', 'v_g_fox_o', 'v_g_gmlp_o', 'v_w_out', 'v_g_ca', 'v_g_mem', 'v_w_cq', 'v_w_ckv', 'v_g_cq', 'v_g_ck', 'v_w_co', 'v_g_ffn2', 'v_w_ffn2_in', 'v_w_ffn2_out']
TWIN_OUTPUTS = ['loss', 'grad_x', 'grad_g_ffn1', 'grad_w_ffn1_in', 'grad_w_ffn1_out', 'grad_g_mix', 'grad_w_in', 'grad_b_f', 'grad_g_q', 'grad_g_k', 'grad_g_sgu', 'grad_w_s', 'grad_b_s', 'grad_g_fox_o', 'grad_g_gmlp_o', 'grad_w_out', 'grad_g_ca', 'grad_g_mem', 'grad_w_cq', 'grad_w_ckv', 'grad_g_cq', 'grad_g_ck', 'grad_w_co', 'grad_g_ffn2', 'grad_w_ffn2_in', 'grad_w_ffn2_out', 'delta_g_ffn1', 'delta_w_ffn1_in', 'delta_w_ffn1_out', 'delta_g_mix', 'delta_w_in', 'delta_b_f', 'delta_g_q', 'delta_g_k', 'delta_g_sgu', 'delta_w_s', 'delta_b_s', 'delta_g_fox_o', 'delta_g_gmlp_o', 'delta_w_out', 'delta_g_ca', 'delta_g_mem', 'delta_w_cq', 'delta_w_ckv', 'delta_g_cq', 'delta_g_ck', 'delta_w_co', 'delta_g_ffn2', 'delta_w_ffn2_in', 'delta_w_ffn2_out', 'new_m_g_ffn1', 'new_m_w_ffn1_in', 'new_m_w_ffn1_out', 'new_m_g_mix', 'new_m_w_in', 'new_m_b_f', 'new_m_g_q', 'new_m_g_k', 'new_m_g_sgu', 'new_m_w_s', 'new_m_b_s', 'new_m_g_fox_o', 'new_m_g_gmlp_o', 'new_m_w_out', 'new_m_g_ca', 'new_m_g_mem', 'new_m_w_cq', 'new_m_w_ckv', 'new_m_g_cq', 'new_m_g_ck', 'new_m_w_co', 'new_m_g_ffn2', 'new_m_w_ffn2_in', 'new_m_w_ffn2_out', 'new_v_g_ffn1', 'new_v_w_ffn1_in', 'new_v_w_ffn1_out', 'new_v_g_mix', 'new_v_w_in', 'new_v_b_f', 'new_v_g_q', 'new_v_g_k', 'new_v_g_sgu', 'new_v_w_s', 'new_v_b_s', 'new_v_g_fox_o', 'new_v_g_gmlp_o', 'new_v_w_out', 'new_v_g_ca', 'new_v_g_mem', 'new_v_w_cq', 'new_v_w_ckv', 'new_v_g_cq', 'new_v_g_ck', 'new_v_w_co', 'new_v_g_ffn2', 'new_v_w_ffn2_in', 'new_v_w_ffn2_out']
TWIN_LEAF_KINDS = {'loss': 'loss', 'grad_x': 'grad_x', 'grad_g_ffn1': 'grad_w', 'grad_w_ffn1_in': 'grad_w', 'grad_w_ffn1_out': 'grad_w', 'grad_g_mix': 'grad_w', 'grad_w_in': 'grad_w', 'grad_b_f': 'grad_w', 'grad_g_q': 'grad_w', 'grad_g_k': 'grad_w', 'grad_g_sgu': 'grad_w', 'grad_w_s': 'grad_w', 'grad_b_s': 'grad_w', 'grad_g_fox_o': 'grad_w', 'grad_g_gmlp_o': 'grad_w', 'grad_w_out': 'grad_w', 'grad_g_ca': 'grad_w', 'grad_g_mem': 'grad_w', 'grad_w_cq': 'grad_w', 'grad_w_ckv': 'grad_w', 'grad_g_cq': 'grad_w', 'grad_g_ck': 'grad_w', 'grad_w_co': 'grad_w', 'grad_g_ffn2': 'grad_w', 'grad_w_ffn2_in': 'grad_w', 'grad_w_ffn2_out': 'grad_w', 'delta_g_ffn1': 'delta_w', 'delta_w_ffn1_in': 'delta_w', 'delta_w_ffn1_out': 'delta_w', 'delta_g_mix': 'delta_w', 'delta_w_in': 'delta_w', 'delta_b_f': 'delta_w', 'delta_g_q': 'delta_w', 'delta_g_k': 'delta_w', 'delta_g_sgu': 'delta_w', 'delta_w_s': 'delta_w', 'delta_b_s': 'delta_w', 'delta_g_fox_o': 'delta_w', 'delta_g_gmlp_o': 'delta_w', 'delta_w_out': 'delta_w', 'delta_g_ca': 'delta_w', 'delta_g_mem': 'delta_w', 'delta_w_cq': 'delta_w', 'delta_w_ckv': 'delta_w', 'delta_g_cq': 'delta_w', 'delta_g_ck': 'delta_w', 'delta_w_co': 'delta_w', 'delta_g_ffn2': 'delta_w', 'delta_w_ffn2_in': 'delta_w', 'delta_w_ffn2_out': 'delta_w', 'new_m_g_ffn1': 'new_m', 'new_m_w_ffn1_in': 'new_m', 'new_m_w_ffn1_out': 'new_m', 'new_m_g_mix': 'new_m', 'new_m_w_in': 'new_m', 'new_m_b_f': 'new_m', 'new_m_g_q': 'new_m', 'new_m_g_k': 'new_m', 'new_m_g_sgu': 'new_m', 'new_m_w_s': 'new_m', 'new_m_b_s': 'new_m', 'new_m_g_fox_o': 'new_m', 'new_m_g_gmlp_o': 'new_m', 'new_m_w_out': 'new_m', 'new_m_g_ca': 'new_m', 'new_m_g_mem': 'new_m', 'new_m_w_cq': 'new_m', 'new_m_w_ckv': 'new_m', 'new_m_g_cq': 'new_m', 'new_m_g_ck': 'new_m', 'new_m_w_co': 'new_m', 'new_m_g_ffn2': 'new_m', 'new_m_w_ffn2_in': 'new_m', 'new_m_w_ffn2_out': 'new_m', 'new_v_g_ffn1': 'new_v', 'new_v_w_ffn1_in': 'new_v', 'new_v_w_ffn1_out': 'new_v', 'new_v_g_mix': 'new_v', 'new_v_w_in': 'new_v', 'new_v_b_f': 'new_v', 'new_v_g_q': 'new_v', 'new_v_g_k': 'new_v', 'new_v_g_sgu': 'new_v', 'new_v_w_s': 'new_v', 'new_v_b_s': 'new_v', 'new_v_g_fox_o': 'new_v', 'new_v_g_gmlp_o': 'new_v', 'new_v_w_out': 'new_v', 'new_v_g_ca': 'new_v', 'new_v_g_mem': 'new_v', 'new_v_w_cq': 'new_v', 'new_v_w_ckv': 'new_v', 'new_v_g_cq': 'new_v', 'new_v_g_ck': 'new_v', 'new_v_w_co': 'new_v', 'new_v_g_ffn2': 'new_v', 'new_v_w_ffn2_in': 'new_v', 'new_v_w_ffn2_out': 'new_v'}


def _forward(args):
    return _fwd_reference(*[args[k] for k in FWD_PARAMS])


def _output_shape():
    def fwd():
        inp = _fwd_setup_inputs(0)
        return _fwd_reference(*[inp[k] for k in FWD_PARAMS])
    out = _jax.eval_shape(fwd)
    return out.shape, out.dtype

N_MICROBATCH = 1
ADAM_LR = 0.001
ADAM_B1 = 0.9
ADAM_B2 = 0.999
ADAM_EPS = 1e-08
ADAM_WD = 0.01
ADAM_STEP = 10
PER_EXAMPLE_BATCH_AXIS = {'x': 0, 'mem': 0, 'loss_target': 0}
SHARED_INPUTS = []
_WEIGHT_DTYPES = {'g_ffn1': _jnp.float32, 'w_ffn1_in': _jnp.float32, 'w_ffn1_out': _jnp.float32, 'g_mix': _jnp.float32, 'w_in': _jnp.float32, 'b_f': _jnp.float32, 'g_q': _jnp.float32, 'g_k': _jnp.float32, 'g_sgu': _jnp.float32, 'w_s': _jnp.float32, 'b_s': _jnp.float32, 'g_fox_o': _jnp.float32, 'g_gmlp_o': _jnp.float32, 'w_out': _jnp.float32, 'g_ca': _jnp.float32, 'g_mem': _jnp.float32, 'w_cq': _jnp.float32, 'w_ckv': _jnp.float32, 'g_cq': _jnp.float32, 'g_ck': _jnp.float32, 'w_co': _jnp.float32, 'g_ffn2': _jnp.float32, 'w_ffn2_in': _jnp.float32, 'w_ffn2_out': _jnp.float32}
MOMENT_SCALE = {'g_ffn1': 6.183251e+00, 'w_ffn1_in': 1.308950e-01, 'w_ffn1_out': 2.270840e-01, 'g_mix': 8.026417e-01, 'w_in': 4.525701e-01, 'b_f': 4.322933e+00, 'g_q': 7.429130e-01, 'g_k': 7.411374e-01, 'g_sgu': 4.930046e-01, 'w_s': 1.350155e-01, 'b_s': 2.183841e-01, 'g_fox_o': 4.142436e+01, 'g_gmlp_o': 3.211447e+01, 'w_out': 2.372267e+00, 'g_ca': 8.063103e-02, 'g_mem': 4.222668e-01, 'w_cq': 7.808099e-02, 'w_ckv': 1.441694e-01, 'g_cq': 1.314213e+00, 'g_ck': 1.315629e+00, 'w_co': 1.901120e-01, 'g_ffn2': 6.201283e+00, 'w_ffn2_in': 1.557117e-01, 'w_ffn2_out': 2.504159e-01}


def _to_microbatches(a, axis):
    t = _jnp.moveaxis(a, axis, 0)
    t = t.reshape((N_MICROBATCH, t.shape[0] // N_MICROBATCH) + t.shape[1:])
    return _jnp.moveaxis(t, 1, axis + 1)


def setup_inputs(seed: int = 0) -> dict:
    inp = _fwd_setup_inputs(seed)
    key = _jax.random.fold_in(_jax.random.key(seed), 7919)
    shape, _ = _output_shape()
    out = dict(inp)
    out["loss_target"] = _jax.random.normal(_jax.random.fold_in(key, 0), shape, _jnp.float32)
    for i, name in enumerate(TWIN_WEIGHTS):
        w = inp[name].astype(_jnp.float32)
        if MOMENT_SCALE is None:
            s = _jnp.sqrt(_jnp.mean(_jnp.square(w)) + 1e-30)
        else:
            s = MOMENT_SCALE[name]
        km, kv = _jax.random.split(_jax.random.fold_in(key, i + 1))
        out[name] = w
        out["m_" + name] = s * _jax.random.normal(km, w.shape, _jnp.float32)
        out["v_" + name] = (s * s) * _jax.random.uniform(kv, w.shape, _jnp.float32, 0.5, 1.5)
    if N_MICROBATCH > 1:
        for name, axis in PER_EXAMPLE_BATCH_AXIS.items():
            out[name] = _to_microbatches(out[name], axis)
    return {'x': out['x'], 'mem': out['mem'], 'g_ffn1': out['g_ffn1'], 'w_ffn1_in': out['w_ffn1_in'], 'w_ffn1_out': out['w_ffn1_out'], 'g_mix': out['g_mix'], 'w_in': out['w_in'], 'b_f': out['b_f'], 'g_q': out['g_q'], 'g_k': out['g_k'], 'g_sgu': out['g_sgu'], 'w_s': out['w_s'], 'b_s': out['b_s'], 'g_fox_o': out['g_fox_o'], 'g_gmlp_o': out['g_gmlp_o'], 'w_out': out['w_out'], 'g_ca': out['g_ca'], 'g_mem': out['g_mem'], 'w_cq': out['w_cq'], 'w_ckv': out['w_ckv'], 'g_cq': out['g_cq'], 'g_ck': out['g_ck'], 'w_co': out['w_co'], 'g_ffn2': out['g_ffn2'], 'w_ffn2_in': out['w_ffn2_in'], 'w_ffn2_out': out['w_ffn2_out'], 'loss_target': out['loss_target'], 'm_g_ffn1': out['m_g_ffn1'], 'm_w_ffn1_in': out['m_w_ffn1_in'], 'm_w_ffn1_out': out['m_w_ffn1_out'], 'm_g_mix': out['m_g_mix'], 'm_w_in': out['m_w_in'], 'm_b_f': out['m_b_f'], 'm_g_q': out['m_g_q'], 'm_g_k': out['m_g_k'], 'm_g_sgu': out['m_g_sgu'], 'm_w_s': out['m_w_s'], 'm_b_s': out['m_b_s'], 'm_g_fox_o': out['m_g_fox_o'], 'm_g_gmlp_o': out['m_g_gmlp_o'], 'm_w_out': out['m_w_out'], 'm_g_ca': out['m_g_ca'], 'm_g_mem': out['m_g_mem'], 'm_w_cq': out['m_w_cq'], 'm_w_ckv': out['m_w_ckv'], 'm_g_cq': out['m_g_cq'], 'm_g_ck': out['m_g_ck'], 'm_w_co': out['m_w_co'], 'm_g_ffn2': out['m_g_ffn2'], 'm_w_ffn2_in': out['m_w_ffn2_in'], 'm_w_ffn2_out': out['m_w_ffn2_out'], 'v_g_ffn1': out['v_g_ffn1'], 'v_w_ffn1_in': out['v_w_ffn1_in'], 'v_w_ffn1_out': out['v_w_ffn1_out'], 'v_g_mix': out['v_g_mix'], 'v_w_in': out['v_w_in'], 'v_b_f': out['v_b_f'], 'v_g_q': out['v_g_q'], 'v_g_k': out['v_g_k'], 'v_g_sgu': out['v_g_sgu'], 'v_w_s': out['v_w_s'], 'v_b_s': out['v_b_s'], 'v_g_fox_o': out['v_g_fox_o'], 'v_g_gmlp_o': out['v_g_gmlp_o'], 'v_w_out': out['v_w_out'], 'v_g_ca': out['v_g_ca'], 'v_g_mem': out['v_g_mem'], 'v_w_cq': out['v_w_cq'], 'v_w_ckv': out['v_w_ckv'], 'v_g_cq': out['v_g_cq'], 'v_g_ck': out['v_g_ck'], 'v_w_co': out['v_w_co'], 'v_g_ffn2': out['v_g_ffn2'], 'v_w_ffn2_in': out['v_w_ffn2_in'], 'v_w_ffn2_out': out['v_w_ffn2_out']}


def _loss(weights, diff, rest, loss_target):
    with _jax.named_scope("forward"):
        args = {**rest, TWIN_DIFF_INPUT: diff, **{k: w.astype(_WEIGHT_DTYPES[k]) for k, w in weights.items()}}
        y = _forward(args)
    with _jax.named_scope("loss_head"):
        err = _jnp.square(y.astype(_jnp.float32) - loss_target)
        return 0.5 * _jnp.sum(_jnp.mean(err, axis=-1)) if err.ndim else 0.5 * err


def _adamw(w, g, m, v):
    m = ADAM_B1 * m + (1.0 - ADAM_B1) * g
    v = ADAM_B2 * v + (1.0 - ADAM_B2) * _jnp.square(g)
    m_hat = m / (1.0 - ADAM_B1 ** ADAM_STEP)
    v_hat = v / (1.0 - ADAM_B2 ** ADAM_STEP)
    delta = -ADAM_LR * (m_hat / (_jnp.sqrt(v_hat) + ADAM_EPS) + ADAM_WD * w)
    return delta, m, v


def reference(x, mem, g_ffn1, w_ffn1_in, w_ffn1_out, g_mix, w_in, b_f, g_q, g_k, g_sgu, w_s, b_s, g_fox_o, g_gmlp_o, w_out, g_ca, g_mem, w_cq, w_ckv, g_cq, g_ck, w_co, g_ffn2, w_ffn2_in, w_ffn2_out, loss_target, m_g_ffn1, m_w_ffn1_in, m_w_ffn1_out, m_g_mix, m_w_in, m_b_f, m_g_q, m_g_k, m_g_sgu, m_w_s, m_b_s, m_g_fox_o, m_g_gmlp_o, m_w_out, m_g_ca, m_g_mem, m_w_cq, m_w_ckv, m_g_cq, m_g_ck, m_w_co, m_g_ffn2, m_w_ffn2_in, m_w_ffn2_out, v_g_ffn1, v_w_ffn1_in, v_w_ffn1_out, v_g_mix, v_w_in, v_b_f, v_g_q, v_g_k, v_g_sgu, v_w_s, v_b_s, v_g_fox_o, v_g_gmlp_o, v_w_out, v_g_ca, v_g_mem, v_w_cq, v_w_ckv, v_g_cq, v_g_ck, v_w_co, v_g_ffn2, v_w_ffn2_in, v_w_ffn2_out):
    given = dict(x=x, mem=mem, g_ffn1=g_ffn1, w_ffn1_in=w_ffn1_in, w_ffn1_out=w_ffn1_out, g_mix=g_mix, w_in=w_in, b_f=b_f, g_q=g_q, g_k=g_k, g_sgu=g_sgu, w_s=w_s, b_s=b_s, g_fox_o=g_fox_o, g_gmlp_o=g_gmlp_o, w_out=w_out, g_ca=g_ca, g_mem=g_mem, w_cq=w_cq, w_ckv=w_ckv, g_cq=g_cq, g_ck=g_ck, w_co=w_co, g_ffn2=g_ffn2, w_ffn2_in=w_ffn2_in, w_ffn2_out=w_ffn2_out, loss_target=loss_target, m_g_ffn1=m_g_ffn1, m_w_ffn1_in=m_w_ffn1_in, m_w_ffn1_out=m_w_ffn1_out, m_g_mix=m_g_mix, m_w_in=m_w_in, m_b_f=m_b_f, m_g_q=m_g_q, m_g_k=m_g_k, m_g_sgu=m_g_sgu, m_w_s=m_w_s, m_b_s=m_b_s, m_g_fox_o=m_g_fox_o, m_g_gmlp_o=m_g_gmlp_o, m_w_out=m_w_out, m_g_ca=m_g_ca, m_g_mem=m_g_mem, m_w_cq=m_w_cq, m_w_ckv=m_w_ckv, m_g_cq=m_g_cq, m_g_ck=m_g_ck, m_w_co=m_w_co, m_g_ffn2=m_g_ffn2, m_w_ffn2_in=m_w_ffn2_in, m_w_ffn2_out=m_w_ffn2_out, v_g_ffn1=v_g_ffn1, v_w_ffn1_in=v_w_ffn1_in, v_w_ffn1_out=v_w_ffn1_out, v_g_mix=v_g_mix, v_w_in=v_w_in, v_b_f=v_b_f, v_g_q=v_g_q, v_g_k=v_g_k, v_g_sgu=v_g_sgu, v_w_s=v_w_s, v_b_s=v_b_s, v_g_fox_o=v_g_fox_o, v_g_gmlp_o=v_g_gmlp_o, v_w_out=v_w_out, v_g_ca=v_g_ca, v_g_mem=v_g_mem, v_w_cq=v_w_cq, v_w_ckv=v_w_ckv, v_g_cq=v_g_cq, v_g_ck=v_g_ck, v_w_co=v_w_co, v_g_ffn2=v_g_ffn2, v_w_ffn2_in=v_w_ffn2_in, v_w_ffn2_out=v_w_ffn2_out)
    weights = {n: given[n] for n in TWIN_WEIGHTS}
    shared = {n: given[n] for n in SHARED_INPUTS}
    per_example = {n: given[n] for n in ['x', 'mem']}
    grad_fn = _jax.value_and_grad(_loss, argnums=(0, 1))

    def one_microbatch(ex, loss_target):
        ex = dict(ex)
        diff = ex.pop(TWIN_DIFF_INPUT)
        return grad_fn(weights, diff, {**shared, **ex}, loss_target)

    if N_MICROBATCH == 1:
        loss, (grad_w, grad_x) = one_microbatch(per_example, given["loss_target"])
    else:
        def body(carry, xs):
            loss_sum, grad_sum = carry
            l_k, (gw_k, gx_k) = one_microbatch(xs[0], xs[1])
            with _jax.named_scope("update"):
                return (loss_sum + l_k, _jax.tree.map(_jnp.add, grad_sum, gw_k)), gx_k

        init = (_jnp.zeros((), _jnp.float32), _jax.tree.map(_jnp.zeros_like, weights))
        (loss, grad_w), grad_x = _jax.lax.scan(body, init, (per_example, given["loss_target"]))
    with _jax.named_scope("update"):
        delta_w, new_m, new_v = {}, {}, {}
        for n in TWIN_WEIGHTS:
            delta_w[n], new_m[n], new_v[n] = _adamw(weights[n], grad_w[n], given["m_" + n], given["v_" + n])
    return (loss, grad_x, *[grad_w[n] for n in TWIN_WEIGHTS], *[delta_w[n] for n in TWIN_WEIGHTS],
            *[new_m[n] for n in TWIN_WEIGHTS], *[new_v[n] for n in TWIN_WEIGHTS])
```

```python
import functools

import jax
import jax.numpy as jnp
from jax import lax
from jax.experimental import pallas as pl
from jax.experimental.pallas import tpu as pltpu

F32 = jnp.float32
BF = jnp.bfloat16
S = jax.ShapeDtypeStruct

N_DEV = 8
D_MODEL = 1024
FOX_HEADS, FOX_HD = 8, 64
FOX_W = 512
GMLP_G, GMLP_GD = 8, 64
GMLP_W = 512
CHUNK = 128
CA_HEADS, CA_HD = 4, 256
N_FFN_BLK = 4
ZW = 2688
Z_Q, Z_K, Z_V, Z_U, Z_G, Z_F = 0, 512, 1024, 1536, 2048, 2560
EPS = 1e-6
NEG = -1e30
LANES = 128

ADAM_LR, ADAM_B1, ADAM_B2, ADAM_EPS, ADAM_WD, ADAM_STEP = 0.001, 0.9, 0.999, 1e-08, 0.01, 10

VMEM_LIMIT = 52 * 2 ** 20


def _cp(n_axes):
    return pltpu.CompilerParams(dimension_semantics=("arbitrary",) * n_axes, vmem_limit_bytes=VMEM_LIMIT)


def _nn(a, b):
    return jnp.dot(a, b, preferred_element_type=F32)


def _nt(a, b):
    return lax.dot_general(a, b, (((1,), (1,)), ((), ())), preferred_element_type=F32)


def _tn(a, b):
    return lax.dot_general(a, b, (((0,), (0,)), ((), ())), preferred_element_type=F32)


def _hi(a, b):
    return jnp.dot(a, b, precision=lax.Precision.HIGHEST, preferred_element_type=F32)


def _rstd(x):
    return lax.rsqrt(jnp.mean(x * x, axis=-1, keepdims=True) + EPS)


def _norm_bwd(dy, x, g):
    r = _rstd(x)
    xh = x * r
    dxh = dy * g
    dx = r * (dxh - xh * jnp.mean(dxh * xh, axis=-1, keepdims=True))
    return dx, dy * xh


def _acc_rows(ref, first, val):
    srow = jnp.sum(val, axis=0, keepdims=True)

    @pl.when(first)
    def _():
        ref[...] = srow

    @pl.when(jnp.logical_not(first))
    def _():
        ref[...] += srow


def _gelu(x):
    c = 0.7978845608028654
    return 0.5 * x * (1.0 + jnp.tanh(c * (x + 0.044715 * x * x * x)))


def _gelu_grad(x):
    c = 0.7978845608028654
    t = jnp.tanh(c * (x + 0.044715 * x * x * x))
    return 0.5 * (1.0 + t) + 0.5 * x * (1.0 - t * t) * c * (1.0 + 3 * 0.044715 * x * x)


def _tile(n, pref):
    return pref if n % pref == 0 else n


def _ffn_up(name, x, g, wup):
    T, D = x.shape
    FB = wup.shape[-1]
    tm = _tile(T, 512)

    def body(x_ref, g_ref, w_ref, a_ref, h_ref):
        @pl.when(pl.program_id(1) == 0)
        def _():
            xf = x_ref[...]
            h_ref[...] = (xf * _rstd(xf) * g_ref[...]).astype(BF)

        hb = h_ref[...]
        gg = _nn(hb, w_ref[0])
        uu = _nn(hb, w_ref[1])
        a_ref[...] = (gg * jax.nn.sigmoid(gg) * uu).astype(BF)

    return pl.pallas_call(
        body, name=name, grid=(T // tm, N_FFN_BLK),
        in_specs=[pl.BlockSpec((tm, D), lambda i, j: (i, 0)),
                  pl.BlockSpec((1, D), lambda i, j: (0, 0)),
                  pl.BlockSpec((2, None, D, FB), lambda i, j: (0, j, 0, 0))],
        out_specs=[pl.BlockSpec((None, tm, FB), lambda i, j: (j, i, 0)),
                   pl.BlockSpec((tm, D), lambda i, j: (i, 0))],
        out_shape=[S((N_FFN_BLK, T, FB), BF), S((T, D), BF)],
        compiler_params=_cp(2))(x, g, wup)


def _ffn_down(name, a, wdn, x):
    _, T, FB = a.shape
    D = x.shape[1]
    tm = _tile(T, 512)

    def body(a_ref, w_ref, x_ref, o_ref):
        j = pl.program_id(1)
        p = 0.5 * _nn(a_ref[...], w_ref[...])

        @pl.when(j == 0)
        def _():
            o_ref[...] = x_ref[...] + p

        @pl.when(j > 0)
        def _():
            o_ref[...] += p

    return pl.pallas_call(
        body, name=name, grid=(T // tm, N_FFN_BLK),
        in_specs=[pl.BlockSpec((None, tm, FB), lambda i, j: (j, i, 0)),
                  pl.BlockSpec((None, FB, D), lambda i, j: (j, 0, 0)),
                  pl.BlockSpec((tm, D), lambda i, j: (i, 0))],
        out_specs=pl.BlockSpec((tm, D), lambda i, j: (i, 0)),
        out_shape=S((T, D), F32),
        compiler_params=_cp(2))(a, wdn, x)


def _ffn_down_loss(name, a, wdn, x, target):
    _, T, FB = a.shape
    D = x.shape[1]
    tm = _tile(T, 512)

    def body(a_ref, w_ref, x_ref, t_ref, d_ref, db_ref, loss_ref, acc_ref):
        i, j = pl.program_id(0), pl.program_id(1)
        p = 0.5 * _nn(a_ref[...], w_ref[...])

        @pl.when(j == 0)
        def _():
            acc_ref[...] = x_ref[...] + p

        @pl.when(j > 0)
        def _():
            acc_ref[...] += p

        @pl.when(j == N_FFN_BLK - 1)
        def _():
            diff = acc_ref[...] - t_ref[...]
            dy = diff * (1.0 / D)
            d_ref[...] = dy
            db_ref[...] = dy.astype(BF)
            sq = jnp.zeros((8, LANES), F32) + jnp.sum(diff * diff)

            @pl.when(i == 0)
            def _():
                loss_ref[...] = sq

            @pl.when(i > 0)
            def _():
                loss_ref[...] += sq

    return pl.pallas_call(
        body, name=name, grid=(T // tm, N_FFN_BLK),
        in_specs=[pl.BlockSpec((None, tm, FB), lambda i, j: (j, i, 0)),
                  pl.BlockSpec((None, FB, D), lambda i, j: (j, 0, 0)),
                  pl.BlockSpec((tm, D), lambda i, j: (i, 0)),
                  pl.BlockSpec((tm, D), lambda i, j: (i, 0))],
        out_specs=[pl.BlockSpec((tm, D), lambda i, j: (i, 0)),
                   pl.BlockSpec((tm, D), lambda i, j: (i, 0)),
                   pl.BlockSpec((8, LANES), lambda i, j: (0, 0))],
        out_shape=[S((T, D), F32), S((T, D), BF), S((8, LANES), F32)],
        scratch_shapes=[pltpu.VMEM((tm, D), F32)],
        compiler_params=_cp(2))(a, wdn, x, target)


def _ffn_bwd_act(name, dyb, h, wup, wdn):
    T, D = h.shape
    FB = wup.shape[-1]
    tm = _tile(T, 512)

    def body(d_ref, h_ref, wu_ref, wd_ref, o_ref):
        da = 0.5 * _nt(d_ref[...], wd_ref[...])
        hb = h_ref[...]
        gg = _nn(hb, wu_ref[0])
        uu = _nn(hb, wu_ref[1])
        sg = jax.nn.sigmoid(gg)
        o_ref[0] = (da * uu * (sg * (1.0 + gg * (1.0 - sg)))).astype(BF)
        o_ref[1] = (da * (gg * sg)).astype(BF)

    return pl.pallas_call(
        body, name=name, grid=(T // tm, N_FFN_BLK),
        in_specs=[pl.BlockSpec((tm, D), lambda i, j: (i, 0)),
                  pl.BlockSpec((tm, D), lambda i, j: (i, 0)),
                  pl.BlockSpec((2, None, D, FB), lambda i, j: (0, j, 0, 0)),
                  pl.BlockSpec((None, FB, D), lambda i, j: (j, 0, 0))],
        out_specs=pl.BlockSpec((2, None, tm, FB), lambda i, j: (0, j, i, 0)),
        out_shape=S((2, N_FFN_BLK, T, FB), BF),
        compiler_params=_cp(2))(dyb, h, wup, wdn)


def _ffn_dx(name, dgu, wup, x, g, dy):
    T, D = x.shape
    FB = wup.shape[-1]
    tm = _tile(T, 512)

    def body(d_ref, w_ref, x_ref, g_ref, dy_ref, dx_ref, dg_ref, acc_ref):
        i, j = pl.program_id(0), pl.program_id(1)
        p = _nt(d_ref[0], w_ref[0]) + _nt(d_ref[1], w_ref[1])

        @pl.when(j == 0)
        def _():
            acc_ref[...] = p

        @pl.when(j > 0)
        def _():
            acc_ref[...] += p

        @pl.when(j == N_FFN_BLK - 1)
        def _():
            dx, dgr = _norm_bwd(acc_ref[...], x_ref[...], g_ref[...])
            dx_ref[...] = dx + dy_ref[...]
            _acc_rows(dg_ref, i == 0, dgr)

    return pl.pallas_call(
        body, name=name, grid=(T // tm, N_FFN_BLK),
        in_specs=[pl.BlockSpec((2, None, tm, FB), lambda i, j: (0, j, i, 0)),
                  pl.BlockSpec((2, None, D, FB), lambda i, j: (0, j, 0, 0)),
                  pl.BlockSpec((tm, D), lambda i, j: (i, 0)),
                  pl.BlockSpec((1, D), lambda i, j: (0, 0)),
                  pl.BlockSpec((tm, D), lambda i, j: (i, 0))],
        out_specs=[pl.BlockSpec((tm, D), lambda i, j: (i, 0)),
                   pl.BlockSpec((1, D), lambda i, j: (0, 0))],
        out_shape=[S((T, D), F32), S((1, D), F32)],
        scratch_shapes=[pltpu.VMEM((tm, D), F32)],
        compiler_params=_cp(2))(dgu, wup, x, g, dy)


def _tn_matmul(name, a, a_spec, b, b_spec, out_shape, out_spec, grid, acc_shape, scale=1.0):
    nk = grid[1]

    def body(a_ref, b_ref, o_ref, acc_ref):
        k = pl.program_id(1)
        p = _tn(a_ref[...], b_ref[...])

        @pl.when(k == 0)
        def _():
            acc_ref[...] = p

        @pl.when(k > 0)
        def _():
            acc_ref[...] += p

        @pl.when(k == nk - 1)
        def _():
            o_ref[...] = (acc_ref[...] * scale).astype(o_ref.dtype)

    return pl.pallas_call(
        body, name=name, grid=grid, in_specs=[a_spec, b_spec], out_specs=out_spec, out_shape=out_shape,
        scratch_shapes=[pltpu.VMEM(acc_shape, F32)], compiler_params=_cp(2))(a, b)


def _ffn_dw(name, h, dgu, a, dyb):
    T, D = h.shape
    FB = a.shape[-1]
    tk = _tile(T, 512)
    nk = T // tk
    dgu8 = dgu.reshape(2 * N_FFN_BLK, T, FB)
    dwup = _tn_matmul(
        name + "_dwup", h, pl.BlockSpec((tk, D), lambda j, k: (k, 0)),
        dgu8, pl.BlockSpec((None, tk, FB), lambda j, k: (j, k, 0)),
        S((2 * N_FFN_BLK, D, FB), BF), pl.BlockSpec((None, D, FB), lambda j, k: (j, 0, 0)),
        (2 * N_FFN_BLK, nk), (D, FB))
    dwdn = _tn_matmul(
        name + "_dwdn", a, pl.BlockSpec((None, tk, FB), lambda j, k: (j, k, 0)),
        dyb, pl.BlockSpec((tk, D), lambda j, k: (k, 0)),
        S((N_FFN_BLK, FB, D), BF), pl.BlockSpec((None, FB, D), lambda j, k: (j, 0, 0)),
        (N_FFN_BLK, nk), (FB, D), scale=0.5)
    return dwup, dwdn


def _mix_proj(x, g, wz):
    T, D = x.shape
    tm = _tile(T, 256)

    def body(x_ref, g_ref, w_ref, z_ref, h_ref):
        xf = x_ref[...]
        hb = (xf * _rstd(xf) * g_ref[...]).astype(BF)
        h_ref[...] = hb
        z_ref[...] = _nn(hb, w_ref[...])

    return pl.pallas_call(
        body, name="mix_proj", grid=(T // tm,),
        in_specs=[pl.BlockSpec((tm, D), lambda i: (i, 0)),
                  pl.BlockSpec((1, D), lambda i: (0, 0)),
                  pl.BlockSpec((D, ZW), lambda i: (0, 0))],
        out_specs=[pl.BlockSpec((tm, ZW), lambda i: (i, 0)),
                   pl.BlockSpec((tm, D), lambda i: (i, 0))],
        out_shape=[S((T, ZW), F32), S((T, D), BF)],
        compiler_params=_cp(1))(x, g, wz)


def _tri(n, lower):
    r = lax.broadcasted_iota(jnp.int32, (n, n), 0)
    c = lax.broadcasted_iota(jnp.int32, (n, n), 1)
    return (r >= c) if lower else (r <= c)


def _spatial_mix(vgn_b, ws_ref, bst, tm):
    tril = _tri(CHUNK, True)
    wms = [jnp.where(tril, ws_ref[g], 0.0).astype(BF) for g in range(GMLP_G)]
    rows = []
    for c in range(tm // CHUNK):
        cols = []
        for g in range(GMLP_G):
            vs = vgn_b[c * CHUNK:(c + 1) * CHUNK, g * GMLP_GD:(g + 1) * GMLP_GD]
            cols.append(_nn(wms[g], vs) + bst[:, g:g + 1])
        rows.append(jnp.concatenate(cols, axis=1))
    return jnp.concatenate(rows, axis=0), wms


def _mix_prep(z, bf128, g_q, g_k, g_sgu, w_s, b_st, g_go):
    T = z.shape[0]
    tm = _tile(T, 256)

    def body(z_ref, bf_ref, gq_ref, gk_ref, gs_ref, ws_ref, bst_ref, go_ref,
             q_ref, k_ref, v_ref, c_ref, y_ref, carry_ref):
        i = pl.program_id(0)

        @pl.when(i == 0)
        def _():
            carry_ref[...] = jnp.zeros_like(carry_ref)

        for h in range(FOX_HEADS):
            hs = slice(h * FOX_HD, (h + 1) * FOX_HD)
            qh = z_ref[:, Z_Q + h * FOX_HD:Z_Q + (h + 1) * FOX_HD]
            kh = z_ref[:, Z_K + h * FOX_HD:Z_K + (h + 1) * FOX_HD]
            q_ref[:, hs] = (qh * _rstd(qh) * gq_ref[...] * 0.125).astype(BF)
            k_ref[:, hs] = (kh * _rstd(kh) * gk_ref[...]).astype(BF)
        v_ref[...] = z_ref[:, Z_V:Z_V + FOX_W].astype(BF)

        fl = z_ref[:, Z_F:Z_F + LANES] + bf_ref[...]
        logf = jnp.minimum(fl, 0.0) - jnp.log1p(jnp.exp(-jnp.abs(fl)))
        csum = _hi(_tri(tm, True).astype(F32), logf) + carry_ref[...]
        c_ref[...] = csum
        carry_ref[...] = csum[tm - 1:tm, :]

        u = _gelu(z_ref[:, Z_U:Z_U + GMLP_W])
        vg = _gelu(z_ref[:, Z_G:Z_G + GMLP_W])
        vgn = (vg * _rstd(vg) * gs_ref[...]).astype(BF)
        mixed, _ = _spatial_mix(vgn, ws_ref, bst_ref[...], tm)
        sgu = u * mixed
        y_ref[...] = (sgu * _rstd(sgu) * go_ref[...]).astype(BF)

    row = lambda i: (i, 0)
    fix2 = lambda i: (0, 0)
    return pl.pallas_call(
        body, name="mix_prep", grid=(T // tm,),
        in_specs=[pl.BlockSpec((tm, ZW), row),
                  pl.BlockSpec((1, LANES), fix2), pl.BlockSpec((1, FOX_HD), fix2), pl.BlockSpec((1, FOX_HD), fix2),
                  pl.BlockSpec((1, GMLP_W), fix2), pl.BlockSpec((GMLP_G, CHUNK, CHUNK), lambda i: (0, 0, 0)),
                  pl.BlockSpec((CHUNK, GMLP_G), fix2), pl.BlockSpec((1, GMLP_W), fix2)],
        out_specs=[pl.BlockSpec((tm, FOX_W), row), pl.BlockSpec((tm, FOX_W), row), pl.BlockSpec((tm, FOX_W), row),
                   pl.BlockSpec((tm, LANES), row), pl.BlockSpec((tm, GMLP_W), row)],
        out_shape=[S((T, FOX_W), BF), S((T, FOX_W), BF), S((T, FOX_W), BF), S((T, LANES), F32), S((T, GMLP_W), BF)],
        scratch_shapes=[pltpu.VMEM((1, LANES), F32)],
        compiler_params=_cp(1))(z, bf128, g_q, g_k, g_sgu, w_s, b_st, g_go)


def _fox_fwd(q, k, v, ccol, crow):
    T = q.shape[0]
    tq = _tile(T, 512)
    nq = T // tq

    def body(q_ref, k_ref, v_ref, cc_ref, cr_ref, o_ref, lse_ref, m_sc, l_sc, acc_sc):
        i, j = pl.program_id(0), pl.program_id(1)

        @pl.when(j == 0)
        def _():
            m_sc[...] = jnp.full(m_sc.shape, NEG, F32)
            l_sc[...] = jnp.zeros_like(l_sc)
            acc_sc[...] = jnp.zeros_like(acc_sc)

        def step(masked):
            cc = cc_ref[...]
            cr = cr_ref[...]
            mask = _tri(tq, True) if masked else None
            for h in range(FOX_HEADS):
                hs = slice(h * FOX_HD, (h + 1) * FOX_HD)
                s = _nt(q_ref[:, hs], k_ref[:, hs]) + (cc[:, h:h + 1] - cr[h:h + 1, :])
                if masked:
                    s = jnp.where(mask, s, NEG)
                m_prev = m_sc[h]
                m_new = jnp.maximum(m_prev, jnp.max(s, axis=1, keepdims=True))
                alpha = jnp.exp(m_prev - m_new)
                p = jnp.exp(s - m_new)
                l_sc[h] = alpha * l_sc[h] + jnp.sum(p, axis=1, keepdims=True)
                acc_sc[:, hs] = alpha * acc_sc[:, hs] + _nn(p.astype(BF), v_ref[:, hs])
                m_sc[h] = m_new

        @pl.when(j < i)
        def _():
            step(False)

        @pl.when(j == i)
        def _():
            step(True)
            lse_ref[...] = jnp.zeros_like(lse_ref)
            for h in range(FOX_HEADS):
                hs = slice(h * FOX_HD, (h + 1) * FOX_HD)
                o_ref[:, hs] = acc_sc[:, hs] / l_sc[h]
                lse_ref[:, h:h + 1] = m_sc[h] + jnp.log(l_sc[h])

    qi = lambda i, j: (i, 0)
    kj = lambda i, j: (jnp.minimum(i, j), 0)
    return pl.pallas_call(
        body, name="fox_fwd", grid=(nq, nq),
        in_specs=[pl.BlockSpec((tq, FOX_W), qi), pl.BlockSpec((tq, FOX_W), kj), pl.BlockSpec((tq, FOX_W), kj),
                  pl.BlockSpec((tq, LANES), qi), pl.BlockSpec((FOX_HEADS, tq), lambda i, j: (0, jnp.minimum(i, j)))],
        out_specs=[pl.BlockSpec((tq, FOX_W), qi), pl.BlockSpec((tq, LANES), qi)],
        out_shape=[S((T, FOX_W), F32), S((T, LANES), F32)],
        scratch_shapes=[pltpu.VMEM((FOX_HEADS, tq, 1), F32), pltpu.VMEM((FOX_HEADS, tq, 1), F32),
                        pltpu.VMEM((tq, FOX_W), F32)],
        compiler_params=_cp(2))(q, k, v, ccol, crow)


def _fox_bwd(q, k, v, dob, lse, dsum, ccol, crow):
    T = q.shape[0]
    tq = _tile(T, 512)
    nq = T // tq

    def body(q_ref, k_ref, v_ref, do_ref, lse_ref, ds_ref, cc_ref, cr_ref,
             dq_ref, dk_ref, dv_ref, dcq_ref, dck_ref):
        j, i = pl.program_id(0), pl.program_id(1)

        @pl.when(jnp.logical_and(i == 0, j == 0))
        def _():
            dq_ref[...] = jnp.zeros_like(dq_ref)
            dcq_ref[...] = jnp.zeros_like(dcq_ref)

        @pl.when(i == 0)
        def _():
            dk_ref[...] = jnp.zeros_like(dk_ref)
            dv_ref[...] = jnp.zeros_like(dv_ref)
            dck_ref[...] = jnp.zeros_like(dck_ref)

        def step(masked):
            rows = pl.ds(pl.multiple_of(i * tq, tq), tq)
            cc = cc_ref[...]
            cr = cr_ref[...]
            lse_t = lse_ref[...]
            dsum_t = ds_ref[...]
            mask = _tri(tq, True) if masked else None
            for h in range(FOX_HEADS):
                hs = slice(h * FOX_HD, (h + 1) * FOX_HD)
                qh, kh, vh, doh = q_ref[:, hs], k_ref[:, hs], v_ref[:, hs], do_ref[:, hs]
                s = _nt(qh, kh) + (cc[:, h:h + 1] - cr[h:h + 1, :])
                if masked:
                    s = jnp.where(mask, s, NEG)
                p = jnp.exp(s - lse_t[:, h:h + 1])
                dp = _nt(doh, vh)
                ds = p * (dp - dsum_t[:, h:h + 1])
                dsb = ds.astype(BF)
                dv_ref[:, hs] += _tn(p.astype(BF), doh)
                dk_ref[:, hs] += _tn(dsb, qh)
                dq_ref[rows, hs] += _nn(dsb, kh)
                dcq_ref[rows, h:h + 1] += jnp.sum(ds, axis=1, keepdims=True)
                dck_ref[h:h + 1, :] -= jnp.sum(ds, axis=0, keepdims=True)

        @pl.when(i > j)
        def _():
            step(False)

        @pl.when(i == j)
        def _():
            step(True)

    qi = lambda j, i: (jnp.maximum(i, j), 0)
    kj = lambda j, i: (j, 0)
    whole = lambda j, i: (0, 0)
    return pl.pallas_call(
        body, name="fox_bwd", grid=(nq, nq),
        in_specs=[pl.BlockSpec((tq, FOX_W), qi), pl.BlockSpec((tq, FOX_W), kj), pl.BlockSpec((tq, FOX_W), kj),
                  pl.BlockSpec((tq, FOX_W), qi), pl.BlockSpec((tq, LANES), qi), pl.BlockSpec((tq, LANES), qi),
                  pl.BlockSpec((tq, LANES), qi), pl.BlockSpec((FOX_HEADS, tq), lambda j, i: (0, j))],
        out_specs=[pl.BlockSpec((T, FOX_W), whole), pl.BlockSpec((tq, FOX_W), kj), pl.BlockSpec((tq, FOX_W), kj),
                   pl.BlockSpec((T, LANES), whole), pl.BlockSpec((FOX_HEADS, tq), lambda j, i: (0, j))],
        out_shape=[S((T, FOX_W), F32), S((T, FOX_W), F32), S((T, FOX_W), F32), S((T, LANES), F32),
                   S((FOX_HEADS, T), F32)],
        compiler_params=_cp(2))(q, k, v, dob, lse, dsum, ccol, crow)


def _mix_out(attn, yg, g_fo, wout, x):
    T, D = x.shape
    tm = _tile(T, 512)

    def body(a_ref, y_ref, g_ref, w_ref, x_ref, o_ref):
        at = a_ref[...]
        yf = (at * _rstd(at) * g_ref[...]).astype(BF)
        o_ref[...] = x_ref[...] + _nn(yf, w_ref[:FOX_W, :]) + _nn(y_ref[...], w_ref[FOX_W:, :])

    row = lambda i: (i, 0)
    return pl.pallas_call(
        body, name="mix_out", grid=(T // tm,),
        in_specs=[pl.BlockSpec((tm, FOX_W), row), pl.BlockSpec((tm, GMLP_W), row),
                  pl.BlockSpec((1, FOX_W), lambda i: (0, 0)), pl.BlockSpec((D, D), lambda i: (0, 0)),
                  pl.BlockSpec((tm, D), row)],
        out_specs=pl.BlockSpec((tm, D), row),
        out_shape=S((T, D), F32),
        compiler_params=_cp(1))(attn, yg, g_fo, wout, x)


def _mix_out_bwd(dx, attn, yg, g_fo, wout):
    T, D = dx.shape
    tm = _tile(T, 256)
    n = T // tm

    def body(dx_ref, a_ref, y_ref, g_ref, w_ref, da_ref, dsum_ref, dyg_ref, dw_ref, dg_ref, acc_ref):
        i = pl.program_id(0)
        dxb = dx_ref[...].astype(BF)
        at = a_ref[...]
        yf = (at * _rstd(at) * g_ref[...]).astype(BF)
        dy = _nt(dxb, w_ref[...])
        p_top = _tn(yf, dxb)
        p_bot = _tn(y_ref[...], dxb)

        @pl.when(i == 0)
        def _():
            acc_ref[:FOX_W, :] = p_top
            acc_ref[FOX_W:, :] = p_bot

        @pl.when(i > 0)
        def _():
            acc_ref[:FOX_W, :] += p_top
            acc_ref[FOX_W:, :] += p_bot

        @pl.when(i == n - 1)
        def _():
            dw_ref[...] = acc_ref[...].astype(BF)

        dat, dgr = _norm_bwd(dy[:, :FOX_W], at, g_ref[...])
        _acc_rows(dg_ref, i == 0, dgr)
        da_ref[...] = dat.astype(BF)
        dyg_ref[...] = dy[:, FOX_W:]
        prod = dat * at
        dsum_ref[...] = jnp.zeros_like(dsum_ref)
        for h in range(FOX_HEADS):
            dsum_ref[:, h:h + 1] = jnp.sum(prod[:, h * FOX_HD:(h + 1) * FOX_HD], axis=1, keepdims=True)

    row = lambda i: (i, 0)
    fix = lambda i: (0, 0)
    return pl.pallas_call(
        body, name="mix_out_bwd", grid=(n,),
        in_specs=[pl.BlockSpec((tm, D), row), pl.BlockSpec((tm, FOX_W), row), pl.BlockSpec((tm, GMLP_W), row),
                  pl.BlockSpec((1, FOX_W), fix), pl.BlockSpec((D, D), fix)],
        out_specs=[pl.BlockSpec((tm, FOX_W), row), pl.BlockSpec((tm, LANES), row), pl.BlockSpec((tm, GMLP_W), row),
                   pl.BlockSpec((D, D), fix), pl.BlockSpec((1, FOX_W), fix)],
        out_shape=[S((T, FOX_W), BF), S((T, LANES), F32), S((T, GMLP_W), F32), S((D, D), BF), S((1, FOX_W), F32)],
        scratch_shapes=[pltpu.VMEM((D, D), F32)],
        compiler_params=_cp(1))(dx, attn, yg, g_fo, wout)


def _mix_prep_bwd(z, dq, dk, dv, dcq, dck, dyg, bf128, g_q, g_k, g_sgu, w_s, b_st, g_go):
    T = z.shape[0]
    tm = _tile(T, 256)
    n = T // tm

    def body(z_ref, dq_ref, dk_ref, dv_ref, dcq_ref, dck_ref, dyg_ref, bf_ref, gq_ref, gk_ref, gs_ref, ws_ref,
             bst_ref, go_ref, dz_ref, dgq_ref, dgk_ref, dgs_ref, dgo_ref, dws_ref, dbst_ref, dbf_ref, carry_ref):
        i = pl.program_id(0)
        first = i == 0

        @pl.when(first)
        def _():
            carry_ref[...] = jnp.zeros_like(carry_ref)

        gq_rows, gk_rows = [], []
        for h in range(FOX_HEADS):
            hs = slice(h * FOX_HD, (h + 1) * FOX_HD)
            dqh, gqr = _norm_bwd(dq_ref[:, hs] * 0.125, z_ref[:, Z_Q + h * FOX_HD:Z_Q + (h + 1) * FOX_HD], gq_ref[...])
            dkh, gkr = _norm_bwd(dk_ref[:, hs], z_ref[:, Z_K + h * FOX_HD:Z_K + (h + 1) * FOX_HD], gk_ref[...])
            dz_ref[:, Z_Q + h * FOX_HD:Z_Q + (h + 1) * FOX_HD] = dqh.astype(BF)
            dz_ref[:, Z_K + h * FOX_HD:Z_K + (h + 1) * FOX_HD] = dkh.astype(BF)
            gq_rows.append(gqr)
            gk_rows.append(gkr)
        _acc_rows(dgq_ref, first, functools.reduce(lambda a, b: a + b, gq_rows))
        _acc_rows(dgk_ref, first, functools.reduce(lambda a, b: a + b, gk_rows))
        dz_ref[:, Z_V:Z_V + FOX_W] = dv_ref[...].astype(BF)

        dc = dcq_ref[...] + dck_ref[...]
        dlogf = _hi(_tri(tm, False).astype(F32), dc) + carry_ref[...]
        carry_ref[...] = dlogf[0:1, :]
        fl = z_ref[:, Z_F:Z_F + LANES] + bf_ref[...]
        lane = lax.broadcasted_iota(jnp.int32, (tm, LANES), 1)
        df = jnp.where(lane < FOX_HEADS, dlogf * jax.nn.sigmoid(-fl), 0.0)
        dz_ref[:, Z_F:Z_F + LANES] = df.astype(BF)
        _acc_rows(dbf_ref, first, df)

        u_pre = z_ref[:, Z_U:Z_U + GMLP_W]
        vg_pre = z_ref[:, Z_G:Z_G + GMLP_W]
        u = _gelu(u_pre)
        vg = _gelu(vg_pre)
        vgn = (vg * _rstd(vg) * gs_ref[...]).astype(BF)
        bst = bst_ref[...]
        mixed, wms = _spatial_mix(vgn, ws_ref, bst, tm)
        sgu = u * mixed
        dsgu, gor = _norm_bwd(dyg_ref[...], sgu, go_ref[...])
        _acc_rows(dgo_ref, first, gor)
        du = dsgu * mixed
        dmixed = dsgu * u
        dmb = dmixed.astype(BF)
        tril = _tri(CHUNK, True)
        dvgn_rows = []
        dws = [None] * GMLP_G
        dbs = [None] * GMLP_G
        for c in range(tm // CHUNK):
            cs = slice(c * CHUNK, (c + 1) * CHUNK)
            cols = []
            for g in range(GMLP_G):
                gs = slice(g * GMLP_GD, (g + 1) * GMLP_GD)
                dmc = dmb[cs, gs]
                pw = _nt(dmc, vgn[cs, gs])
                pb = jnp.sum(dmixed[cs, gs], axis=1, keepdims=True)
                dws[g] = pw if dws[g] is None else dws[g] + pw
                dbs[g] = pb if dbs[g] is None else dbs[g] + pb
                cols.append(_tn(wms[g], dmc))
            dvgn_rows.append(jnp.concatenate(cols, axis=1))
        dvgn = jnp.concatenate(dvgn_rows, axis=0)
        dbs_t = jnp.concatenate(dbs, axis=1)
        for g in range(GMLP_G):
            dwg = jnp.where(tril, dws[g], 0.0)

            @pl.when(first)
            def _():
                dws_ref[g] = dwg

            @pl.when(jnp.logical_not(first))
            def _():
                dws_ref[g] += dwg

        @pl.when(first)
        def _():
            dbst_ref[...] = dbs_t

        @pl.when(jnp.logical_not(first))
        def _():
            dbst_ref[...] += dbs_t

        dvg, gsr = _norm_bwd(dvgn, vg, gs_ref[...])
        _acc_rows(dgs_ref, first, gsr)
        dz_ref[:, Z_U:Z_U + GMLP_W] = (du * _gelu_grad(u_pre)).astype(BF)
        dz_ref[:, Z_G:Z_G + GMLP_W] = (dvg * _gelu_grad(vg_pre)).astype(BF)

    rev = lambda i: (n - 1 - i, 0)
    fix = lambda i: (0, 0)
    fix3 = lambda i: (0, 0, 0)
    return pl.pallas_call(
        body, name="mix_prep_bwd", grid=(n,),
        in_specs=[pl.BlockSpec((tm, ZW), rev), pl.BlockSpec((tm, FOX_W), rev), pl.BlockSpec((tm, FOX_W), rev),
                  pl.BlockSpec((tm, FOX_W), rev), pl.BlockSpec((tm, LANES), rev), pl.BlockSpec((tm, LANES), rev),
                  pl.BlockSpec((tm, GMLP_W), rev),
                  pl.BlockSpec((1, LANES), fix), pl.BlockSpec((1, FOX_HD), fix), pl.BlockSpec((1, FOX_HD), fix),
                  pl.BlockSpec((1, GMLP_W), fix), pl.BlockSpec((GMLP_G, CHUNK, CHUNK), fix3),
                  pl.BlockSpec((CHUNK, GMLP_G), fix), pl.BlockSpec((1, GMLP_W), fix)],
        out_specs=[pl.BlockSpec((tm, ZW), rev), pl.BlockSpec((1, FOX_HD), fix), pl.BlockSpec((1, FOX_HD), fix),
                   pl.BlockSpec((1, GMLP_W), fix), pl.BlockSpec((1, GMLP_W), fix),
                   pl.BlockSpec((GMLP_G, CHUNK, CHUNK), fix3), pl.BlockSpec((CHUNK, GMLP_G), fix),
                   pl.BlockSpec((1, LANES), fix)],
        out_shape=[S((T, ZW), BF), S((1, FOX_HD), F32), S((1, FOX_HD), F32), S((1, GMLP_W), F32), S((1, GMLP_W), F32),
                   S((GMLP_G, CHUNK, CHUNK), F32), S((CHUNK, GMLP_G), F32), S((1, LANES), F32)],
        scratch_shapes=[pltpu.VMEM((1, LANES), F32)],
        compiler_params=_cp(1))(z, dq, dk, dv, dcq, dck, dyg, bf128, g_q, g_k, g_sgu, w_s, b_st, g_go)


def _mix_proj_bwd(dz, wz, x, g, dy):
    T, D = x.shape
    tm = _tile(T, 256)

    def body(dz_ref, w_ref, x_ref, g_ref, dy_ref, dx_ref, dxb_ref, dg_ref):
        dh = _nt(dz_ref[...], w_ref[...])
        dx, dgr = _norm_bwd(dh, x_ref[...], g_ref[...])
        dx = dx + dy_ref[...]
        dx_ref[...] = dx
        dxb_ref[...] = dx.astype(BF)
        _acc_rows(dg_ref, pl.program_id(0) == 0, dgr)

    row = lambda i: (i, 0)
    fix = lambda i: (0, 0)
    return pl.pallas_call(
        body, name="mix_proj_bwd", grid=(T // tm,),
        in_specs=[pl.BlockSpec((tm, ZW), row), pl.BlockSpec((D, ZW), fix), pl.BlockSpec((tm, D), row),
                  pl.BlockSpec((1, D), fix), pl.BlockSpec((tm, D), row)],
        out_specs=[pl.BlockSpec((tm, D), row), pl.BlockSpec((tm, D), row), pl.BlockSpec((1, D), fix)],
        out_shape=[S((T, D), F32), S((T, D), BF), S((1, D), F32)],
        compiler_params=_cp(1))(dz, wz, x, g, dy)


def _ca_kv(mem, g_mem, wckv, g_ck):
    M, D = mem.shape

    def body(m_ref, g_ref, w_ref, gk_ref, mn_ref, kr_ref, kn_ref, v_ref):
        mf = m_ref[...]
        mn = (mf * _rstd(mf) * g_ref[...]).astype(BF)
        mn_ref[...] = mn
        for h in range(CA_HEADS):
            kr = _nn(mn, w_ref[h])
            kr_ref[h] = kr
            kn_ref[h] = (kr * _rstd(kr) * gk_ref[...]).astype(BF)
            v_ref[h] = _nn(mn, w_ref[CA_HEADS + h]).astype(BF)

    hd = (CA_HEADS, M, CA_HD)
    return pl.pallas_call(
        body, name="ca_kv", out_shape=[S((M, D), BF), S(hd, F32), S(hd, BF), S(hd, BF)],
        compiler_params=pltpu.CompilerParams(vmem_limit_bytes=VMEM_LIMIT))(mem, g_mem, wckv, g_ck)


def _ca_tile_fwd(xt, gca, wcq, gcq, kn_ref, v_ref):
    hb = (xt * _rstd(xt) * gca).astype(BF)
    qc = _nn(hb, wcq)
    qr, qn, ps = [], [], []
    for h in range(CA_HEADS):
        qh = qc[:, h * CA_HD:(h + 1) * CA_HD]
        qnh = (qh * _rstd(qh) * gcq * 0.0625).astype(BF)
        s = _nt(qnh, kn_ref[h])
        e = jnp.exp(s - jnp.max(s, axis=1, keepdims=True))
        ps.append(e / jnp.sum(e, axis=1, keepdims=True))
        qr.append(qh)
        qn.append(qnh)
    return hb, qr, qn, ps


def _ca_fwd(x, g_ca, wcq, g_cq, kn, vv, wco):
    T, D = x.shape
    M = kn.shape[1]
    tm = _tile(T, 256)

    def body(x_ref, gca_ref, wcq_ref, gcq_ref, kn_ref, v_ref, wco_ref, o_ref, ob_sc):
        xt = x_ref[...]
        _, _, _, ps = _ca_tile_fwd(xt, gca_ref[...], wcq_ref[...], gcq_ref[...], kn_ref, v_ref)
        for h in range(CA_HEADS):
            ob_sc[:, h * CA_HD:(h + 1) * CA_HD] = _nn(ps[h].astype(BF), v_ref[h]).astype(BF)
        o_ref[...] = xt + _nn(ob_sc[...], wco_ref[...])

    row = lambda i: (i, 0)
    fix = lambda i: (0, 0)
    fix3 = lambda i: (0, 0, 0)
    return pl.pallas_call(
        body, name="ca_fwd", grid=(T // tm,),
        in_specs=[pl.BlockSpec((tm, D), row), pl.BlockSpec((1, D), fix), pl.BlockSpec((D, D), fix),
                  pl.BlockSpec((1, CA_HD), fix), pl.BlockSpec((CA_HEADS, M, CA_HD), fix3),
                  pl.BlockSpec((CA_HEADS, M, CA_HD), fix3), pl.BlockSpec((D, D), fix)],
        out_specs=pl.BlockSpec((tm, D), row), out_shape=S((T, D), F32),
        scratch_shapes=[pltpu.VMEM((tm, D), BF)],
        compiler_params=_cp(1))(x, g_ca, wcq, g_cq, kn, vv, wco)


def _ca_bwd(x, dy, g_ca, wcq, g_cq, kn, vv, wco):
    T, D = x.shape
    M = kn.shape[1]
    tm = _tile(T, 256)
    n = T // tm

    def body(x_ref, dy_ref, gca_ref, wcq_ref, gcq_ref, kn_ref, v_ref, wco_ref,
             dx_ref, dwq_ref, dwo_ref, dkn_ref, dv_ref, dgcq_ref, dgca_ref, aq_sc, ao_sc, ob_sc, dq_sc):
        i = pl.program_id(0)
        first = i == 0
        xt = x_ref[...]
        dyt = dy_ref[...]
        dyb = dyt.astype(BF)
        hb, qr, qn, ps = _ca_tile_fwd(xt, gca_ref[...], wcq_ref[...], gcq_ref[...], kn_ref, v_ref)
        do = _nt(dyb, wco_ref[...])
        gcq_rows = None
        for h in range(CA_HEADS):
            hs = slice(h * CA_HD, (h + 1) * CA_HD)
            p = ps[h]
            pb = p.astype(BF)
            ob_sc[:, hs] = _nn(pb, v_ref[h]).astype(BF)
            doh = do[:, hs].astype(BF)
            dp = _nt(doh, v_ref[h])
            ds = (p * (dp - jnp.sum(dp * p, axis=1, keepdims=True))).astype(BF)
            dvh = _tn(pb, doh)
            dkh = _tn(ds, qn[h])

            @pl.when(first)
            def _():
                dv_ref[h] = dvh
                dkn_ref[h] = dkh

            @pl.when(jnp.logical_not(first))
            def _():
                dv_ref[h] += dvh
                dkn_ref[h] += dkh

            dqn = _nn(ds, kn_ref[h]) * 0.0625
            dqh, gr = _norm_bwd(dqn, qr[h], gcq_ref[...])
            gcq_rows = gr if gcq_rows is None else gcq_rows + gr
            dq_sc[:, hs] = dqh.astype(BF)
        _acc_rows(dgcq_ref, first, gcq_rows)
        dqb = dq_sc[...]
        p_o = _tn(ob_sc[...], dyb)
        p_q = _tn(hb, dqb)

        @pl.when(first)
        def _():
            ao_sc[...] = p_o
            aq_sc[...] = p_q

        @pl.when(jnp.logical_not(first))
        def _():
            ao_sc[...] += p_o
            aq_sc[...] += p_q

        @pl.when(i == n - 1)
        def _():
            dwo_ref[...] = ao_sc[...].astype(BF)
            dwq_ref[...] = aq_sc[...].astype(BF)

        dh = _nt(dqb, wcq_ref[...])
        dx, gar = _norm_bwd(dh, xt, gca_ref[...])
        dx_ref[...] = dx + dyt
        _acc_rows(dgca_ref, first, gar)

    row = lambda i: (i, 0)
    fix = lambda i: (0, 0)
    fix3 = lambda i: (0, 0, 0)
    hd = (CA_HEADS, M, CA_HD)
    return pl.pallas_call(
        body, name="ca_bwd", grid=(n,),
        in_specs=[pl.BlockSpec((tm, D), row), pl.BlockSpec((tm, D), row), pl.BlockSpec((1, D), fix),
                  pl.BlockSpec((D, D), fix), pl.BlockSpec((1, CA_HD), fix), pl.BlockSpec(hd, fix3),
                  pl.BlockSpec(hd, fix3), pl.BlockSpec((D, D), fix)],
        out_specs=[pl.BlockSpec((tm, D), row), pl.BlockSpec((D, D), fix), pl.BlockSpec((D, D), fix),
                   pl.BlockSpec(hd, fix3), pl.BlockSpec(hd, fix3), pl.BlockSpec((1, CA_HD), fix),
                   pl.BlockSpec((1, D), fix)],
        out_shape=[S((T, D), F32), S((D, D), BF), S((D, D), BF), S(hd, F32), S(hd, F32), S((1, CA_HD), F32),
                   S((1, D), F32)],
        scratch_shapes=[pltpu.VMEM((D, D), F32), pltpu.VMEM((D, D), F32), pltpu.VMEM((tm, D), BF),
                        pltpu.VMEM((tm, D), BF)],
        compiler_params=_cp(1))(x, dy, g_ca, wcq, g_cq, kn, vv, wco)


def _ca_kv_bwd(mem, g_mem, mn, kraw, dkn, dvv, wckv, g_ck):
    M, D = mem.shape

    def body(m_ref, g_ref, mn_ref, kr_ref, dkn_ref, dv_ref, w_ref, gk_ref, dw_ref, dgk_ref, dgm_ref):
        mn = mn_ref[...]
        dmn = jnp.zeros((M, D), F32)
        gk_rows = None
        for h in range(CA_HEADS):
            dkr, gr = _norm_bwd(dkn_ref[h], kr_ref[h], gk_ref[...])
            gk_rows = gr if gk_rows is None else gk_rows + gr
            dkb = dkr.astype(BF)
            dvb = dv_ref[h].astype(BF)
            dw_ref[h] = _tn(mn, dkb).astype(BF)
            dw_ref[CA_HEADS + h] = _tn(mn, dvb).astype(BF)
            dmn = dmn + _nt(dkb, w_ref[h]) + _nt(dvb, w_ref[CA_HEADS + h])
        dgk_ref[...] = jnp.sum(gk_rows, axis=0, keepdims=True)
        mf = m_ref[...]
        dgm_ref[...] = jnp.sum(dmn * (mf * _rstd(mf)), axis=0, keepdims=True)

    return pl.pallas_call(
        body, name="ca_kv_bwd",
        out_shape=[S((2 * CA_HEADS, D, CA_HD), BF), S((1, CA_HD), F32), S((1, D), F32)],
        compiler_params=pltpu.CompilerParams(vmem_limit_bytes=VMEM_LIMIT))(mem, g_mem, mn, kraw, dkn, dvv, wckv, g_ck)


def _local_step(x, mem, target, small, big):
    T, D = x.shape
    p = small
    bf128 = jnp.pad(p["b_f"], ((0, 0), (0, LANES - FOX_HEADS)))
    b_st = p["b_s"].T

    a1, h1 = _ffn_up("ffn1_up", x, p["g_ffn1"], big["wup1"])
    x1 = _ffn_down("ffn1_down", a1, big["wdn1"], x)
    z, h2 = _mix_proj(x1, p["g_mix"], big["wz"])
    qs, kn, vb, ccol, yg = _mix_prep(z, bf128, p["g_q"], p["g_k"], p["g_sgu"], p["w_s"], b_st, p["g_gmlp_o"])
    crow = ccol[:, :FOX_HEADS].T
    attn, lse = _fox_fwd(qs, kn, vb, ccol, crow)
    x2 = _mix_out(attn, yg, p["g_fox_o"], big["wout"], x1)
    mn, kraw, ckn, cvv = _ca_kv(mem, p["g_mem"], big["wckv"], p["g_ck"])
    x3 = _ca_fwd(x2, p["g_ca"], big["wcq"], p["g_cq"], ckn, cvv, big["wco"])
    a2, h4 = _ffn_up("ffn2_up", x3, p["g_ffn2"], big["wup2"])
    dy4, dy4b, sq = _ffn_down_loss("ffn2_down", a2, big["wdn2"], x3, target)

    gs, gb = {}, {}
    dgu2 = _ffn_bwd_act("ffn2_bwd_act", dy4b, h4, big["wup2"], big["wdn2"])
    dx3, gs["g_ffn2"] = _ffn_dx("ffn2_dx", dgu2, big["wup2"], x3, p["g_ffn2"], dy4)
    gb["wup2"], gb["wdn2"] = _ffn_dw("ffn2", h4, dgu2, a2, dy4b)

    dx2, gb["wcq"], gb["wco"], dckn, dcvv, gs["g_cq"], gs["g_ca"] = _ca_bwd(
        x2, dx3, p["g_ca"], big["wcq"], p["g_cq"], ckn, cvv, big["wco"])
    gb["wckv"], gs["g_ck"], gs["g_mem"] = _ca_kv_bwd(mem, p["g_mem"], mn, kraw, dckn, dcvv, big["wckv"], p["g_ck"])

    dattn, dsum, dyg, gb["wout"], gs["g_fox_o"] = _mix_out_bwd(dx2, attn, yg, p["g_fox_o"], big["wout"])
    dq, dk, dv, dcq, dck = _fox_bwd(qs, kn, vb, dattn, lse, dsum, ccol, crow)
    dck_col = jnp.pad(dck.T, ((0, 0), (0, LANES - FOX_HEADS)))
    dz, gs["g_q"], gs["g_k"], gs["g_sgu"], gs["g_gmlp_o"], gs["w_s"], dbst, dbf = _mix_prep_bwd(
        z, dq, dk, dv, dcq, dck_col, dyg, bf128, p["g_q"], p["g_k"], p["g_sgu"], p["w_s"], b_st, p["g_gmlp_o"])
    gs["b_s"] = dbst.T
    gs["b_f"] = dbf[:, :FOX_HEADS]
    dx1, dx1b, gs["g_mix"] = _mix_proj_bwd(dz, big["wz"], x1, p["g_mix"], dx2)
    tk = _tile(T, 512)
    zb = ZW // 3
    gb["wz"] = _tn_matmul(
        "mix_dwz", h2, pl.BlockSpec((tk, D), lambda j, k: (k, 0)), dz, pl.BlockSpec((tk, zb), lambda j, k: (k, j)),
        S((D, ZW), F32), pl.BlockSpec((D, zb), lambda j, k: (0, j)), (3, T // tk), (D, zb))

    dgu1 = _ffn_bwd_act("ffn1_bwd_act", dx1b, h1, big["wup1"], big["wdn1"])
    dx0, gs["g_ffn1"] = _ffn_dx("ffn1_dx", dgu1, big["wup1"], x, p["g_ffn1"], dx1)
    gb["wup1"], gb["wdn1"] = _ffn_dw("ffn1", h1, dgu1, a1, dx1b)
    return sq, dx0, gb, gs


MESH = pl.DeviceIdType.MESH
HBM_SPEC = pl.BlockSpec(memory_space=pltpu.HBM)
N_PEER = N_DEV - 1


def _place():
    return lax.axis_index("x"), lax.axis_index("y"), lax.axis_index("c")


def _slot(px, py, pc):
    return 4 * px + 2 * py + pc


def _all_gather_big(shards):
    n = len(shards)

    def body(*refs):
        ins, outs = refs[:n], refs[n:2 * n]
        send_sems, recv_sems, local_sems = refs[2 * n:]
        x, y, c = _place()
        me, sibling = (x, y, c), (x, y, 1 - c)
        chips = [(1 - x, y), (x, 1 - y), (1 - x, 1 - y)]

        def copy(a, k, block, to, src=None):
            dst = outs[a].at[_slot(*block)]
            return pltpu.make_async_remote_copy(
                src_ref=dst if src is None else src, dst_ref=dst, send_sem=send_sems.at[a * N_PEER + k],
                recv_sem=recv_sems.at[a * N_PEER + k], device_id=to, device_id_type=MESH)

        mine, first, passed = [], [], []
        for a in range(n):
            cp = pltpu.make_async_copy(ins[a], outs[a].at[_slot(*me)], local_sems.at[a])
            cp.start()
            mine.append(cp)
            f = [copy(a, 0, me, sibling, src=ins[a])]
            f += [copy(a, 1 + j, me, (*chip, c), src=ins[a]) for j, chip in enumerate(chips)]
            for cp in f:
                cp.start()
            first.append(f)
        for a in range(n):
            p = [copy(a, 4 + j, (*chip, c), sibling) for j, chip in enumerate(chips)]
            for j, chip in enumerate(chips):
                copy(a, 1 + j, (*chip, c), me).wait_recv()
                p[j].start()
            passed.append(p)
        for a in range(n):
            copy(a, 0, sibling, me).wait_recv()
            for j, chip in enumerate(chips):
                copy(a, 4 + j, (*chip, 1 - c), me).wait_recv()
            for cp in first[a] + passed[a]:
                cp.wait_send()
            mine[a].wait()

    return pl.pallas_call(
        body, name="gather_weights",
        out_shape=[S((N_DEV,) + s.shape, s.dtype) for s in shards],
        in_specs=[HBM_SPEC] * n, out_specs=[HBM_SPEC] * n,
        scratch_shapes=[pltpu.SemaphoreType.DMA((n * N_PEER,)), pltpu.SemaphoreType.DMA((n * N_PEER,)),
                        pltpu.SemaphoreType.DMA((n,))],
    )(*shards)


def _exchange_grads(parts):
    n = len(parts)

    def body(*refs):
        ins, outs = refs[:n], refs[n:2 * n]
        send_sems, recv_sems, local_sems = refs[2 * n:]
        x, y, c = _place()
        my = _slot(x, y, c)

        def peer(r):
            return (1 - x if r & 4 else x, 1 - y if r & 2 else y, 1 - c if r & 1 else c)

        def copy(a, r):
            p = peer(r)
            return pltpu.make_async_remote_copy(
                src_ref=ins[a].at[_slot(*p)], dst_ref=outs[a].at[my], send_sem=send_sems.at[a * N_PEER + r - 1],
                recv_sem=recv_sems.at[a * N_PEER + r - 1], device_id=p, device_id_type=MESH)

        def arrival(a, r):
            p = _slot(*peer(r))
            return pltpu.make_async_remote_copy(
                src_ref=ins[a].at[p], dst_ref=outs[a].at[p], send_sem=send_sems.at[a * N_PEER + r - 1],
                recv_sem=recv_sems.at[a * N_PEER + r - 1], device_id=peer(r), device_id_type=MESH)

        mine, sent = [], []
        for a in range(n):
            cp = pltpu.make_async_copy(ins[a].at[my], outs[a].at[my], local_sems.at[a])
            cp.start()
            mine.append(cp)
            for r in range(1, N_DEV):
                cp = copy(a, r)
                cp.start()
                sent.append(cp)
        for a in range(n):
            for r in range(1, N_DEV):
                arrival(a, r).wait_recv()
        for cp in sent:
            cp.wait_send()
        for cp in mine:
            cp.wait()

    return pl.pallas_call(
        body, name="exchange_grads",
        out_shape=[S(p.shape, p.dtype) for p in parts],
        in_specs=[HBM_SPEC] * n, out_specs=[HBM_SPEC] * n,
        scratch_shapes=[pltpu.SemaphoreType.DMA((n * N_PEER,)), pltpu.SemaphoreType.DMA((n * N_PEER,)),
                        pltpu.SemaphoreType.DMA((n,))],
    )(*parts)


def _adamw(w, g, m, v):
    m2 = ADAM_B1 * m + (1.0 - ADAM_B1) * g
    v2 = ADAM_B2 * v + (1.0 - ADAM_B2) * (g * g)
    m_hat = m2 / (1.0 - ADAM_B1 ** ADAM_STEP)
    v_hat = v2 / (1.0 - ADAM_B2 ** ADAM_STEP)
    delta = -ADAM_LR * (m_hat / (jnp.sqrt(v_hat) + ADAM_EPS) + ADAM_WD * w)
    return delta, m2, v2


def _adamw_big(name, slots, w, m, v):
    R, C = w.shape
    tr = 256 if R % 256 == 0 else R

    def body(s_ref, w_ref, m_ref, v_ref, g_ref, d_ref, m2_ref, v2_ref):
        g = s_ref[0].astype(F32)
        for k in range(1, N_DEV):
            g = g + s_ref[k].astype(F32)
        d, m2, v2 = _adamw(w_ref[...], g, m_ref[...], v_ref[...])
        g_ref[...] = g
        d_ref[...] = d
        m2_ref[...] = m2
        v2_ref[...] = v2

    row = pl.BlockSpec((tr, C), lambda i: (i, 0))
    return pl.pallas_call(
        body, name=name, grid=(R // tr,),
        in_specs=[pl.BlockSpec((N_DEV, tr, C), lambda i: (0, i, 0)), row, row, row],
        out_specs=[row] * 4, out_shape=[S((R, C), F32)] * 4,
        compiler_params=_cp(1))(slots, w, m, v)


SMALL_ROWS = (("w_s", 1024), ("b_s", 8), ("g_ffn1", 8), ("g_mix", 8), ("g_ca", 8), ("g_mem", 8), ("g_ffn2", 8),
              ("g_sgu", 4), ("g_fox_o", 4), ("g_gmlp_o", 4), ("g_cq", 2), ("g_ck", 2), ("g_q", 1), ("g_k", 1),
              ("b_f", 1))
SMALL_P = 1096


def _pack_small(d):
    rows = []
    for name, r in SMALL_ROWS:
        flat = d[name].reshape(-1)
        rows.append(jnp.pad(flat, (0, r * LANES - flat.shape[0])).reshape(r, LANES))
    used = sum(r for _, r in SMALL_ROWS)
    rows.append(jnp.zeros((SMALL_P - used, LANES), F32))
    return jnp.concatenate(rows, axis=0)


def _unpack_small(packed, shapes):
    out, at = {}, 0
    for name, r in SMALL_ROWS:
        shape = shapes[name]
        size = 1
        for s in shape:
            size *= s
        out[name] = packed[at:at + r].reshape(-1)[:size].reshape(shape)
        at += r
    return out


def _small_reduce_adamw(part, w, m, v):
    P = part.shape[0]

    def body(p_ref, w_ref, m_ref, v_ref, g_ref, d_ref, m2_ref, v2_ref, all_ref, send_sems, recv_sems, local_sem):
        x, y, c = _place()
        me, sibling = (x, y, c), (x, y, 1 - c)
        chips = [(1 - x, y), (x, 1 - y), (1 - x, 1 - y)]

        def copy(k, block, to, src=None):
            dst = all_ref.at[_slot(*block)]
            return pltpu.make_async_remote_copy(
                src_ref=dst if src is None else src, dst_ref=dst, send_sem=send_sems.at[k], recv_sem=recv_sems.at[k],
                device_id=to, device_id_type=MESH)

        mine = pltpu.make_async_copy(p_ref, all_ref.at[_slot(*me)], local_sem)
        mine.start()
        first = [copy(0, me, sibling, src=p_ref)]
        first += [copy(1 + j, me, (*chip, c), src=p_ref) for j, chip in enumerate(chips)]
        for cp in first:
            cp.start()
        passed = [copy(4 + j, (*chip, c), sibling) for j, chip in enumerate(chips)]
        for j, chip in enumerate(chips):
            copy(1 + j, (*chip, c), me).wait_recv()
            passed[j].start()
        copy(0, sibling, me).wait_recv()
        for j, chip in enumerate(chips):
            copy(4 + j, (*chip, 1 - c), me).wait_recv()
        for cp in first + passed:
            cp.wait_send()
        mine.wait()

        g = all_ref[0]
        for k in range(1, N_DEV):
            g = g + all_ref[k]
        d, m2, v2 = _adamw(w_ref[...], g, m_ref[...], v_ref[...])
        g_ref[...] = g
        d_ref[...] = d
        m2_ref[...] = m2
        v2_ref[...] = v2

    vm = pl.BlockSpec(memory_space=pltpu.VMEM)
    return pl.pallas_call(
        body, name="small_reduce_adamw",
        out_shape=[S((P, LANES), F32)] * 4, in_specs=[vm] * 4, out_specs=[vm] * 4,
        scratch_shapes=[pltpu.VMEM((N_DEV, P, LANES), F32), pltpu.SemaphoreType.DMA((N_PEER,)),
                        pltpu.SemaphoreType.DMA((N_PEER,)), pltpu.SemaphoreType.DMA],
        compiler_params=pltpu.CompilerParams(vmem_limit_bytes=VMEM_LIMIT),
    )(part, w, m, v)


WEIGHTS = ('g_ffn1', 'w_ffn1_in', 'w_ffn1_out', 'g_mix', 'w_in', 'b_f', 'g_q', 'g_k', 'g_sgu', 'w_s', 'b_s',
           'g_fox_o', 'g_gmlp_o', 'w_out', 'g_ca', 'g_mem', 'w_cq', 'w_ckv', 'g_cq', 'g_ck', 'w_co', 'g_ffn2',
           'w_ffn2_in', 'w_ffn2_out')
BIG = ('w_ffn1_in', 'w_ffn1_out', 'w_in', 'w_out', 'w_cq', 'w_ckv', 'w_co', 'w_ffn2_in', 'w_ffn2_out')
QKV_W = 3 * FOX_W
UV_OFF = QKV_W + FOX_HEADS


def kernel(x, mem, g_ffn1, w_ffn1_in, w_ffn1_out, g_mix, w_in, b_f, g_q, g_k, g_sgu, w_s, b_s, g_fox_o, g_gmlp_o, w_out, g_ca, g_mem, w_cq, w_ckv, g_cq, g_ck, w_co, g_ffn2, w_ffn2_in, w_ffn2_out, loss_target, m_g_ffn1, m_w_ffn1_in, m_w_ffn1_out, m_g_mix, m_w_in, m_b_f, m_g_q, m_g_k, m_g_sgu, m_w_s, m_b_s, m_g_fox_o, m_g_gmlp_o, m_w_out, m_g_ca, m_g_mem, m_w_cq, m_w_ckv, m_g_cq, m_g_ck, m_w_co, m_g_ffn2, m_w_ffn2_in, m_w_ffn2_out, v_g_ffn1, v_w_ffn1_in, v_w_ffn1_out, v_g_mix, v_w_in, v_b_f, v_g_q, v_g_k, v_g_sgu, v_w_s, v_b_s, v_g_fox_o, v_g_gmlp_o, v_w_out, v_g_ca, v_g_mem, v_w_cq, v_w_ckv, v_g_cq, v_g_ck, v_w_co, v_g_ffn2, v_w_ffn2_in, v_w_ffn2_out):
    args = dict(locals())
    w = {n: args[n] for n in WEIGHTS}
    mo = {n: args["m_" + n] for n in WEIGHTS}
    vo = {n: args["v_" + n] for n in WEIGHTS}
    D = D_MODEL

    gathered = dict(zip(BIG, _all_gather_big([w[n][0].astype(BF) for n in BIG])))
    w_in_full = gathered["w_in"].transpose(1, 0, 2).reshape(D, -1)
    wz = jnp.concatenate([w_in_full[:, :QKV_W], w_in_full[:, UV_OFF:], w_in_full[:, QKV_W:UV_OFF],
                          jnp.zeros((D, LANES - FOX_HEADS), BF)], axis=1)
    fb = gathered["w_ffn1_in"].shape[-1]
    big = {
        "wup1": gathered["w_ffn1_in"].reshape(2, N_FFN_BLK, D, fb),
        "wup2": gathered["w_ffn2_in"].reshape(2, N_FFN_BLK, D, fb),
        "wdn1": gathered["w_ffn1_out"].reshape(N_FFN_BLK, fb, D),
        "wdn2": gathered["w_ffn2_out"].reshape(N_FFN_BLK, fb, D),
        "wz": wz,
        "wout": gathered["w_out"].reshape(D, D), "wcq": gathered["w_cq"].reshape(D, D),
        "wco": gathered["w_co"].reshape(D, D), "wckv": gathered["w_ckv"],
    }
    small_names = [n for n, _ in SMALL_ROWS]
    small = {n: (w[n][0] if n == "w_s" or n == "b_s" else w[n]) for n in small_names}

    sq, dx0, gb, gs = _local_step(x[0], mem[0], loss_target[0], small, big)
    loss = lax.psum(sq[0, 0], ("x", "y", "c")) * (0.5 / D)

    gz = gb["wz"]
    g_in = jnp.concatenate([gz[:, :QKV_W], gz[:, Z_F:Z_F + FOX_HEADS], gz[:, QKV_W:Z_F]], axis=1)
    parts = {
        "w_ffn1_in": gb["wup1"], "w_ffn2_in": gb["wup2"],
        "w_ffn1_out": gb["wdn1"].reshape(N_DEV, -1, D), "w_ffn2_out": gb["wdn2"].reshape(N_DEV, -1, D),
        "w_in": g_in.reshape(D, N_DEV, -1).transpose(1, 0, 2).astype(BF),
        "w_out": gb["wout"].reshape(N_DEV, -1, D), "w_cq": gb["wcq"].reshape(N_DEV, -1, D),
        "w_co": gb["wco"].reshape(N_DEV, -1, D), "w_ckv": gb["wckv"],
    }
    slots = dict(zip(BIG, _exchange_grads([parts[n] for n in BIG])))

    grad, delta, new_m, new_v = {}, {}, {}, {}
    for n in BIG:
        shape = w[n].shape
        g, d, m2, v2 = _adamw_big("adamw_" + n, slots[n], w[n][0], mo[n][0], vo[n][0])
        grad[n], delta[n], new_m[n], new_v[n] = (t.reshape(shape) for t in (g, d, m2, v2))

    shapes = {n: w[n].shape for n in small_names}
    packed = _small_reduce_adamw(_pack_small(gs), _pack_small({n: w[n] for n in small_names}),
                                 _pack_small({n: mo[n] for n in small_names}),
                                 _pack_small({n: vo[n] for n in small_names}))
    for store, pk in zip((grad, delta, new_m, new_v), packed):
        store.update(_unpack_small(pk, shapes))

    return (loss, dx0[None], *[grad[n] for n in WEIGHTS], *[delta[n] for n in WEIGHTS],
            *[new_m[n] for n in WEIGHTS], *[new_v[n] for n in WEIGHTS])
```

```python
import functools

import jax
import jax.numpy as jnp
from jax import lax
from jax.experimental import pallas as pl
from jax.experimental.pallas import tpu as pltpu

F32 = jnp.float32
BF = jnp.bfloat16
S = jax.ShapeDtypeStruct

N_DEV = 8
D_MODEL = 1024
FOX_HEADS, FOX_HD = 8, 64
FOX_W = 512
GMLP_G, GMLP_GD = 8, 64
GMLP_W = 512
CHUNK = 128
CA_HEADS, CA_HD = 4, 256
N_FFN_BLK = 4
ZW = 2688
Z_Q, Z_K, Z_V, Z_U, Z_G, Z_F = 0, 512, 1024, 1536, 2048, 2560
EPS = 1e-6
NEG = -1e30
LANES = 128

ADAM_LR, ADAM_B1, ADAM_B2, ADAM_EPS, ADAM_WD, ADAM_STEP = 0.001, 0.9, 0.999, 1e-08, 0.01, 10

VMEM_LIMIT = 52 * 2 ** 20


def _cp(n_axes):
    return pltpu.CompilerParams(dimension_semantics=("arbitrary",) * n_axes, vmem_limit_bytes=VMEM_LIMIT)


def _nn(a, b):
    return jnp.dot(a, b, preferred_element_type=F32)


def _nt(a, b):
    return lax.dot_general(a, b, (((1,), (1,)), ((), ())), preferred_element_type=F32)


def _tn(a, b):
    return lax.dot_general(a, b, (((0,), (0,)), ((), ())), preferred_element_type=F32)


def _hi(a, b):
    return jnp.dot(a, b, precision=lax.Precision.HIGHEST, preferred_element_type=F32)


def _rstd(x):
    return lax.rsqrt(jnp.mean(x * x, axis=-1, keepdims=True) + EPS)


def _norm_bwd(dy, x, g):
    r = _rstd(x)
    xh = x * r
    dxh = dy * g
    dx = r * (dxh - xh * jnp.mean(dxh * xh, axis=-1, keepdims=True))
    return dx, dy * xh


def _acc_rows(ref, first, val):
    srow = jnp.sum(val, axis=0, keepdims=True)

    @pl.when(first)
    def _():
        ref[...] = srow

    @pl.when(jnp.logical_not(first))
    def _():
        ref[...] += srow


def _gelu(x):
    c = 0.7978845608028654
    return 0.5 * x * (1.0 + jnp.tanh(c * (x + 0.044715 * x * x * x)))


def _gelu_grad(x):
    c = 0.7978845608028654
    t = jnp.tanh(c * (x + 0.044715 * x * x * x))
    return 0.5 * (1.0 + t) + 0.5 * x * (1.0 - t * t) * c * (1.0 + 3 * 0.044715 * x * x)


def _tile(n, pref):
    return pref if n % pref == 0 else n


def _ffn_up(name, x, g, wup):
    T, D = x.shape
    FB = wup.shape[-1]
    tm = _tile(T, 512)

    def body(x_ref, g_ref, w_ref, a_ref, h_ref):
        @pl.when(pl.program_id(1) == 0)
        def _():
            xf = x_ref[...]
            h_ref[...] = (xf * _rstd(xf) * g_ref[...]).astype(BF)

        hb = h_ref[...]
        gg = _nn(hb, w_ref[0])
        uu = _nn(hb, w_ref[1])
        a_ref[...] = (gg * jax.nn.sigmoid(gg) * uu).astype(BF)

    return pl.pallas_call(
        body, name=name, grid=(T // tm, N_FFN_BLK),
        in_specs=[pl.BlockSpec((tm, D), lambda i, j: (i, 0)),
                  pl.BlockSpec((1, D), lambda i, j: (0, 0)),
                  pl.BlockSpec((2, None, D, FB), lambda i, j: (0, j, 0, 0))],
        out_specs=[pl.BlockSpec((None, tm, FB), lambda i, j: (j, i, 0)),
                   pl.BlockSpec((tm, D), lambda i, j: (i, 0))],
        out_shape=[S((N_FFN_BLK, T, FB), BF), S((T, D), BF)],
        compiler_params=_cp(2))(x, g, wup)


def _ffn_down(name, a, wdn, x):
    _, T, FB = a.shape
    D = x.shape[1]
    tm = _tile(T, 512)

    def body(a_ref, w_ref, x_ref, o_ref):
        j = pl.program_id(1)
        p = 0.5 * _nn(a_ref[...], w_ref[...])

        @pl.when(j == 0)
        def _():
            o_ref[...] = x_ref[...] + p

        @pl.when(j > 0)
        def _():
            o_ref[...] += p

    return pl.pallas_call(
        body, name=name, grid=(T // tm, N_FFN_BLK),
        in_specs=[pl.BlockSpec((None, tm, FB), lambda i, j: (j, i, 0)),
                  pl.BlockSpec((None, FB, D), lambda i, j: (j, 0, 0)),
                  pl.BlockSpec((tm, D), lambda i, j: (i, 0))],
        out_specs=pl.BlockSpec((tm, D), lambda i, j: (i, 0)),
        out_shape=S((T, D), F32),
        compiler_params=_cp(2))(a, wdn, x)


def _ffn_down_loss(name, a, wdn, x, target):
    _, T, FB = a.shape
    D = x.shape[1]
    tm = _tile(T, 512)

    def body(a_ref, w_ref, x_ref, t_ref, d_ref, db_ref, loss_ref, acc_ref):
        i, j = pl.program_id(0), pl.program_id(1)
        p = 0.5 * _nn(a_ref[...], w_ref[...])

        @pl.when(j == 0)
        def _():
            acc_ref[...] = x_ref[...] + p

        @pl.when(j > 0)
        def _():
            acc_ref[...] += p

        @pl.when(j == N_FFN_BLK - 1)
        def _():
            diff = acc_ref[...] - t_ref[...]
            dy = diff * (1.0 / D)
            d_ref[...] = dy
            db_ref[...] = dy.astype(BF)
            sq = jnp.zeros((8, LANES), F32) + jnp.sum(diff * diff)

            @pl.when(i == 0)
            def _():
                loss_ref[...] = sq

            @pl.when(i > 0)
            def _():
                loss_ref[...] += sq

    return pl.pallas_call(
        body, name=name, grid=(T // tm, N_FFN_BLK),
        in_specs=[pl.BlockSpec((None, tm, FB), lambda i, j: (j, i, 0)),
                  pl.BlockSpec((None, FB, D), lambda i, j: (j, 0, 0)),
                  pl.BlockSpec((tm, D), lambda i, j: (i, 0)),
                  pl.BlockSpec((tm, D), lambda i, j: (i, 0))],
        out_specs=[pl.BlockSpec((tm, D), lambda i, j: (i, 0)),
                   pl.BlockSpec((tm, D), lambda i, j: (i, 0)),
                   pl.BlockSpec((8, LANES), lambda i, j: (0, 0))],
        out_shape=[S((T, D), F32), S((T, D), BF), S((8, LANES), F32)],
        scratch_shapes=[pltpu.VMEM((tm, D), F32)],
        compiler_params=_cp(2))(a, wdn, x, target)


def _ffn_bwd_act(name, dyb, h, wup, wdn):
    T, D = h.shape
    FB = wup.shape[-1]
    tm = _tile(T, 512)

    def body(d_ref, h_ref, wu_ref, wd_ref, o_ref):
        da = 0.5 * _nt(d_ref[...], wd_ref[...])
        hb = h_ref[...]
        gg = _nn(hb, wu_ref[0])
        uu = _nn(hb, wu_ref[1])
        sg = jax.nn.sigmoid(gg)
        o_ref[0] = (da * uu * (sg * (1.0 + gg * (1.0 - sg)))).astype(BF)
        o_ref[1] = (da * (gg * sg)).astype(BF)

    return pl.pallas_call(
        body, name=name, grid=(T // tm, N_FFN_BLK),
        in_specs=[pl.BlockSpec((tm, D), lambda i, j: (i, 0)),
                  pl.BlockSpec((tm, D), lambda i, j: (i, 0)),
                  pl.BlockSpec((2, None, D, FB), lambda i, j: (0, j, 0, 0)),
                  pl.BlockSpec((None, FB, D), lambda i, j: (j, 0, 0))],
        out_specs=pl.BlockSpec((2, None, tm, FB), lambda i, j: (0, j, i, 0)),
        out_shape=S((2, N_FFN_BLK, T, FB), BF),
        compiler_params=_cp(2))(dyb, h, wup, wdn)


def _ffn_dx(name, dgu, wup, x, g, dy):
    T, D = x.shape
    FB = wup.shape[-1]
    tm = _tile(T, 512)

    def body(d_ref, w_ref, x_ref, g_ref, dy_ref, dx_ref, dg_ref, acc_ref):
        i, j = pl.program_id(0), pl.program_id(1)
        p = _nt(d_ref[0], w_ref[0]) + _nt(d_ref[1], w_ref[1])

        @pl.when(j == 0)
        def _():
            acc_ref[...] = p

        @pl.when(j > 0)
        def _():
            acc_ref[...] += p

        @pl.when(j == N_FFN_BLK - 1)
        def _():
            dx, dgr = _norm_bwd(acc_ref[...], x_ref[...], g_ref[...])
            dx_ref[...] = dx + dy_ref[...]
            _acc_rows(dg_ref, i == 0, dgr)

    return pl.pallas_call(
        body, name=name, grid=(T // tm, N_FFN_BLK),
        in_specs=[pl.BlockSpec((2, None, tm, FB), lambda i, j: (0, j, i, 0)),
                  pl.BlockSpec((2, None, D, FB), lambda i, j: (0, j, 0, 0)),
                  pl.BlockSpec((tm, D), lambda i, j: (i, 0)),
                  pl.BlockSpec((1, D), lambda i, j: (0, 0)),
                  pl.BlockSpec((tm, D), lambda i, j: (i, 0))],
        out_specs=[pl.BlockSpec((tm, D), lambda i, j: (i, 0)),
                   pl.BlockSpec((1, D), lambda i, j: (0, 0))],
        out_shape=[S((T, D), F32), S((1, D), F32)],
        scratch_shapes=[pltpu.VMEM((tm, D), F32)],
        compiler_params=_cp(2))(dgu, wup, x, g, dy)


def _tn_matmul(name, a, a_spec, b, b_spec, out_shape, out_spec, grid, acc_shape, scale=1.0):
    nk = grid[1]

    def body(a_ref, b_ref, o_ref, acc_ref):
        k = pl.program_id(1)
        p = _tn(a_ref[...], b_ref[...])

        @pl.when(k == 0)
        def _():
            acc_ref[...] = p

        @pl.when(k > 0)
        def _():
            acc_ref[...] += p

        @pl.when(k == nk - 1)
        def _():
            o_ref[...] = (acc_ref[...] * scale).astype(o_ref.dtype)

    return pl.pallas_call(
        body, name=name, grid=grid, in_specs=[a_spec, b_spec], out_specs=out_spec, out_shape=out_shape,
        scratch_shapes=[pltpu.VMEM(acc_shape, F32)], compiler_params=_cp(2))(a, b)


def _ffn_dw(name, h, dgu, a, dyb):
    T, D = h.shape
    FB = a.shape[-1]
    tk = _tile(T, 512)
    nk = T // tk
    dgu8 = dgu.reshape(2 * N_FFN_BLK, T, FB)
    dwup = _tn_matmul(
        name + "_dwup", h, pl.BlockSpec((tk, D), lambda j, k: (k, 0)),
        dgu8, pl.BlockSpec((None, tk, FB), lambda j, k: (j, k, 0)),
        S((2 * N_FFN_BLK, D, FB), BF), pl.BlockSpec((None, D, FB), lambda j, k: (j, 0, 0)),
        (2 * N_FFN_BLK, nk), (D, FB))
    dwdn = _tn_matmul(
        name + "_dwdn", a, pl.BlockSpec((None, tk, FB), lambda j, k: (j, k, 0)),
        dyb, pl.BlockSpec((tk, D), lambda j, k: (k, 0)),
        S((N_FFN_BLK, FB, D), BF), pl.BlockSpec((None, FB, D), lambda j, k: (j, 0, 0)),
        (N_FFN_BLK, nk), (FB, D), scale=0.5)
    return dwup, dwdn


def _mix_proj(x, g, wz):
    T, D = x.shape
    tm = _tile(T, 256)

    def body(x_ref, g_ref, w_ref, z_ref, h_ref):
        xf = x_ref[...]
        hb = (xf * _rstd(xf) * g_ref[...]).astype(BF)
        h_ref[...] = hb
        z_ref[...] = _nn(hb, w_ref[...])

    return pl.pallas_call(
        body, name="mix_proj", grid=(T // tm,),
        in_specs=[pl.BlockSpec((tm, D), lambda i: (i, 0)),
                  pl.BlockSpec((1, D), lambda i: (0, 0)),
                  pl.BlockSpec((D, ZW), lambda i: (0, 0))],
        out_specs=[pl.BlockSpec((tm, ZW), lambda i: (i, 0)),
                   pl.BlockSpec((tm, D), lambda i: (i, 0))],
        out_shape=[S((T, ZW), F32), S((T, D), BF)],
        compiler_params=_cp(1))(x, g, wz)


def _tri(n, lower):
    r = lax.broadcasted_iota(jnp.int32, (n, n), 0)
    c = lax.broadcasted_iota(jnp.int32, (n, n), 1)
    return (r >= c) if lower else (r <= c)


def _spatial_mix(vgn_b, ws_ref, bst, tm):
    tril = _tri(CHUNK, True)
    wms = [jnp.where(tril, ws_ref[g], 0.0).astype(BF) for g in range(GMLP_G)]
    rows = []
    for c in range(tm // CHUNK):
        cols = []
        for g in range(GMLP_G):
            vs = vgn_b[c * CHUNK:(c + 1) * CHUNK, g * GMLP_GD:(g + 1) * GMLP_GD]
            cols.append(_nn(wms[g], vs) + bst[:, g:g + 1])
        rows.append(jnp.concatenate(cols, axis=1))
    return jnp.concatenate(rows, axis=0), wms


def _mix_prep(z, bf128, g_q, g_k, g_sgu, w_s, b_st, g_go):
    T = z.shape[0]
    tm = _tile(T, 256)

    def body(z_ref, bf_ref, gq_ref, gk_ref, gs_ref, ws_ref, bst_ref, go_ref,
             q_ref, k_ref, v_ref, c_ref, y_ref, carry_ref):
        i = pl.program_id(0)

        @pl.when(i == 0)
        def _():
            carry_ref[...] = jnp.zeros_like(carry_ref)

        for h in range(FOX_HEADS):
            hs = slice(h * FOX_HD, (h + 1) * FOX_HD)
            qh = z_ref[:, Z_Q + h * FOX_HD:Z_Q + (h + 1) * FOX_HD]
            kh = z_ref[:, Z_K + h * FOX_HD:Z_K + (h + 1) * FOX_HD]
            q_ref[:, hs] = (qh * _rstd(qh) * gq_ref[...] * 0.125).astype(BF)
            k_ref[:, hs] = (kh * _rstd(kh) * gk_ref[...]).astype(BF)
        v_ref[...] = z_ref[:, Z_V:Z_V + FOX_W].astype(BF)

        fl = z_ref[:, Z_F:Z_F + LANES] + bf_ref[...]
        logf = jnp.minimum(fl, 0.0) - jnp.log1p(jnp.exp(-jnp.abs(fl)))
        csum = _hi(_tri(tm, True).astype(F32), logf) + carry_ref[...]
        c_ref[...] = csum
        carry_ref[...] = csum[tm - 1:tm, :]

        u = _gelu(z_ref[:, Z_U:Z_U + GMLP_W])
        vg = _gelu(z_ref[:, Z_G:Z_G + GMLP_W])
        vgn = (vg * _rstd(vg) * gs_ref[...]).astype(BF)
        mixed, _ = _spatial_mix(vgn, ws_ref, bst_ref[...], tm)
        sgu = u * mixed
        y_ref[...] = (sgu * _rstd(sgu) * go_ref[...]).astype(BF)

    row = lambda i: (i, 0)
    fix2 = lambda i: (0, 0)
    return pl.pallas_call(
        body, name="mix_prep", grid=(T // tm,),
        in_specs=[pl.BlockSpec((tm, ZW), row),
                  pl.BlockSpec((1, LANES), fix2), pl.BlockSpec((1, FOX_HD), fix2), pl.BlockSpec((1, FOX_HD), fix2),
                  pl.BlockSpec((1, GMLP_W), fix2), pl.BlockSpec((GMLP_G, CHUNK, CHUNK), lambda i: (0, 0, 0)),
                  pl.BlockSpec((CHUNK, GMLP_G), fix2), pl.BlockSpec((1, GMLP_W), fix2)],
        out_specs=[pl.BlockSpec((tm, FOX_W), row), pl.BlockSpec((tm, FOX_W), row), pl.BlockSpec((tm, FOX_W), row),
                   pl.BlockSpec((tm, LANES), row), pl.BlockSpec((tm, GMLP_W), row)],
        out_shape=[S((T, FOX_W), BF), S((T, FOX_W), BF), S((T, FOX_W), BF), S((T, LANES), F32), S((T, GMLP_W), BF)],
        scratch_shapes=[pltpu.VMEM((1, LANES), F32)],
        compiler_params=_cp(1))(z, bf128, g_q, g_k, g_sgu, w_s, b_st, g_go)


def _fox_fwd(q, k, v, ccol, crow):
    T = q.shape[0]
    tq = _tile(T, 512)
    nq = T // tq

    def body(q_ref, k_ref, v_ref, cc_ref, cr_ref, o_ref, lse_ref, m_sc, l_sc, acc_sc):
        i, j = pl.program_id(0), pl.program_id(1)

        @pl.when(j == 0)
        def _():
            m_sc[...] = jnp.full(m_sc.shape, NEG, F32)
            l_sc[...] = jnp.zeros_like(l_sc)
            acc_sc[...] = jnp.zeros_like(acc_sc)

        def step(masked):
            cc = cc_ref[...]
            cr = cr_ref[...]
            mask = _tri(tq, True) if masked else None
            for h in range(FOX_HEADS):
                hs = slice(h * FOX_HD, (h + 1) * FOX_HD)
                s = _nt(q_ref[:, hs], k_ref[:, hs]) + (cc[:, h:h + 1] - cr[h:h + 1, :])
                if masked:
                    s = jnp.where(mask, s, NEG)
                m_prev = m_sc[h]
                m_new = jnp.maximum(m_prev, jnp.max(s, axis=1, keepdims=True))
                alpha = jnp.exp(m_prev - m_new)
                p = jnp.exp(s - m_new)
                l_sc[h] = alpha * l_sc[h] + jnp.sum(p, axis=1, keepdims=True)
                acc_sc[:, hs] = alpha * acc_sc[:, hs] + _nn(p.astype(BF), v_ref[:, hs])
                m_sc[h] = m_new

        @pl.when(j < i)
        def _():
            step(False)

        @pl.when(j == i)
        def _():
            step(True)
            lse_ref[...] = jnp.zeros_like(lse_ref)
            for h in range(FOX_HEADS):
                hs = slice(h * FOX_HD, (h + 1) * FOX_HD)
                o_ref[:, hs] = acc_sc[:, hs] / l_sc[h]
                lse_ref[:, h:h + 1] = m_sc[h] + jnp.log(l_sc[h])

    qi = lambda i, j: (i, 0)
    kj = lambda i, j: (jnp.minimum(i, j), 0)
    return pl.pallas_call(
        body, name="fox_fwd", grid=(nq, nq),
        in_specs=[pl.BlockSpec((tq, FOX_W), qi), pl.BlockSpec((tq, FOX_W), kj), pl.BlockSpec((tq, FOX_W), kj),
                  pl.BlockSpec((tq, LANES), qi), pl.BlockSpec((FOX_HEADS, tq), lambda i, j: (0, jnp.minimum(i, j)))],
        out_specs=[pl.BlockSpec((tq, FOX_W), qi), pl.BlockSpec((tq, LANES), qi)],
        out_shape=[S((T, FOX_W), F32), S((T, LANES), F32)],
        scratch_shapes=[pltpu.VMEM((FOX_HEADS, tq, 1), F32), pltpu.VMEM((FOX_HEADS, tq, 1), F32),
                        pltpu.VMEM((tq, FOX_W), F32)],
        compiler_params=_cp(2))(q, k, v, ccol, crow)


def _fox_bwd(q, k, v, dob, lse, dsum, ccol, crow):
    T = q.shape[0]
    tq = _tile(T, 512)
    nq = T // tq

    def body(q_ref, k_ref, v_ref, do_ref, lse_ref, ds_ref, cc_ref, cr_ref,
             dq_ref, dk_ref, dv_ref, dcq_ref, dck_ref):
        j, i = pl.program_id(0), pl.program_id(1)

        @pl.when(jnp.logical_and(i == 0, j == 0))
        def _():
            dq_ref[...] = jnp.zeros_like(dq_ref)
            dcq_ref[...] = jnp.zeros_like(dcq_ref)

        @pl.when(i == 0)
        def _():
            dk_ref[...] = jnp.zeros_like(dk_ref)
            dv_ref[...] = jnp.zeros_like(dv_ref)
            dck_ref[...] = jnp.zeros_like(dck_ref)

        def step(masked):
            rows = pl.ds(pl.multiple_of(i * tq, tq), tq)
            cc = cc_ref[...]
            cr = cr_ref[...]
            lse_t = lse_ref[...]
            dsum_t = ds_ref[...]
            mask = _tri(tq, True) if masked else None
            for h in range(FOX_HEADS):
                hs = slice(h * FOX_HD, (h + 1) * FOX_HD)
                qh, kh, vh, doh = q_ref[:, hs], k_ref[:, hs], v_ref[:, hs], do_ref[:, hs]
                s = _nt(qh, kh) + (cc[:, h:h + 1] - cr[h:h + 1, :])
                if masked:
                    s = jnp.where(mask, s, NEG)
                p = jnp.exp(s - lse_t[:, h:h + 1])
                dp = _nt(doh, vh)
                ds = p * (dp - dsum_t[:, h:h + 1])
                dsb = ds.astype(BF)
                dv_ref[:, hs] += _tn(p.astype(BF), doh)
                dk_ref[:, hs] += _tn(dsb, qh)
                dq_ref[rows, hs] += _nn(dsb, kh)
                dcq_ref[rows, h:h + 1] += jnp.sum(ds, axis=1, keepdims=True)
                dck_ref[h:h + 1, :] -= jnp.sum(ds, axis=0, keepdims=True)

        @pl.when(i > j)
        def _():
            step(False)

        @pl.when(i == j)
        def _():
            step(True)

    qi = lambda j, i: (jnp.maximum(i, j), 0)
    kj = lambda j, i: (j, 0)
    whole = lambda j, i: (0, 0)
    return pl.pallas_call(
        body, name="fox_bwd", grid=(nq, nq),
        in_specs=[pl.BlockSpec((tq, FOX_W), qi), pl.BlockSpec((tq, FOX_W), kj), pl.BlockSpec((tq, FOX_W), kj),
                  pl.BlockSpec((tq, FOX_W), qi), pl.BlockSpec((tq, LANES), qi), pl.BlockSpec((tq, LANES), qi),
                  pl.BlockSpec((tq, LANES), qi), pl.BlockSpec((FOX_HEADS, tq), lambda j, i: (0, j))],
        out_specs=[pl.BlockSpec((T, FOX_W), whole), pl.BlockSpec((tq, FOX_W), kj), pl.BlockSpec((tq, FOX_W), kj),
                   pl.BlockSpec((T, LANES), whole), pl.BlockSpec((FOX_HEADS, tq), lambda j, i: (0, j))],
        out_shape=[S((T, FOX_W), F32), S((T, FOX_W), F32), S((T, FOX_W), F32), S((T, LANES), F32),
                   S((FOX_HEADS, T), F32)],
        compiler_params=_cp(2))(q, k, v, dob, lse, dsum, ccol, crow)


def _mix_out(attn, yg, g_fo, wout, x):
    T, D = x.shape
    tm = _tile(T, 512)

    def body(a_ref, y_ref, g_ref, w_ref, x_ref, o_ref):
        at = a_ref[...]
        yf = (at * _rstd(at) * g_ref[...]).astype(BF)
        o_ref[...] = x_ref[...] + _nn(yf, w_ref[:FOX_W, :]) + _nn(y_ref[...], w_ref[FOX_W:, :])

    row = lambda i: (i, 0)
    return pl.pallas_call(
        body, name="mix_out", grid=(T // tm,),
        in_specs=[pl.BlockSpec((tm, FOX_W), row), pl.BlockSpec((tm, GMLP_W), row),
                  pl.BlockSpec((1, FOX_W), lambda i: (0, 0)), pl.BlockSpec((D, D), lambda i: (0, 0)),
                  pl.BlockSpec((tm, D), row)],
        out_specs=pl.BlockSpec((tm, D), row),
        out_shape=S((T, D), F32),
        compiler_params=_cp(1))(attn, yg, g_fo, wout, x)


def _mix_out_bwd(dx, attn, yg, g_fo, wout):
    T, D = dx.shape
    tm = _tile(T, 256)
    n = T // tm

    def body(dx_ref, a_ref, y_ref, g_ref, w_ref, da_ref, dsum_ref, dyg_ref, dw_ref, dg_ref, acc_ref):
        i = pl.program_id(0)
        dxb = dx_ref[...].astype(BF)
        at = a_ref[...]
        yf = (at * _rstd(at) * g_ref[...]).astype(BF)
        dy = _nt(dxb, w_ref[...])
        p_top = _tn(yf, dxb)
        p_bot = _tn(y_ref[...], dxb)

        @pl.when(i == 0)
        def _():
            acc_ref[:FOX_W, :] = p_top
            acc_ref[FOX_W:, :] = p_bot

        @pl.when(i > 0)
        def _():
            acc_ref[:FOX_W, :] += p_top
            acc_ref[FOX_W:, :] += p_bot

        @pl.when(i == n - 1)
        def _():
            dw_ref[...] = acc_ref[...].astype(BF)

        dat, dgr = _norm_bwd(dy[:, :FOX_W], at, g_ref[...])
        _acc_rows(dg_ref, i == 0, dgr)
        da_ref[...] = dat.astype(BF)
        dyg_ref[...] = dy[:, FOX_W:]
        prod = dat * at
        dsum_ref[...] = jnp.zeros_like(dsum_ref)
        for h in range(FOX_HEADS):
            dsum_ref[:, h:h + 1] = jnp.sum(prod[:, h * FOX_HD:(h + 1) * FOX_HD], axis=1, keepdims=True)

    row = lambda i: (i, 0)
    fix = lambda i: (0, 0)
    return pl.pallas_call(
        body, name="mix_out_bwd", grid=(n,),
        in_specs=[pl.BlockSpec((tm, D), row), pl.BlockSpec((tm, FOX_W), row), pl.BlockSpec((tm, GMLP_W), row),
                  pl.BlockSpec((1, FOX_W), fix), pl.BlockSpec((D, D), fix)],
        out_specs=[pl.BlockSpec((tm, FOX_W), row), pl.BlockSpec((tm, LANES), row), pl.BlockSpec((tm, GMLP_W), row),
                   pl.BlockSpec((D, D), fix), pl.BlockSpec((1, FOX_W), fix)],
        out_shape=[S((T, FOX_W), BF), S((T, LANES), F32), S((T, GMLP_W), F32), S((D, D), BF), S((1, FOX_W), F32)],
        scratch_shapes=[pltpu.VMEM((D, D), F32)],
        compiler_params=_cp(1))(dx, attn, yg, g_fo, wout)


def _mix_prep_bwd(z, dq, dk, dv, dcq, dck, dyg, bf128, g_q, g_k, g_sgu, w_s, b_st, g_go):
    T = z.shape[0]
    tm = _tile(T, 256)
    n = T // tm

    def body(z_ref, dq_ref, dk_ref, dv_ref, dcq_ref, dck_ref, dyg_ref, bf_ref, gq_ref, gk_ref, gs_ref, ws_ref,
             bst_ref, go_ref, dz_ref, dgq_ref, dgk_ref, dgs_ref, dgo_ref, dws_ref, dbst_ref, dbf_ref, carry_ref):
        i = pl.program_id(0)
        first = i == 0

        @pl.when(first)
        def _():
            carry_ref[...] = jnp.zeros_like(carry_ref)

        gq_rows, gk_rows = [], []
        for h in range(FOX_HEADS):
            hs = slice(h * FOX_HD, (h + 1) * FOX_HD)
            dqh, gqr = _norm_bwd(dq_ref[:, hs] * 0.125, z_ref[:, Z_Q + h * FOX_HD:Z_Q + (h + 1) * FOX_HD], gq_ref[...])
            dkh, gkr = _norm_bwd(dk_ref[:, hs], z_ref[:, Z_K + h * FOX_HD:Z_K + (h + 1) * FOX_HD], gk_ref[...])
            dz_ref[:, Z_Q + h * FOX_HD:Z_Q + (h + 1) * FOX_HD] = dqh.astype(BF)
            dz_ref[:, Z_K + h * FOX_HD:Z_K + (h + 1) * FOX_HD] = dkh.astype(BF)
            gq_rows.append(gqr)
            gk_rows.append(gkr)
        _acc_rows(dgq_ref, first, functools.reduce(lambda a, b: a + b, gq_rows))
        _acc_rows(dgk_ref, first, functools.reduce(lambda a, b: a + b, gk_rows))
        dz_ref[:, Z_V:Z_V + FOX_W] = dv_ref[...].astype(BF)

        dc = dcq_ref[...] + dck_ref[...]
        dlogf = _hi(_tri(tm, False).astype(F32), dc) + carry_ref[...]
        carry_ref[...] = dlogf[0:1, :]
        fl = z_ref[:, Z_F:Z_F + LANES] + bf_ref[...]
        lane = lax.broadcasted_iota(jnp.int32, (tm, LANES), 1)
        df = jnp.where(lane < FOX_HEADS, dlogf * jax.nn.sigmoid(-fl), 0.0)
        dz_ref[:, Z_F:Z_F + LANES] = df.astype(BF)
        _acc_rows(dbf_ref, first, df)

        u_pre = z_ref[:, Z_U:Z_U + GMLP_W]
        vg_pre = z_ref[:, Z_G:Z_G + GMLP_W]
        u = _gelu(u_pre)
        vg = _gelu(vg_pre)
        vgn = (vg * _rstd(vg) * gs_ref[...]).astype(BF)
        bst = bst_ref[...]
        mixed, wms = _spatial_mix(vgn, ws_ref, bst, tm)
        sgu = u * mixed
        dsgu, gor = _norm_bwd(dyg_ref[...], sgu, go_ref[...])
        _acc_rows(dgo_ref, first, gor)
        du = dsgu * mixed
        dmixed = dsgu * u
        dmb = dmixed.astype(BF)
        tril = _tri(CHUNK, True)
        dvgn_rows = []
        dws = [None] * GMLP_G
        dbs = [None] * GMLP_G
        for c in range(tm // CHUNK):
            cs = slice(c * CHUNK, (c + 1) * CHUNK)
            cols = []
            for g in range(GMLP_G):
                gs = slice(g * GMLP_GD, (g + 1) * GMLP_GD)
                dmc = dmb[cs, gs]
                pw = _nt(dmc, vgn[cs, gs])
                pb = jnp.sum(dmixed[cs, gs], axis=1, keepdims=True)
                dws[g] = pw if dws[g] is None else dws[g] + pw
                dbs[g] = pb if dbs[g] is None else dbs[g] + pb
                cols.append(_tn(wms[g], dmc))
            dvgn_rows.append(jnp.concatenate(cols, axis=1))
        dvgn = jnp.concatenate(dvgn_rows, axis=0)
        dbs_t = jnp.concatenate(dbs, axis=1)
        for g in range(GMLP_G):
            dwg = jnp.where(tril, dws[g], 0.0)

            @pl.when(first)
            def _():
                dws_ref[g] = dwg

            @pl.when(jnp.logical_not(first))
            def _():
                dws_ref[g] += dwg

        @pl.when(first)
        def _():
            dbst_ref[...] = dbs_t

        @pl.when(jnp.logical_not(first))
        def _():
            dbst_ref[...] += dbs_t

        dvg, gsr = _norm_bwd(dvgn, vg, gs_ref[...])
        _acc_rows(dgs_ref, first, gsr)
        dz_ref[:, Z_U:Z_U + GMLP_W] = (du * _gelu_grad(u_pre)).astype(BF)
        dz_ref[:, Z_G:Z_G + GMLP_W] = (dvg * _gelu_grad(vg_pre)).astype(BF)

    rev = lambda i: (n - 1 - i, 0)
    fix = lambda i: (0, 0)
    fix3 = lambda i: (0, 0, 0)
    return pl.pallas_call(
        body, name="mix_prep_bwd", grid=(n,),
        in_specs=[pl.BlockSpec((tm, ZW), rev), pl.BlockSpec((tm, FOX_W), rev), pl.BlockSpec((tm, FOX_W), rev),
                  pl.BlockSpec((tm, FOX_W), rev), pl.BlockSpec((tm, LANES), rev), pl.BlockSpec((tm, LANES), rev),
                  pl.BlockSpec((tm, GMLP_W), rev),
                  pl.BlockSpec((1, LANES), fix), pl.BlockSpec((1, FOX_HD), fix), pl.BlockSpec((1, FOX_HD), fix),
                  pl.BlockSpec((1, GMLP_W), fix), pl.BlockSpec((GMLP_G, CHUNK, CHUNK), fix3),
                  pl.BlockSpec((CHUNK, GMLP_G), fix), pl.BlockSpec((1, GMLP_W), fix)],
        out_specs=[pl.BlockSpec((tm, ZW), rev), pl.BlockSpec((1, FOX_HD), fix), pl.BlockSpec((1, FOX_HD), fix),
                   pl.BlockSpec((1, GMLP_W), fix), pl.BlockSpec((1, GMLP_W), fix),
                   pl.BlockSpec((GMLP_G, CHUNK, CHUNK), fix3), pl.BlockSpec((CHUNK, GMLP_G), fix),
                   pl.BlockSpec((1, LANES), fix)],
        out_shape=[S((T, ZW), BF), S((1, FOX_HD), F32), S((1, FOX_HD), F32), S((1, GMLP_W), F32), S((1, GMLP_W), F32),
                   S((GMLP_G, CHUNK, CHUNK), F32), S((CHUNK, GMLP_G), F32), S((1, LANES), F32)],
        scratch_shapes=[pltpu.VMEM((1, LANES), F32)],
        compiler_params=_cp(1))(z, dq, dk, dv, dcq, dck, dyg, bf128, g_q, g_k, g_sgu, w_s, b_st, g_go)


def _mix_proj_bwd(dz, wz, x, g, dy):
    T, D = x.shape
    tm = _tile(T, 256)

    def body(dz_ref, w_ref, x_ref, g_ref, dy_ref, dx_ref, dxb_ref, dg_ref):
        dh = _nt(dz_ref[...], w_ref[...])
        dx, dgr = _norm_bwd(dh, x_ref[...], g_ref[...])
        dx = dx + dy_ref[...]
        dx_ref[...] = dx
        dxb_ref[...] = dx.astype(BF)
        _acc_rows(dg_ref, pl.program_id(0) == 0, dgr)

    row = lambda i: (i, 0)
    fix = lambda i: (0, 0)
    return pl.pallas_call(
        body, name="mix_proj_bwd", grid=(T // tm,),
        in_specs=[pl.BlockSpec((tm, ZW), row), pl.BlockSpec((D, ZW), fix), pl.BlockSpec((tm, D), row),
                  pl.BlockSpec((1, D), fix), pl.BlockSpec((tm, D), row)],
        out_specs=[pl.BlockSpec((tm, D), row), pl.BlockSpec((tm, D), row), pl.BlockSpec((1, D), fix)],
        out_shape=[S((T, D), F32), S((T, D), BF), S((1, D), F32)],
        compiler_params=_cp(1))(dz, wz, x, g, dy)


def _ca_kv(mem, g_mem, wckv, g_ck):
    M, D = mem.shape

    def body(m_ref, g_ref, w_ref, gk_ref, mn_ref, kr_ref, kn_ref, v_ref):
        mf = m_ref[...]
        mn = (mf * _rstd(mf) * g_ref[...]).astype(BF)
        mn_ref[...] = mn
        for h in range(CA_HEADS):
            kr = _nn(mn, w_ref[h])
            kr_ref[h] = kr
            kn_ref[h] = (kr * _rstd(kr) * gk_ref[...]).astype(BF)
            v_ref[h] = _nn(mn, w_ref[CA_HEADS + h]).astype(BF)

    hd = (CA_HEADS, M, CA_HD)
    return pl.pallas_call(
        body, name="ca_kv", out_shape=[S((M, D), BF), S(hd, F32), S(hd, BF), S(hd, BF)],
        compiler_params=pltpu.CompilerParams(vmem_limit_bytes=VMEM_LIMIT))(mem, g_mem, wckv, g_ck)


def _ca_tile_fwd(xt, gca, wcq, gcq, kn_ref, v_ref):
    hb = (xt * _rstd(xt) * gca).astype(BF)
    qc = _nn(hb, wcq)
    qr, qn, ps = [], [], []
    for h in range(CA_HEADS):
        qh = qc[:, h * CA_HD:(h + 1) * CA_HD]
        qnh = (qh * _rstd(qh) * gcq * 0.0625).astype(BF)
        s = _nt(qnh, kn_ref[h])
        e = jnp.exp(s - jnp.max(s, axis=1, keepdims=True))
        ps.append(e / jnp.sum(e, axis=1, keepdims=True))
        qr.append(qh)
        qn.append(qnh)
    return hb, qr, qn, ps


def _ca_fwd(x, g_ca, wcq, g_cq, kn, vv, wco):
    T, D = x.shape
    M = kn.shape[1]
    tm = _tile(T, 256)

    def body(x_ref, gca_ref, wcq_ref, gcq_ref, kn_ref, v_ref, wco_ref, o_ref, ob_sc):
        xt = x_ref[...]
        _, _, _, ps = _ca_tile_fwd(xt, gca_ref[...], wcq_ref[...], gcq_ref[...], kn_ref, v_ref)
        for h in range(CA_HEADS):
            ob_sc[:, h * CA_HD:(h + 1) * CA_HD] = _nn(ps[h].astype(BF), v_ref[h]).astype(BF)
        o_ref[...] = xt + _nn(ob_sc[...], wco_ref[...])

    row = lambda i: (i, 0)
    fix = lambda i: (0, 0)
    fix3 = lambda i: (0, 0, 0)
    return pl.pallas_call(
        body, name="ca_fwd", grid=(T // tm,),
        in_specs=[pl.BlockSpec((tm, D), row), pl.BlockSpec((1, D), fix), pl.BlockSpec((D, D), fix),
                  pl.BlockSpec((1, CA_HD), fix), pl.BlockSpec((CA_HEADS, M, CA_HD), fix3),
                  pl.BlockSpec((CA_HEADS, M, CA_HD), fix3), pl.BlockSpec((D, D), fix)],
        out_specs=pl.BlockSpec((tm, D), row), out_shape=S((T, D), F32),
        scratch_shapes=[pltpu.VMEM((tm, D), BF)],
        compiler_params=_cp(1))(x, g_ca, wcq, g_cq, kn, vv, wco)


def _ca_bwd(x, dy, g_ca, wcq, g_cq, kn, vv, wco):
    T, D = x.shape
    M = kn.shape[1]
    tm = _tile(T, 256)
    n = T // tm

    def body(x_ref, dy_ref, gca_ref, wcq_ref, gcq_ref, kn_ref, v_ref, wco_ref,
             dx_ref, dwq_ref, dwo_ref, dkn_ref, dv_ref, dgcq_ref, dgca_ref, aq_sc, ao_sc, ob_sc, dq_sc):
        i = pl.program_id(0)
        first = i == 0
        xt = x_ref[...]
        dyt = dy_ref[...]
        dyb = dyt.astype(BF)
        hb, qr, qn, ps = _ca_tile_fwd(xt, gca_ref[...], wcq_ref[...], gcq_ref[...], kn_ref, v_ref)
        do = _nt(dyb, wco_ref[...])
        gcq_rows = None
        for h in range(CA_HEADS):
            hs = slice(h * CA_HD, (h + 1) * CA_HD)
            p = ps[h]
            pb = p.astype(BF)
            ob_sc[:, hs] = _nn(pb, v_ref[h]).astype(BF)
            doh = do[:, hs].astype(BF)
            dp = _nt(doh, v_ref[h])
            ds = (p * (dp - jnp.sum(dp * p, axis=1, keepdims=True))).astype(BF)
            dvh = _tn(pb, doh)
            dkh = _tn(ds, qn[h])

            @pl.when(first)
            def _():
                dv_ref[h] = dvh
                dkn_ref[h] = dkh

            @pl.when(jnp.logical_not(first))
            def _():
                dv_ref[h] += dvh
                dkn_ref[h] += dkh

            dqn = _nn(ds, kn_ref[h]) * 0.0625
            dqh, gr = _norm_bwd(dqn, qr[h], gcq_ref[...])
            gcq_rows = gr if gcq_rows is None else gcq_rows + gr
            dq_sc[:, hs] = dqh.astype(BF)
        _acc_rows(dgcq_ref, first, gcq_rows)
        dqb = dq_sc[...]
        p_o = _tn(ob_sc[...], dyb)
        p_q = _tn(hb, dqb)

        @pl.when(first)
        def _():
            ao_sc[...] = p_o
            aq_sc[...] = p_q

        @pl.when(jnp.logical_not(first))
        def _():
            ao_sc[...] += p_o
            aq_sc[...] += p_q

        @pl.when(i == n - 1)
        def _():
            dwo_ref[...] = ao_sc[...].astype(BF)
            dwq_ref[...] = aq_sc[...].astype(BF)

        dh = _nt(dqb, wcq_ref[...])
        dx, gar = _norm_bwd(dh, xt, gca_ref[...])
        dx_ref[...] = dx + dyt
        _acc_rows(dgca_ref, first, gar)

    row = lambda i: (i, 0)
    fix = lambda i: (0, 0)
    fix3 = lambda i: (0, 0, 0)
    hd = (CA_HEADS, M, CA_HD)
    return pl.pallas_call(
        body, name="ca_bwd", grid=(n,),
        in_specs=[pl.BlockSpec((tm, D), row), pl.BlockSpec((tm, D), row), pl.BlockSpec((1, D), fix),
                  pl.BlockSpec((D, D), fix), pl.BlockSpec((1, CA_HD), fix), pl.BlockSpec(hd, fix3),
                  pl.BlockSpec(hd, fix3), pl.BlockSpec((D, D), fix)],
        out_specs=[pl.BlockSpec((tm, D), row), pl.BlockSpec((D, D), fix), pl.BlockSpec((D, D), fix),
                   pl.BlockSpec(hd, fix3), pl.BlockSpec(hd, fix3), pl.BlockSpec((1, CA_HD), fix),
                   pl.BlockSpec((1, D), fix)],
        out_shape=[S((T, D), F32), S((D, D), BF), S((D, D), BF), S(hd, F32), S(hd, F32), S((1, CA_HD), F32),
                   S((1, D), F32)],
        scratch_shapes=[pltpu.VMEM((D, D), F32), pltpu.VMEM((D, D), F32), pltpu.VMEM((tm, D), BF),
                        pltpu.VMEM((tm, D), BF)],
        compiler_params=_cp(1))(x, dy, g_ca, wcq, g_cq, kn, vv, wco)


def _ca_kv_bwd(mem, g_mem, mn, kraw, dkn, dvv, wckv, g_ck):
    M, D = mem.shape

    def body(m_ref, g_ref, mn_ref, kr_ref, dkn_ref, dv_ref, w_ref, gk_ref, dw_ref, dgk_ref, dgm_ref):
        mn = mn_ref[...]
        dmn = jnp.zeros((M, D), F32)
        gk_rows = None
        for h in range(CA_HEADS):
            dkr, gr = _norm_bwd(dkn_ref[h], kr_ref[h], gk_ref[...])
            gk_rows = gr if gk_rows is None else gk_rows + gr
            dkb = dkr.astype(BF)
            dvb = dv_ref[h].astype(BF)
            dw_ref[h] = _tn(mn, dkb).astype(BF)
            dw_ref[CA_HEADS + h] = _tn(mn, dvb).astype(BF)
            dmn = dmn + _nt(dkb, w_ref[h]) + _nt(dvb, w_ref[CA_HEADS + h])
        dgk_ref[...] = jnp.sum(gk_rows, axis=0, keepdims=True)
        mf = m_ref[...]
        dgm_ref[...] = jnp.sum(dmn * (mf * _rstd(mf)), axis=0, keepdims=True)

    return pl.pallas_call(
        body, name="ca_kv_bwd",
        out_shape=[S((2 * CA_HEADS, D, CA_HD), BF), S((1, CA_HD), F32), S((1, D), F32)],
        compiler_params=pltpu.CompilerParams(vmem_limit_bytes=VMEM_LIMIT))(mem, g_mem, mn, kraw, dkn, dvv, wckv, g_ck)


def _after(g, token):
    return g if token is None else g + token[0:1, 0:1]


def _local_step(x, mem, target, small, weights, emit):
    T, D = x.shape
    p = small
    bf128 = jnp.pad(p["b_f"], ((0, 0), (0, LANES - FOX_HEADS)))
    b_st = p["b_s"].T

    wup1 = weights("ffn1_up", x)["wup1"]
    a1, h1 = _ffn_up("ffn1_up", x, p["g_ffn1"], wup1)
    wdn1 = weights("ffn1_dn", h1)["wdn1"]
    x1 = _ffn_down("ffn1_down", a1, wdn1, x)
    wm = weights("mix", x1)
    z, h2 = _mix_proj(x1, p["g_mix"], wm["wz"])
    qs, kn, vb, ccol, yg = _mix_prep(z, bf128, p["g_q"], p["g_k"], p["g_sgu"], p["w_s"], b_st, p["g_gmlp_o"])
    crow = ccol[:, :FOX_HEADS].T
    attn, lse = _fox_fwd(qs, kn, vb, ccol, crow)
    x2 = _mix_out(attn, yg, p["g_fox_o"], wm["wout"], x1)
    wc = weights("ca", x2)
    mn, kraw, ckn, cvv = _ca_kv(mem, p["g_mem"], wc["wckv"], p["g_ck"])
    x3 = _ca_fwd(x2, p["g_ca"], wc["wcq"], p["g_cq"], ckn, cvv, wc["wco"])
    w2 = weights("ffn2", x3)
    a2, h4 = _ffn_up("ffn2_up", x3, p["g_ffn2"], w2["wup2"])
    dy4, dy4b, sq = _ffn_down_loss("ffn2_down", a2, w2["wdn2"], x3, target)

    gs = {}
    dgu2 = _ffn_bwd_act("ffn2_bwd_act", dy4b, h4, w2["wup2"], w2["wdn2"])
    dwup2, dwdn2 = _ffn_dw("ffn2", h4, dgu2, a2, dy4b)
    tok = emit("ffn2", {"wup2": dwup2, "wdn2": dwdn2})
    dx3, gs["g_ffn2"] = _ffn_dx("ffn2_dx", dgu2, w2["wup2"], x3, _after(p["g_ffn2"], tok), dy4)

    dx2, dwcq, dwco, dckn, dcvv, gs["g_cq"], gs["g_ca"] = _ca_bwd(
        x2, dx3, p["g_ca"], wc["wcq"], p["g_cq"], ckn, cvv, wc["wco"])
    dwckv, gs["g_ck"], gs["g_mem"] = _ca_kv_bwd(mem, p["g_mem"], mn, kraw, dckn, dcvv, wc["wckv"], p["g_ck"])

    dattn, dsum, dyg, dwout, gs["g_fox_o"] = _mix_out_bwd(dx2, attn, yg, p["g_fox_o"], wm["wout"])
    dq, dk, dv, dcq, dck = _fox_bwd(qs, kn, vb, dattn, lse, dsum, ccol, crow)
    dck_col = jnp.pad(dck.T, ((0, 0), (0, LANES - FOX_HEADS)))
    dz, gs["g_q"], gs["g_k"], gs["g_sgu"], gs["g_gmlp_o"], gs["w_s"], dbst, dbf = _mix_prep_bwd(
        z, dq, dk, dv, dcq, dck_col, dyg, bf128, p["g_q"], p["g_k"], p["g_sgu"], p["w_s"], b_st, p["g_gmlp_o"])
    gs["b_s"] = dbst.T
    gs["b_f"] = dbf[:, :FOX_HEADS]
    tk = _tile(T, 512)
    zb = ZW // 3
    dwz = _tn_matmul(
        "mix_dwz", h2, pl.BlockSpec((tk, D), lambda j, k: (k, 0)), dz, pl.BlockSpec((tk, zb), lambda j, k: (k, j)),
        S((D, ZW), F32), pl.BlockSpec((D, zb), lambda j, k: (0, j)), (3, T // tk), (D, zb))
    tok = emit("mid", {"wcq": dwcq, "wco": dwco, "wckv": dwckv, "wout": dwout, "wz": dwz})
    dx1, dx1b, gs["g_mix"] = _mix_proj_bwd(dz, wm["wz"], x1, _after(p["g_mix"], tok), dx2)

    dgu1 = _ffn_bwd_act("ffn1_bwd_act", dx1b, h1, wup1, wdn1)
    dwup1, dwdn1 = _ffn_dw("ffn1", h1, dgu1, a1, dx1b)
    tok = emit("ffn1", {"wup1": dwup1, "wdn1": dwdn1})
    dx0, gs["g_ffn1"] = _ffn_dx("ffn1_dx", dgu1, wup1, x, _after(p["g_ffn1"], tok), dx1)
    return sq, dx0, gs


MESH = pl.DeviceIdType.MESH
HBM_SPEC = pl.BlockSpec(memory_space=pltpu.HBM)
N_PEER = N_DEV - 1


def _place():
    return lax.axis_index("x"), lax.axis_index("y"), lax.axis_index("c")


def _slot(px, py, pc):
    return 4 * px + 2 * py + pc


SEM_SPEC = pl.BlockSpec(memory_space=pltpu.SEMAPHORE)
ANY_SPEC = pl.BlockSpec(memory_space=pl.ANY)
DATAFLOW = pltpu.SideEffectType.DATAFLOW_SIDE_EFFECTING


def _hbm(a):
    return pltpu.with_memory_space_constraint(a, pltpu.HBM)


def _peer(x, y, c, r):
    return (1 - x if r & 4 else x, 1 - y if r & 2 else y, 1 - c if r & 1 else c)


def _place_own(name, srcs, whole):
    n = len(srcs)

    def body(*refs):
        ins, outs, sems = refs[:n], refs[n:2 * n], refs[2 * n]
        my = _slot(*_place())
        cps = [pltpu.make_async_copy(ins[a] if whole else ins[a].at[my], outs[a].at[my], sems.at[a])
               for a in range(n)]
        for cp in cps:
            cp.start()
        for cp in cps:
            cp.wait()

    return pl.pallas_call(
        body, name=name,
        out_shape=[S((N_DEV,) + s.shape if whole else s.shape, s.dtype) for s in srcs],
        in_specs=[HBM_SPEC] * n, out_specs=[HBM_SPEC] * n,
        scratch_shapes=[pltpu.SemaphoreType.DMA((n,))],
    )(*srcs)


def _copy_start(name, srcs, lands, whole):
    n = len(srcs)

    def body(*refs):
        src, land = refs[:n], refs[n:2 * n]
        send, recv = refs[2 * n:3 * n], refs[3 * n:4 * n]
        token = refs[6 * n]
        x, y, c = _place()
        my = _slot(x, y, c)
        for a in range(n):
            for r in range(1, N_DEV):
                p = _peer(x, y, c, r)
                pltpu.make_async_remote_copy(
                    src_ref=src[a] if whole else src[a].at[_slot(*p)], dst_ref=land[a].at[my],
                    send_sem=send[a].at[r - 1], recv_sem=recv[a].at[r - 1], device_id=p, device_id_type=MESH).start()
        token[...] = jnp.zeros_like(token)

    out = pl.pallas_call(
        body, name=name,
        out_shape=([pltpu.SemaphoreType.DMA((N_PEER,))] * (2 * n)
                   + [pltpu.HBM(s.shape, s.dtype) for s in srcs] + [pltpu.HBM(s.shape, s.dtype) for s in lands]
                   + [S((8, LANES), F32)]),
        in_specs=[HBM_SPEC] * (2 * n),
        out_specs=[SEM_SPEC] * (2 * n) + [HBM_SPEC] * (2 * n) + [pl.BlockSpec(memory_space=pltpu.VMEM)],
        input_output_aliases={i: 2 * n + i for i in range(2 * n)},
        compiler_params=pltpu.CompilerParams(has_side_effects=DATAFLOW),
    )(*[_hbm(s) for s in srcs], *[_hbm(s) for s in lands])
    return out[:n], out[n:2 * n], out[2 * n:3 * n], out[3 * n:4 * n], out[4 * n]


def _copy_wait(name, srcs, lands, send, recv, after, whole):
    n = len(srcs)

    def body(*refs):
        src, land = refs[:n], refs[n:2 * n]
        snd, rcv = refs[2 * n:3 * n], refs[3 * n:4 * n]
        x, y, c = _place()
        for a in range(n):
            for r in range(1, N_DEV):
                p = _peer(x, y, c, r)
                ps = _slot(*p)
                cp = pltpu.make_async_remote_copy(
                    src_ref=src[a] if whole else src[a].at[ps], dst_ref=land[a].at[ps],
                    send_sem=snd[a].at[r - 1], recv_sem=rcv[a].at[r - 1], device_id=p, device_id_type=MESH)
                cp.wait_send()
                cp.wait_recv()

    out = pl.pallas_call(
        body, name=name,
        out_shape=[pltpu.HBM(s.shape, s.dtype) for s in srcs] + [pltpu.HBM(s.shape, s.dtype) for s in lands],
        in_specs=[HBM_SPEC] * (2 * n) + [SEM_SPEC] * (2 * n) + [ANY_SPEC],
        out_specs=[HBM_SPEC] * (2 * n),
        input_output_aliases={i: i for i in range(2 * n)},
        compiler_params=pltpu.CompilerParams(has_side_effects=DATAFLOW),
    )(*srcs, *lands, *send, *recv, after)
    return out[n:]


def _adamw(w, g, m, v):
    m2 = ADAM_B1 * m + (1.0 - ADAM_B1) * g
    v2 = ADAM_B2 * v + (1.0 - ADAM_B2) * (g * g)
    m_hat = m2 / (1.0 - ADAM_B1 ** ADAM_STEP)
    v_hat = v2 / (1.0 - ADAM_B2 ** ADAM_STEP)
    delta = -ADAM_LR * (m_hat / (jnp.sqrt(v_hat) + ADAM_EPS) + ADAM_WD * w)
    return delta, m2, v2


def _adamw_big(name, slots, w, m, v):
    R, C = w.shape
    tr = 256 if R % 256 == 0 else R

    def body(s_ref, w_ref, m_ref, v_ref, g_ref, d_ref, m2_ref, v2_ref):
        g = s_ref[0].astype(F32)
        for k in range(1, N_DEV):
            g = g + s_ref[k].astype(F32)
        d, m2, v2 = _adamw(w_ref[...], g, m_ref[...], v_ref[...])
        g_ref[...] = g
        d_ref[...] = d
        m2_ref[...] = m2
        v2_ref[...] = v2

    row = pl.BlockSpec((tr, C), lambda i: (i, 0))
    return pl.pallas_call(
        body, name=name, grid=(R // tr,),
        in_specs=[pl.BlockSpec((N_DEV, tr, C), lambda i: (0, i, 0)), row, row, row],
        out_specs=[row] * 4, out_shape=[S((R, C), F32)] * 4,
        compiler_params=_cp(1))(slots, w, m, v)


SMALL_ROWS = (("w_s", 1024), ("b_s", 8), ("g_ffn1", 8), ("g_mix", 8), ("g_ca", 8), ("g_mem", 8), ("g_ffn2", 8),
              ("g_sgu", 4), ("g_fox_o", 4), ("g_gmlp_o", 4), ("g_cq", 2), ("g_ck", 2), ("g_q", 1), ("g_k", 1),
              ("b_f", 1))
SMALL_P = 1096


def _pack_small(d):
    rows = []
    for name, r in SMALL_ROWS:
        flat = d[name].reshape(-1)
        rows.append(jnp.pad(flat, (0, r * LANES - flat.shape[0])).reshape(r, LANES))
    used = sum(r for _, r in SMALL_ROWS)
    rows.append(jnp.zeros((SMALL_P - used, LANES), F32))
    return jnp.concatenate(rows, axis=0)


def _unpack_small(packed, shapes):
    out, at = {}, 0
    for name, r in SMALL_ROWS:
        shape = shapes[name]
        size = 1
        for s in shape:
            size *= s
        out[name] = packed[at:at + r].reshape(-1)[:size].reshape(shape)
        at += r
    return out


def _small_reduce_adamw(part, w, m, v):
    P = part.shape[0]

    def body(p_ref, w_ref, m_ref, v_ref, g_ref, d_ref, m2_ref, v2_ref, all_ref, send_sems, recv_sems, local_sem):
        x, y, c = _place()
        me, sibling = (x, y, c), (x, y, 1 - c)
        chips = [(1 - x, y), (x, 1 - y), (1 - x, 1 - y)]

        def copy(k, block, to, src=None):
            dst = all_ref.at[_slot(*block)]
            return pltpu.make_async_remote_copy(
                src_ref=dst if src is None else src, dst_ref=dst, send_sem=send_sems.at[k], recv_sem=recv_sems.at[k],
                device_id=to, device_id_type=MESH)

        mine = pltpu.make_async_copy(p_ref, all_ref.at[_slot(*me)], local_sem)
        mine.start()
        first = [copy(0, me, sibling, src=p_ref)]
        first += [copy(1 + j, me, (*chip, c), src=p_ref) for j, chip in enumerate(chips)]
        for cp in first:
            cp.start()
        passed = [copy(4 + j, (*chip, c), sibling) for j, chip in enumerate(chips)]
        for j, chip in enumerate(chips):
            copy(1 + j, (*chip, c), me).wait_recv()
            passed[j].start()
        copy(0, sibling, me).wait_recv()
        for j, chip in enumerate(chips):
            copy(4 + j, (*chip, 1 - c), me).wait_recv()
        for cp in first + passed:
            cp.wait_send()
        mine.wait()

        g = all_ref[0]
        for k in range(1, N_DEV):
            g = g + all_ref[k]
        d, m2, v2 = _adamw(w_ref[...], g, m_ref[...], v_ref[...])
        g_ref[...] = g
        d_ref[...] = d
        m2_ref[...] = m2
        v2_ref[...] = v2

    vm = pl.BlockSpec(memory_space=pltpu.VMEM)
    return pl.pallas_call(
        body, name="small_reduce_adamw",
        out_shape=[S((P, LANES), F32)] * 4, in_specs=[vm] * 4, out_specs=[vm] * 4,
        scratch_shapes=[pltpu.VMEM((N_DEV, P, LANES), F32), pltpu.SemaphoreType.DMA((N_PEER,)),
                        pltpu.SemaphoreType.DMA((N_PEER,)), pltpu.SemaphoreType.DMA],
        compiler_params=pltpu.CompilerParams(vmem_limit_bytes=VMEM_LIMIT),
    )(part, w, m, v)


WEIGHTS = ('g_ffn1', 'w_ffn1_in', 'w_ffn1_out', 'g_mix', 'w_in', 'b_f', 'g_q', 'g_k', 'g_sgu', 'w_s', 'b_s',
           'g_fox_o', 'g_gmlp_o', 'w_out', 'g_ca', 'g_mem', 'w_cq', 'w_ckv', 'g_cq', 'g_ck', 'w_co', 'g_ffn2',
           'w_ffn2_in', 'w_ffn2_out')
BIG = ('w_ffn1_in', 'w_ffn1_out', 'w_in', 'w_out', 'w_cq', 'w_ckv', 'w_co', 'w_ffn2_in', 'w_ffn2_out')
GATHER_GROUPS = {"ffn1_up": ("w_ffn1_in",), "ffn1_dn": ("w_ffn1_out",), "mix": ("w_in", "w_out"),
                 "ca": ("w_cq", "w_ckv", "w_co"), "ffn2": ("w_ffn2_in", "w_ffn2_out")}
QKV_W = 3 * FOX_W
UV_OFF = QKV_W + FOX_HEADS


def kernel(x, mem, g_ffn1, w_ffn1_in, w_ffn1_out, g_mix, w_in, b_f, g_q, g_k, g_sgu, w_s, b_s, g_fox_o, g_gmlp_o, w_out, g_ca, g_mem, w_cq, w_ckv, g_cq, g_ck, w_co, g_ffn2, w_ffn2_in, w_ffn2_out, loss_target, m_g_ffn1, m_w_ffn1_in, m_w_ffn1_out, m_g_mix, m_w_in, m_b_f, m_g_q, m_g_k, m_g_sgu, m_w_s, m_b_s, m_g_fox_o, m_g_gmlp_o, m_w_out, m_g_ca, m_g_mem, m_w_cq, m_w_ckv, m_g_cq, m_g_ck, m_w_co, m_g_ffn2, m_w_ffn2_in, m_w_ffn2_out, v_g_ffn1, v_w_ffn1_in, v_w_ffn1_out, v_g_mix, v_w_in, v_b_f, v_g_q, v_g_k, v_g_sgu, v_w_s, v_b_s, v_g_fox_o, v_g_gmlp_o, v_w_out, v_g_ca, v_g_mem, v_w_cq, v_w_ckv, v_g_cq, v_g_ck, v_w_co, v_g_ffn2, v_w_ffn2_in, v_w_ffn2_out):
    args = dict(locals())
    w = {n: args[n] for n in WEIGHTS}
    mo = {n: args["m_" + n] for n in WEIGHTS}
    vo = {n: args["v_" + n] for n in WEIGHTS}
    D = D_MODEL

    shards = [w[n][0].astype(BF) for n in BIG]
    send, recv, src, land, _ = _copy_start("gather_start", shards, _place_own("gather_own", shards, True), True)
    at = {n: i for i, n in enumerate(BIG)}
    fb = shards[0].shape[-1]

    def weights(group, after):
        names = GATHER_GROUPS[group]
        ii = [at[n] for n in names]
        got = dict(zip(names, _copy_wait("gather_wait_" + group, [src[i] for i in ii], [land[i] for i in ii],
                                         [send[i] for i in ii], [recv[i] for i in ii], after, True)))
        if group == "ffn1_up":
            return {"wup1": got["w_ffn1_in"].reshape(2, N_FFN_BLK, D, fb)}
        if group == "ffn1_dn":
            return {"wdn1": got["w_ffn1_out"].reshape(N_FFN_BLK, fb, D)}
        if group == "mix":
            full = got["w_in"].transpose(1, 0, 2).reshape(D, -1)
            wz = jnp.concatenate([full[:, :QKV_W], full[:, UV_OFF:], full[:, QKV_W:UV_OFF],
                                  jnp.zeros((D, LANES - FOX_HEADS), BF)], axis=1)
            return {"wz": wz, "wout": got["w_out"].reshape(D, D)}
        if group == "ca":
            return {"wcq": got["w_cq"].reshape(D, D), "wco": got["w_co"].reshape(D, D), "wckv": got["w_ckv"]}
        return {"wup2": got["w_ffn2_in"].reshape(2, N_FFN_BLK, D, fb),
                "wdn2": got["w_ffn2_out"].reshape(N_FFN_BLK, fb, D)}

    flying = {}

    def emit(group, g):
        if group == "ffn2":
            parts = {"w_ffn2_in": g["wup2"], "w_ffn2_out": g["wdn2"].reshape(N_DEV, -1, D)}
        elif group == "ffn1":
            parts = {"w_ffn1_in": g["wup1"], "w_ffn1_out": g["wdn1"].reshape(N_DEV, -1, D)}
        else:
            gz = g["wz"]
            g_in = jnp.concatenate([gz[:, :QKV_W], gz[:, Z_F:Z_F + FOX_HEADS], gz[:, QKV_W:Z_F]], axis=1)
            parts = {"w_in": g_in.reshape(D, N_DEV, -1).transpose(1, 0, 2).astype(BF),
                     "w_out": g["wout"].reshape(N_DEV, -1, D), "w_cq": g["wcq"].reshape(N_DEV, -1, D),
                     "w_co": g["wco"].reshape(N_DEV, -1, D), "w_ckv": g["wckv"]}
        names = list(parts)
        srcs = [parts[n] for n in names]
        *handles, token = _copy_start("exchange_start_" + group, srcs, _place_own("exchange_own_" + group, srcs, False),
                                      False)
        flying[group] = (names, handles)
        return token

    small_names = [n for n, _ in SMALL_ROWS]
    small = {n: (w[n][0] if n == "w_s" or n == "b_s" else w[n]) for n in small_names}

    sq, dx0, gs = _local_step(x[0], mem[0], loss_target[0], small, weights, emit)
    loss = lax.psum(sq[0, 0], ("x", "y", "c")) * (0.5 / D)

    grad, delta, new_m, new_v = {}, {}, {}, {}

    def update(group, after):
        names, (snd, rcv, srcs, lands) = flying[group]
        slots = _copy_wait("exchange_wait_" + group, srcs, lands, snd, rcv, after, False)
        for n, sl in zip(names, slots):
            shape = w[n].shape
            g, d, m2, v2 = _adamw_big("adamw_" + n, sl, w[n][0], mo[n][0], vo[n][0])
            grad[n], delta[n], new_m[n], new_v[n] = (t.reshape(shape) for t in (g, d, m2, v2))
        return d

    last = update("ffn2", dx0)
    last = update("mid", last)
    shapes = {n: w[n].shape for n in small_names}
    packed = _small_reduce_adamw(_pack_small(gs), _pack_small({n: w[n] for n in small_names}),
                                 _pack_small({n: mo[n] for n in small_names}),
                                 _pack_small({n: vo[n] for n in small_names}))
    for store, pk in zip((grad, delta, new_m, new_v), packed):
        store.update(_unpack_small(pk, shapes))
    update("ffn1", packed[0])

    return (loss, dx0[None], *[grad[n] for n in WEIGHTS], *[delta[n] for n in WEIGHTS],
            *[new_m[n] for n in WEIGHTS], *[new_v[n] for n in WEIGHTS])
```

```python
import functools

import jax
import jax.numpy as jnp
from jax import lax
from jax.experimental import pallas as pl
from jax.experimental.pallas import tpu as pltpu

F32 = jnp.float32
BF = jnp.bfloat16
S = jax.ShapeDtypeStruct

N_DEV = 8
D_MODEL = 1024
FOX_HEADS, FOX_HD = 8, 64
FOX_W = 512
GMLP_G, GMLP_GD = 8, 64
GMLP_W = 512
CHUNK = 128
CA_HEADS, CA_HD = 4, 256
N_FFN_BLK = 4
ZW = 2688
Z_Q, Z_K, Z_V, Z_U, Z_G, Z_F = 0, 512, 1024, 1536, 2048, 2560
EPS = 1e-6
NEG = -1e30
LANES = 128

ADAM_LR, ADAM_B1, ADAM_B2, ADAM_EPS, ADAM_WD, ADAM_STEP = 0.001, 0.9, 0.999, 1e-08, 0.01, 10

VMEM_LIMIT = 52 * 2 ** 20


def _cp(n_axes):
    return pltpu.CompilerParams(dimension_semantics=("arbitrary",) * n_axes, vmem_limit_bytes=VMEM_LIMIT)


def _nn(a, b):
    return jnp.dot(a, b, preferred_element_type=F32)


def _nt(a, b):
    return lax.dot_general(a, b, (((1,), (1,)), ((), ())), preferred_element_type=F32)


def _tn(a, b):
    return lax.dot_general(a, b, (((0,), (0,)), ((), ())), preferred_element_type=F32)


def _hi(a, b):
    return jnp.dot(a, b, precision=lax.Precision.HIGHEST, preferred_element_type=F32)


def _rstd(x):
    return lax.rsqrt(jnp.mean(x * x, axis=-1, keepdims=True) + EPS)


def _norm_bwd(dy, x, g):
    r = _rstd(x)
    xh = x * r
    dxh = dy * g
    dx = r * (dxh - xh * jnp.mean(dxh * xh, axis=-1, keepdims=True))
    return dx, dy * xh


def _acc_rows(ref, first, val):
    srow = jnp.sum(val, axis=0, keepdims=True)

    @pl.when(first)
    def _():
        ref[...] = srow

    @pl.when(jnp.logical_not(first))
    def _():
        ref[...] += srow


def _gelu(x):
    c = 0.7978845608028654
    return 0.5 * x * (1.0 + jnp.tanh(c * (x + 0.044715 * x * x * x)))


def _gelu_grad(x):
    c = 0.7978845608028654
    t = jnp.tanh(c * (x + 0.044715 * x * x * x))
    return 0.5 * (1.0 + t) + 0.5 * x * (1.0 - t * t) * c * (1.0 + 3 * 0.044715 * x * x)


def _tile(n, pref):
    return pref if n % pref == 0 else n


def _ffn_up(name, x, g, wup):
    T, D = x.shape
    FB = wup.shape[-1]
    tm = _tile(T, 512)

    def body(x_ref, g_ref, w_ref, a_ref, h_ref):
        @pl.when(pl.program_id(1) == 0)
        def _():
            xf = x_ref[...]
            h_ref[...] = (xf * _rstd(xf) * g_ref[...]).astype(BF)

        hb = h_ref[...]
        gg = _nn(hb, w_ref[0])
        uu = _nn(hb, w_ref[1])
        a_ref[...] = (gg * jax.nn.sigmoid(gg) * uu).astype(BF)

    return pl.pallas_call(
        body, name=name, grid=(T // tm, N_FFN_BLK),
        in_specs=[pl.BlockSpec((tm, D), lambda i, j: (i, 0)),
                  pl.BlockSpec((1, D), lambda i, j: (0, 0)),
                  pl.BlockSpec((2, None, D, FB), lambda i, j: (0, j, 0, 0))],
        out_specs=[pl.BlockSpec((None, tm, FB), lambda i, j: (j, i, 0)),
                   pl.BlockSpec((tm, D), lambda i, j: (i, 0))],
        out_shape=[S((N_FFN_BLK, T, FB), BF), S((T, D), BF)],
        compiler_params=_cp(2))(x, g, wup)


def _ffn_down(name, a, wdn, x):
    _, T, FB = a.shape
    D = x.shape[1]
    tm = _tile(T, 512)

    def body(a_ref, w_ref, x_ref, o_ref):
        j = pl.program_id(1)
        p = 0.5 * _nn(a_ref[...], w_ref[...])

        @pl.when(j == 0)
        def _():
            o_ref[...] = x_ref[...] + p

        @pl.when(j > 0)
        def _():
            o_ref[...] += p

    return pl.pallas_call(
        body, name=name, grid=(T // tm, N_FFN_BLK),
        in_specs=[pl.BlockSpec((None, tm, FB), lambda i, j: (j, i, 0)),
                  pl.BlockSpec((None, FB, D), lambda i, j: (j, 0, 0)),
                  pl.BlockSpec((tm, D), lambda i, j: (i, 0))],
        out_specs=pl.BlockSpec((tm, D), lambda i, j: (i, 0)),
        out_shape=S((T, D), F32),
        compiler_params=_cp(2))(a, wdn, x)


def _ffn_down_loss(name, a, wdn, x, target):
    _, T, FB = a.shape
    D = x.shape[1]
    tm = _tile(T, 512)

    def body(a_ref, w_ref, x_ref, t_ref, d_ref, db_ref, loss_ref, acc_ref):
        i, j = pl.program_id(0), pl.program_id(1)
        p = 0.5 * _nn(a_ref[...], w_ref[...])

        @pl.when(j == 0)
        def _():
            acc_ref[...] = x_ref[...] + p

        @pl.when(j > 0)
        def _():
            acc_ref[...] += p

        @pl.when(j == N_FFN_BLK - 1)
        def _():
            diff = acc_ref[...] - t_ref[...]
            dy = diff * (1.0 / D)
            d_ref[...] = dy
            db_ref[...] = dy.astype(BF)
            sq = jnp.zeros((8, LANES), F32) + jnp.sum(diff * diff)

            @pl.when(i == 0)
            def _():
                loss_ref[...] = sq

            @pl.when(i > 0)
            def _():
                loss_ref[...] += sq

    return pl.pallas_call(
        body, name=name, grid=(T // tm, N_FFN_BLK),
        in_specs=[pl.BlockSpec((None, tm, FB), lambda i, j: (j, i, 0)),
                  pl.BlockSpec((None, FB, D), lambda i, j: (j, 0, 0)),
                  pl.BlockSpec((tm, D), lambda i, j: (i, 0)),
                  pl.BlockSpec((tm, D), lambda i, j: (i, 0))],
        out_specs=[pl.BlockSpec((tm, D), lambda i, j: (i, 0)),
                   pl.BlockSpec((tm, D), lambda i, j: (i, 0)),
                   pl.BlockSpec((8, LANES), lambda i, j: (0, 0))],
        out_shape=[S((T, D), F32), S((T, D), BF), S((8, LANES), F32)],
        scratch_shapes=[pltpu.VMEM((tm, D), F32)],
        compiler_params=_cp(2))(a, wdn, x, target)


def _ffn_bwd_act(name, dyb, h, wup, wdn):
    T, D = h.shape
    FB = wup.shape[-1]
    tm = _tile(T, 512)

    def body(d_ref, h_ref, wu_ref, wd_ref, o_ref):
        da = 0.5 * _nt(d_ref[...], wd_ref[...])
        hb = h_ref[...]
        gg = _nn(hb, wu_ref[0])
        uu = _nn(hb, wu_ref[1])
        sg = jax.nn.sigmoid(gg)
        o_ref[0] = (da * uu * (sg * (1.0 + gg * (1.0 - sg)))).astype(BF)
        o_ref[1] = (da * (gg * sg)).astype(BF)

    return pl.pallas_call(
        body, name=name, grid=(T // tm, N_FFN_BLK),
        in_specs=[pl.BlockSpec((tm, D), lambda i, j: (i, 0)),
                  pl.BlockSpec((tm, D), lambda i, j: (i, 0)),
                  pl.BlockSpec((2, None, D, FB), lambda i, j: (0, j, 0, 0)),
                  pl.BlockSpec((None, FB, D), lambda i, j: (j, 0, 0))],
        out_specs=pl.BlockSpec((2, None, tm, FB), lambda i, j: (0, j, i, 0)),
        out_shape=S((2, N_FFN_BLK, T, FB), BF),
        compiler_params=_cp(2))(dyb, h, wup, wdn)


def _ffn_dx(name, dgu, wup, x, g, dy):
    T, D = x.shape
    FB = wup.shape[-1]
    tm = _tile(T, 512)

    def body(d_ref, w_ref, x_ref, g_ref, dy_ref, dx_ref, dg_ref, acc_ref):
        i, j = pl.program_id(0), pl.program_id(1)
        p = _nt(d_ref[0], w_ref[0]) + _nt(d_ref[1], w_ref[1])

        @pl.when(j == 0)
        def _():
            acc_ref[...] = p

        @pl.when(j > 0)
        def _():
            acc_ref[...] += p

        @pl.when(j == N_FFN_BLK - 1)
        def _():
            dx, dgr = _norm_bwd(acc_ref[...], x_ref[...], g_ref[...])
            dx_ref[...] = dx + dy_ref[...]
            _acc_rows(dg_ref, i == 0, dgr)

    return pl.pallas_call(
        body, name=name, grid=(T // tm, N_FFN_BLK),
        in_specs=[pl.BlockSpec((2, None, tm, FB), lambda i, j: (0, j, i, 0)),
                  pl.BlockSpec((2, None, D, FB), lambda i, j: (0, j, 0, 0)),
                  pl.BlockSpec((tm, D), lambda i, j: (i, 0)),
                  pl.BlockSpec((1, D), lambda i, j: (0, 0)),
                  pl.BlockSpec((tm, D), lambda i, j: (i, 0))],
        out_specs=[pl.BlockSpec((tm, D), lambda i, j: (i, 0)),
                   pl.BlockSpec((1, D), lambda i, j: (0, 0))],
        out_shape=[S((T, D), F32), S((1, D), F32)],
        scratch_shapes=[pltpu.VMEM((tm, D), F32)],
        compiler_params=_cp(2))(dgu, wup, x, g, dy)


def _tn_matmul(name, a, a_spec, b, b_spec, out_shape, out_spec, grid, acc_shape, scale=1.0):
    nk = grid[1]

    def body(a_ref, b_ref, o_ref, acc_ref):
        k = pl.program_id(1)
        p = _tn(a_ref[...], b_ref[...])

        @pl.when(k == 0)
        def _():
            acc_ref[...] = p

        @pl.when(k > 0)
        def _():
            acc_ref[...] += p

        @pl.when(k == nk - 1)
        def _():
            o_ref[...] = (acc_ref[...] * scale).astype(o_ref.dtype)

    return pl.pallas_call(
        body, name=name, grid=grid, in_specs=[a_spec, b_spec], out_specs=out_spec, out_shape=out_shape,
        scratch_shapes=[pltpu.VMEM(acc_shape, F32)], compiler_params=_cp(2))(a, b)


def _ffn_dw(name, h, dgu, a, dyb):
    T, D = h.shape
    FB = a.shape[-1]
    tk = _tile(T, 512)
    nk = T // tk
    dgu8 = dgu.reshape(2 * N_FFN_BLK, T, FB)
    dwup = _tn_matmul(
        name + "_dwup", h, pl.BlockSpec((tk, D), lambda j, k: (k, 0)),
        dgu8, pl.BlockSpec((None, tk, FB), lambda j, k: (j, k, 0)),
        S((2 * N_FFN_BLK, D, FB), BF), pl.BlockSpec((None, D, FB), lambda j, k: (j, 0, 0)),
        (2 * N_FFN_BLK, nk), (D, FB))
    dwdn = _tn_matmul(
        name + "_dwdn", a, pl.BlockSpec((None, tk, FB), lambda j, k: (j, k, 0)),
        dyb, pl.BlockSpec((tk, D), lambda j, k: (k, 0)),
        S((N_FFN_BLK, FB, D), BF), pl.BlockSpec((None, FB, D), lambda j, k: (j, 0, 0)),
        (N_FFN_BLK, nk), (FB, D), scale=0.5)
    return dwup, dwdn


def _mix_proj(x, g, wz):
    T, D = x.shape
    tm = _tile(T, 256)

    def body(x_ref, g_ref, w_ref, z_ref, h_ref):
        xf = x_ref[...]
        hb = (xf * _rstd(xf) * g_ref[...]).astype(BF)
        h_ref[...] = hb
        z_ref[...] = _nn(hb, w_ref[...])

    return pl.pallas_call(
        body, name="mix_proj", grid=(T // tm,),
        in_specs=[pl.BlockSpec((tm, D), lambda i: (i, 0)),
                  pl.BlockSpec((1, D), lambda i: (0, 0)),
                  pl.BlockSpec((D, ZW), lambda i: (0, 0))],
        out_specs=[pl.BlockSpec((tm, ZW), lambda i: (i, 0)),
                   pl.BlockSpec((tm, D), lambda i: (i, 0))],
        out_shape=[S((T, ZW), F32), S((T, D), BF)],
        compiler_params=_cp(1))(x, g, wz)


def _tri(n, lower):
    r = lax.broadcasted_iota(jnp.int32, (n, n), 0)
    c = lax.broadcasted_iota(jnp.int32, (n, n), 1)
    return (r >= c) if lower else (r <= c)


def _spatial_mix(vgn_b, ws_ref, bst, tm):
    tril = _tri(CHUNK, True)
    wms = [jnp.where(tril, ws_ref[g], 0.0).astype(BF) for g in range(GMLP_G)]
    rows = []
    for c in range(tm // CHUNK):
        cols = []
        for g in range(GMLP_G):
            vs = vgn_b[c * CHUNK:(c + 1) * CHUNK, g * GMLP_GD:(g + 1) * GMLP_GD]
            cols.append(_nn(wms[g], vs) + bst[:, g:g + 1])
        rows.append(jnp.concatenate(cols, axis=1))
    return jnp.concatenate(rows, axis=0), wms


def _mix_prep(z, bf128, g_q, g_k, g_sgu, w_s, b_st, g_go):
    T = z.shape[0]
    tm = _tile(T, 256)

    def body(z_ref, bf_ref, gq_ref, gk_ref, gs_ref, ws_ref, bst_ref, go_ref,
             q_ref, k_ref, v_ref, c_ref, y_ref, carry_ref):
        i = pl.program_id(0)

        @pl.when(i == 0)
        def _():
            carry_ref[...] = jnp.zeros_like(carry_ref)

        for h in range(FOX_HEADS):
            hs = slice(h * FOX_HD, (h + 1) * FOX_HD)
            qh = z_ref[:, Z_Q + h * FOX_HD:Z_Q + (h + 1) * FOX_HD]
            kh = z_ref[:, Z_K + h * FOX_HD:Z_K + (h + 1) * FOX_HD]
            q_ref[:, hs] = (qh * _rstd(qh) * gq_ref[...] * 0.125).astype(BF)
            k_ref[:, hs] = (kh * _rstd(kh) * gk_ref[...]).astype(BF)
        v_ref[...] = z_ref[:, Z_V:Z_V + FOX_W].astype(BF)

        fl = z_ref[:, Z_F:Z_F + LANES] + bf_ref[...]
        logf = jnp.minimum(fl, 0.0) - jnp.log1p(jnp.exp(-jnp.abs(fl)))
        csum = _hi(_tri(tm, True).astype(F32), logf) + carry_ref[...]
        c_ref[...] = csum
        carry_ref[...] = csum[tm - 1:tm, :]

        u = _gelu(z_ref[:, Z_U:Z_U + GMLP_W])
        vg = _gelu(z_ref[:, Z_G:Z_G + GMLP_W])
        vgn = (vg * _rstd(vg) * gs_ref[...]).astype(BF)
        mixed, _ = _spatial_mix(vgn, ws_ref, bst_ref[...], tm)
        sgu = u * mixed
        y_ref[...] = (sgu * _rstd(sgu) * go_ref[...]).astype(BF)

    row = lambda i: (i, 0)
    fix2 = lambda i: (0, 0)
    return pl.pallas_call(
        body, name="mix_prep", grid=(T // tm,),
        in_specs=[pl.BlockSpec((tm, ZW), row),
                  pl.BlockSpec((1, LANES), fix2), pl.BlockSpec((1, FOX_HD), fix2), pl.BlockSpec((1, FOX_HD), fix2),
                  pl.BlockSpec((1, GMLP_W), fix2), pl.BlockSpec((GMLP_G, CHUNK, CHUNK), lambda i: (0, 0, 0)),
                  pl.BlockSpec((CHUNK, GMLP_G), fix2), pl.BlockSpec((1, GMLP_W), fix2)],
        out_specs=[pl.BlockSpec((tm, FOX_W), row), pl.BlockSpec((tm, FOX_W), row), pl.BlockSpec((tm, FOX_W), row),
                   pl.BlockSpec((tm, LANES), row), pl.BlockSpec((tm, GMLP_W), row)],
        out_shape=[S((T, FOX_W), BF), S((T, FOX_W), BF), S((T, FOX_W), BF), S((T, LANES), F32), S((T, GMLP_W), BF)],
        scratch_shapes=[pltpu.VMEM((1, LANES), F32)],
        compiler_params=_cp(1))(z, bf128, g_q, g_k, g_sgu, w_s, b_st, g_go)


def _fox_fwd(q, k, v, ccol, crow):
    T = q.shape[0]
    tq = _tile(T, 512)
    nq = T // tq

    def body(q_ref, k_ref, v_ref, cc_ref, cr_ref, o_ref, lse_ref, m_sc, l_sc, acc_sc):
        i, j = pl.program_id(0), pl.program_id(1)

        @pl.when(j == 0)
        def _():
            m_sc[...] = jnp.full(m_sc.shape, NEG, F32)
            l_sc[...] = jnp.zeros_like(l_sc)
            acc_sc[...] = jnp.zeros_like(acc_sc)

        def step(masked):
            cc = cc_ref[...]
            cr = cr_ref[...]
            mask = _tri(tq, True) if masked else None
            for h in range(FOX_HEADS):
                hs = slice(h * FOX_HD, (h + 1) * FOX_HD)
                s = _nt(q_ref[:, hs], k_ref[:, hs]) + (cc[:, h:h + 1] - cr[h:h + 1, :])
                if masked:
                    s = jnp.where(mask, s, NEG)
                m_prev = m_sc[h]
                m_new = jnp.maximum(m_prev, jnp.max(s, axis=1, keepdims=True))
                alpha = jnp.exp(m_prev - m_new)
                p = jnp.exp(s - m_new)
                l_sc[h] = alpha * l_sc[h] + jnp.sum(p, axis=1, keepdims=True)
                acc_sc[:, hs] = alpha * acc_sc[:, hs] + _nn(p.astype(BF), v_ref[:, hs])
                m_sc[h] = m_new

        @pl.when(j < i)
        def _():
            step(False)

        @pl.when(j == i)
        def _():
            step(True)
            lse_ref[...] = jnp.zeros_like(lse_ref)
            for h in range(FOX_HEADS):
                hs = slice(h * FOX_HD, (h + 1) * FOX_HD)
                o_ref[:, hs] = acc_sc[:, hs] / l_sc[h]
                lse_ref[:, h:h + 1] = m_sc[h] + jnp.log(l_sc[h])

    qi = lambda i, j: (i, 0)
    kj = lambda i, j: (jnp.minimum(i, j), 0)
    return pl.pallas_call(
        body, name="fox_fwd", grid=(nq, nq),
        in_specs=[pl.BlockSpec((tq, FOX_W), qi), pl.BlockSpec((tq, FOX_W), kj), pl.BlockSpec((tq, FOX_W), kj),
                  pl.BlockSpec((tq, LANES), qi), pl.BlockSpec((FOX_HEADS, tq), lambda i, j: (0, jnp.minimum(i, j)))],
        out_specs=[pl.BlockSpec((tq, FOX_W), qi), pl.BlockSpec((tq, LANES), qi)],
        out_shape=[S((T, FOX_W), F32), S((T, LANES), F32)],
        scratch_shapes=[pltpu.VMEM((FOX_HEADS, tq, 1), F32), pltpu.VMEM((FOX_HEADS, tq, 1), F32),
                        pltpu.VMEM((tq, FOX_W), F32)],
        compiler_params=_cp(2))(q, k, v, ccol, crow)


def _fox_bwd(q, k, v, dob, lse, dsum, ccol, crow):
    T = q.shape[0]
    tq = _tile(T, 512)
    nq = T // tq

    def body(q_ref, k_ref, v_ref, do_ref, lse_ref, ds_ref, cc_ref, cr_ref,
             dq_ref, dk_ref, dv_ref, dcq_ref, dck_ref):
        j, i = pl.program_id(0), pl.program_id(1)

        @pl.when(jnp.logical_and(i == 0, j == 0))
        def _():
            dq_ref[...] = jnp.zeros_like(dq_ref)
            dcq_ref[...] = jnp.zeros_like(dcq_ref)

        @pl.when(i == 0)
        def _():
            dk_ref[...] = jnp.zeros_like(dk_ref)
            dv_ref[...] = jnp.zeros_like(dv_ref)
            dck_ref[...] = jnp.zeros_like(dck_ref)

        def step(masked):
            rows = pl.ds(pl.multiple_of(i * tq, tq), tq)
            cc = cc_ref[...]
            cr = cr_ref[...]
            lse_t = lse_ref[...]
            dsum_t = ds_ref[...]
            mask = _tri(tq, True) if masked else None
            for h in range(FOX_HEADS):
                hs = slice(h * FOX_HD, (h + 1) * FOX_HD)
                qh, kh, vh, doh = q_ref[:, hs], k_ref[:, hs], v_ref[:, hs], do_ref[:, hs]
                s = _nt(qh, kh) + (cc[:, h:h + 1] - cr[h:h + 1, :])
                if masked:
                    s = jnp.where(mask, s, NEG)
                p = jnp.exp(s - lse_t[:, h:h + 1])
                dp = _nt(doh, vh)
                ds = p * (dp - dsum_t[:, h:h + 1])
                dsb = ds.astype(BF)
                dv_ref[:, hs] += _tn(p.astype(BF), doh)
                dk_ref[:, hs] += _tn(dsb, qh)
                dq_ref[rows, hs] += _nn(dsb, kh)
                dcq_ref[rows, h:h + 1] += jnp.sum(ds, axis=1, keepdims=True)
                dck_ref[h:h + 1, :] -= jnp.sum(ds, axis=0, keepdims=True)

        @pl.when(i > j)
        def _():
            step(False)

        @pl.when(i == j)
        def _():
            step(True)

    qi = lambda j, i: (jnp.maximum(i, j), 0)
    kj = lambda j, i: (j, 0)
    whole = lambda j, i: (0, 0)
    return pl.pallas_call(
        body, name="fox_bwd", grid=(nq, nq),
        in_specs=[pl.BlockSpec((tq, FOX_W), qi), pl.BlockSpec((tq, FOX_W), kj), pl.BlockSpec((tq, FOX_W), kj),
                  pl.BlockSpec((tq, FOX_W), qi), pl.BlockSpec((tq, LANES), qi), pl.BlockSpec((tq, LANES), qi),
                  pl.BlockSpec((tq, LANES), qi), pl.BlockSpec((FOX_HEADS, tq), lambda j, i: (0, j))],
        out_specs=[pl.BlockSpec((T, FOX_W), whole), pl.BlockSpec((tq, FOX_W), kj), pl.BlockSpec((tq, FOX_W), kj),
                   pl.BlockSpec((T, LANES), whole), pl.BlockSpec((FOX_HEADS, tq), lambda j, i: (0, j))],
        out_shape=[S((T, FOX_W), F32), S((T, FOX_W), F32), S((T, FOX_W), F32), S((T, LANES), F32),
                   S((FOX_HEADS, T), F32)],
        compiler_params=_cp(2))(q, k, v, dob, lse, dsum, ccol, crow)


def _mix_out(attn, yg, g_fo, wout, x):
    T, D = x.shape
    tm = _tile(T, 512)

    def body(a_ref, y_ref, g_ref, w_ref, x_ref, o_ref):
        at = a_ref[...]
        yf = (at * _rstd(at) * g_ref[...]).astype(BF)
        o_ref[...] = x_ref[...] + _nn(yf, w_ref[:FOX_W, :]) + _nn(y_ref[...], w_ref[FOX_W:, :])

    row = lambda i: (i, 0)
    return pl.pallas_call(
        body, name="mix_out", grid=(T // tm,),
        in_specs=[pl.BlockSpec((tm, FOX_W), row), pl.BlockSpec((tm, GMLP_W), row),
                  pl.BlockSpec((1, FOX_W), lambda i: (0, 0)), pl.BlockSpec((D, D), lambda i: (0, 0)),
                  pl.BlockSpec((tm, D), row)],
        out_specs=pl.BlockSpec((tm, D), row),
        out_shape=S((T, D), F32),
        compiler_params=_cp(1))(attn, yg, g_fo, wout, x)


def _mix_out_bwd(dx, attn, yg, g_fo, wout):
    T, D = dx.shape
    tm = _tile(T, 256)
    n = T // tm

    def body(dx_ref, a_ref, y_ref, g_ref, w_ref, da_ref, dsum_ref, dyg_ref, dw_ref, dg_ref, acc_ref):
        i = pl.program_id(0)
        dxb = dx_ref[...].astype(BF)
        at = a_ref[...]
        yf = (at * _rstd(at) * g_ref[...]).astype(BF)
        dy = _nt(dxb, w_ref[...])
        p_top = _tn(yf, dxb)
        p_bot = _tn(y_ref[...], dxb)

        @pl.when(i == 0)
        def _():
            acc_ref[:FOX_W, :] = p_top
            acc_ref[FOX_W:, :] = p_bot

        @pl.when(i > 0)
        def _():
            acc_ref[:FOX_W, :] += p_top
            acc_ref[FOX_W:, :] += p_bot

        @pl.when(i == n - 1)
        def _():
            dw_ref[...] = acc_ref[...].astype(BF)

        dat, dgr = _norm_bwd(dy[:, :FOX_W], at, g_ref[...])
        _acc_rows(dg_ref, i == 0, dgr)
        da_ref[...] = dat.astype(BF)
        dyg_ref[...] = dy[:, FOX_W:]
        prod = dat * at
        dsum_ref[...] = jnp.zeros_like(dsum_ref)
        for h in range(FOX_HEADS):
            dsum_ref[:, h:h + 1] = jnp.sum(prod[:, h * FOX_HD:(h + 1) * FOX_HD], axis=1, keepdims=True)

    row = lambda i: (i, 0)
    fix = lambda i: (0, 0)
    return pl.pallas_call(
        body, name="mix_out_bwd", grid=(n,),
        in_specs=[pl.BlockSpec((tm, D), row), pl.BlockSpec((tm, FOX_W), row), pl.BlockSpec((tm, GMLP_W), row),
                  pl.BlockSpec((1, FOX_W), fix), pl.BlockSpec((D, D), fix)],
        out_specs=[pl.BlockSpec((tm, FOX_W), row), pl.BlockSpec((tm, LANES), row), pl.BlockSpec((tm, GMLP_W), row),
                   pl.BlockSpec((D, D), fix), pl.BlockSpec((1, FOX_W), fix)],
        out_shape=[S((T, FOX_W), BF), S((T, LANES), F32), S((T, GMLP_W), F32), S((D, D), BF), S((1, FOX_W), F32)],
        scratch_shapes=[pltpu.VMEM((D, D), F32)],
        compiler_params=_cp(1))(dx, attn, yg, g_fo, wout)


def _mix_prep_bwd(z, dq, dk, dv, dcq, dck, dyg, bf128, g_q, g_k, g_sgu, w_s, b_st, g_go):
    T = z.shape[0]
    tm = _tile(T, 256)
    n = T // tm

    def body(z_ref, dq_ref, dk_ref, dv_ref, dcq_ref, dck_ref, dyg_ref, bf_ref, gq_ref, gk_ref, gs_ref, ws_ref,
             bst_ref, go_ref, dz_ref, dgq_ref, dgk_ref, dgs_ref, dgo_ref, dws_ref, dbst_ref, dbf_ref, carry_ref):
        i = pl.program_id(0)
        first = i == 0

        @pl.when(first)
        def _():
            carry_ref[...] = jnp.zeros_like(carry_ref)

        gq_rows, gk_rows = [], []
        for h in range(FOX_HEADS):
            hs = slice(h * FOX_HD, (h + 1) * FOX_HD)
            dqh, gqr = _norm_bwd(dq_ref[:, hs] * 0.125, z_ref[:, Z_Q + h * FOX_HD:Z_Q + (h + 1) * FOX_HD], gq_ref[...])
            dkh, gkr = _norm_bwd(dk_ref[:, hs], z_ref[:, Z_K + h * FOX_HD:Z_K + (h + 1) * FOX_HD], gk_ref[...])
            dz_ref[:, Z_Q + h * FOX_HD:Z_Q + (h + 1) * FOX_HD] = dqh.astype(BF)
            dz_ref[:, Z_K + h * FOX_HD:Z_K + (h + 1) * FOX_HD] = dkh.astype(BF)
            gq_rows.append(gqr)
            gk_rows.append(gkr)
        _acc_rows(dgq_ref, first, functools.reduce(lambda a, b: a + b, gq_rows))
        _acc_rows(dgk_ref, first, functools.reduce(lambda a, b: a + b, gk_rows))
        dz_ref[:, Z_V:Z_V + FOX_W] = dv_ref[...].astype(BF)

        dc = dcq_ref[...] + dck_ref[...]
        dlogf = _hi(_tri(tm, False).astype(F32), dc) + carry_ref[...]
        carry_ref[...] = dlogf[0:1, :]
        fl = z_ref[:, Z_F:Z_F + LANES] + bf_ref[...]
        lane = lax.broadcasted_iota(jnp.int32, (tm, LANES), 1)
        df = jnp.where(lane < FOX_HEADS, dlogf * jax.nn.sigmoid(-fl), 0.0)
        dz_ref[:, Z_F:Z_F + LANES] = df.astype(BF)
        _acc_rows(dbf_ref, first, df)

        u_pre = z_ref[:, Z_U:Z_U + GMLP_W]
        vg_pre = z_ref[:, Z_G:Z_G + GMLP_W]
        u = _gelu(u_pre)
        vg = _gelu(vg_pre)
        vgn = (vg * _rstd(vg) * gs_ref[...]).astype(BF)
        bst = bst_ref[...]
        mixed, wms = _spatial_mix(vgn, ws_ref, bst, tm)
        sgu = u * mixed
        dsgu, gor = _norm_bwd(dyg_ref[...], sgu, go_ref[...])
        _acc_rows(dgo_ref, first, gor)
        du = dsgu * mixed
        dmixed = dsgu * u
        dmb = dmixed.astype(BF)
        tril = _tri(CHUNK, True)
        dvgn_rows = []
        dws = [None] * GMLP_G
        dbs = [None] * GMLP_G
        for c in range(tm // CHUNK):
            cs = slice(c * CHUNK, (c + 1) * CHUNK)
            cols = []
            for g in range(GMLP_G):
                gs = slice(g * GMLP_GD, (g + 1) * GMLP_GD)
                dmc = dmb[cs, gs]
                pw = _nt(dmc, vgn[cs, gs])
                pb = jnp.sum(dmixed[cs, gs], axis=1, keepdims=True)
                dws[g] = pw if dws[g] is None else dws[g] + pw
                dbs[g] = pb if dbs[g] is None else dbs[g] + pb
                cols.append(_tn(wms[g], dmc))
            dvgn_rows.append(jnp.concatenate(cols, axis=1))
        dvgn = jnp.concatenate(dvgn_rows, axis=0)
        dbs_t = jnp.concatenate(dbs, axis=1)
        for g in range(GMLP_G):
            dwg = jnp.where(tril, dws[g], 0.0)

            @pl.when(first)
            def _():
                dws_ref[g] = dwg

            @pl.when(jnp.logical_not(first))
            def _():
                dws_ref[g] += dwg

        @pl.when(first)
        def _():
            dbst_ref[...] = dbs_t

        @pl.when(jnp.logical_not(first))
        def _():
            dbst_ref[...] += dbs_t

        dvg, gsr = _norm_bwd(dvgn, vg, gs_ref[...])
        _acc_rows(dgs_ref, first, gsr)
        dz_ref[:, Z_U:Z_U + GMLP_W] = (du * _gelu_grad(u_pre)).astype(BF)
        dz_ref[:, Z_G:Z_G + GMLP_W] = (dvg * _gelu_grad(vg_pre)).astype(BF)

    rev = lambda i: (n - 1 - i, 0)
    fix = lambda i: (0, 0)
    fix3 = lambda i: (0, 0, 0)
    return pl.pallas_call(
        body, name="mix_prep_bwd", grid=(n,),
        in_specs=[pl.BlockSpec((tm, ZW), rev), pl.BlockSpec((tm, FOX_W), rev), pl.BlockSpec((tm, FOX_W), rev),
                  pl.BlockSpec((tm, FOX_W), rev), pl.BlockSpec((tm, LANES), rev), pl.BlockSpec((tm, LANES), rev),
                  pl.BlockSpec((tm, GMLP_W), rev),
                  pl.BlockSpec((1, LANES), fix), pl.BlockSpec((1, FOX_HD), fix), pl.BlockSpec((1, FOX_HD), fix),
                  pl.BlockSpec((1, GMLP_W), fix), pl.BlockSpec((GMLP_G, CHUNK, CHUNK), fix3),
                  pl.BlockSpec((CHUNK, GMLP_G), fix), pl.BlockSpec((1, GMLP_W), fix)],
        out_specs=[pl.BlockSpec((tm, ZW), rev), pl.BlockSpec((1, FOX_HD), fix), pl.BlockSpec((1, FOX_HD), fix),
                   pl.BlockSpec((1, GMLP_W), fix), pl.BlockSpec((1, GMLP_W), fix),
                   pl.BlockSpec((GMLP_G, CHUNK, CHUNK), fix3), pl.BlockSpec((CHUNK, GMLP_G), fix),
                   pl.BlockSpec((1, LANES), fix)],
        out_shape=[S((T, ZW), BF), S((1, FOX_HD), F32), S((1, FOX_HD), F32), S((1, GMLP_W), F32), S((1, GMLP_W), F32),
                   S((GMLP_G, CHUNK, CHUNK), F32), S((CHUNK, GMLP_G), F32), S((1, LANES), F32)],
        scratch_shapes=[pltpu.VMEM((1, LANES), F32)],
        compiler_params=_cp(1))(z, dq, dk, dv, dcq, dck, dyg, bf128, g_q, g_k, g_sgu, w_s, b_st, g_go)


def _mix_proj_bwd(dz, wz, x, g, dy):
    T, D = x.shape
    tm = _tile(T, 256)

    def body(dz_ref, w_ref, x_ref, g_ref, dy_ref, dx_ref, dxb_ref, dg_ref):
        dh = _nt(dz_ref[...], w_ref[...])
        dx, dgr = _norm_bwd(dh, x_ref[...], g_ref[...])
        dx = dx + dy_ref[...]
        dx_ref[...] = dx
        dxb_ref[...] = dx.astype(BF)
        _acc_rows(dg_ref, pl.program_id(0) == 0, dgr)

    row = lambda i: (i, 0)
    fix = lambda i: (0, 0)
    return pl.pallas_call(
        body, name="mix_proj_bwd", grid=(T // tm,),
        in_specs=[pl.BlockSpec((tm, ZW), row), pl.BlockSpec((D, ZW), fix), pl.BlockSpec((tm, D), row),
                  pl.BlockSpec((1, D), fix), pl.BlockSpec((tm, D), row)],
        out_specs=[pl.BlockSpec((tm, D), row), pl.BlockSpec((tm, D), row), pl.BlockSpec((1, D), fix)],
        out_shape=[S((T, D), F32), S((T, D), BF), S((1, D), F32)],
        compiler_params=_cp(1))(dz, wz, x, g, dy)


def _ca_kv(mem, g_mem, wckv, g_ck):
    M, D = mem.shape

    def body(m_ref, g_ref, w_ref, gk_ref, mn_ref, kr_ref, kn_ref, v_ref):
        mf = m_ref[...]
        mn = (mf * _rstd(mf) * g_ref[...]).astype(BF)
        mn_ref[...] = mn
        for h in range(CA_HEADS):
            kr = _nn(mn, w_ref[h])
            kr_ref[h] = kr
            kn_ref[h] = (kr * _rstd(kr) * gk_ref[...]).astype(BF)
            v_ref[h] = _nn(mn, w_ref[CA_HEADS + h]).astype(BF)

    hd = (CA_HEADS, M, CA_HD)
    return pl.pallas_call(
        body, name="ca_kv", out_shape=[S((M, D), BF), S(hd, F32), S(hd, BF), S(hd, BF)],
        compiler_params=pltpu.CompilerParams(vmem_limit_bytes=VMEM_LIMIT))(mem, g_mem, wckv, g_ck)


def _ca_tile_fwd(xt, gca, wcq, gcq, kn_ref, v_ref):
    hb = (xt * _rstd(xt) * gca).astype(BF)
    qc = _nn(hb, wcq)
    qr, qn, ps = [], [], []
    for h in range(CA_HEADS):
        qh = qc[:, h * CA_HD:(h + 1) * CA_HD]
        qnh = (qh * _rstd(qh) * gcq * 0.0625).astype(BF)
        s = _nt(qnh, kn_ref[h])
        e = jnp.exp(s - jnp.max(s, axis=1, keepdims=True))
        ps.append(e / jnp.sum(e, axis=1, keepdims=True))
        qr.append(qh)
        qn.append(qnh)
    return hb, qr, qn, ps


def _ca_fwd(x, g_ca, wcq, g_cq, kn, vv, wco):
    T, D = x.shape
    M = kn.shape[1]
    tm = _tile(T, 256)

    def body(x_ref, gca_ref, wcq_ref, gcq_ref, kn_ref, v_ref, wco_ref, o_ref, ob_sc):
        xt = x_ref[...]
        _, _, _, ps = _ca_tile_fwd(xt, gca_ref[...], wcq_ref[...], gcq_ref[...], kn_ref, v_ref)
        for h in range(CA_HEADS):
            ob_sc[:, h * CA_HD:(h + 1) * CA_HD] = _nn(ps[h].astype(BF), v_ref[h]).astype(BF)
        o_ref[...] = xt + _nn(ob_sc[...], wco_ref[...])

    row = lambda i: (i, 0)
    fix = lambda i: (0, 0)
    fix3 = lambda i: (0, 0, 0)
    return pl.pallas_call(
        body, name="ca_fwd", grid=(T // tm,),
        in_specs=[pl.BlockSpec((tm, D), row), pl.BlockSpec((1, D), fix), pl.BlockSpec((D, D), fix),
                  pl.BlockSpec((1, CA_HD), fix), pl.BlockSpec((CA_HEADS, M, CA_HD), fix3),
                  pl.BlockSpec((CA_HEADS, M, CA_HD), fix3), pl.BlockSpec((D, D), fix)],
        out_specs=pl.BlockSpec((tm, D), row), out_shape=S((T, D), F32),
        scratch_shapes=[pltpu.VMEM((tm, D), BF)],
        compiler_params=_cp(1))(x, g_ca, wcq, g_cq, kn, vv, wco)


def _ca_bwd(x, dy, g_ca, wcq, g_cq, kn, vv, wco):
    T, D = x.shape
    M = kn.shape[1]
    tm = _tile(T, 256)
    n = T // tm

    def body(x_ref, dy_ref, gca_ref, wcq_ref, gcq_ref, kn_ref, v_ref, wco_ref,
             dx_ref, dwq_ref, dwo_ref, dkn_ref, dv_ref, dgcq_ref, dgca_ref, aq_sc, ao_sc, ob_sc, dq_sc):
        i = pl.program_id(0)
        first = i == 0
        xt = x_ref[...]
        dyt = dy_ref[...]
        dyb = dyt.astype(BF)
        hb, qr, qn, ps = _ca_tile_fwd(xt, gca_ref[...], wcq_ref[...], gcq_ref[...], kn_ref, v_ref)
        do = _nt(dyb, wco_ref[...])
        gcq_rows = None
        for h in range(CA_HEADS):
            hs = slice(h * CA_HD, (h + 1) * CA_HD)
            p = ps[h]
            pb = p.astype(BF)
            ob_sc[:, hs] = _nn(pb, v_ref[h]).astype(BF)
            doh = do[:, hs].astype(BF)
            dp = _nt(doh, v_ref[h])
            ds = (p * (dp - jnp.sum(dp * p, axis=1, keepdims=True))).astype(BF)
            dvh = _tn(pb, doh)
            dkh = _tn(ds, qn[h])

            @pl.when(first)
            def _():
                dv_ref[h] = dvh
                dkn_ref[h] = dkh

            @pl.when(jnp.logical_not(first))
            def _():
                dv_ref[h] += dvh
                dkn_ref[h] += dkh

            dqn = _nn(ds, kn_ref[h]) * 0.0625
            dqh, gr = _norm_bwd(dqn, qr[h], gcq_ref[...])
            gcq_rows = gr if gcq_rows is None else gcq_rows + gr
            dq_sc[:, hs] = dqh.astype(BF)
        _acc_rows(dgcq_ref, first, gcq_rows)
        dqb = dq_sc[...]
        p_o = _tn(ob_sc[...], dyb)
        p_q = _tn(hb, dqb)

        @pl.when(first)
        def _():
            ao_sc[...] = p_o
            aq_sc[...] = p_q

        @pl.when(jnp.logical_not(first))
        def _():
            ao_sc[...] += p_o
            aq_sc[...] += p_q

        @pl.when(i == n - 1)
        def _():
            dwo_ref[...] = ao_sc[...].astype(BF)
            dwq_ref[...] = aq_sc[...].astype(BF)

        dh = _nt(dqb, wcq_ref[...])
        dx, gar = _norm_bwd(dh, xt, gca_ref[...])
        dx_ref[...] = dx + dyt
        _acc_rows(dgca_ref, first, gar)

    row = lambda i: (i, 0)
    fix = lambda i: (0, 0)
    fix3 = lambda i: (0, 0, 0)
    hd = (CA_HEADS, M, CA_HD)
    return pl.pallas_call(
        body, name="ca_bwd", grid=(n,),
        in_specs=[pl.BlockSpec((tm, D), row), pl.BlockSpec((tm, D), row), pl.BlockSpec((1, D), fix),
                  pl.BlockSpec((D, D), fix), pl.BlockSpec((1, CA_HD), fix), pl.BlockSpec(hd, fix3),
                  pl.BlockSpec(hd, fix3), pl.BlockSpec((D, D), fix)],
        out_specs=[pl.BlockSpec((tm, D), row), pl.BlockSpec((D, D), fix), pl.BlockSpec((D, D), fix),
                   pl.BlockSpec(hd, fix3), pl.BlockSpec(hd, fix3), pl.BlockSpec((1, CA_HD), fix),
                   pl.BlockSpec((1, D), fix)],
        out_shape=[S((T, D), F32), S((D, D), BF), S((D, D), BF), S(hd, F32), S(hd, F32), S((1, CA_HD), F32),
                   S((1, D), F32)],
        scratch_shapes=[pltpu.VMEM((D, D), F32), pltpu.VMEM((D, D), F32), pltpu.VMEM((tm, D), BF),
                        pltpu.VMEM((tm, D), BF)],
        compiler_params=_cp(1))(x, dy, g_ca, wcq, g_cq, kn, vv, wco)


def _ca_kv_bwd(mem, g_mem, mn, kraw, dkn, dvv, wckv, g_ck):
    M, D = mem.shape

    def body(m_ref, g_ref, mn_ref, kr_ref, dkn_ref, dv_ref, w_ref, gk_ref, dw_ref, dgk_ref, dgm_ref):
        mn = mn_ref[...]
        dmn = jnp.zeros((M, D), F32)
        gk_rows = None
        for h in range(CA_HEADS):
            dkr, gr = _norm_bwd(dkn_ref[h], kr_ref[h], gk_ref[...])
            gk_rows = gr if gk_rows is None else gk_rows + gr
            dkb = dkr.astype(BF)
            dvb = dv_ref[h].astype(BF)
            dw_ref[h] = _tn(mn, dkb).astype(BF)
            dw_ref[CA_HEADS + h] = _tn(mn, dvb).astype(BF)
            dmn = dmn + _nt(dkb, w_ref[h]) + _nt(dvb, w_ref[CA_HEADS + h])
        dgk_ref[...] = jnp.sum(gk_rows, axis=0, keepdims=True)
        mf = m_ref[...]
        dgm_ref[...] = jnp.sum(dmn * (mf * _rstd(mf)), axis=0, keepdims=True)

    return pl.pallas_call(
        body, name="ca_kv_bwd",
        out_shape=[S((2 * CA_HEADS, D, CA_HD), BF), S((1, CA_HD), F32), S((1, D), F32)],
        compiler_params=pltpu.CompilerParams(vmem_limit_bytes=VMEM_LIMIT))(mem, g_mem, mn, kraw, dkn, dvv, wckv, g_ck)


def _after(g, token):
    return g if token is None else g + token[0:1, 0:1]


def _local_step(x, mem, target, small, weights, emit):
    T, D = x.shape
    p = small
    bf128 = jnp.pad(p["b_f"], ((0, 0), (0, LANES - FOX_HEADS)))
    b_st = p["b_s"].T

    wup1 = weights("ffn1_up", x)["wup1"]
    a1, h1 = _ffn_up("ffn1_up", x, p["g_ffn1"], wup1)
    wdn1 = weights("ffn1_dn", h1)["wdn1"]
    x1 = _ffn_down("ffn1_down", a1, wdn1, x)
    wm = weights("mix", x1)
    z, h2 = _mix_proj(x1, p["g_mix"], wm["wz"])
    qs, kn, vb, ccol, yg = _mix_prep(z, bf128, p["g_q"], p["g_k"], p["g_sgu"], p["w_s"], b_st, p["g_gmlp_o"])
    crow = ccol[:, :FOX_HEADS].T
    attn, lse = _fox_fwd(qs, kn, vb, ccol, crow)
    x2 = _mix_out(attn, yg, p["g_fox_o"], wm["wout"], x1)
    wc = weights("ca", x2)
    mn, kraw, ckn, cvv = _ca_kv(mem, p["g_mem"], wc["wckv"], p["g_ck"])
    x3 = _ca_fwd(x2, p["g_ca"], wc["wcq"], p["g_cq"], ckn, cvv, wc["wco"])
    w2 = weights("ffn2", x3)
    a2, h4 = _ffn_up("ffn2_up", x3, p["g_ffn2"], w2["wup2"])
    dy4, dy4b, sq = _ffn_down_loss("ffn2_down", a2, w2["wdn2"], x3, target)

    gs = {}
    dgu2 = _ffn_bwd_act("ffn2_bwd_act", dy4b, h4, w2["wup2"], w2["wdn2"])
    dwup2, dwdn2 = _ffn_dw("ffn2", h4, dgu2, a2, dy4b)
    tok = emit("ffn2", {"wup2": dwup2, "wdn2": dwdn2})
    dx3, gs["g_ffn2"] = _ffn_dx("ffn2_dx", dgu2, w2["wup2"], x3, _after(p["g_ffn2"], tok), dy4)

    dx2, dwcq, dwco, dckn, dcvv, gs["g_cq"], gs["g_ca"] = _ca_bwd(
        x2, dx3, p["g_ca"], wc["wcq"], p["g_cq"], ckn, cvv, wc["wco"])
    dwckv, gs["g_ck"], gs["g_mem"] = _ca_kv_bwd(mem, p["g_mem"], mn, kraw, dckn, dcvv, wc["wckv"], p["g_ck"])

    dattn, dsum, dyg, dwout, gs["g_fox_o"] = _mix_out_bwd(dx2, attn, yg, p["g_fox_o"], wm["wout"])
    dq, dk, dv, dcq, dck = _fox_bwd(qs, kn, vb, dattn, lse, dsum, ccol, crow)
    dck_col = jnp.pad(dck.T, ((0, 0), (0, LANES - FOX_HEADS)))
    dz, gs["g_q"], gs["g_k"], gs["g_sgu"], gs["g_gmlp_o"], gs["w_s"], dbst, dbf = _mix_prep_bwd(
        z, dq, dk, dv, dcq, dck_col, dyg, bf128, p["g_q"], p["g_k"], p["g_sgu"], p["w_s"], b_st, p["g_gmlp_o"])
    gs["b_s"] = dbst.T
    gs["b_f"] = dbf[:, :FOX_HEADS]
    tk = _tile(T, 512)
    zb = ZW // 3
    dwz = _tn_matmul(
        "mix_dwz", h2, pl.BlockSpec((tk, D), lambda j, k: (k, 0)), dz, pl.BlockSpec((tk, zb), lambda j, k: (k, j)),
        S((D, ZW), F32), pl.BlockSpec((D, zb), lambda j, k: (0, j)), (3, T // tk), (D, zb))
    tok = emit("mid", {"wcq": dwcq, "wco": dwco, "wckv": dwckv, "wout": dwout, "wz": dwz})
    dx1, dx1b, gs["g_mix"] = _mix_proj_bwd(dz, wm["wz"], x1, _after(p["g_mix"], tok), dx2)

    dgu1 = _ffn_bwd_act("ffn1_bwd_act", dx1b, h1, wup1, wdn1)
    dwup1, dwdn1 = _ffn_dw("ffn1", h1, dgu1, a1, dx1b)
    tok = emit("ffn1", {"wup1": dwup1, "wdn1": dwdn1})
    dx0, gs["g_ffn1"] = _ffn_dx("ffn1_dx", dgu1, wup1, x, _after(p["g_ffn1"], tok), dx1)
    return sq, dx0, gs


MESH = pl.DeviceIdType.MESH
HBM_SPEC = pl.BlockSpec(memory_space=pltpu.HBM)
N_PEER = N_DEV - 1


def _place():
    return lax.axis_index("x"), lax.axis_index("y"), lax.axis_index("c")


def _slot(px, py, pc):
    return 4 * px + 2 * py + pc


SEM_SPEC = pl.BlockSpec(memory_space=pltpu.SEMAPHORE)
ANY_SPEC = pl.BlockSpec(memory_space=pl.ANY)
DATAFLOW = pltpu.SideEffectType.DATAFLOW_SIDE_EFFECTING


def _hbm(a):
    return pltpu.with_memory_space_constraint(a, pltpu.HBM)


def _peer(x, y, c, r):
    return (1 - x if r & 4 else x, 1 - y if r & 2 else y, 1 - c if r & 1 else c)


def _place_own(srcs, whole):
    my = _slot(*_place())
    lands = []
    for s in srcs:
        blk = s[None] if whole else lax.dynamic_slice_in_dim(s, my, 1, 0)
        shape = (N_DEV,) + s.shape if whole else s.shape
        lands.append(lax.dynamic_update_slice_in_dim(lax.empty(shape, s.dtype), blk, my, 0))
    return lands


def _copy_start(name, srcs, lands, whole):
    n = len(srcs)

    def body(*refs):
        src, land = refs[:n], refs[n:2 * n]
        send, recv = refs[2 * n:3 * n], refs[3 * n:4 * n]
        token = refs[6 * n]
        x, y, c = _place()
        my = _slot(x, y, c)
        for a in range(n):
            for r in range(1, N_DEV):
                p = _peer(x, y, c, r)
                pltpu.make_async_remote_copy(
                    src_ref=src[a] if whole else src[a].at[_slot(*p)], dst_ref=land[a].at[my],
                    send_sem=send[a].at[r - 1], recv_sem=recv[a].at[r - 1], device_id=p, device_id_type=MESH).start()
        token[...] = jnp.zeros_like(token)

    out = pl.pallas_call(
        body, name=name,
        out_shape=([pltpu.SemaphoreType.DMA((N_PEER,))] * (2 * n)
                   + [pltpu.HBM(s.shape, s.dtype) for s in srcs] + [pltpu.HBM(s.shape, s.dtype) for s in lands]
                   + [S((8, LANES), F32)]),
        in_specs=[HBM_SPEC] * (2 * n),
        out_specs=[SEM_SPEC] * (2 * n) + [HBM_SPEC] * (2 * n) + [pl.BlockSpec(memory_space=pltpu.VMEM)],
        input_output_aliases={i: 2 * n + i for i in range(2 * n)},
        compiler_params=pltpu.CompilerParams(has_side_effects=DATAFLOW),
    )(*[_hbm(s) for s in srcs], *[_hbm(s) for s in lands])
    return out[:n], out[n:2 * n], out[2 * n:3 * n], out[3 * n:4 * n], out[4 * n]


def _copy_wait(name, srcs, lands, send, recv, after, whole):
    n = len(srcs)

    def body(*refs):
        src, land = refs[:n], refs[n:2 * n]
        snd, rcv = refs[2 * n:3 * n], refs[3 * n:4 * n]
        x, y, c = _place()
        for a in range(n):
            for r in range(1, N_DEV):
                p = _peer(x, y, c, r)
                ps = _slot(*p)
                cp = pltpu.make_async_remote_copy(
                    src_ref=src[a] if whole else src[a].at[ps], dst_ref=land[a].at[ps],
                    send_sem=snd[a].at[r - 1], recv_sem=rcv[a].at[r - 1], device_id=p, device_id_type=MESH)
                cp.wait_send()
                cp.wait_recv()

    out = pl.pallas_call(
        body, name=name,
        out_shape=[pltpu.HBM(s.shape, s.dtype) for s in srcs] + [pltpu.HBM(s.shape, s.dtype) for s in lands],
        in_specs=[HBM_SPEC] * (2 * n) + [SEM_SPEC] * (2 * n) + [ANY_SPEC],
        out_specs=[HBM_SPEC] * (2 * n),
        input_output_aliases={i: i for i in range(2 * n)},
        compiler_params=pltpu.CompilerParams(has_side_effects=DATAFLOW),
    )(*srcs, *lands, *send, *recv, after)
    return out[n:]


def _adamw(w, g, m, v):
    m2 = ADAM_B1 * m + (1.0 - ADAM_B1) * g
    v2 = ADAM_B2 * v + (1.0 - ADAM_B2) * (g * g)
    m_hat = m2 / (1.0 - ADAM_B1 ** ADAM_STEP)
    v_hat = v2 / (1.0 - ADAM_B2 ** ADAM_STEP)
    delta = -ADAM_LR * (m_hat / (jnp.sqrt(v_hat) + ADAM_EPS) + ADAM_WD * w)
    return delta, m2, v2


def _adamw_big(name, slots, w, m, v):
    R, C = w.shape
    tr = 256 if R % 256 == 0 else R

    def body(s_ref, w_ref, m_ref, v_ref, g_ref, d_ref, m2_ref, v2_ref):
        g = s_ref[0].astype(F32)
        for k in range(1, N_DEV):
            g = g + s_ref[k].astype(F32)
        d, m2, v2 = _adamw(w_ref[...], g, m_ref[...], v_ref[...])
        g_ref[...] = g
        d_ref[...] = d
        m2_ref[...] = m2
        v2_ref[...] = v2

    row = pl.BlockSpec((tr, C), lambda i: (i, 0))
    return pl.pallas_call(
        body, name=name, grid=(R // tr,),
        in_specs=[pl.BlockSpec((N_DEV, tr, C), lambda i: (0, i, 0)), row, row, row],
        out_specs=[row] * 4, out_shape=[S((R, C), F32)] * 4,
        compiler_params=_cp(1))(slots, w, m, v)


SMALL_ROWS = (("w_s", 1024), ("b_s", 8), ("g_ffn1", 8), ("g_mix", 8), ("g_ca", 8), ("g_mem", 8), ("g_ffn2", 8),
              ("g_sgu", 4), ("g_fox_o", 4), ("g_gmlp_o", 4), ("g_cq", 2), ("g_ck", 2), ("g_q", 1), ("g_k", 1),
              ("b_f", 1))
SMALL_P = 1096


def _pack_small(d):
    rows = []
    for name, r in SMALL_ROWS:
        flat = d[name].reshape(-1)
        rows.append(jnp.pad(flat, (0, r * LANES - flat.shape[0])).reshape(r, LANES))
    used = sum(r for _, r in SMALL_ROWS)
    rows.append(jnp.zeros((SMALL_P - used, LANES), F32))
    return jnp.concatenate(rows, axis=0)


def _unpack_small(packed, shapes):
    out, at = {}, 0
    for name, r in SMALL_ROWS:
        shape = shapes[name]
        size = 1
        for s in shape:
            size *= s
        out[name] = packed[at:at + r].reshape(-1)[:size].reshape(shape)
        at += r
    return out


def _small_reduce_adamw(part, w, m, v):
    P = part.shape[0]

    def body(p_ref, w_ref, m_ref, v_ref, g_ref, d_ref, m2_ref, v2_ref, all_ref, send_sems, recv_sems, local_sem):
        x, y, c = _place()
        me, sibling = (x, y, c), (x, y, 1 - c)
        chips = [(1 - x, y), (x, 1 - y), (1 - x, 1 - y)]

        def copy(k, block, to, src=None):
            dst = all_ref.at[_slot(*block)]
            return pltpu.make_async_remote_copy(
                src_ref=dst if src is None else src, dst_ref=dst, send_sem=send_sems.at[k], recv_sem=recv_sems.at[k],
                device_id=to, device_id_type=MESH)

        mine = pltpu.make_async_copy(p_ref, all_ref.at[_slot(*me)], local_sem)
        mine.start()
        first = [copy(0, me, sibling, src=p_ref)]
        first += [copy(1 + j, me, (*chip, c), src=p_ref) for j, chip in enumerate(chips)]
        for cp in first:
            cp.start()
        passed = [copy(4 + j, (*chip, c), sibling) for j, chip in enumerate(chips)]
        for j, chip in enumerate(chips):
            copy(1 + j, (*chip, c), me).wait_recv()
            passed[j].start()
        copy(0, sibling, me).wait_recv()
        for j, chip in enumerate(chips):
            copy(4 + j, (*chip, 1 - c), me).wait_recv()
        for cp in first + passed:
            cp.wait_send()
        mine.wait()

        g = all_ref[0]
        for k in range(1, N_DEV):
            g = g + all_ref[k]
        d, m2, v2 = _adamw(w_ref[...], g, m_ref[...], v_ref[...])
        g_ref[...] = g
        d_ref[...] = d
        m2_ref[...] = m2
        v2_ref[...] = v2

    vm = pl.BlockSpec(memory_space=pltpu.VMEM)
    return pl.pallas_call(
        body, name="small_reduce_adamw",
        out_shape=[S((P, LANES), F32)] * 4, in_specs=[vm] * 4, out_specs=[vm] * 4,
        scratch_shapes=[pltpu.VMEM((N_DEV, P, LANES), F32), pltpu.SemaphoreType.DMA((N_PEER,)),
                        pltpu.SemaphoreType.DMA((N_PEER,)), pltpu.SemaphoreType.DMA],
        compiler_params=pltpu.CompilerParams(vmem_limit_bytes=VMEM_LIMIT),
    )(part, w, m, v)


WEIGHTS = ('g_ffn1', 'w_ffn1_in', 'w_ffn1_out', 'g_mix', 'w_in', 'b_f', 'g_q', 'g_k', 'g_sgu', 'w_s', 'b_s',
           'g_fox_o', 'g_gmlp_o', 'w_out', 'g_ca', 'g_mem', 'w_cq', 'w_ckv', 'g_cq', 'g_ck', 'w_co', 'g_ffn2',
           'w_ffn2_in', 'w_ffn2_out')
BIG = ('w_ffn1_in', 'w_ffn1_out', 'w_in', 'w_out', 'w_cq', 'w_ckv', 'w_co', 'w_ffn2_in', 'w_ffn2_out')
GATHER_GROUPS = {"ffn1_up": ("w_ffn1_in",), "ffn1_dn": ("w_ffn1_out",), "mix": ("w_in", "w_out"),
                 "ca": ("w_cq", "w_ckv", "w_co"), "ffn2": ("w_ffn2_in", "w_ffn2_out")}
QKV_W = 3 * FOX_W
UV_OFF = QKV_W + FOX_HEADS


def kernel(x, mem, g_ffn1, w_ffn1_in, w_ffn1_out, g_mix, w_in, b_f, g_q, g_k, g_sgu, w_s, b_s, g_fox_o, g_gmlp_o, w_out, g_ca, g_mem, w_cq, w_ckv, g_cq, g_ck, w_co, g_ffn2, w_ffn2_in, w_ffn2_out, loss_target, m_g_ffn1, m_w_ffn1_in, m_w_ffn1_out, m_g_mix, m_w_in, m_b_f, m_g_q, m_g_k, m_g_sgu, m_w_s, m_b_s, m_g_fox_o, m_g_gmlp_o, m_w_out, m_g_ca, m_g_mem, m_w_cq, m_w_ckv, m_g_cq, m_g_ck, m_w_co, m_g_ffn2, m_w_ffn2_in, m_w_ffn2_out, v_g_ffn1, v_w_ffn1_in, v_w_ffn1_out, v_g_mix, v_w_in, v_b_f, v_g_q, v_g_k, v_g_sgu, v_w_s, v_b_s, v_g_fox_o, v_g_gmlp_o, v_w_out, v_g_ca, v_g_mem, v_w_cq, v_w_ckv, v_g_cq, v_g_ck, v_w_co, v_g_ffn2, v_w_ffn2_in, v_w_ffn2_out):
    args = dict(locals())
    w = {n: args[n] for n in WEIGHTS}
    mo = {n: args["m_" + n] for n in WEIGHTS}
    vo = {n: args["v_" + n] for n in WEIGHTS}
    D = D_MODEL

    shards = [w[n][0].astype(BF) for n in BIG]
    fb = shards[0].shape[-1]
    n_first = 2
    handles = {}

    def start_gather(name, names, arrays):
        snd, rcv, src, land, _ = _copy_start(name, arrays, _place_own(arrays, True), True)
        for i, n in enumerate(names):
            handles[n] = (src[i], land[i], snd[i], rcv[i])

    start_gather("gather_start_first", BIG[:n_first], shards[:n_first])

    def weights(group, after):
        names = GATHER_GROUPS[group]
        hs = [handles[n] for n in names]
        got = _copy_wait("gather_wait_" + group, [h[0] for h in hs], [h[1] for h in hs], [h[2] for h in hs],
                         [h[3] for h in hs], after, True)
        if group == "ffn1_up":
            rest = lax.optimization_barrier((tuple(shards[n_first:]), got[0]))[0]
            start_gather("gather_start_rest", BIG[n_first:], list(rest))
        got = dict(zip(names, got))
        if group == "ffn1_up":
            return {"wup1": got["w_ffn1_in"].reshape(2, N_FFN_BLK, D, fb)}
        if group == "ffn1_dn":
            return {"wdn1": got["w_ffn1_out"].reshape(N_FFN_BLK, fb, D)}
        if group == "mix":
            full = got["w_in"].transpose(1, 0, 2).reshape(D, -1)
            wz = jnp.concatenate([full[:, :QKV_W], full[:, UV_OFF:], full[:, QKV_W:UV_OFF],
                                  jnp.zeros((D, LANES - FOX_HEADS), BF)], axis=1)
            return {"wz": wz, "wout": got["w_out"].reshape(D, D)}
        if group == "ca":
            return {"wcq": got["w_cq"].reshape(D, D), "wco": got["w_co"].reshape(D, D), "wckv": got["w_ckv"]}
        return {"wup2": got["w_ffn2_in"].reshape(2, N_FFN_BLK, D, fb),
                "wdn2": got["w_ffn2_out"].reshape(N_FFN_BLK, fb, D)}

    flying = {}

    def emit(group, g):
        if group == "ffn2":
            parts = {"w_ffn2_in": g["wup2"], "w_ffn2_out": g["wdn2"].reshape(N_DEV, -1, D)}
        elif group == "ffn1":
            parts = {"w_ffn1_in": g["wup1"], "w_ffn1_out": g["wdn1"].reshape(N_DEV, -1, D)}
        else:
            gz = g["wz"]
            g_in = jnp.concatenate([gz[:, :QKV_W], gz[:, Z_F:Z_F + FOX_HEADS], gz[:, QKV_W:Z_F]], axis=1)
            parts = {"w_in": g_in.reshape(D, N_DEV, -1).transpose(1, 0, 2).astype(BF),
                     "w_out": g["wout"].reshape(N_DEV, -1, D), "w_cq": g["wcq"].reshape(N_DEV, -1, D),
                     "w_co": g["wco"].reshape(N_DEV, -1, D), "w_ckv": g["wckv"]}
        names = list(parts)
        srcs = [parts[n] for n in names]
        *copies, token = _copy_start("exchange_start_" + group, srcs, _place_own(srcs, False), False)
        flying[group] = (names, copies)
        return token

    small_names = [n for n, _ in SMALL_ROWS]
    small = {n: (w[n][0] if n == "w_s" or n == "b_s" else w[n]) for n in small_names}

    sq, dx0, gs = _local_step(x[0], mem[0], loss_target[0], small, weights, emit)
    loss = lax.psum(sq[0, 0], ("x", "y", "c")) * (0.5 / D)

    grad, delta, new_m, new_v = {}, {}, {}, {}

    def update(group, after):
        names, (snd, rcv, srcs, lands) = flying[group]
        slots = _copy_wait("exchange_wait_" + group, srcs, lands, snd, rcv, after, False)
        for n, sl in zip(names, slots):
            shape = w[n].shape
            g, d, m2, v2 = _adamw_big("adamw_" + n, sl, w[n][0], mo[n][0], vo[n][0])
            grad[n], delta[n], new_m[n], new_v[n] = (t.reshape(shape) for t in (g, d, m2, v2))
        return d

    last = update("ffn2", dx0)
    last = update("mid", last)
    shapes = {n: w[n].shape for n in small_names}
    packed = _small_reduce_adamw(_pack_small(gs), _pack_small({n: w[n] for n in small_names}),
                                 _pack_small({n: mo[n] for n in small_names}),
                                 _pack_small({n: vo[n] for n in small_names}))
    for store, pk in zip((grad, delta, new_m, new_v), packed):
        store.update(_unpack_small(pk, shapes))
    update("ffn1", packed[0])

    return (loss, dx0[None], *[grad[n] for n in WEIGHTS], *[delta[n] for n in WEIGHTS],
            *[new_m[n] for n in WEIGHTS], *[new_v[n] for n in WEIGHTS])
```

```python
import functools

import jax
import jax.numpy as jnp
from jax import lax
from jax.experimental import pallas as pl
from jax.experimental.pallas import tpu as pltpu

F32 = jnp.float32
BF = jnp.bfloat16
S = jax.ShapeDtypeStruct

N_DEV = 8
D_MODEL = 1024
FOX_HEADS, FOX_HD = 8, 64
FOX_W = 512
GMLP_G, GMLP_GD = 8, 64
GMLP_W = 512
CHUNK = 128
CA_HEADS, CA_HD = 4, 256
N_FFN_BLK = 4
ZW = 2688
Z_Q, Z_K, Z_V, Z_U, Z_G, Z_F = 0, 512, 1024, 1536, 2048, 2560
EPS = 1e-6
NEG = -1e30
LANES = 128

ADAM_LR, ADAM_B1, ADAM_B2, ADAM_EPS, ADAM_WD, ADAM_STEP = 0.001, 0.9, 0.999, 1e-08, 0.01, 10

VMEM_LIMIT = 52 * 2 ** 20


def _cp(n_axes):
    return pltpu.CompilerParams(dimension_semantics=("arbitrary",) * n_axes, vmem_limit_bytes=VMEM_LIMIT)


def _nn(a, b):
    return jnp.dot(a, b, preferred_element_type=F32)


def _nt(a, b):
    return lax.dot_general(a, b, (((1,), (1,)), ((), ())), preferred_element_type=F32)


def _tn(a, b):
    return lax.dot_general(a, b, (((0,), (0,)), ((), ())), preferred_element_type=F32)


def _hi(a, b):
    return jnp.dot(a, b, precision=lax.Precision.HIGHEST, preferred_element_type=F32)


def _rstd(x):
    return lax.rsqrt(jnp.mean(x * x, axis=-1, keepdims=True) + EPS)


def _norm_bwd(dy, x, g):
    r = _rstd(x)
    xh = x * r
    dxh = dy * g
    dx = r * (dxh - xh * jnp.mean(dxh * xh, axis=-1, keepdims=True))
    return dx, dy * xh


def _acc_rows(ref, first, val):
    srow = jnp.sum(val, axis=0, keepdims=True)

    @pl.when(first)
    def _():
        ref[...] = srow

    @pl.when(jnp.logical_not(first))
    def _():
        ref[...] += srow


def _gelu(x):
    c = 0.7978845608028654
    return 0.5 * x * (1.0 + jnp.tanh(c * (x + 0.044715 * x * x * x)))


def _gelu_grad(x):
    c = 0.7978845608028654
    t = jnp.tanh(c * (x + 0.044715 * x * x * x))
    return 0.5 * (1.0 + t) + 0.5 * x * (1.0 - t * t) * c * (1.0 + 3 * 0.044715 * x * x)


def _tile(n, pref):
    return pref if n % pref == 0 else n


def _ffn_up(name, x, g, wup):
    T, D = x.shape
    FB = wup.shape[-2]
    tm = _tile(T, 512)

    def body(x_ref, g_ref, w_ref, a_ref, h_ref):
        @pl.when(pl.program_id(1) == 0)
        def _():
            xf = x_ref[...]
            h_ref[...] = (xf * _rstd(xf) * g_ref[...]).astype(BF)

        hb = h_ref[...]
        gg = _nt(hb, w_ref[0])
        uu = _nt(hb, w_ref[1])
        a_ref[...] = (gg * jax.nn.sigmoid(gg) * uu).astype(BF)

    return pl.pallas_call(
        body, name=name, grid=(T // tm, N_FFN_BLK),
        in_specs=[pl.BlockSpec((tm, D), lambda i, j: (i, 0)),
                  pl.BlockSpec((1, D), lambda i, j: (0, 0)),
                  pl.BlockSpec((2, None, FB, D), lambda i, j: (0, j, 0, 0))],
        out_specs=[pl.BlockSpec((None, tm, FB), lambda i, j: (j, i, 0)),
                   pl.BlockSpec((tm, D), lambda i, j: (i, 0))],
        out_shape=[S((N_FFN_BLK, T, FB), BF), S((T, D), BF)],
        compiler_params=_cp(2))(x, g, wup)


def _ffn_down(name, a, wdn, x):
    _, T, FB = a.shape
    D = x.shape[1]
    tm = _tile(T, 512)

    def body(a_ref, w_ref, x_ref, o_ref):
        j = pl.program_id(1)
        p = 0.5 * _nn(a_ref[...], w_ref[...])

        @pl.when(j == 0)
        def _():
            o_ref[...] = x_ref[...] + p

        @pl.when(j > 0)
        def _():
            o_ref[...] += p

    return pl.pallas_call(
        body, name=name, grid=(T // tm, N_FFN_BLK),
        in_specs=[pl.BlockSpec((None, tm, FB), lambda i, j: (j, i, 0)),
                  pl.BlockSpec((None, FB, D), lambda i, j: (j, 0, 0)),
                  pl.BlockSpec((tm, D), lambda i, j: (i, 0))],
        out_specs=pl.BlockSpec((tm, D), lambda i, j: (i, 0)),
        out_shape=S((T, D), F32),
        compiler_params=_cp(2))(a, wdn, x)


def _ffn_down_loss(name, a, wdn, x, target):
    _, T, FB = a.shape
    D = x.shape[1]
    tm = _tile(T, 512)

    def body(a_ref, w_ref, x_ref, t_ref, d_ref, db_ref, loss_ref, acc_ref):
        i, j = pl.program_id(0), pl.program_id(1)
        p = 0.5 * _nn(a_ref[...], w_ref[...])

        @pl.when(j == 0)
        def _():
            acc_ref[...] = x_ref[...] + p

        @pl.when(j > 0)
        def _():
            acc_ref[...] += p

        @pl.when(j == N_FFN_BLK - 1)
        def _():
            diff = acc_ref[...] - t_ref[...]
            dy = diff * (1.0 / D)
            d_ref[...] = dy
            db_ref[...] = dy.astype(BF)
            sq = jnp.zeros((8, LANES), F32) + jnp.sum(diff * diff)

            @pl.when(i == 0)
            def _():
                loss_ref[...] = sq

            @pl.when(i > 0)
            def _():
                loss_ref[...] += sq

    return pl.pallas_call(
        body, name=name, grid=(T // tm, N_FFN_BLK),
        in_specs=[pl.BlockSpec((None, tm, FB), lambda i, j: (j, i, 0)),
                  pl.BlockSpec((None, FB, D), lambda i, j: (j, 0, 0)),
                  pl.BlockSpec((tm, D), lambda i, j: (i, 0)),
                  pl.BlockSpec((tm, D), lambda i, j: (i, 0))],
        out_specs=[pl.BlockSpec((tm, D), lambda i, j: (i, 0)),
                   pl.BlockSpec((tm, D), lambda i, j: (i, 0)),
                   pl.BlockSpec((8, LANES), lambda i, j: (0, 0))],
        out_shape=[S((T, D), F32), S((T, D), BF), S((8, LANES), F32)],
        scratch_shapes=[pltpu.VMEM((tm, D), F32)],
        compiler_params=_cp(2))(a, wdn, x, target)


def _ffn_bwd_act(name, dyb, h, wup, wdn):
    T, D = h.shape
    FB = wup.shape[-2]
    tm = _tile(T, 512)

    def body(d_ref, h_ref, wu_ref, wd_ref, o_ref):
        da = 0.5 * _nt(d_ref[...], wd_ref[...])
        hb = h_ref[...]
        gg = _nt(hb, wu_ref[0])
        uu = _nt(hb, wu_ref[1])
        sg = jax.nn.sigmoid(gg)
        o_ref[0] = (da * uu * (sg * (1.0 + gg * (1.0 - sg)))).astype(BF)
        o_ref[1] = (da * (gg * sg)).astype(BF)

    return pl.pallas_call(
        body, name=name, grid=(T // tm, N_FFN_BLK),
        in_specs=[pl.BlockSpec((tm, D), lambda i, j: (i, 0)),
                  pl.BlockSpec((tm, D), lambda i, j: (i, 0)),
                  pl.BlockSpec((2, None, FB, D), lambda i, j: (0, j, 0, 0)),
                  pl.BlockSpec((None, FB, D), lambda i, j: (j, 0, 0))],
        out_specs=pl.BlockSpec((2, None, tm, FB), lambda i, j: (0, j, i, 0)),
        out_shape=S((2, N_FFN_BLK, T, FB), BF),
        compiler_params=_cp(2))(dyb, h, wup, wdn)


def _ffn_dx(name, dgu, wup, x, g, dy):
    T, D = x.shape
    FB = wup.shape[-2]
    tm = _tile(T, 512)

    def body(d_ref, w_ref, x_ref, g_ref, dy_ref, dx_ref, dg_ref, acc_ref):
        i, j = pl.program_id(0), pl.program_id(1)
        p = _nn(d_ref[0], w_ref[0]) + _nn(d_ref[1], w_ref[1])

        @pl.when(j == 0)
        def _():
            acc_ref[...] = p

        @pl.when(j > 0)
        def _():
            acc_ref[...] += p

        @pl.when(j == N_FFN_BLK - 1)
        def _():
            dx, dgr = _norm_bwd(acc_ref[...], x_ref[...], g_ref[...])
            dx_ref[...] = dx + dy_ref[...]
            _acc_rows(dg_ref, i == 0, dgr)

    return pl.pallas_call(
        body, name=name, grid=(T // tm, N_FFN_BLK),
        in_specs=[pl.BlockSpec((2, None, tm, FB), lambda i, j: (0, j, i, 0)),
                  pl.BlockSpec((2, None, FB, D), lambda i, j: (0, j, 0, 0)),
                  pl.BlockSpec((tm, D), lambda i, j: (i, 0)),
                  pl.BlockSpec((1, D), lambda i, j: (0, 0)),
                  pl.BlockSpec((tm, D), lambda i, j: (i, 0))],
        out_specs=[pl.BlockSpec((tm, D), lambda i, j: (i, 0)),
                   pl.BlockSpec((1, D), lambda i, j: (0, 0))],
        out_shape=[S((T, D), F32), S((1, D), F32)],
        scratch_shapes=[pltpu.VMEM((tm, D), F32)],
        compiler_params=_cp(2))(dgu, wup, x, g, dy)


def _tn_matmul(name, a, a_spec, b, b_spec, out_shape, out_spec, grid, acc_shape, scale=1.0):
    nk = grid[1]

    def body(a_ref, b_ref, o_ref, acc_ref):
        k = pl.program_id(1)
        p = _tn(a_ref[...], b_ref[...])

        @pl.when(k == 0)
        def _():
            acc_ref[...] = p

        @pl.when(k > 0)
        def _():
            acc_ref[...] += p

        @pl.when(k == nk - 1)
        def _():
            o_ref[...] = (acc_ref[...] * scale).astype(o_ref.dtype)

    return pl.pallas_call(
        body, name=name, grid=grid, in_specs=[a_spec, b_spec], out_specs=out_spec, out_shape=out_shape,
        scratch_shapes=[pltpu.VMEM(acc_shape, F32)], compiler_params=_cp(2))(a, b)


def _ffn_dw(name, h, dgu, a, dyb):
    T, D = h.shape
    FB = a.shape[-1]
    tk = _tile(T, 512)
    nk = T // tk
    dgu8 = dgu.reshape(2 * N_FFN_BLK, T, FB)
    dwup = _tn_matmul(
        name + "_dwup", dgu8, pl.BlockSpec((None, tk, FB), lambda j, k: (j, k, 0)),
        h, pl.BlockSpec((tk, D), lambda j, k: (k, 0)),
        S((2 * N_FFN_BLK, FB, D), BF), pl.BlockSpec((None, FB, D), lambda j, k: (j, 0, 0)),
        (2 * N_FFN_BLK, nk), (FB, D))
    dwdn = _tn_matmul(
        name + "_dwdn", a, pl.BlockSpec((None, tk, FB), lambda j, k: (j, k, 0)),
        dyb, pl.BlockSpec((tk, D), lambda j, k: (k, 0)),
        S((N_FFN_BLK, FB, D), BF), pl.BlockSpec((None, FB, D), lambda j, k: (j, 0, 0)),
        (N_FFN_BLK, nk), (FB, D), scale=0.5)
    return dwup, dwdn


def _mix_proj(x, g, wz):
    T, D = x.shape
    tm = _tile(T, 256)

    def body(x_ref, g_ref, w_ref, z_ref, h_ref):
        xf = x_ref[...]
        hb = (xf * _rstd(xf) * g_ref[...]).astype(BF)
        h_ref[...] = hb
        z_ref[...] = _nt(hb, w_ref[...])

    return pl.pallas_call(
        body, name="mix_proj", grid=(T // tm,),
        in_specs=[pl.BlockSpec((tm, D), lambda i: (i, 0)),
                  pl.BlockSpec((1, D), lambda i: (0, 0)),
                  pl.BlockSpec((ZW, D), lambda i: (0, 0))],
        out_specs=[pl.BlockSpec((tm, ZW), lambda i: (i, 0)),
                   pl.BlockSpec((tm, D), lambda i: (i, 0))],
        out_shape=[S((T, ZW), F32), S((T, D), BF)],
        compiler_params=_cp(1))(x, g, wz)


def _tri(n, lower):
    r = lax.broadcasted_iota(jnp.int32, (n, n), 0)
    c = lax.broadcasted_iota(jnp.int32, (n, n), 1)
    return (r >= c) if lower else (r <= c)


def _spatial_mix(vgn_b, ws_ref, bst, tm):
    tril = _tri(CHUNK, True)
    wms = [jnp.where(tril, ws_ref[g], 0.0).astype(BF) for g in range(GMLP_G)]
    rows = []
    for c in range(tm // CHUNK):
        cols = []
        for g in range(GMLP_G):
            vs = vgn_b[c * CHUNK:(c + 1) * CHUNK, g * GMLP_GD:(g + 1) * GMLP_GD]
            cols.append(_nn(wms[g], vs) + bst[:, g:g + 1])
        rows.append(jnp.concatenate(cols, axis=1))
    return jnp.concatenate(rows, axis=0), wms


def _mix_prep(z, bf128, g_q, g_k, g_sgu, w_s, b_st, g_go):
    T = z.shape[0]
    tm = _tile(T, 256)

    def body(z_ref, bf_ref, gq_ref, gk_ref, gs_ref, ws_ref, bst_ref, go_ref,
             q_ref, k_ref, v_ref, c_ref, y_ref, carry_ref):
        i = pl.program_id(0)

        @pl.when(i == 0)
        def _():
            carry_ref[...] = jnp.zeros_like(carry_ref)

        for h in range(FOX_HEADS):
            hs = slice(h * FOX_HD, (h + 1) * FOX_HD)
            qh = z_ref[:, Z_Q + h * FOX_HD:Z_Q + (h + 1) * FOX_HD]
            kh = z_ref[:, Z_K + h * FOX_HD:Z_K + (h + 1) * FOX_HD]
            q_ref[:, hs] = (qh * _rstd(qh) * gq_ref[...] * 0.125).astype(BF)
            k_ref[:, hs] = (kh * _rstd(kh) * gk_ref[...]).astype(BF)
        v_ref[...] = z_ref[:, Z_V:Z_V + FOX_W].astype(BF)

        fl = z_ref[:, Z_F:Z_F + LANES] + bf_ref[...]
        logf = jnp.minimum(fl, 0.0) - jnp.log1p(jnp.exp(-jnp.abs(fl)))
        csum = _hi(_tri(tm, True).astype(F32), logf) + carry_ref[...]
        c_ref[...] = csum
        carry_ref[...] = csum[tm - 1:tm, :]

        u = _gelu(z_ref[:, Z_U:Z_U + GMLP_W])
        vg = _gelu(z_ref[:, Z_G:Z_G + GMLP_W])
        vgn = (vg * _rstd(vg) * gs_ref[...]).astype(BF)
        mixed, _ = _spatial_mix(vgn, ws_ref, bst_ref[...], tm)
        sgu = u * mixed
        y_ref[...] = (sgu * _rstd(sgu) * go_ref[...]).astype(BF)

    row = lambda i: (i, 0)
    fix2 = lambda i: (0, 0)
    return pl.pallas_call(
        body, name="mix_prep", grid=(T // tm,),
        in_specs=[pl.BlockSpec((tm, ZW), row),
                  pl.BlockSpec((1, LANES), fix2), pl.BlockSpec((1, FOX_HD), fix2), pl.BlockSpec((1, FOX_HD), fix2),
                  pl.BlockSpec((1, GMLP_W), fix2), pl.BlockSpec((GMLP_G, CHUNK, CHUNK), lambda i: (0, 0, 0)),
                  pl.BlockSpec((CHUNK, GMLP_G), fix2), pl.BlockSpec((1, GMLP_W), fix2)],
        out_specs=[pl.BlockSpec((tm, FOX_W), row), pl.BlockSpec((tm, FOX_W), row), pl.BlockSpec((tm, FOX_W), row),
                   pl.BlockSpec((tm, LANES), row), pl.BlockSpec((tm, GMLP_W), row)],
        out_shape=[S((T, FOX_W), BF), S((T, FOX_W), BF), S((T, FOX_W), BF), S((T, LANES), F32), S((T, GMLP_W), BF)],
        scratch_shapes=[pltpu.VMEM((1, LANES), F32)],
        compiler_params=_cp(1))(z, bf128, g_q, g_k, g_sgu, w_s, b_st, g_go)


def _fox_fwd(q, k, v, ccol, crow):
    T = q.shape[0]
    tq = _tile(T, 512)
    nq = T // tq

    def body(q_ref, k_ref, v_ref, cc_ref, cr_ref, o_ref, lse_ref, m_sc, l_sc, acc_sc):
        i, j = pl.program_id(0), pl.program_id(1)

        @pl.when(j == 0)
        def _():
            m_sc[...] = jnp.full(m_sc.shape, NEG, F32)
            l_sc[...] = jnp.zeros_like(l_sc)
            acc_sc[...] = jnp.zeros_like(acc_sc)

        def step(masked):
            cc = cc_ref[...]
            cr = cr_ref[...]
            mask = _tri(tq, True) if masked else None
            for h in range(FOX_HEADS):
                hs = slice(h * FOX_HD, (h + 1) * FOX_HD)
                s = _nt(q_ref[:, hs], k_ref[:, hs]) + (cc[:, h:h + 1] - cr[h:h + 1, :])
                if masked:
                    s = jnp.where(mask, s, NEG)
                m_prev = m_sc[h]
                m_new = jnp.maximum(m_prev, jnp.max(s, axis=1, keepdims=True))
                alpha = jnp.exp(m_prev - m_new)
                p = jnp.exp(s - m_new)
                l_sc[h] = alpha * l_sc[h] + jnp.sum(p, axis=1, keepdims=True)
                acc_sc[:, hs] = alpha * acc_sc[:, hs] + _nn(p.astype(BF), v_ref[:, hs])
                m_sc[h] = m_new

        @pl.when(j < i)
        def _():
            step(False)

        @pl.when(j == i)
        def _():
            step(True)
            lse_ref[...] = jnp.zeros_like(lse_ref)
            for h in range(FOX_HEADS):
                hs = slice(h * FOX_HD, (h + 1) * FOX_HD)
                o_ref[:, hs] = acc_sc[:, hs] / l_sc[h]
                lse_ref[:, h:h + 1] = m_sc[h] + jnp.log(l_sc[h])

    qi = lambda i, j: (i, 0)
    kj = lambda i, j: (jnp.minimum(i, j), 0)
    return pl.pallas_call(
        body, name="fox_fwd", grid=(nq, nq),
        in_specs=[pl.BlockSpec((tq, FOX_W), qi), pl.BlockSpec((tq, FOX_W), kj), pl.BlockSpec((tq, FOX_W), kj),
                  pl.BlockSpec((tq, LANES), qi), pl.BlockSpec((FOX_HEADS, tq), lambda i, j: (0, jnp.minimum(i, j)))],
        out_specs=[pl.BlockSpec((tq, FOX_W), qi), pl.BlockSpec((tq, LANES), qi)],
        out_shape=[S((T, FOX_W), F32), S((T, LANES), F32)],
        scratch_shapes=[pltpu.VMEM((FOX_HEADS, tq, 1), F32), pltpu.VMEM((FOX_HEADS, tq, 1), F32),
                        pltpu.VMEM((tq, FOX_W), F32)],
        compiler_params=_cp(2))(q, k, v, ccol, crow)


def _fox_bwd(q, k, v, dob, lse, dsum, ccol, crow):
    T = q.shape[0]
    tq = _tile(T, 512)
    nq = T // tq

    def body(q_ref, k_ref, v_ref, do_ref, lse_ref, ds_ref, cc_ref, cr_ref,
             dq_ref, dk_ref, dv_ref, dcq_ref, dck_ref):
        j, i = pl.program_id(0), pl.program_id(1)

        @pl.when(jnp.logical_and(i == 0, j == 0))
        def _():
            dq_ref[...] = jnp.zeros_like(dq_ref)
            dcq_ref[...] = jnp.zeros_like(dcq_ref)

        @pl.when(i == 0)
        def _():
            dk_ref[...] = jnp.zeros_like(dk_ref)
            dv_ref[...] = jnp.zeros_like(dv_ref)
            dck_ref[...] = jnp.zeros_like(dck_ref)

        def step(masked):
            rows = pl.ds(pl.multiple_of(i * tq, tq), tq)
            cc = cc_ref[...]
            cr = cr_ref[...]
            lse_t = lse_ref[...]
            dsum_t = ds_ref[...]
            mask = _tri(tq, True) if masked else None
            for h in range(FOX_HEADS):
                hs = slice(h * FOX_HD, (h + 1) * FOX_HD)
                qh, kh, vh, doh = q_ref[:, hs], k_ref[:, hs], v_ref[:, hs], do_ref[:, hs]
                s = _nt(qh, kh) + (cc[:, h:h + 1] - cr[h:h + 1, :])
                if masked:
                    s = jnp.where(mask, s, NEG)
                p = jnp.exp(s - lse_t[:, h:h + 1])
                dp = _nt(doh, vh)
                ds = p * (dp - dsum_t[:, h:h + 1])
                dsb = ds.astype(BF)
                dv_ref[:, hs] += _tn(p.astype(BF), doh)
                dk_ref[:, hs] += _tn(dsb, qh)
                dq_ref[rows, hs] += _nn(dsb, kh)
                dcq_ref[rows, h:h + 1] += jnp.sum(ds, axis=1, keepdims=True)
                dck_ref[h:h + 1, :] -= jnp.sum(ds, axis=0, keepdims=True)

        @pl.when(i > j)
        def _():
            step(False)

        @pl.when(i == j)
        def _():
            step(True)

    qi = lambda j, i: (jnp.maximum(i, j), 0)
    kj = lambda j, i: (j, 0)
    whole = lambda j, i: (0, 0)
    return pl.pallas_call(
        body, name="fox_bwd", grid=(nq, nq),
        in_specs=[pl.BlockSpec((tq, FOX_W), qi), pl.BlockSpec((tq, FOX_W), kj), pl.BlockSpec((tq, FOX_W), kj),
                  pl.BlockSpec((tq, FOX_W), qi), pl.BlockSpec((tq, LANES), qi), pl.BlockSpec((tq, LANES), qi),
                  pl.BlockSpec((tq, LANES), qi), pl.BlockSpec((FOX_HEADS, tq), lambda j, i: (0, j))],
        out_specs=[pl.BlockSpec((T, FOX_W), whole), pl.BlockSpec((tq, FOX_W), kj), pl.BlockSpec((tq, FOX_W), kj),
                   pl.BlockSpec((T, LANES), whole), pl.BlockSpec((FOX_HEADS, tq), lambda j, i: (0, j))],
        out_shape=[S((T, FOX_W), F32), S((T, FOX_W), F32), S((T, FOX_W), F32), S((T, LANES), F32),
                   S((FOX_HEADS, T), F32)],
        compiler_params=_cp(2))(q, k, v, dob, lse, dsum, ccol, crow)


def _mix_out(attn, yg, g_fo, wout, x):
    T, D = x.shape
    tm = _tile(T, 512)

    def body(a_ref, y_ref, g_ref, w_ref, x_ref, o_ref):
        at = a_ref[...]
        yf = (at * _rstd(at) * g_ref[...]).astype(BF)
        o_ref[...] = x_ref[...] + _nn(yf, w_ref[:FOX_W, :]) + _nn(y_ref[...], w_ref[FOX_W:, :])

    row = lambda i: (i, 0)
    return pl.pallas_call(
        body, name="mix_out", grid=(T // tm,),
        in_specs=[pl.BlockSpec((tm, FOX_W), row), pl.BlockSpec((tm, GMLP_W), row),
                  pl.BlockSpec((1, FOX_W), lambda i: (0, 0)), pl.BlockSpec((D, D), lambda i: (0, 0)),
                  pl.BlockSpec((tm, D), row)],
        out_specs=pl.BlockSpec((tm, D), row),
        out_shape=S((T, D), F32),
        compiler_params=_cp(1))(attn, yg, g_fo, wout, x)


def _mix_out_bwd(dx, attn, yg, g_fo, wout):
    T, D = dx.shape
    tm = _tile(T, 256)
    n = T // tm

    def body(dx_ref, a_ref, y_ref, g_ref, w_ref, da_ref, dsum_ref, dyg_ref, dw_ref, dg_ref, acc_ref):
        i = pl.program_id(0)
        dxb = dx_ref[...].astype(BF)
        at = a_ref[...]
        yf = (at * _rstd(at) * g_ref[...]).astype(BF)
        dy = _nt(dxb, w_ref[...])
        p_top = _tn(yf, dxb)
        p_bot = _tn(y_ref[...], dxb)

        @pl.when(i == 0)
        def _():
            acc_ref[:FOX_W, :] = p_top
            acc_ref[FOX_W:, :] = p_bot

        @pl.when(i > 0)
        def _():
            acc_ref[:FOX_W, :] += p_top
            acc_ref[FOX_W:, :] += p_bot

        @pl.when(i == n - 1)
        def _():
            dw_ref[...] = acc_ref[...].astype(BF)

        dat, dgr = _norm_bwd(dy[:, :FOX_W], at, g_ref[...])
        _acc_rows(dg_ref, i == 0, dgr)
        da_ref[...] = dat.astype(BF)
        dyg_ref[...] = dy[:, FOX_W:]
        prod = dat * at
        dsum_ref[...] = jnp.zeros_like(dsum_ref)
        for h in range(FOX_HEADS):
            dsum_ref[:, h:h + 1] = jnp.sum(prod[:, h * FOX_HD:(h + 1) * FOX_HD], axis=1, keepdims=True)

    row = lambda i: (i, 0)
    fix = lambda i: (0, 0)
    return pl.pallas_call(
        body, name="mix_out_bwd", grid=(n,),
        in_specs=[pl.BlockSpec((tm, D), row), pl.BlockSpec((tm, FOX_W), row), pl.BlockSpec((tm, GMLP_W), row),
                  pl.BlockSpec((1, FOX_W), fix), pl.BlockSpec((D, D), fix)],
        out_specs=[pl.BlockSpec((tm, FOX_W), row), pl.BlockSpec((tm, LANES), row), pl.BlockSpec((tm, GMLP_W), row),
                   pl.BlockSpec((D, D), fix), pl.BlockSpec((1, FOX_W), fix)],
        out_shape=[S((T, FOX_W), BF), S((T, LANES), F32), S((T, GMLP_W), F32), S((D, D), BF), S((1, FOX_W), F32)],
        scratch_shapes=[pltpu.VMEM((D, D), F32)],
        compiler_params=_cp(1))(dx, attn, yg, g_fo, wout)


def _mix_prep_bwd(z, dq, dk, dv, dcq, dck, dyg, bf128, g_q, g_k, g_sgu, w_s, b_st, g_go):
    T = z.shape[0]
    tm = _tile(T, 256)
    n = T // tm

    def body(z_ref, dq_ref, dk_ref, dv_ref, dcq_ref, dck_ref, dyg_ref, bf_ref, gq_ref, gk_ref, gs_ref, ws_ref,
             bst_ref, go_ref, dz_ref, dgq_ref, dgk_ref, dgs_ref, dgo_ref, dws_ref, dbst_ref, dbf_ref, carry_ref):
        i = pl.program_id(0)
        first = i == 0

        @pl.when(first)
        def _():
            carry_ref[...] = jnp.zeros_like(carry_ref)

        gq_rows, gk_rows = [], []
        for h in range(FOX_HEADS):
            hs = slice(h * FOX_HD, (h + 1) * FOX_HD)
            dqh, gqr = _norm_bwd(dq_ref[:, hs] * 0.125, z_ref[:, Z_Q + h * FOX_HD:Z_Q + (h + 1) * FOX_HD], gq_ref[...])
            dkh, gkr = _norm_bwd(dk_ref[:, hs], z_ref[:, Z_K + h * FOX_HD:Z_K + (h + 1) * FOX_HD], gk_ref[...])
            dz_ref[:, Z_Q + h * FOX_HD:Z_Q + (h + 1) * FOX_HD] = dqh.astype(BF)
            dz_ref[:, Z_K + h * FOX_HD:Z_K + (h + 1) * FOX_HD] = dkh.astype(BF)
            gq_rows.append(gqr)
            gk_rows.append(gkr)
        _acc_rows(dgq_ref, first, functools.reduce(lambda a, b: a + b, gq_rows))
        _acc_rows(dgk_ref, first, functools.reduce(lambda a, b: a + b, gk_rows))
        dz_ref[:, Z_V:Z_V + FOX_W] = dv_ref[...].astype(BF)

        dc = dcq_ref[...] + dck_ref[...]
        dlogf = _hi(_tri(tm, False).astype(F32), dc) + carry_ref[...]
        carry_ref[...] = dlogf[0:1, :]
        fl = z_ref[:, Z_F:Z_F + LANES] + bf_ref[...]
        lane = lax.broadcasted_iota(jnp.int32, (tm, LANES), 1)
        df = jnp.where(lane < FOX_HEADS, dlogf * jax.nn.sigmoid(-fl), 0.0)
        dz_ref[:, Z_F:Z_F + LANES] = df.astype(BF)
        _acc_rows(dbf_ref, first, df)

        u_pre = z_ref[:, Z_U:Z_U + GMLP_W]
        vg_pre = z_ref[:, Z_G:Z_G + GMLP_W]
        u = _gelu(u_pre)
        vg = _gelu(vg_pre)
        vgn = (vg * _rstd(vg) * gs_ref[...]).astype(BF)
        bst = bst_ref[...]
        mixed, wms = _spatial_mix(vgn, ws_ref, bst, tm)
        sgu = u * mixed
        dsgu, gor = _norm_bwd(dyg_ref[...], sgu, go_ref[...])
        _acc_rows(dgo_ref, first, gor)
        du = dsgu * mixed
        dmixed = dsgu * u
        dmb = dmixed.astype(BF)
        tril = _tri(CHUNK, True)
        dvgn_rows = []
        dws = [None] * GMLP_G
        dbs = [None] * GMLP_G
        for c in range(tm // CHUNK):
            cs = slice(c * CHUNK, (c + 1) * CHUNK)
            cols = []
            for g in range(GMLP_G):
                gs = slice(g * GMLP_GD, (g + 1) * GMLP_GD)
                dmc = dmb[cs, gs]
                pw = _nt(dmc, vgn[cs, gs])
                pb = jnp.sum(dmixed[cs, gs], axis=1, keepdims=True)
                dws[g] = pw if dws[g] is None else dws[g] + pw
                dbs[g] = pb if dbs[g] is None else dbs[g] + pb
                cols.append(_tn(wms[g], dmc))
            dvgn_rows.append(jnp.concatenate(cols, axis=1))
        dvgn = jnp.concatenate(dvgn_rows, axis=0)
        dbs_t = jnp.concatenate(dbs, axis=1)
        for g in range(GMLP_G):
            dwg = jnp.where(tril, dws[g], 0.0)

            @pl.when(first)
            def _():
                dws_ref[g] = dwg

            @pl.when(jnp.logical_not(first))
            def _():
                dws_ref[g] += dwg

        @pl.when(first)
        def _():
            dbst_ref[...] = dbs_t

        @pl.when(jnp.logical_not(first))
        def _():
            dbst_ref[...] += dbs_t

        dvg, gsr = _norm_bwd(dvgn, vg, gs_ref[...])
        _acc_rows(dgs_ref, first, gsr)
        dz_ref[:, Z_U:Z_U + GMLP_W] = (du * _gelu_grad(u_pre)).astype(BF)
        dz_ref[:, Z_G:Z_G + GMLP_W] = (dvg * _gelu_grad(vg_pre)).astype(BF)

    rev = lambda i: (n - 1 - i, 0)
    fix = lambda i: (0, 0)
    fix3 = lambda i: (0, 0, 0)
    return pl.pallas_call(
        body, name="mix_prep_bwd", grid=(n,),
        in_specs=[pl.BlockSpec((tm, ZW), rev), pl.BlockSpec((tm, FOX_W), rev), pl.BlockSpec((tm, FOX_W), rev),
                  pl.BlockSpec((tm, FOX_W), rev), pl.BlockSpec((tm, LANES), rev), pl.BlockSpec((tm, LANES), rev),
                  pl.BlockSpec((tm, GMLP_W), rev),
                  pl.BlockSpec((1, LANES), fix), pl.BlockSpec((1, FOX_HD), fix), pl.BlockSpec((1, FOX_HD), fix),
                  pl.BlockSpec((1, GMLP_W), fix), pl.BlockSpec((GMLP_G, CHUNK, CHUNK), fix3),
                  pl.BlockSpec((CHUNK, GMLP_G), fix), pl.BlockSpec((1, GMLP_W), fix)],
        out_specs=[pl.BlockSpec((tm, ZW), rev), pl.BlockSpec((1, FOX_HD), fix), pl.BlockSpec((1, FOX_HD), fix),
                   pl.BlockSpec((1, GMLP_W), fix), pl.BlockSpec((1, GMLP_W), fix),
                   pl.BlockSpec((GMLP_G, CHUNK, CHUNK), fix3), pl.BlockSpec((CHUNK, GMLP_G), fix),
                   pl.BlockSpec((1, LANES), fix)],
        out_shape=[S((T, ZW), BF), S((1, FOX_HD), F32), S((1, FOX_HD), F32), S((1, GMLP_W), F32), S((1, GMLP_W), F32),
                   S((GMLP_G, CHUNK, CHUNK), F32), S((CHUNK, GMLP_G), F32), S((1, LANES), F32)],
        scratch_shapes=[pltpu.VMEM((1, LANES), F32)],
        compiler_params=_cp(1))(z, dq, dk, dv, dcq, dck, dyg, bf128, g_q, g_k, g_sgu, w_s, b_st, g_go)


def _mix_proj_bwd(dz, wz, x, g, dy):
    T, D = x.shape
    tm = _tile(T, 256)

    def body(dz_ref, w_ref, x_ref, g_ref, dy_ref, dx_ref, dxb_ref, dg_ref):
        dh = _nn(dz_ref[...], w_ref[...])
        dx, dgr = _norm_bwd(dh, x_ref[...], g_ref[...])
        dx = dx + dy_ref[...]
        dx_ref[...] = dx
        dxb_ref[...] = dx.astype(BF)
        _acc_rows(dg_ref, pl.program_id(0) == 0, dgr)

    row = lambda i: (i, 0)
    fix = lambda i: (0, 0)
    return pl.pallas_call(
        body, name="mix_proj_bwd", grid=(T // tm,),
        in_specs=[pl.BlockSpec((tm, ZW), row), pl.BlockSpec((ZW, D), fix), pl.BlockSpec((tm, D), row),
                  pl.BlockSpec((1, D), fix), pl.BlockSpec((tm, D), row)],
        out_specs=[pl.BlockSpec((tm, D), row), pl.BlockSpec((tm, D), row), pl.BlockSpec((1, D), fix)],
        out_shape=[S((T, D), F32), S((T, D), BF), S((1, D), F32)],
        compiler_params=_cp(1))(dz, wz, x, g, dy)


def _ca_kv(mem, g_mem, wckv, g_ck):
    M, D = mem.shape

    def body(m_ref, g_ref, w_ref, gk_ref, mn_ref, kr_ref, kn_ref, v_ref):
        mf = m_ref[...]
        mn = (mf * _rstd(mf) * g_ref[...]).astype(BF)
        mn_ref[...] = mn
        for h in range(CA_HEADS):
            kr = _nn(mn, w_ref[h])
            kr_ref[h] = kr
            kn_ref[h] = (kr * _rstd(kr) * gk_ref[...]).astype(BF)
            v_ref[h] = _nn(mn, w_ref[CA_HEADS + h]).astype(BF)

    hd = (CA_HEADS, M, CA_HD)
    return pl.pallas_call(
        body, name="ca_kv", out_shape=[S((M, D), BF), S(hd, F32), S(hd, BF), S(hd, BF)],
        compiler_params=pltpu.CompilerParams(vmem_limit_bytes=VMEM_LIMIT))(mem, g_mem, wckv, g_ck)


def _ca_tile_fwd(xt, gca, wcq, gcq, kn_ref, v_ref):
    hb = (xt * _rstd(xt) * gca).astype(BF)
    qc = _nn(hb, wcq)
    qr, qn, ps = [], [], []
    for h in range(CA_HEADS):
        qh = qc[:, h * CA_HD:(h + 1) * CA_HD]
        qnh = (qh * _rstd(qh) * gcq * 0.0625).astype(BF)
        s = _nt(qnh, kn_ref[h])
        e = jnp.exp(s - jnp.max(s, axis=1, keepdims=True))
        ps.append(e / jnp.sum(e, axis=1, keepdims=True))
        qr.append(qh)
        qn.append(qnh)
    return hb, qr, qn, ps


def _ca_fwd(x, g_ca, wcq, g_cq, kn, vv, wco):
    T, D = x.shape
    M = kn.shape[1]
    tm = _tile(T, 256)

    def body(x_ref, gca_ref, wcq_ref, gcq_ref, kn_ref, v_ref, wco_ref, o_ref, ob_sc):
        xt = x_ref[...]
        _, _, _, ps = _ca_tile_fwd(xt, gca_ref[...], wcq_ref[...], gcq_ref[...], kn_ref, v_ref)
        for h in range(CA_HEADS):
            ob_sc[:, h * CA_HD:(h + 1) * CA_HD] = _nn(ps[h].astype(BF), v_ref[h]).astype(BF)
        o_ref[...] = xt + _nn(ob_sc[...], wco_ref[...])

    row = lambda i: (i, 0)
    fix = lambda i: (0, 0)
    fix3 = lambda i: (0, 0, 0)
    return pl.pallas_call(
        body, name="ca_fwd", grid=(T // tm,),
        in_specs=[pl.BlockSpec((tm, D), row), pl.BlockSpec((1, D), fix), pl.BlockSpec((D, D), fix),
                  pl.BlockSpec((1, CA_HD), fix), pl.BlockSpec((CA_HEADS, M, CA_HD), fix3),
                  pl.BlockSpec((CA_HEADS, M, CA_HD), fix3), pl.BlockSpec((D, D), fix)],
        out_specs=pl.BlockSpec((tm, D), row), out_shape=S((T, D), F32),
        scratch_shapes=[pltpu.VMEM((tm, D), BF)],
        compiler_params=_cp(1))(x, g_ca, wcq, g_cq, kn, vv, wco)


def _ca_bwd(x, dy, g_ca, wcq, g_cq, kn, vv, wco):
    T, D = x.shape
    M = kn.shape[1]
    tm = _tile(T, 256)
    n = T // tm

    def body(x_ref, dy_ref, gca_ref, wcq_ref, gcq_ref, kn_ref, v_ref, wco_ref,
             dx_ref, dwq_ref, dwo_ref, dkn_ref, dv_ref, dgcq_ref, dgca_ref, aq_sc, ao_sc, ob_sc, dq_sc):
        i = pl.program_id(0)
        first = i == 0
        xt = x_ref[...]
        dyt = dy_ref[...]
        dyb = dyt.astype(BF)
        hb, qr, qn, ps = _ca_tile_fwd(xt, gca_ref[...], wcq_ref[...], gcq_ref[...], kn_ref, v_ref)
        do = _nt(dyb, wco_ref[...])
        gcq_rows = None
        for h in range(CA_HEADS):
            hs = slice(h * CA_HD, (h + 1) * CA_HD)
            p = ps[h]
            pb = p.astype(BF)
            ob_sc[:, hs] = _nn(pb, v_ref[h]).astype(BF)
            doh = do[:, hs].astype(BF)
            dp = _nt(doh, v_ref[h])
            ds = (p * (dp - jnp.sum(dp * p, axis=1, keepdims=True))).astype(BF)
            dvh = _tn(pb, doh)
            dkh = _tn(ds, qn[h])

            @pl.when(first)
            def _():
                dv_ref[h] = dvh
                dkn_ref[h] = dkh

            @pl.when(jnp.logical_not(first))
            def _():
                dv_ref[h] += dvh
                dkn_ref[h] += dkh

            dqn = _nn(ds, kn_ref[h]) * 0.0625
            dqh, gr = _norm_bwd(dqn, qr[h], gcq_ref[...])
            gcq_rows = gr if gcq_rows is None else gcq_rows + gr
            dq_sc[:, hs] = dqh.astype(BF)
        _acc_rows(dgcq_ref, first, gcq_rows)
        dqb = dq_sc[...]
        p_o = _tn(ob_sc[...], dyb)
        p_q = _tn(hb, dqb)

        @pl.when(first)
        def _():
            ao_sc[...] = p_o
            aq_sc[...] = p_q

        @pl.when(jnp.logical_not(first))
        def _():
            ao_sc[...] += p_o
            aq_sc[...] += p_q

        @pl.when(i == n - 1)
        def _():
            dwo_ref[...] = ao_sc[...].astype(BF)
            dwq_ref[...] = aq_sc[...].astype(BF)

        dh = _nt(dqb, wcq_ref[...])
        dx, gar = _norm_bwd(dh, xt, gca_ref[...])
        dx_ref[...] = dx + dyt
        _acc_rows(dgca_ref, first, gar)

    row = lambda i: (i, 0)
    fix = lambda i: (0, 0)
    fix3 = lambda i: (0, 0, 0)
    hd = (CA_HEADS, M, CA_HD)
    return pl.pallas_call(
        body, name="ca_bwd", grid=(n,),
        in_specs=[pl.BlockSpec((tm, D), row), pl.BlockSpec((tm, D), row), pl.BlockSpec((1, D), fix),
                  pl.BlockSpec((D, D), fix), pl.BlockSpec((1, CA_HD), fix), pl.BlockSpec(hd, fix3),
                  pl.BlockSpec(hd, fix3), pl.BlockSpec((D, D), fix)],
        out_specs=[pl.BlockSpec((tm, D), row), pl.BlockSpec((D, D), fix), pl.BlockSpec((D, D), fix),
                   pl.BlockSpec(hd, fix3), pl.BlockSpec(hd, fix3), pl.BlockSpec((1, CA_HD), fix),
                   pl.BlockSpec((1, D), fix)],
        out_shape=[S((T, D), F32), S((D, D), BF), S((D, D), BF), S(hd, F32), S(hd, F32), S((1, CA_HD), F32),
                   S((1, D), F32)],
        scratch_shapes=[pltpu.VMEM((D, D), F32), pltpu.VMEM((D, D), F32), pltpu.VMEM((tm, D), BF),
                        pltpu.VMEM((tm, D), BF)],
        compiler_params=_cp(1))(x, dy, g_ca, wcq, g_cq, kn, vv, wco)


def _ca_kv_bwd(mem, g_mem, mn, kraw, dkn, dvv, wckv, g_ck):
    M, D = mem.shape

    def body(m_ref, g_ref, mn_ref, kr_ref, dkn_ref, dv_ref, w_ref, gk_ref, dw_ref, dgk_ref, dgm_ref):
        mn = mn_ref[...]
        dmn = jnp.zeros((M, D), F32)
        gk_rows = None
        for h in range(CA_HEADS):
            dkr, gr = _norm_bwd(dkn_ref[h], kr_ref[h], gk_ref[...])
            gk_rows = gr if gk_rows is None else gk_rows + gr
            dkb = dkr.astype(BF)
            dvb = dv_ref[h].astype(BF)
            dw_ref[h] = _tn(mn, dkb).astype(BF)
            dw_ref[CA_HEADS + h] = _tn(mn, dvb).astype(BF)
            dmn = dmn + _nt(dkb, w_ref[h]) + _nt(dvb, w_ref[CA_HEADS + h])
        dgk_ref[...] = jnp.sum(gk_rows, axis=0, keepdims=True)
        mf = m_ref[...]
        dgm_ref[...] = jnp.sum(dmn * (mf * _rstd(mf)), axis=0, keepdims=True)

    return pl.pallas_call(
        body, name="ca_kv_bwd",
        out_shape=[S((2 * CA_HEADS, D, CA_HD), BF), S((1, CA_HD), F32), S((1, D), F32)],
        compiler_params=pltpu.CompilerParams(vmem_limit_bytes=VMEM_LIMIT))(mem, g_mem, mn, kraw, dkn, dvv, wckv, g_ck)


def _after(g, token):
    return g if token is None else g + token[0:1, 0:1]


def _local_step(x, mem, target, small, weights, emit):
    T, D = x.shape
    p = small
    bf128 = jnp.pad(p["b_f"], ((0, 0), (0, LANES - FOX_HEADS)))
    b_st = p["b_s"].T

    wup1 = weights("ffn1_up", x)["wup1"]
    a1, h1 = _ffn_up("ffn1_up", x, p["g_ffn1"], wup1)
    wdn1 = weights("ffn1_dn", h1)["wdn1"]
    x1 = _ffn_down("ffn1_down", a1, wdn1, x)
    wm = weights("mix", x1)
    z, h2 = _mix_proj(x1, p["g_mix"], wm["wz"])
    qs, kn, vb, ccol, yg = _mix_prep(z, bf128, p["g_q"], p["g_k"], p["g_sgu"], p["w_s"], b_st, p["g_gmlp_o"])
    crow = ccol[:, :FOX_HEADS].T
    attn, lse = _fox_fwd(qs, kn, vb, ccol, crow)
    x2 = _mix_out(attn, yg, p["g_fox_o"], wm["wout"], x1)
    wc = weights("ca", x2)
    mn, kraw, ckn, cvv = _ca_kv(mem, p["g_mem"], wc["wckv"], p["g_ck"])
    x3 = _ca_fwd(x2, p["g_ca"], wc["wcq"], p["g_cq"], ckn, cvv, wc["wco"])
    w2 = weights("ffn2", x3)
    a2, h4 = _ffn_up("ffn2_up", x3, p["g_ffn2"], w2["wup2"])
    dy4, dy4b, sq = _ffn_down_loss("ffn2_down", a2, w2["wdn2"], x3, target)

    gs = {}
    dgu2 = _ffn_bwd_act("ffn2_bwd_act", dy4b, h4, w2["wup2"], w2["wdn2"])
    dwup2, dwdn2 = _ffn_dw("ffn2", h4, dgu2, a2, dy4b)
    tok = emit("ffn2", {"wup2": dwup2, "wdn2": dwdn2})
    dx3, gs["g_ffn2"] = _ffn_dx("ffn2_dx", dgu2, w2["wup2"], x3, _after(p["g_ffn2"], tok), dy4)

    dx2, dwcq, dwco, dckn, dcvv, gs["g_cq"], gs["g_ca"] = _ca_bwd(
        x2, dx3, p["g_ca"], wc["wcq"], p["g_cq"], ckn, cvv, wc["wco"])
    dwckv, gs["g_ck"], gs["g_mem"] = _ca_kv_bwd(mem, p["g_mem"], mn, kraw, dckn, dcvv, wc["wckv"], p["g_ck"])

    dattn, dsum, dyg, dwout, gs["g_fox_o"] = _mix_out_bwd(dx2, attn, yg, p["g_fox_o"], wm["wout"])
    dq, dk, dv, dcq, dck = _fox_bwd(qs, kn, vb, dattn, lse, dsum, ccol, crow)
    dck_col = jnp.pad(dck.T, ((0, 0), (0, LANES - FOX_HEADS)))
    dz, gs["g_q"], gs["g_k"], gs["g_sgu"], gs["g_gmlp_o"], gs["w_s"], dbst, dbf = _mix_prep_bwd(
        z, dq, dk, dv, dcq, dck_col, dyg, bf128, p["g_q"], p["g_k"], p["g_sgu"], p["w_s"], b_st, p["g_gmlp_o"])
    gs["b_s"] = dbst.T
    gs["b_f"] = dbf[:, :FOX_HEADS]
    tk = _tile(T, 512)
    zb = ZW // 3
    dwz = _tn_matmul(
        "mix_dwz", dz, pl.BlockSpec((tk, zb), lambda j, k: (k, j)), h2, pl.BlockSpec((tk, D), lambda j, k: (k, 0)),
        S((ZW, D), F32), pl.BlockSpec((zb, D), lambda j, k: (j, 0)), (3, T // tk), (zb, D))
    tok = emit("mid", {"wcq": dwcq, "wco": dwco, "wckv": dwckv, "wout": dwout, "wz": dwz})
    dx1, dx1b, gs["g_mix"] = _mix_proj_bwd(dz, wm["wz"], x1, _after(p["g_mix"], tok), dx2)

    dgu1 = _ffn_bwd_act("ffn1_bwd_act", dx1b, h1, wup1, wdn1)
    dwup1, dwdn1 = _ffn_dw("ffn1", h1, dgu1, a1, dx1b)
    tok = emit("ffn1", {"wup1": dwup1, "wdn1": dwdn1})
    dx0, gs["g_ffn1"] = _ffn_dx("ffn1_dx", dgu1, wup1, x, _after(p["g_ffn1"], tok), dx1)
    return sq, dx0, gs


MESH = pl.DeviceIdType.MESH
HBM_SPEC = pl.BlockSpec(memory_space=pltpu.HBM)
N_PEER = N_DEV - 1


def _place():
    return lax.axis_index("x"), lax.axis_index("y"), lax.axis_index("c")


def _slot(px, py, pc):
    return 4 * px + 2 * py + pc


SEM_SPEC = pl.BlockSpec(memory_space=pltpu.SEMAPHORE)
ANY_SPEC = pl.BlockSpec(memory_space=pl.ANY)
DATAFLOW = pltpu.SideEffectType.DATAFLOW_SIDE_EFFECTING


def _hbm(a):
    return pltpu.with_memory_space_constraint(a, pltpu.HBM)


def _peer(x, y, c, r):
    return (1 - x if r & 4 else x, 1 - y if r & 2 else y, 1 - c if r & 1 else c)


def _place_own(srcs, whole):
    my = _slot(*_place())
    lands = []
    for s in srcs:
        blk = s[None] if whole else lax.dynamic_slice_in_dim(s, my, 1, 0)
        shape = (N_DEV,) + s.shape if whole else s.shape
        lands.append(lax.dynamic_update_slice_in_dim(lax.empty(shape, s.dtype), blk, my, 0))
    return lands


def _copy_start(name, srcs, lands, whole):
    n = len(srcs)

    def body(*refs):
        src, land = refs[:n], refs[n:2 * n]
        send, recv = refs[2 * n:3 * n], refs[3 * n:4 * n]
        token = refs[6 * n]
        x, y, c = _place()
        my = _slot(x, y, c)
        for a in range(n):
            for r in range(1, N_DEV):
                p = _peer(x, y, c, r)
                pltpu.make_async_remote_copy(
                    src_ref=src[a] if whole else src[a].at[_slot(*p)], dst_ref=land[a].at[my],
                    send_sem=send[a].at[r - 1], recv_sem=recv[a].at[r - 1], device_id=p, device_id_type=MESH).start()
        token[...] = jnp.zeros_like(token)

    out = pl.pallas_call(
        body, name=name,
        out_shape=([pltpu.SemaphoreType.DMA((N_PEER,))] * (2 * n)
                   + [pltpu.HBM(s.shape, s.dtype) for s in srcs] + [pltpu.HBM(s.shape, s.dtype) for s in lands]
                   + [S((8, LANES), F32)]),
        in_specs=[HBM_SPEC] * (2 * n),
        out_specs=[SEM_SPEC] * (2 * n) + [HBM_SPEC] * (2 * n) + [pl.BlockSpec(memory_space=pltpu.VMEM)],
        input_output_aliases={i: 2 * n + i for i in range(2 * n)},
        compiler_params=pltpu.CompilerParams(has_side_effects=DATAFLOW),
    )(*[_hbm(s) for s in srcs], *[_hbm(s) for s in lands])
    return out[:n], out[n:2 * n], out[2 * n:3 * n], out[3 * n:4 * n], out[4 * n]


def _copy_wait(name, srcs, lands, send, recv, after, whole):
    n = len(srcs)

    def body(*refs):
        src, land = refs[:n], refs[n:2 * n]
        snd, rcv = refs[2 * n:3 * n], refs[3 * n:4 * n]
        x, y, c = _place()
        for a in range(n):
            for r in range(1, N_DEV):
                p = _peer(x, y, c, r)
                ps = _slot(*p)
                cp = pltpu.make_async_remote_copy(
                    src_ref=src[a] if whole else src[a].at[ps], dst_ref=land[a].at[ps],
                    send_sem=snd[a].at[r - 1], recv_sem=rcv[a].at[r - 1], device_id=p, device_id_type=MESH)
                cp.wait_send()
                cp.wait_recv()

    out = pl.pallas_call(
        body, name=name,
        out_shape=[pltpu.HBM(s.shape, s.dtype) for s in srcs] + [pltpu.HBM(s.shape, s.dtype) for s in lands],
        in_specs=[HBM_SPEC] * (2 * n) + [SEM_SPEC] * (2 * n) + [ANY_SPEC],
        out_specs=[HBM_SPEC] * (2 * n),
        input_output_aliases={i: i for i in range(2 * n)},
        compiler_params=pltpu.CompilerParams(has_side_effects=DATAFLOW),
    )(*srcs, *lands, *send, *recv, after)
    return out[n:]


def _adamw(w, g, m, v):
    m2 = ADAM_B1 * m + (1.0 - ADAM_B1) * g
    v2 = ADAM_B2 * v + (1.0 - ADAM_B2) * (g * g)
    m_hat = m2 / (1.0 - ADAM_B1 ** ADAM_STEP)
    v_hat = v2 / (1.0 - ADAM_B2 ** ADAM_STEP)
    delta = -ADAM_LR * (m_hat / (jnp.sqrt(v_hat) + ADAM_EPS) + ADAM_WD * w)
    return delta, m2, v2


def _adamw_big(name, slots, w, m, v):
    R, C = w.shape
    tr = next((t for t in (256, 352) if R % t == 0), R)

    def body(s_ref, w_ref, m_ref, v_ref, g_ref, d_ref, m2_ref, v2_ref):
        g = s_ref[0].astype(F32)
        for k in range(1, N_DEV):
            g = g + s_ref[k].astype(F32)
        d, m2, v2 = _adamw(w_ref[...], g, m_ref[...], v_ref[...])
        g_ref[...] = g
        d_ref[...] = d
        m2_ref[...] = m2
        v2_ref[...] = v2

    row = pl.BlockSpec((tr, C), lambda i: (i, 0))
    return pl.pallas_call(
        body, name=name, grid=(R // tr,),
        in_specs=[pl.BlockSpec((N_DEV, tr, C), lambda i: (0, i, 0)), row, row, row],
        out_specs=[row] * 4, out_shape=[S((R, C), F32)] * 4,
        compiler_params=_cp(1))(slots, w, m, v)


TINY_ROWS = (("b_s", 8), ("g_ffn1", 8), ("g_mix", 8), ("g_ca", 8), ("g_mem", 8), ("g_ffn2", 8), ("g_sgu", 4),
             ("g_fox_o", 4), ("g_gmlp_o", 4), ("g_cq", 2), ("g_ck", 2), ("g_q", 1), ("g_k", 1), ("b_f", 1))
TINY_P = 72


def _pack_tiny(d):
    rows = []
    for name, r in TINY_ROWS:
        flat = d[name].reshape(-1)
        rows.append(jnp.pad(flat, (0, r * LANES - flat.shape[0])).reshape(r, LANES))
    used = sum(r for _, r in TINY_ROWS)
    rows.append(jnp.zeros((TINY_P - used, LANES), F32))
    return jnp.concatenate(rows, axis=0)


def _unpack_tiny(packed, shapes):
    out, at = {}, 0
    for name, r in TINY_ROWS:
        shape = shapes[name]
        size = 1
        for s in shape:
            size *= s
        out[name] = packed[at:at + r].reshape(-1)[:size].reshape(shape)
        at += r
    return out


WEIGHTS =('g_ffn1', 'w_ffn1_in', 'w_ffn1_out', 'g_mix', 'w_in', 'b_f', 'g_q', 'g_k', 'g_sgu', 'w_s', 'b_s',
           'g_fox_o', 'g_gmlp_o', 'w_out', 'g_ca', 'g_mem', 'w_cq', 'w_ckv', 'g_cq', 'g_ck', 'w_co', 'g_ffn2',
           'w_ffn2_in', 'w_ffn2_out')
BIG = ('w_ffn1_in', 'w_ffn1_out', 'w_in', 'w_out', 'w_cq', 'w_ckv', 'w_co', 'w_ffn2_in', 'w_ffn2_out')
TRANSPOSED = ('w_ffn1_in', 'w_in', 'w_ffn2_in')
GATHER_GROUPS = {"ffn1_up": ("w_ffn1_in",), "ffn1_dn": ("w_ffn1_out",), "mix": ("w_in", "w_out"),
                 "ca": ("w_cq", "w_ckv", "w_co"), "ffn2": ("w_ffn2_in", "w_ffn2_out")}
GATHER_STAGES = ((None, ("w_ffn1_in",)), ("ffn1_up", ("w_ffn1_out", "w_in", "w_out")),
                 ("ffn1_dn", ("w_cq", "w_ckv", "w_co", "w_ffn2_in", "w_ffn2_out")))
QKV_W = 3 * FOX_W
UV_OFF = QKV_W + FOX_HEADS


def kernel(x, mem, g_ffn1, w_ffn1_in, w_ffn1_out, g_mix, w_in, b_f, g_q, g_k, g_sgu, w_s, b_s, g_fox_o, g_gmlp_o, w_out, g_ca, g_mem, w_cq, w_ckv, g_cq, g_ck, w_co, g_ffn2, w_ffn2_in, w_ffn2_out, loss_target, m_g_ffn1, m_w_ffn1_in, m_w_ffn1_out, m_g_mix, m_w_in, m_b_f, m_g_q, m_g_k, m_g_sgu, m_w_s, m_b_s, m_g_fox_o, m_g_gmlp_o, m_w_out, m_g_ca, m_g_mem, m_w_cq, m_w_ckv, m_g_cq, m_g_ck, m_w_co, m_g_ffn2, m_w_ffn2_in, m_w_ffn2_out, v_g_ffn1, v_w_ffn1_in, v_w_ffn1_out, v_g_mix, v_w_in, v_b_f, v_g_q, v_g_k, v_g_sgu, v_w_s, v_b_s, v_g_fox_o, v_g_gmlp_o, v_w_out, v_g_ca, v_g_mem, v_w_cq, v_w_ckv, v_g_cq, v_g_ck, v_w_co, v_g_ffn2, v_w_ffn2_in, v_w_ffn2_out):
    args = dict(locals())
    w = {n: args[n] for n in WEIGHTS}
    mo = {n: args["m_" + n] for n in WEIGHTS}
    vo = {n: args["v_" + n] for n in WEIGHTS}
    D = D_MODEL

    def local(n, a):
        return a[0].T if n in TRANSPOSED else a[0]

    shards = {n: local(n, w[n]).astype(BF) for n in BIG}
    fb = shards["w_ffn1_in"].shape[0]
    handles = {}

    def start_gather(stage, names, arrays):
        snd, rcv, src, land, _ = _copy_start("gather_start_%d" % stage, arrays, _place_own(arrays, True), True)
        for i, n in enumerate(names):
            handles[n] = (src[i], land[i], snd[i], rcv[i])

    start_gather(0, GATHER_STAGES[0][1], [shards[n] for n in GATHER_STAGES[0][1]])

    def weights(group, after):
        names = GATHER_GROUPS[group]
        hs = [handles[n] for n in names]
        got = _copy_wait("gather_wait_" + group, [h[0] for h in hs], [h[1] for h in hs], [h[2] for h in hs],
                         [h[3] for h in hs], after, True)
        for stage, (trigger, members) in enumerate(GATHER_STAGES):
            if trigger == group:
                held = lax.optimization_barrier((tuple(shards[n] for n in members), got[0]))[0]
                start_gather(stage, members, list(held))
        got = dict(zip(names, got))
        if group == "ffn1_up":
            return {"wup1": got["w_ffn1_in"].reshape(2, N_FFN_BLK, fb, D)}
        if group == "ffn1_dn":
            return {"wdn1": got["w_ffn1_out"].reshape(N_FFN_BLK, fb, D)}
        if group == "mix":
            full = got["w_in"].reshape(-1, D)
            wz = jnp.concatenate([full[:QKV_W], full[UV_OFF:], full[QKV_W:UV_OFF],
                                  jnp.zeros((LANES - FOX_HEADS, D), BF)], axis=0)
            return {"wz": wz, "wout": got["w_out"].reshape(D, D)}
        if group == "ca":
            return {"wcq": got["w_cq"].reshape(D, D), "wco": got["w_co"].reshape(D, D), "wckv": got["w_ckv"]}
        return {"wup2": got["w_ffn2_in"].reshape(2, N_FFN_BLK, fb, D),
                "wdn2": got["w_ffn2_out"].reshape(N_FFN_BLK, fb, D)}

    flying = {}

    def emit(group, g):
        if group == "ffn2":
            parts = {"w_ffn2_in": g["wup2"], "w_ffn2_out": g["wdn2"].reshape(N_DEV, -1, D)}
        elif group == "ffn1":
            parts = {"w_ffn1_in": g["wup1"], "w_ffn1_out": g["wdn1"].reshape(N_DEV, -1, D)}
        else:
            gz = g["wz"]
            g_in = jnp.concatenate([gz[:QKV_W], gz[Z_F:Z_F + FOX_HEADS], gz[QKV_W:Z_F]], axis=0)
            parts = {"w_in": g_in.reshape(N_DEV, -1, D).astype(BF),
                     "w_out": g["wout"].reshape(N_DEV, -1, D), "w_cq": g["wcq"].reshape(N_DEV, -1, D),
                     "w_co": g["wco"].reshape(N_DEV, -1, D), "w_ckv": g["wckv"]}
        names = list(parts)
        srcs = [parts[n] for n in names]
        *copies, token = _copy_start("exchange_start_" + group, srcs, _place_own(srcs, False), False)
        flying[group] = (names, copies)
        return token

    tiny_names = [n for n, _ in TINY_ROWS]
    small = {n: (w[n][0] if n == "b_s" else w[n]) for n in tiny_names}
    small["w_s"] = w["w_s"][0]

    sq, dx0, gs = _local_step(x[0], mem[0], loss_target[0], small, weights, emit)
    loss = lax.psum(sq[0, 0], ("x", "y", "c")) * (0.5 / D)

    sm_parts = [gs["w_s"].reshape(-1, LANES), _pack_tiny(gs)]
    sm_snd, sm_rcv, sm_src, sm_land, sm_token = _copy_start("small_start", sm_parts, _place_own(sm_parts, True), True)

    grad, delta, new_m, new_v = {}, {}, {}, {}

    def update(group, after):
        names, (snd, rcv, srcs, lands) = flying[group]
        slots = _copy_wait("exchange_wait_" + group, srcs, lands, snd, rcv, after, False)
        for n, sl in zip(names, slots):
            g, d, m2, v2 = _adamw_big("adamw_" + n, sl, local(n, w[n]), local(n, mo[n]), local(n, vo[n]))
            grad[n], delta[n], new_m[n], new_v[n] = (
                (t.T if n in TRANSPOSED else t).reshape(w[n].shape) for t in (g, d, m2, v2))
        return d

    last = update("ffn2", sm_token)
    last = update("mid", last)
    last = update("ffn1", last)
    ws_all, tiny_all = _copy_wait("small_wait", sm_src, sm_land, sm_snd, sm_rcv, last, True)
    ws_shape = w["w_s"].shape
    for store, t in zip((grad, delta, new_m, new_v), _adamw_big(
            "adamw_w_s", ws_all, *[a["w_s"].reshape(-1, LANES) for a in (w, mo, vo)])):
        store["w_s"] = t.reshape(ws_shape)
    shapes = {n: w[n].shape for n in tiny_names}
    for store, t in zip((grad, delta, new_m, new_v), _adamw_big(
            "adamw_tiny", tiny_all, *[_pack_tiny({n: a[n] for n in tiny_names}) for a in (w, mo, vo)])):
        store.update(_unpack_tiny(t, shapes))

    return (loss, dx0[None], *[grad[n] for n in WEIGHTS], *[delta[n] for n in WEIGHTS],
            *[new_m[n] for n in WEIGHTS], *[new_v[n] for n in WEIGHTS])
```

```python
import functools

import jax
import jax.numpy as jnp
from jax import lax
from jax.experimental import pallas as pl
from jax.experimental.pallas import tpu as pltpu

F32 = jnp.float32
BF = jnp.bfloat16
S = jax.ShapeDtypeStruct

N_DEV = 8
D_MODEL = 1024
FOX_HEADS, FOX_HD = 8, 64
FOX_W = 512
GMLP_G, GMLP_GD = 8, 64
GMLP_W = 512
CHUNK = 128
CA_HEADS, CA_HD = 4, 256
N_FFN_BLK = 4
ZW = 2688
Z_Q, Z_K, Z_V, Z_U, Z_G, Z_F = 0, 512, 1024, 1536, 2048, 2560
EPS = 1e-6
NEG = -1e30
LANES = 128

ADAM_LR, ADAM_B1, ADAM_B2, ADAM_EPS, ADAM_WD, ADAM_STEP = 0.001, 0.9, 0.999, 1e-08, 0.01, 10

VMEM_LIMIT = 52 * 2 ** 20


def _cp(n_axes):
    return pltpu.CompilerParams(dimension_semantics=("arbitrary",) * n_axes, vmem_limit_bytes=VMEM_LIMIT)


def _nn(a, b):
    return jnp.dot(a, b, preferred_element_type=F32)


def _nt(a, b):
    return lax.dot_general(a, b, (((1,), (1,)), ((), ())), preferred_element_type=F32)


def _tn(a, b):
    return lax.dot_general(a, b, (((0,), (0,)), ((), ())), preferred_element_type=F32)


def _hi(a, b):
    return jnp.dot(a, b, precision=lax.Precision.HIGHEST, preferred_element_type=F32)


def _rstd(x):
    return lax.rsqrt(jnp.mean(x * x, axis=-1, keepdims=True) + EPS)


def _norm_bwd(dy, x, g):
    r = _rstd(x)
    xh = x * r
    dxh = dy * g
    dx = r * (dxh - xh * jnp.mean(dxh * xh, axis=-1, keepdims=True))
    return dx, dy * xh


def _acc_rows(ref, first, val):
    srow = jnp.sum(val, axis=0, keepdims=True)

    @pl.when(first)
    def _():
        ref[...] = srow

    @pl.when(jnp.logical_not(first))
    def _():
        ref[...] += srow


def _gelu(x):
    c = 0.7978845608028654
    return 0.5 * x * (1.0 + jnp.tanh(c * (x + 0.044715 * x * x * x)))


def _gelu_grad(x):
    c = 0.7978845608028654
    t = jnp.tanh(c * (x + 0.044715 * x * x * x))
    return 0.5 * (1.0 + t) + 0.5 * x * (1.0 - t * t) * c * (1.0 + 3 * 0.044715 * x * x)


def _tile(n, pref):
    return pref if n % pref == 0 else n


def _ffn_up(name, x, g, wup):
    T, D = x.shape
    FB = wup.shape[-2]
    tm = _tile(T, 512)

    def body(x_ref, g_ref, w_ref, a_ref, h_ref):
        @pl.when(pl.program_id(1) == 0)
        def _():
            xf = x_ref[...]
            h_ref[...] = (xf * _rstd(xf) * g_ref[...]).astype(BF)

        hb = h_ref[...]
        gg = _nt(hb, w_ref[0])
        uu = _nt(hb, w_ref[1])
        a_ref[...] = (gg * jax.nn.sigmoid(gg) * uu).astype(BF)

    return pl.pallas_call(
        body, name=name, grid=(T // tm, N_FFN_BLK),
        in_specs=[pl.BlockSpec((tm, D), lambda i, j: (i, 0)),
                  pl.BlockSpec((1, D), lambda i, j: (0, 0)),
                  pl.BlockSpec((2, None, FB, D), lambda i, j: (0, j, 0, 0))],
        out_specs=[pl.BlockSpec((None, tm, FB), lambda i, j: (j, i, 0)),
                   pl.BlockSpec((tm, D), lambda i, j: (i, 0))],
        out_shape=[S((N_FFN_BLK, T, FB), BF), S((T, D), BF)],
        compiler_params=_cp(2))(x, g, wup)


def _ffn_down(name, a, wdn, x):
    _, T, FB = a.shape
    D = x.shape[1]
    tm = _tile(T, 512)

    def body(a_ref, w_ref, x_ref, o_ref):
        j = pl.program_id(1)
        p = 0.5 * _nn(a_ref[...], w_ref[...])

        @pl.when(j == 0)
        def _():
            o_ref[...] = x_ref[...] + p

        @pl.when(j > 0)
        def _():
            o_ref[...] += p

    return pl.pallas_call(
        body, name=name, grid=(T // tm, N_FFN_BLK),
        in_specs=[pl.BlockSpec((None, tm, FB), lambda i, j: (j, i, 0)),
                  pl.BlockSpec((None, FB, D), lambda i, j: (j, 0, 0)),
                  pl.BlockSpec((tm, D), lambda i, j: (i, 0))],
        out_specs=pl.BlockSpec((tm, D), lambda i, j: (i, 0)),
        out_shape=S((T, D), F32),
        compiler_params=_cp(2))(a, wdn, x)


def _ffn_down_loss(name, a, wdn, x, target):
    _, T, FB = a.shape
    D = x.shape[1]
    tm = _tile(T, 512)

    def body(a_ref, w_ref, x_ref, t_ref, d_ref, db_ref, loss_ref, acc_ref):
        i, j = pl.program_id(0), pl.program_id(1)
        p = 0.5 * _nn(a_ref[...], w_ref[...])

        @pl.when(j == 0)
        def _():
            acc_ref[...] = x_ref[...] + p

        @pl.when(j > 0)
        def _():
            acc_ref[...] += p

        @pl.when(j == N_FFN_BLK - 1)
        def _():
            diff = acc_ref[...] - t_ref[...]
            dy = diff * (1.0 / D)
            d_ref[...] = dy
            db_ref[...] = dy.astype(BF)
            sq = jnp.zeros((8, LANES), F32) + jnp.sum(diff * diff)

            @pl.when(i == 0)
            def _():
                loss_ref[...] = sq

            @pl.when(i > 0)
            def _():
                loss_ref[...] += sq

    return pl.pallas_call(
        body, name=name, grid=(T // tm, N_FFN_BLK),
        in_specs=[pl.BlockSpec((None, tm, FB), lambda i, j: (j, i, 0)),
                  pl.BlockSpec((None, FB, D), lambda i, j: (j, 0, 0)),
                  pl.BlockSpec((tm, D), lambda i, j: (i, 0)),
                  pl.BlockSpec((tm, D), lambda i, j: (i, 0))],
        out_specs=[pl.BlockSpec((tm, D), lambda i, j: (i, 0)),
                   pl.BlockSpec((tm, D), lambda i, j: (i, 0)),
                   pl.BlockSpec((8, LANES), lambda i, j: (0, 0))],
        out_shape=[S((T, D), F32), S((T, D), BF), S((8, LANES), F32)],
        scratch_shapes=[pltpu.VMEM((tm, D), F32)],
        compiler_params=_cp(2))(a, wdn, x, target)


def _ffn_bwd_act(name, dyb, h, wup, wdn):
    T, D = h.shape
    FB = wup.shape[-2]
    tm = _tile(T, 512)

    def body(d_ref, h_ref, wu_ref, wd_ref, o_ref):
        da = 0.5 * _nt(d_ref[...], wd_ref[...])
        hb = h_ref[...]
        gg = _nt(hb, wu_ref[0])
        uu = _nt(hb, wu_ref[1])
        sg = jax.nn.sigmoid(gg)
        o_ref[0] = (da * uu * (sg * (1.0 + gg * (1.0 - sg)))).astype(BF)
        o_ref[1] = (da * (gg * sg)).astype(BF)

    return pl.pallas_call(
        body, name=name, grid=(T // tm, N_FFN_BLK),
        in_specs=[pl.BlockSpec((tm, D), lambda i, j: (i, 0)),
                  pl.BlockSpec((tm, D), lambda i, j: (i, 0)),
                  pl.BlockSpec((2, None, FB, D), lambda i, j: (0, j, 0, 0)),
                  pl.BlockSpec((None, FB, D), lambda i, j: (j, 0, 0))],
        out_specs=pl.BlockSpec((2, None, tm, FB), lambda i, j: (0, j, i, 0)),
        out_shape=S((2, N_FFN_BLK, T, FB), BF),
        compiler_params=_cp(2))(dyb, h, wup, wdn)


def _ffn_dx(name, dgu, wup, x, g, dy):
    T, D = x.shape
    FB = wup.shape[-2]
    tm = _tile(T, 512)

    def body(d_ref, w_ref, x_ref, g_ref, dy_ref, dx_ref, dg_ref, acc_ref):
        i, j = pl.program_id(0), pl.program_id(1)
        p = _nn(d_ref[0], w_ref[0]) + _nn(d_ref[1], w_ref[1])

        @pl.when(j == 0)
        def _():
            acc_ref[...] = p

        @pl.when(j > 0)
        def _():
            acc_ref[...] += p

        @pl.when(j == N_FFN_BLK - 1)
        def _():
            dx, dgr = _norm_bwd(acc_ref[...], x_ref[...], g_ref[...])
            dx_ref[...] = dx + dy_ref[...]
            _acc_rows(dg_ref, i == 0, dgr)

    return pl.pallas_call(
        body, name=name, grid=(T // tm, N_FFN_BLK),
        in_specs=[pl.BlockSpec((2, None, tm, FB), lambda i, j: (0, j, i, 0)),
                  pl.BlockSpec((2, None, FB, D), lambda i, j: (0, j, 0, 0)),
                  pl.BlockSpec((tm, D), lambda i, j: (i, 0)),
                  pl.BlockSpec((1, D), lambda i, j: (0, 0)),
                  pl.BlockSpec((tm, D), lambda i, j: (i, 0))],
        out_specs=[pl.BlockSpec((tm, D), lambda i, j: (i, 0)),
                   pl.BlockSpec((1, D), lambda i, j: (0, 0))],
        out_shape=[S((T, D), F32), S((1, D), F32)],
        scratch_shapes=[pltpu.VMEM((tm, D), F32)],
        compiler_params=_cp(2))(dgu, wup, x, g, dy)


def _tn_matmul(name, a, a_spec, b, b_spec, out_shape, out_spec, grid, acc_shape, scale=1.0):
    nk = grid[1]

    def body(a_ref, b_ref, o_ref, acc_ref):
        k = pl.program_id(1)
        p = _tn(a_ref[...], b_ref[...])

        @pl.when(k == 0)
        def _():
            acc_ref[...] = p

        @pl.when(k > 0)
        def _():
            acc_ref[...] += p

        @pl.when(k == nk - 1)
        def _():
            o_ref[...] = (acc_ref[...] * scale).astype(o_ref.dtype)

    return pl.pallas_call(
        body, name=name, grid=grid, in_specs=[a_spec, b_spec], out_specs=out_spec, out_shape=out_shape,
        scratch_shapes=[pltpu.VMEM(acc_shape, F32)], compiler_params=_cp(2))(a, b)


def _ffn_dw(name, h, dgu, a, dyb):
    T, D = h.shape
    FB = a.shape[-1]
    tk = _tile(T, 512)
    nk = T // tk
    dgu8 = dgu.reshape(2 * N_FFN_BLK, T, FB)
    dwup = _tn_matmul(
        name + "_dwup", dgu8, pl.BlockSpec((None, tk, FB), lambda j, k: (j, k, 0)),
        h, pl.BlockSpec((tk, D), lambda j, k: (k, 0)),
        S((2 * N_FFN_BLK, FB, D), BF), pl.BlockSpec((None, FB, D), lambda j, k: (j, 0, 0)),
        (2 * N_FFN_BLK, nk), (FB, D))
    dwdn = _tn_matmul(
        name + "_dwdn", a, pl.BlockSpec((None, tk, FB), lambda j, k: (j, k, 0)),
        dyb, pl.BlockSpec((tk, D), lambda j, k: (k, 0)),
        S((N_FFN_BLK, FB, D), BF), pl.BlockSpec((None, FB, D), lambda j, k: (j, 0, 0)),
        (N_FFN_BLK, nk), (FB, D), scale=0.5)
    return dwup, dwdn


def _mix_proj(x, g, wz):
    T, D = x.shape
    tm = _tile(T, 256)

    def body(x_ref, g_ref, w_ref, z_ref, h_ref):
        xf = x_ref[...]
        hb = (xf * _rstd(xf) * g_ref[...]).astype(BF)
        h_ref[...] = hb
        z_ref[...] = _nt(hb, w_ref[...])

    return pl.pallas_call(
        body, name="mix_proj", grid=(T // tm,),
        in_specs=[pl.BlockSpec((tm, D), lambda i: (i, 0)),
                  pl.BlockSpec((1, D), lambda i: (0, 0)),
                  pl.BlockSpec((ZW, D), lambda i: (0, 0))],
        out_specs=[pl.BlockSpec((tm, ZW), lambda i: (i, 0)),
                   pl.BlockSpec((tm, D), lambda i: (i, 0))],
        out_shape=[S((T, ZW), F32), S((T, D), BF)],
        compiler_params=_cp(1))(x, g, wz)


def _tri(n, lower):
    r = lax.broadcasted_iota(jnp.int32, (n, n), 0)
    c = lax.broadcasted_iota(jnp.int32, (n, n), 1)
    return (r >= c) if lower else (r <= c)


def _spatial_mix(vgn_b, ws_ref, bst, tm):
    tril = _tri(CHUNK, True)
    wms = [jnp.where(tril, ws_ref[g], 0.0).astype(BF) for g in range(GMLP_G)]
    rows = []
    for c in range(tm // CHUNK):
        cols = []
        for g in range(GMLP_G):
            vs = vgn_b[c * CHUNK:(c + 1) * CHUNK, g * GMLP_GD:(g + 1) * GMLP_GD]
            cols.append(_nn(wms[g], vs) + bst[:, g:g + 1])
        rows.append(jnp.concatenate(cols, axis=1))
    return jnp.concatenate(rows, axis=0), wms


def _mix_prep(z, bf128, g_q, g_k, g_sgu, w_s, b_st, g_go):
    T = z.shape[0]
    tm = _tile(T, 256)

    def body(z_ref, bf_ref, gq_ref, gk_ref, gs_ref, ws_ref, bst_ref, go_ref,
             q_ref, k_ref, v_ref, c_ref, y_ref, carry_ref):
        i = pl.program_id(0)

        @pl.when(i == 0)
        def _():
            carry_ref[...] = jnp.zeros_like(carry_ref)

        for h in range(FOX_HEADS):
            hs = slice(h * FOX_HD, (h + 1) * FOX_HD)
            qh = z_ref[:, Z_Q + h * FOX_HD:Z_Q + (h + 1) * FOX_HD]
            kh = z_ref[:, Z_K + h * FOX_HD:Z_K + (h + 1) * FOX_HD]
            q_ref[:, hs] = (qh * _rstd(qh) * gq_ref[...] * 0.125).astype(BF)
            k_ref[:, hs] = (kh * _rstd(kh) * gk_ref[...]).astype(BF)
        v_ref[...] = z_ref[:, Z_V:Z_V + FOX_W].astype(BF)

        fl = z_ref[:, Z_F:Z_F + LANES] + bf_ref[...]
        logf = jnp.minimum(fl, 0.0) - jnp.log1p(jnp.exp(-jnp.abs(fl)))
        csum = _hi(_tri(tm, True).astype(F32), logf) + carry_ref[...]
        c_ref[...] = csum
        carry_ref[...] = csum[tm - 1:tm, :]

        u = _gelu(z_ref[:, Z_U:Z_U + GMLP_W])
        vg = _gelu(z_ref[:, Z_G:Z_G + GMLP_W])
        vgn = (vg * _rstd(vg) * gs_ref[...]).astype(BF)
        mixed, _ = _spatial_mix(vgn, ws_ref, bst_ref[...], tm)
        sgu = u * mixed
        y_ref[...] = (sgu * _rstd(sgu) * go_ref[...]).astype(BF)

    row = lambda i: (i, 0)
    fix2 = lambda i: (0, 0)
    return pl.pallas_call(
        body, name="mix_prep", grid=(T // tm,),
        in_specs=[pl.BlockSpec((tm, ZW), row),
                  pl.BlockSpec((1, LANES), fix2), pl.BlockSpec((1, FOX_HD), fix2), pl.BlockSpec((1, FOX_HD), fix2),
                  pl.BlockSpec((1, GMLP_W), fix2), pl.BlockSpec((GMLP_G, CHUNK, CHUNK), lambda i: (0, 0, 0)),
                  pl.BlockSpec((CHUNK, GMLP_G), fix2), pl.BlockSpec((1, GMLP_W), fix2)],
        out_specs=[pl.BlockSpec((tm, FOX_W), row), pl.BlockSpec((tm, FOX_W), row), pl.BlockSpec((tm, FOX_W), row),
                   pl.BlockSpec((tm, LANES), row), pl.BlockSpec((tm, GMLP_W), row)],
        out_shape=[S((T, FOX_W), BF), S((T, FOX_W), BF), S((T, FOX_W), BF), S((T, LANES), F32), S((T, GMLP_W), BF)],
        scratch_shapes=[pltpu.VMEM((1, LANES), F32)],
        compiler_params=_cp(1))(z, bf128, g_q, g_k, g_sgu, w_s, b_st, g_go)


def _fox_fwd(q, k, v, ccol, crow):
    T = q.shape[0]
    tq = _tile(T, 512)
    nq = T // tq

    def body(q_ref, k_ref, v_ref, cc_ref, cr_ref, o_ref, lse_ref, m_sc, l_sc, acc_sc):
        i, j = pl.program_id(0), pl.program_id(1)

        @pl.when(j == 0)
        def _():
            m_sc[...] = jnp.full(m_sc.shape, NEG, F32)
            l_sc[...] = jnp.zeros_like(l_sc)
            acc_sc[...] = jnp.zeros_like(acc_sc)

        def step(masked):
            cc = cc_ref[...]
            cr = cr_ref[...]
            mask = _tri(tq, True) if masked else None
            for h in range(FOX_HEADS):
                hs = slice(h * FOX_HD, (h + 1) * FOX_HD)
                s = _nt(q_ref[:, hs], k_ref[:, hs]) + (cc[:, h:h + 1] - cr[h:h + 1, :])
                if masked:
                    s = jnp.where(mask, s, NEG)
                m_prev = m_sc[h]
                m_new = jnp.maximum(m_prev, jnp.max(s, axis=1, keepdims=True))
                alpha = jnp.exp(m_prev - m_new)
                p = jnp.exp(s - m_new)
                l_sc[h] = alpha * l_sc[h] + jnp.sum(p, axis=1, keepdims=True)
                acc_sc[:, hs] = alpha * acc_sc[:, hs] + _nn(p.astype(BF), v_ref[:, hs])
                m_sc[h] = m_new

        @pl.when(j < i)
        def _():
            step(False)

        @pl.when(j == i)
        def _():
            step(True)
            lse_ref[...] = jnp.zeros_like(lse_ref)
            for h in range(FOX_HEADS):
                hs = slice(h * FOX_HD, (h + 1) * FOX_HD)
                o_ref[:, hs] = acc_sc[:, hs] / l_sc[h]
                lse_ref[:, h:h + 1] = m_sc[h] + jnp.log(l_sc[h])

    qi = lambda i, j: (i, 0)
    kj = lambda i, j: (jnp.minimum(i, j), 0)
    return pl.pallas_call(
        body, name="fox_fwd", grid=(nq, nq),
        in_specs=[pl.BlockSpec((tq, FOX_W), qi), pl.BlockSpec((tq, FOX_W), kj), pl.BlockSpec((tq, FOX_W), kj),
                  pl.BlockSpec((tq, LANES), qi), pl.BlockSpec((FOX_HEADS, tq), lambda i, j: (0, jnp.minimum(i, j)))],
        out_specs=[pl.BlockSpec((tq, FOX_W), qi), pl.BlockSpec((tq, LANES), qi)],
        out_shape=[S((T, FOX_W), F32), S((T, LANES), F32)],
        scratch_shapes=[pltpu.VMEM((FOX_HEADS, tq, 1), F32), pltpu.VMEM((FOX_HEADS, tq, 1), F32),
                        pltpu.VMEM((tq, FOX_W), F32)],
        compiler_params=_cp(2))(q, k, v, ccol, crow)


def _fox_bwd(q, k, v, dob, lse, dsum, ccol, crow):
    T = q.shape[0]
    tq = _tile(T, 512)
    nq = T // tq

    def body(q_ref, k_ref, v_ref, do_ref, lse_ref, ds_ref, cc_ref, cr_ref,
             dq_ref, dk_ref, dv_ref, dcq_ref, dck_ref):
        j, i = pl.program_id(0), pl.program_id(1)

        @pl.when(jnp.logical_and(i == 0, j == 0))
        def _():
            dq_ref[...] = jnp.zeros_like(dq_ref)
            dcq_ref[...] = jnp.zeros_like(dcq_ref)

        @pl.when(i == 0)
        def _():
            dk_ref[...] = jnp.zeros_like(dk_ref)
            dv_ref[...] = jnp.zeros_like(dv_ref)
            dck_ref[...] = jnp.zeros_like(dck_ref)

        def step(masked):
            rows = pl.ds(pl.multiple_of(i * tq, tq), tq)
            cc = cc_ref[...]
            cr = cr_ref[...]
            lse_t = lse_ref[...]
            dsum_t = ds_ref[...]
            mask = _tri(tq, True) if masked else None
            for h in range(FOX_HEADS):
                hs = slice(h * FOX_HD, (h + 1) * FOX_HD)
                qh, kh, vh, doh = q_ref[:, hs], k_ref[:, hs], v_ref[:, hs], do_ref[:, hs]
                s = _nt(qh, kh) + (cc[:, h:h + 1] - cr[h:h + 1, :])
                if masked:
                    s = jnp.where(mask, s, NEG)
                p = jnp.exp(s - lse_t[:, h:h + 1])
                dp = _nt(doh, vh)
                ds = p * (dp - dsum_t[:, h:h + 1])
                dsb = ds.astype(BF)
                dv_ref[:, hs] += _tn(p.astype(BF), doh)
                dk_ref[:, hs] += _tn(dsb, qh)
                dq_ref[rows, hs] += _nn(dsb, kh)
                dcq_ref[rows, h:h + 1] += jnp.sum(ds, axis=1, keepdims=True)
                dck_ref[h:h + 1, :] -= jnp.sum(ds, axis=0, keepdims=True)

        @pl.when(i > j)
        def _():
            step(False)

        @pl.when(i == j)
        def _():
            step(True)

    qi = lambda j, i: (jnp.maximum(i, j), 0)
    kj = lambda j, i: (j, 0)
    whole = lambda j, i: (0, 0)
    return pl.pallas_call(
        body, name="fox_bwd", grid=(nq, nq),
        in_specs=[pl.BlockSpec((tq, FOX_W), qi), pl.BlockSpec((tq, FOX_W), kj), pl.BlockSpec((tq, FOX_W), kj),
                  pl.BlockSpec((tq, FOX_W), qi), pl.BlockSpec((tq, LANES), qi), pl.BlockSpec((tq, LANES), qi),
                  pl.BlockSpec((tq, LANES), qi), pl.BlockSpec((FOX_HEADS, tq), lambda j, i: (0, j))],
        out_specs=[pl.BlockSpec((T, FOX_W), whole), pl.BlockSpec((tq, FOX_W), kj), pl.BlockSpec((tq, FOX_W), kj),
                   pl.BlockSpec((T, LANES), whole), pl.BlockSpec((FOX_HEADS, tq), lambda j, i: (0, j))],
        out_shape=[S((T, FOX_W), F32), S((T, FOX_W), F32), S((T, FOX_W), F32), S((T, LANES), F32),
                   S((FOX_HEADS, T), F32)],
        compiler_params=_cp(2))(q, k, v, dob, lse, dsum, ccol, crow)


def _mix_out(attn, yg, g_fo, wout, x):
    T, D = x.shape
    tm = _tile(T, 512)

    def body(a_ref, y_ref, g_ref, w_ref, x_ref, o_ref):
        at = a_ref[...]
        yf = (at * _rstd(at) * g_ref[...]).astype(BF)
        o_ref[...] = x_ref[...] + _nn(yf, w_ref[:FOX_W, :]) + _nn(y_ref[...], w_ref[FOX_W:, :])

    row = lambda i: (i, 0)
    return pl.pallas_call(
        body, name="mix_out", grid=(T // tm,),
        in_specs=[pl.BlockSpec((tm, FOX_W), row), pl.BlockSpec((tm, GMLP_W), row),
                  pl.BlockSpec((1, FOX_W), lambda i: (0, 0)), pl.BlockSpec((D, D), lambda i: (0, 0)),
                  pl.BlockSpec((tm, D), row)],
        out_specs=pl.BlockSpec((tm, D), row),
        out_shape=S((T, D), F32),
        compiler_params=_cp(1))(attn, yg, g_fo, wout, x)


def _mix_out_bwd(dx, attn, yg, g_fo, wout):
    T, D = dx.shape
    tm = _tile(T, 256)
    n = T // tm

    def body(dx_ref, a_ref, y_ref, g_ref, w_ref, da_ref, dsum_ref, dyg_ref, dw_ref, dg_ref, acc_ref):
        i = pl.program_id(0)
        dxb = dx_ref[...].astype(BF)
        at = a_ref[...]
        yf = (at * _rstd(at) * g_ref[...]).astype(BF)
        dy = _nt(dxb, w_ref[...])
        p_top = _tn(yf, dxb)
        p_bot = _tn(y_ref[...], dxb)

        @pl.when(i == 0)
        def _():
            acc_ref[:FOX_W, :] = p_top
            acc_ref[FOX_W:, :] = p_bot

        @pl.when(i > 0)
        def _():
            acc_ref[:FOX_W, :] += p_top
            acc_ref[FOX_W:, :] += p_bot

        @pl.when(i == n - 1)
        def _():
            dw_ref[...] = acc_ref[...].astype(BF)

        dat, dgr = _norm_bwd(dy[:, :FOX_W], at, g_ref[...])
        _acc_rows(dg_ref, i == 0, dgr)
        da_ref[...] = dat.astype(BF)
        dyg_ref[...] = dy[:, FOX_W:]
        prod = dat * at
        dsum_ref[...] = jnp.zeros_like(dsum_ref)
        for h in range(FOX_HEADS):
            dsum_ref[:, h:h + 1] = jnp.sum(prod[:, h * FOX_HD:(h + 1) * FOX_HD], axis=1, keepdims=True)

    row = lambda i: (i, 0)
    fix = lambda i: (0, 0)
    return pl.pallas_call(
        body, name="mix_out_bwd", grid=(n,),
        in_specs=[pl.BlockSpec((tm, D), row), pl.BlockSpec((tm, FOX_W), row), pl.BlockSpec((tm, GMLP_W), row),
                  pl.BlockSpec((1, FOX_W), fix), pl.BlockSpec((D, D), fix)],
        out_specs=[pl.BlockSpec((tm, FOX_W), row), pl.BlockSpec((tm, LANES), row), pl.BlockSpec((tm, GMLP_W), row),
                   pl.BlockSpec((D, D), fix), pl.BlockSpec((1, FOX_W), fix)],
        out_shape=[S((T, FOX_W), BF), S((T, LANES), F32), S((T, GMLP_W), F32), S((D, D), BF), S((1, FOX_W), F32)],
        scratch_shapes=[pltpu.VMEM((D, D), F32)],
        compiler_params=_cp(1))(dx, attn, yg, g_fo, wout)


def _mix_prep_bwd(z, dq, dk, dv, dcq, dck, dyg, bf128, g_q, g_k, g_sgu, w_s, b_st, g_go):
    T = z.shape[0]
    tm = _tile(T, 256)
    n = T // tm

    def body(z_ref, dq_ref, dk_ref, dv_ref, dcq_ref, dck_ref, dyg_ref, bf_ref, gq_ref, gk_ref, gs_ref, ws_ref,
             bst_ref, go_ref, dz_ref, dgq_ref, dgk_ref, dgs_ref, dgo_ref, dws_ref, dbst_ref, dbf_ref, carry_ref):
        i = pl.program_id(0)
        first = i == 0

        @pl.when(first)
        def _():
            carry_ref[...] = jnp.zeros_like(carry_ref)

        gq_rows, gk_rows = [], []
        for h in range(FOX_HEADS):
            hs = slice(h * FOX_HD, (h + 1) * FOX_HD)
            dqh, gqr = _norm_bwd(dq_ref[:, hs] * 0.125, z_ref[:, Z_Q + h * FOX_HD:Z_Q + (h + 1) * FOX_HD], gq_ref[...])
            dkh, gkr = _norm_bwd(dk_ref[:, hs], z_ref[:, Z_K + h * FOX_HD:Z_K + (h + 1) * FOX_HD], gk_ref[...])
            dz_ref[:, Z_Q + h * FOX_HD:Z_Q + (h + 1) * FOX_HD] = dqh.astype(BF)
            dz_ref[:, Z_K + h * FOX_HD:Z_K + (h + 1) * FOX_HD] = dkh.astype(BF)
            gq_rows.append(gqr)
            gk_rows.append(gkr)
        _acc_rows(dgq_ref, first, functools.reduce(lambda a, b: a + b, gq_rows))
        _acc_rows(dgk_ref, first, functools.reduce(lambda a, b: a + b, gk_rows))
        dz_ref[:, Z_V:Z_V + FOX_W] = dv_ref[...].astype(BF)

        dc = dcq_ref[...] + dck_ref[...]
        dlogf = _hi(_tri(tm, False).astype(F32), dc) + carry_ref[...]
        carry_ref[...] = dlogf[0:1, :]
        fl = z_ref[:, Z_F:Z_F + LANES] + bf_ref[...]
        lane = lax.broadcasted_iota(jnp.int32, (tm, LANES), 1)
        df = jnp.where(lane < FOX_HEADS, dlogf * jax.nn.sigmoid(-fl), 0.0)
        dz_ref[:, Z_F:Z_F + LANES] = df.astype(BF)
        _acc_rows(dbf_ref, first, df)

        u_pre = z_ref[:, Z_U:Z_U + GMLP_W]
        vg_pre = z_ref[:, Z_G:Z_G + GMLP_W]
        u = _gelu(u_pre)
        vg = _gelu(vg_pre)
        vgn = (vg * _rstd(vg) * gs_ref[...]).astype(BF)
        bst = bst_ref[...]
        mixed, wms = _spatial_mix(vgn, ws_ref, bst, tm)
        sgu = u * mixed
        dsgu, gor = _norm_bwd(dyg_ref[...], sgu, go_ref[...])
        _acc_rows(dgo_ref, first, gor)
        du = dsgu * mixed
        dmixed = dsgu * u
        dmb = dmixed.astype(BF)
        tril = _tri(CHUNK, True)
        dvgn_rows = []
        dws = [None] * GMLP_G
        dbs = [None] * GMLP_G
        for c in range(tm // CHUNK):
            cs = slice(c * CHUNK, (c + 1) * CHUNK)
            cols = []
            for g in range(GMLP_G):
                gs = slice(g * GMLP_GD, (g + 1) * GMLP_GD)
                dmc = dmb[cs, gs]
                pw = _nt(dmc, vgn[cs, gs])
                pb = jnp.sum(dmixed[cs, gs], axis=1, keepdims=True)
                dws[g] = pw if dws[g] is None else dws[g] + pw
                dbs[g] = pb if dbs[g] is None else dbs[g] + pb
                cols.append(_tn(wms[g], dmc))
            dvgn_rows.append(jnp.concatenate(cols, axis=1))
        dvgn = jnp.concatenate(dvgn_rows, axis=0)
        dbs_t = jnp.concatenate(dbs, axis=1)
        for g in range(GMLP_G):
            dwg = jnp.where(tril, dws[g], 0.0)

            @pl.when(first)
            def _():
                dws_ref[g] = dwg

            @pl.when(jnp.logical_not(first))
            def _():
                dws_ref[g] += dwg

        @pl.when(first)
        def _():
            dbst_ref[...] = dbs_t

        @pl.when(jnp.logical_not(first))
        def _():
            dbst_ref[...] += dbs_t

        dvg, gsr = _norm_bwd(dvgn, vg, gs_ref[...])
        _acc_rows(dgs_ref, first, gsr)
        dz_ref[:, Z_U:Z_U + GMLP_W] = (du * _gelu_grad(u_pre)).astype(BF)
        dz_ref[:, Z_G:Z_G + GMLP_W] = (dvg * _gelu_grad(vg_pre)).astype(BF)

    rev = lambda i: (n - 1 - i, 0)
    fix = lambda i: (0, 0)
    fix3 = lambda i: (0, 0, 0)
    return pl.pallas_call(
        body, name="mix_prep_bwd", grid=(n,),
        in_specs=[pl.BlockSpec((tm, ZW), rev), pl.BlockSpec((tm, FOX_W), rev), pl.BlockSpec((tm, FOX_W), rev),
                  pl.BlockSpec((tm, FOX_W), rev), pl.BlockSpec((tm, LANES), rev), pl.BlockSpec((tm, LANES), rev),
                  pl.BlockSpec((tm, GMLP_W), rev),
                  pl.BlockSpec((1, LANES), fix), pl.BlockSpec((1, FOX_HD), fix), pl.BlockSpec((1, FOX_HD), fix),
                  pl.BlockSpec((1, GMLP_W), fix), pl.BlockSpec((GMLP_G, CHUNK, CHUNK), fix3),
                  pl.BlockSpec((CHUNK, GMLP_G), fix), pl.BlockSpec((1, GMLP_W), fix)],
        out_specs=[pl.BlockSpec((tm, ZW), rev), pl.BlockSpec((1, FOX_HD), fix), pl.BlockSpec((1, FOX_HD), fix),
                   pl.BlockSpec((1, GMLP_W), fix), pl.BlockSpec((1, GMLP_W), fix),
                   pl.BlockSpec((GMLP_G, CHUNK, CHUNK), fix3), pl.BlockSpec((CHUNK, GMLP_G), fix),
                   pl.BlockSpec((1, LANES), fix)],
        out_shape=[S((T, ZW), BF), S((1, FOX_HD), F32), S((1, FOX_HD), F32), S((1, GMLP_W), F32), S((1, GMLP_W), F32),
                   S((GMLP_G, CHUNK, CHUNK), F32), S((CHUNK, GMLP_G), F32), S((1, LANES), F32)],
        scratch_shapes=[pltpu.VMEM((1, LANES), F32)],
        compiler_params=_cp(1))(z, dq, dk, dv, dcq, dck, dyg, bf128, g_q, g_k, g_sgu, w_s, b_st, g_go)


def _mix_proj_bwd(dz, wz, x, g, dy):
    T, D = x.shape
    tm = _tile(T, 256)

    def body(dz_ref, w_ref, x_ref, g_ref, dy_ref, dx_ref, dxb_ref, dg_ref):
        dh = _nn(dz_ref[...], w_ref[...])
        dx, dgr = _norm_bwd(dh, x_ref[...], g_ref[...])
        dx = dx + dy_ref[...]
        dx_ref[...] = dx
        dxb_ref[...] = dx.astype(BF)
        _acc_rows(dg_ref, pl.program_id(0) == 0, dgr)

    row = lambda i: (i, 0)
    fix = lambda i: (0, 0)
    return pl.pallas_call(
        body, name="mix_proj_bwd", grid=(T // tm,),
        in_specs=[pl.BlockSpec((tm, ZW), row), pl.BlockSpec((ZW, D), fix), pl.BlockSpec((tm, D), row),
                  pl.BlockSpec((1, D), fix), pl.BlockSpec((tm, D), row)],
        out_specs=[pl.BlockSpec((tm, D), row), pl.BlockSpec((tm, D), row), pl.BlockSpec((1, D), fix)],
        out_shape=[S((T, D), F32), S((T, D), BF), S((1, D), F32)],
        compiler_params=_cp(1))(dz, wz, x, g, dy)


def _ca_kv(mem, g_mem, wckv, g_ck):
    M, D = mem.shape

    def body(m_ref, g_ref, w_ref, gk_ref, mn_ref, kr_ref, kn_ref, v_ref):
        mf = m_ref[...]
        mn = (mf * _rstd(mf) * g_ref[...]).astype(BF)
        mn_ref[...] = mn
        for h in range(CA_HEADS):
            kr = _nn(mn, w_ref[h])
            kr_ref[h] = kr
            kn_ref[h] = (kr * _rstd(kr) * gk_ref[...]).astype(BF)
            v_ref[h] = _nn(mn, w_ref[CA_HEADS + h]).astype(BF)

    hd = (CA_HEADS, M, CA_HD)
    return pl.pallas_call(
        body, name="ca_kv", out_shape=[S((M, D), BF), S(hd, F32), S(hd, BF), S(hd, BF)],
        compiler_params=pltpu.CompilerParams(vmem_limit_bytes=VMEM_LIMIT))(mem, g_mem, wckv, g_ck)


def _ca_tile_fwd(xt, gca, wcq, gcq, kn_ref, v_ref):
    hb = (xt * _rstd(xt) * gca).astype(BF)
    qc = _nn(hb, wcq)
    qr, qn, ps = [], [], []
    for h in range(CA_HEADS):
        qh = qc[:, h * CA_HD:(h + 1) * CA_HD]
        qnh = (qh * _rstd(qh) * gcq * 0.0625).astype(BF)
        s = _nt(qnh, kn_ref[h])
        e = jnp.exp(s - jnp.max(s, axis=1, keepdims=True))
        ps.append(e / jnp.sum(e, axis=1, keepdims=True))
        qr.append(qh)
        qn.append(qnh)
    return hb, qr, qn, ps


def _ca_fwd(x, g_ca, wcq, g_cq, kn, vv, wco):
    T, D = x.shape
    M = kn.shape[1]
    tm = _tile(T, 256)

    def body(x_ref, gca_ref, wcq_ref, gcq_ref, kn_ref, v_ref, wco_ref, o_ref, ob_sc):
        xt = x_ref[...]
        _, _, _, ps = _ca_tile_fwd(xt, gca_ref[...], wcq_ref[...], gcq_ref[...], kn_ref, v_ref)
        for h in range(CA_HEADS):
            ob_sc[:, h * CA_HD:(h + 1) * CA_HD] = _nn(ps[h].astype(BF), v_ref[h]).astype(BF)
        o_ref[...] = xt + _nn(ob_sc[...], wco_ref[...])

    row = lambda i: (i, 0)
    fix = lambda i: (0, 0)
    fix3 = lambda i: (0, 0, 0)
    return pl.pallas_call(
        body, name="ca_fwd", grid=(T // tm,),
        in_specs=[pl.BlockSpec((tm, D), row), pl.BlockSpec((1, D), fix), pl.BlockSpec((D, D), fix),
                  pl.BlockSpec((1, CA_HD), fix), pl.BlockSpec((CA_HEADS, M, CA_HD), fix3),
                  pl.BlockSpec((CA_HEADS, M, CA_HD), fix3), pl.BlockSpec((D, D), fix)],
        out_specs=pl.BlockSpec((tm, D), row), out_shape=S((T, D), F32),
        scratch_shapes=[pltpu.VMEM((tm, D), BF)],
        compiler_params=_cp(1))(x, g_ca, wcq, g_cq, kn, vv, wco)


def _ca_bwd(x, dy, g_ca, wcq, g_cq, kn, vv, wco):
    T, D = x.shape
    M = kn.shape[1]
    tm = _tile(T, 256)
    n = T // tm

    def body(x_ref, dy_ref, gca_ref, wcq_ref, gcq_ref, kn_ref, v_ref, wco_ref,
             dx_ref, dwq_ref, dwo_ref, dkn_ref, dv_ref, dgcq_ref, dgca_ref, aq_sc, ao_sc, ob_sc, dq_sc):
        i = pl.program_id(0)
        first = i == 0
        xt = x_ref[...]
        dyt = dy_ref[...]
        dyb = dyt.astype(BF)
        hb, qr, qn, ps = _ca_tile_fwd(xt, gca_ref[...], wcq_ref[...], gcq_ref[...], kn_ref, v_ref)
        do = _nt(dyb, wco_ref[...])
        gcq_rows = None
        for h in range(CA_HEADS):
            hs = slice(h * CA_HD, (h + 1) * CA_HD)
            p = ps[h]
            pb = p.astype(BF)
            ob_sc[:, hs] = _nn(pb, v_ref[h]).astype(BF)
            doh = do[:, hs].astype(BF)
            dp = _nt(doh, v_ref[h])
            ds = (p * (dp - jnp.sum(dp * p, axis=1, keepdims=True))).astype(BF)
            dvh = _tn(pb, doh)
            dkh = _tn(ds, qn[h])

            @pl.when(first)
            def _():
                dv_ref[h] = dvh
                dkn_ref[h] = dkh

            @pl.when(jnp.logical_not(first))
            def _():
                dv_ref[h] += dvh
                dkn_ref[h] += dkh

            dqn = _nn(ds, kn_ref[h]) * 0.0625
            dqh, gr = _norm_bwd(dqn, qr[h], gcq_ref[...])
            gcq_rows = gr if gcq_rows is None else gcq_rows + gr
            dq_sc[:, hs] = dqh.astype(BF)
        _acc_rows(dgcq_ref, first, gcq_rows)
        dqb = dq_sc[...]
        p_o = _tn(ob_sc[...], dyb)
        p_q = _tn(hb, dqb)

        @pl.when(first)
        def _():
            ao_sc[...] = p_o
            aq_sc[...] = p_q

        @pl.when(jnp.logical_not(first))
        def _():
            ao_sc[...] += p_o
            aq_sc[...] += p_q

        @pl.when(i == n - 1)
        def _():
            dwo_ref[...] = ao_sc[...].astype(BF)
            dwq_ref[...] = aq_sc[...].astype(BF)

        dh = _nt(dqb, wcq_ref[...])
        dx, gar = _norm_bwd(dh, xt, gca_ref[...])
        dx_ref[...] = dx + dyt
        _acc_rows(dgca_ref, first, gar)

    row = lambda i: (i, 0)
    fix = lambda i: (0, 0)
    fix3 = lambda i: (0, 0, 0)
    hd = (CA_HEADS, M, CA_HD)
    return pl.pallas_call(
        body, name="ca_bwd", grid=(n,),
        in_specs=[pl.BlockSpec((tm, D), row), pl.BlockSpec((tm, D), row), pl.BlockSpec((1, D), fix),
                  pl.BlockSpec((D, D), fix), pl.BlockSpec((1, CA_HD), fix), pl.BlockSpec(hd, fix3),
                  pl.BlockSpec(hd, fix3), pl.BlockSpec((D, D), fix)],
        out_specs=[pl.BlockSpec((tm, D), row), pl.BlockSpec((D, D), fix), pl.BlockSpec((D, D), fix),
                   pl.BlockSpec(hd, fix3), pl.BlockSpec(hd, fix3), pl.BlockSpec((1, CA_HD), fix),
                   pl.BlockSpec((1, D), fix)],
        out_shape=[S((T, D), F32), S((D, D), BF), S((D, D), BF), S(hd, F32), S(hd, F32), S((1, CA_HD), F32),
                   S((1, D), F32)],
        scratch_shapes=[pltpu.VMEM((D, D), F32), pltpu.VMEM((D, D), F32), pltpu.VMEM((tm, D), BF),
                        pltpu.VMEM((tm, D), BF)],
        compiler_params=_cp(1))(x, dy, g_ca, wcq, g_cq, kn, vv, wco)


def _ca_kv_bwd(mem, g_mem, mn, kraw, dkn, dvv, wckv, g_ck):
    M, D = mem.shape

    def body(m_ref, g_ref, mn_ref, kr_ref, dkn_ref, dv_ref, w_ref, gk_ref, dw_ref, dgk_ref, dgm_ref):
        mn = mn_ref[...]
        dmn = jnp.zeros((M, D), F32)
        gk_rows = None
        for h in range(CA_HEADS):
            dkr, gr = _norm_bwd(dkn_ref[h], kr_ref[h], gk_ref[...])
            gk_rows = gr if gk_rows is None else gk_rows + gr
            dkb = dkr.astype(BF)
            dvb = dv_ref[h].astype(BF)
            dw_ref[h] = _tn(mn, dkb).astype(BF)
            dw_ref[CA_HEADS + h] = _tn(mn, dvb).astype(BF)
            dmn = dmn + _nt(dkb, w_ref[h]) + _nt(dvb, w_ref[CA_HEADS + h])
        dgk_ref[...] = jnp.sum(gk_rows, axis=0, keepdims=True)
        mf = m_ref[...]
        dgm_ref[...] = jnp.sum(dmn * (mf * _rstd(mf)), axis=0, keepdims=True)

    return pl.pallas_call(
        body, name="ca_kv_bwd",
        out_shape=[S((2 * CA_HEADS, D, CA_HD), BF), S((1, CA_HD), F32), S((1, D), F32)],
        compiler_params=pltpu.CompilerParams(vmem_limit_bytes=VMEM_LIMIT))(mem, g_mem, mn, kraw, dkn, dvv, wckv, g_ck)


def _after(g, token):
    return g if token is None else g + token[0:1, 0:1]


def _local_step(x, mem, target, small, weights, emit):
    T, D = x.shape
    p = small
    bf128 = jnp.pad(p["b_f"], ((0, 0), (0, LANES - FOX_HEADS)))
    b_st = p["b_s"].T

    wup1 = weights("ffn1_up", x)["wup1"]
    a1, h1 = _ffn_up("ffn1_up", x, p["g_ffn1"], wup1)
    wdn1 = weights("ffn1_dn", h1)["wdn1"]
    x1 = _ffn_down("ffn1_down", a1, wdn1, x)
    wm = weights("mix", x1)
    z, h2 = _mix_proj(x1, p["g_mix"], wm["wz"])
    qs, kn, vb, ccol, yg = _mix_prep(z, bf128, p["g_q"], p["g_k"], p["g_sgu"], p["w_s"], b_st, p["g_gmlp_o"])
    crow = ccol[:, :FOX_HEADS].T
    attn, lse = _fox_fwd(qs, kn, vb, ccol, crow)
    x2 = _mix_out(attn, yg, p["g_fox_o"], wm["wout"], x1)
    wc = weights("ca", x2)
    mn, kraw, ckn, cvv = _ca_kv(mem, p["g_mem"], wc["wckv"], p["g_ck"])
    x3 = _ca_fwd(x2, p["g_ca"], wc["wcq"], p["g_cq"], ckn, cvv, wc["wco"])
    w2 = weights("ffn2", x3)
    a2, h4 = _ffn_up("ffn2_up", x3, p["g_ffn2"], w2["wup2"])
    dy4, dy4b, sq = _ffn_down_loss("ffn2_down", a2, w2["wdn2"], x3, target)

    gs = {}
    dgu2 = _ffn_bwd_act("ffn2_bwd_act", dy4b, h4, w2["wup2"], w2["wdn2"])
    dwup2, dwdn2 = _ffn_dw("ffn2", h4, dgu2, a2, dy4b)
    tok = emit("ffn2", {"wup2": dwup2, "wdn2": dwdn2})
    dx3, gs["g_ffn2"] = _ffn_dx("ffn2_dx", dgu2, w2["wup2"], x3, _after(p["g_ffn2"], tok), dy4)

    dx2, dwcq, dwco, dckn, dcvv, gs["g_cq"], gs["g_ca"] = _ca_bwd(
        x2, dx3, p["g_ca"], wc["wcq"], p["g_cq"], ckn, cvv, wc["wco"])
    dwckv, gs["g_ck"], gs["g_mem"] = _ca_kv_bwd(mem, p["g_mem"], mn, kraw, dckn, dcvv, wc["wckv"], p["g_ck"])

    dattn, dsum, dyg, dwout, gs["g_fox_o"] = _mix_out_bwd(dx2, attn, yg, p["g_fox_o"], wm["wout"])
    dq, dk, dv, dcq, dck = _fox_bwd(qs, kn, vb, dattn, lse, dsum, ccol, crow)
    dck_col = jnp.pad(dck.T, ((0, 0), (0, LANES - FOX_HEADS)))
    dz, gs["g_q"], gs["g_k"], gs["g_sgu"], gs["g_gmlp_o"], gs["w_s"], dbst, dbf = _mix_prep_bwd(
        z, dq, dk, dv, dcq, dck_col, dyg, bf128, p["g_q"], p["g_k"], p["g_sgu"], p["w_s"], b_st, p["g_gmlp_o"])
    gs["b_s"] = dbst.T
    gs["b_f"] = dbf[:, :FOX_HEADS]
    tk = _tile(T, 512)
    zb = ZW // 3
    dwz = _tn_matmul(
        "mix_dwz", dz, pl.BlockSpec((tk, zb), lambda j, k: (k, j)), h2, pl.BlockSpec((tk, D), lambda j, k: (k, 0)),
        S((ZW, D), F32), pl.BlockSpec((zb, D), lambda j, k: (j, 0)), (3, T // tk), (zb, D))
    tok = emit("mid", {"wcq": dwcq, "wco": dwco, "wckv": dwckv, "wout": dwout, "wz": dwz})
    dx1, dx1b, gs["g_mix"] = _mix_proj_bwd(dz, wm["wz"], x1, _after(p["g_mix"], tok), dx2)

    dgu1 = _ffn_bwd_act("ffn1_bwd_act", dx1b, h1, wup1, wdn1)
    dwup1, dwdn1 = _ffn_dw("ffn1", h1, dgu1, a1, dx1b)
    tok = emit("ffn1", {"wup1": dwup1, "wdn1": dwdn1})
    dx0, gs["g_ffn1"] = _ffn_dx("ffn1_dx", dgu1, wup1, x, _after(p["g_ffn1"], tok), dx1)
    return sq, dx0, gs


MESH = pl.DeviceIdType.MESH
HBM_SPEC = pl.BlockSpec(memory_space=pltpu.HBM)
N_PEER = N_DEV - 1


def _place():
    return lax.axis_index("x"), lax.axis_index("y"), lax.axis_index("c")


def _slot(px, py, pc):
    return 4 * px + 2 * py + pc


SEM_SPEC = pl.BlockSpec(memory_space=pltpu.SEMAPHORE)
ANY_SPEC = pl.BlockSpec(memory_space=pl.ANY)
DATAFLOW = pltpu.SideEffectType.DATAFLOW_SIDE_EFFECTING


def _hbm(a):
    return pltpu.with_memory_space_constraint(a, pltpu.HBM)


def _peer(x, y, c, r):
    return (1 - x if r & 4 else x, 1 - y if r & 2 else y, 1 - c if r & 1 else c)


def _place_own(srcs, whole):
    my = _slot(*_place())
    lands = []
    for s in srcs:
        blk = s[None] if whole else lax.dynamic_slice_in_dim(s, my, 1, 0)
        shape = (N_DEV,) + s.shape if whole else s.shape
        lands.append(lax.dynamic_update_slice_in_dim(lax.empty(shape, s.dtype), blk, my, 0))
    return lands


def _copy_start(name, srcs, lands, whole):
    n = len(srcs)

    def body(*refs):
        src, land = refs[:n], refs[n:2 * n]
        send, recv = refs[2 * n:3 * n], refs[3 * n:4 * n]
        token = refs[6 * n]
        x, y, c = _place()
        my = _slot(x, y, c)
        for a in range(n):
            for r in range(1, N_DEV):
                p = _peer(x, y, c, r)
                pltpu.make_async_remote_copy(
                    src_ref=src[a] if whole else src[a].at[_slot(*p)], dst_ref=land[a].at[my],
                    send_sem=send[a].at[r - 1], recv_sem=recv[a].at[r - 1], device_id=p, device_id_type=MESH).start()
        token[...] = jnp.zeros_like(token)

    out = pl.pallas_call(
        body, name=name,
        out_shape=([pltpu.SemaphoreType.DMA((N_PEER,))] * (2 * n)
                   + [pltpu.HBM(s.shape, s.dtype) for s in srcs] + [pltpu.HBM(s.shape, s.dtype) for s in lands]
                   + [S((8, LANES), F32)]),
        in_specs=[HBM_SPEC] * (2 * n),
        out_specs=[SEM_SPEC] * (2 * n) + [HBM_SPEC] * (2 * n) + [pl.BlockSpec(memory_space=pltpu.VMEM)],
        input_output_aliases={i: 2 * n + i for i in range(2 * n)},
        compiler_params=pltpu.CompilerParams(has_side_effects=DATAFLOW),
    )(*[_hbm(s) for s in srcs], *[_hbm(s) for s in lands])
    return out[:n], out[n:2 * n], out[2 * n:3 * n], out[3 * n:4 * n], out[4 * n]


def _copy_wait(name, srcs, lands, send, recv, after, whole):
    n = len(srcs)

    def body(*refs):
        src, land = refs[:n], refs[n:2 * n]
        snd, rcv = refs[2 * n:3 * n], refs[3 * n:4 * n]
        x, y, c = _place()
        for a in range(n):
            for r in range(1, N_DEV):
                p = _peer(x, y, c, r)
                ps = _slot(*p)
                cp = pltpu.make_async_remote_copy(
                    src_ref=src[a] if whole else src[a].at[ps], dst_ref=land[a].at[ps],
                    send_sem=snd[a].at[r - 1], recv_sem=rcv[a].at[r - 1], device_id=p, device_id_type=MESH)
                cp.wait_send()
                cp.wait_recv()

    out = pl.pallas_call(
        body, name=name,
        out_shape=[pltpu.HBM(s.shape, s.dtype) for s in srcs] + [pltpu.HBM(s.shape, s.dtype) for s in lands],
        in_specs=[HBM_SPEC] * (2 * n) + [SEM_SPEC] * (2 * n) + [ANY_SPEC],
        out_specs=[HBM_SPEC] * (2 * n),
        input_output_aliases={i: i for i in range(2 * n)},
        compiler_params=pltpu.CompilerParams(has_side_effects=DATAFLOW),
    )(*srcs, *lands, *send, *recv, after)
    return out[n:]


def _adamw(w, g, m, v):
    m2 = ADAM_B1 * m + (1.0 - ADAM_B1) * g
    v2 = ADAM_B2 * v + (1.0 - ADAM_B2) * (g * g)
    m_hat = m2 / (1.0 - ADAM_B1 ** ADAM_STEP)
    v_hat = v2 / (1.0 - ADAM_B2 ** ADAM_STEP)
    delta = -ADAM_LR * (m_hat / (jnp.sqrt(v_hat) + ADAM_EPS) + ADAM_WD * w)
    return delta, m2, v2


def _adamw_big(name, slots, w, m, v):
    R, C = w.shape
    tr = next((t for t in (256, 352) if R % t == 0), R)

    def body(s_ref, w_ref, m_ref, v_ref, g_ref, d_ref, m2_ref, v2_ref):
        g = s_ref[0].astype(F32)
        for k in range(1, N_DEV):
            g = g + s_ref[k].astype(F32)
        d, m2, v2 = _adamw(w_ref[...], g, m_ref[...], v_ref[...])
        g_ref[...] = g
        d_ref[...] = d
        m2_ref[...] = m2
        v2_ref[...] = v2

    row = pl.BlockSpec((tr, C), lambda i: (i, 0))
    return pl.pallas_call(
        body, name=name, grid=(R // tr,),
        in_specs=[pl.BlockSpec((N_DEV, tr, C), lambda i: (0, i, 0)), row, row, row],
        out_specs=[row] * 4, out_shape=[S((R, C), F32)] * 4,
        compiler_params=_cp(1))(slots, w, m, v)


TINY_ROWS = (("b_s", 8), ("g_ffn1", 8), ("g_mix", 8), ("g_ca", 8), ("g_mem", 8), ("g_ffn2", 8), ("g_sgu", 4),
             ("g_fox_o", 4), ("g_gmlp_o", 4), ("g_cq", 2), ("g_ck", 2), ("g_q", 1), ("g_k", 1), ("b_f", 1))
TINY_P = 72


def _pack_tiny(d):
    rows = []
    for name, r in TINY_ROWS:
        flat = d[name].reshape(-1)
        rows.append(jnp.pad(flat, (0, r * LANES - flat.shape[0])).reshape(r, LANES))
    used = sum(r for _, r in TINY_ROWS)
    rows.append(jnp.zeros((TINY_P - used, LANES), F32))
    return jnp.concatenate(rows, axis=0)


def _unpack_tiny(packed, shapes):
    out, at = {}, 0
    for name, r in TINY_ROWS:
        shape = shapes[name]
        size = 1
        for s in shape:
            size *= s
        out[name] = packed[at:at + r].reshape(-1)[:size].reshape(shape)
        at += r
    return out


WEIGHTS =('g_ffn1', 'w_ffn1_in', 'w_ffn1_out', 'g_mix', 'w_in', 'b_f', 'g_q', 'g_k', 'g_sgu', 'w_s', 'b_s',
           'g_fox_o', 'g_gmlp_o', 'w_out', 'g_ca', 'g_mem', 'w_cq', 'w_ckv', 'g_cq', 'g_ck', 'w_co', 'g_ffn2',
           'w_ffn2_in', 'w_ffn2_out')
BIG = ('w_ffn1_in', 'w_ffn1_out', 'w_in', 'w_out', 'w_cq', 'w_ckv', 'w_co', 'w_ffn2_in', 'w_ffn2_out')
TRANSPOSED = ('w_ffn1_in', 'w_in', 'w_ffn2_in')
GATHER_GROUPS = {"ffn1_up": ("w_ffn1_in",), "ffn1_dn": ("w_ffn1_out",), "mix": ("w_in", "w_out"),
                 "ca": ("w_cq", "w_ckv", "w_co"), "ffn2": ("w_ffn2_in", "w_ffn2_out")}
GATHER_STAGES = ((None, ("w_ffn1_in", "w_ffn1_out")),
                 ("ffn1_up", ("w_in", "w_out", "w_cq", "w_ckv", "w_co", "w_ffn2_in", "w_ffn2_out")))
QKV_W = 3 * FOX_W
UV_OFF = QKV_W + FOX_HEADS


def kernel(x, mem, g_ffn1, w_ffn1_in, w_ffn1_out, g_mix, w_in, b_f, g_q, g_k, g_sgu, w_s, b_s, g_fox_o, g_gmlp_o, w_out, g_ca, g_mem, w_cq, w_ckv, g_cq, g_ck, w_co, g_ffn2, w_ffn2_in, w_ffn2_out, loss_target, m_g_ffn1, m_w_ffn1_in, m_w_ffn1_out, m_g_mix, m_w_in, m_b_f, m_g_q, m_g_k, m_g_sgu, m_w_s, m_b_s, m_g_fox_o, m_g_gmlp_o, m_w_out, m_g_ca, m_g_mem, m_w_cq, m_w_ckv, m_g_cq, m_g_ck, m_w_co, m_g_ffn2, m_w_ffn2_in, m_w_ffn2_out, v_g_ffn1, v_w_ffn1_in, v_w_ffn1_out, v_g_mix, v_w_in, v_b_f, v_g_q, v_g_k, v_g_sgu, v_w_s, v_b_s, v_g_fox_o, v_g_gmlp_o, v_w_out, v_g_ca, v_g_mem, v_w_cq, v_w_ckv, v_g_cq, v_g_ck, v_w_co, v_g_ffn2, v_w_ffn2_in, v_w_ffn2_out):
    args = dict(locals())
    w = {n: args[n] for n in WEIGHTS}
    mo = {n: args["m_" + n] for n in WEIGHTS}
    vo = {n: args["v_" + n] for n in WEIGHTS}
    D = D_MODEL

    def local(n, a):
        return a[0].T if n in TRANSPOSED else a[0]

    shards = {n: local(n, w[n]).astype(BF) for n in BIG}
    fb = shards["w_ffn1_in"].shape[0]
    handles = {}

    def start_gather(stage, names, arrays):
        snd, rcv, src, land, _ = _copy_start("gather_start_%d" % stage, arrays, _place_own(arrays, True), True)
        for i, n in enumerate(names):
            handles[n] = (src[i], land[i], snd[i], rcv[i])

    start_gather(0, GATHER_STAGES[0][1], [shards[n] for n in GATHER_STAGES[0][1]])

    def weights(group, after):
        names = GATHER_GROUPS[group]
        hs = [handles[n] for n in names]
        got = _copy_wait("gather_wait_" + group, [h[0] for h in hs], [h[1] for h in hs], [h[2] for h in hs],
                         [h[3] for h in hs], after, True)
        for stage, (trigger, members) in enumerate(GATHER_STAGES):
            if trigger == group:
                held = lax.optimization_barrier((tuple(shards[n] for n in members), got[0]))[0]
                start_gather(stage, members, list(held))
        got = dict(zip(names, got))
        if group == "ffn1_up":
            return {"wup1": got["w_ffn1_in"].reshape(2, N_FFN_BLK, fb, D)}
        if group == "ffn1_dn":
            return {"wdn1": got["w_ffn1_out"].reshape(N_FFN_BLK, fb, D)}
        if group == "mix":
            full = got["w_in"].reshape(-1, D)
            wz = jnp.concatenate([full[:QKV_W], full[UV_OFF:], full[QKV_W:UV_OFF],
                                  jnp.zeros((LANES - FOX_HEADS, D), BF)], axis=0)
            return {"wz": wz, "wout": got["w_out"].reshape(D, D)}
        if group == "ca":
            return {"wcq": got["w_cq"].reshape(D, D), "wco": got["w_co"].reshape(D, D), "wckv": got["w_ckv"]}
        return {"wup2": got["w_ffn2_in"].reshape(2, N_FFN_BLK, fb, D),
                "wdn2": got["w_ffn2_out"].reshape(N_FFN_BLK, fb, D)}

    flying = {}

    def emit(group, g):
        if group == "ffn2":
            parts = {"w_ffn2_in": g["wup2"], "w_ffn2_out": g["wdn2"].reshape(N_DEV, -1, D)}
        elif group == "ffn1":
            parts = {"w_ffn1_in": g["wup1"], "w_ffn1_out": g["wdn1"].reshape(N_DEV, -1, D)}
        else:
            gz = g["wz"]
            g_in = jnp.concatenate([gz[:QKV_W], gz[Z_F:Z_F + FOX_HEADS], gz[QKV_W:Z_F]], axis=0)
            parts = {"w_in": g_in.reshape(N_DEV, -1, D).astype(BF),
                     "w_out": g["wout"].reshape(N_DEV, -1, D), "w_cq": g["wcq"].reshape(N_DEV, -1, D),
                     "w_co": g["wco"].reshape(N_DEV, -1, D), "w_ckv": g["wckv"]}
        names = list(parts)
        srcs = [parts[n] for n in names]
        *copies, token = _copy_start("exchange_start_" + group, srcs, _place_own(srcs, False), False)
        flying[group] = (names, copies)
        return token

    tiny_names = [n for n, _ in TINY_ROWS]
    small = {n: (w[n][0] if n == "b_s" else w[n]) for n in tiny_names}
    small["w_s"] = w["w_s"][0]

    sq, dx0, gs = _local_step(x[0], mem[0], loss_target[0], small, weights, emit)
    loss = lax.psum(sq[0, 0], ("x", "y", "c")) * (0.5 / D)

    sm_parts = [gs["w_s"].reshape(-1, LANES), _pack_tiny(gs)]
    sm_snd, sm_rcv, sm_src, sm_land, sm_token = _copy_start("small_start", sm_parts, _place_own(sm_parts, True), True)

    grad, delta, new_m, new_v = {}, {}, {}, {}

    def update(group, after):
        names, (snd, rcv, srcs, lands) = flying[group]
        slots = _copy_wait("exchange_wait_" + group, srcs, lands, snd, rcv, after, False)
        for n, sl in zip(names, slots):
            g, d, m2, v2 = _adamw_big("adamw_" + n, sl, local(n, w[n]), local(n, mo[n]), local(n, vo[n]))
            grad[n], delta[n], new_m[n], new_v[n] = (
                (t.T if n in TRANSPOSED else t).reshape(w[n].shape) for t in (g, d, m2, v2))
        return d

    last = update("ffn2", sm_token)
    last = update("mid", last)
    last = update("ffn1", last)
    ws_all, tiny_all = _copy_wait("small_wait", sm_src, sm_land, sm_snd, sm_rcv, last, True)
    ws_shape = w["w_s"].shape
    for store, t in zip((grad, delta, new_m, new_v), _adamw_big(
            "adamw_w_s", ws_all, *[a["w_s"].reshape(-1, LANES) for a in (w, mo, vo)])):
        store["w_s"] = t.reshape(ws_shape)
    shapes = {n: w[n].shape for n in tiny_names}
    for store, t in zip((grad, delta, new_m, new_v), _adamw_big(
            "adamw_tiny", tiny_all, *[_pack_tiny({n: a[n] for n in tiny_names}) for a in (w, mo, vo)])):
        store.update(_unpack_tiny(t, shapes))

    return (loss, dx0[None], *[grad[n] for n in WEIGHTS], *[delta[n] for n in WEIGHTS],
            *[new_m[n] for n in WEIGHTS], *[new_v[n] for n in WEIGHTS])
```

```python
import functools

import jax
import jax.numpy as jnp
from jax import lax
from jax.experimental import pallas as pl
from jax.experimental.pallas import tpu as pltpu

F32 = jnp.float32
BF = jnp.bfloat16
S = jax.ShapeDtypeStruct

N_DEV = 8
D_MODEL = 1024
FOX_HEADS, FOX_HD = 8, 64
FOX_W = 512
GMLP_G, GMLP_GD = 8, 64
GMLP_W = 512
CHUNK = 128
CA_HEADS, CA_HD = 4, 256
N_FFN_BLK = 4
ZW = 2688
Z_Q, Z_K, Z_V, Z_U, Z_G, Z_F = 0, 512, 1024, 1536, 2048, 2560
EPS = 1e-6
NEG = -1e30
LANES = 128

ADAM_LR, ADAM_B1, ADAM_B2, ADAM_EPS, ADAM_WD, ADAM_STEP = 0.001, 0.9, 0.999, 1e-08, 0.01, 10

VMEM_LIMIT = 52 * 2 ** 20


def _cp(n_axes):
    return pltpu.CompilerParams(dimension_semantics=("arbitrary",) * n_axes, vmem_limit_bytes=VMEM_LIMIT)


def _nn(a, b):
    return jnp.dot(a, b, preferred_element_type=F32)


def _nt(a, b):
    return lax.dot_general(a, b, (((1,), (1,)), ((), ())), preferred_element_type=F32)


def _tn(a, b):
    return lax.dot_general(a, b, (((0,), (0,)), ((), ())), preferred_element_type=F32)


def _hi(a, b):
    return jnp.dot(a, b, precision=lax.Precision.HIGHEST, preferred_element_type=F32)


def _rstd(x):
    return lax.rsqrt(jnp.mean(x * x, axis=-1, keepdims=True) + EPS)


def _norm_bwd(dy, x, g):
    r = _rstd(x)
    xh = x * r
    dxh = dy * g
    dx = r * (dxh - xh * jnp.mean(dxh * xh, axis=-1, keepdims=True))
    return dx, dy * xh


def _acc_rows(ref, first, val):
    srow = jnp.sum(val, axis=0, keepdims=True)

    @pl.when(first)
    def _():
        ref[...] = srow

    @pl.when(jnp.logical_not(first))
    def _():
        ref[...] += srow


def _gelu(x):
    c = 0.7978845608028654
    return 0.5 * x * (1.0 + jnp.tanh(c * (x + 0.044715 * x * x * x)))


def _gelu_grad(x):
    c = 0.7978845608028654
    t = jnp.tanh(c * (x + 0.044715 * x * x * x))
    return 0.5 * (1.0 + t) + 0.5 * x * (1.0 - t * t) * c * (1.0 + 3 * 0.044715 * x * x)


def _tile(n, pref):
    return pref if n % pref == 0 else n


def _ffn_up(name, x, g, wup):
    T, D = x.shape
    FB = wup.shape[-2]
    tm = _tile(T, 512)

    def body(x_ref, g_ref, w_ref, a_ref, h_ref):
        @pl.when(pl.program_id(1) == 0)
        def _():
            xf = x_ref[...]
            h_ref[...] = (xf * _rstd(xf) * g_ref[...]).astype(BF)

        hb = h_ref[...]
        gg = _nt(hb, w_ref[0])
        uu = _nt(hb, w_ref[1])
        a_ref[...] = (gg * jax.nn.sigmoid(gg) * uu).astype(BF)

    return pl.pallas_call(
        body, name=name, grid=(T // tm, N_FFN_BLK),
        in_specs=[pl.BlockSpec((tm, D), lambda i, j: (i, 0)),
                  pl.BlockSpec((1, D), lambda i, j: (0, 0)),
                  pl.BlockSpec((2, None, FB, D), lambda i, j: (0, j, 0, 0))],
        out_specs=[pl.BlockSpec((None, tm, FB), lambda i, j: (j, i, 0)),
                   pl.BlockSpec((tm, D), lambda i, j: (i, 0))],
        out_shape=[S((N_FFN_BLK, T, FB), BF), S((T, D), BF)],
        compiler_params=_cp(2))(x, g, wup)


def _ffn_down(name, a, wdn, x):
    _, T, FB = a.shape
    D = x.shape[1]
    tm = _tile(T, 512)

    def body(a_ref, w_ref, x_ref, o_ref):
        j = pl.program_id(1)
        p = 0.5 * _nn(a_ref[...], w_ref[...])

        @pl.when(j == 0)
        def _():
            o_ref[...] = x_ref[...] + p

        @pl.when(j > 0)
        def _():
            o_ref[...] += p

    return pl.pallas_call(
        body, name=name, grid=(T // tm, N_FFN_BLK),
        in_specs=[pl.BlockSpec((None, tm, FB), lambda i, j: (j, i, 0)),
                  pl.BlockSpec((None, FB, D), lambda i, j: (j, 0, 0)),
                  pl.BlockSpec((tm, D), lambda i, j: (i, 0))],
        out_specs=pl.BlockSpec((tm, D), lambda i, j: (i, 0)),
        out_shape=S((T, D), F32),
        compiler_params=_cp(2))(a, wdn, x)


def _ffn_down_loss(name, a, wdn, x, target):
    _, T, FB = a.shape
    D = x.shape[1]
    tm = _tile(T, 512)

    def body(a_ref, w_ref, x_ref, t_ref, d_ref, db_ref, loss_ref, acc_ref):
        i, j = pl.program_id(0), pl.program_id(1)
        p = 0.5 * _nn(a_ref[...], w_ref[...])

        @pl.when(j == 0)
        def _():
            acc_ref[...] = x_ref[...] + p

        @pl.when(j > 0)
        def _():
            acc_ref[...] += p

        @pl.when(j == N_FFN_BLK - 1)
        def _():
            diff = acc_ref[...] - t_ref[...]
            dy = diff * (1.0 / D)
            d_ref[...] = dy
            db_ref[...] = dy.astype(BF)
            sq = jnp.zeros((8, LANES), F32) + jnp.sum(diff * diff)

            @pl.when(i == 0)
            def _():
                loss_ref[...] = sq

            @pl.when(i > 0)
            def _():
                loss_ref[...] += sq

    return pl.pallas_call(
        body, name=name, grid=(T // tm, N_FFN_BLK),
        in_specs=[pl.BlockSpec((None, tm, FB), lambda i, j: (j, i, 0)),
                  pl.BlockSpec((None, FB, D), lambda i, j: (j, 0, 0)),
                  pl.BlockSpec((tm, D), lambda i, j: (i, 0)),
                  pl.BlockSpec((tm, D), lambda i, j: (i, 0))],
        out_specs=[pl.BlockSpec((tm, D), lambda i, j: (i, 0)),
                   pl.BlockSpec((tm, D), lambda i, j: (i, 0)),
                   pl.BlockSpec((8, LANES), lambda i, j: (0, 0))],
        out_shape=[S((T, D), F32), S((T, D), BF), S((8, LANES), F32)],
        scratch_shapes=[pltpu.VMEM((tm, D), F32)],
        compiler_params=_cp(2))(a, wdn, x, target)


def _ffn_bwd_act(name, dyb, h, wup, wdn):
    T, D = h.shape
    FB = wup.shape[-2]
    tm = _tile(T, 512)

    def body(d_ref, h_ref, wu_ref, wd_ref, o_ref):
        da = 0.5 * _nt(d_ref[...], wd_ref[...])
        hb = h_ref[...]
        gg = _nt(hb, wu_ref[0])
        uu = _nt(hb, wu_ref[1])
        sg = jax.nn.sigmoid(gg)
        o_ref[0] = (da * uu * (sg * (1.0 + gg * (1.0 - sg)))).astype(BF)
        o_ref[1] = (da * (gg * sg)).astype(BF)

    return pl.pallas_call(
        body, name=name, grid=(T // tm, N_FFN_BLK),
        in_specs=[pl.BlockSpec((tm, D), lambda i, j: (i, 0)),
                  pl.BlockSpec((tm, D), lambda i, j: (i, 0)),
                  pl.BlockSpec((2, None, FB, D), lambda i, j: (0, j, 0, 0)),
                  pl.BlockSpec((None, FB, D), lambda i, j: (j, 0, 0))],
        out_specs=pl.BlockSpec((2, None, tm, FB), lambda i, j: (0, j, i, 0)),
        out_shape=S((2, N_FFN_BLK, T, FB), BF),
        compiler_params=_cp(2))(dyb, h, wup, wdn)


def _ffn_dx(name, dgu, wup, x, g, dy):
    T, D = x.shape
    FB = wup.shape[-2]
    tm = _tile(T, 512)

    def body(d_ref, w_ref, x_ref, g_ref, dy_ref, dx_ref, dg_ref, acc_ref):
        i, j = pl.program_id(0), pl.program_id(1)
        p = _nn(d_ref[0], w_ref[0]) + _nn(d_ref[1], w_ref[1])

        @pl.when(j == 0)
        def _():
            acc_ref[...] = p

        @pl.when(j > 0)
        def _():
            acc_ref[...] += p

        @pl.when(j == N_FFN_BLK - 1)
        def _():
            dx, dgr = _norm_bwd(acc_ref[...], x_ref[...], g_ref[...])
            dx_ref[...] = dx + dy_ref[...]
            _acc_rows(dg_ref, i == 0, dgr)

    return pl.pallas_call(
        body, name=name, grid=(T // tm, N_FFN_BLK),
        in_specs=[pl.BlockSpec((2, None, tm, FB), lambda i, j: (0, j, i, 0)),
                  pl.BlockSpec((2, None, FB, D), lambda i, j: (0, j, 0, 0)),
                  pl.BlockSpec((tm, D), lambda i, j: (i, 0)),
                  pl.BlockSpec((1, D), lambda i, j: (0, 0)),
                  pl.BlockSpec((tm, D), lambda i, j: (i, 0))],
        out_specs=[pl.BlockSpec((tm, D), lambda i, j: (i, 0)),
                   pl.BlockSpec((1, D), lambda i, j: (0, 0))],
        out_shape=[S((T, D), F32), S((1, D), F32)],
        scratch_shapes=[pltpu.VMEM((tm, D), F32)],
        compiler_params=_cp(2))(dgu, wup, x, g, dy)


def _tn_matmul(name, a, a_spec, b, b_spec, out_shape, out_spec, grid, acc_shape, scale=1.0):
    nk = grid[1]

    def body(a_ref, b_ref, o_ref, acc_ref):
        k = pl.program_id(1)
        p = _tn(a_ref[...], b_ref[...])

        @pl.when(k == 0)
        def _():
            acc_ref[...] = p

        @pl.when(k > 0)
        def _():
            acc_ref[...] += p

        @pl.when(k == nk - 1)
        def _():
            o_ref[...] = (acc_ref[...] * scale).astype(o_ref.dtype)

    return pl.pallas_call(
        body, name=name, grid=grid, in_specs=[a_spec, b_spec], out_specs=out_spec, out_shape=out_shape,
        scratch_shapes=[pltpu.VMEM(acc_shape, F32)], compiler_params=_cp(2))(a, b)


def _ffn_dw(name, h, dgu, a, dyb):
    T, D = h.shape
    FB = a.shape[-1]
    tk = _tile(T, 512)
    nk = T // tk
    dgu8 = dgu.reshape(2 * N_FFN_BLK, T, FB)
    dwup = _tn_matmul(
        name + "_dwup", dgu8, pl.BlockSpec((None, tk, FB), lambda j, k: (j, k, 0)),
        h, pl.BlockSpec((tk, D), lambda j, k: (k, 0)),
        S((2 * N_FFN_BLK, FB, D), BF), pl.BlockSpec((None, FB, D), lambda j, k: (j, 0, 0)),
        (2 * N_FFN_BLK, nk), (FB, D))
    dwdn = _tn_matmul(
        name + "_dwdn", a, pl.BlockSpec((None, tk, FB), lambda j, k: (j, k, 0)),
        dyb, pl.BlockSpec((tk, D), lambda j, k: (k, 0)),
        S((N_FFN_BLK, FB, D), BF), pl.BlockSpec((None, FB, D), lambda j, k: (j, 0, 0)),
        (N_FFN_BLK, nk), (FB, D), scale=0.5)
    return dwup, dwdn


def _mix_proj(x, g, wz):
    T, D = x.shape
    tm = _tile(T, 256)

    def body(x_ref, g_ref, w_ref, z_ref, h_ref):
        xf = x_ref[...]
        hb = (xf * _rstd(xf) * g_ref[...]).astype(BF)
        h_ref[...] = hb
        z_ref[...] = _nt(hb, w_ref[...])

    return pl.pallas_call(
        body, name="mix_proj", grid=(T // tm,),
        in_specs=[pl.BlockSpec((tm, D), lambda i: (i, 0)),
                  pl.BlockSpec((1, D), lambda i: (0, 0)),
                  pl.BlockSpec((ZW, D), lambda i: (0, 0))],
        out_specs=[pl.BlockSpec((tm, ZW), lambda i: (i, 0)),
                   pl.BlockSpec((tm, D), lambda i: (i, 0))],
        out_shape=[S((T, ZW), F32), S((T, D), BF)],
        compiler_params=_cp(1))(x, g, wz)


def _tri(n, lower):
    r = lax.broadcasted_iota(jnp.int32, (n, n), 0)
    c = lax.broadcasted_iota(jnp.int32, (n, n), 1)
    return (r >= c) if lower else (r <= c)


def _spatial_mix(vgn_b, ws_ref, bst, tm):
    tril = _tri(CHUNK, True)
    wms = [jnp.where(tril, ws_ref[g], 0.0).astype(BF) for g in range(GMLP_G)]
    rows = []
    for c in range(tm // CHUNK):
        cols = []
        for g in range(GMLP_G):
            vs = vgn_b[c * CHUNK:(c + 1) * CHUNK, g * GMLP_GD:(g + 1) * GMLP_GD]
            cols.append(_nn(wms[g], vs) + bst[:, g:g + 1])
        rows.append(jnp.concatenate(cols, axis=1))
    return jnp.concatenate(rows, axis=0), wms


HB = 128
AUG_W = FOX_HEADS * HB
COL_A, COL_B, COL_C = 64, 67, 70


def _spread_matrix():
    r = jnp.arange(FOX_W)
    return (jnp.arange(AUG_W)[None, :] == ((r // FOX_HD) * HB + r % FOX_HD)[:, None]).astype(BF)


def _piece_matrix(col):
    r = jnp.arange(LANES)
    dst = jnp.where(r < 3 * FOX_HEADS, (r % FOX_HEADS) * HB + col + r // FOX_HEADS, -1)
    return (jnp.arange(AUG_W)[None, :] == dst[:, None]).astype(BF)


def _ones_row(cols):
    c = jnp.arange(AUG_W) % HB
    hit = functools.reduce(jnp.logical_or, [(c >= a) & (c < a + 3) for a in cols])
    return hit.astype(F32)[None, :]


def _pieces(x):
    lane = lax.broadcasted_iota(jnp.int32, x.shape, 1)
    x = jnp.where(lane < FOX_HEADS, x, 0.0)
    hi = x.astype(BF).astype(F32)
    r1 = x - hi
    mid = r1.astype(BF).astype(F32)
    lo = (r1 - mid).astype(BF).astype(F32)
    return (hi + pltpu.roll(mid, FOX_HEADS, 1) + pltpu.roll(lo, 2 * FOX_HEADS, 1)).astype(BF)


def _mix_prep(z, bf128, g_q, g_k, g_sgu, w_s, b_st, g_go):
    T = z.shape[0]
    tm = _tile(T, 256)
    spread, pc_q, pc_k = _spread_matrix(), _piece_matrix(COL_A), _piece_matrix(COL_B)
    one_q, one_k, one_v = _ones_row([COL_B]), _ones_row([COL_A, COL_C]), _ones_row([COL_A])

    def body(z_ref, bf_ref, gq_ref, gk_ref, gs_ref, ws_ref, bst_ref, go_ref, sp_ref, pq_ref, pk_ref, oq_ref, ok_ref,
             ov_ref, q_ref, k_ref, v_ref, y_ref, carry_ref, qn_sc, kn_sc):
        i = pl.program_id(0)

        @pl.when(i == 0)
        def _():
            carry_ref[...] = jnp.zeros_like(carry_ref)

        for h in range(FOX_HEADS):
            hs = slice(h * FOX_HD, (h + 1) * FOX_HD)
            qh = z_ref[:, Z_Q + h * FOX_HD:Z_Q + (h + 1) * FOX_HD]
            kh = z_ref[:, Z_K + h * FOX_HD:Z_K + (h + 1) * FOX_HD]
            qn_sc[:, hs] = (qh * _rstd(qh) * gq_ref[...] * 0.125).astype(BF)
            kn_sc[:, hs] = (kh * _rstd(kh) * gk_ref[...]).astype(BF)

        fl = z_ref[:, Z_F:Z_F + LANES] + bf_ref[...]
        logf = jnp.minimum(fl, 0.0) - jnp.log1p(jnp.exp(-jnp.abs(fl)))
        csum = _hi(_tri(tm, True).astype(F32), logf) + carry_ref[...]
        carry_ref[...] = csum[tm - 1:tm, :]
        sp = sp_ref[...]
        q_ref[...] = (_nn(qn_sc[...], sp) + _nn(_pieces(csum), pq_ref[...]) + oq_ref[...]).astype(BF)
        k_ref[...] = (_nn(kn_sc[...], sp) + _nn(_pieces(-csum), pk_ref[...]) + ok_ref[...]).astype(BF)
        v_ref[...] = (_nn(z_ref[:, Z_V:Z_V + FOX_W].astype(BF), sp) + ov_ref[...]).astype(BF)

        u = _gelu(z_ref[:, Z_U:Z_U + GMLP_W])
        vg = _gelu(z_ref[:, Z_G:Z_G + GMLP_W])
        vgn = (vg * _rstd(vg) * gs_ref[...]).astype(BF)
        mixed, _ = _spatial_mix(vgn, ws_ref, bst_ref[...], tm)
        sgu = u * mixed
        y_ref[...] = (sgu * _rstd(sgu) * go_ref[...]).astype(BF)

    row = lambda i: (i, 0)
    fix2 = lambda i: (0, 0)
    return pl.pallas_call(
        body, name="mix_prep", grid=(T // tm,),
        in_specs=[pl.BlockSpec((tm, ZW), row),
                  pl.BlockSpec((1, LANES), fix2), pl.BlockSpec((1, FOX_HD), fix2), pl.BlockSpec((1, FOX_HD), fix2),
                  pl.BlockSpec((1, GMLP_W), fix2), pl.BlockSpec((GMLP_G, CHUNK, CHUNK), lambda i: (0, 0, 0)),
                  pl.BlockSpec((CHUNK, GMLP_G), fix2), pl.BlockSpec((1, GMLP_W), fix2),
                  pl.BlockSpec((FOX_W, AUG_W), fix2), pl.BlockSpec((LANES, AUG_W), fix2),
                  pl.BlockSpec((LANES, AUG_W), fix2), pl.BlockSpec((1, AUG_W), fix2), pl.BlockSpec((1, AUG_W), fix2),
                  pl.BlockSpec((1, AUG_W), fix2)],
        out_specs=[pl.BlockSpec((tm, AUG_W), row), pl.BlockSpec((tm, AUG_W), row), pl.BlockSpec((tm, AUG_W), row),
                   pl.BlockSpec((tm, GMLP_W), row)],
        out_shape=[S((T, AUG_W), BF), S((T, AUG_W), BF), S((T, AUG_W), BF), S((T, GMLP_W), BF)],
        scratch_shapes=[pltpu.VMEM((1, LANES), F32), pltpu.VMEM((tm, FOX_W), BF), pltpu.VMEM((tm, FOX_W), BF)],
        compiler_params=_cp(1))(z, bf128, g_q, g_k, g_sgu, w_s, b_st, g_go, spread, pc_q, pc_k, one_q, one_k, one_v)


def _fox_fwd(q, k, v):
    T = q.shape[0]
    tq = _tile(T, 512)
    nq = T // tq

    def body(q_ref, k_ref, v_ref, o_ref, lse_ref, m_sc, acc_sc):
        i, j = pl.program_id(0), pl.program_id(1)

        @pl.when(j == 0)
        def _():
            m_sc[...] = jnp.full(m_sc.shape, NEG, F32)
            acc_sc[...] = jnp.zeros_like(acc_sc)

        def step(masked):
            mask = _tri(tq, True) if masked else None
            for h in range(FOX_HEADS):
                hb = slice(h * HB, (h + 1) * HB)
                s = _nt(q_ref[:, hb], k_ref[:, hb])
                if masked:
                    s = jnp.where(mask, s, NEG)
                m_prev = m_sc[h]
                m_new = jnp.maximum(m_prev, jnp.max(s, axis=1, keepdims=True))
                p = jnp.exp(s - m_new).astype(BF)
                acc_sc[:, hb] = jnp.exp(m_prev - m_new) * acc_sc[:, hb] + _nn(p, v_ref[:, hb])
                m_sc[h] = m_new

        @pl.when(j < i)
        def _():
            step(False)

        @pl.when(j == i)
        def _():
            step(True)
            lse_ref[...] = jnp.zeros_like(lse_ref)
            for h in range(FOX_HEADS):
                l = acc_sc[:, h * HB + COL_A:h * HB + COL_A + 1]
                o_ref[:, h * FOX_HD:(h + 1) * FOX_HD] = acc_sc[:, h * HB:h * HB + FOX_HD] / l
                lse_ref[:, h:h + 1] = m_sc[h] + jnp.log(l)

    qi = lambda i, j: (i, 0)
    kj = lambda i, j: (jnp.minimum(i, j), 0)
    return pl.pallas_call(
        body, name="fox_fwd", grid=(nq, nq),
        in_specs=[pl.BlockSpec((tq, AUG_W), qi), pl.BlockSpec((tq, AUG_W), kj), pl.BlockSpec((tq, AUG_W), kj)],
        out_specs=[pl.BlockSpec((tq, FOX_W), qi), pl.BlockSpec((tq, LANES), qi)],
        out_shape=[S((T, FOX_W), F32), S((T, LANES), F32)],
        scratch_shapes=[pltpu.VMEM((FOX_HEADS, tq, 1), F32), pltpu.VMEM((tq, AUG_W), F32)],
        compiler_params=_cp(2))(q, k, v)


def _fox_bwd(q, k, v, dob):
    T = q.shape[0]
    tq = _tile(T, 512)
    nq = T // tq
    half = AUG_W // 2
    hpg = FOX_HEADS // 2

    def body(q_ref, k_ref, v_ref, do_ref, dq_ref, dk_ref, dv_ref, dq_sc):
        j, i = pl.program_id(1), pl.program_id(2)

        @pl.when(jnp.logical_and(i == 0, j == 0))
        def _():
            dq_sc[...] = jnp.zeros_like(dq_sc)

        @pl.when(i == 0)
        def _():
            dk_ref[...] = jnp.zeros_like(dk_ref)
            dv_ref[...] = jnp.zeros_like(dv_ref)

        def step(masked):
            rows = pl.ds(pl.multiple_of(i * tq, tq), tq)
            mask = _tri(tq, True) if masked else None
            for h in range(hpg):
                hb = slice(h * HB, (h + 1) * HB)
                qh, kh, vh, doh = q_ref[:, hb], k_ref[:, hb], v_ref[:, hb], do_ref[:, hb]
                s = _nt(qh, kh)
                if masked:
                    s = jnp.where(mask, s, NEG)
                p = jnp.exp(s)
                dsb = (p * _nt(doh, vh)).astype(BF)
                dv_ref[:, hb] += _tn(p.astype(BF), doh)
                dk_ref[:, hb] += _tn(dsb, qh)
                dq_sc[rows, hb] += _nn(dsb, kh)

        @pl.when(i > j)
        def _():
            step(False)

        @pl.when(i == j)
        def _():
            step(True)
            dq_ref[...] = dq_sc[pl.ds(pl.multiple_of(j * tq, tq), tq), :]

    qi = lambda g, j, i: (jnp.maximum(i, j), g)
    kj = lambda g, j, i: (j, g)
    return pl.pallas_call(
        body, name="fox_bwd", grid=(2, nq, nq),
        in_specs=[pl.BlockSpec((tq, half), qi), pl.BlockSpec((tq, half), kj), pl.BlockSpec((tq, half), kj),
                  pl.BlockSpec((tq, half), qi)],
        out_specs=[pl.BlockSpec((tq, half), kj), pl.BlockSpec((tq, half), kj), pl.BlockSpec((tq, half), kj)],
        out_shape=[S((T, AUG_W), F32), S((T, AUG_W), F32), S((T, AUG_W), F32)],
        scratch_shapes=[pltpu.VMEM((T, half), F32)],
        compiler_params=_cp(3))(q, k, v, dob)


def _mix_out(attn, yg, g_fo, wout, x):
    T, D = x.shape
    tm = _tile(T, 512)

    def body(a_ref, y_ref, g_ref, w_ref, x_ref, o_ref):
        at = a_ref[...]
        yf = (at * _rstd(at) * g_ref[...]).astype(BF)
        o_ref[...] = x_ref[...] + _nn(yf, w_ref[:FOX_W, :]) + _nn(y_ref[...], w_ref[FOX_W:, :])

    row = lambda i: (i, 0)
    return pl.pallas_call(
        body, name="mix_out", grid=(T // tm,),
        in_specs=[pl.BlockSpec((tm, FOX_W), row), pl.BlockSpec((tm, GMLP_W), row),
                  pl.BlockSpec((1, FOX_W), lambda i: (0, 0)), pl.BlockSpec((D, D), lambda i: (0, 0)),
                  pl.BlockSpec((tm, D), row)],
        out_specs=pl.BlockSpec((tm, D), row),
        out_shape=S((T, D), F32),
        compiler_params=_cp(1))(attn, yg, g_fo, wout, x)


def _mix_out_bwd(dx, attn, yg, g_fo, wout, qf, lse):
    T, D = dx.shape
    tm = _tile(T, 256)
    n = T // tm
    spread, pc_l, pc_d = _spread_matrix(), _piece_matrix(COL_C), _piece_matrix(COL_A)

    def body(dx_ref, a_ref, y_ref, g_ref, w_ref, qf_ref, lse_ref, sp_ref, pl_ref, pd_ref,
             qb_ref, dob_ref, dyg_ref, dw_ref, dg_ref, acc_ref, dsum_ref):
        i = pl.program_id(0)
        dxb = dx_ref[...].astype(BF)
        at = a_ref[...]
        yf = (at * _rstd(at) * g_ref[...]).astype(BF)
        dy = _nt(dxb, w_ref[...])
        p_top = _tn(yf, dxb)
        p_bot = _tn(y_ref[...], dxb)

        @pl.when(i == 0)
        def _():
            acc_ref[:FOX_W, :] = p_top
            acc_ref[FOX_W:, :] = p_bot

        @pl.when(i > 0)
        def _():
            acc_ref[:FOX_W, :] += p_top
            acc_ref[FOX_W:, :] += p_bot

        @pl.when(i == n - 1)
        def _():
            dw_ref[...] = acc_ref[...].astype(BF)

        dat, dgr = _norm_bwd(dy[:, :FOX_W], at, g_ref[...])
        _acc_rows(dg_ref, i == 0, dgr)
        dyg_ref[...] = dy[:, FOX_W:]
        prod = dat * at
        dsum_ref[...] = jnp.zeros_like(dsum_ref)
        for h in range(FOX_HEADS):
            dsum_ref[:, h:h + 1] = jnp.sum(prod[:, h * FOX_HD:(h + 1) * FOX_HD], axis=1, keepdims=True)
        dob_ref[...] = (_nn(dat.astype(BF), sp_ref[...]) + _nn(_pieces(-dsum_ref[...]), pd_ref[...])).astype(BF)
        qb_ref[...] = (qf_ref[...].astype(F32) + _nn(_pieces(-lse_ref[...]), pl_ref[...])).astype(BF)

    row = lambda i: (i, 0)
    fix = lambda i: (0, 0)
    return pl.pallas_call(
        body, name="mix_out_bwd", grid=(n,),
        in_specs=[pl.BlockSpec((tm, D), row), pl.BlockSpec((tm, FOX_W), row), pl.BlockSpec((tm, GMLP_W), row),
                  pl.BlockSpec((1, FOX_W), fix), pl.BlockSpec((D, D), fix), pl.BlockSpec((tm, AUG_W), row),
                  pl.BlockSpec((tm, LANES), row), pl.BlockSpec((FOX_W, AUG_W), fix), pl.BlockSpec((LANES, AUG_W), fix),
                  pl.BlockSpec((LANES, AUG_W), fix)],
        out_specs=[pl.BlockSpec((tm, AUG_W), row), pl.BlockSpec((tm, AUG_W), row), pl.BlockSpec((tm, GMLP_W), row),
                   pl.BlockSpec((D, D), fix), pl.BlockSpec((1, FOX_W), fix)],
        out_shape=[S((T, AUG_W), BF), S((T, AUG_W), BF), S((T, GMLP_W), F32), S((D, D), BF), S((1, FOX_W), F32)],
        scratch_shapes=[pltpu.VMEM((D, D), F32), pltpu.VMEM((tm, LANES), F32)],
        compiler_params=_cp(1))(dx, attn, yg, g_fo, wout, qf, lse, spread, pc_l, pc_d)


def _mix_prep_bwd(z, dq, dk, dv, dyg, bf128, g_q, g_k, g_sgu, w_s, b_st, g_go):
    T = z.shape[0]
    tm = _tile(T, 256)
    n = T // tm

    def body(z_ref, dq_ref, dk_ref, dv_ref, dyg_ref, bf_ref, gq_ref, gk_ref, gs_ref, ws_ref,
             bst_ref, go_ref, dz_ref, dgq_ref, dgk_ref, dgs_ref, dgo_ref, dws_ref, dbst_ref, dbf_ref, carry_ref):
        i = pl.program_id(0)
        first = i == 0

        @pl.when(first)
        def _():
            carry_ref[...] = jnp.zeros_like(carry_ref)

        lane = lax.broadcasted_iota(jnp.int32, (tm, LANES), 1)
        dc = jnp.zeros((tm, LANES), F32)
        gq_rows, gk_rows = [], []
        for h in range(FOX_HEADS):
            hp = slice(h * HB, h * HB + FOX_HD)
            dqh, gqr = _norm_bwd(dq_ref[:, hp] * 0.125, z_ref[:, Z_Q + h * FOX_HD:Z_Q + (h + 1) * FOX_HD], gq_ref[...])
            dkh, gkr = _norm_bwd(dk_ref[:, hp], z_ref[:, Z_K + h * FOX_HD:Z_K + (h + 1) * FOX_HD], gk_ref[...])
            dz_ref[:, Z_Q + h * FOX_HD:Z_Q + (h + 1) * FOX_HD] = dqh.astype(BF)
            dz_ref[:, Z_K + h * FOX_HD:Z_K + (h + 1) * FOX_HD] = dkh.astype(BF)
            dz_ref[:, Z_V + h * FOX_HD:Z_V + (h + 1) * FOX_HD] = dv_ref[:, hp].astype(BF)
            dch = dq_ref[:, h * HB + COL_A:h * HB + COL_A + 1] - dk_ref[:, h * HB + COL_B:h * HB + COL_B + 1]
            dc = jnp.where(lane == h, dch, dc)
            gq_rows.append(gqr)
            gk_rows.append(gkr)
        _acc_rows(dgq_ref, first, functools.reduce(lambda a, b: a + b, gq_rows))
        _acc_rows(dgk_ref, first, functools.reduce(lambda a, b: a + b, gk_rows))

        dlogf = _hi(_tri(tm, False).astype(F32), dc) + carry_ref[...]
        carry_ref[...] = dlogf[0:1, :]
        fl = z_ref[:, Z_F:Z_F + LANES] + bf_ref[...]
        lane = lax.broadcasted_iota(jnp.int32, (tm, LANES), 1)
        df = jnp.where(lane < FOX_HEADS, dlogf * jax.nn.sigmoid(-fl), 0.0)
        dz_ref[:, Z_F:Z_F + LANES] = df.astype(BF)
        _acc_rows(dbf_ref, first, df)

        u_pre = z_ref[:, Z_U:Z_U + GMLP_W]
        vg_pre = z_ref[:, Z_G:Z_G + GMLP_W]
        u = _gelu(u_pre)
        vg = _gelu(vg_pre)
        vgn = (vg * _rstd(vg) * gs_ref[...]).astype(BF)
        bst = bst_ref[...]
        mixed, wms = _spatial_mix(vgn, ws_ref, bst, tm)
        sgu = u * mixed
        dsgu, gor = _norm_bwd(dyg_ref[...], sgu, go_ref[...])
        _acc_rows(dgo_ref, first, gor)
        du = dsgu * mixed
        dmixed = dsgu * u
        dmb = dmixed.astype(BF)
        tril = _tri(CHUNK, True)
        dvgn_rows = []
        dws = [None] * GMLP_G
        dbs = [None] * GMLP_G
        for c in range(tm // CHUNK):
            cs = slice(c * CHUNK, (c + 1) * CHUNK)
            cols = []
            for g in range(GMLP_G):
                gs = slice(g * GMLP_GD, (g + 1) * GMLP_GD)
                dmc = dmb[cs, gs]
                pw = _nt(dmc, vgn[cs, gs])
                pb = jnp.sum(dmixed[cs, gs], axis=1, keepdims=True)
                dws[g] = pw if dws[g] is None else dws[g] + pw
                dbs[g] = pb if dbs[g] is None else dbs[g] + pb
                cols.append(_tn(wms[g], dmc))
            dvgn_rows.append(jnp.concatenate(cols, axis=1))
        dvgn = jnp.concatenate(dvgn_rows, axis=0)
        dbs_t = jnp.concatenate(dbs, axis=1)
        for g in range(GMLP_G):
            dwg = jnp.where(tril, dws[g], 0.0)

            @pl.when(first)
            def _():
                dws_ref[g] = dwg

            @pl.when(jnp.logical_not(first))
            def _():
                dws_ref[g] += dwg

        @pl.when(first)
        def _():
            dbst_ref[...] = dbs_t

        @pl.when(jnp.logical_not(first))
        def _():
            dbst_ref[...] += dbs_t

        dvg, gsr = _norm_bwd(dvgn, vg, gs_ref[...])
        _acc_rows(dgs_ref, first, gsr)
        dz_ref[:, Z_U:Z_U + GMLP_W] = (du * _gelu_grad(u_pre)).astype(BF)
        dz_ref[:, Z_G:Z_G + GMLP_W] = (dvg * _gelu_grad(vg_pre)).astype(BF)

    rev = lambda i: (n - 1 - i, 0)
    fix = lambda i: (0, 0)
    fix3 = lambda i: (0, 0, 0)
    return pl.pallas_call(
        body, name="mix_prep_bwd", grid=(n,),
        in_specs=[pl.BlockSpec((tm, ZW), rev), pl.BlockSpec((tm, AUG_W), rev), pl.BlockSpec((tm, AUG_W), rev),
                  pl.BlockSpec((tm, AUG_W), rev), pl.BlockSpec((tm, GMLP_W), rev),
                  pl.BlockSpec((1, LANES), fix), pl.BlockSpec((1, FOX_HD), fix), pl.BlockSpec((1, FOX_HD), fix),
                  pl.BlockSpec((1, GMLP_W), fix), pl.BlockSpec((GMLP_G, CHUNK, CHUNK), fix3),
                  pl.BlockSpec((CHUNK, GMLP_G), fix), pl.BlockSpec((1, GMLP_W), fix)],
        out_specs=[pl.BlockSpec((tm, ZW), rev), pl.BlockSpec((1, FOX_HD), fix), pl.BlockSpec((1, FOX_HD), fix),
                   pl.BlockSpec((1, GMLP_W), fix), pl.BlockSpec((1, GMLP_W), fix),
                   pl.BlockSpec((GMLP_G, CHUNK, CHUNK), fix3), pl.BlockSpec((CHUNK, GMLP_G), fix),
                   pl.BlockSpec((1, LANES), fix)],
        out_shape=[S((T, ZW), BF), S((1, FOX_HD), F32), S((1, FOX_HD), F32), S((1, GMLP_W), F32), S((1, GMLP_W), F32),
                   S((GMLP_G, CHUNK, CHUNK), F32), S((CHUNK, GMLP_G), F32), S((1, LANES), F32)],
        scratch_shapes=[pltpu.VMEM((1, LANES), F32)],
        compiler_params=_cp(1))(z, dq, dk, dv, dyg, bf128, g_q, g_k, g_sgu, w_s, b_st, g_go)


def _mix_proj_bwd(dz, wz, x, g, dy):
    T, D = x.shape
    tm = _tile(T, 256)

    def body(dz_ref, w_ref, x_ref, g_ref, dy_ref, dx_ref, dxb_ref, dg_ref):
        dh = _nn(dz_ref[...], w_ref[...])
        dx, dgr = _norm_bwd(dh, x_ref[...], g_ref[...])
        dx = dx + dy_ref[...]
        dx_ref[...] = dx
        dxb_ref[...] = dx.astype(BF)
        _acc_rows(dg_ref, pl.program_id(0) == 0, dgr)

    row = lambda i: (i, 0)
    fix = lambda i: (0, 0)
    return pl.pallas_call(
        body, name="mix_proj_bwd", grid=(T // tm,),
        in_specs=[pl.BlockSpec((tm, ZW), row), pl.BlockSpec((ZW, D), fix), pl.BlockSpec((tm, D), row),
                  pl.BlockSpec((1, D), fix), pl.BlockSpec((tm, D), row)],
        out_specs=[pl.BlockSpec((tm, D), row), pl.BlockSpec((tm, D), row), pl.BlockSpec((1, D), fix)],
        out_shape=[S((T, D), F32), S((T, D), BF), S((1, D), F32)],
        compiler_params=_cp(1))(dz, wz, x, g, dy)


def _ca_kv(mem, g_mem, wckv, g_ck):
    M, D = mem.shape

    def body(m_ref, g_ref, w_ref, gk_ref, mn_ref, kr_ref, kn_ref, v_ref):
        mf = m_ref[...]
        mn = (mf * _rstd(mf) * g_ref[...]).astype(BF)
        mn_ref[...] = mn
        for h in range(CA_HEADS):
            kr = _nn(mn, w_ref[h])
            kr_ref[h] = kr
            kn_ref[h] = (kr * _rstd(kr) * gk_ref[...]).astype(BF)
            v_ref[h] = _nn(mn, w_ref[CA_HEADS + h]).astype(BF)

    hd = (CA_HEADS, M, CA_HD)
    return pl.pallas_call(
        body, name="ca_kv", out_shape=[S((M, D), BF), S(hd, F32), S(hd, BF), S(hd, BF)],
        compiler_params=pltpu.CompilerParams(vmem_limit_bytes=VMEM_LIMIT))(mem, g_mem, wckv, g_ck)


def _ca_tile_fwd(xt, gca, wcq, gcq, kn_ref, v_ref):
    hb = (xt * _rstd(xt) * gca).astype(BF)
    qc = _nn(hb, wcq)
    qr, qn, ps = [], [], []
    for h in range(CA_HEADS):
        qh = qc[:, h * CA_HD:(h + 1) * CA_HD]
        qnh = (qh * _rstd(qh) * gcq * 0.0625).astype(BF)
        s = _nt(qnh, kn_ref[h])
        e = jnp.exp(s - jnp.max(s, axis=1, keepdims=True))
        ps.append(e / jnp.sum(e, axis=1, keepdims=True))
        qr.append(qh)
        qn.append(qnh)
    return hb, qr, qn, ps


def _ca_fwd(x, g_ca, wcq, g_cq, kn, vv, wco):
    T, D = x.shape
    M = kn.shape[1]
    tm = _tile(T, 256)

    def body(x_ref, gca_ref, wcq_ref, gcq_ref, kn_ref, v_ref, wco_ref, o_ref, ob_sc):
        xt = x_ref[...]
        _, _, _, ps = _ca_tile_fwd(xt, gca_ref[...], wcq_ref[...], gcq_ref[...], kn_ref, v_ref)
        for h in range(CA_HEADS):
            ob_sc[:, h * CA_HD:(h + 1) * CA_HD] = _nn(ps[h].astype(BF), v_ref[h]).astype(BF)
        o_ref[...] = xt + _nn(ob_sc[...], wco_ref[...])

    row = lambda i: (i, 0)
    fix = lambda i: (0, 0)
    fix3 = lambda i: (0, 0, 0)
    return pl.pallas_call(
        body, name="ca_fwd", grid=(T // tm,),
        in_specs=[pl.BlockSpec((tm, D), row), pl.BlockSpec((1, D), fix), pl.BlockSpec((D, D), fix),
                  pl.BlockSpec((1, CA_HD), fix), pl.BlockSpec((CA_HEADS, M, CA_HD), fix3),
                  pl.BlockSpec((CA_HEADS, M, CA_HD), fix3), pl.BlockSpec((D, D), fix)],
        out_specs=pl.BlockSpec((tm, D), row), out_shape=S((T, D), F32),
        scratch_shapes=[pltpu.VMEM((tm, D), BF)],
        compiler_params=_cp(1))(x, g_ca, wcq, g_cq, kn, vv, wco)


def _ca_bwd(x, dy, g_ca, wcq, g_cq, kn, vv, wco):
    T, D = x.shape
    M = kn.shape[1]
    tm = _tile(T, 256)
    n = T // tm

    def body(x_ref, dy_ref, gca_ref, wcq_ref, gcq_ref, kn_ref, v_ref, wco_ref,
             dx_ref, dwq_ref, dwo_ref, dkn_ref, dv_ref, dgcq_ref, dgca_ref, aq_sc, ao_sc, ob_sc, dq_sc):
        i = pl.program_id(0)
        first = i == 0
        xt = x_ref[...]
        dyt = dy_ref[...]
        dyb = dyt.astype(BF)
        hb, qr, qn, ps = _ca_tile_fwd(xt, gca_ref[...], wcq_ref[...], gcq_ref[...], kn_ref, v_ref)
        do = _nt(dyb, wco_ref[...])
        gcq_rows = None
        for h in range(CA_HEADS):
            hs = slice(h * CA_HD, (h + 1) * CA_HD)
            p = ps[h]
            pb = p.astype(BF)
            ob_sc[:, hs] = _nn(pb, v_ref[h]).astype(BF)
            doh = do[:, hs].astype(BF)
            dp = _nt(doh, v_ref[h])
            ds = (p * (dp - jnp.sum(dp * p, axis=1, keepdims=True))).astype(BF)
            dvh = _tn(pb, doh)
            dkh = _tn(ds, qn[h])

            @pl.when(first)
            def _():
                dv_ref[h] = dvh
                dkn_ref[h] = dkh

            @pl.when(jnp.logical_not(first))
            def _():
                dv_ref[h] += dvh
                dkn_ref[h] += dkh

            dqn = _nn(ds, kn_ref[h]) * 0.0625
            dqh, gr = _norm_bwd(dqn, qr[h], gcq_ref[...])
            gcq_rows = gr if gcq_rows is None else gcq_rows + gr
            dq_sc[:, hs] = dqh.astype(BF)
        _acc_rows(dgcq_ref, first, gcq_rows)
        dqb = dq_sc[...]
        p_o = _tn(ob_sc[...], dyb)
        p_q = _tn(hb, dqb)

        @pl.when(first)
        def _():
            ao_sc[...] = p_o
            aq_sc[...] = p_q

        @pl.when(jnp.logical_not(first))
        def _():
            ao_sc[...] += p_o
            aq_sc[...] += p_q

        @pl.when(i == n - 1)
        def _():
            dwo_ref[...] = ao_sc[...].astype(BF)
            dwq_ref[...] = aq_sc[...].astype(BF)

        dh = _nt(dqb, wcq_ref[...])
        dx, gar = _norm_bwd(dh, xt, gca_ref[...])
        dx_ref[...] = dx + dyt
        _acc_rows(dgca_ref, first, gar)

    row = lambda i: (i, 0)
    fix = lambda i: (0, 0)
    fix3 = lambda i: (0, 0, 0)
    hd = (CA_HEADS, M, CA_HD)
    return pl.pallas_call(
        body, name="ca_bwd", grid=(n,),
        in_specs=[pl.BlockSpec((tm, D), row), pl.BlockSpec((tm, D), row), pl.BlockSpec((1, D), fix),
                  pl.BlockSpec((D, D), fix), pl.BlockSpec((1, CA_HD), fix), pl.BlockSpec(hd, fix3),
                  pl.BlockSpec(hd, fix3), pl.BlockSpec((D, D), fix)],
        out_specs=[pl.BlockSpec((tm, D), row), pl.BlockSpec((D, D), fix), pl.BlockSpec((D, D), fix),
                   pl.BlockSpec(hd, fix3), pl.BlockSpec(hd, fix3), pl.BlockSpec((1, CA_HD), fix),
                   pl.BlockSpec((1, D), fix)],
        out_shape=[S((T, D), F32), S((D, D), BF), S((D, D), BF), S(hd, F32), S(hd, F32), S((1, CA_HD), F32),
                   S((1, D), F32)],
        scratch_shapes=[pltpu.VMEM((D, D), F32), pltpu.VMEM((D, D), F32), pltpu.VMEM((tm, D), BF),
                        pltpu.VMEM((tm, D), BF)],
        compiler_params=_cp(1))(x, dy, g_ca, wcq, g_cq, kn, vv, wco)


def _ca_kv_bwd(mem, g_mem, mn, kraw, dkn, dvv, wckv, g_ck):
    M, D = mem.shape

    def body(m_ref, g_ref, mn_ref, kr_ref, dkn_ref, dv_ref, w_ref, gk_ref, dw_ref, dgk_ref, dgm_ref):
        mn = mn_ref[...]
        dmn = jnp.zeros((M, D), F32)
        gk_rows = None
        for h in range(CA_HEADS):
            dkr, gr = _norm_bwd(dkn_ref[h], kr_ref[h], gk_ref[...])
            gk_rows = gr if gk_rows is None else gk_rows + gr
            dkb = dkr.astype(BF)
            dvb = dv_ref[h].astype(BF)
            dw_ref[h] = _tn(mn, dkb).astype(BF)
            dw_ref[CA_HEADS + h] = _tn(mn, dvb).astype(BF)
            dmn = dmn + _nt(dkb, w_ref[h]) + _nt(dvb, w_ref[CA_HEADS + h])
        dgk_ref[...] = jnp.sum(gk_rows, axis=0, keepdims=True)
        mf = m_ref[...]
        dgm_ref[...] = jnp.sum(dmn * (mf * _rstd(mf)), axis=0, keepdims=True)

    return pl.pallas_call(
        body, name="ca_kv_bwd",
        out_shape=[S((2 * CA_HEADS, D, CA_HD), BF), S((1, CA_HD), F32), S((1, D), F32)],
        compiler_params=pltpu.CompilerParams(vmem_limit_bytes=VMEM_LIMIT))(mem, g_mem, mn, kraw, dkn, dvv, wckv, g_ck)


def _after(g, token):
    return g if token is None else g + token[0:1, 0:1]


def _local_step(x, mem, target, small, weights, emit):
    T, D = x.shape
    p = small
    bf128 = jnp.pad(p["b_f"], ((0, 0), (0, LANES - FOX_HEADS)))
    b_st = p["b_s"].T

    wup1 = weights("ffn1_up", x)["wup1"]
    a1, h1 = _ffn_up("ffn1_up", x, p["g_ffn1"], wup1)
    wdn1 = weights("ffn1_dn", h1)["wdn1"]
    x1 = _ffn_down("ffn1_down", a1, wdn1, x)
    wm = weights("mix", x1)
    z, h2 = _mix_proj(x1, p["g_mix"], wm["wz"])
    qf, ka, va, yg = _mix_prep(z, bf128, p["g_q"], p["g_k"], p["g_sgu"], p["w_s"], b_st, p["g_gmlp_o"])
    attn, lse = _fox_fwd(qf, ka, va)
    x2 = _mix_out(attn, yg, p["g_fox_o"], wm["wout"], x1)
    wc = weights("ca", x2)
    mn, kraw, ckn, cvv = _ca_kv(mem, p["g_mem"], wc["wckv"], p["g_ck"])
    x3 = _ca_fwd(x2, p["g_ca"], wc["wcq"], p["g_cq"], ckn, cvv, wc["wco"])
    w2 = weights("ffn2", x3)
    a2, h4 = _ffn_up("ffn2_up", x3, p["g_ffn2"], w2["wup2"])
    dy4, dy4b, sq = _ffn_down_loss("ffn2_down", a2, w2["wdn2"], x3, target)

    gs = {}
    dgu2 = _ffn_bwd_act("ffn2_bwd_act", dy4b, h4, w2["wup2"], w2["wdn2"])
    dwup2, dwdn2 = _ffn_dw("ffn2", h4, dgu2, a2, dy4b)
    tok = emit("ffn2", {"wup2": dwup2, "wdn2": dwdn2})
    dx3, gs["g_ffn2"] = _ffn_dx("ffn2_dx", dgu2, w2["wup2"], x3, _after(p["g_ffn2"], tok), dy4)

    dx2, dwcq, dwco, dckn, dcvv, gs["g_cq"], gs["g_ca"] = _ca_bwd(
        x2, dx3, p["g_ca"], wc["wcq"], p["g_cq"], ckn, cvv, wc["wco"])
    dwckv, gs["g_ck"], gs["g_mem"] = _ca_kv_bwd(mem, p["g_mem"], mn, kraw, dckn, dcvv, wc["wckv"], p["g_ck"])

    qb, dob, dyg, dwout, gs["g_fox_o"] = _mix_out_bwd(dx2, attn, yg, p["g_fox_o"], wm["wout"], qf, lse)
    dq, dk, dv = _fox_bwd(qb, ka, va, dob)
    dz, gs["g_q"], gs["g_k"], gs["g_sgu"], gs["g_gmlp_o"], gs["w_s"], dbst, dbf = _mix_prep_bwd(
        z, dq, dk, dv, dyg, bf128, p["g_q"], p["g_k"], p["g_sgu"], p["w_s"], b_st, p["g_gmlp_o"])
    gs["b_s"] = dbst.T
    gs["b_f"] = dbf[:, :FOX_HEADS]
    tk = _tile(T, 512)
    zb = ZW // 3
    dwz = _tn_matmul(
        "mix_dwz", dz, pl.BlockSpec((tk, zb), lambda j, k: (k, j)), h2, pl.BlockSpec((tk, D), lambda j, k: (k, 0)),
        S((ZW, D), F32), pl.BlockSpec((zb, D), lambda j, k: (j, 0)), (3, T // tk), (zb, D))
    tok = emit("mid", {"wcq": dwcq, "wco": dwco, "wckv": dwckv, "wout": dwout, "wz": dwz})
    dx1, dx1b, gs["g_mix"] = _mix_proj_bwd(dz, wm["wz"], x1, _after(p["g_mix"], tok), dx2)

    dgu1 = _ffn_bwd_act("ffn1_bwd_act", dx1b, h1, wup1, wdn1)
    dwup1, dwdn1 = _ffn_dw("ffn1", h1, dgu1, a1, dx1b)
    tok = emit("ffn1", {"wup1": dwup1, "wdn1": dwdn1})
    dx0, gs["g_ffn1"] = _ffn_dx("ffn1_dx", dgu1, wup1, x, _after(p["g_ffn1"], tok), dx1)
    return sq, dx0, gs


MESH = pl.DeviceIdType.MESH
HBM_SPEC = pl.BlockSpec(memory_space=pltpu.HBM)
N_PEER = N_DEV - 1


def _place():
    return lax.axis_index("x"), lax.axis_index("y"), lax.axis_index("c")


def _slot(px, py, pc):
    return 4 * px + 2 * py + pc


SEM_SPEC = pl.BlockSpec(memory_space=pltpu.SEMAPHORE)
ANY_SPEC = pl.BlockSpec(memory_space=pl.ANY)
DATAFLOW = pltpu.SideEffectType.DATAFLOW_SIDE_EFFECTING


def _hbm(a):
    return pltpu.with_memory_space_constraint(a, pltpu.HBM)


def _peer(x, y, c, r):
    return (1 - x if r & 4 else x, 1 - y if r & 2 else y, 1 - c if r & 1 else c)


def _place_own(srcs, whole):
    my = _slot(*_place())
    lands = []
    for s in srcs:
        blk = s[None] if whole else lax.dynamic_slice_in_dim(s, my, 1, 0)
        shape = (N_DEV,) + s.shape if whole else s.shape
        lands.append(lax.dynamic_update_slice_in_dim(lax.empty(shape, s.dtype), blk, my, 0))
    return lands


def _copy_start(name, srcs, lands, whole):
    n = len(srcs)

    def body(*refs):
        src, land = refs[:n], refs[n:2 * n]
        send, recv = refs[2 * n:3 * n], refs[3 * n:4 * n]
        token = refs[6 * n]
        x, y, c = _place()
        my = _slot(x, y, c)
        for a in range(n):
            for r in range(1, N_DEV):
                p = _peer(x, y, c, r)
                pltpu.make_async_remote_copy(
                    src_ref=src[a] if whole else src[a].at[_slot(*p)], dst_ref=land[a].at[my],
                    send_sem=send[a].at[r - 1], recv_sem=recv[a].at[r - 1], device_id=p, device_id_type=MESH).start()
        token[...] = jnp.zeros_like(token)

    out = pl.pallas_call(
        body, name=name,
        out_shape=([pltpu.SemaphoreType.DMA((N_PEER,))] * (2 * n)
                   + [pltpu.HBM(s.shape, s.dtype) for s in srcs] + [pltpu.HBM(s.shape, s.dtype) for s in lands]
                   + [S((8, LANES), F32)]),
        in_specs=[HBM_SPEC] * (2 * n),
        out_specs=[SEM_SPEC] * (2 * n) + [HBM_SPEC] * (2 * n) + [pl.BlockSpec(memory_space=pltpu.VMEM)],
        input_output_aliases={i: 2 * n + i for i in range(2 * n)},
        compiler_params=pltpu.CompilerParams(has_side_effects=DATAFLOW),
    )(*[_hbm(s) for s in srcs], *[_hbm(s) for s in lands])
    return out[:n], out[n:2 * n], out[2 * n:3 * n], out[3 * n:4 * n], out[4 * n]


def _copy_wait(name, srcs, lands, send, recv, after, whole):
    n = len(srcs)

    def body(*refs):
        src, land = refs[:n], refs[n:2 * n]
        snd, rcv = refs[2 * n:3 * n], refs[3 * n:4 * n]
        x, y, c = _place()
        for a in range(n):
            for r in range(1, N_DEV):
                p = _peer(x, y, c, r)
                ps = _slot(*p)
                cp = pltpu.make_async_remote_copy(
                    src_ref=src[a] if whole else src[a].at[ps], dst_ref=land[a].at[ps],
                    send_sem=snd[a].at[r - 1], recv_sem=rcv[a].at[r - 1], device_id=p, device_id_type=MESH)
                cp.wait_send()
                cp.wait_recv()

    out = pl.pallas_call(
        body, name=name,
        out_shape=[pltpu.HBM(s.shape, s.dtype) for s in srcs] + [pltpu.HBM(s.shape, s.dtype) for s in lands],
        in_specs=[HBM_SPEC] * (2 * n) + [SEM_SPEC] * (2 * n) + [ANY_SPEC],
        out_specs=[HBM_SPEC] * (2 * n),
        input_output_aliases={i: i for i in range(2 * n)},
        compiler_params=pltpu.CompilerParams(has_side_effects=DATAFLOW),
    )(*srcs, *lands, *send, *recv, after)
    return out[n:]


def _adamw(w, g, m, v):
    m2 = ADAM_B1 * m + (1.0 - ADAM_B1) * g
    v2 = ADAM_B2 * v + (1.0 - ADAM_B2) * (g * g)
    m_hat = m2 / (1.0 - ADAM_B1 ** ADAM_STEP)
    v_hat = v2 / (1.0 - ADAM_B2 ** ADAM_STEP)
    delta = -ADAM_LR * (m_hat / (jnp.sqrt(v_hat) + ADAM_EPS) + ADAM_WD * w)
    return delta, m2, v2


def _adamw_big(name, slots, w, m, v):
    R, C = w.shape
    tr = next((t for t in (256, 352) if R % t == 0), R)

    def body(s_ref, w_ref, m_ref, v_ref, g_ref, d_ref, m2_ref, v2_ref):
        g = s_ref[0].astype(F32)
        for k in range(1, N_DEV):
            g = g + s_ref[k].astype(F32)
        d, m2, v2 = _adamw(w_ref[...], g, m_ref[...], v_ref[...])
        g_ref[...] = g
        d_ref[...] = d
        m2_ref[...] = m2
        v2_ref[...] = v2

    row = pl.BlockSpec((tr, C), lambda i: (i, 0))
    return pl.pallas_call(
        body, name=name, grid=(R // tr,),
        in_specs=[pl.BlockSpec((N_DEV, tr, C), lambda i: (0, i, 0)), row, row, row],
        out_specs=[row] * 4, out_shape=[S((R, C), F32)] * 4,
        compiler_params=_cp(1))(slots, w, m, v)


TINY_ROWS = (("b_s", 8), ("g_ffn1", 8), ("g_mix", 8), ("g_ca", 8), ("g_mem", 8), ("g_ffn2", 8), ("g_sgu", 4),
             ("g_fox_o", 4), ("g_gmlp_o", 4), ("g_cq", 2), ("g_ck", 2), ("g_q", 1), ("g_k", 1), ("b_f", 1))
TINY_P = 72


def _pack_tiny(d):
    rows = []
    for name, r in TINY_ROWS:
        flat = d[name].reshape(-1)
        rows.append(jnp.pad(flat, (0, r * LANES - flat.shape[0])).reshape(r, LANES))
    used = sum(r for _, r in TINY_ROWS)
    rows.append(jnp.zeros((TINY_P - used, LANES), F32))
    return jnp.concatenate(rows, axis=0)


def _unpack_tiny(packed, shapes):
    out, at = {}, 0
    for name, r in TINY_ROWS:
        shape = shapes[name]
        size = 1
        for s in shape:
            size *= s
        out[name] = packed[at:at + r].reshape(-1)[:size].reshape(shape)
        at += r
    return out


WEIGHTS =('g_ffn1', 'w_ffn1_in', 'w_ffn1_out', 'g_mix', 'w_in', 'b_f', 'g_q', 'g_k', 'g_sgu', 'w_s', 'b_s',
           'g_fox_o', 'g_gmlp_o', 'w_out', 'g_ca', 'g_mem', 'w_cq', 'w_ckv', 'g_cq', 'g_ck', 'w_co', 'g_ffn2',
           'w_ffn2_in', 'w_ffn2_out')
BIG = ('w_ffn1_in', 'w_ffn1_out', 'w_in', 'w_out', 'w_cq', 'w_ckv', 'w_co', 'w_ffn2_in', 'w_ffn2_out')
TRANSPOSED = ('w_ffn1_in', 'w_in', 'w_ffn2_in')
GATHER_GROUPS = {"ffn1_up": ("w_ffn1_in",), "ffn1_dn": ("w_ffn1_out",), "mix": ("w_in", "w_out"),
                 "ca": ("w_cq", "w_ckv", "w_co"), "ffn2": ("w_ffn2_in", "w_ffn2_out")}
GATHER_STAGES = ((None, ("w_ffn1_in", "w_ffn1_out")),
                 ("ffn1_up", ("w_in", "w_out", "w_cq", "w_ckv", "w_co", "w_ffn2_in", "w_ffn2_out")))
QKV_W = 3 * FOX_W
UV_OFF = QKV_W + FOX_HEADS


def kernel(x, mem, g_ffn1, w_ffn1_in, w_ffn1_out, g_mix, w_in, b_f, g_q, g_k, g_sgu, w_s, b_s, g_fox_o, g_gmlp_o, w_out, g_ca, g_mem, w_cq, w_ckv, g_cq, g_ck, w_co, g_ffn2, w_ffn2_in, w_ffn2_out, loss_target, m_g_ffn1, m_w_ffn1_in, m_w_ffn1_out, m_g_mix, m_w_in, m_b_f, m_g_q, m_g_k, m_g_sgu, m_w_s, m_b_s, m_g_fox_o, m_g_gmlp_o, m_w_out, m_g_ca, m_g_mem, m_w_cq, m_w_ckv, m_g_cq, m_g_ck, m_w_co, m_g_ffn2, m_w_ffn2_in, m_w_ffn2_out, v_g_ffn1, v_w_ffn1_in, v_w_ffn1_out, v_g_mix, v_w_in, v_b_f, v_g_q, v_g_k, v_g_sgu, v_w_s, v_b_s, v_g_fox_o, v_g_gmlp_o, v_w_out, v_g_ca, v_g_mem, v_w_cq, v_w_ckv, v_g_cq, v_g_ck, v_w_co, v_g_ffn2, v_w_ffn2_in, v_w_ffn2_out):
    args = dict(locals())
    w = {n: args[n] for n in WEIGHTS}
    mo = {n: args["m_" + n] for n in WEIGHTS}
    vo = {n: args["v_" + n] for n in WEIGHTS}
    D = D_MODEL

    def local(n, a):
        return a[0].T if n in TRANSPOSED else a[0]

    shards = {n: local(n, w[n]).astype(BF) for n in BIG}
    fb = shards["w_ffn1_in"].shape[0]
    handles = {}

    def start_gather(stage, names, arrays):
        snd, rcv, src, land, _ = _copy_start("gather_start_%d" % stage, arrays, _place_own(arrays, True), True)
        for i, n in enumerate(names):
            handles[n] = (src[i], land[i], snd[i], rcv[i])

    start_gather(0, GATHER_STAGES[0][1], [shards[n] for n in GATHER_STAGES[0][1]])

    def weights(group, after):
        names = GATHER_GROUPS[group]
        hs = [handles[n] for n in names]
        got = _copy_wait("gather_wait_" + group, [h[0] for h in hs], [h[1] for h in hs], [h[2] for h in hs],
                         [h[3] for h in hs], after, True)
        for stage, (trigger, members) in enumerate(GATHER_STAGES):
            if trigger == group:
                held = lax.optimization_barrier((tuple(shards[n] for n in members), got[0]))[0]
                start_gather(stage, members, list(held))
        got = dict(zip(names, got))
        if group == "ffn1_up":
            return {"wup1": got["w_ffn1_in"].reshape(2, N_FFN_BLK, fb, D)}
        if group == "ffn1_dn":
            return {"wdn1": got["w_ffn1_out"].reshape(N_FFN_BLK, fb, D)}
        if group == "mix":
            full = got["w_in"].reshape(-1, D)
            wz = jnp.concatenate([full[:QKV_W], full[UV_OFF:], full[QKV_W:UV_OFF],
                                  jnp.zeros((LANES - FOX_HEADS, D), BF)], axis=0)
            return {"wz": wz, "wout": got["w_out"].reshape(D, D)}
        if group == "ca":
            return {"wcq": got["w_cq"].reshape(D, D), "wco": got["w_co"].reshape(D, D), "wckv": got["w_ckv"]}
        return {"wup2": got["w_ffn2_in"].reshape(2, N_FFN_BLK, fb, D),
                "wdn2": got["w_ffn2_out"].reshape(N_FFN_BLK, fb, D)}

    flying = {}

    def emit(group, g):
        if group == "ffn2":
            parts = {"w_ffn2_in": g["wup2"], "w_ffn2_out": g["wdn2"].reshape(N_DEV, -1, D)}
        elif group == "ffn1":
            parts = {"w_ffn1_in": g["wup1"], "w_ffn1_out": g["wdn1"].reshape(N_DEV, -1, D)}
        else:
            gz = g["wz"]
            g_in = jnp.concatenate([gz[:QKV_W], gz[Z_F:Z_F + FOX_HEADS], gz[QKV_W:Z_F]], axis=0)
            parts = {"w_in": g_in.reshape(N_DEV, -1, D).astype(BF),
                     "w_out": g["wout"].reshape(N_DEV, -1, D), "w_cq": g["wcq"].reshape(N_DEV, -1, D),
                     "w_co": g["wco"].reshape(N_DEV, -1, D), "w_ckv": g["wckv"]}
        names = list(parts)
        srcs = [parts[n] for n in names]
        *copies, token = _copy_start("exchange_start_" + group, srcs, _place_own(srcs, False), False)
        flying[group] = (names, copies)
        return token

    tiny_names = [n for n, _ in TINY_ROWS]
    small = {n: (w[n][0] if n == "b_s" else w[n]) for n in tiny_names}
    small["w_s"] = w["w_s"][0]

    sq, dx0, gs = _local_step(x[0], mem[0], loss_target[0], small, weights, emit)
    loss = lax.psum(sq[0, 0], ("x", "y", "c")) * (0.5 / D)

    sm_parts = [gs["w_s"].reshape(-1, LANES), _pack_tiny(gs)]
    sm_snd, sm_rcv, sm_src, sm_land, sm_token = _copy_start("small_start", sm_parts, _place_own(sm_parts, True), True)

    grad, delta, new_m, new_v = {}, {}, {}, {}

    def update(group, after):
        names, (snd, rcv, srcs, lands) = flying[group]
        slots = _copy_wait("exchange_wait_" + group, srcs, lands, snd, rcv, after, False)
        for n, sl in zip(names, slots):
            g, d, m2, v2 = _adamw_big("adamw_" + n, sl, local(n, w[n]), local(n, mo[n]), local(n, vo[n]))
            grad[n], delta[n], new_m[n], new_v[n] = (
                (t.T if n in TRANSPOSED else t).reshape(w[n].shape) for t in (g, d, m2, v2))
        return d

    last = update("ffn2", sm_token)
    last = update("mid", last)
    last = update("ffn1", last)
    ws_all, tiny_all = _copy_wait("small_wait", sm_src, sm_land, sm_snd, sm_rcv, last, True)
    ws_shape = w["w_s"].shape
    for store, t in zip((grad, delta, new_m, new_v), _adamw_big(
            "adamw_w_s", ws_all, *[a["w_s"].reshape(-1, LANES) for a in (w, mo, vo)])):
        store["w_s"] = t.reshape(ws_shape)
    shapes = {n: w[n].shape for n in tiny_names}
    for store, t in zip((grad, delta, new_m, new_v), _adamw_big(
            "adamw_tiny", tiny_all, *[_pack_tiny({n: a[n] for n in tiny_names}) for a in (w, mo, vo)])):
        store.update(_unpack_tiny(t, shapes))

    return (loss, dx0[None], *[grad[n] for n in WEIGHTS], *[delta[n] for n in WEIGHTS],
            *[new_m[n] for n in WEIGHTS], *[new_v[n] for n in WEIGHTS])
```

```python
import functools

import jax
import jax.numpy as jnp
from jax import lax
from jax.experimental import pallas as pl
from jax.experimental.pallas import tpu as pltpu

F32 = jnp.float32
BF = jnp.bfloat16
S = jax.ShapeDtypeStruct

N_DEV = 8
D_MODEL = 1024
FOX_HEADS, FOX_HD = 8, 64
FOX_W = 512
GMLP_G, GMLP_GD = 8, 64
GMLP_W = 512
CHUNK = 128
CA_HEADS, CA_HD = 4, 256
N_FFN_BLK = 4
ZW = 2688
Z_Q, Z_K, Z_V, Z_U, Z_G, Z_F = 0, 512, 1024, 1536, 2048, 2560
EPS = 1e-6
NEG = -1e30
LANES = 128

ADAM_LR, ADAM_B1, ADAM_B2, ADAM_EPS, ADAM_WD, ADAM_STEP = 0.001, 0.9, 0.999, 1e-08, 0.01, 10

VMEM_LIMIT = 52 * 2 ** 20


def _cp(n_axes):
    return pltpu.CompilerParams(dimension_semantics=("arbitrary",) * n_axes, vmem_limit_bytes=VMEM_LIMIT)


def _nn(a, b):
    return jnp.dot(a, b, preferred_element_type=F32)


def _nt(a, b):
    return lax.dot_general(a, b, (((1,), (1,)), ((), ())), preferred_element_type=F32)


def _tn(a, b):
    return lax.dot_general(a, b, (((0,), (0,)), ((), ())), preferred_element_type=F32)


def _hi(a, b):
    return jnp.dot(a, b, precision=lax.Precision.HIGHEST, preferred_element_type=F32)


def _rstd(x):
    return lax.rsqrt(jnp.mean(x * x, axis=-1, keepdims=True) + EPS)


def _norm_bwd(dy, x, g):
    r = _rstd(x)
    xh = x * r
    dxh = dy * g
    dx = r * (dxh - xh * jnp.mean(dxh * xh, axis=-1, keepdims=True))
    return dx, dy * xh


def _acc_rows(ref, first, val):
    srow = jnp.sum(val, axis=0, keepdims=True)

    @pl.when(first)
    def _():
        ref[...] = srow

    @pl.when(jnp.logical_not(first))
    def _():
        ref[...] += srow


def _gelu(x):
    c = 0.7978845608028654
    return 0.5 * x * (1.0 + jnp.tanh(c * (x + 0.044715 * x * x * x)))


def _gelu_grad(x):
    c = 0.7978845608028654
    t = jnp.tanh(c * (x + 0.044715 * x * x * x))
    return 0.5 * (1.0 + t) + 0.5 * x * (1.0 - t * t) * c * (1.0 + 3 * 0.044715 * x * x)


def _tile(n, pref):
    return pref if n % pref == 0 else n


def _ffn_up(name, x, g, wup):
    T, D = x.shape
    FB = wup.shape[-2]
    tm = _tile(T, 1024)

    def body(x_ref, g_ref, w_ref, a_ref, h_ref):
        @pl.when(pl.program_id(1) == 0)
        def _():
            xf = x_ref[...]
            h_ref[...] = (xf * _rstd(xf) * g_ref[...]).astype(BF)

        hb = h_ref[...]
        gg = _nt(hb, w_ref[0])
        uu = _nt(hb, w_ref[1])
        a_ref[...] = (gg * jax.nn.sigmoid(gg) * uu).astype(BF)

    return pl.pallas_call(
        body, name=name, grid=(T // tm, N_FFN_BLK),
        in_specs=[pl.BlockSpec((tm, D), lambda i, j: (i, 0)),
                  pl.BlockSpec((1, D), lambda i, j: (0, 0)),
                  pl.BlockSpec((2, None, FB, D), lambda i, j: (0, j, 0, 0))],
        out_specs=[pl.BlockSpec((None, tm, FB), lambda i, j: (j, i, 0)),
                   pl.BlockSpec((tm, D), lambda i, j: (i, 0))],
        out_shape=[S((N_FFN_BLK, T, FB), BF), S((T, D), BF)],
        compiler_params=_cp(2))(x, g, wup)


def _ffn_down(name, a, wdn, x):
    _, T, FB = a.shape
    D = x.shape[1]
    tm = _tile(T, 512)

    def body(a_ref, w_ref, x_ref, o_ref):
        j = pl.program_id(1)
        p = 0.5 * _nn(a_ref[...], w_ref[...])

        @pl.when(j == 0)
        def _():
            o_ref[...] = x_ref[...] + p

        @pl.when(j > 0)
        def _():
            o_ref[...] += p

    return pl.pallas_call(
        body, name=name, grid=(T // tm, N_FFN_BLK),
        in_specs=[pl.BlockSpec((None, tm, FB), lambda i, j: (j, i, 0)),
                  pl.BlockSpec((None, FB, D), lambda i, j: (j, 0, 0)),
                  pl.BlockSpec((tm, D), lambda i, j: (i, 0))],
        out_specs=pl.BlockSpec((tm, D), lambda i, j: (i, 0)),
        out_shape=S((T, D), F32),
        compiler_params=_cp(2))(a, wdn, x)


def _ffn_down_loss(name, a, wdn, x, target):
    _, T, FB = a.shape
    D = x.shape[1]
    tm = _tile(T, 512)

    def body(a_ref, w_ref, x_ref, t_ref, d_ref, db_ref, loss_ref, acc_ref):
        i, j = pl.program_id(0), pl.program_id(1)
        p = 0.5 * _nn(a_ref[...], w_ref[...])

        @pl.when(j == 0)
        def _():
            acc_ref[...] = x_ref[...] + p

        @pl.when(j > 0)
        def _():
            acc_ref[...] += p

        @pl.when(j == N_FFN_BLK - 1)
        def _():
            diff = acc_ref[...] - t_ref[...]
            dy = diff * (1.0 / D)
            d_ref[...] = dy
            db_ref[...] = dy.astype(BF)
            sq = jnp.zeros((8, LANES), F32) + jnp.sum(diff * diff)

            @pl.when(i == 0)
            def _():
                loss_ref[...] = sq

            @pl.when(i > 0)
            def _():
                loss_ref[...] += sq

    return pl.pallas_call(
        body, name=name, grid=(T // tm, N_FFN_BLK),
        in_specs=[pl.BlockSpec((None, tm, FB), lambda i, j: (j, i, 0)),
                  pl.BlockSpec((None, FB, D), lambda i, j: (j, 0, 0)),
                  pl.BlockSpec((tm, D), lambda i, j: (i, 0)),
                  pl.BlockSpec((tm, D), lambda i, j: (i, 0))],
        out_specs=[pl.BlockSpec((tm, D), lambda i, j: (i, 0)),
                   pl.BlockSpec((tm, D), lambda i, j: (i, 0)),
                   pl.BlockSpec((8, LANES), lambda i, j: (0, 0))],
        out_shape=[S((T, D), F32), S((T, D), BF), S((8, LANES), F32)],
        scratch_shapes=[pltpu.VMEM((tm, D), F32)],
        compiler_params=_cp(2))(a, wdn, x, target)


def _ffn_bwd_act(name, dyb, h, wup, wdn):
    T, D = h.shape
    FB = wup.shape[-2]
    tm = _tile(T, 1024)

    def body(d_ref, h_ref, wu_ref, wd_ref, o_ref):
        da = 0.5 * _nt(d_ref[...], wd_ref[...])
        hb = h_ref[...]
        gg = _nt(hb, wu_ref[0])
        uu = _nt(hb, wu_ref[1])
        sg = jax.nn.sigmoid(gg)
        o_ref[0] = (da * uu * (sg * (1.0 + gg * (1.0 - sg)))).astype(BF)
        o_ref[1] = (da * (gg * sg)).astype(BF)

    return pl.pallas_call(
        body, name=name, grid=(T // tm, N_FFN_BLK),
        in_specs=[pl.BlockSpec((tm, D), lambda i, j: (i, 0)),
                  pl.BlockSpec((tm, D), lambda i, j: (i, 0)),
                  pl.BlockSpec((2, None, FB, D), lambda i, j: (0, j, 0, 0)),
                  pl.BlockSpec((None, FB, D), lambda i, j: (j, 0, 0))],
        out_specs=pl.BlockSpec((2, None, tm, FB), lambda i, j: (0, j, i, 0)),
        out_shape=S((2, N_FFN_BLK, T, FB), BF),
        compiler_params=_cp(2))(dyb, h, wup, wdn)


def _ffn_dx(name, dgu, wup, x, g, dy):
    T, D = x.shape
    FB = wup.shape[-2]
    tm = _tile(T, 1024)

    def body(d_ref, w_ref, x_ref, g_ref, dy_ref, dx_ref, dg_ref, acc_ref):
        i, j = pl.program_id(0), pl.program_id(1)
        p = _nn(d_ref[0], w_ref[0]) + _nn(d_ref[1], w_ref[1])

        @pl.when(j == 0)
        def _():
            acc_ref[...] = p

        @pl.when(j > 0)
        def _():
            acc_ref[...] += p

        @pl.when(j == N_FFN_BLK - 1)
        def _():
            dx, dgr = _norm_bwd(acc_ref[...], x_ref[...], g_ref[...])
            dx_ref[...] = dx + dy_ref[...]
            _acc_rows(dg_ref, i == 0, dgr)

    return pl.pallas_call(
        body, name=name, grid=(T // tm, N_FFN_BLK),
        in_specs=[pl.BlockSpec((2, None, tm, FB), lambda i, j: (0, j, i, 0)),
                  pl.BlockSpec((2, None, FB, D), lambda i, j: (0, j, 0, 0)),
                  pl.BlockSpec((tm, D), lambda i, j: (i, 0)),
                  pl.BlockSpec((1, D), lambda i, j: (0, 0)),
                  pl.BlockSpec((tm, D), lambda i, j: (i, 0))],
        out_specs=[pl.BlockSpec((tm, D), lambda i, j: (i, 0)),
                   pl.BlockSpec((1, D), lambda i, j: (0, 0))],
        out_shape=[S((T, D), F32), S((1, D), F32)],
        scratch_shapes=[pltpu.VMEM((tm, D), F32)],
        compiler_params=_cp(2))(dgu, wup, x, g, dy)


def _tn_matmul(name, a, a_spec, b, b_spec, out_shape, out_spec, grid, acc_shape, scale=1.0):
    nk = grid[1]

    def body(a_ref, b_ref, o_ref, acc_ref):
        k = pl.program_id(1)
        p = _tn(a_ref[...], b_ref[...])

        @pl.when(k == 0)
        def _():
            acc_ref[...] = p

        @pl.when(k > 0)
        def _():
            acc_ref[...] += p

        @pl.when(k == nk - 1)
        def _():
            o_ref[...] = (acc_ref[...] * scale).astype(o_ref.dtype)

    return pl.pallas_call(
        body, name=name, grid=grid, in_specs=[a_spec, b_spec], out_specs=out_spec, out_shape=out_shape,
        scratch_shapes=[pltpu.VMEM(acc_shape, F32)], compiler_params=_cp(2))(a, b)


def _ffn_dw(name, h, dgu, a, dyb):
    T, D = h.shape
    FB = a.shape[-1]
    tk = _tile(T, 1024)
    nk = T // tk
    dgu8 = dgu.reshape(2 * N_FFN_BLK, T, FB)
    dwup = _tn_matmul(
        name + "_dwup", dgu8, pl.BlockSpec((None, tk, FB), lambda j, k: (j, k, 0)),
        h, pl.BlockSpec((tk, D), lambda j, k: (k, 0)),
        S((2 * N_FFN_BLK, FB, D), BF), pl.BlockSpec((None, FB, D), lambda j, k: (j, 0, 0)),
        (2 * N_FFN_BLK, nk), (FB, D))
    dwdn = _tn_matmul(
        name + "_dwdn", a, pl.BlockSpec((None, tk, FB), lambda j, k: (j, k, 0)),
        dyb, pl.BlockSpec((tk, D), lambda j, k: (k, 0)),
        S((N_FFN_BLK, FB, D), BF), pl.BlockSpec((None, FB, D), lambda j, k: (j, 0, 0)),
        (N_FFN_BLK, nk), (FB, D), scale=0.5)
    return dwup, dwdn


def _mix_proj(x, g, wz):
    T, D = x.shape
    tm = _tile(T, 512)

    def body(x_ref, g_ref, w_ref, z_ref, h_ref):
        xf = x_ref[...]
        hb = (xf * _rstd(xf) * g_ref[...]).astype(BF)
        h_ref[...] = hb
        z_ref[...] = _nt(hb, w_ref[...])

    return pl.pallas_call(
        body, name="mix_proj", grid=(T // tm,),
        in_specs=[pl.BlockSpec((tm, D), lambda i: (i, 0)),
                  pl.BlockSpec((1, D), lambda i: (0, 0)),
                  pl.BlockSpec((ZW, D), lambda i: (0, 0))],
        out_specs=[pl.BlockSpec((tm, ZW), lambda i: (i, 0)),
                   pl.BlockSpec((tm, D), lambda i: (i, 0))],
        out_shape=[S((T, ZW), F32), S((T, D), BF)],
        compiler_params=_cp(1))(x, g, wz)


def _tri(n, lower):
    r = lax.broadcasted_iota(jnp.int32, (n, n), 0)
    c = lax.broadcasted_iota(jnp.int32, (n, n), 1)
    return (r >= c) if lower else (r <= c)


def _spatial_mix(vgn_b, ws_ref, bst, tm):
    tril = _tri(CHUNK, True)
    wms = [jnp.where(tril, ws_ref[g], 0.0).astype(BF) for g in range(GMLP_G)]
    rows = []
    for c in range(tm // CHUNK):
        cols = []
        for g in range(GMLP_G):
            vs = vgn_b[c * CHUNK:(c + 1) * CHUNK, g * GMLP_GD:(g + 1) * GMLP_GD]
            cols.append(_nn(wms[g], vs) + bst[:, g:g + 1])
        rows.append(jnp.concatenate(cols, axis=1))
    return jnp.concatenate(rows, axis=0), wms


HB = 128
AUG_W = FOX_HEADS * HB
COL_A, COL_B, COL_C = 64, 67, 70


def _spread_matrix():
    r = jnp.arange(FOX_W)
    return (jnp.arange(AUG_W)[None, :] == ((r // FOX_HD) * HB + r % FOX_HD)[:, None]).astype(BF)


def _piece_matrix(col):
    r = jnp.arange(LANES)
    dst = jnp.where(r < 3 * FOX_HEADS, (r % FOX_HEADS) * HB + col + r // FOX_HEADS, -1)
    return (jnp.arange(AUG_W)[None, :] == dst[:, None]).astype(BF)


def _ones_row(cols):
    c = jnp.arange(AUG_W) % HB
    hit = functools.reduce(jnp.logical_or, [(c >= a) & (c < a + 3) for a in cols])
    return hit.astype(F32)[None, :]


def _pieces(x):
    lane = lax.broadcasted_iota(jnp.int32, x.shape, 1)
    x = jnp.where(lane < FOX_HEADS, x, 0.0)
    hi = x.astype(BF).astype(F32)
    r1 = x - hi
    mid = r1.astype(BF).astype(F32)
    lo = (r1 - mid).astype(BF).astype(F32)
    return (hi + pltpu.roll(mid, FOX_HEADS, 1) + pltpu.roll(lo, 2 * FOX_HEADS, 1)).astype(BF)


def _mix_prep(z, bf128, g_q, g_k, g_sgu, w_s, b_st, g_go):
    T = z.shape[0]
    tm = _tile(T, 256)
    spread, pc_q, pc_k = _spread_matrix(), _piece_matrix(COL_A), _piece_matrix(COL_B)
    one_q, one_k, one_v = _ones_row([COL_B]), _ones_row([COL_A, COL_C]), _ones_row([COL_A])

    def body(z_ref, bf_ref, gq_ref, gk_ref, gs_ref, ws_ref, bst_ref, go_ref, sp_ref, pq_ref, pk_ref, oq_ref, ok_ref,
             ov_ref, q_ref, k_ref, v_ref, y_ref, carry_ref, qn_sc, kn_sc):
        i = pl.program_id(0)

        @pl.when(i == 0)
        def _():
            carry_ref[...] = jnp.zeros_like(carry_ref)

        for h in range(FOX_HEADS):
            hs = slice(h * FOX_HD, (h + 1) * FOX_HD)
            qh = z_ref[:, Z_Q + h * FOX_HD:Z_Q + (h + 1) * FOX_HD]
            kh = z_ref[:, Z_K + h * FOX_HD:Z_K + (h + 1) * FOX_HD]
            qn_sc[:, hs] = (qh * _rstd(qh) * gq_ref[...] * 0.125).astype(BF)
            kn_sc[:, hs] = (kh * _rstd(kh) * gk_ref[...]).astype(BF)

        fl = z_ref[:, Z_F:Z_F + LANES] + bf_ref[...]
        logf = jnp.minimum(fl, 0.0) - jnp.log1p(jnp.exp(-jnp.abs(fl)))
        csum = _hi(_tri(tm, True).astype(F32), logf) + carry_ref[...]
        carry_ref[...] = csum[tm - 1:tm, :]
        sp = sp_ref[...]
        q_ref[...] = (_nn(qn_sc[...], sp) + _nn(_pieces(csum), pq_ref[...]) + oq_ref[...]).astype(BF)
        k_ref[...] = (_nn(kn_sc[...], sp) + _nn(_pieces(-csum), pk_ref[...]) + ok_ref[...]).astype(BF)
        v_ref[...] = (_nn(z_ref[:, Z_V:Z_V + FOX_W].astype(BF), sp) + ov_ref[...]).astype(BF)

        u = _gelu(z_ref[:, Z_U:Z_U + GMLP_W])
        vg = _gelu(z_ref[:, Z_G:Z_G + GMLP_W])
        vgn = (vg * _rstd(vg) * gs_ref[...]).astype(BF)
        mixed, _ = _spatial_mix(vgn, ws_ref, bst_ref[...], tm)
        sgu = u * mixed
        y_ref[...] = (sgu * _rstd(sgu) * go_ref[...]).astype(BF)

    row = lambda i: (i, 0)
    fix2 = lambda i: (0, 0)
    return pl.pallas_call(
        body, name="mix_prep", grid=(T // tm,),
        in_specs=[pl.BlockSpec((tm, ZW), row),
                  pl.BlockSpec((1, LANES), fix2), pl.BlockSpec((1, FOX_HD), fix2), pl.BlockSpec((1, FOX_HD), fix2),
                  pl.BlockSpec((1, GMLP_W), fix2), pl.BlockSpec((GMLP_G, CHUNK, CHUNK), lambda i: (0, 0, 0)),
                  pl.BlockSpec((CHUNK, GMLP_G), fix2), pl.BlockSpec((1, GMLP_W), fix2),
                  pl.BlockSpec((FOX_W, AUG_W), fix2), pl.BlockSpec((LANES, AUG_W), fix2),
                  pl.BlockSpec((LANES, AUG_W), fix2), pl.BlockSpec((1, AUG_W), fix2), pl.BlockSpec((1, AUG_W), fix2),
                  pl.BlockSpec((1, AUG_W), fix2)],
        out_specs=[pl.BlockSpec((tm, AUG_W), row), pl.BlockSpec((tm, AUG_W), row), pl.BlockSpec((tm, AUG_W), row),
                   pl.BlockSpec((tm, GMLP_W), row)],
        out_shape=[S((T, AUG_W), BF), S((T, AUG_W), BF), S((T, AUG_W), BF), S((T, GMLP_W), BF)],
        scratch_shapes=[pltpu.VMEM((1, LANES), F32), pltpu.VMEM((tm, FOX_W), BF), pltpu.VMEM((tm, FOX_W), BF)],
        compiler_params=_cp(1))(z, bf128, g_q, g_k, g_sgu, w_s, b_st, g_go, spread, pc_q, pc_k, one_q, one_k, one_v)


def _fox_fwd(q, k, v):
    T = q.shape[0]
    tq = _tile(T, 512)
    nq = T // tq

    def body(q_ref, k_ref, v_ref, o_ref, lse_ref, m_sc, acc_sc):
        i, j = pl.program_id(0), pl.program_id(1)

        @pl.when(j == 0)
        def _():
            m_sc[...] = jnp.full(m_sc.shape, NEG, F32)
            acc_sc[...] = jnp.zeros_like(acc_sc)

        def step(masked):
            mask = _tri(tq, True) if masked else None
            for h in range(FOX_HEADS):
                hb = slice(h * HB, (h + 1) * HB)
                s = _nt(q_ref[:, hb], k_ref[:, hb])
                if masked:
                    s = jnp.where(mask, s, NEG)
                m_prev = m_sc[h]
                m_new = jnp.maximum(m_prev, jnp.broadcast_to(jnp.max(s, axis=1, keepdims=True), (tq, HB)))
                p = jnp.exp(s - jnp.tile(m_new, (1, tq // HB))).astype(BF)
                acc_sc[:, hb] = jnp.exp(m_prev - m_new) * acc_sc[:, hb] + _nn(p, v_ref[:, hb])
                m_sc[h] = m_new

        @pl.when(j < i)
        def _():
            step(False)

        @pl.when(j == i)
        def _():
            step(True)
            lse_ref[...] = jnp.zeros_like(lse_ref)
            for h in range(FOX_HEADS):
                l = acc_sc[:, h * HB + COL_A:h * HB + COL_A + 1]
                o_ref[:, h * FOX_HD:(h + 1) * FOX_HD] = acc_sc[:, h * HB:h * HB + FOX_HD] / l
                lse_ref[:, h:h + 1] = m_sc[h][:, 0:1] + jnp.log(l)

    qi = lambda i, j: (i, 0)
    kj = lambda i, j: (jnp.minimum(i, j), 0)
    return pl.pallas_call(
        body, name="fox_fwd", grid=(nq, nq),
        in_specs=[pl.BlockSpec((tq, AUG_W), qi), pl.BlockSpec((tq, AUG_W), kj), pl.BlockSpec((tq, AUG_W), kj)],
        out_specs=[pl.BlockSpec((tq, FOX_W), qi), pl.BlockSpec((tq, LANES), qi)],
        out_shape=[S((T, FOX_W), F32), S((T, LANES), F32)],
        scratch_shapes=[pltpu.VMEM((FOX_HEADS, tq, HB), F32), pltpu.VMEM((tq, AUG_W), F32)],
        compiler_params=_cp(2))(q, k, v)


def _fox_bwd(q, k, v, dob):
    T = q.shape[0]
    tq = _tile(T, 512)
    nq = T // tq
    half = AUG_W // 2
    hpg = FOX_HEADS // 2

    def body(q_ref, k_ref, v_ref, do_ref, dq_ref, dk_ref, dv_ref, dq_sc):
        j, i = pl.program_id(1), pl.program_id(2)

        @pl.when(jnp.logical_and(i == 0, j == 0))
        def _():
            dq_sc[...] = jnp.zeros_like(dq_sc)

        @pl.when(i == 0)
        def _():
            dk_ref[...] = jnp.zeros_like(dk_ref)
            dv_ref[...] = jnp.zeros_like(dv_ref)

        def step(masked):
            rows = pl.ds(pl.multiple_of(i * tq, tq), tq)
            mask = _tri(tq, True) if masked else None
            for h in range(hpg):
                hb = slice(h * HB, (h + 1) * HB)
                qh, kh, vh, doh = q_ref[:, hb], k_ref[:, hb], v_ref[:, hb], do_ref[:, hb]
                s = _nt(qh, kh)
                if masked:
                    s = jnp.where(mask, s, NEG)
                p = jnp.exp(s)
                dsb = (p * _nt(doh, vh)).astype(BF)
                dv_ref[:, hb] += _tn(p.astype(BF), doh)
                dk_ref[:, hb] += _tn(dsb, qh)
                dq_sc[rows, hb] += _nn(dsb, kh)

        @pl.when(i > j)
        def _():
            step(False)

        @pl.when(i == j)
        def _():
            step(True)
            dq_ref[...] = dq_sc[pl.ds(pl.multiple_of(j * tq, tq), tq), :]

    qi = lambda g, j, i: (jnp.maximum(i, j), g)
    kj = lambda g, j, i: (j, g)
    return pl.pallas_call(
        body, name="fox_bwd", grid=(2, nq, nq),
        in_specs=[pl.BlockSpec((tq, half), qi), pl.BlockSpec((tq, half), kj), pl.BlockSpec((tq, half), kj),
                  pl.BlockSpec((tq, half), qi)],
        out_specs=[pl.BlockSpec((tq, half), kj), pl.BlockSpec((tq, half), kj), pl.BlockSpec((tq, half), kj)],
        out_shape=[S((T, AUG_W), F32), S((T, AUG_W), F32), S((T, AUG_W), F32)],
        scratch_shapes=[pltpu.VMEM((T, half), F32)],
        compiler_params=_cp(3))(q, k, v, dob)


def _mix_out(attn, yg, g_fo, wout, x):
    T, D = x.shape
    tm = _tile(T, 512)

    def body(a_ref, y_ref, g_ref, w_ref, x_ref, o_ref):
        at = a_ref[...]
        yf = (at * _rstd(at) * g_ref[...]).astype(BF)
        o_ref[...] = x_ref[...] + _nn(yf, w_ref[:FOX_W, :]) + _nn(y_ref[...], w_ref[FOX_W:, :])

    row = lambda i: (i, 0)
    return pl.pallas_call(
        body, name="mix_out", grid=(T // tm,),
        in_specs=[pl.BlockSpec((tm, FOX_W), row), pl.BlockSpec((tm, GMLP_W), row),
                  pl.BlockSpec((1, FOX_W), lambda i: (0, 0)), pl.BlockSpec((D, D), lambda i: (0, 0)),
                  pl.BlockSpec((tm, D), row)],
        out_specs=pl.BlockSpec((tm, D), row),
        out_shape=S((T, D), F32),
        compiler_params=_cp(1))(attn, yg, g_fo, wout, x)


def _mix_out_bwd(dx, attn, yg, g_fo, wout, qf, lse):
    T, D = dx.shape
    tm = _tile(T, 256)
    n = T // tm
    spread, pc_l, pc_d = _spread_matrix(), _piece_matrix(COL_C), _piece_matrix(COL_A)

    def body(dx_ref, a_ref, y_ref, g_ref, w_ref, qf_ref, lse_ref, sp_ref, pl_ref, pd_ref,
             qb_ref, dob_ref, dyg_ref, dw_ref, dg_ref, acc_ref, dsum_ref):
        i = pl.program_id(0)
        dxb = dx_ref[...].astype(BF)
        at = a_ref[...]
        yf = (at * _rstd(at) * g_ref[...]).astype(BF)
        dy = _nt(dxb, w_ref[...])
        p_top = _tn(yf, dxb)
        p_bot = _tn(y_ref[...], dxb)

        @pl.when(i == 0)
        def _():
            acc_ref[:FOX_W, :] = p_top
            acc_ref[FOX_W:, :] = p_bot

        @pl.when(i > 0)
        def _():
            acc_ref[:FOX_W, :] += p_top
            acc_ref[FOX_W:, :] += p_bot

        @pl.when(i == n - 1)
        def _():
            dw_ref[...] = acc_ref[...].astype(BF)

        dat, dgr = _norm_bwd(dy[:, :FOX_W], at, g_ref[...])
        _acc_rows(dg_ref, i == 0, dgr)
        dyg_ref[...] = dy[:, FOX_W:]
        prod = dat * at
        dsum_ref[...] = jnp.zeros_like(dsum_ref)
        for h in range(FOX_HEADS):
            dsum_ref[:, h:h + 1] = jnp.sum(prod[:, h * FOX_HD:(h + 1) * FOX_HD], axis=1, keepdims=True)
        dob_ref[...] = (_nn(dat.astype(BF), sp_ref[...]) + _nn(_pieces(-dsum_ref[...]), pd_ref[...])).astype(BF)
        qb_ref[...] = (qf_ref[...].astype(F32) + _nn(_pieces(-lse_ref[...]), pl_ref[...])).astype(BF)

    row = lambda i: (i, 0)
    fix = lambda i: (0, 0)
    return pl.pallas_call(
        body, name="mix_out_bwd", grid=(n,),
        in_specs=[pl.BlockSpec((tm, D), row), pl.BlockSpec((tm, FOX_W), row), pl.BlockSpec((tm, GMLP_W), row),
                  pl.BlockSpec((1, FOX_W), fix), pl.BlockSpec((D, D), fix), pl.BlockSpec((tm, AUG_W), row),
                  pl.BlockSpec((tm, LANES), row), pl.BlockSpec((FOX_W, AUG_W), fix), pl.BlockSpec((LANES, AUG_W), fix),
                  pl.BlockSpec((LANES, AUG_W), fix)],
        out_specs=[pl.BlockSpec((tm, AUG_W), row), pl.BlockSpec((tm, AUG_W), row), pl.BlockSpec((tm, GMLP_W), row),
                   pl.BlockSpec((D, D), fix), pl.BlockSpec((1, FOX_W), fix)],
        out_shape=[S((T, AUG_W), BF), S((T, AUG_W), BF), S((T, GMLP_W), F32), S((D, D), BF), S((1, FOX_W), F32)],
        scratch_shapes=[pltpu.VMEM((D, D), F32), pltpu.VMEM((tm, LANES), F32)],
        compiler_params=_cp(1))(dx, attn, yg, g_fo, wout, qf, lse, spread, pc_l, pc_d)


def _mix_prep_bwd(z, dq, dk, dv, dyg, bf128, g_q, g_k, g_sgu, w_s, b_st, g_go):
    T = z.shape[0]
    tm = _tile(T, 256)
    n = T // tm

    def body(z_ref, dq_ref, dk_ref, dv_ref, dyg_ref, bf_ref, gq_ref, gk_ref, gs_ref, ws_ref,
             bst_ref, go_ref, dz_ref, dgq_ref, dgk_ref, dgs_ref, dgo_ref, dws_ref, dbst_ref, dbf_ref, carry_ref):
        i = pl.program_id(0)
        first = i == 0

        @pl.when(first)
        def _():
            carry_ref[...] = jnp.zeros_like(carry_ref)

        lane = lax.broadcasted_iota(jnp.int32, (tm, LANES), 1)
        dc = jnp.zeros((tm, LANES), F32)
        gq_rows, gk_rows = [], []
        for h in range(FOX_HEADS):
            hp = slice(h * HB, h * HB + FOX_HD)
            dqh, gqr = _norm_bwd(dq_ref[:, hp] * 0.125, z_ref[:, Z_Q + h * FOX_HD:Z_Q + (h + 1) * FOX_HD], gq_ref[...])
            dkh, gkr = _norm_bwd(dk_ref[:, hp], z_ref[:, Z_K + h * FOX_HD:Z_K + (h + 1) * FOX_HD], gk_ref[...])
            dz_ref[:, Z_Q + h * FOX_HD:Z_Q + (h + 1) * FOX_HD] = dqh.astype(BF)
            dz_ref[:, Z_K + h * FOX_HD:Z_K + (h + 1) * FOX_HD] = dkh.astype(BF)
            dz_ref[:, Z_V + h * FOX_HD:Z_V + (h + 1) * FOX_HD] = dv_ref[:, hp].astype(BF)
            dch = dq_ref[:, h * HB + COL_A:h * HB + COL_A + 1] - dk_ref[:, h * HB + COL_B:h * HB + COL_B + 1]
            dc = jnp.where(lane == h, dch, dc)
            gq_rows.append(gqr)
            gk_rows.append(gkr)
        _acc_rows(dgq_ref, first, functools.reduce(lambda a, b: a + b, gq_rows))
        _acc_rows(dgk_ref, first, functools.reduce(lambda a, b: a + b, gk_rows))

        dlogf = _hi(_tri(tm, False).astype(F32), dc) + carry_ref[...]
        carry_ref[...] = dlogf[0:1, :]
        fl = z_ref[:, Z_F:Z_F + LANES] + bf_ref[...]
        lane = lax.broadcasted_iota(jnp.int32, (tm, LANES), 1)
        df = jnp.where(lane < FOX_HEADS, dlogf * jax.nn.sigmoid(-fl), 0.0)
        dz_ref[:, Z_F:Z_F + LANES] = df.astype(BF)
        _acc_rows(dbf_ref, first, df)

        u_pre = z_ref[:, Z_U:Z_U + GMLP_W]
        vg_pre = z_ref[:, Z_G:Z_G + GMLP_W]
        u = _gelu(u_pre)
        vg = _gelu(vg_pre)
        vgn = (vg * _rstd(vg) * gs_ref[...]).astype(BF)
        bst = bst_ref[...]
        mixed, wms = _spatial_mix(vgn, ws_ref, bst, tm)
        sgu = u * mixed
        dsgu, gor = _norm_bwd(dyg_ref[...], sgu, go_ref[...])
        _acc_rows(dgo_ref, first, gor)
        du = dsgu * mixed
        dmixed = dsgu * u
        dmb = dmixed.astype(BF)
        tril = _tri(CHUNK, True)
        dvgn_rows = []
        dws = [None] * GMLP_G
        dbs = [None] * GMLP_G
        for c in range(tm // CHUNK):
            cs = slice(c * CHUNK, (c + 1) * CHUNK)
            cols = []
            for g in range(GMLP_G):
                gs = slice(g * GMLP_GD, (g + 1) * GMLP_GD)
                dmc = dmb[cs, gs]
                pw = _nt(dmc, vgn[cs, gs])
                pb = jnp.sum(dmixed[cs, gs], axis=1, keepdims=True)
                dws[g] = pw if dws[g] is None else dws[g] + pw
                dbs[g] = pb if dbs[g] is None else dbs[g] + pb
                cols.append(_tn(wms[g], dmc))
            dvgn_rows.append(jnp.concatenate(cols, axis=1))
        dvgn = jnp.concatenate(dvgn_rows, axis=0)
        dbs_t = jnp.concatenate(dbs, axis=1)
        for g in range(GMLP_G):
            dwg = jnp.where(tril, dws[g], 0.0)

            @pl.when(first)
            def _():
                dws_ref[g] = dwg

            @pl.when(jnp.logical_not(first))
            def _():
                dws_ref[g] += dwg

        @pl.when(first)
        def _():
            dbst_ref[...] = dbs_t

        @pl.when(jnp.logical_not(first))
        def _():
            dbst_ref[...] += dbs_t

        dvg, gsr = _norm_bwd(dvgn, vg, gs_ref[...])
        _acc_rows(dgs_ref, first, gsr)
        dz_ref[:, Z_U:Z_U + GMLP_W] = (du * _gelu_grad(u_pre)).astype(BF)
        dz_ref[:, Z_G:Z_G + GMLP_W] = (dvg * _gelu_grad(vg_pre)).astype(BF)

    rev = lambda i: (n - 1 - i, 0)
    fix = lambda i: (0, 0)
    fix3 = lambda i: (0, 0, 0)
    return pl.pallas_call(
        body, name="mix_prep_bwd", grid=(n,),
        in_specs=[pl.BlockSpec((tm, ZW), rev), pl.BlockSpec((tm, AUG_W), rev), pl.BlockSpec((tm, AUG_W), rev),
                  pl.BlockSpec((tm, AUG_W), rev), pl.BlockSpec((tm, GMLP_W), rev),
                  pl.BlockSpec((1, LANES), fix), pl.BlockSpec((1, FOX_HD), fix), pl.BlockSpec((1, FOX_HD), fix),
                  pl.BlockSpec((1, GMLP_W), fix), pl.BlockSpec((GMLP_G, CHUNK, CHUNK), fix3),
                  pl.BlockSpec((CHUNK, GMLP_G), fix), pl.BlockSpec((1, GMLP_W), fix)],
        out_specs=[pl.BlockSpec((tm, ZW), rev), pl.BlockSpec((1, FOX_HD), fix), pl.BlockSpec((1, FOX_HD), fix),
                   pl.BlockSpec((1, GMLP_W), fix), pl.BlockSpec((1, GMLP_W), fix),
                   pl.BlockSpec((GMLP_G, CHUNK, CHUNK), fix3), pl.BlockSpec((CHUNK, GMLP_G), fix),
                   pl.BlockSpec((1, LANES), fix)],
        out_shape=[S((T, ZW), BF), S((1, FOX_HD), F32), S((1, FOX_HD), F32), S((1, GMLP_W), F32), S((1, GMLP_W), F32),
                   S((GMLP_G, CHUNK, CHUNK), F32), S((CHUNK, GMLP_G), F32), S((1, LANES), F32)],
        scratch_shapes=[pltpu.VMEM((1, LANES), F32)],
        compiler_params=_cp(1))(z, dq, dk, dv, dyg, bf128, g_q, g_k, g_sgu, w_s, b_st, g_go)


def _mix_proj_bwd(dz, wz, x, g, dy):
    T, D = x.shape
    tm = _tile(T, 512)

    def body(dz_ref, w_ref, x_ref, g_ref, dy_ref, dx_ref, dxb_ref, dg_ref):
        dh = _nn(dz_ref[...], w_ref[...])
        dx, dgr = _norm_bwd(dh, x_ref[...], g_ref[...])
        dx = dx + dy_ref[...]
        dx_ref[...] = dx
        dxb_ref[...] = dx.astype(BF)
        _acc_rows(dg_ref, pl.program_id(0) == 0, dgr)

    row = lambda i: (i, 0)
    fix = lambda i: (0, 0)
    return pl.pallas_call(
        body, name="mix_proj_bwd", grid=(T // tm,),
        in_specs=[pl.BlockSpec((tm, ZW), row), pl.BlockSpec((ZW, D), fix), pl.BlockSpec((tm, D), row),
                  pl.BlockSpec((1, D), fix), pl.BlockSpec((tm, D), row)],
        out_specs=[pl.BlockSpec((tm, D), row), pl.BlockSpec((tm, D), row), pl.BlockSpec((1, D), fix)],
        out_shape=[S((T, D), F32), S((T, D), BF), S((1, D), F32)],
        compiler_params=_cp(1))(dz, wz, x, g, dy)


def _ca_kv(mem, g_mem, wckv, g_ck):
    M, D = mem.shape

    def body(m_ref, g_ref, w_ref, gk_ref, mn_ref, kr_ref, kn_ref, v_ref):
        mf = m_ref[...]
        mn = (mf * _rstd(mf) * g_ref[...]).astype(BF)
        mn_ref[...] = mn
        for h in range(CA_HEADS):
            kr = _nn(mn, w_ref[h])
            kr_ref[h] = kr
            kn_ref[h] = (kr * _rstd(kr) * gk_ref[...]).astype(BF)
            v_ref[h] = _nn(mn, w_ref[CA_HEADS + h]).astype(BF)

    hd = (CA_HEADS, M, CA_HD)
    return pl.pallas_call(
        body, name="ca_kv", out_shape=[S((M, D), BF), S(hd, F32), S(hd, BF), S(hd, BF)],
        compiler_params=pltpu.CompilerParams(vmem_limit_bytes=VMEM_LIMIT))(mem, g_mem, wckv, g_ck)


def _ca_tile_fwd(xt, gca, wcq, gcq, kn_ref, v_ref):
    hb = (xt * _rstd(xt) * gca).astype(BF)
    qc = _nn(hb, wcq)
    qr, qn, ps = [], [], []
    for h in range(CA_HEADS):
        qh = qc[:, h * CA_HD:(h + 1) * CA_HD]
        qnh = (qh * _rstd(qh) * gcq * 0.0625).astype(BF)
        s = _nt(qnh, kn_ref[h])
        e = jnp.exp(s - jnp.max(s, axis=1, keepdims=True))
        ps.append(e / jnp.sum(e, axis=1, keepdims=True))
        qr.append(qh)
        qn.append(qnh)
    return hb, qr, qn, ps


def _ca_fwd(x, g_ca, wcq, g_cq, kn, vv, wco):
    T, D = x.shape
    M = kn.shape[1]
    tm = _tile(T, 256)

    def body(x_ref, gca_ref, wcq_ref, gcq_ref, kn_ref, v_ref, wco_ref, o_ref, ob_sc):
        xt = x_ref[...]
        _, _, _, ps = _ca_tile_fwd(xt, gca_ref[...], wcq_ref[...], gcq_ref[...], kn_ref, v_ref)
        for h in range(CA_HEADS):
            ob_sc[:, h * CA_HD:(h + 1) * CA_HD] = _nn(ps[h].astype(BF), v_ref[h]).astype(BF)
        o_ref[...] = xt + _nn(ob_sc[...], wco_ref[...])

    row = lambda i: (i, 0)
    fix = lambda i: (0, 0)
    fix3 = lambda i: (0, 0, 0)
    return pl.pallas_call(
        body, name="ca_fwd", grid=(T // tm,),
        in_specs=[pl.BlockSpec((tm, D), row), pl.BlockSpec((1, D), fix), pl.BlockSpec((D, D), fix),
                  pl.BlockSpec((1, CA_HD), fix), pl.BlockSpec((CA_HEADS, M, CA_HD), fix3),
                  pl.BlockSpec((CA_HEADS, M, CA_HD), fix3), pl.BlockSpec((D, D), fix)],
        out_specs=pl.BlockSpec((tm, D), row), out_shape=S((T, D), F32),
        scratch_shapes=[pltpu.VMEM((tm, D), BF)],
        compiler_params=_cp(1))(x, g_ca, wcq, g_cq, kn, vv, wco)


def _ca_bwd(x, dy, g_ca, wcq, g_cq, kn, vv, wco):
    T, D = x.shape
    M = kn.shape[1]
    tm = _tile(T, 256)
    n = T // tm

    def body(x_ref, dy_ref, gca_ref, wcq_ref, gcq_ref, kn_ref, v_ref, wco_ref,
             dx_ref, dwq_ref, dwo_ref, dkn_ref, dv_ref, dgcq_ref, dgca_ref, aq_sc, ao_sc, ob_sc, dq_sc):
        i = pl.program_id(0)
        first = i == 0
        xt = x_ref[...]
        dyt = dy_ref[...]
        dyb = dyt.astype(BF)
        hb, qr, qn, ps = _ca_tile_fwd(xt, gca_ref[...], wcq_ref[...], gcq_ref[...], kn_ref, v_ref)
        do = _nt(dyb, wco_ref[...])
        gcq_rows = None
        for h in range(CA_HEADS):
            hs = slice(h * CA_HD, (h + 1) * CA_HD)
            p = ps[h]
            pb = p.astype(BF)
            ob_sc[:, hs] = _nn(pb, v_ref[h]).astype(BF)
            doh = do[:, hs].astype(BF)
            dp = _nt(doh, v_ref[h])
            ds = (p * (dp - jnp.sum(dp * p, axis=1, keepdims=True))).astype(BF)
            dvh = _tn(pb, doh)
            dkh = _tn(ds, qn[h])

            @pl.when(first)
            def _():
                dv_ref[h] = dvh
                dkn_ref[h] = dkh

            @pl.when(jnp.logical_not(first))
            def _():
                dv_ref[h] += dvh
                dkn_ref[h] += dkh

            dqn = _nn(ds, kn_ref[h]) * 0.0625
            dqh, gr = _norm_bwd(dqn, qr[h], gcq_ref[...])
            gcq_rows = gr if gcq_rows is None else gcq_rows + gr
            dq_sc[:, hs] = dqh.astype(BF)
        _acc_rows(dgcq_ref, first, gcq_rows)
        dqb = dq_sc[...]
        p_o = _tn(ob_sc[...], dyb)
        p_q = _tn(hb, dqb)

        @pl.when(first)
        def _():
            ao_sc[...] = p_o
            aq_sc[...] = p_q

        @pl.when(jnp.logical_not(first))
        def _():
            ao_sc[...] += p_o
            aq_sc[...] += p_q

        @pl.when(i == n - 1)
        def _():
            dwo_ref[...] = ao_sc[...].astype(BF)
            dwq_ref[...] = aq_sc[...].astype(BF)

        dh = _nt(dqb, wcq_ref[...])
        dx, gar = _norm_bwd(dh, xt, gca_ref[...])
        dx_ref[...] = dx + dyt
        _acc_rows(dgca_ref, first, gar)

    row = lambda i: (i, 0)
    fix = lambda i: (0, 0)
    fix3 = lambda i: (0, 0, 0)
    hd = (CA_HEADS, M, CA_HD)
    return pl.pallas_call(
        body, name="ca_bwd", grid=(n,),
        in_specs=[pl.BlockSpec((tm, D), row), pl.BlockSpec((tm, D), row), pl.BlockSpec((1, D), fix),
                  pl.BlockSpec((D, D), fix), pl.BlockSpec((1, CA_HD), fix), pl.BlockSpec(hd, fix3),
                  pl.BlockSpec(hd, fix3), pl.BlockSpec((D, D), fix)],
        out_specs=[pl.BlockSpec((tm, D), row), pl.BlockSpec((D, D), fix), pl.BlockSpec((D, D), fix),
                   pl.BlockSpec(hd, fix3), pl.BlockSpec(hd, fix3), pl.BlockSpec((1, CA_HD), fix),
                   pl.BlockSpec((1, D), fix)],
        out_shape=[S((T, D), F32), S((D, D), BF), S((D, D), BF), S(hd, F32), S(hd, F32), S((1, CA_HD), F32),
                   S((1, D), F32)],
        scratch_shapes=[pltpu.VMEM((D, D), F32), pltpu.VMEM((D, D), F32), pltpu.VMEM((tm, D), BF),
                        pltpu.VMEM((tm, D), BF)],
        compiler_params=_cp(1))(x, dy, g_ca, wcq, g_cq, kn, vv, wco)


def _ca_kv_bwd(mem, g_mem, mn, kraw, dkn, dvv, wckv, g_ck):
    M, D = mem.shape

    def body(m_ref, g_ref, mn_ref, kr_ref, dkn_ref, dv_ref, w_ref, gk_ref, dw_ref, dgk_ref, dgm_ref):
        mn = mn_ref[...]
        dmn = jnp.zeros((M, D), F32)
        gk_rows = None
        for h in range(CA_HEADS):
            dkr, gr = _norm_bwd(dkn_ref[h], kr_ref[h], gk_ref[...])
            gk_rows = gr if gk_rows is None else gk_rows + gr
            dkb = dkr.astype(BF)
            dvb = dv_ref[h].astype(BF)
            dw_ref[h] = _tn(mn, dkb).astype(BF)
            dw_ref[CA_HEADS + h] = _tn(mn, dvb).astype(BF)
            dmn = dmn + _nt(dkb, w_ref[h]) + _nt(dvb, w_ref[CA_HEADS + h])
        dgk_ref[...] = jnp.sum(gk_rows, axis=0, keepdims=True)
        mf = m_ref[...]
        dgm_ref[...] = jnp.sum(dmn * (mf * _rstd(mf)), axis=0, keepdims=True)

    return pl.pallas_call(
        body, name="ca_kv_bwd",
        out_shape=[S((2 * CA_HEADS, D, CA_HD), BF), S((1, CA_HD), F32), S((1, D), F32)],
        compiler_params=pltpu.CompilerParams(vmem_limit_bytes=VMEM_LIMIT))(mem, g_mem, mn, kraw, dkn, dvv, wckv, g_ck)


def _after(g, token):
    return g if token is None else g + token[0:1, 0:1]


def _local_step(x, mem, target, small, weights, emit):
    T, D = x.shape
    p = small
    bf128 = jnp.pad(p["b_f"], ((0, 0), (0, LANES - FOX_HEADS)))
    b_st = p["b_s"].T

    wup1 = weights("ffn1_up", x)["wup1"]
    a1, h1 = _ffn_up("ffn1_up", x, p["g_ffn1"], wup1)
    wdn1 = weights("ffn1_dn", h1)["wdn1"]
    x1 = _ffn_down("ffn1_down", a1, wdn1, x)
    wm = weights("mix", x1)
    z, h2 = _mix_proj(x1, p["g_mix"], wm["wz"])
    qf, ka, va, yg = _mix_prep(z, bf128, p["g_q"], p["g_k"], p["g_sgu"], p["w_s"], b_st, p["g_gmlp_o"])
    attn, lse = _fox_fwd(qf, ka, va)
    x2 = _mix_out(attn, yg, p["g_fox_o"], wm["wout"], x1)
    wc = weights("ca", x2)
    mn, kraw, ckn, cvv = _ca_kv(mem, p["g_mem"], wc["wckv"], p["g_ck"])
    x3 = _ca_fwd(x2, p["g_ca"], wc["wcq"], p["g_cq"], ckn, cvv, wc["wco"])
    w2 = weights("ffn2", x3)
    a2, h4 = _ffn_up("ffn2_up", x3, p["g_ffn2"], w2["wup2"])
    dy4, dy4b, sq = _ffn_down_loss("ffn2_down", a2, w2["wdn2"], x3, target)

    gs = {}
    dgu2 = _ffn_bwd_act("ffn2_bwd_act", dy4b, h4, w2["wup2"], w2["wdn2"])
    dwup2, dwdn2 = _ffn_dw("ffn2", h4, dgu2, a2, dy4b)
    tok = emit("ffn2", {"wup2": dwup2, "wdn2": dwdn2})
    dx3, gs["g_ffn2"] = _ffn_dx("ffn2_dx", dgu2, w2["wup2"], x3, _after(p["g_ffn2"], tok), dy4)

    dx2, dwcq, dwco, dckn, dcvv, gs["g_cq"], gs["g_ca"] = _ca_bwd(
        x2, dx3, p["g_ca"], wc["wcq"], p["g_cq"], ckn, cvv, wc["wco"])
    dwckv, gs["g_ck"], gs["g_mem"] = _ca_kv_bwd(mem, p["g_mem"], mn, kraw, dckn, dcvv, wc["wckv"], p["g_ck"])

    qb, dob, dyg, dwout, gs["g_fox_o"] = _mix_out_bwd(dx2, attn, yg, p["g_fox_o"], wm["wout"], qf, lse)
    dq, dk, dv = _fox_bwd(qb, ka, va, dob)
    dz, gs["g_q"], gs["g_k"], gs["g_sgu"], gs["g_gmlp_o"], gs["w_s"], dbst, dbf = _mix_prep_bwd(
        z, dq, dk, dv, dyg, bf128, p["g_q"], p["g_k"], p["g_sgu"], p["w_s"], b_st, p["g_gmlp_o"])
    gs["b_s"] = dbst.T
    gs["b_f"] = dbf[:, :FOX_HEADS]
    tk = _tile(T, 1024)
    zb = ZW // 3
    dwz = _tn_matmul(
        "mix_dwz", dz, pl.BlockSpec((tk, zb), lambda j, k: (k, j)), h2, pl.BlockSpec((tk, D), lambda j, k: (k, 0)),
        S((ZW, D), F32), pl.BlockSpec((zb, D), lambda j, k: (j, 0)), (3, T // tk), (zb, D))
    tok = emit("mid", {"wcq": dwcq, "wco": dwco, "wckv": dwckv, "wout": dwout, "wz": dwz})
    dx1, dx1b, gs["g_mix"] = _mix_proj_bwd(dz, wm["wz"], x1, _after(p["g_mix"], tok), dx2)

    dgu1 = _ffn_bwd_act("ffn1_bwd_act", dx1b, h1, wup1, wdn1)
    dwup1, dwdn1 = _ffn_dw("ffn1", h1, dgu1, a1, dx1b)
    tok = emit("ffn1", {"wup1": dwup1, "wdn1": dwdn1})
    dx0, gs["g_ffn1"] = _ffn_dx("ffn1_dx", dgu1, wup1, x, _after(p["g_ffn1"], tok), dx1)
    return sq, dx0, gs


MESH = pl.DeviceIdType.MESH
HBM_SPEC = pl.BlockSpec(memory_space=pltpu.HBM)
N_PEER = N_DEV - 1


def _place():
    return lax.axis_index("x"), lax.axis_index("y"), lax.axis_index("c")


def _slot(px, py, pc):
    return 4 * px + 2 * py + pc


SEM_SPEC = pl.BlockSpec(memory_space=pltpu.SEMAPHORE)
ANY_SPEC = pl.BlockSpec(memory_space=pl.ANY)
DATAFLOW = pltpu.SideEffectType.DATAFLOW_SIDE_EFFECTING


def _hbm(a):
    return pltpu.with_memory_space_constraint(a, pltpu.HBM)


def _peer(x, y, c, r):
    return (1 - x if r & 4 else x, 1 - y if r & 2 else y, 1 - c if r & 1 else c)


def _place_own(srcs, whole):
    my = _slot(*_place())
    lands = []
    for s in srcs:
        blk = s[None] if whole else lax.dynamic_slice_in_dim(s, my, 1, 0)
        shape = (N_DEV,) + s.shape if whole else s.shape
        lands.append(lax.dynamic_update_slice_in_dim(lax.empty(shape, s.dtype), blk, my, 0))
    return lands


def _copy_start(name, srcs, lands, whole):
    n = len(srcs)

    def body(*refs):
        src, land = refs[:n], refs[n:2 * n]
        send, recv = refs[2 * n:3 * n], refs[3 * n:4 * n]
        token = refs[6 * n]
        x, y, c = _place()
        my = _slot(x, y, c)
        for a in range(n):
            for r in range(1, N_DEV):
                p = _peer(x, y, c, r)
                pltpu.make_async_remote_copy(
                    src_ref=src[a] if whole else src[a].at[_slot(*p)], dst_ref=land[a].at[my],
                    send_sem=send[a].at[r - 1], recv_sem=recv[a].at[r - 1], device_id=p, device_id_type=MESH).start()
        token[...] = jnp.zeros_like(token)

    out = pl.pallas_call(
        body, name=name,
        out_shape=([pltpu.SemaphoreType.DMA((N_PEER,))] * (2 * n)
                   + [pltpu.HBM(s.shape, s.dtype) for s in srcs] + [pltpu.HBM(s.shape, s.dtype) for s in lands]
                   + [S((8, LANES), F32)]),
        in_specs=[HBM_SPEC] * (2 * n),
        out_specs=[SEM_SPEC] * (2 * n) + [HBM_SPEC] * (2 * n) + [pl.BlockSpec(memory_space=pltpu.VMEM)],
        input_output_aliases={i: 2 * n + i for i in range(2 * n)},
        compiler_params=pltpu.CompilerParams(has_side_effects=DATAFLOW),
    )(*[_hbm(s) for s in srcs], *[_hbm(s) for s in lands])
    return out[:n], out[n:2 * n], out[2 * n:3 * n], out[3 * n:4 * n], out[4 * n]


def _copy_wait(name, srcs, lands, send, recv, after, whole):
    n = len(srcs)

    def body(*refs):
        src, land = refs[:n], refs[n:2 * n]
        snd, rcv = refs[2 * n:3 * n], refs[3 * n:4 * n]
        x, y, c = _place()
        for a in range(n):
            for r in range(1, N_DEV):
                p = _peer(x, y, c, r)
                ps = _slot(*p)
                cp = pltpu.make_async_remote_copy(
                    src_ref=src[a] if whole else src[a].at[ps], dst_ref=land[a].at[ps],
                    send_sem=snd[a].at[r - 1], recv_sem=rcv[a].at[r - 1], device_id=p, device_id_type=MESH)
                cp.wait_send()
                cp.wait_recv()

    out = pl.pallas_call(
        body, name=name,
        out_shape=[pltpu.HBM(s.shape, s.dtype) for s in srcs] + [pltpu.HBM(s.shape, s.dtype) for s in lands],
        in_specs=[HBM_SPEC] * (2 * n) + [SEM_SPEC] * (2 * n) + [ANY_SPEC],
        out_specs=[HBM_SPEC] * (2 * n),
        input_output_aliases={i: i for i in range(2 * n)},
        compiler_params=pltpu.CompilerParams(has_side_effects=DATAFLOW),
    )(*srcs, *lands, *send, *recv, after)
    return out[n:]


def _adamw(w, g, m, v):
    m2 = ADAM_B1 * m + (1.0 - ADAM_B1) * g
    v2 = ADAM_B2 * v + (1.0 - ADAM_B2) * (g * g)
    m_hat = m2 / (1.0 - ADAM_B1 ** ADAM_STEP)
    v_hat = v2 / (1.0 - ADAM_B2 ** ADAM_STEP)
    delta = -ADAM_LR * (m_hat / (jnp.sqrt(v_hat) + ADAM_EPS) + ADAM_WD * w)
    return delta, m2, v2


def _adamw_big(name, slots, w, m, v):
    R, C = w.shape
    tr = next((t for t in (256, 352) if R % t == 0), R)

    def body(s_ref, w_ref, m_ref, v_ref, g_ref, d_ref, m2_ref, v2_ref):
        g = s_ref[0].astype(F32)
        for k in range(1, N_DEV):
            g = g + s_ref[k].astype(F32)
        d, m2, v2 = _adamw(w_ref[...], g, m_ref[...], v_ref[...])
        g_ref[...] = g
        d_ref[...] = d
        m2_ref[...] = m2
        v2_ref[...] = v2

    row = pl.BlockSpec((tr, C), lambda i: (i, 0))
    return pl.pallas_call(
        body, name=name, grid=(R // tr,),
        in_specs=[pl.BlockSpec((N_DEV, tr, C), lambda i: (0, i, 0)), row, row, row],
        out_specs=[row] * 4, out_shape=[S((R, C), F32)] * 4,
        compiler_params=_cp(1))(slots, w, m, v)


TINY_ROWS = (("b_s", 8), ("g_ffn1", 8), ("g_mix", 8), ("g_ca", 8), ("g_mem", 8), ("g_ffn2", 8), ("g_sgu", 4),
             ("g_fox_o", 4), ("g_gmlp_o", 4), ("g_cq", 2), ("g_ck", 2), ("g_q", 1), ("g_k", 1), ("b_f", 1))
TINY_P = 72


def _pack_tiny(d):
    rows = []
    for name, r in TINY_ROWS:
        flat = d[name].reshape(-1)
        rows.append(jnp.pad(flat, (0, r * LANES - flat.shape[0])).reshape(r, LANES))
    used = sum(r for _, r in TINY_ROWS)
    rows.append(jnp.zeros((TINY_P - used, LANES), F32))
    return jnp.concatenate(rows, axis=0)


def _unpack_tiny(packed, shapes):
    out, at = {}, 0
    for name, r in TINY_ROWS:
        shape = shapes[name]
        size = 1
        for s in shape:
            size *= s
        out[name] = packed[at:at + r].reshape(-1)[:size].reshape(shape)
        at += r
    return out


WEIGHTS =('g_ffn1', 'w_ffn1_in', 'w_ffn1_out', 'g_mix', 'w_in', 'b_f', 'g_q', 'g_k', 'g_sgu', 'w_s', 'b_s',
           'g_fox_o', 'g_gmlp_o', 'w_out', 'g_ca', 'g_mem', 'w_cq', 'w_ckv', 'g_cq', 'g_ck', 'w_co', 'g_ffn2',
           'w_ffn2_in', 'w_ffn2_out')
BIG = ('w_ffn1_in', 'w_ffn1_out', 'w_in', 'w_out', 'w_cq', 'w_ckv', 'w_co', 'w_ffn2_in', 'w_ffn2_out')
TRANSPOSED = ('w_ffn1_in', 'w_in', 'w_ffn2_in')
GATHER_GROUPS = {"ffn1_up": ("w_ffn1_in",), "ffn1_dn": ("w_ffn1_out",), "mix": ("w_in", "w_out"),
                 "ca": ("w_cq", "w_ckv", "w_co"), "ffn2": ("w_ffn2_in", "w_ffn2_out")}
GATHER_STAGES = ((None, ("w_ffn1_in", "w_ffn1_out")),
                 ("ffn1_up", ("w_in", "w_out", "w_cq", "w_ckv", "w_co", "w_ffn2_in", "w_ffn2_out")))
QKV_W = 3 * FOX_W
UV_OFF = QKV_W + FOX_HEADS


def kernel(x, mem, g_ffn1, w_ffn1_in, w_ffn1_out, g_mix, w_in, b_f, g_q, g_k, g_sgu, w_s, b_s, g_fox_o, g_gmlp_o, w_out, g_ca, g_mem, w_cq, w_ckv, g_cq, g_ck, w_co, g_ffn2, w_ffn2_in, w_ffn2_out, loss_target, m_g_ffn1, m_w_ffn1_in, m_w_ffn1_out, m_g_mix, m_w_in, m_b_f, m_g_q, m_g_k, m_g_sgu, m_w_s, m_b_s, m_g_fox_o, m_g_gmlp_o, m_w_out, m_g_ca, m_g_mem, m_w_cq, m_w_ckv, m_g_cq, m_g_ck, m_w_co, m_g_ffn2, m_w_ffn2_in, m_w_ffn2_out, v_g_ffn1, v_w_ffn1_in, v_w_ffn1_out, v_g_mix, v_w_in, v_b_f, v_g_q, v_g_k, v_g_sgu, v_w_s, v_b_s, v_g_fox_o, v_g_gmlp_o, v_w_out, v_g_ca, v_g_mem, v_w_cq, v_w_ckv, v_g_cq, v_g_ck, v_w_co, v_g_ffn2, v_w_ffn2_in, v_w_ffn2_out):
    args = dict(locals())
    w = {n: args[n] for n in WEIGHTS}
    mo = {n: args["m_" + n] for n in WEIGHTS}
    vo = {n: args["v_" + n] for n in WEIGHTS}
    D = D_MODEL

    def local(n, a):
        return a[0].T if n in TRANSPOSED else a[0]

    shards = {n: local(n, w[n]).astype(BF) for n in BIG}
    fb = shards["w_ffn1_in"].shape[0]
    handles = {}

    def start_gather(stage, names, arrays):
        snd, rcv, src, land, _ = _copy_start("gather_start_%d" % stage, arrays, _place_own(arrays, True), True)
        for i, n in enumerate(names):
            handles[n] = (src[i], land[i], snd[i], rcv[i])

    start_gather(0, GATHER_STAGES[0][1], [shards[n] for n in GATHER_STAGES[0][1]])

    def weights(group, after):
        names = GATHER_GROUPS[group]
        hs = [handles[n] for n in names]
        got = _copy_wait("gather_wait_" + group, [h[0] for h in hs], [h[1] for h in hs], [h[2] for h in hs],
                         [h[3] for h in hs], after, True)
        for stage, (trigger, members) in enumerate(GATHER_STAGES):
            if trigger == group:
                held = lax.optimization_barrier((tuple(shards[n] for n in members), got[0]))[0]
                start_gather(stage, members, list(held))
        got = dict(zip(names, got))
        if group == "ffn1_up":
            return {"wup1": got["w_ffn1_in"].reshape(2, N_FFN_BLK, fb, D)}
        if group == "ffn1_dn":
            return {"wdn1": got["w_ffn1_out"].reshape(N_FFN_BLK, fb, D)}
        if group == "mix":
            full = got["w_in"].reshape(-1, D)
            wz = jnp.concatenate([full[:QKV_W], full[UV_OFF:], full[QKV_W:UV_OFF],
                                  jnp.zeros((LANES - FOX_HEADS, D), BF)], axis=0)
            return {"wz": wz, "wout": got["w_out"].reshape(D, D)}
        if group == "ca":
            return {"wcq": got["w_cq"].reshape(D, D), "wco": got["w_co"].reshape(D, D), "wckv": got["w_ckv"]}
        return {"wup2": got["w_ffn2_in"].reshape(2, N_FFN_BLK, fb, D),
                "wdn2": got["w_ffn2_out"].reshape(N_FFN_BLK, fb, D)}

    flying = {}

    def emit(group, g):
        if group == "ffn2":
            parts = {"w_ffn2_in": g["wup2"], "w_ffn2_out": g["wdn2"].reshape(N_DEV, -1, D)}
        elif group == "ffn1":
            parts = {"w_ffn1_in": g["wup1"], "w_ffn1_out": g["wdn1"].reshape(N_DEV, -1, D)}
        else:
            gz = g["wz"]
            g_in = jnp.concatenate([gz[:QKV_W], gz[Z_F:Z_F + FOX_HEADS], gz[QKV_W:Z_F]], axis=0)
            parts = {"w_in": g_in.reshape(N_DEV, -1, D).astype(BF),
                     "w_out": g["wout"].reshape(N_DEV, -1, D), "w_cq": g["wcq"].reshape(N_DEV, -1, D),
                     "w_co": g["wco"].reshape(N_DEV, -1, D), "w_ckv": g["wckv"]}
        names = list(parts)
        srcs = [parts[n] for n in names]
        *copies, token = _copy_start("exchange_start_" + group, srcs, _place_own(srcs, False), False)
        flying[group] = (names, copies)
        return token

    tiny_names = [n for n, _ in TINY_ROWS]
    small = {n: (w[n][0] if n == "b_s" else w[n]) for n in tiny_names}
    small["w_s"] = w["w_s"][0]

    sq, dx0, gs = _local_step(x[0], mem[0], loss_target[0], small, weights, emit)
    loss = lax.psum(sq[0, 0], ("x", "y", "c")) * (0.5 / D)

    sm_parts = [gs["w_s"].reshape(-1, LANES), _pack_tiny(gs)]
    sm_snd, sm_rcv, sm_src, sm_land, sm_token = _copy_start("small_start", sm_parts, _place_own(sm_parts, True), True)

    grad, delta, new_m, new_v = {}, {}, {}, {}

    def update(group, after):
        names, (snd, rcv, srcs, lands) = flying[group]
        slots = _copy_wait("exchange_wait_" + group, srcs, lands, snd, rcv, after, False)
        for n, sl in zip(names, slots):
            g, d, m2, v2 = _adamw_big("adamw_" + n, sl, local(n, w[n]), local(n, mo[n]), local(n, vo[n]))
            grad[n], delta[n], new_m[n], new_v[n] = (
                (t.T if n in TRANSPOSED else t).reshape(w[n].shape) for t in (g, d, m2, v2))
        return d

    last = update("ffn2", sm_token)
    last = update("mid", last)
    last = update("ffn1", last)
    ws_all, tiny_all = _copy_wait("small_wait", sm_src, sm_land, sm_snd, sm_rcv, last, True)
    ws_shape = w["w_s"].shape
    for store, t in zip((grad, delta, new_m, new_v), _adamw_big(
            "adamw_w_s", ws_all, *[a["w_s"].reshape(-1, LANES) for a in (w, mo, vo)])):
        store["w_s"] = t.reshape(ws_shape)
    shapes = {n: w[n].shape for n in tiny_names}
    for store, t in zip((grad, delta, new_m, new_v), _adamw_big(
            "adamw_tiny", tiny_all, *[_pack_tiny({n: a[n] for n in tiny_names}) for a in (w, mo, vo)])):
        store.update(_unpack_tiny(t, shapes))

    return (loss, dx0[None], *[grad[n] for n in WEIGHTS], *[delta[n] for n in WEIGHTS],
            *[new_m[n] for n in WEIGHTS], *[new_v[n] for n in WEIGHTS])
```

```python
import functools

import jax
import jax.numpy as jnp
from jax import lax
from jax.experimental import pallas as pl
from jax.experimental.pallas import tpu as pltpu

F32 = jnp.float32
BF = jnp.bfloat16
S = jax.ShapeDtypeStruct

N_DEV = 8
D_MODEL = 1024
FOX_HEADS, FOX_HD = 8, 64
FOX_W = 512
GMLP_G, GMLP_GD = 8, 64
GMLP_W = 512
CHUNK = 128
CA_HEADS, CA_HD = 4, 256
N_FFN_BLK = 4
ZW = 2688
Z_Q, Z_K, Z_V, Z_U, Z_G, Z_F = 0, 512, 1024, 1536, 2048, 2560
EPS = 1e-6
NEG = -1e30
LANES = 128

ADAM_LR, ADAM_B1, ADAM_B2, ADAM_EPS, ADAM_WD, ADAM_STEP = 0.001, 0.9, 0.999, 1e-08, 0.01, 10

VMEM_LIMIT = 52 * 2 ** 20


def _cp(n_axes):
    return pltpu.CompilerParams(dimension_semantics=("arbitrary",) * n_axes, vmem_limit_bytes=VMEM_LIMIT)


def _nn(a, b):
    return jnp.dot(a, b, preferred_element_type=F32)


def _nt(a, b):
    return lax.dot_general(a, b, (((1,), (1,)), ((), ())), preferred_element_type=F32)


def _tn(a, b):
    return lax.dot_general(a, b, (((0,), (0,)), ((), ())), preferred_element_type=F32)


def _hi(a, b):
    return jnp.dot(a, b, precision=lax.Precision.HIGHEST, preferred_element_type=F32)


def _rstd(x):
    return lax.rsqrt(jnp.mean(x * x, axis=-1, keepdims=True) + EPS)


def _norm_bwd(dy, x, g):
    r = _rstd(x)
    xh = x * r
    dxh = dy * g
    dx = r * (dxh - xh * jnp.mean(dxh * xh, axis=-1, keepdims=True))
    return dx, dy * xh


def _acc_rows(ref, first, val):
    srow = jnp.sum(val, axis=0, keepdims=True)

    @pl.when(first)
    def _():
        ref[...] = srow

    @pl.when(jnp.logical_not(first))
    def _():
        ref[...] += srow


def _gelu(x):
    c = 0.7978845608028654
    return 0.5 * x * (1.0 + jnp.tanh(c * (x + 0.044715 * x * x * x)))


def _gelu_grad(x):
    c = 0.7978845608028654
    t = jnp.tanh(c * (x + 0.044715 * x * x * x))
    return 0.5 * (1.0 + t) + 0.5 * x * (1.0 - t * t) * c * (1.0 + 3 * 0.044715 * x * x)


def _tile(n, pref):
    return pref if n % pref == 0 else n


def _ffn_up(name, x, g, wup):
    T, D = x.shape
    FB = wup.shape[-2]
    tm = _tile(T, 1024)

    def body(x_ref, g_ref, w_ref, a_ref, h_ref):
        @pl.when(pl.program_id(1) == 0)
        def _():
            xf = x_ref[...]
            h_ref[...] = (xf * _rstd(xf) * g_ref[...]).astype(BF)

        hb = h_ref[...]
        gg = _nt(hb, w_ref[0])
        uu = _nt(hb, w_ref[1])
        a_ref[...] = (gg * jax.nn.sigmoid(gg) * uu).astype(BF)

    return pl.pallas_call(
        body, name=name, grid=(T // tm, N_FFN_BLK),
        in_specs=[pl.BlockSpec((tm, D), lambda i, j: (i, 0)),
                  pl.BlockSpec((1, D), lambda i, j: (0, 0)),
                  pl.BlockSpec((2, None, FB, D), lambda i, j: (0, j, 0, 0))],
        out_specs=[pl.BlockSpec((None, tm, FB), lambda i, j: (j, i, 0)),
                   pl.BlockSpec((tm, D), lambda i, j: (i, 0))],
        out_shape=[S((N_FFN_BLK, T, FB), BF), S((T, D), BF)],
        compiler_params=_cp(2))(x, g, wup)


def _ffn_down(name, a, wdn, x):
    _, T, FB = a.shape
    D = x.shape[1]
    tm = _tile(T, 512)

    def body(a_ref, w_ref, x_ref, o_ref):
        j = pl.program_id(1)
        p = 0.5 * _nn(a_ref[...], w_ref[...])

        @pl.when(j == 0)
        def _():
            o_ref[...] = x_ref[...] + p

        @pl.when(j > 0)
        def _():
            o_ref[...] += p

    return pl.pallas_call(
        body, name=name, grid=(T // tm, N_FFN_BLK),
        in_specs=[pl.BlockSpec((None, tm, FB), lambda i, j: (j, i, 0)),
                  pl.BlockSpec((None, FB, D), lambda i, j: (j, 0, 0)),
                  pl.BlockSpec((tm, D), lambda i, j: (i, 0))],
        out_specs=pl.BlockSpec((tm, D), lambda i, j: (i, 0)),
        out_shape=S((T, D), F32),
        compiler_params=_cp(2))(a, wdn, x)


def _ffn_down_loss(name, a, wdn, x, target):
    _, T, FB = a.shape
    D = x.shape[1]
    tm = _tile(T, 512)

    def body(a_ref, w_ref, x_ref, t_ref, d_ref, db_ref, loss_ref, acc_ref):
        i, j = pl.program_id(0), pl.program_id(1)
        p = 0.5 * _nn(a_ref[...], w_ref[...])

        @pl.when(j == 0)
        def _():
            acc_ref[...] = x_ref[...] + p

        @pl.when(j > 0)
        def _():
            acc_ref[...] += p

        @pl.when(j == N_FFN_BLK - 1)
        def _():
            diff = acc_ref[...] - t_ref[...]
            dy = diff * (1.0 / D)
            d_ref[...] = dy
            db_ref[...] = dy.astype(BF)
            sq = jnp.zeros((8, LANES), F32) + jnp.sum(diff * diff)

            @pl.when(i == 0)
            def _():
                loss_ref[...] = sq

            @pl.when(i > 0)
            def _():
                loss_ref[...] += sq

    return pl.pallas_call(
        body, name=name, grid=(T // tm, N_FFN_BLK),
        in_specs=[pl.BlockSpec((None, tm, FB), lambda i, j: (j, i, 0)),
                  pl.BlockSpec((None, FB, D), lambda i, j: (j, 0, 0)),
                  pl.BlockSpec((tm, D), lambda i, j: (i, 0)),
                  pl.BlockSpec((tm, D), lambda i, j: (i, 0))],
        out_specs=[pl.BlockSpec((tm, D), lambda i, j: (i, 0)),
                   pl.BlockSpec((tm, D), lambda i, j: (i, 0)),
                   pl.BlockSpec((8, LANES), lambda i, j: (0, 0))],
        out_shape=[S((T, D), F32), S((T, D), BF), S((8, LANES), F32)],
        scratch_shapes=[pltpu.VMEM((tm, D), F32)],
        compiler_params=_cp(2))(a, wdn, x, target)


def _ffn_bwd_act(name, dyb, h, wup, wdn):
    T, D = h.shape
    FB = wup.shape[-2]
    tm = _tile(T, 1024)

    def body(d_ref, h_ref, wu_ref, wd_ref, o_ref):
        da = 0.5 * _nt(d_ref[...], wd_ref[...])
        hb = h_ref[...]
        gg = _nt(hb, wu_ref[0])
        uu = _nt(hb, wu_ref[1])
        sg = jax.nn.sigmoid(gg)
        o_ref[0] = (da * uu * (sg * (1.0 + gg * (1.0 - sg)))).astype(BF)
        o_ref[1] = (da * (gg * sg)).astype(BF)

    return pl.pallas_call(
        body, name=name, grid=(T // tm, N_FFN_BLK),
        in_specs=[pl.BlockSpec((tm, D), lambda i, j: (i, 0)),
                  pl.BlockSpec((tm, D), lambda i, j: (i, 0)),
                  pl.BlockSpec((2, None, FB, D), lambda i, j: (0, j, 0, 0)),
                  pl.BlockSpec((None, FB, D), lambda i, j: (j, 0, 0))],
        out_specs=pl.BlockSpec((2, None, tm, FB), lambda i, j: (0, j, i, 0)),
        out_shape=S((2, N_FFN_BLK, T, FB), BF),
        compiler_params=_cp(2))(dyb, h, wup, wdn)


def _ffn_dx(name, dgu, wup, x, g, dy):
    T, D = x.shape
    FB = wup.shape[-2]
    tm = _tile(T, 1024)

    def body(d_ref, w_ref, x_ref, g_ref, dy_ref, dx_ref, dg_ref, acc_ref):
        i, j = pl.program_id(0), pl.program_id(1)
        p = _nn(d_ref[0], w_ref[0]) + _nn(d_ref[1], w_ref[1])

        @pl.when(j == 0)
        def _():
            acc_ref[...] = p

        @pl.when(j > 0)
        def _():
            acc_ref[...] += p

        @pl.when(j == N_FFN_BLK - 1)
        def _():
            dx, dgr = _norm_bwd(acc_ref[...], x_ref[...], g_ref[...])
            dx_ref[...] = dx + dy_ref[...]
            _acc_rows(dg_ref, i == 0, dgr)

    return pl.pallas_call(
        body, name=name, grid=(T // tm, N_FFN_BLK),
        in_specs=[pl.BlockSpec((2, None, tm, FB), lambda i, j: (0, j, i, 0)),
                  pl.BlockSpec((2, None, FB, D), lambda i, j: (0, j, 0, 0)),
                  pl.BlockSpec((tm, D), lambda i, j: (i, 0)),
                  pl.BlockSpec((1, D), lambda i, j: (0, 0)),
                  pl.BlockSpec((tm, D), lambda i, j: (i, 0))],
        out_specs=[pl.BlockSpec((tm, D), lambda i, j: (i, 0)),
                   pl.BlockSpec((1, D), lambda i, j: (0, 0))],
        out_shape=[S((T, D), F32), S((1, D), F32)],
        scratch_shapes=[pltpu.VMEM((tm, D), F32)],
        compiler_params=_cp(2))(dgu, wup, x, g, dy)


def _tn_matmul(name, a, a_spec, b, b_spec, out_shape, out_spec, grid, acc_shape, scale=1.0):
    nk = grid[1]

    def body(a_ref, b_ref, o_ref, acc_ref):
        k = pl.program_id(1)
        p = _tn(a_ref[...], b_ref[...])

        @pl.when(k == 0)
        def _():
            acc_ref[...] = p

        @pl.when(k > 0)
        def _():
            acc_ref[...] += p

        @pl.when(k == nk - 1)
        def _():
            o_ref[...] = (acc_ref[...] * scale).astype(o_ref.dtype)

    return pl.pallas_call(
        body, name=name, grid=grid, in_specs=[a_spec, b_spec], out_specs=out_spec, out_shape=out_shape,
        scratch_shapes=[pltpu.VMEM(acc_shape, F32)], compiler_params=_cp(2))(a, b)


def _ffn_dwup(name, h, dgu):
    T, D = h.shape
    FB = dgu.shape[-1]
    tk = _tile(T, 1024)
    return _tn_matmul(
        name + "_dwup", dgu.reshape(2 * N_FFN_BLK, T, FB), pl.BlockSpec((None, tk, FB), lambda j, k: (j, k, 0)),
        h, pl.BlockSpec((tk, D), lambda j, k: (k, 0)),
        S((2 * N_FFN_BLK, FB, D), BF), pl.BlockSpec((None, FB, D), lambda j, k: (j, 0, 0)),
        (2 * N_FFN_BLK, T // tk), (FB, D))


def _ffn_dwdn(name, a, dyb):
    _, T, FB = a.shape
    D = dyb.shape[1]
    tk = _tile(T, 1024)
    return _tn_matmul(
        name + "_dwdn", a, pl.BlockSpec((None, tk, FB), lambda j, k: (j, k, 0)),
        dyb, pl.BlockSpec((tk, D), lambda j, k: (k, 0)),
        S((N_FFN_BLK, FB, D), BF), pl.BlockSpec((None, FB, D), lambda j, k: (j, 0, 0)),
        (N_FFN_BLK, T // tk), (FB, D), scale=0.5)


def _mix_proj(x, g, wz):
    T, D = x.shape
    tm = _tile(T, 512)

    def body(x_ref, g_ref, w_ref, z_ref, h_ref):
        xf = x_ref[...]
        hb = (xf * _rstd(xf) * g_ref[...]).astype(BF)
        h_ref[...] = hb
        z_ref[...] = _nt(hb, w_ref[...])

    return pl.pallas_call(
        body, name="mix_proj", grid=(T // tm,),
        in_specs=[pl.BlockSpec((tm, D), lambda i: (i, 0)),
                  pl.BlockSpec((1, D), lambda i: (0, 0)),
                  pl.BlockSpec((ZW, D), lambda i: (0, 0))],
        out_specs=[pl.BlockSpec((tm, ZW), lambda i: (i, 0)),
                   pl.BlockSpec((tm, D), lambda i: (i, 0))],
        out_shape=[S((T, ZW), F32), S((T, D), BF)],
        compiler_params=_cp(1))(x, g, wz)


def _tri(n, lower):
    r = lax.broadcasted_iota(jnp.int32, (n, n), 0)
    c = lax.broadcasted_iota(jnp.int32, (n, n), 1)
    return (r >= c) if lower else (r <= c)


def _spatial_mix(vgn_b, ws_ref, bst, tm):
    tril = _tri(CHUNK, True)
    wms = [jnp.where(tril, ws_ref[g], 0.0).astype(BF) for g in range(GMLP_G)]
    rows = []
    for c in range(tm // CHUNK):
        cols = []
        for g in range(GMLP_G):
            vs = vgn_b[c * CHUNK:(c + 1) * CHUNK, g * GMLP_GD:(g + 1) * GMLP_GD]
            cols.append(_nn(wms[g], vs) + bst[:, g:g + 1])
        rows.append(jnp.concatenate(cols, axis=1))
    return jnp.concatenate(rows, axis=0), wms


HB = 128
AUG_W = FOX_HEADS * HB
COL_A, COL_B, COL_C = 64, 67, 70


def _spread_matrix():
    r = jnp.arange(FOX_W)
    return (jnp.arange(AUG_W)[None, :] == ((r // FOX_HD) * HB + r % FOX_HD)[:, None]).astype(BF)


def _piece_matrix(col):
    r = jnp.arange(LANES)
    dst = jnp.where(r < 3 * FOX_HEADS, (r % FOX_HEADS) * HB + col + r // FOX_HEADS, -1)
    return (jnp.arange(AUG_W)[None, :] == dst[:, None]).astype(BF)


def _ones_row(cols):
    c = jnp.arange(AUG_W) % HB
    hit = functools.reduce(jnp.logical_or, [(c >= a) & (c < a + 3) for a in cols])
    return hit.astype(F32)[None, :]


def _pieces(x):
    lane = lax.broadcasted_iota(jnp.int32, x.shape, 1)
    x = jnp.where(lane < FOX_HEADS, x, 0.0)
    hi = x.astype(BF).astype(F32)
    r1 = x - hi
    mid = r1.astype(BF).astype(F32)
    lo = (r1 - mid).astype(BF).astype(F32)
    return (hi + pltpu.roll(mid, FOX_HEADS, 1) + pltpu.roll(lo, 2 * FOX_HEADS, 1)).astype(BF)


def _mix_prep(z, bf128, g_q, g_k, g_sgu, w_s, b_st, g_go):
    T = z.shape[0]
    tm = _tile(T, 256)
    spread, pc_q, pc_k = _spread_matrix(), _piece_matrix(COL_A), _piece_matrix(COL_B)
    one_q, one_k, one_v = _ones_row([COL_B]), _ones_row([COL_A, COL_C]), _ones_row([COL_A])

    def body(z_ref, bf_ref, gq_ref, gk_ref, gs_ref, ws_ref, bst_ref, go_ref, sp_ref, pq_ref, pk_ref, oq_ref, ok_ref,
             ov_ref, q_ref, k_ref, v_ref, y_ref, carry_ref, qn_sc, kn_sc):
        i = pl.program_id(0)

        @pl.when(i == 0)
        def _():
            carry_ref[...] = jnp.zeros_like(carry_ref)

        for h in range(FOX_HEADS):
            hs = slice(h * FOX_HD, (h + 1) * FOX_HD)
            qh = z_ref[:, Z_Q + h * FOX_HD:Z_Q + (h + 1) * FOX_HD]
            kh = z_ref[:, Z_K + h * FOX_HD:Z_K + (h + 1) * FOX_HD]
            qn_sc[:, hs] = (qh * _rstd(qh) * gq_ref[...] * 0.125).astype(BF)
            kn_sc[:, hs] = (kh * _rstd(kh) * gk_ref[...]).astype(BF)

        fl = z_ref[:, Z_F:Z_F + LANES] + bf_ref[...]
        logf = jnp.minimum(fl, 0.0) - jnp.log1p(jnp.exp(-jnp.abs(fl)))
        csum = _hi(_tri(tm, True).astype(F32), logf) + carry_ref[...]
        carry_ref[...] = csum[tm - 1:tm, :]
        sp = sp_ref[...]
        q_ref[...] = (_nn(qn_sc[...], sp) + _nn(_pieces(csum), pq_ref[...]) + oq_ref[...]).astype(BF)
        k_ref[...] = (_nn(kn_sc[...], sp) + _nn(_pieces(-csum), pk_ref[...]) + ok_ref[...]).astype(BF)
        v_ref[...] = (_nn(z_ref[:, Z_V:Z_V + FOX_W].astype(BF), sp) + ov_ref[...]).astype(BF)

        u = _gelu(z_ref[:, Z_U:Z_U + GMLP_W])
        vg = _gelu(z_ref[:, Z_G:Z_G + GMLP_W])
        vgn = (vg * _rstd(vg) * gs_ref[...]).astype(BF)
        mixed, _ = _spatial_mix(vgn, ws_ref, bst_ref[...], tm)
        sgu = u * mixed
        y_ref[...] = (sgu * _rstd(sgu) * go_ref[...]).astype(BF)

    row = lambda i: (i, 0)
    fix2 = lambda i: (0, 0)
    return pl.pallas_call(
        body, name="mix_prep", grid=(T // tm,),
        in_specs=[pl.BlockSpec((tm, ZW), row),
                  pl.BlockSpec((1, LANES), fix2), pl.BlockSpec((1, FOX_HD), fix2), pl.BlockSpec((1, FOX_HD), fix2),
                  pl.BlockSpec((1, GMLP_W), fix2), pl.BlockSpec((GMLP_G, CHUNK, CHUNK), lambda i: (0, 0, 0)),
                  pl.BlockSpec((CHUNK, GMLP_G), fix2), pl.BlockSpec((1, GMLP_W), fix2),
                  pl.BlockSpec((FOX_W, AUG_W), fix2), pl.BlockSpec((LANES, AUG_W), fix2),
                  pl.BlockSpec((LANES, AUG_W), fix2), pl.BlockSpec((1, AUG_W), fix2), pl.BlockSpec((1, AUG_W), fix2),
                  pl.BlockSpec((1, AUG_W), fix2)],
        out_specs=[pl.BlockSpec((tm, AUG_W), row), pl.BlockSpec((tm, AUG_W), row), pl.BlockSpec((tm, AUG_W), row),
                   pl.BlockSpec((tm, GMLP_W), row)],
        out_shape=[S((T, AUG_W), BF), S((T, AUG_W), BF), S((T, AUG_W), BF), S((T, GMLP_W), BF)],
        scratch_shapes=[pltpu.VMEM((1, LANES), F32), pltpu.VMEM((tm, FOX_W), BF), pltpu.VMEM((tm, FOX_W), BF)],
        compiler_params=_cp(1))(z, bf128, g_q, g_k, g_sgu, w_s, b_st, g_go, spread, pc_q, pc_k, one_q, one_k, one_v)


def _fox_fwd(q, k, v):
    T = q.shape[0]
    tq = _tile(T, 512)
    nq = T // tq

    def body(q_ref, k_ref, v_ref, o_ref, lse_ref, m_sc, acc_sc):
        i, j = pl.program_id(0), pl.program_id(1)

        @pl.when(j == 0)
        def _():
            m_sc[...] = jnp.full(m_sc.shape, NEG, F32)
            acc_sc[...] = jnp.zeros_like(acc_sc)

        def step(masked):
            mask = _tri(tq, True) if masked else None
            for h in range(FOX_HEADS):
                hb = slice(h * HB, (h + 1) * HB)
                s = _nt(q_ref[:, hb], k_ref[:, hb])
                if masked:
                    s = jnp.where(mask, s, NEG)
                m_prev = m_sc[h]
                m_new = jnp.maximum(m_prev, jnp.broadcast_to(jnp.max(s, axis=1, keepdims=True), (tq, HB)))
                p = jnp.exp(s - jnp.tile(m_new, (1, tq // HB))).astype(BF)
                acc_sc[:, hb] = jnp.exp(m_prev - m_new) * acc_sc[:, hb] + _nn(p, v_ref[:, hb])
                m_sc[h] = m_new

        @pl.when(j < i)
        def _():
            step(False)

        @pl.when(j == i)
        def _():
            step(True)
            lse_ref[...] = jnp.zeros_like(lse_ref)
            for h in range(FOX_HEADS):
                l = acc_sc[:, h * HB + COL_A:h * HB + COL_A + 1]
                o_ref[:, h * FOX_HD:(h + 1) * FOX_HD] = acc_sc[:, h * HB:h * HB + FOX_HD] / l
                lse_ref[:, h:h + 1] = m_sc[h][:, 0:1] + jnp.log(l)

    qi = lambda i, j: (i, 0)
    kj = lambda i, j: (jnp.minimum(i, j), 0)
    return pl.pallas_call(
        body, name="fox_fwd", grid=(nq, nq),
        in_specs=[pl.BlockSpec((tq, AUG_W), qi), pl.BlockSpec((tq, AUG_W), kj), pl.BlockSpec((tq, AUG_W), kj)],
        out_specs=[pl.BlockSpec((tq, FOX_W), qi), pl.BlockSpec((tq, LANES), qi)],
        out_shape=[S((T, FOX_W), F32), S((T, LANES), F32)],
        scratch_shapes=[pltpu.VMEM((FOX_HEADS, tq, HB), F32), pltpu.VMEM((tq, AUG_W), F32)],
        compiler_params=_cp(2))(q, k, v)


def _fox_bwd(q, k, v, dob):
    T = q.shape[0]
    tq = _tile(T, 512)
    nq = T // tq
    half = AUG_W // 2
    hpg = FOX_HEADS // 2

    def body(q_ref, k_ref, v_ref, do_ref, dq_ref, dk_ref, dv_ref, dq_sc):
        j, i = pl.program_id(1), pl.program_id(2)

        @pl.when(jnp.logical_and(i == 0, j == 0))
        def _():
            dq_sc[...] = jnp.zeros_like(dq_sc)

        @pl.when(i == 0)
        def _():
            dk_ref[...] = jnp.zeros_like(dk_ref)
            dv_ref[...] = jnp.zeros_like(dv_ref)

        def step(masked):
            rows = pl.ds(pl.multiple_of(i * tq, tq), tq)
            mask = _tri(tq, True) if masked else None
            for h in range(hpg):
                hb = slice(h * HB, (h + 1) * HB)
                qh, kh, vh, doh = q_ref[:, hb], k_ref[:, hb], v_ref[:, hb], do_ref[:, hb]
                s = _nt(qh, kh)
                if masked:
                    s = jnp.where(mask, s, NEG)
                p = jnp.exp(s)
                dsb = (p * _nt(doh, vh)).astype(BF)
                dv_ref[:, hb] += _tn(p.astype(BF), doh)
                dk_ref[:, hb] += _tn(dsb, qh)
                dq_sc[rows, hb] += _nn(dsb, kh)

        @pl.when(i > j)
        def _():
            step(False)

        @pl.when(i == j)
        def _():
            step(True)
            dq_ref[...] = dq_sc[pl.ds(pl.multiple_of(j * tq, tq), tq), :]

    qi = lambda g, j, i: (jnp.maximum(i, j), g)
    kj = lambda g, j, i: (j, g)
    return pl.pallas_call(
        body, name="fox_bwd", grid=(2, nq, nq),
        in_specs=[pl.BlockSpec((tq, half), qi), pl.BlockSpec((tq, half), kj), pl.BlockSpec((tq, half), kj),
                  pl.BlockSpec((tq, half), qi)],
        out_specs=[pl.BlockSpec((tq, half), kj), pl.BlockSpec((tq, half), kj), pl.BlockSpec((tq, half), kj)],
        out_shape=[S((T, AUG_W), F32), S((T, AUG_W), F32), S((T, AUG_W), F32)],
        scratch_shapes=[pltpu.VMEM((T, half), F32)],
        compiler_params=_cp(3))(q, k, v, dob)


def _mix_out(attn, yg, g_fo, wout, x):
    T, D = x.shape
    tm = _tile(T, 512)

    def body(a_ref, y_ref, g_ref, w_ref, x_ref, o_ref):
        at = a_ref[...]
        yf = (at * _rstd(at) * g_ref[...]).astype(BF)
        o_ref[...] = x_ref[...] + _nn(yf, w_ref[:FOX_W, :]) + _nn(y_ref[...], w_ref[FOX_W:, :])

    row = lambda i: (i, 0)
    return pl.pallas_call(
        body, name="mix_out", grid=(T // tm,),
        in_specs=[pl.BlockSpec((tm, FOX_W), row), pl.BlockSpec((tm, GMLP_W), row),
                  pl.BlockSpec((1, FOX_W), lambda i: (0, 0)), pl.BlockSpec((D, D), lambda i: (0, 0)),
                  pl.BlockSpec((tm, D), row)],
        out_specs=pl.BlockSpec((tm, D), row),
        out_shape=S((T, D), F32),
        compiler_params=_cp(1))(attn, yg, g_fo, wout, x)


def _mix_out_bwd(dx, attn, yg, g_fo, wout, qf, lse):
    T, D = dx.shape
    tm = _tile(T, 256)
    n = T // tm
    spread, pc_l, pc_d = _spread_matrix(), _piece_matrix(COL_C), _piece_matrix(COL_A)

    def body(dx_ref, a_ref, y_ref, g_ref, w_ref, qf_ref, lse_ref, sp_ref, pl_ref, pd_ref,
             qb_ref, dob_ref, dyg_ref, dw_ref, dg_ref, acc_ref, dsum_ref):
        i = pl.program_id(0)
        dxb = dx_ref[...].astype(BF)
        at = a_ref[...]
        yf = (at * _rstd(at) * g_ref[...]).astype(BF)
        dy = _nt(dxb, w_ref[...])
        p_top = _tn(yf, dxb)
        p_bot = _tn(y_ref[...], dxb)

        @pl.when(i == 0)
        def _():
            acc_ref[:FOX_W, :] = p_top
            acc_ref[FOX_W:, :] = p_bot

        @pl.when(i > 0)
        def _():
            acc_ref[:FOX_W, :] += p_top
            acc_ref[FOX_W:, :] += p_bot

        @pl.when(i == n - 1)
        def _():
            dw_ref[...] = acc_ref[...].astype(BF)

        dat, dgr = _norm_bwd(dy[:, :FOX_W], at, g_ref[...])
        _acc_rows(dg_ref, i == 0, dgr)
        dyg_ref[...] = dy[:, FOX_W:]
        prod = dat * at
        dsum_ref[...] = jnp.zeros_like(dsum_ref)
        for h in range(FOX_HEADS):
            dsum_ref[:, h:h + 1] = jnp.sum(prod[:, h * FOX_HD:(h + 1) * FOX_HD], axis=1, keepdims=True)
        dob_ref[...] = (_nn(dat.astype(BF), sp_ref[...]) + _nn(_pieces(-dsum_ref[...]), pd_ref[...])).astype(BF)
        qb_ref[...] = (qf_ref[...].astype(F32) + _nn(_pieces(-lse_ref[...]), pl_ref[...])).astype(BF)

    row = lambda i: (i, 0)
    fix = lambda i: (0, 0)
    return pl.pallas_call(
        body, name="mix_out_bwd", grid=(n,),
        in_specs=[pl.BlockSpec((tm, D), row), pl.BlockSpec((tm, FOX_W), row), pl.BlockSpec((tm, GMLP_W), row),
                  pl.BlockSpec((1, FOX_W), fix), pl.BlockSpec((D, D), fix), pl.BlockSpec((tm, AUG_W), row),
                  pl.BlockSpec((tm, LANES), row), pl.BlockSpec((FOX_W, AUG_W), fix), pl.BlockSpec((LANES, AUG_W), fix),
                  pl.BlockSpec((LANES, AUG_W), fix)],
        out_specs=[pl.BlockSpec((tm, AUG_W), row), pl.BlockSpec((tm, AUG_W), row), pl.BlockSpec((tm, GMLP_W), row),
                   pl.BlockSpec((D, D), fix), pl.BlockSpec((1, FOX_W), fix)],
        out_shape=[S((T, AUG_W), BF), S((T, AUG_W), BF), S((T, GMLP_W), F32), S((D, D), BF), S((1, FOX_W), F32)],
        scratch_shapes=[pltpu.VMEM((D, D), F32), pltpu.VMEM((tm, LANES), F32)],
        compiler_params=_cp(1))(dx, attn, yg, g_fo, wout, qf, lse, spread, pc_l, pc_d)


def _mix_prep_bwd(z, dq, dk, dv, dyg, bf128, g_q, g_k, g_sgu, w_s, b_st, g_go):
    T = z.shape[0]
    tm = _tile(T, 256)
    n = T // tm

    def body(z_ref, dq_ref, dk_ref, dv_ref, dyg_ref, bf_ref, gq_ref, gk_ref, gs_ref, ws_ref,
             bst_ref, go_ref, dz_ref, dgq_ref, dgk_ref, dgs_ref, dgo_ref, dws_ref, dbst_ref, dbf_ref, carry_ref):
        i = pl.program_id(0)
        first = i == 0

        @pl.when(first)
        def _():
            carry_ref[...] = jnp.zeros_like(carry_ref)

        lane = lax.broadcasted_iota(jnp.int32, (tm, LANES), 1)
        dc = jnp.zeros((tm, LANES), F32)
        gq_rows, gk_rows = [], []
        for h in range(FOX_HEADS):
            hp = slice(h * HB, h * HB + FOX_HD)
            dqh, gqr = _norm_bwd(dq_ref[:, hp] * 0.125, z_ref[:, Z_Q + h * FOX_HD:Z_Q + (h + 1) * FOX_HD], gq_ref[...])
            dkh, gkr = _norm_bwd(dk_ref[:, hp], z_ref[:, Z_K + h * FOX_HD:Z_K + (h + 1) * FOX_HD], gk_ref[...])
            dz_ref[:, Z_Q + h * FOX_HD:Z_Q + (h + 1) * FOX_HD] = dqh.astype(BF)
            dz_ref[:, Z_K + h * FOX_HD:Z_K + (h + 1) * FOX_HD] = dkh.astype(BF)
            dz_ref[:, Z_V + h * FOX_HD:Z_V + (h + 1) * FOX_HD] = dv_ref[:, hp].astype(BF)
            dch = dq_ref[:, h * HB + COL_A:h * HB + COL_A + 1] - dk_ref[:, h * HB + COL_B:h * HB + COL_B + 1]
            dc = jnp.where(lane == h, dch, dc)
            gq_rows.append(gqr)
            gk_rows.append(gkr)
        _acc_rows(dgq_ref, first, functools.reduce(lambda a, b: a + b, gq_rows))
        _acc_rows(dgk_ref, first, functools.reduce(lambda a, b: a + b, gk_rows))

        dlogf = _hi(_tri(tm, False).astype(F32), dc) + carry_ref[...]
        carry_ref[...] = dlogf[0:1, :]
        fl = z_ref[:, Z_F:Z_F + LANES] + bf_ref[...]
        lane = lax.broadcasted_iota(jnp.int32, (tm, LANES), 1)
        df = jnp.where(lane < FOX_HEADS, dlogf * jax.nn.sigmoid(-fl), 0.0)
        dz_ref[:, Z_F:Z_F + LANES] = df.astype(BF)
        _acc_rows(dbf_ref, first, df)

        u_pre = z_ref[:, Z_U:Z_U + GMLP_W]
        vg_pre = z_ref[:, Z_G:Z_G + GMLP_W]
        u = _gelu(u_pre)
        vg = _gelu(vg_pre)
        vgn = (vg * _rstd(vg) * gs_ref[...]).astype(BF)
        bst = bst_ref[...]
        mixed, wms = _spatial_mix(vgn, ws_ref, bst, tm)
        sgu = u * mixed
        dsgu, gor = _norm_bwd(dyg_ref[...], sgu, go_ref[...])
        _acc_rows(dgo_ref, first, gor)
        du = dsgu * mixed
        dmixed = dsgu * u
        dmb = dmixed.astype(BF)
        tril = _tri(CHUNK, True)
        dvgn_rows = []
        dws = [None] * GMLP_G
        dbs = [None] * GMLP_G
        for c in range(tm // CHUNK):
            cs = slice(c * CHUNK, (c + 1) * CHUNK)
            cols = []
            for g in range(GMLP_G):
                gs = slice(g * GMLP_GD, (g + 1) * GMLP_GD)
                dmc = dmb[cs, gs]
                pw = _nt(dmc, vgn[cs, gs])
                pb = jnp.sum(dmixed[cs, gs], axis=1, keepdims=True)
                dws[g] = pw if dws[g] is None else dws[g] + pw
                dbs[g] = pb if dbs[g] is None else dbs[g] + pb
                cols.append(_tn(wms[g], dmc))
            dvgn_rows.append(jnp.concatenate(cols, axis=1))
        dvgn = jnp.concatenate(dvgn_rows, axis=0)
        dbs_t = jnp.concatenate(dbs, axis=1)
        for g in range(GMLP_G):
            dwg = jnp.where(tril, dws[g], 0.0)

            @pl.when(first)
            def _():
                dws_ref[g] = dwg

            @pl.when(jnp.logical_not(first))
            def _():
                dws_ref[g] += dwg

        @pl.when(first)
        def _():
            dbst_ref[...] = dbs_t

        @pl.when(jnp.logical_not(first))
        def _():
            dbst_ref[...] += dbs_t

        dvg, gsr = _norm_bwd(dvgn, vg, gs_ref[...])
        _acc_rows(dgs_ref, first, gsr)
        dz_ref[:, Z_U:Z_U + GMLP_W] = (du * _gelu_grad(u_pre)).astype(BF)
        dz_ref[:, Z_G:Z_G + GMLP_W] = (dvg * _gelu_grad(vg_pre)).astype(BF)

    rev = lambda i: (n - 1 - i, 0)
    fix = lambda i: (0, 0)
    fix3 = lambda i: (0, 0, 0)
    return pl.pallas_call(
        body, name="mix_prep_bwd", grid=(n,),
        in_specs=[pl.BlockSpec((tm, ZW), rev), pl.BlockSpec((tm, AUG_W), rev), pl.BlockSpec((tm, AUG_W), rev),
                  pl.BlockSpec((tm, AUG_W), rev), pl.BlockSpec((tm, GMLP_W), rev),
                  pl.BlockSpec((1, LANES), fix), pl.BlockSpec((1, FOX_HD), fix), pl.BlockSpec((1, FOX_HD), fix),
                  pl.BlockSpec((1, GMLP_W), fix), pl.BlockSpec((GMLP_G, CHUNK, CHUNK), fix3),
                  pl.BlockSpec((CHUNK, GMLP_G), fix), pl.BlockSpec((1, GMLP_W), fix)],
        out_specs=[pl.BlockSpec((tm, ZW), rev), pl.BlockSpec((1, FOX_HD), fix), pl.BlockSpec((1, FOX_HD), fix),
                   pl.BlockSpec((1, GMLP_W), fix), pl.BlockSpec((1, GMLP_W), fix),
                   pl.BlockSpec((GMLP_G, CHUNK, CHUNK), fix3), pl.BlockSpec((CHUNK, GMLP_G), fix),
                   pl.BlockSpec((1, LANES), fix)],
        out_shape=[S((T, ZW), BF), S((1, FOX_HD), F32), S((1, FOX_HD), F32), S((1, GMLP_W), F32), S((1, GMLP_W), F32),
                   S((GMLP_G, CHUNK, CHUNK), F32), S((CHUNK, GMLP_G), F32), S((1, LANES), F32)],
        scratch_shapes=[pltpu.VMEM((1, LANES), F32)],
        compiler_params=_cp(1))(z, dq, dk, dv, dyg, bf128, g_q, g_k, g_sgu, w_s, b_st, g_go)


def _mix_proj_bwd(dz, wz, x, g, dy):
    T, D = x.shape
    tm = _tile(T, 512)

    def body(dz_ref, w_ref, x_ref, g_ref, dy_ref, dx_ref, dxb_ref, dg_ref):
        dh = _nn(dz_ref[...], w_ref[...])
        dx, dgr = _norm_bwd(dh, x_ref[...], g_ref[...])
        dx = dx + dy_ref[...]
        dx_ref[...] = dx
        dxb_ref[...] = dx.astype(BF)
        _acc_rows(dg_ref, pl.program_id(0) == 0, dgr)

    row = lambda i: (i, 0)
    fix = lambda i: (0, 0)
    return pl.pallas_call(
        body, name="mix_proj_bwd", grid=(T // tm,),
        in_specs=[pl.BlockSpec((tm, ZW), row), pl.BlockSpec((ZW, D), fix), pl.BlockSpec((tm, D), row),
                  pl.BlockSpec((1, D), fix), pl.BlockSpec((tm, D), row)],
        out_specs=[pl.BlockSpec((tm, D), row), pl.BlockSpec((tm, D), row), pl.BlockSpec((1, D), fix)],
        out_shape=[S((T, D), F32), S((T, D), BF), S((1, D), F32)],
        compiler_params=_cp(1))(dz, wz, x, g, dy)


def _ca_kv(mem, g_mem, wckv, g_ck):
    M, D = mem.shape

    def body(m_ref, g_ref, w_ref, gk_ref, mn_ref, kr_ref, kn_ref, v_ref):
        mf = m_ref[...]
        mn = (mf * _rstd(mf) * g_ref[...]).astype(BF)
        mn_ref[...] = mn
        for h in range(CA_HEADS):
            kr = _nn(mn, w_ref[h])
            kr_ref[h] = kr
            kn_ref[h] = (kr * _rstd(kr) * gk_ref[...]).astype(BF)
            v_ref[h] = _nn(mn, w_ref[CA_HEADS + h]).astype(BF)

    hd = (CA_HEADS, M, CA_HD)
    return pl.pallas_call(
        body, name="ca_kv", out_shape=[S((M, D), BF), S(hd, F32), S(hd, BF), S(hd, BF)],
        compiler_params=pltpu.CompilerParams(vmem_limit_bytes=VMEM_LIMIT))(mem, g_mem, wckv, g_ck)


def _ca_tile_fwd(xt, gca, wcq, gcq, kn_ref, v_ref):
    hb = (xt * _rstd(xt) * gca).astype(BF)
    qc = _nn(hb, wcq)
    qr, qn, ps = [], [], []
    for h in range(CA_HEADS):
        qh = qc[:, h * CA_HD:(h + 1) * CA_HD]
        qnh = (qh * _rstd(qh) * gcq * 0.0625).astype(BF)
        s = _nt(qnh, kn_ref[h])
        e = jnp.exp(s - jnp.max(s, axis=1, keepdims=True))
        ps.append(e / jnp.sum(e, axis=1, keepdims=True))
        qr.append(qh)
        qn.append(qnh)
    return hb, qr, qn, ps


def _ca_fwd(x, g_ca, wcq, g_cq, kn, vv, wco):
    T, D = x.shape
    M = kn.shape[1]
    tm = _tile(T, 256)

    def body(x_ref, gca_ref, wcq_ref, gcq_ref, kn_ref, v_ref, wco_ref, o_ref, ob_sc):
        xt = x_ref[...]
        _, _, _, ps = _ca_tile_fwd(xt, gca_ref[...], wcq_ref[...], gcq_ref[...], kn_ref, v_ref)
        for h in range(CA_HEADS):
            ob_sc[:, h * CA_HD:(h + 1) * CA_HD] = _nn(ps[h].astype(BF), v_ref[h]).astype(BF)
        o_ref[...] = xt + _nn(ob_sc[...], wco_ref[...])

    row = lambda i: (i, 0)
    fix = lambda i: (0, 0)
    fix3 = lambda i: (0, 0, 0)
    return pl.pallas_call(
        body, name="ca_fwd", grid=(T // tm,),
        in_specs=[pl.BlockSpec((tm, D), row), pl.BlockSpec((1, D), fix), pl.BlockSpec((D, D), fix),
                  pl.BlockSpec((1, CA_HD), fix), pl.BlockSpec((CA_HEADS, M, CA_HD), fix3),
                  pl.BlockSpec((CA_HEADS, M, CA_HD), fix3), pl.BlockSpec((D, D), fix)],
        out_specs=pl.BlockSpec((tm, D), row), out_shape=S((T, D), F32),
        scratch_shapes=[pltpu.VMEM((tm, D), BF)],
        compiler_params=_cp(1))(x, g_ca, wcq, g_cq, kn, vv, wco)


def _ca_bwd(x, dy, g_ca, wcq, g_cq, kn, vv, wco):
    T, D = x.shape
    M = kn.shape[1]
    tm = _tile(T, 256)
    n = T // tm

    def body(x_ref, dy_ref, gca_ref, wcq_ref, gcq_ref, kn_ref, v_ref, wco_ref,
             dx_ref, dwq_ref, dwo_ref, dkn_ref, dv_ref, dgcq_ref, dgca_ref, aq_sc, ao_sc, ob_sc, dq_sc):
        i = pl.program_id(0)
        first = i == 0
        xt = x_ref[...]
        dyt = dy_ref[...]
        dyb = dyt.astype(BF)
        hb, qr, qn, ps = _ca_tile_fwd(xt, gca_ref[...], wcq_ref[...], gcq_ref[...], kn_ref, v_ref)
        do = _nt(dyb, wco_ref[...])
        gcq_rows = None
        for h in range(CA_HEADS):
            hs = slice(h * CA_HD, (h + 1) * CA_HD)
            p = ps[h]
            pb = p.astype(BF)
            ob_sc[:, hs] = _nn(pb, v_ref[h]).astype(BF)
            doh = do[:, hs].astype(BF)
            dp = _nt(doh, v_ref[h])
            ds = (p * (dp - jnp.sum(dp * p, axis=1, keepdims=True))).astype(BF)
            dvh = _tn(pb, doh)
            dkh = _tn(ds, qn[h])

            @pl.when(first)
            def _():
                dv_ref[h] = dvh
                dkn_ref[h] = dkh

            @pl.when(jnp.logical_not(first))
            def _():
                dv_ref[h] += dvh
                dkn_ref[h] += dkh

            dqn = _nn(ds, kn_ref[h]) * 0.0625
            dqh, gr = _norm_bwd(dqn, qr[h], gcq_ref[...])
            gcq_rows = gr if gcq_rows is None else gcq_rows + gr
            dq_sc[:, hs] = dqh.astype(BF)
        _acc_rows(dgcq_ref, first, gcq_rows)
        dqb = dq_sc[...]
        p_o = _tn(ob_sc[...], dyb)
        p_q = _tn(hb, dqb)

        @pl.when(first)
        def _():
            ao_sc[...] = p_o
            aq_sc[...] = p_q

        @pl.when(jnp.logical_not(first))
        def _():
            ao_sc[...] += p_o
            aq_sc[...] += p_q

        @pl.when(i == n - 1)
        def _():
            dwo_ref[...] = ao_sc[...].astype(BF)
            dwq_ref[...] = aq_sc[...].astype(BF)

        dh = _nt(dqb, wcq_ref[...])
        dx, gar = _norm_bwd(dh, xt, gca_ref[...])
        dx_ref[...] = dx + dyt
        _acc_rows(dgca_ref, first, gar)

    row = lambda i: (i, 0)
    fix = lambda i: (0, 0)
    fix3 = lambda i: (0, 0, 0)
    hd = (CA_HEADS, M, CA_HD)
    return pl.pallas_call(
        body, name="ca_bwd", grid=(n,),
        in_specs=[pl.BlockSpec((tm, D), row), pl.BlockSpec((tm, D), row), pl.BlockSpec((1, D), fix),
                  pl.BlockSpec((D, D), fix), pl.BlockSpec((1, CA_HD), fix), pl.BlockSpec(hd, fix3),
                  pl.BlockSpec(hd, fix3), pl.BlockSpec((D, D), fix)],
        out_specs=[pl.BlockSpec((tm, D), row), pl.BlockSpec((D, D), fix), pl.BlockSpec((D, D), fix),
                   pl.BlockSpec(hd, fix3), pl.BlockSpec(hd, fix3), pl.BlockSpec((1, CA_HD), fix),
                   pl.BlockSpec((1, D), fix)],
        out_shape=[S((T, D), F32), S((D, D), BF), S((D, D), BF), S(hd, F32), S(hd, F32), S((1, CA_HD), F32),
                   S((1, D), F32)],
        scratch_shapes=[pltpu.VMEM((D, D), F32), pltpu.VMEM((D, D), F32), pltpu.VMEM((tm, D), BF),
                        pltpu.VMEM((tm, D), BF)],
        compiler_params=_cp(1))(x, dy, g_ca, wcq, g_cq, kn, vv, wco)


def _ca_kv_bwd(mem, g_mem, mn, kraw, dkn, dvv, wckv, g_ck):
    M, D = mem.shape

    def body(m_ref, g_ref, mn_ref, kr_ref, dkn_ref, dv_ref, w_ref, gk_ref, dw_ref, dgk_ref, dgm_ref):
        mn = mn_ref[...]
        dmn = jnp.zeros((M, D), F32)
        gk_rows = None
        for h in range(CA_HEADS):
            dkr, gr = _norm_bwd(dkn_ref[h], kr_ref[h], gk_ref[...])
            gk_rows = gr if gk_rows is None else gk_rows + gr
            dkb = dkr.astype(BF)
            dvb = dv_ref[h].astype(BF)
            dw_ref[h] = _tn(mn, dkb).astype(BF)
            dw_ref[CA_HEADS + h] = _tn(mn, dvb).astype(BF)
            dmn = dmn + _nt(dkb, w_ref[h]) + _nt(dvb, w_ref[CA_HEADS + h])
        dgk_ref[...] = jnp.sum(gk_rows, axis=0, keepdims=True)
        mf = m_ref[...]
        dgm_ref[...] = jnp.sum(dmn * (mf * _rstd(mf)), axis=0, keepdims=True)

    return pl.pallas_call(
        body, name="ca_kv_bwd",
        out_shape=[S((2 * CA_HEADS, D, CA_HD), BF), S((1, CA_HD), F32), S((1, D), F32)],
        compiler_params=pltpu.CompilerParams(vmem_limit_bytes=VMEM_LIMIT))(mem, g_mem, mn, kraw, dkn, dvv, wckv, g_ck)


def _after(g, token):
    return g if token is None else lax.optimization_barrier((g, token))[0]


def _local_step(x, mem, target, small, weights, emit):
    T, D = x.shape
    p = small
    bf128 = jnp.pad(p["b_f"], ((0, 0), (0, LANES - FOX_HEADS)))
    b_st = p["b_s"].T

    wup1 = weights("ffn1_up", x)["wup1"]
    a1, h1 = _ffn_up("ffn1_up", x, p["g_ffn1"], wup1)
    wdn1 = weights("ffn1_dn", h1)["wdn1"]
    x1 = _ffn_down("ffn1_down", a1, wdn1, x)
    wm = weights("mix", x1)
    z, h2 = _mix_proj(x1, p["g_mix"], wm["wz"])
    qf, ka, va, yg = _mix_prep(z, bf128, p["g_q"], p["g_k"], p["g_sgu"], p["w_s"], b_st, p["g_gmlp_o"])
    attn, lse = _fox_fwd(qf, ka, va)
    x2 = _mix_out(attn, yg, p["g_fox_o"], wm["wout"], x1)
    wc = weights("ca", x2)
    mn, kraw, ckn, cvv = _ca_kv(mem, p["g_mem"], wc["wckv"], p["g_ck"])
    x3 = _ca_fwd(x2, p["g_ca"], wc["wcq"], p["g_cq"], ckn, cvv, wc["wco"])
    w2 = weights("ffn2", x3)
    a2, h4 = _ffn_up("ffn2_up", x3, p["g_ffn2"], w2["wup2"])
    dy4, dy4b, sq = _ffn_down_loss("ffn2_down", a2, w2["wdn2"], x3, target)

    gs = {}
    dgu2 = _ffn_bwd_act("ffn2_bwd_act", dy4b, h4, w2["wup2"], w2["wdn2"])
    tok = emit("ffn2", {"wup2": _ffn_dwup("ffn2", h4, dgu2), "wdn2": _ffn_dwdn("ffn2", a2, dy4b)})
    dx3, gs["g_ffn2"] = _ffn_dx("ffn2_dx", dgu2, w2["wup2"], x3, _after(p["g_ffn2"], tok), dy4)

    dx2, dwcq, dwco, dckn, dcvv, gs["g_cq"], gs["g_ca"] = _ca_bwd(
        x2, dx3, p["g_ca"], wc["wcq"], p["g_cq"], ckn, cvv, wc["wco"])
    dwckv, gs["g_ck"], gs["g_mem"] = _ca_kv_bwd(mem, p["g_mem"], mn, kraw, dckn, dcvv, wc["wckv"], p["g_ck"])

    qb, dob, dyg, dwout, gs["g_fox_o"] = _mix_out_bwd(dx2, attn, yg, p["g_fox_o"], wm["wout"], qf, lse)
    dq, dk, dv = _fox_bwd(qb, ka, va, dob)
    dz, gs["g_q"], gs["g_k"], gs["g_sgu"], gs["g_gmlp_o"], gs["w_s"], dbst, dbf = _mix_prep_bwd(
        z, dq, dk, dv, dyg, bf128, p["g_q"], p["g_k"], p["g_sgu"], p["w_s"], b_st, p["g_gmlp_o"])
    gs["b_s"] = dbst.T
    gs["b_f"] = dbf[:, :FOX_HEADS]
    tok_ws = emit("w_s", {"w_s": gs["w_s"]})
    tk = _tile(T, 1024)
    zb = ZW // 3
    dwz = _tn_matmul(
        "mix_dwz", dz, pl.BlockSpec((tk, zb), lambda j, k: (k, j)), h2, pl.BlockSpec((tk, D), lambda j, k: (k, 0)),
        S((ZW, D), F32), pl.BlockSpec((zb, D), lambda j, k: (j, 0)), (3, T // tk), (zb, D))
    tok = emit("mid", {"wcq": dwcq, "wco": dwco, "wckv": dwckv, "wout": dwout, "wz": dwz})
    dx1, dx1b, gs["g_mix"] = _mix_proj_bwd(dz, wm["wz"], x1, _after(_after(p["g_mix"], tok), tok_ws), dx2)

    dgu1 = _ffn_bwd_act("ffn1_bwd_act", dx1b, h1, wup1, wdn1)
    tok = emit("ffn1_dn", {"wdn1": _ffn_dwdn("ffn1", a1, dx1b)})
    tok = emit("ffn1_up", {"wup1": _ffn_dwup("ffn1", h1, _after(dgu1, tok))})
    dx0, gs["g_ffn1"] = _ffn_dx("ffn1_dx", dgu1, wup1, x, _after(p["g_ffn1"], tok), dx1)
    return sq, dx0, gs


MESH = pl.DeviceIdType.MESH
HBM_SPEC = pl.BlockSpec(memory_space=pltpu.HBM)
N_PEER = N_DEV - 1


def _place():
    return lax.axis_index("x"), lax.axis_index("y"), lax.axis_index("c")


def _slot(px, py, pc):
    return 4 * px + 2 * py + pc


SEM_SPEC = pl.BlockSpec(memory_space=pltpu.SEMAPHORE)
ANY_SPEC = pl.BlockSpec(memory_space=pl.ANY)
DATAFLOW = pltpu.SideEffectType.DATAFLOW_SIDE_EFFECTING


def _hbm(a):
    return pltpu.with_memory_space_constraint(a, pltpu.HBM)


def _peer(x, y, c, r):
    return (1 - x if r & 4 else x, 1 - y if r & 2 else y, 1 - c if r & 1 else c)


def _place_own(srcs, whole):
    my = _slot(*_place())
    lands = []
    for s in srcs:
        blk = s[None] if whole else lax.dynamic_slice_in_dim(s, my, 1, 0)
        shape = (N_DEV,) + s.shape if whole else s.shape
        lands.append(lax.dynamic_update_slice_in_dim(lax.empty(shape, s.dtype), blk, my, 0))
    return lands


def _copy_start(name, srcs, lands, whole):
    n = len(srcs)

    def body(*refs):
        src, land = refs[:n], refs[n:2 * n]
        send, recv = refs[2 * n:3 * n], refs[3 * n:4 * n]
        token = refs[6 * n]
        x, y, c = _place()
        my = _slot(x, y, c)
        for a in range(n):
            for r in range(1, N_DEV):
                p = _peer(x, y, c, r)
                pltpu.make_async_remote_copy(
                    src_ref=src[a] if whole else src[a].at[_slot(*p)], dst_ref=land[a].at[my],
                    send_sem=send[a].at[r - 1], recv_sem=recv[a].at[r - 1], device_id=p, device_id_type=MESH).start()
        token[...] = jnp.zeros_like(token)

    out = pl.pallas_call(
        body, name=name,
        out_shape=([pltpu.SemaphoreType.DMA((N_PEER,))] * (2 * n)
                   + [pltpu.HBM(s.shape, s.dtype) for s in srcs] + [pltpu.HBM(s.shape, s.dtype) for s in lands]
                   + [S((8, LANES), F32)]),
        in_specs=[HBM_SPEC] * (2 * n),
        out_specs=[SEM_SPEC] * (2 * n) + [HBM_SPEC] * (2 * n) + [pl.BlockSpec(memory_space=pltpu.VMEM)],
        input_output_aliases={i: 2 * n + i for i in range(2 * n)},
        compiler_params=pltpu.CompilerParams(has_side_effects=DATAFLOW),
    )(*[_hbm(s) for s in srcs], *[_hbm(s) for s in lands])
    return out[:n], out[n:2 * n], out[2 * n:3 * n], out[3 * n:4 * n], out[4 * n]


def _copy_wait(name, srcs, lands, send, recv, after, whole):
    n = len(srcs)

    def body(*refs):
        src, land = refs[:n], refs[n:2 * n]
        snd, rcv = refs[2 * n:3 * n], refs[3 * n:4 * n]
        x, y, c = _place()
        for a in range(n):
            for r in range(1, N_DEV):
                p = _peer(x, y, c, r)
                ps = _slot(*p)
                cp = pltpu.make_async_remote_copy(
                    src_ref=src[a] if whole else src[a].at[ps], dst_ref=land[a].at[ps],
                    send_sem=snd[a].at[r - 1], recv_sem=rcv[a].at[r - 1], device_id=p, device_id_type=MESH)
                cp.wait_send()
                cp.wait_recv()

    out = pl.pallas_call(
        body, name=name,
        out_shape=[pltpu.HBM(s.shape, s.dtype) for s in srcs] + [pltpu.HBM(s.shape, s.dtype) for s in lands],
        in_specs=[HBM_SPEC] * (2 * n) + [SEM_SPEC] * (2 * n) + [ANY_SPEC],
        out_specs=[HBM_SPEC] * (2 * n),
        input_output_aliases={i: i for i in range(2 * n)},
        compiler_params=pltpu.CompilerParams(has_side_effects=DATAFLOW),
    )(*srcs, *lands, *send, *recv, after)
    return out[n:]


def _adamw(w, g, m, v):
    m2 = ADAM_B1 * m + (1.0 - ADAM_B1) * g
    v2 = ADAM_B2 * v + (1.0 - ADAM_B2) * (g * g)
    m_hat = m2 / (1.0 - ADAM_B1 ** ADAM_STEP)
    v_hat = v2 / (1.0 - ADAM_B2 ** ADAM_STEP)
    delta = -ADAM_LR * (m_hat / (jnp.sqrt(v_hat) + ADAM_EPS) + ADAM_WD * w)
    return delta, m2, v2


def _adamw_big(name, slots, w, m, v):
    R, C = w.shape
    tr = next((t for t in (256, 352) if R % t == 0), R)

    def body(s_ref, w_ref, m_ref, v_ref, g_ref, d_ref, m2_ref, v2_ref):
        g = s_ref[0].astype(F32)
        for k in range(1, N_DEV):
            g = g + s_ref[k].astype(F32)
        d, m2, v2 = _adamw(w_ref[...], g, m_ref[...], v_ref[...])
        g_ref[...] = g
        d_ref[...] = d
        m2_ref[...] = m2
        v2_ref[...] = v2

    row = pl.BlockSpec((tr, C), lambda i: (i, 0))
    return pl.pallas_call(
        body, name=name, grid=(R // tr,),
        in_specs=[pl.BlockSpec((N_DEV, tr, C), lambda i: (0, i, 0)), row, row, row],
        out_specs=[row] * 4, out_shape=[S((R, C), F32)] * 4,
        compiler_params=_cp(1))(slots, w, m, v)


TINY_ROWS = (("b_s", 8), ("g_ffn1", 8), ("g_mix", 8), ("g_ca", 8), ("g_mem", 8), ("g_ffn2", 8), ("g_sgu", 4),
             ("g_fox_o", 4), ("g_gmlp_o", 4), ("g_cq", 2), ("g_ck", 2), ("g_q", 1), ("g_k", 1), ("b_f", 1))
TINY_P = 72


def _pack_tiny(d):
    rows = []
    for name, r in TINY_ROWS:
        flat = d[name].reshape(-1)
        rows.append(jnp.pad(flat, (0, r * LANES - flat.shape[0])).reshape(r, LANES))
    used = sum(r for _, r in TINY_ROWS)
    rows.append(jnp.zeros((TINY_P - used, LANES), F32))
    return jnp.concatenate(rows, axis=0)


def _unpack_tiny(packed, shapes):
    out, at = {}, 0
    for name, r in TINY_ROWS:
        shape = shapes[name]
        size = 1
        for s in shape:
            size *= s
        out[name] = packed[at:at + r].reshape(-1)[:size].reshape(shape)
        at += r
    return out


WEIGHTS =('g_ffn1', 'w_ffn1_in', 'w_ffn1_out', 'g_mix', 'w_in', 'b_f', 'g_q', 'g_k', 'g_sgu', 'w_s', 'b_s',
           'g_fox_o', 'g_gmlp_o', 'w_out', 'g_ca', 'g_mem', 'w_cq', 'w_ckv', 'g_cq', 'g_ck', 'w_co', 'g_ffn2',
           'w_ffn2_in', 'w_ffn2_out')
BIG = ('w_ffn1_in', 'w_ffn1_out', 'w_in', 'w_out', 'w_cq', 'w_ckv', 'w_co', 'w_ffn2_in', 'w_ffn2_out')
TRANSPOSED = ('w_ffn1_in', 'w_in', 'w_ffn2_in')
GATHER_GROUPS = {"ffn1_up": ("w_ffn1_in",), "ffn1_dn": ("w_ffn1_out",), "mix": ("w_in", "w_out"),
                 "ca": ("w_cq", "w_ckv", "w_co"), "ffn2": ("w_ffn2_in", "w_ffn2_out")}
GATHER_STAGES = ((None, ("w_ffn1_in", "w_ffn1_out", "w_in", "w_out")),
                 ("ffn1_up", ("w_cq", "w_ckv", "w_co", "w_ffn2_in", "w_ffn2_out")))
QKV_W = 3 * FOX_W
UV_OFF = QKV_W + FOX_HEADS


def kernel(x, mem, g_ffn1, w_ffn1_in, w_ffn1_out, g_mix, w_in, b_f, g_q, g_k, g_sgu, w_s, b_s, g_fox_o, g_gmlp_o, w_out, g_ca, g_mem, w_cq, w_ckv, g_cq, g_ck, w_co, g_ffn2, w_ffn2_in, w_ffn2_out, loss_target, m_g_ffn1, m_w_ffn1_in, m_w_ffn1_out, m_g_mix, m_w_in, m_b_f, m_g_q, m_g_k, m_g_sgu, m_w_s, m_b_s, m_g_fox_o, m_g_gmlp_o, m_w_out, m_g_ca, m_g_mem, m_w_cq, m_w_ckv, m_g_cq, m_g_ck, m_w_co, m_g_ffn2, m_w_ffn2_in, m_w_ffn2_out, v_g_ffn1, v_w_ffn1_in, v_w_ffn1_out, v_g_mix, v_w_in, v_b_f, v_g_q, v_g_k, v_g_sgu, v_w_s, v_b_s, v_g_fox_o, v_g_gmlp_o, v_w_out, v_g_ca, v_g_mem, v_w_cq, v_w_ckv, v_g_cq, v_g_ck, v_w_co, v_g_ffn2, v_w_ffn2_in, v_w_ffn2_out):
    args = dict(locals())
    w = {n: args[n] for n in WEIGHTS}
    mo = {n: args["m_" + n] for n in WEIGHTS}
    vo = {n: args["v_" + n] for n in WEIGHTS}
    D = D_MODEL

    def local(n, a):
        return a[0].T if n in TRANSPOSED else a[0]

    shards = {n: local(n, w[n]).astype(BF) for n in BIG}
    fb = shards["w_ffn1_in"].shape[0]
    handles = {}

    def start_gather(stage, names, arrays):
        snd, rcv, src, land, _ = _copy_start("gather_start_%d" % stage, arrays, _place_own(arrays, True), True)
        for i, n in enumerate(names):
            handles[n] = (src[i], land[i], snd[i], rcv[i])

    start_gather(0, GATHER_STAGES[0][1], [shards[n] for n in GATHER_STAGES[0][1]])

    def weights(group, after):
        names = GATHER_GROUPS[group]
        hs = [handles[n] for n in names]
        got = _copy_wait("gather_wait_" + group, [h[0] for h in hs], [h[1] for h in hs], [h[2] for h in hs],
                         [h[3] for h in hs], after, True)
        for stage, (trigger, members) in enumerate(GATHER_STAGES):
            if trigger == group:
                held = lax.optimization_barrier((tuple(shards[n] for n in members), got[0]))[0]
                start_gather(stage, members, list(held))
        got = dict(zip(names, got))
        if group == "ffn1_up":
            return {"wup1": got["w_ffn1_in"].reshape(2, N_FFN_BLK, fb, D)}
        if group == "ffn1_dn":
            return {"wdn1": got["w_ffn1_out"].reshape(N_FFN_BLK, fb, D)}
        if group == "mix":
            full = got["w_in"].reshape(-1, D)
            wz = jnp.concatenate([full[:QKV_W], full[UV_OFF:], full[QKV_W:UV_OFF],
                                  jnp.zeros((LANES - FOX_HEADS, D), BF)], axis=0)
            return {"wz": wz, "wout": got["w_out"].reshape(D, D)}
        if group == "ca":
            return {"wcq": got["w_cq"].reshape(D, D), "wco": got["w_co"].reshape(D, D), "wckv": got["w_ckv"]}
        return {"wup2": got["w_ffn2_in"].reshape(2, N_FFN_BLK, fb, D),
                "wdn2": got["w_ffn2_out"].reshape(N_FFN_BLK, fb, D)}

    flying = {}

    def emit(group, g):
        if group == "w_s":
            part = [g["w_s"].reshape(-1, LANES)]
            *copies, token = _copy_start("w_s_start", part, _place_own(part, True), True)
            flying[group] = copies
            return token
        if group == "ffn2":
            parts = {"w_ffn2_in": g["wup2"], "w_ffn2_out": g["wdn2"].reshape(N_DEV, -1, D)}
        elif group == "ffn1_dn":
            parts = {"w_ffn1_out": g["wdn1"].reshape(N_DEV, -1, D)}
        elif group == "ffn1_up":
            parts = {"w_ffn1_in": g["wup1"]}
        else:
            gz = g["wz"]
            g_in = jnp.concatenate([gz[:QKV_W], gz[Z_F:Z_F + FOX_HEADS], gz[QKV_W:Z_F]], axis=0)
            parts = {"w_in": g_in.reshape(N_DEV, -1, D).astype(BF),
                     "w_out": g["wout"].reshape(N_DEV, -1, D), "w_cq": g["wcq"].reshape(N_DEV, -1, D),
                     "w_co": g["wco"].reshape(N_DEV, -1, D), "w_ckv": g["wckv"]}
        names = list(parts)
        srcs = [parts[n] for n in names]
        *copies, token = _copy_start("exchange_start_" + group, srcs, _place_own(srcs, False), False)
        flying[group] = (names, copies)
        return token

    tiny_names = [n for n, _ in TINY_ROWS]
    small = {n: (w[n][0] if n == "b_s" else w[n]) for n in tiny_names}
    small["w_s"] = w["w_s"][0]

    sq, dx0, gs = _local_step(x[0], mem[0], loss_target[0], small, weights, emit)
    loss = lax.psum(sq[0, 0], ("x", "y", "c")) * (0.5 / D)

    sm_parts = [_pack_tiny(gs)]
    sm_snd, sm_rcv, sm_src, sm_land, sm_token = _copy_start("tiny_start", sm_parts, _place_own(sm_parts, True), True)

    grad, delta, new_m, new_v = {}, {}, {}, {}

    def update(group, after):
        names, (snd, rcv, srcs, lands) = flying[group]
        slots = _copy_wait("exchange_wait_" + group, srcs, lands, snd, rcv, after, False)
        for n, sl in zip(names, slots):
            g, d, m2, v2 = _adamw_big("adamw_" + n, sl, local(n, w[n]), local(n, mo[n]), local(n, vo[n]))
            grad[n], delta[n], new_m[n], new_v[n] = (
                (t.T if n in TRANSPOSED else t).reshape(w[n].shape) for t in (g, d, m2, v2))
        return d

    last = update("ffn2", sm_token)
    last = update("mid", last)
    last = update("ffn1_dn", last)
    last = update("ffn1_up", last)
    ws_snd, ws_rcv, ws_src, ws_land = flying["w_s"]
    ws_all, = _copy_wait("w_s_wait", ws_src, ws_land, ws_snd, ws_rcv, last, True)
    tiny_all, = _copy_wait("tiny_wait", sm_src, sm_land, sm_snd, sm_rcv, ws_all, True)
    ws_shape = w["w_s"].shape
    for store, t in zip((grad, delta, new_m, new_v), _adamw_big(
            "adamw_w_s", ws_all, *[a["w_s"].reshape(-1, LANES) for a in (w, mo, vo)])):
        store["w_s"] = t.reshape(ws_shape)
    shapes = {n: w[n].shape for n in tiny_names}
    for store, t in zip((grad, delta, new_m, new_v), _adamw_big(
            "adamw_tiny", tiny_all, *[_pack_tiny({n: a[n] for n in tiny_names}) for a in (w, mo, vo)])):
        store.update(_unpack_tiny(t, shapes))

    return (loss, dx0[None], *[grad[n] for n in WEIGHTS], *[delta[n] for n in WEIGHTS],
            *[new_m[n] for n in WEIGHTS], *[new_v[n] for n in WEIGHTS])
```

```python
import functools

import jax
import jax.numpy as jnp
from jax import lax
from jax.experimental import pallas as pl
from jax.experimental.pallas import tpu as pltpu

F32 = jnp.float32
BF = jnp.bfloat16
S = jax.ShapeDtypeStruct

N_DEV = 8
D_MODEL = 1024
FOX_HEADS, FOX_HD = 8, 64
FOX_W = 512
GMLP_G, GMLP_GD = 8, 64
GMLP_W = 512
CHUNK = 128
CA_HEADS, CA_HD = 4, 256
N_FFN_BLK = 4
ZW = 2688
Z_Q, Z_K, Z_V, Z_U, Z_G, Z_F = 0, 512, 1024, 1536, 2048, 2560
EPS = 1e-6
NEG = -1e30
LANES = 128

ADAM_LR, ADAM_B1, ADAM_B2, ADAM_EPS, ADAM_WD, ADAM_STEP = 0.001, 0.9, 0.999, 1e-08, 0.01, 10

VMEM_LIMIT = 52 * 2 ** 20


def _cp(n_axes):
    return pltpu.CompilerParams(dimension_semantics=("arbitrary",) * n_axes, vmem_limit_bytes=VMEM_LIMIT)


def _nn(a, b):
    return jnp.dot(a, b, preferred_element_type=F32)


def _nt(a, b):
    return lax.dot_general(a, b, (((1,), (1,)), ((), ())), preferred_element_type=F32)


def _tn(a, b):
    return lax.dot_general(a, b, (((0,), (0,)), ((), ())), preferred_element_type=F32)


def _hi(a, b):
    return jnp.dot(a, b, precision=lax.Precision.HIGHEST, preferred_element_type=F32)


def _rstd(x):
    return lax.rsqrt(jnp.mean(x * x, axis=-1, keepdims=True) + EPS)


def _norm_bwd(dy, x, g):
    r = _rstd(x)
    xh = x * r
    dxh = dy * g
    dx = r * (dxh - xh * jnp.mean(dxh * xh, axis=-1, keepdims=True))
    return dx, dy * xh


def _acc_rows(ref, first, val):
    srow = jnp.sum(val, axis=0, keepdims=True)

    @pl.when(first)
    def _():
        ref[...] = srow

    @pl.when(jnp.logical_not(first))
    def _():
        ref[...] += srow


def _gelu(x):
    c = 0.7978845608028654
    return 0.5 * x * (1.0 + jnp.tanh(c * (x + 0.044715 * x * x * x)))


def _gelu_grad(x):
    c = 0.7978845608028654
    t = jnp.tanh(c * (x + 0.044715 * x * x * x))
    return 0.5 * (1.0 + t) + 0.5 * x * (1.0 - t * t) * c * (1.0 + 3 * 0.044715 * x * x)


def _tile(n, pref):
    return pref if n % pref == 0 else n


def _ffn_up(name, x, g, wup):
    T, D = x.shape
    FB = wup.shape[-2]
    tm = _tile(T, 1024)

    def body(x_ref, g_ref, w_ref, a_ref, h_ref):
        @pl.when(pl.program_id(1) == 0)
        def _():
            xf = x_ref[...]
            h_ref[...] = (xf * _rstd(xf) * g_ref[...]).astype(BF)

        hb = h_ref[...]
        gg = _nt(hb, w_ref[0])
        uu = _nt(hb, w_ref[1])
        a_ref[...] = (gg * jax.nn.sigmoid(gg) * uu).astype(BF)

    return pl.pallas_call(
        body, name=name, grid=(T // tm, N_FFN_BLK),
        in_specs=[pl.BlockSpec((tm, D), lambda i, j: (i, 0)),
                  pl.BlockSpec((1, D), lambda i, j: (0, 0)),
                  pl.BlockSpec((2, None, FB, D), lambda i, j: (0, j, 0, 0))],
        out_specs=[pl.BlockSpec((None, tm, FB), lambda i, j: (j, i, 0)),
                   pl.BlockSpec((tm, D), lambda i, j: (i, 0))],
        out_shape=[S((N_FFN_BLK, T, FB), BF), S((T, D), BF)],
        compiler_params=_cp(2))(x, g, wup)


def _ffn_down(name, a, wdn, x):
    _, T, FB = a.shape
    D = x.shape[1]
    tm = _tile(T, 512)

    def body(a_ref, w_ref, x_ref, o_ref):
        j = pl.program_id(1)
        p = 0.5 * _nn(a_ref[...], w_ref[...])

        @pl.when(j == 0)
        def _():
            o_ref[...] = x_ref[...] + p

        @pl.when(j > 0)
        def _():
            o_ref[...] += p

    return pl.pallas_call(
        body, name=name, grid=(T // tm, N_FFN_BLK),
        in_specs=[pl.BlockSpec((None, tm, FB), lambda i, j: (j, i, 0)),
                  pl.BlockSpec((None, FB, D), lambda i, j: (j, 0, 0)),
                  pl.BlockSpec((tm, D), lambda i, j: (i, 0))],
        out_specs=pl.BlockSpec((tm, D), lambda i, j: (i, 0)),
        out_shape=S((T, D), F32),
        compiler_params=_cp(2))(a, wdn, x)


def _ffn_down_loss(name, a, wdn, x, target):
    _, T, FB = a.shape
    D = x.shape[1]
    tm = _tile(T, 512)

    def body(a_ref, w_ref, x_ref, t_ref, d_ref, db_ref, loss_ref, acc_ref):
        i, j = pl.program_id(0), pl.program_id(1)
        p = 0.5 * _nn(a_ref[...], w_ref[...])

        @pl.when(j == 0)
        def _():
            acc_ref[...] = x_ref[...] + p

        @pl.when(j > 0)
        def _():
            acc_ref[...] += p

        @pl.when(j == N_FFN_BLK - 1)
        def _():
            diff = acc_ref[...] - t_ref[...]
            dy = diff * (1.0 / D)
            d_ref[...] = dy
            db_ref[...] = dy.astype(BF)
            sq = jnp.zeros((8, LANES), F32) + jnp.sum(diff * diff)

            @pl.when(i == 0)
            def _():
                loss_ref[...] = sq

            @pl.when(i > 0)
            def _():
                loss_ref[...] += sq

    return pl.pallas_call(
        body, name=name, grid=(T // tm, N_FFN_BLK),
        in_specs=[pl.BlockSpec((None, tm, FB), lambda i, j: (j, i, 0)),
                  pl.BlockSpec((None, FB, D), lambda i, j: (j, 0, 0)),
                  pl.BlockSpec((tm, D), lambda i, j: (i, 0)),
                  pl.BlockSpec((tm, D), lambda i, j: (i, 0))],
        out_specs=[pl.BlockSpec((tm, D), lambda i, j: (i, 0)),
                   pl.BlockSpec((tm, D), lambda i, j: (i, 0)),
                   pl.BlockSpec((8, LANES), lambda i, j: (0, 0))],
        out_shape=[S((T, D), F32), S((T, D), BF), S((8, LANES), F32)],
        scratch_shapes=[pltpu.VMEM((tm, D), F32)],
        compiler_params=_cp(2))(a, wdn, x, target)


def _ffn_bwd_act(name, dyb, h, wup, wdn):
    T, D = h.shape
    FB = wup.shape[-2]
    tm = _tile(T, 1024)

    def body(d_ref, h_ref, wu_ref, wd_ref, o_ref):
        da = 0.5 * _nt(d_ref[...], wd_ref[...])
        hb = h_ref[...]
        gg = _nt(hb, wu_ref[0])
        uu = _nt(hb, wu_ref[1])
        sg = jax.nn.sigmoid(gg)
        o_ref[0] = (da * uu * (sg * (1.0 + gg * (1.0 - sg)))).astype(BF)
        o_ref[1] = (da * (gg * sg)).astype(BF)

    return pl.pallas_call(
        body, name=name, grid=(T // tm, N_FFN_BLK),
        in_specs=[pl.BlockSpec((tm, D), lambda i, j: (i, 0)),
                  pl.BlockSpec((tm, D), lambda i, j: (i, 0)),
                  pl.BlockSpec((2, None, FB, D), lambda i, j: (0, j, 0, 0)),
                  pl.BlockSpec((None, FB, D), lambda i, j: (j, 0, 0))],
        out_specs=pl.BlockSpec((2, None, tm, FB), lambda i, j: (0, j, i, 0)),
        out_shape=S((2, N_FFN_BLK, T, FB), BF),
        compiler_params=_cp(2))(dyb, h, wup, wdn)


def _ffn_dx(name, dgu, wup, x, g, dy):
    T, D = x.shape
    FB = wup.shape[-2]
    tm = _tile(T, 1024)

    def body(d_ref, w_ref, x_ref, g_ref, dy_ref, dx_ref, dg_ref, acc_ref):
        i, j = pl.program_id(0), pl.program_id(1)
        p = _nn(d_ref[0], w_ref[0]) + _nn(d_ref[1], w_ref[1])

        @pl.when(j == 0)
        def _():
            acc_ref[...] = p

        @pl.when(j > 0)
        def _():
            acc_ref[...] += p

        @pl.when(j == N_FFN_BLK - 1)
        def _():
            dx, dgr = _norm_bwd(acc_ref[...], x_ref[...], g_ref[...])
            dx_ref[...] = dx + dy_ref[...]
            _acc_rows(dg_ref, i == 0, dgr)

    return pl.pallas_call(
        body, name=name, grid=(T // tm, N_FFN_BLK),
        in_specs=[pl.BlockSpec((2, None, tm, FB), lambda i, j: (0, j, i, 0)),
                  pl.BlockSpec((2, None, FB, D), lambda i, j: (0, j, 0, 0)),
                  pl.BlockSpec((tm, D), lambda i, j: (i, 0)),
                  pl.BlockSpec((1, D), lambda i, j: (0, 0)),
                  pl.BlockSpec((tm, D), lambda i, j: (i, 0))],
        out_specs=[pl.BlockSpec((tm, D), lambda i, j: (i, 0)),
                   pl.BlockSpec((1, D), lambda i, j: (0, 0))],
        out_shape=[S((T, D), F32), S((1, D), F32)],
        scratch_shapes=[pltpu.VMEM((tm, D), F32)],
        compiler_params=_cp(2))(dgu, wup, x, g, dy)


def _tn_matmul(name, a, a_spec, b, b_spec, out_shape, out_spec, grid, acc_shape, scale=1.0, after=None):
    nk = grid[1]
    extra = [] if after is None else [after]

    def body(a_ref, b_ref, *rest):
        o_ref, acc_ref = rest[-2:]
        k = pl.program_id(1)
        p = _tn(a_ref[...], b_ref[...])

        @pl.when(k == 0)
        def _():
            acc_ref[...] = p

        @pl.when(k > 0)
        def _():
            acc_ref[...] += p

        @pl.when(k == nk - 1)
        def _():
            o_ref[...] = (acc_ref[...] * scale).astype(o_ref.dtype)

    return pl.pallas_call(
        body, name=name, grid=grid,
        in_specs=[a_spec, b_spec] + [pl.BlockSpec((8, LANES), lambda j, k: (0, 0)) for _ in extra],
        out_specs=out_spec, out_shape=out_shape,
        scratch_shapes=[pltpu.VMEM(acc_shape, F32)], compiler_params=_cp(2))(a, b, *extra)


def _ffn_dwup(name, h, dgu, after=None):
    T, D = h.shape
    FB = dgu.shape[-1]
    tk = _tile(T, 1024)
    return _tn_matmul(
        name + "_dwup", dgu.reshape(2 * N_FFN_BLK, T, FB), pl.BlockSpec((None, tk, FB), lambda j, k: (j, k, 0)),
        h, pl.BlockSpec((tk, D), lambda j, k: (k, 0)),
        S((2 * N_FFN_BLK, FB, D), BF), pl.BlockSpec((None, FB, D), lambda j, k: (j, 0, 0)),
        (2 * N_FFN_BLK, T // tk), (FB, D), after=after)


def _ffn_dwdn(name, a, dyb):
    _, T, FB = a.shape
    D = dyb.shape[1]
    tk = _tile(T, 1024)
    return _tn_matmul(
        name + "_dwdn", a, pl.BlockSpec((None, tk, FB), lambda j, k: (j, k, 0)),
        dyb, pl.BlockSpec((tk, D), lambda j, k: (k, 0)),
        S((N_FFN_BLK, FB, D), BF), pl.BlockSpec((None, FB, D), lambda j, k: (j, 0, 0)),
        (N_FFN_BLK, T // tk), (FB, D), scale=0.5)


def _mix_proj(x, g, wz):
    T, D = x.shape
    tm = _tile(T, 512)

    def body(x_ref, g_ref, w_ref, z_ref, h_ref):
        xf = x_ref[...]
        hb = (xf * _rstd(xf) * g_ref[...]).astype(BF)
        h_ref[...] = hb
        z_ref[...] = _nt(hb, w_ref[...])

    return pl.pallas_call(
        body, name="mix_proj", grid=(T // tm,),
        in_specs=[pl.BlockSpec((tm, D), lambda i: (i, 0)),
                  pl.BlockSpec((1, D), lambda i: (0, 0)),
                  pl.BlockSpec((ZW, D), lambda i: (0, 0))],
        out_specs=[pl.BlockSpec((tm, ZW), lambda i: (i, 0)),
                   pl.BlockSpec((tm, D), lambda i: (i, 0))],
        out_shape=[S((T, ZW), F32), S((T, D), BF)],
        compiler_params=_cp(1))(x, g, wz)


def _tri(n, lower):
    r = lax.broadcasted_iota(jnp.int32, (n, n), 0)
    c = lax.broadcasted_iota(jnp.int32, (n, n), 1)
    return (r >= c) if lower else (r <= c)


def _spatial_mix(vgn_b, ws_ref, bst, tm):
    tril = _tri(CHUNK, True)
    wms = [jnp.where(tril, ws_ref[g], 0.0).astype(BF) for g in range(GMLP_G)]
    rows = []
    for c in range(tm // CHUNK):
        cols = []
        for g in range(GMLP_G):
            vs = vgn_b[c * CHUNK:(c + 1) * CHUNK, g * GMLP_GD:(g + 1) * GMLP_GD]
            cols.append(_nn(wms[g], vs) + bst[:, g:g + 1])
        rows.append(jnp.concatenate(cols, axis=1))
    return jnp.concatenate(rows, axis=0), wms


HB = 128
AUG_W = FOX_HEADS * HB
COL_A, COL_B, COL_C = 64, 67, 70


def _spread_matrix():
    r = jnp.arange(FOX_W)
    return (jnp.arange(AUG_W)[None, :] == ((r // FOX_HD) * HB + r % FOX_HD)[:, None]).astype(BF)


def _piece_matrix(col):
    r = jnp.arange(LANES)
    dst = jnp.where(r < 3 * FOX_HEADS, (r % FOX_HEADS) * HB + col + r // FOX_HEADS, -1)
    return (jnp.arange(AUG_W)[None, :] == dst[:, None]).astype(BF)


def _ones_row(cols):
    c = jnp.arange(AUG_W) % HB
    hit = functools.reduce(jnp.logical_or, [(c >= a) & (c < a + 3) for a in cols])
    return hit.astype(F32)[None, :]


def _pieces(x):
    lane = lax.broadcasted_iota(jnp.int32, x.shape, 1)
    x = jnp.where(lane < FOX_HEADS, x, 0.0)
    hi = x.astype(BF).astype(F32)
    r1 = x - hi
    mid = r1.astype(BF).astype(F32)
    lo = (r1 - mid).astype(BF).astype(F32)
    return (hi + pltpu.roll(mid, FOX_HEADS, 1) + pltpu.roll(lo, 2 * FOX_HEADS, 1)).astype(BF)


def _mix_prep(z, bf128, g_q, g_k, g_sgu, w_s, b_st, g_go):
    T = z.shape[0]
    tm = _tile(T, 256)
    spread, pc_q, pc_k = _spread_matrix(), _piece_matrix(COL_A), _piece_matrix(COL_B)
    one_q, one_k, one_v = _ones_row([COL_B]), _ones_row([COL_A, COL_C]), _ones_row([COL_A])

    def body(z_ref, bf_ref, gq_ref, gk_ref, gs_ref, ws_ref, bst_ref, go_ref, sp_ref, pq_ref, pk_ref, oq_ref, ok_ref,
             ov_ref, q_ref, k_ref, v_ref, y_ref, carry_ref, qn_sc, kn_sc):
        i = pl.program_id(0)

        @pl.when(i == 0)
        def _():
            carry_ref[...] = jnp.zeros_like(carry_ref)

        for h in range(FOX_HEADS):
            hs = slice(h * FOX_HD, (h + 1) * FOX_HD)
            qh = z_ref[:, Z_Q + h * FOX_HD:Z_Q + (h + 1) * FOX_HD]
            kh = z_ref[:, Z_K + h * FOX_HD:Z_K + (h + 1) * FOX_HD]
            qn_sc[:, hs] = (qh * _rstd(qh) * gq_ref[...] * 0.125).astype(BF)
            kn_sc[:, hs] = (kh * _rstd(kh) * gk_ref[...]).astype(BF)

        fl = z_ref[:, Z_F:Z_F + LANES] + bf_ref[...]
        logf = jnp.minimum(fl, 0.0) - jnp.log1p(jnp.exp(-jnp.abs(fl)))
        csum = _hi(_tri(tm, True).astype(F32), logf) + carry_ref[...]
        carry_ref[...] = csum[tm - 1:tm, :]
        sp = sp_ref[...]
        q_ref[...] = (_nn(qn_sc[...], sp) + _nn(_pieces(csum), pq_ref[...]) + oq_ref[...]).astype(BF)
        k_ref[...] = (_nn(kn_sc[...], sp) + _nn(_pieces(-csum), pk_ref[...]) + ok_ref[...]).astype(BF)
        v_ref[...] = (_nn(z_ref[:, Z_V:Z_V + FOX_W].astype(BF), sp) + ov_ref[...]).astype(BF)

        u = _gelu(z_ref[:, Z_U:Z_U + GMLP_W])
        vg = _gelu(z_ref[:, Z_G:Z_G + GMLP_W])
        vgn = (vg * _rstd(vg) * gs_ref[...]).astype(BF)
        mixed, _ = _spatial_mix(vgn, ws_ref, bst_ref[...], tm)
        sgu = u * mixed
        y_ref[...] = (sgu * _rstd(sgu) * go_ref[...]).astype(BF)

    row = lambda i: (i, 0)
    fix2 = lambda i: (0, 0)
    return pl.pallas_call(
        body, name="mix_prep", grid=(T // tm,),
        in_specs=[pl.BlockSpec((tm, ZW), row),
                  pl.BlockSpec((1, LANES), fix2), pl.BlockSpec((1, FOX_HD), fix2), pl.BlockSpec((1, FOX_HD), fix2),
                  pl.BlockSpec((1, GMLP_W), fix2), pl.BlockSpec((GMLP_G, CHUNK, CHUNK), lambda i: (0, 0, 0)),
                  pl.BlockSpec((CHUNK, GMLP_G), fix2), pl.BlockSpec((1, GMLP_W), fix2),
                  pl.BlockSpec((FOX_W, AUG_W), fix2), pl.BlockSpec((LANES, AUG_W), fix2),
                  pl.BlockSpec((LANES, AUG_W), fix2), pl.BlockSpec((1, AUG_W), fix2), pl.BlockSpec((1, AUG_W), fix2),
                  pl.BlockSpec((1, AUG_W), fix2)],
        out_specs=[pl.BlockSpec((tm, AUG_W), row), pl.BlockSpec((tm, AUG_W), row), pl.BlockSpec((tm, AUG_W), row),
                   pl.BlockSpec((tm, GMLP_W), row)],
        out_shape=[S((T, AUG_W), BF), S((T, AUG_W), BF), S((T, AUG_W), BF), S((T, GMLP_W), BF)],
        scratch_shapes=[pltpu.VMEM((1, LANES), F32), pltpu.VMEM((tm, FOX_W), BF), pltpu.VMEM((tm, FOX_W), BF)],
        compiler_params=_cp(1))(z, bf128, g_q, g_k, g_sgu, w_s, b_st, g_go, spread, pc_q, pc_k, one_q, one_k, one_v)


def _fox_fwd(q, k, v):
    T = q.shape[0]
    tq = _tile(T, 512)
    nq = T // tq

    def body(q_ref, k_ref, v_ref, o_ref, lse_ref, m_sc, acc_sc):
        i, j = pl.program_id(0), pl.program_id(1)

        @pl.when(j == 0)
        def _():
            m_sc[...] = jnp.full(m_sc.shape, NEG, F32)
            acc_sc[...] = jnp.zeros_like(acc_sc)

        def step(masked):
            mask = _tri(tq, True) if masked else None
            for h in range(FOX_HEADS):
                hb = slice(h * HB, (h + 1) * HB)
                s = _nt(q_ref[:, hb], k_ref[:, hb])
                if masked:
                    s = jnp.where(mask, s, NEG)
                m_prev = m_sc[h]
                m_new = jnp.maximum(m_prev, jnp.broadcast_to(jnp.max(s, axis=1, keepdims=True), (tq, HB)))
                p = jnp.exp(s - jnp.tile(m_new, (1, tq // HB))).astype(BF)
                acc_sc[:, hb] = jnp.exp(m_prev - m_new) * acc_sc[:, hb] + _nn(p, v_ref[:, hb])
                m_sc[h] = m_new

        @pl.when(j < i)
        def _():
            step(False)

        @pl.when(j == i)
        def _():
            step(True)
            lse_ref[...] = jnp.zeros_like(lse_ref)
            for h in range(FOX_HEADS):
                l = acc_sc[:, h * HB + COL_A:h * HB + COL_A + 1]
                o_ref[:, h * FOX_HD:(h + 1) * FOX_HD] = acc_sc[:, h * HB:h * HB + FOX_HD] / l
                lse_ref[:, h:h + 1] = m_sc[h][:, 0:1] + jnp.log(l)

    qi = lambda i, j: (i, 0)
    kj = lambda i, j: (jnp.minimum(i, j), 0)
    return pl.pallas_call(
        body, name="fox_fwd", grid=(nq, nq),
        in_specs=[pl.BlockSpec((tq, AUG_W), qi), pl.BlockSpec((tq, AUG_W), kj), pl.BlockSpec((tq, AUG_W), kj)],
        out_specs=[pl.BlockSpec((tq, FOX_W), qi), pl.BlockSpec((tq, LANES), qi)],
        out_shape=[S((T, FOX_W), F32), S((T, LANES), F32)],
        scratch_shapes=[pltpu.VMEM((FOX_HEADS, tq, HB), F32), pltpu.VMEM((tq, AUG_W), F32)],
        compiler_params=_cp(2))(q, k, v)


def _fox_bwd(q, k, v, dob):
    T = q.shape[0]
    tq = _tile(T, 512)
    nq = T // tq
    half = AUG_W // 2
    hpg = FOX_HEADS // 2

    def body(q_ref, k_ref, v_ref, do_ref, dq_ref, dk_ref, dv_ref, dq_sc):
        j, i = pl.program_id(1), pl.program_id(2)

        @pl.when(jnp.logical_and(i == 0, j == 0))
        def _():
            dq_sc[...] = jnp.zeros_like(dq_sc)

        @pl.when(i == 0)
        def _():
            dk_ref[...] = jnp.zeros_like(dk_ref)
            dv_ref[...] = jnp.zeros_like(dv_ref)

        def step(masked):
            rows = pl.ds(pl.multiple_of(i * tq, tq), tq)
            mask = _tri(tq, True) if masked else None
            for h in range(hpg):
                hb = slice(h * HB, (h + 1) * HB)
                qh, kh, vh, doh = q_ref[:, hb], k_ref[:, hb], v_ref[:, hb], do_ref[:, hb]
                s = _nt(qh, kh)
                if masked:
                    s = jnp.where(mask, s, NEG)
                p = jnp.exp(s)
                dsb = (p * _nt(doh, vh)).astype(BF)
                dv_ref[:, hb] += _tn(p.astype(BF), doh)
                dk_ref[:, hb] += _tn(dsb, qh)
                dq_sc[rows, hb] += _nn(dsb, kh)

        @pl.when(i > j)
        def _():
            step(False)

        @pl.when(i == j)
        def _():
            step(True)
            dq_ref[...] = dq_sc[pl.ds(pl.multiple_of(j * tq, tq), tq), :]

    qi = lambda g, j, i: (jnp.maximum(i, j), g)
    kj = lambda g, j, i: (j, g)
    return pl.pallas_call(
        body, name="fox_bwd", grid=(2, nq, nq),
        in_specs=[pl.BlockSpec((tq, half), qi), pl.BlockSpec((tq, half), kj), pl.BlockSpec((tq, half), kj),
                  pl.BlockSpec((tq, half), qi)],
        out_specs=[pl.BlockSpec((tq, half), kj), pl.BlockSpec((tq, half), kj), pl.BlockSpec((tq, half), kj)],
        out_shape=[S((T, AUG_W), F32), S((T, AUG_W), F32), S((T, AUG_W), F32)],
        scratch_shapes=[pltpu.VMEM((T, half), F32)],
        compiler_params=_cp(3))(q, k, v, dob)


def _mix_out(attn, yg, g_fo, wout, x):
    T, D = x.shape
    tm = _tile(T, 512)

    def body(a_ref, y_ref, g_ref, w_ref, x_ref, o_ref):
        at = a_ref[...]
        yf = (at * _rstd(at) * g_ref[...]).astype(BF)
        o_ref[...] = x_ref[...] + _nn(yf, w_ref[:FOX_W, :]) + _nn(y_ref[...], w_ref[FOX_W:, :])

    row = lambda i: (i, 0)
    return pl.pallas_call(
        body, name="mix_out", grid=(T // tm,),
        in_specs=[pl.BlockSpec((tm, FOX_W), row), pl.BlockSpec((tm, GMLP_W), row),
                  pl.BlockSpec((1, FOX_W), lambda i: (0, 0)), pl.BlockSpec((D, D), lambda i: (0, 0)),
                  pl.BlockSpec((tm, D), row)],
        out_specs=pl.BlockSpec((tm, D), row),
        out_shape=S((T, D), F32),
        compiler_params=_cp(1))(attn, yg, g_fo, wout, x)


def _mix_out_bwd(dx, attn, yg, g_fo, wout, qf, lse):
    T, D = dx.shape
    tm = _tile(T, 256)
    n = T // tm
    spread, pc_l, pc_d = _spread_matrix(), _piece_matrix(COL_C), _piece_matrix(COL_A)

    def body(dx_ref, a_ref, y_ref, g_ref, w_ref, qf_ref, lse_ref, sp_ref, pl_ref, pd_ref,
             qb_ref, dob_ref, dyg_ref, dw_ref, dg_ref, acc_ref, dsum_ref):
        i = pl.program_id(0)
        dxb = dx_ref[...].astype(BF)
        at = a_ref[...]
        yf = (at * _rstd(at) * g_ref[...]).astype(BF)
        dy = _nt(dxb, w_ref[...])
        p_top = _tn(yf, dxb)
        p_bot = _tn(y_ref[...], dxb)

        @pl.when(i == 0)
        def _():
            acc_ref[:FOX_W, :] = p_top
            acc_ref[FOX_W:, :] = p_bot

        @pl.when(i > 0)
        def _():
            acc_ref[:FOX_W, :] += p_top
            acc_ref[FOX_W:, :] += p_bot

        @pl.when(i == n - 1)
        def _():
            dw_ref[...] = acc_ref[...].astype(BF)

        dat, dgr = _norm_bwd(dy[:, :FOX_W], at, g_ref[...])
        _acc_rows(dg_ref, i == 0, dgr)
        dyg_ref[...] = dy[:, FOX_W:]
        prod = dat * at
        dsum_ref[...] = jnp.zeros_like(dsum_ref)
        for h in range(FOX_HEADS):
            dsum_ref[:, h:h + 1] = jnp.sum(prod[:, h * FOX_HD:(h + 1) * FOX_HD], axis=1, keepdims=True)
        dob_ref[...] = (_nn(dat.astype(BF), sp_ref[...]) + _nn(_pieces(-dsum_ref[...]), pd_ref[...])).astype(BF)
        qb_ref[...] = (qf_ref[...].astype(F32) + _nn(_pieces(-lse_ref[...]), pl_ref[...])).astype(BF)

    row = lambda i: (i, 0)
    fix = lambda i: (0, 0)
    return pl.pallas_call(
        body, name="mix_out_bwd", grid=(n,),
        in_specs=[pl.BlockSpec((tm, D), row), pl.BlockSpec((tm, FOX_W), row), pl.BlockSpec((tm, GMLP_W), row),
                  pl.BlockSpec((1, FOX_W), fix), pl.BlockSpec((D, D), fix), pl.BlockSpec((tm, AUG_W), row),
                  pl.BlockSpec((tm, LANES), row), pl.BlockSpec((FOX_W, AUG_W), fix), pl.BlockSpec((LANES, AUG_W), fix),
                  pl.BlockSpec((LANES, AUG_W), fix)],
        out_specs=[pl.BlockSpec((tm, AUG_W), row), pl.BlockSpec((tm, AUG_W), row), pl.BlockSpec((tm, GMLP_W), row),
                   pl.BlockSpec((D, D), fix), pl.BlockSpec((1, FOX_W), fix)],
        out_shape=[S((T, AUG_W), BF), S((T, AUG_W), BF), S((T, GMLP_W), F32), S((D, D), BF), S((1, FOX_W), F32)],
        scratch_shapes=[pltpu.VMEM((D, D), F32), pltpu.VMEM((tm, LANES), F32)],
        compiler_params=_cp(1))(dx, attn, yg, g_fo, wout, qf, lse, spread, pc_l, pc_d)


def _mix_prep_bwd(z, dq, dk, dv, dyg, bf128, g_q, g_k, g_sgu, w_s, b_st, g_go):
    T = z.shape[0]
    tm = _tile(T, 256)
    n = T // tm

    def body(z_ref, dq_ref, dk_ref, dv_ref, dyg_ref, bf_ref, gq_ref, gk_ref, gs_ref, ws_ref,
             bst_ref, go_ref, dz_ref, dgq_ref, dgk_ref, dgs_ref, dgo_ref, dws_ref, dbst_ref, dbf_ref, carry_ref):
        i = pl.program_id(0)
        first = i == 0

        @pl.when(first)
        def _():
            carry_ref[...] = jnp.zeros_like(carry_ref)

        lane = lax.broadcasted_iota(jnp.int32, (tm, LANES), 1)
        dc = jnp.zeros((tm, LANES), F32)
        gq_rows, gk_rows = [], []
        for h in range(FOX_HEADS):
            hp = slice(h * HB, h * HB + FOX_HD)
            dqh, gqr = _norm_bwd(dq_ref[:, hp] * 0.125, z_ref[:, Z_Q + h * FOX_HD:Z_Q + (h + 1) * FOX_HD], gq_ref[...])
            dkh, gkr = _norm_bwd(dk_ref[:, hp], z_ref[:, Z_K + h * FOX_HD:Z_K + (h + 1) * FOX_HD], gk_ref[...])
            dz_ref[:, Z_Q + h * FOX_HD:Z_Q + (h + 1) * FOX_HD] = dqh.astype(BF)
            dz_ref[:, Z_K + h * FOX_HD:Z_K + (h + 1) * FOX_HD] = dkh.astype(BF)
            dz_ref[:, Z_V + h * FOX_HD:Z_V + (h + 1) * FOX_HD] = dv_ref[:, hp].astype(BF)
            dch = dq_ref[:, h * HB + COL_A:h * HB + COL_A + 1] - dk_ref[:, h * HB + COL_B:h * HB + COL_B + 1]
            dc = jnp.where(lane == h, dch, dc)
            gq_rows.append(gqr)
            gk_rows.append(gkr)
        _acc_rows(dgq_ref, first, functools.reduce(lambda a, b: a + b, gq_rows))
        _acc_rows(dgk_ref, first, functools.reduce(lambda a, b: a + b, gk_rows))

        dlogf = _hi(_tri(tm, False).astype(F32), dc) + carry_ref[...]
        carry_ref[...] = dlogf[0:1, :]
        fl = z_ref[:, Z_F:Z_F + LANES] + bf_ref[...]
        lane = lax.broadcasted_iota(jnp.int32, (tm, LANES), 1)
        df = jnp.where(lane < FOX_HEADS, dlogf * jax.nn.sigmoid(-fl), 0.0)
        dz_ref[:, Z_F:Z_F + LANES] = df.astype(BF)
        _acc_rows(dbf_ref, first, df)

        u_pre = z_ref[:, Z_U:Z_U + GMLP_W]
        vg_pre = z_ref[:, Z_G:Z_G + GMLP_W]
        u = _gelu(u_pre)
        vg = _gelu(vg_pre)
        vgn = (vg * _rstd(vg) * gs_ref[...]).astype(BF)
        bst = bst_ref[...]
        mixed, wms = _spatial_mix(vgn, ws_ref, bst, tm)
        sgu = u * mixed
        dsgu, gor = _norm_bwd(dyg_ref[...], sgu, go_ref[...])
        _acc_rows(dgo_ref, first, gor)
        du = dsgu * mixed
        dmixed = dsgu * u
        dmb = dmixed.astype(BF)
        tril = _tri(CHUNK, True)
        dvgn_rows = []
        dws = [None] * GMLP_G
        dbs = [None] * GMLP_G
        for c in range(tm // CHUNK):
            cs = slice(c * CHUNK, (c + 1) * CHUNK)
            cols = []
            for g in range(GMLP_G):
                gs = slice(g * GMLP_GD, (g + 1) * GMLP_GD)
                dmc = dmb[cs, gs]
                pw = _nt(dmc, vgn[cs, gs])
                pb = jnp.sum(dmixed[cs, gs], axis=1, keepdims=True)
                dws[g] = pw if dws[g] is None else dws[g] + pw
                dbs[g] = pb if dbs[g] is None else dbs[g] + pb
                cols.append(_tn(wms[g], dmc))
            dvgn_rows.append(jnp.concatenate(cols, axis=1))
        dvgn = jnp.concatenate(dvgn_rows, axis=0)
        dbs_t = jnp.concatenate(dbs, axis=1)
        for g in range(GMLP_G):
            dwg = jnp.where(tril, dws[g], 0.0)

            @pl.when(first)
            def _():
                dws_ref[g] = dwg

            @pl.when(jnp.logical_not(first))
            def _():
                dws_ref[g] += dwg

        @pl.when(first)
        def _():
            dbst_ref[...] = dbs_t

        @pl.when(jnp.logical_not(first))
        def _():
            dbst_ref[...] += dbs_t

        dvg, gsr = _norm_bwd(dvgn, vg, gs_ref[...])
        _acc_rows(dgs_ref, first, gsr)
        dz_ref[:, Z_U:Z_U + GMLP_W] = (du * _gelu_grad(u_pre)).astype(BF)
        dz_ref[:, Z_G:Z_G + GMLP_W] = (dvg * _gelu_grad(vg_pre)).astype(BF)

    rev = lambda i: (n - 1 - i, 0)
    fix = lambda i: (0, 0)
    fix3 = lambda i: (0, 0, 0)
    return pl.pallas_call(
        body, name="mix_prep_bwd", grid=(n,),
        in_specs=[pl.BlockSpec((tm, ZW), rev), pl.BlockSpec((tm, AUG_W), rev), pl.BlockSpec((tm, AUG_W), rev),
                  pl.BlockSpec((tm, AUG_W), rev), pl.BlockSpec((tm, GMLP_W), rev),
                  pl.BlockSpec((1, LANES), fix), pl.BlockSpec((1, FOX_HD), fix), pl.BlockSpec((1, FOX_HD), fix),
                  pl.BlockSpec((1, GMLP_W), fix), pl.BlockSpec((GMLP_G, CHUNK, CHUNK), fix3),
                  pl.BlockSpec((CHUNK, GMLP_G), fix), pl.BlockSpec((1, GMLP_W), fix)],
        out_specs=[pl.BlockSpec((tm, ZW), rev), pl.BlockSpec((1, FOX_HD), fix), pl.BlockSpec((1, FOX_HD), fix),
                   pl.BlockSpec((1, GMLP_W), fix), pl.BlockSpec((1, GMLP_W), fix),
                   pl.BlockSpec((GMLP_G, CHUNK, CHUNK), fix3), pl.BlockSpec((CHUNK, GMLP_G), fix),
                   pl.BlockSpec((1, LANES), fix)],
        out_shape=[S((T, ZW), BF), S((1, FOX_HD), F32), S((1, FOX_HD), F32), S((1, GMLP_W), F32), S((1, GMLP_W), F32),
                   S((GMLP_G, CHUNK, CHUNK), F32), S((CHUNK, GMLP_G), F32), S((1, LANES), F32)],
        scratch_shapes=[pltpu.VMEM((1, LANES), F32)],
        compiler_params=_cp(1))(z, dq, dk, dv, dyg, bf128, g_q, g_k, g_sgu, w_s, b_st, g_go)


def _mix_proj_bwd(dz, wz, x, g, dy):
    T, D = x.shape
    tm = _tile(T, 512)

    def body(dz_ref, w_ref, x_ref, g_ref, dy_ref, dx_ref, dxb_ref, dg_ref):
        dh = _nn(dz_ref[...], w_ref[...])
        dx, dgr = _norm_bwd(dh, x_ref[...], g_ref[...])
        dx = dx + dy_ref[...]
        dx_ref[...] = dx
        dxb_ref[...] = dx.astype(BF)
        _acc_rows(dg_ref, pl.program_id(0) == 0, dgr)

    row = lambda i: (i, 0)
    fix = lambda i: (0, 0)
    return pl.pallas_call(
        body, name="mix_proj_bwd", grid=(T // tm,),
        in_specs=[pl.BlockSpec((tm, ZW), row), pl.BlockSpec((ZW, D), fix), pl.BlockSpec((tm, D), row),
                  pl.BlockSpec((1, D), fix), pl.BlockSpec((tm, D), row)],
        out_specs=[pl.BlockSpec((tm, D), row), pl.BlockSpec((tm, D), row), pl.BlockSpec((1, D), fix)],
        out_shape=[S((T, D), F32), S((T, D), BF), S((1, D), F32)],
        compiler_params=_cp(1))(dz, wz, x, g, dy)


def _ca_kv(mem, g_mem, wckv, g_ck):
    M, D = mem.shape

    def body(m_ref, g_ref, w_ref, gk_ref, mn_ref, kr_ref, kn_ref, v_ref):
        mf = m_ref[...]
        mn = (mf * _rstd(mf) * g_ref[...]).astype(BF)
        mn_ref[...] = mn
        for h in range(CA_HEADS):
            kr = _nn(mn, w_ref[h])
            kr_ref[h] = kr
            kn_ref[h] = (kr * _rstd(kr) * gk_ref[...]).astype(BF)
            v_ref[h] = _nn(mn, w_ref[CA_HEADS + h]).astype(BF)

    hd = (CA_HEADS, M, CA_HD)
    return pl.pallas_call(
        body, name="ca_kv", out_shape=[S((M, D), BF), S(hd, F32), S(hd, BF), S(hd, BF)],
        compiler_params=pltpu.CompilerParams(vmem_limit_bytes=VMEM_LIMIT))(mem, g_mem, wckv, g_ck)


def _ca_tile_fwd(xt, gca, wcq, gcq, kn_ref, v_ref):
    hb = (xt * _rstd(xt) * gca).astype(BF)
    qc = _nn(hb, wcq)
    qr, qn, ps = [], [], []
    for h in range(CA_HEADS):
        qh = qc[:, h * CA_HD:(h + 1) * CA_HD]
        qnh = (qh * _rstd(qh) * gcq * 0.0625).astype(BF)
        s = _nt(qnh, kn_ref[h])
        e = jnp.exp(s - jnp.max(s, axis=1, keepdims=True))
        ps.append(e / jnp.sum(e, axis=1, keepdims=True))
        qr.append(qh)
        qn.append(qnh)
    return hb, qr, qn, ps


def _ca_fwd(x, g_ca, wcq, g_cq, kn, vv, wco):
    T, D = x.shape
    M = kn.shape[1]
    tm = _tile(T, 256)

    def body(x_ref, gca_ref, wcq_ref, gcq_ref, kn_ref, v_ref, wco_ref, o_ref, ob_sc):
        xt = x_ref[...]
        _, _, _, ps = _ca_tile_fwd(xt, gca_ref[...], wcq_ref[...], gcq_ref[...], kn_ref, v_ref)
        for h in range(CA_HEADS):
            ob_sc[:, h * CA_HD:(h + 1) * CA_HD] = _nn(ps[h].astype(BF), v_ref[h]).astype(BF)
        o_ref[...] = xt + _nn(ob_sc[...], wco_ref[...])

    row = lambda i: (i, 0)
    fix = lambda i: (0, 0)
    fix3 = lambda i: (0, 0, 0)
    return pl.pallas_call(
        body, name="ca_fwd", grid=(T // tm,),
        in_specs=[pl.BlockSpec((tm, D), row), pl.BlockSpec((1, D), fix), pl.BlockSpec((D, D), fix),
                  pl.BlockSpec((1, CA_HD), fix), pl.BlockSpec((CA_HEADS, M, CA_HD), fix3),
                  pl.BlockSpec((CA_HEADS, M, CA_HD), fix3), pl.BlockSpec((D, D), fix)],
        out_specs=pl.BlockSpec((tm, D), row), out_shape=S((T, D), F32),
        scratch_shapes=[pltpu.VMEM((tm, D), BF)],
        compiler_params=_cp(1))(x, g_ca, wcq, g_cq, kn, vv, wco)


def _ca_bwd(x, dy, g_ca, wcq, g_cq, kn, vv, wco):
    T, D = x.shape
    M = kn.shape[1]
    tm = _tile(T, 256)
    n = T // tm

    def body(x_ref, dy_ref, gca_ref, wcq_ref, gcq_ref, kn_ref, v_ref, wco_ref,
             dx_ref, dwq_ref, dwo_ref, dkn_ref, dv_ref, dgcq_ref, dgca_ref, aq_sc, ao_sc, ob_sc, dq_sc):
        i = pl.program_id(0)
        first = i == 0
        xt = x_ref[...]
        dyt = dy_ref[...]
        dyb = dyt.astype(BF)
        hb, qr, qn, ps = _ca_tile_fwd(xt, gca_ref[...], wcq_ref[...], gcq_ref[...], kn_ref, v_ref)
        do = _nt(dyb, wco_ref[...])
        gcq_rows = None
        for h in range(CA_HEADS):
            hs = slice(h * CA_HD, (h + 1) * CA_HD)
            p = ps[h]
            pb = p.astype(BF)
            ob_sc[:, hs] = _nn(pb, v_ref[h]).astype(BF)
            doh = do[:, hs].astype(BF)
            dp = _nt(doh, v_ref[h])
            ds = (p * (dp - jnp.sum(dp * p, axis=1, keepdims=True))).astype(BF)
            dvh = _tn(pb, doh)
            dkh = _tn(ds, qn[h])

            @pl.when(first)
            def _():
                dv_ref[h] = dvh
                dkn_ref[h] = dkh

            @pl.when(jnp.logical_not(first))
            def _():
                dv_ref[h] += dvh
                dkn_ref[h] += dkh

            dqn = _nn(ds, kn_ref[h]) * 0.0625
            dqh, gr = _norm_bwd(dqn, qr[h], gcq_ref[...])
            gcq_rows = gr if gcq_rows is None else gcq_rows + gr
            dq_sc[:, hs] = dqh.astype(BF)
        _acc_rows(dgcq_ref, first, gcq_rows)
        dqb = dq_sc[...]
        p_o = _tn(ob_sc[...], dyb)
        p_q = _tn(hb, dqb)

        @pl.when(first)
        def _():
            ao_sc[...] = p_o
            aq_sc[...] = p_q

        @pl.when(jnp.logical_not(first))
        def _():
            ao_sc[...] += p_o
            aq_sc[...] += p_q

        @pl.when(i == n - 1)
        def _():
            dwo_ref[...] = ao_sc[...].astype(BF)
            dwq_ref[...] = aq_sc[...].astype(BF)

        dh = _nt(dqb, wcq_ref[...])
        dx, gar = _norm_bwd(dh, xt, gca_ref[...])
        dx_ref[...] = dx + dyt
        _acc_rows(dgca_ref, first, gar)

    row = lambda i: (i, 0)
    fix = lambda i: (0, 0)
    fix3 = lambda i: (0, 0, 0)
    hd = (CA_HEADS, M, CA_HD)
    return pl.pallas_call(
        body, name="ca_bwd", grid=(n,),
        in_specs=[pl.BlockSpec((tm, D), row), pl.BlockSpec((tm, D), row), pl.BlockSpec((1, D), fix),
                  pl.BlockSpec((D, D), fix), pl.BlockSpec((1, CA_HD), fix), pl.BlockSpec(hd, fix3),
                  pl.BlockSpec(hd, fix3), pl.BlockSpec((D, D), fix)],
        out_specs=[pl.BlockSpec((tm, D), row), pl.BlockSpec((D, D), fix), pl.BlockSpec((D, D), fix),
                   pl.BlockSpec(hd, fix3), pl.BlockSpec(hd, fix3), pl.BlockSpec((1, CA_HD), fix),
                   pl.BlockSpec((1, D), fix)],
        out_shape=[S((T, D), F32), S((D, D), BF), S((D, D), BF), S(hd, F32), S(hd, F32), S((1, CA_HD), F32),
                   S((1, D), F32)],
        scratch_shapes=[pltpu.VMEM((D, D), F32), pltpu.VMEM((D, D), F32), pltpu.VMEM((tm, D), BF),
                        pltpu.VMEM((tm, D), BF)],
        compiler_params=_cp(1))(x, dy, g_ca, wcq, g_cq, kn, vv, wco)


def _ca_kv_bwd(mem, g_mem, mn, kraw, dkn, dvv, wckv, g_ck):
    M, D = mem.shape

    def body(m_ref, g_ref, mn_ref, kr_ref, dkn_ref, dv_ref, w_ref, gk_ref, dw_ref, dgk_ref, dgm_ref):
        mn = mn_ref[...]
        dmn = jnp.zeros((M, D), F32)
        gk_rows = None
        for h in range(CA_HEADS):
            dkr, gr = _norm_bwd(dkn_ref[h], kr_ref[h], gk_ref[...])
            gk_rows = gr if gk_rows is None else gk_rows + gr
            dkb = dkr.astype(BF)
            dvb = dv_ref[h].astype(BF)
            dw_ref[h] = _tn(mn, dkb).astype(BF)
            dw_ref[CA_HEADS + h] = _tn(mn, dvb).astype(BF)
            dmn = dmn + _nt(dkb, w_ref[h]) + _nt(dvb, w_ref[CA_HEADS + h])
        dgk_ref[...] = jnp.sum(gk_rows, axis=0, keepdims=True)
        mf = m_ref[...]
        dgm_ref[...] = jnp.sum(dmn * (mf * _rstd(mf)), axis=0, keepdims=True)

    return pl.pallas_call(
        body, name="ca_kv_bwd",
        out_shape=[S((2 * CA_HEADS, D, CA_HD), BF), S((1, CA_HD), F32), S((1, D), F32)],
        compiler_params=pltpu.CompilerParams(vmem_limit_bytes=VMEM_LIMIT))(mem, g_mem, mn, kraw, dkn, dvv, wckv, g_ck)


def _after(g, token):
    return g if token is None else g + token[0:1, 0:1]


def _local_step(x, mem, target, small, weights, emit):
    T, D = x.shape
    p = small
    bf128 = jnp.pad(p["b_f"], ((0, 0), (0, LANES - FOX_HEADS)))
    b_st = p["b_s"].T

    wup1 = weights("ffn1_up", x)["wup1"]
    a1, h1 = _ffn_up("ffn1_up", x, p["g_ffn1"], wup1)
    wdn1 = weights("ffn1_dn", h1)["wdn1"]
    x1 = _ffn_down("ffn1_down", a1, wdn1, x)
    wm = weights("mix", x1)
    z, h2 = _mix_proj(x1, p["g_mix"], wm["wz"])
    qf, ka, va, yg = _mix_prep(z, bf128, p["g_q"], p["g_k"], p["g_sgu"], p["w_s"], b_st, p["g_gmlp_o"])
    attn, lse = _fox_fwd(qf, ka, va)
    x2 = _mix_out(attn, yg, p["g_fox_o"], wm["wout"], x1)
    wc = weights("ca", x2)
    mn, kraw, ckn, cvv = _ca_kv(mem, p["g_mem"], wc["wckv"], p["g_ck"])
    x3 = _ca_fwd(x2, p["g_ca"], wc["wcq"], p["g_cq"], ckn, cvv, wc["wco"])
    w2 = weights("ffn2", x3)
    a2, h4 = _ffn_up("ffn2_up", x3, p["g_ffn2"], w2["wup2"])
    dy4, dy4b, sq = _ffn_down_loss("ffn2_down", a2, w2["wdn2"], x3, target)

    gs = {}
    dgu2 = _ffn_bwd_act("ffn2_bwd_act", dy4b, h4, w2["wup2"], w2["wdn2"])
    tok = emit("ffn2", {"wup2": _ffn_dwup("ffn2", h4, dgu2), "wdn2": _ffn_dwdn("ffn2", a2, dy4b)})
    dx3, gs["g_ffn2"] = _ffn_dx("ffn2_dx", dgu2, w2["wup2"], x3, _after(p["g_ffn2"], tok), dy4)

    dx2, dwcq, dwco, dckn, dcvv, gs["g_cq"], gs["g_ca"] = _ca_bwd(
        x2, dx3, p["g_ca"], wc["wcq"], p["g_cq"], ckn, cvv, wc["wco"])
    dwckv, gs["g_ck"], gs["g_mem"] = _ca_kv_bwd(mem, p["g_mem"], mn, kraw, dckn, dcvv, wc["wckv"], p["g_ck"])

    qb, dob, dyg, dwout, gs["g_fox_o"] = _mix_out_bwd(dx2, attn, yg, p["g_fox_o"], wm["wout"], qf, lse)
    dq, dk, dv = _fox_bwd(qb, ka, va, dob)
    dz, gs["g_q"], gs["g_k"], gs["g_sgu"], gs["g_gmlp_o"], gs["w_s"], dbst, dbf = _mix_prep_bwd(
        z, dq, dk, dv, dyg, bf128, p["g_q"], p["g_k"], p["g_sgu"], p["w_s"], b_st, p["g_gmlp_o"])
    gs["b_s"] = dbst.T
    gs["b_f"] = dbf[:, :FOX_HEADS]
    tok_ws = emit("w_s", {"w_s": gs["w_s"]})
    tk = _tile(T, 1024)
    zb = ZW // 3
    dwz = _tn_matmul(
        "mix_dwz", dz, pl.BlockSpec((tk, zb), lambda j, k: (k, j)), h2, pl.BlockSpec((tk, D), lambda j, k: (k, 0)),
        S((ZW, D), F32), pl.BlockSpec((zb, D), lambda j, k: (j, 0)), (3, T // tk), (zb, D))
    tok = emit("mid", {"wcq": dwcq, "wco": dwco, "wckv": dwckv, "wout": dwout, "wz": dwz})
    dx1, dx1b, gs["g_mix"] = _mix_proj_bwd(dz, wm["wz"], x1, _after(_after(p["g_mix"], tok), tok_ws), dx2)

    dgu1 = _ffn_bwd_act("ffn1_bwd_act", dx1b, h1, wup1, wdn1)
    tok = emit("ffn1_dn", {"wdn1": _ffn_dwdn("ffn1", a1, dx1b)})
    tok = emit("ffn1_up", {"wup1": _ffn_dwup("ffn1", h1, dgu1, after=tok)})
    dx0, gs["g_ffn1"] = _ffn_dx("ffn1_dx", dgu1, wup1, x, _after(p["g_ffn1"], tok), dx1)
    return sq, dx0, gs


MESH = pl.DeviceIdType.MESH
HBM_SPEC = pl.BlockSpec(memory_space=pltpu.HBM)
N_PEER = N_DEV - 1


def _place():
    return lax.axis_index("x"), lax.axis_index("y"), lax.axis_index("c")


def _slot(px, py, pc):
    return 4 * px + 2 * py + pc


SEM_SPEC = pl.BlockSpec(memory_space=pltpu.SEMAPHORE)
ANY_SPEC = pl.BlockSpec(memory_space=pl.ANY)
DATAFLOW = pltpu.SideEffectType.DATAFLOW_SIDE_EFFECTING


def _hbm(a):
    return pltpu.with_memory_space_constraint(a, pltpu.HBM)


def _peer(x, y, c, r):
    return (1 - x if r & 4 else x, 1 - y if r & 2 else y, 1 - c if r & 1 else c)


def _place_own(srcs, whole):
    my = _slot(*_place())
    lands = []
    for s in srcs:
        blk = s[None] if whole else lax.dynamic_slice_in_dim(s, my, 1, 0)
        shape = (N_DEV,) + s.shape if whole else s.shape
        lands.append(lax.dynamic_update_slice_in_dim(lax.empty(shape, s.dtype), blk, my, 0))
    return lands


def _copy_start(name, srcs, lands, whole):
    n = len(srcs)

    def body(*refs):
        src, land = refs[:n], refs[n:2 * n]
        send, recv = refs[2 * n:3 * n], refs[3 * n:4 * n]
        token = refs[6 * n]
        x, y, c = _place()
        my = _slot(x, y, c)
        for a in range(n):
            for r in range(1, N_DEV):
                p = _peer(x, y, c, r)
                pltpu.make_async_remote_copy(
                    src_ref=src[a] if whole else src[a].at[_slot(*p)], dst_ref=land[a].at[my],
                    send_sem=send[a].at[r - 1], recv_sem=recv[a].at[r - 1], device_id=p, device_id_type=MESH).start()
        token[...] = jnp.zeros_like(token)

    out = pl.pallas_call(
        body, name=name,
        out_shape=([pltpu.SemaphoreType.DMA((N_PEER,))] * (2 * n)
                   + [pltpu.HBM(s.shape, s.dtype) for s in srcs] + [pltpu.HBM(s.shape, s.dtype) for s in lands]
                   + [S((8, LANES), F32)]),
        in_specs=[HBM_SPEC] * (2 * n),
        out_specs=[SEM_SPEC] * (2 * n) + [HBM_SPEC] * (2 * n) + [pl.BlockSpec(memory_space=pltpu.VMEM)],
        input_output_aliases={i: 2 * n + i for i in range(2 * n)},
        compiler_params=pltpu.CompilerParams(has_side_effects=DATAFLOW),
    )(*[_hbm(s) for s in srcs], *[_hbm(s) for s in lands])
    return out[:n], out[n:2 * n], out[2 * n:3 * n], out[3 * n:4 * n], out[4 * n]


def _copy_wait(name, srcs, lands, send, recv, after, whole):
    n = len(srcs)

    def body(*refs):
        src, land = refs[:n], refs[n:2 * n]
        snd, rcv = refs[2 * n:3 * n], refs[3 * n:4 * n]
        x, y, c = _place()
        for a in range(n):
            for r in range(1, N_DEV):
                p = _peer(x, y, c, r)
                ps = _slot(*p)
                cp = pltpu.make_async_remote_copy(
                    src_ref=src[a] if whole else src[a].at[ps], dst_ref=land[a].at[ps],
                    send_sem=snd[a].at[r - 1], recv_sem=rcv[a].at[r - 1], device_id=p, device_id_type=MESH)
                cp.wait_send()
                cp.wait_recv()

    out = pl.pallas_call(
        body, name=name,
        out_shape=[pltpu.HBM(s.shape, s.dtype) for s in srcs] + [pltpu.HBM(s.shape, s.dtype) for s in lands],
        in_specs=[HBM_SPEC] * (2 * n) + [SEM_SPEC] * (2 * n) + [ANY_SPEC],
        out_specs=[HBM_SPEC] * (2 * n),
        input_output_aliases={i: i for i in range(2 * n)},
        compiler_params=pltpu.CompilerParams(has_side_effects=DATAFLOW),
    )(*srcs, *lands, *send, *recv, after)
    return out[n:]


def _adamw(w, g, m, v):
    m2 = ADAM_B1 * m + (1.0 - ADAM_B1) * g
    v2 = ADAM_B2 * v + (1.0 - ADAM_B2) * (g * g)
    m_hat = m2 / (1.0 - ADAM_B1 ** ADAM_STEP)
    v_hat = v2 / (1.0 - ADAM_B2 ** ADAM_STEP)
    delta = -ADAM_LR * (m_hat / (jnp.sqrt(v_hat) + ADAM_EPS) + ADAM_WD * w)
    return delta, m2, v2


def _adamw_big(name, slots, w, m, v):
    R, C = w.shape
    tr = next((t for t in (256, 352) if R % t == 0), R)

    def body(s_ref, w_ref, m_ref, v_ref, g_ref, d_ref, m2_ref, v2_ref):
        g = s_ref[0].astype(F32)
        for k in range(1, N_DEV):
            g = g + s_ref[k].astype(F32)
        d, m2, v2 = _adamw(w_ref[...], g, m_ref[...], v_ref[...])
        g_ref[...] = g
        d_ref[...] = d
        m2_ref[...] = m2
        v2_ref[...] = v2

    row = pl.BlockSpec((tr, C), lambda i: (i, 0))
    return pl.pallas_call(
        body, name=name, grid=(R // tr,),
        in_specs=[pl.BlockSpec((N_DEV, tr, C), lambda i: (0, i, 0)), row, row, row],
        out_specs=[row] * 4, out_shape=[S((R, C), F32)] * 4,
        compiler_params=_cp(1))(slots, w, m, v)


TINY_ROWS = (("b_s", 8), ("g_ffn1", 8), ("g_mix", 8), ("g_ca", 8), ("g_mem", 8), ("g_ffn2", 8), ("g_sgu", 4),
             ("g_fox_o", 4), ("g_gmlp_o", 4), ("g_cq", 2), ("g_ck", 2), ("g_q", 1), ("g_k", 1), ("b_f", 1))
TINY_P = 72


def _pack_tiny(d):
    rows = []
    for name, r in TINY_ROWS:
        flat = d[name].reshape(-1)
        rows.append(jnp.pad(flat, (0, r * LANES - flat.shape[0])).reshape(r, LANES))
    used = sum(r for _, r in TINY_ROWS)
    rows.append(jnp.zeros((TINY_P - used, LANES), F32))
    return jnp.concatenate(rows, axis=0)


def _unpack_tiny(packed, shapes):
    out, at = {}, 0
    for name, r in TINY_ROWS:
        shape = shapes[name]
        size = 1
        for s in shape:
            size *= s
        out[name] = packed[at:at + r].reshape(-1)[:size].reshape(shape)
        at += r
    return out


WEIGHTS =('g_ffn1', 'w_ffn1_in', 'w_ffn1_out', 'g_mix', 'w_in', 'b_f', 'g_q', 'g_k', 'g_sgu', 'w_s', 'b_s',
           'g_fox_o', 'g_gmlp_o', 'w_out', 'g_ca', 'g_mem', 'w_cq', 'w_ckv', 'g_cq', 'g_ck', 'w_co', 'g_ffn2',
           'w_ffn2_in', 'w_ffn2_out')
BIG = ('w_ffn1_in', 'w_ffn1_out', 'w_in', 'w_out', 'w_cq', 'w_ckv', 'w_co', 'w_ffn2_in', 'w_ffn2_out')
TRANSPOSED = ('w_ffn1_in', 'w_in', 'w_ffn2_in')
GATHER_GROUPS = {"ffn1_up": ("w_ffn1_in",), "ffn1_dn": ("w_ffn1_out",), "mix": ("w_in", "w_out"),
                 "ca": ("w_cq", "w_ckv", "w_co"), "ffn2": ("w_ffn2_in", "w_ffn2_out")}
QKV_W = 3 * FOX_W
UV_OFF = QKV_W + FOX_HEADS


def kernel(x, mem, g_ffn1, w_ffn1_in, w_ffn1_out, g_mix, w_in, b_f, g_q, g_k, g_sgu, w_s, b_s, g_fox_o, g_gmlp_o, w_out, g_ca, g_mem, w_cq, w_ckv, g_cq, g_ck, w_co, g_ffn2, w_ffn2_in, w_ffn2_out, loss_target, m_g_ffn1, m_w_ffn1_in, m_w_ffn1_out, m_g_mix, m_w_in, m_b_f, m_g_q, m_g_k, m_g_sgu, m_w_s, m_b_s, m_g_fox_o, m_g_gmlp_o, m_w_out, m_g_ca, m_g_mem, m_w_cq, m_w_ckv, m_g_cq, m_g_ck, m_w_co, m_g_ffn2, m_w_ffn2_in, m_w_ffn2_out, v_g_ffn1, v_w_ffn1_in, v_w_ffn1_out, v_g_mix, v_w_in, v_b_f, v_g_q, v_g_k, v_g_sgu, v_w_s, v_b_s, v_g_fox_o, v_g_gmlp_o, v_w_out, v_g_ca, v_g_mem, v_w_cq, v_w_ckv, v_g_cq, v_g_ck, v_w_co, v_g_ffn2, v_w_ffn2_in, v_w_ffn2_out):
    args = dict(locals())
    w = {n: args[n] for n in WEIGHTS}
    mo = {n: args["m_" + n] for n in WEIGHTS}
    vo = {n: args["v_" + n] for n in WEIGHTS}
    D = D_MODEL

    def local(n, a):
        return a[0].T if n in TRANSPOSED else a[0]

    shards = [local(n, w[n]).astype(BF) for n in BIG]
    fb = shards[0].shape[0]
    g_snd, g_rcv, g_src, g_land, g_token = _copy_start("gather_start", shards, _place_own(shards, True), True)
    handles = {n: (g_src[i], g_land[i], g_snd[i], g_rcv[i]) for i, n in enumerate(BIG)}

    tiny_names = [n for n, _ in TINY_ROWS]
    tiny_wmv = [_pack_tiny({n: a[n] for n in tiny_names}) + g_token[0:1, 0:1] for a in (w, mo, vo)]
    first_after = tiny_wmv[0][0:8] + tiny_wmv[1][0:8] + tiny_wmv[2][0:8]

    def weights(group, after):
        names = GATHER_GROUPS[group]
        hs = [handles[n] for n in names]
        got = _copy_wait("gather_wait_" + group, [h[0] for h in hs], [h[1] for h in hs], [h[2] for h in hs],
                         [h[3] for h in hs], first_after if group == "ffn1_up" else after, True)
        got = dict(zip(names, got))
        if group == "ffn1_up":
            return {"wup1": got["w_ffn1_in"].reshape(2, N_FFN_BLK, fb, D)}
        if group == "ffn1_dn":
            return {"wdn1": got["w_ffn1_out"].reshape(N_FFN_BLK, fb, D)}
        if group == "mix":
            full = got["w_in"].reshape(-1, D)
            wz = jnp.concatenate([full[:QKV_W], full[UV_OFF:], full[QKV_W:UV_OFF],
                                  jnp.zeros((LANES - FOX_HEADS, D), BF)], axis=0)
            return {"wz": wz, "wout": got["w_out"].reshape(D, D)}
        if group == "ca":
            return {"wcq": got["w_cq"].reshape(D, D), "wco": got["w_co"].reshape(D, D), "wckv": got["w_ckv"]}
        return {"wup2": got["w_ffn2_in"].reshape(2, N_FFN_BLK, fb, D),
                "wdn2": got["w_ffn2_out"].reshape(N_FFN_BLK, fb, D)}

    flying = {}

    def emit(group, g):
        if group == "w_s":
            part = [g["w_s"].reshape(-1, LANES)]
            *copies, token = _copy_start("w_s_start", part, _place_own(part, True), True)
            flying[group] = copies
            return token
        if group == "ffn2":
            parts = {"w_ffn2_in": g["wup2"], "w_ffn2_out": g["wdn2"].reshape(N_DEV, -1, D)}
        elif group == "ffn1_dn":
            parts = {"w_ffn1_out": g["wdn1"].reshape(N_DEV, -1, D)}
        elif group == "ffn1_up":
            parts = {"w_ffn1_in": g["wup1"]}
        else:
            gz = g["wz"]
            g_in = jnp.concatenate([gz[:QKV_W], gz[Z_F:Z_F + FOX_HEADS], gz[QKV_W:Z_F]], axis=0)
            parts = {"w_in": g_in.reshape(N_DEV, -1, D).astype(BF),
                     "w_out": g["wout"].reshape(N_DEV, -1, D), "w_cq": g["wcq"].reshape(N_DEV, -1, D),
                     "w_co": g["wco"].reshape(N_DEV, -1, D), "w_ckv": g["wckv"]}
        names = list(parts)
        srcs = [parts[n] for n in names]
        *copies, token = _copy_start("exchange_start_" + group, srcs, _place_own(srcs, False), False)
        flying[group] = (names, copies)
        return token

    small = {n: (w[n][0] if n == "b_s" else w[n]) for n in tiny_names}
    small["w_s"] = w["w_s"][0]

    sq, dx0, gs = _local_step(x[0], mem[0], loss_target[0], small, weights, emit)
    loss = lax.psum(sq[0, 0], ("x", "y", "c")) * (0.5 / D)

    sm_parts = [_pack_tiny(gs)]
    sm_snd, sm_rcv, sm_src, sm_land, sm_token = _copy_start("tiny_start", sm_parts, _place_own(sm_parts, True), True)

    grad, delta, new_m, new_v = {}, {}, {}, {}

    def update(group, after):
        names, (snd, rcv, srcs, lands) = flying[group]
        slots = _copy_wait("exchange_wait_" + group, srcs, lands, snd, rcv, after, False)
        for n, sl in zip(names, slots):
            g, d, m2, v2 = _adamw_big("adamw_" + n, sl, local(n, w[n]), local(n, mo[n]), local(n, vo[n]))
            grad[n], delta[n], new_m[n], new_v[n] = (
                (t.T if n in TRANSPOSED else t).reshape(w[n].shape) for t in (g, d, m2, v2))
        return d

    last = update("ffn2", sm_token)
    last = update("mid", last)
    last = update("ffn1_dn", last)
    last = update("ffn1_up", last)
    ws_snd, ws_rcv, ws_src, ws_land = flying["w_s"]
    ws_all, = _copy_wait("w_s_wait", ws_src, ws_land, ws_snd, ws_rcv, last, True)
    tiny_all, = _copy_wait("tiny_wait", sm_src, sm_land, sm_snd, sm_rcv, ws_all, True)
    ws_shape = w["w_s"].shape
    for store, t in zip((grad, delta, new_m, new_v), _adamw_big(
            "adamw_w_s", ws_all, *[a["w_s"].reshape(-1, LANES) for a in (w, mo, vo)])):
        store["w_s"] = t.reshape(ws_shape)
    shapes = {n: w[n].shape for n in tiny_names}
    for store, t in zip((grad, delta, new_m, new_v), _adamw_big(
            "adamw_tiny", tiny_all, *tiny_wmv)):
        store.update(_unpack_tiny(t, shapes))

    return (loss, dx0[None], *[grad[n] for n in WEIGHTS], *[delta[n] for n in WEIGHTS],
            *[new_m[n] for n in WEIGHTS], *[new_v[n] for n in WEIGHTS])
```

```python
import functools

import jax
import jax.numpy as jnp
from jax import lax
from jax.experimental import pallas as pl
from jax.experimental.pallas import tpu as pltpu

F32 = jnp.float32
BF = jnp.bfloat16
S = jax.ShapeDtypeStruct

N_DEV = 8
D_MODEL = 1024
FOX_HEADS, FOX_HD = 8, 64
FOX_W = 512
GMLP_G, GMLP_GD = 8, 64
GMLP_W = 512
CHUNK = 128
CA_HEADS, CA_HD = 4, 256
N_FFN_BLK = 4
ZW = 2688
Z_Q, Z_K, Z_V, Z_U, Z_G, Z_F = 0, 512, 1024, 1536, 2048, 2560
EPS = 1e-6
NEG = -1e30
LANES = 128

ADAM_LR, ADAM_B1, ADAM_B2, ADAM_EPS, ADAM_WD, ADAM_STEP = 0.001, 0.9, 0.999, 1e-08, 0.01, 10

VMEM_LIMIT = 52 * 2 ** 20


def _cp(n_axes):
    return pltpu.CompilerParams(dimension_semantics=("arbitrary",) * n_axes, vmem_limit_bytes=VMEM_LIMIT)


def _nn(a, b):
    return jnp.dot(a, b, preferred_element_type=F32)


def _nt(a, b):
    return lax.dot_general(a, b, (((1,), (1,)), ((), ())), preferred_element_type=F32)


def _tn(a, b):
    return lax.dot_general(a, b, (((0,), (0,)), ((), ())), preferred_element_type=F32)


def _hi(a, b):
    return jnp.dot(a, b, precision=lax.Precision.HIGHEST, preferred_element_type=F32)


def _rstd(x):
    return lax.rsqrt(jnp.mean(x * x, axis=-1, keepdims=True) + EPS)


def _norm_bwd(dy, x, g):
    r = _rstd(x)
    xh = x * r
    dxh = dy * g
    dx = r * (dxh - xh * jnp.mean(dxh * xh, axis=-1, keepdims=True))
    return dx, dy * xh


def _acc_rows(ref, first, val):
    srow = jnp.sum(val, axis=0, keepdims=True)

    @pl.when(first)
    def _():
        ref[...] = srow

    @pl.when(jnp.logical_not(first))
    def _():
        ref[...] += srow


def _gelu(x):
    c = 0.7978845608028654
    return 0.5 * x * (1.0 + jnp.tanh(c * (x + 0.044715 * x * x * x)))


def _gelu_grad(x):
    c = 0.7978845608028654
    t = jnp.tanh(c * (x + 0.044715 * x * x * x))
    return 0.5 * (1.0 + t) + 0.5 * x * (1.0 - t * t) * c * (1.0 + 3 * 0.044715 * x * x)


def _tile(n, pref):
    return pref if n % pref == 0 else n


def _ffn_up(name, x, g, wup):
    T, D = x.shape
    FB = wup.shape[-2]
    tm = _tile(T, 1024)

    def body(x_ref, g_ref, w_ref, a_ref, h_ref):
        @pl.when(pl.program_id(1) == 0)
        def _():
            xf = x_ref[...]
            h_ref[...] = (xf * _rstd(xf) * g_ref[...]).astype(BF)

        hb = h_ref[...]
        gg = _nt(hb, w_ref[0])
        uu = _nt(hb, w_ref[1])
        a_ref[...] = (gg * jax.nn.sigmoid(gg) * uu).astype(BF)

    return pl.pallas_call(
        body, name=name, grid=(T // tm, N_FFN_BLK),
        in_specs=[pl.BlockSpec((tm, D), lambda i, j: (i, 0)),
                  pl.BlockSpec((1, D), lambda i, j: (0, 0)),
                  pl.BlockSpec((2, None, FB, D), lambda i, j: (0, j, 0, 0))],
        out_specs=[pl.BlockSpec((None, tm, FB), lambda i, j: (j, i, 0)),
                   pl.BlockSpec((tm, D), lambda i, j: (i, 0))],
        out_shape=[S((N_FFN_BLK, T, FB), BF), S((T, D), BF)],
        compiler_params=_cp(2))(x, g, wup)


def _ffn_down(name, a, wdn, x):
    _, T, FB = a.shape
    D = x.shape[1]
    tm = _tile(T, 512)

    def body(a_ref, w_ref, x_ref, o_ref):
        p = _nn(a_ref[0], w_ref[0])
        for j in range(1, N_FFN_BLK):
            p = p + _nn(a_ref[j], w_ref[j])
        o_ref[...] = x_ref[...] + 0.5 * p

    return pl.pallas_call(
        body, name=name, grid=(T // tm,),
        in_specs=[pl.BlockSpec((N_FFN_BLK, tm, FB), lambda i: (0, i, 0)),
                  pl.BlockSpec((N_FFN_BLK, FB, D), lambda i: (0, 0, 0)),
                  pl.BlockSpec((tm, D), lambda i: (i, 0))],
        out_specs=pl.BlockSpec((tm, D), lambda i: (i, 0)),
        out_shape=S((T, D), F32),
        compiler_params=_cp(1))(a, wdn, x)


def _ffn_down_loss(name, a, wdn, x, target):
    _, T, FB = a.shape
    D = x.shape[1]
    tm = _tile(T, 512)

    def body(a_ref, w_ref, x_ref, t_ref, d_ref, db_ref, loss_ref):
        i = pl.program_id(0)
        p = _nn(a_ref[0], w_ref[0])
        for j in range(1, N_FFN_BLK):
            p = p + _nn(a_ref[j], w_ref[j])
        diff = (x_ref[...] + 0.5 * p) - t_ref[...]
        dy = diff * (1.0 / D)
        d_ref[...] = dy
        db_ref[...] = dy.astype(BF)
        sq = jnp.zeros((8, LANES), F32) + jnp.sum(diff * diff)

        @pl.when(i == 0)
        def _():
            loss_ref[...] = sq

        @pl.when(i > 0)
        def _():
            loss_ref[...] += sq

    row = pl.BlockSpec((tm, D), lambda i: (i, 0))
    return pl.pallas_call(
        body, name=name, grid=(T // tm,),
        in_specs=[pl.BlockSpec((N_FFN_BLK, tm, FB), lambda i: (0, i, 0)),
                  pl.BlockSpec((N_FFN_BLK, FB, D), lambda i: (0, 0, 0)), row, row],
        out_specs=[row, row, pl.BlockSpec((8, LANES), lambda i: (0, 0))],
        out_shape=[S((T, D), F32), S((T, D), BF), S((8, LANES), F32)],
        compiler_params=_cp(1))(a, wdn, x, target)


def _ffn_bwd_act(name, dyb, h, wup, wdn):
    T, D = h.shape
    FB = wup.shape[-2]
    tm = _tile(T, 1024)

    def body(d_ref, h_ref, wu_ref, wd_ref, o_ref):
        da = 0.5 * _nt(d_ref[...], wd_ref[...])
        hb = h_ref[...]
        gg = _nt(hb, wu_ref[0])
        uu = _nt(hb, wu_ref[1])
        sg = jax.nn.sigmoid(gg)
        o_ref[0] = (da * uu * (sg * (1.0 + gg * (1.0 - sg)))).astype(BF)
        o_ref[1] = (da * (gg * sg)).astype(BF)

    return pl.pallas_call(
        body, name=name, grid=(T // tm, N_FFN_BLK),
        in_specs=[pl.BlockSpec((tm, D), lambda i, j: (i, 0)),
                  pl.BlockSpec((tm, D), lambda i, j: (i, 0)),
                  pl.BlockSpec((2, None, FB, D), lambda i, j: (0, j, 0, 0)),
                  pl.BlockSpec((None, FB, D), lambda i, j: (j, 0, 0))],
        out_specs=pl.BlockSpec((2, None, tm, FB), lambda i, j: (0, j, i, 0)),
        out_shape=S((2, N_FFN_BLK, T, FB), BF),
        compiler_params=_cp(2))(dyb, h, wup, wdn)


def _ffn_dx(name, dgu, wup, x, g, dy):
    T, D = x.shape
    FB = wup.shape[-2]
    tm = _tile(T, 1024)

    def body(d_ref, w_ref, x_ref, g_ref, dy_ref, dx_ref, dg_ref, acc_ref):
        i, j = pl.program_id(0), pl.program_id(1)
        p = _nn(d_ref[0], w_ref[0]) + _nn(d_ref[1], w_ref[1])

        @pl.when(j == 0)
        def _():
            acc_ref[...] = p

        @pl.when(j > 0)
        def _():
            acc_ref[...] += p

        @pl.when(j == N_FFN_BLK - 1)
        def _():
            dx, dgr = _norm_bwd(acc_ref[...], x_ref[...], g_ref[...])
            dx_ref[...] = dx + dy_ref[...]
            _acc_rows(dg_ref, i == 0, dgr)

    return pl.pallas_call(
        body, name=name, grid=(T // tm, N_FFN_BLK),
        in_specs=[pl.BlockSpec((2, None, tm, FB), lambda i, j: (0, j, i, 0)),
                  pl.BlockSpec((2, None, FB, D), lambda i, j: (0, j, 0, 0)),
                  pl.BlockSpec((tm, D), lambda i, j: (i, 0)),
                  pl.BlockSpec((1, D), lambda i, j: (0, 0)),
                  pl.BlockSpec((tm, D), lambda i, j: (i, 0))],
        out_specs=[pl.BlockSpec((tm, D), lambda i, j: (i, 0)),
                   pl.BlockSpec((1, D), lambda i, j: (0, 0))],
        out_shape=[S((T, D), F32), S((1, D), F32)],
        scratch_shapes=[pltpu.VMEM((tm, D), F32)],
        compiler_params=_cp(2))(dgu, wup, x, g, dy)


def _tn_matmul(name, a, a_spec, b, b_spec, out_shape, out_spec, grid, acc_shape, scale=1.0, after=None):
    nk = grid[1]
    extra = [] if after is None else [after]

    def body(a_ref, b_ref, *rest):
        o_ref, acc_ref = rest[-2:]
        k = pl.program_id(1)
        p = _tn(a_ref[...], b_ref[...])

        @pl.when(k == 0)
        def _():
            acc_ref[...] = p

        @pl.when(k > 0)
        def _():
            acc_ref[...] += p

        @pl.when(k == nk - 1)
        def _():
            o_ref[...] = (acc_ref[...] * scale).astype(o_ref.dtype)

    return pl.pallas_call(
        body, name=name, grid=grid,
        in_specs=[a_spec, b_spec] + [pl.BlockSpec((8, LANES), lambda j, k: (0, 0)) for _ in extra],
        out_specs=out_spec, out_shape=out_shape,
        scratch_shapes=[pltpu.VMEM(acc_shape, F32)], compiler_params=_cp(2))(a, b, *extra)


def _ffn_dwup(name, h, dgu, after=None):
    T, D = h.shape
    FB = dgu.shape[-1]
    tk = _tile(T, 1024)
    return _tn_matmul(
        name + "_dwup", dgu.reshape(2 * N_FFN_BLK, T, FB), pl.BlockSpec((None, tk, FB), lambda j, k: (j, k, 0)),
        h, pl.BlockSpec((tk, D), lambda j, k: (k, 0)),
        S((2 * N_FFN_BLK, FB, D), BF), pl.BlockSpec((None, FB, D), lambda j, k: (j, 0, 0)),
        (2 * N_FFN_BLK, T // tk), (FB, D), after=after)


def _ffn_dwdn(name, a, dyb):
    _, T, FB = a.shape
    D = dyb.shape[1]
    tk = _tile(T, 1024)
    return _tn_matmul(
        name + "_dwdn", a, pl.BlockSpec((None, tk, FB), lambda j, k: (j, k, 0)),
        dyb, pl.BlockSpec((tk, D), lambda j, k: (k, 0)),
        S((N_FFN_BLK, FB, D), BF), pl.BlockSpec((None, FB, D), lambda j, k: (j, 0, 0)),
        (N_FFN_BLK, T // tk), (FB, D), scale=0.5)


def _mix_proj(x, g, wz):
    T, D = x.shape
    tm = _tile(T, 512)

    def body(x_ref, g_ref, w_ref, z_ref, h_ref):
        xf = x_ref[...]
        hb = (xf * _rstd(xf) * g_ref[...]).astype(BF)
        h_ref[...] = hb
        z_ref[...] = _nt(hb, w_ref[...])

    return pl.pallas_call(
        body, name="mix_proj", grid=(T // tm,),
        in_specs=[pl.BlockSpec((tm, D), lambda i: (i, 0)),
                  pl.BlockSpec((1, D), lambda i: (0, 0)),
                  pl.BlockSpec((ZW, D), lambda i: (0, 0))],
        out_specs=[pl.BlockSpec((tm, ZW), lambda i: (i, 0)),
                   pl.BlockSpec((tm, D), lambda i: (i, 0))],
        out_shape=[S((T, ZW), F32), S((T, D), BF)],
        compiler_params=_cp(1))(x, g, wz)


def _tri(n, lower):
    r = lax.broadcasted_iota(jnp.int32, (n, n), 0)
    c = lax.broadcasted_iota(jnp.int32, (n, n), 1)
    return (r >= c) if lower else (r <= c)


def _spatial_mix(vgn_b, ws_ref, bst, tm):
    tril = _tri(CHUNK, True)
    wms = [jnp.where(tril, ws_ref[g], 0.0).astype(BF) for g in range(GMLP_G)]
    rows = []
    for c in range(tm // CHUNK):
        cols = []
        for g in range(GMLP_G):
            vs = vgn_b[c * CHUNK:(c + 1) * CHUNK, g * GMLP_GD:(g + 1) * GMLP_GD]
            cols.append(_nn(wms[g], vs) + bst[:, g:g + 1])
        rows.append(jnp.concatenate(cols, axis=1))
    return jnp.concatenate(rows, axis=0), wms


HB = 128
AUG_W = FOX_HEADS * HB
COL_A, COL_B, COL_C = 64, 67, 70


def _spread_matrix():
    r = jnp.arange(FOX_W)
    return (jnp.arange(AUG_W)[None, :] == ((r // FOX_HD) * HB + r % FOX_HD)[:, None]).astype(BF)


def _piece_matrix(col):
    r = jnp.arange(LANES)
    dst = jnp.where(r < 3 * FOX_HEADS, (r % FOX_HEADS) * HB + col + r // FOX_HEADS, -1)
    return (jnp.arange(AUG_W)[None, :] == dst[:, None]).astype(BF)


def _ones_row(cols):
    c = jnp.arange(AUG_W) % HB
    hit = functools.reduce(jnp.logical_or, [(c >= a) & (c < a + 3) for a in cols])
    return hit.astype(F32)[None, :]


def _pieces(x):
    lane = lax.broadcasted_iota(jnp.int32, x.shape, 1)
    x = jnp.where(lane < FOX_HEADS, x, 0.0)
    hi = x.astype(BF).astype(F32)
    r1 = x - hi
    mid = r1.astype(BF).astype(F32)
    lo = (r1 - mid).astype(BF).astype(F32)
    return (hi + pltpu.roll(mid, FOX_HEADS, 1) + pltpu.roll(lo, 2 * FOX_HEADS, 1)).astype(BF)


def _mix_prep(z, bf128, g_q, g_k, g_sgu, w_s, b_st, g_go):
    T = z.shape[0]
    tm = _tile(T, 256)
    spread, pc_q, pc_k = _spread_matrix(), _piece_matrix(COL_A), _piece_matrix(COL_B)
    one_q, one_k, one_v = _ones_row([COL_B]), _ones_row([COL_A, COL_C]), _ones_row([COL_A])

    def body(z_ref, bf_ref, gq_ref, gk_ref, gs_ref, ws_ref, bst_ref, go_ref, sp_ref, pq_ref, pk_ref, oq_ref, ok_ref,
             ov_ref, q_ref, k_ref, v_ref, y_ref, carry_ref, qn_sc, kn_sc):
        i = pl.program_id(0)

        @pl.when(i == 0)
        def _():
            carry_ref[...] = jnp.zeros_like(carry_ref)

        for h in range(FOX_HEADS):
            hs = slice(h * FOX_HD, (h + 1) * FOX_HD)
            qh = z_ref[:, Z_Q + h * FOX_HD:Z_Q + (h + 1) * FOX_HD]
            kh = z_ref[:, Z_K + h * FOX_HD:Z_K + (h + 1) * FOX_HD]
            qn_sc[:, hs] = (qh * _rstd(qh) * gq_ref[...] * 0.125).astype(BF)
            kn_sc[:, hs] = (kh * _rstd(kh) * gk_ref[...]).astype(BF)

        fl = z_ref[:, Z_F:Z_F + LANES] + bf_ref[...]
        logf = jnp.minimum(fl, 0.0) - jnp.log1p(jnp.exp(-jnp.abs(fl)))
        csum = _hi(_tri(tm, True).astype(F32), logf) + carry_ref[...]
        carry_ref[...] = csum[tm - 1:tm, :]
        sp = sp_ref[...]
        q_ref[...] = (_nn(qn_sc[...], sp) + _nn(_pieces(csum), pq_ref[...]) + oq_ref[...]).astype(BF)
        k_ref[...] = (_nn(kn_sc[...], sp) + _nn(_pieces(-csum), pk_ref[...]) + ok_ref[...]).astype(BF)
        v_ref[...] = (_nn(z_ref[:, Z_V:Z_V + FOX_W].astype(BF), sp) + ov_ref[...]).astype(BF)

        u = _gelu(z_ref[:, Z_U:Z_U + GMLP_W])
        vg = _gelu(z_ref[:, Z_G:Z_G + GMLP_W])
        vgn = (vg * _rstd(vg) * gs_ref[...]).astype(BF)
        mixed, _ = _spatial_mix(vgn, ws_ref, bst_ref[...], tm)
        sgu = u * mixed
        y_ref[...] = (sgu * _rstd(sgu) * go_ref[...]).astype(BF)

    row = lambda i: (i, 0)
    fix2 = lambda i: (0, 0)
    return pl.pallas_call(
        body, name="mix_prep", grid=(T // tm,),
        in_specs=[pl.BlockSpec((tm, ZW), row),
                  pl.BlockSpec((1, LANES), fix2), pl.BlockSpec((1, FOX_HD), fix2), pl.BlockSpec((1, FOX_HD), fix2),
                  pl.BlockSpec((1, GMLP_W), fix2), pl.BlockSpec((GMLP_G, CHUNK, CHUNK), lambda i: (0, 0, 0)),
                  pl.BlockSpec((CHUNK, GMLP_G), fix2), pl.BlockSpec((1, GMLP_W), fix2),
                  pl.BlockSpec((FOX_W, AUG_W), fix2), pl.BlockSpec((LANES, AUG_W), fix2),
                  pl.BlockSpec((LANES, AUG_W), fix2), pl.BlockSpec((1, AUG_W), fix2), pl.BlockSpec((1, AUG_W), fix2),
                  pl.BlockSpec((1, AUG_W), fix2)],
        out_specs=[pl.BlockSpec((tm, AUG_W), row), pl.BlockSpec((tm, AUG_W), row), pl.BlockSpec((tm, AUG_W), row),
                   pl.BlockSpec((tm, GMLP_W), row)],
        out_shape=[S((T, AUG_W), BF), S((T, AUG_W), BF), S((T, AUG_W), BF), S((T, GMLP_W), BF)],
        scratch_shapes=[pltpu.VMEM((1, LANES), F32), pltpu.VMEM((tm, FOX_W), BF), pltpu.VMEM((tm, FOX_W), BF)],
        compiler_params=_cp(1))(z, bf128, g_q, g_k, g_sgu, w_s, b_st, g_go, spread, pc_q, pc_k, one_q, one_k, one_v)


def _fox_fwd(q, k, v):
    T = q.shape[0]
    tq = _tile(T, 512)
    nq = T // tq

    def body(q_ref, k_ref, v_ref, o_ref, lse_ref, m_sc, acc_sc):
        i, j = pl.program_id(0), pl.program_id(1)

        @pl.when(j == 0)
        def _():
            m_sc[...] = jnp.full(m_sc.shape, NEG, F32)
            acc_sc[...] = jnp.zeros_like(acc_sc)

        def step(masked):
            mask = _tri(tq, True) if masked else None
            for h in range(FOX_HEADS):
                hb = slice(h * HB, (h + 1) * HB)
                s = _nt(q_ref[:, hb], k_ref[:, hb])
                if masked:
                    s = jnp.where(mask, s, NEG)
                m_prev = m_sc[h]
                m_new = jnp.maximum(m_prev, jnp.broadcast_to(jnp.max(s, axis=1, keepdims=True), (tq, HB)))
                p = jnp.exp(s - jnp.tile(m_new, (1, tq // HB))).astype(BF)
                acc_sc[:, hb] = jnp.exp(m_prev - m_new) * acc_sc[:, hb] + _nn(p, v_ref[:, hb])
                m_sc[h] = m_new

        @pl.when(j < i)
        def _():
            step(False)

        @pl.when(j == i)
        def _():
            step(True)
            lse_ref[...] = jnp.zeros_like(lse_ref)
            for h in range(FOX_HEADS):
                l = acc_sc[:, h * HB + COL_A:h * HB + COL_A + 1]
                o_ref[:, h * FOX_HD:(h + 1) * FOX_HD] = acc_sc[:, h * HB:h * HB + FOX_HD] / l
                lse_ref[:, h:h + 1] = m_sc[h][:, 0:1] + jnp.log(l)

    qi = lambda i, j: (i, 0)
    kj = lambda i, j: (jnp.minimum(i, j), 0)
    return pl.pallas_call(
        body, name="fox_fwd", grid=(nq, nq),
        in_specs=[pl.BlockSpec((tq, AUG_W), qi), pl.BlockSpec((tq, AUG_W), kj), pl.BlockSpec((tq, AUG_W), kj)],
        out_specs=[pl.BlockSpec((tq, FOX_W), qi), pl.BlockSpec((tq, LANES), qi)],
        out_shape=[S((T, FOX_W), F32), S((T, LANES), F32)],
        scratch_shapes=[pltpu.VMEM((FOX_HEADS, tq, HB), F32), pltpu.VMEM((tq, AUG_W), F32)],
        compiler_params=_cp(2))(q, k, v)


def _fox_bwd(q, k, v, dob):
    T = q.shape[0]
    tq = _tile(T, 512)
    nq = T // tq
    half = AUG_W // 2
    hpg = FOX_HEADS // 2

    def body(q_ref, k_ref, v_ref, do_ref, dq_ref, dk_ref, dv_ref, dq_sc):
        j, i = pl.program_id(1), pl.program_id(2)

        @pl.when(jnp.logical_and(i == 0, j == 0))
        def _():
            dq_sc[...] = jnp.zeros_like(dq_sc)

        @pl.when(i == 0)
        def _():
            dk_ref[...] = jnp.zeros_like(dk_ref)
            dv_ref[...] = jnp.zeros_like(dv_ref)

        def step(masked):
            rows = pl.ds(pl.multiple_of(i * tq, tq), tq)
            mask = _tri(tq, True) if masked else None
            for h in range(hpg):
                hb = slice(h * HB, (h + 1) * HB)
                qh, kh, vh, doh = q_ref[:, hb], k_ref[:, hb], v_ref[:, hb], do_ref[:, hb]
                s = _nt(qh, kh)
                if masked:
                    s = jnp.where(mask, s, NEG)
                p = jnp.exp(s)
                dsb = (p * _nt(doh, vh)).astype(BF)
                dv_ref[:, hb] += _tn(p.astype(BF), doh)
                dk_ref[:, hb] += _tn(dsb, qh)
                dq_sc[rows, hb] += _nn(dsb, kh)

        @pl.when(i > j)
        def _():
            step(False)

        @pl.when(i == j)
        def _():
            step(True)
            dq_ref[...] = dq_sc[pl.ds(pl.multiple_of(j * tq, tq), tq), :]

    qi = lambda g, j, i: (jnp.maximum(i, j), g)
    kj = lambda g, j, i: (j, g)
    return pl.pallas_call(
        body, name="fox_bwd", grid=(2, nq, nq),
        in_specs=[pl.BlockSpec((tq, half), qi), pl.BlockSpec((tq, half), kj), pl.BlockSpec((tq, half), kj),
                  pl.BlockSpec((tq, half), qi)],
        out_specs=[pl.BlockSpec((tq, half), kj), pl.BlockSpec((tq, half), kj), pl.BlockSpec((tq, half), kj)],
        out_shape=[S((T, AUG_W), F32), S((T, AUG_W), F32), S((T, AUG_W), F32)],
        scratch_shapes=[pltpu.VMEM((T, half), F32)],
        compiler_params=_cp(3))(q, k, v, dob)


def _mix_out(attn, yg, g_fo, wout, x):
    T, D = x.shape
    tm = _tile(T, 512)

    def body(a_ref, y_ref, g_ref, w_ref, x_ref, o_ref):
        at = a_ref[...]
        yf = (at * _rstd(at) * g_ref[...]).astype(BF)
        o_ref[...] = x_ref[...] + _nn(yf, w_ref[:FOX_W, :]) + _nn(y_ref[...], w_ref[FOX_W:, :])

    row = lambda i: (i, 0)
    return pl.pallas_call(
        body, name="mix_out", grid=(T // tm,),
        in_specs=[pl.BlockSpec((tm, FOX_W), row), pl.BlockSpec((tm, GMLP_W), row),
                  pl.BlockSpec((1, FOX_W), lambda i: (0, 0)), pl.BlockSpec((D, D), lambda i: (0, 0)),
                  pl.BlockSpec((tm, D), row)],
        out_specs=pl.BlockSpec((tm, D), row),
        out_shape=S((T, D), F32),
        compiler_params=_cp(1))(attn, yg, g_fo, wout, x)


def _mix_out_bwd(dx, attn, yg, g_fo, wout, qf, lse):
    T, D = dx.shape
    tm = _tile(T, 256)
    n = T // tm
    spread, pc_l, pc_d = _spread_matrix(), _piece_matrix(COL_C), _piece_matrix(COL_A)

    def body(dx_ref, a_ref, y_ref, g_ref, w_ref, qf_ref, lse_ref, sp_ref, pl_ref, pd_ref,
             qb_ref, dob_ref, dyg_ref, dw_ref, dg_ref, acc_ref, dsum_ref):
        i = pl.program_id(0)
        dxb = dx_ref[...].astype(BF)
        at = a_ref[...]
        yf = (at * _rstd(at) * g_ref[...]).astype(BF)
        dy = _nt(dxb, w_ref[...])
        p_top = _tn(yf, dxb)
        p_bot = _tn(y_ref[...], dxb)

        @pl.when(i == 0)
        def _():
            acc_ref[:FOX_W, :] = p_top
            acc_ref[FOX_W:, :] = p_bot

        @pl.when(i > 0)
        def _():
            acc_ref[:FOX_W, :] += p_top
            acc_ref[FOX_W:, :] += p_bot

        @pl.when(i == n - 1)
        def _():
            dw_ref[...] = acc_ref[...].astype(BF)

        dat, dgr = _norm_bwd(dy[:, :FOX_W], at, g_ref[...])
        _acc_rows(dg_ref, i == 0, dgr)
        dyg_ref[...] = dy[:, FOX_W:]
        prod = dat * at
        dsum_ref[...] = jnp.zeros_like(dsum_ref)
        for h in range(FOX_HEADS):
            dsum_ref[:, h:h + 1] = jnp.sum(prod[:, h * FOX_HD:(h + 1) * FOX_HD], axis=1, keepdims=True)
        dob_ref[...] = (_nn(dat.astype(BF), sp_ref[...]) + _nn(_pieces(-dsum_ref[...]), pd_ref[...])).astype(BF)
        qb_ref[...] = (qf_ref[...].astype(F32) + _nn(_pieces(-lse_ref[...]), pl_ref[...])).astype(BF)

    row = lambda i: (i, 0)
    fix = lambda i: (0, 0)
    return pl.pallas_call(
        body, name="mix_out_bwd", grid=(n,),
        in_specs=[pl.BlockSpec((tm, D), row), pl.BlockSpec((tm, FOX_W), row), pl.BlockSpec((tm, GMLP_W), row),
                  pl.BlockSpec((1, FOX_W), fix), pl.BlockSpec((D, D), fix), pl.BlockSpec((tm, AUG_W), row),
                  pl.BlockSpec((tm, LANES), row), pl.BlockSpec((FOX_W, AUG_W), fix), pl.BlockSpec((LANES, AUG_W), fix),
                  pl.BlockSpec((LANES, AUG_W), fix)],
        out_specs=[pl.BlockSpec((tm, AUG_W), row), pl.BlockSpec((tm, AUG_W), row), pl.BlockSpec((tm, GMLP_W), row),
                   pl.BlockSpec((D, D), fix), pl.BlockSpec((1, FOX_W), fix)],
        out_shape=[S((T, AUG_W), BF), S((T, AUG_W), BF), S((T, GMLP_W), F32), S((D, D), BF), S((1, FOX_W), F32)],
        scratch_shapes=[pltpu.VMEM((D, D), F32), pltpu.VMEM((tm, LANES), F32)],
        compiler_params=_cp(1))(dx, attn, yg, g_fo, wout, qf, lse, spread, pc_l, pc_d)


def _mix_prep_bwd(z, dq, dk, dv, dyg, bf128, g_q, g_k, g_sgu, w_s, b_st, g_go):
    T = z.shape[0]
    tm = _tile(T, 256)
    n = T // tm

    def body(z_ref, dq_ref, dk_ref, dv_ref, dyg_ref, bf_ref, gq_ref, gk_ref, gs_ref, ws_ref,
             bst_ref, go_ref, dz_ref, dgq_ref, dgk_ref, dgs_ref, dgo_ref, dws_ref, dbst_ref, dbf_ref, carry_ref):
        i = pl.program_id(0)
        first = i == 0

        @pl.when(first)
        def _():
            carry_ref[...] = jnp.zeros_like(carry_ref)

        lane = lax.broadcasted_iota(jnp.int32, (tm, LANES), 1)
        dc = jnp.zeros((tm, LANES), F32)
        gq_rows, gk_rows = [], []
        for h in range(FOX_HEADS):
            hp = slice(h * HB, h * HB + FOX_HD)
            dqh, gqr = _norm_bwd(dq_ref[:, hp] * 0.125, z_ref[:, Z_Q + h * FOX_HD:Z_Q + (h + 1) * FOX_HD], gq_ref[...])
            dkh, gkr = _norm_bwd(dk_ref[:, hp], z_ref[:, Z_K + h * FOX_HD:Z_K + (h + 1) * FOX_HD], gk_ref[...])
            dz_ref[:, Z_Q + h * FOX_HD:Z_Q + (h + 1) * FOX_HD] = dqh.astype(BF)
            dz_ref[:, Z_K + h * FOX_HD:Z_K + (h + 1) * FOX_HD] = dkh.astype(BF)
            dz_ref[:, Z_V + h * FOX_HD:Z_V + (h + 1) * FOX_HD] = dv_ref[:, hp].astype(BF)
            dch = dq_ref[:, h * HB + COL_A:h * HB + COL_A + 1] - dk_ref[:, h * HB + COL_B:h * HB + COL_B + 1]
            dc = jnp.where(lane == h, dch, dc)
            gq_rows.append(gqr)
            gk_rows.append(gkr)
        _acc_rows(dgq_ref, first, functools.reduce(lambda a, b: a + b, gq_rows))
        _acc_rows(dgk_ref, first, functools.reduce(lambda a, b: a + b, gk_rows))

        dlogf = _hi(_tri(tm, False).astype(F32), dc) + carry_ref[...]
        carry_ref[...] = dlogf[0:1, :]
        fl = z_ref[:, Z_F:Z_F + LANES] + bf_ref[...]
        lane = lax.broadcasted_iota(jnp.int32, (tm, LANES), 1)
        df = jnp.where(lane < FOX_HEADS, dlogf * jax.nn.sigmoid(-fl), 0.0)
        dz_ref[:, Z_F:Z_F + LANES] = df.astype(BF)
        _acc_rows(dbf_ref, first, df)

        u_pre = z_ref[:, Z_U:Z_U + GMLP_W]
        vg_pre = z_ref[:, Z_G:Z_G + GMLP_W]
        u = _gelu(u_pre)
        vg = _gelu(vg_pre)
        vgn = (vg * _rstd(vg) * gs_ref[...]).astype(BF)
        bst = bst_ref[...]
        mixed, wms = _spatial_mix(vgn, ws_ref, bst, tm)
        sgu = u * mixed
        dsgu, gor = _norm_bwd(dyg_ref[...], sgu, go_ref[...])
        _acc_rows(dgo_ref, first, gor)
        du = dsgu * mixed
        dmixed = dsgu * u
        dmb = dmixed.astype(BF)
        tril = _tri(CHUNK, True)
        dvgn_rows = []
        dws = [None] * GMLP_G
        dbs = [None] * GMLP_G
        for c in range(tm // CHUNK):
            cs = slice(c * CHUNK, (c + 1) * CHUNK)
            cols = []
            for g in range(GMLP_G):
                gs = slice(g * GMLP_GD, (g + 1) * GMLP_GD)
                dmc = dmb[cs, gs]
                pw = _nt(dmc, vgn[cs, gs])
                pb = jnp.sum(dmixed[cs, gs], axis=1, keepdims=True)
                dws[g] = pw if dws[g] is None else dws[g] + pw
                dbs[g] = pb if dbs[g] is None else dbs[g] + pb
                cols.append(_tn(wms[g], dmc))
            dvgn_rows.append(jnp.concatenate(cols, axis=1))
        dvgn = jnp.concatenate(dvgn_rows, axis=0)
        dbs_t = jnp.concatenate(dbs, axis=1)
        for g in range(GMLP_G):
            dwg = jnp.where(tril, dws[g], 0.0)

            @pl.when(first)
            def _():
                dws_ref[g] = dwg

            @pl.when(jnp.logical_not(first))
            def _():
                dws_ref[g] += dwg

        @pl.when(first)
        def _():
            dbst_ref[...] = dbs_t

        @pl.when(jnp.logical_not(first))
        def _():
            dbst_ref[...] += dbs_t

        dvg, gsr = _norm_bwd(dvgn, vg, gs_ref[...])
        _acc_rows(dgs_ref, first, gsr)
        dz_ref[:, Z_U:Z_U + GMLP_W] = (du * _gelu_grad(u_pre)).astype(BF)
        dz_ref[:, Z_G:Z_G + GMLP_W] = (dvg * _gelu_grad(vg_pre)).astype(BF)

    rev = lambda i: (n - 1 - i, 0)
    fix = lambda i: (0, 0)
    fix3 = lambda i: (0, 0, 0)
    return pl.pallas_call(
        body, name="mix_prep_bwd", grid=(n,),
        in_specs=[pl.BlockSpec((tm, ZW), rev), pl.BlockSpec((tm, AUG_W), rev), pl.BlockSpec((tm, AUG_W), rev),
                  pl.BlockSpec((tm, AUG_W), rev), pl.BlockSpec((tm, GMLP_W), rev),
                  pl.BlockSpec((1, LANES), fix), pl.BlockSpec((1, FOX_HD), fix), pl.BlockSpec((1, FOX_HD), fix),
                  pl.BlockSpec((1, GMLP_W), fix), pl.BlockSpec((GMLP_G, CHUNK, CHUNK), fix3),
                  pl.BlockSpec((CHUNK, GMLP_G), fix), pl.BlockSpec((1, GMLP_W), fix)],
        out_specs=[pl.BlockSpec((tm, ZW), rev), pl.BlockSpec((1, FOX_HD), fix), pl.BlockSpec((1, FOX_HD), fix),
                   pl.BlockSpec((1, GMLP_W), fix), pl.BlockSpec((1, GMLP_W), fix),
                   pl.BlockSpec((GMLP_G, CHUNK, CHUNK), fix3), pl.BlockSpec((CHUNK, GMLP_G), fix),
                   pl.BlockSpec((1, LANES), fix)],
        out_shape=[S((T, ZW), BF), S((1, FOX_HD), F32), S((1, FOX_HD), F32), S((1, GMLP_W), F32), S((1, GMLP_W), F32),
                   S((GMLP_G, CHUNK, CHUNK), F32), S((CHUNK, GMLP_G), F32), S((1, LANES), F32)],
        scratch_shapes=[pltpu.VMEM((1, LANES), F32)],
        compiler_params=_cp(1))(z, dq, dk, dv, dyg, bf128, g_q, g_k, g_sgu, w_s, b_st, g_go)


def _mix_proj_bwd(dz, wz, x, g, dy):
    T, D = x.shape
    tm = _tile(T, 512)

    def body(dz_ref, w_ref, x_ref, g_ref, dy_ref, dx_ref, dxb_ref, dg_ref):
        dh = _nn(dz_ref[...], w_ref[...])
        dx, dgr = _norm_bwd(dh, x_ref[...], g_ref[...])
        dx = dx + dy_ref[...]
        dx_ref[...] = dx
        dxb_ref[...] = dx.astype(BF)
        _acc_rows(dg_ref, pl.program_id(0) == 0, dgr)

    row = lambda i: (i, 0)
    fix = lambda i: (0, 0)
    return pl.pallas_call(
        body, name="mix_proj_bwd", grid=(T // tm,),
        in_specs=[pl.BlockSpec((tm, ZW), row), pl.BlockSpec((ZW, D), fix), pl.BlockSpec((tm, D), row),
                  pl.BlockSpec((1, D), fix), pl.BlockSpec((tm, D), row)],
        out_specs=[pl.BlockSpec((tm, D), row), pl.BlockSpec((tm, D), row), pl.BlockSpec((1, D), fix)],
        out_shape=[S((T, D), F32), S((T, D), BF), S((1, D), F32)],
        compiler_params=_cp(1))(dz, wz, x, g, dy)


def _ca_kv(mem, g_mem, wckv, g_ck):
    M, D = mem.shape

    def body(m_ref, g_ref, w_ref, gk_ref, mn_ref, kr_ref, kn_ref, v_ref):
        mf = m_ref[...]
        mn = (mf * _rstd(mf) * g_ref[...]).astype(BF)
        mn_ref[...] = mn
        for h in range(CA_HEADS):
            kr = _nn(mn, w_ref[h])
            kr_ref[h] = kr
            kn_ref[h] = (kr * _rstd(kr) * gk_ref[...]).astype(BF)
            v_ref[h] = _nn(mn, w_ref[CA_HEADS + h]).astype(BF)

    hd = (CA_HEADS, M, CA_HD)
    return pl.pallas_call(
        body, name="ca_kv", out_shape=[S((M, D), BF), S(hd, F32), S(hd, BF), S(hd, BF)],
        compiler_params=pltpu.CompilerParams(vmem_limit_bytes=VMEM_LIMIT))(mem, g_mem, wckv, g_ck)


def _ca_tile_fwd(xt, gca, wcq, gcq, kn_ref, v_ref):
    hb = (xt * _rstd(xt) * gca).astype(BF)
    qc = _nn(hb, wcq)
    qr, qn, ps = [], [], []
    for h in range(CA_HEADS):
        qh = qc[:, h * CA_HD:(h + 1) * CA_HD]
        qnh = (qh * _rstd(qh) * gcq * 0.0625).astype(BF)
        s = _nt(qnh, kn_ref[h])
        e = jnp.exp(s - jnp.max(s, axis=1, keepdims=True))
        ps.append(e / jnp.sum(e, axis=1, keepdims=True))
        qr.append(qh)
        qn.append(qnh)
    return hb, qr, qn, ps


def _ca_fwd(x, g_ca, wcq, g_cq, kn, vv, wco):
    T, D = x.shape
    M = kn.shape[1]
    tm = _tile(T, 256)

    def body(x_ref, gca_ref, wcq_ref, gcq_ref, kn_ref, v_ref, wco_ref, o_ref, ob_sc):
        xt = x_ref[...]
        _, _, _, ps = _ca_tile_fwd(xt, gca_ref[...], wcq_ref[...], gcq_ref[...], kn_ref, v_ref)
        for h in range(CA_HEADS):
            ob_sc[:, h * CA_HD:(h + 1) * CA_HD] = _nn(ps[h].astype(BF), v_ref[h]).astype(BF)
        o_ref[...] = xt + _nn(ob_sc[...], wco_ref[...])

    row = lambda i: (i, 0)
    fix = lambda i: (0, 0)
    fix3 = lambda i: (0, 0, 0)
    return pl.pallas_call(
        body, name="ca_fwd", grid=(T // tm,),
        in_specs=[pl.BlockSpec((tm, D), row), pl.BlockSpec((1, D), fix), pl.BlockSpec((D, D), fix),
                  pl.BlockSpec((1, CA_HD), fix), pl.BlockSpec((CA_HEADS, M, CA_HD), fix3),
                  pl.BlockSpec((CA_HEADS, M, CA_HD), fix3), pl.BlockSpec((D, D), fix)],
        out_specs=pl.BlockSpec((tm, D), row), out_shape=S((T, D), F32),
        scratch_shapes=[pltpu.VMEM((tm, D), BF)],
        compiler_params=_cp(1))(x, g_ca, wcq, g_cq, kn, vv, wco)


def _ca_bwd(x, dy, g_ca, wcq, g_cq, kn, vv, wco):
    T, D = x.shape
    M = kn.shape[1]
    tm = _tile(T, 256)
    n = T // tm

    def body(x_ref, dy_ref, gca_ref, wcq_ref, gcq_ref, kn_ref, v_ref, wco_ref,
             dx_ref, dwq_ref, dwo_ref, dkn_ref, dv_ref, dgcq_ref, dgca_ref, aq_sc, ao_sc, ob_sc, dq_sc):
        i = pl.program_id(0)
        first = i == 0
        xt = x_ref[...]
        dyt = dy_ref[...]
        dyb = dyt.astype(BF)
        hb, qr, qn, ps = _ca_tile_fwd(xt, gca_ref[...], wcq_ref[...], gcq_ref[...], kn_ref, v_ref)
        do = _nt(dyb, wco_ref[...])
        gcq_rows = None
        for h in range(CA_HEADS):
            hs = slice(h * CA_HD, (h + 1) * CA_HD)
            p = ps[h]
            pb = p.astype(BF)
            ob_sc[:, hs] = _nn(pb, v_ref[h]).astype(BF)
            doh = do[:, hs].astype(BF)
            dp = _nt(doh, v_ref[h])
            ds = (p * (dp - jnp.sum(dp * p, axis=1, keepdims=True))).astype(BF)
            dvh = _tn(pb, doh)
            dkh = _tn(ds, qn[h])

            @pl.when(first)
            def _():
                dv_ref[h] = dvh
                dkn_ref[h] = dkh

            @pl.when(jnp.logical_not(first))
            def _():
                dv_ref[h] += dvh
                dkn_ref[h] += dkh

            dqn = _nn(ds, kn_ref[h]) * 0.0625
            dqh, gr = _norm_bwd(dqn, qr[h], gcq_ref[...])
            gcq_rows = gr if gcq_rows is None else gcq_rows + gr
            dq_sc[:, hs] = dqh.astype(BF)
        _acc_rows(dgcq_ref, first, gcq_rows)
        dqb = dq_sc[...]
        p_o = _tn(ob_sc[...], dyb)
        p_q = _tn(hb, dqb)

        @pl.when(first)
        def _():
            ao_sc[...] = p_o
            aq_sc[...] = p_q

        @pl.when(jnp.logical_not(first))
        def _():
            ao_sc[...] += p_o
            aq_sc[...] += p_q

        @pl.when(i == n - 1)
        def _():
            dwo_ref[...] = ao_sc[...].astype(BF)
            dwq_ref[...] = aq_sc[...].astype(BF)

        dh = _nt(dqb, wcq_ref[...])
        dx, gar = _norm_bwd(dh, xt, gca_ref[...])
        dx_ref[...] = dx + dyt
        _acc_rows(dgca_ref, first, gar)

    row = lambda i: (i, 0)
    fix = lambda i: (0, 0)
    fix3 = lambda i: (0, 0, 0)
    hd = (CA_HEADS, M, CA_HD)
    return pl.pallas_call(
        body, name="ca_bwd", grid=(n,),
        in_specs=[pl.BlockSpec((tm, D), row), pl.BlockSpec((tm, D), row), pl.BlockSpec((1, D), fix),
                  pl.BlockSpec((D, D), fix), pl.BlockSpec((1, CA_HD), fix), pl.BlockSpec(hd, fix3),
                  pl.BlockSpec(hd, fix3), pl.BlockSpec((D, D), fix)],
        out_specs=[pl.BlockSpec((tm, D), row), pl.BlockSpec((D, D), fix), pl.BlockSpec((D, D), fix),
                   pl.BlockSpec(hd, fix3), pl.BlockSpec(hd, fix3), pl.BlockSpec((1, CA_HD), fix),
                   pl.BlockSpec((1, D), fix)],
        out_shape=[S((T, D), F32), S((D, D), BF), S((D, D), BF), S(hd, F32), S(hd, F32), S((1, CA_HD), F32),
                   S((1, D), F32)],
        scratch_shapes=[pltpu.VMEM((D, D), F32), pltpu.VMEM((D, D), F32), pltpu.VMEM((tm, D), BF),
                        pltpu.VMEM((tm, D), BF)],
        compiler_params=_cp(1))(x, dy, g_ca, wcq, g_cq, kn, vv, wco)


def _ca_kv_bwd(mem, g_mem, mn, kraw, dkn, dvv, wckv, g_ck):
    M, D = mem.shape

    def body(m_ref, g_ref, mn_ref, kr_ref, dkn_ref, dv_ref, w_ref, gk_ref, dw_ref, dgk_ref, dgm_ref):
        mn = mn_ref[...]
        dmn = jnp.zeros((M, D), F32)
        gk_rows = None
        for h in range(CA_HEADS):
            dkr, gr = _norm_bwd(dkn_ref[h], kr_ref[h], gk_ref[...])
            gk_rows = gr if gk_rows is None else gk_rows + gr
            dkb = dkr.astype(BF)
            dvb = dv_ref[h].astype(BF)
            dw_ref[h] = _tn(mn, dkb).astype(BF)
            dw_ref[CA_HEADS + h] = _tn(mn, dvb).astype(BF)
            dmn = dmn + _nt(dkb, w_ref[h]) + _nt(dvb, w_ref[CA_HEADS + h])
        dgk_ref[...] = jnp.sum(gk_rows, axis=0, keepdims=True)
        mf = m_ref[...]
        dgm_ref[...] = jnp.sum(dmn * (mf * _rstd(mf)), axis=0, keepdims=True)

    return pl.pallas_call(
        body, name="ca_kv_bwd",
        out_shape=[S((2 * CA_HEADS, D, CA_HD), BF), S((1, CA_HD), F32), S((1, D), F32)],
        compiler_params=pltpu.CompilerParams(vmem_limit_bytes=VMEM_LIMIT))(mem, g_mem, mn, kraw, dkn, dvv, wckv, g_ck)


def _after(g, token):
    return g if token is None else g + token[0:1, 0:1]


def _local_step(x, mem, target, small, weights, emit):
    T, D = x.shape
    p = small
    bf128 = jnp.pad(p["b_f"], ((0, 0), (0, LANES - FOX_HEADS)))
    b_st = p["b_s"].T

    wup1 = weights("ffn1_up", x)["wup1"]
    a1, h1 = _ffn_up("ffn1_up", x, p["g_ffn1"], wup1)
    wdn1 = weights("ffn1_dn", h1)["wdn1"]
    x1 = _ffn_down("ffn1_down", a1, wdn1, x)
    wm = weights("mix", x1)
    z, h2 = _mix_proj(x1, p["g_mix"], wm["wz"])
    qf, ka, va, yg = _mix_prep(z, bf128, p["g_q"], p["g_k"], p["g_sgu"], p["w_s"], b_st, p["g_gmlp_o"])
    attn, lse = _fox_fwd(qf, ka, va)
    x2 = _mix_out(attn, yg, p["g_fox_o"], wm["wout"], x1)
    wc = weights("ca", x2)
    mn, kraw, ckn, cvv = _ca_kv(mem, p["g_mem"], wc["wckv"], p["g_ck"])
    x3 = _ca_fwd(x2, p["g_ca"], wc["wcq"], p["g_cq"], ckn, cvv, wc["wco"])
    w2 = weights("ffn2", x3)
    a2, h4 = _ffn_up("ffn2_up", x3, p["g_ffn2"], w2["wup2"])
    dy4, dy4b, sq = _ffn_down_loss("ffn2_down", a2, w2["wdn2"], x3, target)

    gs = {}
    dgu2 = _ffn_bwd_act("ffn2_bwd_act", dy4b, h4, w2["wup2"], w2["wdn2"])
    tok = emit("ffn2", {"wup2": _ffn_dwup("ffn2", h4, dgu2), "wdn2": _ffn_dwdn("ffn2", a2, dy4b)})
    dx3, gs["g_ffn2"] = _ffn_dx("ffn2_dx", dgu2, w2["wup2"], x3, _after(p["g_ffn2"], tok), dy4)

    dx2, dwcq, dwco, dckn, dcvv, gs["g_cq"], gs["g_ca"] = _ca_bwd(
        x2, dx3, p["g_ca"], wc["wcq"], p["g_cq"], ckn, cvv, wc["wco"])
    dwckv, gs["g_ck"], gs["g_mem"] = _ca_kv_bwd(mem, p["g_mem"], mn, kraw, dckn, dcvv, wc["wckv"], p["g_ck"])

    qb, dob, dyg, dwout, gs["g_fox_o"] = _mix_out_bwd(dx2, attn, yg, p["g_fox_o"], wm["wout"], qf, lse)
    dq, dk, dv = _fox_bwd(qb, ka, va, dob)
    dz, gs["g_q"], gs["g_k"], gs["g_sgu"], gs["g_gmlp_o"], gs["w_s"], dbst, dbf = _mix_prep_bwd(
        z, dq, dk, dv, dyg, bf128, p["g_q"], p["g_k"], p["g_sgu"], p["w_s"], b_st, p["g_gmlp_o"])
    gs["b_s"] = dbst.T
    gs["b_f"] = dbf[:, :FOX_HEADS]
    tok_ws = emit("w_s", {"w_s": gs["w_s"]})
    tk = _tile(T, 1024)
    zb = ZW // 3
    dwz = _tn_matmul(
        "mix_dwz", dz, pl.BlockSpec((tk, zb), lambda j, k: (k, j)), h2, pl.BlockSpec((tk, D), lambda j, k: (k, 0)),
        S((ZW, D), F32), pl.BlockSpec((zb, D), lambda j, k: (j, 0)), (3, T // tk), (zb, D))
    tok = emit("mid", {"wcq": dwcq, "wco": dwco, "wckv": dwckv, "wout": dwout, "wz": dwz})
    dx1, dx1b, gs["g_mix"] = _mix_proj_bwd(dz, wm["wz"], x1, _after(_after(p["g_mix"], tok), tok_ws), dx2)

    dgu1 = _ffn_bwd_act("ffn1_bwd_act", dx1b, h1, wup1, wdn1)
    tok = emit("ffn1_dn", {"wdn1": _ffn_dwdn("ffn1", a1, dx1b)})
    tok = emit("ffn1_up", {"wup1": _ffn_dwup("ffn1", h1, dgu1, after=tok)})
    dx0, gs["g_ffn1"] = _ffn_dx("ffn1_dx", dgu1, wup1, x, _after(p["g_ffn1"], tok), dx1)
    return sq, dx0, gs


MESH = pl.DeviceIdType.MESH
HBM_SPEC = pl.BlockSpec(memory_space=pltpu.HBM)
N_PEER = N_DEV - 1


def _place():
    return lax.axis_index("x"), lax.axis_index("y"), lax.axis_index("c")


def _slot(px, py, pc):
    return 4 * px + 2 * py + pc


SEM_SPEC = pl.BlockSpec(memory_space=pltpu.SEMAPHORE)
ANY_SPEC = pl.BlockSpec(memory_space=pl.ANY)
DATAFLOW = pltpu.SideEffectType.DATAFLOW_SIDE_EFFECTING


def _hbm(a):
    return pltpu.with_memory_space_constraint(a, pltpu.HBM)


def _peer(x, y, c, r):
    return (1 - x if r & 4 else x, 1 - y if r & 2 else y, 1 - c if r & 1 else c)


def _place_own(srcs, whole):
    my = _slot(*_place())
    lands = []
    for s in srcs:
        blk = s[None] if whole else lax.dynamic_slice_in_dim(s, my, 1, 0)
        shape = (N_DEV,) + s.shape if whole else s.shape
        lands.append(lax.dynamic_update_slice_in_dim(lax.empty(shape, s.dtype), blk, my, 0))
    return lands


def _copy_start(name, srcs, lands, whole):
    n = len(srcs)

    def body(*refs):
        src, land = refs[:n], refs[n:2 * n]
        send, recv = refs[2 * n:3 * n], refs[3 * n:4 * n]
        token = refs[6 * n]
        x, y, c = _place()
        my = _slot(x, y, c)
        for a in range(n):
            for r in range(1, N_DEV):
                p = _peer(x, y, c, r)
                pltpu.make_async_remote_copy(
                    src_ref=src[a] if whole else src[a].at[_slot(*p)], dst_ref=land[a].at[my],
                    send_sem=send[a].at[r - 1], recv_sem=recv[a].at[r - 1], device_id=p, device_id_type=MESH).start()
        token[...] = jnp.zeros_like(token)

    out = pl.pallas_call(
        body, name=name,
        out_shape=([pltpu.SemaphoreType.DMA((N_PEER,))] * (2 * n)
                   + [pltpu.HBM(s.shape, s.dtype) for s in srcs] + [pltpu.HBM(s.shape, s.dtype) for s in lands]
                   + [S((8, LANES), F32)]),
        in_specs=[HBM_SPEC] * (2 * n),
        out_specs=[SEM_SPEC] * (2 * n) + [HBM_SPEC] * (2 * n) + [pl.BlockSpec(memory_space=pltpu.VMEM)],
        input_output_aliases={i: 2 * n + i for i in range(2 * n)},
        compiler_params=pltpu.CompilerParams(has_side_effects=DATAFLOW),
    )(*[_hbm(s) for s in srcs], *[_hbm(s) for s in lands])
    return out[:n], out[n:2 * n], out[2 * n:3 * n], out[3 * n:4 * n], out[4 * n]


def _copy_wait(name, srcs, lands, send, recv, after, whole):
    n = len(srcs)

    def body(*refs):
        src, land = refs[:n], refs[n:2 * n]
        snd, rcv = refs[2 * n:3 * n], refs[3 * n:4 * n]
        x, y, c = _place()
        for a in range(n):
            for r in range(1, N_DEV):
                p = _peer(x, y, c, r)
                ps = _slot(*p)
                cp = pltpu.make_async_remote_copy(
                    src_ref=src[a] if whole else src[a].at[ps], dst_ref=land[a].at[ps],
                    send_sem=snd[a].at[r - 1], recv_sem=rcv[a].at[r - 1], device_id=p, device_id_type=MESH)
                cp.wait_send()
                cp.wait_recv()

    out = pl.pallas_call(
        body, name=name,
        out_shape=[pltpu.HBM(s.shape, s.dtype) for s in srcs] + [pltpu.HBM(s.shape, s.dtype) for s in lands],
        in_specs=[HBM_SPEC] * (2 * n) + [SEM_SPEC] * (2 * n) + [ANY_SPEC],
        out_specs=[HBM_SPEC] * (2 * n),
        input_output_aliases={i: i for i in range(2 * n)},
        compiler_params=pltpu.CompilerParams(has_side_effects=DATAFLOW),
    )(*srcs, *lands, *send, *recv, after)
    return out[n:]


def _adamw(w, g, m, v):
    m2 = ADAM_B1 * m + (1.0 - ADAM_B1) * g
    v2 = ADAM_B2 * v + (1.0 - ADAM_B2) * (g * g)
    m_hat = m2 / (1.0 - ADAM_B1 ** ADAM_STEP)
    v_hat = v2 / (1.0 - ADAM_B2 ** ADAM_STEP)
    delta = -ADAM_LR * (m_hat / (jnp.sqrt(v_hat) + ADAM_EPS) + ADAM_WD * w)
    return delta, m2, v2


def _adamw_big(name, slots, w, m, v):
    R, C = w.shape
    tr = next((t for t in (256, 352) if R % t == 0), R)

    def body(s_ref, w_ref, m_ref, v_ref, g_ref, d_ref, m2_ref, v2_ref):
        g = s_ref[0].astype(F32)
        for k in range(1, N_DEV):
            g = g + s_ref[k].astype(F32)
        d, m2, v2 = _adamw(w_ref[...], g, m_ref[...], v_ref[...])
        g_ref[...] = g
        d_ref[...] = d
        m2_ref[...] = m2
        v2_ref[...] = v2

    row = pl.BlockSpec((tr, C), lambda i: (i, 0))
    return pl.pallas_call(
        body, name=name, grid=(R // tr,),
        in_specs=[pl.BlockSpec((N_DEV, tr, C), lambda i: (0, i, 0)), row, row, row],
        out_specs=[row] * 4, out_shape=[S((R, C), F32)] * 4,
        compiler_params=_cp(1))(slots, w, m, v)


TINY_ROWS = (("b_s", 8), ("g_ffn1", 8), ("g_mix", 8), ("g_ca", 8), ("g_mem", 8), ("g_ffn2", 8), ("g_sgu", 4),
             ("g_fox_o", 4), ("g_gmlp_o", 4), ("g_cq", 2), ("g_ck", 2), ("g_q", 1), ("g_k", 1), ("b_f", 1))
TINY_P = 72


def _pack_tiny(d):
    rows = []
    for name, r in TINY_ROWS:
        flat = d[name].reshape(-1)
        rows.append(jnp.pad(flat, (0, r * LANES - flat.shape[0])).reshape(r, LANES))
    used = sum(r for _, r in TINY_ROWS)
    rows.append(jnp.zeros((TINY_P - used, LANES), F32))
    return jnp.concatenate(rows, axis=0)


def _unpack_tiny(packed, shapes):
    out, at = {}, 0
    for name, r in TINY_ROWS:
        shape = shapes[name]
        size = 1
        for s in shape:
            size *= s
        out[name] = packed[at:at + r].reshape(-1)[:size].reshape(shape)
        at += r
    return out


WEIGHTS =('g_ffn1', 'w_ffn1_in', 'w_ffn1_out', 'g_mix', 'w_in', 'b_f', 'g_q', 'g_k', 'g_sgu', 'w_s', 'b_s',
           'g_fox_o', 'g_gmlp_o', 'w_out', 'g_ca', 'g_mem', 'w_cq', 'w_ckv', 'g_cq', 'g_ck', 'w_co', 'g_ffn2',
           'w_ffn2_in', 'w_ffn2_out')
BIG = ('w_ffn1_in', 'w_ffn1_out', 'w_in', 'w_out', 'w_cq', 'w_ckv', 'w_co', 'w_ffn2_in', 'w_ffn2_out')
TRANSPOSED = ('w_ffn1_in', 'w_in', 'w_ffn2_in')
GATHER_GROUPS = {"ffn1_up": ("w_ffn1_in",), "ffn1_dn": ("w_ffn1_out",), "mix": ("w_in", "w_out"),
                 "ca": ("w_cq", "w_ckv", "w_co"), "ffn2": ("w_ffn2_in", "w_ffn2_out")}
QKV_W = 3 * FOX_W
UV_OFF = QKV_W + FOX_HEADS


def kernel(x, mem, g_ffn1, w_ffn1_in, w_ffn1_out, g_mix, w_in, b_f, g_q, g_k, g_sgu, w_s, b_s, g_fox_o, g_gmlp_o, w_out, g_ca, g_mem, w_cq, w_ckv, g_cq, g_ck, w_co, g_ffn2, w_ffn2_in, w_ffn2_out, loss_target, m_g_ffn1, m_w_ffn1_in, m_w_ffn1_out, m_g_mix, m_w_in, m_b_f, m_g_q, m_g_k, m_g_sgu, m_w_s, m_b_s, m_g_fox_o, m_g_gmlp_o, m_w_out, m_g_ca, m_g_mem, m_w_cq, m_w_ckv, m_g_cq, m_g_ck, m_w_co, m_g_ffn2, m_w_ffn2_in, m_w_ffn2_out, v_g_ffn1, v_w_ffn1_in, v_w_ffn1_out, v_g_mix, v_w_in, v_b_f, v_g_q, v_g_k, v_g_sgu, v_w_s, v_b_s, v_g_fox_o, v_g_gmlp_o, v_w_out, v_g_ca, v_g_mem, v_w_cq, v_w_ckv, v_g_cq, v_g_ck, v_w_co, v_g_ffn2, v_w_ffn2_in, v_w_ffn2_out):
    args = dict(locals())
    w = {n: args[n] for n in WEIGHTS}
    mo = {n: args["m_" + n] for n in WEIGHTS}
    vo = {n: args["v_" + n] for n in WEIGHTS}
    D = D_MODEL

    def local(n, a):
        return a[0].T if n in TRANSPOSED else a[0]

    shards = [local(n, w[n]).astype(BF) for n in BIG]
    fb = shards[0].shape[0]
    g_snd, g_rcv, g_src, g_land, g_token = _copy_start("gather_start", shards, _place_own(shards, True), True)
    handles = {n: (g_src[i], g_land[i], g_snd[i], g_rcv[i]) for i, n in enumerate(BIG)}

    tiny_names = [n for n, _ in TINY_ROWS]
    tiny_wmv = [_pack_tiny({n: a[n] for n in tiny_names}) + g_token[0:1, 0:1] for a in (w, mo, vo)]
    first_after = tiny_wmv[0][0:8] + tiny_wmv[1][0:8] + tiny_wmv[2][0:8]

    def weights(group, after):
        names = GATHER_GROUPS[group]
        hs = [handles[n] for n in names]
        got = _copy_wait("gather_wait_" + group, [h[0] for h in hs], [h[1] for h in hs], [h[2] for h in hs],
                         [h[3] for h in hs], first_after if group == "ffn1_up" else after, True)
        got = dict(zip(names, got))
        if group == "ffn1_up":
            return {"wup1": got["w_ffn1_in"].reshape(2, N_FFN_BLK, fb, D)}
        if group == "ffn1_dn":
            return {"wdn1": got["w_ffn1_out"].reshape(N_FFN_BLK, fb, D)}
        if group == "mix":
            full = got["w_in"].reshape(-1, D)
            wz = jnp.concatenate([full[:QKV_W], full[UV_OFF:], full[QKV_W:UV_OFF],
                                  jnp.zeros((LANES - FOX_HEADS, D), BF)], axis=0)
            return {"wz": wz, "wout": got["w_out"].reshape(D, D)}
        if group == "ca":
            return {"wcq": got["w_cq"].reshape(D, D), "wco": got["w_co"].reshape(D, D), "wckv": got["w_ckv"]}
        return {"wup2": got["w_ffn2_in"].reshape(2, N_FFN_BLK, fb, D),
                "wdn2": got["w_ffn2_out"].reshape(N_FFN_BLK, fb, D)}

    flying = {}

    def emit(group, g):
        if group == "w_s":
            part = [g["w_s"].reshape(-1, LANES)]
            *copies, token = _copy_start("w_s_start", part, _place_own(part, True), True)
            flying[group] = copies
            return token
        if group == "ffn2":
            parts = {"w_ffn2_in": g["wup2"], "w_ffn2_out": g["wdn2"].reshape(N_DEV, -1, D)}
        elif group == "ffn1_dn":
            parts = {"w_ffn1_out": g["wdn1"].reshape(N_DEV, -1, D)}
        elif group == "ffn1_up":
            parts = {"w_ffn1_in": g["wup1"]}
        else:
            gz = g["wz"]
            g_in = jnp.concatenate([gz[:QKV_W], gz[Z_F:Z_F + FOX_HEADS], gz[QKV_W:Z_F]], axis=0)
            parts = {"w_in": g_in.reshape(N_DEV, -1, D).astype(BF),
                     "w_out": g["wout"].reshape(N_DEV, -1, D), "w_cq": g["wcq"].reshape(N_DEV, -1, D),
                     "w_co": g["wco"].reshape(N_DEV, -1, D), "w_ckv": g["wckv"]}
        names = list(parts)
        srcs = [parts[n] for n in names]
        *copies, token = _copy_start("exchange_start_" + group, srcs, _place_own(srcs, False), False)
        flying[group] = (names, copies)
        return token

    small = {n: (w[n][0] if n == "b_s" else w[n]) for n in tiny_names}
    small["w_s"] = w["w_s"][0]

    sq, dx0, gs = _local_step(x[0], mem[0], loss_target[0], small, weights, emit)
    loss = lax.psum(sq[0, 0], ("x", "y", "c")) * (0.5 / D)

    sm_parts = [_pack_tiny(gs)]
    sm_snd, sm_rcv, sm_src, sm_land, sm_token = _copy_start("tiny_start", sm_parts, _place_own(sm_parts, True), True)

    grad, delta, new_m, new_v = {}, {}, {}, {}

    def update(group, after):
        names, (snd, rcv, srcs, lands) = flying[group]
        slots = _copy_wait("exchange_wait_" + group, srcs, lands, snd, rcv, after, False)
        for n, sl in zip(names, slots):
            g, d, m2, v2 = _adamw_big("adamw_" + n, sl, local(n, w[n]), local(n, mo[n]), local(n, vo[n]))
            grad[n], delta[n], new_m[n], new_v[n] = (
                (t.T if n in TRANSPOSED else t).reshape(w[n].shape) for t in (g, d, m2, v2))
        return d

    last = update("ffn2", sm_token)
    last = update("mid", last)
    last = update("ffn1_dn", last)
    last = update("ffn1_up", last)
    ws_snd, ws_rcv, ws_src, ws_land = flying["w_s"]
    ws_all, = _copy_wait("w_s_wait", ws_src, ws_land, ws_snd, ws_rcv, last, True)
    tiny_all, = _copy_wait("tiny_wait", sm_src, sm_land, sm_snd, sm_rcv, ws_all, True)
    ws_shape = w["w_s"].shape
    for store, t in zip((grad, delta, new_m, new_v), _adamw_big(
            "adamw_w_s", ws_all, *[a["w_s"].reshape(-1, LANES) for a in (w, mo, vo)])):
        store["w_s"] = t.reshape(ws_shape)
    shapes = {n: w[n].shape for n in tiny_names}
    for store, t in zip((grad, delta, new_m, new_v), _adamw_big(
            "adamw_tiny", tiny_all, *tiny_wmv)):
        store.update(_unpack_tiny(t, shapes))

    return (loss, dx0[None], *[grad[n] for n in WEIGHTS], *[delta[n] for n in WEIGHTS],
            *[new_m[n] for n in WEIGHTS], *[new_v[n] for n in WEIGHTS])
```

```python
import functools

import jax
import jax.numpy as jnp
from jax import lax
from jax.experimental import pallas as pl
from jax.experimental.pallas import tpu as pltpu

F32 = jnp.float32
BF = jnp.bfloat16
S = jax.ShapeDtypeStruct

N_DEV = 8
D_MODEL = 1024
FOX_HEADS, FOX_HD = 8, 64
FOX_W = 512
GMLP_G, GMLP_GD = 8, 64
GMLP_W = 512
CHUNK = 128
CA_HEADS, CA_HD = 4, 256
N_FFN_BLK = 4
ZW = 2688
Z_Q, Z_K, Z_V, Z_U, Z_G, Z_F = 0, 512, 1024, 1536, 2048, 2560
EPS = 1e-6
NEG = -1e30
LANES = 128

ADAM_LR, ADAM_B1, ADAM_B2, ADAM_EPS, ADAM_WD, ADAM_STEP = 0.001, 0.9, 0.999, 1e-08, 0.01, 10

VMEM_LIMIT = 52 * 2 ** 20


def _cp(n_axes):
    return pltpu.CompilerParams(dimension_semantics=("arbitrary",) * n_axes, vmem_limit_bytes=VMEM_LIMIT)


def _nn(a, b):
    return jnp.dot(a, b, preferred_element_type=F32)


def _nt(a, b):
    return lax.dot_general(a, b, (((1,), (1,)), ((), ())), preferred_element_type=F32)


def _tn(a, b):
    return lax.dot_general(a, b, (((0,), (0,)), ((), ())), preferred_element_type=F32)


def _hi(a, b):
    return jnp.dot(a, b, precision=lax.Precision.HIGHEST, preferred_element_type=F32)


def _rstd(x):
    return lax.rsqrt(jnp.mean(x * x, axis=-1, keepdims=True) + EPS)


def _norm_bwd(dy, x, g):
    r = _rstd(x)
    xh = x * r
    dxh = dy * g
    dx = r * (dxh - xh * jnp.mean(dxh * xh, axis=-1, keepdims=True))
    return dx, dy * xh


def _acc_rows(ref, first, val):
    srow = jnp.sum(val, axis=0, keepdims=True)

    @pl.when(first)
    def _():
        ref[...] = srow

    @pl.when(jnp.logical_not(first))
    def _():
        ref[...] += srow


def _gelu(x):
    c = 0.7978845608028654
    return 0.5 * x * (1.0 + jnp.tanh(c * (x + 0.044715 * x * x * x)))


def _gelu_grad(x):
    c = 0.7978845608028654
    t = jnp.tanh(c * (x + 0.044715 * x * x * x))
    return 0.5 * (1.0 + t) + 0.5 * x * (1.0 - t * t) * c * (1.0 + 3 * 0.044715 * x * x)


def _tile(n, pref):
    return pref if n % pref == 0 else n


def _ffn_up(name, x, g, wup):
    T, D = x.shape
    FB = wup.shape[-2]
    tm = _tile(T, 1024)

    def body(x_ref, g_ref, w_ref, a_ref, h_ref):
        @pl.when(pl.program_id(1) == 0)
        def _():
            xf = x_ref[...]
            h_ref[...] = (xf * _rstd(xf) * g_ref[...]).astype(BF)

        hb = h_ref[...]
        gg = _nt(hb, w_ref[0])
        uu = _nt(hb, w_ref[1])
        a_ref[...] = (gg * jax.nn.sigmoid(gg) * uu).astype(BF)

    return pl.pallas_call(
        body, name=name, grid=(T // tm, N_FFN_BLK),
        in_specs=[pl.BlockSpec((tm, D), lambda i, j: (i, 0)),
                  pl.BlockSpec((1, D), lambda i, j: (0, 0)),
                  pl.BlockSpec((2, None, FB, D), lambda i, j: (0, j, 0, 0))],
        out_specs=[pl.BlockSpec((None, tm, FB), lambda i, j: (j, i, 0)),
                   pl.BlockSpec((tm, D), lambda i, j: (i, 0))],
        out_shape=[S((N_FFN_BLK, T, FB), BF), S((T, D), BF)],
        compiler_params=_cp(2))(x, g, wup)


def _ffn_down(name, a, wdn, x):
    _, T, FB = a.shape
    D = x.shape[1]
    tm = _tile(T, 512)

    def body(a_ref, w_ref, x_ref, o_ref):
        p = _nn(a_ref[0], w_ref[0])
        for j in range(1, N_FFN_BLK):
            p = p + _nn(a_ref[j], w_ref[j])
        o_ref[...] = x_ref[...] + 0.5 * p

    return pl.pallas_call(
        body, name=name, grid=(T // tm,),
        in_specs=[pl.BlockSpec((N_FFN_BLK, tm, FB), lambda i: (0, i, 0)),
                  pl.BlockSpec((N_FFN_BLK, FB, D), lambda i: (0, 0, 0)),
                  pl.BlockSpec((tm, D), lambda i: (i, 0))],
        out_specs=pl.BlockSpec((tm, D), lambda i: (i, 0)),
        out_shape=S((T, D), F32),
        compiler_params=_cp(1))(a, wdn, x)


def _ffn_down_loss(name, a, wdn, x, target):
    _, T, FB = a.shape
    D = x.shape[1]
    tm = _tile(T, 512)

    def body(a_ref, w_ref, x_ref, t_ref, d_ref, db_ref, loss_ref):
        i = pl.program_id(0)
        p = _nn(a_ref[0], w_ref[0])
        for j in range(1, N_FFN_BLK):
            p = p + _nn(a_ref[j], w_ref[j])
        diff = (x_ref[...] + 0.5 * p) - t_ref[...]
        dy = diff * (1.0 / D)
        d_ref[...] = dy
        db_ref[...] = dy.astype(BF)
        sq = jnp.zeros((8, LANES), F32) + jnp.sum(diff * diff)

        @pl.when(i == 0)
        def _():
            loss_ref[...] = sq

        @pl.when(i > 0)
        def _():
            loss_ref[...] += sq

    row = pl.BlockSpec((tm, D), lambda i: (i, 0))
    return pl.pallas_call(
        body, name=name, grid=(T // tm,),
        in_specs=[pl.BlockSpec((N_FFN_BLK, tm, FB), lambda i: (0, i, 0)),
                  pl.BlockSpec((N_FFN_BLK, FB, D), lambda i: (0, 0, 0)), row, row],
        out_specs=[row, row, pl.BlockSpec((8, LANES), lambda i: (0, 0))],
        out_shape=[S((T, D), F32), S((T, D), BF), S((8, LANES), F32)],
        compiler_params=_cp(1))(a, wdn, x, target)


def _ffn_bwd_act(name, dyb, h, wup, wdn):
    T, D = h.shape
    FB = wup.shape[-2]
    tm = _tile(T, 1024)

    def body(d_ref, h_ref, wu_ref, wd_ref, o_ref):
        da = 0.5 * _nt(d_ref[...], wd_ref[...])
        hb = h_ref[...]
        gg = _nt(hb, wu_ref[0])
        uu = _nt(hb, wu_ref[1])
        sg = jax.nn.sigmoid(gg)
        o_ref[0] = (da * uu * (sg * (1.0 + gg * (1.0 - sg)))).astype(BF)
        o_ref[1] = (da * (gg * sg)).astype(BF)

    return pl.pallas_call(
        body, name=name, grid=(T // tm, N_FFN_BLK),
        in_specs=[pl.BlockSpec((tm, D), lambda i, j: (i, 0)),
                  pl.BlockSpec((tm, D), lambda i, j: (i, 0)),
                  pl.BlockSpec((2, None, FB, D), lambda i, j: (0, j, 0, 0)),
                  pl.BlockSpec((None, FB, D), lambda i, j: (j, 0, 0))],
        out_specs=pl.BlockSpec((2, None, tm, FB), lambda i, j: (0, j, i, 0)),
        out_shape=S((2, N_FFN_BLK, T, FB), BF),
        compiler_params=_cp(2))(dyb, h, wup, wdn)


def _ffn_dx(name, dgu, wup, x, g, dy):
    T, D = x.shape
    FB = wup.shape[-2]
    tm = _tile(T, 1024)

    def body(d_ref, w_ref, x_ref, g_ref, dy_ref, dx_ref, dg_ref, acc_ref):
        i, j = pl.program_id(0), pl.program_id(1)
        p = _nn(d_ref[0], w_ref[0]) + _nn(d_ref[1], w_ref[1])

        @pl.when(j == 0)
        def _():
            acc_ref[...] = p

        @pl.when(j > 0)
        def _():
            acc_ref[...] += p

        @pl.when(j == N_FFN_BLK - 1)
        def _():
            dx, dgr = _norm_bwd(acc_ref[...], x_ref[...], g_ref[...])
            dx_ref[...] = dx + dy_ref[...]
            _acc_rows(dg_ref, i == 0, dgr)

    return pl.pallas_call(
        body, name=name, grid=(T // tm, N_FFN_BLK),
        in_specs=[pl.BlockSpec((2, None, tm, FB), lambda i, j: (0, j, i, 0)),
                  pl.BlockSpec((2, None, FB, D), lambda i, j: (0, j, 0, 0)),
                  pl.BlockSpec((tm, D), lambda i, j: (i, 0)),
                  pl.BlockSpec((1, D), lambda i, j: (0, 0)),
                  pl.BlockSpec((tm, D), lambda i, j: (i, 0))],
        out_specs=[pl.BlockSpec((tm, D), lambda i, j: (i, 0)),
                   pl.BlockSpec((1, D), lambda i, j: (0, 0))],
        out_shape=[S((T, D), F32), S((1, D), F32)],
        scratch_shapes=[pltpu.VMEM((tm, D), F32)],
        compiler_params=_cp(2))(dgu, wup, x, g, dy)


def _tn_matmul(name, a, a_spec, b, b_spec, out_shape, out_spec, grid, acc_shape, scale=1.0, after=None):
    nk = grid[1]
    extra = [] if after is None else [after]

    def body(a_ref, b_ref, *rest):
        o_ref, acc_ref = rest[-2:]
        k = pl.program_id(1)
        p = _tn(a_ref[...], b_ref[...])

        @pl.when(k == 0)
        def _():
            acc_ref[...] = p

        @pl.when(k > 0)
        def _():
            acc_ref[...] += p

        @pl.when(k == nk - 1)
        def _():
            o_ref[...] = (acc_ref[...] * scale).astype(o_ref.dtype)

    return pl.pallas_call(
        body, name=name, grid=grid,
        in_specs=[a_spec, b_spec] + [pl.BlockSpec((8, LANES), lambda j, k: (0, 0)) for _ in extra],
        out_specs=out_spec, out_shape=out_shape,
        scratch_shapes=[pltpu.VMEM(acc_shape, F32)], compiler_params=_cp(2))(a, b, *extra)


def _ffn_dwup(name, h, dgu, after=None):
    T, D = h.shape
    FB = dgu.shape[-1]
    tk = _tile(T, 1024)
    return _tn_matmul(
        name + "_dwup", dgu.reshape(2 * N_FFN_BLK, T, FB), pl.BlockSpec((None, tk, FB), lambda j, k: (j, k, 0)),
        h, pl.BlockSpec((tk, D), lambda j, k: (k, 0)),
        S((2 * N_FFN_BLK, FB, D), BF), pl.BlockSpec((None, FB, D), lambda j, k: (j, 0, 0)),
        (2 * N_FFN_BLK, T // tk), (FB, D), after=after)


def _ffn_dwdn(name, a, dyb):
    _, T, FB = a.shape
    D = dyb.shape[1]
    tk = _tile(T, 1024)
    return _tn_matmul(
        name + "_dwdn", a, pl.BlockSpec((None, tk, FB), lambda j, k: (j, k, 0)),
        dyb, pl.BlockSpec((tk, D), lambda j, k: (k, 0)),
        S((N_FFN_BLK, FB, D), BF), pl.BlockSpec((None, FB, D), lambda j, k: (j, 0, 0)),
        (N_FFN_BLK, T // tk), (FB, D), scale=0.5)


def _mix_proj(x, g, wz):
    T, D = x.shape
    tm = _tile(T, 512)

    def body(x_ref, g_ref, w_ref, z_ref, h_ref):
        xf = x_ref[...]
        hb = (xf * _rstd(xf) * g_ref[...]).astype(BF)
        h_ref[...] = hb
        z_ref[...] = _nt(hb, w_ref[...])

    return pl.pallas_call(
        body, name="mix_proj", grid=(T // tm,),
        in_specs=[pl.BlockSpec((tm, D), lambda i: (i, 0)),
                  pl.BlockSpec((1, D), lambda i: (0, 0)),
                  pl.BlockSpec((ZW, D), lambda i: (0, 0))],
        out_specs=[pl.BlockSpec((tm, ZW), lambda i: (i, 0)),
                   pl.BlockSpec((tm, D), lambda i: (i, 0))],
        out_shape=[S((T, ZW), F32), S((T, D), BF)],
        compiler_params=_cp(1))(x, g, wz)


def _tri(n, lower):
    r = lax.broadcasted_iota(jnp.int32, (n, n), 0)
    c = lax.broadcasted_iota(jnp.int32, (n, n), 1)
    return (r >= c) if lower else (r <= c)


def _spatial_mix(vgn_b, ws_ref, bst, tm):
    tril = _tri(CHUNK, True)
    wms = [jnp.where(tril, ws_ref[g], 0.0).astype(BF) for g in range(GMLP_G)]
    rows = []
    for c in range(tm // CHUNK):
        cols = []
        for g in range(GMLP_G):
            vs = vgn_b[c * CHUNK:(c + 1) * CHUNK, g * GMLP_GD:(g + 1) * GMLP_GD]
            cols.append(_nn(wms[g], vs) + bst[:, g:g + 1])
        rows.append(jnp.concatenate(cols, axis=1))
    return jnp.concatenate(rows, axis=0), wms


HB = 128
AUG_W = FOX_HEADS * HB
COL_A, COL_B, COL_C = 64, 67, 70


def _spread_matrix():
    r = jnp.arange(FOX_W)
    return (jnp.arange(AUG_W)[None, :] == ((r // FOX_HD) * HB + r % FOX_HD)[:, None]).astype(BF)


def _piece_matrix(col):
    r = jnp.arange(LANES)
    dst = jnp.where(r < 3 * FOX_HEADS, (r % FOX_HEADS) * HB + col + r // FOX_HEADS, -1)
    return (jnp.arange(AUG_W)[None, :] == dst[:, None]).astype(BF)


def _ones_row(cols):
    c = jnp.arange(AUG_W) % HB
    hit = functools.reduce(jnp.logical_or, [(c >= a) & (c < a + 3) for a in cols])
    return hit.astype(F32)[None, :]


def _pieces(x):
    lane = lax.broadcasted_iota(jnp.int32, x.shape, 1)
    x = jnp.where(lane < FOX_HEADS, x, 0.0)
    hi = x.astype(BF).astype(F32)
    r1 = x - hi
    mid = r1.astype(BF).astype(F32)
    lo = (r1 - mid).astype(BF).astype(F32)
    return (hi + pltpu.roll(mid, FOX_HEADS, 1) + pltpu.roll(lo, 2 * FOX_HEADS, 1)).astype(BF)


def _mix_prep(z, bf128, g_q, g_k, g_sgu, w_s, b_st, g_go):
    T = z.shape[0]
    tm = _tile(T, 256)
    spread, pc_q, pc_k = _spread_matrix(), _piece_matrix(COL_A), _piece_matrix(COL_B)
    one_q, one_k, one_v = _ones_row([COL_B]), _ones_row([COL_A, COL_C]), _ones_row([COL_A])

    def body(z_ref, bf_ref, gq_ref, gk_ref, gs_ref, ws_ref, bst_ref, go_ref, sp_ref, pq_ref, pk_ref, oq_ref, ok_ref,
             ov_ref, q_ref, k_ref, v_ref, y_ref, carry_ref, qn_sc, kn_sc):
        i = pl.program_id(0)

        @pl.when(i == 0)
        def _():
            carry_ref[...] = jnp.zeros_like(carry_ref)

        for h in range(FOX_HEADS):
            hs = slice(h * FOX_HD, (h + 1) * FOX_HD)
            qh = z_ref[:, Z_Q + h * FOX_HD:Z_Q + (h + 1) * FOX_HD]
            kh = z_ref[:, Z_K + h * FOX_HD:Z_K + (h + 1) * FOX_HD]
            qn_sc[:, hs] = (qh * _rstd(qh) * gq_ref[...] * 0.125).astype(BF)
            kn_sc[:, hs] = (kh * _rstd(kh) * gk_ref[...]).astype(BF)

        fl = z_ref[:, Z_F:Z_F + LANES] + bf_ref[...]
        logf = jnp.minimum(fl, 0.0) - jnp.log1p(jnp.exp(-jnp.abs(fl)))
        csum = _hi(_tri(tm, True).astype(F32), logf) + carry_ref[...]
        carry_ref[...] = csum[tm - 1:tm, :]
        sp = sp_ref[...]
        q_ref[...] = (_nn(qn_sc[...], sp) + _nn(_pieces(csum), pq_ref[...]) + oq_ref[...]).astype(BF)
        k_ref[...] = (_nn(kn_sc[...], sp) + _nn(_pieces(-csum), pk_ref[...]) + ok_ref[...]).astype(BF)
        v_ref[...] = (_nn(z_ref[:, Z_V:Z_V + FOX_W].astype(BF), sp) + ov_ref[...]).astype(BF)

        u = _gelu(z_ref[:, Z_U:Z_U + GMLP_W])
        vg = _gelu(z_ref[:, Z_G:Z_G + GMLP_W])
        vgn = (vg * _rstd(vg) * gs_ref[...]).astype(BF)
        mixed, _ = _spatial_mix(vgn, ws_ref, bst_ref[...], tm)
        sgu = u * mixed
        y_ref[...] = (sgu * _rstd(sgu) * go_ref[...]).astype(BF)

    row = lambda i: (i, 0)
    fix2 = lambda i: (0, 0)
    return pl.pallas_call(
        body, name="mix_prep", grid=(T // tm,),
        in_specs=[pl.BlockSpec((tm, ZW), row),
                  pl.BlockSpec((1, LANES), fix2), pl.BlockSpec((1, FOX_HD), fix2), pl.BlockSpec((1, FOX_HD), fix2),
                  pl.BlockSpec((1, GMLP_W), fix2), pl.BlockSpec((GMLP_G, CHUNK, CHUNK), lambda i: (0, 0, 0)),
                  pl.BlockSpec((CHUNK, GMLP_G), fix2), pl.BlockSpec((1, GMLP_W), fix2),
                  pl.BlockSpec((FOX_W, AUG_W), fix2), pl.BlockSpec((LANES, AUG_W), fix2),
                  pl.BlockSpec((LANES, AUG_W), fix2), pl.BlockSpec((1, AUG_W), fix2), pl.BlockSpec((1, AUG_W), fix2),
                  pl.BlockSpec((1, AUG_W), fix2)],
        out_specs=[pl.BlockSpec((tm, AUG_W), row), pl.BlockSpec((tm, AUG_W), row), pl.BlockSpec((tm, AUG_W), row),
                   pl.BlockSpec((tm, GMLP_W), row)],
        out_shape=[S((T, AUG_W), BF), S((T, AUG_W), BF), S((T, AUG_W), BF), S((T, GMLP_W), BF)],
        scratch_shapes=[pltpu.VMEM((1, LANES), F32), pltpu.VMEM((tm, FOX_W), BF), pltpu.VMEM((tm, FOX_W), BF)],
        compiler_params=_cp(1))(z, bf128, g_q, g_k, g_sgu, w_s, b_st, g_go, spread, pc_q, pc_k, one_q, one_k, one_v)


def _fox_fwd(q, k, v):
    T = q.shape[0]
    tq = _tile(T, 1024)
    nq = T // tq

    def body(q_ref, k_ref, v_ref, o_ref, lse_ref, m_sc, acc_sc):
        i, j = pl.program_id(0), pl.program_id(1)

        @pl.when(j == 0)
        def _():
            m_sc[...] = jnp.full(m_sc.shape, NEG, F32)
            acc_sc[...] = jnp.zeros_like(acc_sc)

        def step(masked):
            mask = _tri(tq, True) if masked else None
            for h in range(FOX_HEADS):
                hb = slice(h * HB, (h + 1) * HB)
                s = _nt(q_ref[:, hb], k_ref[:, hb])
                if masked:
                    s = jnp.where(mask, s, NEG)
                m_prev = m_sc[h]
                m_new = jnp.maximum(m_prev, jnp.broadcast_to(jnp.max(s, axis=1, keepdims=True), (tq, HB)))
                p = jnp.exp(s - jnp.tile(m_new, (1, tq // HB))).astype(BF)
                acc_sc[:, hb] = jnp.exp(m_prev - m_new) * acc_sc[:, hb] + _nn(p, v_ref[:, hb])
                m_sc[h] = m_new

        @pl.when(j < i)
        def _():
            step(False)

        @pl.when(j == i)
        def _():
            step(True)
            lse_ref[...] = jnp.zeros_like(lse_ref)
            for h in range(FOX_HEADS):
                l = acc_sc[:, h * HB + COL_A:h * HB + COL_A + 1]
                o_ref[:, h * FOX_HD:(h + 1) * FOX_HD] = acc_sc[:, h * HB:h * HB + FOX_HD] / l
                lse_ref[:, h:h + 1] = m_sc[h][:, 0:1] + jnp.log(l)

    qi = lambda i, j: (i, 0)
    kj = lambda i, j: (jnp.minimum(i, j), 0)
    return pl.pallas_call(
        body, name="fox_fwd", grid=(nq, nq),
        in_specs=[pl.BlockSpec((tq, AUG_W), qi), pl.BlockSpec((tq, AUG_W), kj), pl.BlockSpec((tq, AUG_W), kj)],
        out_specs=[pl.BlockSpec((tq, FOX_W), qi), pl.BlockSpec((tq, LANES), qi)],
        out_shape=[S((T, FOX_W), F32), S((T, LANES), F32)],
        scratch_shapes=[pltpu.VMEM((FOX_HEADS, tq, HB), F32), pltpu.VMEM((tq, AUG_W), F32)],
        compiler_params=_cp(2))(q, k, v)


def _fox_bwd(q, k, v, dob):
    T = q.shape[0]
    tq = _tile(T, 512)
    nq = T // tq
    half = AUG_W // 2
    hpg = FOX_HEADS // 2

    def body(q_ref, k_ref, v_ref, do_ref, dq_ref, dk_ref, dv_ref, dq_sc):
        j, i = pl.program_id(1), pl.program_id(2)

        @pl.when(jnp.logical_and(i == 0, j == 0))
        def _():
            dq_sc[...] = jnp.zeros_like(dq_sc)

        @pl.when(i == 0)
        def _():
            dk_ref[...] = jnp.zeros_like(dk_ref)
            dv_ref[...] = jnp.zeros_like(dv_ref)

        def step(masked):
            rows = pl.ds(pl.multiple_of(i * tq, tq), tq)
            mask = _tri(tq, True) if masked else None
            for h in range(hpg):
                hb = slice(h * HB, (h + 1) * HB)
                qh, kh, vh, doh = q_ref[:, hb], k_ref[:, hb], v_ref[:, hb], do_ref[:, hb]
                s = _nt(qh, kh)
                if masked:
                    s = jnp.where(mask, s, NEG)
                p = jnp.exp(s)
                dsb = (p * _nt(doh, vh)).astype(BF)
                dv_ref[:, hb] += _tn(p.astype(BF), doh)
                dk_ref[:, hb] += _tn(dsb, qh)
                dq_sc[rows, hb] += _nn(dsb, kh)

        @pl.when(i > j)
        def _():
            step(False)

        @pl.when(i == j)
        def _():
            step(True)
            dq_ref[...] = dq_sc[pl.ds(pl.multiple_of(j * tq, tq), tq), :]

    qi = lambda g, j, i: (jnp.maximum(i, j), g)
    kj = lambda g, j, i: (j, g)
    return pl.pallas_call(
        body, name="fox_bwd", grid=(2, nq, nq),
        in_specs=[pl.BlockSpec((tq, half), qi), pl.BlockSpec((tq, half), kj), pl.BlockSpec((tq, half), kj),
                  pl.BlockSpec((tq, half), qi)],
        out_specs=[pl.BlockSpec((tq, half), kj), pl.BlockSpec((tq, half), kj), pl.BlockSpec((tq, half), kj)],
        out_shape=[S((T, AUG_W), F32), S((T, AUG_W), F32), S((T, AUG_W), F32)],
        scratch_shapes=[pltpu.VMEM((T, half), F32)],
        compiler_params=_cp(3))(q, k, v, dob)


def _mix_out(attn, yg, g_fo, wout, x):
    T, D = x.shape
    tm = _tile(T, 512)

    def body(a_ref, y_ref, g_ref, w_ref, x_ref, o_ref):
        at = a_ref[...]
        yf = (at * _rstd(at) * g_ref[...]).astype(BF)
        o_ref[...] = x_ref[...] + _nn(yf, w_ref[:FOX_W, :]) + _nn(y_ref[...], w_ref[FOX_W:, :])

    row = lambda i: (i, 0)
    return pl.pallas_call(
        body, name="mix_out", grid=(T // tm,),
        in_specs=[pl.BlockSpec((tm, FOX_W), row), pl.BlockSpec((tm, GMLP_W), row),
                  pl.BlockSpec((1, FOX_W), lambda i: (0, 0)), pl.BlockSpec((D, D), lambda i: (0, 0)),
                  pl.BlockSpec((tm, D), row)],
        out_specs=pl.BlockSpec((tm, D), row),
        out_shape=S((T, D), F32),
        compiler_params=_cp(1))(attn, yg, g_fo, wout, x)


def _mix_out_bwd(dx, attn, yg, g_fo, wout, qf, lse):
    T, D = dx.shape
    tm = _tile(T, 256)
    n = T // tm
    spread, pc_l, pc_d = _spread_matrix(), _piece_matrix(COL_C), _piece_matrix(COL_A)

    def body(dx_ref, a_ref, y_ref, g_ref, w_ref, qf_ref, lse_ref, sp_ref, pl_ref, pd_ref,
             qb_ref, dob_ref, dyg_ref, dw_ref, dg_ref, acc_ref, dsum_ref):
        i = pl.program_id(0)
        dxb = dx_ref[...].astype(BF)
        at = a_ref[...]
        yf = (at * _rstd(at) * g_ref[...]).astype(BF)
        dy = _nt(dxb, w_ref[...])
        p_top = _tn(yf, dxb)
        p_bot = _tn(y_ref[...], dxb)

        @pl.when(i == 0)
        def _():
            acc_ref[:FOX_W, :] = p_top
            acc_ref[FOX_W:, :] = p_bot

        @pl.when(i > 0)
        def _():
            acc_ref[:FOX_W, :] += p_top
            acc_ref[FOX_W:, :] += p_bot

        @pl.when(i == n - 1)
        def _():
            dw_ref[...] = acc_ref[...].astype(BF)

        dat, dgr = _norm_bwd(dy[:, :FOX_W], at, g_ref[...])
        _acc_rows(dg_ref, i == 0, dgr)
        dyg_ref[...] = dy[:, FOX_W:]
        prod = dat * at
        dsum_ref[...] = jnp.zeros_like(dsum_ref)
        for h in range(FOX_HEADS):
            dsum_ref[:, h:h + 1] = jnp.sum(prod[:, h * FOX_HD:(h + 1) * FOX_HD], axis=1, keepdims=True)
        dob_ref[...] = (_nn(dat.astype(BF), sp_ref[...]) + _nn(_pieces(-dsum_ref[...]), pd_ref[...])).astype(BF)
        qb_ref[...] = (qf_ref[...].astype(F32) + _nn(_pieces(-lse_ref[...]), pl_ref[...])).astype(BF)

    row = lambda i: (i, 0)
    fix = lambda i: (0, 0)
    return pl.pallas_call(
        body, name="mix_out_bwd", grid=(n,),
        in_specs=[pl.BlockSpec((tm, D), row), pl.BlockSpec((tm, FOX_W), row), pl.BlockSpec((tm, GMLP_W), row),
                  pl.BlockSpec((1, FOX_W), fix), pl.BlockSpec((D, D), fix), pl.BlockSpec((tm, AUG_W), row),
                  pl.BlockSpec((tm, LANES), row), pl.BlockSpec((FOX_W, AUG_W), fix), pl.BlockSpec((LANES, AUG_W), fix),
                  pl.BlockSpec((LANES, AUG_W), fix)],
        out_specs=[pl.BlockSpec((tm, AUG_W), row), pl.BlockSpec((tm, AUG_W), row), pl.BlockSpec((tm, GMLP_W), row),
                   pl.BlockSpec((D, D), fix), pl.BlockSpec((1, FOX_W), fix)],
        out_shape=[S((T, AUG_W), BF), S((T, AUG_W), BF), S((T, GMLP_W), F32), S((D, D), BF), S((1, FOX_W), F32)],
        scratch_shapes=[pltpu.VMEM((D, D), F32), pltpu.VMEM((tm, LANES), F32)],
        compiler_params=_cp(1))(dx, attn, yg, g_fo, wout, qf, lse, spread, pc_l, pc_d)


def _mix_prep_bwd(z, dq, dk, dv, dyg, bf128, g_q, g_k, g_sgu, w_s, b_st, g_go):
    T = z.shape[0]
    tm = _tile(T, 256)
    n = T // tm

    def body(z_ref, dq_ref, dk_ref, dv_ref, dyg_ref, bf_ref, gq_ref, gk_ref, gs_ref, ws_ref,
             bst_ref, go_ref, dz_ref, dgq_ref, dgk_ref, dgs_ref, dgo_ref, dws_ref, dbst_ref, dbf_ref, carry_ref):
        i = pl.program_id(0)
        first = i == 0

        @pl.when(first)
        def _():
            carry_ref[...] = jnp.zeros_like(carry_ref)

        lane = lax.broadcasted_iota(jnp.int32, (tm, LANES), 1)
        dc = jnp.zeros((tm, LANES), F32)
        gq_rows, gk_rows = [], []
        for h in range(FOX_HEADS):
            hp = slice(h * HB, h * HB + FOX_HD)
            dqh, gqr = _norm_bwd(dq_ref[:, hp] * 0.125, z_ref[:, Z_Q + h * FOX_HD:Z_Q + (h + 1) * FOX_HD], gq_ref[...])
            dkh, gkr = _norm_bwd(dk_ref[:, hp], z_ref[:, Z_K + h * FOX_HD:Z_K + (h + 1) * FOX_HD], gk_ref[...])
            dz_ref[:, Z_Q + h * FOX_HD:Z_Q + (h + 1) * FOX_HD] = dqh.astype(BF)
            dz_ref[:, Z_K + h * FOX_HD:Z_K + (h + 1) * FOX_HD] = dkh.astype(BF)
            dz_ref[:, Z_V + h * FOX_HD:Z_V + (h + 1) * FOX_HD] = dv_ref[:, hp].astype(BF)
            dch = dq_ref[:, h * HB + COL_A:h * HB + COL_A + 1] - dk_ref[:, h * HB + COL_B:h * HB + COL_B + 1]
            dc = jnp.where(lane == h, dch, dc)
            gq_rows.append(gqr)
            gk_rows.append(gkr)
        _acc_rows(dgq_ref, first, functools.reduce(lambda a, b: a + b, gq_rows))
        _acc_rows(dgk_ref, first, functools.reduce(lambda a, b: a + b, gk_rows))

        dlogf = _hi(_tri(tm, False).astype(F32), dc) + carry_ref[...]
        carry_ref[...] = dlogf[0:1, :]
        fl = z_ref[:, Z_F:Z_F + LANES] + bf_ref[...]
        lane = lax.broadcasted_iota(jnp.int32, (tm, LANES), 1)
        df = jnp.where(lane < FOX_HEADS, dlogf * jax.nn.sigmoid(-fl), 0.0)
        dz_ref[:, Z_F:Z_F + LANES] = df.astype(BF)
        _acc_rows(dbf_ref, first, df)

        u_pre = z_ref[:, Z_U:Z_U + GMLP_W]
        vg_pre = z_ref[:, Z_G:Z_G + GMLP_W]
        u = _gelu(u_pre)
        vg = _gelu(vg_pre)
        vgn = (vg * _rstd(vg) * gs_ref[...]).astype(BF)
        bst = bst_ref[...]
        mixed, wms = _spatial_mix(vgn, ws_ref, bst, tm)
        sgu = u * mixed
        dsgu, gor = _norm_bwd(dyg_ref[...], sgu, go_ref[...])
        _acc_rows(dgo_ref, first, gor)
        du = dsgu * mixed
        dmixed = dsgu * u
        dmb = dmixed.astype(BF)
        tril = _tri(CHUNK, True)
        dvgn_rows = []
        dws = [None] * GMLP_G
        dbs = [None] * GMLP_G
        for c in range(tm // CHUNK):
            cs = slice(c * CHUNK, (c + 1) * CHUNK)
            cols = []
            for g in range(GMLP_G):
                gs = slice(g * GMLP_GD, (g + 1) * GMLP_GD)
                dmc = dmb[cs, gs]
                pw = _nt(dmc, vgn[cs, gs])
                pb = jnp.sum(dmixed[cs, gs], axis=1, keepdims=True)
                dws[g] = pw if dws[g] is None else dws[g] + pw
                dbs[g] = pb if dbs[g] is None else dbs[g] + pb
                cols.append(_tn(wms[g], dmc))
            dvgn_rows.append(jnp.concatenate(cols, axis=1))
        dvgn = jnp.concatenate(dvgn_rows, axis=0)
        dbs_t = jnp.concatenate(dbs, axis=1)
        for g in range(GMLP_G):
            dwg = jnp.where(tril, dws[g], 0.0)

            @pl.when(first)
            def _():
                dws_ref[g] = dwg

            @pl.when(jnp.logical_not(first))
            def _():
                dws_ref[g] += dwg

        @pl.when(first)
        def _():
            dbst_ref[...] = dbs_t

        @pl.when(jnp.logical_not(first))
        def _():
            dbst_ref[...] += dbs_t

        dvg, gsr = _norm_bwd(dvgn, vg, gs_ref[...])
        _acc_rows(dgs_ref, first, gsr)
        dz_ref[:, Z_U:Z_U + GMLP_W] = (du * _gelu_grad(u_pre)).astype(BF)
        dz_ref[:, Z_G:Z_G + GMLP_W] = (dvg * _gelu_grad(vg_pre)).astype(BF)

    rev = lambda i: (n - 1 - i, 0)
    fix = lambda i: (0, 0)
    fix3 = lambda i: (0, 0, 0)
    return pl.pallas_call(
        body, name="mix_prep_bwd", grid=(n,),
        in_specs=[pl.BlockSpec((tm, ZW), rev), pl.BlockSpec((tm, AUG_W), rev), pl.BlockSpec((tm, AUG_W), rev),
                  pl.BlockSpec((tm, AUG_W), rev), pl.BlockSpec((tm, GMLP_W), rev),
                  pl.BlockSpec((1, LANES), fix), pl.BlockSpec((1, FOX_HD), fix), pl.BlockSpec((1, FOX_HD), fix),
                  pl.BlockSpec((1, GMLP_W), fix), pl.BlockSpec((GMLP_G, CHUNK, CHUNK), fix3),
                  pl.BlockSpec((CHUNK, GMLP_G), fix), pl.BlockSpec((1, GMLP_W), fix)],
        out_specs=[pl.BlockSpec((tm, ZW), rev), pl.BlockSpec((1, FOX_HD), fix), pl.BlockSpec((1, FOX_HD), fix),
                   pl.BlockSpec((1, GMLP_W), fix), pl.BlockSpec((1, GMLP_W), fix),
                   pl.BlockSpec((GMLP_G, CHUNK, CHUNK), fix3), pl.BlockSpec((CHUNK, GMLP_G), fix),
                   pl.BlockSpec((1, LANES), fix)],
        out_shape=[S((T, ZW), BF), S((1, FOX_HD), F32), S((1, FOX_HD), F32), S((1, GMLP_W), F32), S((1, GMLP_W), F32),
                   S((GMLP_G, CHUNK, CHUNK), F32), S((CHUNK, GMLP_G), F32), S((1, LANES), F32)],
        scratch_shapes=[pltpu.VMEM((1, LANES), F32)],
        compiler_params=_cp(1))(z, dq, dk, dv, dyg, bf128, g_q, g_k, g_sgu, w_s, b_st, g_go)


def _mix_proj_bwd(dz, wz, x, g, dy):
    T, D = x.shape
    tm = _tile(T, 512)

    def body(dz_ref, w_ref, x_ref, g_ref, dy_ref, dx_ref, dxb_ref, dg_ref):
        dh = _nn(dz_ref[...], w_ref[...])
        dx, dgr = _norm_bwd(dh, x_ref[...], g_ref[...])
        dx = dx + dy_ref[...]
        dx_ref[...] = dx
        dxb_ref[...] = dx.astype(BF)
        _acc_rows(dg_ref, pl.program_id(0) == 0, dgr)

    row = lambda i: (i, 0)
    fix = lambda i: (0, 0)
    return pl.pallas_call(
        body, name="mix_proj_bwd", grid=(T // tm,),
        in_specs=[pl.BlockSpec((tm, ZW), row), pl.BlockSpec((ZW, D), fix), pl.BlockSpec((tm, D), row),
                  pl.BlockSpec((1, D), fix), pl.BlockSpec((tm, D), row)],
        out_specs=[pl.BlockSpec((tm, D), row), pl.BlockSpec((tm, D), row), pl.BlockSpec((1, D), fix)],
        out_shape=[S((T, D), F32), S((T, D), BF), S((1, D), F32)],
        compiler_params=_cp(1))(dz, wz, x, g, dy)


def _ca_kv(mem, g_mem, wckv, g_ck):
    M, D = mem.shape

    def body(m_ref, g_ref, w_ref, gk_ref, mn_ref, kr_ref, kn_ref, v_ref):
        mf = m_ref[...]
        mn = (mf * _rstd(mf) * g_ref[...]).astype(BF)
        mn_ref[...] = mn
        for h in range(CA_HEADS):
            kr = _nn(mn, w_ref[h])
            kr_ref[h] = kr
            kn_ref[h] = (kr * _rstd(kr) * gk_ref[...]).astype(BF)
            v_ref[h] = _nn(mn, w_ref[CA_HEADS + h]).astype(BF)

    hd = (CA_HEADS, M, CA_HD)
    return pl.pallas_call(
        body, name="ca_kv", out_shape=[S((M, D), BF), S(hd, F32), S(hd, BF), S(hd, BF)],
        compiler_params=pltpu.CompilerParams(vmem_limit_bytes=VMEM_LIMIT))(mem, g_mem, wckv, g_ck)


def _ca_tile_fwd(xt, gca, wcq, gcq, kn_ref, v_ref):
    hb = (xt * _rstd(xt) * gca).astype(BF)
    qc = _nn(hb, wcq)
    qr, qn, ps = [], [], []
    for h in range(CA_HEADS):
        qh = qc[:, h * CA_HD:(h + 1) * CA_HD]
        qnh = (qh * _rstd(qh) * gcq * 0.0625).astype(BF)
        s = _nt(qnh, kn_ref[h])
        e = jnp.exp(s - jnp.max(s, axis=1, keepdims=True))
        ps.append(e / jnp.sum(e, axis=1, keepdims=True))
        qr.append(qh)
        qn.append(qnh)
    return hb, qr, qn, ps


def _ca_fwd(x, g_ca, wcq, g_cq, kn, vv, wco):
    T, D = x.shape
    M = kn.shape[1]
    tm = _tile(T, 256)

    def body(x_ref, gca_ref, wcq_ref, gcq_ref, kn_ref, v_ref, wco_ref, o_ref, ob_sc):
        xt = x_ref[...]
        _, _, _, ps = _ca_tile_fwd(xt, gca_ref[...], wcq_ref[...], gcq_ref[...], kn_ref, v_ref)
        for h in range(CA_HEADS):
            ob_sc[:, h * CA_HD:(h + 1) * CA_HD] = _nn(ps[h].astype(BF), v_ref[h]).astype(BF)
        o_ref[...] = xt + _nn(ob_sc[...], wco_ref[...])

    row = lambda i: (i, 0)
    fix = lambda i: (0, 0)
    fix3 = lambda i: (0, 0, 0)
    return pl.pallas_call(
        body, name="ca_fwd", grid=(T // tm,),
        in_specs=[pl.BlockSpec((tm, D), row), pl.BlockSpec((1, D), fix), pl.BlockSpec((D, D), fix),
                  pl.BlockSpec((1, CA_HD), fix), pl.BlockSpec((CA_HEADS, M, CA_HD), fix3),
                  pl.BlockSpec((CA_HEADS, M, CA_HD), fix3), pl.BlockSpec((D, D), fix)],
        out_specs=pl.BlockSpec((tm, D), row), out_shape=S((T, D), F32),
        scratch_shapes=[pltpu.VMEM((tm, D), BF)],
        compiler_params=_cp(1))(x, g_ca, wcq, g_cq, kn, vv, wco)


def _ca_bwd(x, dy, g_ca, wcq, g_cq, kn, vv, wco):
    T, D = x.shape
    M = kn.shape[1]
    tm = _tile(T, 256)
    n = T // tm

    def body(x_ref, dy_ref, gca_ref, wcq_ref, gcq_ref, kn_ref, v_ref, wco_ref,
             dx_ref, dwq_ref, dwo_ref, dkn_ref, dv_ref, dgcq_ref, dgca_ref, aq_sc, ao_sc, ob_sc, dq_sc):
        i = pl.program_id(0)
        first = i == 0
        xt = x_ref[...]
        dyt = dy_ref[...]
        dyb = dyt.astype(BF)
        hb, qr, qn, ps = _ca_tile_fwd(xt, gca_ref[...], wcq_ref[...], gcq_ref[...], kn_ref, v_ref)
        do = _nt(dyb, wco_ref[...])
        gcq_rows = None
        for h in range(CA_HEADS):
            hs = slice(h * CA_HD, (h + 1) * CA_HD)
            p = ps[h]
            pb = p.astype(BF)
            ob_sc[:, hs] = _nn(pb, v_ref[h]).astype(BF)
            doh = do[:, hs].astype(BF)
            dp = _nt(doh, v_ref[h])
            ds = (p * (dp - jnp.sum(dp * p, axis=1, keepdims=True))).astype(BF)
            dvh = _tn(pb, doh)
            dkh = _tn(ds, qn[h])

            @pl.when(first)
            def _():
                dv_ref[h] = dvh
                dkn_ref[h] = dkh

            @pl.when(jnp.logical_not(first))
            def _():
                dv_ref[h] += dvh
                dkn_ref[h] += dkh

            dqn = _nn(ds, kn_ref[h]) * 0.0625
            dqh, gr = _norm_bwd(dqn, qr[h], gcq_ref[...])
            gcq_rows = gr if gcq_rows is None else gcq_rows + gr
            dq_sc[:, hs] = dqh.astype(BF)
        _acc_rows(dgcq_ref, first, gcq_rows)
        dqb = dq_sc[...]
        p_o = _tn(ob_sc[...], dyb)
        p_q = _tn(hb, dqb)

        @pl.when(first)
        def _():
            ao_sc[...] = p_o
            aq_sc[...] = p_q

        @pl.when(jnp.logical_not(first))
        def _():
            ao_sc[...] += p_o
            aq_sc[...] += p_q

        @pl.when(i == n - 1)
        def _():
            dwo_ref[...] = ao_sc[...].astype(BF)
            dwq_ref[...] = aq_sc[...].astype(BF)

        dh = _nt(dqb, wcq_ref[...])
        dx, gar = _norm_bwd(dh, xt, gca_ref[...])
        dx_ref[...] = dx + dyt
        _acc_rows(dgca_ref, first, gar)

    row = lambda i: (i, 0)
    fix = lambda i: (0, 0)
    fix3 = lambda i: (0, 0, 0)
    hd = (CA_HEADS, M, CA_HD)
    return pl.pallas_call(
        body, name="ca_bwd", grid=(n,),
        in_specs=[pl.BlockSpec((tm, D), row), pl.BlockSpec((tm, D), row), pl.BlockSpec((1, D), fix),
                  pl.BlockSpec((D, D), fix), pl.BlockSpec((1, CA_HD), fix), pl.BlockSpec(hd, fix3),
                  pl.BlockSpec(hd, fix3), pl.BlockSpec((D, D), fix)],
        out_specs=[pl.BlockSpec((tm, D), row), pl.BlockSpec((D, D), fix), pl.BlockSpec((D, D), fix),
                   pl.BlockSpec(hd, fix3), pl.BlockSpec(hd, fix3), pl.BlockSpec((1, CA_HD), fix),
                   pl.BlockSpec((1, D), fix)],
        out_shape=[S((T, D), F32), S((D, D), BF), S((D, D), BF), S(hd, F32), S(hd, F32), S((1, CA_HD), F32),
                   S((1, D), F32)],
        scratch_shapes=[pltpu.VMEM((D, D), F32), pltpu.VMEM((D, D), F32), pltpu.VMEM((tm, D), BF),
                        pltpu.VMEM((tm, D), BF)],
        compiler_params=_cp(1))(x, dy, g_ca, wcq, g_cq, kn, vv, wco)


def _ca_kv_bwd(mem, g_mem, mn, kraw, dkn, dvv, wckv, g_ck):
    M, D = mem.shape

    def body(m_ref, g_ref, mn_ref, kr_ref, dkn_ref, dv_ref, w_ref, gk_ref, dw_ref, dgk_ref, dgm_ref):
        mn = mn_ref[...]
        dmn = jnp.zeros((M, D), F32)
        gk_rows = None
        for h in range(CA_HEADS):
            dkr, gr = _norm_bwd(dkn_ref[h], kr_ref[h], gk_ref[...])
            gk_rows = gr if gk_rows is None else gk_rows + gr
            dkb = dkr.astype(BF)
            dvb = dv_ref[h].astype(BF)
            dw_ref[h] = _tn(mn, dkb).astype(BF)
            dw_ref[CA_HEADS + h] = _tn(mn, dvb).astype(BF)
            dmn = dmn + _nt(dkb, w_ref[h]) + _nt(dvb, w_ref[CA_HEADS + h])
        dgk_ref[...] = jnp.sum(gk_rows, axis=0, keepdims=True)
        mf = m_ref[...]
        dgm_ref[...] = jnp.sum(dmn * (mf * _rstd(mf)), axis=0, keepdims=True)

    return pl.pallas_call(
        body, name="ca_kv_bwd",
        out_shape=[S((2 * CA_HEADS, D, CA_HD), BF), S((1, CA_HD), F32), S((1, D), F32)],
        compiler_params=pltpu.CompilerParams(vmem_limit_bytes=VMEM_LIMIT))(mem, g_mem, mn, kraw, dkn, dvv, wckv, g_ck)


def _after(g, token):
    return g if token is None else g + token[0:1, 0:1]


def _local_step(x, mem, target, small, weights, emit):
    T, D = x.shape
    p = small
    bf128 = jnp.pad(p["b_f"], ((0, 0), (0, LANES - FOX_HEADS)))
    b_st = p["b_s"].T

    wup1 = weights("ffn1_up", x)["wup1"]
    a1, h1 = _ffn_up("ffn1_up", x, p["g_ffn1"], wup1)
    wdn1 = weights("ffn1_dn", h1)["wdn1"]
    x1 = _ffn_down("ffn1_down", a1, wdn1, x)
    wm = weights("mix", x1)
    z, h2 = _mix_proj(x1, p["g_mix"], wm["wz"])
    qf, ka, va, yg = _mix_prep(z, bf128, p["g_q"], p["g_k"], p["g_sgu"], p["w_s"], b_st, p["g_gmlp_o"])
    attn, lse = _fox_fwd(qf, ka, va)
    x2 = _mix_out(attn, yg, p["g_fox_o"], wm["wout"], x1)
    wc = weights("ca", x2)
    mn, kraw, ckn, cvv = _ca_kv(mem, p["g_mem"], wc["wckv"], p["g_ck"])
    x3 = _ca_fwd(x2, p["g_ca"], wc["wcq"], p["g_cq"], ckn, cvv, wc["wco"])
    w2 = weights("ffn2", x3)
    a2, h4 = _ffn_up("ffn2_up", x3, p["g_ffn2"], w2["wup2"])
    dy4, dy4b, sq = _ffn_down_loss("ffn2_down", a2, w2["wdn2"], x3, target)

    gs = {}
    dgu2 = _ffn_bwd_act("ffn2_bwd_act", dy4b, h4, w2["wup2"], w2["wdn2"])
    tok = emit("ffn2", {"wup2": _ffn_dwup("ffn2", h4, dgu2), "wdn2": _ffn_dwdn("ffn2", a2, dy4b)})
    dx3, gs["g_ffn2"] = _ffn_dx("ffn2_dx", dgu2, w2["wup2"], x3, _after(p["g_ffn2"], tok), dy4)

    dx2, dwcq, dwco, dckn, dcvv, gs["g_cq"], gs["g_ca"] = _ca_bwd(
        x2, dx3, p["g_ca"], wc["wcq"], p["g_cq"], ckn, cvv, wc["wco"])
    dwckv, gs["g_ck"], gs["g_mem"] = _ca_kv_bwd(mem, p["g_mem"], mn, kraw, dckn, dcvv, wc["wckv"], p["g_ck"])

    qb, dob, dyg, dwout, gs["g_fox_o"] = _mix_out_bwd(dx2, attn, yg, p["g_fox_o"], wm["wout"], qf, lse)
    dq, dk, dv = _fox_bwd(qb, ka, va, dob)
    dz, gs["g_q"], gs["g_k"], gs["g_sgu"], gs["g_gmlp_o"], gs["w_s"], dbst, dbf = _mix_prep_bwd(
        z, dq, dk, dv, dyg, bf128, p["g_q"], p["g_k"], p["g_sgu"], p["w_s"], b_st, p["g_gmlp_o"])
    gs["b_s"] = dbst.T
    gs["b_f"] = dbf[:, :FOX_HEADS]
    tok_ws = emit("w_s", {"w_s": gs["w_s"]})
    tk = _tile(T, 1024)
    zb = ZW // 3
    dwz = _tn_matmul(
        "mix_dwz", dz, pl.BlockSpec((tk, zb), lambda j, k: (k, j)), h2, pl.BlockSpec((tk, D), lambda j, k: (k, 0)),
        S((ZW, D), F32), pl.BlockSpec((zb, D), lambda j, k: (j, 0)), (3, T // tk), (zb, D))
    tok = emit("mid", {"wcq": dwcq, "wco": dwco, "wckv": dwckv, "wout": dwout, "wz": dwz})
    dx1, dx1b, gs["g_mix"] = _mix_proj_bwd(dz, wm["wz"], x1, _after(_after(p["g_mix"], tok), tok_ws), dx2)

    dgu1 = _ffn_bwd_act("ffn1_bwd_act", dx1b, h1, wup1, wdn1)
    tok = emit("ffn1_dn", {"wdn1": _ffn_dwdn("ffn1", a1, dx1b)})
    tok = emit("ffn1_up", {"wup1": _ffn_dwup("ffn1", h1, dgu1, after=tok)})
    dx0, gs["g_ffn1"] = _ffn_dx("ffn1_dx", dgu1, wup1, x, _after(p["g_ffn1"], tok), dx1)
    return sq, dx0, gs


MESH = pl.DeviceIdType.MESH
HBM_SPEC = pl.BlockSpec(memory_space=pltpu.HBM)
N_PEER = N_DEV - 1


def _place():
    return lax.axis_index("x"), lax.axis_index("y"), lax.axis_index("c")


def _slot(px, py, pc):
    return 4 * px + 2 * py + pc


SEM_SPEC = pl.BlockSpec(memory_space=pltpu.SEMAPHORE)
ANY_SPEC = pl.BlockSpec(memory_space=pl.ANY)
DATAFLOW = pltpu.SideEffectType.DATAFLOW_SIDE_EFFECTING


def _hbm(a):
    return pltpu.with_memory_space_constraint(a, pltpu.HBM)


def _peer(x, y, c, r):
    return (1 - x if r & 4 else x, 1 - y if r & 2 else y, 1 - c if r & 1 else c)


def _place_own(srcs, whole):
    my = _slot(*_place())
    lands = []
    for s in srcs:
        blk = s[None] if whole else lax.dynamic_slice_in_dim(s, my, 1, 0)
        shape = (N_DEV,) + s.shape if whole else s.shape
        lands.append(lax.dynamic_update_slice_in_dim(lax.empty(shape, s.dtype), blk, my, 0))
    return lands


def _copy_start(name, srcs, lands, whole):
    n = len(srcs)

    def body(*refs):
        src, land = refs[:n], refs[n:2 * n]
        send, recv = refs[2 * n:3 * n], refs[3 * n:4 * n]
        token = refs[6 * n]
        x, y, c = _place()
        my = _slot(x, y, c)
        for a in range(n):
            for r in range(1, N_DEV):
                p = _peer(x, y, c, r)
                pltpu.make_async_remote_copy(
                    src_ref=src[a] if whole else src[a].at[_slot(*p)], dst_ref=land[a].at[my],
                    send_sem=send[a].at[r - 1], recv_sem=recv[a].at[r - 1], device_id=p, device_id_type=MESH).start()
        token[...] = jnp.zeros_like(token)

    out = pl.pallas_call(
        body, name=name,
        out_shape=([pltpu.SemaphoreType.DMA((N_PEER,))] * (2 * n)
                   + [pltpu.HBM(s.shape, s.dtype) for s in srcs] + [pltpu.HBM(s.shape, s.dtype) for s in lands]
                   + [S((8, LANES), F32)]),
        in_specs=[HBM_SPEC] * (2 * n),
        out_specs=[SEM_SPEC] * (2 * n) + [HBM_SPEC] * (2 * n) + [pl.BlockSpec(memory_space=pltpu.VMEM)],
        input_output_aliases={i: 2 * n + i for i in range(2 * n)},
        compiler_params=pltpu.CompilerParams(has_side_effects=DATAFLOW),
    )(*[_hbm(s) for s in srcs], *[_hbm(s) for s in lands])
    return out[:n], out[n:2 * n], out[2 * n:3 * n], out[3 * n:4 * n], out[4 * n]


def _copy_wait(name, srcs, lands, send, recv, after, whole):
    n = len(srcs)

    def body(*refs):
        src, land = refs[:n], refs[n:2 * n]
        snd, rcv = refs[2 * n:3 * n], refs[3 * n:4 * n]
        x, y, c = _place()
        for a in range(n):
            for r in range(1, N_DEV):
                p = _peer(x, y, c, r)
                ps = _slot(*p)
                cp = pltpu.make_async_remote_copy(
                    src_ref=src[a] if whole else src[a].at[ps], dst_ref=land[a].at[ps],
                    send_sem=snd[a].at[r - 1], recv_sem=rcv[a].at[r - 1], device_id=p, device_id_type=MESH)
                cp.wait_send()
                cp.wait_recv()

    out = pl.pallas_call(
        body, name=name,
        out_shape=[pltpu.HBM(s.shape, s.dtype) for s in srcs] + [pltpu.HBM(s.shape, s.dtype) for s in lands],
        in_specs=[HBM_SPEC] * (2 * n) + [SEM_SPEC] * (2 * n) + [ANY_SPEC],
        out_specs=[HBM_SPEC] * (2 * n),
        input_output_aliases={i: i for i in range(2 * n)},
        compiler_params=pltpu.CompilerParams(has_side_effects=DATAFLOW),
    )(*srcs, *lands, *send, *recv, after)
    return out[n:]


def _adamw(w, g, m, v):
    m2 = ADAM_B1 * m + (1.0 - ADAM_B1) * g
    v2 = ADAM_B2 * v + (1.0 - ADAM_B2) * (g * g)
    m_hat = m2 / (1.0 - ADAM_B1 ** ADAM_STEP)
    v_hat = v2 / (1.0 - ADAM_B2 ** ADAM_STEP)
    delta = -ADAM_LR * (m_hat / (jnp.sqrt(v_hat) + ADAM_EPS) + ADAM_WD * w)
    return delta, m2, v2


def _adamw_big(name, slots, w, m, v):
    R, C = w.shape
    tr = next((t for t in (256, 352) if R % t == 0), R)

    def body(s_ref, w_ref, m_ref, v_ref, g_ref, d_ref, m2_ref, v2_ref):
        g = s_ref[0].astype(F32)
        for k in range(1, N_DEV):
            g = g + s_ref[k].astype(F32)
        d, m2, v2 = _adamw(w_ref[...], g, m_ref[...], v_ref[...])
        g_ref[...] = g
        d_ref[...] = d
        m2_ref[...] = m2
        v2_ref[...] = v2

    row = pl.BlockSpec((tr, C), lambda i: (i, 0))
    return pl.pallas_call(
        body, name=name, grid=(R // tr,),
        in_specs=[pl.BlockSpec((N_DEV, tr, C), lambda i: (0, i, 0)), row, row, row],
        out_specs=[row] * 4, out_shape=[S((R, C), F32)] * 4,
        compiler_params=_cp(1))(slots, w, m, v)


TINY_ROWS = (("b_s", 8), ("g_ffn1", 8), ("g_mix", 8), ("g_ca", 8), ("g_mem", 8), ("g_ffn2", 8), ("g_sgu", 4),
             ("g_fox_o", 4), ("g_gmlp_o", 4), ("g_cq", 2), ("g_ck", 2), ("g_q", 1), ("g_k", 1), ("b_f", 1),
             ("loss", 1))
TINY_P = 72


def _pack_tiny(d):
    rows = []
    for name, r in TINY_ROWS:
        flat = d[name].reshape(-1) if name in d else jnp.zeros((r * LANES,), F32)
        rows.append(jnp.pad(flat, (0, r * LANES - flat.shape[0])).reshape(r, LANES))
    used = sum(r for _, r in TINY_ROWS)
    rows.append(jnp.zeros((TINY_P - used, LANES), F32))
    return jnp.concatenate(rows, axis=0)


def _unpack_tiny(packed, shapes):
    out, at = {}, 0
    for name, r in TINY_ROWS:
        shape = shapes[name]
        size = 1
        for s in shape:
            size *= s
        out[name] = packed[at:at + r].reshape(-1)[:size].reshape(shape)
        at += r
    return out


WEIGHTS =('g_ffn1', 'w_ffn1_in', 'w_ffn1_out', 'g_mix', 'w_in', 'b_f', 'g_q', 'g_k', 'g_sgu', 'w_s', 'b_s',
           'g_fox_o', 'g_gmlp_o', 'w_out', 'g_ca', 'g_mem', 'w_cq', 'w_ckv', 'g_cq', 'g_ck', 'w_co', 'g_ffn2',
           'w_ffn2_in', 'w_ffn2_out')
BIG = ('w_ffn1_in', 'w_ffn1_out', 'w_in', 'w_out', 'w_cq', 'w_ckv', 'w_co', 'w_ffn2_in', 'w_ffn2_out')
TRANSPOSED = ('w_ffn1_in', 'w_in', 'w_ffn2_in')
GATHER_GROUPS = {"ffn1_up": ("w_ffn1_in",), "ffn1_dn": ("w_ffn1_out",), "mix": ("w_in", "w_out"),
                 "ca": ("w_cq", "w_ckv", "w_co"), "ffn2": ("w_ffn2_in", "w_ffn2_out")}
QKV_W = 3 * FOX_W
UV_OFF = QKV_W + FOX_HEADS


def kernel(x, mem, g_ffn1, w_ffn1_in, w_ffn1_out, g_mix, w_in, b_f, g_q, g_k, g_sgu, w_s, b_s, g_fox_o, g_gmlp_o, w_out, g_ca, g_mem, w_cq, w_ckv, g_cq, g_ck, w_co, g_ffn2, w_ffn2_in, w_ffn2_out, loss_target, m_g_ffn1, m_w_ffn1_in, m_w_ffn1_out, m_g_mix, m_w_in, m_b_f, m_g_q, m_g_k, m_g_sgu, m_w_s, m_b_s, m_g_fox_o, m_g_gmlp_o, m_w_out, m_g_ca, m_g_mem, m_w_cq, m_w_ckv, m_g_cq, m_g_ck, m_w_co, m_g_ffn2, m_w_ffn2_in, m_w_ffn2_out, v_g_ffn1, v_w_ffn1_in, v_w_ffn1_out, v_g_mix, v_w_in, v_b_f, v_g_q, v_g_k, v_g_sgu, v_w_s, v_b_s, v_g_fox_o, v_g_gmlp_o, v_w_out, v_g_ca, v_g_mem, v_w_cq, v_w_ckv, v_g_cq, v_g_ck, v_w_co, v_g_ffn2, v_w_ffn2_in, v_w_ffn2_out):
    args = dict(locals())
    w = {n: args[n] for n in WEIGHTS}
    mo = {n: args["m_" + n] for n in WEIGHTS}
    vo = {n: args["v_" + n] for n in WEIGHTS}
    D = D_MODEL

    def local(n, a):
        return a[0].T if n in TRANSPOSED else a[0]

    shards = [local(n, w[n]).astype(BF) for n in BIG]
    fb = shards[0].shape[0]
    g_snd, g_rcv, g_src, g_land, g_token = _copy_start("gather_start", shards, _place_own(shards, True), True)
    handles = {n: (g_src[i], g_land[i], g_snd[i], g_rcv[i]) for i, n in enumerate(BIG)}

    tiny_names = [n for n, _ in TINY_ROWS if n != "loss"]
    tiny_wmv =[_pack_tiny({n: a[n] for n in tiny_names}) + g_token[0:1, 0:1] for a in (w, mo, vo)]
    first_after = tiny_wmv[0][0:8] + tiny_wmv[1][0:8] + tiny_wmv[2][0:8]

    def weights(group, after):
        names = GATHER_GROUPS[group]
        hs = [handles[n] for n in names]
        got = _copy_wait("gather_wait_" + group, [h[0] for h in hs], [h[1] for h in hs], [h[2] for h in hs],
                         [h[3] for h in hs], first_after if group == "ffn1_up" else after, True)
        got = dict(zip(names, got))
        if group == "ffn1_up":
            return {"wup1": got["w_ffn1_in"].reshape(2, N_FFN_BLK, fb, D)}
        if group == "ffn1_dn":
            return {"wdn1": got["w_ffn1_out"].reshape(N_FFN_BLK, fb, D)}
        if group == "mix":
            full = got["w_in"].reshape(-1, D)
            wz = jnp.concatenate([full[:QKV_W], full[UV_OFF:], full[QKV_W:UV_OFF],
                                  jnp.zeros((LANES - FOX_HEADS, D), BF)], axis=0)
            return {"wz": wz, "wout": got["w_out"].reshape(D, D)}
        if group == "ca":
            return {"wcq": got["w_cq"].reshape(D, D), "wco": got["w_co"].reshape(D, D), "wckv": got["w_ckv"]}
        return {"wup2": got["w_ffn2_in"].reshape(2, N_FFN_BLK, fb, D),
                "wdn2": got["w_ffn2_out"].reshape(N_FFN_BLK, fb, D)}

    flying = {}

    def emit(group, g):
        if group == "w_s":
            part = [g["w_s"].reshape(-1, LANES)]
            *copies, token = _copy_start("w_s_start", part, _place_own(part, True), True)
            flying[group] = copies
            return token
        if group == "ffn2":
            parts = {"w_ffn2_in": g["wup2"], "w_ffn2_out": g["wdn2"].reshape(N_DEV, -1, D)}
        elif group == "ffn1_dn":
            parts = {"w_ffn1_out": g["wdn1"].reshape(N_DEV, -1, D)}
        elif group == "ffn1_up":
            parts = {"w_ffn1_in": g["wup1"]}
        else:
            gz = g["wz"]
            g_in = jnp.concatenate([gz[:QKV_W], gz[Z_F:Z_F + FOX_HEADS], gz[QKV_W:Z_F]], axis=0)
            parts = {"w_in": g_in.reshape(N_DEV, -1, D).astype(BF),
                     "w_out": g["wout"].reshape(N_DEV, -1, D), "w_cq": g["wcq"].reshape(N_DEV, -1, D),
                     "w_co": g["wco"].reshape(N_DEV, -1, D), "w_ckv": g["wckv"]}
        names = list(parts)
        srcs = [parts[n] for n in names]
        *copies, token = _copy_start("exchange_start_" + group, srcs, _place_own(srcs, False), False)
        flying[group] = (names, copies)
        return token

    small = {n: (w[n][0] if n == "b_s" else w[n]) for n in tiny_names}
    small["w_s"] = w["w_s"][0]

    sq, dx0, gs = _local_step(x[0], mem[0], loss_target[0], small, weights, emit)

    sm_parts = [_pack_tiny({**gs, "loss": sq[0:1]})]
    sm_snd, sm_rcv, sm_src, sm_land, sm_token = _copy_start("tiny_start", sm_parts, _place_own(sm_parts, True), True)

    grad, delta, new_m, new_v = {}, {}, {}, {}

    def update(group, after):
        names, (snd, rcv, srcs, lands) = flying[group]
        slots = _copy_wait("exchange_wait_" + group, srcs, lands, snd, rcv, after, False)
        for n, sl in zip(names, slots):
            g, d, m2, v2 = _adamw_big("adamw_" + n, sl, local(n, w[n]), local(n, mo[n]), local(n, vo[n]))
            grad[n], delta[n], new_m[n], new_v[n] = (
                (t.T if n in TRANSPOSED else t).reshape(w[n].shape) for t in (g, d, m2, v2))
        return d

    last = update("ffn2", sm_token)
    last = update("mid", last)
    last = update("ffn1_dn", last)
    last = update("ffn1_up", last)
    ws_snd, ws_rcv, ws_src, ws_land = flying["w_s"]
    ws_all, = _copy_wait("w_s_wait", ws_src, ws_land, ws_snd, ws_rcv, last, True)
    tiny_all, = _copy_wait("tiny_wait", sm_src, sm_land, sm_snd, sm_rcv, ws_all, True)
    ws_shape = w["w_s"].shape
    for store, t in zip((grad, delta, new_m, new_v), _adamw_big(
            "adamw_w_s", ws_all, *[a["w_s"].reshape(-1, LANES) for a in (w, mo, vo)])):
        store["w_s"] = t.reshape(ws_shape)
    shapes = {n: w[n].shape for n in tiny_names}
    shapes["loss"] = (1, LANES)
    for store, t in zip((grad, delta, new_m, new_v), _adamw_big(
            "adamw_tiny", tiny_all, *tiny_wmv)):
        store.update(_unpack_tiny(t, shapes))
    loss = grad["loss"][0, 0] * (0.5 / D)

    return (loss, dx0[None], *[grad[n] for n in WEIGHTS], *[delta[n] for n in WEIGHTS],
            *[new_m[n] for n in WEIGHTS], *[new_v[n] for n in WEIGHTS])
```

```python
import functools

import jax
import jax.numpy as jnp
from jax import lax
from jax.experimental import pallas as pl
from jax.experimental.pallas import tpu as pltpu

F32 = jnp.float32
BF = jnp.bfloat16
S = jax.ShapeDtypeStruct

N_DEV = 8
D_MODEL = 1024
FOX_HEADS, FOX_HD = 8, 64
FOX_W = 512
GMLP_G, GMLP_GD = 8, 64
GMLP_W = 512
CHUNK = 128
CA_HEADS, CA_HD = 4, 256
N_FFN_BLK = 4
ZW = 2688
Z_Q, Z_K, Z_V, Z_U, Z_G, Z_F = 0, 512, 1024, 1536, 2048, 2560
EPS = 1e-6
NEG = -1e30
LANES = 128

ADAM_LR, ADAM_B1, ADAM_B2, ADAM_EPS, ADAM_WD, ADAM_STEP = 0.001, 0.9, 0.999, 1e-08, 0.01, 10

VMEM_LIMIT = 52 * 2 ** 20


def _cp(n_axes):
    return pltpu.CompilerParams(dimension_semantics=("arbitrary",) * n_axes, vmem_limit_bytes=VMEM_LIMIT)


def _nn(a, b):
    return jnp.dot(a, b, preferred_element_type=F32)


def _nt(a, b):
    return lax.dot_general(a, b, (((1,), (1,)), ((), ())), preferred_element_type=F32)


def _tn(a, b):
    return lax.dot_general(a, b, (((0,), (0,)), ((), ())), preferred_element_type=F32)


def _hi(a, b):
    return jnp.dot(a, b, precision=lax.Precision.HIGHEST, preferred_element_type=F32)


def _rstd(x):
    return lax.rsqrt(jnp.mean(x * x, axis=-1, keepdims=True) + EPS)


def _norm_bwd(dy, x, g):
    r = _rstd(x)
    xh = x * r
    dxh = dy * g
    dx = r * (dxh - xh * jnp.mean(dxh * xh, axis=-1, keepdims=True))
    return dx, dy * xh


def _acc_rows(ref, first, val):
    srow = jnp.sum(val, axis=0, keepdims=True)

    @pl.when(first)
    def _():
        ref[...] = srow

    @pl.when(jnp.logical_not(first))
    def _():
        ref[...] += srow


def _gelu(x):
    c = 0.7978845608028654
    return 0.5 * x * (1.0 + jnp.tanh(c * (x + 0.044715 * x * x * x)))


def _gelu_grad(x):
    c = 0.7978845608028654
    t = jnp.tanh(c * (x + 0.044715 * x * x * x))
    return 0.5 * (1.0 + t) + 0.5 * x * (1.0 - t * t) * c * (1.0 + 3 * 0.044715 * x * x)


def _tile(n, pref):
    return pref if n % pref == 0 else n


def _ffn_up(name, x, g, wup):
    T, D = x.shape
    FB = wup.shape[-2]
    tm = _tile(T, 1024)

    def body(x_ref, g_ref, w_ref, a_ref, h_ref):
        @pl.when(pl.program_id(1) == 0)
        def _():
            xf = x_ref[...]
            h_ref[...] = (xf * _rstd(xf) * g_ref[...]).astype(BF)

        hb = h_ref[...]
        gg = _nt(hb, w_ref[0])
        uu = _nt(hb, w_ref[1])
        a_ref[...] = (gg * jax.nn.sigmoid(gg) * uu).astype(BF)

    return pl.pallas_call(
        body, name=name, grid=(T // tm, N_FFN_BLK),
        in_specs=[pl.BlockSpec((tm, D), lambda i, j: (i, 0)),
                  pl.BlockSpec((1, D), lambda i, j: (0, 0)),
                  pl.BlockSpec((2, None, FB, D), lambda i, j: (0, j, 0, 0))],
        out_specs=[pl.BlockSpec((None, tm, FB), lambda i, j: (j, i, 0)),
                   pl.BlockSpec((tm, D), lambda i, j: (i, 0))],
        out_shape=[S((N_FFN_BLK, T, FB), BF), S((T, D), BF)],
        compiler_params=_cp(2))(x, g, wup)


def _ffn_down(name, a, wdn, x):
    _, T, FB = a.shape
    D = x.shape[1]
    tm = _tile(T, 512)

    def body(a_ref, w_ref, x_ref, o_ref):
        p = _nn(a_ref[0], w_ref[0])
        for j in range(1, N_FFN_BLK):
            p = p + _nn(a_ref[j], w_ref[j])
        o_ref[...] = x_ref[...] + 0.5 * p

    return pl.pallas_call(
        body, name=name, grid=(T // tm,),
        in_specs=[pl.BlockSpec((N_FFN_BLK, tm, FB), lambda i: (0, i, 0)),
                  pl.BlockSpec((N_FFN_BLK, FB, D), lambda i: (0, 0, 0)),
                  pl.BlockSpec((tm, D), lambda i: (i, 0))],
        out_specs=pl.BlockSpec((tm, D), lambda i: (i, 0)),
        out_shape=S((T, D), F32),
        compiler_params=_cp(1))(a, wdn, x)


def _ffn_down_loss(name, a, wdn, x, target):
    _, T, FB = a.shape
    D = x.shape[1]
    tm = _tile(T, 512)

    def body(a_ref, w_ref, x_ref, t_ref, d_ref, db_ref, loss_ref):
        i = pl.program_id(0)
        p = _nn(a_ref[0], w_ref[0])
        for j in range(1, N_FFN_BLK):
            p = p + _nn(a_ref[j], w_ref[j])
        diff = (x_ref[...] + 0.5 * p) - t_ref[...]
        dy = diff * (1.0 / D)
        d_ref[...] = dy
        db_ref[...] = dy.astype(BF)
        sq = jnp.zeros((8, LANES), F32) + jnp.sum(diff * diff)

        @pl.when(i == 0)
        def _():
            loss_ref[...] = sq

        @pl.when(i > 0)
        def _():
            loss_ref[...] += sq

    row = pl.BlockSpec((tm, D), lambda i: (i, 0))
    return pl.pallas_call(
        body, name=name, grid=(T // tm,),
        in_specs=[pl.BlockSpec((N_FFN_BLK, tm, FB), lambda i: (0, i, 0)),
                  pl.BlockSpec((N_FFN_BLK, FB, D), lambda i: (0, 0, 0)), row, row],
        out_specs=[row, row, pl.BlockSpec((8, LANES), lambda i: (0, 0))],
        out_shape=[S((T, D), F32), S((T, D), BF), S((8, LANES), F32)],
        compiler_params=_cp(1))(a, wdn, x, target)


def _ffn_bwd_act(name, dyb, h, wup, wdn):
    T, D = h.shape
    FB = wup.shape[-2]
    tm = _tile(T, 1024)

    def body(d_ref, h_ref, wu_ref, wd_ref, o_ref):
        da = 0.5 * _nt(d_ref[...], wd_ref[...])
        hb = h_ref[...]
        gg = _nt(hb, wu_ref[0])
        uu = _nt(hb, wu_ref[1])
        sg = jax.nn.sigmoid(gg)
        o_ref[0] = (da * uu * (sg * (1.0 + gg * (1.0 - sg)))).astype(BF)
        o_ref[1] = (da * (gg * sg)).astype(BF)

    return pl.pallas_call(
        body, name=name, grid=(T // tm, N_FFN_BLK),
        in_specs=[pl.BlockSpec((tm, D), lambda i, j: (i, 0)),
                  pl.BlockSpec((tm, D), lambda i, j: (i, 0)),
                  pl.BlockSpec((2, None, FB, D), lambda i, j: (0, j, 0, 0)),
                  pl.BlockSpec((None, FB, D), lambda i, j: (j, 0, 0))],
        out_specs=pl.BlockSpec((2, None, tm, FB), lambda i, j: (0, j, i, 0)),
        out_shape=S((2, N_FFN_BLK, T, FB), BF),
        compiler_params=_cp(2))(dyb, h, wup, wdn)


def _ffn_dx(name, dgu, wup, x, g, dy):
    T, D = x.shape
    FB = wup.shape[-2]
    tm = _tile(T, 1024)

    def body(d_ref, w_ref, x_ref, g_ref, dy_ref, dx_ref, dg_ref, acc_ref):
        i, j = pl.program_id(0), pl.program_id(1)
        p = _nn(d_ref[0], w_ref[0]) + _nn(d_ref[1], w_ref[1])

        @pl.when(j == 0)
        def _():
            acc_ref[...] = p

        @pl.when(j > 0)
        def _():
            acc_ref[...] += p

        @pl.when(j == N_FFN_BLK - 1)
        def _():
            dx, dgr = _norm_bwd(acc_ref[...], x_ref[...], g_ref[...])
            dx_ref[...] = dx + dy_ref[...]
            _acc_rows(dg_ref, i == 0, dgr)

    return pl.pallas_call(
        body, name=name, grid=(T // tm, N_FFN_BLK),
        in_specs=[pl.BlockSpec((2, None, tm, FB), lambda i, j: (0, j, i, 0)),
                  pl.BlockSpec((2, None, FB, D), lambda i, j: (0, j, 0, 0)),
                  pl.BlockSpec((tm, D), lambda i, j: (i, 0)),
                  pl.BlockSpec((1, D), lambda i, j: (0, 0)),
                  pl.BlockSpec((tm, D), lambda i, j: (i, 0))],
        out_specs=[pl.BlockSpec((tm, D), lambda i, j: (i, 0)),
                   pl.BlockSpec((1, D), lambda i, j: (0, 0))],
        out_shape=[S((T, D), F32), S((1, D), F32)],
        scratch_shapes=[pltpu.VMEM((tm, D), F32)],
        compiler_params=_cp(2))(dgu, wup, x, g, dy)


def _tn_matmul(name, a, a_spec, b, b_spec, out_shape, out_spec, grid, acc_shape, scale=1.0, after=None):
    nk = grid[1]
    extra = [] if after is None else [after]

    def body(a_ref, b_ref, *rest):
        o_ref, acc_ref = rest[-2:]
        k = pl.program_id(1)
        p = _tn(a_ref[...], b_ref[...])

        @pl.when(k == 0)
        def _():
            acc_ref[...] = p

        @pl.when(k > 0)
        def _():
            acc_ref[...] += p

        @pl.when(k == nk - 1)
        def _():
            o_ref[...] = (acc_ref[...] * scale).astype(o_ref.dtype)

    return pl.pallas_call(
        body, name=name, grid=grid,
        in_specs=[a_spec, b_spec] + [pl.BlockSpec((8, LANES), lambda j, k: (0, 0)) for _ in extra],
        out_specs=out_spec, out_shape=out_shape,
        scratch_shapes=[pltpu.VMEM(acc_shape, F32)], compiler_params=_cp(2))(a, b, *extra)


def _ffn_dwup(name, h, dgu, after=None):
    T, D = h.shape
    FB = dgu.shape[-1]
    tk = _tile(T, 1024)
    return _tn_matmul(
        name + "_dwup", dgu.reshape(2 * N_FFN_BLK, T, FB), pl.BlockSpec((None, tk, FB), lambda j, k: (j, k, 0)),
        h, pl.BlockSpec((tk, D), lambda j, k: (k, 0)),
        S((2 * N_FFN_BLK, FB, D), BF), pl.BlockSpec((None, FB, D), lambda j, k: (j, 0, 0)),
        (2 * N_FFN_BLK, T // tk), (FB, D), after=after)


def _ffn_dwdn(name, a, dyb):
    _, T, FB = a.shape
    D = dyb.shape[1]
    tk = _tile(T, 1024)
    return _tn_matmul(
        name + "_dwdn", a, pl.BlockSpec((None, tk, FB), lambda j, k: (j, k, 0)),
        dyb, pl.BlockSpec((tk, D), lambda j, k: (k, 0)),
        S((N_FFN_BLK, FB, D), BF), pl.BlockSpec((None, FB, D), lambda j, k: (j, 0, 0)),
        (N_FFN_BLK, T // tk), (FB, D), scale=0.5)


def _mix_proj(x, g, wz):
    T, D = x.shape
    tm = _tile(T, 512)

    def body(x_ref, g_ref, w_ref, z_ref, h_ref):
        xf = x_ref[...]
        hb = (xf * _rstd(xf) * g_ref[...]).astype(BF)
        h_ref[...] = hb
        z_ref[...] = _nt(hb, w_ref[...])

    return pl.pallas_call(
        body, name="mix_proj", grid=(T // tm,),
        in_specs=[pl.BlockSpec((tm, D), lambda i: (i, 0)),
                  pl.BlockSpec((1, D), lambda i: (0, 0)),
                  pl.BlockSpec((ZW, D), lambda i: (0, 0))],
        out_specs=[pl.BlockSpec((tm, ZW), lambda i: (i, 0)),
                   pl.BlockSpec((tm, D), lambda i: (i, 0))],
        out_shape=[S((T, ZW), F32), S((T, D), BF)],
        compiler_params=_cp(1))(x, g, wz)


def _tri(n, lower):
    r = lax.broadcasted_iota(jnp.int32, (n, n), 0)
    c = lax.broadcasted_iota(jnp.int32, (n, n), 1)
    return (r >= c) if lower else (r <= c)


def _spatial_mix(vgn_b, ws_ref, bst, tm):
    tril = _tri(CHUNK, True)
    wms = [jnp.where(tril, ws_ref[g], 0.0).astype(BF) for g in range(GMLP_G)]
    rows = []
    for c in range(tm // CHUNK):
        cols = []
        for g in range(GMLP_G):
            vs = vgn_b[c * CHUNK:(c + 1) * CHUNK, g * GMLP_GD:(g + 1) * GMLP_GD]
            cols.append(_nn(wms[g], vs) + bst[:, g:g + 1])
        rows.append(jnp.concatenate(cols, axis=1))
    return jnp.concatenate(rows, axis=0), wms


HB = 128
AUG_W = FOX_HEADS * HB
COL_A, COL_B, COL_C = 64, 67, 70


def _spread_matrix():
    r = jnp.arange(FOX_W)
    return (jnp.arange(AUG_W)[None, :] == ((r // FOX_HD) * HB + r % FOX_HD)[:, None]).astype(BF)


def _piece_matrix(col):
    r = jnp.arange(LANES)
    dst = jnp.where(r < 3 * FOX_HEADS, (r % FOX_HEADS) * HB + col + r // FOX_HEADS, -1)
    return (jnp.arange(AUG_W)[None, :] == dst[:, None]).astype(BF)


def _ones_row(cols):
    c = jnp.arange(AUG_W) % HB
    hit = functools.reduce(jnp.logical_or, [(c >= a) & (c < a + 3) for a in cols])
    return hit.astype(F32)[None, :]


def _pieces(x):
    lane = lax.broadcasted_iota(jnp.int32, x.shape, 1)
    x = jnp.where(lane < FOX_HEADS, x, 0.0)
    hi = x.astype(BF).astype(F32)
    r1 = x - hi
    mid = r1.astype(BF).astype(F32)
    lo = (r1 - mid).astype(BF).astype(F32)
    return (hi + pltpu.roll(mid, FOX_HEADS, 1) + pltpu.roll(lo, 2 * FOX_HEADS, 1)).astype(BF)


def _mix_prep(z, bf128, g_q, g_k, g_sgu, w_s, b_st, g_go):
    T = z.shape[0]
    tm = _tile(T, 512)
    spread, pc_q, pc_k = _spread_matrix(), _piece_matrix(COL_A), _piece_matrix(COL_B)
    one_q, one_k, one_v = _ones_row([COL_B]), _ones_row([COL_A, COL_C]), _ones_row([COL_A])

    def body(z_ref, bf_ref, gq_ref, gk_ref, gs_ref, ws_ref, bst_ref, go_ref, sp_ref, pq_ref, pk_ref, oq_ref, ok_ref,
             ov_ref, q_ref, k_ref, v_ref, y_ref, carry_ref, qn_sc, kn_sc):
        i = pl.program_id(0)

        @pl.when(i == 0)
        def _():
            carry_ref[...] = jnp.zeros_like(carry_ref)

        for h in range(FOX_HEADS):
            hs = slice(h * FOX_HD, (h + 1) * FOX_HD)
            qh = z_ref[:, Z_Q + h * FOX_HD:Z_Q + (h + 1) * FOX_HD]
            kh = z_ref[:, Z_K + h * FOX_HD:Z_K + (h + 1) * FOX_HD]
            qn_sc[:, hs] = (qh * _rstd(qh) * gq_ref[...] * 0.125).astype(BF)
            kn_sc[:, hs] = (kh * _rstd(kh) * gk_ref[...]).astype(BF)

        fl = z_ref[:, Z_F:Z_F + LANES] + bf_ref[...]
        logf = jnp.minimum(fl, 0.0) - jnp.log1p(jnp.exp(-jnp.abs(fl)))
        csum = _hi(_tri(tm, True).astype(F32), logf) + carry_ref[...]
        carry_ref[...] = csum[tm - 1:tm, :]
        sp = sp_ref[...]
        q_ref[...] = (_nn(qn_sc[...], sp) + _nn(_pieces(csum), pq_ref[...]) + oq_ref[...]).astype(BF)
        k_ref[...] = (_nn(kn_sc[...], sp) + _nn(_pieces(-csum), pk_ref[...]) + ok_ref[...]).astype(BF)
        v_ref[...] = (_nn(z_ref[:, Z_V:Z_V + FOX_W].astype(BF), sp) + ov_ref[...]).astype(BF)

        u = _gelu(z_ref[:, Z_U:Z_U + GMLP_W])
        vg = _gelu(z_ref[:, Z_G:Z_G + GMLP_W])
        vgn = (vg * _rstd(vg) * gs_ref[...]).astype(BF)
        mixed, _ = _spatial_mix(vgn, ws_ref, bst_ref[...], tm)
        sgu = u * mixed
        y_ref[...] = (sgu * _rstd(sgu) * go_ref[...]).astype(BF)

    row = lambda i: (i, 0)
    fix2 = lambda i: (0, 0)
    return pl.pallas_call(
        body, name="mix_prep", grid=(T // tm,),
        in_specs=[pl.BlockSpec((tm, ZW), row),
                  pl.BlockSpec((1, LANES), fix2), pl.BlockSpec((1, FOX_HD), fix2), pl.BlockSpec((1, FOX_HD), fix2),
                  pl.BlockSpec((1, GMLP_W), fix2), pl.BlockSpec((GMLP_G, CHUNK, CHUNK), lambda i: (0, 0, 0)),
                  pl.BlockSpec((CHUNK, GMLP_G), fix2), pl.BlockSpec((1, GMLP_W), fix2),
                  pl.BlockSpec((FOX_W, AUG_W), fix2), pl.BlockSpec((LANES, AUG_W), fix2),
                  pl.BlockSpec((LANES, AUG_W), fix2), pl.BlockSpec((1, AUG_W), fix2), pl.BlockSpec((1, AUG_W), fix2),
                  pl.BlockSpec((1, AUG_W), fix2)],
        out_specs=[pl.BlockSpec((tm, AUG_W), row), pl.BlockSpec((tm, AUG_W), row), pl.BlockSpec((tm, AUG_W), row),
                   pl.BlockSpec((tm, GMLP_W), row)],
        out_shape=[S((T, AUG_W), BF), S((T, AUG_W), BF), S((T, AUG_W), BF), S((T, GMLP_W), BF)],
        scratch_shapes=[pltpu.VMEM((1, LANES), F32), pltpu.VMEM((tm, FOX_W), BF), pltpu.VMEM((tm, FOX_W), BF)],
        compiler_params=_cp(1))(z, bf128, g_q, g_k, g_sgu, w_s, b_st, g_go, spread, pc_q, pc_k, one_q, one_k, one_v)


def _fox_fwd(q, k, v):
    T = q.shape[0]
    tq = _tile(T, 1024)
    nq = T // tq

    def body(q_ref, k_ref, v_ref, o_ref, lse_ref, m_sc, acc_sc):
        i, j = pl.program_id(0), pl.program_id(1)

        @pl.when(j == 0)
        def _():
            m_sc[...] = jnp.full(m_sc.shape, NEG, F32)
            acc_sc[...] = jnp.zeros_like(acc_sc)

        def step(masked):
            mask = _tri(tq, True) if masked else None
            for h in range(FOX_HEADS):
                hb = slice(h * HB, (h + 1) * HB)
                s = _nt(q_ref[:, hb], k_ref[:, hb])
                if masked:
                    s = jnp.where(mask, s, NEG)
                m_prev = m_sc[h]
                m_new = jnp.maximum(m_prev, jnp.broadcast_to(jnp.max(s, axis=1, keepdims=True), (tq, HB)))
                p = jnp.exp(s - jnp.tile(m_new, (1, tq // HB))).astype(BF)
                acc_sc[:, hb] = jnp.exp(m_prev - m_new) * acc_sc[:, hb] + _nn(p, v_ref[:, hb])
                m_sc[h] = m_new

        @pl.when(j < i)
        def _():
            step(False)

        @pl.when(j == i)
        def _():
            step(True)
            lse_ref[...] = jnp.zeros_like(lse_ref)
            for h in range(FOX_HEADS):
                l = acc_sc[:, h * HB + COL_A:h * HB + COL_A + 1]
                o_ref[:, h * FOX_HD:(h + 1) * FOX_HD] = acc_sc[:, h * HB:h * HB + FOX_HD] / l
                lse_ref[:, h:h + 1] = m_sc[h][:, 0:1] + jnp.log(l)

    qi = lambda i, j: (i, 0)
    kj = lambda i, j: (jnp.minimum(i, j), 0)
    return pl.pallas_call(
        body, name="fox_fwd", grid=(nq, nq),
        in_specs=[pl.BlockSpec((tq, AUG_W), qi), pl.BlockSpec((tq, AUG_W), kj), pl.BlockSpec((tq, AUG_W), kj)],
        out_specs=[pl.BlockSpec((tq, FOX_W), qi), pl.BlockSpec((tq, LANES), qi)],
        out_shape=[S((T, FOX_W), F32), S((T, LANES), F32)],
        scratch_shapes=[pltpu.VMEM((FOX_HEADS, tq, HB), F32), pltpu.VMEM((tq, AUG_W), F32)],
        compiler_params=_cp(2))(q, k, v)


def _fox_bwd(q, k, v, dob):
    T = q.shape[0]
    tq = _tile(T, 1024)
    nq = T // tq
    half = AUG_W // 2
    hpg = FOX_HEADS // 2

    def body(q_ref, k_ref, v_ref, do_ref, dq_ref, dk_ref, dv_ref, dq_sc):
        j, i = pl.program_id(1), pl.program_id(2)

        @pl.when(jnp.logical_and(i == 0, j == 0))
        def _():
            dq_sc[...] = jnp.zeros_like(dq_sc)

        @pl.when(i == 0)
        def _():
            dk_ref[...] = jnp.zeros_like(dk_ref)
            dv_ref[...] = jnp.zeros_like(dv_ref)

        def step(masked):
            rows = pl.ds(pl.multiple_of(i * tq, tq), tq)
            mask = _tri(tq, True) if masked else None
            for h in range(hpg):
                hb = slice(h * HB, (h + 1) * HB)
                qh, kh, vh, doh = q_ref[:, hb], k_ref[:, hb], v_ref[:, hb], do_ref[:, hb]
                s = _nt(qh, kh)
                if masked:
                    s = jnp.where(mask, s, NEG)
                p = jnp.exp(s)
                dsb = (p * _nt(doh, vh)).astype(BF)
                dv_ref[:, hb] += _tn(p.astype(BF), doh)
                dk_ref[:, hb] += _tn(dsb, qh)
                dq_sc[rows, hb] += _nn(dsb, kh)

        @pl.when(i > j)
        def _():
            step(False)

        @pl.when(i == j)
        def _():
            step(True)
            dq_ref[...] = dq_sc[pl.ds(pl.multiple_of(j * tq, tq), tq), :]

    qi = lambda g, j, i: (jnp.maximum(i, j), g)
    kj = lambda g, j, i: (j, g)
    return pl.pallas_call(
        body, name="fox_bwd", grid=(2, nq, nq),
        in_specs=[pl.BlockSpec((tq, half), qi), pl.BlockSpec((tq, half), kj), pl.BlockSpec((tq, half), kj),
                  pl.BlockSpec((tq, half), qi)],
        out_specs=[pl.BlockSpec((tq, half), kj), pl.BlockSpec((tq, half), kj), pl.BlockSpec((tq, half), kj)],
        out_shape=[S((T, AUG_W), F32), S((T, AUG_W), F32), S((T, AUG_W), F32)],
        scratch_shapes=[pltpu.VMEM((T, half), F32)],
        compiler_params=_cp(3))(q, k, v, dob)


def _mix_out(attn, yg, g_fo, wout, x):
    T, D = x.shape
    tm = _tile(T, 512)

    def body(a_ref, y_ref, g_ref, w_ref, x_ref, o_ref):
        at = a_ref[...]
        yf = (at * _rstd(at) * g_ref[...]).astype(BF)
        o_ref[...] = x_ref[...] + _nn(yf, w_ref[:FOX_W, :]) + _nn(y_ref[...], w_ref[FOX_W:, :])

    row = lambda i: (i, 0)
    return pl.pallas_call(
        body, name="mix_out", grid=(T // tm,),
        in_specs=[pl.BlockSpec((tm, FOX_W), row), pl.BlockSpec((tm, GMLP_W), row),
                  pl.BlockSpec((1, FOX_W), lambda i: (0, 0)), pl.BlockSpec((D, D), lambda i: (0, 0)),
                  pl.BlockSpec((tm, D), row)],
        out_specs=pl.BlockSpec((tm, D), row),
        out_shape=S((T, D), F32),
        compiler_params=_cp(1))(attn, yg, g_fo, wout, x)


def _mix_out_bwd(dx, attn, yg, g_fo, wout, qf, lse):
    T, D = dx.shape
    tm = _tile(T, 512)
    n = T // tm
    spread, pc_l, pc_d = _spread_matrix(), _piece_matrix(COL_C), _piece_matrix(COL_A)

    def body(dx_ref, a_ref, y_ref, g_ref, w_ref, qf_ref, lse_ref, sp_ref, pl_ref, pd_ref,
             qb_ref, dob_ref, dyg_ref, dw_ref, dg_ref, acc_ref, dsum_ref):
        i = pl.program_id(0)
        dxb = dx_ref[...].astype(BF)
        at = a_ref[...]
        yf = (at * _rstd(at) * g_ref[...]).astype(BF)
        dy = _nt(dxb, w_ref[...])
        p_top = _tn(yf, dxb)
        p_bot = _tn(y_ref[...], dxb)

        @pl.when(i == 0)
        def _():
            acc_ref[:FOX_W, :] = p_top
            acc_ref[FOX_W:, :] = p_bot

        @pl.when(i > 0)
        def _():
            acc_ref[:FOX_W, :] += p_top
            acc_ref[FOX_W:, :] += p_bot

        @pl.when(i == n - 1)
        def _():
            dw_ref[...] = acc_ref[...].astype(BF)

        dat, dgr = _norm_bwd(dy[:, :FOX_W], at, g_ref[...])
        _acc_rows(dg_ref, i == 0, dgr)
        dyg_ref[...] = dy[:, FOX_W:]
        prod = dat * at
        dsum_ref[...] = jnp.zeros_like(dsum_ref)
        for h in range(FOX_HEADS):
            dsum_ref[:, h:h + 1] = jnp.sum(prod[:, h * FOX_HD:(h + 1) * FOX_HD], axis=1, keepdims=True)
        dob_ref[...] = (_nn(dat.astype(BF), sp_ref[...]) + _nn(_pieces(-dsum_ref[...]), pd_ref[...])).astype(BF)
        qb_ref[...] = (qf_ref[...].astype(F32) + _nn(_pieces(-lse_ref[...]), pl_ref[...])).astype(BF)

    row = lambda i: (i, 0)
    fix = lambda i: (0, 0)
    return pl.pallas_call(
        body, name="mix_out_bwd", grid=(n,),
        in_specs=[pl.BlockSpec((tm, D), row), pl.BlockSpec((tm, FOX_W), row), pl.BlockSpec((tm, GMLP_W), row),
                  pl.BlockSpec((1, FOX_W), fix), pl.BlockSpec((D, D), fix), pl.BlockSpec((tm, AUG_W), row),
                  pl.BlockSpec((tm, LANES), row), pl.BlockSpec((FOX_W, AUG_W), fix), pl.BlockSpec((LANES, AUG_W), fix),
                  pl.BlockSpec((LANES, AUG_W), fix)],
        out_specs=[pl.BlockSpec((tm, AUG_W), row), pl.BlockSpec((tm, AUG_W), row), pl.BlockSpec((tm, GMLP_W), row),
                   pl.BlockSpec((D, D), fix), pl.BlockSpec((1, FOX_W), fix)],
        out_shape=[S((T, AUG_W), BF), S((T, AUG_W), BF), S((T, GMLP_W), F32), S((D, D), BF), S((1, FOX_W), F32)],
        scratch_shapes=[pltpu.VMEM((D, D), F32), pltpu.VMEM((tm, LANES), F32)],
        compiler_params=_cp(1))(dx, attn, yg, g_fo, wout, qf, lse, spread, pc_l, pc_d)


def _mix_prep_bwd(z, dq, dk, dv, dyg, bf128, g_q, g_k, g_sgu, w_s, b_st, g_go):
    T = z.shape[0]
    tm = _tile(T, 512)
    n = T // tm

    def body(z_ref, dq_ref, dk_ref, dv_ref, dyg_ref, bf_ref, gq_ref, gk_ref, gs_ref, ws_ref,
             bst_ref, go_ref, dz_ref, dgq_ref, dgk_ref, dgs_ref, dgo_ref, dws_ref, dbst_ref, dbf_ref, carry_ref):
        i = pl.program_id(0)
        first = i == 0

        @pl.when(first)
        def _():
            carry_ref[...] = jnp.zeros_like(carry_ref)

        lane = lax.broadcasted_iota(jnp.int32, (tm, LANES), 1)
        dc = jnp.zeros((tm, LANES), F32)
        gq_rows, gk_rows = [], []
        for h in range(FOX_HEADS):
            hp = slice(h * HB, h * HB + FOX_HD)
            dqh, gqr = _norm_bwd(dq_ref[:, hp] * 0.125, z_ref[:, Z_Q + h * FOX_HD:Z_Q + (h + 1) * FOX_HD], gq_ref[...])
            dkh, gkr = _norm_bwd(dk_ref[:, hp], z_ref[:, Z_K + h * FOX_HD:Z_K + (h + 1) * FOX_HD], gk_ref[...])
            dz_ref[:, Z_Q + h * FOX_HD:Z_Q + (h + 1) * FOX_HD] = dqh.astype(BF)
            dz_ref[:, Z_K + h * FOX_HD:Z_K + (h + 1) * FOX_HD] = dkh.astype(BF)
            dz_ref[:, Z_V + h * FOX_HD:Z_V + (h + 1) * FOX_HD] = dv_ref[:, hp].astype(BF)
            dch = dq_ref[:, h * HB + COL_A:h * HB + COL_A + 1] - dk_ref[:, h * HB + COL_B:h * HB + COL_B + 1]
            dc = jnp.where(lane == h, dch, dc)
            gq_rows.append(gqr)
            gk_rows.append(gkr)
        _acc_rows(dgq_ref, first, functools.reduce(lambda a, b: a + b, gq_rows))
        _acc_rows(dgk_ref, first, functools.reduce(lambda a, b: a + b, gk_rows))

        dlogf = _hi(_tri(tm, False).astype(F32), dc) + carry_ref[...]
        carry_ref[...] = dlogf[0:1, :]
        fl = z_ref[:, Z_F:Z_F + LANES] + bf_ref[...]
        lane = lax.broadcasted_iota(jnp.int32, (tm, LANES), 1)
        df = jnp.where(lane < FOX_HEADS, dlogf * jax.nn.sigmoid(-fl), 0.0)
        dz_ref[:, Z_F:Z_F + LANES] = df.astype(BF)
        _acc_rows(dbf_ref, first, df)

        u_pre = z_ref[:, Z_U:Z_U + GMLP_W]
        vg_pre = z_ref[:, Z_G:Z_G + GMLP_W]
        u = _gelu(u_pre)
        vg = _gelu(vg_pre)
        vgn = (vg * _rstd(vg) * gs_ref[...]).astype(BF)
        bst = bst_ref[...]
        mixed, wms = _spatial_mix(vgn, ws_ref, bst, tm)
        sgu = u * mixed
        dsgu, gor = _norm_bwd(dyg_ref[...], sgu, go_ref[...])
        _acc_rows(dgo_ref, first, gor)
        du = dsgu * mixed
        dmixed = dsgu * u
        dmb = dmixed.astype(BF)
        tril = _tri(CHUNK, True)
        dvgn_rows = []
        dws = [None] * GMLP_G
        dbs = [None] * GMLP_G
        for c in range(tm // CHUNK):
            cs = slice(c * CHUNK, (c + 1) * CHUNK)
            cols = []
            for g in range(GMLP_G):
                gs = slice(g * GMLP_GD, (g + 1) * GMLP_GD)
                dmc = dmb[cs, gs]
                pw = _nt(dmc, vgn[cs, gs])
                pb = jnp.sum(dmixed[cs, gs], axis=1, keepdims=True)
                dws[g] = pw if dws[g] is None else dws[g] + pw
                dbs[g] = pb if dbs[g] is None else dbs[g] + pb
                cols.append(_tn(wms[g], dmc))
            dvgn_rows.append(jnp.concatenate(cols, axis=1))
        dvgn = jnp.concatenate(dvgn_rows, axis=0)
        dbs_t = jnp.concatenate(dbs, axis=1)
        for g in range(GMLP_G):
            dwg = jnp.where(tril, dws[g], 0.0)

            @pl.when(first)
            def _():
                dws_ref[g] = dwg

            @pl.when(jnp.logical_not(first))
            def _():
                dws_ref[g] += dwg

        @pl.when(first)
        def _():
            dbst_ref[...] = dbs_t

        @pl.when(jnp.logical_not(first))
        def _():
            dbst_ref[...] += dbs_t

        dvg, gsr = _norm_bwd(dvgn, vg, gs_ref[...])
        _acc_rows(dgs_ref, first, gsr)
        dz_ref[:, Z_U:Z_U + GMLP_W] = (du * _gelu_grad(u_pre)).astype(BF)
        dz_ref[:, Z_G:Z_G + GMLP_W] = (dvg * _gelu_grad(vg_pre)).astype(BF)

    rev = lambda i: (n - 1 - i, 0)
    fix = lambda i: (0, 0)
    fix3 = lambda i: (0, 0, 0)
    return pl.pallas_call(
        body, name="mix_prep_bwd", grid=(n,),
        in_specs=[pl.BlockSpec((tm, ZW), rev), pl.BlockSpec((tm, AUG_W), rev), pl.BlockSpec((tm, AUG_W), rev),
                  pl.BlockSpec((tm, AUG_W), rev), pl.BlockSpec((tm, GMLP_W), rev),
                  pl.BlockSpec((1, LANES), fix), pl.BlockSpec((1, FOX_HD), fix), pl.BlockSpec((1, FOX_HD), fix),
                  pl.BlockSpec((1, GMLP_W), fix), pl.BlockSpec((GMLP_G, CHUNK, CHUNK), fix3),
                  pl.BlockSpec((CHUNK, GMLP_G), fix), pl.BlockSpec((1, GMLP_W), fix)],
        out_specs=[pl.BlockSpec((tm, ZW), rev), pl.BlockSpec((1, FOX_HD), fix), pl.BlockSpec((1, FOX_HD), fix),
                   pl.BlockSpec((1, GMLP_W), fix), pl.BlockSpec((1, GMLP_W), fix),
                   pl.BlockSpec((GMLP_G, CHUNK, CHUNK), fix3), pl.BlockSpec((CHUNK, GMLP_G), fix),
                   pl.BlockSpec((1, LANES), fix)],
        out_shape=[S((T, ZW), BF), S((1, FOX_HD), F32), S((1, FOX_HD), F32), S((1, GMLP_W), F32), S((1, GMLP_W), F32),
                   S((GMLP_G, CHUNK, CHUNK), F32), S((CHUNK, GMLP_G), F32), S((1, LANES), F32)],
        scratch_shapes=[pltpu.VMEM((1, LANES), F32)],
        compiler_params=_cp(1))(z, dq, dk, dv, dyg, bf128, g_q, g_k, g_sgu, w_s, b_st, g_go)


def _mix_proj_bwd(dz, wz, x, g, dy):
    T, D = x.shape
    tm = _tile(T, 512)

    def body(dz_ref, w_ref, x_ref, g_ref, dy_ref, dx_ref, dxb_ref, dg_ref):
        dh = _nn(dz_ref[...], w_ref[...])
        dx, dgr = _norm_bwd(dh, x_ref[...], g_ref[...])
        dx = dx + dy_ref[...]
        dx_ref[...] = dx
        dxb_ref[...] = dx.astype(BF)
        _acc_rows(dg_ref, pl.program_id(0) == 0, dgr)

    row = lambda i: (i, 0)
    fix = lambda i: (0, 0)
    return pl.pallas_call(
        body, name="mix_proj_bwd", grid=(T // tm,),
        in_specs=[pl.BlockSpec((tm, ZW), row), pl.BlockSpec((ZW, D), fix), pl.BlockSpec((tm, D), row),
                  pl.BlockSpec((1, D), fix), pl.BlockSpec((tm, D), row)],
        out_specs=[pl.BlockSpec((tm, D), row), pl.BlockSpec((tm, D), row), pl.BlockSpec((1, D), fix)],
        out_shape=[S((T, D), F32), S((T, D), BF), S((1, D), F32)],
        compiler_params=_cp(1))(dz, wz, x, g, dy)


def _ca_kv(mem, g_mem, wckv, g_ck):
    M, D = mem.shape

    def body(m_ref, g_ref, w_ref, gk_ref, mn_ref, kr_ref, kn_ref, v_ref):
        mf = m_ref[...]
        mn = (mf * _rstd(mf) * g_ref[...]).astype(BF)
        mn_ref[...] = mn
        for h in range(CA_HEADS):
            kr = _nn(mn, w_ref[h])
            kr_ref[h] = kr
            kn_ref[h] = (kr * _rstd(kr) * gk_ref[...]).astype(BF)
            v_ref[h] = _nn(mn, w_ref[CA_HEADS + h]).astype(BF)

    hd = (CA_HEADS, M, CA_HD)
    return pl.pallas_call(
        body, name="ca_kv", out_shape=[S((M, D), BF), S(hd, F32), S(hd, BF), S(hd, BF)],
        compiler_params=pltpu.CompilerParams(vmem_limit_bytes=VMEM_LIMIT))(mem, g_mem, wckv, g_ck)


def _ca_tile_fwd(xt, gca, wcq, gcq, kn_ref, v_ref):
    hb = (xt * _rstd(xt) * gca).astype(BF)
    qc = _nn(hb, wcq)
    qr, qn, ps = [], [], []
    for h in range(CA_HEADS):
        qh = qc[:, h * CA_HD:(h + 1) * CA_HD]
        qnh = (qh * _rstd(qh) * gcq * 0.0625).astype(BF)
        s = _nt(qnh, kn_ref[h])
        e = jnp.exp(s - jnp.max(s, axis=1, keepdims=True))
        ps.append(e / jnp.sum(e, axis=1, keepdims=True))
        qr.append(qh)
        qn.append(qnh)
    return hb, qr, qn, ps


def _ca_fwd(x, g_ca, wcq, g_cq, kn, vv, wco):
    T, D = x.shape
    M = kn.shape[1]
    tm = _tile(T, 512)

    def body(x_ref, gca_ref, wcq_ref, gcq_ref, kn_ref, v_ref, wco_ref, o_ref, ob_sc):
        xt = x_ref[...]
        _, _, _, ps = _ca_tile_fwd(xt, gca_ref[...], wcq_ref[...], gcq_ref[...], kn_ref, v_ref)
        for h in range(CA_HEADS):
            ob_sc[:, h * CA_HD:(h + 1) * CA_HD] = _nn(ps[h].astype(BF), v_ref[h]).astype(BF)
        o_ref[...] = xt + _nn(ob_sc[...], wco_ref[...])

    row = lambda i: (i, 0)
    fix = lambda i: (0, 0)
    fix3 = lambda i: (0, 0, 0)
    return pl.pallas_call(
        body, name="ca_fwd", grid=(T // tm,),
        in_specs=[pl.BlockSpec((tm, D), row), pl.BlockSpec((1, D), fix), pl.BlockSpec((D, D), fix),
                  pl.BlockSpec((1, CA_HD), fix), pl.BlockSpec((CA_HEADS, M, CA_HD), fix3),
                  pl.BlockSpec((CA_HEADS, M, CA_HD), fix3), pl.BlockSpec((D, D), fix)],
        out_specs=pl.BlockSpec((tm, D), row), out_shape=S((T, D), F32),
        scratch_shapes=[pltpu.VMEM((tm, D), BF)],
        compiler_params=_cp(1))(x, g_ca, wcq, g_cq, kn, vv, wco)


def _ca_bwd(x, dy, g_ca, wcq, g_cq, kn, vv, wco):
    T, D = x.shape
    M = kn.shape[1]
    tm = _tile(T, 512)
    n = T // tm

    def body(x_ref, dy_ref, gca_ref, wcq_ref, gcq_ref, kn_ref, v_ref, wco_ref,
             dx_ref, dwq_ref, dwo_ref, dkn_ref, dv_ref, dgcq_ref, dgca_ref, aq_sc, ao_sc, ob_sc, dq_sc):
        i = pl.program_id(0)
        first = i == 0
        xt = x_ref[...]
        dyt = dy_ref[...]
        dyb = dyt.astype(BF)
        hb, qr, qn, ps = _ca_tile_fwd(xt, gca_ref[...], wcq_ref[...], gcq_ref[...], kn_ref, v_ref)
        do = _nt(dyb, wco_ref[...])
        gcq_rows = None
        for h in range(CA_HEADS):
            hs = slice(h * CA_HD, (h + 1) * CA_HD)
            p = ps[h]
            pb = p.astype(BF)
            ob_sc[:, hs] = _nn(pb, v_ref[h]).astype(BF)
            doh = do[:, hs].astype(BF)
            dp = _nt(doh, v_ref[h])
            ds = (p * (dp - jnp.sum(dp * p, axis=1, keepdims=True))).astype(BF)
            dvh = _tn(pb, doh)
            dkh = _tn(ds, qn[h])

            @pl.when(first)
            def _():
                dv_ref[h] = dvh
                dkn_ref[h] = dkh

            @pl.when(jnp.logical_not(first))
            def _():
                dv_ref[h] += dvh
                dkn_ref[h] += dkh

            dqn = _nn(ds, kn_ref[h]) * 0.0625
            dqh, gr = _norm_bwd(dqn, qr[h], gcq_ref[...])
            gcq_rows = gr if gcq_rows is None else gcq_rows + gr
            dq_sc[:, hs] = dqh.astype(BF)
        _acc_rows(dgcq_ref, first, gcq_rows)
        dqb = dq_sc[...]
        p_o = _tn(ob_sc[...], dyb)
        p_q = _tn(hb, dqb)

        @pl.when(first)
        def _():
            ao_sc[...] = p_o
            aq_sc[...] = p_q

        @pl.when(jnp.logical_not(first))
        def _():
            ao_sc[...] += p_o
            aq_sc[...] += p_q

        @pl.when(i == n - 1)
        def _():
            dwo_ref[...] = ao_sc[...].astype(BF)
            dwq_ref[...] = aq_sc[...].astype(BF)

        dh = _nt(dqb, wcq_ref[...])
        dx, gar = _norm_bwd(dh, xt, gca_ref[...])
        dx_ref[...] = dx + dyt
        _acc_rows(dgca_ref, first, gar)

    row = lambda i: (i, 0)
    fix = lambda i: (0, 0)
    fix3 = lambda i: (0, 0, 0)
    hd = (CA_HEADS, M, CA_HD)
    return pl.pallas_call(
        body, name="ca_bwd", grid=(n,),
        in_specs=[pl.BlockSpec((tm, D), row), pl.BlockSpec((tm, D), row), pl.BlockSpec((1, D), fix),
                  pl.BlockSpec((D, D), fix), pl.BlockSpec((1, CA_HD), fix), pl.BlockSpec(hd, fix3),
                  pl.BlockSpec(hd, fix3), pl.BlockSpec((D, D), fix)],
        out_specs=[pl.BlockSpec((tm, D), row), pl.BlockSpec((D, D), fix), pl.BlockSpec((D, D), fix),
                   pl.BlockSpec(hd, fix3), pl.BlockSpec(hd, fix3), pl.BlockSpec((1, CA_HD), fix),
                   pl.BlockSpec((1, D), fix)],
        out_shape=[S((T, D), F32), S((D, D), BF), S((D, D), BF), S(hd, F32), S(hd, F32), S((1, CA_HD), F32),
                   S((1, D), F32)],
        scratch_shapes=[pltpu.VMEM((D, D), F32), pltpu.VMEM((D, D), F32), pltpu.VMEM((tm, D), BF),
                        pltpu.VMEM((tm, D), BF)],
        compiler_params=_cp(1))(x, dy, g_ca, wcq, g_cq, kn, vv, wco)


def _ca_kv_bwd(mem, g_mem, mn, kraw, dkn, dvv, wckv, g_ck):
    M, D = mem.shape

    def body(m_ref, g_ref, mn_ref, kr_ref, dkn_ref, dv_ref, w_ref, gk_ref, dw_ref, dgk_ref, dgm_ref):
        mn = mn_ref[...]
        dmn = jnp.zeros((M, D), F32)
        gk_rows = None
        for h in range(CA_HEADS):
            dkr, gr = _norm_bwd(dkn_ref[h], kr_ref[h], gk_ref[...])
            gk_rows = gr if gk_rows is None else gk_rows + gr
            dkb = dkr.astype(BF)
            dvb = dv_ref[h].astype(BF)
            dw_ref[h] = _tn(mn, dkb).astype(BF)
            dw_ref[CA_HEADS + h] = _tn(mn, dvb).astype(BF)
            dmn = dmn + _nt(dkb, w_ref[h]) + _nt(dvb, w_ref[CA_HEADS + h])
        dgk_ref[...] = jnp.sum(gk_rows, axis=0, keepdims=True)
        mf = m_ref[...]
        dgm_ref[...] = jnp.sum(dmn * (mf * _rstd(mf)), axis=0, keepdims=True)

    return pl.pallas_call(
        body, name="ca_kv_bwd",
        out_shape=[S((2 * CA_HEADS, D, CA_HD), BF), S((1, CA_HD), F32), S((1, D), F32)],
        compiler_params=pltpu.CompilerParams(vmem_limit_bytes=VMEM_LIMIT))(mem, g_mem, mn, kraw, dkn, dvv, wckv, g_ck)


def _after(g, token):
    return g if token is None else g + token[0:1, 0:1]


def _local_step(x, mem, target, small, weights, emit):
    T, D = x.shape
    p = small
    bf128 = jnp.pad(p["b_f"], ((0, 0), (0, LANES - FOX_HEADS)))
    b_st = p["b_s"].T

    wup1 = weights("ffn1_up", x)["wup1"]
    a1, h1 = _ffn_up("ffn1_up", x, p["g_ffn1"], wup1)
    wdn1 = weights("ffn1_dn", h1)["wdn1"]
    x1 = _ffn_down("ffn1_down", a1, wdn1, x)
    wm = weights("mix", x1)
    z, h2 = _mix_proj(x1, p["g_mix"], wm["wz"])
    qf, ka, va, yg = _mix_prep(z, bf128, p["g_q"], p["g_k"], p["g_sgu"], p["w_s"], b_st, p["g_gmlp_o"])
    attn, lse = _fox_fwd(qf, ka, va)
    x2 = _mix_out(attn, yg, p["g_fox_o"], wm["wout"], x1)
    wc = weights("ca", x2)
    mn, kraw, ckn, cvv = _ca_kv(mem, p["g_mem"], wc["wckv"], p["g_ck"])
    x3 = _ca_fwd(x2, p["g_ca"], wc["wcq"], p["g_cq"], ckn, cvv, wc["wco"])
    w2 = weights("ffn2", x3)
    a2, h4 = _ffn_up("ffn2_up", x3, p["g_ffn2"], w2["wup2"])
    dy4, dy4b, sq = _ffn_down_loss("ffn2_down", a2, w2["wdn2"], x3, target)

    gs = {}
    dgu2 = _ffn_bwd_act("ffn2_bwd_act", dy4b, h4, w2["wup2"], w2["wdn2"])
    tok = emit("ffn2", {"wup2": _ffn_dwup("ffn2", h4, dgu2), "wdn2": _ffn_dwdn("ffn2", a2, dy4b)})
    dx3, gs["g_ffn2"] = _ffn_dx("ffn2_dx", dgu2, w2["wup2"], x3, _after(p["g_ffn2"], tok), dy4)

    dx2, dwcq, dwco, dckn, dcvv, gs["g_cq"], gs["g_ca"] = _ca_bwd(
        x2, dx3, p["g_ca"], wc["wcq"], p["g_cq"], ckn, cvv, wc["wco"])
    dwckv, gs["g_ck"], gs["g_mem"] = _ca_kv_bwd(mem, p["g_mem"], mn, kraw, dckn, dcvv, wc["wckv"], p["g_ck"])

    qb, dob, dyg, dwout, gs["g_fox_o"] = _mix_out_bwd(dx2, attn, yg, p["g_fox_o"], wm["wout"], qf, lse)
    dq, dk, dv = _fox_bwd(qb, ka, va, dob)
    dz, gs["g_q"], gs["g_k"], gs["g_sgu"], gs["g_gmlp_o"], gs["w_s"], dbst, dbf = _mix_prep_bwd(
        z, dq, dk, dv, dyg, bf128, p["g_q"], p["g_k"], p["g_sgu"], p["w_s"], b_st, p["g_gmlp_o"])
    gs["b_s"] = dbst.T
    gs["b_f"] = dbf[:, :FOX_HEADS]
    tok_ws = emit("w_s", {"w_s": gs["w_s"]})
    tk = _tile(T, 1024)
    zb = ZW // 3
    dwz = _tn_matmul(
        "mix_dwz", dz, pl.BlockSpec((tk, zb), lambda j, k: (k, j)), h2, pl.BlockSpec((tk, D), lambda j, k: (k, 0)),
        S((ZW, D), F32), pl.BlockSpec((zb, D), lambda j, k: (j, 0)), (3, T // tk), (zb, D))
    tok = emit("mid", {"wcq": dwcq, "wco": dwco, "wckv": dwckv, "wout": dwout, "wz": dwz})
    dx1, dx1b, gs["g_mix"] = _mix_proj_bwd(dz, wm["wz"], x1, _after(_after(p["g_mix"], tok), tok_ws), dx2)

    dgu1 = _ffn_bwd_act("ffn1_bwd_act", dx1b, h1, wup1, wdn1)
    tok = emit("ffn1_dn", {"wdn1": _ffn_dwdn("ffn1", a1, dx1b)})
    tok = emit("ffn1_up", {"wup1": _ffn_dwup("ffn1", h1, dgu1, after=tok)})
    dx0, gs["g_ffn1"] = _ffn_dx("ffn1_dx", dgu1, wup1, x, _after(p["g_ffn1"], tok), dx1)
    return sq, dx0, gs


MESH = pl.DeviceIdType.MESH
HBM_SPEC = pl.BlockSpec(memory_space=pltpu.HBM)
N_PEER = N_DEV - 1


def _place():
    return lax.axis_index("x"), lax.axis_index("y"), lax.axis_index("c")


def _slot(px, py, pc):
    return 4 * px + 2 * py + pc


SEM_SPEC = pl.BlockSpec(memory_space=pltpu.SEMAPHORE)
ANY_SPEC = pl.BlockSpec(memory_space=pl.ANY)
DATAFLOW = pltpu.SideEffectType.DATAFLOW_SIDE_EFFECTING


def _hbm(a):
    return pltpu.with_memory_space_constraint(a, pltpu.HBM)


def _peer(x, y, c, r):
    return (1 - x if r & 4 else x, 1 - y if r & 2 else y, 1 - c if r & 1 else c)


def _place_own(srcs, whole):
    my = _slot(*_place())
    lands = []
    for s in srcs:
        blk = s[None] if whole else lax.dynamic_slice_in_dim(s, my, 1, 0)
        shape = (N_DEV,) + s.shape if whole else s.shape
        lands.append(lax.dynamic_update_slice_in_dim(lax.empty(shape, s.dtype), blk, my, 0))
    return lands


def _copy_start(name, srcs, lands, whole):
    n = len(srcs)

    def body(*refs):
        src, land = refs[:n], refs[n:2 * n]
        send, recv = refs[2 * n:3 * n], refs[3 * n:4 * n]
        token = refs[6 * n]
        x, y, c = _place()
        my = _slot(x, y, c)
        for a in range(n):
            for r in range(1, N_DEV):
                p = _peer(x, y, c, r)
                pltpu.make_async_remote_copy(
                    src_ref=src[a] if whole else src[a].at[_slot(*p)], dst_ref=land[a].at[my],
                    send_sem=send[a].at[r - 1], recv_sem=recv[a].at[r - 1], device_id=p, device_id_type=MESH).start()
        token[...] = jnp.zeros_like(token)

    out = pl.pallas_call(
        body, name=name,
        out_shape=([pltpu.SemaphoreType.DMA((N_PEER,))] * (2 * n)
                   + [pltpu.HBM(s.shape, s.dtype) for s in srcs] + [pltpu.HBM(s.shape, s.dtype) for s in lands]
                   + [S((8, LANES), F32)]),
        in_specs=[HBM_SPEC] * (2 * n),
        out_specs=[SEM_SPEC] * (2 * n) + [HBM_SPEC] * (2 * n) + [pl.BlockSpec(memory_space=pltpu.VMEM)],
        input_output_aliases={i: 2 * n + i for i in range(2 * n)},
        compiler_params=pltpu.CompilerParams(has_side_effects=DATAFLOW),
    )(*[_hbm(s) for s in srcs], *[_hbm(s) for s in lands])
    return out[:n], out[n:2 * n], out[2 * n:3 * n], out[3 * n:4 * n], out[4 * n]


def _copy_wait(name, srcs, lands, send, recv, after, whole):
    n = len(srcs)

    def body(*refs):
        src, land = refs[:n], refs[n:2 * n]
        snd, rcv = refs[2 * n:3 * n], refs[3 * n:4 * n]
        x, y, c = _place()
        for a in range(n):
            for r in range(1, N_DEV):
                p = _peer(x, y, c, r)
                ps = _slot(*p)
                cp = pltpu.make_async_remote_copy(
                    src_ref=src[a] if whole else src[a].at[ps], dst_ref=land[a].at[ps],
                    send_sem=snd[a].at[r - 1], recv_sem=rcv[a].at[r - 1], device_id=p, device_id_type=MESH)
                cp.wait_send()
                cp.wait_recv()

    out = pl.pallas_call(
        body, name=name,
        out_shape=[pltpu.HBM(s.shape, s.dtype) for s in srcs] + [pltpu.HBM(s.shape, s.dtype) for s in lands],
        in_specs=[HBM_SPEC] * (2 * n) + [SEM_SPEC] * (2 * n) + [ANY_SPEC],
        out_specs=[HBM_SPEC] * (2 * n),
        input_output_aliases={i: i for i in range(2 * n)},
        compiler_params=pltpu.CompilerParams(has_side_effects=DATAFLOW),
    )(*srcs, *lands, *send, *recv, after)
    return out[n:]


def _adamw(w, g, m, v):
    m2 = ADAM_B1 * m + (1.0 - ADAM_B1) * g
    v2 = ADAM_B2 * v + (1.0 - ADAM_B2) * (g * g)
    m_hat = m2 / (1.0 - ADAM_B1 ** ADAM_STEP)
    v_hat = v2 / (1.0 - ADAM_B2 ** ADAM_STEP)
    delta = -ADAM_LR * (m_hat / (jnp.sqrt(v_hat) + ADAM_EPS) + ADAM_WD * w)
    return delta, m2, v2


def _adamw_big(name, slots, w, m, v):
    R, C = w.shape
    tr = next((t for t in (256, 352) if R % t == 0), R)

    def body(s_ref, w_ref, m_ref, v_ref, g_ref, d_ref, m2_ref, v2_ref):
        g = s_ref[0].astype(F32)
        for k in range(1, N_DEV):
            g = g + s_ref[k].astype(F32)
        d, m2, v2 = _adamw(w_ref[...], g, m_ref[...], v_ref[...])
        g_ref[...] = g
        d_ref[...] = d
        m2_ref[...] = m2
        v2_ref[...] = v2

    row = pl.BlockSpec((tr, C), lambda i: (i, 0))
    return pl.pallas_call(
        body, name=name, grid=(R // tr,),
        in_specs=[pl.BlockSpec((N_DEV, tr, C), lambda i: (0, i, 0)), row, row, row],
        out_specs=[row] * 4, out_shape=[S((R, C), F32)] * 4,
        compiler_params=_cp(1))(slots, w, m, v)


TINY_ROWS = (("b_s", 8), ("g_ffn1", 8), ("g_mix", 8), ("g_ca", 8), ("g_mem", 8), ("g_ffn2", 8), ("g_sgu", 4),
             ("g_fox_o", 4), ("g_gmlp_o", 4), ("g_cq", 2), ("g_ck", 2), ("g_q", 1), ("g_k", 1), ("b_f", 1),
             ("loss", 1))
TINY_P = 72


def _pack_tiny(d):
    rows = []
    for name, r in TINY_ROWS:
        flat = d[name].reshape(-1) if name in d else jnp.zeros((r * LANES,), F32)
        rows.append(jnp.pad(flat, (0, r * LANES - flat.shape[0])).reshape(r, LANES))
    used = sum(r for _, r in TINY_ROWS)
    rows.append(jnp.zeros((TINY_P - used, LANES), F32))
    return jnp.concatenate(rows, axis=0)


def _unpack_tiny(packed, shapes):
    out, at = {}, 0
    for name, r in TINY_ROWS:
        shape = shapes[name]
        size = 1
        for s in shape:
            size *= s
        out[name] = packed[at:at + r].reshape(-1)[:size].reshape(shape)
        at += r
    return out


WEIGHTS =('g_ffn1', 'w_ffn1_in', 'w_ffn1_out', 'g_mix', 'w_in', 'b_f', 'g_q', 'g_k', 'g_sgu', 'w_s', 'b_s',
           'g_fox_o', 'g_gmlp_o', 'w_out', 'g_ca', 'g_mem', 'w_cq', 'w_ckv', 'g_cq', 'g_ck', 'w_co', 'g_ffn2',
           'w_ffn2_in', 'w_ffn2_out')
BIG = ('w_ffn1_in', 'w_ffn1_out', 'w_in', 'w_out', 'w_cq', 'w_ckv', 'w_co', 'w_ffn2_in', 'w_ffn2_out')
TRANSPOSED = ('w_ffn1_in', 'w_in', 'w_ffn2_in')
GATHER_GROUPS = {"ffn1_up": ("w_ffn1_in",), "ffn1_dn": ("w_ffn1_out",), "mix": ("w_in", "w_out"),
                 "ca": ("w_cq", "w_ckv", "w_co"), "ffn2": ("w_ffn2_in", "w_ffn2_out")}
QKV_W = 3 * FOX_W
UV_OFF = QKV_W + FOX_HEADS


def kernel(x, mem, g_ffn1, w_ffn1_in, w_ffn1_out, g_mix, w_in, b_f, g_q, g_k, g_sgu, w_s, b_s, g_fox_o, g_gmlp_o, w_out, g_ca, g_mem, w_cq, w_ckv, g_cq, g_ck, w_co, g_ffn2, w_ffn2_in, w_ffn2_out, loss_target, m_g_ffn1, m_w_ffn1_in, m_w_ffn1_out, m_g_mix, m_w_in, m_b_f, m_g_q, m_g_k, m_g_sgu, m_w_s, m_b_s, m_g_fox_o, m_g_gmlp_o, m_w_out, m_g_ca, m_g_mem, m_w_cq, m_w_ckv, m_g_cq, m_g_ck, m_w_co, m_g_ffn2, m_w_ffn2_in, m_w_ffn2_out, v_g_ffn1, v_w_ffn1_in, v_w_ffn1_out, v_g_mix, v_w_in, v_b_f, v_g_q, v_g_k, v_g_sgu, v_w_s, v_b_s, v_g_fox_o, v_g_gmlp_o, v_w_out, v_g_ca, v_g_mem, v_w_cq, v_w_ckv, v_g_cq, v_g_ck, v_w_co, v_g_ffn2, v_w_ffn2_in, v_w_ffn2_out):
    args = dict(locals())
    w = {n: args[n] for n in WEIGHTS}
    mo = {n: args["m_" + n] for n in WEIGHTS}
    vo = {n: args["v_" + n] for n in WEIGHTS}
    D = D_MODEL

    def local(n, a):
        return a[0].T if n in TRANSPOSED else a[0]

    shards = [local(n, w[n]).astype(BF) for n in BIG]
    fb = shards[0].shape[0]
    g_snd, g_rcv, g_src, g_land, g_token = _copy_start("gather_start", shards, _place_own(shards, True), True)
    handles = {n: (g_src[i], g_land[i], g_snd[i], g_rcv[i]) for i, n in enumerate(BIG)}

    tiny_names = [n for n, _ in TINY_ROWS if n != "loss"]
    tiny_wmv =[_pack_tiny({n: a[n] for n in tiny_names}) + g_token[0:1, 0:1] for a in (w, mo, vo)]
    first_after = tiny_wmv[0][0:8] + tiny_wmv[1][0:8] + tiny_wmv[2][0:8]

    def weights(group, after):
        names = GATHER_GROUPS[group]
        hs = [handles[n] for n in names]
        got = _copy_wait("gather_wait_" + group, [h[0] for h in hs], [h[1] for h in hs], [h[2] for h in hs],
                         [h[3] for h in hs], first_after if group == "ffn1_up" else after, True)
        got = dict(zip(names, got))
        if group == "ffn1_up":
            return {"wup1": got["w_ffn1_in"].reshape(2, N_FFN_BLK, fb, D)}
        if group == "ffn1_dn":
            return {"wdn1": got["w_ffn1_out"].reshape(N_FFN_BLK, fb, D)}
        if group == "mix":
            full = got["w_in"].reshape(-1, D)
            wz = jnp.concatenate([full[:QKV_W], full[UV_OFF:], full[QKV_W:UV_OFF],
                                  jnp.zeros((LANES - FOX_HEADS, D), BF)], axis=0)
            return {"wz": wz, "wout": got["w_out"].reshape(D, D)}
        if group == "ca":
            return {"wcq": got["w_cq"].reshape(D, D), "wco": got["w_co"].reshape(D, D), "wckv": got["w_ckv"]}
        return {"wup2": got["w_ffn2_in"].reshape(2, N_FFN_BLK, fb, D),
                "wdn2": got["w_ffn2_out"].reshape(N_FFN_BLK, fb, D)}

    flying = {}

    def emit(group, g):
        if group == "w_s":
            part = [g["w_s"].reshape(-1, LANES)]
            *copies, token = _copy_start("w_s_start", part, _place_own(part, True), True)
            flying[group] = copies
            return token
        if group == "ffn2":
            parts = {"w_ffn2_in": g["wup2"], "w_ffn2_out": g["wdn2"].reshape(N_DEV, -1, D)}
        elif group == "ffn1_dn":
            parts = {"w_ffn1_out": g["wdn1"].reshape(N_DEV, -1, D)}
        elif group == "ffn1_up":
            parts = {"w_ffn1_in": g["wup1"]}
        else:
            gz = g["wz"]
            g_in = jnp.concatenate([gz[:QKV_W], gz[Z_F:Z_F + FOX_HEADS], gz[QKV_W:Z_F]], axis=0)
            parts = {"w_in": g_in.reshape(N_DEV, -1, D).astype(BF),
                     "w_out": g["wout"].reshape(N_DEV, -1, D), "w_cq": g["wcq"].reshape(N_DEV, -1, D),
                     "w_co": g["wco"].reshape(N_DEV, -1, D), "w_ckv": g["wckv"]}
        names = list(parts)
        srcs = [parts[n] for n in names]
        *copies, token = _copy_start("exchange_start_" + group, srcs, _place_own(srcs, False), False)
        flying[group] = (names, copies)
        return token

    small = {n: (w[n][0] if n == "b_s" else w[n]) for n in tiny_names}
    small["w_s"] = w["w_s"][0]

    sq, dx0, gs = _local_step(x[0], mem[0], loss_target[0], small, weights, emit)

    sm_parts = [_pack_tiny({**gs, "loss": sq[0:1]})]
    sm_snd, sm_rcv, sm_src, sm_land, sm_token = _copy_start("tiny_start", sm_parts, _place_own(sm_parts, True), True)

    grad, delta, new_m, new_v = {}, {}, {}, {}

    def update(group, after):
        names, (snd, rcv, srcs, lands) = flying[group]
        slots = _copy_wait("exchange_wait_" + group, srcs, lands, snd, rcv, after, False)
        for n, sl in zip(names, slots):
            g, d, m2, v2 = _adamw_big("adamw_" + n, sl, local(n, w[n]), local(n, mo[n]), local(n, vo[n]))
            grad[n], delta[n], new_m[n], new_v[n] = (
                (t.T if n in TRANSPOSED else t).reshape(w[n].shape) for t in (g, d, m2, v2))
        return d

    last = update("ffn2", sm_token)
    last = update("mid", last)
    last = update("ffn1_dn", last)
    last = update("ffn1_up", last)
    ws_snd, ws_rcv, ws_src, ws_land = flying["w_s"]
    ws_all, = _copy_wait("w_s_wait", ws_src, ws_land, ws_snd, ws_rcv, last, True)
    tiny_all, = _copy_wait("tiny_wait", sm_src, sm_land, sm_snd, sm_rcv, ws_all, True)
    ws_shape = w["w_s"].shape
    for store, t in zip((grad, delta, new_m, new_v), _adamw_big(
            "adamw_w_s", ws_all, *[a["w_s"].reshape(-1, LANES) for a in (w, mo, vo)])):
        store["w_s"] = t.reshape(ws_shape)
    shapes = {n: w[n].shape for n in tiny_names}
    shapes["loss"] = (1, LANES)
    for store, t in zip((grad, delta, new_m, new_v), _adamw_big(
            "adamw_tiny", tiny_all, *tiny_wmv)):
        store.update(_unpack_tiny(t, shapes))
    loss = grad["loss"][0, 0] * (0.5 / D)

    return (loss, dx0[None], *[grad[n] for n in WEIGHTS], *[delta[n] for n in WEIGHTS],
            *[new_m[n] for n in WEIGHTS], *[new_v[n] for n in WEIGHTS])
```

```python
import functools

import jax
import jax.numpy as jnp
from jax import lax
from jax.experimental import pallas as pl
from jax.experimental.pallas import tpu as pltpu

F32 = jnp.float32
BF = jnp.bfloat16
S = jax.ShapeDtypeStruct

N_DEV = 8
D_MODEL = 1024
FOX_HEADS, FOX_HD = 8, 64
FOX_W = 512
GMLP_G, GMLP_GD = 8, 64
GMLP_W = 512
CHUNK = 128
CA_HEADS, CA_HD = 4, 256
N_FFN_BLK = 4
ZW = 2688
Z_Q, Z_K, Z_V, Z_U, Z_G, Z_F = 0, 512, 1024, 1536, 2048, 2560
EPS = 1e-6
NEG = -1e30
LANES = 128

ADAM_LR, ADAM_B1, ADAM_B2, ADAM_EPS, ADAM_WD, ADAM_STEP = 0.001, 0.9, 0.999, 1e-08, 0.01, 10

VMEM_LIMIT = 52 * 2 ** 20


def _cp(n_axes):
    return pltpu.CompilerParams(dimension_semantics=("arbitrary",) * n_axes, vmem_limit_bytes=VMEM_LIMIT)


def _nn(a, b):
    return jnp.dot(a, b, preferred_element_type=F32)


def _nt(a, b):
    return lax.dot_general(a, b, (((1,), (1,)), ((), ())), preferred_element_type=F32)


def _tn(a, b):
    return lax.dot_general(a, b, (((0,), (0,)), ((), ())), preferred_element_type=F32)


def _hi(a, b):
    return jnp.dot(a, b, precision=lax.Precision.HIGHEST, preferred_element_type=F32)


def _rstd(x):
    return lax.rsqrt(jnp.mean(x * x, axis=-1, keepdims=True) + EPS)


def _norm_bwd(dy, x, g):
    r = _rstd(x)
    xh = x * r
    dxh = dy * g
    dx = r * (dxh - xh * jnp.mean(dxh * xh, axis=-1, keepdims=True))
    return dx, dy * xh


def _acc_rows(ref, first, val):
    srow = jnp.sum(val, axis=0, keepdims=True)

    @pl.when(first)
    def _():
        ref[...] = srow

    @pl.when(jnp.logical_not(first))
    def _():
        ref[...] += srow


def _gelu(x):
    c = 0.7978845608028654
    return 0.5 * x * (1.0 + jnp.tanh(c * (x + 0.044715 * x * x * x)))


def _gelu_grad(x):
    c = 0.7978845608028654
    t = jnp.tanh(c * (x + 0.044715 * x * x * x))
    return 0.5 * (1.0 + t) + 0.5 * x * (1.0 - t * t) * c * (1.0 + 3 * 0.044715 * x * x)


def _tile(n, pref):
    return pref if n % pref == 0 else n


def _ffn_up(name, x, g, wup):
    T, D = x.shape
    FB = wup.shape[-2]
    tm = _tile(T, 1024)

    def body(x_ref, g_ref, w_ref, a_ref, h_ref):
        @pl.when(pl.program_id(1) == 0)
        def _():
            xf = x_ref[...]
            h_ref[...] = (xf * _rstd(xf) * g_ref[...]).astype(BF)

        hb = h_ref[...]
        gg = _nt(hb, w_ref[0])
        uu = _nt(hb, w_ref[1])
        a_ref[...] = (gg * jax.nn.sigmoid(gg) * uu).astype(BF)

    return pl.pallas_call(
        body, name=name, grid=(T // tm, N_FFN_BLK),
        in_specs=[pl.BlockSpec((tm, D), lambda i, j: (i, 0)),
                  pl.BlockSpec((1, D), lambda i, j: (0, 0)),
                  pl.BlockSpec((2, None, FB, D), lambda i, j: (0, j, 0, 0))],
        out_specs=[pl.BlockSpec((None, tm, FB), lambda i, j: (j, i, 0)),
                   pl.BlockSpec((tm, D), lambda i, j: (i, 0))],
        out_shape=[S((N_FFN_BLK, T, FB), BF), S((T, D), BF)],
        compiler_params=_cp(2))(x, g, wup)


def _ffn_down(name, a, wdn, x):
    _, T, FB = a.shape
    D = x.shape[1]
    tm = _tile(T, 512)

    def body(a_ref, w_ref, x_ref, o_ref):
        p = _nn(a_ref[0], w_ref[0])
        for j in range(1, N_FFN_BLK):
            p = p + _nn(a_ref[j], w_ref[j])
        o_ref[...] = x_ref[...] + 0.5 * p

    return pl.pallas_call(
        body, name=name, grid=(T // tm,),
        in_specs=[pl.BlockSpec((N_FFN_BLK, tm, FB), lambda i: (0, i, 0)),
                  pl.BlockSpec((N_FFN_BLK, FB, D), lambda i: (0, 0, 0)),
                  pl.BlockSpec((tm, D), lambda i: (i, 0))],
        out_specs=pl.BlockSpec((tm, D), lambda i: (i, 0)),
        out_shape=S((T, D), F32),
        compiler_params=_cp(1))(a, wdn, x)


def _ffn_down_loss(name, a, wdn, x, target):
    _, T, FB = a.shape
    D = x.shape[1]
    tm = _tile(T, 512)

    def body(a_ref, w_ref, x_ref, t_ref, d_ref, db_ref, loss_ref):
        i = pl.program_id(0)
        p = _nn(a_ref[0], w_ref[0])
        for j in range(1, N_FFN_BLK):
            p = p + _nn(a_ref[j], w_ref[j])
        diff = (x_ref[...] + 0.5 * p) - t_ref[...]
        dy = diff * (1.0 / D)
        d_ref[...] = dy
        db_ref[...] = dy.astype(BF)
        sq = jnp.zeros((8, LANES), F32) + jnp.sum(diff * diff)

        @pl.when(i == 0)
        def _():
            loss_ref[...] = sq

        @pl.when(i > 0)
        def _():
            loss_ref[...] += sq

    row = pl.BlockSpec((tm, D), lambda i: (i, 0))
    return pl.pallas_call(
        body, name=name, grid=(T // tm,),
        in_specs=[pl.BlockSpec((N_FFN_BLK, tm, FB), lambda i: (0, i, 0)),
                  pl.BlockSpec((N_FFN_BLK, FB, D), lambda i: (0, 0, 0)), row, row],
        out_specs=[row, row, pl.BlockSpec((8, LANES), lambda i: (0, 0))],
        out_shape=[S((T, D), F32), S((T, D), BF), S((8, LANES), F32)],
        compiler_params=_cp(1))(a, wdn, x, target)


def _ffn_bwd_act(name, dyb, h, wup, wdn):
    T, D = h.shape
    FB = wup.shape[-2]
    tm = _tile(T, 1024)

    def body(d_ref, h_ref, wu_ref, wd_ref, o_ref):
        da = 0.5 * _nt(d_ref[...], wd_ref[...])
        hb = h_ref[...]
        gg = _nt(hb, wu_ref[0])
        uu = _nt(hb, wu_ref[1])
        sg = jax.nn.sigmoid(gg)
        o_ref[0] = (da * uu * (sg * (1.0 + gg * (1.0 - sg)))).astype(BF)
        o_ref[1] = (da * (gg * sg)).astype(BF)

    return pl.pallas_call(
        body, name=name, grid=(T // tm, N_FFN_BLK),
        in_specs=[pl.BlockSpec((tm, D), lambda i, j: (i, 0)),
                  pl.BlockSpec((tm, D), lambda i, j: (i, 0)),
                  pl.BlockSpec((2, None, FB, D), lambda i, j: (0, j, 0, 0)),
                  pl.BlockSpec((None, FB, D), lambda i, j: (j, 0, 0))],
        out_specs=pl.BlockSpec((2, None, tm, FB), lambda i, j: (0, j, i, 0)),
        out_shape=S((2, N_FFN_BLK, T, FB), BF),
        compiler_params=_cp(2))(dyb, h, wup, wdn)


def _ffn_dx(name, dgu, wup, x, g, dy):
    T, D = x.shape
    FB = wup.shape[-2]
    tm = _tile(T, 1024)

    def body(d_ref, w_ref, x_ref, g_ref, dy_ref, dx_ref, dg_ref, acc_ref):
        i, j = pl.program_id(0), pl.program_id(1)
        p = _nn(d_ref[0], w_ref[0]) + _nn(d_ref[1], w_ref[1])

        @pl.when(j == 0)
        def _():
            acc_ref[...] = p

        @pl.when(j > 0)
        def _():
            acc_ref[...] += p

        @pl.when(j == N_FFN_BLK - 1)
        def _():
            dx, dgr = _norm_bwd(acc_ref[...], x_ref[...], g_ref[...])
            dx_ref[...] = dx + dy_ref[...]
            _acc_rows(dg_ref, i == 0, dgr)

    return pl.pallas_call(
        body, name=name, grid=(T // tm, N_FFN_BLK),
        in_specs=[pl.BlockSpec((2, None, tm, FB), lambda i, j: (0, j, i, 0)),
                  pl.BlockSpec((2, None, FB, D), lambda i, j: (0, j, 0, 0)),
                  pl.BlockSpec((tm, D), lambda i, j: (i, 0)),
                  pl.BlockSpec((1, D), lambda i, j: (0, 0)),
                  pl.BlockSpec((tm, D), lambda i, j: (i, 0))],
        out_specs=[pl.BlockSpec((tm, D), lambda i, j: (i, 0)),
                   pl.BlockSpec((1, D), lambda i, j: (0, 0))],
        out_shape=[S((T, D), F32), S((1, D), F32)],
        scratch_shapes=[pltpu.VMEM((tm, D), F32)],
        compiler_params=_cp(2))(dgu, wup, x, g, dy)


def _tn_matmul(name, a, a_spec, b, b_spec, out_shape, out_spec, grid, acc_shape, scale=1.0, after=None):
    nk = grid[1]
    extra = [] if after is None else [after]

    def body(a_ref, b_ref, *rest):
        o_ref, acc_ref = rest[-2:]
        k = pl.program_id(1)
        p = _tn(a_ref[...], b_ref[...])

        @pl.when(k == 0)
        def _():
            acc_ref[...] = p

        @pl.when(k > 0)
        def _():
            acc_ref[...] += p

        @pl.when(k == nk - 1)
        def _():
            o_ref[...] = (acc_ref[...] * scale).astype(o_ref.dtype)

    return pl.pallas_call(
        body, name=name, grid=grid,
        in_specs=[a_spec, b_spec] + [pl.BlockSpec((8, LANES), lambda j, k: (0, 0)) for _ in extra],
        out_specs=out_spec, out_shape=out_shape,
        scratch_shapes=[pltpu.VMEM(acc_shape, F32)], compiler_params=_cp(2))(a, b, *extra)


def _ffn_dwup(name, h, dgu, after=None):
    T, D = h.shape
    FB = dgu.shape[-1]
    tk = _tile(T, 1024)
    return _tn_matmul(
        name + "_dwup", dgu.reshape(2 * N_FFN_BLK, T, FB), pl.BlockSpec((None, tk, FB), lambda j, k: (j, k, 0)),
        h, pl.BlockSpec((tk, D), lambda j, k: (k, 0)),
        S((2 * N_FFN_BLK, FB, D), BF), pl.BlockSpec((None, FB, D), lambda j, k: (j, 0, 0)),
        (2 * N_FFN_BLK, T // tk), (FB, D), after=after)


def _ffn_dwdn(name, a, dyb):
    _, T, FB = a.shape
    D = dyb.shape[1]
    tk = _tile(T, 1024)
    return _tn_matmul(
        name + "_dwdn", a, pl.BlockSpec((None, tk, FB), lambda j, k: (j, k, 0)),
        dyb, pl.BlockSpec((tk, D), lambda j, k: (k, 0)),
        S((N_FFN_BLK, FB, D), BF), pl.BlockSpec((None, FB, D), lambda j, k: (j, 0, 0)),
        (N_FFN_BLK, T // tk), (FB, D), scale=0.5)


def _mix_proj(x, g, wz):
    T, D = x.shape
    tm = _tile(T, 512)

    def body(x_ref, g_ref, w_ref, z_ref, h_ref):
        xf = x_ref[...]
        hb = (xf * _rstd(xf) * g_ref[...]).astype(BF)
        h_ref[...] = hb
        z_ref[...] = _nt(hb, w_ref[...])

    return pl.pallas_call(
        body, name="mix_proj", grid=(T // tm,),
        in_specs=[pl.BlockSpec((tm, D), lambda i: (i, 0)),
                  pl.BlockSpec((1, D), lambda i: (0, 0)),
                  pl.BlockSpec((ZW, D), lambda i: (0, 0))],
        out_specs=[pl.BlockSpec((tm, ZW), lambda i: (i, 0)),
                   pl.BlockSpec((tm, D), lambda i: (i, 0))],
        out_shape=[S((T, ZW), F32), S((T, D), BF)],
        compiler_params=_cp(1))(x, g, wz)


def _tri(n, lower):
    r = lax.broadcasted_iota(jnp.int32, (n, n), 0)
    c = lax.broadcasted_iota(jnp.int32, (n, n), 1)
    return (r >= c) if lower else (r <= c)


def _spatial_mix(vgn_b, ws_ref, bst, tm):
    tril = _tri(CHUNK, True)
    wms = [jnp.where(tril, ws_ref[g], 0.0).astype(BF) for g in range(GMLP_G)]
    rows = []
    for c in range(tm // CHUNK):
        cols = []
        for g in range(GMLP_G):
            vs = vgn_b[c * CHUNK:(c + 1) * CHUNK, g * GMLP_GD:(g + 1) * GMLP_GD]
            cols.append(_nn(wms[g], vs) + bst[:, g:g + 1])
        rows.append(jnp.concatenate(cols, axis=1))
    return jnp.concatenate(rows, axis=0), wms


HB = 128
AUG_W = FOX_HEADS * HB
COL_A, COL_B, COL_C = 64, 67, 70


def _spread_matrix():
    r = jnp.arange(FOX_W)
    return (jnp.arange(AUG_W)[None, :] == ((r // FOX_HD) * HB + r % FOX_HD)[:, None]).astype(BF)


def _piece_matrix(col):
    r = jnp.arange(LANES)
    dst = jnp.where(r < 3 * FOX_HEADS, (r % FOX_HEADS) * HB + col + r // FOX_HEADS, -1)
    return (jnp.arange(AUG_W)[None, :] == dst[:, None]).astype(BF)


def _ones_row(cols):
    c = jnp.arange(AUG_W) % HB
    hit = functools.reduce(jnp.logical_or, [(c >= a) & (c < a + 3) for a in cols])
    return hit.astype(F32)[None, :]


def _pieces(x):
    lane = lax.broadcasted_iota(jnp.int32, x.shape, 1)
    x = jnp.where(lane < FOX_HEADS, x, 0.0)
    hi = x.astype(BF).astype(F32)
    r1 = x - hi
    mid = r1.astype(BF).astype(F32)
    lo = (r1 - mid).astype(BF).astype(F32)
    return (hi + pltpu.roll(mid, FOX_HEADS, 1) + pltpu.roll(lo, 2 * FOX_HEADS, 1)).astype(BF)


def _mix_prep(z, bf128, g_q, g_k, g_sgu, w_s, b_st, g_go):
    T = z.shape[0]
    tm = _tile(T, 512)
    spread, pc_q, pc_k = _spread_matrix(), _piece_matrix(COL_A), _piece_matrix(COL_B)
    one_q, one_k, one_v = _ones_row([COL_B]), _ones_row([COL_A, COL_C]), _ones_row([COL_A])

    def body(z_ref, bf_ref, gq_ref, gk_ref, gs_ref, ws_ref, bst_ref, go_ref, sp_ref, pq_ref, pk_ref, oq_ref, ok_ref,
             ov_ref, q_ref, k_ref, v_ref, y_ref, carry_ref, qn_sc, kn_sc):
        i = pl.program_id(0)

        @pl.when(i == 0)
        def _():
            carry_ref[...] = jnp.zeros_like(carry_ref)

        for h in range(FOX_HEADS):
            hs = slice(h * FOX_HD, (h + 1) * FOX_HD)
            qh = z_ref[:, Z_Q + h * FOX_HD:Z_Q + (h + 1) * FOX_HD]
            kh = z_ref[:, Z_K + h * FOX_HD:Z_K + (h + 1) * FOX_HD]
            qn_sc[:, hs] = (qh * _rstd(qh) * gq_ref[...] * 0.125).astype(BF)
            kn_sc[:, hs] = (kh * _rstd(kh) * gk_ref[...]).astype(BF)

        fl = z_ref[:, Z_F:Z_F + LANES] + bf_ref[...]
        logf = jnp.minimum(fl, 0.0) - jnp.log1p(jnp.exp(-jnp.abs(fl)))
        csum = _hi(_tri(tm, True).astype(F32), logf) + carry_ref[...]
        carry_ref[...] = csum[tm - 1:tm, :]
        sp = sp_ref[...]
        q_ref[...] = (_nn(qn_sc[...], sp) + _nn(_pieces(csum), pq_ref[...]) + oq_ref[...]).astype(BF)
        k_ref[...] = (_nn(kn_sc[...], sp) + _nn(_pieces(-csum), pk_ref[...]) + ok_ref[...]).astype(BF)
        v_ref[...] = (_nn(z_ref[:, Z_V:Z_V + FOX_W].astype(BF), sp) + ov_ref[...]).astype(BF)

        u = _gelu(z_ref[:, Z_U:Z_U + GMLP_W])
        vg = _gelu(z_ref[:, Z_G:Z_G + GMLP_W])
        vgn = (vg * _rstd(vg) * gs_ref[...]).astype(BF)
        mixed, _ = _spatial_mix(vgn, ws_ref, bst_ref[...], tm)
        sgu = u * mixed
        y_ref[...] = (sgu * _rstd(sgu) * go_ref[...]).astype(BF)

    row = lambda i: (i, 0)
    fix2 = lambda i: (0, 0)
    return pl.pallas_call(
        body, name="mix_prep", grid=(T // tm,),
        in_specs=[pl.BlockSpec((tm, ZW), row),
                  pl.BlockSpec((1, LANES), fix2), pl.BlockSpec((1, FOX_HD), fix2), pl.BlockSpec((1, FOX_HD), fix2),
                  pl.BlockSpec((1, GMLP_W), fix2), pl.BlockSpec((GMLP_G, CHUNK, CHUNK), lambda i: (0, 0, 0)),
                  pl.BlockSpec((CHUNK, GMLP_G), fix2), pl.BlockSpec((1, GMLP_W), fix2),
                  pl.BlockSpec((FOX_W, AUG_W), fix2), pl.BlockSpec((LANES, AUG_W), fix2),
                  pl.BlockSpec((LANES, AUG_W), fix2), pl.BlockSpec((1, AUG_W), fix2), pl.BlockSpec((1, AUG_W), fix2),
                  pl.BlockSpec((1, AUG_W), fix2)],
        out_specs=[pl.BlockSpec((tm, AUG_W), row), pl.BlockSpec((tm, AUG_W), row), pl.BlockSpec((tm, AUG_W), row),
                   pl.BlockSpec((tm, GMLP_W), row)],
        out_shape=[S((T, AUG_W), BF), S((T, AUG_W), BF), S((T, AUG_W), BF), S((T, GMLP_W), BF)],
        scratch_shapes=[pltpu.VMEM((1, LANES), F32), pltpu.VMEM((tm, FOX_W), BF), pltpu.VMEM((tm, FOX_W), BF)],
        compiler_params=_cp(1))(z, bf128, g_q, g_k, g_sgu, w_s, b_st, g_go, spread, pc_q, pc_k, one_q, one_k, one_v)


def _fox_fwd(q, k, v):
    T = q.shape[0]
    tq = _tile(T, 1024)
    nq = T // tq

    def body(q_ref, k_ref, v_ref, o_ref, lse_ref, m_sc, acc_sc):
        i, j = pl.program_id(0), pl.program_id(1)

        @pl.when(j == 0)
        def _():
            m_sc[...] = jnp.full(m_sc.shape, NEG, F32)
            acc_sc[...] = jnp.zeros_like(acc_sc)

        def step(masked):
            mask = _tri(tq, True) if masked else None
            for h in range(FOX_HEADS):
                hb = slice(h * HB, (h + 1) * HB)
                s = _nt(q_ref[:, hb], k_ref[:, hb])
                if masked:
                    s = jnp.where(mask, s, NEG)
                m_prev = m_sc[h]
                m_new = jnp.maximum(m_prev, jnp.broadcast_to(jnp.max(s, axis=1, keepdims=True), (tq, HB)))
                p = jnp.exp(s - jnp.tile(m_new, (1, tq // HB))).astype(BF)
                acc_sc[:, hb] = jnp.exp(m_prev - m_new) * acc_sc[:, hb] + _nn(p, v_ref[:, hb])
                m_sc[h] = m_new

        @pl.when(j < i)
        def _():
            step(False)

        @pl.when(j == i)
        def _():
            step(True)
            lse_ref[...] = jnp.zeros_like(lse_ref)
            for h in range(FOX_HEADS):
                l = acc_sc[:, h * HB + COL_A:h * HB + COL_A + 1]
                o_ref[:, h * FOX_HD:(h + 1) * FOX_HD] = acc_sc[:, h * HB:h * HB + FOX_HD] / l
                lse_ref[:, h:h + 1] = m_sc[h][:, 0:1] + jnp.log(l)

    qi = lambda i, j: (i, 0)
    kj = lambda i, j: (jnp.minimum(i, j), 0)
    return pl.pallas_call(
        body, name="fox_fwd", grid=(nq, nq),
        in_specs=[pl.BlockSpec((tq, AUG_W), qi), pl.BlockSpec((tq, AUG_W), kj), pl.BlockSpec((tq, AUG_W), kj)],
        out_specs=[pl.BlockSpec((tq, FOX_W), qi), pl.BlockSpec((tq, LANES), qi)],
        out_shape=[S((T, FOX_W), F32), S((T, LANES), F32)],
        scratch_shapes=[pltpu.VMEM((FOX_HEADS, tq, HB), F32), pltpu.VMEM((tq, AUG_W), F32)],
        compiler_params=_cp(2))(q, k, v)


def _fox_bwd(q, k, v, dob):
    T = q.shape[0]
    tq = _tile(T, 512)
    nq = T // tq
    half = AUG_W // 2
    hpg = FOX_HEADS // 2

    def body(q_ref, k_ref, v_ref, do_ref, dq_ref, dk_ref, dv_ref, dq_sc):
        j, i = pl.program_id(1), pl.program_id(2)

        @pl.when(jnp.logical_and(i == 0, j == 0))
        def _():
            dq_sc[...] = jnp.zeros_like(dq_sc)

        @pl.when(i == 0)
        def _():
            dk_ref[...] = jnp.zeros_like(dk_ref)
            dv_ref[...] = jnp.zeros_like(dv_ref)

        def step(masked):
            rows = pl.ds(pl.multiple_of(i * tq, tq), tq)
            mask = _tri(tq, True) if masked else None
            for h in range(hpg):
                hb = slice(h * HB, (h + 1) * HB)
                qh, kh, vh, doh = q_ref[:, hb], k_ref[:, hb], v_ref[:, hb], do_ref[:, hb]
                s = _nt(qh, kh)
                if masked:
                    s = jnp.where(mask, s, NEG)
                p = jnp.exp(s)
                dsb = (p * _nt(doh, vh)).astype(BF)
                dv_ref[:, hb] += _tn(p.astype(BF), doh)
                dk_ref[:, hb] += _tn(dsb, qh)
                dq_sc[rows, hb] += _nn(dsb, kh)

        @pl.when(i > j)
        def _():
            step(False)

        @pl.when(i == j)
        def _():
            step(True)
            dq_ref[...] = dq_sc[pl.ds(pl.multiple_of(j * tq, tq), tq), :]

    qi = lambda g, j, i: (jnp.maximum(i, j), g)
    kj = lambda g, j, i: (j, g)
    return pl.pallas_call(
        body, name="fox_bwd", grid=(2, nq, nq),
        in_specs=[pl.BlockSpec((tq, half), qi), pl.BlockSpec((tq, half), kj), pl.BlockSpec((tq, half), kj),
                  pl.BlockSpec((tq, half), qi)],
        out_specs=[pl.BlockSpec((tq, half), kj), pl.BlockSpec((tq, half), kj), pl.BlockSpec((tq, half), kj)],
        out_shape=[S((T, AUG_W), F32), S((T, AUG_W), F32), S((T, AUG_W), F32)],
        scratch_shapes=[pltpu.VMEM((T, half), F32)],
        compiler_params=_cp(3))(q, k, v, dob)


def _mix_out(attn, yg, g_fo, wout, x):
    T, D = x.shape
    tm = _tile(T, 512)

    def body(a_ref, y_ref, g_ref, w_ref, x_ref, o_ref):
        at = a_ref[...]
        yf = (at * _rstd(at) * g_ref[...]).astype(BF)
        o_ref[...] = x_ref[...] + _nn(yf, w_ref[:FOX_W, :]) + _nn(y_ref[...], w_ref[FOX_W:, :])

    row = lambda i: (i, 0)
    return pl.pallas_call(
        body, name="mix_out", grid=(T // tm,),
        in_specs=[pl.BlockSpec((tm, FOX_W), row), pl.BlockSpec((tm, GMLP_W), row),
                  pl.BlockSpec((1, FOX_W), lambda i: (0, 0)), pl.BlockSpec((D, D), lambda i: (0, 0)),
                  pl.BlockSpec((tm, D), row)],
        out_specs=pl.BlockSpec((tm, D), row),
        out_shape=S((T, D), F32),
        compiler_params=_cp(1))(attn, yg, g_fo, wout, x)


def _mix_out_bwd(dx, attn, yg, g_fo, wout, qf, lse):
    T, D = dx.shape
    tm = _tile(T, 512)
    n = T // tm
    spread, pc_l, pc_d = _spread_matrix(), _piece_matrix(COL_C), _piece_matrix(COL_A)

    def body(dx_ref, a_ref, y_ref, g_ref, w_ref, qf_ref, lse_ref, sp_ref, pl_ref, pd_ref,
             qb_ref, dob_ref, dyg_ref, dw_ref, dg_ref, acc_ref, dsum_ref):
        i = pl.program_id(0)
        dxb = dx_ref[...].astype(BF)
        at = a_ref[...]
        yf = (at * _rstd(at) * g_ref[...]).astype(BF)
        dy = _nt(dxb, w_ref[...])
        p_top = _tn(yf, dxb)
        p_bot = _tn(y_ref[...], dxb)

        @pl.when(i == 0)
        def _():
            acc_ref[:FOX_W, :] = p_top
            acc_ref[FOX_W:, :] = p_bot

        @pl.when(i > 0)
        def _():
            acc_ref[:FOX_W, :] += p_top
            acc_ref[FOX_W:, :] += p_bot

        @pl.when(i == n - 1)
        def _():
            dw_ref[...] = acc_ref[...].astype(BF)

        dat, dgr = _norm_bwd(dy[:, :FOX_W], at, g_ref[...])
        _acc_rows(dg_ref, i == 0, dgr)
        dyg_ref[...] = dy[:, FOX_W:]
        prod = dat * at
        dsum_ref[...] = jnp.zeros_like(dsum_ref)
        for h in range(FOX_HEADS):
            dsum_ref[:, h:h + 1] = jnp.sum(prod[:, h * FOX_HD:(h + 1) * FOX_HD], axis=1, keepdims=True)
        dob_ref[...] = (_nn(dat.astype(BF), sp_ref[...]) + _nn(_pieces(-dsum_ref[...]), pd_ref[...])).astype(BF)
        qb_ref[...] = (qf_ref[...].astype(F32) + _nn(_pieces(-lse_ref[...]), pl_ref[...])).astype(BF)

    row = lambda i: (i, 0)
    fix = lambda i: (0, 0)
    return pl.pallas_call(
        body, name="mix_out_bwd", grid=(n,),
        in_specs=[pl.BlockSpec((tm, D), row), pl.BlockSpec((tm, FOX_W), row), pl.BlockSpec((tm, GMLP_W), row),
                  pl.BlockSpec((1, FOX_W), fix), pl.BlockSpec((D, D), fix), pl.BlockSpec((tm, AUG_W), row),
                  pl.BlockSpec((tm, LANES), row), pl.BlockSpec((FOX_W, AUG_W), fix), pl.BlockSpec((LANES, AUG_W), fix),
                  pl.BlockSpec((LANES, AUG_W), fix)],
        out_specs=[pl.BlockSpec((tm, AUG_W), row), pl.BlockSpec((tm, AUG_W), row), pl.BlockSpec((tm, GMLP_W), row),
                   pl.BlockSpec((D, D), fix), pl.BlockSpec((1, FOX_W), fix)],
        out_shape=[S((T, AUG_W), BF), S((T, AUG_W), BF), S((T, GMLP_W), F32), S((D, D), BF), S((1, FOX_W), F32)],
        scratch_shapes=[pltpu.VMEM((D, D), F32), pltpu.VMEM((tm, LANES), F32)],
        compiler_params=_cp(1))(dx, attn, yg, g_fo, wout, qf, lse, spread, pc_l, pc_d)


def _mix_prep_bwd(z, dq, dk, dv, dyg, bf128, g_q, g_k, g_sgu, w_s, b_st, g_go):
    T = z.shape[0]
    tm = _tile(T, 512)
    n = T // tm

    def body(z_ref, dq_ref, dk_ref, dv_ref, dyg_ref, bf_ref, gq_ref, gk_ref, gs_ref, ws_ref,
             bst_ref, go_ref, dz_ref, dgq_ref, dgk_ref, dgs_ref, dgo_ref, dws_ref, dbst_ref, dbf_ref, carry_ref):
        i = pl.program_id(0)
        first = i == 0

        @pl.when(first)
        def _():
            carry_ref[...] = jnp.zeros_like(carry_ref)

        lane = lax.broadcasted_iota(jnp.int32, (tm, LANES), 1)
        dc = jnp.zeros((tm, LANES), F32)
        gq_rows, gk_rows = [], []
        for h in range(FOX_HEADS):
            hp = slice(h * HB, h * HB + FOX_HD)
            dqh, gqr = _norm_bwd(dq_ref[:, hp] * 0.125, z_ref[:, Z_Q + h * FOX_HD:Z_Q + (h + 1) * FOX_HD], gq_ref[...])
            dkh, gkr = _norm_bwd(dk_ref[:, hp], z_ref[:, Z_K + h * FOX_HD:Z_K + (h + 1) * FOX_HD], gk_ref[...])
            dz_ref[:, Z_Q + h * FOX_HD:Z_Q + (h + 1) * FOX_HD] = dqh.astype(BF)
            dz_ref[:, Z_K + h * FOX_HD:Z_K + (h + 1) * FOX_HD] = dkh.astype(BF)
            dz_ref[:, Z_V + h * FOX_HD:Z_V + (h + 1) * FOX_HD] = dv_ref[:, hp].astype(BF)
            dch = dq_ref[:, h * HB + COL_A:h * HB + COL_A + 1] - dk_ref[:, h * HB + COL_B:h * HB + COL_B + 1]
            dc = jnp.where(lane == h, dch, dc)
            gq_rows.append(gqr)
            gk_rows.append(gkr)
        _acc_rows(dgq_ref, first, functools.reduce(lambda a, b: a + b, gq_rows))
        _acc_rows(dgk_ref, first, functools.reduce(lambda a, b: a + b, gk_rows))

        dlogf = _hi(_tri(tm, False).astype(F32), dc) + carry_ref[...]
        carry_ref[...] = dlogf[0:1, :]
        fl = z_ref[:, Z_F:Z_F + LANES] + bf_ref[...]
        lane = lax.broadcasted_iota(jnp.int32, (tm, LANES), 1)
        df = jnp.where(lane < FOX_HEADS, dlogf * jax.nn.sigmoid(-fl), 0.0)
        dz_ref[:, Z_F:Z_F + LANES] = df.astype(BF)
        _acc_rows(dbf_ref, first, df)

        u_pre = z_ref[:, Z_U:Z_U + GMLP_W]
        vg_pre = z_ref[:, Z_G:Z_G + GMLP_W]
        u = _gelu(u_pre)
        vg = _gelu(vg_pre)
        vgn = (vg * _rstd(vg) * gs_ref[...]).astype(BF)
        bst = bst_ref[...]
        mixed, wms = _spatial_mix(vgn, ws_ref, bst, tm)
        sgu = u * mixed
        dsgu, gor = _norm_bwd(dyg_ref[...], sgu, go_ref[...])
        _acc_rows(dgo_ref, first, gor)
        du = dsgu * mixed
        dmixed = dsgu * u
        dmb = dmixed.astype(BF)
        tril = _tri(CHUNK, True)
        dvgn_rows = []
        dws = [None] * GMLP_G
        dbs = [None] * GMLP_G
        for c in range(tm // CHUNK):
            cs = slice(c * CHUNK, (c + 1) * CHUNK)
            cols = []
            for g in range(GMLP_G):
                gs = slice(g * GMLP_GD, (g + 1) * GMLP_GD)
                dmc = dmb[cs, gs]
                pw = _nt(dmc, vgn[cs, gs])
                pb = jnp.sum(dmixed[cs, gs], axis=1, keepdims=True)
                dws[g] = pw if dws[g] is None else dws[g] + pw
                dbs[g] = pb if dbs[g] is None else dbs[g] + pb
                cols.append(_tn(wms[g], dmc))
            dvgn_rows.append(jnp.concatenate(cols, axis=1))
        dvgn = jnp.concatenate(dvgn_rows, axis=0)
        dbs_t = jnp.concatenate(dbs, axis=1)
        for g in range(GMLP_G):
            dwg = jnp.where(tril, dws[g], 0.0)

            @pl.when(first)
            def _():
                dws_ref[g] = dwg

            @pl.when(jnp.logical_not(first))
            def _():
                dws_ref[g] += dwg

        @pl.when(first)
        def _():
            dbst_ref[...] = dbs_t

        @pl.when(jnp.logical_not(first))
        def _():
            dbst_ref[...] += dbs_t

        dvg, gsr = _norm_bwd(dvgn, vg, gs_ref[...])
        _acc_rows(dgs_ref, first, gsr)
        dz_ref[:, Z_U:Z_U + GMLP_W] = (du * _gelu_grad(u_pre)).astype(BF)
        dz_ref[:, Z_G:Z_G + GMLP_W] = (dvg * _gelu_grad(vg_pre)).astype(BF)

    rev = lambda i: (n - 1 - i, 0)
    fix = lambda i: (0, 0)
    fix3 = lambda i: (0, 0, 0)
    return pl.pallas_call(
        body, name="mix_prep_bwd", grid=(n,),
        in_specs=[pl.BlockSpec((tm, ZW), rev), pl.BlockSpec((tm, AUG_W), rev), pl.BlockSpec((tm, AUG_W), rev),
                  pl.BlockSpec((tm, AUG_W), rev), pl.BlockSpec((tm, GMLP_W), rev),
                  pl.BlockSpec((1, LANES), fix), pl.BlockSpec((1, FOX_HD), fix), pl.BlockSpec((1, FOX_HD), fix),
                  pl.BlockSpec((1, GMLP_W), fix), pl.BlockSpec((GMLP_G, CHUNK, CHUNK), fix3),
                  pl.BlockSpec((CHUNK, GMLP_G), fix), pl.BlockSpec((1, GMLP_W), fix)],
        out_specs=[pl.BlockSpec((tm, ZW), rev), pl.BlockSpec((1, FOX_HD), fix), pl.BlockSpec((1, FOX_HD), fix),
                   pl.BlockSpec((1, GMLP_W), fix), pl.BlockSpec((1, GMLP_W), fix),
                   pl.BlockSpec((GMLP_G, CHUNK, CHUNK), fix3), pl.BlockSpec((CHUNK, GMLP_G), fix),
                   pl.BlockSpec((1, LANES), fix)],
        out_shape=[S((T, ZW), BF), S((1, FOX_HD), F32), S((1, FOX_HD), F32), S((1, GMLP_W), F32), S((1, GMLP_W), F32),
                   S((GMLP_G, CHUNK, CHUNK), F32), S((CHUNK, GMLP_G), F32), S((1, LANES), F32)],
        scratch_shapes=[pltpu.VMEM((1, LANES), F32)],
        compiler_params=_cp(1))(z, dq, dk, dv, dyg, bf128, g_q, g_k, g_sgu, w_s, b_st, g_go)


def _mix_proj_bwd(dz, wz, x, g, dy):
    T, D = x.shape
    tm = _tile(T, 512)

    def body(dz_ref, w_ref, x_ref, g_ref, dy_ref, dx_ref, dxb_ref, dg_ref):
        dh = _nn(dz_ref[...], w_ref[...])
        dx, dgr = _norm_bwd(dh, x_ref[...], g_ref[...])
        dx = dx + dy_ref[...]
        dx_ref[...] = dx
        dxb_ref[...] = dx.astype(BF)
        _acc_rows(dg_ref, pl.program_id(0) == 0, dgr)

    row = lambda i: (i, 0)
    fix = lambda i: (0, 0)
    return pl.pallas_call(
        body, name="mix_proj_bwd", grid=(T // tm,),
        in_specs=[pl.BlockSpec((tm, ZW), row), pl.BlockSpec((ZW, D), fix), pl.BlockSpec((tm, D), row),
                  pl.BlockSpec((1, D), fix), pl.BlockSpec((tm, D), row)],
        out_specs=[pl.BlockSpec((tm, D), row), pl.BlockSpec((tm, D), row), pl.BlockSpec((1, D), fix)],
        out_shape=[S((T, D), F32), S((T, D), BF), S((1, D), F32)],
        compiler_params=_cp(1))(dz, wz, x, g, dy)


def _ca_kv(mem, g_mem, wckv, g_ck):
    M, D = mem.shape

    def body(m_ref, g_ref, w_ref, gk_ref, mn_ref, kr_ref, kn_ref, v_ref):
        mf = m_ref[...]
        mn = (mf * _rstd(mf) * g_ref[...]).astype(BF)
        mn_ref[...] = mn
        for h in range(CA_HEADS):
            kr = _nn(mn, w_ref[h])
            kr_ref[h] = kr
            kn_ref[h] = (kr * _rstd(kr) * gk_ref[...]).astype(BF)
            v_ref[h] = _nn(mn, w_ref[CA_HEADS + h]).astype(BF)

    hd = (CA_HEADS, M, CA_HD)
    return pl.pallas_call(
        body, name="ca_kv", out_shape=[S((M, D), BF), S(hd, F32), S(hd, BF), S(hd, BF)],
        compiler_params=pltpu.CompilerParams(vmem_limit_bytes=VMEM_LIMIT))(mem, g_mem, wckv, g_ck)


def _ca_tile_fwd(xt, gca, wcq, gcq, kn_ref, v_ref):
    hb = (xt * _rstd(xt) * gca).astype(BF)
    qc = _nn(hb, wcq)
    qr, qn, ps = [], [], []
    for h in range(CA_HEADS):
        qh = qc[:, h * CA_HD:(h + 1) * CA_HD]
        qnh = (qh * _rstd(qh) * gcq * 0.0625).astype(BF)
        s = _nt(qnh, kn_ref[h])
        e = jnp.exp(s - jnp.max(s, axis=1, keepdims=True))
        ps.append(e / jnp.sum(e, axis=1, keepdims=True))
        qr.append(qh)
        qn.append(qnh)
    return hb, qr, qn, ps


def _ca_fwd(x, g_ca, wcq, g_cq, kn, vv, wco):
    T, D = x.shape
    M = kn.shape[1]
    tm = _tile(T, 512)

    def body(x_ref, gca_ref, wcq_ref, gcq_ref, kn_ref, v_ref, wco_ref, o_ref, ob_sc):
        xt = x_ref[...]
        _, _, _, ps = _ca_tile_fwd(xt, gca_ref[...], wcq_ref[...], gcq_ref[...], kn_ref, v_ref)
        for h in range(CA_HEADS):
            ob_sc[:, h * CA_HD:(h + 1) * CA_HD] = _nn(ps[h].astype(BF), v_ref[h]).astype(BF)
        o_ref[...] = xt + _nn(ob_sc[...], wco_ref[...])

    row = lambda i: (i, 0)
    fix = lambda i: (0, 0)
    fix3 = lambda i: (0, 0, 0)
    return pl.pallas_call(
        body, name="ca_fwd", grid=(T // tm,),
        in_specs=[pl.BlockSpec((tm, D), row), pl.BlockSpec((1, D), fix), pl.BlockSpec((D, D), fix),
                  pl.BlockSpec((1, CA_HD), fix), pl.BlockSpec((CA_HEADS, M, CA_HD), fix3),
                  pl.BlockSpec((CA_HEADS, M, CA_HD), fix3), pl.BlockSpec((D, D), fix)],
        out_specs=pl.BlockSpec((tm, D), row), out_shape=S((T, D), F32),
        scratch_shapes=[pltpu.VMEM((tm, D), BF)],
        compiler_params=_cp(1))(x, g_ca, wcq, g_cq, kn, vv, wco)


def _ca_bwd(x, dy, g_ca, wcq, g_cq, kn, vv, wco):
    T, D = x.shape
    M = kn.shape[1]
    tm = _tile(T, 512)
    n = T // tm

    def body(x_ref, dy_ref, gca_ref, wcq_ref, gcq_ref, kn_ref, v_ref, wco_ref,
             dx_ref, dwq_ref, dwo_ref, dkn_ref, dv_ref, dgcq_ref, dgca_ref, aq_sc, ao_sc, ob_sc, dq_sc):
        i = pl.program_id(0)
        first = i == 0
        xt = x_ref[...]
        dyt = dy_ref[...]
        dyb = dyt.astype(BF)
        hb, qr, qn, ps = _ca_tile_fwd(xt, gca_ref[...], wcq_ref[...], gcq_ref[...], kn_ref, v_ref)
        do = _nt(dyb, wco_ref[...])
        gcq_rows = None
        for h in range(CA_HEADS):
            hs = slice(h * CA_HD, (h + 1) * CA_HD)
            p = ps[h]
            pb = p.astype(BF)
            ob_sc[:, hs] = _nn(pb, v_ref[h]).astype(BF)
            doh = do[:, hs].astype(BF)
            dp = _nt(doh, v_ref[h])
            ds = (p * (dp - jnp.sum(dp * p, axis=1, keepdims=True))).astype(BF)
            dvh = _tn(pb, doh)
            dkh = _tn(ds, qn[h])

            @pl.when(first)
            def _():
                dv_ref[h] = dvh
                dkn_ref[h] = dkh

            @pl.when(jnp.logical_not(first))
            def _():
                dv_ref[h] += dvh
                dkn_ref[h] += dkh

            dqn = _nn(ds, kn_ref[h]) * 0.0625
            dqh, gr = _norm_bwd(dqn, qr[h], gcq_ref[...])
            gcq_rows = gr if gcq_rows is None else gcq_rows + gr
            dq_sc[:, hs] = dqh.astype(BF)
        _acc_rows(dgcq_ref, first, gcq_rows)
        dqb = dq_sc[...]
        p_o = _tn(ob_sc[...], dyb)
        p_q = _tn(hb, dqb)

        @pl.when(first)
        def _():
            ao_sc[...] = p_o
            aq_sc[...] = p_q

        @pl.when(jnp.logical_not(first))
        def _():
            ao_sc[...] += p_o
            aq_sc[...] += p_q

        @pl.when(i == n - 1)
        def _():
            dwo_ref[...] = ao_sc[...].astype(BF)
            dwq_ref[...] = aq_sc[...].astype(BF)

        dh = _nt(dqb, wcq_ref[...])
        dx, gar = _norm_bwd(dh, xt, gca_ref[...])
        dx_ref[...] = dx + dyt
        _acc_rows(dgca_ref, first, gar)

    row = lambda i: (i, 0)
    fix = lambda i: (0, 0)
    fix3 = lambda i: (0, 0, 0)
    hd = (CA_HEADS, M, CA_HD)
    return pl.pallas_call(
        body, name="ca_bwd", grid=(n,),
        in_specs=[pl.BlockSpec((tm, D), row), pl.BlockSpec((tm, D), row), pl.BlockSpec((1, D), fix),
                  pl.BlockSpec((D, D), fix), pl.BlockSpec((1, CA_HD), fix), pl.BlockSpec(hd, fix3),
                  pl.BlockSpec(hd, fix3), pl.BlockSpec((D, D), fix)],
        out_specs=[pl.BlockSpec((tm, D), row), pl.BlockSpec((D, D), fix), pl.BlockSpec((D, D), fix),
                   pl.BlockSpec(hd, fix3), pl.BlockSpec(hd, fix3), pl.BlockSpec((1, CA_HD), fix),
                   pl.BlockSpec((1, D), fix)],
        out_shape=[S((T, D), F32), S((D, D), BF), S((D, D), BF), S(hd, F32), S(hd, F32), S((1, CA_HD), F32),
                   S((1, D), F32)],
        scratch_shapes=[pltpu.VMEM((D, D), F32), pltpu.VMEM((D, D), F32), pltpu.VMEM((tm, D), BF),
                        pltpu.VMEM((tm, D), BF)],
        compiler_params=_cp(1))(x, dy, g_ca, wcq, g_cq, kn, vv, wco)


def _ca_kv_bwd(mem, g_mem, mn, kraw, dkn, dvv, wckv, g_ck):
    M, D = mem.shape

    def body(m_ref, g_ref, mn_ref, kr_ref, dkn_ref, dv_ref, w_ref, gk_ref, dw_ref, dgk_ref, dgm_ref):
        mn = mn_ref[...]
        dmn = jnp.zeros((M, D), F32)
        gk_rows = None
        for h in range(CA_HEADS):
            dkr, gr = _norm_bwd(dkn_ref[h], kr_ref[h], gk_ref[...])
            gk_rows = gr if gk_rows is None else gk_rows + gr
            dkb = dkr.astype(BF)
            dvb = dv_ref[h].astype(BF)
            dw_ref[h] = _tn(mn, dkb).astype(BF)
            dw_ref[CA_HEADS + h] = _tn(mn, dvb).astype(BF)
            dmn = dmn + _nt(dkb, w_ref[h]) + _nt(dvb, w_ref[CA_HEADS + h])
        dgk_ref[...] = jnp.sum(gk_rows, axis=0, keepdims=True)
        mf = m_ref[...]
        dgm_ref[...] = jnp.sum(dmn * (mf * _rstd(mf)), axis=0, keepdims=True)

    return pl.pallas_call(
        body, name="ca_kv_bwd",
        out_shape=[S((2 * CA_HEADS, D, CA_HD), BF), S((1, CA_HD), F32), S((1, D), F32)],
        compiler_params=pltpu.CompilerParams(vmem_limit_bytes=VMEM_LIMIT))(mem, g_mem, mn, kraw, dkn, dvv, wckv, g_ck)


def _after(g, token):
    return g if token is None else g + token[0:1, 0:1]


def _local_step(x, mem, target, small, weights, emit):
    T, D = x.shape
    p = small
    bf128 = jnp.pad(p["b_f"], ((0, 0), (0, LANES - FOX_HEADS)))
    b_st = p["b_s"].T

    wup1 = weights("ffn1_up", x)["wup1"]
    a1, h1 = _ffn_up("ffn1_up", x, p["g_ffn1"], wup1)
    wdn1 = weights("ffn1_dn", h1)["wdn1"]
    x1 = _ffn_down("ffn1_down", a1, wdn1, x)
    wm = weights("mix", x1)
    z, h2 = _mix_proj(x1, p["g_mix"], wm["wz"])
    qf, ka, va, yg = _mix_prep(z, bf128, p["g_q"], p["g_k"], p["g_sgu"], p["w_s"], b_st, p["g_gmlp_o"])
    attn, lse = _fox_fwd(qf, ka, va)
    x2 = _mix_out(attn, yg, p["g_fox_o"], wm["wout"], x1)
    wc = weights("ca", x2)
    mn, kraw, ckn, cvv = _ca_kv(mem, p["g_mem"], wc["wckv"], p["g_ck"])
    x3 = _ca_fwd(x2, p["g_ca"], wc["wcq"], p["g_cq"], ckn, cvv, wc["wco"])
    w2 = weights("ffn2", x3)
    a2, h4 = _ffn_up("ffn2_up", x3, p["g_ffn2"], w2["wup2"])
    dy4, dy4b, sq = _ffn_down_loss("ffn2_down", a2, w2["wdn2"], x3, target)

    gs = {}
    dgu2 = _ffn_bwd_act("ffn2_bwd_act", dy4b, h4, w2["wup2"], w2["wdn2"])
    tok = emit("ffn2", {"wup2": _ffn_dwup("ffn2", h4, dgu2), "wdn2": _ffn_dwdn("ffn2", a2, dy4b)})
    dx3, gs["g_ffn2"] = _ffn_dx("ffn2_dx", dgu2, w2["wup2"], x3, _after(p["g_ffn2"], tok), dy4)

    dx2, dwcq, dwco, dckn, dcvv, gs["g_cq"], gs["g_ca"] = _ca_bwd(
        x2, dx3, p["g_ca"], wc["wcq"], p["g_cq"], ckn, cvv, wc["wco"])
    dwckv, gs["g_ck"], gs["g_mem"] = _ca_kv_bwd(mem, p["g_mem"], mn, kraw, dckn, dcvv, wc["wckv"], p["g_ck"])

    qb, dob, dyg, dwout, gs["g_fox_o"] = _mix_out_bwd(dx2, attn, yg, p["g_fox_o"], wm["wout"], qf, lse)
    dq, dk, dv = _fox_bwd(qb, ka, va, dob)
    dz, gs["g_q"], gs["g_k"], gs["g_sgu"], gs["g_gmlp_o"], gs["w_s"], dbst, dbf = _mix_prep_bwd(
        z, dq, dk, dv, dyg, bf128, p["g_q"], p["g_k"], p["g_sgu"], p["w_s"], b_st, p["g_gmlp_o"])
    gs["b_s"] = dbst.T
    gs["b_f"] = dbf[:, :FOX_HEADS]
    tok_ws = emit("w_s", {"w_s": gs["w_s"]})
    tk = _tile(T, 1024)
    zb = ZW // 3
    dwz = _tn_matmul(
        "mix_dwz", dz, pl.BlockSpec((tk, zb), lambda j, k: (k, j)), h2, pl.BlockSpec((tk, D), lambda j, k: (k, 0)),
        S((ZW, D), F32), pl.BlockSpec((zb, D), lambda j, k: (j, 0)), (3, T // tk), (zb, D))
    tok = emit("mid", {"wcq": dwcq, "wco": dwco, "wckv": dwckv, "wout": dwout, "wz": dwz})
    dx1, dx1b, gs["g_mix"] = _mix_proj_bwd(dz, wm["wz"], x1, _after(_after(p["g_mix"], tok), tok_ws), dx2)

    dgu1 = _ffn_bwd_act("ffn1_bwd_act", dx1b, h1, wup1, wdn1)
    tok = emit("ffn1_dn", {"wdn1": _ffn_dwdn("ffn1", a1, dx1b)})
    tok = emit("ffn1_up", {"wup1": _ffn_dwup("ffn1", h1, dgu1, after=tok)})
    dx0, gs["g_ffn1"] = _ffn_dx("ffn1_dx", dgu1, wup1, x, _after(p["g_ffn1"], tok), dx1)
    return sq, dx0, gs


MESH = pl.DeviceIdType.MESH
HBM_SPEC = pl.BlockSpec(memory_space=pltpu.HBM)
N_PEER = N_DEV - 1


def _place():
    return lax.axis_index("x"), lax.axis_index("y"), lax.axis_index("c")


def _slot(px, py, pc):
    return 4 * px + 2 * py + pc


SEM_SPEC = pl.BlockSpec(memory_space=pltpu.SEMAPHORE)
ANY_SPEC = pl.BlockSpec(memory_space=pl.ANY)
DATAFLOW = pltpu.SideEffectType.DATAFLOW_SIDE_EFFECTING


def _hbm(a):
    return pltpu.with_memory_space_constraint(a, pltpu.HBM)


def _peer(x, y, c, r):
    return (1 - x if r & 4 else x, 1 - y if r & 2 else y, 1 - c if r & 1 else c)


def _place_own(srcs, whole):
    my = _slot(*_place())
    lands = []
    for s in srcs:
        blk = s[None] if whole else lax.dynamic_slice_in_dim(s, my, 1, 0)
        shape = (N_DEV,) + s.shape if whole else s.shape
        lands.append(lax.dynamic_update_slice_in_dim(lax.empty(shape, s.dtype), blk, my, 0))
    return lands


ALL_PEERS = tuple(range(1, N_DEV))
NEAR_PEERS = (1, 2, 4, 6)
SAME_CORE = (2, 4, 6)


def _copy_start(name, srcs, lands, whole, peers=None):
    n = len(srcs)
    peers = peers or [ALL_PEERS] * n

    def body(*refs):
        src, land = refs[:n], refs[n:2 * n]
        send, recv = refs[2 * n:3 * n], refs[3 * n:4 * n]
        token = refs[6 * n]
        x, y, c = _place()
        my = _slot(x, y, c)
        for a in range(n):
            for r in peers[a]:
                p = _peer(x, y, c, r)
                pltpu.make_async_remote_copy(
                    src_ref=src[a] if whole else src[a].at[_slot(*p)], dst_ref=land[a].at[my],
                    send_sem=send[a].at[r - 1], recv_sem=recv[a].at[r - 1], device_id=p, device_id_type=MESH).start()
        token[...] = jnp.zeros_like(token)

    out = pl.pallas_call(
        body, name=name,
        out_shape=([pltpu.SemaphoreType.DMA((N_PEER,))] * (2 * n)
                   + [pltpu.HBM(s.shape, s.dtype) for s in srcs] + [pltpu.HBM(s.shape, s.dtype) for s in lands]
                   + [S((8, LANES), F32)]),
        in_specs=[HBM_SPEC] * (2 * n),
        out_specs=[SEM_SPEC] * (2 * n) + [HBM_SPEC] * (2 * n) + [pl.BlockSpec(memory_space=pltpu.VMEM)],
        input_output_aliases={i: 2 * n + i for i in range(2 * n)},
        compiler_params=pltpu.CompilerParams(has_side_effects=DATAFLOW),
    )(*[_hbm(s) for s in srcs], *[_hbm(s) for s in lands])
    return out[:n], out[n:2 * n], out[2 * n:3 * n], out[3 * n:4 * n], out[4 * n]


def _copy_wait(name, srcs, lands, send, recv, after, whole, peers=None):
    n = len(srcs)
    peers = peers or [ALL_PEERS] * n

    def body(*refs):
        src, land = refs[:n], refs[n:2 * n]
        snd, rcv = refs[2 * n:3 * n], refs[3 * n:4 * n]
        x, y, c = _place()
        for a in range(n):
            for r in peers[a]:
                p = _peer(x, y, c, r)
                ps = _slot(*p)
                cp = pltpu.make_async_remote_copy(
                    src_ref=src[a] if whole else src[a].at[ps], dst_ref=land[a].at[ps],
                    send_sem=snd[a].at[r - 1], recv_sem=rcv[a].at[r - 1], device_id=p, device_id_type=MESH)
                cp.wait_send()
                cp.wait_recv()

    out = pl.pallas_call(
        body, name=name,
        out_shape=[pltpu.HBM(s.shape, s.dtype) for s in srcs] + [pltpu.HBM(s.shape, s.dtype) for s in lands],
        in_specs=[HBM_SPEC] * (2 * n) + [SEM_SPEC] * (2 * n) + [ANY_SPEC],
        out_specs=[HBM_SPEC] * (2 * n),
        input_output_aliases={i: i for i in range(2 * n)},
        compiler_params=pltpu.CompilerParams(has_side_effects=DATAFLOW),
    )(*srcs, *lands, *send, *recv, after)
    return out[n:]


def _forward_start(name, lands):
    n = len(lands)

    def body(*refs):
        land = refs[:n]
        send, recv = refs[n:2 * n], refs[2 * n:3 * n]
        token = refs[4 * n]
        x, y, c = _place()
        for a in range(n):
            for r in SAME_CORE:
                blk = land[a].at[_slot(*_peer(x, y, c, r))]
                pltpu.make_async_remote_copy(
                    src_ref=blk, dst_ref=blk, send_sem=send[a].at[r - 1], recv_sem=recv[a].at[r - 1],
                    device_id=(x, y, 1 - c), device_id_type=MESH).start()
        token[...] = jnp.zeros_like(token)

    out = pl.pallas_call(
        body, name=name,
        out_shape=([pltpu.SemaphoreType.DMA((N_PEER,))] * (2 * n) + [pltpu.HBM(s.shape, s.dtype) for s in lands]
                   + [S((8, LANES), F32)]),
        in_specs=[HBM_SPEC] * n,
        out_specs=[SEM_SPEC] * (2 * n) + [HBM_SPEC] * n + [pl.BlockSpec(memory_space=pltpu.VMEM)],
        input_output_aliases={i: 2 * n + i for i in range(n)},
        compiler_params=pltpu.CompilerParams(has_side_effects=DATAFLOW),
    )(*[_hbm(s) for s in lands])
    return out[:n], out[n:2 * n], out[2 * n:3 * n], out[3 * n]


def _forward_wait(name, lands, send, recv, after):
    n = len(lands)

    def body(*refs):
        land = refs[:n]
        snd, rcv = refs[n:2 * n], refs[2 * n:3 * n]
        x, y, c = _place()
        for a in range(n):
            for r in SAME_CORE:
                cp = pltpu.make_async_remote_copy(
                    src_ref=land[a].at[_slot(*_peer(x, y, c, r))], dst_ref=land[a].at[_slot(*_peer(x, y, c, r | 1))],
                    send_sem=snd[a].at[r - 1], recv_sem=rcv[a].at[r - 1], device_id=(x, y, 1 - c),
                    device_id_type=MESH)
                cp.wait_send()
                cp.wait_recv()

    return pl.pallas_call(
        body, name=name,
        out_shape=[pltpu.HBM(s.shape, s.dtype) for s in lands],
        in_specs=[HBM_SPEC] * n + [SEM_SPEC] * (2 * n) + [ANY_SPEC],
        out_specs=[HBM_SPEC] * n,
        input_output_aliases={i: i for i in range(n)},
        compiler_params=pltpu.CompilerParams(has_side_effects=DATAFLOW),
    )(*lands, *send, *recv, after)


def _adamw(w, g, m, v):
    m2 = ADAM_B1 * m + (1.0 - ADAM_B1) * g
    v2 = ADAM_B2 * v + (1.0 - ADAM_B2) * (g * g)
    m_hat = m2 / (1.0 - ADAM_B1 ** ADAM_STEP)
    v_hat = v2 / (1.0 - ADAM_B2 ** ADAM_STEP)
    delta = -ADAM_LR * (m_hat / (jnp.sqrt(v_hat) + ADAM_EPS) + ADAM_WD * w)
    return delta, m2, v2


def _adamw_big(name, slots, w, m, v):
    R, C = w.shape
    tr = next((t for t in (256, 352) if R % t == 0), R)

    def body(s_ref, w_ref, m_ref, v_ref, g_ref, d_ref, m2_ref, v2_ref):
        g = s_ref[0].astype(F32)
        for k in range(1, N_DEV):
            g = g + s_ref[k].astype(F32)
        d, m2, v2 = _adamw(w_ref[...], g, m_ref[...], v_ref[...])
        g_ref[...] = g
        d_ref[...] = d
        m2_ref[...] = m2
        v2_ref[...] = v2

    row = pl.BlockSpec((tr, C), lambda i: (i, 0))
    return pl.pallas_call(
        body, name=name, grid=(R // tr,),
        in_specs=[pl.BlockSpec((N_DEV, tr, C), lambda i: (0, i, 0)), row, row, row],
        out_specs=[row] * 4, out_shape=[S((R, C), F32)] * 4,
        compiler_params=_cp(1))(slots, w, m, v)


TINY_ROWS = (("b_s", 8), ("g_ffn1", 8), ("g_mix", 8), ("g_ca", 8), ("g_mem", 8), ("g_ffn2", 8), ("g_sgu", 4),
             ("g_fox_o", 4), ("g_gmlp_o", 4), ("g_cq", 2), ("g_ck", 2), ("g_q", 1), ("g_k", 1), ("b_f", 1),
             ("loss", 1))
TINY_P = 72


def _pack_tiny(d):
    rows = []
    for name, r in TINY_ROWS:
        flat = d[name].reshape(-1) if name in d else jnp.zeros((r * LANES,), F32)
        rows.append(jnp.pad(flat, (0, r * LANES - flat.shape[0])).reshape(r, LANES))
    used = sum(r for _, r in TINY_ROWS)
    rows.append(jnp.zeros((TINY_P - used, LANES), F32))
    return jnp.concatenate(rows, axis=0)


def _unpack_tiny(packed, shapes):
    out, at = {}, 0
    for name, r in TINY_ROWS:
        shape = shapes[name]
        size = 1
        for s in shape:
            size *= s
        out[name] = packed[at:at + r].reshape(-1)[:size].reshape(shape)
        at += r
    return out


WEIGHTS =('g_ffn1', 'w_ffn1_in', 'w_ffn1_out', 'g_mix', 'w_in', 'b_f', 'g_q', 'g_k', 'g_sgu', 'w_s', 'b_s',
           'g_fox_o', 'g_gmlp_o', 'w_out', 'g_ca', 'g_mem', 'w_cq', 'w_ckv', 'g_cq', 'g_ck', 'w_co', 'g_ffn2',
           'w_ffn2_in', 'w_ffn2_out')
BIG = ('w_ffn1_in', 'w_ffn1_out', 'w_in', 'w_out', 'w_cq', 'w_ckv', 'w_co', 'w_ffn2_in', 'w_ffn2_out')
TRANSPOSED = ('w_ffn1_in', 'w_in', 'w_ffn2_in')
GATHER_GROUPS = {"ffn1_up": ("w_ffn1_in",), "ffn1_dn": ("w_ffn1_out",), "mix": ("w_in", "w_out"),
                 "ca": ("w_cq", "w_ckv", "w_co"), "ffn2": ("w_ffn2_in", "w_ffn2_out")}
QKV_W = 3 * FOX_W
UV_OFF = QKV_W + FOX_HEADS


def kernel(x, mem, g_ffn1, w_ffn1_in, w_ffn1_out, g_mix, w_in, b_f, g_q, g_k, g_sgu, w_s, b_s, g_fox_o, g_gmlp_o, w_out, g_ca, g_mem, w_cq, w_ckv, g_cq, g_ck, w_co, g_ffn2, w_ffn2_in, w_ffn2_out, loss_target, m_g_ffn1, m_w_ffn1_in, m_w_ffn1_out, m_g_mix, m_w_in, m_b_f, m_g_q, m_g_k, m_g_sgu, m_w_s, m_b_s, m_g_fox_o, m_g_gmlp_o, m_w_out, m_g_ca, m_g_mem, m_w_cq, m_w_ckv, m_g_cq, m_g_ck, m_w_co, m_g_ffn2, m_w_ffn2_in, m_w_ffn2_out, v_g_ffn1, v_w_ffn1_in, v_w_ffn1_out, v_g_mix, v_w_in, v_b_f, v_g_q, v_g_k, v_g_sgu, v_w_s, v_b_s, v_g_fox_o, v_g_gmlp_o, v_w_out, v_g_ca, v_g_mem, v_w_cq, v_w_ckv, v_g_cq, v_g_ck, v_w_co, v_g_ffn2, v_w_ffn2_in, v_w_ffn2_out):
    args = dict(locals())
    w = {n: args[n] for n in WEIGHTS}
    mo = {n: args["m_" + n] for n in WEIGHTS}
    vo = {n: args["v_" + n] for n in WEIGHTS}
    D = D_MODEL

    def local(n, a):
        return a[0].T if n in TRANSPOSED else a[0]

    shards = [local(n, w[n]).astype(BF) for n in BIG]
    fb = shards[0].shape[0]
    g_peers = [NEAR_PEERS if n == BIG[0] else ALL_PEERS for n in BIG]
    g_snd, g_rcv, g_src, g_land, g_token = _copy_start("gather_start", shards, _place_own(shards, True), True,
                                                       peers=g_peers)
    handles = {n: (g_src[i], g_land[i], g_snd[i], g_rcv[i]) for i, n in enumerate(BIG)}

    tiny_names = [n for n, _ in TINY_ROWS if n != "loss"]
    tiny_wmv =[_pack_tiny({n: a[n] for n in tiny_names}) + g_token[0:1, 0:1] for a in (w, mo, vo)]
    first_after = tiny_wmv[0][0:8] + tiny_wmv[1][0:8] + tiny_wmv[2][0:8]

    def weights(group, after):
        names = GATHER_GROUPS[group]
        hs = [handles[n] for n in names]
        got = _copy_wait("gather_wait_" + group, [h[0] for h in hs], [h[1] for h in hs], [h[2] for h in hs],
                         [h[3] for h in hs], first_after if group == "ffn1_up" else after, True,
                         peers=[g_peers[BIG.index(n)] for n in names])
        if group == "ffn1_up":
            f_snd, f_rcv, f_land, f_token = _forward_start("gather_pass_start", got)
            got = _forward_wait("gather_pass_wait", f_land, f_snd, f_rcv, f_token)
        got = dict(zip(names, got))
        if group == "ffn1_up":
            return {"wup1": got["w_ffn1_in"].reshape(2, N_FFN_BLK, fb, D)}
        if group == "ffn1_dn":
            return {"wdn1": got["w_ffn1_out"].reshape(N_FFN_BLK, fb, D)}
        if group == "mix":
            full = got["w_in"].reshape(-1, D)
            wz = jnp.concatenate([full[:QKV_W], full[UV_OFF:], full[QKV_W:UV_OFF],
                                  jnp.zeros((LANES - FOX_HEADS, D), BF)], axis=0)
            return {"wz": wz, "wout": got["w_out"].reshape(D, D)}
        if group == "ca":
            return {"wcq": got["w_cq"].reshape(D, D), "wco": got["w_co"].reshape(D, D), "wckv": got["w_ckv"]}
        return {"wup2": got["w_ffn2_in"].reshape(2, N_FFN_BLK, fb, D),
                "wdn2": got["w_ffn2_out"].reshape(N_FFN_BLK, fb, D)}

    flying = {}

    def emit(group, g):
        if group == "w_s":
            part = [g["w_s"].reshape(-1, LANES)]
            *copies, token = _copy_start("w_s_start", part, _place_own(part, True), True)
            flying[group] = copies
            return token
        if group == "ffn2":
            parts = {"w_ffn2_in": g["wup2"], "w_ffn2_out": g["wdn2"].reshape(N_DEV, -1, D)}
        elif group == "ffn1_dn":
            parts = {"w_ffn1_out": g["wdn1"].reshape(N_DEV, -1, D)}
        elif group == "ffn1_up":
            parts = {"w_ffn1_in": g["wup1"]}
        else:
            gz = g["wz"]
            g_in = jnp.concatenate([gz[:QKV_W], gz[Z_F:Z_F + FOX_HEADS], gz[QKV_W:Z_F]], axis=0)
            parts = {"w_in": g_in.reshape(N_DEV, -1, D).astype(BF),
                     "w_out": g["wout"].reshape(N_DEV, -1, D), "w_cq": g["wcq"].reshape(N_DEV, -1, D),
                     "w_co": g["wco"].reshape(N_DEV, -1, D), "w_ckv": g["wckv"]}
        names = list(parts)
        srcs = [parts[n] for n in names]
        *copies, token = _copy_start("exchange_start_" + group, srcs, _place_own(srcs, False), False)
        flying[group] = (names, copies)
        return token

    small = {n: (w[n][0] if n == "b_s" else w[n]) for n in tiny_names}
    small["w_s"] = w["w_s"][0]

    sq, dx0, gs = _local_step(x[0], mem[0], loss_target[0], small, weights, emit)

    sm_parts = [_pack_tiny({**gs, "loss": sq[0:1]})]
    sm_snd, sm_rcv, sm_src, sm_land, sm_token = _copy_start("tiny_start", sm_parts, _place_own(sm_parts, True), True)

    grad, delta, new_m, new_v = {}, {}, {}, {}

    def update(group, after):
        names, (snd, rcv, srcs, lands) = flying[group]
        slots = _copy_wait("exchange_wait_" + group, srcs, lands, snd, rcv, after, False)
        for n, sl in zip(names, slots):
            g, d, m2, v2 = _adamw_big("adamw_" + n, sl, local(n, w[n]), local(n, mo[n]), local(n, vo[n]))
            grad[n], delta[n], new_m[n], new_v[n] = (
                (t.T if n in TRANSPOSED else t).reshape(w[n].shape) for t in (g, d, m2, v2))
        return d

    last = update("ffn2", sm_token)
    last = update("mid", last)
    last = update("ffn1_dn", last)
    last = update("ffn1_up", last)
    ws_snd, ws_rcv, ws_src, ws_land = flying["w_s"]
    ws_all, = _copy_wait("w_s_wait", ws_src, ws_land, ws_snd, ws_rcv, last, True)
    tiny_all, = _copy_wait("tiny_wait", sm_src, sm_land, sm_snd, sm_rcv, ws_all, True)
    ws_shape = w["w_s"].shape
    for store, t in zip((grad, delta, new_m, new_v), _adamw_big(
            "adamw_w_s", ws_all, *[a["w_s"].reshape(-1, LANES) for a in (w, mo, vo)])):
        store["w_s"] = t.reshape(ws_shape)
    shapes = {n: w[n].shape for n in tiny_names}
    shapes["loss"] = (1, LANES)
    for store, t in zip((grad, delta, new_m, new_v), _adamw_big(
            "adamw_tiny", tiny_all, *tiny_wmv)):
        store.update(_unpack_tiny(t, shapes))
    loss = grad["loss"][0, 0] * (0.5 / D)

    return (loss, dx0[None], *[grad[n] for n in WEIGHTS], *[delta[n] for n in WEIGHTS],
            *[new_m[n] for n in WEIGHTS], *[new_v[n] for n in WEIGHTS])
```

```python
import functools

import jax
import jax.numpy as jnp
from jax import lax
from jax.experimental import pallas as pl
from jax.experimental.pallas import tpu as pltpu

F32 = jnp.float32
BF = jnp.bfloat16
S = jax.ShapeDtypeStruct

N_DEV = 8
D_MODEL = 1024
FOX_HEADS, FOX_HD = 8, 64
FOX_W = 512
GMLP_G, GMLP_GD = 8, 64
GMLP_W = 512
CHUNK = 128
CA_HEADS, CA_HD = 4, 256
N_FFN_BLK = 4
ZW = 2688
Z_Q, Z_K, Z_V, Z_U, Z_G, Z_F = 0, 512, 1024, 1536, 2048, 2560
EPS = 1e-6
NEG = -1e30
LANES = 128

ADAM_LR, ADAM_B1, ADAM_B2, ADAM_EPS, ADAM_WD, ADAM_STEP = 0.001, 0.9, 0.999, 1e-08, 0.01, 10

VMEM_LIMIT = 52 * 2 ** 20


def _cp(n_axes):
    return pltpu.CompilerParams(dimension_semantics=("arbitrary",) * n_axes, vmem_limit_bytes=VMEM_LIMIT)


def _nn(a, b):
    return jnp.dot(a, b, preferred_element_type=F32)


def _nt(a, b):
    return lax.dot_general(a, b, (((1,), (1,)), ((), ())), preferred_element_type=F32)


def _tn(a, b):
    return lax.dot_general(a, b, (((0,), (0,)), ((), ())), preferred_element_type=F32)


def _hi(a, b):
    return jnp.dot(a, b, precision=lax.Precision.HIGHEST, preferred_element_type=F32)


def _rstd(x):
    return lax.rsqrt(jnp.mean(x * x, axis=-1, keepdims=True) + EPS)


def _norm_bwd(dy, x, g):
    r = _rstd(x)
    xh = x * r
    dxh = dy * g
    dx = r * (dxh - xh * jnp.mean(dxh * xh, axis=-1, keepdims=True))
    return dx, dy * xh


def _acc_rows(ref, first, val):
    srow = jnp.sum(val, axis=0, keepdims=True)

    @pl.when(first)
    def _():
        ref[...] = srow

    @pl.when(jnp.logical_not(first))
    def _():
        ref[...] += srow


def _gelu(x):
    c = 0.7978845608028654
    return 0.5 * x * (1.0 + jnp.tanh(c * (x + 0.044715 * x * x * x)))


def _gelu_grad(x):
    c = 0.7978845608028654
    t = jnp.tanh(c * (x + 0.044715 * x * x * x))
    return 0.5 * (1.0 + t) + 0.5 * x * (1.0 - t * t) * c * (1.0 + 3 * 0.044715 * x * x)


def _tile(n, pref):
    return pref if n % pref == 0 else n


def _ffn_up(name, x, g, wup):
    T, D = x.shape
    FB = wup.shape[-2]
    tm = _tile(T, 1024)

    def body(x_ref, g_ref, w_ref, a_ref, h_ref):
        @pl.when(pl.program_id(1) == 0)
        def _():
            xf = x_ref[...]
            h_ref[...] = (xf * _rstd(xf) * g_ref[...]).astype(BF)

        hb = h_ref[...]
        gg = _nt(hb, w_ref[0])
        uu = _nt(hb, w_ref[1])
        a_ref[...] = (gg * jax.nn.sigmoid(gg) * uu).astype(BF)

    return pl.pallas_call(
        body, name=name, grid=(T // tm, N_FFN_BLK),
        in_specs=[pl.BlockSpec((tm, D), lambda i, j: (i, 0)),
                  pl.BlockSpec((1, D), lambda i, j: (0, 0)),
                  pl.BlockSpec((2, None, FB, D), lambda i, j: (0, j, 0, 0))],
        out_specs=[pl.BlockSpec((None, tm, FB), lambda i, j: (j, i, 0)),
                   pl.BlockSpec((tm, D), lambda i, j: (i, 0))],
        out_shape=[S((N_FFN_BLK, T, FB), BF), S((T, D), BF)],
        compiler_params=_cp(2))(x, g, wup)


def _ffn_down(name, a, wdn, x):
    _, T, FB = a.shape
    D = x.shape[1]
    tm = _tile(T, 512)

    def body(a_ref, w_ref, x_ref, o_ref):
        p = _nn(a_ref[0], w_ref[0])
        for j in range(1, N_FFN_BLK):
            p = p + _nn(a_ref[j], w_ref[j])
        o_ref[...] = x_ref[...] + 0.5 * p

    return pl.pallas_call(
        body, name=name, grid=(T // tm,),
        in_specs=[pl.BlockSpec((N_FFN_BLK, tm, FB), lambda i: (0, i, 0)),
                  pl.BlockSpec((N_FFN_BLK, FB, D), lambda i: (0, 0, 0)),
                  pl.BlockSpec((tm, D), lambda i: (i, 0))],
        out_specs=pl.BlockSpec((tm, D), lambda i: (i, 0)),
        out_shape=S((T, D), F32),
        compiler_params=_cp(1))(a, wdn, x)


def _ffn_down_loss(name, a, wdn, x, target):
    _, T, FB = a.shape
    D = x.shape[1]
    tm = _tile(T, 512)

    def body(a_ref, w_ref, x_ref, t_ref, d_ref, db_ref, loss_ref):
        i = pl.program_id(0)
        p = _nn(a_ref[0], w_ref[0])
        for j in range(1, N_FFN_BLK):
            p = p + _nn(a_ref[j], w_ref[j])
        diff = (x_ref[...] + 0.5 * p) - t_ref[...]
        dy = diff * (1.0 / D)
        d_ref[...] = dy
        db_ref[...] = dy.astype(BF)
        sq = jnp.zeros((8, LANES), F32) + jnp.sum(diff * diff)

        @pl.when(i == 0)
        def _():
            loss_ref[...] = sq

        @pl.when(i > 0)
        def _():
            loss_ref[...] += sq

    row = pl.BlockSpec((tm, D), lambda i: (i, 0))
    return pl.pallas_call(
        body, name=name, grid=(T // tm,),
        in_specs=[pl.BlockSpec((N_FFN_BLK, tm, FB), lambda i: (0, i, 0)),
                  pl.BlockSpec((N_FFN_BLK, FB, D), lambda i: (0, 0, 0)), row, row],
        out_specs=[row, row, pl.BlockSpec((8, LANES), lambda i: (0, 0))],
        out_shape=[S((T, D), F32), S((T, D), BF), S((8, LANES), F32)],
        compiler_params=_cp(1))(a, wdn, x, target)


def _ffn_bwd_act(name, dyb, h, wup, wdn):
    T, D = h.shape
    FB = wup.shape[-2]
    tm = _tile(T, 1024)

    def body(d_ref, h_ref, wu_ref, wd_ref, o_ref):
        da = 0.5 * _nt(d_ref[...], wd_ref[...])
        hb = h_ref[...]
        gg = _nt(hb, wu_ref[0])
        uu = _nt(hb, wu_ref[1])
        sg = jax.nn.sigmoid(gg)
        o_ref[0] = (da * uu * (sg * (1.0 + gg * (1.0 - sg)))).astype(BF)
        o_ref[1] = (da * (gg * sg)).astype(BF)

    return pl.pallas_call(
        body, name=name, grid=(T // tm, N_FFN_BLK),
        in_specs=[pl.BlockSpec((tm, D), lambda i, j: (i, 0)),
                  pl.BlockSpec((tm, D), lambda i, j: (i, 0)),
                  pl.BlockSpec((2, None, FB, D), lambda i, j: (0, j, 0, 0)),
                  pl.BlockSpec((None, FB, D), lambda i, j: (j, 0, 0))],
        out_specs=pl.BlockSpec((2, None, tm, FB), lambda i, j: (0, j, i, 0)),
        out_shape=S((2, N_FFN_BLK, T, FB), BF),
        compiler_params=_cp(2))(dyb, h, wup, wdn)


def _ffn_dx(name, dgu, wup, x, g, dy):
    T, D = x.shape
    FB = wup.shape[-2]
    tm = _tile(T, 1024)

    def body(d_ref, w_ref, x_ref, g_ref, dy_ref, dx_ref, dg_ref, acc_ref):
        i, j = pl.program_id(0), pl.program_id(1)
        p = _nn(d_ref[0], w_ref[0]) + _nn(d_ref[1], w_ref[1])

        @pl.when(j == 0)
        def _():
            acc_ref[...] = p

        @pl.when(j > 0)
        def _():
            acc_ref[...] += p

        @pl.when(j == N_FFN_BLK - 1)
        def _():
            dx, dgr = _norm_bwd(acc_ref[...], x_ref[...], g_ref[...])
            dx_ref[...] = dx + dy_ref[...]
            _acc_rows(dg_ref, i == 0, dgr)

    return pl.pallas_call(
        body, name=name, grid=(T // tm, N_FFN_BLK),
        in_specs=[pl.BlockSpec((2, None, tm, FB), lambda i, j: (0, j, i, 0)),
                  pl.BlockSpec((2, None, FB, D), lambda i, j: (0, j, 0, 0)),
                  pl.BlockSpec((tm, D), lambda i, j: (i, 0)),
                  pl.BlockSpec((1, D), lambda i, j: (0, 0)),
                  pl.BlockSpec((tm, D), lambda i, j: (i, 0))],
        out_specs=[pl.BlockSpec((tm, D), lambda i, j: (i, 0)),
                   pl.BlockSpec((1, D), lambda i, j: (0, 0))],
        out_shape=[S((T, D), F32), S((1, D), F32)],
        scratch_shapes=[pltpu.VMEM((tm, D), F32)],
        compiler_params=_cp(2))(dgu, wup, x, g, dy)


def _tn_matmul(name, a, a_spec, b, b_spec, out_shape, out_spec, grid, acc_shape, scale=1.0, after=None):
    nk = grid[1]
    extra = [] if after is None else [after]

    def body(a_ref, b_ref, *rest):
        o_ref, acc_ref = rest[-2:]
        k = pl.program_id(1)
        p = _tn(a_ref[...], b_ref[...])

        @pl.when(k == 0)
        def _():
            acc_ref[...] = p

        @pl.when(k > 0)
        def _():
            acc_ref[...] += p

        @pl.when(k == nk - 1)
        def _():
            o_ref[...] = (acc_ref[...] * scale).astype(o_ref.dtype)

    return pl.pallas_call(
        body, name=name, grid=grid,
        in_specs=[a_spec, b_spec] + [pl.BlockSpec((8, LANES), lambda j, k: (0, 0)) for _ in extra],
        out_specs=out_spec, out_shape=out_shape,
        scratch_shapes=[pltpu.VMEM(acc_shape, F32)], compiler_params=_cp(2))(a, b, *extra)


def _ffn_dwup(name, h, dgu, after=None):
    T, D = h.shape
    FB = dgu.shape[-1]
    tk = _tile(T, 1024)
    return _tn_matmul(
        name + "_dwup", dgu.reshape(2 * N_FFN_BLK, T, FB), pl.BlockSpec((None, tk, FB), lambda j, k: (j, k, 0)),
        h, pl.BlockSpec((tk, D), lambda j, k: (k, 0)),
        S((2 * N_FFN_BLK, FB, D), BF), pl.BlockSpec((None, FB, D), lambda j, k: (j, 0, 0)),
        (2 * N_FFN_BLK, T // tk), (FB, D), after=after)


def _ffn_dwdn(name, a, dyb):
    _, T, FB = a.shape
    D = dyb.shape[1]
    tk = _tile(T, 1024)
    return _tn_matmul(
        name + "_dwdn", a, pl.BlockSpec((None, tk, FB), lambda j, k: (j, k, 0)),
        dyb, pl.BlockSpec((tk, D), lambda j, k: (k, 0)),
        S((N_FFN_BLK, FB, D), BF), pl.BlockSpec((None, FB, D), lambda j, k: (j, 0, 0)),
        (N_FFN_BLK, T // tk), (FB, D), scale=0.5)


def _mix_proj(x, g, wz):
    T, D = x.shape
    tm = _tile(T, 512)

    def body(x_ref, g_ref, w_ref, z_ref, h_ref):
        xf = x_ref[...]
        hb = (xf * _rstd(xf) * g_ref[...]).astype(BF)
        h_ref[...] = hb
        z_ref[...] = _nt(hb, w_ref[...])

    return pl.pallas_call(
        body, name="mix_proj", grid=(T // tm,),
        in_specs=[pl.BlockSpec((tm, D), lambda i: (i, 0)),
                  pl.BlockSpec((1, D), lambda i: (0, 0)),
                  pl.BlockSpec((ZW, D), lambda i: (0, 0))],
        out_specs=[pl.BlockSpec((tm, ZW), lambda i: (i, 0)),
                   pl.BlockSpec((tm, D), lambda i: (i, 0))],
        out_shape=[S((T, ZW), F32), S((T, D), BF)],
        compiler_params=_cp(1))(x, g, wz)


def _tri(n, lower):
    r = lax.broadcasted_iota(jnp.int32, (n, n), 0)
    c = lax.broadcasted_iota(jnp.int32, (n, n), 1)
    return (r >= c) if lower else (r <= c)


def _spatial_mix(vgn_b, ws_ref, bst, tm):
    tril = _tri(CHUNK, True)
    wms = [jnp.where(tril, ws_ref[g], 0.0).astype(BF) for g in range(GMLP_G)]
    rows = []
    for c in range(tm // CHUNK):
        cols = []
        for g in range(GMLP_G):
            vs = vgn_b[c * CHUNK:(c + 1) * CHUNK, g * GMLP_GD:(g + 1) * GMLP_GD]
            cols.append(_nn(wms[g], vs) + bst[:, g:g + 1])
        rows.append(jnp.concatenate(cols, axis=1))
    return jnp.concatenate(rows, axis=0), wms


HB = 128
AUG_W = FOX_HEADS * HB
COL_A, COL_B, COL_C = 64, 67, 70


def _spread_matrix():
    r = jnp.arange(FOX_W)
    return (jnp.arange(AUG_W)[None, :] == ((r // FOX_HD) * HB + r % FOX_HD)[:, None]).astype(BF)


def _piece_matrix(col):
    r = jnp.arange(LANES)
    dst = jnp.where(r < 3 * FOX_HEADS, (r % FOX_HEADS) * HB + col + r // FOX_HEADS, -1)
    return (jnp.arange(AUG_W)[None, :] == dst[:, None]).astype(BF)


def _ones_row(cols):
    c = jnp.arange(AUG_W) % HB
    hit = functools.reduce(jnp.logical_or, [(c >= a) & (c < a + 3) for a in cols])
    return hit.astype(F32)[None, :]


def _pieces(x):
    lane = lax.broadcasted_iota(jnp.int32, x.shape, 1)
    x = jnp.where(lane < FOX_HEADS, x, 0.0)
    hi = x.astype(BF).astype(F32)
    r1 = x - hi
    mid = r1.astype(BF).astype(F32)
    lo = (r1 - mid).astype(BF).astype(F32)
    return (hi + pltpu.roll(mid, FOX_HEADS, 1) + pltpu.roll(lo, 2 * FOX_HEADS, 1)).astype(BF)


def _mix_prep(z, bf128, g_q, g_k, g_sgu, w_s, b_st, g_go):
    T = z.shape[0]
    tm = _tile(T, 512)
    spread, pc_q, pc_k = _spread_matrix(), _piece_matrix(COL_A), _piece_matrix(COL_B)
    one_q, one_k, one_v = _ones_row([COL_B]), _ones_row([COL_A, COL_C]), _ones_row([COL_A])

    def body(z_ref, bf_ref, gq_ref, gk_ref, gs_ref, ws_ref, bst_ref, go_ref, sp_ref, pq_ref, pk_ref, oq_ref, ok_ref,
             ov_ref, q_ref, k_ref, v_ref, y_ref, carry_ref, qn_sc, kn_sc):
        i = pl.program_id(0)

        @pl.when(i == 0)
        def _():
            carry_ref[...] = jnp.zeros_like(carry_ref)

        for h in range(FOX_HEADS):
            hs = slice(h * FOX_HD, (h + 1) * FOX_HD)
            qh = z_ref[:, Z_Q + h * FOX_HD:Z_Q + (h + 1) * FOX_HD]
            kh = z_ref[:, Z_K + h * FOX_HD:Z_K + (h + 1) * FOX_HD]
            qn_sc[:, hs] = (qh * _rstd(qh) * gq_ref[...] * 0.125).astype(BF)
            kn_sc[:, hs] = (kh * _rstd(kh) * gk_ref[...]).astype(BF)

        fl = z_ref[:, Z_F:Z_F + LANES] + bf_ref[...]
        logf = jnp.minimum(fl, 0.0) - jnp.log1p(jnp.exp(-jnp.abs(fl)))
        csum = _hi(_tri(tm, True).astype(F32), logf) + carry_ref[...]
        carry_ref[...] = csum[tm - 1:tm, :]
        sp = sp_ref[...]
        q_ref[...] = (_nn(qn_sc[...], sp) + _nn(_pieces(csum), pq_ref[...]) + oq_ref[...]).astype(BF)
        k_ref[...] = (_nn(kn_sc[...], sp) + _nn(_pieces(-csum), pk_ref[...]) + ok_ref[...]).astype(BF)
        v_ref[...] = (_nn(z_ref[:, Z_V:Z_V + FOX_W].astype(BF), sp) + ov_ref[...]).astype(BF)

        u = _gelu(z_ref[:, Z_U:Z_U + GMLP_W])
        vg = _gelu(z_ref[:, Z_G:Z_G + GMLP_W])
        vgn = (vg * _rstd(vg) * gs_ref[...]).astype(BF)
        mixed, _ = _spatial_mix(vgn, ws_ref, bst_ref[...], tm)
        sgu = u * mixed
        y_ref[...] = (sgu * _rstd(sgu) * go_ref[...]).astype(BF)

    row = lambda i: (i, 0)
    fix2 = lambda i: (0, 0)
    return pl.pallas_call(
        body, name="mix_prep", grid=(T // tm,),
        in_specs=[pl.BlockSpec((tm, ZW), row),
                  pl.BlockSpec((1, LANES), fix2), pl.BlockSpec((1, FOX_HD), fix2), pl.BlockSpec((1, FOX_HD), fix2),
                  pl.BlockSpec((1, GMLP_W), fix2), pl.BlockSpec((GMLP_G, CHUNK, CHUNK), lambda i: (0, 0, 0)),
                  pl.BlockSpec((CHUNK, GMLP_G), fix2), pl.BlockSpec((1, GMLP_W), fix2),
                  pl.BlockSpec((FOX_W, AUG_W), fix2), pl.BlockSpec((LANES, AUG_W), fix2),
                  pl.BlockSpec((LANES, AUG_W), fix2), pl.BlockSpec((1, AUG_W), fix2), pl.BlockSpec((1, AUG_W), fix2),
                  pl.BlockSpec((1, AUG_W), fix2)],
        out_specs=[pl.BlockSpec((tm, AUG_W), row), pl.BlockSpec((tm, AUG_W), row), pl.BlockSpec((tm, AUG_W), row),
                   pl.BlockSpec((tm, GMLP_W), row)],
        out_shape=[S((T, AUG_W), BF), S((T, AUG_W), BF), S((T, AUG_W), BF), S((T, GMLP_W), BF)],
        scratch_shapes=[pltpu.VMEM((1, LANES), F32), pltpu.VMEM((tm, FOX_W), BF), pltpu.VMEM((tm, FOX_W), BF)],
        compiler_params=_cp(1))(z, bf128, g_q, g_k, g_sgu, w_s, b_st, g_go, spread, pc_q, pc_k, one_q, one_k, one_v)


def _fox_fwd(q, k, v):
    T = q.shape[0]
    tq = _tile(T, 1024)
    nq = T // tq

    def body(q_ref, k_ref, v_ref, o_ref, lse_ref, m_sc, acc_sc):
        i, j = pl.program_id(0), pl.program_id(1)

        @pl.when(j == 0)
        def _():
            m_sc[...] = jnp.full(m_sc.shape, NEG, F32)
            acc_sc[...] = jnp.zeros_like(acc_sc)

        def step(masked):
            mask = _tri(tq, True) if masked else None
            for h in range(FOX_HEADS):
                hb = slice(h * HB, (h + 1) * HB)
                s = _nt(q_ref[:, hb], k_ref[:, hb])
                if masked:
                    s = jnp.where(mask, s, NEG)
                m_prev = m_sc[h]
                m_new = jnp.maximum(m_prev, jnp.broadcast_to(jnp.max(s, axis=1, keepdims=True), (tq, HB)))
                p = jnp.exp(s - jnp.tile(m_new, (1, tq // HB))).astype(BF)
                acc_sc[:, hb] = jnp.exp(m_prev - m_new) * acc_sc[:, hb] + _nn(p, v_ref[:, hb])
                m_sc[h] = m_new

        @pl.when(j < i)
        def _():
            step(False)

        @pl.when(j == i)
        def _():
            step(True)
            lse_ref[...] = jnp.zeros_like(lse_ref)
            for h in range(FOX_HEADS):
                l = acc_sc[:, h * HB + COL_A:h * HB + COL_A + 1]
                o_ref[:, h * FOX_HD:(h + 1) * FOX_HD] = acc_sc[:, h * HB:h * HB + FOX_HD] / l
                lse_ref[:, h:h + 1] = m_sc[h][:, 0:1] + jnp.log(l)

    qi = lambda i, j: (i, 0)
    kj = lambda i, j: (jnp.minimum(i, j), 0)
    return pl.pallas_call(
        body, name="fox_fwd", grid=(nq, nq),
        in_specs=[pl.BlockSpec((tq, AUG_W), qi), pl.BlockSpec((tq, AUG_W), kj), pl.BlockSpec((tq, AUG_W), kj)],
        out_specs=[pl.BlockSpec((tq, FOX_W), qi), pl.BlockSpec((tq, LANES), qi)],
        out_shape=[S((T, FOX_W), F32), S((T, LANES), F32)],
        scratch_shapes=[pltpu.VMEM((FOX_HEADS, tq, HB), F32), pltpu.VMEM((tq, AUG_W), F32)],
        compiler_params=_cp(2))(q, k, v)


def _fox_bwd(q, k, v, dob):
    T = q.shape[0]
    tq = _tile(T, 512)
    nq = T // tq
    half = AUG_W // 2
    hpg = FOX_HEADS // 2

    def body(q_ref, k_ref, v_ref, do_ref, dq_ref, dk_ref, dv_ref, dq_sc):
        j, i = pl.program_id(1), pl.program_id(2)

        @pl.when(jnp.logical_and(i == 0, j == 0))
        def _():
            dq_sc[...] = jnp.zeros_like(dq_sc)

        @pl.when(i == 0)
        def _():
            dk_ref[...] = jnp.zeros_like(dk_ref)
            dv_ref[...] = jnp.zeros_like(dv_ref)

        def step(masked):
            rows = pl.ds(pl.multiple_of(i * tq, tq), tq)
            mask = _tri(tq, True) if masked else None
            for h in range(hpg):
                hb = slice(h * HB, (h + 1) * HB)
                qh, kh, vh, doh = q_ref[:, hb], k_ref[:, hb], v_ref[:, hb], do_ref[:, hb]
                s = _nt(qh, kh)
                if masked:
                    s = jnp.where(mask, s, NEG)
                p = jnp.exp(s)
                dsb = (p * _nt(doh, vh)).astype(BF)
                dv_ref[:, hb] += _tn(p.astype(BF), doh)
                dk_ref[:, hb] += _tn(dsb, qh)
                dq_sc[rows, hb] += _nn(dsb, kh)

        @pl.when(i > j)
        def _():
            step(False)

        @pl.when(i == j)
        def _():
            step(True)
            dq_ref[...] = dq_sc[pl.ds(pl.multiple_of(j * tq, tq), tq), :]

    qi = lambda g, j, i: (jnp.maximum(i, j), g)
    kj = lambda g, j, i: (j, g)
    return pl.pallas_call(
        body, name="fox_bwd", grid=(2, nq, nq),
        in_specs=[pl.BlockSpec((tq, half), qi), pl.BlockSpec((tq, half), kj), pl.BlockSpec((tq, half), kj),
                  pl.BlockSpec((tq, half), qi)],
        out_specs=[pl.BlockSpec((tq, half), kj), pl.BlockSpec((tq, half), kj), pl.BlockSpec((tq, half), kj)],
        out_shape=[S((T, AUG_W), F32), S((T, AUG_W), F32), S((T, AUG_W), F32)],
        scratch_shapes=[pltpu.VMEM((T, half), F32)],
        compiler_params=_cp(3))(q, k, v, dob)


def _mix_out(attn, yg, g_fo, wout, x):
    T, D = x.shape
    tm = _tile(T, 512)

    def body(a_ref, y_ref, g_ref, w_ref, x_ref, o_ref):
        at = a_ref[...]
        yf = (at * _rstd(at) * g_ref[...]).astype(BF)
        o_ref[...] = x_ref[...] + _nn(yf, w_ref[:FOX_W, :]) + _nn(y_ref[...], w_ref[FOX_W:, :])

    row = lambda i: (i, 0)
    return pl.pallas_call(
        body, name="mix_out", grid=(T // tm,),
        in_specs=[pl.BlockSpec((tm, FOX_W), row), pl.BlockSpec((tm, GMLP_W), row),
                  pl.BlockSpec((1, FOX_W), lambda i: (0, 0)), pl.BlockSpec((D, D), lambda i: (0, 0)),
                  pl.BlockSpec((tm, D), row)],
        out_specs=pl.BlockSpec((tm, D), row),
        out_shape=S((T, D), F32),
        compiler_params=_cp(1))(attn, yg, g_fo, wout, x)


def _mix_out_bwd(dx, attn, yg, g_fo, wout, qf, lse):
    T, D = dx.shape
    tm = _tile(T, 512)
    n = T // tm
    spread, pc_l, pc_d = _spread_matrix(), _piece_matrix(COL_C), _piece_matrix(COL_A)

    def body(dx_ref, a_ref, y_ref, g_ref, w_ref, qf_ref, lse_ref, sp_ref, pl_ref, pd_ref,
             qb_ref, dob_ref, dyg_ref, dw_ref, dg_ref, acc_ref, dsum_ref):
        i = pl.program_id(0)
        dxb = dx_ref[...].astype(BF)
        at = a_ref[...]
        yf = (at * _rstd(at) * g_ref[...]).astype(BF)
        dy = _nt(dxb, w_ref[...])
        p_top = _tn(yf, dxb)
        p_bot = _tn(y_ref[...], dxb)

        @pl.when(i == 0)
        def _():
            acc_ref[:FOX_W, :] = p_top
            acc_ref[FOX_W:, :] = p_bot

        @pl.when(i > 0)
        def _():
            acc_ref[:FOX_W, :] += p_top
            acc_ref[FOX_W:, :] += p_bot

        @pl.when(i == n - 1)
        def _():
            dw_ref[...] = acc_ref[...].astype(BF)

        dat, dgr = _norm_bwd(dy[:, :FOX_W], at, g_ref[...])
        _acc_rows(dg_ref, i == 0, dgr)
        dyg_ref[...] = dy[:, FOX_W:]
        prod = dat * at
        dsum_ref[...] = jnp.zeros_like(dsum_ref)
        for h in range(FOX_HEADS):
            dsum_ref[:, h:h + 1] = jnp.sum(prod[:, h * FOX_HD:(h + 1) * FOX_HD], axis=1, keepdims=True)
        dob_ref[...] = (_nn(dat.astype(BF), sp_ref[...]) + _nn(_pieces(-dsum_ref[...]), pd_ref[...])).astype(BF)
        qb_ref[...] = (qf_ref[...].astype(F32) + _nn(_pieces(-lse_ref[...]), pl_ref[...])).astype(BF)

    row = lambda i: (i, 0)
    fix = lambda i: (0, 0)
    return pl.pallas_call(
        body, name="mix_out_bwd", grid=(n,),
        in_specs=[pl.BlockSpec((tm, D), row), pl.BlockSpec((tm, FOX_W), row), pl.BlockSpec((tm, GMLP_W), row),
                  pl.BlockSpec((1, FOX_W), fix), pl.BlockSpec((D, D), fix), pl.BlockSpec((tm, AUG_W), row),
                  pl.BlockSpec((tm, LANES), row), pl.BlockSpec((FOX_W, AUG_W), fix), pl.BlockSpec((LANES, AUG_W), fix),
                  pl.BlockSpec((LANES, AUG_W), fix)],
        out_specs=[pl.BlockSpec((tm, AUG_W), row), pl.BlockSpec((tm, AUG_W), row), pl.BlockSpec((tm, GMLP_W), row),
                   pl.BlockSpec((D, D), fix), pl.BlockSpec((1, FOX_W), fix)],
        out_shape=[S((T, AUG_W), BF), S((T, AUG_W), BF), S((T, GMLP_W), F32), S((D, D), BF), S((1, FOX_W), F32)],
        scratch_shapes=[pltpu.VMEM((D, D), F32), pltpu.VMEM((tm, LANES), F32)],
        compiler_params=_cp(1))(dx, attn, yg, g_fo, wout, qf, lse, spread, pc_l, pc_d)


def _mix_prep_bwd(z, dq, dk, dv, dyg, bf128, g_q, g_k, g_sgu, w_s, b_st, g_go):
    T = z.shape[0]
    tm = _tile(T, 512)
    n = T // tm

    def body(z_ref, dq_ref, dk_ref, dv_ref, dyg_ref, bf_ref, gq_ref, gk_ref, gs_ref, ws_ref,
             bst_ref, go_ref, dz_ref, dgq_ref, dgk_ref, dgs_ref, dgo_ref, dws_ref, dbst_ref, dbf_ref, carry_ref):
        i = pl.program_id(0)
        first = i == 0

        @pl.when(first)
        def _():
            carry_ref[...] = jnp.zeros_like(carry_ref)

        lane = lax.broadcasted_iota(jnp.int32, (tm, LANES), 1)
        dc = jnp.zeros((tm, LANES), F32)
        gq_rows, gk_rows = [], []
        for h in range(FOX_HEADS):
            hp = slice(h * HB, h * HB + FOX_HD)
            dqh, gqr = _norm_bwd(dq_ref[:, hp] * 0.125, z_ref[:, Z_Q + h * FOX_HD:Z_Q + (h + 1) * FOX_HD], gq_ref[...])
            dkh, gkr = _norm_bwd(dk_ref[:, hp], z_ref[:, Z_K + h * FOX_HD:Z_K + (h + 1) * FOX_HD], gk_ref[...])
            dz_ref[:, Z_Q + h * FOX_HD:Z_Q + (h + 1) * FOX_HD] = dqh.astype(BF)
            dz_ref[:, Z_K + h * FOX_HD:Z_K + (h + 1) * FOX_HD] = dkh.astype(BF)
            dz_ref[:, Z_V + h * FOX_HD:Z_V + (h + 1) * FOX_HD] = dv_ref[:, hp].astype(BF)
            dch = dq_ref[:, h * HB + COL_A:h * HB + COL_A + 1] - dk_ref[:, h * HB + COL_B:h * HB + COL_B + 1]
            dc = jnp.where(lane == h, dch, dc)
            gq_rows.append(gqr)
            gk_rows.append(gkr)
        _acc_rows(dgq_ref, first, functools.reduce(lambda a, b: a + b, gq_rows))
        _acc_rows(dgk_ref, first, functools.reduce(lambda a, b: a + b, gk_rows))

        dlogf = _hi(_tri(tm, False).astype(F32), dc) + carry_ref[...]
        carry_ref[...] = dlogf[0:1, :]
        fl = z_ref[:, Z_F:Z_F + LANES] + bf_ref[...]
        lane = lax.broadcasted_iota(jnp.int32, (tm, LANES), 1)
        df = jnp.where(lane < FOX_HEADS, dlogf * jax.nn.sigmoid(-fl), 0.0)
        dz_ref[:, Z_F:Z_F + LANES] = df.astype(BF)
        _acc_rows(dbf_ref, first, df)

        u_pre = z_ref[:, Z_U:Z_U + GMLP_W]
        vg_pre = z_ref[:, Z_G:Z_G + GMLP_W]
        u = _gelu(u_pre)
        vg = _gelu(vg_pre)
        vgn = (vg * _rstd(vg) * gs_ref[...]).astype(BF)
        bst = bst_ref[...]
        mixed, wms = _spatial_mix(vgn, ws_ref, bst, tm)
        sgu = u * mixed
        dsgu, gor = _norm_bwd(dyg_ref[...], sgu, go_ref[...])
        _acc_rows(dgo_ref, first, gor)
        du = dsgu * mixed
        dmixed = dsgu * u
        dmb = dmixed.astype(BF)
        tril = _tri(CHUNK, True)
        dvgn_rows = []
        dws = [None] * GMLP_G
        dbs = [None] * GMLP_G
        for c in range(tm // CHUNK):
            cs = slice(c * CHUNK, (c + 1) * CHUNK)
            cols = []
            for g in range(GMLP_G):
                gs = slice(g * GMLP_GD, (g + 1) * GMLP_GD)
                dmc = dmb[cs, gs]
                pw = _nt(dmc, vgn[cs, gs])
                pb = jnp.sum(dmixed[cs, gs], axis=1, keepdims=True)
                dws[g] = pw if dws[g] is None else dws[g] + pw
                dbs[g] = pb if dbs[g] is None else dbs[g] + pb
                cols.append(_tn(wms[g], dmc))
            dvgn_rows.append(jnp.concatenate(cols, axis=1))
        dvgn = jnp.concatenate(dvgn_rows, axis=0)
        dbs_t = jnp.concatenate(dbs, axis=1)
        for g in range(GMLP_G):
            dwg = jnp.where(tril, dws[g], 0.0)

            @pl.when(first)
            def _():
                dws_ref[g] = dwg

            @pl.when(jnp.logical_not(first))
            def _():
                dws_ref[g] += dwg

        @pl.when(first)
        def _():
            dbst_ref[...] = dbs_t

        @pl.when(jnp.logical_not(first))
        def _():
            dbst_ref[...] += dbs_t

        dvg, gsr = _norm_bwd(dvgn, vg, gs_ref[...])
        _acc_rows(dgs_ref, first, gsr)
        dz_ref[:, Z_U:Z_U + GMLP_W] = (du * _gelu_grad(u_pre)).astype(BF)
        dz_ref[:, Z_G:Z_G + GMLP_W] = (dvg * _gelu_grad(vg_pre)).astype(BF)

    rev = lambda i: (n - 1 - i, 0)
    fix = lambda i: (0, 0)
    fix3 = lambda i: (0, 0, 0)
    return pl.pallas_call(
        body, name="mix_prep_bwd", grid=(n,),
        in_specs=[pl.BlockSpec((tm, ZW), rev), pl.BlockSpec((tm, AUG_W), rev), pl.BlockSpec((tm, AUG_W), rev),
                  pl.BlockSpec((tm, AUG_W), rev), pl.BlockSpec((tm, GMLP_W), rev),
                  pl.BlockSpec((1, LANES), fix), pl.BlockSpec((1, FOX_HD), fix), pl.BlockSpec((1, FOX_HD), fix),
                  pl.BlockSpec((1, GMLP_W), fix), pl.BlockSpec((GMLP_G, CHUNK, CHUNK), fix3),
                  pl.BlockSpec((CHUNK, GMLP_G), fix), pl.BlockSpec((1, GMLP_W), fix)],
        out_specs=[pl.BlockSpec((tm, ZW), rev), pl.BlockSpec((1, FOX_HD), fix), pl.BlockSpec((1, FOX_HD), fix),
                   pl.BlockSpec((1, GMLP_W), fix), pl.BlockSpec((1, GMLP_W), fix),
                   pl.BlockSpec((GMLP_G, CHUNK, CHUNK), fix3), pl.BlockSpec((CHUNK, GMLP_G), fix),
                   pl.BlockSpec((1, LANES), fix)],
        out_shape=[S((T, ZW), BF), S((1, FOX_HD), F32), S((1, FOX_HD), F32), S((1, GMLP_W), F32), S((1, GMLP_W), F32),
                   S((GMLP_G, CHUNK, CHUNK), F32), S((CHUNK, GMLP_G), F32), S((1, LANES), F32)],
        scratch_shapes=[pltpu.VMEM((1, LANES), F32)],
        compiler_params=_cp(1))(z, dq, dk, dv, dyg, bf128, g_q, g_k, g_sgu, w_s, b_st, g_go)


def _mix_proj_bwd(dz, wz, x, g, dy):
    T, D = x.shape
    tm = _tile(T, 512)

    def body(dz_ref, w_ref, x_ref, g_ref, dy_ref, dx_ref, dxb_ref, dg_ref):
        dh = _nn(dz_ref[...], w_ref[...])
        dx, dgr = _norm_bwd(dh, x_ref[...], g_ref[...])
        dx = dx + dy_ref[...]
        dx_ref[...] = dx
        dxb_ref[...] = dx.astype(BF)
        _acc_rows(dg_ref, pl.program_id(0) == 0, dgr)

    row = lambda i: (i, 0)
    fix = lambda i: (0, 0)
    return pl.pallas_call(
        body, name="mix_proj_bwd", grid=(T // tm,),
        in_specs=[pl.BlockSpec((tm, ZW), row), pl.BlockSpec((ZW, D), fix), pl.BlockSpec((tm, D), row),
                  pl.BlockSpec((1, D), fix), pl.BlockSpec((tm, D), row)],
        out_specs=[pl.BlockSpec((tm, D), row), pl.BlockSpec((tm, D), row), pl.BlockSpec((1, D), fix)],
        out_shape=[S((T, D), F32), S((T, D), BF), S((1, D), F32)],
        compiler_params=_cp(1))(dz, wz, x, g, dy)


def _ca_kv(mem, g_mem, wckv, g_ck):
    M, D = mem.shape

    def body(m_ref, g_ref, w_ref, gk_ref, mn_ref, kr_ref, kn_ref, v_ref):
        mf = m_ref[...]
        mn = (mf * _rstd(mf) * g_ref[...]).astype(BF)
        mn_ref[...] = mn
        for h in range(CA_HEADS):
            kr = _nn(mn, w_ref[h])
            kr_ref[h] = kr
            kn_ref[h] = (kr * _rstd(kr) * gk_ref[...]).astype(BF)
            v_ref[h] = _nn(mn, w_ref[CA_HEADS + h]).astype(BF)

    hd = (CA_HEADS, M, CA_HD)
    return pl.pallas_call(
        body, name="ca_kv", out_shape=[S((M, D), BF), S(hd, F32), S(hd, BF), S(hd, BF)],
        compiler_params=pltpu.CompilerParams(vmem_limit_bytes=VMEM_LIMIT))(mem, g_mem, wckv, g_ck)


def _ca_tile_fwd(xt, gca, wcq, gcq, kn_ref, v_ref):
    hb = (xt * _rstd(xt) * gca).astype(BF)
    qc = _nn(hb, wcq)
    qr, qn, ps = [], [], []
    for h in range(CA_HEADS):
        qh = qc[:, h * CA_HD:(h + 1) * CA_HD]
        qnh = (qh * _rstd(qh) * gcq * 0.0625).astype(BF)
        s = _nt(qnh, kn_ref[h])
        e = jnp.exp(s - jnp.max(s, axis=1, keepdims=True))
        ps.append(e / jnp.sum(e, axis=1, keepdims=True))
        qr.append(qh)
        qn.append(qnh)
    return hb, qr, qn, ps


def _ca_fwd(x, g_ca, wcq, g_cq, kn, vv, wco):
    T, D = x.shape
    M = kn.shape[1]
    tm = _tile(T, 512)

    def body(x_ref, gca_ref, wcq_ref, gcq_ref, kn_ref, v_ref, wco_ref, o_ref, ob_sc):
        xt = x_ref[...]
        _, _, _, ps = _ca_tile_fwd(xt, gca_ref[...], wcq_ref[...], gcq_ref[...], kn_ref, v_ref)
        for h in range(CA_HEADS):
            ob_sc[:, h * CA_HD:(h + 1) * CA_HD] = _nn(ps[h].astype(BF), v_ref[h]).astype(BF)
        o_ref[...] = xt + _nn(ob_sc[...], wco_ref[...])

    row = lambda i: (i, 0)
    fix = lambda i: (0, 0)
    fix3 = lambda i: (0, 0, 0)
    return pl.pallas_call(
        body, name="ca_fwd", grid=(T // tm,),
        in_specs=[pl.BlockSpec((tm, D), row), pl.BlockSpec((1, D), fix), pl.BlockSpec((D, D), fix),
                  pl.BlockSpec((1, CA_HD), fix), pl.BlockSpec((CA_HEADS, M, CA_HD), fix3),
                  pl.BlockSpec((CA_HEADS, M, CA_HD), fix3), pl.BlockSpec((D, D), fix)],
        out_specs=pl.BlockSpec((tm, D), row), out_shape=S((T, D), F32),
        scratch_shapes=[pltpu.VMEM((tm, D), BF)],
        compiler_params=_cp(1))(x, g_ca, wcq, g_cq, kn, vv, wco)


def _ca_bwd(x, dy, g_ca, wcq, g_cq, kn, vv, wco):
    T, D = x.shape
    M = kn.shape[1]
    tm = _tile(T, 512)
    n = T // tm

    def body(x_ref, dy_ref, gca_ref, wcq_ref, gcq_ref, kn_ref, v_ref, wco_ref,
             dx_ref, dwq_ref, dwo_ref, dkn_ref, dv_ref, dgcq_ref, dgca_ref, aq_sc, ao_sc, ob_sc, dq_sc):
        i = pl.program_id(0)
        first = i == 0
        xt = x_ref[...]
        dyt = dy_ref[...]
        dyb = dyt.astype(BF)
        hb, qr, qn, ps = _ca_tile_fwd(xt, gca_ref[...], wcq_ref[...], gcq_ref[...], kn_ref, v_ref)
        do = _nt(dyb, wco_ref[...])
        gcq_rows = None
        for h in range(CA_HEADS):
            hs = slice(h * CA_HD, (h + 1) * CA_HD)
            p = ps[h]
            pb = p.astype(BF)
            ob_sc[:, hs] = _nn(pb, v_ref[h]).astype(BF)
            doh = do[:, hs].astype(BF)
            dp = _nt(doh, v_ref[h])
            ds = (p * (dp - jnp.sum(dp * p, axis=1, keepdims=True))).astype(BF)
            dvh = _tn(pb, doh)
            dkh = _tn(ds, qn[h])

            @pl.when(first)
            def _():
                dv_ref[h] = dvh
                dkn_ref[h] = dkh

            @pl.when(jnp.logical_not(first))
            def _():
                dv_ref[h] += dvh
                dkn_ref[h] += dkh

            dqn = _nn(ds, kn_ref[h]) * 0.0625
            dqh, gr = _norm_bwd(dqn, qr[h], gcq_ref[...])
            gcq_rows = gr if gcq_rows is None else gcq_rows + gr
            dq_sc[:, hs] = dqh.astype(BF)
        _acc_rows(dgcq_ref, first, gcq_rows)
        dqb = dq_sc[...]
        p_o = _tn(ob_sc[...], dyb)
        p_q = _tn(hb, dqb)

        @pl.when(first)
        def _():
            ao_sc[...] = p_o
            aq_sc[...] = p_q

        @pl.when(jnp.logical_not(first))
        def _():
            ao_sc[...] += p_o
            aq_sc[...] += p_q

        @pl.when(i == n - 1)
        def _():
            dwo_ref[...] = ao_sc[...].astype(BF)
            dwq_ref[...] = aq_sc[...].astype(BF)

        dh = _nt(dqb, wcq_ref[...])
        dx, gar = _norm_bwd(dh, xt, gca_ref[...])
        dx_ref[...] = dx + dyt
        _acc_rows(dgca_ref, first, gar)

    row = lambda i: (i, 0)
    fix = lambda i: (0, 0)
    fix3 = lambda i: (0, 0, 0)
    hd = (CA_HEADS, M, CA_HD)
    return pl.pallas_call(
        body, name="ca_bwd", grid=(n,),
        in_specs=[pl.BlockSpec((tm, D), row), pl.BlockSpec((tm, D), row), pl.BlockSpec((1, D), fix),
                  pl.BlockSpec((D, D), fix), pl.BlockSpec((1, CA_HD), fix), pl.BlockSpec(hd, fix3),
                  pl.BlockSpec(hd, fix3), pl.BlockSpec((D, D), fix)],
        out_specs=[pl.BlockSpec((tm, D), row), pl.BlockSpec((D, D), fix), pl.BlockSpec((D, D), fix),
                   pl.BlockSpec(hd, fix3), pl.BlockSpec(hd, fix3), pl.BlockSpec((1, CA_HD), fix),
                   pl.BlockSpec((1, D), fix)],
        out_shape=[S((T, D), F32), S((D, D), BF), S((D, D), BF), S(hd, F32), S(hd, F32), S((1, CA_HD), F32),
                   S((1, D), F32)],
        scratch_shapes=[pltpu.VMEM((D, D), F32), pltpu.VMEM((D, D), F32), pltpu.VMEM((tm, D), BF),
                        pltpu.VMEM((tm, D), BF)],
        compiler_params=_cp(1))(x, dy, g_ca, wcq, g_cq, kn, vv, wco)


def _ca_kv_bwd(mem, g_mem, mn, kraw, dkn, dvv, wckv, g_ck):
    M, D = mem.shape

    def body(m_ref, g_ref, mn_ref, kr_ref, dkn_ref, dv_ref, w_ref, gk_ref, dw_ref, dgk_ref, dgm_ref):
        mn = mn_ref[...]
        dmn = jnp.zeros((M, D), F32)
        gk_rows = None
        for h in range(CA_HEADS):
            dkr, gr = _norm_bwd(dkn_ref[h], kr_ref[h], gk_ref[...])
            gk_rows = gr if gk_rows is None else gk_rows + gr
            dkb = dkr.astype(BF)
            dvb = dv_ref[h].astype(BF)
            dw_ref[h] = _tn(mn, dkb).astype(BF)
            dw_ref[CA_HEADS + h] = _tn(mn, dvb).astype(BF)
            dmn = dmn + _nt(dkb, w_ref[h]) + _nt(dvb, w_ref[CA_HEADS + h])
        dgk_ref[...] = jnp.sum(gk_rows, axis=0, keepdims=True)
        mf = m_ref[...]
        dgm_ref[...] = jnp.sum(dmn * (mf * _rstd(mf)), axis=0, keepdims=True)

    return pl.pallas_call(
        body, name="ca_kv_bwd",
        out_shape=[S((2 * CA_HEADS, D, CA_HD), BF), S((1, CA_HD), F32), S((1, D), F32)],
        compiler_params=pltpu.CompilerParams(vmem_limit_bytes=VMEM_LIMIT))(mem, g_mem, mn, kraw, dkn, dvv, wckv, g_ck)


def _after(g, token):
    return g if token is None else g + token[0:1, 0:1]


def _local_step(x, mem, target, small, weights, emit):
    T, D = x.shape
    p = small
    bf128 = jnp.pad(p["b_f"], ((0, 0), (0, LANES - FOX_HEADS)))
    b_st = p["b_s"].T

    wup1 = weights("ffn1_up", x)["wup1"]
    a1, h1 = _ffn_up("ffn1_up", x, p["g_ffn1"], wup1)
    wdn1 = weights("ffn1_dn", h1)["wdn1"]
    x1 = _ffn_down("ffn1_down", a1, wdn1, x)
    wm = weights("mix", x1)
    z, h2 = _mix_proj(x1, p["g_mix"], wm["wz"])
    qf, ka, va, yg = _mix_prep(z, bf128, p["g_q"], p["g_k"], p["g_sgu"], p["w_s"], b_st, p["g_gmlp_o"])
    attn, lse = _fox_fwd(qf, ka, va)
    x2 = _mix_out(attn, yg, p["g_fox_o"], wm["wout"], x1)
    wc = weights("ca", x2)
    mn, kraw, ckn, cvv = _ca_kv(mem, p["g_mem"], wc["wckv"], p["g_ck"])
    x3 = _ca_fwd(x2, p["g_ca"], wc["wcq"], p["g_cq"], ckn, cvv, wc["wco"])
    w2 = weights("ffn2", x3)
    a2, h4 = _ffn_up("ffn2_up", x3, p["g_ffn2"], w2["wup2"])
    dy4, dy4b, sq = _ffn_down_loss("ffn2_down", a2, w2["wdn2"], x3, target)

    gs = {}
    dgu2 = _ffn_bwd_act("ffn2_bwd_act", dy4b, h4, w2["wup2"], w2["wdn2"])
    tok = emit("ffn2", {"wup2": _ffn_dwup("ffn2", h4, dgu2), "wdn2": _ffn_dwdn("ffn2", a2, dy4b)})
    dx3, gs["g_ffn2"] = _ffn_dx("ffn2_dx", dgu2, w2["wup2"], x3, _after(p["g_ffn2"], tok), dy4)

    dx2, dwcq, dwco, dckn, dcvv, gs["g_cq"], gs["g_ca"] = _ca_bwd(
        x2, dx3, p["g_ca"], wc["wcq"], p["g_cq"], ckn, cvv, wc["wco"])
    dwckv, gs["g_ck"], gs["g_mem"] = _ca_kv_bwd(mem, p["g_mem"], mn, kraw, dckn, dcvv, wc["wckv"], p["g_ck"])

    qb, dob, dyg, dwout, gs["g_fox_o"] = _mix_out_bwd(dx2, attn, yg, p["g_fox_o"], wm["wout"], qf, lse)
    dq, dk, dv = _fox_bwd(qb, ka, va, dob)
    dz, gs["g_q"], gs["g_k"], gs["g_sgu"], gs["g_gmlp_o"], gs["w_s"], dbst, dbf = _mix_prep_bwd(
        z, dq, dk, dv, dyg, bf128, p["g_q"], p["g_k"], p["g_sgu"], p["w_s"], b_st, p["g_gmlp_o"])
    gs["b_s"] = dbst.T
    gs["b_f"] = dbf[:, :FOX_HEADS]
    tok_ws = emit("w_s", {"w_s": gs["w_s"]})
    tk = _tile(T, 1024)
    zb = ZW // 3
    dwz = _tn_matmul(
        "mix_dwz", dz, pl.BlockSpec((tk, zb), lambda j, k: (k, j)), h2, pl.BlockSpec((tk, D), lambda j, k: (k, 0)),
        S((ZW, D), F32), pl.BlockSpec((zb, D), lambda j, k: (j, 0)), (3, T // tk), (zb, D))
    tok = emit("mid", {"wcq": dwcq, "wco": dwco, "wckv": dwckv, "wout": dwout, "wz": dwz})
    dx1, dx1b, gs["g_mix"] = _mix_proj_bwd(dz, wm["wz"], x1, _after(_after(p["g_mix"], tok), tok_ws), dx2)

    dgu1 = _ffn_bwd_act("ffn1_bwd_act", dx1b, h1, wup1, wdn1)
    tok = emit("ffn1_dn", {"wdn1": _ffn_dwdn("ffn1", a1, dx1b)})
    tok = emit("ffn1_up", {"wup1": _ffn_dwup("ffn1", h1, dgu1, after=tok)})
    dx0, gs["g_ffn1"] = _ffn_dx("ffn1_dx", dgu1, wup1, x, _after(p["g_ffn1"], tok), dx1)
    return sq, dx0, gs


MESH = pl.DeviceIdType.MESH
HBM_SPEC = pl.BlockSpec(memory_space=pltpu.HBM)
N_PEER = N_DEV - 1


def _place():
    return lax.axis_index("x"), lax.axis_index("y"), lax.axis_index("c")


def _slot(px, py, pc):
    return 4 * px + 2 * py + pc


SEM_SPEC = pl.BlockSpec(memory_space=pltpu.SEMAPHORE)
ANY_SPEC = pl.BlockSpec(memory_space=pl.ANY)
DATAFLOW = pltpu.SideEffectType.DATAFLOW_SIDE_EFFECTING


def _hbm(a):
    return pltpu.with_memory_space_constraint(a, pltpu.HBM)


def _peer(x, y, c, r):
    return (1 - x if r & 4 else x, 1 - y if r & 2 else y, 1 - c if r & 1 else c)


def _place_own(srcs, whole):
    my = _slot(*_place())
    lands = []
    for s in srcs:
        blk = s[None] if whole else lax.dynamic_slice_in_dim(s, my, 1, 0)
        shape = (N_DEV,) + s.shape if whole else s.shape
        lands.append(lax.dynamic_update_slice_in_dim(lax.empty(shape, s.dtype), blk, my, 0))
    return lands


ALL_PEERS = tuple(range(1, N_DEV))
NEAR_PEERS = (1, 2, 4, 6)
SAME_CORE = (2, 4, 6)


def _copy_start(name, srcs, lands, whole, peers=None):
    n = len(srcs)
    peers = peers or [ALL_PEERS] * n

    def body(*refs):
        src, land = refs[:n], refs[n:2 * n]
        send, recv = refs[2 * n:3 * n], refs[3 * n:4 * n]
        token = refs[6 * n]
        x, y, c = _place()
        my = _slot(x, y, c)
        for a in range(n):
            for r in peers[a]:
                p = _peer(x, y, c, r)
                pltpu.make_async_remote_copy(
                    src_ref=src[a] if whole else src[a].at[_slot(*p)], dst_ref=land[a].at[my],
                    send_sem=send[a].at[r - 1], recv_sem=recv[a].at[r - 1], device_id=p, device_id_type=MESH).start()
        token[...] = jnp.zeros_like(token)

    out = pl.pallas_call(
        body, name=name,
        out_shape=([pltpu.SemaphoreType.DMA((N_PEER,))] * (2 * n)
                   + [pltpu.HBM(s.shape, s.dtype) for s in srcs] + [pltpu.HBM(s.shape, s.dtype) for s in lands]
                   + [S((8, LANES), F32)]),
        in_specs=[HBM_SPEC] * (2 * n),
        out_specs=[SEM_SPEC] * (2 * n) + [HBM_SPEC] * (2 * n) + [pl.BlockSpec(memory_space=pltpu.VMEM)],
        input_output_aliases={i: 2 * n + i for i in range(2 * n)},
        compiler_params=pltpu.CompilerParams(has_side_effects=DATAFLOW),
    )(*[_hbm(s) for s in srcs], *[_hbm(s) for s in lands])
    return out[:n], out[n:2 * n], out[2 * n:3 * n], out[3 * n:4 * n], out[4 * n]


def _copy_wait(name, srcs, lands, send, recv, after, whole, peers=None):
    n = len(srcs)
    peers = peers or [ALL_PEERS] * n

    def body(*refs):
        src, land = refs[:n], refs[n:2 * n]
        snd, rcv = refs[2 * n:3 * n], refs[3 * n:4 * n]
        x, y, c = _place()
        for a in range(n):
            for r in peers[a]:
                p = _peer(x, y, c, r)
                ps = _slot(*p)
                cp = pltpu.make_async_remote_copy(
                    src_ref=src[a] if whole else src[a].at[ps], dst_ref=land[a].at[ps],
                    send_sem=snd[a].at[r - 1], recv_sem=rcv[a].at[r - 1], device_id=p, device_id_type=MESH)
                cp.wait_send()
                cp.wait_recv()

    out = pl.pallas_call(
        body, name=name,
        out_shape=[pltpu.HBM(s.shape, s.dtype) for s in srcs] + [pltpu.HBM(s.shape, s.dtype) for s in lands],
        in_specs=[HBM_SPEC] * (2 * n) + [SEM_SPEC] * (2 * n) + [ANY_SPEC],
        out_specs=[HBM_SPEC] * (2 * n),
        input_output_aliases={i: i for i in range(2 * n)},
        compiler_params=pltpu.CompilerParams(has_side_effects=DATAFLOW),
    )(*srcs, *lands, *send, *recv, after)
    return out[n:]


def _forward_start(name, lands):
    n = len(lands)

    def body(*refs):
        land = refs[:n]
        send, recv = refs[n:2 * n], refs[2 * n:3 * n]
        token = refs[4 * n]
        x, y, c = _place()
        for a in range(n):
            for r in SAME_CORE:
                blk = land[a].at[_slot(*_peer(x, y, c, r))]
                pltpu.make_async_remote_copy(
                    src_ref=blk, dst_ref=blk, send_sem=send[a].at[r - 1], recv_sem=recv[a].at[r - 1],
                    device_id=(x, y, 1 - c), device_id_type=MESH).start()
        token[...] = jnp.zeros_like(token)

    out = pl.pallas_call(
        body, name=name,
        out_shape=([pltpu.SemaphoreType.DMA((N_PEER,))] * (2 * n) + [pltpu.HBM(s.shape, s.dtype) for s in lands]
                   + [S((8, LANES), F32)]),
        in_specs=[HBM_SPEC] * n,
        out_specs=[SEM_SPEC] * (2 * n) + [HBM_SPEC] * n + [pl.BlockSpec(memory_space=pltpu.VMEM)],
        input_output_aliases={i: 2 * n + i for i in range(n)},
        compiler_params=pltpu.CompilerParams(has_side_effects=DATAFLOW),
    )(*[_hbm(s) for s in lands])
    return out[:n], out[n:2 * n], out[2 * n:3 * n], out[3 * n]


def _forward_wait(name, lands, send, recv, after):
    n = len(lands)

    def body(*refs):
        land = refs[:n]
        snd, rcv = refs[n:2 * n], refs[2 * n:3 * n]
        x, y, c = _place()
        for a in range(n):
            for r in SAME_CORE:
                cp = pltpu.make_async_remote_copy(
                    src_ref=land[a].at[_slot(*_peer(x, y, c, r))], dst_ref=land[a].at[_slot(*_peer(x, y, c, r | 1))],
                    send_sem=snd[a].at[r - 1], recv_sem=rcv[a].at[r - 1], device_id=(x, y, 1 - c),
                    device_id_type=MESH)
                cp.wait_send()
                cp.wait_recv()

    return pl.pallas_call(
        body, name=name,
        out_shape=[pltpu.HBM(s.shape, s.dtype) for s in lands],
        in_specs=[HBM_SPEC] * n + [SEM_SPEC] * (2 * n) + [ANY_SPEC],
        out_specs=[HBM_SPEC] * n,
        input_output_aliases={i: i for i in range(n)},
        compiler_params=pltpu.CompilerParams(has_side_effects=DATAFLOW),
    )(*lands, *send, *recv, after)


def _adamw(w, g, m, v):
    m2 = ADAM_B1 * m + (1.0 - ADAM_B1) * g
    v2 = ADAM_B2 * v + (1.0 - ADAM_B2) * (g * g)
    m_hat = m2 / (1.0 - ADAM_B1 ** ADAM_STEP)
    v_hat = v2 / (1.0 - ADAM_B2 ** ADAM_STEP)
    delta = -ADAM_LR * (m_hat / (jnp.sqrt(v_hat) + ADAM_EPS) + ADAM_WD * w)
    return delta, m2, v2


def _adamw_big(name, slots, w, m, v):
    R, C = w.shape
    tr = next((t for t in (256, 352) if R % t == 0), R)

    def body(s_ref, w_ref, m_ref, v_ref, g_ref, d_ref, m2_ref, v2_ref):
        g = s_ref[0].astype(F32)
        for k in range(1, N_DEV):
            g = g + s_ref[k].astype(F32)
        d, m2, v2 = _adamw(w_ref[...], g, m_ref[...], v_ref[...])
        g_ref[...] = g
        d_ref[...] = d
        m2_ref[...] = m2
        v2_ref[...] = v2

    row = pl.BlockSpec((tr, C), lambda i: (i, 0))
    return pl.pallas_call(
        body, name=name, grid=(R // tr,),
        in_specs=[pl.BlockSpec((N_DEV, tr, C), lambda i: (0, i, 0)), row, row, row],
        out_specs=[row] * 4, out_shape=[S((R, C), F32)] * 4,
        compiler_params=_cp(1))(slots, w, m, v)


TINY_ROWS = (("b_s", 8), ("g_ffn1", 8), ("g_mix", 8), ("g_ca", 8), ("g_mem", 8), ("g_ffn2", 8), ("g_sgu", 4),
             ("g_fox_o", 4), ("g_gmlp_o", 4), ("g_cq", 2), ("g_ck", 2), ("g_q", 1), ("g_k", 1), ("b_f", 1),
             ("loss", 1))
TINY_P = 72


def _tiny_pieces(width):
    return [(j, slice(j * LANES, min((j + 1) * LANES, width))) for j in range(-(-width // LANES))]


def _pack_tiny(grads, sq):
    names = [n for n, _ in TINY_ROWS if n != "loss"]

    def body(*refs):
        ins, sq_ref, o_ref = refs[:len(names)], refs[len(names)], refs[len(names) + 1]
        o_ref[...] = jnp.zeros_like(o_ref)
        at = 0
        for ref, (name, r) in zip(ins, TINY_ROWS):
            if name == "b_s":
                o_ref[at:at + r, :] = ref[...]
            else:
                for j, cols in _tiny_pieces(ref.shape[1]):
                    o_ref[at + j:at + j + 1, 0:cols.stop - cols.start] = ref[:, cols]
            at += r
        o_ref[at:at + 1, :] = sq_ref[0:1, :]

    return pl.pallas_call(body, name="tiny_pack", out_shape=S((TINY_P, LANES), F32))(
        *[grads[n] for n in names], sq)


def _adamw_tiny(slots, w, m, v):
    names = [n for n, _ in TINY_ROWS if n != "loss"]
    k = len(names)

    def body(s_ref, *refs):
        ins, outs, loss_ref = refs[:3 * k], refs[3 * k:7 * k], refs[7 * k]
        g_all = s_ref[0]
        for d in range(1, N_DEV):
            g_all = g_all + s_ref[d]
        at = 0
        for i, (name, r) in enumerate(TINY_ROWS[:k]):
            w_ref, m_ref, v_ref = ins[i], ins[k + i], ins[2 * k + i]
            o = outs[4 * i:4 * i + 4]
            if name == "b_s":
                pieces = [(slice(at, at + r), slice(0, LANES), (slice(None), slice(None)))]
            else:
                pieces = [(slice(at + j, at + j + 1), slice(0, c.stop - c.start), (slice(None), c))
                          for j, c in _tiny_pieces(w_ref.shape[1])]
            for rows, lanes, dst in pieces:
                g = g_all[rows, lanes]
                res = (g,) + _adamw(w_ref[dst], g, m_ref[dst], v_ref[dst])
                for ref, val in zip(o, res):
                    ref[dst] = val
            at += r
        loss_ref[...] = g_all[at:at + 1, :]

    shapes = [S(w[n].shape, F32) for n in names]
    out = pl.pallas_call(
        body, name="adamw_tiny", out_shape=[s for s in shapes for _ in range(4)] + [S((1, LANES), F32)],
    )(slots, *[w[n] for n in names], *[m[n] for n in names], *[v[n] for n in names])
    stores = ({}, {}, {}, {})
    for i, n in enumerate(names):
        for store, t in zip(stores, out[4 * i:4 * i + 4]):
            store[n] = t
    return stores, out[4 * k]


WEIGHTS =('g_ffn1', 'w_ffn1_in', 'w_ffn1_out', 'g_mix', 'w_in', 'b_f', 'g_q', 'g_k', 'g_sgu', 'w_s', 'b_s',
           'g_fox_o', 'g_gmlp_o', 'w_out', 'g_ca', 'g_mem', 'w_cq', 'w_ckv', 'g_cq', 'g_ck', 'w_co', 'g_ffn2',
           'w_ffn2_in', 'w_ffn2_out')
BIG = ('w_ffn1_in', 'w_ffn1_out', 'w_in', 'w_out', 'w_cq', 'w_ckv', 'w_co', 'w_ffn2_in', 'w_ffn2_out')
TRANSPOSED = ('w_ffn1_in', 'w_in', 'w_ffn2_in')
TWO_LEVEL = ('w_ffn1_in', 'w_in')
GATHER_GROUPS = {"ffn1_up": ("w_ffn1_in",), "ffn1_dn": ("w_ffn1_out",), "mix": ("w_in", "w_out"),
                 "ca": ("w_cq", "w_ckv", "w_co"), "ffn2": ("w_ffn2_in", "w_ffn2_out")}
QKV_W = 3 * FOX_W
UV_OFF = QKV_W + FOX_HEADS


def kernel(x, mem, g_ffn1, w_ffn1_in, w_ffn1_out, g_mix, w_in, b_f, g_q, g_k, g_sgu, w_s, b_s, g_fox_o, g_gmlp_o, w_out, g_ca, g_mem, w_cq, w_ckv, g_cq, g_ck, w_co, g_ffn2, w_ffn2_in, w_ffn2_out, loss_target, m_g_ffn1, m_w_ffn1_in, m_w_ffn1_out, m_g_mix, m_w_in, m_b_f, m_g_q, m_g_k, m_g_sgu, m_w_s, m_b_s, m_g_fox_o, m_g_gmlp_o, m_w_out, m_g_ca, m_g_mem, m_w_cq, m_w_ckv, m_g_cq, m_g_ck, m_w_co, m_g_ffn2, m_w_ffn2_in, m_w_ffn2_out, v_g_ffn1, v_w_ffn1_in, v_w_ffn1_out, v_g_mix, v_w_in, v_b_f, v_g_q, v_g_k, v_g_sgu, v_w_s, v_b_s, v_g_fox_o, v_g_gmlp_o, v_w_out, v_g_ca, v_g_mem, v_w_cq, v_w_ckv, v_g_cq, v_g_ck, v_w_co, v_g_ffn2, v_w_ffn2_in, v_w_ffn2_out):
    args = dict(locals())
    w = {n: args[n] for n in WEIGHTS}
    mo = {n: args["m_" + n] for n in WEIGHTS}
    vo = {n: args["v_" + n] for n in WEIGHTS}
    D = D_MODEL

    def local(n, a):
        return a[0].T if n in TRANSPOSED else a[0]

    shards = [local(n, w[n]).astype(BF) for n in BIG]
    fb = shards[0].shape[0]
    g_peers = [NEAR_PEERS if n in TWO_LEVEL else ALL_PEERS for n in BIG]
    g_snd, g_rcv, g_src, g_land, g_token = _copy_start("gather_start", shards, _place_own(shards, True), True,
                                                       peers=g_peers)
    handles = {n: (g_src[i], g_land[i], g_snd[i], g_rcv[i]) for i, n in enumerate(BIG)}

    tiny_names = [n for n, _ in TINY_ROWS if n != "loss"]

    def weights(group, after):
        names = GATHER_GROUPS[group]
        hs = [handles[n] for n in names]
        got = list(_copy_wait("gather_wait_" + group, [h[0] for h in hs], [h[1] for h in hs], [h[2] for h in hs],
                              [h[3] for h in hs], after, True, peers=[g_peers[BIG.index(n)] for n in names]))
        passed = [i for i, n in enumerate(names) if n in TWO_LEVEL]
        if passed:
            f_snd, f_rcv, f_land, f_token = _forward_start("gather_pass_start_" + group, [got[i] for i in passed])
            for i, t in zip(passed, _forward_wait("gather_pass_wait_" + group, f_land, f_snd, f_rcv, f_token)):
                got[i] = t
        got = dict(zip(names, got))
        if group == "ffn1_up":
            return {"wup1": got["w_ffn1_in"].reshape(2, N_FFN_BLK, fb, D)}
        if group == "ffn1_dn":
            return {"wdn1": got["w_ffn1_out"].reshape(N_FFN_BLK, fb, D)}
        if group == "mix":
            full = got["w_in"].reshape(-1, D)
            wz = jnp.concatenate([full[:QKV_W], full[UV_OFF:], full[QKV_W:UV_OFF],
                                  jnp.zeros((LANES - FOX_HEADS, D), BF)], axis=0)
            return {"wz": wz, "wout": got["w_out"].reshape(D, D)}
        if group == "ca":
            return {"wcq": got["w_cq"].reshape(D, D), "wco": got["w_co"].reshape(D, D), "wckv": got["w_ckv"]}
        return {"wup2": got["w_ffn2_in"].reshape(2, N_FFN_BLK, fb, D),
                "wdn2": got["w_ffn2_out"].reshape(N_FFN_BLK, fb, D)}

    flying = {}

    def emit(group, g):
        if group == "w_s":
            part = [g["w_s"].reshape(-1, LANES)]
            *copies, token = _copy_start("w_s_start", part, _place_own(part, True), True)
            flying[group] = copies
            return token
        if group == "ffn2":
            parts = {"w_ffn2_in": g["wup2"], "w_ffn2_out": g["wdn2"].reshape(N_DEV, -1, D)}
        elif group == "ffn1_dn":
            parts = {"w_ffn1_out": g["wdn1"].reshape(N_DEV, -1, D)}
        elif group == "ffn1_up":
            parts = {"w_ffn1_in": g["wup1"]}
        else:
            gz = g["wz"]
            g_in = jnp.concatenate([gz[:QKV_W], gz[Z_F:Z_F + FOX_HEADS], gz[QKV_W:Z_F]], axis=0)
            parts = {"w_in": g_in.reshape(N_DEV, -1, D).astype(BF),
                     "w_out": g["wout"].reshape(N_DEV, -1, D), "w_cq": g["wcq"].reshape(N_DEV, -1, D),
                     "w_co": g["wco"].reshape(N_DEV, -1, D), "w_ckv": g["wckv"]}
        names = list(parts)
        srcs = [parts[n] for n in names]
        *copies, token = _copy_start("exchange_start_" + group, srcs, _place_own(srcs, False), False)
        flying[group] = (names, copies)
        return token

    small = {n: (w[n][0] if n == "b_s" else w[n]) for n in tiny_names}
    small["w_s"] = w["w_s"][0]

    sq, dx0, gs = _local_step(x[0], mem[0], loss_target[0], small, weights, emit)

    sm_parts = [_pack_tiny(gs, sq)]
    sm_snd, sm_rcv, sm_src, sm_land, sm_token = _copy_start("tiny_start", sm_parts, _place_own(sm_parts, True), True)

    grad, delta, new_m, new_v = {}, {}, {}, {}

    def update(group, after):
        names, (snd, rcv, srcs, lands) = flying[group]
        slots = _copy_wait("exchange_wait_" + group, srcs, lands, snd, rcv, after, False)
        for n, sl in zip(names, slots):
            g, d, m2, v2 = _adamw_big("adamw_" + n, sl, local(n, w[n]), local(n, mo[n]), local(n, vo[n]))
            grad[n], delta[n], new_m[n], new_v[n] = (
                (t.T if n in TRANSPOSED else t).reshape(w[n].shape) for t in (g, d, m2, v2))
        return d

    last = update("ffn2", sm_token)
    last = update("mid", last)
    last = update("ffn1_dn", last)
    last = update("ffn1_up", last)
    ws_snd, ws_rcv, ws_src, ws_land = flying["w_s"]
    ws_all, = _copy_wait("w_s_wait", ws_src, ws_land, ws_snd, ws_rcv, last, True)
    tiny_all, = _copy_wait("tiny_wait", sm_src, sm_land, sm_snd, sm_rcv, ws_all, True)
    ws_shape = w["w_s"].shape
    for store, t in zip((grad, delta, new_m, new_v), _adamw_big(
            "adamw_w_s", ws_all, *[a["w_s"].reshape(-1, LANES) for a in (w, mo, vo)])):
        store["w_s"] = t.reshape(ws_shape)
    stores, loss_row = _adamw_tiny(tiny_all, *[{n: (a[n][0] if n == "b_s" else a[n]) for n in tiny_names}
                                               for a in (w, mo, vo)])
    for store, t in zip((grad, delta, new_m, new_v), stores):
        store.update({n: v.reshape(w[n].shape) for n, v in t.items()})
    loss = loss_row[0, 0] * (0.5 / D)

    return (loss, dx0[None], *[grad[n] for n in WEIGHTS], *[delta[n] for n in WEIGHTS],
            *[new_m[n] for n in WEIGHTS], *[new_v[n] for n in WEIGHTS])
```

```python
import functools

import jax
import jax.numpy as jnp
from jax import lax
from jax.experimental import pallas as pl
from jax.experimental.pallas import tpu as pltpu

F32 = jnp.float32
BF = jnp.bfloat16
S = jax.ShapeDtypeStruct

N_DEV = 8
D_MODEL = 1024
FOX_HEADS, FOX_HD = 8, 64
FOX_W = 512
GMLP_G, GMLP_GD = 8, 64
GMLP_W = 512
CHUNK = 128
CA_HEADS, CA_HD = 4, 256
N_FFN_BLK = 4
ZW = 2688
Z_Q, Z_K, Z_V, Z_U, Z_G, Z_F = 0, 512, 1024, 1536, 2048, 2560
EPS = 1e-6
NEG = -1e30
LANES = 128

ADAM_LR, ADAM_B1, ADAM_B2, ADAM_EPS, ADAM_WD, ADAM_STEP = 0.001, 0.9, 0.999, 1e-08, 0.01, 10

VMEM_LIMIT = 52 * 2 ** 20


def _cp(n_axes):
    return pltpu.CompilerParams(dimension_semantics=("arbitrary",) * n_axes, vmem_limit_bytes=VMEM_LIMIT)


def _nn(a, b):
    return jnp.dot(a, b, preferred_element_type=F32)


def _nt(a, b):
    return lax.dot_general(a, b, (((1,), (1,)), ((), ())), preferred_element_type=F32)


def _tn(a, b):
    return lax.dot_general(a, b, (((0,), (0,)), ((), ())), preferred_element_type=F32)


def _hi(a, b):
    return jnp.dot(a, b, precision=lax.Precision.HIGHEST, preferred_element_type=F32)


def _rstd(x):
    return lax.rsqrt(jnp.mean(x * x, axis=-1, keepdims=True) + EPS)


def _norm_bwd(dy, x, g):
    r = _rstd(x)
    xh = x * r
    dxh = dy * g
    dx = r * (dxh - xh * jnp.mean(dxh * xh, axis=-1, keepdims=True))
    return dx, dy * xh


def _acc_rows(ref, first, val):
    srow = jnp.sum(val, axis=0, keepdims=True)

    @pl.when(first)
    def _():
        ref[...] = srow

    @pl.when(jnp.logical_not(first))
    def _():
        ref[...] += srow


def _gelu(x):
    c = 0.7978845608028654
    return 0.5 * x * (1.0 + jnp.tanh(c * (x + 0.044715 * x * x * x)))


def _gelu_grad(x):
    c = 0.7978845608028654
    t = jnp.tanh(c * (x + 0.044715 * x * x * x))
    return 0.5 * (1.0 + t) + 0.5 * x * (1.0 - t * t) * c * (1.0 + 3 * 0.044715 * x * x)


def _tile(n, pref):
    return pref if n % pref == 0 else n


def _ffn_up(name, x, g, wup):
    T, D = x.shape
    FB = wup.shape[-2]
    tm = _tile(T, 1024)

    def body(x_ref, g_ref, w_ref, a_ref, h_ref):
        @pl.when(pl.program_id(1) == 0)
        def _():
            xf = x_ref[...]
            h_ref[...] = (xf * _rstd(xf) * g_ref[...]).astype(BF)

        hb = h_ref[...]
        gg = _nt(hb, w_ref[0])
        uu = _nt(hb, w_ref[1])
        a_ref[...] = (gg * jax.nn.sigmoid(gg) * uu).astype(BF)

    return pl.pallas_call(
        body, name=name, grid=(T // tm, N_FFN_BLK),
        in_specs=[pl.BlockSpec((tm, D), lambda i, j: (i, 0)),
                  pl.BlockSpec((1, D), lambda i, j: (0, 0)),
                  pl.BlockSpec((2, None, FB, D), lambda i, j: (0, j, 0, 0))],
        out_specs=[pl.BlockSpec((None, tm, FB), lambda i, j: (j, i, 0)),
                   pl.BlockSpec((tm, D), lambda i, j: (i, 0))],
        out_shape=[S((N_FFN_BLK, T, FB), BF), S((T, D), BF)],
        compiler_params=_cp(2))(x, g, wup)


def _ffn_down(name, a, wdn, x):
    _, T, FB = a.shape
    D = x.shape[1]
    tm = _tile(T, 512)

    def body(a_ref, w_ref, x_ref, o_ref):
        p = _nn(a_ref[0], w_ref[0])
        for j in range(1, N_FFN_BLK):
            p = p + _nn(a_ref[j], w_ref[j])
        o_ref[...] = x_ref[...] + 0.5 * p

    return pl.pallas_call(
        body, name=name, grid=(T // tm,),
        in_specs=[pl.BlockSpec((N_FFN_BLK, tm, FB), lambda i: (0, i, 0)),
                  pl.BlockSpec((N_FFN_BLK, FB, D), lambda i: (0, 0, 0)),
                  pl.BlockSpec((tm, D), lambda i: (i, 0))],
        out_specs=pl.BlockSpec((tm, D), lambda i: (i, 0)),
        out_shape=S((T, D), F32),
        compiler_params=_cp(1))(a, wdn, x)


def _ffn_down_loss(name, a, wdn, x, target):
    _, T, FB = a.shape
    D = x.shape[1]
    tm = _tile(T, 512)

    def body(a_ref, w_ref, x_ref, t_ref, d_ref, db_ref, loss_ref):
        i = pl.program_id(0)
        p = _nn(a_ref[0], w_ref[0])
        for j in range(1, N_FFN_BLK):
            p = p + _nn(a_ref[j], w_ref[j])
        diff = (x_ref[...] + 0.5 * p) - t_ref[...]
        dy = diff * (1.0 / D)
        d_ref[...] = dy
        db_ref[...] = dy.astype(BF)
        sq = jnp.zeros((8, LANES), F32) + jnp.sum(diff * diff)

        @pl.when(i == 0)
        def _():
            loss_ref[...] = sq

        @pl.when(i > 0)
        def _():
            loss_ref[...] += sq

    row = pl.BlockSpec((tm, D), lambda i: (i, 0))
    return pl.pallas_call(
        body, name=name, grid=(T // tm,),
        in_specs=[pl.BlockSpec((N_FFN_BLK, tm, FB), lambda i: (0, i, 0)),
                  pl.BlockSpec((N_FFN_BLK, FB, D), lambda i: (0, 0, 0)), row, row],
        out_specs=[row, row, pl.BlockSpec((8, LANES), lambda i: (0, 0))],
        out_shape=[S((T, D), F32), S((T, D), BF), S((8, LANES), F32)],
        compiler_params=_cp(1))(a, wdn, x, target)


def _ffn_bwd_act(name, dyb, h, wup, wdn):
    T, D = h.shape
    FB = wup.shape[-2]
    tm = _tile(T, 1024)

    def body(d_ref, h_ref, wu_ref, wd_ref, o_ref):
        da = 0.5 * _nt(d_ref[...], wd_ref[...])
        hb = h_ref[...]
        gg = _nt(hb, wu_ref[0])
        uu = _nt(hb, wu_ref[1])
        sg = jax.nn.sigmoid(gg)
        o_ref[0] = (da * uu * (sg * (1.0 + gg * (1.0 - sg)))).astype(BF)
        o_ref[1] = (da * (gg * sg)).astype(BF)

    return pl.pallas_call(
        body, name=name, grid=(T // tm, N_FFN_BLK),
        in_specs=[pl.BlockSpec((tm, D), lambda i, j: (i, 0)),
                  pl.BlockSpec((tm, D), lambda i, j: (i, 0)),
                  pl.BlockSpec((2, None, FB, D), lambda i, j: (0, j, 0, 0)),
                  pl.BlockSpec((None, FB, D), lambda i, j: (j, 0, 0))],
        out_specs=pl.BlockSpec((2, None, tm, FB), lambda i, j: (0, j, i, 0)),
        out_shape=S((2, N_FFN_BLK, T, FB), BF),
        compiler_params=_cp(2))(dyb, h, wup, wdn)


def _ffn_dx(name, dgu, wup, x, g, dy):
    T, D = x.shape
    FB = wup.shape[-2]
    tm = _tile(T, 1024)

    def body(d_ref, w_ref, x_ref, g_ref, dy_ref, dx_ref, dg_ref, acc_ref):
        i, j = pl.program_id(0), pl.program_id(1)
        p = _nn(d_ref[0], w_ref[0]) + _nn(d_ref[1], w_ref[1])

        @pl.when(j == 0)
        def _():
            acc_ref[...] = p

        @pl.when(j > 0)
        def _():
            acc_ref[...] += p

        @pl.when(j == N_FFN_BLK - 1)
        def _():
            dx, dgr = _norm_bwd(acc_ref[...], x_ref[...], g_ref[...])
            dx_ref[...] = dx + dy_ref[...]
            _acc_rows(dg_ref, i == 0, dgr)

    return pl.pallas_call(
        body, name=name, grid=(T // tm, N_FFN_BLK),
        in_specs=[pl.BlockSpec((2, None, tm, FB), lambda i, j: (0, j, i, 0)),
                  pl.BlockSpec((2, None, FB, D), lambda i, j: (0, j, 0, 0)),
                  pl.BlockSpec((tm, D), lambda i, j: (i, 0)),
                  pl.BlockSpec((1, D), lambda i, j: (0, 0)),
                  pl.BlockSpec((tm, D), lambda i, j: (i, 0))],
        out_specs=[pl.BlockSpec((tm, D), lambda i, j: (i, 0)),
                   pl.BlockSpec((1, D), lambda i, j: (0, 0))],
        out_shape=[S((T, D), F32), S((1, D), F32)],
        scratch_shapes=[pltpu.VMEM((tm, D), F32)],
        compiler_params=_cp(2))(dgu, wup, x, g, dy)


def _tn_matmul(name, a, a_spec, b, b_spec, out_shape, out_spec, grid, acc_shape, scale=1.0, after=None):
    nk = grid[1]
    extra = [] if after is None else [after]

    def body(a_ref, b_ref, *rest):
        o_ref, acc_ref = rest[-2:]
        k = pl.program_id(1)
        p = _tn(a_ref[...], b_ref[...].astype(a_ref.dtype))

        @pl.when(k == 0)
        def _():
            acc_ref[...] = p

        @pl.when(k > 0)
        def _():
            acc_ref[...] += p

        @pl.when(k == nk - 1)
        def _():
            o_ref[...] = (acc_ref[...] * scale).astype(o_ref.dtype)

    return pl.pallas_call(
        body, name=name, grid=grid,
        in_specs=[a_spec, b_spec] + [pl.BlockSpec((8, LANES), lambda j, k: (0, 0)) for _ in extra],
        out_specs=out_spec, out_shape=out_shape,
        scratch_shapes=[pltpu.VMEM(acc_shape, F32)], compiler_params=_cp(2))(a, b, *extra)


def _ffn_dwup(name, h, dgu, after=None):
    T, D = h.shape
    FB = dgu.shape[-1]
    tk = _tile(T, 1024)
    return _tn_matmul(
        name + "_dwup", dgu.reshape(2 * N_FFN_BLK, T, FB), pl.BlockSpec((None, tk, FB), lambda j, k: (j, k, 0)),
        h, pl.BlockSpec((tk, D), lambda j, k: (k, 0)),
        S((2 * N_FFN_BLK, FB, D), BF), pl.BlockSpec((None, FB, D), lambda j, k: (j, 0, 0)),
        (2 * N_FFN_BLK, T // tk), (FB, D), after=after)


def _ffn_dwdn(name, a, dyb):
    _, T, FB = a.shape
    D = dyb.shape[1]
    tk = _tile(T, 1024)
    return _tn_matmul(
        name + "_dwdn", a, pl.BlockSpec((None, tk, FB), lambda j, k: (j, k, 0)),
        dyb, pl.BlockSpec((tk, D), lambda j, k: (k, 0)),
        S((N_FFN_BLK, FB, D), BF), pl.BlockSpec((None, FB, D), lambda j, k: (j, 0, 0)),
        (N_FFN_BLK, T // tk), (FB, D), scale=0.5)


def _mix_proj(x, g, wz):
    T, D = x.shape
    tm = _tile(T, 512)

    def body(x_ref, g_ref, w_ref, z_ref, h_ref):
        xf = x_ref[...]
        hb = (xf * _rstd(xf) * g_ref[...]).astype(BF)
        h_ref[...] = hb
        z_ref[...] = _nt(hb, w_ref[...])

    return pl.pallas_call(
        body, name="mix_proj", grid=(T // tm,),
        in_specs=[pl.BlockSpec((tm, D), lambda i: (i, 0)),
                  pl.BlockSpec((1, D), lambda i: (0, 0)),
                  pl.BlockSpec((ZW, D), lambda i: (0, 0))],
        out_specs=[pl.BlockSpec((tm, ZW), lambda i: (i, 0)),
                   pl.BlockSpec((tm, D), lambda i: (i, 0))],
        out_shape=[S((T, ZW), F32), S((T, D), BF)],
        compiler_params=_cp(1))(x, g, wz)


def _tri(n, lower):
    r = lax.broadcasted_iota(jnp.int32, (n, n), 0)
    c = lax.broadcasted_iota(jnp.int32, (n, n), 1)
    return (r >= c) if lower else (r <= c)


def _spatial_mix(vgn_b, ws_ref, bst, tm):
    tril = _tri(CHUNK, True)
    wms = [jnp.where(tril, ws_ref[g], 0.0).astype(BF) for g in range(GMLP_G)]
    rows = []
    for c in range(tm // CHUNK):
        cols = []
        for g in range(GMLP_G):
            vs = vgn_b[c * CHUNK:(c + 1) * CHUNK, g * GMLP_GD:(g + 1) * GMLP_GD]
            cols.append(_nn(wms[g], vs) + bst[:, g:g + 1])
        rows.append(jnp.concatenate(cols, axis=1))
    return jnp.concatenate(rows, axis=0), wms


HB = 128
AUG_W = FOX_HEADS * HB
COL_A, COL_B, COL_C = 64, 67, 70


def _spread_matrix():
    r = jnp.arange(FOX_W)
    return (jnp.arange(AUG_W)[None, :] == ((r // FOX_HD) * HB + r % FOX_HD)[:, None]).astype(BF)


def _piece_matrix(col):
    r = jnp.arange(LANES)
    dst = jnp.where(r < 3 * FOX_HEADS, (r % FOX_HEADS) * HB + col + r // FOX_HEADS, -1)
    return (jnp.arange(AUG_W)[None, :] == dst[:, None]).astype(BF)


def _ones_row(cols):
    c = jnp.arange(AUG_W) % HB
    hit = functools.reduce(jnp.logical_or, [(c >= a) & (c < a + 3) for a in cols])
    return hit.astype(F32)[None, :]


def _pieces(x):
    lane = lax.broadcasted_iota(jnp.int32, x.shape, 1)
    x = jnp.where(lane < FOX_HEADS, x, 0.0)
    hi = x.astype(BF).astype(F32)
    r1 = x - hi
    mid = r1.astype(BF).astype(F32)
    lo = (r1 - mid).astype(BF).astype(F32)
    return (hi + pltpu.roll(mid, FOX_HEADS, 1) + pltpu.roll(lo, 2 * FOX_HEADS, 1)).astype(BF)


def _mix_prep(z, bf128, g_q, g_k, g_sgu, w_s, b_st, g_go):
    T = z.shape[0]
    tm = _tile(T, 512)
    spread, pc_q, pc_k = _spread_matrix(), _piece_matrix(COL_A), _piece_matrix(COL_B)
    one_q, one_k, one_v = _ones_row([COL_B]), _ones_row([COL_A, COL_C]), _ones_row([COL_A])

    def body(z_ref, bf_ref, gq_ref, gk_ref, gs_ref, ws_ref, bst_ref, go_ref, sp_ref, pq_ref, pk_ref, oq_ref, ok_ref,
             ov_ref, q_ref, k_ref, v_ref, y_ref, carry_ref, qn_sc, kn_sc):
        i = pl.program_id(0)

        @pl.when(i == 0)
        def _():
            carry_ref[...] = jnp.zeros_like(carry_ref)

        for h in range(FOX_HEADS):
            hs = slice(h * FOX_HD, (h + 1) * FOX_HD)
            qh = z_ref[:, Z_Q + h * FOX_HD:Z_Q + (h + 1) * FOX_HD]
            kh = z_ref[:, Z_K + h * FOX_HD:Z_K + (h + 1) * FOX_HD]
            qn_sc[:, hs] = (qh * _rstd(qh) * gq_ref[...] * 0.125).astype(BF)
            kn_sc[:, hs] = (kh * _rstd(kh) * gk_ref[...]).astype(BF)

        fl = z_ref[:, Z_F:Z_F + LANES] + bf_ref[...]
        logf = jnp.minimum(fl, 0.0) - jnp.log1p(jnp.exp(-jnp.abs(fl)))
        csum = _hi(_tri(tm, True).astype(F32), logf) + carry_ref[...]
        carry_ref[...] = csum[tm - 1:tm, :]
        sp = sp_ref[...]
        q_ref[...] = (_nn(qn_sc[...], sp) + _nn(_pieces(csum), pq_ref[...]) + oq_ref[...]).astype(BF)
        k_ref[...] = (_nn(kn_sc[...], sp) + _nn(_pieces(-csum), pk_ref[...]) + ok_ref[...]).astype(BF)
        v_ref[...] = (_nn(z_ref[:, Z_V:Z_V + FOX_W].astype(BF), sp) + ov_ref[...]).astype(BF)

        u = _gelu(z_ref[:, Z_U:Z_U + GMLP_W])
        vg = _gelu(z_ref[:, Z_G:Z_G + GMLP_W])
        vgn = (vg * _rstd(vg) * gs_ref[...]).astype(BF)
        mixed, _ = _spatial_mix(vgn, ws_ref, bst_ref[...], tm)
        sgu = u * mixed
        y_ref[...] = (sgu * _rstd(sgu) * go_ref[...]).astype(BF)

    row = lambda i: (i, 0)
    fix2 = lambda i: (0, 0)
    return pl.pallas_call(
        body, name="mix_prep", grid=(T // tm,),
        in_specs=[pl.BlockSpec((tm, ZW), row),
                  pl.BlockSpec((1, LANES), fix2), pl.BlockSpec((1, FOX_HD), fix2), pl.BlockSpec((1, FOX_HD), fix2),
                  pl.BlockSpec((1, GMLP_W), fix2), pl.BlockSpec((GMLP_G, CHUNK, CHUNK), lambda i: (0, 0, 0)),
                  pl.BlockSpec((CHUNK, GMLP_G), fix2), pl.BlockSpec((1, GMLP_W), fix2),
                  pl.BlockSpec((FOX_W, AUG_W), fix2), pl.BlockSpec((LANES, AUG_W), fix2),
                  pl.BlockSpec((LANES, AUG_W), fix2), pl.BlockSpec((1, AUG_W), fix2), pl.BlockSpec((1, AUG_W), fix2),
                  pl.BlockSpec((1, AUG_W), fix2)],
        out_specs=[pl.BlockSpec((tm, AUG_W), row), pl.BlockSpec((tm, AUG_W), row), pl.BlockSpec((tm, AUG_W), row),
                   pl.BlockSpec((tm, GMLP_W), row)],
        out_shape=[S((T, AUG_W), BF), S((T, AUG_W), BF), S((T, AUG_W), BF), S((T, GMLP_W), BF)],
        scratch_shapes=[pltpu.VMEM((1, LANES), F32), pltpu.VMEM((tm, FOX_W), BF), pltpu.VMEM((tm, FOX_W), BF)],
        compiler_params=_cp(1))(z, bf128, g_q, g_k, g_sgu, w_s, b_st, g_go, spread, pc_q, pc_k, one_q, one_k, one_v)


def _fox_fwd(q, k, v):
    T = q.shape[0]
    tq = _tile(T, 1024)
    nq = T // tq

    def body(q_ref, k_ref, v_ref, o_ref, lse_ref, m_sc, acc_sc):
        i, j = pl.program_id(0), pl.program_id(1)

        @pl.when(j == 0)
        def _():
            m_sc[...] = jnp.full(m_sc.shape, NEG, F32)
            acc_sc[...] = jnp.zeros_like(acc_sc)

        def step(masked):
            mask = _tri(tq, True) if masked else None
            for h in range(FOX_HEADS):
                hb = slice(h * HB, (h + 1) * HB)
                s = _nt(q_ref[:, hb], k_ref[:, hb])
                if masked:
                    s = jnp.where(mask, s, NEG)
                m_prev = m_sc[h]
                m_new = jnp.maximum(m_prev, jnp.broadcast_to(jnp.max(s, axis=1, keepdims=True), (tq, HB)))
                p = jnp.exp(s - jnp.tile(m_new, (1, tq // HB))).astype(BF)
                acc_sc[:, hb] = jnp.exp(m_prev - m_new) * acc_sc[:, hb] + _nn(p, v_ref[:, hb])
                m_sc[h] = m_new

        @pl.when(j < i)
        def _():
            step(False)

        @pl.when(j == i)
        def _():
            step(True)
            lse_ref[...] = jnp.zeros_like(lse_ref)
            for h in range(FOX_HEADS):
                l = acc_sc[:, h * HB + COL_A:h * HB + COL_A + 1]
                o_ref[:, h * FOX_HD:(h + 1) * FOX_HD] = acc_sc[:, h * HB:h * HB + FOX_HD] / l
                lse_ref[:, h:h + 1] = m_sc[h][:, 0:1] + jnp.log(l)

    qi = lambda i, j: (i, 0)
    kj = lambda i, j: (jnp.minimum(i, j), 0)
    return pl.pallas_call(
        body, name="fox_fwd", grid=(nq, nq),
        in_specs=[pl.BlockSpec((tq, AUG_W), qi), pl.BlockSpec((tq, AUG_W), kj), pl.BlockSpec((tq, AUG_W), kj)],
        out_specs=[pl.BlockSpec((tq, FOX_W), qi), pl.BlockSpec((tq, LANES), qi)],
        out_shape=[S((T, FOX_W), F32), S((T, LANES), F32)],
        scratch_shapes=[pltpu.VMEM((FOX_HEADS, tq, HB), F32), pltpu.VMEM((tq, AUG_W), F32)],
        compiler_params=_cp(2))(q, k, v)


def _fox_bwd(q, k, v, dob):
    T = q.shape[0]
    tq = _tile(T, 512)
    nq = T // tq
    half = AUG_W // 2
    hpg = FOX_HEADS // 2

    def body(q_ref, k_ref, v_ref, do_ref, dq_ref, dk_ref, dv_ref, dq_sc):
        j, i = pl.program_id(1), pl.program_id(2)

        @pl.when(jnp.logical_and(i == 0, j == 0))
        def _():
            dq_sc[...] = jnp.zeros_like(dq_sc)

        @pl.when(i == 0)
        def _():
            dk_ref[...] = jnp.zeros_like(dk_ref)
            dv_ref[...] = jnp.zeros_like(dv_ref)

        def step(masked):
            rows = pl.ds(pl.multiple_of(i * tq, tq), tq)
            mask = _tri(tq, True) if masked else None
            for h in range(hpg):
                hb = slice(h * HB, (h + 1) * HB)
                qh, kh, vh, doh = q_ref[:, hb], k_ref[:, hb], v_ref[:, hb], do_ref[:, hb]
                s = _nt(qh, kh)
                if masked:
                    s = jnp.where(mask, s, NEG)
                p = jnp.exp(s)
                dsb = (p * _nt(doh, vh)).astype(BF)
                dv_ref[:, hb] += _tn(p.astype(BF), doh)
                dk_ref[:, hb] += _tn(dsb, qh)
                dq_sc[rows, hb] += _nn(dsb, kh)

        @pl.when(i > j)
        def _():
            step(False)

        @pl.when(i == j)
        def _():
            step(True)
            dq_ref[...] = dq_sc[pl.ds(pl.multiple_of(j * tq, tq), tq), :]

    qi = lambda g, j, i: (jnp.maximum(i, j), g)
    kj = lambda g, j, i: (j, g)
    return pl.pallas_call(
        body, name="fox_bwd", grid=(2, nq, nq),
        in_specs=[pl.BlockSpec((tq, half), qi), pl.BlockSpec((tq, half), kj), pl.BlockSpec((tq, half), kj),
                  pl.BlockSpec((tq, half), qi)],
        out_specs=[pl.BlockSpec((tq, half), kj), pl.BlockSpec((tq, half), kj), pl.BlockSpec((tq, half), kj)],
        out_shape=[S((T, AUG_W), F32), S((T, AUG_W), F32), S((T, AUG_W), F32)],
        scratch_shapes=[pltpu.VMEM((T, half), F32)],
        compiler_params=_cp(3))(q, k, v, dob)


def _mix_out(attn, yg, g_fo, wout, x):
    T, D = x.shape
    tm = _tile(T, 512)

    def body(a_ref, y_ref, g_ref, w_ref, x_ref, o_ref):
        at = a_ref[...]
        yf = (at * _rstd(at) * g_ref[...]).astype(BF)
        o_ref[...] = x_ref[...] + _nn(yf, w_ref[:FOX_W, :]) + _nn(y_ref[...], w_ref[FOX_W:, :])

    row = lambda i: (i, 0)
    return pl.pallas_call(
        body, name="mix_out", grid=(T // tm,),
        in_specs=[pl.BlockSpec((tm, FOX_W), row), pl.BlockSpec((tm, GMLP_W), row),
                  pl.BlockSpec((1, FOX_W), lambda i: (0, 0)), pl.BlockSpec((D, D), lambda i: (0, 0)),
                  pl.BlockSpec((tm, D), row)],
        out_specs=pl.BlockSpec((tm, D), row),
        out_shape=S((T, D), F32),
        compiler_params=_cp(1))(attn, yg, g_fo, wout, x)


def _mix_out_bwd(dx, attn, yg, g_fo, wout, qf, lse):
    T, D = dx.shape
    tm = _tile(T, 512)
    n = T // tm
    spread, pc_l, pc_d = _spread_matrix(), _piece_matrix(COL_C), _piece_matrix(COL_A)

    def body(dx_ref, a_ref, y_ref, g_ref, w_ref, qf_ref, lse_ref, sp_ref, pl_ref, pd_ref,
             qb_ref, dob_ref, dyg_ref, yy_ref, dg_ref, dsum_ref):
        i = pl.program_id(0)
        dxb = dx_ref[...].astype(BF)
        at = a_ref[...]
        yy_ref[:, :FOX_W] = (at * _rstd(at) * g_ref[...]).astype(BF)
        yy_ref[:, FOX_W:] = y_ref[...]
        dy = _nt(dxb, w_ref[...])
        dat, dgr = _norm_bwd(dy[:, :FOX_W], at, g_ref[...])
        _acc_rows(dg_ref, i == 0, dgr)
        dyg_ref[...] = dy[:, FOX_W:]
        prod = dat * at
        dsum_ref[...] = jnp.zeros_like(dsum_ref)
        for h in range(FOX_HEADS):
            dsum_ref[:, h:h + 1] = jnp.sum(prod[:, h * FOX_HD:(h + 1) * FOX_HD], axis=1, keepdims=True)
        dob_ref[...] = (_nn(dat.astype(BF), sp_ref[...]) + _nn(_pieces(-dsum_ref[...]), pd_ref[...])).astype(BF)
        qb_ref[...] = (qf_ref[...].astype(F32) + _nn(_pieces(-lse_ref[...]), pl_ref[...])).astype(BF)

    row = lambda i: (i, 0)
    fix = lambda i: (0, 0)
    return pl.pallas_call(
        body, name="mix_out_bwd", grid=(n,),
        in_specs=[pl.BlockSpec((tm, D), row), pl.BlockSpec((tm, FOX_W), row), pl.BlockSpec((tm, GMLP_W), row),
                  pl.BlockSpec((1, FOX_W), fix), pl.BlockSpec((D, D), fix), pl.BlockSpec((tm, AUG_W), row),
                  pl.BlockSpec((tm, LANES), row), pl.BlockSpec((FOX_W, AUG_W), fix), pl.BlockSpec((LANES, AUG_W), fix),
                  pl.BlockSpec((LANES, AUG_W), fix)],
        out_specs=[pl.BlockSpec((tm, AUG_W), row), pl.BlockSpec((tm, AUG_W), row), pl.BlockSpec((tm, GMLP_W), row),
                   pl.BlockSpec((tm, D), row), pl.BlockSpec((1, FOX_W), fix)],
        out_shape=[S((T, AUG_W), BF), S((T, AUG_W), BF), S((T, GMLP_W), F32), S((T, D), BF), S((1, FOX_W), F32)],
        scratch_shapes=[pltpu.VMEM((tm, LANES), F32)],
        compiler_params=_cp(1))(dx, attn, yg, g_fo, wout, qf, lse, spread, pc_l, pc_d)


def _mix_prep_bwd(z, dq, dk, dv, dyg, bf128, g_q, g_k, g_sgu, w_s, b_st, g_go):
    T = z.shape[0]
    tm = _tile(T, 512)
    n = T // tm

    def body(z_ref, dq_ref, dk_ref, dv_ref, dyg_ref, bf_ref, gq_ref, gk_ref, gs_ref, ws_ref,
             bst_ref, go_ref, dz_ref, dgq_ref, dgk_ref, dgs_ref, dgo_ref, dws_ref, dbst_ref, dbf_ref, carry_ref):
        i = pl.program_id(0)
        first = i == 0

        @pl.when(first)
        def _():
            carry_ref[...] = jnp.zeros_like(carry_ref)

        lane = lax.broadcasted_iota(jnp.int32, (tm, LANES), 1)
        dc = jnp.zeros((tm, LANES), F32)
        gq_rows, gk_rows = [], []
        for h in range(FOX_HEADS):
            hp = slice(h * HB, h * HB + FOX_HD)
            dqh, gqr = _norm_bwd(dq_ref[:, hp] * 0.125, z_ref[:, Z_Q + h * FOX_HD:Z_Q + (h + 1) * FOX_HD], gq_ref[...])
            dkh, gkr = _norm_bwd(dk_ref[:, hp], z_ref[:, Z_K + h * FOX_HD:Z_K + (h + 1) * FOX_HD], gk_ref[...])
            dz_ref[:, Z_Q + h * FOX_HD:Z_Q + (h + 1) * FOX_HD] = dqh.astype(BF)
            dz_ref[:, Z_K + h * FOX_HD:Z_K + (h + 1) * FOX_HD] = dkh.astype(BF)
            dz_ref[:, Z_V + h * FOX_HD:Z_V + (h + 1) * FOX_HD] = dv_ref[:, hp].astype(BF)
            dch = dq_ref[:, h * HB + COL_A:h * HB + COL_A + 1] - dk_ref[:, h * HB + COL_B:h * HB + COL_B + 1]
            dc = jnp.where(lane == h, dch, dc)
            gq_rows.append(gqr)
            gk_rows.append(gkr)
        _acc_rows(dgq_ref, first, functools.reduce(lambda a, b: a + b, gq_rows))
        _acc_rows(dgk_ref, first, functools.reduce(lambda a, b: a + b, gk_rows))

        dlogf = _hi(_tri(tm, False).astype(F32), dc) + carry_ref[...]
        carry_ref[...] = dlogf[0:1, :]
        fl = z_ref[:, Z_F:Z_F + LANES] + bf_ref[...]
        lane = lax.broadcasted_iota(jnp.int32, (tm, LANES), 1)
        df = jnp.where(lane < FOX_HEADS, dlogf * jax.nn.sigmoid(-fl), 0.0)
        dz_ref[:, Z_F:Z_F + LANES] = df.astype(BF)
        _acc_rows(dbf_ref, first, df)

        u_pre = z_ref[:, Z_U:Z_U + GMLP_W]
        vg_pre = z_ref[:, Z_G:Z_G + GMLP_W]
        u = _gelu(u_pre)
        vg = _gelu(vg_pre)
        vgn = (vg * _rstd(vg) * gs_ref[...]).astype(BF)
        bst = bst_ref[...]
        mixed, wms = _spatial_mix(vgn, ws_ref, bst, tm)
        sgu = u * mixed
        dsgu, gor = _norm_bwd(dyg_ref[...], sgu, go_ref[...])
        _acc_rows(dgo_ref, first, gor)
        du = dsgu * mixed
        dmixed = dsgu * u
        dmb = dmixed.astype(BF)
        tril = _tri(CHUNK, True)
        dvgn_rows = []
        dws = [None] * GMLP_G
        dbs = [None] * GMLP_G
        for c in range(tm // CHUNK):
            cs = slice(c * CHUNK, (c + 1) * CHUNK)
            cols = []
            for g in range(GMLP_G):
                gs = slice(g * GMLP_GD, (g + 1) * GMLP_GD)
                dmc = dmb[cs, gs]
                pw = _nt(dmc, vgn[cs, gs])
                pb = jnp.sum(dmixed[cs, gs], axis=1, keepdims=True)
                dws[g] = pw if dws[g] is None else dws[g] + pw
                dbs[g] = pb if dbs[g] is None else dbs[g] + pb
                cols.append(_tn(wms[g], dmc))
            dvgn_rows.append(jnp.concatenate(cols, axis=1))
        dvgn = jnp.concatenate(dvgn_rows, axis=0)
        dbs_t = jnp.concatenate(dbs, axis=1)
        for g in range(GMLP_G):
            dwg = jnp.where(tril, dws[g], 0.0)

            @pl.when(first)
            def _():
                dws_ref[g] = dwg

            @pl.when(jnp.logical_not(first))
            def _():
                dws_ref[g] += dwg

        @pl.when(first)
        def _():
            dbst_ref[...] = dbs_t

        @pl.when(jnp.logical_not(first))
        def _():
            dbst_ref[...] += dbs_t

        dvg, gsr = _norm_bwd(dvgn, vg, gs_ref[...])
        _acc_rows(dgs_ref, first, gsr)
        dz_ref[:, Z_U:Z_U + GMLP_W] = (du * _gelu_grad(u_pre)).astype(BF)
        dz_ref[:, Z_G:Z_G + GMLP_W] = (dvg * _gelu_grad(vg_pre)).astype(BF)

    rev = lambda i: (n - 1 - i, 0)
    fix = lambda i: (0, 0)
    fix3 = lambda i: (0, 0, 0)
    return pl.pallas_call(
        body, name="mix_prep_bwd", grid=(n,),
        in_specs=[pl.BlockSpec((tm, ZW), rev), pl.BlockSpec((tm, AUG_W), rev), pl.BlockSpec((tm, AUG_W), rev),
                  pl.BlockSpec((tm, AUG_W), rev), pl.BlockSpec((tm, GMLP_W), rev),
                  pl.BlockSpec((1, LANES), fix), pl.BlockSpec((1, FOX_HD), fix), pl.BlockSpec((1, FOX_HD), fix),
                  pl.BlockSpec((1, GMLP_W), fix), pl.BlockSpec((GMLP_G, CHUNK, CHUNK), fix3),
                  pl.BlockSpec((CHUNK, GMLP_G), fix), pl.BlockSpec((1, GMLP_W), fix)],
        out_specs=[pl.BlockSpec((tm, ZW), rev), pl.BlockSpec((1, FOX_HD), fix), pl.BlockSpec((1, FOX_HD), fix),
                   pl.BlockSpec((1, GMLP_W), fix), pl.BlockSpec((1, GMLP_W), fix),
                   pl.BlockSpec((GMLP_G, CHUNK, CHUNK), fix3), pl.BlockSpec((CHUNK, GMLP_G), fix),
                   pl.BlockSpec((1, LANES), fix)],
        out_shape=[S((T, ZW), BF), S((1, FOX_HD), F32), S((1, FOX_HD), F32), S((1, GMLP_W), F32), S((1, GMLP_W), F32),
                   S((GMLP_G, CHUNK, CHUNK), F32), S((CHUNK, GMLP_G), F32), S((1, LANES), F32)],
        scratch_shapes=[pltpu.VMEM((1, LANES), F32)],
        compiler_params=_cp(1))(z, dq, dk, dv, dyg, bf128, g_q, g_k, g_sgu, w_s, b_st, g_go)


def _mix_proj_bwd(dz, wz, x, g, dy):
    T, D = x.shape
    tm = _tile(T, 512)

    def body(dz_ref, w_ref, x_ref, g_ref, dy_ref, dx_ref, dxb_ref, dg_ref):
        dh = _nn(dz_ref[...], w_ref[...])
        dx, dgr = _norm_bwd(dh, x_ref[...], g_ref[...])
        dx = dx + dy_ref[...]
        dx_ref[...] = dx
        dxb_ref[...] = dx.astype(BF)
        _acc_rows(dg_ref, pl.program_id(0) == 0, dgr)

    row = lambda i: (i, 0)
    fix = lambda i: (0, 0)
    return pl.pallas_call(
        body, name="mix_proj_bwd", grid=(T // tm,),
        in_specs=[pl.BlockSpec((tm, ZW), row), pl.BlockSpec((ZW, D), fix), pl.BlockSpec((tm, D), row),
                  pl.BlockSpec((1, D), fix), pl.BlockSpec((tm, D), row)],
        out_specs=[pl.BlockSpec((tm, D), row), pl.BlockSpec((tm, D), row), pl.BlockSpec((1, D), fix)],
        out_shape=[S((T, D), F32), S((T, D), BF), S((1, D), F32)],
        compiler_params=_cp(1))(dz, wz, x, g, dy)


def _ca_kv(mem, g_mem, wckv, g_ck):
    M, D = mem.shape

    def body(m_ref, g_ref, w_ref, gk_ref, mn_ref, kr_ref, kn_ref, v_ref):
        mf = m_ref[...]
        mn = (mf * _rstd(mf) * g_ref[...]).astype(BF)
        mn_ref[...] = mn
        for h in range(CA_HEADS):
            kr = _nn(mn, w_ref[h])
            kr_ref[h] = kr
            kn_ref[h] = (kr * _rstd(kr) * gk_ref[...]).astype(BF)
            v_ref[h] = _nn(mn, w_ref[CA_HEADS + h]).astype(BF)

    hd = (CA_HEADS, M, CA_HD)
    return pl.pallas_call(
        body, name="ca_kv", out_shape=[S((M, D), BF), S(hd, F32), S(hd, BF), S(hd, BF)],
        compiler_params=pltpu.CompilerParams(vmem_limit_bytes=VMEM_LIMIT))(mem, g_mem, wckv, g_ck)


def _ca_tile_fwd(xt, gca, wcq, gcq, kn_ref, v_ref):
    hb = (xt * _rstd(xt) * gca).astype(BF)
    qc = _nn(hb, wcq)
    qr, qn, ps = [], [], []
    for h in range(CA_HEADS):
        qh = qc[:, h * CA_HD:(h + 1) * CA_HD]
        qnh = (qh * _rstd(qh) * gcq * 0.0625).astype(BF)
        s = _nt(qnh, kn_ref[h])
        e = jnp.exp(s - jnp.max(s, axis=1, keepdims=True))
        ps.append(e / jnp.sum(e, axis=1, keepdims=True))
        qr.append(qh)
        qn.append(qnh)
    return hb, qr, qn, ps


def _ca_fwd(x, g_ca, wcq, g_cq, kn, vv, wco):
    T, D = x.shape
    M = kn.shape[1]
    tm = _tile(T, 512)

    def body(x_ref, gca_ref, wcq_ref, gcq_ref, kn_ref, v_ref, wco_ref, o_ref, ob_sc):
        xt = x_ref[...]
        _, _, _, ps = _ca_tile_fwd(xt, gca_ref[...], wcq_ref[...], gcq_ref[...], kn_ref, v_ref)
        for h in range(CA_HEADS):
            ob_sc[:, h * CA_HD:(h + 1) * CA_HD] = _nn(ps[h].astype(BF), v_ref[h]).astype(BF)
        o_ref[...] = xt + _nn(ob_sc[...], wco_ref[...])

    row = lambda i: (i, 0)
    fix = lambda i: (0, 0)
    fix3 = lambda i: (0, 0, 0)
    return pl.pallas_call(
        body, name="ca_fwd", grid=(T // tm,),
        in_specs=[pl.BlockSpec((tm, D), row), pl.BlockSpec((1, D), fix), pl.BlockSpec((D, D), fix),
                  pl.BlockSpec((1, CA_HD), fix), pl.BlockSpec((CA_HEADS, M, CA_HD), fix3),
                  pl.BlockSpec((CA_HEADS, M, CA_HD), fix3), pl.BlockSpec((D, D), fix)],
        out_specs=pl.BlockSpec((tm, D), row), out_shape=S((T, D), F32),
        scratch_shapes=[pltpu.VMEM((tm, D), BF)],
        compiler_params=_cp(1))(x, g_ca, wcq, g_cq, kn, vv, wco)


def _ca_bwd(x, dy, g_ca, wcq, g_cq, kn, vv, wco):
    T, D = x.shape
    M = kn.shape[1]
    tm = _tile(T, 512)
    n = T // tm

    def body(x_ref, dy_ref, gca_ref, wcq_ref, gcq_ref, kn_ref, v_ref, wco_ref,
             dx_ref, hb_ref, dq_sc, ob_sc, dkn_ref, dv_ref, dgcq_ref, dgca_ref):
        i = pl.program_id(0)
        first = i == 0
        xt = x_ref[...]
        dyt = dy_ref[...]
        dyb = dyt.astype(BF)
        hb, qr, qn, ps = _ca_tile_fwd(xt, gca_ref[...], wcq_ref[...], gcq_ref[...], kn_ref, v_ref)
        do = _nt(dyb, wco_ref[...])
        gcq_rows = None
        for h in range(CA_HEADS):
            hs = slice(h * CA_HD, (h + 1) * CA_HD)
            p = ps[h]
            pb = p.astype(BF)
            ob_sc[:, hs] = _nn(pb, v_ref[h]).astype(BF)
            doh = do[:, hs].astype(BF)
            dp = _nt(doh, v_ref[h])
            ds = (p * (dp - jnp.sum(dp * p, axis=1, keepdims=True))).astype(BF)
            dvh = _tn(pb, doh)
            dkh = _tn(ds, qn[h])

            @pl.when(first)
            def _():
                dv_ref[h] = dvh
                dkn_ref[h] = dkh

            @pl.when(jnp.logical_not(first))
            def _():
                dv_ref[h] += dvh
                dkn_ref[h] += dkh

            dqn = _nn(ds, kn_ref[h]) * 0.0625
            dqh, gr = _norm_bwd(dqn, qr[h], gcq_ref[...])
            gcq_rows = gr if gcq_rows is None else gcq_rows + gr
            dq_sc[:, hs] = dqh.astype(BF)
        _acc_rows(dgcq_ref, first, gcq_rows)
        hb_ref[...] = hb
        dh = _nt(dq_sc[...], wcq_ref[...])
        dx, gar = _norm_bwd(dh, xt, gca_ref[...])
        dx_ref[...] = dx + dyt
        _acc_rows(dgca_ref, first, gar)

    row = lambda i: (i, 0)
    fix = lambda i: (0, 0)
    fix3 = lambda i: (0, 0, 0)
    hd = (CA_HEADS, M, CA_HD)
    return pl.pallas_call(
        body, name="ca_bwd", grid=(n,),
        in_specs=[pl.BlockSpec((tm, D), row), pl.BlockSpec((tm, D), row), pl.BlockSpec((1, D), fix),
                  pl.BlockSpec((D, D), fix), pl.BlockSpec((1, CA_HD), fix), pl.BlockSpec(hd, fix3),
                  pl.BlockSpec(hd, fix3), pl.BlockSpec((D, D), fix)],
        out_specs=[pl.BlockSpec((tm, D), row), pl.BlockSpec((tm, D), row), pl.BlockSpec((tm, D), row),
                   pl.BlockSpec((tm, D), row), pl.BlockSpec(hd, fix3), pl.BlockSpec(hd, fix3),
                   pl.BlockSpec((1, CA_HD), fix), pl.BlockSpec((1, D), fix)],
        out_shape=[S((T, D), F32), S((T, D), BF), S((T, D), BF), S((T, D), BF), S(hd, F32), S(hd, F32),
                   S((1, CA_HD), F32), S((1, D), F32)],
        compiler_params=_cp(1))(x, dy, g_ca, wcq, g_cq, kn, vv, wco)


def _tn_dw(name, a, b):
    T, D = a.shape
    tk = _tile(T, 1024)
    half = D // 2
    return _tn_matmul(
        name, a, pl.BlockSpec((tk, half), lambda j, k: (k, j)), b, pl.BlockSpec((tk, D), lambda j, k: (k, 0)),
        S((D, D), BF), pl.BlockSpec((half, D), lambda j, k: (j, 0)), (2, T // tk), (half, D))


def _ca_kv_bwd(mem, g_mem, mn, kraw, dkn, dvv, wckv, g_ck):
    M, D = mem.shape

    def body(m_ref, g_ref, mn_ref, kr_ref, dkn_ref, dv_ref, w_ref, gk_ref, dw_ref, dgk_ref, dgm_ref):
        mn = mn_ref[...]
        dmn = jnp.zeros((M, D), F32)
        gk_rows = None
        for h in range(CA_HEADS):
            dkr, gr = _norm_bwd(dkn_ref[h], kr_ref[h], gk_ref[...])
            gk_rows = gr if gk_rows is None else gk_rows + gr
            dkb = dkr.astype(BF)
            dvb = dv_ref[h].astype(BF)
            dw_ref[h] = _tn(mn, dkb).astype(BF)
            dw_ref[CA_HEADS + h] = _tn(mn, dvb).astype(BF)
            dmn = dmn + _nt(dkb, w_ref[h]) + _nt(dvb, w_ref[CA_HEADS + h])
        dgk_ref[...] = jnp.sum(gk_rows, axis=0, keepdims=True)
        mf = m_ref[...]
        dgm_ref[...] = jnp.sum(dmn * (mf * _rstd(mf)), axis=0, keepdims=True)

    return pl.pallas_call(
        body, name="ca_kv_bwd",
        out_shape=[S((2 * CA_HEADS, D, CA_HD), BF), S((1, CA_HD), F32), S((1, D), F32)],
        compiler_params=pltpu.CompilerParams(vmem_limit_bytes=VMEM_LIMIT))(mem, g_mem, mn, kraw, dkn, dvv, wckv, g_ck)


def _after(g, token):
    return g if token is None else g + token[0:1, 0:1]


def _local_step(x, mem, target, small, weights, emit):
    T, D = x.shape
    p = small
    bf128 = jnp.pad(p["b_f"], ((0, 0), (0, LANES - FOX_HEADS)))
    b_st = p["b_s"].T

    wup1 = weights("ffn1_up", x)["wup1"]
    a1, h1 = _ffn_up("ffn1_up", x, p["g_ffn1"], wup1)
    wdn1 = weights("ffn1_dn", h1)["wdn1"]
    x1 = _ffn_down("ffn1_down", a1, wdn1, x)
    wm = weights("mix", x1)
    z, h2 = _mix_proj(x1, p["g_mix"], wm["wz"])
    qf, ka, va, yg = _mix_prep(z, bf128, p["g_q"], p["g_k"], p["g_sgu"], p["w_s"], b_st, p["g_gmlp_o"])
    attn, lse = _fox_fwd(qf, ka, va)
    x2 = _mix_out(attn, yg, p["g_fox_o"], wm["wout"], x1)
    wc = weights("ca", x2)
    mn, kraw, ckn, cvv = _ca_kv(mem, p["g_mem"], wc["wckv"], p["g_ck"])
    x3 = _ca_fwd(x2, p["g_ca"], wc["wcq"], p["g_cq"], ckn, cvv, wc["wco"])
    w2 = weights("ffn2", x3)
    a2, h4 = _ffn_up("ffn2_up", x3, p["g_ffn2"], w2["wup2"])
    dy4, dy4b, sq = _ffn_down_loss("ffn2_down", a2, w2["wdn2"], x3, target)

    gs = {}
    dgu2 = _ffn_bwd_act("ffn2_bwd_act", dy4b, h4, w2["wup2"], w2["wdn2"])
    tok = emit("ffn2", {"wup2": _ffn_dwup("ffn2", h4, dgu2), "wdn2": _ffn_dwdn("ffn2", a2, dy4b)})
    dx3, gs["g_ffn2"] = _ffn_dx("ffn2_dx", dgu2, w2["wup2"], x3, _after(p["g_ffn2"], tok), dy4)

    dx2, h3, dqc, oc, dckn, dcvv, gs["g_cq"], gs["g_ca"] = _ca_bwd(
        x2, dx3, p["g_ca"], wc["wcq"], p["g_cq"], ckn, cvv, wc["wco"])
    dwcq = _tn_dw("ca_dwcq", h3, dqc)
    dwco = _tn_dw("ca_dwco", oc, dx3)
    dwckv, gs["g_ck"], gs["g_mem"] = _ca_kv_bwd(mem, p["g_mem"], mn, kraw, dckn, dcvv, wc["wckv"], p["g_ck"])

    qb, dob, dyg, yy, gs["g_fox_o"] = _mix_out_bwd(dx2, attn, yg, p["g_fox_o"], wm["wout"], qf, lse)
    dwout = _tn_dw("mix_dwout", yy, dx2)
    dq, dk, dv = _fox_bwd(qb, ka, va, dob)
    dz, gs["g_q"], gs["g_k"], gs["g_sgu"], gs["g_gmlp_o"], gs["w_s"], dbst, dbf = _mix_prep_bwd(
        z, dq, dk, dv, dyg, bf128, p["g_q"], p["g_k"], p["g_sgu"], p["w_s"], b_st, p["g_gmlp_o"])
    gs["b_s"] = dbst.T
    gs["b_f"] = dbf[:, :FOX_HEADS]
    tok_ws = emit("w_s", {"w_s": gs["w_s"]})
    tk = _tile(T, 1024)
    zb = ZW // 3
    dwz = _tn_matmul(
        "mix_dwz", dz, pl.BlockSpec((tk, zb), lambda j, k: (k, j)), h2, pl.BlockSpec((tk, D), lambda j, k: (k, 0)),
        S((ZW, D), F32), pl.BlockSpec((zb, D), lambda j, k: (j, 0)), (3, T // tk), (zb, D))
    tok = emit("mid", {"wcq": dwcq, "wco": dwco, "wckv": dwckv, "wout": dwout, "wz": dwz})
    dx1, dx1b, gs["g_mix"] = _mix_proj_bwd(dz, wm["wz"], x1, _after(_after(p["g_mix"], tok), tok_ws), dx2)

    dgu1 = _ffn_bwd_act("ffn1_bwd_act", dx1b, h1, wup1, wdn1)
    tok = emit("ffn1_dn", {"wdn1": _ffn_dwdn("ffn1", a1, dx1b)})
    tok = emit("ffn1_up", {"wup1": _ffn_dwup("ffn1", h1, dgu1, after=tok)})
    dx0, gs["g_ffn1"] = _ffn_dx("ffn1_dx", dgu1, wup1, x, _after(p["g_ffn1"], tok), dx1)
    return sq, dx0, gs


MESH = pl.DeviceIdType.MESH
HBM_SPEC = pl.BlockSpec(memory_space=pltpu.HBM)
N_PEER = N_DEV - 1


def _place():
    return lax.axis_index("x"), lax.axis_index("y"), lax.axis_index("c")


def _slot(px, py, pc):
    return 4 * px + 2 * py + pc


SEM_SPEC = pl.BlockSpec(memory_space=pltpu.SEMAPHORE)
ANY_SPEC = pl.BlockSpec(memory_space=pl.ANY)
DATAFLOW = pltpu.SideEffectType.DATAFLOW_SIDE_EFFECTING


def _hbm(a):
    return pltpu.with_memory_space_constraint(a, pltpu.HBM)


def _peer(x, y, c, r):
    return (1 - x if r & 4 else x, 1 - y if r & 2 else y, 1 - c if r & 1 else c)


def _place_own(srcs, whole):
    my = _slot(*_place())
    lands = []
    for s in srcs:
        blk = s[None] if whole else lax.dynamic_slice_in_dim(s, my, 1, 0)
        shape = (N_DEV,) + s.shape if whole else s.shape
        lands.append(lax.dynamic_update_slice_in_dim(lax.empty(shape, s.dtype), blk, my, 0))
    return lands


ALL_PEERS = tuple(range(1, N_DEV))
NEAR_PEERS = (1, 2, 4, 6)
SAME_CORE = (2, 4, 6)


def _copy_start(name, srcs, lands, whole, peers=None):
    n = len(srcs)
    peers = peers or [ALL_PEERS] * n

    def body(*refs):
        src, land = refs[:n], refs[n:2 * n]
        send, recv = refs[2 * n:3 * n], refs[3 * n:4 * n]
        token = refs[6 * n]
        x, y, c = _place()
        my = _slot(x, y, c)
        for a in range(n):
            for r in peers[a]:
                p = _peer(x, y, c, r)
                pltpu.make_async_remote_copy(
                    src_ref=src[a] if whole else src[a].at[_slot(*p)], dst_ref=land[a].at[my],
                    send_sem=send[a].at[r - 1], recv_sem=recv[a].at[r - 1], device_id=p, device_id_type=MESH).start()
        token[...] = jnp.zeros_like(token)

    out = pl.pallas_call(
        body, name=name,
        out_shape=([pltpu.SemaphoreType.DMA((N_PEER,))] * (2 * n)
                   + [pltpu.HBM(s.shape, s.dtype) for s in srcs] + [pltpu.HBM(s.shape, s.dtype) for s in lands]
                   + [S((8, LANES), F32)]),
        in_specs=[HBM_SPEC] * (2 * n),
        out_specs=[SEM_SPEC] * (2 * n) + [HBM_SPEC] * (2 * n) + [pl.BlockSpec(memory_space=pltpu.VMEM)],
        input_output_aliases={i: 2 * n + i for i in range(2 * n)},
        compiler_params=pltpu.CompilerParams(has_side_effects=DATAFLOW),
    )(*[_hbm(s) for s in srcs], *[_hbm(s) for s in lands])
    return out[:n], out[n:2 * n], out[2 * n:3 * n], out[3 * n:4 * n], out[4 * n]


def _copy_wait(name, srcs, lands, send, recv, after, whole, peers=None):
    n = len(srcs)
    peers = peers or [ALL_PEERS] * n

    def body(*refs):
        src, land = refs[:n], refs[n:2 * n]
        snd, rcv = refs[2 * n:3 * n], refs[3 * n:4 * n]
        x, y, c = _place()
        for a in range(n):
            for r in peers[a]:
                p = _peer(x, y, c, r)
                ps = _slot(*p)
                cp = pltpu.make_async_remote_copy(
                    src_ref=src[a] if whole else src[a].at[ps], dst_ref=land[a].at[ps],
                    send_sem=snd[a].at[r - 1], recv_sem=rcv[a].at[r - 1], device_id=p, device_id_type=MESH)
                cp.wait_send()
                cp.wait_recv()

    out = pl.pallas_call(
        body, name=name,
        out_shape=[pltpu.HBM(s.shape, s.dtype) for s in srcs] + [pltpu.HBM(s.shape, s.dtype) for s in lands],
        in_specs=[HBM_SPEC] * (2 * n) + [SEM_SPEC] * (2 * n) + [ANY_SPEC],
        out_specs=[HBM_SPEC] * (2 * n),
        input_output_aliases={i: i for i in range(2 * n)},
        compiler_params=pltpu.CompilerParams(has_side_effects=DATAFLOW),
    )(*srcs, *lands, *send, *recv, after)
    return out[n:]


def _forward_start(name, lands):
    n = len(lands)

    def body(*refs):
        land = refs[:n]
        send, recv = refs[n:2 * n], refs[2 * n:3 * n]
        token = refs[4 * n]
        x, y, c = _place()
        for a in range(n):
            for r in SAME_CORE:
                blk = land[a].at[_slot(*_peer(x, y, c, r))]
                pltpu.make_async_remote_copy(
                    src_ref=blk, dst_ref=blk, send_sem=send[a].at[r - 1], recv_sem=recv[a].at[r - 1],
                    device_id=(x, y, 1 - c), device_id_type=MESH).start()
        token[...] = jnp.zeros_like(token)

    out = pl.pallas_call(
        body, name=name,
        out_shape=([pltpu.SemaphoreType.DMA((N_PEER,))] * (2 * n) + [pltpu.HBM(s.shape, s.dtype) for s in lands]
                   + [S((8, LANES), F32)]),
        in_specs=[HBM_SPEC] * n,
        out_specs=[SEM_SPEC] * (2 * n) + [HBM_SPEC] * n + [pl.BlockSpec(memory_space=pltpu.VMEM)],
        input_output_aliases={i: 2 * n + i for i in range(n)},
        compiler_params=pltpu.CompilerParams(has_side_effects=DATAFLOW),
    )(*[_hbm(s) for s in lands])
    return out[:n], out[n:2 * n], out[2 * n:3 * n], out[3 * n]


def _forward_wait(name, lands, send, recv, after):
    n = len(lands)

    def body(*refs):
        land = refs[:n]
        snd, rcv = refs[n:2 * n], refs[2 * n:3 * n]
        x, y, c = _place()
        for a in range(n):
            for r in SAME_CORE:
                cp = pltpu.make_async_remote_copy(
                    src_ref=land[a].at[_slot(*_peer(x, y, c, r))], dst_ref=land[a].at[_slot(*_peer(x, y, c, r | 1))],
                    send_sem=snd[a].at[r - 1], recv_sem=rcv[a].at[r - 1], device_id=(x, y, 1 - c),
                    device_id_type=MESH)
                cp.wait_send()
                cp.wait_recv()

    return pl.pallas_call(
        body, name=name,
        out_shape=[pltpu.HBM(s.shape, s.dtype) for s in lands],
        in_specs=[HBM_SPEC] * n + [SEM_SPEC] * (2 * n) + [ANY_SPEC],
        out_specs=[HBM_SPEC] * n,
        input_output_aliases={i: i for i in range(n)},
        compiler_params=pltpu.CompilerParams(has_side_effects=DATAFLOW),
    )(*lands, *send, *recv, after)


def _adamw(w, g, m, v):
    m2 = ADAM_B1 * m + (1.0 - ADAM_B1) * g
    v2 = ADAM_B2 * v + (1.0 - ADAM_B2) * (g * g)
    m_hat = m2 / (1.0 - ADAM_B1 ** ADAM_STEP)
    v_hat = v2 / (1.0 - ADAM_B2 ** ADAM_STEP)
    delta = -ADAM_LR * (m_hat / (jnp.sqrt(v_hat) + ADAM_EPS) + ADAM_WD * w)
    return delta, m2, v2


def _adamw_big(name, slots, w, m, v):
    R, C = w.shape
    tr = next((t for t in (256, 352) if R % t == 0), R)

    def body(s_ref, w_ref, m_ref, v_ref, g_ref, d_ref, m2_ref, v2_ref):
        g = s_ref[0].astype(F32)
        for k in range(1, N_DEV):
            g = g + s_ref[k].astype(F32)
        d, m2, v2 = _adamw(w_ref[...], g, m_ref[...], v_ref[...])
        g_ref[...] = g
        d_ref[...] = d
        m2_ref[...] = m2
        v2_ref[...] = v2

    row = pl.BlockSpec((tr, C), lambda i: (i, 0))
    return pl.pallas_call(
        body, name=name, grid=(R // tr,),
        in_specs=[pl.BlockSpec((N_DEV, tr, C), lambda i: (0, i, 0)), row, row, row],
        out_specs=[row] * 4, out_shape=[S((R, C), F32)] * 4,
        compiler_params=_cp(1))(slots, w, m, v)


TINY_ROWS = (("b_s", 8), ("g_ffn1", 8), ("g_mix", 8), ("g_ca", 8), ("g_mem", 8), ("g_ffn2", 8), ("g_sgu", 4),
             ("g_fox_o", 4), ("g_gmlp_o", 4), ("g_cq", 2), ("g_ck", 2), ("g_q", 1), ("g_k", 1), ("b_f", 1),
             ("loss", 1))
TINY_P = 72


def _tiny_pieces(width):
    return [(j, slice(j * LANES, min((j + 1) * LANES, width))) for j in range(-(-width // LANES))]


def _pack_tiny(grads, sq):
    names = [n for n, _ in TINY_ROWS if n != "loss"]

    def body(*refs):
        ins, sq_ref, o_ref = refs[:len(names)], refs[len(names)], refs[len(names) + 1]
        o_ref[...] = jnp.zeros_like(o_ref)
        at = 0
        for ref, (name, r) in zip(ins, TINY_ROWS):
            if name == "b_s":
                o_ref[at:at + r, :] = ref[...]
            else:
                for j, cols in _tiny_pieces(ref.shape[1]):
                    o_ref[at + j:at + j + 1, 0:cols.stop - cols.start] = ref[:, cols]
            at += r
        o_ref[at:at + 1, :] = sq_ref[0:1, :]

    return pl.pallas_call(body, name="tiny_pack", out_shape=S((TINY_P, LANES), F32))(
        *[grads[n] for n in names], sq)


def _adamw_tiny(slots, w, m, v):
    names = [n for n, _ in TINY_ROWS if n != "loss"]
    k = len(names)

    def body(s_ref, *refs):
        ins, outs, loss_ref = refs[:3 * k], refs[3 * k:7 * k], refs[7 * k]
        g_all = s_ref[0]
        for d in range(1, N_DEV):
            g_all = g_all + s_ref[d]
        at = 0
        for i, (name, r) in enumerate(TINY_ROWS[:k]):
            w_ref, m_ref, v_ref = ins[i], ins[k + i], ins[2 * k + i]
            o = outs[4 * i:4 * i + 4]
            if name == "b_s":
                pieces = [(slice(at, at + r), slice(0, LANES), (slice(None), slice(None)))]
            else:
                pieces = [(slice(at + j, at + j + 1), slice(0, c.stop - c.start), (slice(None), c))
                          for j, c in _tiny_pieces(w_ref.shape[1])]
            for rows, lanes, dst in pieces:
                g = g_all[rows, lanes]
                res = (g,) + _adamw(w_ref[dst], g, m_ref[dst], v_ref[dst])
                for ref, val in zip(o, res):
                    ref[dst] = val
            at += r
        loss_ref[...] = g_all[at:at + 1, :]

    shapes = [S(w[n].shape, F32) for n in names]
    out = pl.pallas_call(
        body, name="adamw_tiny", out_shape=[s for s in shapes for _ in range(4)] + [S((1, LANES), F32)],
    )(slots, *[w[n] for n in names], *[m[n] for n in names], *[v[n] for n in names])
    stores = ({}, {}, {}, {})
    for i, n in enumerate(names):
        for store, t in zip(stores, out[4 * i:4 * i + 4]):
            store[n] = t
    return stores, out[4 * k]


WEIGHTS =('g_ffn1', 'w_ffn1_in', 'w_ffn1_out', 'g_mix', 'w_in', 'b_f', 'g_q', 'g_k', 'g_sgu', 'w_s', 'b_s',
           'g_fox_o', 'g_gmlp_o', 'w_out', 'g_ca', 'g_mem', 'w_cq', 'w_ckv', 'g_cq', 'g_ck', 'w_co', 'g_ffn2',
           'w_ffn2_in', 'w_ffn2_out')
BIG = ('w_ffn1_in', 'w_ffn1_out', 'w_in', 'w_out', 'w_cq', 'w_ckv', 'w_co', 'w_ffn2_in', 'w_ffn2_out')
TRANSPOSED = ('w_ffn1_in', 'w_in', 'w_ffn2_in')
TWO_LEVEL = ('w_ffn1_in', 'w_in')
GATHER_GROUPS = {"ffn1_up": ("w_ffn1_in",), "ffn1_dn": ("w_ffn1_out",), "mix": ("w_in", "w_out"),
                 "ca": ("w_cq", "w_ckv", "w_co"), "ffn2": ("w_ffn2_in", "w_ffn2_out")}
QKV_W = 3 * FOX_W
UV_OFF = QKV_W + FOX_HEADS


def kernel(x, mem, g_ffn1, w_ffn1_in, w_ffn1_out, g_mix, w_in, b_f, g_q, g_k, g_sgu, w_s, b_s, g_fox_o, g_gmlp_o, w_out, g_ca, g_mem, w_cq, w_ckv, g_cq, g_ck, w_co, g_ffn2, w_ffn2_in, w_ffn2_out, loss_target, m_g_ffn1, m_w_ffn1_in, m_w_ffn1_out, m_g_mix, m_w_in, m_b_f, m_g_q, m_g_k, m_g_sgu, m_w_s, m_b_s, m_g_fox_o, m_g_gmlp_o, m_w_out, m_g_ca, m_g_mem, m_w_cq, m_w_ckv, m_g_cq, m_g_ck, m_w_co, m_g_ffn2, m_w_ffn2_in, m_w_ffn2_out, v_g_ffn1, v_w_ffn1_in, v_w_ffn1_out, v_g_mix, v_w_in, v_b_f, v_g_q, v_g_k, v_g_sgu, v_w_s, v_b_s, v_g_fox_o, v_g_gmlp_o, v_w_out, v_g_ca, v_g_mem, v_w_cq, v_w_ckv, v_g_cq, v_g_ck, v_w_co, v_g_ffn2, v_w_ffn2_in, v_w_ffn2_out):
    args = dict(locals())
    w = {n: args[n] for n in WEIGHTS}
    mo = {n: args["m_" + n] for n in WEIGHTS}
    vo = {n: args["v_" + n] for n in WEIGHTS}
    D = D_MODEL

    def local(n, a):
        return a[0].T if n in TRANSPOSED else a[0]

    shards = [local(n, w[n]).astype(BF) for n in BIG]
    fb = shards[0].shape[0]
    g_peers = [NEAR_PEERS if n in TWO_LEVEL else ALL_PEERS for n in BIG]
    g_snd, g_rcv, g_src, g_land, g_token = _copy_start("gather_start", shards, _place_own(shards, True), True,
                                                       peers=g_peers)
    handles = {n: (g_src[i], g_land[i], g_snd[i], g_rcv[i]) for i, n in enumerate(BIG)}

    tiny_names = [n for n, _ in TINY_ROWS if n != "loss"]

    def weights(group, after):
        names = GATHER_GROUPS[group]
        hs = [handles[n] for n in names]
        got = list(_copy_wait("gather_wait_" + group, [h[0] for h in hs], [h[1] for h in hs], [h[2] for h in hs],
                              [h[3] for h in hs], after, True, peers=[g_peers[BIG.index(n)] for n in names]))
        passed = [i for i, n in enumerate(names) if n in TWO_LEVEL]
        if passed:
            f_snd, f_rcv, f_land, f_token = _forward_start("gather_pass_start_" + group, [got[i] for i in passed])
            for i, t in zip(passed, _forward_wait("gather_pass_wait_" + group, f_land, f_snd, f_rcv, f_token)):
                got[i] = t
        got = dict(zip(names, got))
        if group == "ffn1_up":
            return {"wup1": got["w_ffn1_in"].reshape(2, N_FFN_BLK, fb, D)}
        if group == "ffn1_dn":
            return {"wdn1": got["w_ffn1_out"].reshape(N_FFN_BLK, fb, D)}
        if group == "mix":
            full = got["w_in"].reshape(-1, D)
            wz = jnp.concatenate([full[:QKV_W], full[UV_OFF:], full[QKV_W:UV_OFF],
                                  jnp.zeros((LANES - FOX_HEADS, D), BF)], axis=0)
            return {"wz": wz, "wout": got["w_out"].reshape(D, D)}
        if group == "ca":
            return {"wcq": got["w_cq"].reshape(D, D), "wco": got["w_co"].reshape(D, D), "wckv": got["w_ckv"]}
        return {"wup2": got["w_ffn2_in"].reshape(2, N_FFN_BLK, fb, D),
                "wdn2": got["w_ffn2_out"].reshape(N_FFN_BLK, fb, D)}

    flying = {}

    def emit(group, g):
        if group == "w_s":
            part = [g["w_s"].reshape(-1, LANES)]
            *copies, token = _copy_start("w_s_start", part, _place_own(part, True), True)
            flying[group] = copies
            return token
        if group == "ffn2":
            parts = {"w_ffn2_in": g["wup2"], "w_ffn2_out": g["wdn2"].reshape(N_DEV, -1, D)}
        elif group == "ffn1_dn":
            parts = {"w_ffn1_out": g["wdn1"].reshape(N_DEV, -1, D)}
        elif group == "ffn1_up":
            parts = {"w_ffn1_in": g["wup1"]}
        else:
            gz = g["wz"]
            g_in = jnp.concatenate([gz[:QKV_W], gz[Z_F:Z_F + FOX_HEADS], gz[QKV_W:Z_F]], axis=0)
            parts = {"w_in": g_in.reshape(N_DEV, -1, D).astype(BF),
                     "w_out": g["wout"].reshape(N_DEV, -1, D), "w_cq": g["wcq"].reshape(N_DEV, -1, D),
                     "w_co": g["wco"].reshape(N_DEV, -1, D), "w_ckv": g["wckv"]}
        names = list(parts)
        srcs = [parts[n] for n in names]
        *copies, token = _copy_start("exchange_start_" + group, srcs, _place_own(srcs, False), False)
        flying[group] = (names, copies)
        return token

    small = {n: (w[n][0] if n == "b_s" else w[n]) for n in tiny_names}
    small["w_s"] = w["w_s"][0]

    sq, dx0, gs = _local_step(x[0], mem[0], loss_target[0], small, weights, emit)

    sm_parts = [_pack_tiny(gs, sq)]
    sm_snd, sm_rcv, sm_src, sm_land, sm_token = _copy_start("tiny_start", sm_parts, _place_own(sm_parts, True), True)

    grad, delta, new_m, new_v = {}, {}, {}, {}

    def update(group, after):
        names, (snd, rcv, srcs, lands) = flying[group]
        slots = _copy_wait("exchange_wait_" + group, srcs, lands, snd, rcv, after, False)
        for n, sl in zip(names, slots):
            g, d, m2, v2 = _adamw_big("adamw_" + n, sl, local(n, w[n]), local(n, mo[n]), local(n, vo[n]))
            grad[n], delta[n], new_m[n], new_v[n] = (
                (t.T if n in TRANSPOSED else t).reshape(w[n].shape) for t in (g, d, m2, v2))
        return d

    last = update("ffn2", sm_token)
    last = update("mid", last)
    last = update("ffn1_dn", last)
    last = update("ffn1_up", last)
    ws_snd, ws_rcv, ws_src, ws_land = flying["w_s"]
    ws_all, = _copy_wait("w_s_wait", ws_src, ws_land, ws_snd, ws_rcv, last, True)
    tiny_all, = _copy_wait("tiny_wait", sm_src, sm_land, sm_snd, sm_rcv, ws_all, True)
    ws_shape = w["w_s"].shape
    for store, t in zip((grad, delta, new_m, new_v), _adamw_big(
            "adamw_w_s", ws_all, *[a["w_s"].reshape(-1, LANES) for a in (w, mo, vo)])):
        store["w_s"] = t.reshape(ws_shape)
    stores, loss_row = _adamw_tiny(tiny_all, *[{n: (a[n][0] if n == "b_s" else a[n]) for n in tiny_names}
                                               for a in (w, mo, vo)])
    for store, t in zip((grad, delta, new_m, new_v), stores):
        store.update({n: v.reshape(w[n].shape) for n, v in t.items()})
    loss = loss_row[0, 0] * (0.5 / D)

    return (loss, dx0[None], *[grad[n] for n in WEIGHTS], *[delta[n] for n in WEIGHTS],
            *[new_m[n] for n in WEIGHTS], *[new_v[n] for n in WEIGHTS])
```

```python
import functools

import jax
import jax.numpy as jnp
from jax import lax
from jax.experimental import pallas as pl
from jax.experimental.pallas import tpu as pltpu

F32 = jnp.float32
BF = jnp.bfloat16
S = jax.ShapeDtypeStruct

N_DEV = 8
D_MODEL = 1024
FOX_HEADS, FOX_HD = 8, 64
FOX_W = 512
GMLP_G, GMLP_GD = 8, 64
GMLP_W = 512
CHUNK = 128
CA_HEADS, CA_HD = 4, 256
N_FFN_BLK = 4
ZW = 2688
Z_Q, Z_K, Z_V, Z_U, Z_G, Z_F = 0, 512, 1024, 1536, 2048, 2560
EPS = 1e-6
NEG = -1e30
LANES = 128

ADAM_LR, ADAM_B1, ADAM_B2, ADAM_EPS, ADAM_WD, ADAM_STEP = 0.001, 0.9, 0.999, 1e-08, 0.01, 10

VMEM_LIMIT = 52 * 2 ** 20


def _cp(n_axes):
    return pltpu.CompilerParams(dimension_semantics=("arbitrary",) * n_axes, vmem_limit_bytes=VMEM_LIMIT)


def _nn(a, b):
    return jnp.dot(a, b, preferred_element_type=F32)


def _nt(a, b):
    return lax.dot_general(a, b, (((1,), (1,)), ((), ())), preferred_element_type=F32)


def _tn(a, b):
    return lax.dot_general(a, b, (((0,), (0,)), ((), ())), preferred_element_type=F32)


def _hi(a, b):
    return jnp.dot(a, b, precision=lax.Precision.HIGHEST, preferred_element_type=F32)


def _rstd(x):
    return lax.rsqrt(jnp.mean(x * x, axis=-1, keepdims=True) + EPS)


def _norm_bwd(dy, x, g):
    r = _rstd(x)
    xh = x * r
    dxh = dy * g
    dx = r * (dxh - xh * jnp.mean(dxh * xh, axis=-1, keepdims=True))
    return dx, dy * xh


def _acc_rows(ref, first, val):
    srow = jnp.sum(val, axis=0, keepdims=True)

    @pl.when(first)
    def _():
        ref[...] = srow

    @pl.when(jnp.logical_not(first))
    def _():
        ref[...] += srow


def _gelu(x):
    c = 0.7978845608028654
    return 0.5 * x * (1.0 + jnp.tanh(c * (x + 0.044715 * x * x * x)))


def _gelu_grad(x):
    c = 0.7978845608028654
    t = jnp.tanh(c * (x + 0.044715 * x * x * x))
    return 0.5 * (1.0 + t) + 0.5 * x * (1.0 - t * t) * c * (1.0 + 3 * 0.044715 * x * x)


def _tile(n, pref):
    return pref if n % pref == 0 else n


def _ffn_up(name, x, g, wup):
    T, D = x.shape
    FB = wup.shape[-2]
    tm = _tile(T, 1024)

    def body(x_ref, g_ref, w_ref, a_ref, h_ref):
        @pl.when(pl.program_id(1) == 0)
        def _():
            xf = x_ref[...]
            h_ref[...] = (xf * _rstd(xf) * g_ref[...]).astype(BF)

        hb = h_ref[...]
        gg = _nt(hb, w_ref[0])
        uu = _nt(hb, w_ref[1])
        a_ref[...] = (gg * jax.nn.sigmoid(gg) * uu).astype(BF)

    return pl.pallas_call(
        body, name=name, grid=(T // tm, N_FFN_BLK),
        in_specs=[pl.BlockSpec((tm, D), lambda i, j: (i, 0)),
                  pl.BlockSpec((1, D), lambda i, j: (0, 0)),
                  pl.BlockSpec((2, None, FB, D), lambda i, j: (0, j, 0, 0))],
        out_specs=[pl.BlockSpec((None, tm, FB), lambda i, j: (j, i, 0)),
                   pl.BlockSpec((tm, D), lambda i, j: (i, 0))],
        out_shape=[S((N_FFN_BLK, T, FB), BF), S((T, D), BF)],
        compiler_params=_cp(2))(x, g, wup)


def _ffn_down(name, a, wdn, x):
    _, T, FB = a.shape
    D = x.shape[1]
    tm = _tile(T, 512)

    def body(a_ref, w_ref, x_ref, o_ref):
        p = _nn(a_ref[0], w_ref[0])
        for j in range(1, N_FFN_BLK):
            p = p + _nn(a_ref[j], w_ref[j])
        o_ref[...] = x_ref[...] + 0.5 * p

    return pl.pallas_call(
        body, name=name, grid=(T // tm,),
        in_specs=[pl.BlockSpec((N_FFN_BLK, tm, FB), lambda i: (0, i, 0)),
                  pl.BlockSpec((N_FFN_BLK, FB, D), lambda i: (0, 0, 0)),
                  pl.BlockSpec((tm, D), lambda i: (i, 0))],
        out_specs=pl.BlockSpec((tm, D), lambda i: (i, 0)),
        out_shape=S((T, D), F32),
        compiler_params=_cp(1))(a, wdn, x)


def _ffn_down_loss(name, a, wdn, x, target):
    _, T, FB = a.shape
    D = x.shape[1]
    tm = _tile(T, 512)

    def body(a_ref, w_ref, x_ref, t_ref, d_ref, db_ref, loss_ref):
        i = pl.program_id(0)
        p = _nn(a_ref[0], w_ref[0])
        for j in range(1, N_FFN_BLK):
            p = p + _nn(a_ref[j], w_ref[j])
        diff = (x_ref[...] + 0.5 * p) - t_ref[...]
        dy = diff * (1.0 / D)
        d_ref[...] = dy
        db_ref[...] = dy.astype(BF)
        sq = jnp.zeros((8, LANES), F32) + jnp.sum(diff * diff)

        @pl.when(i == 0)
        def _():
            loss_ref[...] = sq

        @pl.when(i > 0)
        def _():
            loss_ref[...] += sq

    row = pl.BlockSpec((tm, D), lambda i: (i, 0))
    return pl.pallas_call(
        body, name=name, grid=(T // tm,),
        in_specs=[pl.BlockSpec((N_FFN_BLK, tm, FB), lambda i: (0, i, 0)),
                  pl.BlockSpec((N_FFN_BLK, FB, D), lambda i: (0, 0, 0)), row, row],
        out_specs=[row, row, pl.BlockSpec((8, LANES), lambda i: (0, 0))],
        out_shape=[S((T, D), F32), S((T, D), BF), S((8, LANES), F32)],
        compiler_params=_cp(1))(a, wdn, x, target)


def _ffn_bwd_act(name, dyb, h, wup, wdn):
    T, D = h.shape
    FB = wup.shape[-2]
    tm = _tile(T, 1024)

    def body(d_ref, h_ref, wu_ref, wd_ref, o_ref):
        da = 0.5 * _nt(d_ref[...], wd_ref[...])
        hb = h_ref[...]
        gg = _nt(hb, wu_ref[0])
        uu = _nt(hb, wu_ref[1])
        sg = jax.nn.sigmoid(gg)
        o_ref[0] = (da * uu * (sg * (1.0 + gg * (1.0 - sg)))).astype(BF)
        o_ref[1] = (da * (gg * sg)).astype(BF)

    return pl.pallas_call(
        body, name=name, grid=(T // tm, N_FFN_BLK),
        in_specs=[pl.BlockSpec((tm, D), lambda i, j: (i, 0)),
                  pl.BlockSpec((tm, D), lambda i, j: (i, 0)),
                  pl.BlockSpec((2, None, FB, D), lambda i, j: (0, j, 0, 0)),
                  pl.BlockSpec((None, FB, D), lambda i, j: (j, 0, 0))],
        out_specs=pl.BlockSpec((2, None, tm, FB), lambda i, j: (0, j, i, 0)),
        out_shape=S((2, N_FFN_BLK, T, FB), BF),
        compiler_params=_cp(2))(dyb, h, wup, wdn)


def _ffn_dx(name, dgu, wup, x, g, dy):
    T, D = x.shape
    FB = wup.shape[-2]
    tm = _tile(T, 1024)

    def body(d_ref, w_ref, x_ref, g_ref, dy_ref, dx_ref, dg_ref, acc_ref):
        i, j = pl.program_id(0), pl.program_id(1)
        p = _nn(d_ref[0], w_ref[0]) + _nn(d_ref[1], w_ref[1])

        @pl.when(j == 0)
        def _():
            acc_ref[...] = p

        @pl.when(j > 0)
        def _():
            acc_ref[...] += p

        @pl.when(j == N_FFN_BLK - 1)
        def _():
            dx, dgr = _norm_bwd(acc_ref[...], x_ref[...], g_ref[...])
            dx_ref[...] = dx + dy_ref[...]
            _acc_rows(dg_ref, i == 0, dgr)

    return pl.pallas_call(
        body, name=name, grid=(T // tm, N_FFN_BLK),
        in_specs=[pl.BlockSpec((2, None, tm, FB), lambda i, j: (0, j, i, 0)),
                  pl.BlockSpec((2, None, FB, D), lambda i, j: (0, j, 0, 0)),
                  pl.BlockSpec((tm, D), lambda i, j: (i, 0)),
                  pl.BlockSpec((1, D), lambda i, j: (0, 0)),
                  pl.BlockSpec((tm, D), lambda i, j: (i, 0))],
        out_specs=[pl.BlockSpec((tm, D), lambda i, j: (i, 0)),
                   pl.BlockSpec((1, D), lambda i, j: (0, 0))],
        out_shape=[S((T, D), F32), S((1, D), F32)],
        scratch_shapes=[pltpu.VMEM((tm, D), F32)],
        compiler_params=_cp(2))(dgu, wup, x, g, dy)


def _tn_matmul(name, a, a_spec, b, b_spec, out_shape, out_spec, grid, acc_shape, scale=1.0, after=None):
    nk = grid[1]
    extra = [] if after is None else [after]

    def body(a_ref, b_ref, *rest):
        o_ref, acc_ref = rest[-2:]
        k = pl.program_id(1)
        p = _tn(a_ref[...], b_ref[...])

        @pl.when(k == 0)
        def _():
            acc_ref[...] = p

        @pl.when(k > 0)
        def _():
            acc_ref[...] += p

        @pl.when(k == nk - 1)
        def _():
            o_ref[...] = (acc_ref[...] * scale).astype(o_ref.dtype)

    return pl.pallas_call(
        body, name=name, grid=grid,
        in_specs=[a_spec, b_spec] + [pl.BlockSpec((8, LANES), lambda j, k: (0, 0)) for _ in extra],
        out_specs=out_spec, out_shape=out_shape,
        scratch_shapes=[pltpu.VMEM(acc_shape, F32)], compiler_params=_cp(2))(a, b, *extra)


def _ffn_dwup(name, h, dgu, after=None):
    T, D = h.shape
    FB = dgu.shape[-1]
    tk = _tile(T, 1024)
    return _tn_matmul(
        name + "_dwup", dgu.reshape(2 * N_FFN_BLK, T, FB), pl.BlockSpec((None, tk, FB), lambda j, k: (j, k, 0)),
        h, pl.BlockSpec((tk, D), lambda j, k: (k, 0)),
        S((2 * N_FFN_BLK, FB, D), BF), pl.BlockSpec((None, FB, D), lambda j, k: (j, 0, 0)),
        (2 * N_FFN_BLK, T // tk), (FB, D), after=after)


def _ffn_dwdn(name, a, dyb):
    _, T, FB = a.shape
    D = dyb.shape[1]
    tk = _tile(T, 1024)
    return _tn_matmul(
        name + "_dwdn", a, pl.BlockSpec((None, tk, FB), lambda j, k: (j, k, 0)),
        dyb, pl.BlockSpec((tk, D), lambda j, k: (k, 0)),
        S((N_FFN_BLK, FB, D), BF), pl.BlockSpec((None, FB, D), lambda j, k: (j, 0, 0)),
        (N_FFN_BLK, T // tk), (FB, D), scale=0.5)


def _mix_proj(x, g, wz):
    T, D = x.shape
    tm = _tile(T, 512)

    def body(x_ref, g_ref, w_ref, z_ref, h_ref):
        xf = x_ref[...]
        hb = (xf * _rstd(xf) * g_ref[...]).astype(BF)
        h_ref[...] = hb
        z_ref[...] = _nt(hb, w_ref[...])

    return pl.pallas_call(
        body, name="mix_proj", grid=(T // tm,),
        in_specs=[pl.BlockSpec((tm, D), lambda i: (i, 0)),
                  pl.BlockSpec((1, D), lambda i: (0, 0)),
                  pl.BlockSpec((ZW, D), lambda i: (0, 0))],
        out_specs=[pl.BlockSpec((tm, ZW), lambda i: (i, 0)),
                   pl.BlockSpec((tm, D), lambda i: (i, 0))],
        out_shape=[S((T, ZW), F32), S((T, D), BF)],
        compiler_params=_cp(1))(x, g, wz)


def _tri(n, lower):
    r = lax.broadcasted_iota(jnp.int32, (n, n), 0)
    c = lax.broadcasted_iota(jnp.int32, (n, n), 1)
    return (r >= c) if lower else (r <= c)


def _spatial_mix(vgn_b, ws_ref, bst, tm):
    tril = _tri(CHUNK, True)
    wms = [jnp.where(tril, ws_ref[g], 0.0).astype(BF) for g in range(GMLP_G)]
    rows = []
    for c in range(tm // CHUNK):
        cols = []
        for g in range(GMLP_G):
            vs = vgn_b[c * CHUNK:(c + 1) * CHUNK, g * GMLP_GD:(g + 1) * GMLP_GD]
            cols.append(_nn(wms[g], vs) + bst[:, g:g + 1])
        rows.append(jnp.concatenate(cols, axis=1))
    return jnp.concatenate(rows, axis=0), wms


HB = 128
AUG_W = FOX_HEADS * HB
COL_A, COL_B, COL_C = 64, 67, 70


def _spread_matrix():
    r = jnp.arange(FOX_W)
    return (jnp.arange(AUG_W)[None, :] == ((r // FOX_HD) * HB + r % FOX_HD)[:, None]).astype(BF)


def _piece_matrix(col):
    r = jnp.arange(LANES)
    dst = jnp.where(r < 3 * FOX_HEADS, (r % FOX_HEADS) * HB + col + r // FOX_HEADS, -1)
    return (jnp.arange(AUG_W)[None, :] == dst[:, None]).astype(BF)


def _ones_row(cols):
    c = jnp.arange(AUG_W) % HB
    hit = functools.reduce(jnp.logical_or, [(c >= a) & (c < a + 3) for a in cols])
    return hit.astype(F32)[None, :]


def _pieces(x):
    lane = lax.broadcasted_iota(jnp.int32, x.shape, 1)
    x = jnp.where(lane < FOX_HEADS, x, 0.0)
    hi = x.astype(BF).astype(F32)
    r1 = x - hi
    mid = r1.astype(BF).astype(F32)
    lo = (r1 - mid).astype(BF).astype(F32)
    return (hi + pltpu.roll(mid, FOX_HEADS, 1) + pltpu.roll(lo, 2 * FOX_HEADS, 1)).astype(BF)


def _mix_prep(z, bf128, g_q, g_k, g_sgu, w_s, b_st, g_go):
    T = z.shape[0]
    tm = _tile(T, 512)
    spread, pc_q, pc_k = _spread_matrix(), _piece_matrix(COL_A), _piece_matrix(COL_B)
    one_q, one_k, one_v = _ones_row([COL_B]), _ones_row([COL_A, COL_C]), _ones_row([COL_A])

    def body(z_ref, bf_ref, gq_ref, gk_ref, gs_ref, ws_ref, bst_ref, go_ref, sp_ref, pq_ref, pk_ref, oq_ref, ok_ref,
             ov_ref, q_ref, k_ref, v_ref, y_ref, carry_ref, qn_sc, kn_sc):
        i = pl.program_id(0)

        @pl.when(i == 0)
        def _():
            carry_ref[...] = jnp.zeros_like(carry_ref)

        for h in range(FOX_HEADS):
            hs = slice(h * FOX_HD, (h + 1) * FOX_HD)
            qh = z_ref[:, Z_Q + h * FOX_HD:Z_Q + (h + 1) * FOX_HD]
            kh = z_ref[:, Z_K + h * FOX_HD:Z_K + (h + 1) * FOX_HD]
            qn_sc[:, hs] = (qh * _rstd(qh) * gq_ref[...] * 0.125).astype(BF)
            kn_sc[:, hs] = (kh * _rstd(kh) * gk_ref[...]).astype(BF)

        fl = z_ref[:, Z_F:Z_F + LANES] + bf_ref[...]
        logf = jnp.minimum(fl, 0.0) - jnp.log1p(jnp.exp(-jnp.abs(fl)))
        csum = _hi(_tri(tm, True).astype(F32), logf) + carry_ref[...]
        carry_ref[...] = csum[tm - 1:tm, :]
        sp = sp_ref[...]
        q_ref[...] = (_nn(qn_sc[...], sp) + _nn(_pieces(csum), pq_ref[...]) + oq_ref[...]).astype(BF)
        k_ref[...] = (_nn(kn_sc[...], sp) + _nn(_pieces(-csum), pk_ref[...]) + ok_ref[...]).astype(BF)
        v_ref[...] = (_nn(z_ref[:, Z_V:Z_V + FOX_W].astype(BF), sp) + ov_ref[...]).astype(BF)

        u = _gelu(z_ref[:, Z_U:Z_U + GMLP_W])
        vg = _gelu(z_ref[:, Z_G:Z_G + GMLP_W])
        vgn = (vg * _rstd(vg) * gs_ref[...]).astype(BF)
        mixed, _ = _spatial_mix(vgn, ws_ref, bst_ref[...], tm)
        sgu = u * mixed
        y_ref[...] = (sgu * _rstd(sgu) * go_ref[...]).astype(BF)

    row = lambda i: (i, 0)
    fix2 = lambda i: (0, 0)
    return pl.pallas_call(
        body, name="mix_prep", grid=(T // tm,),
        in_specs=[pl.BlockSpec((tm, ZW), row),
                  pl.BlockSpec((1, LANES), fix2), pl.BlockSpec((1, FOX_HD), fix2), pl.BlockSpec((1, FOX_HD), fix2),
                  pl.BlockSpec((1, GMLP_W), fix2), pl.BlockSpec((GMLP_G, CHUNK, CHUNK), lambda i: (0, 0, 0)),
                  pl.BlockSpec((CHUNK, GMLP_G), fix2), pl.BlockSpec((1, GMLP_W), fix2),
                  pl.BlockSpec((FOX_W, AUG_W), fix2), pl.BlockSpec((LANES, AUG_W), fix2),
                  pl.BlockSpec((LANES, AUG_W), fix2), pl.BlockSpec((1, AUG_W), fix2), pl.BlockSpec((1, AUG_W), fix2),
                  pl.BlockSpec((1, AUG_W), fix2)],
        out_specs=[pl.BlockSpec((tm, AUG_W), row), pl.BlockSpec((tm, AUG_W), row), pl.BlockSpec((tm, AUG_W), row),
                   pl.BlockSpec((tm, GMLP_W), row)],
        out_shape=[S((T, AUG_W), BF), S((T, AUG_W), BF), S((T, AUG_W), BF), S((T, GMLP_W), BF)],
        scratch_shapes=[pltpu.VMEM((1, LANES), F32), pltpu.VMEM((tm, FOX_W), BF), pltpu.VMEM((tm, FOX_W), BF)],
        compiler_params=_cp(1))(z, bf128, g_q, g_k, g_sgu, w_s, b_st, g_go, spread, pc_q, pc_k, one_q, one_k, one_v)


def _fox_fwd(q, k, v):
    T = q.shape[0]
    tq = _tile(T, 1024)
    nq = T // tq

    def body(q_ref, k_ref, v_ref, o_ref, lse_ref, m_sc, acc_sc):
        i, j = pl.program_id(0), pl.program_id(1)

        @pl.when(j == 0)
        def _():
            m_sc[...] = jnp.full(m_sc.shape, NEG, F32)
            acc_sc[...] = jnp.zeros_like(acc_sc)

        def step(masked):
            mask = _tri(tq, True) if masked else None
            for h in range(FOX_HEADS):
                hb = slice(h * HB, (h + 1) * HB)
                s = _nt(q_ref[:, hb], k_ref[:, hb])
                if masked:
                    s = jnp.where(mask, s, NEG)
                m_prev = m_sc[h]
                m_new = jnp.maximum(m_prev, jnp.broadcast_to(jnp.max(s, axis=1, keepdims=True), (tq, HB)))
                p = jnp.exp(s - jnp.tile(m_new, (1, tq // HB))).astype(BF)
                acc_sc[:, hb] = jnp.exp(m_prev - m_new) * acc_sc[:, hb] + _nn(p, v_ref[:, hb])
                m_sc[h] = m_new

        @pl.when(j < i)
        def _():
            step(False)

        @pl.when(j == i)
        def _():
            step(True)
            lse_ref[...] = jnp.zeros_like(lse_ref)
            for h in range(FOX_HEADS):
                l = acc_sc[:, h * HB + COL_A:h * HB + COL_A + 1]
                o_ref[:, h * FOX_HD:(h + 1) * FOX_HD] = acc_sc[:, h * HB:h * HB + FOX_HD] / l
                lse_ref[:, h:h + 1] = m_sc[h][:, 0:1] + jnp.log(l)

    qi = lambda i, j: (i, 0)
    kj = lambda i, j: (jnp.minimum(i, j), 0)
    return pl.pallas_call(
        body, name="fox_fwd", grid=(nq, nq),
        in_specs=[pl.BlockSpec((tq, AUG_W), qi), pl.BlockSpec((tq, AUG_W), kj), pl.BlockSpec((tq, AUG_W), kj)],
        out_specs=[pl.BlockSpec((tq, FOX_W), qi), pl.BlockSpec((tq, LANES), qi)],
        out_shape=[S((T, FOX_W), F32), S((T, LANES), F32)],
        scratch_shapes=[pltpu.VMEM((FOX_HEADS, tq, HB), F32), pltpu.VMEM((tq, AUG_W), F32)],
        compiler_params=_cp(2))(q, k, v)


def _fox_bwd(q, k, v, dob):
    T = q.shape[0]
    tq = _tile(T, 512)
    nq = T // tq
    half = AUG_W // 2
    hpg = FOX_HEADS // 2

    pairs = [(j, i) for j in range(nq) for i in range(j, nq)]
    jt = jnp.asarray([p[0] for p in pairs], jnp.int32)
    it = jnp.asarray([p[1] for p in pairs], jnp.int32)

    def body(jt_ref, it_ref, q_ref, k_ref, v_ref, do_ref, dq_ref, dk_ref, dv_ref, dq_sc):
        t = pl.program_id(1)
        j, i = jt_ref[t], it_ref[t]

        @pl.when(t == 0)
        def _():
            dq_sc[...] = jnp.zeros_like(dq_sc)

        @pl.when(i == j)
        def _():
            dk_ref[...] = jnp.zeros_like(dk_ref)
            dv_ref[...] = jnp.zeros_like(dv_ref)

        def step(masked):
            rows = pl.ds(pl.multiple_of(i * tq, tq), tq)
            mask = _tri(tq, True) if masked else None
            for h in range(hpg):
                hb = slice(h * HB, (h + 1) * HB)
                qh, kh, vh, doh = q_ref[:, hb], k_ref[:, hb], v_ref[:, hb], do_ref[:, hb]
                s = _nt(qh, kh)
                if masked:
                    s = jnp.where(mask, s, NEG)
                p = jnp.exp(s)
                dsb = (p * _nt(doh, vh)).astype(BF)
                dv_ref[:, hb] += _tn(p.astype(BF), doh)
                dk_ref[:, hb] += _tn(dsb, qh)
                dq_sc[rows, hb] += _nn(dsb, kh)

        @pl.when(i > j)
        def _():
            step(False)

        @pl.when(i == j)
        def _():
            step(True)
            dq_ref[...] = dq_sc[pl.ds(pl.multiple_of(j * tq, tq), tq), :]

    qi = pl.BlockSpec((tq, half), lambda g, t, jt_ref, it_ref: (it_ref[t], g))
    kj = pl.BlockSpec((tq, half), lambda g, t, jt_ref, it_ref: (jt_ref[t], g))
    return pl.pallas_call(
        body, name="fox_bwd",
        grid_spec=pltpu.PrefetchScalarGridSpec(
            num_scalar_prefetch=2, grid=(2, len(pairs)), in_specs=[qi, kj, kj, qi], out_specs=[kj, kj, kj],
            scratch_shapes=[pltpu.VMEM((T, half), F32)]),
        out_shape=[S((T, AUG_W), F32), S((T, AUG_W), F32), S((T, AUG_W), F32)],
        compiler_params=_cp(2))(jt, it, q, k, v, dob)


def _mix_out(attn, yg, g_fo, wout, x):
    T, D = x.shape
    tm = _tile(T, 512)

    def body(a_ref, y_ref, g_ref, w_ref, x_ref, o_ref):
        at = a_ref[...]
        yf = (at * _rstd(at) * g_ref[...]).astype(BF)
        o_ref[...] = x_ref[...] + _nn(yf, w_ref[:FOX_W, :]) + _nn(y_ref[...], w_ref[FOX_W:, :])

    row = lambda i: (i, 0)
    return pl.pallas_call(
        body, name="mix_out", grid=(T // tm,),
        in_specs=[pl.BlockSpec((tm, FOX_W), row), pl.BlockSpec((tm, GMLP_W), row),
                  pl.BlockSpec((1, FOX_W), lambda i: (0, 0)), pl.BlockSpec((D, D), lambda i: (0, 0)),
                  pl.BlockSpec((tm, D), row)],
        out_specs=pl.BlockSpec((tm, D), row),
        out_shape=S((T, D), F32),
        compiler_params=_cp(1))(attn, yg, g_fo, wout, x)


def _mix_out_bwd(dx, attn, yg, g_fo, wout, qf, lse):
    T, D = dx.shape
    tm = _tile(T, 512)
    n = T // tm
    spread, pc_l, pc_d = _spread_matrix(), _piece_matrix(COL_C), _piece_matrix(COL_A)

    def body(dx_ref, a_ref, y_ref, g_ref, w_ref, qf_ref, lse_ref, sp_ref, pl_ref, pd_ref,
             qb_ref, dob_ref, dyg_ref, dw_ref, dg_ref, acc_ref, dsum_ref):
        i = pl.program_id(0)
        dxb = dx_ref[...].astype(BF)
        at = a_ref[...]
        yf = (at * _rstd(at) * g_ref[...]).astype(BF)
        dy = _nt(dxb, w_ref[...])
        p_top = _tn(yf, dxb)
        p_bot = _tn(y_ref[...], dxb)

        @pl.when(i == 0)
        def _():
            acc_ref[:FOX_W, :] = p_top
            acc_ref[FOX_W:, :] = p_bot

        @pl.when(i > 0)
        def _():
            acc_ref[:FOX_W, :] += p_top
            acc_ref[FOX_W:, :] += p_bot

        @pl.when(i == n - 1)
        def _():
            dw_ref[...] = acc_ref[...].astype(BF)

        dat, dgr = _norm_bwd(dy[:, :FOX_W], at, g_ref[...])
        _acc_rows(dg_ref, i == 0, dgr)
        dyg_ref[...] = dy[:, FOX_W:]
        prod = dat * at
        dsum_ref[...] = jnp.zeros_like(dsum_ref)
        for h in range(FOX_HEADS):
            dsum_ref[:, h:h + 1] = jnp.sum(prod[:, h * FOX_HD:(h + 1) * FOX_HD], axis=1, keepdims=True)
        dob_ref[...] = (_nn(dat.astype(BF), sp_ref[...]) + _nn(_pieces(-dsum_ref[...]), pd_ref[...])).astype(BF)
        qb_ref[...] = (qf_ref[...].astype(F32) + _nn(_pieces(-lse_ref[...]), pl_ref[...])).astype(BF)

    row = lambda i: (i, 0)
    fix = lambda i: (0, 0)
    return pl.pallas_call(
        body, name="mix_out_bwd", grid=(n,),
        in_specs=[pl.BlockSpec((tm, D), row), pl.BlockSpec((tm, FOX_W), row), pl.BlockSpec((tm, GMLP_W), row),
                  pl.BlockSpec((1, FOX_W), fix), pl.BlockSpec((D, D), fix), pl.BlockSpec((tm, AUG_W), row),
                  pl.BlockSpec((tm, LANES), row), pl.BlockSpec((FOX_W, AUG_W), fix), pl.BlockSpec((LANES, AUG_W), fix),
                  pl.BlockSpec((LANES, AUG_W), fix)],
        out_specs=[pl.BlockSpec((tm, AUG_W), row), pl.BlockSpec((tm, AUG_W), row), pl.BlockSpec((tm, GMLP_W), row),
                   pl.BlockSpec((D, D), fix), pl.BlockSpec((1, FOX_W), fix)],
        out_shape=[S((T, AUG_W), BF), S((T, AUG_W), BF), S((T, GMLP_W), F32), S((D, D), BF), S((1, FOX_W), F32)],
        scratch_shapes=[pltpu.VMEM((D, D), F32), pltpu.VMEM((tm, LANES), F32)],
        compiler_params=_cp(1))(dx, attn, yg, g_fo, wout, qf, lse, spread, pc_l, pc_d)


def _mix_prep_bwd(z, dq, dk, dv, dyg, bf128, g_q, g_k, g_sgu, w_s, b_st, g_go):
    T = z.shape[0]
    tm = _tile(T, 512)
    n = T // tm

    def body(z_ref, dq_ref, dk_ref, dv_ref, dyg_ref, bf_ref, gq_ref, gk_ref, gs_ref, ws_ref,
             bst_ref, go_ref, dz_ref, dgq_ref, dgk_ref, dgs_ref, dgo_ref, dws_ref, dbst_ref, dbf_ref, carry_ref):
        i = pl.program_id(0)
        first = i == 0

        @pl.when(first)
        def _():
            carry_ref[...] = jnp.zeros_like(carry_ref)

        lane = lax.broadcasted_iota(jnp.int32, (tm, LANES), 1)
        dc = jnp.zeros((tm, LANES), F32)
        gq_rows, gk_rows = [], []
        for h in range(FOX_HEADS):
            hp = slice(h * HB, h * HB + FOX_HD)
            dqh, gqr = _norm_bwd(dq_ref[:, hp] * 0.125, z_ref[:, Z_Q + h * FOX_HD:Z_Q + (h + 1) * FOX_HD], gq_ref[...])
            dkh, gkr = _norm_bwd(dk_ref[:, hp], z_ref[:, Z_K + h * FOX_HD:Z_K + (h + 1) * FOX_HD], gk_ref[...])
            dz_ref[:, Z_Q + h * FOX_HD:Z_Q + (h + 1) * FOX_HD] = dqh.astype(BF)
            dz_ref[:, Z_K + h * FOX_HD:Z_K + (h + 1) * FOX_HD] = dkh.astype(BF)
            dz_ref[:, Z_V + h * FOX_HD:Z_V + (h + 1) * FOX_HD] = dv_ref[:, hp].astype(BF)
            dch = dq_ref[:, h * HB + COL_A:h * HB + COL_A + 1] - dk_ref[:, h * HB + COL_B:h * HB + COL_B + 1]
            dc = jnp.where(lane == h, dch, dc)
            gq_rows.append(gqr)
            gk_rows.append(gkr)
        _acc_rows(dgq_ref, first, functools.reduce(lambda a, b: a + b, gq_rows))
        _acc_rows(dgk_ref, first, functools.reduce(lambda a, b: a + b, gk_rows))

        dlogf = _hi(_tri(tm, False).astype(F32), dc) + carry_ref[...]
        carry_ref[...] = dlogf[0:1, :]
        fl = z_ref[:, Z_F:Z_F + LANES] + bf_ref[...]
        lane = lax.broadcasted_iota(jnp.int32, (tm, LANES), 1)
        df = jnp.where(lane < FOX_HEADS, dlogf * jax.nn.sigmoid(-fl), 0.0)
        dz_ref[:, Z_F:Z_F + LANES] = df.astype(BF)
        _acc_rows(dbf_ref, first, df)

        u_pre = z_ref[:, Z_U:Z_U + GMLP_W]
        vg_pre = z_ref[:, Z_G:Z_G + GMLP_W]
        u = _gelu(u_pre)
        vg = _gelu(vg_pre)
        vgn = (vg * _rstd(vg) * gs_ref[...]).astype(BF)
        bst = bst_ref[...]
        mixed, wms = _spatial_mix(vgn, ws_ref, bst, tm)
        sgu = u * mixed
        dsgu, gor = _norm_bwd(dyg_ref[...], sgu, go_ref[...])
        _acc_rows(dgo_ref, first, gor)
        du = dsgu * mixed
        dmixed = dsgu * u
        dmb = dmixed.astype(BF)
        tril = _tri(CHUNK, True)
        dvgn_rows = []
        dws = [None] * GMLP_G
        dbs = [None] * GMLP_G
        for c in range(tm // CHUNK):
            cs = slice(c * CHUNK, (c + 1) * CHUNK)
            cols = []
            for g in range(GMLP_G):
                gs = slice(g * GMLP_GD, (g + 1) * GMLP_GD)
                dmc = dmb[cs, gs]
                pw = _nt(dmc, vgn[cs, gs])
                pb = jnp.sum(dmixed[cs, gs], axis=1, keepdims=True)
                dws[g] = pw if dws[g] is None else dws[g] + pw
                dbs[g] = pb if dbs[g] is None else dbs[g] + pb
                cols.append(_tn(wms[g], dmc))
            dvgn_rows.append(jnp.concatenate(cols, axis=1))
        dvgn = jnp.concatenate(dvgn_rows, axis=0)
        dbs_t = jnp.concatenate(dbs, axis=1)
        for g in range(GMLP_G):
            dwg = jnp.where(tril, dws[g], 0.0)

            @pl.when(first)
            def _():
                dws_ref[g] = dwg

            @pl.when(jnp.logical_not(first))
            def _():
                dws_ref[g] += dwg

        @pl.when(first)
        def _():
            dbst_ref[...] = dbs_t

        @pl.when(jnp.logical_not(first))
        def _():
            dbst_ref[...] += dbs_t

        dvg, gsr = _norm_bwd(dvgn, vg, gs_ref[...])
        _acc_rows(dgs_ref, first, gsr)
        dz_ref[:, Z_U:Z_U + GMLP_W] = (du * _gelu_grad(u_pre)).astype(BF)
        dz_ref[:, Z_G:Z_G + GMLP_W] = (dvg * _gelu_grad(vg_pre)).astype(BF)

    rev = lambda i: (n - 1 - i, 0)
    fix = lambda i: (0, 0)
    fix3 = lambda i: (0, 0, 0)
    return pl.pallas_call(
        body, name="mix_prep_bwd", grid=(n,),
        in_specs=[pl.BlockSpec((tm, ZW), rev), pl.BlockSpec((tm, AUG_W), rev), pl.BlockSpec((tm, AUG_W), rev),
                  pl.BlockSpec((tm, AUG_W), rev), pl.BlockSpec((tm, GMLP_W), rev),
                  pl.BlockSpec((1, LANES), fix), pl.BlockSpec((1, FOX_HD), fix), pl.BlockSpec((1, FOX_HD), fix),
                  pl.BlockSpec((1, GMLP_W), fix), pl.BlockSpec((GMLP_G, CHUNK, CHUNK), fix3),
                  pl.BlockSpec((CHUNK, GMLP_G), fix), pl.BlockSpec((1, GMLP_W), fix)],
        out_specs=[pl.BlockSpec((tm, ZW), rev), pl.BlockSpec((1, FOX_HD), fix), pl.BlockSpec((1, FOX_HD), fix),
                   pl.BlockSpec((1, GMLP_W), fix), pl.BlockSpec((1, GMLP_W), fix),
                   pl.BlockSpec((GMLP_G, CHUNK, CHUNK), fix3), pl.BlockSpec((CHUNK, GMLP_G), fix),
                   pl.BlockSpec((1, LANES), fix)],
        out_shape=[S((T, ZW), BF), S((1, FOX_HD), F32), S((1, FOX_HD), F32), S((1, GMLP_W), F32), S((1, GMLP_W), F32),
                   S((GMLP_G, CHUNK, CHUNK), F32), S((CHUNK, GMLP_G), F32), S((1, LANES), F32)],
        scratch_shapes=[pltpu.VMEM((1, LANES), F32)],
        compiler_params=_cp(1))(z, dq, dk, dv, dyg, bf128, g_q, g_k, g_sgu, w_s, b_st, g_go)


def _mix_proj_bwd(dz, wz, x, g, dy):
    T, D = x.shape
    tm = _tile(T, 512)

    def body(dz_ref, w_ref, x_ref, g_ref, dy_ref, dx_ref, dxb_ref, dg_ref):
        dh = _nn(dz_ref[...], w_ref[...])
        dx, dgr = _norm_bwd(dh, x_ref[...], g_ref[...])
        dx = dx + dy_ref[...]
        dx_ref[...] = dx
        dxb_ref[...] = dx.astype(BF)
        _acc_rows(dg_ref, pl.program_id(0) == 0, dgr)

    row = lambda i: (i, 0)
    fix = lambda i: (0, 0)
    return pl.pallas_call(
        body, name="mix_proj_bwd", grid=(T // tm,),
        in_specs=[pl.BlockSpec((tm, ZW), row), pl.BlockSpec((ZW, D), fix), pl.BlockSpec((tm, D), row),
                  pl.BlockSpec((1, D), fix), pl.BlockSpec((tm, D), row)],
        out_specs=[pl.BlockSpec((tm, D), row), pl.BlockSpec((tm, D), row), pl.BlockSpec((1, D), fix)],
        out_shape=[S((T, D), F32), S((T, D), BF), S((1, D), F32)],
        compiler_params=_cp(1))(dz, wz, x, g, dy)


def _ca_kv(mem, g_mem, wckv, g_ck):
    M, D = mem.shape

    def body(m_ref, g_ref, w_ref, gk_ref, mn_ref, kr_ref, kn_ref, v_ref):
        mf = m_ref[...]
        mn = (mf * _rstd(mf) * g_ref[...]).astype(BF)
        mn_ref[...] = mn
        for h in range(CA_HEADS):
            kr = _nn(mn, w_ref[h])
            kr_ref[h] = kr
            kn_ref[h] = (kr * _rstd(kr) * gk_ref[...]).astype(BF)
            v_ref[h] = _nn(mn, w_ref[CA_HEADS + h]).astype(BF)

    hd = (CA_HEADS, M, CA_HD)
    return pl.pallas_call(
        body, name="ca_kv", out_shape=[S((M, D), BF), S(hd, F32), S(hd, BF), S(hd, BF)],
        compiler_params=pltpu.CompilerParams(vmem_limit_bytes=VMEM_LIMIT))(mem, g_mem, wckv, g_ck)


def _ca_tile_fwd(xt, gca, wcq, gcq, kn_ref, v_ref):
    hb = (xt * _rstd(xt) * gca).astype(BF)
    qc = _nn(hb, wcq)
    qr, qn, ps = [], [], []
    for h in range(CA_HEADS):
        qh = qc[:, h * CA_HD:(h + 1) * CA_HD]
        qnh = (qh * _rstd(qh) * gcq * 0.0625).astype(BF)
        s = _nt(qnh, kn_ref[h])
        e = jnp.exp(s - jnp.max(s, axis=1, keepdims=True))
        ps.append(e / jnp.sum(e, axis=1, keepdims=True))
        qr.append(qh)
        qn.append(qnh)
    return hb, qr, qn, ps


def _ca_fwd(x, g_ca, wcq, g_cq, kn, vv, wco):
    T, D = x.shape
    M = kn.shape[1]
    tm = _tile(T, 512)

    def body(x_ref, gca_ref, wcq_ref, gcq_ref, kn_ref, v_ref, wco_ref, o_ref, ob_sc):
        xt = x_ref[...]
        _, _, _, ps = _ca_tile_fwd(xt, gca_ref[...], wcq_ref[...], gcq_ref[...], kn_ref, v_ref)
        for h in range(CA_HEADS):
            ob_sc[:, h * CA_HD:(h + 1) * CA_HD] = _nn(ps[h].astype(BF), v_ref[h]).astype(BF)
        o_ref[...] = xt + _nn(ob_sc[...], wco_ref[...])

    row = lambda i: (i, 0)
    fix = lambda i: (0, 0)
    fix3 = lambda i: (0, 0, 0)
    return pl.pallas_call(
        body, name="ca_fwd", grid=(T // tm,),
        in_specs=[pl.BlockSpec((tm, D), row), pl.BlockSpec((1, D), fix), pl.BlockSpec((D, D), fix),
                  pl.BlockSpec((1, CA_HD), fix), pl.BlockSpec((CA_HEADS, M, CA_HD), fix3),
                  pl.BlockSpec((CA_HEADS, M, CA_HD), fix3), pl.BlockSpec((D, D), fix)],
        out_specs=pl.BlockSpec((tm, D), row), out_shape=S((T, D), F32),
        scratch_shapes=[pltpu.VMEM((tm, D), BF)],
        compiler_params=_cp(1))(x, g_ca, wcq, g_cq, kn, vv, wco)


def _ca_bwd(x, dy, g_ca, wcq, g_cq, kn, vv, wco):
    T, D = x.shape
    M = kn.shape[1]
    tm = _tile(T, 512)
    n = T // tm

    def body(x_ref, dy_ref, gca_ref, wcq_ref, gcq_ref, kn_ref, v_ref, wco_ref,
             dx_ref, dwq_ref, dwo_ref, dkn_ref, dv_ref, dgcq_ref, dgca_ref, aq_sc, ao_sc, ob_sc, dq_sc):
        i = pl.program_id(0)
        first = i == 0
        xt = x_ref[...]
        dyt = dy_ref[...]
        dyb = dyt.astype(BF)
        hb, qr, qn, ps = _ca_tile_fwd(xt, gca_ref[...], wcq_ref[...], gcq_ref[...], kn_ref, v_ref)
        do = _nt(dyb, wco_ref[...])
        gcq_rows = None
        for h in range(CA_HEADS):
            hs = slice(h * CA_HD, (h + 1) * CA_HD)
            p = ps[h]
            pb = p.astype(BF)
            ob_sc[:, hs] = _nn(pb, v_ref[h]).astype(BF)
            doh = do[:, hs].astype(BF)
            dp = _nt(doh, v_ref[h])
            ds = (p * (dp - jnp.sum(dp * p, axis=1, keepdims=True))).astype(BF)
            dvh = _tn(pb, doh)
            dkh = _tn(ds, qn[h])

            @pl.when(first)
            def _():
                dv_ref[h] = dvh
                dkn_ref[h] = dkh

            @pl.when(jnp.logical_not(first))
            def _():
                dv_ref[h] += dvh
                dkn_ref[h] += dkh

            dqn = _nn(ds, kn_ref[h]) * 0.0625
            dqh, gr = _norm_bwd(dqn, qr[h], gcq_ref[...])
            gcq_rows = gr if gcq_rows is None else gcq_rows + gr
            dq_sc[:, hs] = dqh.astype(BF)
        _acc_rows(dgcq_ref, first, gcq_rows)
        dqb = dq_sc[...]
        p_o = _tn(ob_sc[...], dyb)
        p_q = _tn(hb, dqb)

        @pl.when(first)
        def _():
            ao_sc[...] = p_o
            aq_sc[...] = p_q

        @pl.when(jnp.logical_not(first))
        def _():
            ao_sc[...] += p_o
            aq_sc[...] += p_q

        @pl.when(i == n - 1)
        def _():
            dwo_ref[...] = ao_sc[...].astype(BF)
            dwq_ref[...] = aq_sc[...].astype(BF)

        dh = _nt(dqb, wcq_ref[...])
        dx, gar = _norm_bwd(dh, xt, gca_ref[...])
        dx_ref[...] = dx + dyt
        _acc_rows(dgca_ref, first, gar)

    row = lambda i: (i, 0)
    fix = lambda i: (0, 0)
    fix3 = lambda i: (0, 0, 0)
    hd = (CA_HEADS, M, CA_HD)
    return pl.pallas_call(
        body, name="ca_bwd", grid=(n,),
        in_specs=[pl.BlockSpec((tm, D), row), pl.BlockSpec((tm, D), row), pl.BlockSpec((1, D), fix),
                  pl.BlockSpec((D, D), fix), pl.BlockSpec((1, CA_HD), fix), pl.BlockSpec(hd, fix3),
                  pl.BlockSpec(hd, fix3), pl.BlockSpec((D, D), fix)],
        out_specs=[pl.BlockSpec((tm, D), row), pl.BlockSpec((D, D), fix), pl.BlockSpec((D, D), fix),
                   pl.BlockSpec(hd, fix3), pl.BlockSpec(hd, fix3), pl.BlockSpec((1, CA_HD), fix),
                   pl.BlockSpec((1, D), fix)],
        out_shape=[S((T, D), F32), S((D, D), BF), S((D, D), BF), S(hd, F32), S(hd, F32), S((1, CA_HD), F32),
                   S((1, D), F32)],
        scratch_shapes=[pltpu.VMEM((D, D), F32), pltpu.VMEM((D, D), F32), pltpu.VMEM((tm, D), BF),
                        pltpu.VMEM((tm, D), BF)],
        compiler_params=_cp(1))(x, dy, g_ca, wcq, g_cq, kn, vv, wco)


def _ca_kv_bwd(mem, g_mem, mn, kraw, dkn, dvv, wckv, g_ck):
    M, D = mem.shape

    def body(m_ref, g_ref, mn_ref, kr_ref, dkn_ref, dv_ref, w_ref, gk_ref, dw_ref, dgk_ref, dgm_ref):
        mn = mn_ref[...]
        dmn = jnp.zeros((M, D), F32)
        gk_rows = None
        for h in range(CA_HEADS):
            dkr, gr = _norm_bwd(dkn_ref[h], kr_ref[h], gk_ref[...])
            gk_rows = gr if gk_rows is None else gk_rows + gr
            dkb = dkr.astype(BF)
            dvb = dv_ref[h].astype(BF)
            dw_ref[h] = _tn(mn, dkb).astype(BF)
            dw_ref[CA_HEADS + h] = _tn(mn, dvb).astype(BF)
            dmn = dmn + _nt(dkb, w_ref[h]) + _nt(dvb, w_ref[CA_HEADS + h])
        dgk_ref[...] = jnp.sum(gk_rows, axis=0, keepdims=True)
        mf = m_ref[...]
        dgm_ref[...] = jnp.sum(dmn * (mf * _rstd(mf)), axis=0, keepdims=True)

    return pl.pallas_call(
        body, name="ca_kv_bwd",
        out_shape=[S((2 * CA_HEADS, D, CA_HD), BF), S((1, CA_HD), F32), S((1, D), F32)],
        compiler_params=pltpu.CompilerParams(vmem_limit_bytes=VMEM_LIMIT))(mem, g_mem, mn, kraw, dkn, dvv, wckv, g_ck)


def _after(g, token):
    return g if token is None else g + token[0:1, 0:1]


def _local_step(x, mem, target, small, weights, emit):
    T, D = x.shape
    p = small
    bf128 = jnp.pad(p["b_f"], ((0, 0), (0, LANES - FOX_HEADS)))
    b_st = p["b_s"].T

    wup1 = weights("ffn1_up", x)["wup1"]
    a1, h1 = _ffn_up("ffn1_up", x, p["g_ffn1"], wup1)
    wdn1 = weights("ffn1_dn", h1)["wdn1"]
    x1 = _ffn_down("ffn1_down", a1, wdn1, x)
    wm = weights("mix", x1)
    z, h2 = _mix_proj(x1, p["g_mix"], wm["wz"])
    qf, ka, va, yg = _mix_prep(z, bf128, p["g_q"], p["g_k"], p["g_sgu"], p["w_s"], b_st, p["g_gmlp_o"])
    attn, lse = _fox_fwd(qf, ka, va)
    x2 = _mix_out(attn, yg, p["g_fox_o"], wm["wout"], x1)
    wc = weights("ca", x2)
    mn, kraw, ckn, cvv = _ca_kv(mem, p["g_mem"], wc["wckv"], p["g_ck"])
    x3 = _ca_fwd(x2, p["g_ca"], wc["wcq"], p["g_cq"], ckn, cvv, wc["wco"])
    w2 = weights("ffn2", x3)
    a2, h4 = _ffn_up("ffn2_up", x3, p["g_ffn2"], w2["wup2"])
    dy4, dy4b, sq = _ffn_down_loss("ffn2_down", a2, w2["wdn2"], x3, target)

    gs = {}
    dgu2 = _ffn_bwd_act("ffn2_bwd_act", dy4b, h4, w2["wup2"], w2["wdn2"])
    tok = emit("ffn2", {"wup2": _ffn_dwup("ffn2", h4, dgu2), "wdn2": _ffn_dwdn("ffn2", a2, dy4b)})
    dx3, gs["g_ffn2"] = _ffn_dx("ffn2_dx", dgu2, w2["wup2"], x3, _after(p["g_ffn2"], tok), dy4)

    dx2, dwcq, dwco, dckn, dcvv, gs["g_cq"], gs["g_ca"] = _ca_bwd(
        x2, dx3, p["g_ca"], wc["wcq"], p["g_cq"], ckn, cvv, wc["wco"])
    dwckv, gs["g_ck"], gs["g_mem"] = _ca_kv_bwd(mem, p["g_mem"], mn, kraw, dckn, dcvv, wc["wckv"], p["g_ck"])

    qb, dob, dyg, dwout, gs["g_fox_o"] = _mix_out_bwd(dx2, attn, yg, p["g_fox_o"], wm["wout"], qf, lse)
    dq, dk, dv = _fox_bwd(qb, ka, va, dob)
    dz, gs["g_q"], gs["g_k"], gs["g_sgu"], gs["g_gmlp_o"], gs["w_s"], dbst, dbf = _mix_prep_bwd(
        z, dq, dk, dv, dyg, bf128, p["g_q"], p["g_k"], p["g_sgu"], p["w_s"], b_st, p["g_gmlp_o"])
    gs["b_s"] = dbst.T
    gs["b_f"] = dbf[:, :FOX_HEADS]
    tok_ws = emit("w_s", {"w_s": gs["w_s"]})
    tk = _tile(T, 1024)
    zb = ZW // 3
    dwz = _tn_matmul(
        "mix_dwz", dz, pl.BlockSpec((tk, zb), lambda j, k: (k, j)), h2, pl.BlockSpec((tk, D), lambda j, k: (k, 0)),
        S((ZW, D), F32), pl.BlockSpec((zb, D), lambda j, k: (j, 0)), (3, T // tk), (zb, D))
    tok = emit("mid", {"wcq": dwcq, "wco": dwco, "wckv": dwckv, "wout": dwout, "wz": dwz})
    dx1, dx1b, gs["g_mix"] = _mix_proj_bwd(dz, wm["wz"], x1, _after(_after(p["g_mix"], tok), tok_ws), dx2)

    dgu1 = _ffn_bwd_act("ffn1_bwd_act", dx1b, h1, wup1, wdn1)
    tok = emit("ffn1_dn", {"wdn1": _ffn_dwdn("ffn1", a1, dx1b)})
    tok = emit("ffn1_up", {"wup1": _ffn_dwup("ffn1", h1, dgu1, after=tok)})
    dx0, gs["g_ffn1"] = _ffn_dx("ffn1_dx", dgu1, wup1, x, _after(p["g_ffn1"], tok), dx1)
    return sq, dx0, gs


MESH = pl.DeviceIdType.MESH
HBM_SPEC = pl.BlockSpec(memory_space=pltpu.HBM)
N_PEER = N_DEV - 1


def _place():
    return lax.axis_index("x"), lax.axis_index("y"), lax.axis_index("c")


def _slot(px, py, pc):
    return 4 * px + 2 * py + pc


SEM_SPEC = pl.BlockSpec(memory_space=pltpu.SEMAPHORE)
ANY_SPEC = pl.BlockSpec(memory_space=pl.ANY)
DATAFLOW = pltpu.SideEffectType.DATAFLOW_SIDE_EFFECTING


def _hbm(a):
    return pltpu.with_memory_space_constraint(a, pltpu.HBM)


def _peer(x, y, c, r):
    return (1 - x if r & 4 else x, 1 - y if r & 2 else y, 1 - c if r & 1 else c)


def _place_own(srcs, whole):
    my = _slot(*_place())
    lands = []
    for s in srcs:
        blk = s[None] if whole else lax.dynamic_slice_in_dim(s, my, 1, 0)
        shape = (N_DEV,) + s.shape if whole else s.shape
        lands.append(lax.dynamic_update_slice_in_dim(lax.empty(shape, s.dtype), blk, my, 0))
    return lands


ALL_PEERS = tuple(range(1, N_DEV))
NEAR_PEERS = (1, 2, 4, 6)
SAME_CORE = (2, 4, 6)


def _copy_start(name, srcs, lands, whole, peers=None):
    n = len(srcs)
    peers = peers or [ALL_PEERS] * n

    def body(*refs):
        src, land = refs[:n], refs[n:2 * n]
        send, recv = refs[2 * n:3 * n], refs[3 * n:4 * n]
        token = refs[6 * n]
        x, y, c = _place()
        my = _slot(x, y, c)
        for a in range(n):
            for r in peers[a]:
                p = _peer(x, y, c, r)
                pltpu.make_async_remote_copy(
                    src_ref=src[a] if whole else src[a].at[_slot(*p)], dst_ref=land[a].at[my],
                    send_sem=send[a].at[r - 1], recv_sem=recv[a].at[r - 1], device_id=p, device_id_type=MESH).start()
        token[...] = jnp.zeros_like(token)

    out = pl.pallas_call(
        body, name=name,
        out_shape=([pltpu.SemaphoreType.DMA((N_PEER,))] * (2 * n)
                   + [pltpu.HBM(s.shape, s.dtype) for s in srcs] + [pltpu.HBM(s.shape, s.dtype) for s in lands]
                   + [S((8, LANES), F32)]),
        in_specs=[HBM_SPEC] * (2 * n),
        out_specs=[SEM_SPEC] * (2 * n) + [HBM_SPEC] * (2 * n) + [pl.BlockSpec(memory_space=pltpu.VMEM)],
        input_output_aliases={i: 2 * n + i for i in range(2 * n)},
        compiler_params=pltpu.CompilerParams(has_side_effects=DATAFLOW),
    )(*[_hbm(s) for s in srcs], *[_hbm(s) for s in lands])
    return out[:n], out[n:2 * n], out[2 * n:3 * n], out[3 * n:4 * n], out[4 * n]


def _copy_wait(name, srcs, lands, send, recv, after, whole, peers=None, with_srcs=False):
    n = len(srcs)
    peers = peers or [ALL_PEERS] * n

    def body(*refs):
        src, land = refs[:n], refs[n:2 * n]
        snd, rcv = refs[2 * n:3 * n], refs[3 * n:4 * n]
        x, y, c = _place()
        for a in range(n):
            for r in peers[a]:
                p = _peer(x, y, c, r)
                ps = _slot(*p)
                cp = pltpu.make_async_remote_copy(
                    src_ref=src[a] if whole else src[a].at[ps], dst_ref=land[a].at[ps],
                    send_sem=snd[a].at[r - 1], recv_sem=rcv[a].at[r - 1], device_id=p, device_id_type=MESH)
                cp.wait_send()
                cp.wait_recv()

    out = pl.pallas_call(
        body, name=name,
        out_shape=[pltpu.HBM(s.shape, s.dtype) for s in srcs] + [pltpu.HBM(s.shape, s.dtype) for s in lands],
        in_specs=[HBM_SPEC] * (2 * n) + [SEM_SPEC] * (2 * n) + [ANY_SPEC],
        out_specs=[HBM_SPEC] * (2 * n),
        input_output_aliases={i: i for i in range(2 * n)},
        compiler_params=pltpu.CompilerParams(has_side_effects=DATAFLOW),
    )(*srcs, *lands, *send, *recv, after)
    return (out[:n], out[n:]) if with_srcs else out[n:]


def _forward_start(name, lands):
    n = len(lands)

    def body(*refs):
        land = refs[:n]
        send, recv = refs[n:2 * n], refs[2 * n:3 * n]
        token = refs[4 * n]
        x, y, c = _place()
        for a in range(n):
            for r in SAME_CORE:
                blk = land[a].at[_slot(*_peer(x, y, c, r))]
                pltpu.make_async_remote_copy(
                    src_ref=blk, dst_ref=blk, send_sem=send[a].at[r - 1], recv_sem=recv[a].at[r - 1],
                    device_id=(x, y, 1 - c), device_id_type=MESH).start()
        token[...] = jnp.zeros_like(token)

    out = pl.pallas_call(
        body, name=name,
        out_shape=([pltpu.SemaphoreType.DMA((N_PEER,))] * (2 * n) + [pltpu.HBM(s.shape, s.dtype) for s in lands]
                   + [S((8, LANES), F32)]),
        in_specs=[HBM_SPEC] * n,
        out_specs=[SEM_SPEC] * (2 * n) + [HBM_SPEC] * n + [pl.BlockSpec(memory_space=pltpu.VMEM)],
        input_output_aliases={i: 2 * n + i for i in range(n)},
        compiler_params=pltpu.CompilerParams(has_side_effects=DATAFLOW),
    )(*[_hbm(s) for s in lands])
    return out[:n], out[n:2 * n], out[2 * n:3 * n], out[3 * n]


def _forward_wait(name, lands, send, recv, after):
    n = len(lands)

    def body(*refs):
        land = refs[:n]
        snd, rcv = refs[n:2 * n], refs[2 * n:3 * n]
        x, y, c = _place()
        for a in range(n):
            for r in SAME_CORE:
                cp = pltpu.make_async_remote_copy(
                    src_ref=land[a].at[_slot(*_peer(x, y, c, r))], dst_ref=land[a].at[_slot(*_peer(x, y, c, r | 1))],
                    send_sem=snd[a].at[r - 1], recv_sem=rcv[a].at[r - 1], device_id=(x, y, 1 - c),
                    device_id_type=MESH)
                cp.wait_send()
                cp.wait_recv()

    return pl.pallas_call(
        body, name=name,
        out_shape=[pltpu.HBM(s.shape, s.dtype) for s in lands],
        in_specs=[HBM_SPEC] * n + [SEM_SPEC] * (2 * n) + [ANY_SPEC],
        out_specs=[HBM_SPEC] * n,
        input_output_aliases={i: i for i in range(n)},
        compiler_params=pltpu.CompilerParams(has_side_effects=DATAFLOW),
    )(*lands, *send, *recv, after)


def _adamw(w, g, m, v):
    m2 = ADAM_B1 * m + (1.0 - ADAM_B1) * g
    v2 = ADAM_B2 * v + (1.0 - ADAM_B2) * (g * g)
    m_hat = m2 / (1.0 - ADAM_B1 ** ADAM_STEP)
    v_hat = v2 / (1.0 - ADAM_B2 ** ADAM_STEP)
    delta = -ADAM_LR * (m_hat / (jnp.sqrt(v_hat) + ADAM_EPS) + ADAM_WD * w)
    return delta, m2, v2


def _adamw_big(name, slots, w, m, v, own=None):
    R, C = w.shape
    tr = next((t for t in (256, 352) if R % t == 0), R)

    def finish(g, w_ref, m_ref, v_ref, g_ref, d_ref, m2_ref, v2_ref):
        d, m2, v2 = _adamw(w_ref[...], g, m_ref[...], v_ref[...])
        g_ref[...] = g
        d_ref[...] = d
        m2_ref[...] = m2
        v2_ref[...] = v2

    if own is None:
        def body(s_ref, *refs):
            g = s_ref[0].astype(F32)
            for k in range(1, N_DEV):
                g = g + s_ref[k].astype(F32)
            finish(g, *refs)

        row = pl.BlockSpec((tr, C), lambda i: (i, 0))
        return pl.pallas_call(
            body, name=name, grid=(R // tr,),
            in_specs=[pl.BlockSpec((N_DEV, tr, C), lambda i: (0, i, 0)), row, row, row],
            out_specs=[row] * 4, out_shape=[S((R, C), F32)] * 4,
            compiler_params=_cp(1))(slots, w, m, v)

    def body(my_ref, s_ref, own_ref, *refs):
        mine = own_ref[...]
        g = None
        for k in range(N_DEV):
            part = jnp.where(my_ref[0] == k, mine, s_ref[k]).astype(F32)
            g = part if g is None else g + part
        finish(g, *refs)

    row = pl.BlockSpec((tr, C), lambda i, my_ref: (i, 0))
    my = jnp.reshape(_slot(*_place()), (1,)).astype(jnp.int32)
    return pl.pallas_call(
        body, name=name,
        grid_spec=pltpu.PrefetchScalarGridSpec(
            num_scalar_prefetch=1, grid=(R // tr,),
            in_specs=[pl.BlockSpec((N_DEV, tr, C), lambda i, my_ref: (0, i, 0)),
                      pl.BlockSpec((None, tr, C), lambda i, my_ref: (my_ref[0], i, 0)), row, row, row],
            out_specs=[row] * 4),
        out_shape=[S((R, C), F32)] * 4, compiler_params=_cp(1))(my, slots, own, w, m, v)


TINY_ROWS = (("b_s", 8), ("g_ffn1", 8), ("g_mix", 8), ("g_ca", 8), ("g_mem", 8), ("g_ffn2", 8), ("g_sgu", 4),
             ("g_fox_o", 4), ("g_gmlp_o", 4), ("g_cq", 2), ("g_ck", 2), ("g_q", 1), ("g_k", 1), ("b_f", 1),
             ("loss", 1))
TINY_P = 72


def _tiny_pieces(width):
    return [(j, slice(j * LANES, min((j + 1) * LANES, width))) for j in range(-(-width // LANES))]


def _pack_tiny(grads, sq):
    names = [n for n, _ in TINY_ROWS if n != "loss"]

    def body(*refs):
        ins, sq_ref, o_ref = refs[:len(names)], refs[len(names)], refs[len(names) + 1]
        o_ref[...] = jnp.zeros_like(o_ref)
        at = 0
        for ref, (name, r) in zip(ins, TINY_ROWS):
            if name == "b_s":
                o_ref[at:at + r, :] = ref[...]
            else:
                for j, cols in _tiny_pieces(ref.shape[1]):
                    o_ref[at + j:at + j + 1, 0:cols.stop - cols.start] = ref[:, cols]
            at += r
        o_ref[at:at + 1, :] = sq_ref[0:1, :]

    return pl.pallas_call(body, name="tiny_pack", out_shape=S((TINY_P, LANES), F32))(
        *[grads[n] for n in names], sq)


def _adamw_tiny(slots, w, m, v):
    names = [n for n, _ in TINY_ROWS if n != "loss"]
    k = len(names)

    def body(s_ref, *refs):
        ins, outs, loss_ref = refs[:3 * k], refs[3 * k:7 * k], refs[7 * k]
        g_all = s_ref[0]
        for d in range(1, N_DEV):
            g_all = g_all + s_ref[d]
        at = 0
        for i, (name, r) in enumerate(TINY_ROWS[:k]):
            w_ref, m_ref, v_ref = ins[i], ins[k + i], ins[2 * k + i]
            o = outs[4 * i:4 * i + 4]
            if name == "b_s":
                pieces = [(slice(at, at + r), slice(0, LANES), (slice(None), slice(None)))]
            else:
                pieces = [(slice(at + j, at + j + 1), slice(0, c.stop - c.start), (slice(None), c))
                          for j, c in _tiny_pieces(w_ref.shape[1])]
            for rows, lanes, dst in pieces:
                g = g_all[rows, lanes]
                res = (g,) + _adamw(w_ref[dst], g, m_ref[dst], v_ref[dst])
                for ref, val in zip(o, res):
                    ref[dst] = val
            at += r
        loss_ref[...] = g_all[at:at + 1, :]

    shapes = [S(w[n].shape, F32) for n in names]
    out = pl.pallas_call(
        body, name="adamw_tiny", out_shape=[s for s in shapes for _ in range(4)] + [S((1, LANES), F32)],
    )(slots, *[w[n] for n in names], *[m[n] for n in names], *[v[n] for n in names])
    stores = ({}, {}, {}, {})
    for i, n in enumerate(names):
        for store, t in zip(stores, out[4 * i:4 * i + 4]):
            store[n] = t
    return stores, out[4 * k]


WEIGHTS =('g_ffn1', 'w_ffn1_in', 'w_ffn1_out', 'g_mix', 'w_in', 'b_f', 'g_q', 'g_k', 'g_sgu', 'w_s', 'b_s',
           'g_fox_o', 'g_gmlp_o', 'w_out', 'g_ca', 'g_mem', 'w_cq', 'w_ckv', 'g_cq', 'g_ck', 'w_co', 'g_ffn2',
           'w_ffn2_in', 'w_ffn2_out')
BIG = ('w_ffn1_in', 'w_ffn1_out', 'w_in', 'w_out', 'w_cq', 'w_ckv', 'w_co', 'w_ffn2_in', 'w_ffn2_out')
TRANSPOSED = ('w_ffn1_in', 'w_in', 'w_ffn2_in')
TWO_LEVEL = ('w_ffn1_in', 'w_in')
GATHER_GROUPS = {"ffn1_up": ("w_ffn1_in",), "ffn1_dn": ("w_ffn1_out",), "mix": ("w_in", "w_out"),
                 "ca": ("w_cq", "w_ckv", "w_co"), "ffn2": ("w_ffn2_in", "w_ffn2_out")}
QKV_W = 3 * FOX_W
UV_OFF = QKV_W + FOX_HEADS


def kernel(x, mem, g_ffn1, w_ffn1_in, w_ffn1_out, g_mix, w_in, b_f, g_q, g_k, g_sgu, w_s, b_s, g_fox_o, g_gmlp_o, w_out, g_ca, g_mem, w_cq, w_ckv, g_cq, g_ck, w_co, g_ffn2, w_ffn2_in, w_ffn2_out, loss_target, m_g_ffn1, m_w_ffn1_in, m_w_ffn1_out, m_g_mix, m_w_in, m_b_f, m_g_q, m_g_k, m_g_sgu, m_w_s, m_b_s, m_g_fox_o, m_g_gmlp_o, m_w_out, m_g_ca, m_g_mem, m_w_cq, m_w_ckv, m_g_cq, m_g_ck, m_w_co, m_g_ffn2, m_w_ffn2_in, m_w_ffn2_out, v_g_ffn1, v_w_ffn1_in, v_w_ffn1_out, v_g_mix, v_w_in, v_b_f, v_g_q, v_g_k, v_g_sgu, v_w_s, v_b_s, v_g_fox_o, v_g_gmlp_o, v_w_out, v_g_ca, v_g_mem, v_w_cq, v_w_ckv, v_g_cq, v_g_ck, v_w_co, v_g_ffn2, v_w_ffn2_in, v_w_ffn2_out):
    args = dict(locals())
    w = {n: args[n] for n in WEIGHTS}
    mo = {n: args["m_" + n] for n in WEIGHTS}
    vo = {n: args["v_" + n] for n in WEIGHTS}
    D = D_MODEL

    def local(n, a):
        return a[0].T if n in TRANSPOSED else a[0]

    shards = [local(n, w[n]).astype(BF) for n in BIG]
    fb = shards[0].shape[0]
    g_peers = [NEAR_PEERS if n in TWO_LEVEL else ALL_PEERS for n in BIG]
    g_snd, g_rcv, g_src, g_land, g_token = _copy_start("gather_start", shards, _place_own(shards, True), True,
                                                       peers=g_peers)
    handles = {n: (g_src[i], g_land[i], g_snd[i], g_rcv[i]) for i, n in enumerate(BIG)}

    tiny_names = [n for n, _ in TINY_ROWS if n != "loss"]

    def weights(group, after):
        names = GATHER_GROUPS[group]
        hs = [handles[n] for n in names]
        got = list(_copy_wait("gather_wait_" + group, [h[0] for h in hs], [h[1] for h in hs], [h[2] for h in hs],
                              [h[3] for h in hs], after, True, peers=[g_peers[BIG.index(n)] for n in names]))
        passed = [i for i, n in enumerate(names) if n in TWO_LEVEL]
        if passed:
            f_snd, f_rcv, f_land, f_token = _forward_start("gather_pass_start_" + group, [got[i] for i in passed])
            for i, t in zip(passed, _forward_wait("gather_pass_wait_" + group, f_land, f_snd, f_rcv, f_token)):
                got[i] = t
        got = dict(zip(names, got))
        if group == "ffn1_up":
            return {"wup1": got["w_ffn1_in"].reshape(2, N_FFN_BLK, fb, D)}
        if group == "ffn1_dn":
            return {"wdn1": got["w_ffn1_out"].reshape(N_FFN_BLK, fb, D)}
        if group == "mix":
            full = got["w_in"].reshape(-1, D)
            wz = jnp.concatenate([full[:QKV_W], full[UV_OFF:], full[QKV_W:UV_OFF],
                                  jnp.zeros((LANES - FOX_HEADS, D), BF)], axis=0)
            return {"wz": wz, "wout": got["w_out"].reshape(D, D)}
        if group == "ca":
            return {"wcq": got["w_cq"].reshape(D, D), "wco": got["w_co"].reshape(D, D), "wckv": got["w_ckv"]}
        return {"wup2": got["w_ffn2_in"].reshape(2, N_FFN_BLK, fb, D),
                "wdn2": got["w_ffn2_out"].reshape(N_FFN_BLK, fb, D)}

    flying = {}

    def emit(group, g):
        if group == "w_s":
            part = [g["w_s"].reshape(-1, LANES)]
            *copies, token = _copy_start("w_s_start", part, _place_own(part, True), True)
            flying[group] = copies
            return token
        if group == "ffn2":
            parts = {"w_ffn2_in": g["wup2"], "w_ffn2_out": g["wdn2"].reshape(N_DEV, -1, D)}
        elif group == "ffn1_dn":
            parts = {"w_ffn1_out": g["wdn1"].reshape(N_DEV, -1, D)}
        elif group == "ffn1_up":
            parts = {"w_ffn1_in": g["wup1"]}
        else:
            gz = g["wz"]
            g_in = jnp.concatenate([gz[:QKV_W], gz[Z_F:Z_F + FOX_HEADS], gz[QKV_W:Z_F]], axis=0)
            parts = {"w_in": g_in.reshape(N_DEV, -1, D).astype(BF),
                     "w_out": g["wout"].reshape(N_DEV, -1, D), "w_cq": g["wcq"].reshape(N_DEV, -1, D),
                     "w_co": g["wco"].reshape(N_DEV, -1, D), "w_ckv": g["wckv"]}
        names = list(parts)
        srcs = [parts[n] for n in names]
        *copies, token = _copy_start("exchange_start_" + group, srcs, [lax.empty(s.shape, s.dtype) for s in srcs],
                                     False)
        flying[group] = (names, copies)
        return token

    small = {n: (w[n][0] if n == "b_s" else w[n]) for n in tiny_names}
    small["w_s"] = w["w_s"][0]

    sq, dx0, gs = _local_step(x[0], mem[0], loss_target[0], small, weights, emit)

    sm_parts = [_pack_tiny(gs, sq)]
    sm_snd, sm_rcv, sm_src, sm_land, sm_token = _copy_start("tiny_start", sm_parts, _place_own(sm_parts, True), True)

    grad, delta, new_m, new_v = {}, {}, {}, {}

    def update(group, after):
        names, (snd, rcv, srcs, lands) = flying[group]
        owns, slots = _copy_wait("exchange_wait_" + group, srcs, lands, snd, rcv, after, False, with_srcs=True)
        for n, sl, own in zip(names, slots, owns):
            g, d, m2, v2 = _adamw_big("adamw_" + n, sl, local(n, w[n]), local(n, mo[n]), local(n, vo[n]), own=own)
            grad[n], delta[n], new_m[n], new_v[n] = (
                (t.T if n in TRANSPOSED else t).reshape(w[n].shape) for t in (g, d, m2, v2))
        return d

    last = update("ffn2", sm_token)
    last = update("mid", last)
    last = update("ffn1_dn", last)
    last = update("ffn1_up", last)
    ws_snd, ws_rcv, ws_src, ws_land = flying["w_s"]
    ws_all, = _copy_wait("w_s_wait", ws_src, ws_land, ws_snd, ws_rcv, last, True)
    tiny_all, = _copy_wait("tiny_wait", sm_src, sm_land, sm_snd, sm_rcv, ws_all, True)
    ws_shape = w["w_s"].shape
    for store, t in zip((grad, delta, new_m, new_v), _adamw_big(
            "adamw_w_s", ws_all, *[a["w_s"].reshape(-1, LANES) for a in (w, mo, vo)])):
        store["w_s"] = t.reshape(ws_shape)
    stores, loss_row = _adamw_tiny(tiny_all, *[{n: (a[n][0] if n == "b_s" else a[n]) for n in tiny_names}
                                               for a in (w, mo, vo)])
    for store, t in zip((grad, delta, new_m, new_v), stores):
        store.update({n: v.reshape(w[n].shape) for n, v in t.items()})
    loss = loss_row[0, 0] * (0.5 / D)

    return (loss, dx0[None], *[grad[n] for n in WEIGHTS], *[delta[n] for n in WEIGHTS],
            *[new_m[n] for n in WEIGHTS], *[new_v[n] for n in WEIGHTS])
```

```python
import functools

import jax
import jax.numpy as jnp
from jax import lax
from jax.experimental import pallas as pl
from jax.experimental.pallas import tpu as pltpu

F32 = jnp.float32
BF = jnp.bfloat16
S = jax.ShapeDtypeStruct

N_DEV = 8
D_MODEL = 1024
FOX_HEADS, FOX_HD = 8, 64
FOX_W = 512
GMLP_G, GMLP_GD = 8, 64
GMLP_W = 512
CHUNK = 128
CA_HEADS, CA_HD = 4, 256
N_FFN_BLK = 4
ZW = 2688
Z_Q, Z_K, Z_V, Z_U, Z_G, Z_F = 0, 512, 1024, 1536, 2048, 2560
EPS = 1e-6
NEG = -1e30
LANES = 128

ADAM_LR, ADAM_B1, ADAM_B2, ADAM_EPS, ADAM_WD, ADAM_STEP = 0.001, 0.9, 0.999, 1e-08, 0.01, 10

VMEM_LIMIT = 52 * 2 ** 20


def _cp(n_axes):
    return pltpu.CompilerParams(dimension_semantics=("arbitrary",) * n_axes, vmem_limit_bytes=VMEM_LIMIT)


def _nn(a, b):
    return jnp.dot(a, b, preferred_element_type=F32)


def _nt(a, b):
    return lax.dot_general(a, b, (((1,), (1,)), ((), ())), preferred_element_type=F32)


def _tn(a, b):
    return lax.dot_general(a, b, (((0,), (0,)), ((), ())), preferred_element_type=F32)


def _hi(a, b):
    return jnp.dot(a, b, precision=lax.Precision.HIGHEST, preferred_element_type=F32)


def _rstd(x):
    return lax.rsqrt(jnp.mean(x * x, axis=-1, keepdims=True) + EPS)


def _norm_bwd(dy, x, g):
    r = _rstd(x)
    xh = x * r
    dxh = dy * g
    dx = r * (dxh - xh * jnp.mean(dxh * xh, axis=-1, keepdims=True))
    return dx, dy * xh


def _acc_rows(ref, first, val):
    srow = jnp.sum(val, axis=0, keepdims=True)

    @pl.when(first)
    def _():
        ref[...] = srow

    @pl.when(jnp.logical_not(first))
    def _():
        ref[...] += srow


def _gelu(x):
    c = 0.7978845608028654
    return 0.5 * x * (1.0 + jnp.tanh(c * (x + 0.044715 * x * x * x)))


def _gelu_grad(x):
    c = 0.7978845608028654
    t = jnp.tanh(c * (x + 0.044715 * x * x * x))
    return 0.5 * (1.0 + t) + 0.5 * x * (1.0 - t * t) * c * (1.0 + 3 * 0.044715 * x * x)


def _tile(n, pref):
    return pref if n % pref == 0 else n


def _ffn_up(name, x, g, wup):
    T, D = x.shape
    FB = wup.shape[-2]
    tm = _tile(T, 1024)

    def body(x_ref, g_ref, w_ref, a_ref, h_ref):
        @pl.when(pl.program_id(1) == 0)
        def _():
            xf = x_ref[...]
            h_ref[...] = (xf * _rstd(xf) * g_ref[...]).astype(BF)

        hb = h_ref[...]
        gg = _nt(hb, w_ref[0])
        uu = _nt(hb, w_ref[1])
        a_ref[...] = (gg * jax.nn.sigmoid(gg) * uu).astype(BF)

    return pl.pallas_call(
        body, name=name, grid=(T // tm, N_FFN_BLK),
        in_specs=[pl.BlockSpec((tm, D), lambda i, j: (i, 0)),
                  pl.BlockSpec((1, D), lambda i, j: (0, 0)),
                  pl.BlockSpec((2, None, FB, D), lambda i, j: (0, j, 0, 0))],
        out_specs=[pl.BlockSpec((None, tm, FB), lambda i, j: (j, i, 0)),
                   pl.BlockSpec((tm, D), lambda i, j: (i, 0))],
        out_shape=[S((N_FFN_BLK, T, FB), BF), S((T, D), BF)],
        compiler_params=_cp(2))(x, g, wup)


def _ffn_down(name, a, wdn, x):
    _, T, FB = a.shape
    D = x.shape[1]
    tm = _tile(T, 1024)

    def body(a_ref, w_ref, x_ref, o_ref):
        p = _nn(a_ref[0], w_ref[0])
        for j in range(1, N_FFN_BLK):
            p = p + _nn(a_ref[j], w_ref[j])
        o_ref[...] = x_ref[...] + 0.5 * p

    return pl.pallas_call(
        body, name=name, grid=(T // tm,),
        in_specs=[pl.BlockSpec((N_FFN_BLK, tm, FB), lambda i: (0, i, 0)),
                  pl.BlockSpec((N_FFN_BLK, FB, D), lambda i: (0, 0, 0)),
                  pl.BlockSpec((tm, D), lambda i: (i, 0))],
        out_specs=pl.BlockSpec((tm, D), lambda i: (i, 0)),
        out_shape=S((T, D), F32),
        compiler_params=_cp(1))(a, wdn, x)


def _ffn_down_loss(name, a, wdn, x, target):
    _, T, FB = a.shape
    D = x.shape[1]
    tm = _tile(T, 1024)

    def body(a_ref, w_ref, x_ref, t_ref, d_ref, db_ref, loss_ref):
        i = pl.program_id(0)
        p = _nn(a_ref[0], w_ref[0])
        for j in range(1, N_FFN_BLK):
            p = p + _nn(a_ref[j], w_ref[j])
        diff = (x_ref[...] + 0.5 * p) - t_ref[...]
        dy = diff * (1.0 / D)
        d_ref[...] = dy
        db_ref[...] = dy.astype(BF)
        sq = jnp.zeros((8, LANES), F32) + jnp.sum(diff * diff)

        @pl.when(i == 0)
        def _():
            loss_ref[...] = sq

        @pl.when(i > 0)
        def _():
            loss_ref[...] += sq

    row = pl.BlockSpec((tm, D), lambda i: (i, 0))
    return pl.pallas_call(
        body, name=name, grid=(T // tm,),
        in_specs=[pl.BlockSpec((N_FFN_BLK, tm, FB), lambda i: (0, i, 0)),
                  pl.BlockSpec((N_FFN_BLK, FB, D), lambda i: (0, 0, 0)), row, row],
        out_specs=[row, row, pl.BlockSpec((8, LANES), lambda i: (0, 0))],
        out_shape=[S((T, D), F32), S((T, D), BF), S((8, LANES), F32)],
        compiler_params=_cp(1))(a, wdn, x, target)


def _ffn_bwd_act(name, dyb, h, wup, wdn):
    T, D = h.shape
    FB = wup.shape[-2]
    tm = _tile(T, 1024)

    def body(d_ref, h_ref, wu_ref, wd_ref, o_ref):
        da = 0.5 * _nt(d_ref[...], wd_ref[...])
        hb = h_ref[...]
        gg = _nt(hb, wu_ref[0])
        uu = _nt(hb, wu_ref[1])
        sg = jax.nn.sigmoid(gg)
        o_ref[0] = (da * uu * (sg * (1.0 + gg * (1.0 - sg)))).astype(BF)
        o_ref[1] = (da * (gg * sg)).astype(BF)

    return pl.pallas_call(
        body, name=name, grid=(T // tm, N_FFN_BLK),
        in_specs=[pl.BlockSpec((tm, D), lambda i, j: (i, 0)),
                  pl.BlockSpec((tm, D), lambda i, j: (i, 0)),
                  pl.BlockSpec((2, None, FB, D), lambda i, j: (0, j, 0, 0)),
                  pl.BlockSpec((None, FB, D), lambda i, j: (j, 0, 0))],
        out_specs=pl.BlockSpec((2, None, tm, FB), lambda i, j: (0, j, i, 0)),
        out_shape=S((2, N_FFN_BLK, T, FB), BF),
        compiler_params=_cp(2))(dyb, h, wup, wdn)


def _ffn_dx(name, dgu, wup, x, g, dy):
    T, D = x.shape
    FB = wup.shape[-2]
    tm = _tile(T, 1024)

    def body(d_ref, w_ref, x_ref, g_ref, dy_ref, dx_ref, dg_ref, acc_ref):
        i, j = pl.program_id(0), pl.program_id(1)
        p = _nn(d_ref[0], w_ref[0]) + _nn(d_ref[1], w_ref[1])

        @pl.when(j == 0)
        def _():
            acc_ref[...] = p

        @pl.when(j > 0)
        def _():
            acc_ref[...] += p

        @pl.when(j == N_FFN_BLK - 1)
        def _():
            dx, dgr = _norm_bwd(acc_ref[...], x_ref[...], g_ref[...])
            dx_ref[...] = dx + dy_ref[...]
            _acc_rows(dg_ref, i == 0, dgr)

    return pl.pallas_call(
        body, name=name, grid=(T // tm, N_FFN_BLK),
        in_specs=[pl.BlockSpec((2, None, tm, FB), lambda i, j: (0, j, i, 0)),
                  pl.BlockSpec((2, None, FB, D), lambda i, j: (0, j, 0, 0)),
                  pl.BlockSpec((tm, D), lambda i, j: (i, 0)),
                  pl.BlockSpec((1, D), lambda i, j: (0, 0)),
                  pl.BlockSpec((tm, D), lambda i, j: (i, 0))],
        out_specs=[pl.BlockSpec((tm, D), lambda i, j: (i, 0)),
                   pl.BlockSpec((1, D), lambda i, j: (0, 0))],
        out_shape=[S((T, D), F32), S((1, D), F32)],
        scratch_shapes=[pltpu.VMEM((tm, D), F32)],
        compiler_params=_cp(2))(dgu, wup, x, g, dy)


def _tn_matmul(name, a, a_spec, b, b_spec, out_shape, out_spec, grid, acc_shape, scale=1.0, after=None):
    nk = grid[1]
    extra = [] if after is None else [after]

    def body(a_ref, b_ref, *rest):
        o_ref, acc_ref = rest[-2:]
        k = pl.program_id(1)
        p = _tn(a_ref[...], b_ref[...])

        @pl.when(k == 0)
        def _():
            acc_ref[...] = p

        @pl.when(k > 0)
        def _():
            acc_ref[...] += p

        @pl.when(k == nk - 1)
        def _():
            o_ref[...] = (acc_ref[...] * scale).astype(o_ref.dtype)

    return pl.pallas_call(
        body, name=name, grid=grid,
        in_specs=[a_spec, b_spec] + [pl.BlockSpec((8, LANES), lambda j, k: (0, 0)) for _ in extra],
        out_specs=out_spec, out_shape=out_shape,
        scratch_shapes=[pltpu.VMEM(acc_shape, F32)], compiler_params=_cp(2))(a, b, *extra)


def _ffn_dwup(name, h, dgu, after=None):
    T, D = h.shape
    FB = dgu.shape[-1]
    tk = _tile(T, 1024)
    return _tn_matmul(
        name + "_dwup", dgu.reshape(2 * N_FFN_BLK, T, FB), pl.BlockSpec((None, tk, FB), lambda j, k: (j, k, 0)),
        h, pl.BlockSpec((tk, D), lambda j, k: (k, 0)),
        S((2 * N_FFN_BLK, FB, D), BF), pl.BlockSpec((None, FB, D), lambda j, k: (j, 0, 0)),
        (2 * N_FFN_BLK, T // tk), (FB, D), after=after)


def _ffn_dwdn(name, a, dyb):
    _, T, FB = a.shape
    D = dyb.shape[1]
    tk = _tile(T, 1024)
    return _tn_matmul(
        name + "_dwdn", a, pl.BlockSpec((None, tk, FB), lambda j, k: (j, k, 0)),
        dyb, pl.BlockSpec((tk, D), lambda j, k: (k, 0)),
        S((N_FFN_BLK, FB, D), BF), pl.BlockSpec((None, FB, D), lambda j, k: (j, 0, 0)),
        (N_FFN_BLK, T // tk), (FB, D), scale=0.5)


def _mix_proj(x, g, wz):
    T, D = x.shape
    tm = _tile(T, 1024)

    def body(x_ref, g_ref, w_ref, z_ref, h_ref):
        xf = x_ref[...]
        hb = (xf * _rstd(xf) * g_ref[...]).astype(BF)
        h_ref[...] = hb
        z_ref[...] = _nt(hb, w_ref[...])

    return pl.pallas_call(
        body, name="mix_proj", grid=(T // tm,),
        in_specs=[pl.BlockSpec((tm, D), lambda i: (i, 0)),
                  pl.BlockSpec((1, D), lambda i: (0, 0)),
                  pl.BlockSpec((ZW, D), lambda i: (0, 0))],
        out_specs=[pl.BlockSpec((tm, ZW), lambda i: (i, 0)),
                   pl.BlockSpec((tm, D), lambda i: (i, 0))],
        out_shape=[S((T, ZW), F32), S((T, D), BF)],
        compiler_params=_cp(1))(x, g, wz)


def _tri(n, lower):
    r = lax.broadcasted_iota(jnp.int32, (n, n), 0)
    c = lax.broadcasted_iota(jnp.int32, (n, n), 1)
    return (r >= c) if lower else (r <= c)


def _spatial_mix(vgn_b, ws_ref, bst, tm):
    tril = _tri(CHUNK, True)
    wms = [jnp.where(tril, ws_ref[g], 0.0).astype(BF) for g in range(GMLP_G)]
    rows = []
    for c in range(tm // CHUNK):
        cols = []
        for g in range(GMLP_G):
            vs = vgn_b[c * CHUNK:(c + 1) * CHUNK, g * GMLP_GD:(g + 1) * GMLP_GD]
            cols.append(_nn(wms[g], vs) + bst[:, g:g + 1])
        rows.append(jnp.concatenate(cols, axis=1))
    return jnp.concatenate(rows, axis=0), wms


HB = 128
AUG_W = FOX_HEADS * HB
COL_A, COL_B, COL_C = 64, 67, 70


def _spread_matrix():
    r = jnp.arange(FOX_W)
    return (jnp.arange(AUG_W)[None, :] == ((r // FOX_HD) * HB + r % FOX_HD)[:, None]).astype(BF)


def _piece_matrix(col):
    r = jnp.arange(LANES)
    dst = jnp.where(r < 3 * FOX_HEADS, (r % FOX_HEADS) * HB + col + r // FOX_HEADS, -1)
    return (jnp.arange(AUG_W)[None, :] == dst[:, None]).astype(BF)


def _ones_row(cols):
    c = jnp.arange(AUG_W) % HB
    hit = functools.reduce(jnp.logical_or, [(c >= a) & (c < a + 3) for a in cols])
    return hit.astype(F32)[None, :]


def _pieces(x):
    lane = lax.broadcasted_iota(jnp.int32, x.shape, 1)
    x = jnp.where(lane < FOX_HEADS, x, 0.0)
    hi = x.astype(BF).astype(F32)
    r1 = x - hi
    mid = r1.astype(BF).astype(F32)
    lo = (r1 - mid).astype(BF).astype(F32)
    return (hi + pltpu.roll(mid, FOX_HEADS, 1) + pltpu.roll(lo, 2 * FOX_HEADS, 1)).astype(BF)


def _mix_prep(z, bf128, g_q, g_k, g_sgu, w_s, b_st, g_go):
    T = z.shape[0]
    tm = _tile(T, 512)
    spread, pc_q, pc_k = _spread_matrix(), _piece_matrix(COL_A), _piece_matrix(COL_B)
    one_q, one_k, one_v = _ones_row([COL_B]), _ones_row([COL_A, COL_C]), _ones_row([COL_A])

    def body(z_ref, bf_ref, gq_ref, gk_ref, gs_ref, ws_ref, bst_ref, go_ref, sp_ref, pq_ref, pk_ref, oq_ref, ok_ref,
             ov_ref, q_ref, k_ref, v_ref, y_ref, carry_ref, qn_sc, kn_sc):
        i = pl.program_id(0)

        @pl.when(i == 0)
        def _():
            carry_ref[...] = jnp.zeros_like(carry_ref)

        for h in range(FOX_HEADS):
            hs = slice(h * FOX_HD, (h + 1) * FOX_HD)
            qh = z_ref[:, Z_Q + h * FOX_HD:Z_Q + (h + 1) * FOX_HD]
            kh = z_ref[:, Z_K + h * FOX_HD:Z_K + (h + 1) * FOX_HD]
            qn_sc[:, hs] = (qh * _rstd(qh) * gq_ref[...] * 0.125).astype(BF)
            kn_sc[:, hs] = (kh * _rstd(kh) * gk_ref[...]).astype(BF)

        fl = z_ref[:, Z_F:Z_F + LANES] + bf_ref[...]
        logf = jnp.minimum(fl, 0.0) - jnp.log1p(jnp.exp(-jnp.abs(fl)))
        csum = _hi(_tri(tm, True).astype(F32), logf) + carry_ref[...]
        carry_ref[...] = csum[tm - 1:tm, :]
        sp = sp_ref[...]
        q_ref[...] = (_nn(qn_sc[...], sp) + _nn(_pieces(csum), pq_ref[...]) + oq_ref[...]).astype(BF)
        k_ref[...] = (_nn(kn_sc[...], sp) + _nn(_pieces(-csum), pk_ref[...]) + ok_ref[...]).astype(BF)
        v_ref[...] = (_nn(z_ref[:, Z_V:Z_V + FOX_W].astype(BF), sp) + ov_ref[...]).astype(BF)

        u = _gelu(z_ref[:, Z_U:Z_U + GMLP_W])
        vg = _gelu(z_ref[:, Z_G:Z_G + GMLP_W])
        vgn = (vg * _rstd(vg) * gs_ref[...]).astype(BF)
        mixed, _ = _spatial_mix(vgn, ws_ref, bst_ref[...], tm)
        sgu = u * mixed
        y_ref[...] = (sgu * _rstd(sgu) * go_ref[...]).astype(BF)

    row = lambda i: (i, 0)
    fix2 = lambda i: (0, 0)
    return pl.pallas_call(
        body, name="mix_prep", grid=(T // tm,),
        in_specs=[pl.BlockSpec((tm, ZW), row),
                  pl.BlockSpec((1, LANES), fix2), pl.BlockSpec((1, FOX_HD), fix2), pl.BlockSpec((1, FOX_HD), fix2),
                  pl.BlockSpec((1, GMLP_W), fix2), pl.BlockSpec((GMLP_G, CHUNK, CHUNK), lambda i: (0, 0, 0)),
                  pl.BlockSpec((CHUNK, GMLP_G), fix2), pl.BlockSpec((1, GMLP_W), fix2),
                  pl.BlockSpec((FOX_W, AUG_W), fix2), pl.BlockSpec((LANES, AUG_W), fix2),
                  pl.BlockSpec((LANES, AUG_W), fix2), pl.BlockSpec((1, AUG_W), fix2), pl.BlockSpec((1, AUG_W), fix2),
                  pl.BlockSpec((1, AUG_W), fix2)],
        out_specs=[pl.BlockSpec((tm, AUG_W), row), pl.BlockSpec((tm, AUG_W), row), pl.BlockSpec((tm, AUG_W), row),
                   pl.BlockSpec((tm, GMLP_W), row)],
        out_shape=[S((T, AUG_W), BF), S((T, AUG_W), BF), S((T, AUG_W), BF), S((T, GMLP_W), BF)],
        scratch_shapes=[pltpu.VMEM((1, LANES), F32), pltpu.VMEM((tm, FOX_W), BF), pltpu.VMEM((tm, FOX_W), BF)],
        compiler_params=_cp(1))(z, bf128, g_q, g_k, g_sgu, w_s, b_st, g_go, spread, pc_q, pc_k, one_q, one_k, one_v)


def _fox_fwd(q, k, v):
    T = q.shape[0]
    tq = _tile(T, 1024)
    nq = T // tq

    def body(q_ref, k_ref, v_ref, o_ref, lse_ref, m_sc, acc_sc):
        i, j = pl.program_id(0), pl.program_id(1)

        @pl.when(j == 0)
        def _():
            m_sc[...] = jnp.full(m_sc.shape, NEG, F32)
            acc_sc[...] = jnp.zeros_like(acc_sc)

        def step(masked):
            mask = _tri(tq, True) if masked else None
            for h in range(FOX_HEADS):
                hb = slice(h * HB, (h + 1) * HB)
                s = _nt(q_ref[:, hb], k_ref[:, hb])
                if masked:
                    s = jnp.where(mask, s, NEG)
                m_prev = m_sc[h]
                m_new = jnp.maximum(m_prev, jnp.broadcast_to(jnp.max(s, axis=1, keepdims=True), (tq, HB)))
                p = jnp.exp(s - jnp.tile(m_new, (1, tq // HB))).astype(BF)
                acc_sc[:, hb] = jnp.exp(m_prev - m_new) * acc_sc[:, hb] + _nn(p, v_ref[:, hb])
                m_sc[h] = m_new

        @pl.when(j < i)
        def _():
            step(False)

        @pl.when(j == i)
        def _():
            step(True)
            lse_ref[...] = jnp.zeros_like(lse_ref)
            for h in range(FOX_HEADS):
                l = acc_sc[:, h * HB + COL_A:h * HB + COL_A + 1]
                o_ref[:, h * FOX_HD:(h + 1) * FOX_HD] = acc_sc[:, h * HB:h * HB + FOX_HD] / l
                lse_ref[:, h:h + 1] = m_sc[h][:, 0:1] + jnp.log(l)

    qi = lambda i, j: (i, 0)
    kj = lambda i, j: (jnp.minimum(i, j), 0)
    return pl.pallas_call(
        body, name="fox_fwd", grid=(nq, nq),
        in_specs=[pl.BlockSpec((tq, AUG_W), qi), pl.BlockSpec((tq, AUG_W), kj), pl.BlockSpec((tq, AUG_W), kj)],
        out_specs=[pl.BlockSpec((tq, FOX_W), qi), pl.BlockSpec((tq, LANES), qi)],
        out_shape=[S((T, FOX_W), F32), S((T, LANES), F32)],
        scratch_shapes=[pltpu.VMEM((FOX_HEADS, tq, HB), F32), pltpu.VMEM((tq, AUG_W), F32)],
        compiler_params=_cp(2))(q, k, v)


def _fox_bwd(q, k, v, dob):
    T = q.shape[0]
    tq = _tile(T, 512)
    nq = T // tq
    half = AUG_W // 2
    hpg = FOX_HEADS // 2

    pairs = [(j, i) for j in range(nq) for i in range(j, nq)]
    jt = jnp.asarray([p[0] for p in pairs], jnp.int32)
    it = jnp.asarray([p[1] for p in pairs], jnp.int32)

    def body(jt_ref, it_ref, q_ref, k_ref, v_ref, do_ref, dq_ref, dk_ref, dv_ref, dq_sc):
        t = pl.program_id(1)
        j, i = jt_ref[t], it_ref[t]

        @pl.when(t == 0)
        def _():
            dq_sc[...] = jnp.zeros_like(dq_sc)

        @pl.when(i == j)
        def _():
            dk_ref[...] = jnp.zeros_like(dk_ref)
            dv_ref[...] = jnp.zeros_like(dv_ref)

        def step(masked):
            rows = pl.ds(pl.multiple_of(i * tq, tq), tq)
            mask = _tri(tq, True) if masked else None
            for h in range(hpg):
                hb = slice(h * HB, (h + 1) * HB)
                qh, kh, vh, doh = q_ref[:, hb], k_ref[:, hb], v_ref[:, hb], do_ref[:, hb]
                s = _nt(qh, kh)
                if masked:
                    s = jnp.where(mask, s, NEG)
                p = jnp.exp(s)
                dsb = (p * _nt(doh, vh)).astype(BF)
                dv_ref[:, hb] += _tn(p.astype(BF), doh)
                dk_ref[:, hb] += _tn(dsb, qh)
                dq_sc[rows, hb] += _nn(dsb, kh)

        @pl.when(i > j)
        def _():
            step(False)

        @pl.when(i == j)
        def _():
            step(True)
            dq_ref[...] = dq_sc[pl.ds(pl.multiple_of(j * tq, tq), tq), :]

    qi = pl.BlockSpec((tq, half), lambda g, t, jt_ref, it_ref: (it_ref[t], g))
    kj = pl.BlockSpec((tq, half), lambda g, t, jt_ref, it_ref: (jt_ref[t], g))
    return pl.pallas_call(
        body, name="fox_bwd",
        grid_spec=pltpu.PrefetchScalarGridSpec(
            num_scalar_prefetch=2, grid=(2, len(pairs)), in_specs=[qi, kj, kj, qi], out_specs=[kj, kj, kj],
            scratch_shapes=[pltpu.VMEM((T, half), F32)]),
        out_shape=[S((T, AUG_W), F32), S((T, AUG_W), F32), S((T, AUG_W), F32)],
        compiler_params=_cp(2))(jt, it, q, k, v, dob)


def _mix_out(attn, yg, g_fo, wout, x):
    T, D = x.shape
    tm = _tile(T, 1024)

    def body(a_ref, y_ref, g_ref, w_ref, x_ref, o_ref):
        at = a_ref[...]
        yf = (at * _rstd(at) * g_ref[...]).astype(BF)
        o_ref[...] = x_ref[...] + _nn(yf, w_ref[:FOX_W, :]) + _nn(y_ref[...], w_ref[FOX_W:, :])

    row = lambda i: (i, 0)
    return pl.pallas_call(
        body, name="mix_out", grid=(T // tm,),
        in_specs=[pl.BlockSpec((tm, FOX_W), row), pl.BlockSpec((tm, GMLP_W), row),
                  pl.BlockSpec((1, FOX_W), lambda i: (0, 0)), pl.BlockSpec((D, D), lambda i: (0, 0)),
                  pl.BlockSpec((tm, D), row)],
        out_specs=pl.BlockSpec((tm, D), row),
        out_shape=S((T, D), F32),
        compiler_params=_cp(1))(attn, yg, g_fo, wout, x)


def _mix_out_bwd(dx, attn, yg, g_fo, wout, qf, lse):
    T, D = dx.shape
    tm = _tile(T, 512)
    n = T // tm
    spread, pc_l, pc_d = _spread_matrix(), _piece_matrix(COL_C), _piece_matrix(COL_A)

    def body(dx_ref, a_ref, y_ref, g_ref, w_ref, qf_ref, lse_ref, sp_ref, pl_ref, pd_ref,
             qb_ref, dob_ref, dyg_ref, dw_ref, dg_ref, acc_ref, dsum_ref):
        i = pl.program_id(0)
        dxb = dx_ref[...].astype(BF)
        at = a_ref[...]
        yf = (at * _rstd(at) * g_ref[...]).astype(BF)
        dy = _nt(dxb, w_ref[...])
        p_top = _tn(yf, dxb)
        p_bot = _tn(y_ref[...], dxb)

        @pl.when(i == 0)
        def _():
            acc_ref[:FOX_W, :] = p_top
            acc_ref[FOX_W:, :] = p_bot

        @pl.when(i > 0)
        def _():
            acc_ref[:FOX_W, :] += p_top
            acc_ref[FOX_W:, :] += p_bot

        @pl.when(i == n - 1)
        def _():
            dw_ref[...] = acc_ref[...].astype(BF)

        dat, dgr = _norm_bwd(dy[:, :FOX_W], at, g_ref[...])
        _acc_rows(dg_ref, i == 0, dgr)
        dyg_ref[...] = dy[:, FOX_W:]
        prod = dat * at
        dsum_ref[...] = jnp.zeros_like(dsum_ref)
        for h in range(FOX_HEADS):
            dsum_ref[:, h:h + 1] = jnp.sum(prod[:, h * FOX_HD:(h + 1) * FOX_HD], axis=1, keepdims=True)
        dob_ref[...] = (_nn(dat.astype(BF), sp_ref[...]) + _nn(_pieces(-dsum_ref[...]), pd_ref[...])).astype(BF)
        qb_ref[...] = (qf_ref[...].astype(F32) + _nn(_pieces(-lse_ref[...]), pl_ref[...])).astype(BF)

    row = lambda i: (i, 0)
    fix = lambda i: (0, 0)
    return pl.pallas_call(
        body, name="mix_out_bwd", grid=(n,),
        in_specs=[pl.BlockSpec((tm, D), row), pl.BlockSpec((tm, FOX_W), row), pl.BlockSpec((tm, GMLP_W), row),
                  pl.BlockSpec((1, FOX_W), fix), pl.BlockSpec((D, D), fix), pl.BlockSpec((tm, AUG_W), row),
                  pl.BlockSpec((tm, LANES), row), pl.BlockSpec((FOX_W, AUG_W), fix), pl.BlockSpec((LANES, AUG_W), fix),
                  pl.BlockSpec((LANES, AUG_W), fix)],
        out_specs=[pl.BlockSpec((tm, AUG_W), row), pl.BlockSpec((tm, AUG_W), row), pl.BlockSpec((tm, GMLP_W), row),
                   pl.BlockSpec((D, D), fix), pl.BlockSpec((1, FOX_W), fix)],
        out_shape=[S((T, AUG_W), BF), S((T, AUG_W), BF), S((T, GMLP_W), F32), S((D, D), BF), S((1, FOX_W), F32)],
        scratch_shapes=[pltpu.VMEM((D, D), F32), pltpu.VMEM((tm, LANES), F32)],
        compiler_params=_cp(1))(dx, attn, yg, g_fo, wout, qf, lse, spread, pc_l, pc_d)


def _mix_prep_bwd(z, dq, dk, dv, dyg, bf128, g_q, g_k, g_sgu, w_s, b_st, g_go):
    T = z.shape[0]
    tm = _tile(T, 512)
    n = T // tm

    def body(z_ref, dq_ref, dk_ref, dv_ref, dyg_ref, bf_ref, gq_ref, gk_ref, gs_ref, ws_ref,
             bst_ref, go_ref, dz_ref, dgq_ref, dgk_ref, dgs_ref, dgo_ref, dws_ref, dbst_ref, dbf_ref, carry_ref):
        i = pl.program_id(0)
        first = i == 0

        @pl.when(first)
        def _():
            carry_ref[...] = jnp.zeros_like(carry_ref)

        lane = lax.broadcasted_iota(jnp.int32, (tm, LANES), 1)
        dc = jnp.zeros((tm, LANES), F32)
        gq_rows, gk_rows = [], []
        for h in range(FOX_HEADS):
            hp = slice(h * HB, h * HB + FOX_HD)
            dqh, gqr = _norm_bwd(dq_ref[:, hp] * 0.125, z_ref[:, Z_Q + h * FOX_HD:Z_Q + (h + 1) * FOX_HD], gq_ref[...])
            dkh, gkr = _norm_bwd(dk_ref[:, hp], z_ref[:, Z_K + h * FOX_HD:Z_K + (h + 1) * FOX_HD], gk_ref[...])
            dz_ref[:, Z_Q + h * FOX_HD:Z_Q + (h + 1) * FOX_HD] = dqh.astype(BF)
            dz_ref[:, Z_K + h * FOX_HD:Z_K + (h + 1) * FOX_HD] = dkh.astype(BF)
            dz_ref[:, Z_V + h * FOX_HD:Z_V + (h + 1) * FOX_HD] = dv_ref[:, hp].astype(BF)
            dch = dq_ref[:, h * HB + COL_A:h * HB + COL_A + 1] - dk_ref[:, h * HB + COL_B:h * HB + COL_B + 1]
            dc = jnp.where(lane == h, dch, dc)
            gq_rows.append(gqr)
            gk_rows.append(gkr)
        _acc_rows(dgq_ref, first, functools.reduce(lambda a, b: a + b, gq_rows))
        _acc_rows(dgk_ref, first, functools.reduce(lambda a, b: a + b, gk_rows))

        dlogf = _hi(_tri(tm, False).astype(F32), dc) + carry_ref[...]
        carry_ref[...] = dlogf[0:1, :]
        fl = z_ref[:, Z_F:Z_F + LANES] + bf_ref[...]
        lane = lax.broadcasted_iota(jnp.int32, (tm, LANES), 1)
        df = jnp.where(lane < FOX_HEADS, dlogf * jax.nn.sigmoid(-fl), 0.0)
        dz_ref[:, Z_F:Z_F + LANES] = df.astype(BF)
        _acc_rows(dbf_ref, first, df)

        u_pre = z_ref[:, Z_U:Z_U + GMLP_W]
        vg_pre = z_ref[:, Z_G:Z_G + GMLP_W]
        u = _gelu(u_pre)
        vg = _gelu(vg_pre)
        vgn = (vg * _rstd(vg) * gs_ref[...]).astype(BF)
        bst = bst_ref[...]
        mixed, wms = _spatial_mix(vgn, ws_ref, bst, tm)
        sgu = u * mixed
        dsgu, gor = _norm_bwd(dyg_ref[...], sgu, go_ref[...])
        _acc_rows(dgo_ref, first, gor)
        du = dsgu * mixed
        dmixed = dsgu * u
        dmb = dmixed.astype(BF)
        tril = _tri(CHUNK, True)
        dvgn_rows = []
        dws = [None] * GMLP_G
        dbs = [None] * GMLP_G
        for c in range(tm // CHUNK):
            cs = slice(c * CHUNK, (c + 1) * CHUNK)
            cols = []
            for g in range(GMLP_G):
                gs = slice(g * GMLP_GD, (g + 1) * GMLP_GD)
                dmc = dmb[cs, gs]
                pw = _nt(dmc, vgn[cs, gs])
                pb = jnp.sum(dmixed[cs, gs], axis=1, keepdims=True)
                dws[g] = pw if dws[g] is None else dws[g] + pw
                dbs[g] = pb if dbs[g] is None else dbs[g] + pb
                cols.append(_tn(wms[g], dmc))
            dvgn_rows.append(jnp.concatenate(cols, axis=1))
        dvgn = jnp.concatenate(dvgn_rows, axis=0)
        dbs_t = jnp.concatenate(dbs, axis=1)
        for g in range(GMLP_G):
            dwg = jnp.where(tril, dws[g], 0.0)

            @pl.when(first)
            def _():
                dws_ref[g] = dwg

            @pl.when(jnp.logical_not(first))
            def _():
                dws_ref[g] += dwg

        @pl.when(first)
        def _():
            dbst_ref[...] = dbs_t

        @pl.when(jnp.logical_not(first))
        def _():
            dbst_ref[...] += dbs_t

        dvg, gsr = _norm_bwd(dvgn, vg, gs_ref[...])
        _acc_rows(dgs_ref, first, gsr)
        dz_ref[:, Z_U:Z_U + GMLP_W] = (du * _gelu_grad(u_pre)).astype(BF)
        dz_ref[:, Z_G:Z_G + GMLP_W] = (dvg * _gelu_grad(vg_pre)).astype(BF)

    rev = lambda i: (n - 1 - i, 0)
    fix = lambda i: (0, 0)
    fix3 = lambda i: (0, 0, 0)
    return pl.pallas_call(
        body, name="mix_prep_bwd", grid=(n,),
        in_specs=[pl.BlockSpec((tm, ZW), rev), pl.BlockSpec((tm, AUG_W), rev), pl.BlockSpec((tm, AUG_W), rev),
                  pl.BlockSpec((tm, AUG_W), rev), pl.BlockSpec((tm, GMLP_W), rev),
                  pl.BlockSpec((1, LANES), fix), pl.BlockSpec((1, FOX_HD), fix), pl.BlockSpec((1, FOX_HD), fix),
                  pl.BlockSpec((1, GMLP_W), fix), pl.BlockSpec((GMLP_G, CHUNK, CHUNK), fix3),
                  pl.BlockSpec((CHUNK, GMLP_G), fix), pl.BlockSpec((1, GMLP_W), fix)],
        out_specs=[pl.BlockSpec((tm, ZW), rev), pl.BlockSpec((1, FOX_HD), fix), pl.BlockSpec((1, FOX_HD), fix),
                   pl.BlockSpec((1, GMLP_W), fix), pl.BlockSpec((1, GMLP_W), fix),
                   pl.BlockSpec((GMLP_G, CHUNK, CHUNK), fix3), pl.BlockSpec((CHUNK, GMLP_G), fix),
                   pl.BlockSpec((1, LANES), fix)],
        out_shape=[S((T, ZW), BF), S((1, FOX_HD), F32), S((1, FOX_HD), F32), S((1, GMLP_W), F32), S((1, GMLP_W), F32),
                   S((GMLP_G, CHUNK, CHUNK), F32), S((CHUNK, GMLP_G), F32), S((1, LANES), F32)],
        scratch_shapes=[pltpu.VMEM((1, LANES), F32)],
        compiler_params=_cp(1))(z, dq, dk, dv, dyg, bf128, g_q, g_k, g_sgu, w_s, b_st, g_go)


def _mix_proj_bwd(dz, wz, x, g, dy):
    T, D = x.shape
    tm = _tile(T, 1024)

    def body(dz_ref, w_ref, x_ref, g_ref, dy_ref, dx_ref, dxb_ref, dg_ref):
        dh = _nn(dz_ref[...], w_ref[...])
        dx, dgr = _norm_bwd(dh, x_ref[...], g_ref[...])
        dx = dx + dy_ref[...]
        dx_ref[...] = dx
        dxb_ref[...] = dx.astype(BF)
        _acc_rows(dg_ref, pl.program_id(0) == 0, dgr)

    row = lambda i: (i, 0)
    fix = lambda i: (0, 0)
    return pl.pallas_call(
        body, name="mix_proj_bwd", grid=(T // tm,),
        in_specs=[pl.BlockSpec((tm, ZW), row), pl.BlockSpec((ZW, D), fix), pl.BlockSpec((tm, D), row),
                  pl.BlockSpec((1, D), fix), pl.BlockSpec((tm, D), row)],
        out_specs=[pl.BlockSpec((tm, D), row), pl.BlockSpec((tm, D), row), pl.BlockSpec((1, D), fix)],
        out_shape=[S((T, D), F32), S((T, D), BF), S((1, D), F32)],
        compiler_params=_cp(1))(dz, wz, x, g, dy)


def _ca_kv(mem, g_mem, wckv, g_ck):
    M, D = mem.shape

    def body(m_ref, g_ref, w_ref, gk_ref, mn_ref, kr_ref, kn_ref, v_ref):
        mf = m_ref[...]
        mn = (mf * _rstd(mf) * g_ref[...]).astype(BF)
        mn_ref[...] = mn
        for h in range(CA_HEADS):
            kr = _nn(mn, w_ref[h])
            kr_ref[h] = kr
            kn_ref[h] = (kr * _rstd(kr) * gk_ref[...]).astype(BF)
            v_ref[h] = _nn(mn, w_ref[CA_HEADS + h]).astype(BF)

    hd = (CA_HEADS, M, CA_HD)
    return pl.pallas_call(
        body, name="ca_kv", out_shape=[S((M, D), BF), S(hd, F32), S(hd, BF), S(hd, BF)],
        compiler_params=pltpu.CompilerParams(vmem_limit_bytes=VMEM_LIMIT))(mem, g_mem, wckv, g_ck)


def _ca_tile_fwd(xt, gca, wcq, gcq, kn_ref, v_ref):
    hb = (xt * _rstd(xt) * gca).astype(BF)
    qc = _nn(hb, wcq)
    qr, qn, ps = [], [], []
    for h in range(CA_HEADS):
        qh = qc[:, h * CA_HD:(h + 1) * CA_HD]
        qnh = (qh * _rstd(qh) * gcq * 0.0625).astype(BF)
        s = _nt(qnh, kn_ref[h])
        e = jnp.exp(s - jnp.max(s, axis=1, keepdims=True))
        ps.append(e / jnp.sum(e, axis=1, keepdims=True))
        qr.append(qh)
        qn.append(qnh)
    return hb, qr, qn, ps


def _ca_fwd(x, g_ca, wcq, g_cq, kn, vv, wco):
    T, D = x.shape
    M = kn.shape[1]
    tm = _tile(T, 1024)

    def body(x_ref, gca_ref, wcq_ref, gcq_ref, kn_ref, v_ref, wco_ref, o_ref, ob_sc):
        xt = x_ref[...]
        _, _, _, ps = _ca_tile_fwd(xt, gca_ref[...], wcq_ref[...], gcq_ref[...], kn_ref, v_ref)
        for h in range(CA_HEADS):
            ob_sc[:, h * CA_HD:(h + 1) * CA_HD] = _nn(ps[h].astype(BF), v_ref[h]).astype(BF)
        o_ref[...] = xt + _nn(ob_sc[...], wco_ref[...])

    row = lambda i: (i, 0)
    fix = lambda i: (0, 0)
    fix3 = lambda i: (0, 0, 0)
    return pl.pallas_call(
        body, name="ca_fwd", grid=(T // tm,),
        in_specs=[pl.BlockSpec((tm, D), row), pl.BlockSpec((1, D), fix), pl.BlockSpec((D, D), fix),
                  pl.BlockSpec((1, CA_HD), fix), pl.BlockSpec((CA_HEADS, M, CA_HD), fix3),
                  pl.BlockSpec((CA_HEADS, M, CA_HD), fix3), pl.BlockSpec((D, D), fix)],
        out_specs=pl.BlockSpec((tm, D), row), out_shape=S((T, D), F32),
        scratch_shapes=[pltpu.VMEM((tm, D), BF)],
        compiler_params=_cp(1))(x, g_ca, wcq, g_cq, kn, vv, wco)


def _ca_bwd(x, dy, g_ca, wcq, g_cq, kn, vv, wco):
    T, D = x.shape
    M = kn.shape[1]
    tm = _tile(T, 512)
    n = T // tm

    def body(x_ref, dy_ref, gca_ref, wcq_ref, gcq_ref, kn_ref, v_ref, wco_ref,
             dx_ref, dwq_ref, dwo_ref, dkn_ref, dv_ref, dgcq_ref, dgca_ref, aq_sc, ao_sc, ob_sc, dq_sc):
        i = pl.program_id(0)
        first = i == 0
        xt = x_ref[...]
        dyt = dy_ref[...]
        dyb = dyt.astype(BF)
        hb, qr, qn, ps = _ca_tile_fwd(xt, gca_ref[...], wcq_ref[...], gcq_ref[...], kn_ref, v_ref)
        do = _nt(dyb, wco_ref[...])
        gcq_rows = None
        for h in range(CA_HEADS):
            hs = slice(h * CA_HD, (h + 1) * CA_HD)
            p = ps[h]
            pb = p.astype(BF)
            ob_sc[:, hs] = _nn(pb, v_ref[h]).astype(BF)
            doh = do[:, hs].astype(BF)
            dp = _nt(doh, v_ref[h])
            ds = (p * (dp - jnp.sum(dp * p, axis=1, keepdims=True))).astype(BF)
            dvh = _tn(pb, doh)
            dkh = _tn(ds, qn[h])

            @pl.when(first)
            def _():
                dv_ref[h] = dvh
                dkn_ref[h] = dkh

            @pl.when(jnp.logical_not(first))
            def _():
                dv_ref[h] += dvh
                dkn_ref[h] += dkh

            dqn = _nn(ds, kn_ref[h]) * 0.0625
            dqh, gr = _norm_bwd(dqn, qr[h], gcq_ref[...])
            gcq_rows = gr if gcq_rows is None else gcq_rows + gr
            dq_sc[:, hs] = dqh.astype(BF)
        _acc_rows(dgcq_ref, first, gcq_rows)
        dqb = dq_sc[...]
        p_o = _tn(ob_sc[...], dyb)
        p_q = _tn(hb, dqb)

        @pl.when(first)
        def _():
            ao_sc[...] = p_o
            aq_sc[...] = p_q

        @pl.when(jnp.logical_not(first))
        def _():
            ao_sc[...] += p_o
            aq_sc[...] += p_q

        @pl.when(i == n - 1)
        def _():
            dwo_ref[...] = ao_sc[...].astype(BF)
            dwq_ref[...] = aq_sc[...].astype(BF)

        dh = _nt(dqb, wcq_ref[...])
        dx, gar = _norm_bwd(dh, xt, gca_ref[...])
        dx_ref[...] = dx + dyt
        _acc_rows(dgca_ref, first, gar)

    row = lambda i: (i, 0)
    fix = lambda i: (0, 0)
    fix3 = lambda i: (0, 0, 0)
    hd = (CA_HEADS, M, CA_HD)
    return pl.pallas_call(
        body, name="ca_bwd", grid=(n,),
        in_specs=[pl.BlockSpec((tm, D), row), pl.BlockSpec((tm, D), row), pl.BlockSpec((1, D), fix),
                  pl.BlockSpec((D, D), fix), pl.BlockSpec((1, CA_HD), fix), pl.BlockSpec(hd, fix3),
                  pl.BlockSpec(hd, fix3), pl.BlockSpec((D, D), fix)],
        out_specs=[pl.BlockSpec((tm, D), row), pl.BlockSpec((D, D), fix), pl.BlockSpec((D, D), fix),
                   pl.BlockSpec(hd, fix3), pl.BlockSpec(hd, fix3), pl.BlockSpec((1, CA_HD), fix),
                   pl.BlockSpec((1, D), fix)],
        out_shape=[S((T, D), F32), S((D, D), BF), S((D, D), BF), S(hd, F32), S(hd, F32), S((1, CA_HD), F32),
                   S((1, D), F32)],
        scratch_shapes=[pltpu.VMEM((D, D), F32), pltpu.VMEM((D, D), F32), pltpu.VMEM((tm, D), BF),
                        pltpu.VMEM((tm, D), BF)],
        compiler_params=_cp(1))(x, dy, g_ca, wcq, g_cq, kn, vv, wco)


def _ca_kv_bwd(mem, g_mem, mn, kraw, dkn, dvv, wckv, g_ck):
    M, D = mem.shape

    def body(m_ref, g_ref, mn_ref, kr_ref, dkn_ref, dv_ref, w_ref, gk_ref, dw_ref, dgk_ref, dgm_ref):
        mn = mn_ref[...]
        dmn = jnp.zeros((M, D), F32)
        gk_rows = None
        for h in range(CA_HEADS):
            dkr, gr = _norm_bwd(dkn_ref[h], kr_ref[h], gk_ref[...])
            gk_rows = gr if gk_rows is None else gk_rows + gr
            dkb = dkr.astype(BF)
            dvb = dv_ref[h].astype(BF)
            dw_ref[h] = _tn(mn, dkb).astype(BF)
            dw_ref[CA_HEADS + h] = _tn(mn, dvb).astype(BF)
            dmn = dmn + _nt(dkb, w_ref[h]) + _nt(dvb, w_ref[CA_HEADS + h])
        dgk_ref[...] = jnp.sum(gk_rows, axis=0, keepdims=True)
        mf = m_ref[...]
        dgm_ref[...] = jnp.sum(dmn * (mf * _rstd(mf)), axis=0, keepdims=True)

    return pl.pallas_call(
        body, name="ca_kv_bwd",
        out_shape=[S((2 * CA_HEADS, D, CA_HD), BF), S((1, CA_HD), F32), S((1, D), F32)],
        compiler_params=pltpu.CompilerParams(vmem_limit_bytes=VMEM_LIMIT))(mem, g_mem, mn, kraw, dkn, dvv, wckv, g_ck)


def _after(g, token):
    return g if token is None else g + token[0:1, 0:1]


def _local_step(x, mem, target, small, weights, emit):
    T, D = x.shape
    p = small
    bf128 = jnp.pad(p["b_f"], ((0, 0), (0, LANES - FOX_HEADS)))
    b_st = p["b_s"].T

    wup1 = weights("ffn1_up", x)["wup1"]
    a1, h1 = _ffn_up("ffn1_up", x, p["g_ffn1"], wup1)
    wdn1 = weights("ffn1_dn", h1)["wdn1"]
    x1 = _ffn_down("ffn1_down", a1, wdn1, x)
    wm = weights("mix", x1)
    z, h2 = _mix_proj(x1, p["g_mix"], wm["wz"])
    qf, ka, va, yg = _mix_prep(z, bf128, p["g_q"], p["g_k"], p["g_sgu"], p["w_s"], b_st, p["g_gmlp_o"])
    attn, lse = _fox_fwd(qf, ka, va)
    x2 = _mix_out(attn, yg, p["g_fox_o"], wm["wout"], x1)
    wc = weights("ca", x2)
    mn, kraw, ckn, cvv = _ca_kv(mem, p["g_mem"], wc["wckv"], p["g_ck"])
    x3 = _ca_fwd(x2, p["g_ca"], wc["wcq"], p["g_cq"], ckn, cvv, wc["wco"])
    w2 = weights("ffn2", x3)
    a2, h4 = _ffn_up("ffn2_up", x3, p["g_ffn2"], w2["wup2"])
    dy4, dy4b, sq = _ffn_down_loss("ffn2_down", a2, w2["wdn2"], x3, target)

    gs = {}
    dgu2 = _ffn_bwd_act("ffn2_bwd_act", dy4b, h4, w2["wup2"], w2["wdn2"])
    tok = emit("ffn2", {"wup2": _ffn_dwup("ffn2", h4, dgu2), "wdn2": _ffn_dwdn("ffn2", a2, dy4b)})
    dx3, gs["g_ffn2"] = _ffn_dx("ffn2_dx", dgu2, w2["wup2"], x3, _after(p["g_ffn2"], tok), dy4)

    dx2, dwcq, dwco, dckn, dcvv, gs["g_cq"], gs["g_ca"] = _ca_bwd(
        x2, dx3, p["g_ca"], wc["wcq"], p["g_cq"], ckn, cvv, wc["wco"])
    dwckv, gs["g_ck"], gs["g_mem"] = _ca_kv_bwd(mem, p["g_mem"], mn, kraw, dckn, dcvv, wc["wckv"], p["g_ck"])

    qb, dob, dyg, dwout, gs["g_fox_o"] = _mix_out_bwd(dx2, attn, yg, p["g_fox_o"], wm["wout"], qf, lse)
    dq, dk, dv = _fox_bwd(qb, ka, va, dob)
    dz, gs["g_q"], gs["g_k"], gs["g_sgu"], gs["g_gmlp_o"], gs["w_s"], dbst, dbf = _mix_prep_bwd(
        z, dq, dk, dv, dyg, bf128, p["g_q"], p["g_k"], p["g_sgu"], p["w_s"], b_st, p["g_gmlp_o"])
    gs["b_s"] = dbst.T
    gs["b_f"] = dbf[:, :FOX_HEADS]
    tok_ws = emit("w_s", {"w_s": gs["w_s"]})
    tk = _tile(T, 1024)
    zb = ZW // 3
    dwz = _tn_matmul(
        "mix_dwz", dz, pl.BlockSpec((tk, zb), lambda j, k: (k, j)), h2, pl.BlockSpec((tk, D), lambda j, k: (k, 0)),
        S((ZW, D), BF), pl.BlockSpec((zb, D), lambda j, k: (j, 0)), (3, T // tk), (zb, D))
    tok = emit("mid", {"wcq": dwcq, "wco": dwco, "wckv": dwckv, "wout": dwout, "wz": dwz})
    dx1, dx1b, gs["g_mix"] = _mix_proj_bwd(dz, wm["wz"], x1, _after(_after(p["g_mix"], tok), tok_ws), dx2)

    dgu1 = _ffn_bwd_act("ffn1_bwd_act", dx1b, h1, wup1, wdn1)
    tok = emit("ffn1_dn", {"wdn1": _ffn_dwdn("ffn1", a1, dx1b)})
    tok = emit("ffn1_up", {"wup1": _ffn_dwup("ffn1", h1, dgu1, after=tok)})
    dx0, gs["g_ffn1"] = _ffn_dx("ffn1_dx", dgu1, wup1, x, _after(p["g_ffn1"], tok), dx1)
    return sq, dx0, gs


MESH = pl.DeviceIdType.MESH
HBM_SPEC = pl.BlockSpec(memory_space=pltpu.HBM)
N_PEER = N_DEV - 1


def _place():
    return lax.axis_index("x"), lax.axis_index("y"), lax.axis_index("c")


def _slot(px, py, pc):
    return 4 * px + 2 * py + pc


SEM_SPEC = pl.BlockSpec(memory_space=pltpu.SEMAPHORE)
ANY_SPEC = pl.BlockSpec(memory_space=pl.ANY)
DATAFLOW = pltpu.SideEffectType.DATAFLOW_SIDE_EFFECTING


def _hbm(a):
    return pltpu.with_memory_space_constraint(a, pltpu.HBM)


def _peer(x, y, c, r):
    return (1 - x if r & 4 else x, 1 - y if r & 2 else y, 1 - c if r & 1 else c)


def _place_own(srcs, whole):
    my = _slot(*_place())
    lands = []
    for s in srcs:
        blk = s[None] if whole else lax.dynamic_slice_in_dim(s, my, 1, 0)
        shape = (N_DEV,) + s.shape if whole else s.shape
        lands.append(lax.dynamic_update_slice_in_dim(lax.empty(shape, s.dtype), blk, my, 0))
    return lands


ALL_PEERS = tuple(range(1, N_DEV))
NEAR_PEERS = (1, 2, 4, 6)
SAME_CORE = (2, 4, 6)


def _copy_start(name, srcs, lands, whole, peers=None):
    n = len(srcs)
    peers = peers or [ALL_PEERS] * n

    def body(*refs):
        src, land = refs[:n], refs[n:2 * n]
        send, recv = refs[2 * n:3 * n], refs[3 * n:4 * n]
        token = refs[6 * n]
        x, y, c = _place()
        my = _slot(x, y, c)
        for a in range(n):
            for r in peers[a]:
                p = _peer(x, y, c, r)
                pltpu.make_async_remote_copy(
                    src_ref=src[a] if whole else src[a].at[_slot(*p)], dst_ref=land[a].at[my],
                    send_sem=send[a].at[r - 1], recv_sem=recv[a].at[r - 1], device_id=p, device_id_type=MESH).start()
        token[...] = jnp.zeros_like(token)

    out = pl.pallas_call(
        body, name=name,
        out_shape=([pltpu.SemaphoreType.DMA((N_PEER,))] * (2 * n)
                   + [pltpu.HBM(s.shape, s.dtype) for s in srcs] + [pltpu.HBM(s.shape, s.dtype) for s in lands]
                   + [S((8, LANES), F32)]),
        in_specs=[HBM_SPEC] * (2 * n),
        out_specs=[SEM_SPEC] * (2 * n) + [HBM_SPEC] * (2 * n) + [pl.BlockSpec(memory_space=pltpu.VMEM)],
        input_output_aliases={i: 2 * n + i for i in range(2 * n)},
        compiler_params=pltpu.CompilerParams(has_side_effects=DATAFLOW),
    )(*[_hbm(s) for s in srcs], *[_hbm(s) for s in lands])
    return out[:n], out[n:2 * n], out[2 * n:3 * n], out[3 * n:4 * n], out[4 * n]


def _copy_wait(name, srcs, lands, send, recv, after, whole, peers=None, with_srcs=False):
    n = len(srcs)
    peers = peers or [ALL_PEERS] * n

    def body(*refs):
        src, land = refs[:n], refs[n:2 * n]
        snd, rcv = refs[2 * n:3 * n], refs[3 * n:4 * n]
        x, y, c = _place()
        for a in range(n):
            for r in peers[a]:
                p = _peer(x, y, c, r)
                ps = _slot(*p)
                cp = pltpu.make_async_remote_copy(
                    src_ref=src[a] if whole else src[a].at[ps], dst_ref=land[a].at[ps],
                    send_sem=snd[a].at[r - 1], recv_sem=rcv[a].at[r - 1], device_id=p, device_id_type=MESH)
                cp.wait_send()
                cp.wait_recv()

    out = pl.pallas_call(
        body, name=name,
        out_shape=[pltpu.HBM(s.shape, s.dtype) for s in srcs] + [pltpu.HBM(s.shape, s.dtype) for s in lands],
        in_specs=[HBM_SPEC] * (2 * n) + [SEM_SPEC] * (2 * n) + [ANY_SPEC],
        out_specs=[HBM_SPEC] * (2 * n),
        input_output_aliases={i: i for i in range(2 * n)},
        compiler_params=pltpu.CompilerParams(has_side_effects=DATAFLOW),
    )(*srcs, *lands, *send, *recv, after)
    return (out[:n], out[n:]) if with_srcs else out[n:]


def _forward_start(name, lands):
    n = len(lands)

    def body(*refs):
        land = refs[:n]
        send, recv = refs[n:2 * n], refs[2 * n:3 * n]
        token = refs[4 * n]
        x, y, c = _place()
        for a in range(n):
            for r in SAME_CORE:
                blk = land[a].at[_slot(*_peer(x, y, c, r))]
                pltpu.make_async_remote_copy(
                    src_ref=blk, dst_ref=blk, send_sem=send[a].at[r - 1], recv_sem=recv[a].at[r - 1],
                    device_id=(x, y, 1 - c), device_id_type=MESH).start()
        token[...] = jnp.zeros_like(token)

    out = pl.pallas_call(
        body, name=name,
        out_shape=([pltpu.SemaphoreType.DMA((N_PEER,))] * (2 * n) + [pltpu.HBM(s.shape, s.dtype) for s in lands]
                   + [S((8, LANES), F32)]),
        in_specs=[HBM_SPEC] * n,
        out_specs=[SEM_SPEC] * (2 * n) + [HBM_SPEC] * n + [pl.BlockSpec(memory_space=pltpu.VMEM)],
        input_output_aliases={i: 2 * n + i for i in range(n)},
        compiler_params=pltpu.CompilerParams(has_side_effects=DATAFLOW),
    )(*[_hbm(s) for s in lands])
    return out[:n], out[n:2 * n], out[2 * n:3 * n], out[3 * n]


def _forward_wait(name, lands, send, recv, after):
    n = len(lands)

    def body(*refs):
        land = refs[:n]
        snd, rcv = refs[n:2 * n], refs[2 * n:3 * n]
        x, y, c = _place()
        for a in range(n):
            for r in SAME_CORE:
                cp = pltpu.make_async_remote_copy(
                    src_ref=land[a].at[_slot(*_peer(x, y, c, r))], dst_ref=land[a].at[_slot(*_peer(x, y, c, r | 1))],
                    send_sem=snd[a].at[r - 1], recv_sem=rcv[a].at[r - 1], device_id=(x, y, 1 - c),
                    device_id_type=MESH)
                cp.wait_send()
                cp.wait_recv()

    return pl.pallas_call(
        body, name=name,
        out_shape=[pltpu.HBM(s.shape, s.dtype) for s in lands],
        in_specs=[HBM_SPEC] * n + [SEM_SPEC] * (2 * n) + [ANY_SPEC],
        out_specs=[HBM_SPEC] * n,
        input_output_aliases={i: i for i in range(n)},
        compiler_params=pltpu.CompilerParams(has_side_effects=DATAFLOW),
    )(*lands, *send, *recv, after)


def _adamw(w, g, m, v):
    m2 = ADAM_B1 * m + (1.0 - ADAM_B1) * g
    v2 = ADAM_B2 * v + (1.0 - ADAM_B2) * (g * g)
    m_hat = m2 / (1.0 - ADAM_B1 ** ADAM_STEP)
    v_hat = v2 / (1.0 - ADAM_B2 ** ADAM_STEP)
    delta = -ADAM_LR * (m_hat / (jnp.sqrt(v_hat) + ADAM_EPS) + ADAM_WD * w)
    return delta, m2, v2


def _adamw_big(name, slots, w, m, v, own=None):
    R, C = w.shape
    tr = next((t for t in (256, 352) if R % t == 0), R)

    def finish(g, w_ref, m_ref, v_ref, g_ref, d_ref, m2_ref, v2_ref):
        d, m2, v2 = _adamw(w_ref[...], g, m_ref[...], v_ref[...])
        g_ref[...] = g
        d_ref[...] = d
        m2_ref[...] = m2
        v2_ref[...] = v2

    if own is None:
        def body(s_ref, *refs):
            g = s_ref[0].astype(F32)
            for k in range(1, N_DEV):
                g = g + s_ref[k].astype(F32)
            finish(g, *refs)

        row = pl.BlockSpec((tr, C), lambda i: (i, 0))
        return pl.pallas_call(
            body, name=name, grid=(R // tr,),
            in_specs=[pl.BlockSpec((N_DEV, tr, C), lambda i: (0, i, 0)), row, row, row],
            out_specs=[row] * 4, out_shape=[S((R, C), F32)] * 4,
            compiler_params=_cp(1))(slots, w, m, v)

    def body(my_ref, s_ref, own_ref, *refs):
        mine = own_ref[...]
        g = None
        for k in range(N_DEV):
            part = jnp.where(my_ref[0] == k, mine, s_ref[k]).astype(F32)
            g = part if g is None else g + part
        finish(g, *refs)

    row = pl.BlockSpec((tr, C), lambda i, my_ref: (i, 0))
    my = jnp.reshape(_slot(*_place()), (1,)).astype(jnp.int32)
    return pl.pallas_call(
        body, name=name,
        grid_spec=pltpu.PrefetchScalarGridSpec(
            num_scalar_prefetch=1, grid=(R // tr,),
            in_specs=[pl.BlockSpec((N_DEV, tr, C), lambda i, my_ref: (0, i, 0)),
                      pl.BlockSpec((None, tr, C), lambda i, my_ref: (my_ref[0], i, 0)), row, row, row],
            out_specs=[row] * 4),
        out_shape=[S((R, C), F32)] * 4, compiler_params=_cp(1))(my, slots, own, w, m, v)


TINY_ROWS = (("b_s", 8), ("g_ffn1", 8), ("g_mix", 8), ("g_ca", 8), ("g_mem", 8), ("g_ffn2", 8), ("g_sgu", 4),
             ("g_fox_o", 4), ("g_gmlp_o", 4), ("g_cq", 2), ("g_ck", 2), ("g_q", 1), ("g_k", 1), ("b_f", 1),
             ("loss", 1))
TINY_P = 72


def _tiny_pieces(width):
    return [(j, slice(j * LANES, min((j + 1) * LANES, width))) for j in range(-(-width // LANES))]


def _pack_tiny(grads, sq):
    names = [n for n, _ in TINY_ROWS if n != "loss"]

    def body(*refs):
        ins, sq_ref, o_ref = refs[:len(names)], refs[len(names)], refs[len(names) + 1]
        o_ref[...] = jnp.zeros_like(o_ref)
        at = 0
        for ref, (name, r) in zip(ins, TINY_ROWS):
            if name == "b_s":
                o_ref[at:at + r, :] = ref[...]
            else:
                for j, cols in _tiny_pieces(ref.shape[1]):
                    o_ref[at + j:at + j + 1, 0:cols.stop - cols.start] = ref[:, cols]
            at += r
        o_ref[at:at + 1, :] = sq_ref[0:1, :]

    return pl.pallas_call(body, name="tiny_pack", out_shape=S((TINY_P, LANES), F32))(
        *[grads[n] for n in names], sq)


def _adamw_tiny(slots, w, m, v):
    names = [n for n, _ in TINY_ROWS if n != "loss"]
    k = len(names)

    def body(s_ref, *refs):
        ins, outs, loss_ref = refs[:3 * k], refs[3 * k:7 * k], refs[7 * k]
        g_all = s_ref[0]
        for d in range(1, N_DEV):
            g_all = g_all + s_ref[d]
        at = 0
        for i, (name, r) in enumerate(TINY_ROWS[:k]):
            w_ref, m_ref, v_ref = ins[i], ins[k + i], ins[2 * k + i]
            o = outs[4 * i:4 * i + 4]
            if name == "b_s":
                pieces = [(slice(at, at + r), slice(0, LANES), (slice(None), slice(None)))]
            else:
                pieces = [(slice(at + j, at + j + 1), slice(0, c.stop - c.start), (slice(None), c))
                          for j, c in _tiny_pieces(w_ref.shape[1])]
            for rows, lanes, dst in pieces:
                g = g_all[rows, lanes]
                res = (g,) + _adamw(w_ref[dst], g, m_ref[dst], v_ref[dst])
                for ref, val in zip(o, res):
                    ref[dst] = val
            at += r
        loss_ref[...] = g_all[at:at + 1, :]

    shapes = [S(w[n].shape, F32) for n in names]
    out = pl.pallas_call(
        body, name="adamw_tiny", out_shape=[s for s in shapes for _ in range(4)] + [S((1, LANES), F32)],
    )(slots, *[w[n] for n in names], *[m[n] for n in names], *[v[n] for n in names])
    stores = ({}, {}, {}, {})
    for i, n in enumerate(names):
        for store, t in zip(stores, out[4 * i:4 * i + 4]):
            store[n] = t
    return stores, out[4 * k]


WEIGHTS =('g_ffn1', 'w_ffn1_in', 'w_ffn1_out', 'g_mix', 'w_in', 'b_f', 'g_q', 'g_k', 'g_sgu', 'w_s', 'b_s',
           'g_fox_o', 'g_gmlp_o', 'w_out', 'g_ca', 'g_mem', 'w_cq', 'w_ckv', 'g_cq', 'g_ck', 'w_co', 'g_ffn2',
           'w_ffn2_in', 'w_ffn2_out')
BIG = ('w_ffn1_in', 'w_ffn1_out', 'w_in', 'w_out', 'w_cq', 'w_ckv', 'w_co', 'w_ffn2_in', 'w_ffn2_out')
TRANSPOSED = ('w_ffn1_in', 'w_in', 'w_ffn2_in')
TWO_LEVEL = ('w_ffn1_in', 'w_in')
GATHER_GROUPS = {"ffn1_up": ("w_ffn1_in",), "ffn1_dn": ("w_ffn1_out",), "mix": ("w_in", "w_out"),
                 "ca": ("w_cq", "w_ckv", "w_co"), "ffn2": ("w_ffn2_in", "w_ffn2_out")}
QKV_W = 3 * FOX_W
UV_OFF = QKV_W + FOX_HEADS


def kernel(x, mem, g_ffn1, w_ffn1_in, w_ffn1_out, g_mix, w_in, b_f, g_q, g_k, g_sgu, w_s, b_s, g_fox_o, g_gmlp_o, w_out, g_ca, g_mem, w_cq, w_ckv, g_cq, g_ck, w_co, g_ffn2, w_ffn2_in, w_ffn2_out, loss_target, m_g_ffn1, m_w_ffn1_in, m_w_ffn1_out, m_g_mix, m_w_in, m_b_f, m_g_q, m_g_k, m_g_sgu, m_w_s, m_b_s, m_g_fox_o, m_g_gmlp_o, m_w_out, m_g_ca, m_g_mem, m_w_cq, m_w_ckv, m_g_cq, m_g_ck, m_w_co, m_g_ffn2, m_w_ffn2_in, m_w_ffn2_out, v_g_ffn1, v_w_ffn1_in, v_w_ffn1_out, v_g_mix, v_w_in, v_b_f, v_g_q, v_g_k, v_g_sgu, v_w_s, v_b_s, v_g_fox_o, v_g_gmlp_o, v_w_out, v_g_ca, v_g_mem, v_w_cq, v_w_ckv, v_g_cq, v_g_ck, v_w_co, v_g_ffn2, v_w_ffn2_in, v_w_ffn2_out):
    args = dict(locals())
    w = {n: args[n] for n in WEIGHTS}
    mo = {n: args["m_" + n] for n in WEIGHTS}
    vo = {n: args["v_" + n] for n in WEIGHTS}
    D = D_MODEL

    def local(n, a):
        return a[0].T if n in TRANSPOSED else a[0]

    shards = [local(n, w[n]).astype(BF) for n in BIG]
    fb = shards[0].shape[0]
    g_peers = [NEAR_PEERS if n in TWO_LEVEL else ALL_PEERS for n in BIG]
    g_snd, g_rcv, g_src, g_land, g_token = _copy_start("gather_start", shards, _place_own(shards, True), True,
                                                       peers=g_peers)
    handles = {n: (g_src[i], g_land[i], g_snd[i], g_rcv[i]) for i, n in enumerate(BIG)}

    tiny_names = [n for n, _ in TINY_ROWS if n != "loss"]

    def weights(group, after):
        names = GATHER_GROUPS[group]
        hs = [handles[n] for n in names]
        got = list(_copy_wait("gather_wait_" + group, [h[0] for h in hs], [h[1] for h in hs], [h[2] for h in hs],
                              [h[3] for h in hs], after, True, peers=[g_peers[BIG.index(n)] for n in names]))
        passed = [i for i, n in enumerate(names) if n in TWO_LEVEL]
        if passed:
            f_snd, f_rcv, f_land, f_token = _forward_start("gather_pass_start_" + group, [got[i] for i in passed])
            for i, t in zip(passed, _forward_wait("gather_pass_wait_" + group, f_land, f_snd, f_rcv, f_token)):
                got[i] = t
        got = dict(zip(names, got))
        if group == "ffn1_up":
            return {"wup1": got["w_ffn1_in"].reshape(2, N_FFN_BLK, fb, D)}
        if group == "ffn1_dn":
            return {"wdn1": got["w_ffn1_out"].reshape(N_FFN_BLK, fb, D)}
        if group == "mix":
            full = got["w_in"].reshape(-1, D)
            wz = jnp.concatenate([full[:QKV_W], full[UV_OFF:], full[QKV_W:UV_OFF],
                                  jnp.zeros((LANES - FOX_HEADS, D), BF)], axis=0)
            return {"wz": wz, "wout": got["w_out"].reshape(D, D)}
        if group == "ca":
            return {"wcq": got["w_cq"].reshape(D, D), "wco": got["w_co"].reshape(D, D), "wckv": got["w_ckv"]}
        return {"wup2": got["w_ffn2_in"].reshape(2, N_FFN_BLK, fb, D),
                "wdn2": got["w_ffn2_out"].reshape(N_FFN_BLK, fb, D)}

    flying = {}

    def emit(group, g):
        if group == "w_s":
            part = [g["w_s"].reshape(-1, LANES)]
            *copies, token = _copy_start("w_s_start", part, _place_own(part, True), True)
            flying[group] = copies
            return token
        if group == "ffn2":
            parts = {"w_ffn2_in": g["wup2"], "w_ffn2_out": g["wdn2"].reshape(N_DEV, -1, D)}
        elif group == "ffn1_dn":
            parts = {"w_ffn1_out": g["wdn1"].reshape(N_DEV, -1, D)}
        elif group == "ffn1_up":
            parts = {"w_ffn1_in": g["wup1"]}
        else:
            gz = g["wz"]
            g_in = jnp.concatenate([gz[:QKV_W], gz[Z_F:Z_F + FOX_HEADS], gz[QKV_W:Z_F]], axis=0)
            parts = {"w_in": g_in.reshape(N_DEV, -1, D).astype(BF),
                     "w_out": g["wout"].reshape(N_DEV, -1, D), "w_cq": g["wcq"].reshape(N_DEV, -1, D),
                     "w_co": g["wco"].reshape(N_DEV, -1, D), "w_ckv": g["wckv"]}
        names = list(parts)
        srcs = [parts[n] for n in names]
        *copies, token = _copy_start("exchange_start_" + group, srcs, [lax.empty(s.shape, s.dtype) for s in srcs],
                                     False)
        flying[group] = (names, copies)
        return token

    small = {n: (w[n][0] if n == "b_s" else w[n]) for n in tiny_names}
    small["w_s"] = w["w_s"][0]

    sq, dx0, gs = _local_step(x[0], mem[0], loss_target[0], small, weights, emit)

    sm_parts = [_pack_tiny(gs, sq)]
    sm_snd, sm_rcv, sm_src, sm_land, sm_token = _copy_start("tiny_start", sm_parts, _place_own(sm_parts, True), True)

    grad, delta, new_m, new_v = {}, {}, {}, {}

    def update(group, after):
        names, (snd, rcv, srcs, lands) = flying[group]
        owns, slots = _copy_wait("exchange_wait_" + group, srcs, lands, snd, rcv, after, False, with_srcs=True)
        for n, sl, own in zip(names, slots, owns):
            g, d, m2, v2 = _adamw_big("adamw_" + n, sl, local(n, w[n]), local(n, mo[n]), local(n, vo[n]), own=own)
            grad[n], delta[n], new_m[n], new_v[n] = (
                (t.T if n in TRANSPOSED else t).reshape(w[n].shape) for t in (g, d, m2, v2))
        return d

    last = update("ffn2", sm_token)
    last = update("mid", last)
    last = update("ffn1_dn", last)
    last = update("ffn1_up", last)
    ws_snd, ws_rcv, ws_src, ws_land = flying["w_s"]
    ws_all, = _copy_wait("w_s_wait", ws_src, ws_land, ws_snd, ws_rcv, last, True)
    tiny_all, = _copy_wait("tiny_wait", sm_src, sm_land, sm_snd, sm_rcv, ws_all, True)
    ws_shape = w["w_s"].shape
    for store, t in zip((grad, delta, new_m, new_v), _adamw_big(
            "adamw_w_s", ws_all, *[a["w_s"].reshape(-1, LANES) for a in (w, mo, vo)])):
        store["w_s"] = t.reshape(ws_shape)
    stores, loss_row = _adamw_tiny(tiny_all, *[{n: (a[n][0] if n == "b_s" else a[n]) for n in tiny_names}
                                               for a in (w, mo, vo)])
    for store, t in zip((grad, delta, new_m, new_v), stores):
        store.update({n: v.reshape(w[n].shape) for n, v in t.items()})
    loss = loss_row[0, 0] * (0.5 / D)

    return (loss, dx0[None], *[grad[n] for n in WEIGHTS], *[delta[n] for n in WEIGHTS],
            *[new_m[n] for n in WEIGHTS], *[new_v[n] for n in WEIGHTS])
```

```python
import functools

import jax
import jax.numpy as jnp
from jax import lax
from jax.experimental import pallas as pl
from jax.experimental.pallas import tpu as pltpu

F32 = jnp.float32
BF = jnp.bfloat16
S = jax.ShapeDtypeStruct

N_DEV = 8
D_MODEL = 1024
FOX_HEADS, FOX_HD = 8, 64
FOX_W = 512
GMLP_G, GMLP_GD = 8, 64
GMLP_W = 512
CHUNK = 128
CA_HEADS, CA_HD = 4, 256
N_FFN_BLK = 4
ZW = 2688
Z_Q, Z_K, Z_V, Z_U, Z_G, Z_F = 0, 512, 1024, 1536, 2048, 2560
EPS = 1e-6
NEG = -1e30
LANES = 128

ADAM_LR, ADAM_B1, ADAM_B2, ADAM_EPS, ADAM_WD, ADAM_STEP = 0.001, 0.9, 0.999, 1e-08, 0.01, 10

VMEM_LIMIT = 52 * 2 ** 20


def _cp(n_axes):
    return pltpu.CompilerParams(dimension_semantics=("arbitrary",) * n_axes, vmem_limit_bytes=VMEM_LIMIT)


def _nn(a, b):
    return jnp.dot(a, b, preferred_element_type=F32)


def _nt(a, b):
    return lax.dot_general(a, b, (((1,), (1,)), ((), ())), preferred_element_type=F32)


def _tn(a, b):
    return lax.dot_general(a, b, (((0,), (0,)), ((), ())), preferred_element_type=F32)


def _hi(a, b):
    return jnp.dot(a, b, precision=lax.Precision.HIGHEST, preferred_element_type=F32)


def _rstd(x):
    return lax.rsqrt(jnp.mean(x * x, axis=-1, keepdims=True) + EPS)


def _norm_bwd(dy, x, g):
    r = _rstd(x)
    xh = x * r
    dxh = dy * g
    dx = r * (dxh - xh * jnp.mean(dxh * xh, axis=-1, keepdims=True))
    return dx, dy * xh


def _acc_rows(ref, first, val):
    srow = jnp.sum(val, axis=0, keepdims=True)

    @pl.when(first)
    def _():
        ref[...] = srow

    @pl.when(jnp.logical_not(first))
    def _():
        ref[...] += srow


def _gelu(x):
    c = 0.7978845608028654
    return 0.5 * x * (1.0 + jnp.tanh(c * (x + 0.044715 * x * x * x)))


def _gelu_grad(x):
    c = 0.7978845608028654
    t = jnp.tanh(c * (x + 0.044715 * x * x * x))
    return 0.5 * (1.0 + t) + 0.5 * x * (1.0 - t * t) * c * (1.0 + 3 * 0.044715 * x * x)


def _tile(n, pref):
    return pref if n % pref == 0 else n


def _ffn_up(name, x, g, wup):
    T, D = x.shape
    FB = wup.shape[-2]
    tm = _tile(T, 1024)

    def body(x_ref, g_ref, w_ref, a_ref, h_ref):
        @pl.when(pl.program_id(1) == 0)
        def _():
            xf = x_ref[...]
            h_ref[...] = (xf * _rstd(xf) * g_ref[...]).astype(BF)

        hb = h_ref[...]
        gg = _nt(hb, w_ref[0])
        uu = _nt(hb, w_ref[1])
        a_ref[...] = (gg * jax.nn.sigmoid(gg) * uu).astype(BF)

    return pl.pallas_call(
        body, name=name, grid=(T // tm, N_FFN_BLK),
        in_specs=[pl.BlockSpec((tm, D), lambda i, j: (i, 0)),
                  pl.BlockSpec((1, D), lambda i, j: (0, 0)),
                  pl.BlockSpec((2, None, FB, D), lambda i, j: (0, j, 0, 0))],
        out_specs=[pl.BlockSpec((None, tm, FB), lambda i, j: (j, i, 0)),
                   pl.BlockSpec((tm, D), lambda i, j: (i, 0))],
        out_shape=[S((N_FFN_BLK, T, FB), BF), S((T, D), BF)],
        compiler_params=_cp(2))(x, g, wup)


def _ffn_down(name, a, wdn, x):
    _, T, FB = a.shape
    D = x.shape[1]
    tm = _tile(T, 512)

    def body(a_ref, w_ref, x_ref, o_ref):
        p = _nn(a_ref[0], w_ref[0])
        for j in range(1, N_FFN_BLK):
            p = p + _nn(a_ref[j], w_ref[j])
        o_ref[...] = x_ref[...] + 0.5 * p

    return pl.pallas_call(
        body, name=name, grid=(T // tm,),
        in_specs=[pl.BlockSpec((N_FFN_BLK, tm, FB), lambda i: (0, i, 0)),
                  pl.BlockSpec((N_FFN_BLK, FB, D), lambda i: (0, 0, 0)),
                  pl.BlockSpec((tm, D), lambda i: (i, 0))],
        out_specs=pl.BlockSpec((tm, D), lambda i: (i, 0)),
        out_shape=S((T, D), F32),
        compiler_params=_cp(1))(a, wdn, x)


def _ffn_down_loss(name, a, wdn, x, target):
    _, T, FB = a.shape
    D = x.shape[1]
    tm = _tile(T, 512)

    def body(a_ref, w_ref, x_ref, t_ref, d_ref, db_ref, loss_ref):
        i = pl.program_id(0)
        p = _nn(a_ref[0], w_ref[0])
        for j in range(1, N_FFN_BLK):
            p = p + _nn(a_ref[j], w_ref[j])
        diff = (x_ref[...] + 0.5 * p) - t_ref[...]
        dy = diff * (1.0 / D)
        d_ref[...] = dy
        db_ref[...] = dy.astype(BF)
        sq = jnp.zeros((8, LANES), F32) + jnp.sum(diff * diff)

        @pl.when(i == 0)
        def _():
            loss_ref[...] = sq

        @pl.when(i > 0)
        def _():
            loss_ref[...] += sq

    row = pl.BlockSpec((tm, D), lambda i: (i, 0))
    return pl.pallas_call(
        body, name=name, grid=(T // tm,),
        in_specs=[pl.BlockSpec((N_FFN_BLK, tm, FB), lambda i: (0, i, 0)),
                  pl.BlockSpec((N_FFN_BLK, FB, D), lambda i: (0, 0, 0)), row, row],
        out_specs=[row, row, pl.BlockSpec((8, LANES), lambda i: (0, 0))],
        out_shape=[S((T, D), F32), S((T, D), BF), S((8, LANES), F32)],
        compiler_params=_cp(1))(a, wdn, x, target)


def _ffn_bwd_act(name, dyb, h, wup, wdn):
    T, D = h.shape
    FB = wup.shape[-2]
    tm = _tile(T, 1024)

    def body(d_ref, h_ref, wu_ref, wd_ref, o_ref):
        da = 0.5 * _nt(d_ref[...], wd_ref[...])
        hb = h_ref[...]
        gg = _nt(hb, wu_ref[0])
        uu = _nt(hb, wu_ref[1])
        sg = jax.nn.sigmoid(gg)
        o_ref[0] = (da * uu * (sg * (1.0 + gg * (1.0 - sg)))).astype(BF)
        o_ref[1] = (da * (gg * sg)).astype(BF)

    return pl.pallas_call(
        body, name=name, grid=(T // tm, N_FFN_BLK),
        in_specs=[pl.BlockSpec((tm, D), lambda i, j: (i, 0)),
                  pl.BlockSpec((tm, D), lambda i, j: (i, 0)),
                  pl.BlockSpec((2, None, FB, D), lambda i, j: (0, j, 0, 0)),
                  pl.BlockSpec((None, FB, D), lambda i, j: (j, 0, 0))],
        out_specs=pl.BlockSpec((2, None, tm, FB), lambda i, j: (0, j, i, 0)),
        out_shape=S((2, N_FFN_BLK, T, FB), BF),
        compiler_params=_cp(2))(dyb, h, wup, wdn)


def _ffn_dx(name, dgu, wup, x, g, dy):
    T, D = x.shape
    FB = wup.shape[-2]
    tm = _tile(T, 512)

    def body(d_ref, w_ref, x_ref, g_ref, dy_ref, dx_ref, dg_ref):
        p = None
        for j in range(N_FFN_BLK):
            for half in range(2):
                t = _nn(d_ref[half, j], w_ref[half, j])
                p = t if p is None else p + t
        dx, dgr = _norm_bwd(p, x_ref[...], g_ref[...])
        dx_ref[...] = dx + dy_ref[...]
        _acc_rows(dg_ref, pl.program_id(0) == 0, dgr)

    return pl.pallas_call(
        body, name=name, grid=(T // tm,),
        in_specs=[pl.BlockSpec((2, N_FFN_BLK, tm, FB), lambda i: (0, 0, i, 0)),
                  pl.BlockSpec((2, N_FFN_BLK, FB, D), lambda i: (0, 0, 0, 0), pipeline_mode=pl.Buffered(1)),
                  pl.BlockSpec((tm, D), lambda i: (i, 0)),
                  pl.BlockSpec((1, D), lambda i: (0, 0)),
                  pl.BlockSpec((tm, D), lambda i: (i, 0))],
        out_specs=[pl.BlockSpec((tm, D), lambda i: (i, 0)),
                   pl.BlockSpec((1, D), lambda i: (0, 0))],
        out_shape=[S((T, D), F32), S((1, D), F32)],
        compiler_params=_cp(1))(dgu, wup, x, g, dy)


def _tn_matmul(name, a, a_spec, b, b_spec, out_shape, out_spec, grid, acc_shape, scale=1.0, after=None):
    nk = grid[1]
    extra = [] if after is None else [after]

    def body(a_ref, b_ref, *rest):
        o_ref, acc_ref = rest[-2:]
        k = pl.program_id(1)
        p = _tn(a_ref[...], b_ref[...])

        @pl.when(k == 0)
        def _():
            acc_ref[...] = p

        @pl.when(k > 0)
        def _():
            acc_ref[...] += p

        @pl.when(k == nk - 1)
        def _():
            o_ref[...] = (acc_ref[...] * scale).astype(o_ref.dtype)

    return pl.pallas_call(
        body, name=name, grid=grid,
        in_specs=[a_spec, b_spec] + [pl.BlockSpec((8, LANES), lambda j, k: (0, 0)) for _ in extra],
        out_specs=out_spec, out_shape=out_shape,
        scratch_shapes=[pltpu.VMEM(acc_shape, F32)], compiler_params=_cp(2))(a, b, *extra)


def _ffn_dwup(name, h, dgu, after=None):
    T, D = h.shape
    FB = dgu.shape[-1]
    tk = _tile(T, 1024)
    return _tn_matmul(
        name + "_dwup", dgu.reshape(2 * N_FFN_BLK, T, FB), pl.BlockSpec((None, tk, FB), lambda j, k: (j, k, 0)),
        h, pl.BlockSpec((tk, D), lambda j, k: (k, 0)),
        S((2 * N_FFN_BLK, FB, D), BF), pl.BlockSpec((None, FB, D), lambda j, k: (j, 0, 0)),
        (2 * N_FFN_BLK, T // tk), (FB, D), after=after)


def _ffn_dwdn(name, a, dyb):
    _, T, FB = a.shape
    D = dyb.shape[1]
    tk = _tile(T, 1024)
    return _tn_matmul(
        name + "_dwdn", a, pl.BlockSpec((None, tk, FB), lambda j, k: (j, k, 0)),
        dyb, pl.BlockSpec((tk, D), lambda j, k: (k, 0)),
        S((N_FFN_BLK, FB, D), BF), pl.BlockSpec((None, FB, D), lambda j, k: (j, 0, 0)),
        (N_FFN_BLK, T // tk), (FB, D), scale=0.5)


def _mix_proj(x, g, wz):
    T, D = x.shape
    tm = _tile(T, 512)

    def body(x_ref, g_ref, w_ref, z_ref, h_ref):
        xf = x_ref[...]
        hb = (xf * _rstd(xf) * g_ref[...]).astype(BF)
        h_ref[...] = hb
        z_ref[...] = _nt(hb, w_ref[...])

    return pl.pallas_call(
        body, name="mix_proj", grid=(T // tm,),
        in_specs=[pl.BlockSpec((tm, D), lambda i: (i, 0)),
                  pl.BlockSpec((1, D), lambda i: (0, 0)),
                  pl.BlockSpec((ZW, D), lambda i: (0, 0))],
        out_specs=[pl.BlockSpec((tm, ZW), lambda i: (i, 0)),
                   pl.BlockSpec((tm, D), lambda i: (i, 0))],
        out_shape=[S((T, ZW), F32), S((T, D), BF)],
        compiler_params=_cp(1))(x, g, wz)


def _tri(n, lower):
    r = lax.broadcasted_iota(jnp.int32, (n, n), 0)
    c = lax.broadcasted_iota(jnp.int32, (n, n), 1)
    return (r >= c) if lower else (r <= c)


def _spatial_mix(vgn_b, ws_ref, bst, tm):
    tril = _tri(CHUNK, True)
    wms = [jnp.where(tril, ws_ref[g], 0.0).astype(BF) for g in range(GMLP_G)]
    rows = []
    for c in range(tm // CHUNK):
        cols = []
        for g in range(GMLP_G):
            vs = vgn_b[c * CHUNK:(c + 1) * CHUNK, g * GMLP_GD:(g + 1) * GMLP_GD]
            cols.append(_nn(wms[g], vs) + bst[:, g:g + 1])
        rows.append(jnp.concatenate(cols, axis=1))
    return jnp.concatenate(rows, axis=0), wms


HB = 128
AUG_W = FOX_HEADS * HB
COL_A, COL_B, COL_C = 64, 67, 70


def _spread_matrix():
    r = jnp.arange(FOX_W)
    return (jnp.arange(AUG_W)[None, :] == ((r // FOX_HD) * HB + r % FOX_HD)[:, None]).astype(BF)


def _piece_matrix(col):
    r = jnp.arange(LANES)
    dst = jnp.where(r < 3 * FOX_HEADS, (r % FOX_HEADS) * HB + col + r // FOX_HEADS, -1)
    return (jnp.arange(AUG_W)[None, :] == dst[:, None]).astype(BF)


def _ones_row(cols):
    c = jnp.arange(AUG_W) % HB
    hit = functools.reduce(jnp.logical_or, [(c >= a) & (c < a + 3) for a in cols])
    return hit.astype(F32)[None, :]


def _pieces(x):
    lane = lax.broadcasted_iota(jnp.int32, x.shape, 1)
    x = jnp.where(lane < FOX_HEADS, x, 0.0)
    hi = x.astype(BF).astype(F32)
    r1 = x - hi
    mid = r1.astype(BF).astype(F32)
    lo = (r1 - mid).astype(BF).astype(F32)
    return (hi + pltpu.roll(mid, FOX_HEADS, 1) + pltpu.roll(lo, 2 * FOX_HEADS, 1)).astype(BF)


def _mix_prep(z, bf128, g_q, g_k, g_sgu, w_s, b_st, g_go):
    T = z.shape[0]
    tm = _tile(T, 512)
    spread, pc_q, pc_k = _spread_matrix(), _piece_matrix(COL_A), _piece_matrix(COL_B)
    one_q, one_k, one_v = _ones_row([COL_B]), _ones_row([COL_A, COL_C]), _ones_row([COL_A])

    def body(z_ref, bf_ref, gq_ref, gk_ref, gs_ref, ws_ref, bst_ref, go_ref, sp_ref, pq_ref, pk_ref, oq_ref, ok_ref,
             ov_ref, q_ref, k_ref, v_ref, y_ref, carry_ref, qn_sc, kn_sc):
        i = pl.program_id(0)

        @pl.when(i == 0)
        def _():
            carry_ref[...] = jnp.zeros_like(carry_ref)

        for h in range(FOX_HEADS):
            hs = slice(h * FOX_HD, (h + 1) * FOX_HD)
            qh = z_ref[:, Z_Q + h * FOX_HD:Z_Q + (h + 1) * FOX_HD]
            kh = z_ref[:, Z_K + h * FOX_HD:Z_K + (h + 1) * FOX_HD]
            qn_sc[:, hs] = (qh * _rstd(qh) * gq_ref[...] * 0.125).astype(BF)
            kn_sc[:, hs] = (kh * _rstd(kh) * gk_ref[...]).astype(BF)

        fl = z_ref[:, Z_F:Z_F + LANES] + bf_ref[...]
        logf = jnp.minimum(fl, 0.0) - jnp.log1p(jnp.exp(-jnp.abs(fl)))
        csum = _hi(_tri(tm, True).astype(F32), logf) + carry_ref[...]
        carry_ref[...] = csum[tm - 1:tm, :]
        sp = sp_ref[...]
        q_ref[...] = (_nn(qn_sc[...], sp) + _nn(_pieces(csum), pq_ref[...]) + oq_ref[...]).astype(BF)
        k_ref[...] = (_nn(kn_sc[...], sp) + _nn(_pieces(-csum), pk_ref[...]) + ok_ref[...]).astype(BF)
        v_ref[...] = (_nn(z_ref[:, Z_V:Z_V + FOX_W].astype(BF), sp) + ov_ref[...]).astype(BF)

        u = _gelu(z_ref[:, Z_U:Z_U + GMLP_W])
        vg = _gelu(z_ref[:, Z_G:Z_G + GMLP_W])
        vgn = (vg * _rstd(vg) * gs_ref[...]).astype(BF)
        mixed, _ = _spatial_mix(vgn, ws_ref, bst_ref[...], tm)
        sgu = u * mixed
        y_ref[...] = (sgu * _rstd(sgu) * go_ref[...]).astype(BF)

    row = lambda i: (i, 0)
    fix2 = lambda i: (0, 0)
    return pl.pallas_call(
        body, name="mix_prep", grid=(T // tm,),
        in_specs=[pl.BlockSpec((tm, ZW), row),
                  pl.BlockSpec((1, LANES), fix2), pl.BlockSpec((1, FOX_HD), fix2), pl.BlockSpec((1, FOX_HD), fix2),
                  pl.BlockSpec((1, GMLP_W), fix2), pl.BlockSpec((GMLP_G, CHUNK, CHUNK), lambda i: (0, 0, 0)),
                  pl.BlockSpec((CHUNK, GMLP_G), fix2), pl.BlockSpec((1, GMLP_W), fix2),
                  pl.BlockSpec((FOX_W, AUG_W), fix2), pl.BlockSpec((LANES, AUG_W), fix2),
                  pl.BlockSpec((LANES, AUG_W), fix2), pl.BlockSpec((1, AUG_W), fix2), pl.BlockSpec((1, AUG_W), fix2),
                  pl.BlockSpec((1, AUG_W), fix2)],
        out_specs=[pl.BlockSpec((tm, AUG_W), row), pl.BlockSpec((tm, AUG_W), row), pl.BlockSpec((tm, AUG_W), row),
                   pl.BlockSpec((tm, GMLP_W), row)],
        out_shape=[S((T, AUG_W), BF), S((T, AUG_W), BF), S((T, AUG_W), BF), S((T, GMLP_W), BF)],
        scratch_shapes=[pltpu.VMEM((1, LANES), F32), pltpu.VMEM((tm, FOX_W), BF), pltpu.VMEM((tm, FOX_W), BF)],
        compiler_params=_cp(1))(z, bf128, g_q, g_k, g_sgu, w_s, b_st, g_go, spread, pc_q, pc_k, one_q, one_k, one_v)


def _fox_fwd(q, k, v):
    T = q.shape[0]
    tq = _tile(T, 1024)
    nq = T // tq

    def body(q_ref, k_ref, v_ref, o_ref, lse_ref, m_sc, acc_sc):
        i, j = pl.program_id(0), pl.program_id(1)

        @pl.when(j == 0)
        def _():
            m_sc[...] = jnp.full(m_sc.shape, NEG, F32)
            acc_sc[...] = jnp.zeros_like(acc_sc)

        def step(masked):
            mask = _tri(tq, True) if masked else None
            for h in range(FOX_HEADS):
                hb = slice(h * HB, (h + 1) * HB)
                s = _nt(q_ref[:, hb], k_ref[:, hb])
                if masked:
                    s = jnp.where(mask, s, NEG)
                m_prev = m_sc[h]
                m_new = jnp.maximum(m_prev, jnp.broadcast_to(jnp.max(s, axis=1, keepdims=True), (tq, HB)))
                p = jnp.exp(s - jnp.tile(m_new, (1, tq // HB))).astype(BF)
                acc_sc[:, hb] = jnp.exp(m_prev - m_new) * acc_sc[:, hb] + _nn(p, v_ref[:, hb])
                m_sc[h] = m_new

        @pl.when(j < i)
        def _():
            step(False)

        @pl.when(j == i)
        def _():
            step(True)
            lse_ref[...] = jnp.zeros_like(lse_ref)
            for h in range(FOX_HEADS):
                l = acc_sc[:, h * HB + COL_A:h * HB + COL_A + 1]
                o_ref[:, h * FOX_HD:(h + 1) * FOX_HD] = acc_sc[:, h * HB:h * HB + FOX_HD] / l
                lse_ref[:, h:h + 1] = m_sc[h][:, 0:1] + jnp.log(l)

    qi = lambda i, j: (i, 0)
    kj = lambda i, j: (jnp.minimum(i, j), 0)
    return pl.pallas_call(
        body, name="fox_fwd", grid=(nq, nq),
        in_specs=[pl.BlockSpec((tq, AUG_W), qi), pl.BlockSpec((tq, AUG_W), kj), pl.BlockSpec((tq, AUG_W), kj)],
        out_specs=[pl.BlockSpec((tq, FOX_W), qi), pl.BlockSpec((tq, LANES), qi)],
        out_shape=[S((T, FOX_W), F32), S((T, LANES), F32)],
        scratch_shapes=[pltpu.VMEM((FOX_HEADS, tq, HB), F32), pltpu.VMEM((tq, AUG_W), F32)],
        compiler_params=_cp(2))(q, k, v)


def _fox_bwd(q, k, v, dob):
    T = q.shape[0]
    tq = _tile(T, 512)
    nq = T // tq
    half = AUG_W // 2
    hpg = FOX_HEADS // 2

    pairs = [(j, i) for j in range(nq) for i in range(j, nq)]
    jt = jnp.asarray([p[0] for p in pairs], jnp.int32)
    it = jnp.asarray([p[1] for p in pairs], jnp.int32)

    def body(jt_ref, it_ref, q_ref, k_ref, v_ref, do_ref, dq_ref, dk_ref, dv_ref, dq_sc):
        t = pl.program_id(1)
        j, i = jt_ref[t], it_ref[t]

        @pl.when(t == 0)
        def _():
            dq_sc[...] = jnp.zeros_like(dq_sc)

        @pl.when(i == j)
        def _():
            dk_ref[...] = jnp.zeros_like(dk_ref)
            dv_ref[...] = jnp.zeros_like(dv_ref)

        def step(masked):
            rows = pl.ds(pl.multiple_of(i * tq, tq), tq)
            mask = _tri(tq, True) if masked else None
            for h in range(hpg):
                hb = slice(h * HB, (h + 1) * HB)
                qh, kh, vh, doh = q_ref[:, hb], k_ref[:, hb], v_ref[:, hb], do_ref[:, hb]
                s = _nt(qh, kh)
                if masked:
                    s = jnp.where(mask, s, NEG)
                p = jnp.exp(s)
                dsb = (p * _nt(doh, vh)).astype(BF)
                dv_ref[:, hb] += _tn(p.astype(BF), doh)
                dk_ref[:, hb] += _tn(dsb, qh)
                dq_sc[rows, hb] += _nn(dsb, kh)

        @pl.when(i > j)
        def _():
            step(False)

        @pl.when(i == j)
        def _():
            step(True)
            dq_ref[...] = dq_sc[pl.ds(pl.multiple_of(j * tq, tq), tq), :]

    qi = pl.BlockSpec((tq, half), lambda g, t, jt_ref, it_ref: (it_ref[t], g))
    kj = pl.BlockSpec((tq, half), lambda g, t, jt_ref, it_ref: (jt_ref[t], g))
    return pl.pallas_call(
        body, name="fox_bwd",
        grid_spec=pltpu.PrefetchScalarGridSpec(
            num_scalar_prefetch=2, grid=(2, len(pairs)), in_specs=[qi, kj, kj, qi], out_specs=[kj, kj, kj],
            scratch_shapes=[pltpu.VMEM((T, half), F32)]),
        out_shape=[S((T, AUG_W), F32), S((T, AUG_W), F32), S((T, AUG_W), F32)],
        compiler_params=_cp(2))(jt, it, q, k, v, dob)


def _mix_out(attn, yg, g_fo, wout, x):
    T, D = x.shape
    tm = _tile(T, 1024)

    def body(a_ref, y_ref, g_ref, w_ref, x_ref, o_ref):
        at = a_ref[...]
        yf = (at * _rstd(at) * g_ref[...]).astype(BF)
        o_ref[...] = x_ref[...] + _nn(yf, w_ref[:FOX_W, :]) + _nn(y_ref[...], w_ref[FOX_W:, :])

    row = lambda i: (i, 0)
    return pl.pallas_call(
        body, name="mix_out", grid=(T // tm,),
        in_specs=[pl.BlockSpec((tm, FOX_W), row), pl.BlockSpec((tm, GMLP_W), row),
                  pl.BlockSpec((1, FOX_W), lambda i: (0, 0)), pl.BlockSpec((D, D), lambda i: (0, 0)),
                  pl.BlockSpec((tm, D), row)],
        out_specs=pl.BlockSpec((tm, D), row),
        out_shape=S((T, D), F32),
        compiler_params=_cp(1))(attn, yg, g_fo, wout, x)


def _mix_out_bwd(dx, attn, yg, g_fo, wout, qf, lse):
    T, D = dx.shape
    tm = _tile(T, 512)
    n = T // tm
    spread, pc_l, pc_d = _spread_matrix(), _piece_matrix(COL_C), _piece_matrix(COL_A)

    def body(dx_ref, a_ref, y_ref, g_ref, w_ref, qf_ref, lse_ref, sp_ref, pl_ref, pd_ref,
             qb_ref, dob_ref, dyg_ref, dw_ref, dg_ref, acc_ref, dsum_ref):
        i = pl.program_id(0)
        dxb = dx_ref[...].astype(BF)
        at = a_ref[...]
        yf = (at * _rstd(at) * g_ref[...]).astype(BF)
        dy = _nt(dxb, w_ref[...])
        p_top = _tn(yf, dxb)
        p_bot = _tn(y_ref[...], dxb)

        @pl.when(i == 0)
        def _():
            acc_ref[:FOX_W, :] = p_top
            acc_ref[FOX_W:, :] = p_bot

        @pl.when(i > 0)
        def _():
            acc_ref[:FOX_W, :] += p_top
            acc_ref[FOX_W:, :] += p_bot

        @pl.when(i == n - 1)
        def _():
            dw_ref[...] = acc_ref[...].astype(BF)

        dat, dgr = _norm_bwd(dy[:, :FOX_W], at, g_ref[...])
        _acc_rows(dg_ref, i == 0, dgr)
        dyg_ref[...] = dy[:, FOX_W:]
        prod = dat * at
        dsum_ref[...] = jnp.zeros_like(dsum_ref)
        for h in range(FOX_HEADS):
            dsum_ref[:, h:h + 1] = jnp.sum(prod[:, h * FOX_HD:(h + 1) * FOX_HD], axis=1, keepdims=True)
        dob_ref[...] = (_nn(dat.astype(BF), sp_ref[...]) + _nn(_pieces(-dsum_ref[...]), pd_ref[...])).astype(BF)
        qb_ref[...] = (qf_ref[...].astype(F32) + _nn(_pieces(-lse_ref[...]), pl_ref[...])).astype(BF)

    row = lambda i: (i, 0)
    fix = lambda i: (0, 0)
    return pl.pallas_call(
        body, name="mix_out_bwd", grid=(n,),
        in_specs=[pl.BlockSpec((tm, D), row), pl.BlockSpec((tm, FOX_W), row), pl.BlockSpec((tm, GMLP_W), row),
                  pl.BlockSpec((1, FOX_W), fix), pl.BlockSpec((D, D), fix), pl.BlockSpec((tm, AUG_W), row),
                  pl.BlockSpec((tm, LANES), row), pl.BlockSpec((FOX_W, AUG_W), fix), pl.BlockSpec((LANES, AUG_W), fix),
                  pl.BlockSpec((LANES, AUG_W), fix)],
        out_specs=[pl.BlockSpec((tm, AUG_W), row), pl.BlockSpec((tm, AUG_W), row), pl.BlockSpec((tm, GMLP_W), row),
                   pl.BlockSpec((D, D), fix), pl.BlockSpec((1, FOX_W), fix)],
        out_shape=[S((T, AUG_W), BF), S((T, AUG_W), BF), S((T, GMLP_W), F32), S((D, D), BF), S((1, FOX_W), F32)],
        scratch_shapes=[pltpu.VMEM((D, D), F32), pltpu.VMEM((tm, LANES), F32)],
        compiler_params=_cp(1))(dx, attn, yg, g_fo, wout, qf, lse, spread, pc_l, pc_d)


def _mix_prep_bwd(z, dq, dk, dv, dyg, bf128, g_q, g_k, g_sgu, w_s, b_st, g_go):
    T = z.shape[0]
    tm = _tile(T, 512)
    n = T // tm

    def body(z_ref, dq_ref, dk_ref, dv_ref, dyg_ref, bf_ref, gq_ref, gk_ref, gs_ref, ws_ref,
             bst_ref, go_ref, dz_ref, dgq_ref, dgk_ref, dgs_ref, dgo_ref, dws_ref, dbst_ref, dbf_ref, carry_ref):
        i = pl.program_id(0)
        first = i == 0

        @pl.when(first)
        def _():
            carry_ref[...] = jnp.zeros_like(carry_ref)

        lane = lax.broadcasted_iota(jnp.int32, (tm, LANES), 1)
        dc = jnp.zeros((tm, LANES), F32)
        gq_rows, gk_rows = [], []
        for h in range(FOX_HEADS):
            hp = slice(h * HB, h * HB + FOX_HD)
            dqh, gqr = _norm_bwd(dq_ref[:, hp] * 0.125, z_ref[:, Z_Q + h * FOX_HD:Z_Q + (h + 1) * FOX_HD], gq_ref[...])
            dkh, gkr = _norm_bwd(dk_ref[:, hp], z_ref[:, Z_K + h * FOX_HD:Z_K + (h + 1) * FOX_HD], gk_ref[...])
            dz_ref[:, Z_Q + h * FOX_HD:Z_Q + (h + 1) * FOX_HD] = dqh.astype(BF)
            dz_ref[:, Z_K + h * FOX_HD:Z_K + (h + 1) * FOX_HD] = dkh.astype(BF)
            dz_ref[:, Z_V + h * FOX_HD:Z_V + (h + 1) * FOX_HD] = dv_ref[:, hp].astype(BF)
            dch = dq_ref[:, h * HB + COL_A:h * HB + COL_A + 1] - dk_ref[:, h * HB + COL_B:h * HB + COL_B + 1]
            dc = jnp.where(lane == h, dch, dc)
            gq_rows.append(gqr)
            gk_rows.append(gkr)
        _acc_rows(dgq_ref, first, functools.reduce(lambda a, b: a + b, gq_rows))
        _acc_rows(dgk_ref, first, functools.reduce(lambda a, b: a + b, gk_rows))

        dlogf = _hi(_tri(tm, False).astype(F32), dc) + carry_ref[...]
        carry_ref[...] = dlogf[0:1, :]
        fl = z_ref[:, Z_F:Z_F + LANES] + bf_ref[...]
        lane = lax.broadcasted_iota(jnp.int32, (tm, LANES), 1)
        df = jnp.where(lane < FOX_HEADS, dlogf * jax.nn.sigmoid(-fl), 0.0)
        dz_ref[:, Z_F:Z_F + LANES] = df.astype(BF)
        _acc_rows(dbf_ref, first, df)

        u_pre = z_ref[:, Z_U:Z_U + GMLP_W]
        vg_pre = z_ref[:, Z_G:Z_G + GMLP_W]
        u = _gelu(u_pre)
        vg = _gelu(vg_pre)
        vgn = (vg * _rstd(vg) * gs_ref[...]).astype(BF)
        bst = bst_ref[...]
        mixed, wms = _spatial_mix(vgn, ws_ref, bst, tm)
        sgu = u * mixed
        dsgu, gor = _norm_bwd(dyg_ref[...], sgu, go_ref[...])
        _acc_rows(dgo_ref, first, gor)
        du = dsgu * mixed
        dmixed = dsgu * u
        dmb = dmixed.astype(BF)
        tril = _tri(CHUNK, True)
        dvgn_rows = []
        dws = [None] * GMLP_G
        dbs = [None] * GMLP_G
        for c in range(tm // CHUNK):
            cs = slice(c * CHUNK, (c + 1) * CHUNK)
            cols = []
            for g in range(GMLP_G):
                gs = slice(g * GMLP_GD, (g + 1) * GMLP_GD)
                dmc = dmb[cs, gs]
                pw = _nt(dmc, vgn[cs, gs])
                pb = jnp.sum(dmixed[cs, gs], axis=1, keepdims=True)
                dws[g] = pw if dws[g] is None else dws[g] + pw
                dbs[g] = pb if dbs[g] is None else dbs[g] + pb
                cols.append(_tn(wms[g], dmc))
            dvgn_rows.append(jnp.concatenate(cols, axis=1))
        dvgn = jnp.concatenate(dvgn_rows, axis=0)
        dbs_t = jnp.concatenate(dbs, axis=1)
        for g in range(GMLP_G):
            dwg = jnp.where(tril, dws[g], 0.0)

            @pl.when(first)
            def _():
                dws_ref[g] = dwg

            @pl.when(jnp.logical_not(first))
            def _():
                dws_ref[g] += dwg

        @pl.when(first)
        def _():
            dbst_ref[...] = dbs_t

        @pl.when(jnp.logical_not(first))
        def _():
            dbst_ref[...] += dbs_t

        dvg, gsr = _norm_bwd(dvgn, vg, gs_ref[...])
        _acc_rows(dgs_ref, first, gsr)
        dz_ref[:, Z_U:Z_U + GMLP_W] = (du * _gelu_grad(u_pre)).astype(BF)
        dz_ref[:, Z_G:Z_G + GMLP_W] = (dvg * _gelu_grad(vg_pre)).astype(BF)

    rev = lambda i: (n - 1 - i, 0)
    fix = lambda i: (0, 0)
    fix3 = lambda i: (0, 0, 0)
    return pl.pallas_call(
        body, name="mix_prep_bwd", grid=(n,),
        in_specs=[pl.BlockSpec((tm, ZW), rev), pl.BlockSpec((tm, AUG_W), rev), pl.BlockSpec((tm, AUG_W), rev),
                  pl.BlockSpec((tm, AUG_W), rev), pl.BlockSpec((tm, GMLP_W), rev),
                  pl.BlockSpec((1, LANES), fix), pl.BlockSpec((1, FOX_HD), fix), pl.BlockSpec((1, FOX_HD), fix),
                  pl.BlockSpec((1, GMLP_W), fix), pl.BlockSpec((GMLP_G, CHUNK, CHUNK), fix3),
                  pl.BlockSpec((CHUNK, GMLP_G), fix), pl.BlockSpec((1, GMLP_W), fix)],
        out_specs=[pl.BlockSpec((tm, ZW), rev), pl.BlockSpec((1, FOX_HD), fix), pl.BlockSpec((1, FOX_HD), fix),
                   pl.BlockSpec((1, GMLP_W), fix), pl.BlockSpec((1, GMLP_W), fix),
                   pl.BlockSpec((GMLP_G, CHUNK, CHUNK), fix3), pl.BlockSpec((CHUNK, GMLP_G), fix),
                   pl.BlockSpec((1, LANES), fix)],
        out_shape=[S((T, ZW), BF), S((1, FOX_HD), F32), S((1, FOX_HD), F32), S((1, GMLP_W), F32), S((1, GMLP_W), F32),
                   S((GMLP_G, CHUNK, CHUNK), F32), S((CHUNK, GMLP_G), F32), S((1, LANES), F32)],
        scratch_shapes=[pltpu.VMEM((1, LANES), F32)],
        compiler_params=_cp(1))(z, dq, dk, dv, dyg, bf128, g_q, g_k, g_sgu, w_s, b_st, g_go)


def _mix_proj_bwd(dz, wz, x, g, dy):
    T, D = x.shape
    tm = _tile(T, 512)

    def body(dz_ref, w_ref, x_ref, g_ref, dy_ref, dx_ref, dxb_ref, dg_ref):
        dh = _nn(dz_ref[...], w_ref[...])
        dx, dgr = _norm_bwd(dh, x_ref[...], g_ref[...])
        dx = dx + dy_ref[...]
        dx_ref[...] = dx
        dxb_ref[...] = dx.astype(BF)
        _acc_rows(dg_ref, pl.program_id(0) == 0, dgr)

    row = lambda i: (i, 0)
    fix = lambda i: (0, 0)
    return pl.pallas_call(
        body, name="mix_proj_bwd", grid=(T // tm,),
        in_specs=[pl.BlockSpec((tm, ZW), row), pl.BlockSpec((ZW, D), fix), pl.BlockSpec((tm, D), row),
                  pl.BlockSpec((1, D), fix), pl.BlockSpec((tm, D), row)],
        out_specs=[pl.BlockSpec((tm, D), row), pl.BlockSpec((tm, D), row), pl.BlockSpec((1, D), fix)],
        out_shape=[S((T, D), F32), S((T, D), BF), S((1, D), F32)],
        compiler_params=_cp(1))(dz, wz, x, g, dy)


def _ca_kv(mem, g_mem, wckv, g_ck):
    M, D = mem.shape

    def body(m_ref, g_ref, w_ref, gk_ref, mn_ref, kr_ref, kn_ref, v_ref):
        mf = m_ref[...]
        mn = (mf * _rstd(mf) * g_ref[...]).astype(BF)
        mn_ref[...] = mn
        for h in range(CA_HEADS):
            kr = _nn(mn, w_ref[h])
            kr_ref[h] = kr
            kn_ref[h] = (kr * _rstd(kr) * gk_ref[...]).astype(BF)
            v_ref[h] = _nn(mn, w_ref[CA_HEADS + h]).astype(BF)

    hd = (CA_HEADS, M, CA_HD)
    return pl.pallas_call(
        body, name="ca_kv", out_shape=[S((M, D), BF), S(hd, F32), S(hd, BF), S(hd, BF)],
        compiler_params=pltpu.CompilerParams(vmem_limit_bytes=VMEM_LIMIT))(mem, g_mem, wckv, g_ck)


def _ca_tile_fwd(xt, gca, wcq, gcq, kn_ref, v_ref):
    hb = (xt * _rstd(xt) * gca).astype(BF)
    qc = _nn(hb, wcq)
    qr, qn, ps = [], [], []
    for h in range(CA_HEADS):
        qh = qc[:, h * CA_HD:(h + 1) * CA_HD]
        qnh = (qh * _rstd(qh) * gcq * 0.0625).astype(BF)
        s = _nt(qnh, kn_ref[h])
        e = jnp.exp(s - jnp.max(s, axis=1, keepdims=True))
        ps.append(e / jnp.sum(e, axis=1, keepdims=True))
        qr.append(qh)
        qn.append(qnh)
    return hb, qr, qn, ps


def _ca_fwd(x, g_ca, wcq, g_cq, kn, vv, wco):
    T, D = x.shape
    M = kn.shape[1]
    tm = _tile(T, 1024)

    def body(x_ref, gca_ref, wcq_ref, gcq_ref, kn_ref, v_ref, wco_ref, o_ref, ob_sc):
        xt = x_ref[...]
        _, _, _, ps = _ca_tile_fwd(xt, gca_ref[...], wcq_ref[...], gcq_ref[...], kn_ref, v_ref)
        for h in range(CA_HEADS):
            ob_sc[:, h * CA_HD:(h + 1) * CA_HD] = _nn(ps[h].astype(BF), v_ref[h]).astype(BF)
        o_ref[...] = xt + _nn(ob_sc[...], wco_ref[...])

    row = lambda i: (i, 0)
    fix = lambda i: (0, 0)
    fix3 = lambda i: (0, 0, 0)
    return pl.pallas_call(
        body, name="ca_fwd", grid=(T // tm,),
        in_specs=[pl.BlockSpec((tm, D), row), pl.BlockSpec((1, D), fix), pl.BlockSpec((D, D), fix),
                  pl.BlockSpec((1, CA_HD), fix), pl.BlockSpec((CA_HEADS, M, CA_HD), fix3),
                  pl.BlockSpec((CA_HEADS, M, CA_HD), fix3), pl.BlockSpec((D, D), fix)],
        out_specs=pl.BlockSpec((tm, D), row), out_shape=S((T, D), F32),
        scratch_shapes=[pltpu.VMEM((tm, D), BF)],
        compiler_params=_cp(1))(x, g_ca, wcq, g_cq, kn, vv, wco)


def _ca_bwd(x, dy, g_ca, wcq, g_cq, kn, vv, wco):
    T, D = x.shape
    M = kn.shape[1]
    tm = _tile(T, 512)
    n = T // tm

    def body(x_ref, dy_ref, gca_ref, wcq_ref, gcq_ref, kn_ref, v_ref, wco_ref,
             dx_ref, dwq_ref, dwo_ref, dkn_ref, dv_ref, dgcq_ref, dgca_ref, aq_sc, ao_sc, ob_sc, dq_sc):
        i = pl.program_id(0)
        first = i == 0
        xt = x_ref[...]
        dyt = dy_ref[...]
        dyb = dyt.astype(BF)
        hb, qr, qn, ps = _ca_tile_fwd(xt, gca_ref[...], wcq_ref[...], gcq_ref[...], kn_ref, v_ref)
        do = _nt(dyb, wco_ref[...])
        gcq_rows = None
        for h in range(CA_HEADS):
            hs = slice(h * CA_HD, (h + 1) * CA_HD)
            p = ps[h]
            pb = p.astype(BF)
            ob_sc[:, hs] = _nn(pb, v_ref[h]).astype(BF)
            doh = do[:, hs].astype(BF)
            dp = _nt(doh, v_ref[h])
            ds = (p * (dp - jnp.sum(dp * p, axis=1, keepdims=True))).astype(BF)
            dvh = _tn(pb, doh)
            dkh = _tn(ds, qn[h])

            @pl.when(first)
            def _():
                dv_ref[h] = dvh
                dkn_ref[h] = dkh

            @pl.when(jnp.logical_not(first))
            def _():
                dv_ref[h] += dvh
                dkn_ref[h] += dkh

            dqn = _nn(ds, kn_ref[h]) * 0.0625
            dqh, gr = _norm_bwd(dqn, qr[h], gcq_ref[...])
            gcq_rows = gr if gcq_rows is None else gcq_rows + gr
            dq_sc[:, hs] = dqh.astype(BF)
        _acc_rows(dgcq_ref, first, gcq_rows)
        dqb = dq_sc[...]
        p_o = _tn(ob_sc[...], dyb)
        p_q = _tn(hb, dqb)

        @pl.when(first)
        def _():
            ao_sc[...] = p_o
            aq_sc[...] = p_q

        @pl.when(jnp.logical_not(first))
        def _():
            ao_sc[...] += p_o
            aq_sc[...] += p_q

        @pl.when(i == n - 1)
        def _():
            dwo_ref[...] = ao_sc[...].astype(BF)
            dwq_ref[...] = aq_sc[...].astype(BF)

        dh = _nt(dqb, wcq_ref[...])
        dx, gar = _norm_bwd(dh, xt, gca_ref[...])
        dx_ref[...] = dx + dyt
        _acc_rows(dgca_ref, first, gar)

    row = lambda i: (i, 0)
    fix = lambda i: (0, 0)
    fix3 = lambda i: (0, 0, 0)
    hd = (CA_HEADS, M, CA_HD)
    return pl.pallas_call(
        body, name="ca_bwd", grid=(n,),
        in_specs=[pl.BlockSpec((tm, D), row), pl.BlockSpec((tm, D), row), pl.BlockSpec((1, D), fix),
                  pl.BlockSpec((D, D), fix), pl.BlockSpec((1, CA_HD), fix), pl.BlockSpec(hd, fix3),
                  pl.BlockSpec(hd, fix3), pl.BlockSpec((D, D), fix)],
        out_specs=[pl.BlockSpec((tm, D), row), pl.BlockSpec((D, D), fix), pl.BlockSpec((D, D), fix),
                   pl.BlockSpec(hd, fix3), pl.BlockSpec(hd, fix3), pl.BlockSpec((1, CA_HD), fix),
                   pl.BlockSpec((1, D), fix)],
        out_shape=[S((T, D), F32), S((D, D), BF), S((D, D), BF), S(hd, F32), S(hd, F32), S((1, CA_HD), F32),
                   S((1, D), F32)],
        scratch_shapes=[pltpu.VMEM((D, D), F32), pltpu.VMEM((D, D), F32), pltpu.VMEM((tm, D), BF),
                        pltpu.VMEM((tm, D), BF)],
        compiler_params=_cp(1))(x, dy, g_ca, wcq, g_cq, kn, vv, wco)


def _ca_kv_bwd(mem, g_mem, mn, kraw, dkn, dvv, wckv, g_ck):
    M, D = mem.shape

    def body(m_ref, g_ref, mn_ref, kr_ref, dkn_ref, dv_ref, w_ref, gk_ref, dw_ref, dgk_ref, dgm_ref):
        mn = mn_ref[...]
        dmn = jnp.zeros((M, D), F32)
        gk_rows = None
        for h in range(CA_HEADS):
            dkr, gr = _norm_bwd(dkn_ref[h], kr_ref[h], gk_ref[...])
            gk_rows = gr if gk_rows is None else gk_rows + gr
            dkb = dkr.astype(BF)
            dvb = dv_ref[h].astype(BF)
            dw_ref[h] = _tn(mn, dkb).astype(BF)
            dw_ref[CA_HEADS + h] = _tn(mn, dvb).astype(BF)
            dmn = dmn + _nt(dkb, w_ref[h]) + _nt(dvb, w_ref[CA_HEADS + h])
        dgk_ref[...] = jnp.sum(gk_rows, axis=0, keepdims=True)
        mf = m_ref[...]
        dgm_ref[...] = jnp.sum(dmn * (mf * _rstd(mf)), axis=0, keepdims=True)

    return pl.pallas_call(
        body, name="ca_kv_bwd",
        out_shape=[S((2 * CA_HEADS, D, CA_HD), BF), S((1, CA_HD), F32), S((1, D), F32)],
        compiler_params=pltpu.CompilerParams(vmem_limit_bytes=VMEM_LIMIT))(mem, g_mem, mn, kraw, dkn, dvv, wckv, g_ck)


def _after(g, token):
    return g if token is None else g + token[0:1, 0:1]


def _local_step(x, mem, target, small, weights, emit):
    T, D = x.shape
    p = small
    bf128 = jnp.pad(p["b_f"], ((0, 0), (0, LANES - FOX_HEADS)))
    b_st = p["b_s"].T

    wup1 = weights("ffn1_up", x)["wup1"]
    a1, h1 = _ffn_up("ffn1_up", x, p["g_ffn1"], wup1)
    wdn1 = weights("ffn1_dn", h1)["wdn1"]
    x1 = _ffn_down("ffn1_down", a1, wdn1, x)
    wm = weights("mix", x1)
    z, h2 = _mix_proj(x1, p["g_mix"], wm["wz"])
    qf, ka, va, yg = _mix_prep(z, bf128, p["g_q"], p["g_k"], p["g_sgu"], p["w_s"], b_st, p["g_gmlp_o"])
    attn, lse = _fox_fwd(qf, ka, va)
    x2 = _mix_out(attn, yg, p["g_fox_o"], wm["wout"], x1)
    wc = weights("ca", x2)
    mn, kraw, ckn, cvv = _ca_kv(mem, p["g_mem"], wc["wckv"], p["g_ck"])
    x3 = _ca_fwd(x2, p["g_ca"], wc["wcq"], p["g_cq"], ckn, cvv, wc["wco"])
    w2 = weights("ffn2", x3)
    a2, h4 = _ffn_up("ffn2_up", x3, p["g_ffn2"], w2["wup2"])
    dy4, dy4b, sq = _ffn_down_loss("ffn2_down", a2, w2["wdn2"], x3, target)

    gs = {}
    dgu2 = _ffn_bwd_act("ffn2_bwd_act", dy4b, h4, w2["wup2"], w2["wdn2"])
    tok = emit("ffn2", {"wup2": _ffn_dwup("ffn2", h4, dgu2), "wdn2": _ffn_dwdn("ffn2", a2, dy4b)})
    dx3, gs["g_ffn2"] = _ffn_dx("ffn2_dx", dgu2, w2["wup2"], x3, _after(p["g_ffn2"], tok), dy4)

    dx2, dwcq, dwco, dckn, dcvv, gs["g_cq"], gs["g_ca"] = _ca_bwd(
        x2, dx3, p["g_ca"], wc["wcq"], p["g_cq"], ckn, cvv, wc["wco"])
    dwckv, gs["g_ck"], gs["g_mem"] = _ca_kv_bwd(mem, p["g_mem"], mn, kraw, dckn, dcvv, wc["wckv"], p["g_ck"])

    qb, dob, dyg, dwout, gs["g_fox_o"] = _mix_out_bwd(dx2, attn, yg, p["g_fox_o"], wm["wout"], qf, lse)
    dq, dk, dv = _fox_bwd(qb, ka, va, dob)
    dz, gs["g_q"], gs["g_k"], gs["g_sgu"], gs["g_gmlp_o"], gs["w_s"], dbst, dbf = _mix_prep_bwd(
        z, dq, dk, dv, dyg, bf128, p["g_q"], p["g_k"], p["g_sgu"], p["w_s"], b_st, p["g_gmlp_o"])
    gs["b_s"] = dbst.T
    gs["b_f"] = dbf[:, :FOX_HEADS]
    tok_ws = emit("w_s", {"w_s": gs["w_s"]})
    tk = _tile(T, 1024)
    zb = ZW // 3
    dwz = _tn_matmul(
        "mix_dwz", dz, pl.BlockSpec((tk, zb), lambda j, k: (k, j)), h2, pl.BlockSpec((tk, D), lambda j, k: (k, 0)),
        S((ZW, D), BF), pl.BlockSpec((zb, D), lambda j, k: (j, 0)), (3, T // tk), (zb, D))
    tok = emit("mid", {"wcq": dwcq, "wco": dwco, "wckv": dwckv, "wout": dwout, "wz": dwz})
    dx1, dx1b, gs["g_mix"] = _mix_proj_bwd(dz, wm["wz"], x1, _after(_after(p["g_mix"], tok), tok_ws), dx2)

    dgu1 = _ffn_bwd_act("ffn1_bwd_act", dx1b, h1, wup1, wdn1)
    tok = emit("ffn1_dn", {"wdn1": _ffn_dwdn("ffn1", a1, dx1b)})
    tok = emit("ffn1_up", {"wup1": _ffn_dwup("ffn1", h1, dgu1, after=tok)})
    dx0, gs["g_ffn1"] = _ffn_dx("ffn1_dx", dgu1, wup1, x, _after(p["g_ffn1"], tok), dx1)
    return sq, dx0, gs


MESH = pl.DeviceIdType.MESH
HBM_SPEC = pl.BlockSpec(memory_space=pltpu.HBM)
N_PEER = N_DEV - 1


def _place():
    return lax.axis_index("x"), lax.axis_index("y"), lax.axis_index("c")


def _slot(px, py, pc):
    return 4 * px + 2 * py + pc


SEM_SPEC = pl.BlockSpec(memory_space=pltpu.SEMAPHORE)
ANY_SPEC = pl.BlockSpec(memory_space=pl.ANY)
DATAFLOW = pltpu.SideEffectType.DATAFLOW_SIDE_EFFECTING


def _hbm(a):
    return pltpu.with_memory_space_constraint(a, pltpu.HBM)


def _peer(x, y, c, r):
    return (1 - x if r & 4 else x, 1 - y if r & 2 else y, 1 - c if r & 1 else c)


def _place_own(srcs, whole):
    my = _slot(*_place())
    lands = []
    for s in srcs:
        blk = s[None] if whole else lax.dynamic_slice_in_dim(s, my, 1, 0)
        shape = (N_DEV,) + s.shape if whole else s.shape
        lands.append(lax.dynamic_update_slice_in_dim(lax.empty(shape, s.dtype), blk, my, 0))
    return lands


ALL_PEERS = tuple(range(1, N_DEV))
NEAR_PEERS = (1, 2, 4, 6)
SAME_CORE = (2, 4, 6)


def _copy_start(name, srcs, lands, whole, peers=None):
    n = len(srcs)
    peers = peers or [ALL_PEERS] * n

    def body(*refs):
        src, land = refs[:n], refs[n:2 * n]
        send, recv = refs[2 * n:3 * n], refs[3 * n:4 * n]
        token = refs[6 * n]
        x, y, c = _place()
        my = _slot(x, y, c)
        for a in range(n):
            for r in peers[a]:
                p = _peer(x, y, c, r)
                pltpu.make_async_remote_copy(
                    src_ref=src[a] if whole else src[a].at[_slot(*p)], dst_ref=land[a].at[my],
                    send_sem=send[a].at[r - 1], recv_sem=recv[a].at[r - 1], device_id=p, device_id_type=MESH).start()
        token[...] = jnp.zeros_like(token)

    out = pl.pallas_call(
        body, name=name,
        out_shape=([pltpu.SemaphoreType.DMA((N_PEER,))] * (2 * n)
                   + [pltpu.HBM(s.shape, s.dtype) for s in srcs] + [pltpu.HBM(s.shape, s.dtype) for s in lands]
                   + [S((8, LANES), F32)]),
        in_specs=[HBM_SPEC] * (2 * n),
        out_specs=[SEM_SPEC] * (2 * n) + [HBM_SPEC] * (2 * n) + [pl.BlockSpec(memory_space=pltpu.VMEM)],
        input_output_aliases={i: 2 * n + i for i in range(2 * n)},
        compiler_params=pltpu.CompilerParams(has_side_effects=DATAFLOW),
    )(*[_hbm(s) for s in srcs], *[_hbm(s) for s in lands])
    return out[:n], out[n:2 * n], out[2 * n:3 * n], out[3 * n:4 * n], out[4 * n]


def _copy_wait(name, srcs, lands, send, recv, after, whole, peers=None, with_srcs=False):
    n = len(srcs)
    peers = peers or [ALL_PEERS] * n

    def body(*refs):
        src, land = refs[:n], refs[n:2 * n]
        snd, rcv = refs[2 * n:3 * n], refs[3 * n:4 * n]
        x, y, c = _place()
        for a in range(n):
            for r in peers[a]:
                p = _peer(x, y, c, r)
                ps = _slot(*p)
                cp = pltpu.make_async_remote_copy(
                    src_ref=src[a] if whole else src[a].at[ps], dst_ref=land[a].at[ps],
                    send_sem=snd[a].at[r - 1], recv_sem=rcv[a].at[r - 1], device_id=p, device_id_type=MESH)
                cp.wait_send()
                cp.wait_recv()

    out = pl.pallas_call(
        body, name=name,
        out_shape=[pltpu.HBM(s.shape, s.dtype) for s in srcs] + [pltpu.HBM(s.shape, s.dtype) for s in lands],
        in_specs=[HBM_SPEC] * (2 * n) + [SEM_SPEC] * (2 * n) + [ANY_SPEC],
        out_specs=[HBM_SPEC] * (2 * n),
        input_output_aliases={i: i for i in range(2 * n)},
        compiler_params=pltpu.CompilerParams(has_side_effects=DATAFLOW),
    )(*srcs, *lands, *send, *recv, after)
    return (out[:n], out[n:]) if with_srcs else out[n:]


def _forward_start(name, lands):
    n = len(lands)

    def body(*refs):
        land = refs[:n]
        send, recv = refs[n:2 * n], refs[2 * n:3 * n]
        token = refs[4 * n]
        x, y, c = _place()
        for a in range(n):
            for r in SAME_CORE:
                blk = land[a].at[_slot(*_peer(x, y, c, r))]
                pltpu.make_async_remote_copy(
                    src_ref=blk, dst_ref=blk, send_sem=send[a].at[r - 1], recv_sem=recv[a].at[r - 1],
                    device_id=(x, y, 1 - c), device_id_type=MESH).start()
        token[...] = jnp.zeros_like(token)

    out = pl.pallas_call(
        body, name=name,
        out_shape=([pltpu.SemaphoreType.DMA((N_PEER,))] * (2 * n) + [pltpu.HBM(s.shape, s.dtype) for s in lands]
                   + [S((8, LANES), F32)]),
        in_specs=[HBM_SPEC] * n,
        out_specs=[SEM_SPEC] * (2 * n) + [HBM_SPEC] * n + [pl.BlockSpec(memory_space=pltpu.VMEM)],
        input_output_aliases={i: 2 * n + i for i in range(n)},
        compiler_params=pltpu.CompilerParams(has_side_effects=DATAFLOW),
    )(*[_hbm(s) for s in lands])
    return out[:n], out[n:2 * n], out[2 * n:3 * n], out[3 * n]


def _forward_wait(name, lands, send, recv, after):
    n = len(lands)

    def body(*refs):
        land = refs[:n]
        snd, rcv = refs[n:2 * n], refs[2 * n:3 * n]
        x, y, c = _place()
        for a in range(n):
            for r in SAME_CORE:
                cp = pltpu.make_async_remote_copy(
                    src_ref=land[a].at[_slot(*_peer(x, y, c, r))], dst_ref=land[a].at[_slot(*_peer(x, y, c, r | 1))],
                    send_sem=snd[a].at[r - 1], recv_sem=rcv[a].at[r - 1], device_id=(x, y, 1 - c),
                    device_id_type=MESH)
                cp.wait_send()
                cp.wait_recv()

    return pl.pallas_call(
        body, name=name,
        out_shape=[pltpu.HBM(s.shape, s.dtype) for s in lands],
        in_specs=[HBM_SPEC] * n + [SEM_SPEC] * (2 * n) + [ANY_SPEC],
        out_specs=[HBM_SPEC] * n,
        input_output_aliases={i: i for i in range(n)},
        compiler_params=pltpu.CompilerParams(has_side_effects=DATAFLOW),
    )(*lands, *send, *recv, after)


def _adamw(w, g, m, v):
    m2 = ADAM_B1 * m + (1.0 - ADAM_B1) * g
    v2 = ADAM_B2 * v + (1.0 - ADAM_B2) * (g * g)
    m_hat = m2 / (1.0 - ADAM_B1 ** ADAM_STEP)
    v_hat = v2 / (1.0 - ADAM_B2 ** ADAM_STEP)
    delta = -ADAM_LR * (m_hat / (jnp.sqrt(v_hat) + ADAM_EPS) + ADAM_WD * w)
    return delta, m2, v2


def _adamw_big(name, slots, w, m, v, own=None):
    R, C = w.shape
    tr = next((t for t in (256, 352) if R % t == 0), R)

    def finish(g, w_ref, m_ref, v_ref, g_ref, d_ref, m2_ref, v2_ref):
        d, m2, v2 = _adamw(w_ref[...], g, m_ref[...], v_ref[...])
        g_ref[...] = g
        d_ref[...] = d
        m2_ref[...] = m2
        v2_ref[...] = v2

    if own is None:
        def body(s_ref, *refs):
            g = s_ref[0].astype(F32)
            for k in range(1, N_DEV):
                g = g + s_ref[k].astype(F32)
            finish(g, *refs)

        row = pl.BlockSpec((tr, C), lambda i: (i, 0))
        return pl.pallas_call(
            body, name=name, grid=(R // tr,),
            in_specs=[pl.BlockSpec((N_DEV, tr, C), lambda i: (0, i, 0)), row, row, row],
            out_specs=[row] * 4, out_shape=[S((R, C), F32)] * 4,
            compiler_params=_cp(1))(slots, w, m, v)

    def body(my_ref, s_ref, own_ref, *refs):
        mine = own_ref[...]
        g = None
        for k in range(N_DEV):
            part = jnp.where(my_ref[0] == k, mine, s_ref[k]).astype(F32)
            g = part if g is None else g + part
        finish(g, *refs)

    row = pl.BlockSpec((tr, C), lambda i, my_ref: (i, 0))
    my = jnp.reshape(_slot(*_place()), (1,)).astype(jnp.int32)
    return pl.pallas_call(
        body, name=name,
        grid_spec=pltpu.PrefetchScalarGridSpec(
            num_scalar_prefetch=1, grid=(R // tr,),
            in_specs=[pl.BlockSpec((N_DEV, tr, C), lambda i, my_ref: (0, i, 0)),
                      pl.BlockSpec((None, tr, C), lambda i, my_ref: (my_ref[0], i, 0)), row, row, row],
            out_specs=[row] * 4),
        out_shape=[S((R, C), F32)] * 4, compiler_params=_cp(1))(my, slots, own, w, m, v)


TINY_ROWS = (("b_s", 8), ("g_ffn1", 8), ("g_mix", 8), ("g_ca", 8), ("g_mem", 8), ("g_ffn2", 8), ("g_sgu", 4),
             ("g_fox_o", 4), ("g_gmlp_o", 4), ("g_cq", 2), ("g_ck", 2), ("g_q", 1), ("g_k", 1), ("b_f", 1),
             ("loss", 1))
TINY_P = 72


def _tiny_pieces(width):
    return [(j, slice(j * LANES, min((j + 1) * LANES, width))) for j in range(-(-width // LANES))]


def _pack_tiny(grads, sq):
    names = [n for n, _ in TINY_ROWS if n != "loss"]

    def body(*refs):
        ins, sq_ref, o_ref = refs[:len(names)], refs[len(names)], refs[len(names) + 1]
        o_ref[...] = jnp.zeros_like(o_ref)
        at = 0
        for ref, (name, r) in zip(ins, TINY_ROWS):
            if name == "b_s":
                o_ref[at:at + r, :] = ref[...]
            else:
                for j, cols in _tiny_pieces(ref.shape[1]):
                    o_ref[at + j:at + j + 1, 0:cols.stop - cols.start] = ref[:, cols]
            at += r
        o_ref[at:at + 1, :] = sq_ref[0:1, :]

    return pl.pallas_call(body, name="tiny_pack", out_shape=S((TINY_P, LANES), F32))(
        *[grads[n] for n in names], sq)


def _adamw_tiny(slots, w, m, v):
    names = [n for n, _ in TINY_ROWS if n != "loss"]
    k = len(names)

    def body(s_ref, *refs):
        ins, outs, loss_ref = refs[:3 * k], refs[3 * k:7 * k], refs[7 * k]
        g_all = s_ref[0]
        for d in range(1, N_DEV):
            g_all = g_all + s_ref[d]
        at = 0
        for i, (name, r) in enumerate(TINY_ROWS[:k]):
            w_ref, m_ref, v_ref = ins[i], ins[k + i], ins[2 * k + i]
            o = outs[4 * i:4 * i + 4]
            if name == "b_s":
                pieces = [(slice(at, at + r), slice(0, LANES), (slice(None), slice(None)))]
            else:
                pieces = [(slice(at + j, at + j + 1), slice(0, c.stop - c.start), (slice(None), c))
                          for j, c in _tiny_pieces(w_ref.shape[1])]
            for rows, lanes, dst in pieces:
                g = g_all[rows, lanes]
                res = (g,) + _adamw(w_ref[dst], g, m_ref[dst], v_ref[dst])
                for ref, val in zip(o, res):
                    ref[dst] = val
            at += r
        loss_ref[...] = g_all[at:at + 1, :]

    shapes = [S(w[n].shape, F32) for n in names]
    out = pl.pallas_call(
        body, name="adamw_tiny", out_shape=[s for s in shapes for _ in range(4)] + [S((1, LANES), F32)],
    )(slots, *[w[n] for n in names], *[m[n] for n in names], *[v[n] for n in names])
    stores = ({}, {}, {}, {})
    for i, n in enumerate(names):
        for store, t in zip(stores, out[4 * i:4 * i + 4]):
            store[n] = t
    return stores, out[4 * k]


WEIGHTS =('g_ffn1', 'w_ffn1_in', 'w_ffn1_out', 'g_mix', 'w_in', 'b_f', 'g_q', 'g_k', 'g_sgu', 'w_s', 'b_s',
           'g_fox_o', 'g_gmlp_o', 'w_out', 'g_ca', 'g_mem', 'w_cq', 'w_ckv', 'g_cq', 'g_ck', 'w_co', 'g_ffn2',
           'w_ffn2_in', 'w_ffn2_out')
BIG = ('w_ffn1_in', 'w_ffn1_out', 'w_in', 'w_out', 'w_cq', 'w_ckv', 'w_co', 'w_ffn2_in', 'w_ffn2_out')
TRANSPOSED = ('w_ffn1_in', 'w_in', 'w_ffn2_in')
TWO_LEVEL = ('w_ffn1_in', 'w_in')
GATHER_GROUPS = {"ffn1_up": ("w_ffn1_in",), "ffn1_dn": ("w_ffn1_out",), "mix": ("w_in", "w_out"),
                 "ca": ("w_cq", "w_ckv", "w_co"), "ffn2": ("w_ffn2_in", "w_ffn2_out")}
QKV_W = 3 * FOX_W
UV_OFF = QKV_W + FOX_HEADS


def kernel(x, mem, g_ffn1, w_ffn1_in, w_ffn1_out, g_mix, w_in, b_f, g_q, g_k, g_sgu, w_s, b_s, g_fox_o, g_gmlp_o, w_out, g_ca, g_mem, w_cq, w_ckv, g_cq, g_ck, w_co, g_ffn2, w_ffn2_in, w_ffn2_out, loss_target, m_g_ffn1, m_w_ffn1_in, m_w_ffn1_out, m_g_mix, m_w_in, m_b_f, m_g_q, m_g_k, m_g_sgu, m_w_s, m_b_s, m_g_fox_o, m_g_gmlp_o, m_w_out, m_g_ca, m_g_mem, m_w_cq, m_w_ckv, m_g_cq, m_g_ck, m_w_co, m_g_ffn2, m_w_ffn2_in, m_w_ffn2_out, v_g_ffn1, v_w_ffn1_in, v_w_ffn1_out, v_g_mix, v_w_in, v_b_f, v_g_q, v_g_k, v_g_sgu, v_w_s, v_b_s, v_g_fox_o, v_g_gmlp_o, v_w_out, v_g_ca, v_g_mem, v_w_cq, v_w_ckv, v_g_cq, v_g_ck, v_w_co, v_g_ffn2, v_w_ffn2_in, v_w_ffn2_out):
    args = dict(locals())
    w = {n: args[n] for n in WEIGHTS}
    mo = {n: args["m_" + n] for n in WEIGHTS}
    vo = {n: args["v_" + n] for n in WEIGHTS}
    D = D_MODEL

    def local(n, a):
        return a[0].T if n in TRANSPOSED else a[0]

    shards = [local(n, w[n]).astype(BF) for n in BIG]
    fb = shards[0].shape[0]
    g_peers = [NEAR_PEERS if n in TWO_LEVEL else ALL_PEERS for n in BIG]
    g_snd, g_rcv, g_src, g_land, g_token = _copy_start("gather_start", shards, _place_own(shards, True), True,
                                                       peers=g_peers)
    handles = {n: (g_src[i], g_land[i], g_snd[i], g_rcv[i]) for i, n in enumerate(BIG)}

    tiny_names = [n for n, _ in TINY_ROWS if n != "loss"]

    def weights(group, after):
        names = GATHER_GROUPS[group]
        hs = [handles[n] for n in names]
        got = list(_copy_wait("gather_wait_" + group, [h[0] for h in hs], [h[1] for h in hs], [h[2] for h in hs],
                              [h[3] for h in hs], after, True, peers=[g_peers[BIG.index(n)] for n in names]))
        passed = [i for i, n in enumerate(names) if n in TWO_LEVEL]
        if passed:
            f_snd, f_rcv, f_land, f_token = _forward_start("gather_pass_start_" + group, [got[i] for i in passed])
            for i, t in zip(passed, _forward_wait("gather_pass_wait_" + group, f_land, f_snd, f_rcv, f_token)):
                got[i] = t
        got = dict(zip(names, got))
        if group == "ffn1_up":
            return {"wup1": got["w_ffn1_in"].reshape(2, N_FFN_BLK, fb, D)}
        if group == "ffn1_dn":
            return {"wdn1": got["w_ffn1_out"].reshape(N_FFN_BLK, fb, D)}
        if group == "mix":
            full = got["w_in"].reshape(-1, D)
            wz = jnp.concatenate([full[:QKV_W], full[UV_OFF:], full[QKV_W:UV_OFF],
                                  jnp.zeros((LANES - FOX_HEADS, D), BF)], axis=0)
            return {"wz": wz, "wout": got["w_out"].reshape(D, D)}
        if group == "ca":
            return {"wcq": got["w_cq"].reshape(D, D), "wco": got["w_co"].reshape(D, D), "wckv": got["w_ckv"]}
        return {"wup2": got["w_ffn2_in"].reshape(2, N_FFN_BLK, fb, D),
                "wdn2": got["w_ffn2_out"].reshape(N_FFN_BLK, fb, D)}

    flying = {}

    def emit(group, g):
        if group == "w_s":
            part = [g["w_s"].reshape(-1, LANES)]
            *copies, token = _copy_start("w_s_start", part, _place_own(part, True), True)
            flying[group] = copies
            return token
        if group == "ffn2":
            parts = {"w_ffn2_in": g["wup2"], "w_ffn2_out": g["wdn2"].reshape(N_DEV, -1, D)}
        elif group == "ffn1_dn":
            parts = {"w_ffn1_out": g["wdn1"].reshape(N_DEV, -1, D)}
        elif group == "ffn1_up":
            parts = {"w_ffn1_in": g["wup1"]}
        else:
            gz = g["wz"]
            g_in = jnp.concatenate([gz[:QKV_W], gz[Z_F:Z_F + FOX_HEADS], gz[QKV_W:Z_F]], axis=0)
            parts = {"w_in": g_in.reshape(N_DEV, -1, D).astype(BF),
                     "w_out": g["wout"].reshape(N_DEV, -1, D), "w_cq": g["wcq"].reshape(N_DEV, -1, D),
                     "w_co": g["wco"].reshape(N_DEV, -1, D), "w_ckv": g["wckv"]}
        names = list(parts)
        srcs = [parts[n] for n in names]
        *copies, token = _copy_start("exchange_start_" + group, srcs, [lax.empty(s.shape, s.dtype) for s in srcs],
                                     False)
        flying[group] = (names, copies)
        return token

    small = {n: (w[n][0] if n == "b_s" else w[n]) for n in tiny_names}
    small["w_s"] = w["w_s"][0]

    sq, dx0, gs = _local_step(x[0], mem[0], loss_target[0], small, weights, emit)

    sm_parts = [_pack_tiny(gs, sq)]
    sm_snd, sm_rcv, sm_src, sm_land, sm_token = _copy_start("tiny_start", sm_parts, _place_own(sm_parts, True), True)

    grad, delta, new_m, new_v = {}, {}, {}, {}

    def update(group, after):
        names, (snd, rcv, srcs, lands) = flying[group]
        owns, slots = _copy_wait("exchange_wait_" + group, srcs, lands, snd, rcv, after, False, with_srcs=True)
        for n, sl, own in zip(names, slots, owns):
            g, d, m2, v2 = _adamw_big("adamw_" + n, sl, local(n, w[n]), local(n, mo[n]), local(n, vo[n]), own=own)
            grad[n], delta[n], new_m[n], new_v[n] = (
                (t.T if n in TRANSPOSED else t).reshape(w[n].shape) for t in (g, d, m2, v2))
        return d

    last = update("ffn2", sm_token)
    last = update("mid", last)
    last = update("ffn1_dn", last)
    last = update("ffn1_up", last)
    ws_snd, ws_rcv, ws_src, ws_land = flying["w_s"]
    ws_all, = _copy_wait("w_s_wait", ws_src, ws_land, ws_snd, ws_rcv, last, True)
    tiny_all, = _copy_wait("tiny_wait", sm_src, sm_land, sm_snd, sm_rcv, ws_all, True)
    ws_shape = w["w_s"].shape
    for store, t in zip((grad, delta, new_m, new_v), _adamw_big(
            "adamw_w_s", ws_all, *[a["w_s"].reshape(-1, LANES) for a in (w, mo, vo)])):
        store["w_s"] = t.reshape(ws_shape)
    stores, loss_row = _adamw_tiny(tiny_all, *[{n: (a[n][0] if n == "b_s" else a[n]) for n in tiny_names}
                                               for a in (w, mo, vo)])
    for store, t in zip((grad, delta, new_m, new_v), stores):
        store.update({n: v.reshape(w[n].shape) for n, v in t.items()})
    loss = loss_row[0, 0] * (0.5 / D)

    return (loss, dx0[None], *[grad[n] for n in WEIGHTS], *[delta[n] for n in WEIGHTS],
            *[new_m[n] for n in WEIGHTS], *[new_v[n] for n in WEIGHTS])
```

```python
import functools

import jax
import jax.numpy as jnp
from jax import lax
from jax.experimental import pallas as pl
from jax.experimental.pallas import tpu as pltpu

F32 = jnp.float32
BF = jnp.bfloat16
S = jax.ShapeDtypeStruct

N_DEV = 8
D_MODEL = 1024
FOX_HEADS, FOX_HD = 8, 64
FOX_W = 512
GMLP_G, GMLP_GD = 8, 64
GMLP_W = 512
CHUNK = 128
CA_HEADS, CA_HD = 4, 256
N_FFN_BLK = 4
ZW = 2688
Z_Q, Z_K, Z_V, Z_U, Z_G, Z_F = 0, 512, 1024, 1536, 2048, 2560
EPS = 1e-6
NEG = -1e30
LANES = 128

ADAM_LR, ADAM_B1, ADAM_B2, ADAM_EPS, ADAM_WD, ADAM_STEP = 0.001, 0.9, 0.999, 1e-08, 0.01, 10

VMEM_LIMIT = 52 * 2 ** 20


def _cp(n_axes):
    return pltpu.CompilerParams(dimension_semantics=("arbitrary",) * n_axes, vmem_limit_bytes=VMEM_LIMIT)


def _nn(a, b):
    return jnp.dot(a, b, preferred_element_type=F32)


def _nt(a, b):
    return lax.dot_general(a, b, (((1,), (1,)), ((), ())), preferred_element_type=F32)


def _tn(a, b):
    return lax.dot_general(a, b, (((0,), (0,)), ((), ())), preferred_element_type=F32)


def _hi(a, b):
    return jnp.dot(a, b, precision=lax.Precision.HIGHEST, preferred_element_type=F32)


def _rstd(x):
    return lax.rsqrt(jnp.mean(x * x, axis=-1, keepdims=True) + EPS)


def _norm_bwd(dy, x, g):
    r = _rstd(x)
    xh = x * r
    dxh = dy * g
    dx = r * (dxh - xh * jnp.mean(dxh * xh, axis=-1, keepdims=True))
    return dx, dy * xh


def _acc_rows(ref, first, val):
    srow = jnp.sum(val, axis=0, keepdims=True)

    @pl.when(first)
    def _():
        ref[...] = srow

    @pl.when(jnp.logical_not(first))
    def _():
        ref[...] += srow


def _gelu(x):
    c = 0.7978845608028654
    return 0.5 * x * (1.0 + jnp.tanh(c * (x + 0.044715 * x * x * x)))


def _gelu_grad(x):
    c = 0.7978845608028654
    t = jnp.tanh(c * (x + 0.044715 * x * x * x))
    return 0.5 * (1.0 + t) + 0.5 * x * (1.0 - t * t) * c * (1.0 + 3 * 0.044715 * x * x)


def _tile(n, pref):
    return pref if n % pref == 0 else n


def _ffn_up(name, x, g, wup):
    T, D = x.shape
    FB = wup.shape[-2]
    tm = _tile(T, 1024)

    def body(x_ref, g_ref, w_ref, a_ref, h_ref):
        @pl.when(pl.program_id(1) == 0)
        def _():
            xf = x_ref[...]
            h_ref[...] = (xf * _rstd(xf) * g_ref[...]).astype(BF)

        hb = h_ref[...]
        gg = _nt(hb, w_ref[0])
        uu = _nt(hb, w_ref[1])
        a_ref[...] = (gg * jax.nn.sigmoid(gg) * uu).astype(BF)

    return pl.pallas_call(
        body, name=name, grid=(T // tm, N_FFN_BLK),
        in_specs=[pl.BlockSpec((tm, D), lambda i, j: (i, 0)),
                  pl.BlockSpec((1, D), lambda i, j: (0, 0)),
                  pl.BlockSpec((2, None, FB, D), lambda i, j: (0, j, 0, 0))],
        out_specs=[pl.BlockSpec((None, tm, FB), lambda i, j: (j, i, 0)),
                   pl.BlockSpec((tm, D), lambda i, j: (i, 0))],
        out_shape=[S((N_FFN_BLK, T, FB), BF), S((T, D), BF)],
        compiler_params=_cp(2))(x, g, wup)


def _ffn_down(name, a, wdn, x):
    _, T, FB = a.shape
    D = x.shape[1]
    tm = _tile(T, 512)

    def body(a_ref, w_ref, x_ref, o_ref):
        p = _nn(a_ref[0], w_ref[0])
        for j in range(1, N_FFN_BLK):
            p = p + _nn(a_ref[j], w_ref[j])
        o_ref[...] = x_ref[...] + 0.5 * p

    return pl.pallas_call(
        body, name=name, grid=(T // tm,),
        in_specs=[pl.BlockSpec((N_FFN_BLK, tm, FB), lambda i: (0, i, 0)),
                  pl.BlockSpec((N_FFN_BLK, FB, D), lambda i: (0, 0, 0)),
                  pl.BlockSpec((tm, D), lambda i: (i, 0))],
        out_specs=pl.BlockSpec((tm, D), lambda i: (i, 0)),
        out_shape=S((T, D), F32),
        compiler_params=_cp(1))(a, wdn, x)


def _ffn_down_loss(name, a, wdn, x, target):
    _, T, FB = a.shape
    D = x.shape[1]
    tm = _tile(T, 512)

    def body(a_ref, w_ref, x_ref, t_ref, d_ref, db_ref, loss_ref):
        i = pl.program_id(0)
        p = _nn(a_ref[0], w_ref[0])
        for j in range(1, N_FFN_BLK):
            p = p + _nn(a_ref[j], w_ref[j])
        diff = (x_ref[...] + 0.5 * p) - t_ref[...]
        dy = diff * (1.0 / D)
        d_ref[...] = dy
        db_ref[...] = dy.astype(BF)
        sq = jnp.zeros((8, LANES), F32) + jnp.sum(diff * diff)

        @pl.when(i == 0)
        def _():
            loss_ref[...] = sq

        @pl.when(i > 0)
        def _():
            loss_ref[...] += sq

    row = pl.BlockSpec((tm, D), lambda i: (i, 0))
    return pl.pallas_call(
        body, name=name, grid=(T // tm,),
        in_specs=[pl.BlockSpec((N_FFN_BLK, tm, FB), lambda i: (0, i, 0)),
                  pl.BlockSpec((N_FFN_BLK, FB, D), lambda i: (0, 0, 0)), row, row],
        out_specs=[row, row, pl.BlockSpec((8, LANES), lambda i: (0, 0))],
        out_shape=[S((T, D), F32), S((T, D), BF), S((8, LANES), F32)],
        compiler_params=_cp(1))(a, wdn, x, target)


def _ffn_bwd_act(name, dyb, h, wup, wdn):
    T, D = h.shape
    FB = wup.shape[-2]
    tm = _tile(T, 1024)

    def body(d_ref, h_ref, wu_ref, wd_ref, o_ref):
        da = 0.5 * _nt(d_ref[...], wd_ref[...])
        hb = h_ref[...]
        gg = _nt(hb, wu_ref[0])
        uu = _nt(hb, wu_ref[1])
        sg = jax.nn.sigmoid(gg)
        o_ref[0] = (da * uu * (sg * (1.0 + gg * (1.0 - sg)))).astype(BF)
        o_ref[1] = (da * (gg * sg)).astype(BF)

    return pl.pallas_call(
        body, name=name, grid=(T // tm, N_FFN_BLK),
        in_specs=[pl.BlockSpec((tm, D), lambda i, j: (i, 0)),
                  pl.BlockSpec((tm, D), lambda i, j: (i, 0)),
                  pl.BlockSpec((2, None, FB, D), lambda i, j: (0, j, 0, 0)),
                  pl.BlockSpec((None, FB, D), lambda i, j: (j, 0, 0))],
        out_specs=pl.BlockSpec((2, None, tm, FB), lambda i, j: (0, j, i, 0)),
        out_shape=S((2, N_FFN_BLK, T, FB), BF),
        compiler_params=_cp(2))(dyb, h, wup, wdn)


def _ffn_dx(name, dgu, wup, x, g, dy):
    T, D = x.shape
    FB = wup.shape[-2]
    tm = _tile(T, 512)

    def body(d_ref, w_ref, x_ref, g_ref, dy_ref, dx_ref, dg_ref):
        p = None
        for j in range(N_FFN_BLK):
            for half in range(2):
                t = _nn(d_ref[half, j], w_ref[half, j])
                p = t if p is None else p + t
        dx, dgr = _norm_bwd(p, x_ref[...], g_ref[...])
        dx_ref[...] = dx + dy_ref[...]
        _acc_rows(dg_ref, pl.program_id(0) == 0, dgr)

    return pl.pallas_call(
        body, name=name, grid=(T // tm,),
        in_specs=[pl.BlockSpec((2, N_FFN_BLK, tm, FB), lambda i: (0, 0, i, 0)),
                  pl.BlockSpec((2, N_FFN_BLK, FB, D), lambda i: (0, 0, 0, 0), pipeline_mode=pl.Buffered(1)),
                  pl.BlockSpec((tm, D), lambda i: (i, 0)),
                  pl.BlockSpec((1, D), lambda i: (0, 0)),
                  pl.BlockSpec((tm, D), lambda i: (i, 0))],
        out_specs=[pl.BlockSpec((tm, D), lambda i: (i, 0)),
                   pl.BlockSpec((1, D), lambda i: (0, 0))],
        out_shape=[S((T, D), F32), S((1, D), F32)],
        compiler_params=_cp(1))(dgu, wup, x, g, dy)


def _tn_matmul(name, a, a_spec, b, out_shape, out_spec, n_blocks, scale=1.0, after=None):
    extra = [] if after is None else [after]

    def body(a_ref, b_ref, *rest):
        o_ref = rest[-1]
        o_ref[...] = (_tn(a_ref[...], b_ref[...]) * scale).astype(o_ref.dtype)

    return pl.pallas_call(
        body, name=name, grid=(n_blocks,),
        in_specs=[a_spec, pl.BlockSpec(b.shape, lambda j: (0, 0), pipeline_mode=pl.Buffered(1))]
        + [pl.BlockSpec((8, LANES), lambda j: (0, 0)) for _ in extra],
        out_specs=out_spec, out_shape=out_shape, compiler_params=_cp(1))(a, b, *extra)


def _ffn_dwup(name, h, dgu, after=None):
    T, D = h.shape
    FB = dgu.shape[-1]
    return _tn_matmul(
        name + "_dwup", dgu.reshape(2 * N_FFN_BLK, T, FB), pl.BlockSpec((None, T, FB), lambda j: (j, 0, 0)), h,
        S((2 * N_FFN_BLK, FB, D), BF), pl.BlockSpec((None, FB, D), lambda j: (j, 0, 0)), 2 * N_FFN_BLK,
        after=after)


def _ffn_dwdn(name, a, dyb):
    _, T, FB = a.shape
    D = dyb.shape[1]
    return _tn_matmul(
        name + "_dwdn", a, pl.BlockSpec((None, T, FB), lambda j: (j, 0, 0)), dyb,
        S((N_FFN_BLK, FB, D), BF), pl.BlockSpec((None, FB, D), lambda j: (j, 0, 0)), N_FFN_BLK, scale=0.5)


def _mix_proj(x, g, wz):
    T, D = x.shape
    tm = _tile(T, 512)

    def body(x_ref, g_ref, w_ref, z_ref, h_ref):
        xf = x_ref[...]
        hb = (xf * _rstd(xf) * g_ref[...]).astype(BF)
        h_ref[...] = hb
        z_ref[...] = _nt(hb, w_ref[...])

    return pl.pallas_call(
        body, name="mix_proj", grid=(T // tm,),
        in_specs=[pl.BlockSpec((tm, D), lambda i: (i, 0)),
                  pl.BlockSpec((1, D), lambda i: (0, 0)),
                  pl.BlockSpec((ZW, D), lambda i: (0, 0))],
        out_specs=[pl.BlockSpec((tm, ZW), lambda i: (i, 0)),
                   pl.BlockSpec((tm, D), lambda i: (i, 0))],
        out_shape=[S((T, ZW), F32), S((T, D), BF)],
        compiler_params=_cp(1))(x, g, wz)


def _tri(n, lower):
    r = lax.broadcasted_iota(jnp.int32, (n, n), 0)
    c = lax.broadcasted_iota(jnp.int32, (n, n), 1)
    return (r >= c) if lower else (r <= c)


def _spatial_mix(vgn_b, ws_ref, bst, tm):
    tril = _tri(CHUNK, True)
    wms = [jnp.where(tril, ws_ref[g], 0.0).astype(BF) for g in range(GMLP_G)]
    rows = []
    for c in range(tm // CHUNK):
        cols = []
        for g in range(GMLP_G):
            vs = vgn_b[c * CHUNK:(c + 1) * CHUNK, g * GMLP_GD:(g + 1) * GMLP_GD]
            cols.append(_nn(wms[g], vs) + bst[:, g:g + 1])
        rows.append(jnp.concatenate(cols, axis=1))
    return jnp.concatenate(rows, axis=0), wms


HB = 128
AUG_W = FOX_HEADS * HB
COL_A, COL_B, COL_C = 64, 67, 70


def _spread_matrix():
    r = jnp.arange(FOX_W)
    return (jnp.arange(AUG_W)[None, :] == ((r // FOX_HD) * HB + r % FOX_HD)[:, None]).astype(BF)


def _piece_matrix(col):
    r = jnp.arange(LANES)
    dst = jnp.where(r < 3 * FOX_HEADS, (r % FOX_HEADS) * HB + col + r // FOX_HEADS, -1)
    return (jnp.arange(AUG_W)[None, :] == dst[:, None]).astype(BF)


def _ones_row(cols):
    c = jnp.arange(AUG_W) % HB
    hit = functools.reduce(jnp.logical_or, [(c >= a) & (c < a + 3) for a in cols])
    return hit.astype(F32)[None, :]


def _pieces(x):
    lane = lax.broadcasted_iota(jnp.int32, x.shape, 1)
    x = jnp.where(lane < FOX_HEADS, x, 0.0)
    hi = x.astype(BF).astype(F32)
    r1 = x - hi
    mid = r1.astype(BF).astype(F32)
    lo = (r1 - mid).astype(BF).astype(F32)
    return (hi + pltpu.roll(mid, FOX_HEADS, 1) + pltpu.roll(lo, 2 * FOX_HEADS, 1)).astype(BF)


def _mix_prep(z, bf128, g_q, g_k, g_sgu, w_s, b_st, g_go):
    T = z.shape[0]
    tm = _tile(T, 512)
    spread, pc_q, pc_k = _spread_matrix(), _piece_matrix(COL_A), _piece_matrix(COL_B)
    one_q, one_k, one_v = _ones_row([COL_B]), _ones_row([COL_A, COL_C]), _ones_row([COL_A])

    def body(z_ref, bf_ref, gq_ref, gk_ref, gs_ref, ws_ref, bst_ref, go_ref, sp_ref, pq_ref, pk_ref, oq_ref, ok_ref,
             ov_ref, q_ref, k_ref, v_ref, y_ref, carry_ref, qn_sc, kn_sc):
        i = pl.program_id(0)

        @pl.when(i == 0)
        def _():
            carry_ref[...] = jnp.zeros_like(carry_ref)

        for h in range(FOX_HEADS):
            hs = slice(h * FOX_HD, (h + 1) * FOX_HD)
            qh = z_ref[:, Z_Q + h * FOX_HD:Z_Q + (h + 1) * FOX_HD]
            kh = z_ref[:, Z_K + h * FOX_HD:Z_K + (h + 1) * FOX_HD]
            qn_sc[:, hs] = (qh * _rstd(qh) * gq_ref[...] * 0.125).astype(BF)
            kn_sc[:, hs] = (kh * _rstd(kh) * gk_ref[...]).astype(BF)

        fl = z_ref[:, Z_F:Z_F + LANES] + bf_ref[...]
        logf = jnp.minimum(fl, 0.0) - jnp.log1p(jnp.exp(-jnp.abs(fl)))
        csum = _hi(_tri(tm, True).astype(F32), logf) + carry_ref[...]
        carry_ref[...] = csum[tm - 1:tm, :]
        sp = sp_ref[...]
        q_ref[...] = (_nn(qn_sc[...], sp) + _nn(_pieces(csum), pq_ref[...]) + oq_ref[...]).astype(BF)
        k_ref[...] = (_nn(kn_sc[...], sp) + _nn(_pieces(-csum), pk_ref[...]) + ok_ref[...]).astype(BF)
        v_ref[...] = (_nn(z_ref[:, Z_V:Z_V + FOX_W].astype(BF), sp) + ov_ref[...]).astype(BF)

        u = _gelu(z_ref[:, Z_U:Z_U + GMLP_W])
        vg = _gelu(z_ref[:, Z_G:Z_G + GMLP_W])
        vgn = (vg * _rstd(vg) * gs_ref[...]).astype(BF)
        mixed, _ = _spatial_mix(vgn, ws_ref, bst_ref[...], tm)
        sgu = u * mixed
        y_ref[...] = (sgu * _rstd(sgu) * go_ref[...]).astype(BF)

    row = lambda i: (i, 0)
    fix2 = lambda i: (0, 0)
    return pl.pallas_call(
        body, name="mix_prep", grid=(T // tm,),
        in_specs=[pl.BlockSpec((tm, ZW), row),
                  pl.BlockSpec((1, LANES), fix2), pl.BlockSpec((1, FOX_HD), fix2), pl.BlockSpec((1, FOX_HD), fix2),
                  pl.BlockSpec((1, GMLP_W), fix2), pl.BlockSpec((GMLP_G, CHUNK, CHUNK), lambda i: (0, 0, 0)),
                  pl.BlockSpec((CHUNK, GMLP_G), fix2), pl.BlockSpec((1, GMLP_W), fix2),
                  pl.BlockSpec((FOX_W, AUG_W), fix2), pl.BlockSpec((LANES, AUG_W), fix2),
                  pl.BlockSpec((LANES, AUG_W), fix2), pl.BlockSpec((1, AUG_W), fix2), pl.BlockSpec((1, AUG_W), fix2),
                  pl.BlockSpec((1, AUG_W), fix2)],
        out_specs=[pl.BlockSpec((tm, AUG_W), row), pl.BlockSpec((tm, AUG_W), row), pl.BlockSpec((tm, AUG_W), row),
                   pl.BlockSpec((tm, GMLP_W), row)],
        out_shape=[S((T, AUG_W), BF), S((T, AUG_W), BF), S((T, AUG_W), BF), S((T, GMLP_W), BF)],
        scratch_shapes=[pltpu.VMEM((1, LANES), F32), pltpu.VMEM((tm, FOX_W), BF), pltpu.VMEM((tm, FOX_W), BF)],
        compiler_params=_cp(1))(z, bf128, g_q, g_k, g_sgu, w_s, b_st, g_go, spread, pc_q, pc_k, one_q, one_k, one_v)


def _fox_fwd(q, k, v):
    T = q.shape[0]
    tq = _tile(T, 1024)
    nq = T // tq

    def body(q_ref, k_ref, v_ref, o_ref, lse_ref, m_sc, acc_sc):
        i, j = pl.program_id(0), pl.program_id(1)

        @pl.when(j == 0)
        def _():
            m_sc[...] = jnp.full(m_sc.shape, NEG, F32)
            acc_sc[...] = jnp.zeros_like(acc_sc)

        def step(masked):
            mask = _tri(tq, True) if masked else None
            for h in range(FOX_HEADS):
                hb = slice(h * HB, (h + 1) * HB)
                s = _nt(q_ref[:, hb], k_ref[:, hb])
                if masked:
                    s = jnp.where(mask, s, NEG)
                m_prev = m_sc[h]
                m_new = jnp.maximum(m_prev, jnp.broadcast_to(jnp.max(s, axis=1, keepdims=True), (tq, HB)))
                p = jnp.exp(s - jnp.tile(m_new, (1, tq // HB))).astype(BF)
                acc_sc[:, hb] = jnp.exp(m_prev - m_new) * acc_sc[:, hb] + _nn(p, v_ref[:, hb])
                m_sc[h] = m_new

        @pl.when(j < i)
        def _():
            step(False)

        @pl.when(j == i)
        def _():
            step(True)
            lse_ref[...] = jnp.zeros_like(lse_ref)
            for h in range(FOX_HEADS):
                l = acc_sc[:, h * HB + COL_A:h * HB + COL_A + 1]
                o_ref[:, h * FOX_HD:(h + 1) * FOX_HD] = acc_sc[:, h * HB:h * HB + FOX_HD] / l
                lse_ref[:, h:h + 1] = m_sc[h][:, 0:1] + jnp.log(l)

    qi = lambda i, j: (i, 0)
    kj = lambda i, j: (jnp.minimum(i, j), 0)
    return pl.pallas_call(
        body, name="fox_fwd", grid=(nq, nq),
        in_specs=[pl.BlockSpec((tq, AUG_W), qi), pl.BlockSpec((tq, AUG_W), kj), pl.BlockSpec((tq, AUG_W), kj)],
        out_specs=[pl.BlockSpec((tq, FOX_W), qi), pl.BlockSpec((tq, LANES), qi)],
        out_shape=[S((T, FOX_W), F32), S((T, LANES), F32)],
        scratch_shapes=[pltpu.VMEM((FOX_HEADS, tq, HB), F32), pltpu.VMEM((tq, AUG_W), F32)],
        compiler_params=_cp(2))(q, k, v)


def _fox_bwd(q, k, v, dob):
    T = q.shape[0]
    tq = _tile(T, 512)
    nq = T // tq
    half = AUG_W // 2
    hpg = FOX_HEADS // 2

    pairs = [(j, i) for j in range(nq) for i in range(j, nq)]
    jt = jnp.asarray([p[0] for p in pairs], jnp.int32)
    it = jnp.asarray([p[1] for p in pairs], jnp.int32)

    def body(jt_ref, it_ref, q_ref, k_ref, v_ref, do_ref, dq_ref, dk_ref, dv_ref, dq_sc):
        t = pl.program_id(1)
        j, i = jt_ref[t], it_ref[t]

        @pl.when(t == 0)
        def _():
            dq_sc[...] = jnp.zeros_like(dq_sc)

        @pl.when(i == j)
        def _():
            dk_ref[...] = jnp.zeros_like(dk_ref)
            dv_ref[...] = jnp.zeros_like(dv_ref)

        def step(masked):
            rows = pl.ds(pl.multiple_of(i * tq, tq), tq)
            mask = _tri(tq, True) if masked else None
            for h in range(hpg):
                hb = slice(h * HB, (h + 1) * HB)
                qh, kh, vh, doh = q_ref[:, hb], k_ref[:, hb], v_ref[:, hb], do_ref[:, hb]
                s = _nt(qh, kh)
                if masked:
                    s = jnp.where(mask, s, NEG)
                p = jnp.exp(s)
                dsb = (p * _nt(doh, vh)).astype(BF)
                dv_ref[:, hb] += _tn(p.astype(BF), doh)
                dk_ref[:, hb] += _tn(dsb, qh)
                dq_sc[rows, hb] += _nn(dsb, kh)

        @pl.when(i > j)
        def _():
            step(False)

        @pl.when(i == j)
        def _():
            step(True)
            dq_ref[...] = dq_sc[pl.ds(pl.multiple_of(j * tq, tq), tq), :]

    qi = pl.BlockSpec((tq, half), lambda g, t, jt_ref, it_ref: (it_ref[t], g))
    kj = pl.BlockSpec((tq, half), lambda g, t, jt_ref, it_ref: (jt_ref[t], g))
    return pl.pallas_call(
        body, name="fox_bwd",
        grid_spec=pltpu.PrefetchScalarGridSpec(
            num_scalar_prefetch=2, grid=(2, len(pairs)), in_specs=[qi, kj, kj, qi], out_specs=[kj, kj, kj],
            scratch_shapes=[pltpu.VMEM((T, half), F32)]),
        out_shape=[S((T, AUG_W), F32), S((T, AUG_W), F32), S((T, AUG_W), F32)],
        compiler_params=_cp(2))(jt, it, q, k, v, dob)


def _mix_out(attn, yg, g_fo, wout, x):
    T, D = x.shape
    tm = _tile(T, 1024)

    def body(a_ref, y_ref, g_ref, w_ref, x_ref, o_ref):
        at = a_ref[...]
        yf = (at * _rstd(at) * g_ref[...]).astype(BF)
        o_ref[...] = x_ref[...] + _nn(yf, w_ref[:FOX_W, :]) + _nn(y_ref[...], w_ref[FOX_W:, :])

    row = lambda i: (i, 0)
    return pl.pallas_call(
        body, name="mix_out", grid=(T // tm,),
        in_specs=[pl.BlockSpec((tm, FOX_W), row), pl.BlockSpec((tm, GMLP_W), row),
                  pl.BlockSpec((1, FOX_W), lambda i: (0, 0)), pl.BlockSpec((D, D), lambda i: (0, 0)),
                  pl.BlockSpec((tm, D), row)],
        out_specs=pl.BlockSpec((tm, D), row),
        out_shape=S((T, D), F32),
        compiler_params=_cp(1))(attn, yg, g_fo, wout, x)


def _mix_out_bwd(dx, attn, yg, g_fo, wout, qf, lse):
    T, D = dx.shape
    tm = _tile(T, 512)
    n = T // tm
    spread, pc_l, pc_d = _spread_matrix(), _piece_matrix(COL_C), _piece_matrix(COL_A)

    def body(dx_ref, a_ref, y_ref, g_ref, w_ref, qf_ref, lse_ref, sp_ref, pl_ref, pd_ref,
             qb_ref, dob_ref, dyg_ref, dw_ref, dg_ref, acc_ref, dsum_ref):
        i = pl.program_id(0)
        dxb = dx_ref[...].astype(BF)
        at = a_ref[...]
        yf = (at * _rstd(at) * g_ref[...]).astype(BF)
        dy = _nt(dxb, w_ref[...])
        p_top = _tn(yf, dxb)
        p_bot = _tn(y_ref[...], dxb)

        @pl.when(i == 0)
        def _():
            acc_ref[:FOX_W, :] = p_top
            acc_ref[FOX_W:, :] = p_bot

        @pl.when(i > 0)
        def _():
            acc_ref[:FOX_W, :] += p_top
            acc_ref[FOX_W:, :] += p_bot

        @pl.when(i == n - 1)
        def _():
            dw_ref[...] = acc_ref[...].astype(BF)

        dat, dgr = _norm_bwd(dy[:, :FOX_W], at, g_ref[...])
        _acc_rows(dg_ref, i == 0, dgr)
        dyg_ref[...] = dy[:, FOX_W:]
        prod = dat * at
        dsum_ref[...] = jnp.zeros_like(dsum_ref)
        for h in range(FOX_HEADS):
            dsum_ref[:, h:h + 1] = jnp.sum(prod[:, h * FOX_HD:(h + 1) * FOX_HD], axis=1, keepdims=True)
        dob_ref[...] = (_nn(dat.astype(BF), sp_ref[...]) + _nn(_pieces(-dsum_ref[...]), pd_ref[...])).astype(BF)
        qb_ref[...] = (qf_ref[...].astype(F32) + _nn(_pieces(-lse_ref[...]), pl_ref[...])).astype(BF)

    row = lambda i: (i, 0)
    fix = lambda i: (0, 0)
    return pl.pallas_call(
        body, name="mix_out_bwd", grid=(n,),
        in_specs=[pl.BlockSpec((tm, D), row), pl.BlockSpec((tm, FOX_W), row), pl.BlockSpec((tm, GMLP_W), row),
                  pl.BlockSpec((1, FOX_W), fix), pl.BlockSpec((D, D), fix), pl.BlockSpec((tm, AUG_W), row),
                  pl.BlockSpec((tm, LANES), row), pl.BlockSpec((FOX_W, AUG_W), fix), pl.BlockSpec((LANES, AUG_W), fix),
                  pl.BlockSpec((LANES, AUG_W), fix)],
        out_specs=[pl.BlockSpec((tm, AUG_W), row), pl.BlockSpec((tm, AUG_W), row), pl.BlockSpec((tm, GMLP_W), row),
                   pl.BlockSpec((D, D), fix), pl.BlockSpec((1, FOX_W), fix)],
        out_shape=[S((T, AUG_W), BF), S((T, AUG_W), BF), S((T, GMLP_W), F32), S((D, D), BF), S((1, FOX_W), F32)],
        scratch_shapes=[pltpu.VMEM((D, D), F32), pltpu.VMEM((tm, LANES), F32)],
        compiler_params=_cp(1))(dx, attn, yg, g_fo, wout, qf, lse, spread, pc_l, pc_d)


def _mix_prep_bwd(z, dq, dk, dv, dyg, bf128, g_q, g_k, g_sgu, w_s, b_st, g_go):
    T = z.shape[0]
    tm = _tile(T, 512)
    n = T // tm

    def body(z_ref, dq_ref, dk_ref, dv_ref, dyg_ref, bf_ref, gq_ref, gk_ref, gs_ref, ws_ref,
             bst_ref, go_ref, dz_ref, dgq_ref, dgk_ref, dgs_ref, dgo_ref, dws_ref, dbst_ref, dbf_ref, carry_ref):
        i = pl.program_id(0)
        first = i == 0

        @pl.when(first)
        def _():
            carry_ref[...] = jnp.zeros_like(carry_ref)

        lane = lax.broadcasted_iota(jnp.int32, (tm, LANES), 1)
        dc = jnp.zeros((tm, LANES), F32)
        gq_rows, gk_rows = [], []
        for h in range(FOX_HEADS):
            hp = slice(h * HB, h * HB + FOX_HD)
            dqh, gqr = _norm_bwd(dq_ref[:, hp] * 0.125, z_ref[:, Z_Q + h * FOX_HD:Z_Q + (h + 1) * FOX_HD], gq_ref[...])
            dkh, gkr = _norm_bwd(dk_ref[:, hp], z_ref[:, Z_K + h * FOX_HD:Z_K + (h + 1) * FOX_HD], gk_ref[...])
            dz_ref[:, Z_Q + h * FOX_HD:Z_Q + (h + 1) * FOX_HD] = dqh.astype(BF)
            dz_ref[:, Z_K + h * FOX_HD:Z_K + (h + 1) * FOX_HD] = dkh.astype(BF)
            dz_ref[:, Z_V + h * FOX_HD:Z_V + (h + 1) * FOX_HD] = dv_ref[:, hp].astype(BF)
            dch = dq_ref[:, h * HB + COL_A:h * HB + COL_A + 1] - dk_ref[:, h * HB + COL_B:h * HB + COL_B + 1]
            dc = jnp.where(lane == h, dch, dc)
            gq_rows.append(gqr)
            gk_rows.append(gkr)
        _acc_rows(dgq_ref, first, functools.reduce(lambda a, b: a + b, gq_rows))
        _acc_rows(dgk_ref, first, functools.reduce(lambda a, b: a + b, gk_rows))

        dlogf = _hi(_tri(tm, False).astype(F32), dc) + carry_ref[...]
        carry_ref[...] = dlogf[0:1, :]
        fl = z_ref[:, Z_F:Z_F + LANES] + bf_ref[...]
        lane = lax.broadcasted_iota(jnp.int32, (tm, LANES), 1)
        df = jnp.where(lane < FOX_HEADS, dlogf * jax.nn.sigmoid(-fl), 0.0)
        dz_ref[:, Z_F:Z_F + LANES] = df.astype(BF)
        _acc_rows(dbf_ref, first, df)

        u_pre = z_ref[:, Z_U:Z_U + GMLP_W]
        vg_pre = z_ref[:, Z_G:Z_G + GMLP_W]
        u = _gelu(u_pre)
        vg = _gelu(vg_pre)
        vgn = (vg * _rstd(vg) * gs_ref[...]).astype(BF)
        bst = bst_ref[...]
        mixed, wms = _spatial_mix(vgn, ws_ref, bst, tm)
        sgu = u * mixed
        dsgu, gor = _norm_bwd(dyg_ref[...], sgu, go_ref[...])
        _acc_rows(dgo_ref, first, gor)
        du = dsgu * mixed
        dmixed = dsgu * u
        dmb = dmixed.astype(BF)
        tril = _tri(CHUNK, True)
        dvgn_rows = []
        dws = [None] * GMLP_G
        dbs = [None] * GMLP_G
        for c in range(tm // CHUNK):
            cs = slice(c * CHUNK, (c + 1) * CHUNK)
            cols = []
            for g in range(GMLP_G):
                gs = slice(g * GMLP_GD, (g + 1) * GMLP_GD)
                dmc = dmb[cs, gs]
                pw = _nt(dmc, vgn[cs, gs])
                pb = jnp.sum(dmixed[cs, gs], axis=1, keepdims=True)
                dws[g] = pw if dws[g] is None else dws[g] + pw
                dbs[g] = pb if dbs[g] is None else dbs[g] + pb
                cols.append(_tn(wms[g], dmc))
            dvgn_rows.append(jnp.concatenate(cols, axis=1))
        dvgn = jnp.concatenate(dvgn_rows, axis=0)
        dbs_t = jnp.concatenate(dbs, axis=1)
        for g in range(GMLP_G):
            dwg = jnp.where(tril, dws[g], 0.0)

            @pl.when(first)
            def _():
                dws_ref[g] = dwg

            @pl.when(jnp.logical_not(first))
            def _():
                dws_ref[g] += dwg

        @pl.when(first)
        def _():
            dbst_ref[...] = dbs_t

        @pl.when(jnp.logical_not(first))
        def _():
            dbst_ref[...] += dbs_t

        dvg, gsr = _norm_bwd(dvgn, vg, gs_ref[...])
        _acc_rows(dgs_ref, first, gsr)
        dz_ref[:, Z_U:Z_U + GMLP_W] = (du * _gelu_grad(u_pre)).astype(BF)
        dz_ref[:, Z_G:Z_G + GMLP_W] = (dvg * _gelu_grad(vg_pre)).astype(BF)

    rev = lambda i: (n - 1 - i, 0)
    fix = lambda i: (0, 0)
    fix3 = lambda i: (0, 0, 0)
    return pl.pallas_call(
        body, name="mix_prep_bwd", grid=(n,),
        in_specs=[pl.BlockSpec((tm, ZW), rev), pl.BlockSpec((tm, AUG_W), rev), pl.BlockSpec((tm, AUG_W), rev),
                  pl.BlockSpec((tm, AUG_W), rev), pl.BlockSpec((tm, GMLP_W), rev),
                  pl.BlockSpec((1, LANES), fix), pl.BlockSpec((1, FOX_HD), fix), pl.BlockSpec((1, FOX_HD), fix),
                  pl.BlockSpec((1, GMLP_W), fix), pl.BlockSpec((GMLP_G, CHUNK, CHUNK), fix3),
                  pl.BlockSpec((CHUNK, GMLP_G), fix), pl.BlockSpec((1, GMLP_W), fix)],
        out_specs=[pl.BlockSpec((tm, ZW), rev), pl.BlockSpec((1, FOX_HD), fix), pl.BlockSpec((1, FOX_HD), fix),
                   pl.BlockSpec((1, GMLP_W), fix), pl.BlockSpec((1, GMLP_W), fix),
                   pl.BlockSpec((GMLP_G, CHUNK, CHUNK), fix3), pl.BlockSpec((CHUNK, GMLP_G), fix),
                   pl.BlockSpec((1, LANES), fix)],
        out_shape=[S((T, ZW), BF), S((1, FOX_HD), F32), S((1, FOX_HD), F32), S((1, GMLP_W), F32), S((1, GMLP_W), F32),
                   S((GMLP_G, CHUNK, CHUNK), F32), S((CHUNK, GMLP_G), F32), S((1, LANES), F32)],
        scratch_shapes=[pltpu.VMEM((1, LANES), F32)],
        compiler_params=_cp(1))(z, dq, dk, dv, dyg, bf128, g_q, g_k, g_sgu, w_s, b_st, g_go)


def _mix_proj_bwd(dz, wz, x, g, dy):
    T, D = x.shape
    tm = _tile(T, 512)

    def body(dz_ref, w_ref, x_ref, g_ref, dy_ref, dx_ref, dxb_ref, dg_ref):
        dh = _nn(dz_ref[...], w_ref[...])
        dx, dgr = _norm_bwd(dh, x_ref[...], g_ref[...])
        dx = dx + dy_ref[...]
        dx_ref[...] = dx
        dxb_ref[...] = dx.astype(BF)
        _acc_rows(dg_ref, pl.program_id(0) == 0, dgr)

    row = lambda i: (i, 0)
    fix = lambda i: (0, 0)
    return pl.pallas_call(
        body, name="mix_proj_bwd", grid=(T // tm,),
        in_specs=[pl.BlockSpec((tm, ZW), row), pl.BlockSpec((ZW, D), fix), pl.BlockSpec((tm, D), row),
                  pl.BlockSpec((1, D), fix), pl.BlockSpec((tm, D), row)],
        out_specs=[pl.BlockSpec((tm, D), row), pl.BlockSpec((tm, D), row), pl.BlockSpec((1, D), fix)],
        out_shape=[S((T, D), F32), S((T, D), BF), S((1, D), F32)],
        compiler_params=_cp(1))(dz, wz, x, g, dy)


def _ca_kv(mem, g_mem, wckv, g_ck):
    M, D = mem.shape

    def body(m_ref, g_ref, w_ref, gk_ref, mn_ref, kr_ref, kn_ref, v_ref):
        mf = m_ref[...]
        mn = (mf * _rstd(mf) * g_ref[...]).astype(BF)
        mn_ref[...] = mn
        for h in range(CA_HEADS):
            kr = _nn(mn, w_ref[h])
            kr_ref[h] = kr
            kn_ref[h] = (kr * _rstd(kr) * gk_ref[...]).astype(BF)
            v_ref[h] = _nn(mn, w_ref[CA_HEADS + h]).astype(BF)

    hd = (CA_HEADS, M, CA_HD)
    return pl.pallas_call(
        body, name="ca_kv", out_shape=[S((M, D), BF), S(hd, F32), S(hd, BF), S(hd, BF)],
        compiler_params=pltpu.CompilerParams(vmem_limit_bytes=VMEM_LIMIT))(mem, g_mem, wckv, g_ck)


def _ca_tile_fwd(xt, gca, wcq, gcq, kn_ref, v_ref):
    hb = (xt * _rstd(xt) * gca).astype(BF)
    qc = _nn(hb, wcq)
    qr, qn, ps = [], [], []
    for h in range(CA_HEADS):
        qh = qc[:, h * CA_HD:(h + 1) * CA_HD]
        qnh = (qh * _rstd(qh) * gcq * 0.0625).astype(BF)
        s = _nt(qnh, kn_ref[h])
        e = jnp.exp(s - jnp.max(s, axis=1, keepdims=True))
        ps.append(e / jnp.sum(e, axis=1, keepdims=True))
        qr.append(qh)
        qn.append(qnh)
    return hb, qr, qn, ps


def _ca_fwd(x, g_ca, wcq, g_cq, kn, vv, wco):
    T, D = x.shape
    M = kn.shape[1]
    tm = _tile(T, 1024)

    def body(x_ref, gca_ref, wcq_ref, gcq_ref, kn_ref, v_ref, wco_ref, o_ref, ob_sc):
        xt = x_ref[...]
        _, _, _, ps = _ca_tile_fwd(xt, gca_ref[...], wcq_ref[...], gcq_ref[...], kn_ref, v_ref)
        for h in range(CA_HEADS):
            ob_sc[:, h * CA_HD:(h + 1) * CA_HD] = _nn(ps[h].astype(BF), v_ref[h]).astype(BF)
        o_ref[...] = xt + _nn(ob_sc[...], wco_ref[...])

    row = lambda i: (i, 0)
    fix = lambda i: (0, 0)
    fix3 = lambda i: (0, 0, 0)
    return pl.pallas_call(
        body, name="ca_fwd", grid=(T // tm,),
        in_specs=[pl.BlockSpec((tm, D), row), pl.BlockSpec((1, D), fix), pl.BlockSpec((D, D), fix),
                  pl.BlockSpec((1, CA_HD), fix), pl.BlockSpec((CA_HEADS, M, CA_HD), fix3),
                  pl.BlockSpec((CA_HEADS, M, CA_HD), fix3), pl.BlockSpec((D, D), fix)],
        out_specs=pl.BlockSpec((tm, D), row), out_shape=S((T, D), F32),
        scratch_shapes=[pltpu.VMEM((tm, D), BF)],
        compiler_params=_cp(1))(x, g_ca, wcq, g_cq, kn, vv, wco)


def _ca_bwd(x, dy, g_ca, wcq, g_cq, kn, vv, wco):
    T, D = x.shape
    M = kn.shape[1]
    tm = _tile(T, 512)
    n = T // tm

    def body(x_ref, dy_ref, gca_ref, wcq_ref, gcq_ref, kn_ref, v_ref, wco_ref,
             dx_ref, dwq_ref, dwo_ref, dkn_ref, dv_ref, dgcq_ref, dgca_ref, aq_sc, ao_sc, ob_sc, dq_sc):
        i = pl.program_id(0)
        first = i == 0
        xt = x_ref[...]
        dyt = dy_ref[...]
        dyb = dyt.astype(BF)
        hb, qr, qn, ps = _ca_tile_fwd(xt, gca_ref[...], wcq_ref[...], gcq_ref[...], kn_ref, v_ref)
        do = _nt(dyb, wco_ref[...])
        gcq_rows = None
        for h in range(CA_HEADS):
            hs = slice(h * CA_HD, (h + 1) * CA_HD)
            p = ps[h]
            pb = p.astype(BF)
            ob_sc[:, hs] = _nn(pb, v_ref[h]).astype(BF)
            doh = do[:, hs].astype(BF)
            dp = _nt(doh, v_ref[h])
            ds = (p * (dp - jnp.sum(dp * p, axis=1, keepdims=True))).astype(BF)
            dvh = _tn(pb, doh)
            dkh = _tn(ds, qn[h])

            @pl.when(first)
            def _():
                dv_ref[h] = dvh
                dkn_ref[h] = dkh

            @pl.when(jnp.logical_not(first))
            def _():
                dv_ref[h] += dvh
                dkn_ref[h] += dkh

            dqn = _nn(ds, kn_ref[h]) * 0.0625
            dqh, gr = _norm_bwd(dqn, qr[h], gcq_ref[...])
            gcq_rows = gr if gcq_rows is None else gcq_rows + gr
            dq_sc[:, hs] = dqh.astype(BF)
        _acc_rows(dgcq_ref, first, gcq_rows)
        dqb = dq_sc[...]
        p_o = _tn(ob_sc[...], dyb)
        p_q = _tn(hb, dqb)

        @pl.when(first)
        def _():
            ao_sc[...] = p_o
            aq_sc[...] = p_q

        @pl.when(jnp.logical_not(first))
        def _():
            ao_sc[...] += p_o
            aq_sc[...] += p_q

        @pl.when(i == n - 1)
        def _():
            dwo_ref[...] = ao_sc[...].astype(BF)
            dwq_ref[...] = aq_sc[...].astype(BF)

        dh = _nt(dqb, wcq_ref[...])
        dx, gar = _norm_bwd(dh, xt, gca_ref[...])
        dx_ref[...] = dx + dyt
        _acc_rows(dgca_ref, first, gar)

    row = lambda i: (i, 0)
    fix = lambda i: (0, 0)
    fix3 = lambda i: (0, 0, 0)
    hd = (CA_HEADS, M, CA_HD)
    return pl.pallas_call(
        body, name="ca_bwd", grid=(n,),
        in_specs=[pl.BlockSpec((tm, D), row), pl.BlockSpec((tm, D), row), pl.BlockSpec((1, D), fix),
                  pl.BlockSpec((D, D), fix), pl.BlockSpec((1, CA_HD), fix), pl.BlockSpec(hd, fix3),
                  pl.BlockSpec(hd, fix3), pl.BlockSpec((D, D), fix)],
        out_specs=[pl.BlockSpec((tm, D), row), pl.BlockSpec((D, D), fix), pl.BlockSpec((D, D), fix),
                   pl.BlockSpec(hd, fix3), pl.BlockSpec(hd, fix3), pl.BlockSpec((1, CA_HD), fix),
                   pl.BlockSpec((1, D), fix)],
        out_shape=[S((T, D), F32), S((D, D), BF), S((D, D), BF), S(hd, F32), S(hd, F32), S((1, CA_HD), F32),
                   S((1, D), F32)],
        scratch_shapes=[pltpu.VMEM((D, D), F32), pltpu.VMEM((D, D), F32), pltpu.VMEM((tm, D), BF),
                        pltpu.VMEM((tm, D), BF)],
        compiler_params=_cp(1))(x, dy, g_ca, wcq, g_cq, kn, vv, wco)


def _ca_kv_bwd(mem, g_mem, mn, kraw, dkn, dvv, wckv, g_ck):
    M, D = mem.shape

    def body(m_ref, g_ref, mn_ref, kr_ref, dkn_ref, dv_ref, w_ref, gk_ref, dw_ref, dgk_ref, dgm_ref):
        mn = mn_ref[...]
        dmn = jnp.zeros((M, D), F32)
        gk_rows = None
        for h in range(CA_HEADS):
            dkr, gr = _norm_bwd(dkn_ref[h], kr_ref[h], gk_ref[...])
            gk_rows = gr if gk_rows is None else gk_rows + gr
            dkb = dkr.astype(BF)
            dvb = dv_ref[h].astype(BF)
            dw_ref[h] = _tn(mn, dkb).astype(BF)
            dw_ref[CA_HEADS + h] = _tn(mn, dvb).astype(BF)
            dmn = dmn + _nt(dkb, w_ref[h]) + _nt(dvb, w_ref[CA_HEADS + h])
        dgk_ref[...] = jnp.sum(gk_rows, axis=0, keepdims=True)
        mf = m_ref[...]
        dgm_ref[...] = jnp.sum(dmn * (mf * _rstd(mf)), axis=0, keepdims=True)

    return pl.pallas_call(
        body, name="ca_kv_bwd",
        out_shape=[S((2 * CA_HEADS, D, CA_HD), BF), S((1, CA_HD), F32), S((1, D), F32)],
        compiler_params=pltpu.CompilerParams(vmem_limit_bytes=VMEM_LIMIT))(mem, g_mem, mn, kraw, dkn, dvv, wckv, g_ck)


def _after(g, token):
    return g if token is None else g + token[0:1, 0:1]


def _local_step(x, mem, target, small, weights, emit):
    T, D = x.shape
    p = small
    bf128 = jnp.pad(p["b_f"], ((0, 0), (0, LANES - FOX_HEADS)))
    b_st = p["b_s"].T

    wup1 = weights("ffn1_up", x)["wup1"]
    a1, h1 = _ffn_up("ffn1_up", x, p["g_ffn1"], wup1)
    wdn1 = weights("ffn1_dn", h1)["wdn1"]
    x1 = _ffn_down("ffn1_down", a1, wdn1, x)
    wm = weights("mix", x1)
    z, h2 = _mix_proj(x1, p["g_mix"], wm["wz"])
    qf, ka, va, yg = _mix_prep(z, bf128, p["g_q"], p["g_k"], p["g_sgu"], p["w_s"], b_st, p["g_gmlp_o"])
    attn, lse = _fox_fwd(qf, ka, va)
    x2 = _mix_out(attn, yg, p["g_fox_o"], wm["wout"], x1)
    wc = weights("ca", x2)
    mn, kraw, ckn, cvv = _ca_kv(mem, p["g_mem"], wc["wckv"], p["g_ck"])
    x3 = _ca_fwd(x2, p["g_ca"], wc["wcq"], p["g_cq"], ckn, cvv, wc["wco"])
    w2 = weights("ffn2", x3)
    a2, h4 = _ffn_up("ffn2_up", x3, p["g_ffn2"], w2["wup2"])
    dy4, dy4b, sq = _ffn_down_loss("ffn2_down", a2, w2["wdn2"], x3, target)

    gs = {}
    dgu2 = _ffn_bwd_act("ffn2_bwd_act", dy4b, h4, w2["wup2"], w2["wdn2"])
    tok = emit("ffn2", {"wup2": _ffn_dwup("ffn2", h4, dgu2), "wdn2": _ffn_dwdn("ffn2", a2, dy4b)})
    dx3, gs["g_ffn2"] = _ffn_dx("ffn2_dx", dgu2, w2["wup2"], x3, _after(p["g_ffn2"], tok), dy4)

    dx2, dwcq, dwco, dckn, dcvv, gs["g_cq"], gs["g_ca"] = _ca_bwd(
        x2, dx3, p["g_ca"], wc["wcq"], p["g_cq"], ckn, cvv, wc["wco"])
    dwckv, gs["g_ck"], gs["g_mem"] = _ca_kv_bwd(mem, p["g_mem"], mn, kraw, dckn, dcvv, wc["wckv"], p["g_ck"])

    qb, dob, dyg, dwout, gs["g_fox_o"] = _mix_out_bwd(dx2, attn, yg, p["g_fox_o"], wm["wout"], qf, lse)
    dq, dk, dv = _fox_bwd(qb, ka, va, dob)
    dz, gs["g_q"], gs["g_k"], gs["g_sgu"], gs["g_gmlp_o"], gs["w_s"], dbst, dbf = _mix_prep_bwd(
        z, dq, dk, dv, dyg, bf128, p["g_q"], p["g_k"], p["g_sgu"], p["w_s"], b_st, p["g_gmlp_o"])
    gs["b_s"] = dbst.T
    gs["b_f"] = dbf[:, :FOX_HEADS]
    tok_ws = emit("w_s", {"w_s": gs["w_s"]})
    zb = ZW // 3
    dwz = _tn_matmul("mix_dwz", dz, pl.BlockSpec((T, zb), lambda j: (0, j)), h2,
                     S((ZW, D), BF), pl.BlockSpec((zb, D), lambda j: (j, 0)), 3)
    tok = emit("mid", {"wcq": dwcq, "wco": dwco, "wckv": dwckv, "wout": dwout, "wz": dwz})
    dx1, dx1b, gs["g_mix"] = _mix_proj_bwd(dz, wm["wz"], x1, _after(_after(p["g_mix"], tok), tok_ws), dx2)

    dgu1 = _ffn_bwd_act("ffn1_bwd_act", dx1b, h1, wup1, wdn1)
    tok = emit("ffn1_dn", {"wdn1": _ffn_dwdn("ffn1", a1, dx1b)})
    tok = emit("ffn1_up", {"wup1": _ffn_dwup("ffn1", h1, dgu1, after=tok)})
    dx0, gs["g_ffn1"] = _ffn_dx("ffn1_dx", dgu1, wup1, x, _after(p["g_ffn1"], tok), dx1)
    return sq, dx0, gs


MESH = pl.DeviceIdType.MESH
HBM_SPEC = pl.BlockSpec(memory_space=pltpu.HBM)
N_PEER = N_DEV - 1


def _place():
    return lax.axis_index("x"), lax.axis_index("y"), lax.axis_index("c")


def _slot(px, py, pc):
    return 4 * px + 2 * py + pc


SEM_SPEC = pl.BlockSpec(memory_space=pltpu.SEMAPHORE)
ANY_SPEC = pl.BlockSpec(memory_space=pl.ANY)
DATAFLOW = pltpu.SideEffectType.DATAFLOW_SIDE_EFFECTING


def _hbm(a):
    return pltpu.with_memory_space_constraint(a, pltpu.HBM)


def _peer(x, y, c, r):
    return (1 - x if r & 4 else x, 1 - y if r & 2 else y, 1 - c if r & 1 else c)


def _place_own(srcs, whole):
    my = _slot(*_place())
    lands = []
    for s in srcs:
        blk = s[None] if whole else lax.dynamic_slice_in_dim(s, my, 1, 0)
        shape = (N_DEV,) + s.shape if whole else s.shape
        lands.append(lax.dynamic_update_slice_in_dim(lax.empty(shape, s.dtype), blk, my, 0))
    return lands


ALL_PEERS = tuple(range(1, N_DEV))
NEAR_PEERS = (1, 2, 4, 6)
SAME_CORE = (2, 4, 6)


def _copy_start(name, srcs, lands, whole, peers=None):
    n = len(srcs)
    peers = peers or [ALL_PEERS] * n

    def body(*refs):
        src, land = refs[:n], refs[n:2 * n]
        send, recv = refs[2 * n:3 * n], refs[3 * n:4 * n]
        token = refs[6 * n]
        x, y, c = _place()
        my = _slot(x, y, c)
        for a in range(n):
            for r in peers[a]:
                p = _peer(x, y, c, r)
                pltpu.make_async_remote_copy(
                    src_ref=src[a] if whole else src[a].at[_slot(*p)], dst_ref=land[a].at[my],
                    send_sem=send[a].at[r - 1], recv_sem=recv[a].at[r - 1], device_id=p, device_id_type=MESH).start()
        token[...] = jnp.zeros_like(token)

    out = pl.pallas_call(
        body, name=name,
        out_shape=([pltpu.SemaphoreType.DMA((N_PEER,))] * (2 * n)
                   + [pltpu.HBM(s.shape, s.dtype) for s in srcs] + [pltpu.HBM(s.shape, s.dtype) for s in lands]
                   + [S((8, LANES), F32)]),
        in_specs=[HBM_SPEC] * (2 * n),
        out_specs=[SEM_SPEC] * (2 * n) + [HBM_SPEC] * (2 * n) + [pl.BlockSpec(memory_space=pltpu.VMEM)],
        input_output_aliases={i: 2 * n + i for i in range(2 * n)},
        compiler_params=pltpu.CompilerParams(has_side_effects=DATAFLOW),
    )(*[_hbm(s) for s in srcs], *[_hbm(s) for s in lands])
    return out[:n], out[n:2 * n], out[2 * n:3 * n], out[3 * n:4 * n], out[4 * n]


def _copy_wait(name, srcs, lands, send, recv, after, whole, peers=None, with_srcs=False):
    n = len(srcs)
    peers = peers or [ALL_PEERS] * n

    def body(*refs):
        src, land = refs[:n], refs[n:2 * n]
        snd, rcv = refs[2 * n:3 * n], refs[3 * n:4 * n]
        x, y, c = _place()
        for a in range(n):
            for r in peers[a]:
                p = _peer(x, y, c, r)
                ps = _slot(*p)
                cp = pltpu.make_async_remote_copy(
                    src_ref=src[a] if whole else src[a].at[ps], dst_ref=land[a].at[ps],
                    send_sem=snd[a].at[r - 1], recv_sem=rcv[a].at[r - 1], device_id=p, device_id_type=MESH)
                cp.wait_send()
                cp.wait_recv()

    out = pl.pallas_call(
        body, name=name,
        out_shape=[pltpu.HBM(s.shape, s.dtype) for s in srcs] + [pltpu.HBM(s.shape, s.dtype) for s in lands],
        in_specs=[HBM_SPEC] * (2 * n) + [SEM_SPEC] * (2 * n) + [ANY_SPEC],
        out_specs=[HBM_SPEC] * (2 * n),
        input_output_aliases={i: i for i in range(2 * n)},
        compiler_params=pltpu.CompilerParams(has_side_effects=DATAFLOW),
    )(*srcs, *lands, *send, *recv, after)
    return (out[:n], out[n:]) if with_srcs else out[n:]


def _forward_start(name, lands):
    n = len(lands)

    def body(*refs):
        land = refs[:n]
        send, recv = refs[n:2 * n], refs[2 * n:3 * n]
        token = refs[4 * n]
        x, y, c = _place()
        for a in range(n):
            for r in SAME_CORE:
                blk = land[a].at[_slot(*_peer(x, y, c, r))]
                pltpu.make_async_remote_copy(
                    src_ref=blk, dst_ref=blk, send_sem=send[a].at[r - 1], recv_sem=recv[a].at[r - 1],
                    device_id=(x, y, 1 - c), device_id_type=MESH).start()
        token[...] = jnp.zeros_like(token)

    out = pl.pallas_call(
        body, name=name,
        out_shape=([pltpu.SemaphoreType.DMA((N_PEER,))] * (2 * n) + [pltpu.HBM(s.shape, s.dtype) for s in lands]
                   + [S((8, LANES), F32)]),
        in_specs=[HBM_SPEC] * n,
        out_specs=[SEM_SPEC] * (2 * n) + [HBM_SPEC] * n + [pl.BlockSpec(memory_space=pltpu.VMEM)],
        input_output_aliases={i: 2 * n + i for i in range(n)},
        compiler_params=pltpu.CompilerParams(has_side_effects=DATAFLOW),
    )(*[_hbm(s) for s in lands])
    return out[:n], out[n:2 * n], out[2 * n:3 * n], out[3 * n]


def _forward_wait(name, lands, send, recv, after):
    n = len(lands)

    def body(*refs):
        land = refs[:n]
        snd, rcv = refs[n:2 * n], refs[2 * n:3 * n]
        x, y, c = _place()
        for a in range(n):
            for r in SAME_CORE:
                cp = pltpu.make_async_remote_copy(
                    src_ref=land[a].at[_slot(*_peer(x, y, c, r))], dst_ref=land[a].at[_slot(*_peer(x, y, c, r | 1))],
                    send_sem=snd[a].at[r - 1], recv_sem=rcv[a].at[r - 1], device_id=(x, y, 1 - c),
                    device_id_type=MESH)
                cp.wait_send()
                cp.wait_recv()

    return pl.pallas_call(
        body, name=name,
        out_shape=[pltpu.HBM(s.shape, s.dtype) for s in lands],
        in_specs=[HBM_SPEC] * n + [SEM_SPEC] * (2 * n) + [ANY_SPEC],
        out_specs=[HBM_SPEC] * n,
        input_output_aliases={i: i for i in range(n)},
        compiler_params=pltpu.CompilerParams(has_side_effects=DATAFLOW),
    )(*lands, *send, *recv, after)


def _adamw(w, g, m, v):
    m2 = ADAM_B1 * m + (1.0 - ADAM_B1) * g
    v2 = ADAM_B2 * v + (1.0 - ADAM_B2) * (g * g)
    m_hat = m2 / (1.0 - ADAM_B1 ** ADAM_STEP)
    v_hat = v2 / (1.0 - ADAM_B2 ** ADAM_STEP)
    delta = -ADAM_LR * (m_hat / (jnp.sqrt(v_hat) + ADAM_EPS) + ADAM_WD * w)
    return delta, m2, v2


def _adamw_big(name, slots, w, m, v, own=None):
    R, C = w.shape
    tr = next((t for t in (256, 352) if R % t == 0), R)

    def finish(g, w_ref, m_ref, v_ref, g_ref, d_ref, m2_ref, v2_ref):
        d, m2, v2 = _adamw(w_ref[...], g, m_ref[...], v_ref[...])
        g_ref[...] = g
        d_ref[...] = d
        m2_ref[...] = m2
        v2_ref[...] = v2

    if own is None:
        def body(s_ref, *refs):
            g = s_ref[0].astype(F32)
            for k in range(1, N_DEV):
                g = g + s_ref[k].astype(F32)
            finish(g, *refs)

        row = pl.BlockSpec((tr, C), lambda i: (i, 0))
        return pl.pallas_call(
            body, name=name, grid=(R // tr,),
            in_specs=[pl.BlockSpec((N_DEV, tr, C), lambda i: (0, i, 0)), row, row, row],
            out_specs=[row] * 4, out_shape=[S((R, C), F32)] * 4,
            compiler_params=_cp(1))(slots, w, m, v)

    def body(my_ref, s_ref, own_ref, *refs):
        mine = own_ref[...]
        g = None
        for k in range(N_DEV):
            part = jnp.where(my_ref[0] == k, mine, s_ref[k]).astype(F32)
            g = part if g is None else g + part
        finish(g, *refs)

    row = pl.BlockSpec((tr, C), lambda i, my_ref: (i, 0))
    my = jnp.reshape(_slot(*_place()), (1,)).astype(jnp.int32)
    return pl.pallas_call(
        body, name=name,
        grid_spec=pltpu.PrefetchScalarGridSpec(
            num_scalar_prefetch=1, grid=(R // tr,),
            in_specs=[pl.BlockSpec((N_DEV, tr, C), lambda i, my_ref: (0, i, 0)),
                      pl.BlockSpec((None, tr, C), lambda i, my_ref: (my_ref[0], i, 0)), row, row, row],
            out_specs=[row] * 4),
        out_shape=[S((R, C), F32)] * 4, compiler_params=_cp(1))(my, slots, own, w, m, v)


TINY_ROWS = (("b_s", 8), ("g_ffn1", 8), ("g_mix", 8), ("g_ca", 8), ("g_mem", 8), ("g_ffn2", 8), ("g_sgu", 4),
             ("g_fox_o", 4), ("g_gmlp_o", 4), ("g_cq", 2), ("g_ck", 2), ("g_q", 1), ("g_k", 1), ("b_f", 1),
             ("loss", 1))
TINY_P = 72


def _tiny_pieces(width):
    return [(j, slice(j * LANES, min((j + 1) * LANES, width))) for j in range(-(-width // LANES))]


def _pack_tiny(grads, sq):
    names = [n for n, _ in TINY_ROWS if n != "loss"]

    def body(*refs):
        ins, sq_ref, o_ref = refs[:len(names)], refs[len(names)], refs[len(names) + 1]
        o_ref[...] = jnp.zeros_like(o_ref)
        at = 0
        for ref, (name, r) in zip(ins, TINY_ROWS):
            if name == "b_s":
                o_ref[at:at + r, :] = ref[...]
            else:
                for j, cols in _tiny_pieces(ref.shape[1]):
                    o_ref[at + j:at + j + 1, 0:cols.stop - cols.start] = ref[:, cols]
            at += r
        o_ref[at:at + 1, :] = sq_ref[0:1, :]

    return pl.pallas_call(body, name="tiny_pack", out_shape=S((TINY_P, LANES), F32))(
        *[grads[n] for n in names], sq)


def _adamw_tiny(slots, w, m, v):
    names = [n for n, _ in TINY_ROWS if n != "loss"]
    k = len(names)

    def body(s_ref, *refs):
        ins, outs, loss_ref = refs[:3 * k], refs[3 * k:7 * k], refs[7 * k]
        g_all = s_ref[0]
        for d in range(1, N_DEV):
            g_all = g_all + s_ref[d]
        at = 0
        for i, (name, r) in enumerate(TINY_ROWS[:k]):
            w_ref, m_ref, v_ref = ins[i], ins[k + i], ins[2 * k + i]
            o = outs[4 * i:4 * i + 4]
            if name == "b_s":
                pieces = [(slice(at, at + r), slice(0, LANES), (slice(None), slice(None)))]
            else:
                pieces = [(slice(at + j, at + j + 1), slice(0, c.stop - c.start), (slice(None), c))
                          for j, c in _tiny_pieces(w_ref.shape[1])]
            for rows, lanes, dst in pieces:
                g = g_all[rows, lanes]
                res = (g,) + _adamw(w_ref[dst], g, m_ref[dst], v_ref[dst])
                for ref, val in zip(o, res):
                    ref[dst] = val
            at += r
        loss_ref[...] = g_all[at:at + 1, :]

    shapes = [S(w[n].shape, F32) for n in names]
    out = pl.pallas_call(
        body, name="adamw_tiny", out_shape=[s for s in shapes for _ in range(4)] + [S((1, LANES), F32)],
    )(slots, *[w[n] for n in names], *[m[n] for n in names], *[v[n] for n in names])
    stores = ({}, {}, {}, {})
    for i, n in enumerate(names):
        for store, t in zip(stores, out[4 * i:4 * i + 4]):
            store[n] = t
    return stores, out[4 * k]


WEIGHTS =('g_ffn1', 'w_ffn1_in', 'w_ffn1_out', 'g_mix', 'w_in', 'b_f', 'g_q', 'g_k', 'g_sgu', 'w_s', 'b_s',
           'g_fox_o', 'g_gmlp_o', 'w_out', 'g_ca', 'g_mem', 'w_cq', 'w_ckv', 'g_cq', 'g_ck', 'w_co', 'g_ffn2',
           'w_ffn2_in', 'w_ffn2_out')
BIG = ('w_ffn1_in', 'w_ffn1_out', 'w_in', 'w_out', 'w_cq', 'w_ckv', 'w_co', 'w_ffn2_in', 'w_ffn2_out')
TRANSPOSED = ('w_ffn1_in', 'w_in', 'w_ffn2_in')
TWO_LEVEL = ('w_ffn1_in', 'w_in')
GATHER_GROUPS = {"ffn1_up": ("w_ffn1_in",), "ffn1_dn": ("w_ffn1_out",), "mix": ("w_in", "w_out"),
                 "ca": ("w_cq", "w_ckv", "w_co"), "ffn2": ("w_ffn2_in", "w_ffn2_out")}
QKV_W = 3 * FOX_W
UV_OFF = QKV_W + FOX_HEADS


def kernel(x, mem, g_ffn1, w_ffn1_in, w_ffn1_out, g_mix, w_in, b_f, g_q, g_k, g_sgu, w_s, b_s, g_fox_o, g_gmlp_o, w_out, g_ca, g_mem, w_cq, w_ckv, g_cq, g_ck, w_co, g_ffn2, w_ffn2_in, w_ffn2_out, loss_target, m_g_ffn1, m_w_ffn1_in, m_w_ffn1_out, m_g_mix, m_w_in, m_b_f, m_g_q, m_g_k, m_g_sgu, m_w_s, m_b_s, m_g_fox_o, m_g_gmlp_o, m_w_out, m_g_ca, m_g_mem, m_w_cq, m_w_ckv, m_g_cq, m_g_ck, m_w_co, m_g_ffn2, m_w_ffn2_in, m_w_ffn2_out, v_g_ffn1, v_w_ffn1_in, v_w_ffn1_out, v_g_mix, v_w_in, v_b_f, v_g_q, v_g_k, v_g_sgu, v_w_s, v_b_s, v_g_fox_o, v_g_gmlp_o, v_w_out, v_g_ca, v_g_mem, v_w_cq, v_w_ckv, v_g_cq, v_g_ck, v_w_co, v_g_ffn2, v_w_ffn2_in, v_w_ffn2_out):
    args = dict(locals())
    w = {n: args[n] for n in WEIGHTS}
    mo = {n: args["m_" + n] for n in WEIGHTS}
    vo = {n: args["v_" + n] for n in WEIGHTS}
    D = D_MODEL

    def local(n, a):
        return a[0].T if n in TRANSPOSED else a[0]

    shards = [local(n, w[n]).astype(BF) for n in BIG]
    fb = shards[0].shape[0]
    g_peers = [NEAR_PEERS if n in TWO_LEVEL else ALL_PEERS for n in BIG]
    g_snd, g_rcv, g_src, g_land, g_token = _copy_start("gather_start", shards, _place_own(shards, True), True,
                                                       peers=g_peers)
    handles = {n: (g_src[i], g_land[i], g_snd[i], g_rcv[i]) for i, n in enumerate(BIG)}

    tiny_names = [n for n, _ in TINY_ROWS if n != "loss"]

    def weights(group, after):
        names = GATHER_GROUPS[group]
        hs = [handles[n] for n in names]
        got = list(_copy_wait("gather_wait_" + group, [h[0] for h in hs], [h[1] for h in hs], [h[2] for h in hs],
                              [h[3] for h in hs], after, True, peers=[g_peers[BIG.index(n)] for n in names]))
        passed = [i for i, n in enumerate(names) if n in TWO_LEVEL]
        if passed:
            f_snd, f_rcv, f_land, f_token = _forward_start("gather_pass_start_" + group, [got[i] for i in passed])
            for i, t in zip(passed, _forward_wait("gather_pass_wait_" + group, f_land, f_snd, f_rcv, f_token)):
                got[i] = t
        got = dict(zip(names, got))
        if group == "ffn1_up":
            return {"wup1": got["w_ffn1_in"].reshape(2, N_FFN_BLK, fb, D)}
        if group == "ffn1_dn":
            return {"wdn1": got["w_ffn1_out"].reshape(N_FFN_BLK, fb, D)}
        if group == "mix":
            full = got["w_in"].reshape(-1, D)
            wz = jnp.concatenate([full[:QKV_W], full[UV_OFF:], full[QKV_W:UV_OFF],
                                  jnp.zeros((LANES - FOX_HEADS, D), BF)], axis=0)
            return {"wz": wz, "wout": got["w_out"].reshape(D, D)}
        if group == "ca":
            return {"wcq": got["w_cq"].reshape(D, D), "wco": got["w_co"].reshape(D, D), "wckv": got["w_ckv"]}
        return {"wup2": got["w_ffn2_in"].reshape(2, N_FFN_BLK, fb, D),
                "wdn2": got["w_ffn2_out"].reshape(N_FFN_BLK, fb, D)}

    flying = {}

    def emit(group, g):
        if group == "w_s":
            part = [g["w_s"].reshape(-1, LANES)]
            *copies, token = _copy_start("w_s_start", part, _place_own(part, True), True)
            flying[group] = copies
            return token
        if group == "ffn2":
            parts = {"w_ffn2_in": g["wup2"], "w_ffn2_out": g["wdn2"].reshape(N_DEV, -1, D)}
        elif group == "ffn1_dn":
            parts = {"w_ffn1_out": g["wdn1"].reshape(N_DEV, -1, D)}
        elif group == "ffn1_up":
            parts = {"w_ffn1_in": g["wup1"]}
        else:
            gz = g["wz"]
            g_in = jnp.concatenate([gz[:QKV_W], gz[Z_F:Z_F + FOX_HEADS], gz[QKV_W:Z_F]], axis=0)
            parts = {"w_in": g_in.reshape(N_DEV, -1, D).astype(BF),
                     "w_out": g["wout"].reshape(N_DEV, -1, D), "w_cq": g["wcq"].reshape(N_DEV, -1, D),
                     "w_co": g["wco"].reshape(N_DEV, -1, D), "w_ckv": g["wckv"]}
        names = list(parts)
        srcs = [parts[n] for n in names]
        *copies, token = _copy_start("exchange_start_" + group, srcs, [lax.empty(s.shape, s.dtype) for s in srcs],
                                     False)
        flying[group] = (names, copies)
        return token

    small = {n: (w[n][0] if n == "b_s" else w[n]) for n in tiny_names}
    small["w_s"] = w["w_s"][0]

    sq, dx0, gs = _local_step(x[0], mem[0], loss_target[0], small, weights, emit)

    sm_parts = [_pack_tiny(gs, sq)]
    sm_snd, sm_rcv, sm_src, sm_land, sm_token = _copy_start("tiny_start", sm_parts, _place_own(sm_parts, True), True)

    grad, delta, new_m, new_v = {}, {}, {}, {}

    def update(group, after):
        names, (snd, rcv, srcs, lands) = flying[group]
        owns, slots = _copy_wait("exchange_wait_" + group, srcs, lands, snd, rcv, after, False, with_srcs=True)
        for n, sl, own in zip(names, slots, owns):
            g, d, m2, v2 = _adamw_big("adamw_" + n, sl, local(n, w[n]), local(n, mo[n]), local(n, vo[n]), own=own)
            grad[n], delta[n], new_m[n], new_v[n] = (
                (t.T if n in TRANSPOSED else t).reshape(w[n].shape) for t in (g, d, m2, v2))
        return d

    last = update("ffn2", sm_token)
    last = update("mid", last)
    last = update("ffn1_dn", last)
    last = update("ffn1_up", last)
    ws_snd, ws_rcv, ws_src, ws_land = flying["w_s"]
    ws_all, = _copy_wait("w_s_wait", ws_src, ws_land, ws_snd, ws_rcv, last, True)
    tiny_all, = _copy_wait("tiny_wait", sm_src, sm_land, sm_snd, sm_rcv, ws_all, True)
    ws_shape = w["w_s"].shape
    for store, t in zip((grad, delta, new_m, new_v), _adamw_big(
            "adamw_w_s", ws_all, *[a["w_s"].reshape(-1, LANES) for a in (w, mo, vo)])):
        store["w_s"] = t.reshape(ws_shape)
    stores, loss_row = _adamw_tiny(tiny_all, *[{n: (a[n][0] if n == "b_s" else a[n]) for n in tiny_names}
                                               for a in (w, mo, vo)])
    for store, t in zip((grad, delta, new_m, new_v), stores):
        store.update({n: v.reshape(w[n].shape) for n, v in t.items()})
    loss = loss_row[0, 0] * (0.5 / D)

    return (loss, dx0[None], *[grad[n] for n in WEIGHTS], *[delta[n] for n in WEIGHTS],
            *[new_m[n] for n in WEIGHTS], *[new_v[n] for n in WEIGHTS])
```

```python
import functools

import jax
import jax.numpy as jnp
from jax import lax
from jax.experimental import pallas as pl
from jax.experimental.pallas import tpu as pltpu

F32 = jnp.float32
BF = jnp.bfloat16
S = jax.ShapeDtypeStruct

N_DEV = 8
D_MODEL = 1024
FOX_HEADS, FOX_HD = 8, 64
FOX_W = 512
GMLP_G, GMLP_GD = 8, 64
GMLP_W = 512
CHUNK = 128
CA_HEADS, CA_HD = 4, 256
N_FFN_BLK = 4
ZW = 2688
Z_Q, Z_K, Z_V, Z_U, Z_G, Z_F = 0, 512, 1024, 1536, 2048, 2560
EPS = 1e-6
NEG = -1e30
LANES = 128

ADAM_LR, ADAM_B1, ADAM_B2, ADAM_EPS, ADAM_WD, ADAM_STEP = 0.001, 0.9, 0.999, 1e-08, 0.01, 10

VMEM_LIMIT = 52 * 2 ** 20


def _cp(n_axes):
    return pltpu.CompilerParams(dimension_semantics=("arbitrary",) * n_axes, vmem_limit_bytes=VMEM_LIMIT)


def _nn(a, b):
    return jnp.dot(a, b, preferred_element_type=F32)


def _nt(a, b):
    return lax.dot_general(a, b, (((1,), (1,)), ((), ())), preferred_element_type=F32)


def _tn(a, b):
    return lax.dot_general(a, b, (((0,), (0,)), ((), ())), preferred_element_type=F32)


def _hi(a, b):
    return jnp.dot(a, b, precision=lax.Precision.HIGHEST, preferred_element_type=F32)


def _rstd(x):
    return lax.rsqrt(jnp.mean(x * x, axis=-1, keepdims=True) + EPS)


def _norm_bwd(dy, x, g):
    r = _rstd(x)
    xh = x * r
    dxh = dy * g
    dx = r * (dxh - xh * jnp.mean(dxh * xh, axis=-1, keepdims=True))
    return dx, dy * xh


def _acc_rows(ref, first, val):
    srow = jnp.sum(val, axis=0, keepdims=True)

    @pl.when(first)
    def _():
        ref[...] = srow

    @pl.when(jnp.logical_not(first))
    def _():
        ref[...] += srow


def _gelu(x):
    c = 0.7978845608028654
    return 0.5 * x * (1.0 + jnp.tanh(c * (x + 0.044715 * x * x * x)))


def _gelu_grad(x):
    c = 0.7978845608028654
    t = jnp.tanh(c * (x + 0.044715 * x * x * x))
    return 0.5 * (1.0 + t) + 0.5 * x * (1.0 - t * t) * c * (1.0 + 3 * 0.044715 * x * x)


def _tile(n, pref):
    return pref if n % pref == 0 else n


def _ffn_up(name, x, g, wup):
    T, D = x.shape
    FB = wup.shape[-2]
    tm = _tile(T, 512)

    def body(x_ref, g_ref, w_ref, a_ref, h_ref):
        xf = x_ref[...]
        hb = (xf * _rstd(xf) * g_ref[...]).astype(BF)
        h_ref[...] = hb
        for j in range(N_FFN_BLK):
            gg = _nt(hb, w_ref[0, j])
            uu = _nt(hb, w_ref[1, j])
            a_ref[j] = (gg * jax.nn.sigmoid(gg) * uu).astype(BF)

    return pl.pallas_call(
        body, name=name, grid=(T // tm,),
        in_specs=[pl.BlockSpec((tm, D), lambda i: (i, 0)),
                  pl.BlockSpec((1, D), lambda i: (0, 0)),
                  pl.BlockSpec((2, N_FFN_BLK, FB, D), lambda i: (0, 0, 0, 0), pipeline_mode=pl.Buffered(1))],
        out_specs=[pl.BlockSpec((N_FFN_BLK, tm, FB), lambda i: (0, i, 0)),
                   pl.BlockSpec((tm, D), lambda i: (i, 0))],
        out_shape=[S((N_FFN_BLK, T, FB), BF), S((T, D), BF)],
        compiler_params=_cp(1))(x, g, wup)


def _ffn_down(name, a, wdn, x):
    _, T, FB = a.shape
    D = x.shape[1]
    tm = _tile(T, 512)

    def body(a_ref, w_ref, x_ref, o_ref):
        p = _nn(a_ref[0], w_ref[0])
        for j in range(1, N_FFN_BLK):
            p = p + _nn(a_ref[j], w_ref[j])
        o_ref[...] = x_ref[...] + 0.5 * p

    return pl.pallas_call(
        body, name=name, grid=(T // tm,),
        in_specs=[pl.BlockSpec((N_FFN_BLK, tm, FB), lambda i: (0, i, 0)),
                  pl.BlockSpec((N_FFN_BLK, FB, D), lambda i: (0, 0, 0)),
                  pl.BlockSpec((tm, D), lambda i: (i, 0))],
        out_specs=pl.BlockSpec((tm, D), lambda i: (i, 0)),
        out_shape=S((T, D), F32),
        compiler_params=_cp(1))(a, wdn, x)


def _ffn_down_loss(name, a, wdn, x, target):
    _, T, FB = a.shape
    D = x.shape[1]
    tm = _tile(T, 512)

    def body(a_ref, w_ref, x_ref, t_ref, d_ref, db_ref, loss_ref):
        i = pl.program_id(0)
        p = _nn(a_ref[0], w_ref[0])
        for j in range(1, N_FFN_BLK):
            p = p + _nn(a_ref[j], w_ref[j])
        diff = (x_ref[...] + 0.5 * p) - t_ref[...]
        dy = diff * (1.0 / D)
        d_ref[...] = dy
        db_ref[...] = dy.astype(BF)
        sq = jnp.zeros((8, LANES), F32) + jnp.sum(diff * diff)

        @pl.when(i == 0)
        def _():
            loss_ref[...] = sq

        @pl.when(i > 0)
        def _():
            loss_ref[...] += sq

    row = pl.BlockSpec((tm, D), lambda i: (i, 0))
    return pl.pallas_call(
        body, name=name, grid=(T // tm,),
        in_specs=[pl.BlockSpec((N_FFN_BLK, tm, FB), lambda i: (0, i, 0)),
                  pl.BlockSpec((N_FFN_BLK, FB, D), lambda i: (0, 0, 0)), row, row],
        out_specs=[row, row, pl.BlockSpec((8, LANES), lambda i: (0, 0))],
        out_shape=[S((T, D), F32), S((T, D), BF), S((8, LANES), F32)],
        compiler_params=_cp(1))(a, wdn, x, target)


def _ffn_bwd_act(name, dyb, h, wup, wdn):
    T, D = h.shape
    FB = wup.shape[-2]
    tm = _tile(T, 512)

    def body(d_ref, h_ref, wu_ref, wd_ref, o_ref):
        db = d_ref[...]
        hb = h_ref[...]
        for j in range(N_FFN_BLK):
            da = 0.5 * _nt(db, wd_ref[j])
            gg = _nt(hb, wu_ref[0, j])
            uu = _nt(hb, wu_ref[1, j])
            sg = jax.nn.sigmoid(gg)
            o_ref[0, j] = (da * uu * (sg * (1.0 + gg * (1.0 - sg)))).astype(BF)
            o_ref[1, j] = (da * (gg * sg)).astype(BF)

    return pl.pallas_call(
        body, name=name, grid=(T // tm,),
        in_specs=[pl.BlockSpec((tm, D), lambda i: (i, 0)),
                  pl.BlockSpec((tm, D), lambda i: (i, 0)),
                  pl.BlockSpec((2, N_FFN_BLK, FB, D), lambda i: (0, 0, 0, 0), pipeline_mode=pl.Buffered(1)),
                  pl.BlockSpec((N_FFN_BLK, FB, D), lambda i: (0, 0, 0), pipeline_mode=pl.Buffered(1))],
        out_specs=pl.BlockSpec((2, N_FFN_BLK, tm, FB), lambda i: (0, 0, i, 0)),
        out_shape=S((2, N_FFN_BLK, T, FB), BF),
        compiler_params=_cp(1))(dyb, h, wup, wdn)


def _ffn_dx(name, dgu, wup, x, g, dy):
    T, D = x.shape
    FB = wup.shape[-2]
    tm = _tile(T, 512)

    def body(d_ref, w_ref, x_ref, g_ref, dy_ref, dx_ref, dg_ref):
        p = None
        for j in range(N_FFN_BLK):
            for half in range(2):
                t = _nn(d_ref[half, j], w_ref[half, j])
                p = t if p is None else p + t
        dx, dgr = _norm_bwd(p, x_ref[...], g_ref[...])
        dx_ref[...] = dx + dy_ref[...]
        _acc_rows(dg_ref, pl.program_id(0) == 0, dgr)

    return pl.pallas_call(
        body, name=name, grid=(T // tm,),
        in_specs=[pl.BlockSpec((2, N_FFN_BLK, tm, FB), lambda i: (0, 0, i, 0)),
                  pl.BlockSpec((2, N_FFN_BLK, FB, D), lambda i: (0, 0, 0, 0), pipeline_mode=pl.Buffered(1)),
                  pl.BlockSpec((tm, D), lambda i: (i, 0)),
                  pl.BlockSpec((1, D), lambda i: (0, 0)),
                  pl.BlockSpec((tm, D), lambda i: (i, 0))],
        out_specs=[pl.BlockSpec((tm, D), lambda i: (i, 0)),
                   pl.BlockSpec((1, D), lambda i: (0, 0))],
        out_shape=[S((T, D), F32), S((1, D), F32)],
        compiler_params=_cp(1))(dgu, wup, x, g, dy)


def _tn_matmul(name, a, a_spec, b, out_shape, out_spec, n_blocks, scale=1.0, after=None):
    extra = [] if after is None else [after]

    def body(a_ref, b_ref, *rest):
        o_ref = rest[-1]
        o_ref[...] = (_tn(a_ref[...], b_ref[...]) * scale).astype(o_ref.dtype)

    return pl.pallas_call(
        body, name=name, grid=(n_blocks,),
        in_specs=[a_spec, pl.BlockSpec(b.shape, lambda j: (0, 0), pipeline_mode=pl.Buffered(1))]
        + [pl.BlockSpec((8, LANES), lambda j: (0, 0)) for _ in extra],
        out_specs=out_spec, out_shape=out_shape, compiler_params=_cp(1))(a, b, *extra)


def _ffn_dwup(name, h, dgu, after=None):
    T, D = h.shape
    FB = dgu.shape[-1]
    return _tn_matmul(
        name + "_dwup", dgu.reshape(2 * N_FFN_BLK, T, FB), pl.BlockSpec((None, T, FB), lambda j: (j, 0, 0)), h,
        S((2 * N_FFN_BLK, FB, D), BF), pl.BlockSpec((None, FB, D), lambda j: (j, 0, 0)), 2 * N_FFN_BLK,
        after=after)


def _ffn_dwdn(name, a, dyb):
    _, T, FB = a.shape
    D = dyb.shape[1]
    return _tn_matmul(
        name + "_dwdn", a, pl.BlockSpec((None, T, FB), lambda j: (j, 0, 0)), dyb,
        S((N_FFN_BLK, FB, D), BF), pl.BlockSpec((None, FB, D), lambda j: (j, 0, 0)), N_FFN_BLK, scale=0.5)


def _mix_proj(x, g, wz):
    T, D = x.shape
    tm = _tile(T, 512)

    def body(x_ref, g_ref, w_ref, z_ref, h_ref):
        xf = x_ref[...]
        hb = (xf * _rstd(xf) * g_ref[...]).astype(BF)
        h_ref[...] = hb
        z_ref[...] = _nt(hb, w_ref[...])

    return pl.pallas_call(
        body, name="mix_proj", grid=(T // tm,),
        in_specs=[pl.BlockSpec((tm, D), lambda i: (i, 0)),
                  pl.BlockSpec((1, D), lambda i: (0, 0)),
                  pl.BlockSpec((ZW, D), lambda i: (0, 0))],
        out_specs=[pl.BlockSpec((tm, ZW), lambda i: (i, 0)),
                   pl.BlockSpec((tm, D), lambda i: (i, 0))],
        out_shape=[S((T, ZW), F32), S((T, D), BF)],
        compiler_params=_cp(1))(x, g, wz)


def _tri(n, lower):
    r = lax.broadcasted_iota(jnp.int32, (n, n), 0)
    c = lax.broadcasted_iota(jnp.int32, (n, n), 1)
    return (r >= c) if lower else (r <= c)


def _spatial_mix(vgn_b, ws_ref, bst, tm):
    tril = _tri(CHUNK, True)
    wms = [jnp.where(tril, ws_ref[g], 0.0).astype(BF) for g in range(GMLP_G)]
    rows = []
    for c in range(tm // CHUNK):
        cols = []
        for g in range(GMLP_G):
            vs = vgn_b[c * CHUNK:(c + 1) * CHUNK, g * GMLP_GD:(g + 1) * GMLP_GD]
            cols.append(_nn(wms[g], vs) + bst[:, g:g + 1])
        rows.append(jnp.concatenate(cols, axis=1))
    return jnp.concatenate(rows, axis=0), wms


HB = 128
AUG_W = FOX_HEADS * HB
COL_A, COL_B, COL_C = 64, 67, 70


def _spread_matrix():
    r = jnp.arange(FOX_W)
    return (jnp.arange(AUG_W)[None, :] == ((r // FOX_HD) * HB + r % FOX_HD)[:, None]).astype(BF)


def _piece_matrix(col):
    r = jnp.arange(LANES)
    dst = jnp.where(r < 3 * FOX_HEADS, (r % FOX_HEADS) * HB + col + r // FOX_HEADS, -1)
    return (jnp.arange(AUG_W)[None, :] == dst[:, None]).astype(BF)


def _ones_row(cols):
    c = jnp.arange(AUG_W) % HB
    hit = functools.reduce(jnp.logical_or, [(c >= a) & (c < a + 3) for a in cols])
    return hit.astype(F32)[None, :]


def _pieces(x):
    lane = lax.broadcasted_iota(jnp.int32, x.shape, 1)
    x = jnp.where(lane < FOX_HEADS, x, 0.0)
    hi = x.astype(BF).astype(F32)
    r1 = x - hi
    mid = r1.astype(BF).astype(F32)
    lo = (r1 - mid).astype(BF).astype(F32)
    return (hi + pltpu.roll(mid, FOX_HEADS, 1) + pltpu.roll(lo, 2 * FOX_HEADS, 1)).astype(BF)


def _mix_prep(z, bf128, g_q, g_k, g_sgu, w_s, b_st, g_go):
    T = z.shape[0]
    tm = _tile(T, 512)
    spread, pc_q, pc_k = _spread_matrix(), _piece_matrix(COL_A), _piece_matrix(COL_B)
    one_q, one_k, one_v = _ones_row([COL_B]), _ones_row([COL_A, COL_C]), _ones_row([COL_A])

    def body(z_ref, bf_ref, gq_ref, gk_ref, gs_ref, ws_ref, bst_ref, go_ref, sp_ref, pq_ref, pk_ref, oq_ref, ok_ref,
             ov_ref, q_ref, k_ref, v_ref, y_ref, carry_ref, qn_sc, kn_sc):
        i = pl.program_id(0)

        @pl.when(i == 0)
        def _():
            carry_ref[...] = jnp.zeros_like(carry_ref)

        for h in range(FOX_HEADS):
            hs = slice(h * FOX_HD, (h + 1) * FOX_HD)
            qh = z_ref[:, Z_Q + h * FOX_HD:Z_Q + (h + 1) * FOX_HD]
            kh = z_ref[:, Z_K + h * FOX_HD:Z_K + (h + 1) * FOX_HD]
            qn_sc[:, hs] = (qh * _rstd(qh) * gq_ref[...] * 0.125).astype(BF)
            kn_sc[:, hs] = (kh * _rstd(kh) * gk_ref[...]).astype(BF)

        fl = z_ref[:, Z_F:Z_F + LANES] + bf_ref[...]
        logf = jnp.minimum(fl, 0.0) - jnp.log1p(jnp.exp(-jnp.abs(fl)))
        csum = _hi(_tri(tm, True).astype(F32), logf) + carry_ref[...]
        carry_ref[...] = csum[tm - 1:tm, :]
        sp = sp_ref[...]
        q_ref[...] = (_nn(qn_sc[...], sp) + _nn(_pieces(csum), pq_ref[...]) + oq_ref[...]).astype(BF)
        k_ref[...] = (_nn(kn_sc[...], sp) + _nn(_pieces(-csum), pk_ref[...]) + ok_ref[...]).astype(BF)
        v_ref[...] = (_nn(z_ref[:, Z_V:Z_V + FOX_W].astype(BF), sp) + ov_ref[...]).astype(BF)

        u = _gelu(z_ref[:, Z_U:Z_U + GMLP_W])
        vg = _gelu(z_ref[:, Z_G:Z_G + GMLP_W])
        vgn = (vg * _rstd(vg) * gs_ref[...]).astype(BF)
        mixed, _ = _spatial_mix(vgn, ws_ref, bst_ref[...], tm)
        sgu = u * mixed
        y_ref[...] = (sgu * _rstd(sgu) * go_ref[...]).astype(BF)

    row = lambda i: (i, 0)
    fix2 = lambda i: (0, 0)
    return pl.pallas_call(
        body, name="mix_prep", grid=(T // tm,),
        in_specs=[pl.BlockSpec((tm, ZW), row),
                  pl.BlockSpec((1, LANES), fix2), pl.BlockSpec((1, FOX_HD), fix2), pl.BlockSpec((1, FOX_HD), fix2),
                  pl.BlockSpec((1, GMLP_W), fix2), pl.BlockSpec((GMLP_G, CHUNK, CHUNK), lambda i: (0, 0, 0)),
                  pl.BlockSpec((CHUNK, GMLP_G), fix2), pl.BlockSpec((1, GMLP_W), fix2),
                  pl.BlockSpec((FOX_W, AUG_W), fix2), pl.BlockSpec((LANES, AUG_W), fix2),
                  pl.BlockSpec((LANES, AUG_W), fix2), pl.BlockSpec((1, AUG_W), fix2), pl.BlockSpec((1, AUG_W), fix2),
                  pl.BlockSpec((1, AUG_W), fix2)],
        out_specs=[pl.BlockSpec((tm, AUG_W), row), pl.BlockSpec((tm, AUG_W), row), pl.BlockSpec((tm, AUG_W), row),
                   pl.BlockSpec((tm, GMLP_W), row)],
        out_shape=[S((T, AUG_W), BF), S((T, AUG_W), BF), S((T, AUG_W), BF), S((T, GMLP_W), BF)],
        scratch_shapes=[pltpu.VMEM((1, LANES), F32), pltpu.VMEM((tm, FOX_W), BF), pltpu.VMEM((tm, FOX_W), BF)],
        compiler_params=_cp(1))(z, bf128, g_q, g_k, g_sgu, w_s, b_st, g_go, spread, pc_q, pc_k, one_q, one_k, one_v)


def _fox_fwd(q, k, v):
    T = q.shape[0]
    tq = _tile(T, 1024)
    nq = T // tq

    def body(q_ref, k_ref, v_ref, o_ref, lse_ref, m_sc, acc_sc):
        i, j = pl.program_id(0), pl.program_id(1)

        @pl.when(j == 0)
        def _():
            m_sc[...] = jnp.full(m_sc.shape, NEG, F32)
            acc_sc[...] = jnp.zeros_like(acc_sc)

        def step(masked):
            mask = _tri(tq, True) if masked else None
            for h in range(FOX_HEADS):
                hb = slice(h * HB, (h + 1) * HB)
                s = _nt(q_ref[:, hb], k_ref[:, hb])
                if masked:
                    s = jnp.where(mask, s, NEG)
                m_prev = m_sc[h]
                m_new = jnp.maximum(m_prev, jnp.broadcast_to(jnp.max(s, axis=1, keepdims=True), (tq, HB)))
                p = jnp.exp(s - jnp.tile(m_new, (1, tq // HB))).astype(BF)
                acc_sc[:, hb] = jnp.exp(m_prev - m_new) * acc_sc[:, hb] + _nn(p, v_ref[:, hb])
                m_sc[h] = m_new

        @pl.when(j < i)
        def _():
            step(False)

        @pl.when(j == i)
        def _():
            step(True)
            lse_ref[...] = jnp.zeros_like(lse_ref)
            for h in range(FOX_HEADS):
                l = acc_sc[:, h * HB + COL_A:h * HB + COL_A + 1]
                o_ref[:, h * FOX_HD:(h + 1) * FOX_HD] = acc_sc[:, h * HB:h * HB + FOX_HD] / l
                lse_ref[:, h:h + 1] = m_sc[h][:, 0:1] + jnp.log(l)

    qi = lambda i, j: (i, 0)
    kj = lambda i, j: (jnp.minimum(i, j), 0)
    return pl.pallas_call(
        body, name="fox_fwd", grid=(nq, nq),
        in_specs=[pl.BlockSpec((tq, AUG_W), qi), pl.BlockSpec((tq, AUG_W), kj), pl.BlockSpec((tq, AUG_W), kj)],
        out_specs=[pl.BlockSpec((tq, FOX_W), qi), pl.BlockSpec((tq, LANES), qi)],
        out_shape=[S((T, FOX_W), F32), S((T, LANES), F32)],
        scratch_shapes=[pltpu.VMEM((FOX_HEADS, tq, HB), F32), pltpu.VMEM((tq, AUG_W), F32)],
        compiler_params=_cp(2))(q, k, v)


def _fox_bwd(q, k, v, dob):
    T = q.shape[0]
    tq = _tile(T, 512)
    nq = T // tq
    half = AUG_W // 2
    hpg = FOX_HEADS // 2

    pairs = [(j, i) for j in range(nq) for i in range(j, nq)]
    jt = jnp.asarray([p[0] for p in pairs], jnp.int32)
    it = jnp.asarray([p[1] for p in pairs], jnp.int32)

    def body(jt_ref, it_ref, q_ref, k_ref, v_ref, do_ref, dq_ref, dk_ref, dv_ref, dq_sc):
        t = pl.program_id(1)
        j, i = jt_ref[t], it_ref[t]

        @pl.when(t == 0)
        def _():
            dq_sc[...] = jnp.zeros_like(dq_sc)

        @pl.when(i == j)
        def _():
            dk_ref[...] = jnp.zeros_like(dk_ref)
            dv_ref[...] = jnp.zeros_like(dv_ref)

        def step(masked):
            rows = pl.ds(pl.multiple_of(i * tq, tq), tq)
            mask = _tri(tq, True) if masked else None
            for h in range(hpg):
                hb = slice(h * HB, (h + 1) * HB)
                qh, kh, vh, doh = q_ref[:, hb], k_ref[:, hb], v_ref[:, hb], do_ref[:, hb]
                s = _nt(qh, kh)
                if masked:
                    s = jnp.where(mask, s, NEG)
                p = jnp.exp(s)
                dsb = (p * _nt(doh, vh)).astype(BF)
                dv_ref[:, hb] += _tn(p.astype(BF), doh)
                dk_ref[:, hb] += _tn(dsb, qh)
                dq_sc[rows, hb] += _nn(dsb, kh)

        @pl.when(i > j)
        def _():
            step(False)

        @pl.when(i == j)
        def _():
            step(True)
            dq_ref[...] = dq_sc[pl.ds(pl.multiple_of(j * tq, tq), tq), :]

    qi = pl.BlockSpec((tq, half), lambda g, t, jt_ref, it_ref: (it_ref[t], g))
    kj = pl.BlockSpec((tq, half), lambda g, t, jt_ref, it_ref: (jt_ref[t], g))
    return pl.pallas_call(
        body, name="fox_bwd",
        grid_spec=pltpu.PrefetchScalarGridSpec(
            num_scalar_prefetch=2, grid=(2, len(pairs)), in_specs=[qi, kj, kj, qi], out_specs=[kj, kj, kj],
            scratch_shapes=[pltpu.VMEM((T, half), F32)]),
        out_shape=[S((T, AUG_W), F32), S((T, AUG_W), F32), S((T, AUG_W), F32)],
        compiler_params=_cp(2))(jt, it, q, k, v, dob)


def _mix_out(attn, yg, g_fo, wout, x):
    T, D = x.shape
    tm = _tile(T, 1024)

    def body(a_ref, y_ref, g_ref, w_ref, x_ref, o_ref):
        at = a_ref[...]
        yf = (at * _rstd(at) * g_ref[...]).astype(BF)
        o_ref[...] = x_ref[...] + _nn(yf, w_ref[:FOX_W, :]) + _nn(y_ref[...], w_ref[FOX_W:, :])

    row = lambda i: (i, 0)
    return pl.pallas_call(
        body, name="mix_out", grid=(T // tm,),
        in_specs=[pl.BlockSpec((tm, FOX_W), row), pl.BlockSpec((tm, GMLP_W), row),
                  pl.BlockSpec((1, FOX_W), lambda i: (0, 0)), pl.BlockSpec((D, D), lambda i: (0, 0)),
                  pl.BlockSpec((tm, D), row)],
        out_specs=pl.BlockSpec((tm, D), row),
        out_shape=S((T, D), F32),
        compiler_params=_cp(1))(attn, yg, g_fo, wout, x)


def _mix_out_bwd(dx, attn, yg, g_fo, wout, qf, lse):
    T, D = dx.shape
    tm = _tile(T, 512)
    n = T // tm
    spread, pc_l, pc_d = _spread_matrix(), _piece_matrix(COL_C), _piece_matrix(COL_A)

    def body(dx_ref, a_ref, y_ref, g_ref, w_ref, qf_ref, lse_ref, sp_ref, pl_ref, pd_ref,
             qb_ref, dob_ref, dyg_ref, dw_ref, dg_ref, acc_ref, dsum_ref):
        i = pl.program_id(0)
        dxb = dx_ref[...].astype(BF)
        at = a_ref[...]
        yf = (at * _rstd(at) * g_ref[...]).astype(BF)
        dy = _nt(dxb, w_ref[...])
        p_top = _tn(yf, dxb)
        p_bot = _tn(y_ref[...], dxb)

        @pl.when(i == 0)
        def _():
            acc_ref[:FOX_W, :] = p_top
            acc_ref[FOX_W:, :] = p_bot

        @pl.when(i > 0)
        def _():
            acc_ref[:FOX_W, :] += p_top
            acc_ref[FOX_W:, :] += p_bot

        @pl.when(i == n - 1)
        def _():
            dw_ref[...] = acc_ref[...].astype(BF)

        dat, dgr = _norm_bwd(dy[:, :FOX_W], at, g_ref[...])
        _acc_rows(dg_ref, i == 0, dgr)
        dyg_ref[...] = dy[:, FOX_W:]
        prod = dat * at
        dsum_ref[...] = jnp.zeros_like(dsum_ref)
        for h in range(FOX_HEADS):
            dsum_ref[:, h:h + 1] = jnp.sum(prod[:, h * FOX_HD:(h + 1) * FOX_HD], axis=1, keepdims=True)
        dob_ref[...] = (_nn(dat.astype(BF), sp_ref[...]) + _nn(_pieces(-dsum_ref[...]), pd_ref[...])).astype(BF)
        qb_ref[...] = (qf_ref[...].astype(F32) + _nn(_pieces(-lse_ref[...]), pl_ref[...])).astype(BF)

    row = lambda i: (i, 0)
    fix = lambda i: (0, 0)
    return pl.pallas_call(
        body, name="mix_out_bwd", grid=(n,),
        in_specs=[pl.BlockSpec((tm, D), row), pl.BlockSpec((tm, FOX_W), row), pl.BlockSpec((tm, GMLP_W), row),
                  pl.BlockSpec((1, FOX_W), fix), pl.BlockSpec((D, D), fix), pl.BlockSpec((tm, AUG_W), row),
                  pl.BlockSpec((tm, LANES), row), pl.BlockSpec((FOX_W, AUG_W), fix), pl.BlockSpec((LANES, AUG_W), fix),
                  pl.BlockSpec((LANES, AUG_W), fix)],
        out_specs=[pl.BlockSpec((tm, AUG_W), row), pl.BlockSpec((tm, AUG_W), row), pl.BlockSpec((tm, GMLP_W), row),
                   pl.BlockSpec((D, D), fix), pl.BlockSpec((1, FOX_W), fix)],
        out_shape=[S((T, AUG_W), BF), S((T, AUG_W), BF), S((T, GMLP_W), F32), S((D, D), BF), S((1, FOX_W), F32)],
        scratch_shapes=[pltpu.VMEM((D, D), F32), pltpu.VMEM((tm, LANES), F32)],
        compiler_params=_cp(1))(dx, attn, yg, g_fo, wout, qf, lse, spread, pc_l, pc_d)


def _mix_prep_bwd(z, dq, dk, dv, dyg, bf128, g_q, g_k, g_sgu, w_s, b_st, g_go):
    T = z.shape[0]
    tm = _tile(T, 512)
    n = T // tm

    def body(z_ref, dq_ref, dk_ref, dv_ref, dyg_ref, bf_ref, gq_ref, gk_ref, gs_ref, ws_ref,
             bst_ref, go_ref, dz_ref, dgq_ref, dgk_ref, dgs_ref, dgo_ref, dws_ref, dbst_ref, dbf_ref, carry_ref):
        i = pl.program_id(0)
        first = i == 0

        @pl.when(first)
        def _():
            carry_ref[...] = jnp.zeros_like(carry_ref)

        lane = lax.broadcasted_iota(jnp.int32, (tm, LANES), 1)
        dc = jnp.zeros((tm, LANES), F32)
        gq_rows, gk_rows = [], []
        for h in range(FOX_HEADS):
            hp = slice(h * HB, h * HB + FOX_HD)
            dqh, gqr = _norm_bwd(dq_ref[:, hp] * 0.125, z_ref[:, Z_Q + h * FOX_HD:Z_Q + (h + 1) * FOX_HD], gq_ref[...])
            dkh, gkr = _norm_bwd(dk_ref[:, hp], z_ref[:, Z_K + h * FOX_HD:Z_K + (h + 1) * FOX_HD], gk_ref[...])
            dz_ref[:, Z_Q + h * FOX_HD:Z_Q + (h + 1) * FOX_HD] = dqh.astype(BF)
            dz_ref[:, Z_K + h * FOX_HD:Z_K + (h + 1) * FOX_HD] = dkh.astype(BF)
            dz_ref[:, Z_V + h * FOX_HD:Z_V + (h + 1) * FOX_HD] = dv_ref[:, hp].astype(BF)
            dch = dq_ref[:, h * HB + COL_A:h * HB + COL_A + 1] - dk_ref[:, h * HB + COL_B:h * HB + COL_B + 1]
            dc = jnp.where(lane == h, dch, dc)
            gq_rows.append(gqr)
            gk_rows.append(gkr)
        _acc_rows(dgq_ref, first, functools.reduce(lambda a, b: a + b, gq_rows))
        _acc_rows(dgk_ref, first, functools.reduce(lambda a, b: a + b, gk_rows))

        dlogf = _hi(_tri(tm, False).astype(F32), dc) + carry_ref[...]
        carry_ref[...] = dlogf[0:1, :]
        fl = z_ref[:, Z_F:Z_F + LANES] + bf_ref[...]
        lane = lax.broadcasted_iota(jnp.int32, (tm, LANES), 1)
        df = jnp.where(lane < FOX_HEADS, dlogf * jax.nn.sigmoid(-fl), 0.0)
        dz_ref[:, Z_F:Z_F + LANES] = df.astype(BF)
        _acc_rows(dbf_ref, first, df)

        u_pre = z_ref[:, Z_U:Z_U + GMLP_W]
        vg_pre = z_ref[:, Z_G:Z_G + GMLP_W]
        u = _gelu(u_pre)
        vg = _gelu(vg_pre)
        vgn = (vg * _rstd(vg) * gs_ref[...]).astype(BF)
        bst = bst_ref[...]
        mixed, wms = _spatial_mix(vgn, ws_ref, bst, tm)
        sgu = u * mixed
        dsgu, gor = _norm_bwd(dyg_ref[...], sgu, go_ref[...])
        _acc_rows(dgo_ref, first, gor)
        du = dsgu * mixed
        dmixed = dsgu * u
        dmb = dmixed.astype(BF)
        tril = _tri(CHUNK, True)
        dvgn_rows = []
        dws = [None] * GMLP_G
        dbs = [None] * GMLP_G
        for c in range(tm // CHUNK):
            cs = slice(c * CHUNK, (c + 1) * CHUNK)
            cols = []
            for g in range(GMLP_G):
                gs = slice(g * GMLP_GD, (g + 1) * GMLP_GD)
                dmc = dmb[cs, gs]
                pw = _nt(dmc, vgn[cs, gs])
                pb = jnp.sum(dmixed[cs, gs], axis=1, keepdims=True)
                dws[g] = pw if dws[g] is None else dws[g] + pw
                dbs[g] = pb if dbs[g] is None else dbs[g] + pb
                cols.append(_tn(wms[g], dmc))
            dvgn_rows.append(jnp.concatenate(cols, axis=1))
        dvgn = jnp.concatenate(dvgn_rows, axis=0)
        dbs_t = jnp.concatenate(dbs, axis=1)
        for g in range(GMLP_G):
            dwg = jnp.where(tril, dws[g], 0.0)

            @pl.when(first)
            def _():
                dws_ref[g] = dwg

            @pl.when(jnp.logical_not(first))
            def _():
                dws_ref[g] += dwg

        @pl.when(first)
        def _():
            dbst_ref[...] = dbs_t

        @pl.when(jnp.logical_not(first))
        def _():
            dbst_ref[...] += dbs_t

        dvg, gsr = _norm_bwd(dvgn, vg, gs_ref[...])
        _acc_rows(dgs_ref, first, gsr)
        dz_ref[:, Z_U:Z_U + GMLP_W] = (du * _gelu_grad(u_pre)).astype(BF)
        dz_ref[:, Z_G:Z_G + GMLP_W] = (dvg * _gelu_grad(vg_pre)).astype(BF)

    rev = lambda i: (n - 1 - i, 0)
    fix = lambda i: (0, 0)
    fix3 = lambda i: (0, 0, 0)
    return pl.pallas_call(
        body, name="mix_prep_bwd", grid=(n,),
        in_specs=[pl.BlockSpec((tm, ZW), rev), pl.BlockSpec((tm, AUG_W), rev), pl.BlockSpec((tm, AUG_W), rev),
                  pl.BlockSpec((tm, AUG_W), rev), pl.BlockSpec((tm, GMLP_W), rev),
                  pl.BlockSpec((1, LANES), fix), pl.BlockSpec((1, FOX_HD), fix), pl.BlockSpec((1, FOX_HD), fix),
                  pl.BlockSpec((1, GMLP_W), fix), pl.BlockSpec((GMLP_G, CHUNK, CHUNK), fix3),
                  pl.BlockSpec((CHUNK, GMLP_G), fix), pl.BlockSpec((1, GMLP_W), fix)],
        out_specs=[pl.BlockSpec((tm, ZW), rev), pl.BlockSpec((1, FOX_HD), fix), pl.BlockSpec((1, FOX_HD), fix),
                   pl.BlockSpec((1, GMLP_W), fix), pl.BlockSpec((1, GMLP_W), fix),
                   pl.BlockSpec((GMLP_G, CHUNK, CHUNK), fix3), pl.BlockSpec((CHUNK, GMLP_G), fix),
                   pl.BlockSpec((1, LANES), fix)],
        out_shape=[S((T, ZW), BF), S((1, FOX_HD), F32), S((1, FOX_HD), F32), S((1, GMLP_W), F32), S((1, GMLP_W), F32),
                   S((GMLP_G, CHUNK, CHUNK), F32), S((CHUNK, GMLP_G), F32), S((1, LANES), F32)],
        scratch_shapes=[pltpu.VMEM((1, LANES), F32)],
        compiler_params=_cp(1))(z, dq, dk, dv, dyg, bf128, g_q, g_k, g_sgu, w_s, b_st, g_go)


def _mix_proj_bwd(dz, wz, x, g, dy):
    T, D = x.shape
    tm = _tile(T, 512)

    def body(dz_ref, w_ref, x_ref, g_ref, dy_ref, dx_ref, dxb_ref, dg_ref):
        dh = _nn(dz_ref[...], w_ref[...])
        dx, dgr = _norm_bwd(dh, x_ref[...], g_ref[...])
        dx = dx + dy_ref[...]
        dx_ref[...] = dx
        dxb_ref[...] = dx.astype(BF)
        _acc_rows(dg_ref, pl.program_id(0) == 0, dgr)

    row = lambda i: (i, 0)
    fix = lambda i: (0, 0)
    return pl.pallas_call(
        body, name="mix_proj_bwd", grid=(T // tm,),
        in_specs=[pl.BlockSpec((tm, ZW), row), pl.BlockSpec((ZW, D), fix), pl.BlockSpec((tm, D), row),
                  pl.BlockSpec((1, D), fix), pl.BlockSpec((tm, D), row)],
        out_specs=[pl.BlockSpec((tm, D), row), pl.BlockSpec((tm, D), row), pl.BlockSpec((1, D), fix)],
        out_shape=[S((T, D), F32), S((T, D), BF), S((1, D), F32)],
        compiler_params=_cp(1))(dz, wz, x, g, dy)


def _ca_kv(mem, g_mem, wckv, g_ck):
    M, D = mem.shape

    def body(m_ref, g_ref, w_ref, gk_ref, mn_ref, kr_ref, kn_ref, v_ref):
        mf = m_ref[...]
        mn = (mf * _rstd(mf) * g_ref[...]).astype(BF)
        mn_ref[...] = mn
        for h in range(CA_HEADS):
            kr = _nn(mn, w_ref[h])
            kr_ref[h] = kr
            kn_ref[h] = (kr * _rstd(kr) * gk_ref[...]).astype(BF)
            v_ref[h] = _nn(mn, w_ref[CA_HEADS + h]).astype(BF)

    hd = (CA_HEADS, M, CA_HD)
    return pl.pallas_call(
        body, name="ca_kv", out_shape=[S((M, D), BF), S(hd, F32), S(hd, BF), S(hd, BF)],
        compiler_params=pltpu.CompilerParams(vmem_limit_bytes=VMEM_LIMIT))(mem, g_mem, wckv, g_ck)


def _ca_tile_fwd(xt, gca, wcq, gcq, kn_ref, v_ref):
    hb = (xt * _rstd(xt) * gca).astype(BF)
    qc = _nn(hb, wcq)
    qr, qn, ps = [], [], []
    for h in range(CA_HEADS):
        qh = qc[:, h * CA_HD:(h + 1) * CA_HD]
        qnh = (qh * _rstd(qh) * gcq * 0.0625).astype(BF)
        s = _nt(qnh, kn_ref[h])
        e = jnp.exp(s - jnp.max(s, axis=1, keepdims=True))
        ps.append(e / jnp.sum(e, axis=1, keepdims=True))
        qr.append(qh)
        qn.append(qnh)
    return hb, qr, qn, ps


def _ca_fwd(x, g_ca, wcq, g_cq, kn, vv, wco):
    T, D = x.shape
    M = kn.shape[1]
    tm = _tile(T, 1024)

    def body(x_ref, gca_ref, wcq_ref, gcq_ref, kn_ref, v_ref, wco_ref, o_ref, ob_sc):
        xt = x_ref[...]
        _, _, _, ps = _ca_tile_fwd(xt, gca_ref[...], wcq_ref[...], gcq_ref[...], kn_ref, v_ref)
        for h in range(CA_HEADS):
            ob_sc[:, h * CA_HD:(h + 1) * CA_HD] = _nn(ps[h].astype(BF), v_ref[h]).astype(BF)
        o_ref[...] = xt + _nn(ob_sc[...], wco_ref[...])

    row = lambda i: (i, 0)
    fix = lambda i: (0, 0)
    fix3 = lambda i: (0, 0, 0)
    return pl.pallas_call(
        body, name="ca_fwd", grid=(T // tm,),
        in_specs=[pl.BlockSpec((tm, D), row), pl.BlockSpec((1, D), fix), pl.BlockSpec((D, D), fix),
                  pl.BlockSpec((1, CA_HD), fix), pl.BlockSpec((CA_HEADS, M, CA_HD), fix3),
                  pl.BlockSpec((CA_HEADS, M, CA_HD), fix3), pl.BlockSpec((D, D), fix)],
        out_specs=pl.BlockSpec((tm, D), row), out_shape=S((T, D), F32),
        scratch_shapes=[pltpu.VMEM((tm, D), BF)],
        compiler_params=_cp(1))(x, g_ca, wcq, g_cq, kn, vv, wco)


def _ca_bwd(x, dy, g_ca, wcq, g_cq, kn, vv, wco):
    T, D = x.shape
    M = kn.shape[1]
    tm = _tile(T, 512)
    n = T // tm

    def body(x_ref, dy_ref, gca_ref, wcq_ref, gcq_ref, kn_ref, v_ref, wco_ref,
             dx_ref, dwq_ref, dwo_ref, dkn_ref, dv_ref, dgcq_ref, dgca_ref, aq_sc, ao_sc, ob_sc, dq_sc):
        i = pl.program_id(0)
        first = i == 0
        xt = x_ref[...]
        dyt = dy_ref[...]
        dyb = dyt.astype(BF)
        hb, qr, qn, ps = _ca_tile_fwd(xt, gca_ref[...], wcq_ref[...], gcq_ref[...], kn_ref, v_ref)
        do = _nt(dyb, wco_ref[...])
        gcq_rows = None
        for h in range(CA_HEADS):
            hs = slice(h * CA_HD, (h + 1) * CA_HD)
            p = ps[h]
            pb = p.astype(BF)
            ob_sc[:, hs] = _nn(pb, v_ref[h]).astype(BF)
            doh = do[:, hs].astype(BF)
            dp = _nt(doh, v_ref[h])
            ds = (p * (dp - jnp.sum(dp * p, axis=1, keepdims=True))).astype(BF)
            dvh = _tn(pb, doh)
            dkh = _tn(ds, qn[h])

            @pl.when(first)
            def _():
                dv_ref[h] = dvh
                dkn_ref[h] = dkh

            @pl.when(jnp.logical_not(first))
            def _():
                dv_ref[h] += dvh
                dkn_ref[h] += dkh

            dqn = _nn(ds, kn_ref[h]) * 0.0625
            dqh, gr = _norm_bwd(dqn, qr[h], gcq_ref[...])
            gcq_rows = gr if gcq_rows is None else gcq_rows + gr
            dq_sc[:, hs] = dqh.astype(BF)
        _acc_rows(dgcq_ref, first, gcq_rows)
        dqb = dq_sc[...]
        p_o = _tn(ob_sc[...], dyb)
        p_q = _tn(hb, dqb)

        @pl.when(first)
        def _():
            ao_sc[...] = p_o
            aq_sc[...] = p_q

        @pl.when(jnp.logical_not(first))
        def _():
            ao_sc[...] += p_o
            aq_sc[...] += p_q

        @pl.when(i == n - 1)
        def _():
            dwo_ref[...] = ao_sc[...].astype(BF)
            dwq_ref[...] = aq_sc[...].astype(BF)

        dh = _nt(dqb, wcq_ref[...])
        dx, gar = _norm_bwd(dh, xt, gca_ref[...])
        dx_ref[...] = dx + dyt
        _acc_rows(dgca_ref, first, gar)

    row = lambda i: (i, 0)
    fix = lambda i: (0, 0)
    fix3 = lambda i: (0, 0, 0)
    hd = (CA_HEADS, M, CA_HD)
    return pl.pallas_call(
        body, name="ca_bwd", grid=(n,),
        in_specs=[pl.BlockSpec((tm, D), row), pl.BlockSpec((tm, D), row), pl.BlockSpec((1, D), fix),
                  pl.BlockSpec((D, D), fix), pl.BlockSpec((1, CA_HD), fix), pl.BlockSpec(hd, fix3),
                  pl.BlockSpec(hd, fix3), pl.BlockSpec((D, D), fix)],
        out_specs=[pl.BlockSpec((tm, D), row), pl.BlockSpec((D, D), fix), pl.BlockSpec((D, D), fix),
                   pl.BlockSpec(hd, fix3), pl.BlockSpec(hd, fix3), pl.BlockSpec((1, CA_HD), fix),
                   pl.BlockSpec((1, D), fix)],
        out_shape=[S((T, D), F32), S((D, D), BF), S((D, D), BF), S(hd, F32), S(hd, F32), S((1, CA_HD), F32),
                   S((1, D), F32)],
        scratch_shapes=[pltpu.VMEM((D, D), F32), pltpu.VMEM((D, D), F32), pltpu.VMEM((tm, D), BF),
                        pltpu.VMEM((tm, D), BF)],
        compiler_params=_cp(1))(x, dy, g_ca, wcq, g_cq, kn, vv, wco)


def _ca_kv_bwd(mem, g_mem, mn, kraw, dkn, dvv, wckv, g_ck):
    M, D = mem.shape

    def body(m_ref, g_ref, mn_ref, kr_ref, dkn_ref, dv_ref, w_ref, gk_ref, dw_ref, dgk_ref, dgm_ref):
        mn = mn_ref[...]
        dmn = jnp.zeros((M, D), F32)
        gk_rows = None
        for h in range(CA_HEADS):
            dkr, gr = _norm_bwd(dkn_ref[h], kr_ref[h], gk_ref[...])
            gk_rows = gr if gk_rows is None else gk_rows + gr
            dkb = dkr.astype(BF)
            dvb = dv_ref[h].astype(BF)
            dw_ref[h] = _tn(mn, dkb).astype(BF)
            dw_ref[CA_HEADS + h] = _tn(mn, dvb).astype(BF)
            dmn = dmn + _nt(dkb, w_ref[h]) + _nt(dvb, w_ref[CA_HEADS + h])
        dgk_ref[...] = jnp.sum(gk_rows, axis=0, keepdims=True)
        mf = m_ref[...]
        dgm_ref[...] = jnp.sum(dmn * (mf * _rstd(mf)), axis=0, keepdims=True)

    return pl.pallas_call(
        body, name="ca_kv_bwd",
        out_shape=[S((2 * CA_HEADS, D, CA_HD), BF), S((1, CA_HD), F32), S((1, D), F32)],
        compiler_params=pltpu.CompilerParams(vmem_limit_bytes=VMEM_LIMIT))(mem, g_mem, mn, kraw, dkn, dvv, wckv, g_ck)


def _after(g, token):
    return g if token is None else g + token[0:1, 0:1]


def _local_step(x, mem, target, small, weights, emit):
    T, D = x.shape
    p = small
    bf128 = jnp.pad(p["b_f"], ((0, 0), (0, LANES - FOX_HEADS)))
    b_st = p["b_s"].T

    wup1 = weights("ffn1_up", x)["wup1"]
    a1, h1 = _ffn_up("ffn1_up", x, p["g_ffn1"], wup1)
    wdn1 = weights("ffn1_dn", h1)["wdn1"]
    x1 = _ffn_down("ffn1_down", a1, wdn1, x)
    wm = weights("mix", x1)
    z, h2 = _mix_proj(x1, p["g_mix"], wm["wz"])
    qf, ka, va, yg = _mix_prep(z, bf128, p["g_q"], p["g_k"], p["g_sgu"], p["w_s"], b_st, p["g_gmlp_o"])
    attn, lse = _fox_fwd(qf, ka, va)
    x2 = _mix_out(attn, yg, p["g_fox_o"], wm["wout"], x1)
    wc = weights("ca", x2)
    mn, kraw, ckn, cvv = _ca_kv(mem, p["g_mem"], wc["wckv"], p["g_ck"])
    x3 = _ca_fwd(x2, p["g_ca"], wc["wcq"], p["g_cq"], ckn, cvv, wc["wco"])
    w2 = weights("ffn2", x3)
    a2, h4 = _ffn_up("ffn2_up", x3, p["g_ffn2"], w2["wup2"])
    dy4, dy4b, sq = _ffn_down_loss("ffn2_down", a2, w2["wdn2"], x3, target)

    gs = {}
    dgu2 = _ffn_bwd_act("ffn2_bwd_act", dy4b, h4, w2["wup2"], w2["wdn2"])
    tok = emit("ffn2", {"wup2": _ffn_dwup("ffn2", h4, dgu2), "wdn2": _ffn_dwdn("ffn2", a2, dy4b)})
    dx3, gs["g_ffn2"] = _ffn_dx("ffn2_dx", dgu2, w2["wup2"], x3, _after(p["g_ffn2"], tok), dy4)

    dx2, dwcq, dwco, dckn, dcvv, gs["g_cq"], gs["g_ca"] = _ca_bwd(
        x2, dx3, p["g_ca"], wc["wcq"], p["g_cq"], ckn, cvv, wc["wco"])
    dwckv, gs["g_ck"], gs["g_mem"] = _ca_kv_bwd(mem, p["g_mem"], mn, kraw, dckn, dcvv, wc["wckv"], p["g_ck"])

    qb, dob, dyg, dwout, gs["g_fox_o"] = _mix_out_bwd(dx2, attn, yg, p["g_fox_o"], wm["wout"], qf, lse)
    dq, dk, dv = _fox_bwd(qb, ka, va, dob)
    dz, gs["g_q"], gs["g_k"], gs["g_sgu"], gs["g_gmlp_o"], gs["w_s"], dbst, dbf = _mix_prep_bwd(
        z, dq, dk, dv, dyg, bf128, p["g_q"], p["g_k"], p["g_sgu"], p["w_s"], b_st, p["g_gmlp_o"])
    gs["b_s"] = dbst.T
    gs["b_f"] = dbf[:, :FOX_HEADS]
    tok_ws = emit("w_s", {"w_s": gs["w_s"]})
    zb = ZW // 3
    dwz = _tn_matmul("mix_dwz", dz, pl.BlockSpec((T, zb), lambda j: (0, j)), h2,
                     S((ZW, D), BF), pl.BlockSpec((zb, D), lambda j: (j, 0)), 3)
    tok = emit("mid", {"wcq": dwcq, "wco": dwco, "wckv": dwckv, "wout": dwout, "wz": dwz})
    dx1, dx1b, gs["g_mix"] = _mix_proj_bwd(dz, wm["wz"], x1, _after(_after(p["g_mix"], tok), tok_ws), dx2)

    dgu1 = _ffn_bwd_act("ffn1_bwd_act", dx1b, h1, wup1, wdn1)
    tok = emit("ffn1_dn", {"wdn1": _ffn_dwdn("ffn1", a1, dx1b)})
    tok = emit("ffn1_up", {"wup1": _ffn_dwup("ffn1", h1, dgu1, after=tok)})
    dx0, gs["g_ffn1"] = _ffn_dx("ffn1_dx", dgu1, wup1, x, _after(p["g_ffn1"], tok), dx1)
    return sq, dx0, gs


MESH = pl.DeviceIdType.MESH
HBM_SPEC = pl.BlockSpec(memory_space=pltpu.HBM)
N_PEER = N_DEV - 1


def _place():
    return lax.axis_index("x"), lax.axis_index("y"), lax.axis_index("c")


def _slot(px, py, pc):
    return 4 * px + 2 * py + pc


SEM_SPEC = pl.BlockSpec(memory_space=pltpu.SEMAPHORE)
ANY_SPEC = pl.BlockSpec(memory_space=pl.ANY)
DATAFLOW = pltpu.SideEffectType.DATAFLOW_SIDE_EFFECTING


def _hbm(a):
    return pltpu.with_memory_space_constraint(a, pltpu.HBM)


def _peer(x, y, c, r):
    return (1 - x if r & 4 else x, 1 - y if r & 2 else y, 1 - c if r & 1 else c)


def _place_own(srcs, whole):
    my = _slot(*_place())
    lands = []
    for s in srcs:
        blk = s[None] if whole else lax.dynamic_slice_in_dim(s, my, 1, 0)
        shape = (N_DEV,) + s.shape if whole else s.shape
        lands.append(lax.dynamic_update_slice_in_dim(lax.empty(shape, s.dtype), blk, my, 0))
    return lands


ALL_PEERS = tuple(range(1, N_DEV))
NEAR_PEERS = (1, 2, 4, 6)
SAME_CORE = (2, 4, 6)


def _copy_start(name, srcs, lands, whole, peers=None):
    n = len(srcs)
    peers = peers or [ALL_PEERS] * n

    def body(*refs):
        src, land = refs[:n], refs[n:2 * n]
        send, recv = refs[2 * n:3 * n], refs[3 * n:4 * n]
        token = refs[6 * n]
        x, y, c = _place()
        my = _slot(x, y, c)
        for a in range(n):
            for r in peers[a]:
                p = _peer(x, y, c, r)
                pltpu.make_async_remote_copy(
                    src_ref=src[a] if whole else src[a].at[_slot(*p)], dst_ref=land[a].at[my],
                    send_sem=send[a].at[r - 1], recv_sem=recv[a].at[r - 1], device_id=p, device_id_type=MESH).start()
        token[...] = jnp.zeros_like(token)

    out = pl.pallas_call(
        body, name=name,
        out_shape=([pltpu.SemaphoreType.DMA((N_PEER,))] * (2 * n)
                   + [pltpu.HBM(s.shape, s.dtype) for s in srcs] + [pltpu.HBM(s.shape, s.dtype) for s in lands]
                   + [S((8, LANES), F32)]),
        in_specs=[HBM_SPEC] * (2 * n),
        out_specs=[SEM_SPEC] * (2 * n) + [HBM_SPEC] * (2 * n) + [pl.BlockSpec(memory_space=pltpu.VMEM)],
        input_output_aliases={i: 2 * n + i for i in range(2 * n)},
        compiler_params=pltpu.CompilerParams(has_side_effects=DATAFLOW),
    )(*[_hbm(s) for s in srcs], *[_hbm(s) for s in lands])
    return out[:n], out[n:2 * n], out[2 * n:3 * n], out[3 * n:4 * n], out[4 * n]


def _copy_wait(name, srcs, lands, send, recv, after, whole, peers=None, with_srcs=False):
    n = len(srcs)
    peers = peers or [ALL_PEERS] * n

    def body(*refs):
        src, land = refs[:n], refs[n:2 * n]
        snd, rcv = refs[2 * n:3 * n], refs[3 * n:4 * n]
        x, y, c = _place()
        for a in range(n):
            for r in peers[a]:
                p = _peer(x, y, c, r)
                ps = _slot(*p)
                cp = pltpu.make_async_remote_copy(
                    src_ref=src[a] if whole else src[a].at[ps], dst_ref=land[a].at[ps],
                    send_sem=snd[a].at[r - 1], recv_sem=rcv[a].at[r - 1], device_id=p, device_id_type=MESH)
                cp.wait_send()
                cp.wait_recv()

    out = pl.pallas_call(
        body, name=name,
        out_shape=[pltpu.HBM(s.shape, s.dtype) for s in srcs] + [pltpu.HBM(s.shape, s.dtype) for s in lands],
        in_specs=[HBM_SPEC] * (2 * n) + [SEM_SPEC] * (2 * n) + [ANY_SPEC],
        out_specs=[HBM_SPEC] * (2 * n),
        input_output_aliases={i: i for i in range(2 * n)},
        compiler_params=pltpu.CompilerParams(has_side_effects=DATAFLOW),
    )(*srcs, *lands, *send, *recv, after)
    return (out[:n], out[n:]) if with_srcs else out[n:]


def _forward_start(name, lands):
    n = len(lands)

    def body(*refs):
        land = refs[:n]
        send, recv = refs[n:2 * n], refs[2 * n:3 * n]
        token = refs[4 * n]
        x, y, c = _place()
        for a in range(n):
            for r in SAME_CORE:
                blk = land[a].at[_slot(*_peer(x, y, c, r))]
                pltpu.make_async_remote_copy(
                    src_ref=blk, dst_ref=blk, send_sem=send[a].at[r - 1], recv_sem=recv[a].at[r - 1],
                    device_id=(x, y, 1 - c), device_id_type=MESH).start()
        token[...] = jnp.zeros_like(token)

    out = pl.pallas_call(
        body, name=name,
        out_shape=([pltpu.SemaphoreType.DMA((N_PEER,))] * (2 * n) + [pltpu.HBM(s.shape, s.dtype) for s in lands]
                   + [S((8, LANES), F32)]),
        in_specs=[HBM_SPEC] * n,
        out_specs=[SEM_SPEC] * (2 * n) + [HBM_SPEC] * n + [pl.BlockSpec(memory_space=pltpu.VMEM)],
        input_output_aliases={i: 2 * n + i for i in range(n)},
        compiler_params=pltpu.CompilerParams(has_side_effects=DATAFLOW),
    )(*[_hbm(s) for s in lands])
    return out[:n], out[n:2 * n], out[2 * n:3 * n], out[3 * n]


def _forward_wait(name, lands, send, recv, after):
    n = len(lands)

    def body(*refs):
        land = refs[:n]
        snd, rcv = refs[n:2 * n], refs[2 * n:3 * n]
        x, y, c = _place()
        for a in range(n):
            for r in SAME_CORE:
                cp = pltpu.make_async_remote_copy(
                    src_ref=land[a].at[_slot(*_peer(x, y, c, r))], dst_ref=land[a].at[_slot(*_peer(x, y, c, r | 1))],
                    send_sem=snd[a].at[r - 1], recv_sem=rcv[a].at[r - 1], device_id=(x, y, 1 - c),
                    device_id_type=MESH)
                cp.wait_send()
                cp.wait_recv()

    return pl.pallas_call(
        body, name=name,
        out_shape=[pltpu.HBM(s.shape, s.dtype) for s in lands],
        in_specs=[HBM_SPEC] * n + [SEM_SPEC] * (2 * n) + [ANY_SPEC],
        out_specs=[HBM_SPEC] * n,
        input_output_aliases={i: i for i in range(n)},
        compiler_params=pltpu.CompilerParams(has_side_effects=DATAFLOW),
    )(*lands, *send, *recv, after)


def _adamw(w, g, m, v):
    m2 = ADAM_B1 * m + (1.0 - ADAM_B1) * g
    v2 = ADAM_B2 * v + (1.0 - ADAM_B2) * (g * g)
    m_hat = m2 / (1.0 - ADAM_B1 ** ADAM_STEP)
    v_hat = v2 / (1.0 - ADAM_B2 ** ADAM_STEP)
    delta = -ADAM_LR * (m_hat / (jnp.sqrt(v_hat) + ADAM_EPS) + ADAM_WD * w)
    return delta, m2, v2


def _adamw_big(name, slots, w, m, v, own=None):
    R, C = w.shape
    tr = next((t for t in (256, 352) if R % t == 0), R)

    def finish(g, w_ref, m_ref, v_ref, g_ref, d_ref, m2_ref, v2_ref):
        d, m2, v2 = _adamw(w_ref[...], g, m_ref[...], v_ref[...])
        g_ref[...] = g
        d_ref[...] = d
        m2_ref[...] = m2
        v2_ref[...] = v2

    if own is None:
        def body(s_ref, *refs):
            g = s_ref[0].astype(F32)
            for k in range(1, N_DEV):
                g = g + s_ref[k].astype(F32)
            finish(g, *refs)

        row = pl.BlockSpec((tr, C), lambda i: (i, 0))
        return pl.pallas_call(
            body, name=name, grid=(R // tr,),
            in_specs=[pl.BlockSpec((N_DEV, tr, C), lambda i: (0, i, 0)), row, row, row],
            out_specs=[row] * 4, out_shape=[S((R, C), F32)] * 4,
            compiler_params=_cp(1))(slots, w, m, v)

    def body(my_ref, s_ref, own_ref, *refs):
        mine = own_ref[...]
        g = None
        for k in range(N_DEV):
            part = jnp.where(my_ref[0] == k, mine, s_ref[k]).astype(F32)
            g = part if g is None else g + part
        finish(g, *refs)

    row = pl.BlockSpec((tr, C), lambda i, my_ref: (i, 0))
    my = jnp.reshape(_slot(*_place()), (1,)).astype(jnp.int32)
    return pl.pallas_call(
        body, name=name,
        grid_spec=pltpu.PrefetchScalarGridSpec(
            num_scalar_prefetch=1, grid=(R // tr,),
            in_specs=[pl.BlockSpec((N_DEV, tr, C), lambda i, my_ref: (0, i, 0)),
                      pl.BlockSpec((None, tr, C), lambda i, my_ref: (my_ref[0], i, 0)), row, row, row],
            out_specs=[row] * 4),
        out_shape=[S((R, C), F32)] * 4, compiler_params=_cp(1))(my, slots, own, w, m, v)


TINY_ROWS = (("b_s", 8), ("g_ffn1", 8), ("g_mix", 8), ("g_ca", 8), ("g_mem", 8), ("g_ffn2", 8), ("g_sgu", 4),
             ("g_fox_o", 4), ("g_gmlp_o", 4), ("g_cq", 2), ("g_ck", 2), ("g_q", 1), ("g_k", 1), ("b_f", 1),
             ("loss", 1))
TINY_P = 72


def _tiny_pieces(width):
    return [(j, slice(j * LANES, min((j + 1) * LANES, width))) for j in range(-(-width // LANES))]


def _pack_tiny(grads, sq):
    names = [n for n, _ in TINY_ROWS if n != "loss"]

    def body(*refs):
        ins, sq_ref, o_ref = refs[:len(names)], refs[len(names)], refs[len(names) + 1]
        o_ref[...] = jnp.zeros_like(o_ref)
        at = 0
        for ref, (name, r) in zip(ins, TINY_ROWS):
            if name == "b_s":
                o_ref[at:at + r, :] = ref[...]
            else:
                for j, cols in _tiny_pieces(ref.shape[1]):
                    o_ref[at + j:at + j + 1, 0:cols.stop - cols.start] = ref[:, cols]
            at += r
        o_ref[at:at + 1, :] = sq_ref[0:1, :]

    return pl.pallas_call(body, name="tiny_pack", out_shape=S((TINY_P, LANES), F32))(
        *[grads[n] for n in names], sq)


def _adamw_tiny(slots, w, m, v):
    names = [n for n, _ in TINY_ROWS if n != "loss"]
    k = len(names)

    def body(s_ref, *refs):
        ins, outs, loss_ref = refs[:3 * k], refs[3 * k:7 * k], refs[7 * k]
        g_all = s_ref[0]
        for d in range(1, N_DEV):
            g_all = g_all + s_ref[d]
        at = 0
        for i, (name, r) in enumerate(TINY_ROWS[:k]):
            w_ref, m_ref, v_ref = ins[i], ins[k + i], ins[2 * k + i]
            o = outs[4 * i:4 * i + 4]
            if name == "b_s":
                pieces = [(slice(at, at + r), slice(0, LANES), (slice(None), slice(None)))]
            else:
                pieces = [(slice(at + j, at + j + 1), slice(0, c.stop - c.start), (slice(None), c))
                          for j, c in _tiny_pieces(w_ref.shape[1])]
            for rows, lanes, dst in pieces:
                g = g_all[rows, lanes]
                res = (g,) + _adamw(w_ref[dst], g, m_ref[dst], v_ref[dst])
                for ref, val in zip(o, res):
                    ref[dst] = val
            at += r
        loss_ref[...] = g_all[at:at + 1, :]

    shapes = [S(w[n].shape, F32) for n in names]
    out = pl.pallas_call(
        body, name="adamw_tiny", out_shape=[s for s in shapes for _ in range(4)] + [S((1, LANES), F32)],
    )(slots, *[w[n] for n in names], *[m[n] for n in names], *[v[n] for n in names])
    stores = ({}, {}, {}, {})
    for i, n in enumerate(names):
        for store, t in zip(stores, out[4 * i:4 * i + 4]):
            store[n] = t
    return stores, out[4 * k]


WEIGHTS =('g_ffn1', 'w_ffn1_in', 'w_ffn1_out', 'g_mix', 'w_in', 'b_f', 'g_q', 'g_k', 'g_sgu', 'w_s', 'b_s',
           'g_fox_o', 'g_gmlp_o', 'w_out', 'g_ca', 'g_mem', 'w_cq', 'w_ckv', 'g_cq', 'g_ck', 'w_co', 'g_ffn2',
           'w_ffn2_in', 'w_ffn2_out')
BIG = ('w_ffn1_in', 'w_ffn1_out', 'w_in', 'w_out', 'w_cq', 'w_ckv', 'w_co', 'w_ffn2_in', 'w_ffn2_out')
TRANSPOSED = ('w_ffn1_in', 'w_in', 'w_ffn2_in')
TWO_LEVEL = ('w_ffn1_in', 'w_in')
GATHER_GROUPS = {"ffn1_up": ("w_ffn1_in",), "ffn1_dn": ("w_ffn1_out",), "mix": ("w_in", "w_out"),
                 "ca": ("w_cq", "w_ckv", "w_co"), "ffn2": ("w_ffn2_in", "w_ffn2_out")}
QKV_W = 3 * FOX_W
UV_OFF = QKV_W + FOX_HEADS


def kernel(x, mem, g_ffn1, w_ffn1_in, w_ffn1_out, g_mix, w_in, b_f, g_q, g_k, g_sgu, w_s, b_s, g_fox_o, g_gmlp_o, w_out, g_ca, g_mem, w_cq, w_ckv, g_cq, g_ck, w_co, g_ffn2, w_ffn2_in, w_ffn2_out, loss_target, m_g_ffn1, m_w_ffn1_in, m_w_ffn1_out, m_g_mix, m_w_in, m_b_f, m_g_q, m_g_k, m_g_sgu, m_w_s, m_b_s, m_g_fox_o, m_g_gmlp_o, m_w_out, m_g_ca, m_g_mem, m_w_cq, m_w_ckv, m_g_cq, m_g_ck, m_w_co, m_g_ffn2, m_w_ffn2_in, m_w_ffn2_out, v_g_ffn1, v_w_ffn1_in, v_w_ffn1_out, v_g_mix, v_w_in, v_b_f, v_g_q, v_g_k, v_g_sgu, v_w_s, v_b_s, v_g_fox_o, v_g_gmlp_o, v_w_out, v_g_ca, v_g_mem, v_w_cq, v_w_ckv, v_g_cq, v_g_ck, v_w_co, v_g_ffn2, v_w_ffn2_in, v_w_ffn2_out):
    args = dict(locals())
    w = {n: args[n] for n in WEIGHTS}
    mo = {n: args["m_" + n] for n in WEIGHTS}
    vo = {n: args["v_" + n] for n in WEIGHTS}
    D = D_MODEL

    def local(n, a):
        return a[0].T if n in TRANSPOSED else a[0]

    shards = [local(n, w[n]).astype(BF) for n in BIG]
    fb = shards[0].shape[0]
    g_peers = [NEAR_PEERS if n in TWO_LEVEL else ALL_PEERS for n in BIG]
    g_snd, g_rcv, g_src, g_land, g_token = _copy_start("gather_start", shards, _place_own(shards, True), True,
                                                       peers=g_peers)
    handles = {n: (g_src[i], g_land[i], g_snd[i], g_rcv[i]) for i, n in enumerate(BIG)}

    tiny_names = [n for n, _ in TINY_ROWS if n != "loss"]

    def weights(group, after):
        names = GATHER_GROUPS[group]
        hs = [handles[n] for n in names]
        got = list(_copy_wait("gather_wait_" + group, [h[0] for h in hs], [h[1] for h in hs], [h[2] for h in hs],
                              [h[3] for h in hs], after, True, peers=[g_peers[BIG.index(n)] for n in names]))
        passed = [i for i, n in enumerate(names) if n in TWO_LEVEL]
        if passed:
            f_snd, f_rcv, f_land, f_token = _forward_start("gather_pass_start_" + group, [got[i] for i in passed])
            for i, t in zip(passed, _forward_wait("gather_pass_wait_" + group, f_land, f_snd, f_rcv, f_token)):
                got[i] = t
        got = dict(zip(names, got))
        if group == "ffn1_up":
            return {"wup1": got["w_ffn1_in"].reshape(2, N_FFN_BLK, fb, D)}
        if group == "ffn1_dn":
            return {"wdn1": got["w_ffn1_out"].reshape(N_FFN_BLK, fb, D)}
        if group == "mix":
            full = got["w_in"].reshape(-1, D)
            wz = jnp.concatenate([full[:QKV_W], full[UV_OFF:], full[QKV_W:UV_OFF],
                                  jnp.zeros((LANES - FOX_HEADS, D), BF)], axis=0)
            return {"wz": wz, "wout": got["w_out"].reshape(D, D)}
        if group == "ca":
            return {"wcq": got["w_cq"].reshape(D, D), "wco": got["w_co"].reshape(D, D), "wckv": got["w_ckv"]}
        return {"wup2": got["w_ffn2_in"].reshape(2, N_FFN_BLK, fb, D),
                "wdn2": got["w_ffn2_out"].reshape(N_FFN_BLK, fb, D)}

    flying = {}

    def emit(group, g):
        if group == "w_s":
            part = [g["w_s"].reshape(-1, LANES)]
            *copies, token = _copy_start("w_s_start", part, _place_own(part, True), True)
            flying[group] = copies
            return token
        if group == "ffn2":
            parts = {"w_ffn2_in": g["wup2"], "w_ffn2_out": g["wdn2"].reshape(N_DEV, -1, D)}
        elif group == "ffn1_dn":
            parts = {"w_ffn1_out": g["wdn1"].reshape(N_DEV, -1, D)}
        elif group == "ffn1_up":
            parts = {"w_ffn1_in": g["wup1"]}
        else:
            gz = g["wz"]
            g_in = jnp.concatenate([gz[:QKV_W], gz[Z_F:Z_F + FOX_HEADS], gz[QKV_W:Z_F]], axis=0)
            parts = {"w_in": g_in.reshape(N_DEV, -1, D).astype(BF),
                     "w_out": g["wout"].reshape(N_DEV, -1, D), "w_cq": g["wcq"].reshape(N_DEV, -1, D),
                     "w_co": g["wco"].reshape(N_DEV, -1, D), "w_ckv": g["wckv"]}
        names = list(parts)
        srcs = [parts[n] for n in names]
        *copies, token = _copy_start("exchange_start_" + group, srcs, [lax.empty(s.shape, s.dtype) for s in srcs],
                                     False)
        flying[group] = (names, copies)
        return token

    small = {n: (w[n][0] if n == "b_s" else w[n]) for n in tiny_names}
    small["w_s"] = w["w_s"][0]

    sq, dx0, gs = _local_step(x[0], mem[0], loss_target[0], small, weights, emit)

    sm_parts = [_pack_tiny(gs, sq)]
    sm_snd, sm_rcv, sm_src, sm_land, sm_token = _copy_start("tiny_start", sm_parts, _place_own(sm_parts, True), True)

    grad, delta, new_m, new_v = {}, {}, {}, {}

    def update(group, after):
        names, (snd, rcv, srcs, lands) = flying[group]
        owns, slots = _copy_wait("exchange_wait_" + group, srcs, lands, snd, rcv, after, False, with_srcs=True)
        for n, sl, own in zip(names, slots, owns):
            g, d, m2, v2 = _adamw_big("adamw_" + n, sl, local(n, w[n]), local(n, mo[n]), local(n, vo[n]), own=own)
            grad[n], delta[n], new_m[n], new_v[n] = (
                (t.T if n in TRANSPOSED else t).reshape(w[n].shape) for t in (g, d, m2, v2))
        return d

    last = update("ffn2", sm_token)
    last = update("mid", last)
    last = update("ffn1_dn", last)
    last = update("ffn1_up", last)
    ws_snd, ws_rcv, ws_src, ws_land = flying["w_s"]
    ws_all, = _copy_wait("w_s_wait", ws_src, ws_land, ws_snd, ws_rcv, last, True)
    tiny_all, = _copy_wait("tiny_wait", sm_src, sm_land, sm_snd, sm_rcv, ws_all, True)
    ws_shape = w["w_s"].shape
    for store, t in zip((grad, delta, new_m, new_v), _adamw_big(
            "adamw_w_s", ws_all, *[a["w_s"].reshape(-1, LANES) for a in (w, mo, vo)])):
        store["w_s"] = t.reshape(ws_shape)
    stores, loss_row = _adamw_tiny(tiny_all, *[{n: (a[n][0] if n == "b_s" else a[n]) for n in tiny_names}
                                               for a in (w, mo, vo)])
    for store, t in zip((grad, delta, new_m, new_v), stores):
        store.update({n: v.reshape(w[n].shape) for n, v in t.items()})
    loss = loss_row[0, 0] * (0.5 / D)

    return (loss, dx0[None], *[grad[n] for n in WEIGHTS], *[delta[n] for n in WEIGHTS],
            *[new_m[n] for n in WEIGHTS], *[new_v[n] for n in WEIGHTS])
```

```python
import functools

import jax
import jax.numpy as jnp
from jax import lax
from jax.experimental import pallas as pl
from jax.experimental.pallas import tpu as pltpu

F32 = jnp.float32
BF = jnp.bfloat16
S = jax.ShapeDtypeStruct

N_DEV = 8
D_MODEL = 1024
FOX_HEADS, FOX_HD = 8, 64
FOX_W = 512
GMLP_G, GMLP_GD = 8, 64
GMLP_W = 512
CHUNK = 128
CA_HEADS, CA_HD = 4, 256
N_FFN_BLK = 4
ZW = 2688
Z_Q, Z_K, Z_V, Z_U, Z_G, Z_F = 0, 512, 1024, 1536, 2048, 2560
EPS = 1e-6
NEG = -1e30
LANES = 128

ADAM_LR, ADAM_B1, ADAM_B2, ADAM_EPS, ADAM_WD, ADAM_STEP = 0.001, 0.9, 0.999, 1e-08, 0.01, 10

VMEM_LIMIT = 52 * 2 ** 20


def _cp(n_axes):
    return pltpu.CompilerParams(dimension_semantics=("arbitrary",) * n_axes, vmem_limit_bytes=VMEM_LIMIT)


def _nn(a, b):
    return jnp.dot(a, b, preferred_element_type=F32)


def _nt(a, b):
    return lax.dot_general(a, b, (((1,), (1,)), ((), ())), preferred_element_type=F32)


def _tn(a, b):
    return lax.dot_general(a, b, (((0,), (0,)), ((), ())), preferred_element_type=F32)


def _hi(a, b):
    return jnp.dot(a, b, precision=lax.Precision.HIGHEST, preferred_element_type=F32)


def _rstd(x):
    return lax.rsqrt(jnp.mean(x * x, axis=-1, keepdims=True) + EPS)


def _norm_bwd(dy, x, g, r=None):
    r = _rstd(x) if r is None else r
    xh = x * r
    dxh = dy * g
    dx = r * (dxh - xh * jnp.mean(dxh * xh, axis=-1, keepdims=True))
    return dx, dy * xh


def _acc_rows(ref, first, val):
    srow = jnp.sum(val, axis=0, keepdims=True)

    @pl.when(first)
    def _():
        ref[...] = srow

    @pl.when(jnp.logical_not(first))
    def _():
        ref[...] += srow


def _gelu(x):
    c = 0.7978845608028654
    return 0.5 * x * (1.0 + jnp.tanh(c * (x + 0.044715 * x * x * x)))


def _gelu_grad(x):
    c = 0.7978845608028654
    t = jnp.tanh(c * (x + 0.044715 * x * x * x))
    return 0.5 * (1.0 + t) + 0.5 * x * (1.0 - t * t) * c * (1.0 + 3 * 0.044715 * x * x)


def _tile(n, pref):
    return pref if n % pref == 0 else n


def _ffn_up(name, x, g, wup):
    T, D = x.shape
    FB = wup.shape[-2]
    tm = _tile(T, 1024)

    def body(x_ref, g_ref, w_ref, a_ref, h_ref):
        @pl.when(pl.program_id(1) == 0)
        def _():
            xf = x_ref[...]
            h_ref[...] = (xf * _rstd(xf) * g_ref[...]).astype(BF)

        hb = h_ref[...]
        gg = _nt(hb, w_ref[0])
        uu = _nt(hb, w_ref[1])
        a_ref[...] = (gg * jax.nn.sigmoid(gg) * uu).astype(BF)

    return pl.pallas_call(
        body, name=name, grid=(T // tm, N_FFN_BLK),
        in_specs=[pl.BlockSpec((tm, D), lambda i, j: (i, 0)),
                  pl.BlockSpec((1, D), lambda i, j: (0, 0)),
                  pl.BlockSpec((2, None, FB, D), lambda i, j: (0, j, 0, 0))],
        out_specs=[pl.BlockSpec((None, tm, FB), lambda i, j: (j, i, 0)),
                   pl.BlockSpec((tm, D), lambda i, j: (i, 0))],
        out_shape=[S((N_FFN_BLK, T, FB), BF), S((T, D), BF)],
        compiler_params=_cp(2))(x, g, wup)


def _ffn_down(name, a, wdn, x):
    _, T, FB = a.shape
    D = x.shape[1]
    tm = _tile(T, 512)

    def body(a_ref, w_ref, x_ref, o_ref):
        p = _nn(a_ref[0], w_ref[0])
        for j in range(1, N_FFN_BLK):
            p = p + _nn(a_ref[j], w_ref[j])
        o_ref[...] = x_ref[...] + 0.5 * p

    return pl.pallas_call(
        body, name=name, grid=(T // tm,),
        in_specs=[pl.BlockSpec((N_FFN_BLK, tm, FB), lambda i: (0, i, 0)),
                  pl.BlockSpec((N_FFN_BLK, FB, D), lambda i: (0, 0, 0)),
                  pl.BlockSpec((tm, D), lambda i: (i, 0))],
        out_specs=pl.BlockSpec((tm, D), lambda i: (i, 0)),
        out_shape=S((T, D), F32),
        compiler_params=_cp(1))(a, wdn, x)


def _ffn_down_loss(name, a, wdn, x, target):
    _, T, FB = a.shape
    D = x.shape[1]
    tm = _tile(T, 512)

    def body(a_ref, w_ref, x_ref, t_ref, d_ref, db_ref, loss_ref):
        i = pl.program_id(0)
        p = _nn(a_ref[0], w_ref[0])
        for j in range(1, N_FFN_BLK):
            p = p + _nn(a_ref[j], w_ref[j])
        diff = (x_ref[...] + 0.5 * p) - t_ref[...]
        dy = diff * (1.0 / D)
        d_ref[...] = dy
        db_ref[...] = dy.astype(BF)
        sq = jnp.zeros((8, LANES), F32) + jnp.sum(diff * diff)

        @pl.when(i == 0)
        def _():
            loss_ref[...] = sq

        @pl.when(i > 0)
        def _():
            loss_ref[...] += sq

    row = pl.BlockSpec((tm, D), lambda i: (i, 0))
    return pl.pallas_call(
        body, name=name, grid=(T // tm,),
        in_specs=[pl.BlockSpec((N_FFN_BLK, tm, FB), lambda i: (0, i, 0)),
                  pl.BlockSpec((N_FFN_BLK, FB, D), lambda i: (0, 0, 0)), row, row],
        out_specs=[row, row, pl.BlockSpec((8, LANES), lambda i: (0, 0))],
        out_shape=[S((T, D), F32), S((T, D), BF), S((8, LANES), F32)],
        compiler_params=_cp(1))(a, wdn, x, target)


def _ffn_bwd_act(name, dyb, h, wup, wdn):
    T, D = h.shape
    FB = wup.shape[-2]
    tm = _tile(T, 1024)

    def body(d_ref, h_ref, wu_ref, wd_ref, o_ref):
        da = 0.5 * _nt(d_ref[...], wd_ref[...])
        hb = h_ref[...]
        gg = _nt(hb, wu_ref[0])
        uu = _nt(hb, wu_ref[1])
        sg = jax.nn.sigmoid(gg)
        o_ref[0] = (da * uu * (sg * (1.0 + gg * (1.0 - sg)))).astype(BF)
        o_ref[1] = (da * (gg * sg)).astype(BF)

    return pl.pallas_call(
        body, name=name, grid=(T // tm, N_FFN_BLK),
        in_specs=[pl.BlockSpec((tm, D), lambda i, j: (i, 0)),
                  pl.BlockSpec((tm, D), lambda i, j: (i, 0)),
                  pl.BlockSpec((2, None, FB, D), lambda i, j: (0, j, 0, 0)),
                  pl.BlockSpec((None, FB, D), lambda i, j: (j, 0, 0))],
        out_specs=pl.BlockSpec((2, None, tm, FB), lambda i, j: (0, j, i, 0)),
        out_shape=S((2, N_FFN_BLK, T, FB), BF),
        compiler_params=_cp(2))(dyb, h, wup, wdn)


def _ffn_dx(name, dgu, wup, x, g, dy):
    T, D = x.shape
    FB = wup.shape[-2]
    tm = _tile(T, 512)

    def body(d_ref, w_ref, x_ref, g_ref, dy_ref, dx_ref, dg_ref):
        p = None
        for j in range(N_FFN_BLK):
            for half in range(2):
                t = _nn(d_ref[half, j], w_ref[half, j])
                p = t if p is None else p + t
        dx, dgr = _norm_bwd(p, x_ref[...], g_ref[...])
        dx_ref[...] = dx + dy_ref[...]
        _acc_rows(dg_ref, pl.program_id(0) == 0, dgr)

    return pl.pallas_call(
        body, name=name, grid=(T // tm,),
        in_specs=[pl.BlockSpec((2, N_FFN_BLK, tm, FB), lambda i: (0, 0, i, 0)),
                  pl.BlockSpec((2, N_FFN_BLK, FB, D), lambda i: (0, 0, 0, 0), pipeline_mode=pl.Buffered(1)),
                  pl.BlockSpec((tm, D), lambda i: (i, 0)),
                  pl.BlockSpec((1, D), lambda i: (0, 0)),
                  pl.BlockSpec((tm, D), lambda i: (i, 0))],
        out_specs=[pl.BlockSpec((tm, D), lambda i: (i, 0)),
                   pl.BlockSpec((1, D), lambda i: (0, 0))],
        out_shape=[S((T, D), F32), S((1, D), F32)],
        compiler_params=_cp(1))(dgu, wup, x, g, dy)


def _tn_matmul(name, a, a_spec, b, out_shape, out_spec, n_blocks, scale=1.0, after=None):
    extra = [] if after is None else [after]

    def body(a_ref, b_ref, *rest):
        o_ref = rest[-1]
        o_ref[...] = (_tn(a_ref[...], b_ref[...]) * scale).astype(o_ref.dtype)

    return pl.pallas_call(
        body, name=name, grid=(n_blocks,),
        in_specs=[a_spec, pl.BlockSpec(b.shape, lambda j: (0, 0), pipeline_mode=pl.Buffered(1))]
        + [pl.BlockSpec((8, LANES), lambda j: (0, 0)) for _ in extra],
        out_specs=out_spec, out_shape=out_shape, compiler_params=_cp(1))(a, b, *extra)


def _ffn_dwup(name, h, dgu, after=None):
    T, D = h.shape
    FB = dgu.shape[-1]
    return _tn_matmul(
        name + "_dwup", dgu.reshape(2 * N_FFN_BLK, T, FB), pl.BlockSpec((None, T, FB), lambda j: (j, 0, 0)), h,
        S((2 * N_FFN_BLK, FB, D), BF), pl.BlockSpec((None, FB, D), lambda j: (j, 0, 0)), 2 * N_FFN_BLK,
        after=after)


def _ffn_dwdn(name, a, dyb):
    _, T, FB = a.shape
    D = dyb.shape[1]
    return _tn_matmul(
        name + "_dwdn", a, pl.BlockSpec((None, T, FB), lambda j: (j, 0, 0)), dyb,
        S((N_FFN_BLK, FB, D), BF), pl.BlockSpec((None, FB, D), lambda j: (j, 0, 0)), N_FFN_BLK, scale=0.5)


def _mix_proj(x, g, wz):
    T, D = x.shape
    tm = _tile(T, 512)

    def body(x_ref, g_ref, w_ref, z_ref, h_ref):
        xf = x_ref[...]
        hb = (xf * _rstd(xf) * g_ref[...]).astype(BF)
        h_ref[...] = hb
        z_ref[...] = _nt(hb, w_ref[...])

    return pl.pallas_call(
        body, name="mix_proj", grid=(T // tm,),
        in_specs=[pl.BlockSpec((tm, D), lambda i: (i, 0)),
                  pl.BlockSpec((1, D), lambda i: (0, 0)),
                  pl.BlockSpec((ZW, D), lambda i: (0, 0))],
        out_specs=[pl.BlockSpec((tm, ZW), lambda i: (i, 0)),
                   pl.BlockSpec((tm, D), lambda i: (i, 0))],
        out_shape=[S((T, ZW), F32), S((T, D), BF)],
        compiler_params=_cp(1))(x, g, wz)


def _tri(n, lower):
    r = lax.broadcasted_iota(jnp.int32, (n, n), 0)
    c = lax.broadcasted_iota(jnp.int32, (n, n), 1)
    return (r >= c) if lower else (r <= c)


def _spatial_mix(vgn_b, ws_ref, bst, tm):
    tril = _tri(CHUNK, True)
    wms = [jnp.where(tril, ws_ref[g], 0.0).astype(BF) for g in range(GMLP_G)]
    rows = []
    for c in range(tm // CHUNK):
        cols = []
        for g in range(GMLP_G):
            vs = vgn_b[c * CHUNK:(c + 1) * CHUNK, g * GMLP_GD:(g + 1) * GMLP_GD]
            cols.append(_nn(wms[g], vs) + bst[:, g:g + 1])
        rows.append(jnp.concatenate(cols, axis=1))
    return jnp.concatenate(rows, axis=0), wms


HB = 128
AUG_W = FOX_HEADS * HB
COL_A, COL_B, COL_C = 64, 67, 70
RS_Q, RS_K, RS_V, RS_O = 0, 8, 16, 17


def _spread_matrix():
    r = jnp.arange(FOX_W)
    return (jnp.arange(AUG_W)[None, :] == ((r // FOX_HD) * HB + r % FOX_HD)[:, None]).astype(BF)


def _piece_matrix(col):
    r = jnp.arange(LANES)
    dst = jnp.where(r < 3 * FOX_HEADS, (r % FOX_HEADS) * HB + col + r // FOX_HEADS, -1)
    return (jnp.arange(AUG_W)[None, :] == dst[:, None]).astype(BF)


def _ones_row(cols):
    c = jnp.arange(AUG_W) % HB
    hit = functools.reduce(jnp.logical_or, [(c >= a) & (c < a + 3) for a in cols])
    return hit.astype(F32)[None, :]


def _pieces(x):
    lane = lax.broadcasted_iota(jnp.int32, x.shape, 1)
    x = jnp.where(lane < FOX_HEADS, x, 0.0)
    hi = x.astype(BF).astype(F32)
    r1 = x - hi
    mid = r1.astype(BF).astype(F32)
    lo = (r1 - mid).astype(BF).astype(F32)
    return (hi + pltpu.roll(mid, FOX_HEADS, 1) + pltpu.roll(lo, 2 * FOX_HEADS, 1)).astype(BF)


def _mix_prep(z, bf128, g_q, g_k, g_sgu, w_s, b_st, g_go):
    T = z.shape[0]
    tm = _tile(T, 512)
    spread, pc_q, pc_k = _spread_matrix(), _piece_matrix(COL_A), _piece_matrix(COL_B)
    one_q, one_k, one_v = _ones_row([COL_B]), _ones_row([COL_A, COL_C]), _ones_row([COL_A])

    def body(z_ref, bf_ref, gq_ref, gk_ref, gs_ref, ws_ref, bst_ref, go_ref, sp_ref, pq_ref, pk_ref, oq_ref, ok_ref,
             ov_ref, q_ref, k_ref, v_ref, y_ref, rs_ref, carry_ref, qn_sc, kn_sc):
        i = pl.program_id(0)

        @pl.when(i == 0)
        def _():
            carry_ref[...] = jnp.zeros_like(carry_ref)

        rs_ref[...] = jnp.zeros_like(rs_ref)
        for h in range(FOX_HEADS):
            hs = slice(h * FOX_HD, (h + 1) * FOX_HD)
            qh = z_ref[:, Z_Q + h * FOX_HD:Z_Q + (h + 1) * FOX_HD]
            kh = z_ref[:, Z_K + h * FOX_HD:Z_K + (h + 1) * FOX_HD]
            rq, rk = _rstd(qh), _rstd(kh)
            rs_ref[:, RS_Q + h:RS_Q + h + 1] = rq
            rs_ref[:, RS_K + h:RS_K + h + 1] = rk
            qn_sc[:, hs] = (qh * rq * gq_ref[...] * 0.125).astype(BF)
            kn_sc[:, hs] = (kh * rk * gk_ref[...]).astype(BF)

        fl = z_ref[:, Z_F:Z_F + LANES] + bf_ref[...]
        logf = jnp.minimum(fl, 0.0) - jnp.log1p(jnp.exp(-jnp.abs(fl)))
        csum = _hi(_tri(tm, True).astype(F32), logf) + carry_ref[...]
        carry_ref[...] = csum[tm - 1:tm, :]
        sp = sp_ref[...]
        q_ref[...] = (_nn(qn_sc[...], sp) + _nn(_pieces(csum), pq_ref[...]) + oq_ref[...]).astype(BF)
        k_ref[...] = (_nn(kn_sc[...], sp) + _nn(_pieces(-csum), pk_ref[...]) + ok_ref[...]).astype(BF)
        v_ref[...] = (_nn(z_ref[:, Z_V:Z_V + FOX_W].astype(BF), sp) + ov_ref[...]).astype(BF)

        u = _gelu(z_ref[:, Z_U:Z_U + GMLP_W])
        vg = _gelu(z_ref[:, Z_G:Z_G + GMLP_W])
        rv = _rstd(vg)
        vgn = (vg * rv * gs_ref[...]).astype(BF)
        mixed, _ = _spatial_mix(vgn, ws_ref, bst_ref[...], tm)
        sgu = u * mixed
        ro = _rstd(sgu)
        y_ref[...] = (sgu * ro * go_ref[...]).astype(BF)
        rs_ref[:, RS_V:RS_V + 1] = rv
        rs_ref[:, RS_O:RS_O + 1] = ro

    row = lambda i: (i, 0)
    fix2 = lambda i: (0, 0)
    return pl.pallas_call(
        body, name="mix_prep", grid=(T // tm,),
        in_specs=[pl.BlockSpec((tm, ZW), row),
                  pl.BlockSpec((1, LANES), fix2), pl.BlockSpec((1, FOX_HD), fix2), pl.BlockSpec((1, FOX_HD), fix2),
                  pl.BlockSpec((1, GMLP_W), fix2), pl.BlockSpec((GMLP_G, CHUNK, CHUNK), lambda i: (0, 0, 0)),
                  pl.BlockSpec((CHUNK, GMLP_G), fix2), pl.BlockSpec((1, GMLP_W), fix2),
                  pl.BlockSpec((FOX_W, AUG_W), fix2), pl.BlockSpec((LANES, AUG_W), fix2),
                  pl.BlockSpec((LANES, AUG_W), fix2), pl.BlockSpec((1, AUG_W), fix2), pl.BlockSpec((1, AUG_W), fix2),
                  pl.BlockSpec((1, AUG_W), fix2)],
        out_specs=[pl.BlockSpec((tm, AUG_W), row), pl.BlockSpec((tm, AUG_W), row), pl.BlockSpec((tm, AUG_W), row),
                   pl.BlockSpec((tm, GMLP_W), row), pl.BlockSpec((tm, LANES), row)],
        out_shape=[S((T, AUG_W), BF), S((T, AUG_W), BF), S((T, AUG_W), BF), S((T, GMLP_W), BF), S((T, LANES), F32)],
        scratch_shapes=[pltpu.VMEM((1, LANES), F32), pltpu.VMEM((tm, FOX_W), BF), pltpu.VMEM((tm, FOX_W), BF)],
        compiler_params=_cp(1))(z, bf128, g_q, g_k, g_sgu, w_s, b_st, g_go, spread, pc_q, pc_k, one_q, one_k, one_v)


def _fox_fwd(q, k, v):
    T = q.shape[0]
    tq = _tile(T, 1024)
    nq = T // tq

    def body(q_ref, k_ref, v_ref, o_ref, lse_ref, m_sc, acc_sc):
        i, j = pl.program_id(0), pl.program_id(1)

        @pl.when(j == 0)
        def _():
            m_sc[...] = jnp.full(m_sc.shape, NEG, F32)
            acc_sc[...] = jnp.zeros_like(acc_sc)

        def step(masked):
            mask = _tri(tq, True) if masked else None
            for h in range(FOX_HEADS):
                hb = slice(h * HB, (h + 1) * HB)
                s = _nt(q_ref[:, hb], k_ref[:, hb])
                if masked:
                    s = jnp.where(mask, s, NEG)
                m_prev = m_sc[h]
                m_new = jnp.maximum(m_prev, jnp.broadcast_to(jnp.max(s, axis=1, keepdims=True), (tq, HB)))
                p = jnp.exp(s - jnp.tile(m_new, (1, tq // HB))).astype(BF)
                acc_sc[:, hb] = jnp.exp(m_prev - m_new) * acc_sc[:, hb] + _nn(p, v_ref[:, hb])
                m_sc[h] = m_new

        @pl.when(j < i)
        def _():
            step(False)

        @pl.when(j == i)
        def _():
            step(True)
            lse_ref[...] = jnp.zeros_like(lse_ref)
            for h in range(FOX_HEADS):
                l = acc_sc[:, h * HB + COL_A:h * HB + COL_A + 1]
                o_ref[:, h * FOX_HD:(h + 1) * FOX_HD] = acc_sc[:, h * HB:h * HB + FOX_HD] / l
                lse_ref[:, h:h + 1] = m_sc[h][:, 0:1] + jnp.log(l)

    qi = lambda i, j: (i, 0)
    kj = lambda i, j: (jnp.minimum(i, j), 0)
    return pl.pallas_call(
        body, name="fox_fwd", grid=(nq, nq),
        in_specs=[pl.BlockSpec((tq, AUG_W), qi), pl.BlockSpec((tq, AUG_W), kj), pl.BlockSpec((tq, AUG_W), kj)],
        out_specs=[pl.BlockSpec((tq, FOX_W), qi), pl.BlockSpec((tq, LANES), qi)],
        out_shape=[S((T, FOX_W), F32), S((T, LANES), F32)],
        scratch_shapes=[pltpu.VMEM((FOX_HEADS, tq, HB), F32), pltpu.VMEM((tq, AUG_W), F32)],
        compiler_params=_cp(2))(q, k, v)


def _fox_bwd(q, k, v, dob):
    T = q.shape[0]
    tq = _tile(T, 512)
    nq = T // tq
    half = AUG_W // 2
    hpg = FOX_HEADS // 2

    pairs = [(j, i) for j in range(nq) for i in range(j, nq)]
    jt = jnp.asarray([p[0] for p in pairs], jnp.int32)
    it = jnp.asarray([p[1] for p in pairs], jnp.int32)

    def body(jt_ref, it_ref, q_ref, k_ref, v_ref, do_ref, dq_ref, dk_ref, dv_ref, dq_sc):
        t = pl.program_id(1)
        j, i = jt_ref[t], it_ref[t]

        @pl.when(t == 0)
        def _():
            dq_sc[...] = jnp.zeros_like(dq_sc)

        @pl.when(i == j)
        def _():
            dk_ref[...] = jnp.zeros_like(dk_ref)
            dv_ref[...] = jnp.zeros_like(dv_ref)

        def step(masked):
            rows = pl.ds(pl.multiple_of(i * tq, tq), tq)
            mask = _tri(tq, True) if masked else None
            for h in range(hpg):
                hb = slice(h * HB, (h + 1) * HB)
                qh, kh, vh, doh = q_ref[:, hb], k_ref[:, hb], v_ref[:, hb], do_ref[:, hb]
                s = _nt(qh, kh)
                if masked:
                    s = jnp.where(mask, s, NEG)
                p = jnp.exp(s)
                dsb = (p * _nt(doh, vh)).astype(BF)
                dv_ref[:, hb] += _tn(p.astype(BF), doh)
                dk_ref[:, hb] += _tn(dsb, qh)
                dq_sc[rows, hb] += _nn(dsb, kh)

        @pl.when(i > j)
        def _():
            step(False)

        @pl.when(i == j)
        def _():
            step(True)
            dq_ref[...] = dq_sc[pl.ds(pl.multiple_of(j * tq, tq), tq), :]

    qi = pl.BlockSpec((tq, half), lambda g, t, jt_ref, it_ref: (it_ref[t], g))
    kj = pl.BlockSpec((tq, half), lambda g, t, jt_ref, it_ref: (jt_ref[t], g))
    return pl.pallas_call(
        body, name="fox_bwd",
        grid_spec=pltpu.PrefetchScalarGridSpec(
            num_scalar_prefetch=2, grid=(2, len(pairs)), in_specs=[qi, kj, kj, qi], out_specs=[kj, kj, kj],
            scratch_shapes=[pltpu.VMEM((T, half), F32)]),
        out_shape=[S((T, AUG_W), F32), S((T, AUG_W), F32), S((T, AUG_W), F32)],
        compiler_params=_cp(2))(jt, it, q, k, v, dob)


def _mix_out(attn, yg, g_fo, wout, x):
    T, D = x.shape
    tm = _tile(T, 1024)

    def body(a_ref, y_ref, g_ref, w_ref, x_ref, o_ref):
        at = a_ref[...]
        yf = (at * _rstd(at) * g_ref[...]).astype(BF)
        o_ref[...] = x_ref[...] + _nn(yf, w_ref[:FOX_W, :]) + _nn(y_ref[...], w_ref[FOX_W:, :])

    row = lambda i: (i, 0)
    return pl.pallas_call(
        body, name="mix_out", grid=(T // tm,),
        in_specs=[pl.BlockSpec((tm, FOX_W), row), pl.BlockSpec((tm, GMLP_W), row),
                  pl.BlockSpec((1, FOX_W), lambda i: (0, 0)), pl.BlockSpec((D, D), lambda i: (0, 0)),
                  pl.BlockSpec((tm, D), row)],
        out_specs=pl.BlockSpec((tm, D), row),
        out_shape=S((T, D), F32),
        compiler_params=_cp(1))(attn, yg, g_fo, wout, x)


def _mix_out_bwd(dx, attn, yg, g_fo, wout, qf, lse):
    T, D = dx.shape
    tm = _tile(T, 512)
    n = T // tm
    spread, pc_l, pc_d = _spread_matrix(), _piece_matrix(COL_C), _piece_matrix(COL_A)

    def body(dx_ref, a_ref, y_ref, g_ref, w_ref, qf_ref, lse_ref, sp_ref, pl_ref, pd_ref,
             qb_ref, dob_ref, dyg_ref, dw_ref, dg_ref, acc_ref, dsum_ref):
        i = pl.program_id(0)
        dxb = dx_ref[...].astype(BF)
        at = a_ref[...]
        yf = (at * _rstd(at) * g_ref[...]).astype(BF)
        dy = _nt(dxb, w_ref[...])
        p_top = _tn(yf, dxb)
        p_bot = _tn(y_ref[...], dxb)

        @pl.when(i == 0)
        def _():
            acc_ref[:FOX_W, :] = p_top
            acc_ref[FOX_W:, :] = p_bot

        @pl.when(i > 0)
        def _():
            acc_ref[:FOX_W, :] += p_top
            acc_ref[FOX_W:, :] += p_bot

        @pl.when(i == n - 1)
        def _():
            dw_ref[...] = acc_ref[...].astype(BF)

        dat, dgr = _norm_bwd(dy[:, :FOX_W], at, g_ref[...])
        _acc_rows(dg_ref, i == 0, dgr)
        dyg_ref[...] = dy[:, FOX_W:]
        prod = dat * at
        dsum_ref[...] = jnp.zeros_like(dsum_ref)
        for h in range(FOX_HEADS):
            dsum_ref[:, h:h + 1] = jnp.sum(prod[:, h * FOX_HD:(h + 1) * FOX_HD], axis=1, keepdims=True)
        dob_ref[...] = (_nn(dat.astype(BF), sp_ref[...]) + _nn(_pieces(-dsum_ref[...]), pd_ref[...])).astype(BF)
        qb_ref[...] = (qf_ref[...].astype(F32) + _nn(_pieces(-lse_ref[...]), pl_ref[...])).astype(BF)

    row = lambda i: (i, 0)
    fix = lambda i: (0, 0)
    return pl.pallas_call(
        body, name="mix_out_bwd", grid=(n,),
        in_specs=[pl.BlockSpec((tm, D), row), pl.BlockSpec((tm, FOX_W), row), pl.BlockSpec((tm, GMLP_W), row),
                  pl.BlockSpec((1, FOX_W), fix), pl.BlockSpec((D, D), fix), pl.BlockSpec((tm, AUG_W), row),
                  pl.BlockSpec((tm, LANES), row), pl.BlockSpec((FOX_W, AUG_W), fix), pl.BlockSpec((LANES, AUG_W), fix),
                  pl.BlockSpec((LANES, AUG_W), fix)],
        out_specs=[pl.BlockSpec((tm, AUG_W), row), pl.BlockSpec((tm, AUG_W), row), pl.BlockSpec((tm, GMLP_W), row),
                   pl.BlockSpec((D, D), fix), pl.BlockSpec((1, FOX_W), fix)],
        out_shape=[S((T, AUG_W), BF), S((T, AUG_W), BF), S((T, GMLP_W), F32), S((D, D), BF), S((1, FOX_W), F32)],
        scratch_shapes=[pltpu.VMEM((D, D), F32), pltpu.VMEM((tm, LANES), F32)],
        compiler_params=_cp(1))(dx, attn, yg, g_fo, wout, qf, lse, spread, pc_l, pc_d)


def _mix_prep_bwd(z, dq, dk, dv, dyg, rs, bf128, g_q, g_k, g_sgu, w_s, b_st, g_go):
    T = z.shape[0]
    tm = _tile(T, 512)
    n = T // tm

    def body(z_ref, dq_ref, dk_ref, dv_ref, dyg_ref, rs_ref, bf_ref, gq_ref, gk_ref, gs_ref, ws_ref,
             bst_ref, go_ref, dz_ref, dgq_ref, dgk_ref, dgs_ref, dgo_ref, dws_ref, dbst_ref, dbf_ref, carry_ref):
        i = pl.program_id(0)
        first = i == 0
        rs = rs_ref[...]

        @pl.when(first)
        def _():
            carry_ref[...] = jnp.zeros_like(carry_ref)

        lane = lax.broadcasted_iota(jnp.int32, (tm, LANES), 1)
        dc = jnp.zeros((tm, LANES), F32)
        gq_rows, gk_rows = [], []
        for h in range(FOX_HEADS):
            hp = slice(h * HB, h * HB + FOX_HD)
            dqh, gqr = _norm_bwd(dq_ref[:, hp] * 0.125, z_ref[:, Z_Q + h * FOX_HD:Z_Q + (h + 1) * FOX_HD], gq_ref[...],
                                 rs[:, RS_Q + h:RS_Q + h + 1])
            dkh, gkr = _norm_bwd(dk_ref[:, hp], z_ref[:, Z_K + h * FOX_HD:Z_K + (h + 1) * FOX_HD], gk_ref[...],
                                 rs[:, RS_K + h:RS_K + h + 1])
            dz_ref[:, Z_Q + h * FOX_HD:Z_Q + (h + 1) * FOX_HD] = dqh.astype(BF)
            dz_ref[:, Z_K + h * FOX_HD:Z_K + (h + 1) * FOX_HD] = dkh.astype(BF)
            dz_ref[:, Z_V + h * FOX_HD:Z_V + (h + 1) * FOX_HD] = dv_ref[:, hp].astype(BF)
            dch = dq_ref[:, h * HB + COL_A:h * HB + COL_A + 1] - dk_ref[:, h * HB + COL_B:h * HB + COL_B + 1]
            dc = jnp.where(lane == h, dch, dc)
            gq_rows.append(gqr)
            gk_rows.append(gkr)
        _acc_rows(dgq_ref, first, functools.reduce(lambda a, b: a + b, gq_rows))
        _acc_rows(dgk_ref, first, functools.reduce(lambda a, b: a + b, gk_rows))

        dlogf = _hi(_tri(tm, False).astype(F32), dc) + carry_ref[...]
        carry_ref[...] = dlogf[0:1, :]
        fl = z_ref[:, Z_F:Z_F + LANES] + bf_ref[...]
        lane = lax.broadcasted_iota(jnp.int32, (tm, LANES), 1)
        df = jnp.where(lane < FOX_HEADS, dlogf * jax.nn.sigmoid(-fl), 0.0)
        dz_ref[:, Z_F:Z_F + LANES] = df.astype(BF)
        _acc_rows(dbf_ref, first, df)

        u_pre = z_ref[:, Z_U:Z_U + GMLP_W]
        vg_pre = z_ref[:, Z_G:Z_G + GMLP_W]
        u = _gelu(u_pre)
        vg = _gelu(vg_pre)
        rv = rs[:, RS_V:RS_V + 1]
        vgn = (vg * rv * gs_ref[...]).astype(BF)
        bst = bst_ref[...]
        mixed, wms = _spatial_mix(vgn, ws_ref, bst, tm)
        sgu = u * mixed
        dsgu, gor = _norm_bwd(dyg_ref[...], sgu, go_ref[...], rs[:, RS_O:RS_O + 1])
        _acc_rows(dgo_ref, first, gor)
        du = dsgu * mixed
        dmixed = dsgu * u
        dmb = dmixed.astype(BF)
        tril = _tri(CHUNK, True)
        dvgn_rows = []
        dws = [None] * GMLP_G
        dbs = [None] * GMLP_G
        for c in range(tm // CHUNK):
            cs = slice(c * CHUNK, (c + 1) * CHUNK)
            cols = []
            for g in range(GMLP_G):
                gs = slice(g * GMLP_GD, (g + 1) * GMLP_GD)
                dmc = dmb[cs, gs]
                pw = _nt(dmc, vgn[cs, gs])
                pb = jnp.sum(dmixed[cs, gs], axis=1, keepdims=True)
                dws[g] = pw if dws[g] is None else dws[g] + pw
                dbs[g] = pb if dbs[g] is None else dbs[g] + pb
                cols.append(_tn(wms[g], dmc))
            dvgn_rows.append(jnp.concatenate(cols, axis=1))
        dvgn = jnp.concatenate(dvgn_rows, axis=0)
        dbs_t = jnp.concatenate(dbs, axis=1)
        for g in range(GMLP_G):
            dwg = jnp.where(tril, dws[g], 0.0)

            @pl.when(first)
            def _():
                dws_ref[g] = dwg

            @pl.when(jnp.logical_not(first))
            def _():
                dws_ref[g] += dwg

        @pl.when(first)
        def _():
            dbst_ref[...] = dbs_t

        @pl.when(jnp.logical_not(first))
        def _():
            dbst_ref[...] += dbs_t

        dvg, gsr = _norm_bwd(dvgn, vg, gs_ref[...], rv)
        _acc_rows(dgs_ref, first, gsr)
        dz_ref[:, Z_U:Z_U + GMLP_W] = (du * _gelu_grad(u_pre)).astype(BF)
        dz_ref[:, Z_G:Z_G + GMLP_W] = (dvg * _gelu_grad(vg_pre)).astype(BF)

    rev = lambda i: (n - 1 - i, 0)
    fix = lambda i: (0, 0)
    fix3 = lambda i: (0, 0, 0)
    return pl.pallas_call(
        body, name="mix_prep_bwd", grid=(n,),
        in_specs=[pl.BlockSpec((tm, ZW), rev), pl.BlockSpec((tm, AUG_W), rev), pl.BlockSpec((tm, AUG_W), rev),
                  pl.BlockSpec((tm, AUG_W), rev), pl.BlockSpec((tm, GMLP_W), rev), pl.BlockSpec((tm, LANES), rev),
                  pl.BlockSpec((1, LANES), fix), pl.BlockSpec((1, FOX_HD), fix), pl.BlockSpec((1, FOX_HD), fix),
                  pl.BlockSpec((1, GMLP_W), fix), pl.BlockSpec((GMLP_G, CHUNK, CHUNK), fix3),
                  pl.BlockSpec((CHUNK, GMLP_G), fix), pl.BlockSpec((1, GMLP_W), fix)],
        out_specs=[pl.BlockSpec((tm, ZW), rev), pl.BlockSpec((1, FOX_HD), fix), pl.BlockSpec((1, FOX_HD), fix),
                   pl.BlockSpec((1, GMLP_W), fix), pl.BlockSpec((1, GMLP_W), fix),
                   pl.BlockSpec((GMLP_G, CHUNK, CHUNK), fix3), pl.BlockSpec((CHUNK, GMLP_G), fix),
                   pl.BlockSpec((1, LANES), fix)],
        out_shape=[S((T, ZW), BF), S((1, FOX_HD), F32), S((1, FOX_HD), F32), S((1, GMLP_W), F32), S((1, GMLP_W), F32),
                   S((GMLP_G, CHUNK, CHUNK), F32), S((CHUNK, GMLP_G), F32), S((1, LANES), F32)],
        scratch_shapes=[pltpu.VMEM((1, LANES), F32)],
        compiler_params=_cp(1))(z, dq, dk, dv, dyg, rs, bf128, g_q, g_k, g_sgu, w_s, b_st, g_go)


def _mix_proj_bwd(dz, wz, x, g, dy):
    T, D = x.shape
    tm = _tile(T, 512)

    def body(dz_ref, w_ref, x_ref, g_ref, dy_ref, dx_ref, dxb_ref, dg_ref):
        dh = _nn(dz_ref[...], w_ref[...])
        dx, dgr = _norm_bwd(dh, x_ref[...], g_ref[...])
        dx = dx + dy_ref[...]
        dx_ref[...] = dx
        dxb_ref[...] = dx.astype(BF)
        _acc_rows(dg_ref, pl.program_id(0) == 0, dgr)

    row = lambda i: (i, 0)
    fix = lambda i: (0, 0)
    return pl.pallas_call(
        body, name="mix_proj_bwd", grid=(T // tm,),
        in_specs=[pl.BlockSpec((tm, ZW), row), pl.BlockSpec((ZW, D), fix), pl.BlockSpec((tm, D), row),
                  pl.BlockSpec((1, D), fix), pl.BlockSpec((tm, D), row)],
        out_specs=[pl.BlockSpec((tm, D), row), pl.BlockSpec((tm, D), row), pl.BlockSpec((1, D), fix)],
        out_shape=[S((T, D), F32), S((T, D), BF), S((1, D), F32)],
        compiler_params=_cp(1))(dz, wz, x, g, dy)


def _ca_kv(mem, g_mem, wckv, g_ck):
    M, D = mem.shape

    def body(m_ref, g_ref, w_ref, gk_ref, mn_ref, kr_ref, kn_ref, v_ref):
        mf = m_ref[...]
        mn = (mf * _rstd(mf) * g_ref[...]).astype(BF)
        mn_ref[...] = mn
        for h in range(CA_HEADS):
            kr = _nn(mn, w_ref[h])
            kr_ref[h] = kr
            kn_ref[h] = (kr * _rstd(kr) * gk_ref[...]).astype(BF)
            v_ref[h] = _nn(mn, w_ref[CA_HEADS + h]).astype(BF)

    hd = (CA_HEADS, M, CA_HD)
    return pl.pallas_call(
        body, name="ca_kv", out_shape=[S((M, D), BF), S(hd, F32), S(hd, BF), S(hd, BF)],
        compiler_params=pltpu.CompilerParams(vmem_limit_bytes=VMEM_LIMIT))(mem, g_mem, wckv, g_ck)


def _ca_tile_fwd(xt, gca, wcq, gcq, kn_ref, v_ref):
    hb = (xt * _rstd(xt) * gca).astype(BF)
    qc = _nn(hb, wcq)
    qr, qn, ps = [], [], []
    for h in range(CA_HEADS):
        qh = qc[:, h * CA_HD:(h + 1) * CA_HD]
        qnh = (qh * _rstd(qh) * gcq * 0.0625).astype(BF)
        s = _nt(qnh, kn_ref[h])
        e = jnp.exp(s - jnp.max(s, axis=1, keepdims=True))
        ps.append(e / jnp.sum(e, axis=1, keepdims=True))
        qr.append(qh)
        qn.append(qnh)
    return hb, qr, qn, ps


def _ca_fwd(x, g_ca, wcq, g_cq, kn, vv, wco):
    T, D = x.shape
    M = kn.shape[1]
    tm = _tile(T, 1024)

    def body(x_ref, gca_ref, wcq_ref, gcq_ref, kn_ref, v_ref, wco_ref, o_ref, ob_sc):
        xt = x_ref[...]
        _, _, _, ps = _ca_tile_fwd(xt, gca_ref[...], wcq_ref[...], gcq_ref[...], kn_ref, v_ref)
        for h in range(CA_HEADS):
            ob_sc[:, h * CA_HD:(h + 1) * CA_HD] = _nn(ps[h].astype(BF), v_ref[h]).astype(BF)
        o_ref[...] = xt + _nn(ob_sc[...], wco_ref[...])

    row = lambda i: (i, 0)
    fix = lambda i: (0, 0)
    fix3 = lambda i: (0, 0, 0)
    return pl.pallas_call(
        body, name="ca_fwd", grid=(T // tm,),
        in_specs=[pl.BlockSpec((tm, D), row), pl.BlockSpec((1, D), fix), pl.BlockSpec((D, D), fix),
                  pl.BlockSpec((1, CA_HD), fix), pl.BlockSpec((CA_HEADS, M, CA_HD), fix3),
                  pl.BlockSpec((CA_HEADS, M, CA_HD), fix3), pl.BlockSpec((D, D), fix)],
        out_specs=pl.BlockSpec((tm, D), row), out_shape=S((T, D), F32),
        scratch_shapes=[pltpu.VMEM((tm, D), BF)],
        compiler_params=_cp(1))(x, g_ca, wcq, g_cq, kn, vv, wco)


def _ca_bwd(x, dy, g_ca, wcq, g_cq, kn, vv, wco):
    T, D = x.shape
    M = kn.shape[1]
    tm = _tile(T, 512)
    n = T // tm

    def body(x_ref, dy_ref, gca_ref, wcq_ref, gcq_ref, kn_ref, v_ref, wco_ref,
             dx_ref, dwq_ref, dwo_ref, dkn_ref, dv_ref, dgcq_ref, dgca_ref, aq_sc, ao_sc, ob_sc, dq_sc):
        i = pl.program_id(0)
        first = i == 0
        xt = x_ref[...]
        dyt = dy_ref[...]
        dyb = dyt.astype(BF)
        hb, qr, qn, ps = _ca_tile_fwd(xt, gca_ref[...], wcq_ref[...], gcq_ref[...], kn_ref, v_ref)
        do = _nt(dyb, wco_ref[...])
        gcq_rows = None
        for h in range(CA_HEADS):
            hs = slice(h * CA_HD, (h + 1) * CA_HD)
            p = ps[h]
            pb = p.astype(BF)
            ob_sc[:, hs] = _nn(pb, v_ref[h]).astype(BF)
            doh = do[:, hs].astype(BF)
            dp = _nt(doh, v_ref[h])
            ds = (p * (dp - jnp.sum(dp * p, axis=1, keepdims=True))).astype(BF)
            dvh = _tn(pb, doh)
            dkh = _tn(ds, qn[h])

            @pl.when(first)
            def _():
                dv_ref[h] = dvh
                dkn_ref[h] = dkh

            @pl.when(jnp.logical_not(first))
            def _():
                dv_ref[h] += dvh
                dkn_ref[h] += dkh

            dqn = _nn(ds, kn_ref[h]) * 0.0625
            dqh, gr = _norm_bwd(dqn, qr[h], gcq_ref[...])
            gcq_rows = gr if gcq_rows is None else gcq_rows + gr
            dq_sc[:, hs] = dqh.astype(BF)
        _acc_rows(dgcq_ref, first, gcq_rows)
        dqb = dq_sc[...]
        p_o = _tn(ob_sc[...], dyb)
        p_q = _tn(hb, dqb)

        @pl.when(first)
        def _():
            ao_sc[...] = p_o
            aq_sc[...] = p_q

        @pl.when(jnp.logical_not(first))
        def _():
            ao_sc[...] += p_o
            aq_sc[...] += p_q

        @pl.when(i == n - 1)
        def _():
            dwo_ref[...] = ao_sc[...].astype(BF)
            dwq_ref[...] = aq_sc[...].astype(BF)

        dh = _nt(dqb, wcq_ref[...])
        dx, gar = _norm_bwd(dh, xt, gca_ref[...])
        dx_ref[...] = dx + dyt
        _acc_rows(dgca_ref, first, gar)

    row = lambda i: (i, 0)
    fix = lambda i: (0, 0)
    fix3 = lambda i: (0, 0, 0)
    hd = (CA_HEADS, M, CA_HD)
    return pl.pallas_call(
        body, name="ca_bwd", grid=(n,),
        in_specs=[pl.BlockSpec((tm, D), row), pl.BlockSpec((tm, D), row), pl.BlockSpec((1, D), fix),
                  pl.BlockSpec((D, D), fix), pl.BlockSpec((1, CA_HD), fix), pl.BlockSpec(hd, fix3),
                  pl.BlockSpec(hd, fix3), pl.BlockSpec((D, D), fix)],
        out_specs=[pl.BlockSpec((tm, D), row), pl.BlockSpec((D, D), fix), pl.BlockSpec((D, D), fix),
                   pl.BlockSpec(hd, fix3), pl.BlockSpec(hd, fix3), pl.BlockSpec((1, CA_HD), fix),
                   pl.BlockSpec((1, D), fix)],
        out_shape=[S((T, D), F32), S((D, D), BF), S((D, D), BF), S(hd, F32), S(hd, F32), S((1, CA_HD), F32),
                   S((1, D), F32)],
        scratch_shapes=[pltpu.VMEM((D, D), F32), pltpu.VMEM((D, D), F32), pltpu.VMEM((tm, D), BF),
                        pltpu.VMEM((tm, D), BF)],
        compiler_params=_cp(1))(x, dy, g_ca, wcq, g_cq, kn, vv, wco)


def _ca_kv_bwd(mem, g_mem, mn, kraw, dkn, dvv, wckv, g_ck):
    M, D = mem.shape

    def body(m_ref, g_ref, mn_ref, kr_ref, dkn_ref, dv_ref, w_ref, gk_ref, dw_ref, dgk_ref, dgm_ref):
        mn = mn_ref[...]
        dmn = jnp.zeros((M, D), F32)
        gk_rows = None
        for h in range(CA_HEADS):
            dkr, gr = _norm_bwd(dkn_ref[h], kr_ref[h], gk_ref[...])
            gk_rows = gr if gk_rows is None else gk_rows + gr
            dkb = dkr.astype(BF)
            dvb = dv_ref[h].astype(BF)
            dw_ref[h] = _tn(mn, dkb).astype(BF)
            dw_ref[CA_HEADS + h] = _tn(mn, dvb).astype(BF)
            dmn = dmn + _nt(dkb, w_ref[h]) + _nt(dvb, w_ref[CA_HEADS + h])
        dgk_ref[...] = jnp.sum(gk_rows, axis=0, keepdims=True)
        mf = m_ref[...]
        dgm_ref[...] = jnp.sum(dmn * (mf * _rstd(mf)), axis=0, keepdims=True)

    return pl.pallas_call(
        body, name="ca_kv_bwd",
        out_shape=[S((2 * CA_HEADS, D, CA_HD), BF), S((1, CA_HD), F32), S((1, D), F32)],
        compiler_params=pltpu.CompilerParams(vmem_limit_bytes=VMEM_LIMIT))(mem, g_mem, mn, kraw, dkn, dvv, wckv, g_ck)


def _after(g, token):
    return g if token is None else g + token[0:1, 0:1]


def _local_step(x, mem, target, small, weights, emit):
    T, D = x.shape
    p = small
    bf128 = jnp.pad(p["b_f"], ((0, 0), (0, LANES - FOX_HEADS)))
    b_st = p["b_s"].T

    wup1 = weights("ffn1_up", x)["wup1"]
    a1, h1 = _ffn_up("ffn1_up", x, p["g_ffn1"], wup1)
    wdn1 = weights("ffn1_dn", h1)["wdn1"]
    x1 = _ffn_down("ffn1_down", a1, wdn1, x)
    wm = weights("mix", x1)
    z, h2 = _mix_proj(x1, p["g_mix"], wm["wz"])
    qf, ka, va, yg, rs = _mix_prep(z, bf128, p["g_q"], p["g_k"], p["g_sgu"], p["w_s"], b_st, p["g_gmlp_o"])
    attn, lse = _fox_fwd(qf, ka, va)
    x2 = _mix_out(attn, yg, p["g_fox_o"], wm["wout"], x1)
    wc = weights("ca", x2)
    mn, kraw, ckn, cvv = _ca_kv(mem, p["g_mem"], wc["wckv"], p["g_ck"])
    x3 = _ca_fwd(x2, p["g_ca"], wc["wcq"], p["g_cq"], ckn, cvv, wc["wco"])
    w2 = weights("ffn2", x3)
    a2, h4 = _ffn_up("ffn2_up", x3, p["g_ffn2"], w2["wup2"])
    dy4, dy4b, sq = _ffn_down_loss("ffn2_down", a2, w2["wdn2"], x3, target)

    gs = {}
    dgu2 = _ffn_bwd_act("ffn2_bwd_act", dy4b, h4, w2["wup2"], w2["wdn2"])
    tok = emit("ffn2", {"wup2": _ffn_dwup("ffn2", h4, dgu2), "wdn2": _ffn_dwdn("ffn2", a2, dy4b)})
    dx3, gs["g_ffn2"] = _ffn_dx("ffn2_dx", dgu2, w2["wup2"], x3, _after(p["g_ffn2"], tok), dy4)

    dx2, dwcq, dwco, dckn, dcvv, gs["g_cq"], gs["g_ca"] = _ca_bwd(
        x2, dx3, p["g_ca"], wc["wcq"], p["g_cq"], ckn, cvv, wc["wco"])
    dwckv, gs["g_ck"], gs["g_mem"] = _ca_kv_bwd(mem, p["g_mem"], mn, kraw, dckn, dcvv, wc["wckv"], p["g_ck"])

    qb, dob, dyg, dwout, gs["g_fox_o"] = _mix_out_bwd(dx2, attn, yg, p["g_fox_o"], wm["wout"], qf, lse)
    dq, dk, dv = _fox_bwd(qb, ka, va, dob)
    dz, gs["g_q"], gs["g_k"], gs["g_sgu"], gs["g_gmlp_o"], gs["w_s"], dbst, dbf = _mix_prep_bwd(
        z, dq, dk, dv, dyg, rs, bf128, p["g_q"], p["g_k"], p["g_sgu"], p["w_s"], b_st, p["g_gmlp_o"])
    gs["b_s"] = dbst.T
    gs["b_f"] = dbf[:, :FOX_HEADS]
    tok_ws = emit("w_s", {"w_s": gs["w_s"]})
    zb = ZW // 3
    dwz = _tn_matmul("mix_dwz", dz, pl.BlockSpec((T, zb), lambda j: (0, j)), h2,
                     S((ZW, D), BF), pl.BlockSpec((zb, D), lambda j: (j, 0)), 3)
    tok = emit("mid", {"wcq": dwcq, "wco": dwco, "wckv": dwckv, "wout": dwout, "wz": dwz})
    dx1, dx1b, gs["g_mix"] = _mix_proj_bwd(dz, wm["wz"], x1, _after(_after(p["g_mix"], tok), tok_ws), dx2)

    dgu1 = _ffn_bwd_act("ffn1_bwd_act", dx1b, h1, wup1, wdn1)
    tok = emit("ffn1_dn", {"wdn1": _ffn_dwdn("ffn1", a1, dx1b)})
    tok = emit("ffn1_up", {"wup1": _ffn_dwup("ffn1", h1, dgu1, after=tok)})
    dx0, gs["g_ffn1"] = _ffn_dx("ffn1_dx", dgu1, wup1, x, _after(p["g_ffn1"], tok), dx1)
    return sq, dx0, gs


MESH = pl.DeviceIdType.MESH
HBM_SPEC = pl.BlockSpec(memory_space=pltpu.HBM)
N_PEER = N_DEV - 1


def _place():
    return lax.axis_index("x"), lax.axis_index("y"), lax.axis_index("c")


def _slot(px, py, pc):
    return 4 * px + 2 * py + pc


SEM_SPEC = pl.BlockSpec(memory_space=pltpu.SEMAPHORE)
ANY_SPEC = pl.BlockSpec(memory_space=pl.ANY)
DATAFLOW = pltpu.SideEffectType.DATAFLOW_SIDE_EFFECTING


def _hbm(a):
    return pltpu.with_memory_space_constraint(a, pltpu.HBM)


def _peer(x, y, c, r):
    return (1 - x if r & 4 else x, 1 - y if r & 2 else y, 1 - c if r & 1 else c)


def _place_own(srcs, whole):
    my = _slot(*_place())
    lands = []
    for s in srcs:
        blk = s[None] if whole else lax.dynamic_slice_in_dim(s, my, 1, 0)
        shape = (N_DEV,) + s.shape if whole else s.shape
        lands.append(lax.dynamic_update_slice_in_dim(lax.empty(shape, s.dtype), blk, my, 0))
    return lands


ALL_PEERS = tuple(range(1, N_DEV))
NEAR_PEERS = (1, 2, 4, 6)
SAME_CORE = (2, 4, 6)


def _copy_start(name, srcs, lands, whole, peers=None):
    n = len(srcs)
    peers = peers or [ALL_PEERS] * n

    def body(*refs):
        src, land = refs[:n], refs[n:2 * n]
        send, recv = refs[2 * n:3 * n], refs[3 * n:4 * n]
        token = refs[6 * n]
        x, y, c = _place()
        my = _slot(x, y, c)
        for a in range(n):
            for r in peers[a]:
                p = _peer(x, y, c, r)
                pltpu.make_async_remote_copy(
                    src_ref=src[a] if whole else src[a].at[_slot(*p)], dst_ref=land[a].at[my],
                    send_sem=send[a].at[r - 1], recv_sem=recv[a].at[r - 1], device_id=p, device_id_type=MESH).start()
        token[...] = jnp.zeros_like(token)

    out = pl.pallas_call(
        body, name=name,
        out_shape=([pltpu.SemaphoreType.DMA((N_PEER,))] * (2 * n)
                   + [pltpu.HBM(s.shape, s.dtype) for s in srcs] + [pltpu.HBM(s.shape, s.dtype) for s in lands]
                   + [S((8, LANES), F32)]),
        in_specs=[HBM_SPEC] * (2 * n),
        out_specs=[SEM_SPEC] * (2 * n) + [HBM_SPEC] * (2 * n) + [pl.BlockSpec(memory_space=pltpu.VMEM)],
        input_output_aliases={i: 2 * n + i for i in range(2 * n)},
        compiler_params=pltpu.CompilerParams(has_side_effects=DATAFLOW),
    )(*[_hbm(s) for s in srcs], *[_hbm(s) for s in lands])
    return out[:n], out[n:2 * n], out[2 * n:3 * n], out[3 * n:4 * n], out[4 * n]


def _copy_wait(name, srcs, lands, send, recv, after, whole, peers=None, with_srcs=False):
    n = len(srcs)
    peers = peers or [ALL_PEERS] * n

    def body(*refs):
        src, land = refs[:n], refs[n:2 * n]
        snd, rcv = refs[2 * n:3 * n], refs[3 * n:4 * n]
        x, y, c = _place()
        for a in range(n):
            for r in peers[a]:
                p = _peer(x, y, c, r)
                ps = _slot(*p)
                cp = pltpu.make_async_remote_copy(
                    src_ref=src[a] if whole else src[a].at[ps], dst_ref=land[a].at[ps],
                    send_sem=snd[a].at[r - 1], recv_sem=rcv[a].at[r - 1], device_id=p, device_id_type=MESH)
                cp.wait_send()
                cp.wait_recv()

    out = pl.pallas_call(
        body, name=name,
        out_shape=[pltpu.HBM(s.shape, s.dtype) for s in srcs] + [pltpu.HBM(s.shape, s.dtype) for s in lands],
        in_specs=[HBM_SPEC] * (2 * n) + [SEM_SPEC] * (2 * n) + [ANY_SPEC],
        out_specs=[HBM_SPEC] * (2 * n),
        input_output_aliases={i: i for i in range(2 * n)},
        compiler_params=pltpu.CompilerParams(has_side_effects=DATAFLOW),
    )(*srcs, *lands, *send, *recv, after)
    return (out[:n], out[n:]) if with_srcs else out[n:]


def _forward_start(name, lands):
    n = len(lands)

    def body(*refs):
        land = refs[:n]
        send, recv = refs[n:2 * n], refs[2 * n:3 * n]
        token = refs[4 * n]
        x, y, c = _place()
        for a in range(n):
            for r in SAME_CORE:
                blk = land[a].at[_slot(*_peer(x, y, c, r))]
                pltpu.make_async_remote_copy(
                    src_ref=blk, dst_ref=blk, send_sem=send[a].at[r - 1], recv_sem=recv[a].at[r - 1],
                    device_id=(x, y, 1 - c), device_id_type=MESH).start()
        token[...] = jnp.zeros_like(token)

    out = pl.pallas_call(
        body, name=name,
        out_shape=([pltpu.SemaphoreType.DMA((N_PEER,))] * (2 * n) + [pltpu.HBM(s.shape, s.dtype) for s in lands]
                   + [S((8, LANES), F32)]),
        in_specs=[HBM_SPEC] * n,
        out_specs=[SEM_SPEC] * (2 * n) + [HBM_SPEC] * n + [pl.BlockSpec(memory_space=pltpu.VMEM)],
        input_output_aliases={i: 2 * n + i for i in range(n)},
        compiler_params=pltpu.CompilerParams(has_side_effects=DATAFLOW),
    )(*[_hbm(s) for s in lands])
    return out[:n], out[n:2 * n], out[2 * n:3 * n], out[3 * n]


def _forward_wait(name, lands, send, recv, after):
    n = len(lands)

    def body(*refs):
        land = refs[:n]
        snd, rcv = refs[n:2 * n], refs[2 * n:3 * n]
        x, y, c = _place()
        for a in range(n):
            for r in SAME_CORE:
                cp = pltpu.make_async_remote_copy(
                    src_ref=land[a].at[_slot(*_peer(x, y, c, r))], dst_ref=land[a].at[_slot(*_peer(x, y, c, r | 1))],
                    send_sem=snd[a].at[r - 1], recv_sem=rcv[a].at[r - 1], device_id=(x, y, 1 - c),
                    device_id_type=MESH)
                cp.wait_send()
                cp.wait_recv()

    return pl.pallas_call(
        body, name=name,
        out_shape=[pltpu.HBM(s.shape, s.dtype) for s in lands],
        in_specs=[HBM_SPEC] * n + [SEM_SPEC] * (2 * n) + [ANY_SPEC],
        out_specs=[HBM_SPEC] * n,
        input_output_aliases={i: i for i in range(n)},
        compiler_params=pltpu.CompilerParams(has_side_effects=DATAFLOW),
    )(*lands, *send, *recv, after)


def _adamw(w, g, m, v):
    m2 = ADAM_B1 * m + (1.0 - ADAM_B1) * g
    v2 = ADAM_B2 * v + (1.0 - ADAM_B2) * (g * g)
    m_hat = m2 / (1.0 - ADAM_B1 ** ADAM_STEP)
    v_hat = v2 / (1.0 - ADAM_B2 ** ADAM_STEP)
    delta = -ADAM_LR * (m_hat / (jnp.sqrt(v_hat) + ADAM_EPS) + ADAM_WD * w)
    return delta, m2, v2


def _adamw_big(name, slots, w, m, v, own=None):
    R, C = w.shape
    tr = next((t for t in (256, 352) if R % t == 0), R)

    def finish(g, w_ref, m_ref, v_ref, g_ref, d_ref, m2_ref, v2_ref):
        d, m2, v2 = _adamw(w_ref[...], g, m_ref[...], v_ref[...])
        g_ref[...] = g
        d_ref[...] = d
        m2_ref[...] = m2
        v2_ref[...] = v2

    if own is None:
        def body(s_ref, *refs):
            g = s_ref[0].astype(F32)
            for k in range(1, N_DEV):
                g = g + s_ref[k].astype(F32)
            finish(g, *refs)

        row = pl.BlockSpec((tr, C), lambda i: (i, 0))
        return pl.pallas_call(
            body, name=name, grid=(R // tr,),
            in_specs=[pl.BlockSpec((N_DEV, tr, C), lambda i: (0, i, 0)), row, row, row],
            out_specs=[row] * 4, out_shape=[S((R, C), F32)] * 4,
            compiler_params=_cp(1))(slots, w, m, v)

    def body(my_ref, s_ref, own_ref, *refs):
        mine = own_ref[...]
        g = None
        for k in range(N_DEV):
            part = jnp.where(my_ref[0] == k, mine, s_ref[k]).astype(F32)
            g = part if g is None else g + part
        finish(g, *refs)

    row = pl.BlockSpec((tr, C), lambda i, my_ref: (i, 0))
    my = jnp.reshape(_slot(*_place()), (1,)).astype(jnp.int32)
    return pl.pallas_call(
        body, name=name,
        grid_spec=pltpu.PrefetchScalarGridSpec(
            num_scalar_prefetch=1, grid=(R // tr,),
            in_specs=[pl.BlockSpec((N_DEV, tr, C), lambda i, my_ref: (0, i, 0)),
                      pl.BlockSpec((None, tr, C), lambda i, my_ref: (my_ref[0], i, 0)), row, row, row],
            out_specs=[row] * 4),
        out_shape=[S((R, C), F32)] * 4, compiler_params=_cp(1))(my, slots, own, w, m, v)


TINY_ROWS = (("b_s", 8), ("g_ffn1", 8), ("g_mix", 8), ("g_ca", 8), ("g_mem", 8), ("g_ffn2", 8), ("g_sgu", 4),
             ("g_fox_o", 4), ("g_gmlp_o", 4), ("g_cq", 2), ("g_ck", 2), ("g_q", 1), ("g_k", 1), ("b_f", 1),
             ("loss", 1))
TINY_P = 72


def _tiny_pieces(width):
    return [(j, slice(j * LANES, min((j + 1) * LANES, width))) for j in range(-(-width // LANES))]


def _pack_tiny(grads, sq):
    names = [n for n, _ in TINY_ROWS if n != "loss"]

    def body(*refs):
        ins, sq_ref, o_ref = refs[:len(names)], refs[len(names)], refs[len(names) + 1]
        o_ref[...] = jnp.zeros_like(o_ref)
        at = 0
        for ref, (name, r) in zip(ins, TINY_ROWS):
            if name == "b_s":
                o_ref[at:at + r, :] = ref[...]
            else:
                for j, cols in _tiny_pieces(ref.shape[1]):
                    o_ref[at + j:at + j + 1, 0:cols.stop - cols.start] = ref[:, cols]
            at += r
        o_ref[at:at + 1, :] = sq_ref[0:1, :]

    return pl.pallas_call(body, name="tiny_pack", out_shape=S((TINY_P, LANES), F32))(
        *[grads[n] for n in names], sq)


def _adamw_tiny(slots, w, m, v):
    names = [n for n, _ in TINY_ROWS if n != "loss"]
    k = len(names)

    def body(s_ref, *refs):
        ins, outs, loss_ref = refs[:3 * k], refs[3 * k:7 * k], refs[7 * k]
        g_all = s_ref[0]
        for d in range(1, N_DEV):
            g_all = g_all + s_ref[d]
        at = 0
        for i, (name, r) in enumerate(TINY_ROWS[:k]):
            w_ref, m_ref, v_ref = ins[i], ins[k + i], ins[2 * k + i]
            o = outs[4 * i:4 * i + 4]
            if name == "b_s":
                pieces = [(slice(at, at + r), slice(0, LANES), (slice(None), slice(None)))]
            else:
                pieces = [(slice(at + j, at + j + 1), slice(0, c.stop - c.start), (slice(None), c))
                          for j, c in _tiny_pieces(w_ref.shape[1])]
            for rows, lanes, dst in pieces:
                g = g_all[rows, lanes]
                res = (g,) + _adamw(w_ref[dst], g, m_ref[dst], v_ref[dst])
                for ref, val in zip(o, res):
                    ref[dst] = val
            at += r
        loss_ref[...] = g_all[at:at + 1, :]

    shapes = [S(w[n].shape, F32) for n in names]
    out = pl.pallas_call(
        body, name="adamw_tiny", out_shape=[s for s in shapes for _ in range(4)] + [S((1, LANES), F32)],
    )(slots, *[w[n] for n in names], *[m[n] for n in names], *[v[n] for n in names])
    stores = ({}, {}, {}, {})
    for i, n in enumerate(names):
        for store, t in zip(stores, out[4 * i:4 * i + 4]):
            store[n] = t
    return stores, out[4 * k]


WEIGHTS =('g_ffn1', 'w_ffn1_in', 'w_ffn1_out', 'g_mix', 'w_in', 'b_f', 'g_q', 'g_k', 'g_sgu', 'w_s', 'b_s',
           'g_fox_o', 'g_gmlp_o', 'w_out', 'g_ca', 'g_mem', 'w_cq', 'w_ckv', 'g_cq', 'g_ck', 'w_co', 'g_ffn2',
           'w_ffn2_in', 'w_ffn2_out')
BIG = ('w_ffn1_in', 'w_ffn1_out', 'w_in', 'w_out', 'w_cq', 'w_ckv', 'w_co', 'w_ffn2_in', 'w_ffn2_out')
TRANSPOSED = ('w_ffn1_in', 'w_in', 'w_ffn2_in')
TWO_LEVEL = ('w_ffn1_in', 'w_in')
GATHER_GROUPS = {"ffn1_up": ("w_ffn1_in",), "ffn1_dn": ("w_ffn1_out",), "mix": ("w_in", "w_out"),
                 "ca": ("w_cq", "w_ckv", "w_co"), "ffn2": ("w_ffn2_in", "w_ffn2_out")}
QKV_W = 3 * FOX_W
UV_OFF = QKV_W + FOX_HEADS


def kernel(x, mem, g_ffn1, w_ffn1_in, w_ffn1_out, g_mix, w_in, b_f, g_q, g_k, g_sgu, w_s, b_s, g_fox_o, g_gmlp_o, w_out, g_ca, g_mem, w_cq, w_ckv, g_cq, g_ck, w_co, g_ffn2, w_ffn2_in, w_ffn2_out, loss_target, m_g_ffn1, m_w_ffn1_in, m_w_ffn1_out, m_g_mix, m_w_in, m_b_f, m_g_q, m_g_k, m_g_sgu, m_w_s, m_b_s, m_g_fox_o, m_g_gmlp_o, m_w_out, m_g_ca, m_g_mem, m_w_cq, m_w_ckv, m_g_cq, m_g_ck, m_w_co, m_g_ffn2, m_w_ffn2_in, m_w_ffn2_out, v_g_ffn1, v_w_ffn1_in, v_w_ffn1_out, v_g_mix, v_w_in, v_b_f, v_g_q, v_g_k, v_g_sgu, v_w_s, v_b_s, v_g_fox_o, v_g_gmlp_o, v_w_out, v_g_ca, v_g_mem, v_w_cq, v_w_ckv, v_g_cq, v_g_ck, v_w_co, v_g_ffn2, v_w_ffn2_in, v_w_ffn2_out):
    args = dict(locals())
    w = {n: args[n] for n in WEIGHTS}
    mo = {n: args["m_" + n] for n in WEIGHTS}
    vo = {n: args["v_" + n] for n in WEIGHTS}
    D = D_MODEL

    def local(n, a):
        return a[0].T if n in TRANSPOSED else a[0]

    shards = [local(n, w[n]).astype(BF) for n in BIG]
    fb = shards[0].shape[0]
    g_peers = [NEAR_PEERS if n in TWO_LEVEL else ALL_PEERS for n in BIG]
    g_snd, g_rcv, g_src, g_land, g_token = _copy_start("gather_start", shards, _place_own(shards, True), True,
                                                       peers=g_peers)
    handles = {n: (g_src[i], g_land[i], g_snd[i], g_rcv[i]) for i, n in enumerate(BIG)}

    tiny_names = [n for n, _ in TINY_ROWS if n != "loss"]

    def weights(group, after):
        names = GATHER_GROUPS[group]
        hs = [handles[n] for n in names]
        got = list(_copy_wait("gather_wait_" + group, [h[0] for h in hs], [h[1] for h in hs], [h[2] for h in hs],
                              [h[3] for h in hs], after, True, peers=[g_peers[BIG.index(n)] for n in names]))
        passed = [i for i, n in enumerate(names) if n in TWO_LEVEL]
        if passed:
            f_snd, f_rcv, f_land, f_token = _forward_start("gather_pass_start_" + group, [got[i] for i in passed])
            for i, t in zip(passed, _forward_wait("gather_pass_wait_" + group, f_land, f_snd, f_rcv, f_token)):
                got[i] = t
        got = dict(zip(names, got))
        if group == "ffn1_up":
            return {"wup1": got["w_ffn1_in"].reshape(2, N_FFN_BLK, fb, D)}
        if group == "ffn1_dn":
            return {"wdn1": got["w_ffn1_out"].reshape(N_FFN_BLK, fb, D)}
        if group == "mix":
            full = got["w_in"].reshape(-1, D)
            wz = jnp.concatenate([full[:QKV_W], full[UV_OFF:], full[QKV_W:UV_OFF],
                                  jnp.zeros((LANES - FOX_HEADS, D), BF)], axis=0)
            return {"wz": wz, "wout": got["w_out"].reshape(D, D)}
        if group == "ca":
            return {"wcq": got["w_cq"].reshape(D, D), "wco": got["w_co"].reshape(D, D), "wckv": got["w_ckv"]}
        return {"wup2": got["w_ffn2_in"].reshape(2, N_FFN_BLK, fb, D),
                "wdn2": got["w_ffn2_out"].reshape(N_FFN_BLK, fb, D)}

    flying = {}

    def emit(group, g):
        if group == "w_s":
            part = [g["w_s"].reshape(-1, LANES)]
            *copies, token = _copy_start("w_s_start", part, _place_own(part, True), True)
            flying[group] = copies
            return token
        if group == "ffn2":
            parts = {"w_ffn2_in": g["wup2"], "w_ffn2_out": g["wdn2"].reshape(N_DEV, -1, D)}
        elif group == "ffn1_dn":
            parts = {"w_ffn1_out": g["wdn1"].reshape(N_DEV, -1, D)}
        elif group == "ffn1_up":
            parts = {"w_ffn1_in": g["wup1"]}
        else:
            gz = g["wz"]
            g_in = jnp.concatenate([gz[:QKV_W], gz[Z_F:Z_F + FOX_HEADS], gz[QKV_W:Z_F]], axis=0)
            parts = {"w_in": g_in.reshape(N_DEV, -1, D).astype(BF),
                     "w_out": g["wout"].reshape(N_DEV, -1, D), "w_cq": g["wcq"].reshape(N_DEV, -1, D),
                     "w_co": g["wco"].reshape(N_DEV, -1, D), "w_ckv": g["wckv"]}
        names = list(parts)
        srcs = [parts[n] for n in names]
        *copies, token = _copy_start("exchange_start_" + group, srcs, [lax.empty(s.shape, s.dtype) for s in srcs],
                                     False)
        flying[group] = (names, copies)
        return token

    small = {n: (w[n][0] if n == "b_s" else w[n]) for n in tiny_names}
    small["w_s"] = w["w_s"][0]

    sq, dx0, gs = _local_step(x[0], mem[0], loss_target[0], small, weights, emit)

    sm_parts = [_pack_tiny(gs, sq)]
    sm_snd, sm_rcv, sm_src, sm_land, sm_token = _copy_start("tiny_start", sm_parts, _place_own(sm_parts, True), True)

    grad, delta, new_m, new_v = {}, {}, {}, {}

    def update(group, after):
        names, (snd, rcv, srcs, lands) = flying[group]
        owns, slots = _copy_wait("exchange_wait_" + group, srcs, lands, snd, rcv, after, False, with_srcs=True)
        for n, sl, own in zip(names, slots, owns):
            g, d, m2, v2 = _adamw_big("adamw_" + n, sl, local(n, w[n]), local(n, mo[n]), local(n, vo[n]), own=own)
            grad[n], delta[n], new_m[n], new_v[n] = (
                (t.T if n in TRANSPOSED else t).reshape(w[n].shape) for t in (g, d, m2, v2))
        return d

    last = update("ffn2", sm_token)
    last = update("mid", last)
    last = update("ffn1_dn", last)
    last = update("ffn1_up", last)
    ws_snd, ws_rcv, ws_src, ws_land = flying["w_s"]
    ws_all, = _copy_wait("w_s_wait", ws_src, ws_land, ws_snd, ws_rcv, last, True)
    tiny_all, = _copy_wait("tiny_wait", sm_src, sm_land, sm_snd, sm_rcv, ws_all, True)
    ws_shape = w["w_s"].shape
    for store, t in zip((grad, delta, new_m, new_v), _adamw_big(
            "adamw_w_s", ws_all, *[a["w_s"].reshape(-1, LANES) for a in (w, mo, vo)])):
        store["w_s"] = t.reshape(ws_shape)
    stores, loss_row = _adamw_tiny(tiny_all, *[{n: (a[n][0] if n == "b_s" else a[n]) for n in tiny_names}
                                               for a in (w, mo, vo)])
    for store, t in zip((grad, delta, new_m, new_v), stores):
        store.update({n: v.reshape(w[n].shape) for n, v in t.items()})
    loss = loss_row[0, 0] * (0.5 / D)

    return (loss, dx0[None], *[grad[n] for n in WEIGHTS], *[delta[n] for n in WEIGHTS],
            *[new_m[n] for n in WEIGHTS], *[new_v[n] for n in WEIGHTS])
```

```python
import functools

import jax
import jax.numpy as jnp
from jax import lax
from jax.experimental import pallas as pl
from jax.experimental.pallas import tpu as pltpu

F32 = jnp.float32
BF = jnp.bfloat16
S = jax.ShapeDtypeStruct

N_DEV = 8
D_MODEL = 1024
FOX_HEADS, FOX_HD = 8, 64
FOX_W = 512
GMLP_G, GMLP_GD = 8, 64
GMLP_W = 512
CHUNK = 128
CA_HEADS, CA_HD = 4, 256
N_FFN_BLK = 4
ZW = 2688
Z_Q, Z_K, Z_V, Z_U, Z_G, Z_F = 0, 512, 1024, 1536, 2048, 2560
EPS = 1e-6
NEG = -1e30
LANES = 128

ADAM_LR, ADAM_B1, ADAM_B2, ADAM_EPS, ADAM_WD, ADAM_STEP = 0.001, 0.9, 0.999, 1e-08, 0.01, 10

VMEM_LIMIT = 52 * 2 ** 20


def _cp(n_axes):
    return pltpu.CompilerParams(dimension_semantics=("arbitrary",) * n_axes, vmem_limit_bytes=VMEM_LIMIT)


def _nn(a, b):
    return jnp.dot(a, b, preferred_element_type=F32)


def _nt(a, b):
    return lax.dot_general(a, b, (((1,), (1,)), ((), ())), preferred_element_type=F32)


def _tn(a, b):
    return lax.dot_general(a, b, (((0,), (0,)), ((), ())), preferred_element_type=F32)


def _hi(a, b):
    return jnp.dot(a, b, precision=lax.Precision.HIGHEST, preferred_element_type=F32)


def _rstd(x):
    return lax.rsqrt(jnp.mean(x * x, axis=-1, keepdims=True) + EPS)


def _norm_bwd(dy, x, g, r=None):
    r = _rstd(x) if r is None else r
    xh = x * r
    dxh = dy * g
    dx = r * (dxh - xh * jnp.mean(dxh * xh, axis=-1, keepdims=True))
    return dx, dy * xh


def _acc_rows(ref, first, val):
    srow = jnp.sum(val, axis=0, keepdims=True)

    @pl.when(first)
    def _():
        ref[...] = srow

    @pl.when(jnp.logical_not(first))
    def _():
        ref[...] += srow


def _gelu(x):
    c = 0.7978845608028654
    return 0.5 * x * (1.0 + jnp.tanh(c * (x + 0.044715 * x * x * x)))


def _gelu_grad(x):
    c = 0.7978845608028654
    t = jnp.tanh(c * (x + 0.044715 * x * x * x))
    return 0.5 * (1.0 + t) + 0.5 * x * (1.0 - t * t) * c * (1.0 + 3 * 0.044715 * x * x)


def _tile(n, pref):
    return pref if n % pref == 0 else n


def _ffn_up(name, x, g, wup):
    T, D = x.shape
    FB = wup.shape[-2]
    tm = _tile(T, 1024)

    def body(x_ref, g_ref, w_ref, a_ref, h_ref):
        @pl.when(pl.program_id(1) == 0)
        def _():
            xf = x_ref[...]
            h_ref[...] = (xf * _rstd(xf) * g_ref[...]).astype(BF)

        hb = h_ref[...]
        gg = _nt(hb, w_ref[0])
        uu = _nt(hb, w_ref[1])
        a_ref[...] = (gg * jax.nn.sigmoid(gg) * uu).astype(BF)

    return pl.pallas_call(
        body, name=name, grid=(T // tm, N_FFN_BLK),
        in_specs=[pl.BlockSpec((tm, D), lambda i, j: (i, 0)),
                  pl.BlockSpec((1, D), lambda i, j: (0, 0)),
                  pl.BlockSpec((2, None, FB, D), lambda i, j: (0, j, 0, 0))],
        out_specs=[pl.BlockSpec((None, tm, FB), lambda i, j: (j, i, 0)),
                   pl.BlockSpec((tm, D), lambda i, j: (i, 0))],
        out_shape=[S((N_FFN_BLK, T, FB), BF), S((T, D), BF)],
        compiler_params=_cp(2))(x, g, wup)


def _ffn_down(name, a, wdn, x):
    _, T, FB = a.shape
    D = x.shape[1]
    tm = _tile(T, 512)

    def body(a_ref, w_ref, x_ref, o_ref):
        p = _nn(a_ref[0], w_ref[0])
        for j in range(1, N_FFN_BLK):
            p = p + _nn(a_ref[j], w_ref[j])
        o_ref[...] = x_ref[...] + 0.5 * p

    return pl.pallas_call(
        body, name=name, grid=(T // tm,),
        in_specs=[pl.BlockSpec((N_FFN_BLK, tm, FB), lambda i: (0, i, 0)),
                  pl.BlockSpec((N_FFN_BLK, FB, D), lambda i: (0, 0, 0)),
                  pl.BlockSpec((tm, D), lambda i: (i, 0))],
        out_specs=pl.BlockSpec((tm, D), lambda i: (i, 0)),
        out_shape=S((T, D), F32),
        compiler_params=_cp(1))(a, wdn, x)


def _ffn_down_loss(name, a, wdn, x, target):
    _, T, FB = a.shape
    D = x.shape[1]
    tm = _tile(T, 512)

    def body(a_ref, w_ref, x_ref, t_ref, d_ref, db_ref, loss_ref):
        i = pl.program_id(0)
        p = _nn(a_ref[0], w_ref[0])
        for j in range(1, N_FFN_BLK):
            p = p + _nn(a_ref[j], w_ref[j])
        diff = (x_ref[...] + 0.5 * p) - t_ref[...]
        dy = diff * (1.0 / D)
        d_ref[...] = dy
        db_ref[...] = dy.astype(BF)
        sq = jnp.zeros((8, LANES), F32) + jnp.sum(diff * diff)

        @pl.when(i == 0)
        def _():
            loss_ref[...] = sq

        @pl.when(i > 0)
        def _():
            loss_ref[...] += sq

    row = pl.BlockSpec((tm, D), lambda i: (i, 0))
    return pl.pallas_call(
        body, name=name, grid=(T // tm,),
        in_specs=[pl.BlockSpec((N_FFN_BLK, tm, FB), lambda i: (0, i, 0)),
                  pl.BlockSpec((N_FFN_BLK, FB, D), lambda i: (0, 0, 0)), row, row],
        out_specs=[row, row, pl.BlockSpec((8, LANES), lambda i: (0, 0))],
        out_shape=[S((T, D), F32), S((T, D), BF), S((8, LANES), F32)],
        compiler_params=_cp(1))(a, wdn, x, target)


def _ffn_bwd_act(name, dyb, h, wup, wdn):
    T, D = h.shape
    FB = wup.shape[-2]
    tm = _tile(T, 1024)

    def body(d_ref, h_ref, wu_ref, wd_ref, o_ref):
        da = 0.5 * _nt(d_ref[...], wd_ref[...])
        hb = h_ref[...]
        gg = _nt(hb, wu_ref[0])
        uu = _nt(hb, wu_ref[1])
        sg = jax.nn.sigmoid(gg)
        o_ref[0] = (da * uu * (sg * (1.0 + gg * (1.0 - sg)))).astype(BF)
        o_ref[1] = (da * (gg * sg)).astype(BF)

    return pl.pallas_call(
        body, name=name, grid=(T // tm, N_FFN_BLK),
        in_specs=[pl.BlockSpec((tm, D), lambda i, j: (i, 0)),
                  pl.BlockSpec((tm, D), lambda i, j: (i, 0)),
                  pl.BlockSpec((2, None, FB, D), lambda i, j: (0, j, 0, 0)),
                  pl.BlockSpec((None, FB, D), lambda i, j: (j, 0, 0))],
        out_specs=pl.BlockSpec((2, None, tm, FB), lambda i, j: (0, j, i, 0)),
        out_shape=S((2, N_FFN_BLK, T, FB), BF),
        compiler_params=_cp(2))(dyb, h, wup, wdn)


def _ffn_dx(name, dgu, wup, x, g, dy):
    T, D = x.shape
    FB = wup.shape[-2]
    tm = _tile(T, 512)

    def body(d_ref, w_ref, x_ref, g_ref, dy_ref, dx_ref, dg_ref):
        p = None
        for j in range(N_FFN_BLK):
            for half in range(2):
                t = _nn(d_ref[half, j], w_ref[half, j])
                p = t if p is None else p + t
        dx, dgr = _norm_bwd(p, x_ref[...], g_ref[...])
        dx_ref[...] = dx + dy_ref[...]
        _acc_rows(dg_ref, pl.program_id(0) == 0, dgr)

    return pl.pallas_call(
        body, name=name, grid=(T // tm,),
        in_specs=[pl.BlockSpec((2, N_FFN_BLK, tm, FB), lambda i: (0, 0, i, 0)),
                  pl.BlockSpec((2, N_FFN_BLK, FB, D), lambda i: (0, 0, 0, 0), pipeline_mode=pl.Buffered(1)),
                  pl.BlockSpec((tm, D), lambda i: (i, 0)),
                  pl.BlockSpec((1, D), lambda i: (0, 0)),
                  pl.BlockSpec((tm, D), lambda i: (i, 0))],
        out_specs=[pl.BlockSpec((tm, D), lambda i: (i, 0)),
                   pl.BlockSpec((1, D), lambda i: (0, 0))],
        out_shape=[S((T, D), F32), S((1, D), F32)],
        compiler_params=_cp(1))(dgu, wup, x, g, dy)


def _tn_matmul(name, a, a_spec, b, out_shape, out_spec, n_blocks, scale=1.0, after=None):
    extra = [] if after is None else [after]

    def body(a_ref, b_ref, *rest):
        o_ref = rest[-1]
        o_ref[...] = (_tn(a_ref[...], b_ref[...]) * scale).astype(o_ref.dtype)

    return pl.pallas_call(
        body, name=name, grid=(n_blocks,),
        in_specs=[a_spec, pl.BlockSpec(b.shape, lambda j: (0, 0), pipeline_mode=pl.Buffered(1))]
        + [pl.BlockSpec((8, LANES), lambda j: (0, 0)) for _ in extra],
        out_specs=out_spec, out_shape=out_shape, compiler_params=_cp(1))(a, b, *extra)


def _ffn_dwup(name, h, dgu, after=None):
    T, D = h.shape
    FB = dgu.shape[-1]
    return _tn_matmul(
        name + "_dwup", dgu.reshape(2 * N_FFN_BLK, T, FB), pl.BlockSpec((None, T, FB), lambda j: (j, 0, 0)), h,
        S((2 * N_FFN_BLK, FB, D), BF), pl.BlockSpec((None, FB, D), lambda j: (j, 0, 0)), 2 * N_FFN_BLK,
        after=after)


def _ffn_dwdn(name, a, dyb):
    _, T, FB = a.shape
    D = dyb.shape[1]
    return _tn_matmul(
        name + "_dwdn", a, pl.BlockSpec((None, T, FB), lambda j: (j, 0, 0)), dyb,
        S((N_FFN_BLK, FB, D), BF), pl.BlockSpec((None, FB, D), lambda j: (j, 0, 0)), N_FFN_BLK, scale=0.5)


def _mix_proj(x, g, wz):
    T, D = x.shape
    tm = _tile(T, 512)

    def body(x_ref, g_ref, w_ref, z_ref, h_ref):
        xf = x_ref[...]
        hb = (xf * _rstd(xf) * g_ref[...]).astype(BF)
        h_ref[...] = hb
        z_ref[...] = _nt(hb, w_ref[...])

    return pl.pallas_call(
        body, name="mix_proj", grid=(T // tm,),
        in_specs=[pl.BlockSpec((tm, D), lambda i: (i, 0)),
                  pl.BlockSpec((1, D), lambda i: (0, 0)),
                  pl.BlockSpec((ZW, D), lambda i: (0, 0))],
        out_specs=[pl.BlockSpec((tm, ZW), lambda i: (i, 0)),
                   pl.BlockSpec((tm, D), lambda i: (i, 0))],
        out_shape=[S((T, ZW), F32), S((T, D), BF)],
        compiler_params=_cp(1))(x, g, wz)


def _tri(n, lower):
    r = lax.broadcasted_iota(jnp.int32, (n, n), 0)
    c = lax.broadcasted_iota(jnp.int32, (n, n), 1)
    return (r >= c) if lower else (r <= c)


def _spatial_mix(vgn_b, ws_ref, bst, tm):
    tril = _tri(CHUNK, True)
    wms = [jnp.where(tril, ws_ref[g], 0.0).astype(BF) for g in range(GMLP_G)]
    rows = []
    for c in range(tm // CHUNK):
        cols = []
        for g in range(GMLP_G):
            vs = vgn_b[c * CHUNK:(c + 1) * CHUNK, g * GMLP_GD:(g + 1) * GMLP_GD]
            cols.append(_nn(wms[g], vs) + bst[:, g:g + 1])
        rows.append(jnp.concatenate(cols, axis=1))
    return jnp.concatenate(rows, axis=0), wms


HB = 128
AUG_W = FOX_HEADS * HB
COL_A, COL_B, COL_C = 64, 67, 70
RS_Q, RS_K, RS_V, RS_O = 0, 8, 16, 17


def _spread_matrix():
    r = jnp.arange(FOX_W)
    return (jnp.arange(AUG_W)[None, :] == ((r // FOX_HD) * HB + r % FOX_HD)[:, None]).astype(BF)


def _piece_matrix(col):
    r = jnp.arange(LANES)
    dst = jnp.where(r < 3 * FOX_HEADS, (r % FOX_HEADS) * HB + col + r // FOX_HEADS, -1)
    return (jnp.arange(AUG_W)[None, :] == dst[:, None]).astype(BF)


def _ones_row(cols):
    c = jnp.arange(AUG_W) % HB
    hit = functools.reduce(jnp.logical_or, [(c >= a) & (c < a + 3) for a in cols])
    return hit.astype(F32)[None, :]


def _pieces(x):
    lane = lax.broadcasted_iota(jnp.int32, x.shape, 1)
    x = jnp.where(lane < FOX_HEADS, x, 0.0)
    hi = x.astype(BF).astype(F32)
    r1 = x - hi
    mid = r1.astype(BF).astype(F32)
    lo = (r1 - mid).astype(BF).astype(F32)
    return (hi + pltpu.roll(mid, FOX_HEADS, 1) + pltpu.roll(lo, 2 * FOX_HEADS, 1)).astype(BF)


def _mix_prep(z, bf128, g_q, g_k, g_sgu, w_s, b_st, g_go):
    T = z.shape[0]
    tm = _tile(T, 512)
    spread, pc_q, pc_k = _spread_matrix(), _piece_matrix(COL_A), _piece_matrix(COL_B)
    one_q, one_k, one_v = _ones_row([COL_B]), _ones_row([COL_A, COL_C]), _ones_row([COL_A])

    def body(z_ref, bf_ref, gq_ref, gk_ref, gs_ref, ws_ref, bst_ref, go_ref, sp_ref, pq_ref, pk_ref, oq_ref, ok_ref,
             ov_ref, q_ref, k_ref, v_ref, y_ref, rs_ref, carry_ref, qn_sc, kn_sc):
        i = pl.program_id(0)

        @pl.when(i == 0)
        def _():
            carry_ref[...] = jnp.zeros_like(carry_ref)

        rs_ref[...] = jnp.zeros_like(rs_ref)
        for h in range(FOX_HEADS):
            hs = slice(h * FOX_HD, (h + 1) * FOX_HD)
            qh = z_ref[:, Z_Q + h * FOX_HD:Z_Q + (h + 1) * FOX_HD]
            kh = z_ref[:, Z_K + h * FOX_HD:Z_K + (h + 1) * FOX_HD]
            rq, rk = _rstd(qh), _rstd(kh)
            rs_ref[:, RS_Q + h:RS_Q + h + 1] = rq
            rs_ref[:, RS_K + h:RS_K + h + 1] = rk
            qn_sc[:, hs] = (qh * rq * gq_ref[...] * 0.125).astype(BF)
            kn_sc[:, hs] = (kh * rk * gk_ref[...]).astype(BF)

        fl = z_ref[:, Z_F:Z_F + LANES] + bf_ref[...]
        logf = jnp.minimum(fl, 0.0) - jnp.log1p(jnp.exp(-jnp.abs(fl)))
        csum = _hi(_tri(tm, True).astype(F32), logf) + carry_ref[...]
        carry_ref[...] = csum[tm - 1:tm, :]
        sp = sp_ref[...]
        q_ref[...] = (_nn(qn_sc[...], sp) + _nn(_pieces(csum), pq_ref[...]) + oq_ref[...]).astype(BF)
        k_ref[...] = (_nn(kn_sc[...], sp) + _nn(_pieces(-csum), pk_ref[...]) + ok_ref[...]).astype(BF)
        v_ref[...] = (_nn(z_ref[:, Z_V:Z_V + FOX_W].astype(BF), sp) + ov_ref[...]).astype(BF)

        u = _gelu(z_ref[:, Z_U:Z_U + GMLP_W])
        vg = _gelu(z_ref[:, Z_G:Z_G + GMLP_W])
        rv = _rstd(vg)
        vgn = (vg * rv * gs_ref[...]).astype(BF)
        mixed, _ = _spatial_mix(vgn, ws_ref, bst_ref[...], tm)
        sgu = u * mixed
        ro = _rstd(sgu)
        y_ref[...] = (sgu * ro * go_ref[...]).astype(BF)
        rs_ref[:, RS_V:RS_V + 1] = rv
        rs_ref[:, RS_O:RS_O + 1] = ro

    row = lambda i: (i, 0)
    fix2 = lambda i: (0, 0)
    return pl.pallas_call(
        body, name="mix_prep", grid=(T // tm,),
        in_specs=[pl.BlockSpec((tm, ZW), row),
                  pl.BlockSpec((1, LANES), fix2), pl.BlockSpec((1, FOX_HD), fix2), pl.BlockSpec((1, FOX_HD), fix2),
                  pl.BlockSpec((1, GMLP_W), fix2), pl.BlockSpec((GMLP_G, CHUNK, CHUNK), lambda i: (0, 0, 0)),
                  pl.BlockSpec((CHUNK, GMLP_G), fix2), pl.BlockSpec((1, GMLP_W), fix2),
                  pl.BlockSpec((FOX_W, AUG_W), fix2), pl.BlockSpec((LANES, AUG_W), fix2),
                  pl.BlockSpec((LANES, AUG_W), fix2), pl.BlockSpec((1, AUG_W), fix2), pl.BlockSpec((1, AUG_W), fix2),
                  pl.BlockSpec((1, AUG_W), fix2)],
        out_specs=[pl.BlockSpec((tm, AUG_W), row), pl.BlockSpec((tm, AUG_W), row), pl.BlockSpec((tm, AUG_W), row),
                   pl.BlockSpec((tm, GMLP_W), row), pl.BlockSpec((tm, LANES), row)],
        out_shape=[S((T, AUG_W), BF), S((T, AUG_W), BF), S((T, AUG_W), BF), S((T, GMLP_W), BF), S((T, LANES), F32)],
        scratch_shapes=[pltpu.VMEM((1, LANES), F32), pltpu.VMEM((tm, FOX_W), BF), pltpu.VMEM((tm, FOX_W), BF)],
        compiler_params=_cp(1))(z, bf128, g_q, g_k, g_sgu, w_s, b_st, g_go, spread, pc_q, pc_k, one_q, one_k, one_v)


def _fox_fwd(q, k, v):
    T = q.shape[0]
    tq = _tile(T, 1024)
    nq = T // tq

    def body(q_ref, k_ref, v_ref, o_ref, lse_ref, m_sc, acc_sc):
        i, j = pl.program_id(0), pl.program_id(1)

        @pl.when(j == 0)
        def _():
            m_sc[...] = jnp.full(m_sc.shape, NEG, F32)
            acc_sc[...] = jnp.zeros_like(acc_sc)

        def step(masked):
            mask = _tri(tq, True) if masked else None
            for h in range(FOX_HEADS):
                hb = slice(h * HB, (h + 1) * HB)
                s = _nt(q_ref[:, hb], k_ref[:, hb])
                if masked:
                    s = jnp.where(mask, s, NEG)
                m_prev = m_sc[h]
                m_new = jnp.maximum(m_prev, jnp.broadcast_to(jnp.max(s, axis=1, keepdims=True), (tq, HB)))
                p = jnp.exp(s - jnp.tile(m_new, (1, tq // HB))).astype(BF)
                acc_sc[:, hb] = jnp.exp(m_prev - m_new) * acc_sc[:, hb] + _nn(p, v_ref[:, hb])
                m_sc[h] = m_new

        @pl.when(j < i)
        def _():
            step(False)

        @pl.when(j == i)
        def _():
            step(True)
            lse_ref[...] = jnp.zeros_like(lse_ref)
            for h in range(FOX_HEADS):
                l = acc_sc[:, h * HB + COL_A:h * HB + COL_A + 1]
                o_ref[:, h * FOX_HD:(h + 1) * FOX_HD] = acc_sc[:, h * HB:h * HB + FOX_HD] / l
                lse_ref[:, h:h + 1] = m_sc[h][:, 0:1] + jnp.log(l)

    qi = lambda i, j: (i, 0)
    kj = lambda i, j: (jnp.minimum(i, j), 0)
    return pl.pallas_call(
        body, name="fox_fwd", grid=(nq, nq),
        in_specs=[pl.BlockSpec((tq, AUG_W), qi), pl.BlockSpec((tq, AUG_W), kj), pl.BlockSpec((tq, AUG_W), kj)],
        out_specs=[pl.BlockSpec((tq, FOX_W), qi), pl.BlockSpec((tq, LANES), qi)],
        out_shape=[S((T, FOX_W), F32), S((T, LANES), F32)],
        scratch_shapes=[pltpu.VMEM((FOX_HEADS, tq, HB), F32), pltpu.VMEM((tq, AUG_W), F32)],
        compiler_params=_cp(2))(q, k, v)


def _fox_bwd(q, k, v, dob):
    T = q.shape[0]
    tq = _tile(T, 512)
    nq = T // tq
    n_sweeps = 1
    half = AUG_W // n_sweeps
    hpg = FOX_HEADS // n_sweeps

    pairs = [(j, i) for j in range(nq) for i in range(j, nq)]
    jt = jnp.asarray([p[0] for p in pairs], jnp.int32)
    it = jnp.asarray([p[1] for p in pairs], jnp.int32)

    def body(jt_ref, it_ref, q_ref, k_ref, v_ref, do_ref, dq_ref, dk_ref, dv_ref, dq_sc):
        t = pl.program_id(1)
        j, i = jt_ref[t], it_ref[t]

        @pl.when(t == 0)
        def _():
            dq_sc[...] = jnp.zeros_like(dq_sc)

        @pl.when(i == j)
        def _():
            dk_ref[...] = jnp.zeros_like(dk_ref)
            dv_ref[...] = jnp.zeros_like(dv_ref)

        def step(masked):
            rows = pl.ds(pl.multiple_of(i * tq, tq), tq)
            mask = _tri(tq, True) if masked else None
            for h in range(hpg):
                hb = slice(h * HB, (h + 1) * HB)
                qh, kh, vh, doh = q_ref[:, hb], k_ref[:, hb], v_ref[:, hb], do_ref[:, hb]
                s = _nt(qh, kh)
                if masked:
                    s = jnp.where(mask, s, NEG)
                p = jnp.exp(s)
                dsb = (p * _nt(doh, vh)).astype(BF)
                dv_ref[:, hb] += _tn(p.astype(BF), doh)
                dk_ref[:, hb] += _tn(dsb, qh)
                dq_sc[rows, hb] += _nn(dsb, kh)

        @pl.when(i > j)
        def _():
            step(False)

        @pl.when(i == j)
        def _():
            step(True)
            dq_ref[...] = dq_sc[pl.ds(pl.multiple_of(j * tq, tq), tq), :]

    qi = pl.BlockSpec((tq, half), lambda g, t, jt_ref, it_ref: (it_ref[t], g))
    kj = pl.BlockSpec((tq, half), lambda g, t, jt_ref, it_ref: (jt_ref[t], g))
    return pl.pallas_call(
        body, name="fox_bwd",
        grid_spec=pltpu.PrefetchScalarGridSpec(
            num_scalar_prefetch=2, grid=(n_sweeps, len(pairs)), in_specs=[qi, kj, kj, qi], out_specs=[kj, kj, kj],
            scratch_shapes=[pltpu.VMEM((T, half), F32)]),
        out_shape=[S((T, AUG_W), F32), S((T, AUG_W), F32), S((T, AUG_W), F32)],
        compiler_params=_cp(2))(jt, it, q, k, v, dob)


def _mix_out(attn, yg, g_fo, wout, x):
    T, D = x.shape
    tm = _tile(T, 1024)

    def body(a_ref, y_ref, g_ref, w_ref, x_ref, o_ref):
        at = a_ref[...]
        yf = (at * _rstd(at) * g_ref[...]).astype(BF)
        o_ref[...] = x_ref[...] + _nn(yf, w_ref[:FOX_W, :]) + _nn(y_ref[...], w_ref[FOX_W:, :])

    row = lambda i: (i, 0)
    return pl.pallas_call(
        body, name="mix_out", grid=(T // tm,),
        in_specs=[pl.BlockSpec((tm, FOX_W), row), pl.BlockSpec((tm, GMLP_W), row),
                  pl.BlockSpec((1, FOX_W), lambda i: (0, 0)), pl.BlockSpec((D, D), lambda i: (0, 0)),
                  pl.BlockSpec((tm, D), row)],
        out_specs=pl.BlockSpec((tm, D), row),
        out_shape=S((T, D), F32),
        compiler_params=_cp(1))(attn, yg, g_fo, wout, x)


def _mix_out_bwd(dx, attn, yg, g_fo, wout, qf, lse):
    T, D = dx.shape
    tm = _tile(T, 512)
    n = T // tm
    spread, pc_l, pc_d = _spread_matrix(), _piece_matrix(COL_C), _piece_matrix(COL_A)

    def body(dx_ref, a_ref, y_ref, g_ref, w_ref, qf_ref, lse_ref, sp_ref, pl_ref, pd_ref,
             qb_ref, dob_ref, dyg_ref, dw_ref, dg_ref, acc_ref, dsum_ref):
        i = pl.program_id(0)
        dxb = dx_ref[...].astype(BF)
        at = a_ref[...]
        yf = (at * _rstd(at) * g_ref[...]).astype(BF)
        dy = _nt(dxb, w_ref[...])
        p_top = _tn(yf, dxb)
        p_bot = _tn(y_ref[...], dxb)

        @pl.when(i == 0)
        def _():
            acc_ref[:FOX_W, :] = p_top
            acc_ref[FOX_W:, :] = p_bot

        @pl.when(i > 0)
        def _():
            acc_ref[:FOX_W, :] += p_top
            acc_ref[FOX_W:, :] += p_bot

        @pl.when(i == n - 1)
        def _():
            dw_ref[...] = acc_ref[...].astype(BF)

        dat, dgr = _norm_bwd(dy[:, :FOX_W], at, g_ref[...])
        _acc_rows(dg_ref, i == 0, dgr)
        dyg_ref[...] = dy[:, FOX_W:]
        prod = dat * at
        dsum_ref[...] = jnp.zeros_like(dsum_ref)
        for h in range(FOX_HEADS):
            dsum_ref[:, h:h + 1] = jnp.sum(prod[:, h * FOX_HD:(h + 1) * FOX_HD], axis=1, keepdims=True)
        dob_ref[...] = (_nn(dat.astype(BF), sp_ref[...]) + _nn(_pieces(-dsum_ref[...]), pd_ref[...])).astype(BF)
        qb_ref[...] = (qf_ref[...].astype(F32) + _nn(_pieces(-lse_ref[...]), pl_ref[...])).astype(BF)

    row = lambda i: (i, 0)
    fix = lambda i: (0, 0)
    return pl.pallas_call(
        body, name="mix_out_bwd", grid=(n,),
        in_specs=[pl.BlockSpec((tm, D), row), pl.BlockSpec((tm, FOX_W), row), pl.BlockSpec((tm, GMLP_W), row),
                  pl.BlockSpec((1, FOX_W), fix), pl.BlockSpec((D, D), fix), pl.BlockSpec((tm, AUG_W), row),
                  pl.BlockSpec((tm, LANES), row), pl.BlockSpec((FOX_W, AUG_W), fix), pl.BlockSpec((LANES, AUG_W), fix),
                  pl.BlockSpec((LANES, AUG_W), fix)],
        out_specs=[pl.BlockSpec((tm, AUG_W), row), pl.BlockSpec((tm, AUG_W), row), pl.BlockSpec((tm, GMLP_W), row),
                   pl.BlockSpec((D, D), fix), pl.BlockSpec((1, FOX_W), fix)],
        out_shape=[S((T, AUG_W), BF), S((T, AUG_W), BF), S((T, GMLP_W), F32), S((D, D), BF), S((1, FOX_W), F32)],
        scratch_shapes=[pltpu.VMEM((D, D), F32), pltpu.VMEM((tm, LANES), F32)],
        compiler_params=_cp(1))(dx, attn, yg, g_fo, wout, qf, lse, spread, pc_l, pc_d)


def _mix_prep_bwd(z, dq, dk, dv, dyg, rs, bf128, g_q, g_k, g_sgu, w_s, b_st, g_go):
    T = z.shape[0]
    tm = _tile(T, 512)
    n = T // tm

    def body(z_ref, dq_ref, dk_ref, dv_ref, dyg_ref, rs_ref, bf_ref, gq_ref, gk_ref, gs_ref, ws_ref,
             bst_ref, go_ref, dz_ref, dgq_ref, dgk_ref, dgs_ref, dgo_ref, dws_ref, dbst_ref, dbf_ref, carry_ref):
        i = pl.program_id(0)
        first = i == 0
        rs = rs_ref[...]

        @pl.when(first)
        def _():
            carry_ref[...] = jnp.zeros_like(carry_ref)

        lane = lax.broadcasted_iota(jnp.int32, (tm, LANES), 1)
        dc = jnp.zeros((tm, LANES), F32)
        gq_rows, gk_rows = [], []
        for h in range(FOX_HEADS):
            hp = slice(h * HB, h * HB + FOX_HD)
            dqh, gqr = _norm_bwd(dq_ref[:, hp] * 0.125, z_ref[:, Z_Q + h * FOX_HD:Z_Q + (h + 1) * FOX_HD], gq_ref[...],
                                 rs[:, RS_Q + h:RS_Q + h + 1])
            dkh, gkr = _norm_bwd(dk_ref[:, hp], z_ref[:, Z_K + h * FOX_HD:Z_K + (h + 1) * FOX_HD], gk_ref[...],
                                 rs[:, RS_K + h:RS_K + h + 1])
            dz_ref[:, Z_Q + h * FOX_HD:Z_Q + (h + 1) * FOX_HD] = dqh.astype(BF)
            dz_ref[:, Z_K + h * FOX_HD:Z_K + (h + 1) * FOX_HD] = dkh.astype(BF)
            dz_ref[:, Z_V + h * FOX_HD:Z_V + (h + 1) * FOX_HD] = dv_ref[:, hp].astype(BF)
            dch = dq_ref[:, h * HB + COL_A:h * HB + COL_A + 1] - dk_ref[:, h * HB + COL_B:h * HB + COL_B + 1]
            dc = jnp.where(lane == h, dch, dc)
            gq_rows.append(gqr)
            gk_rows.append(gkr)
        _acc_rows(dgq_ref, first, functools.reduce(lambda a, b: a + b, gq_rows))
        _acc_rows(dgk_ref, first, functools.reduce(lambda a, b: a + b, gk_rows))

        dlogf = _hi(_tri(tm, False).astype(F32), dc) + carry_ref[...]
        carry_ref[...] = dlogf[0:1, :]
        fl = z_ref[:, Z_F:Z_F + LANES] + bf_ref[...]
        lane = lax.broadcasted_iota(jnp.int32, (tm, LANES), 1)
        df = jnp.where(lane < FOX_HEADS, dlogf * jax.nn.sigmoid(-fl), 0.0)
        dz_ref[:, Z_F:Z_F + LANES] = df.astype(BF)
        _acc_rows(dbf_ref, first, df)

        u_pre = z_ref[:, Z_U:Z_U + GMLP_W]
        vg_pre = z_ref[:, Z_G:Z_G + GMLP_W]
        u = _gelu(u_pre)
        vg = _gelu(vg_pre)
        rv = rs[:, RS_V:RS_V + 1]
        vgn = (vg * rv * gs_ref[...]).astype(BF)
        bst = bst_ref[...]
        mixed, wms = _spatial_mix(vgn, ws_ref, bst, tm)
        sgu = u * mixed
        dsgu, gor = _norm_bwd(dyg_ref[...], sgu, go_ref[...], rs[:, RS_O:RS_O + 1])
        _acc_rows(dgo_ref, first, gor)
        du = dsgu * mixed
        dmixed = dsgu * u
        dmb = dmixed.astype(BF)
        tril = _tri(CHUNK, True)
        dvgn_rows = []
        dws = [None] * GMLP_G
        dbs = [None] * GMLP_G
        for c in range(tm // CHUNK):
            cs = slice(c * CHUNK, (c + 1) * CHUNK)
            cols = []
            for g in range(GMLP_G):
                gs = slice(g * GMLP_GD, (g + 1) * GMLP_GD)
                dmc = dmb[cs, gs]
                pw = _nt(dmc, vgn[cs, gs])
                pb = jnp.sum(dmixed[cs, gs], axis=1, keepdims=True)
                dws[g] = pw if dws[g] is None else dws[g] + pw
                dbs[g] = pb if dbs[g] is None else dbs[g] + pb
                cols.append(_tn(wms[g], dmc))
            dvgn_rows.append(jnp.concatenate(cols, axis=1))
        dvgn = jnp.concatenate(dvgn_rows, axis=0)
        dbs_t = jnp.concatenate(dbs, axis=1)
        for g in range(GMLP_G):
            dwg = jnp.where(tril, dws[g], 0.0)

            @pl.when(first)
            def _():
                dws_ref[g] = dwg

            @pl.when(jnp.logical_not(first))
            def _():
                dws_ref[g] += dwg

        @pl.when(first)
        def _():
            dbst_ref[...] = dbs_t

        @pl.when(jnp.logical_not(first))
        def _():
            dbst_ref[...] += dbs_t

        dvg, gsr = _norm_bwd(dvgn, vg, gs_ref[...], rv)
        _acc_rows(dgs_ref, first, gsr)
        dz_ref[:, Z_U:Z_U + GMLP_W] = (du * _gelu_grad(u_pre)).astype(BF)
        dz_ref[:, Z_G:Z_G + GMLP_W] = (dvg * _gelu_grad(vg_pre)).astype(BF)

    rev = lambda i: (n - 1 - i, 0)
    fix = lambda i: (0, 0)
    fix3 = lambda i: (0, 0, 0)
    return pl.pallas_call(
        body, name="mix_prep_bwd", grid=(n,),
        in_specs=[pl.BlockSpec((tm, ZW), rev), pl.BlockSpec((tm, AUG_W), rev), pl.BlockSpec((tm, AUG_W), rev),
                  pl.BlockSpec((tm, AUG_W), rev), pl.BlockSpec((tm, GMLP_W), rev), pl.BlockSpec((tm, LANES), rev),
                  pl.BlockSpec((1, LANES), fix), pl.BlockSpec((1, FOX_HD), fix), pl.BlockSpec((1, FOX_HD), fix),
                  pl.BlockSpec((1, GMLP_W), fix), pl.BlockSpec((GMLP_G, CHUNK, CHUNK), fix3),
                  pl.BlockSpec((CHUNK, GMLP_G), fix), pl.BlockSpec((1, GMLP_W), fix)],
        out_specs=[pl.BlockSpec((tm, ZW), rev), pl.BlockSpec((1, FOX_HD), fix), pl.BlockSpec((1, FOX_HD), fix),
                   pl.BlockSpec((1, GMLP_W), fix), pl.BlockSpec((1, GMLP_W), fix),
                   pl.BlockSpec((GMLP_G, CHUNK, CHUNK), fix3), pl.BlockSpec((CHUNK, GMLP_G), fix),
                   pl.BlockSpec((1, LANES), fix)],
        out_shape=[S((T, ZW), BF), S((1, FOX_HD), F32), S((1, FOX_HD), F32), S((1, GMLP_W), F32), S((1, GMLP_W), F32),
                   S((GMLP_G, CHUNK, CHUNK), F32), S((CHUNK, GMLP_G), F32), S((1, LANES), F32)],
        scratch_shapes=[pltpu.VMEM((1, LANES), F32)],
        compiler_params=_cp(1))(z, dq, dk, dv, dyg, rs, bf128, g_q, g_k, g_sgu, w_s, b_st, g_go)


def _mix_proj_bwd(dz, wz, x, g, dy):
    T, D = x.shape
    tm = _tile(T, 512)

    def body(dz_ref, w_ref, x_ref, g_ref, dy_ref, dx_ref, dxb_ref, dg_ref):
        dh = _nn(dz_ref[...], w_ref[...])
        dx, dgr = _norm_bwd(dh, x_ref[...], g_ref[...])
        dx = dx + dy_ref[...]
        dx_ref[...] = dx
        dxb_ref[...] = dx.astype(BF)
        _acc_rows(dg_ref, pl.program_id(0) == 0, dgr)

    row = lambda i: (i, 0)
    fix = lambda i: (0, 0)
    return pl.pallas_call(
        body, name="mix_proj_bwd", grid=(T // tm,),
        in_specs=[pl.BlockSpec((tm, ZW), row), pl.BlockSpec((ZW, D), fix), pl.BlockSpec((tm, D), row),
                  pl.BlockSpec((1, D), fix), pl.BlockSpec((tm, D), row)],
        out_specs=[pl.BlockSpec((tm, D), row), pl.BlockSpec((tm, D), row), pl.BlockSpec((1, D), fix)],
        out_shape=[S((T, D), F32), S((T, D), BF), S((1, D), F32)],
        compiler_params=_cp(1))(dz, wz, x, g, dy)


def _ca_kv(mem, g_mem, wckv, g_ck):
    M, D = mem.shape

    def body(m_ref, g_ref, w_ref, gk_ref, mn_ref, kr_ref, kn_ref, v_ref):
        mf = m_ref[...]
        mn = (mf * _rstd(mf) * g_ref[...]).astype(BF)
        mn_ref[...] = mn
        for h in range(CA_HEADS):
            kr = _nn(mn, w_ref[h])
            kr_ref[h] = kr
            kn_ref[h] = (kr * _rstd(kr) * gk_ref[...]).astype(BF)
            v_ref[h] = _nn(mn, w_ref[CA_HEADS + h]).astype(BF)

    hd = (CA_HEADS, M, CA_HD)
    return pl.pallas_call(
        body, name="ca_kv", out_shape=[S((M, D), BF), S(hd, F32), S(hd, BF), S(hd, BF)],
        compiler_params=pltpu.CompilerParams(vmem_limit_bytes=VMEM_LIMIT))(mem, g_mem, wckv, g_ck)


def _ca_tile_fwd(xt, gca, wcq, gcq, kn_ref, v_ref):
    hb = (xt * _rstd(xt) * gca).astype(BF)
    qc = _nn(hb, wcq)
    qr, qn, ps = [], [], []
    for h in range(CA_HEADS):
        qh = qc[:, h * CA_HD:(h + 1) * CA_HD]
        qnh = (qh * _rstd(qh) * gcq * 0.0625).astype(BF)
        s = _nt(qnh, kn_ref[h])
        e = jnp.exp(s - jnp.max(s, axis=1, keepdims=True))
        ps.append(e / jnp.sum(e, axis=1, keepdims=True))
        qr.append(qh)
        qn.append(qnh)
    return hb, qr, qn, ps


def _ca_fwd(x, g_ca, wcq, g_cq, kn, vv, wco):
    T, D = x.shape
    M = kn.shape[1]
    tm = _tile(T, 1024)

    def body(x_ref, gca_ref, wcq_ref, gcq_ref, kn_ref, v_ref, wco_ref, o_ref, ob_sc):
        xt = x_ref[...]
        _, _, _, ps = _ca_tile_fwd(xt, gca_ref[...], wcq_ref[...], gcq_ref[...], kn_ref, v_ref)
        for h in range(CA_HEADS):
            ob_sc[:, h * CA_HD:(h + 1) * CA_HD] = _nn(ps[h].astype(BF), v_ref[h]).astype(BF)
        o_ref[...] = xt + _nn(ob_sc[...], wco_ref[...])

    row = lambda i: (i, 0)
    fix = lambda i: (0, 0)
    fix3 = lambda i: (0, 0, 0)
    return pl.pallas_call(
        body, name="ca_fwd", grid=(T // tm,),
        in_specs=[pl.BlockSpec((tm, D), row), pl.BlockSpec((1, D), fix), pl.BlockSpec((D, D), fix),
                  pl.BlockSpec((1, CA_HD), fix), pl.BlockSpec((CA_HEADS, M, CA_HD), fix3),
                  pl.BlockSpec((CA_HEADS, M, CA_HD), fix3), pl.BlockSpec((D, D), fix)],
        out_specs=pl.BlockSpec((tm, D), row), out_shape=S((T, D), F32),
        scratch_shapes=[pltpu.VMEM((tm, D), BF)],
        compiler_params=_cp(1))(x, g_ca, wcq, g_cq, kn, vv, wco)


def _ca_bwd(x, dy, g_ca, wcq, g_cq, kn, vv, wco):
    T, D = x.shape
    M = kn.shape[1]
    tm = _tile(T, 512)
    n = T // tm

    def body(x_ref, dy_ref, gca_ref, wcq_ref, gcq_ref, kn_ref, v_ref, wco_ref,
             dx_ref, dwq_ref, dwo_ref, dkn_ref, dv_ref, dgcq_ref, dgca_ref, aq_sc, ao_sc, ob_sc, dq_sc):
        i = pl.program_id(0)
        first = i == 0
        xt = x_ref[...]
        dyt = dy_ref[...]
        dyb = dyt.astype(BF)
        hb, qr, qn, ps = _ca_tile_fwd(xt, gca_ref[...], wcq_ref[...], gcq_ref[...], kn_ref, v_ref)
        do = _nt(dyb, wco_ref[...])
        gcq_rows = None
        for h in range(CA_HEADS):
            hs = slice(h * CA_HD, (h + 1) * CA_HD)
            p = ps[h]
            pb = p.astype(BF)
            ob_sc[:, hs] = _nn(pb, v_ref[h]).astype(BF)
            doh = do[:, hs].astype(BF)
            dp = _nt(doh, v_ref[h])
            ds = (p * (dp - jnp.sum(dp * p, axis=1, keepdims=True))).astype(BF)
            dvh = _tn(pb, doh)
            dkh = _tn(ds, qn[h])

            @pl.when(first)
            def _():
                dv_ref[h] = dvh
                dkn_ref[h] = dkh

            @pl.when(jnp.logical_not(first))
            def _():
                dv_ref[h] += dvh
                dkn_ref[h] += dkh

            dqn = _nn(ds, kn_ref[h]) * 0.0625
            dqh, gr = _norm_bwd(dqn, qr[h], gcq_ref[...])
            gcq_rows = gr if gcq_rows is None else gcq_rows + gr
            dq_sc[:, hs] = dqh.astype(BF)
        _acc_rows(dgcq_ref, first, gcq_rows)
        dqb = dq_sc[...]
        p_o = _tn(ob_sc[...], dyb)
        p_q = _tn(hb, dqb)

        @pl.when(first)
        def _():
            ao_sc[...] = p_o
            aq_sc[...] = p_q

        @pl.when(jnp.logical_not(first))
        def _():
            ao_sc[...] += p_o
            aq_sc[...] += p_q

        @pl.when(i == n - 1)
        def _():
            dwo_ref[...] = ao_sc[...].astype(BF)
            dwq_ref[...] = aq_sc[...].astype(BF)

        dh = _nt(dqb, wcq_ref[...])
        dx, gar = _norm_bwd(dh, xt, gca_ref[...])
        dx_ref[...] = dx + dyt
        _acc_rows(dgca_ref, first, gar)

    row = lambda i: (i, 0)
    fix = lambda i: (0, 0)
    fix3 = lambda i: (0, 0, 0)
    hd = (CA_HEADS, M, CA_HD)
    return pl.pallas_call(
        body, name="ca_bwd", grid=(n,),
        in_specs=[pl.BlockSpec((tm, D), row), pl.BlockSpec((tm, D), row), pl.BlockSpec((1, D), fix),
                  pl.BlockSpec((D, D), fix), pl.BlockSpec((1, CA_HD), fix), pl.BlockSpec(hd, fix3),
                  pl.BlockSpec(hd, fix3), pl.BlockSpec((D, D), fix)],
        out_specs=[pl.BlockSpec((tm, D), row), pl.BlockSpec((D, D), fix), pl.BlockSpec((D, D), fix),
                   pl.BlockSpec(hd, fix3), pl.BlockSpec(hd, fix3), pl.BlockSpec((1, CA_HD), fix),
                   pl.BlockSpec((1, D), fix)],
        out_shape=[S((T, D), F32), S((D, D), BF), S((D, D), BF), S(hd, F32), S(hd, F32), S((1, CA_HD), F32),
                   S((1, D), F32)],
        scratch_shapes=[pltpu.VMEM((D, D), F32), pltpu.VMEM((D, D), F32), pltpu.VMEM((tm, D), BF),
                        pltpu.VMEM((tm, D), BF)],
        compiler_params=_cp(1))(x, dy, g_ca, wcq, g_cq, kn, vv, wco)


def _ca_kv_bwd(mem, g_mem, mn, kraw, dkn, dvv, wckv, g_ck):
    M, D = mem.shape

    def body(m_ref, g_ref, mn_ref, kr_ref, dkn_ref, dv_ref, w_ref, gk_ref, dw_ref, dgk_ref, dgm_ref):
        mn = mn_ref[...]
        dmn = jnp.zeros((M, D), F32)
        gk_rows = None
        for h in range(CA_HEADS):
            dkr, gr = _norm_bwd(dkn_ref[h], kr_ref[h], gk_ref[...])
            gk_rows = gr if gk_rows is None else gk_rows + gr
            dkb = dkr.astype(BF)
            dvb = dv_ref[h].astype(BF)
            dw_ref[h] = _tn(mn, dkb).astype(BF)
            dw_ref[CA_HEADS + h] = _tn(mn, dvb).astype(BF)
            dmn = dmn + _nt(dkb, w_ref[h]) + _nt(dvb, w_ref[CA_HEADS + h])
        dgk_ref[...] = jnp.sum(gk_rows, axis=0, keepdims=True)
        mf = m_ref[...]
        dgm_ref[...] = jnp.sum(dmn * (mf * _rstd(mf)), axis=0, keepdims=True)

    return pl.pallas_call(
        body, name="ca_kv_bwd",
        out_shape=[S((2 * CA_HEADS, D, CA_HD), BF), S((1, CA_HD), F32), S((1, D), F32)],
        compiler_params=pltpu.CompilerParams(vmem_limit_bytes=VMEM_LIMIT))(mem, g_mem, mn, kraw, dkn, dvv, wckv, g_ck)


def _after(g, token):
    return g if token is None else g + token[0:1, 0:1]


def _local_step(x, mem, target, small, weights, emit):
    T, D = x.shape
    p = small
    bf128 = jnp.pad(p["b_f"], ((0, 0), (0, LANES - FOX_HEADS)))
    b_st = p["b_s"].T

    wup1 = weights("ffn1_up", x)["wup1"]
    a1, h1 = _ffn_up("ffn1_up", x, p["g_ffn1"], wup1)
    wdn1 = weights("ffn1_dn", h1)["wdn1"]
    x1 = _ffn_down("ffn1_down", a1, wdn1, x)
    wm = weights("mix", x1)
    z, h2 = _mix_proj(x1, p["g_mix"], wm["wz"])
    qf, ka, va, yg, rs = _mix_prep(z, bf128, p["g_q"], p["g_k"], p["g_sgu"], p["w_s"], b_st, p["g_gmlp_o"])
    attn, lse = _fox_fwd(qf, ka, va)
    x2 = _mix_out(attn, yg, p["g_fox_o"], wm["wout"], x1)
    wc = weights("ca", x2)
    mn, kraw, ckn, cvv = _ca_kv(mem, p["g_mem"], wc["wckv"], p["g_ck"])
    x3 = _ca_fwd(x2, p["g_ca"], wc["wcq"], p["g_cq"], ckn, cvv, wc["wco"])
    w2 = weights("ffn2", x3)
    a2, h4 = _ffn_up("ffn2_up", x3, p["g_ffn2"], w2["wup2"])
    dy4, dy4b, sq = _ffn_down_loss("ffn2_down", a2, w2["wdn2"], x3, target)

    gs = {}
    dgu2 = _ffn_bwd_act("ffn2_bwd_act", dy4b, h4, w2["wup2"], w2["wdn2"])
    tok = emit("ffn2", {"wup2": _ffn_dwup("ffn2", h4, dgu2), "wdn2": _ffn_dwdn("ffn2", a2, dy4b)})
    dx3, gs["g_ffn2"] = _ffn_dx("ffn2_dx", dgu2, w2["wup2"], x3, _after(p["g_ffn2"], tok), dy4)

    dx2, dwcq, dwco, dckn, dcvv, gs["g_cq"], gs["g_ca"] = _ca_bwd(
        x2, dx3, p["g_ca"], wc["wcq"], p["g_cq"], ckn, cvv, wc["wco"])
    dwckv, gs["g_ck"], gs["g_mem"] = _ca_kv_bwd(mem, p["g_mem"], mn, kraw, dckn, dcvv, wc["wckv"], p["g_ck"])

    qb, dob, dyg, dwout, gs["g_fox_o"] = _mix_out_bwd(dx2, attn, yg, p["g_fox_o"], wm["wout"], qf, lse)
    dq, dk, dv = _fox_bwd(qb, ka, va, dob)
    dz, gs["g_q"], gs["g_k"], gs["g_sgu"], gs["g_gmlp_o"], gs["w_s"], dbst, dbf = _mix_prep_bwd(
        z, dq, dk, dv, dyg, rs, bf128, p["g_q"], p["g_k"], p["g_sgu"], p["w_s"], b_st, p["g_gmlp_o"])
    gs["b_s"] = dbst.T
    gs["b_f"] = dbf[:, :FOX_HEADS]
    tok_ws = emit("w_s", {"w_s": gs["w_s"]})
    zb = ZW // 3
    dwz = _tn_matmul("mix_dwz", dz, pl.BlockSpec((T, zb), lambda j: (0, j)), h2,
                     S((ZW, D), BF), pl.BlockSpec((zb, D), lambda j: (j, 0)), 3)
    tok = emit("mid", {"wcq": dwcq, "wco": dwco, "wckv": dwckv, "wout": dwout, "wz": dwz})
    dx1, dx1b, gs["g_mix"] = _mix_proj_bwd(dz, wm["wz"], x1, _after(_after(p["g_mix"], tok), tok_ws), dx2)

    dgu1 = _ffn_bwd_act("ffn1_bwd_act", dx1b, h1, wup1, wdn1)
    tok = emit("ffn1_dn", {"wdn1": _ffn_dwdn("ffn1", a1, dx1b)})
    tok = emit("ffn1_up", {"wup1": _ffn_dwup("ffn1", h1, dgu1, after=tok)})
    dx0, gs["g_ffn1"] = _ffn_dx("ffn1_dx", dgu1, wup1, x, _after(p["g_ffn1"], tok), dx1)
    return sq, dx0, gs


MESH = pl.DeviceIdType.MESH
HBM_SPEC = pl.BlockSpec(memory_space=pltpu.HBM)
N_PEER = N_DEV - 1


def _place():
    return lax.axis_index("x"), lax.axis_index("y"), lax.axis_index("c")


def _slot(px, py, pc):
    return 4 * px + 2 * py + pc


SEM_SPEC = pl.BlockSpec(memory_space=pltpu.SEMAPHORE)
ANY_SPEC = pl.BlockSpec(memory_space=pl.ANY)
DATAFLOW = pltpu.SideEffectType.DATAFLOW_SIDE_EFFECTING


def _hbm(a):
    return pltpu.with_memory_space_constraint(a, pltpu.HBM)


def _peer(x, y, c, r):
    return (1 - x if r & 4 else x, 1 - y if r & 2 else y, 1 - c if r & 1 else c)


def _place_own(srcs, whole):
    my = _slot(*_place())
    lands = []
    for s in srcs:
        blk = s[None] if whole else lax.dynamic_slice_in_dim(s, my, 1, 0)
        shape = (N_DEV,) + s.shape if whole else s.shape
        lands.append(lax.dynamic_update_slice_in_dim(lax.empty(shape, s.dtype), blk, my, 0))
    return lands


ALL_PEERS = tuple(range(1, N_DEV))
NEAR_PEERS = (1, 2, 4, 6)
SAME_CORE = (2, 4, 6)


def _copy_start(name, srcs, lands, whole, peers=None):
    n = len(srcs)
    peers = peers or [ALL_PEERS] * n

    def body(*refs):
        src, land = refs[:n], refs[n:2 * n]
        send, recv = refs[2 * n:3 * n], refs[3 * n:4 * n]
        token = refs[6 * n]
        x, y, c = _place()
        my = _slot(x, y, c)
        for a in range(n):
            for r in peers[a]:
                p = _peer(x, y, c, r)
                pltpu.make_async_remote_copy(
                    src_ref=src[a] if whole else src[a].at[_slot(*p)], dst_ref=land[a].at[my],
                    send_sem=send[a].at[r - 1], recv_sem=recv[a].at[r - 1], device_id=p, device_id_type=MESH).start()
        token[...] = jnp.zeros_like(token)

    out = pl.pallas_call(
        body, name=name,
        out_shape=([pltpu.SemaphoreType.DMA((N_PEER,))] * (2 * n)
                   + [pltpu.HBM(s.shape, s.dtype) for s in srcs] + [pltpu.HBM(s.shape, s.dtype) for s in lands]
                   + [S((8, LANES), F32)]),
        in_specs=[HBM_SPEC] * (2 * n),
        out_specs=[SEM_SPEC] * (2 * n) + [HBM_SPEC] * (2 * n) + [pl.BlockSpec(memory_space=pltpu.VMEM)],
        input_output_aliases={i: 2 * n + i for i in range(2 * n)},
        compiler_params=pltpu.CompilerParams(has_side_effects=DATAFLOW),
    )(*[_hbm(s) for s in srcs], *[_hbm(s) for s in lands])
    return out[:n], out[n:2 * n], out[2 * n:3 * n], out[3 * n:4 * n], out[4 * n]


def _copy_wait(name, srcs, lands, send, recv, after, whole, peers=None, with_srcs=False):
    n = len(srcs)
    peers = peers or [ALL_PEERS] * n

    def body(*refs):
        src, land = refs[:n], refs[n:2 * n]
        snd, rcv = refs[2 * n:3 * n], refs[3 * n:4 * n]
        x, y, c = _place()
        for a in range(n):
            for r in peers[a]:
                p = _peer(x, y, c, r)
                ps = _slot(*p)
                cp = pltpu.make_async_remote_copy(
                    src_ref=src[a] if whole else src[a].at[ps], dst_ref=land[a].at[ps],
                    send_sem=snd[a].at[r - 1], recv_sem=rcv[a].at[r - 1], device_id=p, device_id_type=MESH)
                cp.wait_send()
                cp.wait_recv()

    out = pl.pallas_call(
        body, name=name,
        out_shape=[pltpu.HBM(s.shape, s.dtype) for s in srcs] + [pltpu.HBM(s.shape, s.dtype) for s in lands],
        in_specs=[HBM_SPEC] * (2 * n) + [SEM_SPEC] * (2 * n) + [ANY_SPEC],
        out_specs=[HBM_SPEC] * (2 * n),
        input_output_aliases={i: i for i in range(2 * n)},
        compiler_params=pltpu.CompilerParams(has_side_effects=DATAFLOW),
    )(*srcs, *lands, *send, *recv, after)
    return (out[:n], out[n:]) if with_srcs else out[n:]


def _forward_start(name, lands):
    n = len(lands)

    def body(*refs):
        land = refs[:n]
        send, recv = refs[n:2 * n], refs[2 * n:3 * n]
        token = refs[4 * n]
        x, y, c = _place()
        for a in range(n):
            for r in SAME_CORE:
                blk = land[a].at[_slot(*_peer(x, y, c, r))]
                pltpu.make_async_remote_copy(
                    src_ref=blk, dst_ref=blk, send_sem=send[a].at[r - 1], recv_sem=recv[a].at[r - 1],
                    device_id=(x, y, 1 - c), device_id_type=MESH).start()
        token[...] = jnp.zeros_like(token)

    out = pl.pallas_call(
        body, name=name,
        out_shape=([pltpu.SemaphoreType.DMA((N_PEER,))] * (2 * n) + [pltpu.HBM(s.shape, s.dtype) for s in lands]
                   + [S((8, LANES), F32)]),
        in_specs=[HBM_SPEC] * n,
        out_specs=[SEM_SPEC] * (2 * n) + [HBM_SPEC] * n + [pl.BlockSpec(memory_space=pltpu.VMEM)],
        input_output_aliases={i: 2 * n + i for i in range(n)},
        compiler_params=pltpu.CompilerParams(has_side_effects=DATAFLOW),
    )(*[_hbm(s) for s in lands])
    return out[:n], out[n:2 * n], out[2 * n:3 * n], out[3 * n]


def _forward_wait(name, lands, send, recv, after):
    n = len(lands)

    def body(*refs):
        land = refs[:n]
        snd, rcv = refs[n:2 * n], refs[2 * n:3 * n]
        x, y, c = _place()
        for a in range(n):
            for r in SAME_CORE:
                cp = pltpu.make_async_remote_copy(
                    src_ref=land[a].at[_slot(*_peer(x, y, c, r))], dst_ref=land[a].at[_slot(*_peer(x, y, c, r | 1))],
                    send_sem=snd[a].at[r - 1], recv_sem=rcv[a].at[r - 1], device_id=(x, y, 1 - c),
                    device_id_type=MESH)
                cp.wait_send()
                cp.wait_recv()

    return pl.pallas_call(
        body, name=name,
        out_shape=[pltpu.HBM(s.shape, s.dtype) for s in lands],
        in_specs=[HBM_SPEC] * n + [SEM_SPEC] * (2 * n) + [ANY_SPEC],
        out_specs=[HBM_SPEC] * n,
        input_output_aliases={i: i for i in range(n)},
        compiler_params=pltpu.CompilerParams(has_side_effects=DATAFLOW),
    )(*lands, *send, *recv, after)


def _adamw(w, g, m, v):
    m2 = ADAM_B1 * m + (1.0 - ADAM_B1) * g
    v2 = ADAM_B2 * v + (1.0 - ADAM_B2) * (g * g)
    m_hat = m2 / (1.0 - ADAM_B1 ** ADAM_STEP)
    v_hat = v2 / (1.0 - ADAM_B2 ** ADAM_STEP)
    delta = -ADAM_LR * (m_hat / (jnp.sqrt(v_hat) + ADAM_EPS) + ADAM_WD * w)
    return delta, m2, v2


def _adamw_big(name, slots, w, m, v, own=None):
    R, C = w.shape
    tr = next((t for t in (256, 352) if R % t == 0), R)

    def finish(g, w_ref, m_ref, v_ref, g_ref, d_ref, m2_ref, v2_ref):
        d, m2, v2 = _adamw(w_ref[...], g, m_ref[...], v_ref[...])
        g_ref[...] = g
        d_ref[...] = d
        m2_ref[...] = m2
        v2_ref[...] = v2

    if own is None:
        def body(s_ref, *refs):
            g = s_ref[0].astype(F32)
            for k in range(1, N_DEV):
                g = g + s_ref[k].astype(F32)
            finish(g, *refs)

        row = pl.BlockSpec((tr, C), lambda i: (i, 0))
        return pl.pallas_call(
            body, name=name, grid=(R // tr,),
            in_specs=[pl.BlockSpec((N_DEV, tr, C), lambda i: (0, i, 0)), row, row, row],
            out_specs=[row] * 4, out_shape=[S((R, C), F32)] * 4,
            compiler_params=_cp(1))(slots, w, m, v)

    def body(my_ref, s_ref, own_ref, *refs):
        mine = own_ref[...]
        g = None
        for k in range(N_DEV):
            part = jnp.where(my_ref[0] == k, mine, s_ref[k]).astype(F32)
            g = part if g is None else g + part
        finish(g, *refs)

    row = pl.BlockSpec((tr, C), lambda i, my_ref: (i, 0))
    my = jnp.reshape(_slot(*_place()), (1,)).astype(jnp.int32)
    return pl.pallas_call(
        body, name=name,
        grid_spec=pltpu.PrefetchScalarGridSpec(
            num_scalar_prefetch=1, grid=(R // tr,),
            in_specs=[pl.BlockSpec((N_DEV, tr, C), lambda i, my_ref: (0, i, 0)),
                      pl.BlockSpec((None, tr, C), lambda i, my_ref: (my_ref[0], i, 0)), row, row, row],
            out_specs=[row] * 4),
        out_shape=[S((R, C), F32)] * 4, compiler_params=_cp(1))(my, slots, own, w, m, v)


TINY_ROWS = (("b_s", 8), ("g_ffn1", 8), ("g_mix", 8), ("g_ca", 8), ("g_mem", 8), ("g_ffn2", 8), ("g_sgu", 4),
             ("g_fox_o", 4), ("g_gmlp_o", 4), ("g_cq", 2), ("g_ck", 2), ("g_q", 1), ("g_k", 1), ("b_f", 1),
             ("loss", 1))
TINY_P = 72


def _tiny_pieces(width):
    return [(j, slice(j * LANES, min((j + 1) * LANES, width))) for j in range(-(-width // LANES))]


def _pack_tiny(grads, sq):
    names = [n for n, _ in TINY_ROWS if n != "loss"]

    def body(*refs):
        ins, sq_ref, o_ref = refs[:len(names)], refs[len(names)], refs[len(names) + 1]
        o_ref[...] = jnp.zeros_like(o_ref)
        at = 0
        for ref, (name, r) in zip(ins, TINY_ROWS):
            if name == "b_s":
                o_ref[at:at + r, :] = ref[...]
            else:
                for j, cols in _tiny_pieces(ref.shape[1]):
                    o_ref[at + j:at + j + 1, 0:cols.stop - cols.start] = ref[:, cols]
            at += r
        o_ref[at:at + 1, :] = sq_ref[0:1, :]

    return pl.pallas_call(body, name="tiny_pack", out_shape=S((TINY_P, LANES), F32))(
        *[grads[n] for n in names], sq)


def _adamw_tiny(slots, w, m, v):
    names = [n for n, _ in TINY_ROWS if n != "loss"]
    k = len(names)

    def body(s_ref, *refs):
        ins, outs, loss_ref = refs[:3 * k], refs[3 * k:7 * k], refs[7 * k]
        g_all = s_ref[0]
        for d in range(1, N_DEV):
            g_all = g_all + s_ref[d]
        at = 0
        for i, (name, r) in enumerate(TINY_ROWS[:k]):
            w_ref, m_ref, v_ref = ins[i], ins[k + i], ins[2 * k + i]
            o = outs[4 * i:4 * i + 4]
            if name == "b_s":
                pieces = [(slice(at, at + r), slice(0, LANES), (slice(None), slice(None)))]
            else:
                pieces = [(slice(at + j, at + j + 1), slice(0, c.stop - c.start), (slice(None), c))
                          for j, c in _tiny_pieces(w_ref.shape[1])]
            for rows, lanes, dst in pieces:
                g = g_all[rows, lanes]
                res = (g,) + _adamw(w_ref[dst], g, m_ref[dst], v_ref[dst])
                for ref, val in zip(o, res):
                    ref[dst] = val
            at += r
        loss_ref[...] = g_all[at:at + 1, :]

    shapes = [S(w[n].shape, F32) for n in names]
    out = pl.pallas_call(
        body, name="adamw_tiny", out_shape=[s for s in shapes for _ in range(4)] + [S((1, LANES), F32)],
    )(slots, *[w[n] for n in names], *[m[n] for n in names], *[v[n] for n in names])
    stores = ({}, {}, {}, {})
    for i, n in enumerate(names):
        for store, t in zip(stores, out[4 * i:4 * i + 4]):
            store[n] = t
    return stores, out[4 * k]


WEIGHTS =('g_ffn1', 'w_ffn1_in', 'w_ffn1_out', 'g_mix', 'w_in', 'b_f', 'g_q', 'g_k', 'g_sgu', 'w_s', 'b_s',
           'g_fox_o', 'g_gmlp_o', 'w_out', 'g_ca', 'g_mem', 'w_cq', 'w_ckv', 'g_cq', 'g_ck', 'w_co', 'g_ffn2',
           'w_ffn2_in', 'w_ffn2_out')
BIG = ('w_ffn1_in', 'w_ffn1_out', 'w_in', 'w_out', 'w_cq', 'w_ckv', 'w_co', 'w_ffn2_in', 'w_ffn2_out')
TRANSPOSED = ('w_ffn1_in', 'w_in', 'w_ffn2_in')
TWO_LEVEL = ('w_ffn1_in', 'w_in')
GATHER_GROUPS = {"ffn1_up": ("w_ffn1_in",), "ffn1_dn": ("w_ffn1_out",), "mix": ("w_in", "w_out"),
                 "ca": ("w_cq", "w_ckv", "w_co"), "ffn2": ("w_ffn2_in", "w_ffn2_out")}
QKV_W = 3 * FOX_W
UV_OFF = QKV_W + FOX_HEADS


def kernel(x, mem, g_ffn1, w_ffn1_in, w_ffn1_out, g_mix, w_in, b_f, g_q, g_k, g_sgu, w_s, b_s, g_fox_o, g_gmlp_o, w_out, g_ca, g_mem, w_cq, w_ckv, g_cq, g_ck, w_co, g_ffn2, w_ffn2_in, w_ffn2_out, loss_target, m_g_ffn1, m_w_ffn1_in, m_w_ffn1_out, m_g_mix, m_w_in, m_b_f, m_g_q, m_g_k, m_g_sgu, m_w_s, m_b_s, m_g_fox_o, m_g_gmlp_o, m_w_out, m_g_ca, m_g_mem, m_w_cq, m_w_ckv, m_g_cq, m_g_ck, m_w_co, m_g_ffn2, m_w_ffn2_in, m_w_ffn2_out, v_g_ffn1, v_w_ffn1_in, v_w_ffn1_out, v_g_mix, v_w_in, v_b_f, v_g_q, v_g_k, v_g_sgu, v_w_s, v_b_s, v_g_fox_o, v_g_gmlp_o, v_w_out, v_g_ca, v_g_mem, v_w_cq, v_w_ckv, v_g_cq, v_g_ck, v_w_co, v_g_ffn2, v_w_ffn2_in, v_w_ffn2_out):
    args = dict(locals())
    w = {n: args[n] for n in WEIGHTS}
    mo = {n: args["m_" + n] for n in WEIGHTS}
    vo = {n: args["v_" + n] for n in WEIGHTS}
    D = D_MODEL

    def local(n, a):
        return a[0].T if n in TRANSPOSED else a[0]

    shards = [local(n, w[n]).astype(BF) for n in BIG]
    fb = shards[0].shape[0]
    g_peers = [NEAR_PEERS if n in TWO_LEVEL else ALL_PEERS for n in BIG]
    g_snd, g_rcv, g_src, g_land, g_token = _copy_start("gather_start", shards, _place_own(shards, True), True,
                                                       peers=g_peers)
    handles = {n: (g_src[i], g_land[i], g_snd[i], g_rcv[i]) for i, n in enumerate(BIG)}

    tiny_names = [n for n, _ in TINY_ROWS if n != "loss"]

    def weights(group, after):
        names = GATHER_GROUPS[group]
        hs = [handles[n] for n in names]
        got = list(_copy_wait("gather_wait_" + group, [h[0] for h in hs], [h[1] for h in hs], [h[2] for h in hs],
                              [h[3] for h in hs], after, True, peers=[g_peers[BIG.index(n)] for n in names]))
        passed = [i for i, n in enumerate(names) if n in TWO_LEVEL]
        if passed:
            f_snd, f_rcv, f_land, f_token = _forward_start("gather_pass_start_" + group, [got[i] for i in passed])
            for i, t in zip(passed, _forward_wait("gather_pass_wait_" + group, f_land, f_snd, f_rcv, f_token)):
                got[i] = t
        got = dict(zip(names, got))
        if group == "ffn1_up":
            return {"wup1": got["w_ffn1_in"].reshape(2, N_FFN_BLK, fb, D)}
        if group == "ffn1_dn":
            return {"wdn1": got["w_ffn1_out"].reshape(N_FFN_BLK, fb, D)}
        if group == "mix":
            full = got["w_in"].reshape(-1, D)
            wz = jnp.concatenate([full[:QKV_W], full[UV_OFF:], full[QKV_W:UV_OFF],
                                  jnp.zeros((LANES - FOX_HEADS, D), BF)], axis=0)
            return {"wz": wz, "wout": got["w_out"].reshape(D, D)}
        if group == "ca":
            return {"wcq": got["w_cq"].reshape(D, D), "wco": got["w_co"].reshape(D, D), "wckv": got["w_ckv"]}
        return {"wup2": got["w_ffn2_in"].reshape(2, N_FFN_BLK, fb, D),
                "wdn2": got["w_ffn2_out"].reshape(N_FFN_BLK, fb, D)}

    flying = {}

    def emit(group, g):
        if group == "w_s":
            part = [g["w_s"].reshape(-1, LANES)]
            *copies, token = _copy_start("w_s_start", part, _place_own(part, True), True)
            flying[group] = copies
            return token
        if group == "ffn2":
            parts = {"w_ffn2_in": g["wup2"], "w_ffn2_out": g["wdn2"].reshape(N_DEV, -1, D)}
        elif group == "ffn1_dn":
            parts = {"w_ffn1_out": g["wdn1"].reshape(N_DEV, -1, D)}
        elif group == "ffn1_up":
            parts = {"w_ffn1_in": g["wup1"]}
        else:
            gz = g["wz"]
            g_in = jnp.concatenate([gz[:QKV_W], gz[Z_F:Z_F + FOX_HEADS], gz[QKV_W:Z_F]], axis=0)
            parts = {"w_in": g_in.reshape(N_DEV, -1, D).astype(BF),
                     "w_out": g["wout"].reshape(N_DEV, -1, D), "w_cq": g["wcq"].reshape(N_DEV, -1, D),
                     "w_co": g["wco"].reshape(N_DEV, -1, D), "w_ckv": g["wckv"]}
        names = list(parts)
        srcs = [parts[n] for n in names]
        *copies, token = _copy_start("exchange_start_" + group, srcs, [lax.empty(s.shape, s.dtype) for s in srcs],
                                     False)
        flying[group] = (names, copies)
        return token

    small = {n: (w[n][0] if n == "b_s" else w[n]) for n in tiny_names}
    small["w_s"] = w["w_s"][0]

    sq, dx0, gs = _local_step(x[0], mem[0], loss_target[0], small, weights, emit)

    sm_parts = [_pack_tiny(gs, sq)]
    sm_snd, sm_rcv, sm_src, sm_land, sm_token = _copy_start("tiny_start", sm_parts, _place_own(sm_parts, True), True)

    grad, delta, new_m, new_v = {}, {}, {}, {}

    def update(group, after):
        names, (snd, rcv, srcs, lands) = flying[group]
        owns, slots = _copy_wait("exchange_wait_" + group, srcs, lands, snd, rcv, after, False, with_srcs=True)
        for n, sl, own in zip(names, slots, owns):
            g, d, m2, v2 = _adamw_big("adamw_" + n, sl, local(n, w[n]), local(n, mo[n]), local(n, vo[n]), own=own)
            grad[n], delta[n], new_m[n], new_v[n] = (
                (t.T if n in TRANSPOSED else t).reshape(w[n].shape) for t in (g, d, m2, v2))
        return d

    last = update("ffn2", sm_token)
    last = update("mid", last)
    last = update("ffn1_dn", last)
    last = update("ffn1_up", last)
    ws_snd, ws_rcv, ws_src, ws_land = flying["w_s"]
    ws_all, = _copy_wait("w_s_wait", ws_src, ws_land, ws_snd, ws_rcv, last, True)
    tiny_all, = _copy_wait("tiny_wait", sm_src, sm_land, sm_snd, sm_rcv, ws_all, True)
    ws_shape = w["w_s"].shape
    for store, t in zip((grad, delta, new_m, new_v), _adamw_big(
            "adamw_w_s", ws_all, *[a["w_s"].reshape(-1, LANES) for a in (w, mo, vo)])):
        store["w_s"] = t.reshape(ws_shape)
    stores, loss_row = _adamw_tiny(tiny_all, *[{n: (a[n][0] if n == "b_s" else a[n]) for n in tiny_names}
                                               for a in (w, mo, vo)])
    for store, t in zip((grad, delta, new_m, new_v), stores):
        store.update({n: v.reshape(w[n].shape) for n, v in t.items()})
    loss = loss_row[0, 0] * (0.5 / D)

    return (loss, dx0[None], *[grad[n] for n in WEIGHTS], *[delta[n] for n in WEIGHTS],
            *[new_m[n] for n in WEIGHTS], *[new_v[n] for n in WEIGHTS])
```

```python
import functools

import jax
import jax.numpy as jnp
from jax import lax
from jax.experimental import pallas as pl
from jax.experimental.pallas import tpu as pltpu

F32 = jnp.float32
BF = jnp.bfloat16
S = jax.ShapeDtypeStruct

N_DEV = 8
D_MODEL = 1024
FOX_HEADS, FOX_HD = 8, 64
FOX_W = 512
GMLP_G, GMLP_GD = 8, 64
GMLP_W = 512
CHUNK = 128
CA_HEADS, CA_HD = 4, 256
N_FFN_BLK = 4
ZW = 2688
Z_Q, Z_K, Z_V, Z_U, Z_G, Z_F = 0, 512, 1024, 1536, 2048, 2560
EPS = 1e-6
NEG = -1e30
LANES = 128

ADAM_LR, ADAM_B1, ADAM_B2, ADAM_EPS, ADAM_WD, ADAM_STEP = 0.001, 0.9, 0.999, 1e-08, 0.01, 10

VMEM_LIMIT = 52 * 2 ** 20


def _cp(n_axes):
    return pltpu.CompilerParams(dimension_semantics=("arbitrary",) * n_axes, vmem_limit_bytes=VMEM_LIMIT)


def _nn(a, b):
    return jnp.dot(a, b, preferred_element_type=F32)


def _nt(a, b):
    return lax.dot_general(a, b, (((1,), (1,)), ((), ())), preferred_element_type=F32)


def _tn(a, b):
    return lax.dot_general(a, b, (((0,), (0,)), ((), ())), preferred_element_type=F32)


def _hi(a, b):
    return jnp.dot(a, b, precision=lax.Precision.HIGHEST, preferred_element_type=F32)


def _rstd(x):
    return lax.rsqrt(jnp.mean(x * x, axis=-1, keepdims=True) + EPS)


def _norm_bwd(dy, x, g, r=None):
    r = _rstd(x) if r is None else r
    xh = x * r
    dxh = dy * g
    dx = r * (dxh - xh * jnp.mean(dxh * xh, axis=-1, keepdims=True))
    return dx, dy * xh


def _acc_rows(ref, first, val):
    srow = jnp.sum(val, axis=0, keepdims=True)

    @pl.when(first)
    def _():
        ref[...] = srow

    @pl.when(jnp.logical_not(first))
    def _():
        ref[...] += srow


def _gelu(x):
    c = 0.7978845608028654
    return 0.5 * x * (1.0 + jnp.tanh(c * (x + 0.044715 * x * x * x)))


def _gelu_grad(x):
    c = 0.7978845608028654
    t = jnp.tanh(c * (x + 0.044715 * x * x * x))
    return 0.5 * (1.0 + t) + 0.5 * x * (1.0 - t * t) * c * (1.0 + 3 * 0.044715 * x * x)


def _tile(n, pref):
    return pref if n % pref == 0 else n


def _ffn_up(name, x, g, wup):
    T, D = x.shape
    FB = wup.shape[-2]
    tm = _tile(T, 1024)

    def body(x_ref, g_ref, w_ref, a_ref, h_ref):
        @pl.when(pl.program_id(1) == 0)
        def _():
            xf = x_ref[...]
            h_ref[...] = (xf * _rstd(xf) * g_ref[...]).astype(BF)

        hb = h_ref[...]
        gg = _nt(hb, w_ref[0])
        uu = _nt(hb, w_ref[1])
        a_ref[...] = (gg * jax.nn.sigmoid(gg) * uu).astype(BF)

    return pl.pallas_call(
        body, name=name, grid=(T // tm, N_FFN_BLK),
        in_specs=[pl.BlockSpec((tm, D), lambda i, j: (i, 0)),
                  pl.BlockSpec((1, D), lambda i, j: (0, 0)),
                  pl.BlockSpec((2, None, FB, D), lambda i, j: (0, j, 0, 0))],
        out_specs=[pl.BlockSpec((None, tm, FB), lambda i, j: (j, i, 0)),
                   pl.BlockSpec((tm, D), lambda i, j: (i, 0))],
        out_shape=[S((N_FFN_BLK, T, FB), BF), S((T, D), BF)],
        compiler_params=_cp(2))(x, g, wup)


def _ffn_down(name, a, wdn, x):
    _, T, FB = a.shape
    D = x.shape[1]
    tm = _tile(T, 512)

    def body(a_ref, w_ref, x_ref, o_ref):
        p = _nn(a_ref[0], w_ref[0])
        for j in range(1, N_FFN_BLK):
            p = p + _nn(a_ref[j], w_ref[j])
        o_ref[...] = x_ref[...] + 0.5 * p

    return pl.pallas_call(
        body, name=name, grid=(T // tm,),
        in_specs=[pl.BlockSpec((N_FFN_BLK, tm, FB), lambda i: (0, i, 0)),
                  pl.BlockSpec((N_FFN_BLK, FB, D), lambda i: (0, 0, 0)),
                  pl.BlockSpec((tm, D), lambda i: (i, 0))],
        out_specs=pl.BlockSpec((tm, D), lambda i: (i, 0)),
        out_shape=S((T, D), F32),
        compiler_params=_cp(1))(a, wdn, x)


def _ffn_down_loss(name, a, wdn, x, target):
    _, T, FB = a.shape
    D = x.shape[1]
    tm = _tile(T, 512)

    def body(a_ref, w_ref, x_ref, t_ref, d_ref, db_ref, loss_ref):
        i = pl.program_id(0)
        p = _nn(a_ref[0], w_ref[0])
        for j in range(1, N_FFN_BLK):
            p = p + _nn(a_ref[j], w_ref[j])
        diff = (x_ref[...] + 0.5 * p) - t_ref[...]
        dy = diff * (1.0 / D)
        d_ref[...] = dy
        db_ref[...] = dy.astype(BF)
        sq = jnp.zeros((8, LANES), F32) + jnp.sum(diff * diff)

        @pl.when(i == 0)
        def _():
            loss_ref[...] = sq

        @pl.when(i > 0)
        def _():
            loss_ref[...] += sq

    row = pl.BlockSpec((tm, D), lambda i: (i, 0))
    return pl.pallas_call(
        body, name=name, grid=(T // tm,),
        in_specs=[pl.BlockSpec((N_FFN_BLK, tm, FB), lambda i: (0, i, 0)),
                  pl.BlockSpec((N_FFN_BLK, FB, D), lambda i: (0, 0, 0)), row, row],
        out_specs=[row, row, pl.BlockSpec((8, LANES), lambda i: (0, 0))],
        out_shape=[S((T, D), F32), S((T, D), BF), S((8, LANES), F32)],
        compiler_params=_cp(1))(a, wdn, x, target)


def _ffn_bwd_act(name, dyb, h, wup, wdn):
    T, D = h.shape
    FB = wup.shape[-2]
    tm = _tile(T, 1024)

    def body(d_ref, h_ref, wu_ref, wd_ref, o_ref):
        da = 0.5 * _nt(d_ref[...], wd_ref[...])
        hb = h_ref[...]
        gg = _nt(hb, wu_ref[0])
        uu = _nt(hb, wu_ref[1])
        sg = jax.nn.sigmoid(gg)
        o_ref[0] = (da * uu * (sg * (1.0 + gg * (1.0 - sg)))).astype(BF)
        o_ref[1] = (da * (gg * sg)).astype(BF)

    return pl.pallas_call(
        body, name=name, grid=(T // tm, N_FFN_BLK),
        in_specs=[pl.BlockSpec((tm, D), lambda i, j: (i, 0)),
                  pl.BlockSpec((tm, D), lambda i, j: (i, 0)),
                  pl.BlockSpec((2, None, FB, D), lambda i, j: (0, j, 0, 0)),
                  pl.BlockSpec((None, FB, D), lambda i, j: (j, 0, 0))],
        out_specs=pl.BlockSpec((2, None, tm, FB), lambda i, j: (0, j, i, 0)),
        out_shape=S((2, N_FFN_BLK, T, FB), BF),
        compiler_params=_cp(2))(dyb, h, wup, wdn)


def _ffn_dx(name, dgu, wup, x, g, dy):
    T, D = x.shape
    FB = wup.shape[-2]
    tm = _tile(T, 512)

    def body(d_ref, w_ref, x_ref, g_ref, dy_ref, dx_ref, dg_ref):
        p = None
        for j in range(N_FFN_BLK):
            for half in range(2):
                t = _nn(d_ref[half, j], w_ref[half, j])
                p = t if p is None else p + t
        dx, dgr = _norm_bwd(p, x_ref[...], g_ref[...])
        dx_ref[...] = dx + dy_ref[...]
        _acc_rows(dg_ref, pl.program_id(0) == 0, dgr)

    return pl.pallas_call(
        body, name=name, grid=(T // tm,),
        in_specs=[pl.BlockSpec((2, N_FFN_BLK, tm, FB), lambda i: (0, 0, i, 0)),
                  pl.BlockSpec((2, N_FFN_BLK, FB, D), lambda i: (0, 0, 0, 0), pipeline_mode=pl.Buffered(1)),
                  pl.BlockSpec((tm, D), lambda i: (i, 0)),
                  pl.BlockSpec((1, D), lambda i: (0, 0)),
                  pl.BlockSpec((tm, D), lambda i: (i, 0))],
        out_specs=[pl.BlockSpec((tm, D), lambda i: (i, 0)),
                   pl.BlockSpec((1, D), lambda i: (0, 0))],
        out_shape=[S((T, D), F32), S((1, D), F32)],
        compiler_params=_cp(1))(dgu, wup, x, g, dy)


def _tn_matmul(name, a, a_spec, b, out_shape, out_spec, n_blocks, scale=1.0, after=None):
    extra = [] if after is None else [after]

    def body(a_ref, b_ref, *rest):
        o_ref = rest[-1]
        o_ref[...] = (_tn(a_ref[...], b_ref[...]) * scale).astype(o_ref.dtype)

    return pl.pallas_call(
        body, name=name, grid=(n_blocks,),
        in_specs=[a_spec, pl.BlockSpec(b.shape, lambda j: (0, 0), pipeline_mode=pl.Buffered(1))]
        + [pl.BlockSpec((8, LANES), lambda j: (0, 0)) for _ in extra],
        out_specs=out_spec, out_shape=out_shape, compiler_params=_cp(1))(a, b, *extra)


def _ffn_dwup(name, h, dgu, after=None):
    T, D = h.shape
    FB = dgu.shape[-1]
    return _tn_matmul(
        name + "_dwup", dgu.reshape(2 * N_FFN_BLK, T, FB), pl.BlockSpec((None, T, FB), lambda j: (j, 0, 0)), h,
        S((2 * N_FFN_BLK, FB, D), BF), pl.BlockSpec((None, FB, D), lambda j: (j, 0, 0)), 2 * N_FFN_BLK,
        after=after)


def _ffn_dwdn(name, a, dyb):
    _, T, FB = a.shape
    D = dyb.shape[1]
    return _tn_matmul(
        name + "_dwdn", a, pl.BlockSpec((None, T, FB), lambda j: (j, 0, 0)), dyb,
        S((N_FFN_BLK, FB, D), BF), pl.BlockSpec((None, FB, D), lambda j: (j, 0, 0)), N_FFN_BLK, scale=0.5)


def _mix_proj(x, g, wz):
    T, D = x.shape
    tm = _tile(T, 512)

    def body(x_ref, g_ref, w_ref, z_ref, h_ref):
        xf = x_ref[...]
        hb = (xf * _rstd(xf) * g_ref[...]).astype(BF)
        h_ref[...] = hb
        z_ref[...] = _nt(hb, w_ref[...])

    return pl.pallas_call(
        body, name="mix_proj", grid=(T // tm,),
        in_specs=[pl.BlockSpec((tm, D), lambda i: (i, 0)),
                  pl.BlockSpec((1, D), lambda i: (0, 0)),
                  pl.BlockSpec((ZW, D), lambda i: (0, 0))],
        out_specs=[pl.BlockSpec((tm, ZW), lambda i: (i, 0)),
                   pl.BlockSpec((tm, D), lambda i: (i, 0))],
        out_shape=[S((T, ZW), F32), S((T, D), BF)],
        compiler_params=_cp(1))(x, g, wz)


def _tri(n, lower):
    r = lax.broadcasted_iota(jnp.int32, (n, n), 0)
    c = lax.broadcasted_iota(jnp.int32, (n, n), 1)
    return (r >= c) if lower else (r <= c)


def _spatial_mix(vgn_b, ws_ref, bst, tm):
    tril = _tri(CHUNK, True)
    wms = [jnp.where(tril, ws_ref[g], 0.0).astype(BF) for g in range(GMLP_G)]
    rows = []
    for c in range(tm // CHUNK):
        cols = []
        for g in range(GMLP_G):
            vs = vgn_b[c * CHUNK:(c + 1) * CHUNK, g * GMLP_GD:(g + 1) * GMLP_GD]
            cols.append(_nn(wms[g], vs) + bst[:, g:g + 1])
        rows.append(jnp.concatenate(cols, axis=1))
    return jnp.concatenate(rows, axis=0), wms


HB = 128
AUG_W = FOX_HEADS * HB
COL_A, COL_B, COL_C = 64, 67, 70
RS_Q, RS_K, RS_V, RS_O = 0, 8, 16, 17


def _spread_matrix():
    r = jnp.arange(FOX_W)
    return (jnp.arange(AUG_W)[None, :] == ((r // FOX_HD) * HB + r % FOX_HD)[:, None]).astype(BF)


def _piece_matrix(col):
    r = jnp.arange(LANES)
    dst = jnp.where(r < 3 * FOX_HEADS, (r % FOX_HEADS) * HB + col + r // FOX_HEADS, -1)
    return (jnp.arange(AUG_W)[None, :] == dst[:, None]).astype(BF)


def _ones_row(cols):
    c = jnp.arange(AUG_W) % HB
    hit = functools.reduce(jnp.logical_or, [(c >= a) & (c < a + 3) for a in cols])
    return hit.astype(F32)[None, :]


def _pieces(x):
    lane = lax.broadcasted_iota(jnp.int32, x.shape, 1)
    x = jnp.where(lane < FOX_HEADS, x, 0.0)
    hi = x.astype(BF).astype(F32)
    r1 = x - hi
    mid = r1.astype(BF).astype(F32)
    lo = (r1 - mid).astype(BF).astype(F32)
    return (hi + pltpu.roll(mid, FOX_HEADS, 1) + pltpu.roll(lo, 2 * FOX_HEADS, 1)).astype(BF)


def _mix_prep(z, bf128, g_q, g_k, g_sgu, w_s, b_st, g_go):
    T = z.shape[0]
    tm = _tile(T, 512)
    spread, pc_q, pc_k = _spread_matrix(), _piece_matrix(COL_A), _piece_matrix(COL_B)
    one_q, one_k, one_v = _ones_row([COL_B]), _ones_row([COL_A, COL_C]), _ones_row([COL_A])

    def body(z_ref, bf_ref, gq_ref, gk_ref, gs_ref, ws_ref, bst_ref, go_ref, sp_ref, pq_ref, pk_ref, oq_ref, ok_ref,
             ov_ref, q_ref, k_ref, v_ref, y_ref, rs_ref, carry_ref, qn_sc, kn_sc):
        i = pl.program_id(0)

        @pl.when(i == 0)
        def _():
            carry_ref[...] = jnp.zeros_like(carry_ref)

        rs_ref[...] = jnp.zeros_like(rs_ref)
        for h in range(FOX_HEADS):
            hs = slice(h * FOX_HD, (h + 1) * FOX_HD)
            qh = z_ref[:, Z_Q + h * FOX_HD:Z_Q + (h + 1) * FOX_HD]
            kh = z_ref[:, Z_K + h * FOX_HD:Z_K + (h + 1) * FOX_HD]
            rq, rk = _rstd(qh), _rstd(kh)
            rs_ref[:, RS_Q + h:RS_Q + h + 1] = rq
            rs_ref[:, RS_K + h:RS_K + h + 1] = rk
            qn_sc[:, hs] = (qh * rq * gq_ref[...] * 0.125).astype(BF)
            kn_sc[:, hs] = (kh * rk * gk_ref[...]).astype(BF)

        fl = z_ref[:, Z_F:Z_F + LANES] + bf_ref[...]
        logf = jnp.minimum(fl, 0.0) - jnp.log1p(jnp.exp(-jnp.abs(fl)))
        csum = _hi(_tri(tm, True).astype(F32), logf) + carry_ref[...]
        carry_ref[...] = csum[tm - 1:tm, :]
        sp = sp_ref[...]
        q_ref[...] = (_nn(qn_sc[...], sp) + _nn(_pieces(csum), pq_ref[...]) + oq_ref[...]).astype(BF)
        k_ref[...] = (_nn(kn_sc[...], sp) + _nn(_pieces(-csum), pk_ref[...]) + ok_ref[...]).astype(BF)
        v_ref[...] = (_nn(z_ref[:, Z_V:Z_V + FOX_W].astype(BF), sp) + ov_ref[...]).astype(BF)

        u = _gelu(z_ref[:, Z_U:Z_U + GMLP_W])
        vg = _gelu(z_ref[:, Z_G:Z_G + GMLP_W])
        rv = _rstd(vg)
        vgn = (vg * rv * gs_ref[...]).astype(BF)
        mixed, _ = _spatial_mix(vgn, ws_ref, bst_ref[...], tm)
        sgu = u * mixed
        ro = _rstd(sgu)
        y_ref[...] = (sgu * ro * go_ref[...]).astype(BF)
        rs_ref[:, RS_V:RS_V + 1] = rv
        rs_ref[:, RS_O:RS_O + 1] = ro

    row = lambda i: (i, 0)
    fix2 = lambda i: (0, 0)
    return pl.pallas_call(
        body, name="mix_prep", grid=(T // tm,),
        in_specs=[pl.BlockSpec((tm, ZW), row),
                  pl.BlockSpec((1, LANES), fix2), pl.BlockSpec((1, FOX_HD), fix2), pl.BlockSpec((1, FOX_HD), fix2),
                  pl.BlockSpec((1, GMLP_W), fix2), pl.BlockSpec((GMLP_G, CHUNK, CHUNK), lambda i: (0, 0, 0)),
                  pl.BlockSpec((CHUNK, GMLP_G), fix2), pl.BlockSpec((1, GMLP_W), fix2),
                  pl.BlockSpec((FOX_W, AUG_W), fix2), pl.BlockSpec((LANES, AUG_W), fix2),
                  pl.BlockSpec((LANES, AUG_W), fix2), pl.BlockSpec((1, AUG_W), fix2), pl.BlockSpec((1, AUG_W), fix2),
                  pl.BlockSpec((1, AUG_W), fix2)],
        out_specs=[pl.BlockSpec((tm, AUG_W), row), pl.BlockSpec((tm, AUG_W), row), pl.BlockSpec((tm, AUG_W), row),
                   pl.BlockSpec((tm, GMLP_W), row), pl.BlockSpec((tm, LANES), row)],
        out_shape=[S((T, AUG_W), BF), S((T, AUG_W), BF), S((T, AUG_W), BF), S((T, GMLP_W), BF), S((T, LANES), F32)],
        scratch_shapes=[pltpu.VMEM((1, LANES), F32), pltpu.VMEM((tm, FOX_W), BF), pltpu.VMEM((tm, FOX_W), BF)],
        compiler_params=_cp(1))(z, bf128, g_q, g_k, g_sgu, w_s, b_st, g_go, spread, pc_q, pc_k, one_q, one_k, one_v)


def _fox_fwd(q, k, v):
    T = q.shape[0]
    tq = _tile(T, 1024)
    nq = T // tq

    def body(q_ref, k_ref, v_ref, o_ref, lse_ref, m_sc, acc_sc):
        i, j = pl.program_id(0), pl.program_id(1)

        @pl.when(j == 0)
        def _():
            m_sc[...] = jnp.full(m_sc.shape, NEG, F32)
            acc_sc[...] = jnp.zeros_like(acc_sc)

        def step(rows, keys, masked):
            nr, nk = rows.stop - rows.start, keys.stop - keys.start
            mask = _tri(nr, True) if masked else None
            for h in range(FOX_HEADS):
                hb = slice(h * HB, (h + 1) * HB)
                s = _nt(q_ref[rows, hb], k_ref[keys, hb])
                if masked:
                    s = jnp.where(mask, s, NEG)
                m_prev = m_sc[h, rows, :]
                m_new = jnp.maximum(m_prev, jnp.broadcast_to(jnp.max(s, axis=1, keepdims=True), (nr, HB)))
                p = jnp.exp(s - jnp.tile(m_new, (1, nk // HB))).astype(BF)
                acc_sc[rows, hb] = jnp.exp(m_prev - m_new) * acc_sc[rows, hb] + _nn(p, v_ref[keys, hb])
                m_sc[h, rows, :] = m_new

        @pl.when(j < i)
        def _():
            step(slice(0, tq), slice(0, tq), False)

        @pl.when(j == i)
        def _():
            lo, hi = slice(0, tq // 2), slice(tq // 2, tq)
            step(lo, lo, True)
            step(hi, lo, False)
            step(hi, hi, True)
            lse_ref[...] = jnp.zeros_like(lse_ref)
            for h in range(FOX_HEADS):
                l = acc_sc[:, h * HB + COL_A:h * HB + COL_A + 1]
                o_ref[:, h * FOX_HD:(h + 1) * FOX_HD] = acc_sc[:, h * HB:h * HB + FOX_HD] / l
                lse_ref[:, h:h + 1] = m_sc[h][:, 0:1] + jnp.log(l)

    qi = lambda i, j: (i, 0)
    kj = lambda i, j: (jnp.minimum(i, j), 0)
    return pl.pallas_call(
        body, name="fox_fwd", grid=(nq, nq),
        in_specs=[pl.BlockSpec((tq, AUG_W), qi), pl.BlockSpec((tq, AUG_W), kj), pl.BlockSpec((tq, AUG_W), kj)],
        out_specs=[pl.BlockSpec((tq, FOX_W), qi), pl.BlockSpec((tq, LANES), qi)],
        out_shape=[S((T, FOX_W), F32), S((T, LANES), F32)],
        scratch_shapes=[pltpu.VMEM((FOX_HEADS, tq, HB), F32), pltpu.VMEM((tq, AUG_W), F32)],
        compiler_params=_cp(2))(q, k, v)


def _fox_bwd(q, k, v, dob):
    T = q.shape[0]
    tq = _tile(T, 512)
    nq = T // tq
    n_sweeps = 1
    half = AUG_W // n_sweeps
    hpg = FOX_HEADS // n_sweeps

    pairs = [(j, i) for j in range(nq) for i in range(j, nq)]
    jt = jnp.asarray([p[0] for p in pairs], jnp.int32)
    it = jnp.asarray([p[1] for p in pairs], jnp.int32)

    def body(jt_ref, it_ref, q_ref, k_ref, v_ref, do_ref, dq_ref, dk_ref, dv_ref, dq_sc):
        t = pl.program_id(1)
        j, i = jt_ref[t], it_ref[t]

        @pl.when(t == 0)
        def _():
            dq_sc[...] = jnp.zeros_like(dq_sc)

        @pl.when(i == j)
        def _():
            dk_ref[...] = jnp.zeros_like(dk_ref)
            dv_ref[...] = jnp.zeros_like(dv_ref)

        def step(masked):
            rows = pl.ds(pl.multiple_of(i * tq, tq), tq)
            mask = _tri(tq, True) if masked else None
            for h in range(hpg):
                hb = slice(h * HB, (h + 1) * HB)
                qh, kh, vh, doh = q_ref[:, hb], k_ref[:, hb], v_ref[:, hb], do_ref[:, hb]
                s = _nt(qh, kh)
                if masked:
                    s = jnp.where(mask, s, NEG)
                p = jnp.exp(s)
                dsb = (p * _nt(doh, vh)).astype(BF)
                dv_ref[:, hb] += _tn(p.astype(BF), doh)
                dk_ref[:, hb] += _tn(dsb, qh)
                dq_sc[rows, hb] += _nn(dsb, kh)

        @pl.when(i > j)
        def _():
            step(False)

        @pl.when(i == j)
        def _():
            step(True)
            dq_ref[...] = dq_sc[pl.ds(pl.multiple_of(j * tq, tq), tq), :]

    qi = pl.BlockSpec((tq, half), lambda g, t, jt_ref, it_ref: (it_ref[t], g))
    kj = pl.BlockSpec((tq, half), lambda g, t, jt_ref, it_ref: (jt_ref[t], g))
    return pl.pallas_call(
        body, name="fox_bwd",
        grid_spec=pltpu.PrefetchScalarGridSpec(
            num_scalar_prefetch=2, grid=(n_sweeps, len(pairs)), in_specs=[qi, kj, kj, qi], out_specs=[kj, kj, kj],
            scratch_shapes=[pltpu.VMEM((T, half), F32)]),
        out_shape=[S((T, AUG_W), F32), S((T, AUG_W), F32), S((T, AUG_W), F32)],
        compiler_params=_cp(2))(jt, it, q, k, v, dob)


def _mix_out(attn, yg, g_fo, wout, x):
    T, D = x.shape
    tm = _tile(T, 1024)

    def body(a_ref, y_ref, g_ref, w_ref, x_ref, o_ref):
        at = a_ref[...]
        yf = (at * _rstd(at) * g_ref[...]).astype(BF)
        o_ref[...] = x_ref[...] + _nn(yf, w_ref[:FOX_W, :]) + _nn(y_ref[...], w_ref[FOX_W:, :])

    row = lambda i: (i, 0)
    return pl.pallas_call(
        body, name="mix_out", grid=(T // tm,),
        in_specs=[pl.BlockSpec((tm, FOX_W), row), pl.BlockSpec((tm, GMLP_W), row),
                  pl.BlockSpec((1, FOX_W), lambda i: (0, 0)), pl.BlockSpec((D, D), lambda i: (0, 0)),
                  pl.BlockSpec((tm, D), row)],
        out_specs=pl.BlockSpec((tm, D), row),
        out_shape=S((T, D), F32),
        compiler_params=_cp(1))(attn, yg, g_fo, wout, x)


def _mix_out_bwd(dx, attn, yg, g_fo, wout, qf, lse):
    T, D = dx.shape
    tm = _tile(T, 512)
    n = T // tm
    spread, pc_l, pc_d = _spread_matrix(), _piece_matrix(COL_C), _piece_matrix(COL_A)

    def body(dx_ref, a_ref, y_ref, g_ref, w_ref, qf_ref, lse_ref, sp_ref, pl_ref, pd_ref,
             qb_ref, dob_ref, dyg_ref, dw_ref, dg_ref, acc_ref, dsum_ref):
        i = pl.program_id(0)
        dxb = dx_ref[...].astype(BF)
        at = a_ref[...]
        yf = (at * _rstd(at) * g_ref[...]).astype(BF)
        dy = _nt(dxb, w_ref[...])
        p_top = _tn(yf, dxb)
        p_bot = _tn(y_ref[...], dxb)

        @pl.when(i == 0)
        def _():
            acc_ref[:FOX_W, :] = p_top
            acc_ref[FOX_W:, :] = p_bot

        @pl.when(i > 0)
        def _():
            acc_ref[:FOX_W, :] += p_top
            acc_ref[FOX_W:, :] += p_bot

        @pl.when(i == n - 1)
        def _():
            dw_ref[...] = acc_ref[...].astype(BF)

        dat, dgr = _norm_bwd(dy[:, :FOX_W], at, g_ref[...])
        _acc_rows(dg_ref, i == 0, dgr)
        dyg_ref[...] = dy[:, FOX_W:]
        prod = dat * at
        dsum_ref[...] = jnp.zeros_like(dsum_ref)
        for h in range(FOX_HEADS):
            dsum_ref[:, h:h + 1] = jnp.sum(prod[:, h * FOX_HD:(h + 1) * FOX_HD], axis=1, keepdims=True)
        dob_ref[...] = (_nn(dat.astype(BF), sp_ref[...]) + _nn(_pieces(-dsum_ref[...]), pd_ref[...])).astype(BF)
        qb_ref[...] = (qf_ref[...].astype(F32) + _nn(_pieces(-lse_ref[...]), pl_ref[...])).astype(BF)

    row = lambda i: (i, 0)
    fix = lambda i: (0, 0)
    return pl.pallas_call(
        body, name="mix_out_bwd", grid=(n,),
        in_specs=[pl.BlockSpec((tm, D), row), pl.BlockSpec((tm, FOX_W), row), pl.BlockSpec((tm, GMLP_W), row),
                  pl.BlockSpec((1, FOX_W), fix), pl.BlockSpec((D, D), fix), pl.BlockSpec((tm, AUG_W), row),
                  pl.BlockSpec((tm, LANES), row), pl.BlockSpec((FOX_W, AUG_W), fix), pl.BlockSpec((LANES, AUG_W), fix),
                  pl.BlockSpec((LANES, AUG_W), fix)],
        out_specs=[pl.BlockSpec((tm, AUG_W), row), pl.BlockSpec((tm, AUG_W), row), pl.BlockSpec((tm, GMLP_W), row),
                   pl.BlockSpec((D, D), fix), pl.BlockSpec((1, FOX_W), fix)],
        out_shape=[S((T, AUG_W), BF), S((T, AUG_W), BF), S((T, GMLP_W), F32), S((D, D), BF), S((1, FOX_W), F32)],
        scratch_shapes=[pltpu.VMEM((D, D), F32), pltpu.VMEM((tm, LANES), F32)],
        compiler_params=_cp(1))(dx, attn, yg, g_fo, wout, qf, lse, spread, pc_l, pc_d)


def _mix_prep_bwd(z, dq, dk, dv, dyg, rs, bf128, g_q, g_k, g_sgu, w_s, b_st, g_go):
    T = z.shape[0]
    tm = _tile(T, 512)
    n = T // tm

    def body(z_ref, dq_ref, dk_ref, dv_ref, dyg_ref, rs_ref, bf_ref, gq_ref, gk_ref, gs_ref, ws_ref,
             bst_ref, go_ref, dz_ref, dgq_ref, dgk_ref, dgs_ref, dgo_ref, dws_ref, dbst_ref, dbf_ref, carry_ref):
        i = pl.program_id(0)
        first = i == 0
        rs = rs_ref[...]

        @pl.when(first)
        def _():
            carry_ref[...] = jnp.zeros_like(carry_ref)

        lane = lax.broadcasted_iota(jnp.int32, (tm, LANES), 1)
        dc = jnp.zeros((tm, LANES), F32)
        gq_rows, gk_rows = [], []
        for h in range(FOX_HEADS):
            hp = slice(h * HB, h * HB + FOX_HD)
            dqh, gqr = _norm_bwd(dq_ref[:, hp] * 0.125, z_ref[:, Z_Q + h * FOX_HD:Z_Q + (h + 1) * FOX_HD], gq_ref[...],
                                 rs[:, RS_Q + h:RS_Q + h + 1])
            dkh, gkr = _norm_bwd(dk_ref[:, hp], z_ref[:, Z_K + h * FOX_HD:Z_K + (h + 1) * FOX_HD], gk_ref[...],
                                 rs[:, RS_K + h:RS_K + h + 1])
            dz_ref[:, Z_Q + h * FOX_HD:Z_Q + (h + 1) * FOX_HD] = dqh.astype(BF)
            dz_ref[:, Z_K + h * FOX_HD:Z_K + (h + 1) * FOX_HD] = dkh.astype(BF)
            dz_ref[:, Z_V + h * FOX_HD:Z_V + (h + 1) * FOX_HD] = dv_ref[:, hp].astype(BF)
            dch = dq_ref[:, h * HB + COL_A:h * HB + COL_A + 1] - dk_ref[:, h * HB + COL_B:h * HB + COL_B + 1]
            dc = jnp.where(lane == h, dch, dc)
            gq_rows.append(gqr)
            gk_rows.append(gkr)
        _acc_rows(dgq_ref, first, functools.reduce(lambda a, b: a + b, gq_rows))
        _acc_rows(dgk_ref, first, functools.reduce(lambda a, b: a + b, gk_rows))

        dlogf = _hi(_tri(tm, False).astype(F32), dc) + carry_ref[...]
        carry_ref[...] = dlogf[0:1, :]
        fl = z_ref[:, Z_F:Z_F + LANES] + bf_ref[...]
        lane = lax.broadcasted_iota(jnp.int32, (tm, LANES), 1)
        df = jnp.where(lane < FOX_HEADS, dlogf * jax.nn.sigmoid(-fl), 0.0)
        dz_ref[:, Z_F:Z_F + LANES] = df.astype(BF)
        _acc_rows(dbf_ref, first, df)

        u_pre = z_ref[:, Z_U:Z_U + GMLP_W]
        vg_pre = z_ref[:, Z_G:Z_G + GMLP_W]
        u = _gelu(u_pre)
        vg = _gelu(vg_pre)
        rv = rs[:, RS_V:RS_V + 1]
        vgn = (vg * rv * gs_ref[...]).astype(BF)
        bst = bst_ref[...]
        mixed, wms = _spatial_mix(vgn, ws_ref, bst, tm)
        sgu = u * mixed
        dsgu, gor = _norm_bwd(dyg_ref[...], sgu, go_ref[...], rs[:, RS_O:RS_O + 1])
        _acc_rows(dgo_ref, first, gor)
        du = dsgu * mixed
        dmixed = dsgu * u
        dmb = dmixed.astype(BF)
        tril = _tri(CHUNK, True)
        dvgn_rows = []
        dws = [None] * GMLP_G
        dbs = [None] * GMLP_G
        for c in range(tm // CHUNK):
            cs = slice(c * CHUNK, (c + 1) * CHUNK)
            cols = []
            for g in range(GMLP_G):
                gs = slice(g * GMLP_GD, (g + 1) * GMLP_GD)
                dmc = dmb[cs, gs]
                pw = _nt(dmc, vgn[cs, gs])
                pb = jnp.sum(dmixed[cs, gs], axis=1, keepdims=True)
                dws[g] = pw if dws[g] is None else dws[g] + pw
                dbs[g] = pb if dbs[g] is None else dbs[g] + pb
                cols.append(_tn(wms[g], dmc))
            dvgn_rows.append(jnp.concatenate(cols, axis=1))
        dvgn = jnp.concatenate(dvgn_rows, axis=0)
        dbs_t = jnp.concatenate(dbs, axis=1)
        for g in range(GMLP_G):
            dwg = jnp.where(tril, dws[g], 0.0)

            @pl.when(first)
            def _():
                dws_ref[g] = dwg

            @pl.when(jnp.logical_not(first))
            def _():
                dws_ref[g] += dwg

        @pl.when(first)
        def _():
            dbst_ref[...] = dbs_t

        @pl.when(jnp.logical_not(first))
        def _():
            dbst_ref[...] += dbs_t

        dvg, gsr = _norm_bwd(dvgn, vg, gs_ref[...], rv)
        _acc_rows(dgs_ref, first, gsr)
        dz_ref[:, Z_U:Z_U + GMLP_W] = (du * _gelu_grad(u_pre)).astype(BF)
        dz_ref[:, Z_G:Z_G + GMLP_W] = (dvg * _gelu_grad(vg_pre)).astype(BF)

    rev = lambda i: (n - 1 - i, 0)
    fix = lambda i: (0, 0)
    fix3 = lambda i: (0, 0, 0)
    return pl.pallas_call(
        body, name="mix_prep_bwd", grid=(n,),
        in_specs=[pl.BlockSpec((tm, ZW), rev), pl.BlockSpec((tm, AUG_W), rev), pl.BlockSpec((tm, AUG_W), rev),
                  pl.BlockSpec((tm, AUG_W), rev), pl.BlockSpec((tm, GMLP_W), rev), pl.BlockSpec((tm, LANES), rev),
                  pl.BlockSpec((1, LANES), fix), pl.BlockSpec((1, FOX_HD), fix), pl.BlockSpec((1, FOX_HD), fix),
                  pl.BlockSpec((1, GMLP_W), fix), pl.BlockSpec((GMLP_G, CHUNK, CHUNK), fix3),
                  pl.BlockSpec((CHUNK, GMLP_G), fix), pl.BlockSpec((1, GMLP_W), fix)],
        out_specs=[pl.BlockSpec((tm, ZW), rev), pl.BlockSpec((1, FOX_HD), fix), pl.BlockSpec((1, FOX_HD), fix),
                   pl.BlockSpec((1, GMLP_W), fix), pl.BlockSpec((1, GMLP_W), fix),
                   pl.BlockSpec((GMLP_G, CHUNK, CHUNK), fix3), pl.BlockSpec((CHUNK, GMLP_G), fix),
                   pl.BlockSpec((1, LANES), fix)],
        out_shape=[S((T, ZW), BF), S((1, FOX_HD), F32), S((1, FOX_HD), F32), S((1, GMLP_W), F32), S((1, GMLP_W), F32),
                   S((GMLP_G, CHUNK, CHUNK), F32), S((CHUNK, GMLP_G), F32), S((1, LANES), F32)],
        scratch_shapes=[pltpu.VMEM((1, LANES), F32)],
        compiler_params=_cp(1))(z, dq, dk, dv, dyg, rs, bf128, g_q, g_k, g_sgu, w_s, b_st, g_go)


def _mix_proj_bwd(dz, wz, x, g, dy):
    T, D = x.shape
    tm = _tile(T, 512)

    def body(dz_ref, w_ref, x_ref, g_ref, dy_ref, dx_ref, dxb_ref, dg_ref):
        dh = _nn(dz_ref[...], w_ref[...])
        dx, dgr = _norm_bwd(dh, x_ref[...], g_ref[...])
        dx = dx + dy_ref[...]
        dx_ref[...] = dx
        dxb_ref[...] = dx.astype(BF)
        _acc_rows(dg_ref, pl.program_id(0) == 0, dgr)

    row = lambda i: (i, 0)
    fix = lambda i: (0, 0)
    return pl.pallas_call(
        body, name="mix_proj_bwd", grid=(T // tm,),
        in_specs=[pl.BlockSpec((tm, ZW), row), pl.BlockSpec((ZW, D), fix), pl.BlockSpec((tm, D), row),
                  pl.BlockSpec((1, D), fix), pl.BlockSpec((tm, D), row)],
        out_specs=[pl.BlockSpec((tm, D), row), pl.BlockSpec((tm, D), row), pl.BlockSpec((1, D), fix)],
        out_shape=[S((T, D), F32), S((T, D), BF), S((1, D), F32)],
        compiler_params=_cp(1))(dz, wz, x, g, dy)


def _ca_kv(mem, g_mem, wckv, g_ck):
    M, D = mem.shape

    def body(m_ref, g_ref, w_ref, gk_ref, mn_ref, kr_ref, kn_ref, v_ref):
        mf = m_ref[...]
        mn = (mf * _rstd(mf) * g_ref[...]).astype(BF)
        mn_ref[...] = mn
        for h in range(CA_HEADS):
            kr = _nn(mn, w_ref[h])
            kr_ref[h] = kr
            kn_ref[h] = (kr * _rstd(kr) * gk_ref[...]).astype(BF)
            v_ref[h] = _nn(mn, w_ref[CA_HEADS + h]).astype(BF)

    hd = (CA_HEADS, M, CA_HD)
    return pl.pallas_call(
        body, name="ca_kv", out_shape=[S((M, D), BF), S(hd, F32), S(hd, BF), S(hd, BF)],
        compiler_params=pltpu.CompilerParams(vmem_limit_bytes=VMEM_LIMIT))(mem, g_mem, wckv, g_ck)


def _ca_tile_fwd(xt, gca, wcq, gcq, kn_ref, v_ref):
    hb = (xt * _rstd(xt) * gca).astype(BF)
    qc = _nn(hb, wcq)
    qr, qn, ps = [], [], []
    for h in range(CA_HEADS):
        qh = qc[:, h * CA_HD:(h + 1) * CA_HD]
        qnh = (qh * _rstd(qh) * gcq * 0.0625).astype(BF)
        s = _nt(qnh, kn_ref[h])
        e = jnp.exp(s - jnp.max(s, axis=1, keepdims=True))
        ps.append(e / jnp.sum(e, axis=1, keepdims=True))
        qr.append(qh)
        qn.append(qnh)
    return hb, qr, qn, ps


def _ca_fwd(x, g_ca, wcq, g_cq, kn, vv, wco):
    T, D = x.shape
    M = kn.shape[1]
    tm = _tile(T, 1024)

    def body(x_ref, gca_ref, wcq_ref, gcq_ref, kn_ref, v_ref, wco_ref, o_ref, ob_sc):
        xt = x_ref[...]
        _, _, _, ps = _ca_tile_fwd(xt, gca_ref[...], wcq_ref[...], gcq_ref[...], kn_ref, v_ref)
        for h in range(CA_HEADS):
            ob_sc[:, h * CA_HD:(h + 1) * CA_HD] = _nn(ps[h].astype(BF), v_ref[h]).astype(BF)
        o_ref[...] = xt + _nn(ob_sc[...], wco_ref[...])

    row = lambda i: (i, 0)
    fix = lambda i: (0, 0)
    fix3 = lambda i: (0, 0, 0)
    return pl.pallas_call(
        body, name="ca_fwd", grid=(T // tm,),
        in_specs=[pl.BlockSpec((tm, D), row), pl.BlockSpec((1, D), fix), pl.BlockSpec((D, D), fix),
                  pl.BlockSpec((1, CA_HD), fix), pl.BlockSpec((CA_HEADS, M, CA_HD), fix3),
                  pl.BlockSpec((CA_HEADS, M, CA_HD), fix3), pl.BlockSpec((D, D), fix)],
        out_specs=pl.BlockSpec((tm, D), row), out_shape=S((T, D), F32),
        scratch_shapes=[pltpu.VMEM((tm, D), BF)],
        compiler_params=_cp(1))(x, g_ca, wcq, g_cq, kn, vv, wco)


def _ca_bwd(x, dy, g_ca, wcq, g_cq, kn, vv, wco):
    T, D = x.shape
    M = kn.shape[1]
    tm = _tile(T, 512)
    n = T // tm

    def body(x_ref, dy_ref, gca_ref, wcq_ref, gcq_ref, kn_ref, v_ref, wco_ref,
             dx_ref, dwq_ref, dwo_ref, dkn_ref, dv_ref, dgcq_ref, dgca_ref, aq_sc, ao_sc, ob_sc, dq_sc):
        i = pl.program_id(0)
        first = i == 0
        xt = x_ref[...]
        dyt = dy_ref[...]
        dyb = dyt.astype(BF)
        hb, qr, qn, ps = _ca_tile_fwd(xt, gca_ref[...], wcq_ref[...], gcq_ref[...], kn_ref, v_ref)
        do = _nt(dyb, wco_ref[...])
        gcq_rows = None
        for h in range(CA_HEADS):
            hs = slice(h * CA_HD, (h + 1) * CA_HD)
            p = ps[h]
            pb = p.astype(BF)
            ob_sc[:, hs] = _nn(pb, v_ref[h]).astype(BF)
            doh = do[:, hs].astype(BF)
            dp = _nt(doh, v_ref[h])
            ds = (p * (dp - jnp.sum(dp * p, axis=1, keepdims=True))).astype(BF)
            dvh = _tn(pb, doh)
            dkh = _tn(ds, qn[h])

            @pl.when(first)
            def _():
                dv_ref[h] = dvh
                dkn_ref[h] = dkh

            @pl.when(jnp.logical_not(first))
            def _():
                dv_ref[h] += dvh
                dkn_ref[h] += dkh

            dqn = _nn(ds, kn_ref[h]) * 0.0625
            dqh, gr = _norm_bwd(dqn, qr[h], gcq_ref[...])
            gcq_rows = gr if gcq_rows is None else gcq_rows + gr
            dq_sc[:, hs] = dqh.astype(BF)
        _acc_rows(dgcq_ref, first, gcq_rows)
        dqb = dq_sc[...]
        p_o = _tn(ob_sc[...], dyb)
        p_q = _tn(hb, dqb)

        @pl.when(first)
        def _():
            ao_sc[...] = p_o
            aq_sc[...] = p_q

        @pl.when(jnp.logical_not(first))
        def _():
            ao_sc[...] += p_o
            aq_sc[...] += p_q

        @pl.when(i == n - 1)
        def _():
            dwo_ref[...] = ao_sc[...].astype(BF)
            dwq_ref[...] = aq_sc[...].astype(BF)

        dh = _nt(dqb, wcq_ref[...])
        dx, gar = _norm_bwd(dh, xt, gca_ref[...])
        dx_ref[...] = dx + dyt
        _acc_rows(dgca_ref, first, gar)

    row = lambda i: (i, 0)
    fix = lambda i: (0, 0)
    fix3 = lambda i: (0, 0, 0)
    hd = (CA_HEADS, M, CA_HD)
    return pl.pallas_call(
        body, name="ca_bwd", grid=(n,),
        in_specs=[pl.BlockSpec((tm, D), row), pl.BlockSpec((tm, D), row), pl.BlockSpec((1, D), fix),
                  pl.BlockSpec((D, D), fix), pl.BlockSpec((1, CA_HD), fix), pl.BlockSpec(hd, fix3),
                  pl.BlockSpec(hd, fix3), pl.BlockSpec((D, D), fix)],
        out_specs=[pl.BlockSpec((tm, D), row), pl.BlockSpec((D, D), fix), pl.BlockSpec((D, D), fix),
                   pl.BlockSpec(hd, fix3), pl.BlockSpec(hd, fix3), pl.BlockSpec((1, CA_HD), fix),
                   pl.BlockSpec((1, D), fix)],
        out_shape=[S((T, D), F32), S((D, D), BF), S((D, D), BF), S(hd, F32), S(hd, F32), S((1, CA_HD), F32),
                   S((1, D), F32)],
        scratch_shapes=[pltpu.VMEM((D, D), F32), pltpu.VMEM((D, D), F32), pltpu.VMEM((tm, D), BF),
                        pltpu.VMEM((tm, D), BF)],
        compiler_params=_cp(1))(x, dy, g_ca, wcq, g_cq, kn, vv, wco)


def _ca_kv_bwd(mem, g_mem, mn, kraw, dkn, dvv, wckv, g_ck):
    M, D = mem.shape

    def body(m_ref, g_ref, mn_ref, kr_ref, dkn_ref, dv_ref, w_ref, gk_ref, dw_ref, dgk_ref, dgm_ref):
        mn = mn_ref[...]
        dmn = jnp.zeros((M, D), F32)
        gk_rows = None
        for h in range(CA_HEADS):
            dkr, gr = _norm_bwd(dkn_ref[h], kr_ref[h], gk_ref[...])
            gk_rows = gr if gk_rows is None else gk_rows + gr
            dkb = dkr.astype(BF)
            dvb = dv_ref[h].astype(BF)
            dw_ref[h] = _tn(mn, dkb).astype(BF)
            dw_ref[CA_HEADS + h] = _tn(mn, dvb).astype(BF)
            dmn = dmn + _nt(dkb, w_ref[h]) + _nt(dvb, w_ref[CA_HEADS + h])
        dgk_ref[...] = jnp.sum(gk_rows, axis=0, keepdims=True)
        mf = m_ref[...]
        dgm_ref[...] = jnp.sum(dmn * (mf * _rstd(mf)), axis=0, keepdims=True)

    return pl.pallas_call(
        body, name="ca_kv_bwd",
        out_shape=[S((2 * CA_HEADS, D, CA_HD), BF), S((1, CA_HD), F32), S((1, D), F32)],
        compiler_params=pltpu.CompilerParams(vmem_limit_bytes=VMEM_LIMIT))(mem, g_mem, mn, kraw, dkn, dvv, wckv, g_ck)


def _after(g, token):
    return g if token is None else g + token[0:1, 0:1]


def _local_step(x, mem, target, small, weights, emit):
    T, D = x.shape
    p = small
    bf128 = jnp.pad(p["b_f"], ((0, 0), (0, LANES - FOX_HEADS)))
    b_st = p["b_s"].T

    wup1 = weights("ffn1_up", x)["wup1"]
    a1, h1 = _ffn_up("ffn1_up", x, p["g_ffn1"], wup1)
    wdn1 = weights("ffn1_dn", h1)["wdn1"]
    x1 = _ffn_down("ffn1_down", a1, wdn1, x)
    wm = weights("mix", x1)
    z, h2 = _mix_proj(x1, p["g_mix"], wm["wz"])
    qf, ka, va, yg, rs = _mix_prep(z, bf128, p["g_q"], p["g_k"], p["g_sgu"], p["w_s"], b_st, p["g_gmlp_o"])
    attn, lse = _fox_fwd(qf, ka, va)
    x2 = _mix_out(attn, yg, p["g_fox_o"], wm["wout"], x1)
    wc = weights("ca", x2)
    mn, kraw, ckn, cvv = _ca_kv(mem, p["g_mem"], wc["wckv"], p["g_ck"])
    x3 = _ca_fwd(x2, p["g_ca"], wc["wcq"], p["g_cq"], ckn, cvv, wc["wco"])
    w2 = weights("ffn2", x3)
    a2, h4 = _ffn_up("ffn2_up", x3, p["g_ffn2"], w2["wup2"])
    dy4, dy4b, sq = _ffn_down_loss("ffn2_down", a2, w2["wdn2"], x3, target)

    gs = {}
    dgu2 = _ffn_bwd_act("ffn2_bwd_act", dy4b, h4, w2["wup2"], w2["wdn2"])
    tok = emit("ffn2", {"wup2": _ffn_dwup("ffn2", h4, dgu2), "wdn2": _ffn_dwdn("ffn2", a2, dy4b)})
    dx3, gs["g_ffn2"] = _ffn_dx("ffn2_dx", dgu2, w2["wup2"], x3, _after(p["g_ffn2"], tok), dy4)

    dx2, dwcq, dwco, dckn, dcvv, gs["g_cq"], gs["g_ca"] = _ca_bwd(
        x2, dx3, p["g_ca"], wc["wcq"], p["g_cq"], ckn, cvv, wc["wco"])
    dwckv, gs["g_ck"], gs["g_mem"] = _ca_kv_bwd(mem, p["g_mem"], mn, kraw, dckn, dcvv, wc["wckv"], p["g_ck"])

    qb, dob, dyg, dwout, gs["g_fox_o"] = _mix_out_bwd(dx2, attn, yg, p["g_fox_o"], wm["wout"], qf, lse)
    dq, dk, dv = _fox_bwd(qb, ka, va, dob)
    dz, gs["g_q"], gs["g_k"], gs["g_sgu"], gs["g_gmlp_o"], gs["w_s"], dbst, dbf = _mix_prep_bwd(
        z, dq, dk, dv, dyg, rs, bf128, p["g_q"], p["g_k"], p["g_sgu"], p["w_s"], b_st, p["g_gmlp_o"])
    gs["b_s"] = dbst.T
    gs["b_f"] = dbf[:, :FOX_HEADS]
    tok_ws = emit("w_s", {"w_s": gs["w_s"]})
    zb = ZW // 3
    dwz = _tn_matmul("mix_dwz", dz, pl.BlockSpec((T, zb), lambda j: (0, j)), h2,
                     S((ZW, D), BF), pl.BlockSpec((zb, D), lambda j: (j, 0)), 3)
    tok = emit("mid", {"wcq": dwcq, "wco": dwco, "wckv": dwckv, "wout": dwout, "wz": dwz})
    dx1, dx1b, gs["g_mix"] = _mix_proj_bwd(dz, wm["wz"], x1, _after(_after(p["g_mix"], tok), tok_ws), dx2)

    dgu1 = _ffn_bwd_act("ffn1_bwd_act", dx1b, h1, wup1, wdn1)
    tok = emit("ffn1_dn", {"wdn1": _ffn_dwdn("ffn1", a1, dx1b)})
    tok = emit("ffn1_up", {"wup1": _ffn_dwup("ffn1", h1, dgu1, after=tok)})
    dx0, gs["g_ffn1"] = _ffn_dx("ffn1_dx", dgu1, wup1, x, _after(p["g_ffn1"], tok), dx1)
    return sq, dx0, gs


MESH = pl.DeviceIdType.MESH
HBM_SPEC = pl.BlockSpec(memory_space=pltpu.HBM)
N_PEER = N_DEV - 1


def _place():
    return lax.axis_index("x"), lax.axis_index("y"), lax.axis_index("c")


def _slot(px, py, pc):
    return 4 * px + 2 * py + pc


SEM_SPEC = pl.BlockSpec(memory_space=pltpu.SEMAPHORE)
ANY_SPEC = pl.BlockSpec(memory_space=pl.ANY)
DATAFLOW = pltpu.SideEffectType.DATAFLOW_SIDE_EFFECTING


def _hbm(a):
    return pltpu.with_memory_space_constraint(a, pltpu.HBM)


def _peer(x, y, c, r):
    return (1 - x if r & 4 else x, 1 - y if r & 2 else y, 1 - c if r & 1 else c)


def _place_own(srcs, whole):
    my = _slot(*_place())
    lands = []
    for s in srcs:
        blk = s[None] if whole else lax.dynamic_slice_in_dim(s, my, 1, 0)
        shape = (N_DEV,) + s.shape if whole else s.shape
        lands.append(lax.dynamic_update_slice_in_dim(lax.empty(shape, s.dtype), blk, my, 0))
    return lands


ALL_PEERS = tuple(range(1, N_DEV))
NEAR_PEERS = (1, 2, 4, 6)
SAME_CORE = (2, 4, 6)


def _copy_start(name, srcs, lands, whole, peers=None):
    n = len(srcs)
    peers = peers or [ALL_PEERS] * n

    def body(*refs):
        src, land = refs[:n], refs[n:2 * n]
        send, recv = refs[2 * n:3 * n], refs[3 * n:4 * n]
        token = refs[6 * n]
        x, y, c = _place()
        my = _slot(x, y, c)
        for a in range(n):
            for r in peers[a]:
                p = _peer(x, y, c, r)
                pltpu.make_async_remote_copy(
                    src_ref=src[a] if whole else src[a].at[_slot(*p)], dst_ref=land[a].at[my],
                    send_sem=send[a].at[r - 1], recv_sem=recv[a].at[r - 1], device_id=p, device_id_type=MESH).start()
        token[...] = jnp.zeros_like(token)

    out = pl.pallas_call(
        body, name=name,
        out_shape=([pltpu.SemaphoreType.DMA((N_PEER,))] * (2 * n)
                   + [pltpu.HBM(s.shape, s.dtype) for s in srcs] + [pltpu.HBM(s.shape, s.dtype) for s in lands]
                   + [S((8, LANES), F32)]),
        in_specs=[HBM_SPEC] * (2 * n),
        out_specs=[SEM_SPEC] * (2 * n) + [HBM_SPEC] * (2 * n) + [pl.BlockSpec(memory_space=pltpu.VMEM)],
        input_output_aliases={i: 2 * n + i for i in range(2 * n)},
        compiler_params=pltpu.CompilerParams(has_side_effects=DATAFLOW),
    )(*[_hbm(s) for s in srcs], *[_hbm(s) for s in lands])
    return out[:n], out[n:2 * n], out[2 * n:3 * n], out[3 * n:4 * n], out[4 * n]


def _copy_wait(name, srcs, lands, send, recv, after, whole, peers=None, with_srcs=False):
    n = len(srcs)
    peers = peers or [ALL_PEERS] * n

    def body(*refs):
        src, land = refs[:n], refs[n:2 * n]
        snd, rcv = refs[2 * n:3 * n], refs[3 * n:4 * n]
        x, y, c = _place()
        for a in range(n):
            for r in peers[a]:
                p = _peer(x, y, c, r)
                ps = _slot(*p)
                cp = pltpu.make_async_remote_copy(
                    src_ref=src[a] if whole else src[a].at[ps], dst_ref=land[a].at[ps],
                    send_sem=snd[a].at[r - 1], recv_sem=rcv[a].at[r - 1], device_id=p, device_id_type=MESH)
                cp.wait_send()
                cp.wait_recv()

    out = pl.pallas_call(
        body, name=name,
        out_shape=[pltpu.HBM(s.shape, s.dtype) for s in srcs] + [pltpu.HBM(s.shape, s.dtype) for s in lands],
        in_specs=[HBM_SPEC] * (2 * n) + [SEM_SPEC] * (2 * n) + [ANY_SPEC],
        out_specs=[HBM_SPEC] * (2 * n),
        input_output_aliases={i: i for i in range(2 * n)},
        compiler_params=pltpu.CompilerParams(has_side_effects=DATAFLOW),
    )(*srcs, *lands, *send, *recv, after)
    return (out[:n], out[n:]) if with_srcs else out[n:]


def _forward_start(name, lands):
    n = len(lands)

    def body(*refs):
        land = refs[:n]
        send, recv = refs[n:2 * n], refs[2 * n:3 * n]
        token = refs[4 * n]
        x, y, c = _place()
        for a in range(n):
            for r in SAME_CORE:
                blk = land[a].at[_slot(*_peer(x, y, c, r))]
                pltpu.make_async_remote_copy(
                    src_ref=blk, dst_ref=blk, send_sem=send[a].at[r - 1], recv_sem=recv[a].at[r - 1],
                    device_id=(x, y, 1 - c), device_id_type=MESH).start()
        token[...] = jnp.zeros_like(token)

    out = pl.pallas_call(
        body, name=name,
        out_shape=([pltpu.SemaphoreType.DMA((N_PEER,))] * (2 * n) + [pltpu.HBM(s.shape, s.dtype) for s in lands]
                   + [S((8, LANES), F32)]),
        in_specs=[HBM_SPEC] * n,
        out_specs=[SEM_SPEC] * (2 * n) + [HBM_SPEC] * n + [pl.BlockSpec(memory_space=pltpu.VMEM)],
        input_output_aliases={i: 2 * n + i for i in range(n)},
        compiler_params=pltpu.CompilerParams(has_side_effects=DATAFLOW),
    )(*[_hbm(s) for s in lands])
    return out[:n], out[n:2 * n], out[2 * n:3 * n], out[3 * n]


def _forward_wait(name, lands, send, recv, after):
    n = len(lands)

    def body(*refs):
        land = refs[:n]
        snd, rcv = refs[n:2 * n], refs[2 * n:3 * n]
        x, y, c = _place()
        for a in range(n):
            for r in SAME_CORE:
                cp = pltpu.make_async_remote_copy(
                    src_ref=land[a].at[_slot(*_peer(x, y, c, r))], dst_ref=land[a].at[_slot(*_peer(x, y, c, r | 1))],
                    send_sem=snd[a].at[r - 1], recv_sem=rcv[a].at[r - 1], device_id=(x, y, 1 - c),
                    device_id_type=MESH)
                cp.wait_send()
                cp.wait_recv()

    return pl.pallas_call(
        body, name=name,
        out_shape=[pltpu.HBM(s.shape, s.dtype) for s in lands],
        in_specs=[HBM_SPEC] * n + [SEM_SPEC] * (2 * n) + [ANY_SPEC],
        out_specs=[HBM_SPEC] * n,
        input_output_aliases={i: i for i in range(n)},
        compiler_params=pltpu.CompilerParams(has_side_effects=DATAFLOW),
    )(*lands, *send, *recv, after)


def _adamw(w, g, m, v):
    m2 = ADAM_B1 * m + (1.0 - ADAM_B1) * g
    v2 = ADAM_B2 * v + (1.0 - ADAM_B2) * (g * g)
    m_hat = m2 / (1.0 - ADAM_B1 ** ADAM_STEP)
    v_hat = v2 / (1.0 - ADAM_B2 ** ADAM_STEP)
    delta = -ADAM_LR * (m_hat / (jnp.sqrt(v_hat) + ADAM_EPS) + ADAM_WD * w)
    return delta, m2, v2


def _adamw_big(name, slots, w, m, v, own=None):
    R, C = w.shape
    tr = next((t for t in (256, 352) if R % t == 0), R)

    def finish(g, w_ref, m_ref, v_ref, g_ref, d_ref, m2_ref, v2_ref):
        d, m2, v2 = _adamw(w_ref[...], g, m_ref[...], v_ref[...])
        g_ref[...] = g
        d_ref[...] = d
        m2_ref[...] = m2
        v2_ref[...] = v2

    if own is None:
        def body(s_ref, *refs):
            g = s_ref[0].astype(F32)
            for k in range(1, N_DEV):
                g = g + s_ref[k].astype(F32)
            finish(g, *refs)

        row = pl.BlockSpec((tr, C), lambda i: (i, 0))
        return pl.pallas_call(
            body, name=name, grid=(R // tr,),
            in_specs=[pl.BlockSpec((N_DEV, tr, C), lambda i: (0, i, 0)), row, row, row],
            out_specs=[row] * 4, out_shape=[S((R, C), F32)] * 4,
            compiler_params=_cp(1))(slots, w, m, v)

    def body(my_ref, s_ref, own_ref, *refs):
        mine = own_ref[...]
        g = None
        for k in range(N_DEV):
            part = jnp.where(my_ref[0] == k, mine, s_ref[k]).astype(F32)
            g = part if g is None else g + part
        finish(g, *refs)

    row = pl.BlockSpec((tr, C), lambda i, my_ref: (i, 0))
    my = jnp.reshape(_slot(*_place()), (1,)).astype(jnp.int32)
    return pl.pallas_call(
        body, name=name,
        grid_spec=pltpu.PrefetchScalarGridSpec(
            num_scalar_prefetch=1, grid=(R // tr,),
            in_specs=[pl.BlockSpec((N_DEV, tr, C), lambda i, my_ref: (0, i, 0)),
                      pl.BlockSpec((None, tr, C), lambda i, my_ref: (my_ref[0], i, 0)), row, row, row],
            out_specs=[row] * 4),
        out_shape=[S((R, C), F32)] * 4, compiler_params=_cp(1))(my, slots, own, w, m, v)


TINY_ROWS = (("b_s", 8), ("g_ffn1", 8), ("g_mix", 8), ("g_ca", 8), ("g_mem", 8), ("g_ffn2", 8), ("g_sgu", 4),
             ("g_fox_o", 4), ("g_gmlp_o", 4), ("g_cq", 2), ("g_ck", 2), ("g_q", 1), ("g_k", 1), ("b_f", 1),
             ("loss", 1))
TINY_P = 72


def _tiny_pieces(width):
    return [(j, slice(j * LANES, min((j + 1) * LANES, width))) for j in range(-(-width // LANES))]


def _pack_tiny(grads, sq):
    names = [n for n, _ in TINY_ROWS if n != "loss"]

    def body(*refs):
        ins, sq_ref, o_ref = refs[:len(names)], refs[len(names)], refs[len(names) + 1]
        o_ref[...] = jnp.zeros_like(o_ref)
        at = 0
        for ref, (name, r) in zip(ins, TINY_ROWS):
            if name == "b_s":
                o_ref[at:at + r, :] = ref[...]
            else:
                for j, cols in _tiny_pieces(ref.shape[1]):
                    o_ref[at + j:at + j + 1, 0:cols.stop - cols.start] = ref[:, cols]
            at += r
        o_ref[at:at + 1, :] = sq_ref[0:1, :]

    return pl.pallas_call(body, name="tiny_pack", out_shape=S((TINY_P, LANES), F32))(
        *[grads[n] for n in names], sq)


def _adamw_tiny(slots, w, m, v):
    names = [n for n, _ in TINY_ROWS if n != "loss"]
    k = len(names)

    def body(s_ref, *refs):
        ins, outs, loss_ref = refs[:3 * k], refs[3 * k:7 * k], refs[7 * k]
        g_all = s_ref[0]
        for d in range(1, N_DEV):
            g_all = g_all + s_ref[d]
        at = 0
        for i, (name, r) in enumerate(TINY_ROWS[:k]):
            w_ref, m_ref, v_ref = ins[i], ins[k + i], ins[2 * k + i]
            o = outs[4 * i:4 * i + 4]
            if name == "b_s":
                pieces = [(slice(at, at + r), slice(0, LANES), (slice(None), slice(None)))]
            else:
                pieces = [(slice(at + j, at + j + 1), slice(0, c.stop - c.start), (slice(None), c))
                          for j, c in _tiny_pieces(w_ref.shape[1])]
            for rows, lanes, dst in pieces:
                g = g_all[rows, lanes]
                res = (g,) + _adamw(w_ref[dst], g, m_ref[dst], v_ref[dst])
                for ref, val in zip(o, res):
                    ref[dst] = val
            at += r
        loss_ref[...] = g_all[at:at + 1, :]

    shapes = [S(w[n].shape, F32) for n in names]
    out = pl.pallas_call(
        body, name="adamw_tiny", out_shape=[s for s in shapes for _ in range(4)] + [S((1, LANES), F32)],
    )(slots, *[w[n] for n in names], *[m[n] for n in names], *[v[n] for n in names])
    stores = ({}, {}, {}, {})
    for i, n in enumerate(names):
        for store, t in zip(stores, out[4 * i:4 * i + 4]):
            store[n] = t
    return stores, out[4 * k]


WEIGHTS =('g_ffn1', 'w_ffn1_in', 'w_ffn1_out', 'g_mix', 'w_in', 'b_f', 'g_q', 'g_k', 'g_sgu', 'w_s', 'b_s',
           'g_fox_o', 'g_gmlp_o', 'w_out', 'g_ca', 'g_mem', 'w_cq', 'w_ckv', 'g_cq', 'g_ck', 'w_co', 'g_ffn2',
           'w_ffn2_in', 'w_ffn2_out')
BIG = ('w_ffn1_in', 'w_ffn1_out', 'w_in', 'w_out', 'w_cq', 'w_ckv', 'w_co', 'w_ffn2_in', 'w_ffn2_out')
TRANSPOSED = ('w_ffn1_in', 'w_in', 'w_ffn2_in')
TWO_LEVEL = ('w_ffn1_in', 'w_in')
GATHER_GROUPS = {"ffn1_up": ("w_ffn1_in",), "ffn1_dn": ("w_ffn1_out",), "mix": ("w_in", "w_out"),
                 "ca": ("w_cq", "w_ckv", "w_co"), "ffn2": ("w_ffn2_in", "w_ffn2_out")}
QKV_W = 3 * FOX_W
UV_OFF = QKV_W + FOX_HEADS


def kernel(x, mem, g_ffn1, w_ffn1_in, w_ffn1_out, g_mix, w_in, b_f, g_q, g_k, g_sgu, w_s, b_s, g_fox_o, g_gmlp_o, w_out, g_ca, g_mem, w_cq, w_ckv, g_cq, g_ck, w_co, g_ffn2, w_ffn2_in, w_ffn2_out, loss_target, m_g_ffn1, m_w_ffn1_in, m_w_ffn1_out, m_g_mix, m_w_in, m_b_f, m_g_q, m_g_k, m_g_sgu, m_w_s, m_b_s, m_g_fox_o, m_g_gmlp_o, m_w_out, m_g_ca, m_g_mem, m_w_cq, m_w_ckv, m_g_cq, m_g_ck, m_w_co, m_g_ffn2, m_w_ffn2_in, m_w_ffn2_out, v_g_ffn1, v_w_ffn1_in, v_w_ffn1_out, v_g_mix, v_w_in, v_b_f, v_g_q, v_g_k, v_g_sgu, v_w_s, v_b_s, v_g_fox_o, v_g_gmlp_o, v_w_out, v_g_ca, v_g_mem, v_w_cq, v_w_ckv, v_g_cq, v_g_ck, v_w_co, v_g_ffn2, v_w_ffn2_in, v_w_ffn2_out):
    args = dict(locals())
    w = {n: args[n] for n in WEIGHTS}
    mo = {n: args["m_" + n] for n in WEIGHTS}
    vo = {n: args["v_" + n] for n in WEIGHTS}
    D = D_MODEL

    def local(n, a):
        return a[0].T if n in TRANSPOSED else a[0]

    g_peers = [NEAR_PEERS if n in TWO_LEVEL else ALL_PEERS for n in BIG]
    handles = {}

    def start_gather(name, names, arrays):
        snd, rcv, src, land, token = _copy_start(name, arrays, _place_own(arrays, True), True,
                                                 peers=[g_peers[BIG.index(n)] for n in names])
        handles.update({n: (src[i], land[i], snd[i], rcv[i]) for i, n in enumerate(names)})
        return token

    first = local(BIG[0], w[BIG[0]]).astype(BF)
    fb = first.shape[0]
    token_first = start_gather("gather_start_first", BIG[:1], [first])
    token_rest = start_gather("gather_start_rest", BIG[1:],
                              [(local(n, w[n]) + token_first[0:1, 0:1]).astype(BF) for n in BIG[1:]])

    tiny_names = [n for n, _ in TINY_ROWS if n != "loss"]

    def weights(group, after):
        names = GATHER_GROUPS[group]
        hs = [handles[n] for n in names]
        got = list(_copy_wait("gather_wait_" + group, [h[0] for h in hs], [h[1] for h in hs], [h[2] for h in hs],
                              [h[3] for h in hs], token_rest if group == "ffn1_up" else after, True,
                              peers=[g_peers[BIG.index(n)] for n in names]))
        passed = [i for i, n in enumerate(names) if n in TWO_LEVEL]
        if passed:
            f_snd, f_rcv, f_land, f_token = _forward_start("gather_pass_start_" + group, [got[i] for i in passed])
            for i, t in zip(passed, _forward_wait("gather_pass_wait_" + group, f_land, f_snd, f_rcv, f_token)):
                got[i] = t
        got = dict(zip(names, got))
        if group == "ffn1_up":
            return {"wup1": got["w_ffn1_in"].reshape(2, N_FFN_BLK, fb, D)}
        if group == "ffn1_dn":
            return {"wdn1": got["w_ffn1_out"].reshape(N_FFN_BLK, fb, D)}
        if group == "mix":
            full = got["w_in"].reshape(-1, D)
            wz = jnp.concatenate([full[:QKV_W], full[UV_OFF:], full[QKV_W:UV_OFF],
                                  jnp.zeros((LANES - FOX_HEADS, D), BF)], axis=0)
            return {"wz": wz, "wout": got["w_out"].reshape(D, D)}
        if group == "ca":
            return {"wcq": got["w_cq"].reshape(D, D), "wco": got["w_co"].reshape(D, D), "wckv": got["w_ckv"]}
        return {"wup2": got["w_ffn2_in"].reshape(2, N_FFN_BLK, fb, D),
                "wdn2": got["w_ffn2_out"].reshape(N_FFN_BLK, fb, D)}

    flying = {}

    def emit(group, g):
        if group == "w_s":
            part = [g["w_s"].reshape(-1, LANES)]
            *copies, token = _copy_start("w_s_start", part, _place_own(part, True), True)
            flying[group] = copies
            return token
        if group == "ffn2":
            parts = {"w_ffn2_in": g["wup2"], "w_ffn2_out": g["wdn2"].reshape(N_DEV, -1, D)}
        elif group == "ffn1_dn":
            parts = {"w_ffn1_out": g["wdn1"].reshape(N_DEV, -1, D)}
        elif group == "ffn1_up":
            parts = {"w_ffn1_in": g["wup1"]}
        else:
            gz = g["wz"]
            g_in = jnp.concatenate([gz[:QKV_W], gz[Z_F:Z_F + FOX_HEADS], gz[QKV_W:Z_F]], axis=0)
            parts = {"w_in": g_in.reshape(N_DEV, -1, D).astype(BF),
                     "w_out": g["wout"].reshape(N_DEV, -1, D), "w_cq": g["wcq"].reshape(N_DEV, -1, D),
                     "w_co": g["wco"].reshape(N_DEV, -1, D), "w_ckv": g["wckv"]}
        names = list(parts)
        srcs = [parts[n] for n in names]
        *copies, token = _copy_start("exchange_start_" + group, srcs, [lax.empty(s.shape, s.dtype) for s in srcs],
                                     False)
        flying[group] = (names, copies)
        return token

    small = {n: (w[n][0] if n == "b_s" else w[n]) for n in tiny_names}
    small["w_s"] = w["w_s"][0]

    sq, dx0, gs = _local_step(x[0], mem[0], loss_target[0], small, weights, emit)

    sm_parts = [_pack_tiny(gs, sq)]
    sm_snd, sm_rcv, sm_src, sm_land, sm_token = _copy_start("tiny_start", sm_parts, _place_own(sm_parts, True), True)

    grad, delta, new_m, new_v = {}, {}, {}, {}

    def update(group, after):
        names, (snd, rcv, srcs, lands) = flying[group]
        owns, slots = _copy_wait("exchange_wait_" + group, srcs, lands, snd, rcv, after, False, with_srcs=True)
        for n, sl, own in zip(names, slots, owns):
            g, d, m2, v2 = _adamw_big("adamw_" + n, sl, local(n, w[n]), local(n, mo[n]), local(n, vo[n]), own=own)
            grad[n], delta[n], new_m[n], new_v[n] = (
                (t.T if n in TRANSPOSED else t).reshape(w[n].shape) for t in (g, d, m2, v2))
        return d

    last = update("ffn2", sm_token)
    last = update("mid", last)
    last = update("ffn1_dn", last)
    last = update("ffn1_up", last)
    ws_snd, ws_rcv, ws_src, ws_land = flying["w_s"]
    ws_all, = _copy_wait("w_s_wait", ws_src, ws_land, ws_snd, ws_rcv, last, True)
    tiny_all, = _copy_wait("tiny_wait", sm_src, sm_land, sm_snd, sm_rcv, ws_all, True)
    ws_shape = w["w_s"].shape
    for store, t in zip((grad, delta, new_m, new_v), _adamw_big(
            "adamw_w_s", ws_all, *[a["w_s"].reshape(-1, LANES) for a in (w, mo, vo)])):
        store["w_s"] = t.reshape(ws_shape)
    stores, loss_row = _adamw_tiny(tiny_all, *[{n: (a[n][0] if n == "b_s" else a[n]) for n in tiny_names}
                                               for a in (w, mo, vo)])
    for store, t in zip((grad, delta, new_m, new_v), stores):
        store.update({n: v.reshape(w[n].shape) for n, v in t.items()})
    loss = loss_row[0, 0] * (0.5 / D)

    return (loss, dx0[None], *[grad[n] for n in WEIGHTS], *[delta[n] for n in WEIGHTS],
            *[new_m[n] for n in WEIGHTS], *[new_v[n] for n in WEIGHTS])
```

```python
import functools

import jax
import jax.numpy as jnp
from jax import lax
from jax.experimental import pallas as pl
from jax.experimental.pallas import tpu as pltpu

F32 = jnp.float32
BF = jnp.bfloat16
S = jax.ShapeDtypeStruct

N_DEV = 8
D_MODEL = 1024
FOX_HEADS, FOX_HD = 8, 64
FOX_W = 512
GMLP_G, GMLP_GD = 8, 64
GMLP_W = 512
CHUNK = 128
CA_HEADS, CA_HD = 4, 256
N_FFN_BLK = 4
ZW = 2688
Z_Q, Z_K, Z_V, Z_U, Z_G, Z_F = 0, 512, 1024, 1536, 2048, 2560
EPS = 1e-6
NEG = -1e30
LANES = 128

ADAM_LR, ADAM_B1, ADAM_B2, ADAM_EPS, ADAM_WD, ADAM_STEP = 0.001, 0.9, 0.999, 1e-08, 0.01, 10

VMEM_LIMIT = 52 * 2 ** 20


def _cp(n_axes):
    return pltpu.CompilerParams(dimension_semantics=("arbitrary",) * n_axes, vmem_limit_bytes=VMEM_LIMIT)


def _nn(a, b):
    return jnp.dot(a, b, preferred_element_type=F32)


def _nt(a, b):
    return lax.dot_general(a, b, (((1,), (1,)), ((), ())), preferred_element_type=F32)


def _tn(a, b):
    return lax.dot_general(a, b, (((0,), (0,)), ((), ())), preferred_element_type=F32)


def _hi(a, b):
    return jnp.dot(a, b, precision=lax.Precision.HIGHEST, preferred_element_type=F32)


def _rstd(x):
    return lax.rsqrt(jnp.mean(x * x, axis=-1, keepdims=True) + EPS)


def _norm_bwd(dy, x, g, r=None):
    r = _rstd(x) if r is None else r
    xh = x * r
    dxh = dy * g
    dx = r * (dxh - xh * jnp.mean(dxh * xh, axis=-1, keepdims=True))
    return dx, dy * xh


def _acc_rows(ref, first, val):
    srow = jnp.sum(val, axis=0, keepdims=True)

    @pl.when(first)
    def _():
        ref[...] = srow

    @pl.when(jnp.logical_not(first))
    def _():
        ref[...] += srow


def _gelu(x):
    c = 0.7978845608028654
    return 0.5 * x * (1.0 + jnp.tanh(c * (x + 0.044715 * x * x * x)))


def _gelu_grad(x):
    c = 0.7978845608028654
    t = jnp.tanh(c * (x + 0.044715 * x * x * x))
    return 0.5 * (1.0 + t) + 0.5 * x * (1.0 - t * t) * c * (1.0 + 3 * 0.044715 * x * x)


def _tile(n, pref):
    return pref if n % pref == 0 else n


def _ffn_up(name, x, g, wup):
    T, D = x.shape
    FB = wup.shape[-2]
    tm = _tile(T, 1024)

    def body(x_ref, g_ref, w_ref, a_ref, h_ref):
        @pl.when(pl.program_id(1) == 0)
        def _():
            xf = x_ref[...]
            h_ref[...] = (xf * _rstd(xf) * g_ref[...]).astype(BF)

        hb = h_ref[...]
        gg = _nt(hb, w_ref[0])
        uu = _nt(hb, w_ref[1])
        a_ref[...] = (gg * jax.nn.sigmoid(gg) * uu).astype(BF)

    return pl.pallas_call(
        body, name=name, grid=(T // tm, N_FFN_BLK),
        in_specs=[pl.BlockSpec((tm, D), lambda i, j: (i, 0)),
                  pl.BlockSpec((1, D), lambda i, j: (0, 0)),
                  pl.BlockSpec((2, None, FB, D), lambda i, j: (0, j, 0, 0))],
        out_specs=[pl.BlockSpec((None, tm, FB), lambda i, j: (j, i, 0)),
                   pl.BlockSpec((tm, D), lambda i, j: (i, 0))],
        out_shape=[S((N_FFN_BLK, T, FB), BF), S((T, D), BF)],
        compiler_params=_cp(2))(x, g, wup)


def _ffn_down(name, a, wdn, x):
    _, T, FB = a.shape
    D = x.shape[1]
    tm = _tile(T, 512)

    def body(a_ref, w_ref, x_ref, o_ref):
        p = _nn(a_ref[0], w_ref[0])
        for j in range(1, N_FFN_BLK):
            p = p + _nn(a_ref[j], w_ref[j])
        o_ref[...] = x_ref[...] + 0.5 * p

    return pl.pallas_call(
        body, name=name, grid=(T // tm,),
        in_specs=[pl.BlockSpec((N_FFN_BLK, tm, FB), lambda i: (0, i, 0)),
                  pl.BlockSpec((N_FFN_BLK, FB, D), lambda i: (0, 0, 0)),
                  pl.BlockSpec((tm, D), lambda i: (i, 0))],
        out_specs=pl.BlockSpec((tm, D), lambda i: (i, 0)),
        out_shape=S((T, D), F32),
        compiler_params=_cp(1))(a, wdn, x)


def _ffn_down_loss(name, a, wdn, x, target):
    _, T, FB = a.shape
    D = x.shape[1]
    tm = _tile(T, 512)

    def body(a_ref, w_ref, x_ref, t_ref, d_ref, db_ref, loss_ref):
        i = pl.program_id(0)
        p = _nn(a_ref[0], w_ref[0])
        for j in range(1, N_FFN_BLK):
            p = p + _nn(a_ref[j], w_ref[j])
        diff = (x_ref[...] + 0.5 * p) - t_ref[...]
        dy = diff * (1.0 / D)
        d_ref[...] = dy
        db_ref[...] = dy.astype(BF)
        sq = jnp.zeros((8, LANES), F32) + jnp.sum(diff * diff)

        @pl.when(i == 0)
        def _():
            loss_ref[...] = sq

        @pl.when(i > 0)
        def _():
            loss_ref[...] += sq

    row = pl.BlockSpec((tm, D), lambda i: (i, 0))
    return pl.pallas_call(
        body, name=name, grid=(T // tm,),
        in_specs=[pl.BlockSpec((N_FFN_BLK, tm, FB), lambda i: (0, i, 0)),
                  pl.BlockSpec((N_FFN_BLK, FB, D), lambda i: (0, 0, 0)), row, row],
        out_specs=[row, row, pl.BlockSpec((8, LANES), lambda i: (0, 0))],
        out_shape=[S((T, D), F32), S((T, D), BF), S((8, LANES), F32)],
        compiler_params=_cp(1))(a, wdn, x, target)


def _ffn_bwd_act(name, dyb, h, wup, wdn):
    T, D = h.shape
    FB = wup.shape[-2]
    tm = _tile(T, 1024)

    def body(d_ref, h_ref, wu_ref, wd_ref, o_ref):
        da = 0.5 * _nt(d_ref[...], wd_ref[...])
        hb = h_ref[...]
        gg = _nt(hb, wu_ref[0])
        uu = _nt(hb, wu_ref[1])
        sg = jax.nn.sigmoid(gg)
        o_ref[0] = (da * uu * (sg * (1.0 + gg * (1.0 - sg)))).astype(BF)
        o_ref[1] = (da * (gg * sg)).astype(BF)

    return pl.pallas_call(
        body, name=name, grid=(T // tm, N_FFN_BLK),
        in_specs=[pl.BlockSpec((tm, D), lambda i, j: (i, 0)),
                  pl.BlockSpec((tm, D), lambda i, j: (i, 0)),
                  pl.BlockSpec((2, None, FB, D), lambda i, j: (0, j, 0, 0)),
                  pl.BlockSpec((None, FB, D), lambda i, j: (j, 0, 0))],
        out_specs=pl.BlockSpec((2, None, tm, FB), lambda i, j: (0, j, i, 0)),
        out_shape=S((2, N_FFN_BLK, T, FB), BF),
        compiler_params=_cp(2))(dyb, h, wup, wdn)


def _ffn_dx(name, dgu, wup, x, g, dy):
    T, D = x.shape
    FB = wup.shape[-2]
    tm = _tile(T, 512)

    def body(d_ref, w_ref, x_ref, g_ref, dy_ref, dx_ref, dg_ref):
        p = None
        for j in range(N_FFN_BLK):
            for half in range(2):
                t = _nn(d_ref[half, j], w_ref[half, j])
                p = t if p is None else p + t
        dx, dgr = _norm_bwd(p, x_ref[...], g_ref[...])
        dx_ref[...] = dx + dy_ref[...]
        _acc_rows(dg_ref, pl.program_id(0) == 0, dgr)

    return pl.pallas_call(
        body, name=name, grid=(T // tm,),
        in_specs=[pl.BlockSpec((2, N_FFN_BLK, tm, FB), lambda i: (0, 0, i, 0)),
                  pl.BlockSpec((2, N_FFN_BLK, FB, D), lambda i: (0, 0, 0, 0), pipeline_mode=pl.Buffered(1)),
                  pl.BlockSpec((tm, D), lambda i: (i, 0)),
                  pl.BlockSpec((1, D), lambda i: (0, 0)),
                  pl.BlockSpec((tm, D), lambda i: (i, 0))],
        out_specs=[pl.BlockSpec((tm, D), lambda i: (i, 0)),
                   pl.BlockSpec((1, D), lambda i: (0, 0))],
        out_shape=[S((T, D), F32), S((1, D), F32)],
        compiler_params=_cp(1))(dgu, wup, x, g, dy)


def _tn_matmul(name, a, a_spec, b, out_shape, out_spec, n_blocks, scale=1.0, after=None):
    extra = [] if after is None else [after]

    def body(a_ref, b_ref, *rest):
        o_ref = rest[-1]
        o_ref[...] = (_tn(a_ref[...], b_ref[...]) * scale).astype(o_ref.dtype)

    return pl.pallas_call(
        body, name=name, grid=(n_blocks,),
        in_specs=[a_spec, pl.BlockSpec(b.shape, lambda j: (0, 0), pipeline_mode=pl.Buffered(1))]
        + [pl.BlockSpec((8, LANES), lambda j: (0, 0)) for _ in extra],
        out_specs=out_spec, out_shape=out_shape, compiler_params=_cp(1))(a, b, *extra)


def _ffn_dwup(name, h, dgu, after=None):
    T, D = h.shape
    FB = dgu.shape[-1]
    return _tn_matmul(
        name + "_dwup", dgu.reshape(2 * N_FFN_BLK, T, FB), pl.BlockSpec((None, T, FB), lambda j: (j, 0, 0)), h,
        S((2 * N_FFN_BLK, FB, D), BF), pl.BlockSpec((None, FB, D), lambda j: (j, 0, 0)), 2 * N_FFN_BLK,
        after=after)


def _ffn_dwdn(name, a, dyb):
    _, T, FB = a.shape
    D = dyb.shape[1]
    return _tn_matmul(
        name + "_dwdn", a, pl.BlockSpec((None, T, FB), lambda j: (j, 0, 0)), dyb,
        S((N_FFN_BLK, FB, D), BF), pl.BlockSpec((None, FB, D), lambda j: (j, 0, 0)), N_FFN_BLK, scale=0.5)


def _mix_proj(x, g, wz):
    T, D = x.shape
    tm = _tile(T, 512)

    def body(x_ref, g_ref, w_ref, z_ref, h_ref):
        xf = x_ref[...]
        hb = (xf * _rstd(xf) * g_ref[...]).astype(BF)
        h_ref[...] = hb
        z_ref[...] = _nt(hb, w_ref[...])

    return pl.pallas_call(
        body, name="mix_proj", grid=(T // tm,),
        in_specs=[pl.BlockSpec((tm, D), lambda i: (i, 0)),
                  pl.BlockSpec((1, D), lambda i: (0, 0)),
                  pl.BlockSpec((ZW, D), lambda i: (0, 0))],
        out_specs=[pl.BlockSpec((tm, ZW), lambda i: (i, 0)),
                   pl.BlockSpec((tm, D), lambda i: (i, 0))],
        out_shape=[S((T, ZW), F32), S((T, D), BF)],
        compiler_params=_cp(1))(x, g, wz)


def _tri(n, lower):
    r = lax.broadcasted_iota(jnp.int32, (n, n), 0)
    c = lax.broadcasted_iota(jnp.int32, (n, n), 1)
    return (r >= c) if lower else (r <= c)


def _spatial_mix(vgn_b, ws_ref, bst, tm):
    tril = _tri(CHUNK, True)
    wms = [jnp.where(tril, ws_ref[g], 0.0).astype(BF) for g in range(GMLP_G)]
    rows = []
    for c in range(tm // CHUNK):
        cols = []
        for g in range(GMLP_G):
            vs = vgn_b[c * CHUNK:(c + 1) * CHUNK, g * GMLP_GD:(g + 1) * GMLP_GD]
            cols.append(_nn(wms[g], vs) + bst[:, g:g + 1])
        rows.append(jnp.concatenate(cols, axis=1))
    return jnp.concatenate(rows, axis=0), wms


HB = 128
AUG_W = FOX_HEADS * HB
COL_A, COL_B, COL_C = 64, 67, 70
RS_Q, RS_K, RS_V, RS_O = 0, 8, 16, 17


def _spread_matrix():
    r = jnp.arange(FOX_W)
    return (jnp.arange(AUG_W)[None, :] == ((r // FOX_HD) * HB + r % FOX_HD)[:, None]).astype(BF)


def _piece_matrix(col):
    r = jnp.arange(LANES)
    dst = jnp.where(r < 3 * FOX_HEADS, (r % FOX_HEADS) * HB + col + r // FOX_HEADS, -1)
    return (jnp.arange(AUG_W)[None, :] == dst[:, None]).astype(BF)


def _ones_row(cols):
    c = jnp.arange(AUG_W) % HB
    hit = functools.reduce(jnp.logical_or, [(c >= a) & (c < a + 3) for a in cols])
    return hit.astype(F32)[None, :]


def _pieces(x):
    lane = lax.broadcasted_iota(jnp.int32, x.shape, 1)
    x = jnp.where(lane < FOX_HEADS, x, 0.0)
    hi = x.astype(BF).astype(F32)
    r1 = x - hi
    mid = r1.astype(BF).astype(F32)
    lo = (r1 - mid).astype(BF).astype(F32)
    return (hi + pltpu.roll(mid, FOX_HEADS, 1) + pltpu.roll(lo, 2 * FOX_HEADS, 1)).astype(BF)


def _mix_prep(z, bf128, g_q, g_k, g_sgu, w_s, b_st, g_go):
    T = z.shape[0]
    tm = _tile(T, 512)
    spread, pc_q, pc_k = _spread_matrix(), _piece_matrix(COL_A), _piece_matrix(COL_B)
    one_q, one_k, one_v = _ones_row([COL_B]), _ones_row([COL_A, COL_C]), _ones_row([COL_A])

    def body(z_ref, bf_ref, gq_ref, gk_ref, gs_ref, ws_ref, bst_ref, go_ref, sp_ref, pq_ref, pk_ref, oq_ref, ok_ref,
             ov_ref, q_ref, k_ref, v_ref, y_ref, rs_ref, carry_ref, qn_sc, kn_sc):
        i = pl.program_id(0)

        @pl.when(i == 0)
        def _():
            carry_ref[...] = jnp.zeros_like(carry_ref)

        rs_ref[...] = jnp.zeros_like(rs_ref)
        for h in range(FOX_HEADS):
            hs = slice(h * FOX_HD, (h + 1) * FOX_HD)
            qh = z_ref[:, Z_Q + h * FOX_HD:Z_Q + (h + 1) * FOX_HD]
            kh = z_ref[:, Z_K + h * FOX_HD:Z_K + (h + 1) * FOX_HD]
            rq, rk = _rstd(qh), _rstd(kh)
            rs_ref[:, RS_Q + h:RS_Q + h + 1] = rq
            rs_ref[:, RS_K + h:RS_K + h + 1] = rk
            qn_sc[:, hs] = (qh * rq * gq_ref[...] * 0.125).astype(BF)
            kn_sc[:, hs] = (kh * rk * gk_ref[...]).astype(BF)

        fl = z_ref[:, Z_F:Z_F + LANES] + bf_ref[...]
        logf = jnp.minimum(fl, 0.0) - jnp.log1p(jnp.exp(-jnp.abs(fl)))
        csum = _hi(_tri(tm, True).astype(F32), logf) + carry_ref[...]
        carry_ref[...] = csum[tm - 1:tm, :]
        sp = sp_ref[...]
        q_ref[...] = (_nn(qn_sc[...], sp) + _nn(_pieces(csum), pq_ref[...]) + oq_ref[...]).astype(BF)
        k_ref[...] = (_nn(kn_sc[...], sp) + _nn(_pieces(-csum), pk_ref[...]) + ok_ref[...]).astype(BF)
        v_ref[...] = (_nn(z_ref[:, Z_V:Z_V + FOX_W].astype(BF), sp) + ov_ref[...]).astype(BF)

        u = _gelu(z_ref[:, Z_U:Z_U + GMLP_W])
        vg = _gelu(z_ref[:, Z_G:Z_G + GMLP_W])
        rv = _rstd(vg)
        vgn = (vg * rv * gs_ref[...]).astype(BF)
        mixed, _ = _spatial_mix(vgn, ws_ref, bst_ref[...], tm)
        sgu = u * mixed
        ro = _rstd(sgu)
        y_ref[...] = (sgu * ro * go_ref[...]).astype(BF)
        rs_ref[:, RS_V:RS_V + 1] = rv
        rs_ref[:, RS_O:RS_O + 1] = ro

    row = lambda i: (i, 0)
    fix2 = lambda i: (0, 0)
    return pl.pallas_call(
        body, name="mix_prep", grid=(T // tm,),
        in_specs=[pl.BlockSpec((tm, ZW), row),
                  pl.BlockSpec((1, LANES), fix2), pl.BlockSpec((1, FOX_HD), fix2), pl.BlockSpec((1, FOX_HD), fix2),
                  pl.BlockSpec((1, GMLP_W), fix2), pl.BlockSpec((GMLP_G, CHUNK, CHUNK), lambda i: (0, 0, 0)),
                  pl.BlockSpec((CHUNK, GMLP_G), fix2), pl.BlockSpec((1, GMLP_W), fix2),
                  pl.BlockSpec((FOX_W, AUG_W), fix2), pl.BlockSpec((LANES, AUG_W), fix2),
                  pl.BlockSpec((LANES, AUG_W), fix2), pl.BlockSpec((1, AUG_W), fix2), pl.BlockSpec((1, AUG_W), fix2),
                  pl.BlockSpec((1, AUG_W), fix2)],
        out_specs=[pl.BlockSpec((tm, AUG_W), row), pl.BlockSpec((tm, AUG_W), row), pl.BlockSpec((tm, AUG_W), row),
                   pl.BlockSpec((tm, GMLP_W), row), pl.BlockSpec((tm, LANES), row)],
        out_shape=[S((T, AUG_W), BF), S((T, AUG_W), BF), S((T, AUG_W), BF), S((T, GMLP_W), BF), S((T, LANES), F32)],
        scratch_shapes=[pltpu.VMEM((1, LANES), F32), pltpu.VMEM((tm, FOX_W), BF), pltpu.VMEM((tm, FOX_W), BF)],
        compiler_params=_cp(1))(z, bf128, g_q, g_k, g_sgu, w_s, b_st, g_go, spread, pc_q, pc_k, one_q, one_k, one_v)


def _fox_fwd(q, k, v):
    T = q.shape[0]
    tq = _tile(T, 1024)
    nq = T // tq

    def body(q_ref, k_ref, v_ref, o_ref, lse_ref, m_sc, acc_sc):
        i, j = pl.program_id(0), pl.program_id(1)

        @pl.when(j == 0)
        def _():
            m_sc[...] = jnp.full(m_sc.shape, NEG, F32)
            acc_sc[...] = jnp.zeros_like(acc_sc)

        def step(masked):
            mask = _tri(tq, True) if masked else None
            for h in range(FOX_HEADS):
                hb = slice(h * HB, (h + 1) * HB)
                s = _nt(q_ref[:, hb], k_ref[:, hb])
                if masked:
                    s = jnp.where(mask, s, NEG)
                m_prev = m_sc[h]
                m_new = jnp.maximum(m_prev, jnp.broadcast_to(jnp.max(s, axis=1, keepdims=True), (tq, HB)))
                p = jnp.exp(s - jnp.tile(m_new, (1, tq // HB))).astype(BF)
                acc_sc[:, hb] = jnp.exp(m_prev - m_new) * acc_sc[:, hb] + _nn(p, v_ref[:, hb])
                m_sc[h] = m_new

        @pl.when(j < i)
        def _():
            step(False)

        @pl.when(j == i)
        def _():
            step(True)
            lse_ref[...] = jnp.zeros_like(lse_ref)
            for h in range(FOX_HEADS):
                l = acc_sc[:, h * HB + COL_A:h * HB + COL_A + 1]
                o_ref[:, h * FOX_HD:(h + 1) * FOX_HD] = acc_sc[:, h * HB:h * HB + FOX_HD] / l
                lse_ref[:, h:h + 1] = m_sc[h][:, 0:1] + jnp.log(l)

    qi = lambda i, j: (i, 0)
    kj = lambda i, j: (jnp.minimum(i, j), 0)
    return pl.pallas_call(
        body, name="fox_fwd", grid=(nq, nq),
        in_specs=[pl.BlockSpec((tq, AUG_W), qi), pl.BlockSpec((tq, AUG_W), kj), pl.BlockSpec((tq, AUG_W), kj)],
        out_specs=[pl.BlockSpec((tq, FOX_W), qi), pl.BlockSpec((tq, LANES), qi)],
        out_shape=[S((T, FOX_W), F32), S((T, LANES), F32)],
        scratch_shapes=[pltpu.VMEM((FOX_HEADS, tq, HB), F32), pltpu.VMEM((tq, AUG_W), F32)],
        compiler_params=_cp(2))(q, k, v)


def _fox_bwd(q, k, v, dob):
    T = q.shape[0]
    tq = _tile(T, 512)
    nq = T // tq
    n_sweeps = 1
    half = AUG_W // n_sweeps
    hpg = FOX_HEADS // n_sweeps

    pairs = [(j, i) for j in range(nq) for i in range(j, nq)]
    jt = jnp.asarray([p[0] for p in pairs], jnp.int32)
    it = jnp.asarray([p[1] for p in pairs], jnp.int32)

    def body(jt_ref, it_ref, q_ref, k_ref, v_ref, do_ref, dq_ref, dk_ref, dv_ref, dq_sc):
        t = pl.program_id(1)
        j, i = jt_ref[t], it_ref[t]

        @pl.when(t == 0)
        def _():
            dq_sc[...] = jnp.zeros_like(dq_sc)

        @pl.when(i == j)
        def _():
            dk_ref[...] = jnp.zeros_like(dk_ref)
            dv_ref[...] = jnp.zeros_like(dv_ref)

        def step(masked):
            rows = pl.ds(pl.multiple_of(i * tq, tq), tq)
            mask = _tri(tq, True) if masked else None
            for h in range(hpg):
                hb = slice(h * HB, (h + 1) * HB)
                qh, kh, vh, doh = q_ref[:, hb], k_ref[:, hb], v_ref[:, hb], do_ref[:, hb]
                s = _nt(qh, kh)
                if masked:
                    s = jnp.where(mask, s, NEG)
                p = jnp.exp(s)
                dsb = (p * _nt(doh, vh)).astype(BF)
                dv_ref[:, hb] += _tn(p.astype(BF), doh)
                dk_ref[:, hb] += _tn(dsb, qh)
                dq_sc[rows, hb] += _nn(dsb, kh)

        @pl.when(i > j)
        def _():
            step(False)

        @pl.when(i == j)
        def _():
            step(True)
            dq_ref[...] = dq_sc[pl.ds(pl.multiple_of(j * tq, tq), tq), :]

    qi = pl.BlockSpec((tq, half), lambda g, t, jt_ref, it_ref: (it_ref[t], g))
    kj = pl.BlockSpec((tq, half), lambda g, t, jt_ref, it_ref: (jt_ref[t], g))
    return pl.pallas_call(
        body, name="fox_bwd",
        grid_spec=pltpu.PrefetchScalarGridSpec(
            num_scalar_prefetch=2, grid=(n_sweeps, len(pairs)), in_specs=[qi, kj, kj, qi], out_specs=[kj, kj, kj],
            scratch_shapes=[pltpu.VMEM((T, half), F32)]),
        out_shape=[S((T, AUG_W), F32), S((T, AUG_W), F32), S((T, AUG_W), F32)],
        compiler_params=_cp(2))(jt, it, q, k, v, dob)


def _mix_out(attn, yg, g_fo, wout, x):
    T, D = x.shape
    tm = _tile(T, 1024)

    def body(a_ref, y_ref, g_ref, w_ref, x_ref, o_ref):
        at = a_ref[...]
        yf = (at * _rstd(at) * g_ref[...]).astype(BF)
        o_ref[...] = x_ref[...] + _nn(yf, w_ref[:FOX_W, :]) + _nn(y_ref[...], w_ref[FOX_W:, :])

    row = lambda i: (i, 0)
    return pl.pallas_call(
        body, name="mix_out", grid=(T // tm,),
        in_specs=[pl.BlockSpec((tm, FOX_W), row), pl.BlockSpec((tm, GMLP_W), row),
                  pl.BlockSpec((1, FOX_W), lambda i: (0, 0)), pl.BlockSpec((D, D), lambda i: (0, 0)),
                  pl.BlockSpec((tm, D), row)],
        out_specs=pl.BlockSpec((tm, D), row),
        out_shape=S((T, D), F32),
        compiler_params=_cp(1))(attn, yg, g_fo, wout, x)


def _mix_out_bwd(dx, attn, yg, g_fo, wout, qf, lse):
    T, D = dx.shape
    tm = _tile(T, 512)
    n = T // tm
    spread, pc_l, pc_d = _spread_matrix(), _piece_matrix(COL_C), _piece_matrix(COL_A)

    def body(dx_ref, a_ref, y_ref, g_ref, w_ref, qf_ref, lse_ref, sp_ref, pl_ref, pd_ref,
             qb_ref, dob_ref, dyg_ref, dw_ref, dg_ref, acc_ref, dsum_ref):
        i = pl.program_id(0)
        dxb = dx_ref[...].astype(BF)
        at = a_ref[...]
        yf = (at * _rstd(at) * g_ref[...]).astype(BF)
        dy = _nt(dxb, w_ref[...])
        p_top = _tn(yf, dxb)
        p_bot = _tn(y_ref[...], dxb)

        @pl.when(i == 0)
        def _():
            acc_ref[:FOX_W, :] = p_top
            acc_ref[FOX_W:, :] = p_bot

        @pl.when(i > 0)
        def _():
            acc_ref[:FOX_W, :] += p_top
            acc_ref[FOX_W:, :] += p_bot

        @pl.when(i == n - 1)
        def _():
            dw_ref[...] = acc_ref[...].astype(BF)

        dat, dgr = _norm_bwd(dy[:, :FOX_W], at, g_ref[...])
        _acc_rows(dg_ref, i == 0, dgr)
        dyg_ref[...] = dy[:, FOX_W:]
        prod = dat * at
        dsum_ref[...] = jnp.zeros_like(dsum_ref)
        for h in range(FOX_HEADS):
            dsum_ref[:, h:h + 1] = jnp.sum(prod[:, h * FOX_HD:(h + 1) * FOX_HD], axis=1, keepdims=True)
        dob_ref[...] = (_nn(dat.astype(BF), sp_ref[...]) + _nn(_pieces(-dsum_ref[...]), pd_ref[...])).astype(BF)
        qb_ref[...] = (qf_ref[...].astype(F32) + _nn(_pieces(-lse_ref[...]), pl_ref[...])).astype(BF)

    row = lambda i: (i, 0)
    fix = lambda i: (0, 0)
    return pl.pallas_call(
        body, name="mix_out_bwd", grid=(n,),
        in_specs=[pl.BlockSpec((tm, D), row), pl.BlockSpec((tm, FOX_W), row), pl.BlockSpec((tm, GMLP_W), row),
                  pl.BlockSpec((1, FOX_W), fix), pl.BlockSpec((D, D), fix), pl.BlockSpec((tm, AUG_W), row),
                  pl.BlockSpec((tm, LANES), row), pl.BlockSpec((FOX_W, AUG_W), fix), pl.BlockSpec((LANES, AUG_W), fix),
                  pl.BlockSpec((LANES, AUG_W), fix)],
        out_specs=[pl.BlockSpec((tm, AUG_W), row), pl.BlockSpec((tm, AUG_W), row), pl.BlockSpec((tm, GMLP_W), row),
                   pl.BlockSpec((D, D), fix), pl.BlockSpec((1, FOX_W), fix)],
        out_shape=[S((T, AUG_W), BF), S((T, AUG_W), BF), S((T, GMLP_W), F32), S((D, D), BF), S((1, FOX_W), F32)],
        scratch_shapes=[pltpu.VMEM((D, D), F32), pltpu.VMEM((tm, LANES), F32)],
        compiler_params=_cp(1))(dx, attn, yg, g_fo, wout, qf, lse, spread, pc_l, pc_d)


def _mix_prep_bwd(z, dq, dk, dv, dyg, rs, bf128, g_q, g_k, g_sgu, w_s, b_st, g_go):
    T = z.shape[0]
    tm = _tile(T, 512)
    n = T // tm

    def body(z_ref, dq_ref, dk_ref, dv_ref, dyg_ref, rs_ref, bf_ref, gq_ref, gk_ref, gs_ref, ws_ref,
             bst_ref, go_ref, dz_ref, dgq_ref, dgk_ref, dgs_ref, dgo_ref, dws_ref, dbst_ref, dbf_ref, carry_ref):
        i = pl.program_id(0)
        first = i == 0
        rs = rs_ref[...]

        @pl.when(first)
        def _():
            carry_ref[...] = jnp.zeros_like(carry_ref)

        lane = lax.broadcasted_iota(jnp.int32, (tm, LANES), 1)
        dc = jnp.zeros((tm, LANES), F32)
        gq_rows, gk_rows = [], []
        for h in range(FOX_HEADS):
            hp = slice(h * HB, h * HB + FOX_HD)
            dqh, gqr = _norm_bwd(dq_ref[:, hp] * 0.125, z_ref[:, Z_Q + h * FOX_HD:Z_Q + (h + 1) * FOX_HD], gq_ref[...],
                                 rs[:, RS_Q + h:RS_Q + h + 1])
            dkh, gkr = _norm_bwd(dk_ref[:, hp], z_ref[:, Z_K + h * FOX_HD:Z_K + (h + 1) * FOX_HD], gk_ref[...],
                                 rs[:, RS_K + h:RS_K + h + 1])
            dz_ref[:, Z_Q + h * FOX_HD:Z_Q + (h + 1) * FOX_HD] = dqh.astype(BF)
            dz_ref[:, Z_K + h * FOX_HD:Z_K + (h + 1) * FOX_HD] = dkh.astype(BF)
            dz_ref[:, Z_V + h * FOX_HD:Z_V + (h + 1) * FOX_HD] = dv_ref[:, hp].astype(BF)
            dch = dq_ref[:, h * HB + COL_A:h * HB + COL_A + 1] - dk_ref[:, h * HB + COL_B:h * HB + COL_B + 1]
            dc = jnp.where(lane == h, dch, dc)
            gq_rows.append(gqr)
            gk_rows.append(gkr)
        _acc_rows(dgq_ref, first, functools.reduce(lambda a, b: a + b, gq_rows))
        _acc_rows(dgk_ref, first, functools.reduce(lambda a, b: a + b, gk_rows))

        dlogf = _hi(_tri(tm, False).astype(F32), dc) + carry_ref[...]
        carry_ref[...] = dlogf[0:1, :]
        fl = z_ref[:, Z_F:Z_F + LANES] + bf_ref[...]
        lane = lax.broadcasted_iota(jnp.int32, (tm, LANES), 1)
        df = jnp.where(lane < FOX_HEADS, dlogf * jax.nn.sigmoid(-fl), 0.0)
        dz_ref[:, Z_F:Z_F + LANES] = df.astype(BF)
        _acc_rows(dbf_ref, first, df)

        u_pre = z_ref[:, Z_U:Z_U + GMLP_W]
        vg_pre = z_ref[:, Z_G:Z_G + GMLP_W]
        u = _gelu(u_pre)
        vg = _gelu(vg_pre)
        rv = rs[:, RS_V:RS_V + 1]
        vgn = (vg * rv * gs_ref[...]).astype(BF)
        bst = bst_ref[...]
        mixed, wms = _spatial_mix(vgn, ws_ref, bst, tm)
        sgu = u * mixed
        dsgu, gor = _norm_bwd(dyg_ref[...], sgu, go_ref[...], rs[:, RS_O:RS_O + 1])
        _acc_rows(dgo_ref, first, gor)
        du = dsgu * mixed
        dmixed = dsgu * u
        dmb = dmixed.astype(BF)
        tril = _tri(CHUNK, True)
        dvgn_rows = []
        dws = [None] * GMLP_G
        dbs = [None] * GMLP_G
        for c in range(tm // CHUNK):
            cs = slice(c * CHUNK, (c + 1) * CHUNK)
            cols = []
            for g in range(GMLP_G):
                gs = slice(g * GMLP_GD, (g + 1) * GMLP_GD)
                dmc = dmb[cs, gs]
                pw = _nt(dmc, vgn[cs, gs])
                pb = jnp.sum(dmixed[cs, gs], axis=1, keepdims=True)
                dws[g] = pw if dws[g] is None else dws[g] + pw
                dbs[g] = pb if dbs[g] is None else dbs[g] + pb
                cols.append(_tn(wms[g], dmc))
            dvgn_rows.append(jnp.concatenate(cols, axis=1))
        dvgn = jnp.concatenate(dvgn_rows, axis=0)
        dbs_t = jnp.concatenate(dbs, axis=1)
        for g in range(GMLP_G):
            dwg = jnp.where(tril, dws[g], 0.0)

            @pl.when(first)
            def _():
                dws_ref[g] = dwg

            @pl.when(jnp.logical_not(first))
            def _():
                dws_ref[g] += dwg

        @pl.when(first)
        def _():
            dbst_ref[...] = dbs_t

        @pl.when(jnp.logical_not(first))
        def _():
            dbst_ref[...] += dbs_t

        dvg, gsr = _norm_bwd(dvgn, vg, gs_ref[...], rv)
        _acc_rows(dgs_ref, first, gsr)
        dz_ref[:, Z_U:Z_U + GMLP_W] = (du * _gelu_grad(u_pre)).astype(BF)
        dz_ref[:, Z_G:Z_G + GMLP_W] = (dvg * _gelu_grad(vg_pre)).astype(BF)

    rev = lambda i: (n - 1 - i, 0)
    fix = lambda i: (0, 0)
    fix3 = lambda i: (0, 0, 0)
    return pl.pallas_call(
        body, name="mix_prep_bwd", grid=(n,),
        in_specs=[pl.BlockSpec((tm, ZW), rev), pl.BlockSpec((tm, AUG_W), rev), pl.BlockSpec((tm, AUG_W), rev),
                  pl.BlockSpec((tm, AUG_W), rev), pl.BlockSpec((tm, GMLP_W), rev), pl.BlockSpec((tm, LANES), rev),
                  pl.BlockSpec((1, LANES), fix), pl.BlockSpec((1, FOX_HD), fix), pl.BlockSpec((1, FOX_HD), fix),
                  pl.BlockSpec((1, GMLP_W), fix), pl.BlockSpec((GMLP_G, CHUNK, CHUNK), fix3),
                  pl.BlockSpec((CHUNK, GMLP_G), fix), pl.BlockSpec((1, GMLP_W), fix)],
        out_specs=[pl.BlockSpec((tm, ZW), rev), pl.BlockSpec((1, FOX_HD), fix), pl.BlockSpec((1, FOX_HD), fix),
                   pl.BlockSpec((1, GMLP_W), fix), pl.BlockSpec((1, GMLP_W), fix),
                   pl.BlockSpec((GMLP_G, CHUNK, CHUNK), fix3), pl.BlockSpec((CHUNK, GMLP_G), fix),
                   pl.BlockSpec((1, LANES), fix)],
        out_shape=[S((T, ZW), BF), S((1, FOX_HD), F32), S((1, FOX_HD), F32), S((1, GMLP_W), F32), S((1, GMLP_W), F32),
                   S((GMLP_G, CHUNK, CHUNK), F32), S((CHUNK, GMLP_G), F32), S((1, LANES), F32)],
        scratch_shapes=[pltpu.VMEM((1, LANES), F32)],
        compiler_params=_cp(1))(z, dq, dk, dv, dyg, rs, bf128, g_q, g_k, g_sgu, w_s, b_st, g_go)


def _mix_proj_bwd(dz, wz, x, g, dy):
    T, D = x.shape
    tm = _tile(T, 512)

    def body(dz_ref, w_ref, x_ref, g_ref, dy_ref, dx_ref, dxb_ref, dg_ref):
        dh = _nn(dz_ref[...], w_ref[...])
        dx, dgr = _norm_bwd(dh, x_ref[...], g_ref[...])
        dx = dx + dy_ref[...]
        dx_ref[...] = dx
        dxb_ref[...] = dx.astype(BF)
        _acc_rows(dg_ref, pl.program_id(0) == 0, dgr)

    row = lambda i: (i, 0)
    fix = lambda i: (0, 0)
    return pl.pallas_call(
        body, name="mix_proj_bwd", grid=(T // tm,),
        in_specs=[pl.BlockSpec((tm, ZW), row), pl.BlockSpec((ZW, D), fix), pl.BlockSpec((tm, D), row),
                  pl.BlockSpec((1, D), fix), pl.BlockSpec((tm, D), row)],
        out_specs=[pl.BlockSpec((tm, D), row), pl.BlockSpec((tm, D), row), pl.BlockSpec((1, D), fix)],
        out_shape=[S((T, D), F32), S((T, D), BF), S((1, D), F32)],
        compiler_params=_cp(1))(dz, wz, x, g, dy)


def _ca_kv(mem, g_mem, wckv, g_ck):
    M, D = mem.shape

    def body(m_ref, g_ref, w_ref, gk_ref, mn_ref, kr_ref, kn_ref, v_ref):
        mf = m_ref[...]
        mn = (mf * _rstd(mf) * g_ref[...]).astype(BF)
        mn_ref[...] = mn
        for h in range(CA_HEADS):
            kr = _nn(mn, w_ref[h])
            kr_ref[h] = kr
            kn_ref[h] = (kr * _rstd(kr) * gk_ref[...]).astype(BF)
            v_ref[h] = _nn(mn, w_ref[CA_HEADS + h]).astype(BF)

    hd = (CA_HEADS, M, CA_HD)
    return pl.pallas_call(
        body, name="ca_kv", out_shape=[S((M, D), BF), S(hd, F32), S(hd, BF), S(hd, BF)],
        compiler_params=pltpu.CompilerParams(vmem_limit_bytes=VMEM_LIMIT))(mem, g_mem, wckv, g_ck)


def _ca_tile_fwd(xt, gca, wcq, gcq, kn_ref, v_ref):
    hb = (xt * _rstd(xt) * gca).astype(BF)
    qc = _nn(hb, wcq)
    qr, qn, ps = [], [], []
    for h in range(CA_HEADS):
        qh = qc[:, h * CA_HD:(h + 1) * CA_HD]
        qnh = (qh * _rstd(qh) * gcq * 0.0625).astype(BF)
        s = _nt(qnh, kn_ref[h])
        e = jnp.exp(s - jnp.max(s, axis=1, keepdims=True))
        ps.append(e / jnp.sum(e, axis=1, keepdims=True))
        qr.append(qh)
        qn.append(qnh)
    return hb, qr, qn, ps


def _ca_fwd(x, g_ca, wcq, g_cq, kn, vv, wco):
    T, D = x.shape
    M = kn.shape[1]
    tm = _tile(T, 1024)

    def body(x_ref, gca_ref, wcq_ref, gcq_ref, kn_ref, v_ref, wco_ref, o_ref, ob_sc):
        xt = x_ref[...]
        _, _, _, ps = _ca_tile_fwd(xt, gca_ref[...], wcq_ref[...], gcq_ref[...], kn_ref, v_ref)
        for h in range(CA_HEADS):
            ob_sc[:, h * CA_HD:(h + 1) * CA_HD] = _nn(ps[h].astype(BF), v_ref[h]).astype(BF)
        o_ref[...] = xt + _nn(ob_sc[...], wco_ref[...])

    row = lambda i: (i, 0)
    fix = lambda i: (0, 0)
    fix3 = lambda i: (0, 0, 0)
    return pl.pallas_call(
        body, name="ca_fwd", grid=(T // tm,),
        in_specs=[pl.BlockSpec((tm, D), row), pl.BlockSpec((1, D), fix), pl.BlockSpec((D, D), fix),
                  pl.BlockSpec((1, CA_HD), fix), pl.BlockSpec((CA_HEADS, M, CA_HD), fix3),
                  pl.BlockSpec((CA_HEADS, M, CA_HD), fix3), pl.BlockSpec((D, D), fix)],
        out_specs=pl.BlockSpec((tm, D), row), out_shape=S((T, D), F32),
        scratch_shapes=[pltpu.VMEM((tm, D), BF)],
        compiler_params=_cp(1))(x, g_ca, wcq, g_cq, kn, vv, wco)


def _ca_bwd(x, dy, g_ca, wcq, g_cq, kn, vv, wco):
    T, D = x.shape
    M = kn.shape[1]
    tm = _tile(T, 512)
    n = T // tm

    def body(x_ref, dy_ref, gca_ref, wcq_ref, gcq_ref, kn_ref, v_ref, wco_ref,
             dx_ref, dwq_ref, dwo_ref, dkn_ref, dv_ref, dgcq_ref, dgca_ref, aq_sc, ao_sc, ob_sc, dq_sc):
        i = pl.program_id(0)
        first = i == 0
        xt = x_ref[...]
        dyt = dy_ref[...]
        dyb = dyt.astype(BF)
        hb, qr, qn, ps = _ca_tile_fwd(xt, gca_ref[...], wcq_ref[...], gcq_ref[...], kn_ref, v_ref)
        do = _nt(dyb, wco_ref[...])
        gcq_rows = None
        for h in range(CA_HEADS):
            hs = slice(h * CA_HD, (h + 1) * CA_HD)
            p = ps[h]
            pb = p.astype(BF)
            ob_sc[:, hs] = _nn(pb, v_ref[h]).astype(BF)
            doh = do[:, hs].astype(BF)
            dp = _nt(doh, v_ref[h])
            ds = (p * (dp - jnp.sum(dp * p, axis=1, keepdims=True))).astype(BF)
            dvh = _tn(pb, doh)
            dkh = _tn(ds, qn[h])

            @pl.when(first)
            def _():
                dv_ref[h] = dvh
                dkn_ref[h] = dkh

            @pl.when(jnp.logical_not(first))
            def _():
                dv_ref[h] += dvh
                dkn_ref[h] += dkh

            dqn = _nn(ds, kn_ref[h]) * 0.0625
            dqh, gr = _norm_bwd(dqn, qr[h], gcq_ref[...])
            gcq_rows = gr if gcq_rows is None else gcq_rows + gr
            dq_sc[:, hs] = dqh.astype(BF)
        _acc_rows(dgcq_ref, first, gcq_rows)
        dqb = dq_sc[...]
        p_o = _tn(ob_sc[...], dyb)
        p_q = _tn(hb, dqb)

        @pl.when(first)
        def _():
            ao_sc[...] = p_o
            aq_sc[...] = p_q

        @pl.when(jnp.logical_not(first))
        def _():
            ao_sc[...] += p_o
            aq_sc[...] += p_q

        @pl.when(i == n - 1)
        def _():
            dwo_ref[...] = ao_sc[...].astype(BF)
            dwq_ref[...] = aq_sc[...].astype(BF)

        dh = _nt(dqb, wcq_ref[...])
        dx, gar = _norm_bwd(dh, xt, gca_ref[...])
        dx_ref[...] = dx + dyt
        _acc_rows(dgca_ref, first, gar)

    row = lambda i: (i, 0)
    fix = lambda i: (0, 0)
    fix3 = lambda i: (0, 0, 0)
    hd = (CA_HEADS, M, CA_HD)
    return pl.pallas_call(
        body, name="ca_bwd", grid=(n,),
        in_specs=[pl.BlockSpec((tm, D), row), pl.BlockSpec((tm, D), row), pl.BlockSpec((1, D), fix),
                  pl.BlockSpec((D, D), fix), pl.BlockSpec((1, CA_HD), fix), pl.BlockSpec(hd, fix3),
                  pl.BlockSpec(hd, fix3), pl.BlockSpec((D, D), fix)],
        out_specs=[pl.BlockSpec((tm, D), row), pl.BlockSpec((D, D), fix), pl.BlockSpec((D, D), fix),
                   pl.BlockSpec(hd, fix3), pl.BlockSpec(hd, fix3), pl.BlockSpec((1, CA_HD), fix),
                   pl.BlockSpec((1, D), fix)],
        out_shape=[S((T, D), F32), S((D, D), BF), S((D, D), BF), S(hd, F32), S(hd, F32), S((1, CA_HD), F32),
                   S((1, D), F32)],
        scratch_shapes=[pltpu.VMEM((D, D), F32), pltpu.VMEM((D, D), F32), pltpu.VMEM((tm, D), BF),
                        pltpu.VMEM((tm, D), BF)],
        compiler_params=_cp(1))(x, dy, g_ca, wcq, g_cq, kn, vv, wco)


def _ca_kv_bwd(mem, g_mem, mn, kraw, dkn, dvv, wckv, g_ck):
    M, D = mem.shape

    def body(m_ref, g_ref, mn_ref, kr_ref, dkn_ref, dv_ref, w_ref, gk_ref, dw_ref, dgk_ref, dgm_ref):
        mn = mn_ref[...]
        dmn = jnp.zeros((M, D), F32)
        gk_rows = None
        for h in range(CA_HEADS):
            dkr, gr = _norm_bwd(dkn_ref[h], kr_ref[h], gk_ref[...])
            gk_rows = gr if gk_rows is None else gk_rows + gr
            dkb = dkr.astype(BF)
            dvb = dv_ref[h].astype(BF)
            dw_ref[h] = _tn(mn, dkb).astype(BF)
            dw_ref[CA_HEADS + h] = _tn(mn, dvb).astype(BF)
            dmn = dmn + _nt(dkb, w_ref[h]) + _nt(dvb, w_ref[CA_HEADS + h])
        dgk_ref[...] = jnp.sum(gk_rows, axis=0, keepdims=True)
        mf = m_ref[...]
        dgm_ref[...] = jnp.sum(dmn * (mf * _rstd(mf)), axis=0, keepdims=True)

    return pl.pallas_call(
        body, name="ca_kv_bwd",
        out_shape=[S((2 * CA_HEADS, D, CA_HD), BF), S((1, CA_HD), F32), S((1, D), F32)],
        compiler_params=pltpu.CompilerParams(vmem_limit_bytes=VMEM_LIMIT))(mem, g_mem, mn, kraw, dkn, dvv, wckv, g_ck)


def _after(g, token):
    return g if token is None else g + token[0:1, 0:1]


def _local_step(x, mem, target, small, weights, emit):
    T, D = x.shape
    p = small
    bf128 = jnp.pad(p["b_f"], ((0, 0), (0, LANES - FOX_HEADS)))
    b_st = p["b_s"].T

    wup1 = weights("ffn1_up", x)["wup1"]
    a1, h1 = _ffn_up("ffn1_up", x, p["g_ffn1"], wup1)
    wdn1 = weights("ffn1_dn", h1)["wdn1"]
    x1 = _ffn_down("ffn1_down", a1, wdn1, x)
    wm = weights("mix", x1)
    z, h2 = _mix_proj(x1, p["g_mix"], wm["wz"])
    qf, ka, va, yg, rs = _mix_prep(z, bf128, p["g_q"], p["g_k"], p["g_sgu"], p["w_s"], b_st, p["g_gmlp_o"])
    attn, lse = _fox_fwd(qf, ka, va)
    x2 = _mix_out(attn, yg, p["g_fox_o"], wm["wout"], x1)
    wc = weights("ca", x2)
    mn, kraw, ckn, cvv = _ca_kv(mem, p["g_mem"], wc["wckv"], p["g_ck"])
    x3 = _ca_fwd(x2, p["g_ca"], wc["wcq"], p["g_cq"], ckn, cvv, wc["wco"])
    w2 = weights("ffn2", x3)
    a2, h4 = _ffn_up("ffn2_up", x3, p["g_ffn2"], w2["wup2"])
    dy4, dy4b, sq = _ffn_down_loss("ffn2_down", a2, w2["wdn2"], x3, target)

    gs = {}
    dgu2 = _ffn_bwd_act("ffn2_bwd_act", dy4b, h4, w2["wup2"], w2["wdn2"])
    tok = emit("ffn2", {"wup2": _ffn_dwup("ffn2", h4, dgu2), "wdn2": _ffn_dwdn("ffn2", a2, dy4b)})
    dx3, gs["g_ffn2"] = _ffn_dx("ffn2_dx", dgu2, w2["wup2"], x3, _after(p["g_ffn2"], tok), dy4)

    dx2, dwcq, dwco, dckn, dcvv, gs["g_cq"], gs["g_ca"] = _ca_bwd(
        x2, dx3, p["g_ca"], wc["wcq"], p["g_cq"], ckn, cvv, wc["wco"])
    dwckv, gs["g_ck"], gs["g_mem"] = _ca_kv_bwd(mem, p["g_mem"], mn, kraw, dckn, dcvv, wc["wckv"], p["g_ck"])

    qb, dob, dyg, dwout, gs["g_fox_o"] = _mix_out_bwd(dx2, attn, yg, p["g_fox_o"], wm["wout"], qf, lse)
    dq, dk, dv = _fox_bwd(qb, ka, va, dob)
    dz, gs["g_q"], gs["g_k"], gs["g_sgu"], gs["g_gmlp_o"], gs["w_s"], dbst, dbf = _mix_prep_bwd(
        z, dq, dk, dv, dyg, rs, bf128, p["g_q"], p["g_k"], p["g_sgu"], p["w_s"], b_st, p["g_gmlp_o"])
    gs["b_s"] = dbst.T
    gs["b_f"] = dbf[:, :FOX_HEADS]
    tok_ws = emit("w_s", {"w_s": gs["w_s"]})
    zb = ZW // 3
    dwz = _tn_matmul("mix_dwz", dz, pl.BlockSpec((T, zb), lambda j: (0, j)), h2,
                     S((ZW, D), BF), pl.BlockSpec((zb, D), lambda j: (j, 0)), 3)
    tok = emit("mid", {"wcq": dwcq, "wco": dwco, "wckv": dwckv, "wout": dwout, "wz": dwz})
    dx1, dx1b, gs["g_mix"] = _mix_proj_bwd(dz, wm["wz"], x1, _after(_after(p["g_mix"], tok), tok_ws), dx2)

    dgu1 = _ffn_bwd_act("ffn1_bwd_act", dx1b, h1, wup1, wdn1)
    tok = emit("ffn1_dn", {"wdn1": _ffn_dwdn("ffn1", a1, dx1b)})
    tok = emit("ffn1_up", {"wup1": _ffn_dwup("ffn1", h1, dgu1, after=tok)})
    dx0, gs["g_ffn1"] = _ffn_dx("ffn1_dx", dgu1, wup1, x, _after(p["g_ffn1"], tok), dx1)
    return sq, dx0, gs


MESH = pl.DeviceIdType.MESH
HBM_SPEC = pl.BlockSpec(memory_space=pltpu.HBM)
N_PEER = N_DEV - 1


def _place():
    return lax.axis_index("x"), lax.axis_index("y"), lax.axis_index("c")


def _slot(px, py, pc):
    return 4 * px + 2 * py + pc


SEM_SPEC = pl.BlockSpec(memory_space=pltpu.SEMAPHORE)
ANY_SPEC = pl.BlockSpec(memory_space=pl.ANY)
DATAFLOW = pltpu.SideEffectType.DATAFLOW_SIDE_EFFECTING


def _hbm(a):
    return pltpu.with_memory_space_constraint(a, pltpu.HBM)


def _peer(x, y, c, r):
    return (1 - x if r & 4 else x, 1 - y if r & 2 else y, 1 - c if r & 1 else c)


def _place_own(srcs, whole):
    my = _slot(*_place())
    lands = []
    for s in srcs:
        blk = s[None] if whole else lax.dynamic_slice_in_dim(s, my, 1, 0)
        shape = (N_DEV,) + s.shape if whole else s.shape
        lands.append(lax.dynamic_update_slice_in_dim(lax.empty(shape, s.dtype), blk, my, 0))
    return lands


ALL_PEERS = tuple(range(1, N_DEV))
NEAR_PEERS = (1, 2, 4, 6)
SAME_CORE = (2, 4, 6)


def _copy_start(name, srcs, lands, whole, peers=None):
    n = len(srcs)
    peers = peers or [ALL_PEERS] * n

    def body(*refs):
        src, land = refs[:n], refs[n:2 * n]
        send, recv = refs[2 * n:3 * n], refs[3 * n:4 * n]
        token = refs[6 * n]
        x, y, c = _place()
        my = _slot(x, y, c)
        for a in range(n):
            for r in peers[a]:
                p = _peer(x, y, c, r)
                pltpu.make_async_remote_copy(
                    src_ref=src[a] if whole else src[a].at[_slot(*p)], dst_ref=land[a].at[my],
                    send_sem=send[a].at[r - 1], recv_sem=recv[a].at[r - 1], device_id=p, device_id_type=MESH).start()
        token[...] = jnp.zeros_like(token)

    out = pl.pallas_call(
        body, name=name,
        out_shape=([pltpu.SemaphoreType.DMA((N_PEER,))] * (2 * n)
                   + [pltpu.HBM(s.shape, s.dtype) for s in srcs] + [pltpu.HBM(s.shape, s.dtype) for s in lands]
                   + [S((8, LANES), F32)]),
        in_specs=[HBM_SPEC] * (2 * n),
        out_specs=[SEM_SPEC] * (2 * n) + [HBM_SPEC] * (2 * n) + [pl.BlockSpec(memory_space=pltpu.VMEM)],
        input_output_aliases={i: 2 * n + i for i in range(2 * n)},
        compiler_params=pltpu.CompilerParams(has_side_effects=DATAFLOW),
    )(*[_hbm(s) for s in srcs], *[_hbm(s) for s in lands])
    return out[:n], out[n:2 * n], out[2 * n:3 * n], out[3 * n:4 * n], out[4 * n]


def _copy_wait(name, srcs, lands, send, recv, after, whole, peers=None, with_srcs=False):
    n = len(srcs)
    peers = peers or [ALL_PEERS] * n

    def body(*refs):
        src, land = refs[:n], refs[n:2 * n]
        snd, rcv = refs[2 * n:3 * n], refs[3 * n:4 * n]
        x, y, c = _place()
        for a in range(n):
            for r in peers[a]:
                p = _peer(x, y, c, r)
                ps = _slot(*p)
                cp = pltpu.make_async_remote_copy(
                    src_ref=src[a] if whole else src[a].at[ps], dst_ref=land[a].at[ps],
                    send_sem=snd[a].at[r - 1], recv_sem=rcv[a].at[r - 1], device_id=p, device_id_type=MESH)
                cp.wait_send()
                cp.wait_recv()

    out = pl.pallas_call(
        body, name=name,
        out_shape=[pltpu.HBM(s.shape, s.dtype) for s in srcs] + [pltpu.HBM(s.shape, s.dtype) for s in lands],
        in_specs=[HBM_SPEC] * (2 * n) + [SEM_SPEC] * (2 * n) + [ANY_SPEC],
        out_specs=[HBM_SPEC] * (2 * n),
        input_output_aliases={i: i for i in range(2 * n)},
        compiler_params=pltpu.CompilerParams(has_side_effects=DATAFLOW),
    )(*srcs, *lands, *send, *recv, after)
    return (out[:n], out[n:]) if with_srcs else out[n:]


def _forward_start(name, lands):
    n = len(lands)

    def body(*refs):
        land = refs[:n]
        send, recv = refs[n:2 * n], refs[2 * n:3 * n]
        token = refs[4 * n]
        x, y, c = _place()
        for a in range(n):
            for r in SAME_CORE:
                blk = land[a].at[_slot(*_peer(x, y, c, r))]
                pltpu.make_async_remote_copy(
                    src_ref=blk, dst_ref=blk, send_sem=send[a].at[r - 1], recv_sem=recv[a].at[r - 1],
                    device_id=(x, y, 1 - c), device_id_type=MESH).start()
        token[...] = jnp.zeros_like(token)

    out = pl.pallas_call(
        body, name=name,
        out_shape=([pltpu.SemaphoreType.DMA((N_PEER,))] * (2 * n) + [pltpu.HBM(s.shape, s.dtype) for s in lands]
                   + [S((8, LANES), F32)]),
        in_specs=[HBM_SPEC] * n,
        out_specs=[SEM_SPEC] * (2 * n) + [HBM_SPEC] * n + [pl.BlockSpec(memory_space=pltpu.VMEM)],
        input_output_aliases={i: 2 * n + i for i in range(n)},
        compiler_params=pltpu.CompilerParams(has_side_effects=DATAFLOW),
    )(*[_hbm(s) for s in lands])
    return out[:n], out[n:2 * n], out[2 * n:3 * n], out[3 * n]


def _forward_wait(name, lands, send, recv, after):
    n = len(lands)

    def body(*refs):
        land = refs[:n]
        snd, rcv = refs[n:2 * n], refs[2 * n:3 * n]
        x, y, c = _place()
        for a in range(n):
            for r in SAME_CORE:
                cp = pltpu.make_async_remote_copy(
                    src_ref=land[a].at[_slot(*_peer(x, y, c, r))], dst_ref=land[a].at[_slot(*_peer(x, y, c, r | 1))],
                    send_sem=snd[a].at[r - 1], recv_sem=rcv[a].at[r - 1], device_id=(x, y, 1 - c),
                    device_id_type=MESH)
                cp.wait_send()
                cp.wait_recv()

    return pl.pallas_call(
        body, name=name,
        out_shape=[pltpu.HBM(s.shape, s.dtype) for s in lands],
        in_specs=[HBM_SPEC] * n + [SEM_SPEC] * (2 * n) + [ANY_SPEC],
        out_specs=[HBM_SPEC] * n,
        input_output_aliases={i: i for i in range(n)},
        compiler_params=pltpu.CompilerParams(has_side_effects=DATAFLOW),
    )(*lands, *send, *recv, after)


def _adamw(w, g, m, v):
    m2 = ADAM_B1 * m + (1.0 - ADAM_B1) * g
    v2 = ADAM_B2 * v + (1.0 - ADAM_B2) * (g * g)
    m_hat = m2 / (1.0 - ADAM_B1 ** ADAM_STEP)
    v_hat = v2 / (1.0 - ADAM_B2 ** ADAM_STEP)
    delta = -ADAM_LR * (m_hat / (jnp.sqrt(v_hat) + ADAM_EPS) + ADAM_WD * w)
    return delta, m2, v2


def _adamw_big(name, slots, w, m, v, own=None):
    R, C = w.shape
    tr = next((t for t in (256, 352) if R % t == 0), R)

    def finish(g, w_ref, m_ref, v_ref, g_ref, d_ref, m2_ref, v2_ref):
        d, m2, v2 = _adamw(w_ref[...], g, m_ref[...], v_ref[...])
        g_ref[...] = g
        d_ref[...] = d
        m2_ref[...] = m2
        v2_ref[...] = v2

    if own is None:
        def body(s_ref, *refs):
            g = s_ref[0].astype(F32)
            for k in range(1, N_DEV):
                g = g + s_ref[k].astype(F32)
            finish(g, *refs)

        row = pl.BlockSpec((tr, C), lambda i: (i, 0))
        return pl.pallas_call(
            body, name=name, grid=(R // tr,),
            in_specs=[pl.BlockSpec((N_DEV, tr, C), lambda i: (0, i, 0)), row, row, row],
            out_specs=[row] * 4, out_shape=[S((R, C), F32)] * 4,
            compiler_params=_cp(1))(slots, w, m, v)

    def body(my_ref, s_ref, own_ref, *refs):
        mine = own_ref[...]
        g = None
        for k in range(N_DEV):
            part = jnp.where(my_ref[0] == k, mine, s_ref[k]).astype(F32)
            g = part if g is None else g + part
        finish(g, *refs)

    row = pl.BlockSpec((tr, C), lambda i, my_ref: (i, 0))
    my = jnp.reshape(_slot(*_place()), (1,)).astype(jnp.int32)
    return pl.pallas_call(
        body, name=name,
        grid_spec=pltpu.PrefetchScalarGridSpec(
            num_scalar_prefetch=1, grid=(R // tr,),
            in_specs=[pl.BlockSpec((N_DEV, tr, C), lambda i, my_ref: (0, i, 0)),
                      pl.BlockSpec((None, tr, C), lambda i, my_ref: (my_ref[0], i, 0)), row, row, row],
            out_specs=[row] * 4),
        out_shape=[S((R, C), F32)] * 4, compiler_params=_cp(1))(my, slots, own, w, m, v)


TINY_ROWS = (("b_s", 8), ("g_ffn1", 8), ("g_mix", 8), ("g_ca", 8), ("g_mem", 8), ("g_ffn2", 8), ("g_sgu", 4),
             ("g_fox_o", 4), ("g_gmlp_o", 4), ("g_cq", 2), ("g_ck", 2), ("g_q", 1), ("g_k", 1), ("b_f", 1),
             ("loss", 1))
TINY_P = 72


def _tiny_pieces(width):
    return [(j, slice(j * LANES, min((j + 1) * LANES, width))) for j in range(-(-width // LANES))]


def _pack_tiny(grads, sq):
    names = [n for n, _ in TINY_ROWS if n != "loss"]

    def body(*refs):
        ins, sq_ref, o_ref = refs[:len(names)], refs[len(names)], refs[len(names) + 1]
        o_ref[...] = jnp.zeros_like(o_ref)
        at = 0
        for ref, (name, r) in zip(ins, TINY_ROWS):
            if name == "b_s":
                o_ref[at:at + r, :] = ref[...]
            else:
                for j, cols in _tiny_pieces(ref.shape[1]):
                    o_ref[at + j:at + j + 1, 0:cols.stop - cols.start] = ref[:, cols]
            at += r
        o_ref[at:at + 1, :] = sq_ref[0:1, :]

    return pl.pallas_call(body, name="tiny_pack", out_shape=S((TINY_P, LANES), F32))(
        *[grads[n] for n in names], sq)


def _adamw_tiny(slots, w, m, v):
    names = [n for n, _ in TINY_ROWS if n != "loss"]
    k = len(names)

    def body(s_ref, *refs):
        ins, outs, loss_ref = refs[:3 * k], refs[3 * k:7 * k], refs[7 * k]
        g_all = s_ref[0]
        for d in range(1, N_DEV):
            g_all = g_all + s_ref[d]
        at = 0
        for i, (name, r) in enumerate(TINY_ROWS[:k]):
            w_ref, m_ref, v_ref = ins[i], ins[k + i], ins[2 * k + i]
            o = outs[4 * i:4 * i + 4]
            if name == "b_s":
                pieces = [(slice(at, at + r), slice(0, LANES), (slice(None), slice(None)))]
            else:
                pieces = [(slice(at + j, at + j + 1), slice(0, c.stop - c.start), (slice(None), c))
                          for j, c in _tiny_pieces(w_ref.shape[1])]
            for rows, lanes, dst in pieces:
                g = g_all[rows, lanes]
                res = (g,) + _adamw(w_ref[dst], g, m_ref[dst], v_ref[dst])
                for ref, val in zip(o, res):
                    ref[dst] = val
            at += r
        loss_ref[...] = g_all[at:at + 1, :]

    shapes = [S(w[n].shape, F32) for n in names]
    out = pl.pallas_call(
        body, name="adamw_tiny", out_shape=[s for s in shapes for _ in range(4)] + [S((1, LANES), F32)],
    )(slots, *[w[n] for n in names], *[m[n] for n in names], *[v[n] for n in names])
    stores = ({}, {}, {}, {})
    for i, n in enumerate(names):
        for store, t in zip(stores, out[4 * i:4 * i + 4]):
            store[n] = t
    return stores, out[4 * k]


WEIGHTS =('g_ffn1', 'w_ffn1_in', 'w_ffn1_out', 'g_mix', 'w_in', 'b_f', 'g_q', 'g_k', 'g_sgu', 'w_s', 'b_s',
           'g_fox_o', 'g_gmlp_o', 'w_out', 'g_ca', 'g_mem', 'w_cq', 'w_ckv', 'g_cq', 'g_ck', 'w_co', 'g_ffn2',
           'w_ffn2_in', 'w_ffn2_out')
BIG = ('w_ffn1_in', 'w_ffn1_out', 'w_in', 'w_out', 'w_cq', 'w_ckv', 'w_co', 'w_ffn2_in', 'w_ffn2_out')
TRANSPOSED = ('w_ffn1_in', 'w_in', 'w_ffn2_in')
TWO_LEVEL = ('w_ffn1_in', 'w_in')
GATHER_GROUPS = {"ffn1_up": ("w_ffn1_in",), "ffn1_dn": ("w_ffn1_out",), "mix": ("w_in", "w_out"),
                 "ca": ("w_cq", "w_ckv", "w_co"), "ffn2": ("w_ffn2_in", "w_ffn2_out")}
QKV_W = 3 * FOX_W
UV_OFF = QKV_W + FOX_HEADS


def kernel(x, mem, g_ffn1, w_ffn1_in, w_ffn1_out, g_mix, w_in, b_f, g_q, g_k, g_sgu, w_s, b_s, g_fox_o, g_gmlp_o, w_out, g_ca, g_mem, w_cq, w_ckv, g_cq, g_ck, w_co, g_ffn2, w_ffn2_in, w_ffn2_out, loss_target, m_g_ffn1, m_w_ffn1_in, m_w_ffn1_out, m_g_mix, m_w_in, m_b_f, m_g_q, m_g_k, m_g_sgu, m_w_s, m_b_s, m_g_fox_o, m_g_gmlp_o, m_w_out, m_g_ca, m_g_mem, m_w_cq, m_w_ckv, m_g_cq, m_g_ck, m_w_co, m_g_ffn2, m_w_ffn2_in, m_w_ffn2_out, v_g_ffn1, v_w_ffn1_in, v_w_ffn1_out, v_g_mix, v_w_in, v_b_f, v_g_q, v_g_k, v_g_sgu, v_w_s, v_b_s, v_g_fox_o, v_g_gmlp_o, v_w_out, v_g_ca, v_g_mem, v_w_cq, v_w_ckv, v_g_cq, v_g_ck, v_w_co, v_g_ffn2, v_w_ffn2_in, v_w_ffn2_out):
    args = dict(locals())
    w = {n: args[n] for n in WEIGHTS}
    mo = {n: args["m_" + n] for n in WEIGHTS}
    vo = {n: args["v_" + n] for n in WEIGHTS}
    D = D_MODEL

    def local(n, a):
        return a[0].T if n in TRANSPOSED else a[0]

    g_peers = [NEAR_PEERS if n in TWO_LEVEL else ALL_PEERS for n in BIG]
    handles = {}

    def start_gather(name, names, arrays):
        snd, rcv, src, land, token = _copy_start(name, arrays, _place_own(arrays, True), True,
                                                 peers=[g_peers[BIG.index(n)] for n in names])
        handles.update({n: (src[i], land[i], snd[i], rcv[i]) for i, n in enumerate(names)})
        return token

    first = local(BIG[0], w[BIG[0]]).astype(BF)
    fb = first.shape[0]
    token_first = start_gather("gather_start_first", BIG[:1], [first])
    token_rest = start_gather("gather_start_rest", BIG[1:],
                              [(local(n, w[n]) + token_first[0:1, 0:1]).astype(BF) for n in BIG[1:]])

    tiny_names = [n for n, _ in TINY_ROWS if n != "loss"]

    def weights(group, after):
        names = GATHER_GROUPS[group]
        hs = [handles[n] for n in names]
        got = list(_copy_wait("gather_wait_" + group, [h[0] for h in hs], [h[1] for h in hs], [h[2] for h in hs],
                              [h[3] for h in hs], token_rest if group == "ffn1_up" else after, True,
                              peers=[g_peers[BIG.index(n)] for n in names]))
        passed = [i for i, n in enumerate(names) if n in TWO_LEVEL]
        if passed:
            f_snd, f_rcv, f_land, f_token = _forward_start("gather_pass_start_" + group, [got[i] for i in passed])
            for i, t in zip(passed, _forward_wait("gather_pass_wait_" + group, f_land, f_snd, f_rcv, f_token)):
                got[i] = t
        got = dict(zip(names, got))
        if group == "ffn1_up":
            return {"wup1": got["w_ffn1_in"].reshape(2, N_FFN_BLK, fb, D)}
        if group == "ffn1_dn":
            return {"wdn1": got["w_ffn1_out"].reshape(N_FFN_BLK, fb, D)}
        if group == "mix":
            full = got["w_in"].reshape(-1, D)
            wz = jnp.concatenate([full[:QKV_W], full[UV_OFF:], full[QKV_W:UV_OFF],
                                  jnp.zeros((LANES - FOX_HEADS, D), BF)], axis=0)
            return {"wz": wz, "wout": got["w_out"].reshape(D, D)}
        if group == "ca":
            return {"wcq": got["w_cq"].reshape(D, D), "wco": got["w_co"].reshape(D, D), "wckv": got["w_ckv"]}
        return {"wup2": got["w_ffn2_in"].reshape(2, N_FFN_BLK, fb, D),
                "wdn2": got["w_ffn2_out"].reshape(N_FFN_BLK, fb, D)}

    flying = {}

    def emit(group, g):
        if group == "w_s":
            part = [g["w_s"].reshape(-1, LANES)]
            *copies, token = _copy_start("w_s_start", part, _place_own(part, True), True)
            flying[group] = copies
            return token
        if group == "ffn2":
            parts = {"w_ffn2_in": g["wup2"], "w_ffn2_out": g["wdn2"].reshape(N_DEV, -1, D)}
        elif group == "ffn1_dn":
            parts = {"w_ffn1_out": g["wdn1"].reshape(N_DEV, -1, D)}
        elif group == "ffn1_up":
            parts = {"w_ffn1_in": g["wup1"]}
        else:
            gz = g["wz"]
            g_in = jnp.concatenate([gz[:QKV_W], gz[Z_F:Z_F + FOX_HEADS], gz[QKV_W:Z_F]], axis=0)
            parts = {"w_in": g_in.reshape(N_DEV, -1, D).astype(BF),
                     "w_out": g["wout"].reshape(N_DEV, -1, D), "w_cq": g["wcq"].reshape(N_DEV, -1, D),
                     "w_co": g["wco"].reshape(N_DEV, -1, D), "w_ckv": g["wckv"]}
        names = list(parts)
        srcs = [parts[n] for n in names]
        *copies, token = _copy_start("exchange_start_" + group, srcs, [lax.empty(s.shape, s.dtype) for s in srcs],
                                     False)
        flying[group] = (names, copies)
        return token

    small = {n: (w[n][0] if n == "b_s" else w[n]) for n in tiny_names}
    small["w_s"] = w["w_s"][0]

    sq, dx0, gs = _local_step(x[0], mem[0], loss_target[0], small, weights, emit)

    sm_parts = [_pack_tiny(gs, sq)]
    sm_snd, sm_rcv, sm_src, sm_land, sm_token = _copy_start("tiny_start", sm_parts, _place_own(sm_parts, True), True)

    grad, delta, new_m, new_v = {}, {}, {}, {}

    def update(group, after):
        names, (snd, rcv, srcs, lands) = flying[group]
        owns, slots = _copy_wait("exchange_wait_" + group, srcs, lands, snd, rcv, after, False, with_srcs=True)
        for n, sl, own in zip(names, slots, owns):
            g, d, m2, v2 = _adamw_big("adamw_" + n, sl, local(n, w[n]), local(n, mo[n]), local(n, vo[n]), own=own)
            grad[n], delta[n], new_m[n], new_v[n] = (
                (t.T if n in TRANSPOSED else t).reshape(w[n].shape) for t in (g, d, m2, v2))
        return d

    last = update("ffn2", sm_token)
    last = update("mid", last)
    last = update("ffn1_dn", last)
    last = update("ffn1_up", last)
    ws_snd, ws_rcv, ws_src, ws_land = flying["w_s"]
    ws_all, = _copy_wait("w_s_wait", ws_src, ws_land, ws_snd, ws_rcv, last, True)
    tiny_all, = _copy_wait("tiny_wait", sm_src, sm_land, sm_snd, sm_rcv, ws_all, True)
    ws_shape = w["w_s"].shape
    for store, t in zip((grad, delta, new_m, new_v), _adamw_big(
            "adamw_w_s", ws_all, *[a["w_s"].reshape(-1, LANES) for a in (w, mo, vo)])):
        store["w_s"] = t.reshape(ws_shape)
    stores, loss_row = _adamw_tiny(tiny_all, *[{n: (a[n][0] if n == "b_s" else a[n]) for n in tiny_names}
                                               for a in (w, mo, vo)])
    for store, t in zip((grad, delta, new_m, new_v), stores):
        store.update({n: v.reshape(w[n].shape) for n, v in t.items()})
    loss = loss_row[0, 0] * (0.5 / D)

    return (loss, dx0[None], *[grad[n] for n in WEIGHTS], *[delta[n] for n in WEIGHTS],
            *[new_m[n] for n in WEIGHTS], *[new_v[n] for n in WEIGHTS])
```

```python
import functools

import jax
import jax.numpy as jnp
from jax import lax
from jax.experimental import pallas as pl
from jax.experimental.pallas import tpu as pltpu

F32 = jnp.float32
BF = jnp.bfloat16
S = jax.ShapeDtypeStruct

N_DEV = 8
D_MODEL = 1024
FOX_HEADS, FOX_HD = 8, 64
FOX_W = 512
GMLP_G, GMLP_GD = 8, 64
GMLP_W = 512
CHUNK = 128
CA_HEADS, CA_HD = 4, 256
N_FFN_BLK = 4
ZW = 2688
Z_Q, Z_K, Z_V, Z_U, Z_G, Z_F = 0, 512, 1024, 1536, 2048, 2560
EPS = 1e-6
NEG = -1e30
LANES = 128

ADAM_LR, ADAM_B1, ADAM_B2, ADAM_EPS, ADAM_WD, ADAM_STEP = 0.001, 0.9, 0.999, 1e-08, 0.01, 10

VMEM_LIMIT = 52 * 2 ** 20


def _cp(n_axes):
    return pltpu.CompilerParams(dimension_semantics=("arbitrary",) * n_axes, vmem_limit_bytes=VMEM_LIMIT)


def _nn(a, b):
    return jnp.dot(a, b, preferred_element_type=F32)


def _nt(a, b):
    return lax.dot_general(a, b, (((1,), (1,)), ((), ())), preferred_element_type=F32)


def _tn(a, b):
    return lax.dot_general(a, b, (((0,), (0,)), ((), ())), preferred_element_type=F32)


def _hi(a, b):
    return jnp.dot(a, b, precision=lax.Precision.HIGHEST, preferred_element_type=F32)


def _rstd(x):
    return lax.rsqrt(jnp.mean(x * x, axis=-1, keepdims=True) + EPS)


def _norm_bwd(dy, x, g, r=None):
    r = _rstd(x) if r is None else r
    xh = x * r
    dxh = dy * g
    dx = r * (dxh - xh * jnp.mean(dxh * xh, axis=-1, keepdims=True))
    return dx, dy * xh


def _acc_rows(ref, first, val):
    srow = jnp.sum(val, axis=0, keepdims=True)

    @pl.when(first)
    def _():
        ref[...] = srow

    @pl.when(jnp.logical_not(first))
    def _():
        ref[...] += srow


def _gelu(x):
    c = 0.7978845608028654
    return 0.5 * x * (1.0 + jnp.tanh(c * (x + 0.044715 * x * x * x)))


def _gelu_grad(x):
    c = 0.7978845608028654
    t = jnp.tanh(c * (x + 0.044715 * x * x * x))
    return 0.5 * (1.0 + t) + 0.5 * x * (1.0 - t * t) * c * (1.0 + 3 * 0.044715 * x * x)


def _tile(n, pref):
    return pref if n % pref == 0 else n


def _ffn_up(name, x, g, wup):
    T, D = x.shape
    FB = wup.shape[-2]
    tm = _tile(T, 1024)

    def body(x_ref, g_ref, w_ref, a_ref, h_ref):
        @pl.when(pl.program_id(1) == 0)
        def _():
            xf = x_ref[...]
            h_ref[...] = (xf * _rstd(xf) * g_ref[...]).astype(BF)

        hb = h_ref[...]
        gg = _nt(hb, w_ref[0])
        uu = _nt(hb, w_ref[1])
        a_ref[...] = (gg * jax.nn.sigmoid(gg) * uu).astype(BF)

    return pl.pallas_call(
        body, name=name, grid=(T // tm, N_FFN_BLK),
        in_specs=[pl.BlockSpec((tm, D), lambda i, j: (i, 0)),
                  pl.BlockSpec((1, D), lambda i, j: (0, 0)),
                  pl.BlockSpec((2, None, FB, D), lambda i, j: (0, j, 0, 0))],
        out_specs=[pl.BlockSpec((None, tm, FB), lambda i, j: (j, i, 0)),
                   pl.BlockSpec((tm, D), lambda i, j: (i, 0))],
        out_shape=[S((N_FFN_BLK, T, FB), BF), S((T, D), BF)],
        compiler_params=_cp(2))(x, g, wup)


def _ffn_down(name, a, wdn, x):
    _, T, FB = a.shape
    D = x.shape[1]
    tm = _tile(T, 512)

    def body(a_ref, w_ref, x_ref, o_ref):
        p = _nn(a_ref[0], w_ref[0])
        for j in range(1, N_FFN_BLK):
            p = p + _nn(a_ref[j], w_ref[j])
        o_ref[...] = x_ref[...] + 0.5 * p

    return pl.pallas_call(
        body, name=name, grid=(T // tm,),
        in_specs=[pl.BlockSpec((N_FFN_BLK, tm, FB), lambda i: (0, i, 0)),
                  pl.BlockSpec((N_FFN_BLK, FB, D), lambda i: (0, 0, 0)),
                  pl.BlockSpec((tm, D), lambda i: (i, 0))],
        out_specs=pl.BlockSpec((tm, D), lambda i: (i, 0)),
        out_shape=S((T, D), F32),
        compiler_params=_cp(1))(a, wdn, x)


def _ffn_down_loss(name, a, wdn, x, target):
    _, T, FB = a.shape
    D = x.shape[1]
    tm = _tile(T, 512)

    def body(a_ref, w_ref, x_ref, t_ref, d_ref, db_ref, loss_ref):
        i = pl.program_id(0)
        p = _nn(a_ref[0], w_ref[0])
        for j in range(1, N_FFN_BLK):
            p = p + _nn(a_ref[j], w_ref[j])
        diff = (x_ref[...] + 0.5 * p) - t_ref[...]
        dy = diff * (1.0 / D)
        d_ref[...] = dy
        db_ref[...] = dy.astype(BF)
        sq = jnp.zeros((8, LANES), F32) + jnp.sum(diff * diff)

        @pl.when(i == 0)
        def _():
            loss_ref[...] = sq

        @pl.when(i > 0)
        def _():
            loss_ref[...] += sq

    row = pl.BlockSpec((tm, D), lambda i: (i, 0))
    return pl.pallas_call(
        body, name=name, grid=(T // tm,),
        in_specs=[pl.BlockSpec((N_FFN_BLK, tm, FB), lambda i: (0, i, 0)),
                  pl.BlockSpec((N_FFN_BLK, FB, D), lambda i: (0, 0, 0)), row, row],
        out_specs=[row, row, pl.BlockSpec((8, LANES), lambda i: (0, 0))],
        out_shape=[S((T, D), F32), S((T, D), BF), S((8, LANES), F32)],
        compiler_params=_cp(1))(a, wdn, x, target)


def _ffn_bwd_act(name, dyb, h, wup, wdn):
    T, D = h.shape
    FB = wup.shape[-2]
    tm = _tile(T, 1024)

    def body(d_ref, h_ref, wu_ref, wd_ref, o_ref):
        da = 0.5 * _nt(d_ref[...], wd_ref[...])
        hb = h_ref[...]
        gg = _nt(hb, wu_ref[0])
        uu = _nt(hb, wu_ref[1])
        sg = jax.nn.sigmoid(gg)
        o_ref[0] = (da * uu * (sg * (1.0 + gg * (1.0 - sg)))).astype(BF)
        o_ref[1] = (da * (gg * sg)).astype(BF)

    return pl.pallas_call(
        body, name=name, grid=(T // tm, N_FFN_BLK),
        in_specs=[pl.BlockSpec((tm, D), lambda i, j: (i, 0)),
                  pl.BlockSpec((tm, D), lambda i, j: (i, 0)),
                  pl.BlockSpec((2, None, FB, D), lambda i, j: (0, j, 0, 0)),
                  pl.BlockSpec((None, FB, D), lambda i, j: (j, 0, 0))],
        out_specs=pl.BlockSpec((2, None, tm, FB), lambda i, j: (0, j, i, 0)),
        out_shape=S((2, N_FFN_BLK, T, FB), BF),
        compiler_params=_cp(2))(dyb, h, wup, wdn)


def _ffn_dx(name, dgu, wup, x, g, dy):
    T, D = x.shape
    FB = wup.shape[-2]
    tm = _tile(T, 512)

    def body(d_ref, w_ref, x_ref, g_ref, dy_ref, dx_ref, dg_ref):
        p = None
        for j in range(N_FFN_BLK):
            for half in range(2):
                t = _nn(d_ref[half, j], w_ref[half, j])
                p = t if p is None else p + t
        dx, dgr = _norm_bwd(p, x_ref[...], g_ref[...])
        dx_ref[...] = dx + dy_ref[...]
        _acc_rows(dg_ref, pl.program_id(0) == 0, dgr)

    return pl.pallas_call(
        body, name=name, grid=(T // tm,),
        in_specs=[pl.BlockSpec((2, N_FFN_BLK, tm, FB), lambda i: (0, 0, i, 0)),
                  pl.BlockSpec((2, N_FFN_BLK, FB, D), lambda i: (0, 0, 0, 0), pipeline_mode=pl.Buffered(1)),
                  pl.BlockSpec((tm, D), lambda i: (i, 0)),
                  pl.BlockSpec((1, D), lambda i: (0, 0)),
                  pl.BlockSpec((tm, D), lambda i: (i, 0))],
        out_specs=[pl.BlockSpec((tm, D), lambda i: (i, 0)),
                   pl.BlockSpec((1, D), lambda i: (0, 0))],
        out_shape=[S((T, D), F32), S((1, D), F32)],
        compiler_params=_cp(1))(dgu, wup, x, g, dy)


def _tn_matmul(name, a, a_spec, b, out_shape, out_spec, n_blocks, scale=1.0, after=None):
    extra = [] if after is None else [after]

    def body(a_ref, b_ref, *rest):
        o_ref = rest[-1]
        o_ref[...] = (_tn(a_ref[...], b_ref[...]) * scale).astype(o_ref.dtype)

    return pl.pallas_call(
        body, name=name, grid=(n_blocks,),
        in_specs=[a_spec, pl.BlockSpec(b.shape, lambda j: (0, 0), pipeline_mode=pl.Buffered(1))]
        + [pl.BlockSpec((8, LANES), lambda j: (0, 0)) for _ in extra],
        out_specs=out_spec, out_shape=out_shape, compiler_params=_cp(1))(a, b, *extra)


def _ffn_dwup(name, h, dgu, after=None):
    T, D = h.shape
    FB = dgu.shape[-1]
    return _tn_matmul(
        name + "_dwup", dgu.reshape(2 * N_FFN_BLK, T, FB), pl.BlockSpec((None, T, FB), lambda j: (j, 0, 0)), h,
        S((2 * N_FFN_BLK, FB, D), BF), pl.BlockSpec((None, FB, D), lambda j: (j, 0, 0)), 2 * N_FFN_BLK,
        after=after)


def _ffn_dwdn(name, a, dyb):
    _, T, FB = a.shape
    D = dyb.shape[1]
    return _tn_matmul(
        name + "_dwdn", a, pl.BlockSpec((None, T, FB), lambda j: (j, 0, 0)), dyb,
        S((N_FFN_BLK, FB, D), BF), pl.BlockSpec((None, FB, D), lambda j: (j, 0, 0)), N_FFN_BLK, scale=0.5)


def _mix_proj(x, g, wz):
    T, D = x.shape
    tm = _tile(T, 512)

    def body(x_ref, g_ref, w_ref, z_ref, h_ref):
        xf = x_ref[...]
        hb = (xf * _rstd(xf) * g_ref[...]).astype(BF)
        h_ref[...] = hb
        z_ref[...] = _nt(hb, w_ref[...])

    return pl.pallas_call(
        body, name="mix_proj", grid=(T // tm,),
        in_specs=[pl.BlockSpec((tm, D), lambda i: (i, 0)),
                  pl.BlockSpec((1, D), lambda i: (0, 0)),
                  pl.BlockSpec((ZW, D), lambda i: (0, 0))],
        out_specs=[pl.BlockSpec((tm, ZW), lambda i: (i, 0)),
                   pl.BlockSpec((tm, D), lambda i: (i, 0))],
        out_shape=[S((T, ZW), F32), S((T, D), BF)],
        compiler_params=_cp(1))(x, g, wz)


def _tri(n, lower):
    r = lax.broadcasted_iota(jnp.int32, (n, n), 0)
    c = lax.broadcasted_iota(jnp.int32, (n, n), 1)
    return (r >= c) if lower else (r <= c)


def _spatial_mix(vgn_b, ws_ref, bst, tm):
    tril = _tri(CHUNK, True)
    wms = [jnp.where(tril, ws_ref[g], 0.0).astype(BF) for g in range(GMLP_G)]
    rows = []
    for c in range(tm // CHUNK):
        cols = []
        for g in range(GMLP_G):
            vs = vgn_b[c * CHUNK:(c + 1) * CHUNK, g * GMLP_GD:(g + 1) * GMLP_GD]
            cols.append(_nn(wms[g], vs) + bst[:, g:g + 1])
        rows.append(jnp.concatenate(cols, axis=1))
    return jnp.concatenate(rows, axis=0), wms


HB = 128
AUG_W = FOX_HEADS * HB
COL_A, COL_B, COL_C = 64, 67, 70
RS_Q, RS_K, RS_V, RS_O = 0, 8, 16, 17


def _piece_matrix(col):
    r = jnp.arange(LANES)
    dst = jnp.where(r < 3 * FOX_HEADS, (r % FOX_HEADS) * HB + col + r // FOX_HEADS, -1)
    return (jnp.arange(AUG_W)[None, :] == dst[:, None]).astype(BF)


def _ones_row(cols):
    c = jnp.arange(AUG_W) % HB
    hit = functools.reduce(jnp.logical_or, [(c >= a) & (c < a + 3) for a in cols])
    return hit.astype(F32)[None, :]


def _pieces(x):
    lane = lax.broadcasted_iota(jnp.int32, x.shape, 1)
    x = jnp.where(lane < FOX_HEADS, x, 0.0)
    hi = x.astype(BF).astype(F32)
    r1 = x - hi
    mid = r1.astype(BF).astype(F32)
    lo = (r1 - mid).astype(BF).astype(F32)
    return (hi + pltpu.roll(mid, FOX_HEADS, 1) + pltpu.roll(lo, 2 * FOX_HEADS, 1)).astype(BF)


def _mix_prep(z, bf128, g_q, g_k, g_sgu, w_s, b_st, g_go):
    T = z.shape[0]
    tm = _tile(T, 512)
    pc_q, pc_k = _piece_matrix(COL_A), _piece_matrix(COL_B)
    one_q, one_k, one_v = _ones_row([COL_B]), _ones_row([COL_A, COL_C]), _ones_row([COL_A])

    def body(z_ref, bf_ref, gq_ref, gk_ref, gs_ref, ws_ref, bst_ref, go_ref, pq_ref, pk_ref, oq_ref, ok_ref,
             ov_ref, q_ref, k_ref, v_ref, y_ref, rs_ref, carry_ref):
        i = pl.program_id(0)

        @pl.when(i == 0)
        def _():
            carry_ref[...] = jnp.zeros_like(carry_ref)

        fl = z_ref[:, Z_F:Z_F + LANES] + bf_ref[...]
        logf = jnp.minimum(fl, 0.0) - jnp.log1p(jnp.exp(-jnp.abs(fl)))
        csum = _hi(_tri(tm, True).astype(F32), logf) + carry_ref[...]
        carry_ref[...] = csum[tm - 1:tm, :]
        ext_q = (_nn(_pieces(csum), pq_ref[...]) + oq_ref[...]).astype(BF)
        ext_k = (_nn(_pieces(-csum), pk_ref[...]) + ok_ref[...]).astype(BF)
        ext_v = jnp.broadcast_to(ov_ref[...], (tm, AUG_W)).astype(BF)

        rs_ref[...] = jnp.zeros_like(rs_ref)
        for h in range(FOX_HEADS):
            lo, hi = slice(h * HB, h * HB + FOX_HD), slice(h * HB + FOX_HD, (h + 1) * HB)
            qh = z_ref[:, Z_Q + h * FOX_HD:Z_Q + (h + 1) * FOX_HD]
            kh = z_ref[:, Z_K + h * FOX_HD:Z_K + (h + 1) * FOX_HD]
            rq, rk = _rstd(qh), _rstd(kh)
            rs_ref[:, RS_Q + h:RS_Q + h + 1] = rq
            rs_ref[:, RS_K + h:RS_K + h + 1] = rk
            q_ref[:, lo] = (qh * rq * gq_ref[...] * 0.125).astype(BF)
            k_ref[:, lo] = (kh * rk * gk_ref[...]).astype(BF)
            v_ref[:, lo] = z_ref[:, Z_V + h * FOX_HD:Z_V + (h + 1) * FOX_HD].astype(BF)
            q_ref[:, hi] = ext_q[:, hi]
            k_ref[:, hi] = ext_k[:, hi]
            v_ref[:, hi] = ext_v[:, hi]

        u = _gelu(z_ref[:, Z_U:Z_U + GMLP_W])
        vg = _gelu(z_ref[:, Z_G:Z_G + GMLP_W])
        rv = _rstd(vg)
        vgn = (vg * rv * gs_ref[...]).astype(BF)
        mixed, _ = _spatial_mix(vgn, ws_ref, bst_ref[...], tm)
        sgu = u * mixed
        ro = _rstd(sgu)
        y_ref[...] = (sgu * ro * go_ref[...]).astype(BF)
        rs_ref[:, RS_V:RS_V + 1] = rv
        rs_ref[:, RS_O:RS_O + 1] = ro

    row = lambda i: (i, 0)
    fix2 = lambda i: (0, 0)
    return pl.pallas_call(
        body, name="mix_prep", grid=(T // tm,),
        in_specs=[pl.BlockSpec((tm, ZW), row),
                  pl.BlockSpec((1, LANES), fix2), pl.BlockSpec((1, FOX_HD), fix2), pl.BlockSpec((1, FOX_HD), fix2),
                  pl.BlockSpec((1, GMLP_W), fix2), pl.BlockSpec((GMLP_G, CHUNK, CHUNK), lambda i: (0, 0, 0)),
                  pl.BlockSpec((CHUNK, GMLP_G), fix2), pl.BlockSpec((1, GMLP_W), fix2),
                  pl.BlockSpec((LANES, AUG_W), fix2),
                  pl.BlockSpec((LANES, AUG_W), fix2), pl.BlockSpec((1, AUG_W), fix2), pl.BlockSpec((1, AUG_W), fix2),
                  pl.BlockSpec((1, AUG_W), fix2)],
        out_specs=[pl.BlockSpec((tm, AUG_W), row), pl.BlockSpec((tm, AUG_W), row), pl.BlockSpec((tm, AUG_W), row),
                   pl.BlockSpec((tm, GMLP_W), row), pl.BlockSpec((tm, LANES), row)],
        out_shape=[S((T, AUG_W), BF), S((T, AUG_W), BF), S((T, AUG_W), BF), S((T, GMLP_W), BF), S((T, LANES), F32)],
        scratch_shapes=[pltpu.VMEM((1, LANES), F32)],
        compiler_params=_cp(1))(z, bf128, g_q, g_k, g_sgu, w_s, b_st, g_go, pc_q, pc_k, one_q, one_k, one_v)


def _fox_fwd(q, k, v):
    T = q.shape[0]
    tq = _tile(T, 1024)
    nq = T // tq

    def body(q_ref, k_ref, v_ref, o_ref, lse_ref, m_sc, acc_sc):
        i, j = pl.program_id(0), pl.program_id(1)

        @pl.when(j == 0)
        def _():
            m_sc[...] = jnp.full(m_sc.shape, NEG, F32)
            acc_sc[...] = jnp.zeros_like(acc_sc)

        def step(masked):
            mask = _tri(tq, True) if masked else None
            for h in range(FOX_HEADS):
                hb = slice(h * HB, (h + 1) * HB)
                s = _nt(q_ref[:, hb], k_ref[:, hb])
                if masked:
                    s = jnp.where(mask, s, NEG)
                m_prev = m_sc[h]
                m_new = jnp.maximum(m_prev, jnp.broadcast_to(jnp.max(s, axis=1, keepdims=True), (tq, HB)))
                p = jnp.exp(s - jnp.tile(m_new, (1, tq // HB))).astype(BF)
                acc_sc[:, hb] = jnp.exp(m_prev - m_new) * acc_sc[:, hb] + _nn(p, v_ref[:, hb])
                m_sc[h] = m_new

        @pl.when(j < i)
        def _():
            step(False)

        @pl.when(j == i)
        def _():
            step(True)
            lse_ref[...] = jnp.zeros_like(lse_ref)
            for h in range(FOX_HEADS):
                l = acc_sc[:, h * HB + COL_A:h * HB + COL_A + 1]
                o_ref[:, h * FOX_HD:(h + 1) * FOX_HD] = acc_sc[:, h * HB:h * HB + FOX_HD] / l
                lse_ref[:, h:h + 1] = m_sc[h][:, 0:1] + jnp.log(l)

    qi = lambda i, j: (i, 0)
    kj = lambda i, j: (jnp.minimum(i, j), 0)
    return pl.pallas_call(
        body, name="fox_fwd", grid=(nq, nq),
        in_specs=[pl.BlockSpec((tq, AUG_W), qi), pl.BlockSpec((tq, AUG_W), kj), pl.BlockSpec((tq, AUG_W), kj)],
        out_specs=[pl.BlockSpec((tq, FOX_W), qi), pl.BlockSpec((tq, LANES), qi)],
        out_shape=[S((T, FOX_W), F32), S((T, LANES), F32)],
        scratch_shapes=[pltpu.VMEM((FOX_HEADS, tq, HB), F32), pltpu.VMEM((tq, AUG_W), F32)],
        compiler_params=_cp(2))(q, k, v)


def _fox_bwd(q, k, v, dob):
    T = q.shape[0]
    tq = _tile(T, 512)
    nq = T // tq
    n_sweeps = 1
    half = AUG_W // n_sweeps
    hpg = FOX_HEADS // n_sweeps

    pairs = [(j, i) for j in range(nq) for i in range(j, nq)]
    jt = jnp.asarray([p[0] for p in pairs], jnp.int32)
    it = jnp.asarray([p[1] for p in pairs], jnp.int32)

    def body(jt_ref, it_ref, q_ref, k_ref, v_ref, do_ref, dq_ref, dk_ref, dv_ref, dq_sc):
        t = pl.program_id(1)
        j, i = jt_ref[t], it_ref[t]

        @pl.when(t == 0)
        def _():
            dq_sc[...] = jnp.zeros_like(dq_sc)

        @pl.when(i == j)
        def _():
            dk_ref[...] = jnp.zeros_like(dk_ref)
            dv_ref[...] = jnp.zeros_like(dv_ref)

        def step(masked):
            rows = pl.ds(pl.multiple_of(i * tq, tq), tq)
            mask = _tri(tq, True) if masked else None
            for h in range(hpg):
                hb = slice(h * HB, (h + 1) * HB)
                qh, kh, vh, doh = q_ref[:, hb], k_ref[:, hb], v_ref[:, hb], do_ref[:, hb]
                s = _nt(qh, kh)
                if masked:
                    s = jnp.where(mask, s, NEG)
                p = jnp.exp(s)
                dsb = (p * _nt(doh, vh)).astype(BF)
                dv_ref[:, hb] += _tn(p.astype(BF), doh)
                dk_ref[:, hb] += _tn(dsb, qh)
                dq_sc[rows, hb] += _nn(dsb, kh)

        @pl.when(i > j)
        def _():
            step(False)

        @pl.when(i == j)
        def _():
            step(True)
            dq_ref[...] = dq_sc[pl.ds(pl.multiple_of(j * tq, tq), tq), :]

    qi = pl.BlockSpec((tq, half), lambda g, t, jt_ref, it_ref: (it_ref[t], g))
    kj = pl.BlockSpec((tq, half), lambda g, t, jt_ref, it_ref: (jt_ref[t], g))
    return pl.pallas_call(
        body, name="fox_bwd",
        grid_spec=pltpu.PrefetchScalarGridSpec(
            num_scalar_prefetch=2, grid=(n_sweeps, len(pairs)), in_specs=[qi, kj, kj, qi], out_specs=[kj, kj, kj],
            scratch_shapes=[pltpu.VMEM((T, half), F32)]),
        out_shape=[S((T, AUG_W), F32), S((T, AUG_W), F32), S((T, AUG_W), F32)],
        compiler_params=_cp(2))(jt, it, q, k, v, dob)


def _mix_out(attn, yg, g_fo, wout, x):
    T, D = x.shape
    tm = _tile(T, 1024)

    def body(a_ref, y_ref, g_ref, w_ref, x_ref, o_ref):
        at = a_ref[...]
        yf = (at * _rstd(at) * g_ref[...]).astype(BF)
        o_ref[...] = x_ref[...] + _nn(yf, w_ref[:FOX_W, :]) + _nn(y_ref[...], w_ref[FOX_W:, :])

    row = lambda i: (i, 0)
    return pl.pallas_call(
        body, name="mix_out", grid=(T // tm,),
        in_specs=[pl.BlockSpec((tm, FOX_W), row), pl.BlockSpec((tm, GMLP_W), row),
                  pl.BlockSpec((1, FOX_W), lambda i: (0, 0)), pl.BlockSpec((D, D), lambda i: (0, 0)),
                  pl.BlockSpec((tm, D), row)],
        out_specs=pl.BlockSpec((tm, D), row),
        out_shape=S((T, D), F32),
        compiler_params=_cp(1))(attn, yg, g_fo, wout, x)


def _mix_out_bwd(dx, attn, yg, g_fo, wout, qf, lse):
    T, D = dx.shape
    tm = _tile(T, 512)
    n = T // tm
    pc_l, pc_d = _piece_matrix(COL_C), _piece_matrix(COL_A)

    def body(dx_ref, a_ref, y_ref, g_ref, w_ref, qf_ref, lse_ref, pl_ref, pd_ref,
             qb_ref, dob_ref, dyg_ref, dw_ref, dg_ref, acc_ref, dsum_ref):
        i = pl.program_id(0)
        dxb = dx_ref[...].astype(BF)
        at = a_ref[...]
        yf = (at * _rstd(at) * g_ref[...]).astype(BF)
        dy = _nt(dxb, w_ref[...])
        p_top = _tn(yf, dxb)
        p_bot = _tn(y_ref[...], dxb)

        @pl.when(i == 0)
        def _():
            acc_ref[:FOX_W, :] = p_top
            acc_ref[FOX_W:, :] = p_bot

        @pl.when(i > 0)
        def _():
            acc_ref[:FOX_W, :] += p_top
            acc_ref[FOX_W:, :] += p_bot

        @pl.when(i == n - 1)
        def _():
            dw_ref[...] = acc_ref[...].astype(BF)

        dat, dgr = _norm_bwd(dy[:, :FOX_W], at, g_ref[...])
        _acc_rows(dg_ref, i == 0, dgr)
        dyg_ref[...] = dy[:, FOX_W:]
        prod = dat * at
        dsum_ref[...] = jnp.zeros_like(dsum_ref)
        for h in range(FOX_HEADS):
            dsum_ref[:, h:h + 1] = jnp.sum(prod[:, h * FOX_HD:(h + 1) * FOX_HD], axis=1, keepdims=True)
        ext_d = _nn(_pieces(-dsum_ref[...]), pd_ref[...]).astype(BF)
        ext_l = _nn(_pieces(-lse_ref[...]), pl_ref[...])
        datb = dat.astype(BF)
        for h in range(FOX_HEADS):
            lo, hi = slice(h * HB, h * HB + FOX_HD), slice(h * HB + FOX_HD, (h + 1) * HB)
            dob_ref[:, lo] = datb[:, h * FOX_HD:(h + 1) * FOX_HD]
            dob_ref[:, hi] = ext_d[:, hi]
            qb_ref[:, lo] = qf_ref[:, lo]
            qb_ref[:, hi] = (qf_ref[:, hi].astype(F32) + ext_l[:, hi]).astype(BF)

    row = lambda i: (i, 0)
    fix = lambda i: (0, 0)
    return pl.pallas_call(
        body, name="mix_out_bwd", grid=(n,),
        in_specs=[pl.BlockSpec((tm, D), row), pl.BlockSpec((tm, FOX_W), row), pl.BlockSpec((tm, GMLP_W), row),
                  pl.BlockSpec((1, FOX_W), fix), pl.BlockSpec((D, D), fix), pl.BlockSpec((tm, AUG_W), row),
                  pl.BlockSpec((tm, LANES), row), pl.BlockSpec((LANES, AUG_W), fix),
                  pl.BlockSpec((LANES, AUG_W), fix)],
        out_specs=[pl.BlockSpec((tm, AUG_W), row), pl.BlockSpec((tm, AUG_W), row), pl.BlockSpec((tm, GMLP_W), row),
                   pl.BlockSpec((D, D), fix), pl.BlockSpec((1, FOX_W), fix)],
        out_shape=[S((T, AUG_W), BF), S((T, AUG_W), BF), S((T, GMLP_W), F32), S((D, D), BF), S((1, FOX_W), F32)],
        scratch_shapes=[pltpu.VMEM((D, D), F32), pltpu.VMEM((tm, LANES), F32)],
        compiler_params=_cp(1))(dx, attn, yg, g_fo, wout, qf, lse, pc_l, pc_d)


def _mix_prep_bwd(z, dq, dk, dv, dyg, rs, bf128, g_q, g_k, g_sgu, w_s, b_st, g_go):
    T = z.shape[0]
    tm = _tile(T, 512)
    n = T // tm

    def body(z_ref, dq_ref, dk_ref, dv_ref, dyg_ref, rs_ref, bf_ref, gq_ref, gk_ref, gs_ref, ws_ref,
             bst_ref, go_ref, dz_ref, dgq_ref, dgk_ref, dgs_ref, dgo_ref, dws_ref, dbst_ref, dbf_ref, carry_ref):
        i = pl.program_id(0)
        first = i == 0
        rs = rs_ref[...]

        @pl.when(first)
        def _():
            carry_ref[...] = jnp.zeros_like(carry_ref)

        lane = lax.broadcasted_iota(jnp.int32, (tm, LANES), 1)
        dc = jnp.zeros((tm, LANES), F32)
        gq_rows, gk_rows = [], []
        for h in range(FOX_HEADS):
            hp = slice(h * HB, h * HB + FOX_HD)
            dqh, gqr = _norm_bwd(dq_ref[:, hp] * 0.125, z_ref[:, Z_Q + h * FOX_HD:Z_Q + (h + 1) * FOX_HD], gq_ref[...],
                                 rs[:, RS_Q + h:RS_Q + h + 1])
            dkh, gkr = _norm_bwd(dk_ref[:, hp], z_ref[:, Z_K + h * FOX_HD:Z_K + (h + 1) * FOX_HD], gk_ref[...],
                                 rs[:, RS_K + h:RS_K + h + 1])
            dz_ref[:, Z_Q + h * FOX_HD:Z_Q + (h + 1) * FOX_HD] = dqh.astype(BF)
            dz_ref[:, Z_K + h * FOX_HD:Z_K + (h + 1) * FOX_HD] = dkh.astype(BF)
            dz_ref[:, Z_V + h * FOX_HD:Z_V + (h + 1) * FOX_HD] = dv_ref[:, hp].astype(BF)
            dch = dq_ref[:, h * HB + COL_A:h * HB + COL_A + 1] - dk_ref[:, h * HB + COL_B:h * HB + COL_B + 1]
            dc = jnp.where(lane == h, dch, dc)
            gq_rows.append(gqr)
            gk_rows.append(gkr)
        _acc_rows(dgq_ref, first, functools.reduce(lambda a, b: a + b, gq_rows))
        _acc_rows(dgk_ref, first, functools.reduce(lambda a, b: a + b, gk_rows))

        dlogf = _hi(_tri(tm, False).astype(F32), dc) + carry_ref[...]
        carry_ref[...] = dlogf[0:1, :]
        fl = z_ref[:, Z_F:Z_F + LANES] + bf_ref[...]
        lane = lax.broadcasted_iota(jnp.int32, (tm, LANES), 1)
        df = jnp.where(lane < FOX_HEADS, dlogf * jax.nn.sigmoid(-fl), 0.0)
        dz_ref[:, Z_F:Z_F + LANES] = df.astype(BF)
        _acc_rows(dbf_ref, first, df)

        u_pre = z_ref[:, Z_U:Z_U + GMLP_W]
        vg_pre = z_ref[:, Z_G:Z_G + GMLP_W]
        u = _gelu(u_pre)
        vg = _gelu(vg_pre)
        rv = rs[:, RS_V:RS_V + 1]
        vgn = (vg * rv * gs_ref[...]).astype(BF)
        bst = bst_ref[...]
        mixed, wms = _spatial_mix(vgn, ws_ref, bst, tm)
        sgu = u * mixed
        dsgu, gor = _norm_bwd(dyg_ref[...], sgu, go_ref[...], rs[:, RS_O:RS_O + 1])
        _acc_rows(dgo_ref, first, gor)
        du = dsgu * mixed
        dmixed = dsgu * u
        dmb = dmixed.astype(BF)
        tril = _tri(CHUNK, True)
        dvgn_rows = []
        dws = [None] * GMLP_G
        dbs = [None] * GMLP_G
        for c in range(tm // CHUNK):
            cs = slice(c * CHUNK, (c + 1) * CHUNK)
            cols = []
            for g in range(GMLP_G):
                gs = slice(g * GMLP_GD, (g + 1) * GMLP_GD)
                dmc = dmb[cs, gs]
                pw = _nt(dmc, vgn[cs, gs])
                pb = jnp.sum(dmixed[cs, gs], axis=1, keepdims=True)
                dws[g] = pw if dws[g] is None else dws[g] + pw
                dbs[g] = pb if dbs[g] is None else dbs[g] + pb
                cols.append(_tn(wms[g], dmc))
            dvgn_rows.append(jnp.concatenate(cols, axis=1))
        dvgn = jnp.concatenate(dvgn_rows, axis=0)
        dbs_t = jnp.concatenate(dbs, axis=1)
        for g in range(GMLP_G):
            dwg = jnp.where(tril, dws[g], 0.0)

            @pl.when(first)
            def _():
                dws_ref[g] = dwg

            @pl.when(jnp.logical_not(first))
            def _():
                dws_ref[g] += dwg

        @pl.when(first)
        def _():
            dbst_ref[...] = dbs_t

        @pl.when(jnp.logical_not(first))
        def _():
            dbst_ref[...] += dbs_t

        dvg, gsr = _norm_bwd(dvgn, vg, gs_ref[...], rv)
        _acc_rows(dgs_ref, first, gsr)
        dz_ref[:, Z_U:Z_U + GMLP_W] = (du * _gelu_grad(u_pre)).astype(BF)
        dz_ref[:, Z_G:Z_G + GMLP_W] = (dvg * _gelu_grad(vg_pre)).astype(BF)

    rev = lambda i: (n - 1 - i, 0)
    fix = lambda i: (0, 0)
    fix3 = lambda i: (0, 0, 0)
    return pl.pallas_call(
        body, name="mix_prep_bwd", grid=(n,),
        in_specs=[pl.BlockSpec((tm, ZW), rev), pl.BlockSpec((tm, AUG_W), rev), pl.BlockSpec((tm, AUG_W), rev),
                  pl.BlockSpec((tm, AUG_W), rev), pl.BlockSpec((tm, GMLP_W), rev), pl.BlockSpec((tm, LANES), rev),
                  pl.BlockSpec((1, LANES), fix), pl.BlockSpec((1, FOX_HD), fix), pl.BlockSpec((1, FOX_HD), fix),
                  pl.BlockSpec((1, GMLP_W), fix), pl.BlockSpec((GMLP_G, CHUNK, CHUNK), fix3),
                  pl.BlockSpec((CHUNK, GMLP_G), fix), pl.BlockSpec((1, GMLP_W), fix)],
        out_specs=[pl.BlockSpec((tm, ZW), rev), pl.BlockSpec((1, FOX_HD), fix), pl.BlockSpec((1, FOX_HD), fix),
                   pl.BlockSpec((1, GMLP_W), fix), pl.BlockSpec((1, GMLP_W), fix),
                   pl.BlockSpec((GMLP_G, CHUNK, CHUNK), fix3), pl.BlockSpec((CHUNK, GMLP_G), fix),
                   pl.BlockSpec((1, LANES), fix)],
        out_shape=[S((T, ZW), BF), S((1, FOX_HD), F32), S((1, FOX_HD), F32), S((1, GMLP_W), F32), S((1, GMLP_W), F32),
                   S((GMLP_G, CHUNK, CHUNK), F32), S((CHUNK, GMLP_G), F32), S((1, LANES), F32)],
        scratch_shapes=[pltpu.VMEM((1, LANES), F32)],
        compiler_params=_cp(1))(z, dq, dk, dv, dyg, rs, bf128, g_q, g_k, g_sgu, w_s, b_st, g_go)


def _mix_proj_bwd(dz, wz, x, g, dy):
    T, D = x.shape
    tm = _tile(T, 512)

    def body(dz_ref, w_ref, x_ref, g_ref, dy_ref, dx_ref, dxb_ref, dg_ref):
        dh = _nn(dz_ref[...], w_ref[...])
        dx, dgr = _norm_bwd(dh, x_ref[...], g_ref[...])
        dx = dx + dy_ref[...]
        dx_ref[...] = dx
        dxb_ref[...] = dx.astype(BF)
        _acc_rows(dg_ref, pl.program_id(0) == 0, dgr)

    row = lambda i: (i, 0)
    fix = lambda i: (0, 0)
    return pl.pallas_call(
        body, name="mix_proj_bwd", grid=(T // tm,),
        in_specs=[pl.BlockSpec((tm, ZW), row), pl.BlockSpec((ZW, D), fix), pl.BlockSpec((tm, D), row),
                  pl.BlockSpec((1, D), fix), pl.BlockSpec((tm, D), row)],
        out_specs=[pl.BlockSpec((tm, D), row), pl.BlockSpec((tm, D), row), pl.BlockSpec((1, D), fix)],
        out_shape=[S((T, D), F32), S((T, D), BF), S((1, D), F32)],
        compiler_params=_cp(1))(dz, wz, x, g, dy)


def _ca_kv(mem, g_mem, wckv, g_ck):
    M, D = mem.shape

    def body(m_ref, g_ref, w_ref, gk_ref, mn_ref, kr_ref, kn_ref, v_ref):
        mf = m_ref[...]
        mn = (mf * _rstd(mf) * g_ref[...]).astype(BF)
        mn_ref[...] = mn
        for h in range(CA_HEADS):
            kr = _nn(mn, w_ref[h])
            kr_ref[h] = kr
            kn_ref[h] = (kr * _rstd(kr) * gk_ref[...]).astype(BF)
            v_ref[h] = _nn(mn, w_ref[CA_HEADS + h]).astype(BF)

    hd = (CA_HEADS, M, CA_HD)
    return pl.pallas_call(
        body, name="ca_kv", out_shape=[S((M, D), BF), S(hd, F32), S(hd, BF), S(hd, BF)],
        compiler_params=pltpu.CompilerParams(vmem_limit_bytes=VMEM_LIMIT))(mem, g_mem, wckv, g_ck)


def _ca_tile_fwd(xt, gca, wcq, gcq, kn_ref, v_ref):
    hb = (xt * _rstd(xt) * gca).astype(BF)
    qc = _nn(hb, wcq)
    qr, qn, ps = [], [], []
    for h in range(CA_HEADS):
        qh = qc[:, h * CA_HD:(h + 1) * CA_HD]
        qnh = (qh * _rstd(qh) * gcq * 0.0625).astype(BF)
        s = _nt(qnh, kn_ref[h])
        e = jnp.exp(s - jnp.max(s, axis=1, keepdims=True))
        ps.append(e / jnp.sum(e, axis=1, keepdims=True))
        qr.append(qh)
        qn.append(qnh)
    return hb, qr, qn, ps


def _ca_fwd(x, g_ca, wcq, g_cq, kn, vv, wco):
    T, D = x.shape
    M = kn.shape[1]
    tm = _tile(T, 1024)

    def body(x_ref, gca_ref, wcq_ref, gcq_ref, kn_ref, v_ref, wco_ref, o_ref, ob_sc):
        xt = x_ref[...]
        _, _, _, ps = _ca_tile_fwd(xt, gca_ref[...], wcq_ref[...], gcq_ref[...], kn_ref, v_ref)
        for h in range(CA_HEADS):
            ob_sc[:, h * CA_HD:(h + 1) * CA_HD] = _nn(ps[h].astype(BF), v_ref[h]).astype(BF)
        o_ref[...] = xt + _nn(ob_sc[...], wco_ref[...])

    row = lambda i: (i, 0)
    fix = lambda i: (0, 0)
    fix3 = lambda i: (0, 0, 0)
    return pl.pallas_call(
        body, name="ca_fwd", grid=(T // tm,),
        in_specs=[pl.BlockSpec((tm, D), row), pl.BlockSpec((1, D), fix), pl.BlockSpec((D, D), fix),
                  pl.BlockSpec((1, CA_HD), fix), pl.BlockSpec((CA_HEADS, M, CA_HD), fix3),
                  pl.BlockSpec((CA_HEADS, M, CA_HD), fix3), pl.BlockSpec((D, D), fix)],
        out_specs=pl.BlockSpec((tm, D), row), out_shape=S((T, D), F32),
        scratch_shapes=[pltpu.VMEM((tm, D), BF)],
        compiler_params=_cp(1))(x, g_ca, wcq, g_cq, kn, vv, wco)


def _ca_bwd(x, dy, g_ca, wcq, g_cq, kn, vv, wco):
    T, D = x.shape
    M = kn.shape[1]
    tm = _tile(T, 512)
    n = T // tm

    def body(x_ref, dy_ref, gca_ref, wcq_ref, gcq_ref, kn_ref, v_ref, wco_ref,
             dx_ref, dwq_ref, dwo_ref, dkn_ref, dv_ref, dgcq_ref, dgca_ref, aq_sc, ao_sc, ob_sc, dq_sc):
        i = pl.program_id(0)
        first = i == 0
        xt = x_ref[...]
        dyt = dy_ref[...]
        dyb = dyt.astype(BF)
        hb, qr, qn, ps = _ca_tile_fwd(xt, gca_ref[...], wcq_ref[...], gcq_ref[...], kn_ref, v_ref)
        do = _nt(dyb, wco_ref[...])
        gcq_rows = None
        for h in range(CA_HEADS):
            hs = slice(h * CA_HD, (h + 1) * CA_HD)
            p = ps[h]
            pb = p.astype(BF)
            ob_sc[:, hs] = _nn(pb, v_ref[h]).astype(BF)
            doh = do[:, hs].astype(BF)
            dp = _nt(doh, v_ref[h])
            ds = (p * (dp - jnp.sum(dp * p, axis=1, keepdims=True))).astype(BF)
            dvh = _tn(pb, doh)
            dkh = _tn(ds, qn[h])

            @pl.when(first)
            def _():
                dv_ref[h] = dvh
                dkn_ref[h] = dkh

            @pl.when(jnp.logical_not(first))
            def _():
                dv_ref[h] += dvh
                dkn_ref[h] += dkh

            dqn = _nn(ds, kn_ref[h]) * 0.0625
            dqh, gr = _norm_bwd(dqn, qr[h], gcq_ref[...])
            gcq_rows = gr if gcq_rows is None else gcq_rows + gr
            dq_sc[:, hs] = dqh.astype(BF)
        _acc_rows(dgcq_ref, first, gcq_rows)
        dqb = dq_sc[...]
        p_o = _tn(ob_sc[...], dyb)
        p_q = _tn(hb, dqb)

        @pl.when(first)
        def _():
            ao_sc[...] = p_o
            aq_sc[...] = p_q

        @pl.when(jnp.logical_not(first))
        def _():
            ao_sc[...] += p_o
            aq_sc[...] += p_q

        @pl.when(i == n - 1)
        def _():
            dwo_ref[...] = ao_sc[...].astype(BF)
            dwq_ref[...] = aq_sc[...].astype(BF)

        dh = _nt(dqb, wcq_ref[...])
        dx, gar = _norm_bwd(dh, xt, gca_ref[...])
        dx_ref[...] = dx + dyt
        _acc_rows(dgca_ref, first, gar)

    row = lambda i: (i, 0)
    fix = lambda i: (0, 0)
    fix3 = lambda i: (0, 0, 0)
    hd = (CA_HEADS, M, CA_HD)
    return pl.pallas_call(
        body, name="ca_bwd", grid=(n,),
        in_specs=[pl.BlockSpec((tm, D), row), pl.BlockSpec((tm, D), row), pl.BlockSpec((1, D), fix),
                  pl.BlockSpec((D, D), fix), pl.BlockSpec((1, CA_HD), fix), pl.BlockSpec(hd, fix3),
                  pl.BlockSpec(hd, fix3), pl.BlockSpec((D, D), fix)],
        out_specs=[pl.BlockSpec((tm, D), row), pl.BlockSpec((D, D), fix), pl.BlockSpec((D, D), fix),
                   pl.BlockSpec(hd, fix3), pl.BlockSpec(hd, fix3), pl.BlockSpec((1, CA_HD), fix),
                   pl.BlockSpec((1, D), fix)],
        out_shape=[S((T, D), F32), S((D, D), BF), S((D, D), BF), S(hd, F32), S(hd, F32), S((1, CA_HD), F32),
                   S((1, D), F32)],
        scratch_shapes=[pltpu.VMEM((D, D), F32), pltpu.VMEM((D, D), F32), pltpu.VMEM((tm, D), BF),
                        pltpu.VMEM((tm, D), BF)],
        compiler_params=_cp(1))(x, dy, g_ca, wcq, g_cq, kn, vv, wco)


def _ca_kv_bwd(mem, g_mem, mn, kraw, dkn, dvv, wckv, g_ck):
    M, D = mem.shape

    def body(m_ref, g_ref, mn_ref, kr_ref, dkn_ref, dv_ref, w_ref, gk_ref, dw_ref, dgk_ref, dgm_ref):
        mn = mn_ref[...]
        dmn = jnp.zeros((M, D), F32)
        gk_rows = None
        for h in range(CA_HEADS):
            dkr, gr = _norm_bwd(dkn_ref[h], kr_ref[h], gk_ref[...])
            gk_rows = gr if gk_rows is None else gk_rows + gr
            dkb = dkr.astype(BF)
            dvb = dv_ref[h].astype(BF)
            dw_ref[h] = _tn(mn, dkb).astype(BF)
            dw_ref[CA_HEADS + h] = _tn(mn, dvb).astype(BF)
            dmn = dmn + _nt(dkb, w_ref[h]) + _nt(dvb, w_ref[CA_HEADS + h])
        dgk_ref[...] = jnp.sum(gk_rows, axis=0, keepdims=True)
        mf = m_ref[...]
        dgm_ref[...] = jnp.sum(dmn * (mf * _rstd(mf)), axis=0, keepdims=True)

    return pl.pallas_call(
        body, name="ca_kv_bwd",
        out_shape=[S((2 * CA_HEADS, D, CA_HD), BF), S((1, CA_HD), F32), S((1, D), F32)],
        compiler_params=pltpu.CompilerParams(vmem_limit_bytes=VMEM_LIMIT))(mem, g_mem, mn, kraw, dkn, dvv, wckv, g_ck)


def _after(g, token):
    return g if token is None else g + token[0:1, 0:1]


def _local_step(x, mem, target, small, weights, emit):
    T, D = x.shape
    p = small
    bf128 = jnp.pad(p["b_f"], ((0, 0), (0, LANES - FOX_HEADS)))
    b_st = p["b_s"].T

    wup1 = weights("ffn1_up", x)["wup1"]
    a1, h1 = _ffn_up("ffn1_up", x, p["g_ffn1"], wup1)
    wdn1 = weights("ffn1_dn", h1)["wdn1"]
    x1 = _ffn_down("ffn1_down", a1, wdn1, x)
    wm = weights("mix", x1)
    z, h2 = _mix_proj(x1, p["g_mix"], wm["wz"])
    qf, ka, va, yg, rs = _mix_prep(z, bf128, p["g_q"], p["g_k"], p["g_sgu"], p["w_s"], b_st, p["g_gmlp_o"])
    attn, lse = _fox_fwd(qf, ka, va)
    x2 = _mix_out(attn, yg, p["g_fox_o"], wm["wout"], x1)
    wc = weights("ca", x2)
    mn, kraw, ckn, cvv = _ca_kv(mem, p["g_mem"], wc["wckv"], p["g_ck"])
    x3 = _ca_fwd(x2, p["g_ca"], wc["wcq"], p["g_cq"], ckn, cvv, wc["wco"])
    w2 = weights("ffn2", x3)
    a2, h4 = _ffn_up("ffn2_up", x3, p["g_ffn2"], w2["wup2"])
    dy4, dy4b, sq = _ffn_down_loss("ffn2_down", a2, w2["wdn2"], x3, target)

    gs = {}
    dgu2 = _ffn_bwd_act("ffn2_bwd_act", dy4b, h4, w2["wup2"], w2["wdn2"])
    tok = emit("ffn2", {"wup2": _ffn_dwup("ffn2", h4, dgu2), "wdn2": _ffn_dwdn("ffn2", a2, dy4b)})
    dx3, gs["g_ffn2"] = _ffn_dx("ffn2_dx", dgu2, w2["wup2"], x3, _after(p["g_ffn2"], tok), dy4)

    dx2, dwcq, dwco, dckn, dcvv, gs["g_cq"], gs["g_ca"] = _ca_bwd(
        x2, dx3, p["g_ca"], wc["wcq"], p["g_cq"], ckn, cvv, wc["wco"])
    dwckv, gs["g_ck"], gs["g_mem"] = _ca_kv_bwd(mem, p["g_mem"], mn, kraw, dckn, dcvv, wc["wckv"], p["g_ck"])

    qb, dob, dyg, dwout, gs["g_fox_o"] = _mix_out_bwd(dx2, attn, yg, p["g_fox_o"], wm["wout"], qf, lse)
    dq, dk, dv = _fox_bwd(qb, ka, va, dob)
    dz, gs["g_q"], gs["g_k"], gs["g_sgu"], gs["g_gmlp_o"], gs["w_s"], dbst, dbf = _mix_prep_bwd(
        z, dq, dk, dv, dyg, rs, bf128, p["g_q"], p["g_k"], p["g_sgu"], p["w_s"], b_st, p["g_gmlp_o"])
    gs["b_s"] = dbst.T
    gs["b_f"] = dbf[:, :FOX_HEADS]
    tok_ws = emit("w_s", {"w_s": gs["w_s"]})
    zb = ZW // 3
    dwz = _tn_matmul("mix_dwz", dz, pl.BlockSpec((T, zb), lambda j: (0, j)), h2,
                     S((ZW, D), BF), pl.BlockSpec((zb, D), lambda j: (j, 0)), 3)
    tok = emit("mid", {"wcq": dwcq, "wco": dwco, "wckv": dwckv, "wout": dwout, "wz": dwz})
    dx1, dx1b, gs["g_mix"] = _mix_proj_bwd(dz, wm["wz"], x1, _after(_after(p["g_mix"], tok), tok_ws), dx2)

    dgu1 = _ffn_bwd_act("ffn1_bwd_act", dx1b, h1, wup1, wdn1)
    tok = emit("ffn1_dn", {"wdn1": _ffn_dwdn("ffn1", a1, dx1b)})
    tok = emit("ffn1_up", {"wup1": _ffn_dwup("ffn1", h1, dgu1, after=tok)})
    dx0, gs["g_ffn1"] = _ffn_dx("ffn1_dx", dgu1, wup1, x, _after(p["g_ffn1"], tok), dx1)
    return sq, dx0, gs


MESH = pl.DeviceIdType.MESH
HBM_SPEC = pl.BlockSpec(memory_space=pltpu.HBM)
N_PEER = N_DEV - 1


def _place():
    return lax.axis_index("x"), lax.axis_index("y"), lax.axis_index("c")


def _slot(px, py, pc):
    return 4 * px + 2 * py + pc


SEM_SPEC = pl.BlockSpec(memory_space=pltpu.SEMAPHORE)
ANY_SPEC = pl.BlockSpec(memory_space=pl.ANY)
DATAFLOW = pltpu.SideEffectType.DATAFLOW_SIDE_EFFECTING


def _hbm(a):
    return pltpu.with_memory_space_constraint(a, pltpu.HBM)


def _peer(x, y, c, r):
    return (1 - x if r & 4 else x, 1 - y if r & 2 else y, 1 - c if r & 1 else c)


def _place_own(srcs, whole):
    my = _slot(*_place())
    lands = []
    for s in srcs:
        blk = s[None] if whole else lax.dynamic_slice_in_dim(s, my, 1, 0)
        shape = (N_DEV,) + s.shape if whole else s.shape
        lands.append(lax.dynamic_update_slice_in_dim(lax.empty(shape, s.dtype), blk, my, 0))
    return lands


ALL_PEERS = tuple(range(1, N_DEV))
NEAR_PEERS = (1, 2, 4, 6)
SAME_CORE = (2, 4, 6)


def _copy_start(name, srcs, lands, whole, peers=None):
    n = len(srcs)
    peers = peers or [ALL_PEERS] * n

    def body(*refs):
        src, land = refs[:n], refs[n:2 * n]
        send, recv = refs[2 * n:3 * n], refs[3 * n:4 * n]
        token = refs[6 * n]
        x, y, c = _place()
        my = _slot(x, y, c)
        for a in range(n):
            for r in peers[a]:
                p = _peer(x, y, c, r)
                pltpu.make_async_remote_copy(
                    src_ref=src[a] if whole else src[a].at[_slot(*p)], dst_ref=land[a].at[my],
                    send_sem=send[a].at[r - 1], recv_sem=recv[a].at[r - 1], device_id=p, device_id_type=MESH).start()
        token[...] = jnp.zeros_like(token)

    out = pl.pallas_call(
        body, name=name,
        out_shape=([pltpu.SemaphoreType.DMA((N_PEER,))] * (2 * n)
                   + [pltpu.HBM(s.shape, s.dtype) for s in srcs] + [pltpu.HBM(s.shape, s.dtype) for s in lands]
                   + [S((8, LANES), F32)]),
        in_specs=[HBM_SPEC] * (2 * n),
        out_specs=[SEM_SPEC] * (2 * n) + [HBM_SPEC] * (2 * n) + [pl.BlockSpec(memory_space=pltpu.VMEM)],
        input_output_aliases={i: 2 * n + i for i in range(2 * n)},
        compiler_params=pltpu.CompilerParams(has_side_effects=DATAFLOW),
    )(*[_hbm(s) for s in srcs], *[_hbm(s) for s in lands])
    return out[:n], out[n:2 * n], out[2 * n:3 * n], out[3 * n:4 * n], out[4 * n]


def _copy_wait(name, srcs, lands, send, recv, after, whole, peers=None, with_srcs=False):
    n = len(srcs)
    peers = peers or [ALL_PEERS] * n

    def body(*refs):
        src, land = refs[:n], refs[n:2 * n]
        snd, rcv = refs[2 * n:3 * n], refs[3 * n:4 * n]
        x, y, c = _place()
        for a in range(n):
            for r in peers[a]:
                p = _peer(x, y, c, r)
                ps = _slot(*p)
                cp = pltpu.make_async_remote_copy(
                    src_ref=src[a] if whole else src[a].at[ps], dst_ref=land[a].at[ps],
                    send_sem=snd[a].at[r - 1], recv_sem=rcv[a].at[r - 1], device_id=p, device_id_type=MESH)
                cp.wait_send()
                cp.wait_recv()

    out = pl.pallas_call(
        body, name=name,
        out_shape=[pltpu.HBM(s.shape, s.dtype) for s in srcs] + [pltpu.HBM(s.shape, s.dtype) for s in lands],
        in_specs=[HBM_SPEC] * (2 * n) + [SEM_SPEC] * (2 * n) + [ANY_SPEC],
        out_specs=[HBM_SPEC] * (2 * n),
        input_output_aliases={i: i for i in range(2 * n)},
        compiler_params=pltpu.CompilerParams(has_side_effects=DATAFLOW),
    )(*srcs, *lands, *send, *recv, after)
    return (out[:n], out[n:]) if with_srcs else out[n:]


def _forward_start(name, lands):
    n = len(lands)

    def body(*refs):
        land = refs[:n]
        send, recv = refs[n:2 * n], refs[2 * n:3 * n]
        token = refs[4 * n]
        x, y, c = _place()
        for a in range(n):
            for r in SAME_CORE:
                blk = land[a].at[_slot(*_peer(x, y, c, r))]
                pltpu.make_async_remote_copy(
                    src_ref=blk, dst_ref=blk, send_sem=send[a].at[r - 1], recv_sem=recv[a].at[r - 1],
                    device_id=(x, y, 1 - c), device_id_type=MESH).start()
        token[...] = jnp.zeros_like(token)

    out = pl.pallas_call(
        body, name=name,
        out_shape=([pltpu.SemaphoreType.DMA((N_PEER,))] * (2 * n) + [pltpu.HBM(s.shape, s.dtype) for s in lands]
                   + [S((8, LANES), F32)]),
        in_specs=[HBM_SPEC] * n,
        out_specs=[SEM_SPEC] * (2 * n) + [HBM_SPEC] * n + [pl.BlockSpec(memory_space=pltpu.VMEM)],
        input_output_aliases={i: 2 * n + i for i in range(n)},
        compiler_params=pltpu.CompilerParams(has_side_effects=DATAFLOW),
    )(*[_hbm(s) for s in lands])
    return out[:n], out[n:2 * n], out[2 * n:3 * n], out[3 * n]


def _forward_wait(name, lands, send, recv, after):
    n = len(lands)

    def body(*refs):
        land = refs[:n]
        snd, rcv = refs[n:2 * n], refs[2 * n:3 * n]
        x, y, c = _place()
        for a in range(n):
            for r in SAME_CORE:
                cp = pltpu.make_async_remote_copy(
                    src_ref=land[a].at[_slot(*_peer(x, y, c, r))], dst_ref=land[a].at[_slot(*_peer(x, y, c, r | 1))],
                    send_sem=snd[a].at[r - 1], recv_sem=rcv[a].at[r - 1], device_id=(x, y, 1 - c),
                    device_id_type=MESH)
                cp.wait_send()
                cp.wait_recv()

    return pl.pallas_call(
        body, name=name,
        out_shape=[pltpu.HBM(s.shape, s.dtype) for s in lands],
        in_specs=[HBM_SPEC] * n + [SEM_SPEC] * (2 * n) + [ANY_SPEC],
        out_specs=[HBM_SPEC] * n,
        input_output_aliases={i: i for i in range(n)},
        compiler_params=pltpu.CompilerParams(has_side_effects=DATAFLOW),
    )(*lands, *send, *recv, after)


def _adamw(w, g, m, v):
    m2 = ADAM_B1 * m + (1.0 - ADAM_B1) * g
    v2 = ADAM_B2 * v + (1.0 - ADAM_B2) * (g * g)
    m_hat = m2 / (1.0 - ADAM_B1 ** ADAM_STEP)
    v_hat = v2 / (1.0 - ADAM_B2 ** ADAM_STEP)
    delta = -ADAM_LR * (m_hat / (jnp.sqrt(v_hat) + ADAM_EPS) + ADAM_WD * w)
    return delta, m2, v2


def _adamw_big(name, slots, w, m, v, own=None):
    R, C = w.shape
    tr = next((t for t in (256, 352) if R % t == 0), R)

    def finish(g, w_ref, m_ref, v_ref, g_ref, d_ref, m2_ref, v2_ref):
        d, m2, v2 = _adamw(w_ref[...], g, m_ref[...], v_ref[...])
        g_ref[...] = g
        d_ref[...] = d
        m2_ref[...] = m2
        v2_ref[...] = v2

    if own is None:
        def body(s_ref, *refs):
            g = s_ref[0].astype(F32)
            for k in range(1, N_DEV):
                g = g + s_ref[k].astype(F32)
            finish(g, *refs)

        row = pl.BlockSpec((tr, C), lambda i: (i, 0))
        return pl.pallas_call(
            body, name=name, grid=(R // tr,),
            in_specs=[pl.BlockSpec((N_DEV, tr, C), lambda i: (0, i, 0)), row, row, row],
            out_specs=[row] * 4, out_shape=[S((R, C), F32)] * 4,
            compiler_params=_cp(1))(slots, w, m, v)

    def body(my_ref, s_ref, own_ref, *refs):
        mine = own_ref[...]
        g = None
        for k in range(N_DEV):
            part = jnp.where(my_ref[0] == k, mine, s_ref[k]).astype(F32)
            g = part if g is None else g + part
        finish(g, *refs)

    row = pl.BlockSpec((tr, C), lambda i, my_ref: (i, 0))
    my = jnp.reshape(_slot(*_place()), (1,)).astype(jnp.int32)
    return pl.pallas_call(
        body, name=name,
        grid_spec=pltpu.PrefetchScalarGridSpec(
            num_scalar_prefetch=1, grid=(R // tr,),
            in_specs=[pl.BlockSpec((N_DEV, tr, C), lambda i, my_ref: (0, i, 0)),
                      pl.BlockSpec((None, tr, C), lambda i, my_ref: (my_ref[0], i, 0)), row, row, row],
            out_specs=[row] * 4),
        out_shape=[S((R, C), F32)] * 4, compiler_params=_cp(1))(my, slots, own, w, m, v)


TINY_ROWS = (("b_s", 8), ("g_ffn1", 8), ("g_mix", 8), ("g_ca", 8), ("g_mem", 8), ("g_ffn2", 8), ("g_sgu", 4),
             ("g_fox_o", 4), ("g_gmlp_o", 4), ("g_cq", 2), ("g_ck", 2), ("g_q", 1), ("g_k", 1), ("b_f", 1),
             ("loss", 1))
TINY_P = 72


def _tiny_pieces(width):
    return [(j, slice(j * LANES, min((j + 1) * LANES, width))) for j in range(-(-width // LANES))]


def _pack_tiny(grads, sq):
    names = [n for n, _ in TINY_ROWS if n != "loss"]

    def body(*refs):
        ins, sq_ref, o_ref = refs[:len(names)], refs[len(names)], refs[len(names) + 1]
        o_ref[...] = jnp.zeros_like(o_ref)
        at = 0
        for ref, (name, r) in zip(ins, TINY_ROWS):
            if name == "b_s":
                o_ref[at:at + r, :] = ref[...]
            else:
                for j, cols in _tiny_pieces(ref.shape[1]):
                    o_ref[at + j:at + j + 1, 0:cols.stop - cols.start] = ref[:, cols]
            at += r
        o_ref[at:at + 1, :] = sq_ref[0:1, :]

    return pl.pallas_call(body, name="tiny_pack", out_shape=S((TINY_P, LANES), F32))(
        *[grads[n] for n in names], sq)


def _adamw_tiny(slots, w, m, v):
    names = [n for n, _ in TINY_ROWS if n != "loss"]
    k = len(names)

    def body(s_ref, *refs):
        ins, outs, loss_ref = refs[:3 * k], refs[3 * k:7 * k], refs[7 * k]
        g_all = s_ref[0]
        for d in range(1, N_DEV):
            g_all = g_all + s_ref[d]
        at = 0
        for i, (name, r) in enumerate(TINY_ROWS[:k]):
            w_ref, m_ref, v_ref = ins[i], ins[k + i], ins[2 * k + i]
            o = outs[4 * i:4 * i + 4]
            if name == "b_s":
                pieces = [(slice(at, at + r), slice(0, LANES), (slice(None), slice(None)))]
            else:
                pieces = [(slice(at + j, at + j + 1), slice(0, c.stop - c.start), (slice(None), c))
                          for j, c in _tiny_pieces(w_ref.shape[1])]
            for rows, lanes, dst in pieces:
                g = g_all[rows, lanes]
                res = (g,) + _adamw(w_ref[dst], g, m_ref[dst], v_ref[dst])
                for ref, val in zip(o, res):
                    ref[dst] = val
            at += r
        loss_ref[...] = g_all[at:at + 1, :]

    shapes = [S(w[n].shape, F32) for n in names]
    out = pl.pallas_call(
        body, name="adamw_tiny", out_shape=[s for s in shapes for _ in range(4)] + [S((1, LANES), F32)],
    )(slots, *[w[n] for n in names], *[m[n] for n in names], *[v[n] for n in names])
    stores = ({}, {}, {}, {})
    for i, n in enumerate(names):
        for store, t in zip(stores, out[4 * i:4 * i + 4]):
            store[n] = t
    return stores, out[4 * k]


WEIGHTS =('g_ffn1', 'w_ffn1_in', 'w_ffn1_out', 'g_mix', 'w_in', 'b_f', 'g_q', 'g_k', 'g_sgu', 'w_s', 'b_s',
           'g_fox_o', 'g_gmlp_o', 'w_out', 'g_ca', 'g_mem', 'w_cq', 'w_ckv', 'g_cq', 'g_ck', 'w_co', 'g_ffn2',
           'w_ffn2_in', 'w_ffn2_out')
BIG = ('w_ffn1_in', 'w_ffn1_out', 'w_in', 'w_out', 'w_cq', 'w_ckv', 'w_co', 'w_ffn2_in', 'w_ffn2_out')
TRANSPOSED = ('w_ffn1_in', 'w_in', 'w_ffn2_in')
TWO_LEVEL = ('w_ffn1_in', 'w_in')
GATHER_GROUPS = {"ffn1_up": ("w_ffn1_in",), "ffn1_dn": ("w_ffn1_out",), "mix": ("w_in", "w_out"),
                 "ca": ("w_cq", "w_ckv", "w_co"), "ffn2": ("w_ffn2_in", "w_ffn2_out")}
QKV_W = 3 * FOX_W
UV_OFF = QKV_W + FOX_HEADS


def kernel(x, mem, g_ffn1, w_ffn1_in, w_ffn1_out, g_mix, w_in, b_f, g_q, g_k, g_sgu, w_s, b_s, g_fox_o, g_gmlp_o, w_out, g_ca, g_mem, w_cq, w_ckv, g_cq, g_ck, w_co, g_ffn2, w_ffn2_in, w_ffn2_out, loss_target, m_g_ffn1, m_w_ffn1_in, m_w_ffn1_out, m_g_mix, m_w_in, m_b_f, m_g_q, m_g_k, m_g_sgu, m_w_s, m_b_s, m_g_fox_o, m_g_gmlp_o, m_w_out, m_g_ca, m_g_mem, m_w_cq, m_w_ckv, m_g_cq, m_g_ck, m_w_co, m_g_ffn2, m_w_ffn2_in, m_w_ffn2_out, v_g_ffn1, v_w_ffn1_in, v_w_ffn1_out, v_g_mix, v_w_in, v_b_f, v_g_q, v_g_k, v_g_sgu, v_w_s, v_b_s, v_g_fox_o, v_g_gmlp_o, v_w_out, v_g_ca, v_g_mem, v_w_cq, v_w_ckv, v_g_cq, v_g_ck, v_w_co, v_g_ffn2, v_w_ffn2_in, v_w_ffn2_out):
    args = dict(locals())
    w = {n: args[n] for n in WEIGHTS}
    mo = {n: args["m_" + n] for n in WEIGHTS}
    vo = {n: args["v_" + n] for n in WEIGHTS}
    D = D_MODEL

    def local(n, a):
        return a[0].T if n in TRANSPOSED else a[0]

    g_peers = [NEAR_PEERS if n in TWO_LEVEL else ALL_PEERS for n in BIG]
    handles = {}

    def start_gather(name, names, arrays):
        snd, rcv, src, land, token = _copy_start(name, arrays, _place_own(arrays, True), True,
                                                 peers=[g_peers[BIG.index(n)] for n in names])
        handles.update({n: (src[i], land[i], snd[i], rcv[i]) for i, n in enumerate(names)})
        return token

    first = local(BIG[0], w[BIG[0]]).astype(BF)
    fb = first.shape[0]
    token_first = start_gather("gather_start_first", BIG[:1], [first])
    token_rest = start_gather("gather_start_rest", BIG[1:],
                              [(local(n, w[n]) + token_first[0:1, 0:1]).astype(BF) for n in BIG[1:]])

    tiny_names = [n for n, _ in TINY_ROWS if n != "loss"]

    def weights(group, after):
        names = GATHER_GROUPS[group]
        hs = [handles[n] for n in names]
        got = list(_copy_wait("gather_wait_" + group, [h[0] for h in hs], [h[1] for h in hs], [h[2] for h in hs],
                              [h[3] for h in hs], token_rest if group == "ffn1_up" else after, True,
                              peers=[g_peers[BIG.index(n)] for n in names]))
        passed = [i for i, n in enumerate(names) if n in TWO_LEVEL]
        if passed:
            f_snd, f_rcv, f_land, f_token = _forward_start("gather_pass_start_" + group, [got[i] for i in passed])
            for i, t in zip(passed, _forward_wait("gather_pass_wait_" + group, f_land, f_snd, f_rcv, f_token)):
                got[i] = t
        got = dict(zip(names, got))
        if group == "ffn1_up":
            return {"wup1": got["w_ffn1_in"].reshape(2, N_FFN_BLK, fb, D)}
        if group == "ffn1_dn":
            return {"wdn1": got["w_ffn1_out"].reshape(N_FFN_BLK, fb, D)}
        if group == "mix":
            full = got["w_in"].reshape(-1, D)
            wz = jnp.concatenate([full[:QKV_W], full[UV_OFF:], full[QKV_W:UV_OFF],
                                  jnp.zeros((LANES - FOX_HEADS, D), BF)], axis=0)
            return {"wz": wz, "wout": got["w_out"].reshape(D, D)}
        if group == "ca":
            return {"wcq": got["w_cq"].reshape(D, D), "wco": got["w_co"].reshape(D, D), "wckv": got["w_ckv"]}
        return {"wup2": got["w_ffn2_in"].reshape(2, N_FFN_BLK, fb, D),
                "wdn2": got["w_ffn2_out"].reshape(N_FFN_BLK, fb, D)}

    flying = {}

    def emit(group, g):
        if group == "w_s":
            part = [g["w_s"].reshape(-1, LANES)]
            *copies, token = _copy_start("w_s_start", part, _place_own(part, True), True)
            flying[group] = copies
            return token
        if group == "ffn2":
            parts = {"w_ffn2_in": g["wup2"], "w_ffn2_out": g["wdn2"].reshape(N_DEV, -1, D)}
        elif group == "ffn1_dn":
            parts = {"w_ffn1_out": g["wdn1"].reshape(N_DEV, -1, D)}
        elif group == "ffn1_up":
            parts = {"w_ffn1_in": g["wup1"]}
        else:
            gz = g["wz"]
            g_in = jnp.concatenate([gz[:QKV_W], gz[Z_F:Z_F + FOX_HEADS], gz[QKV_W:Z_F]], axis=0)
            parts = {"w_in": g_in.reshape(N_DEV, -1, D).astype(BF),
                     "w_out": g["wout"].reshape(N_DEV, -1, D), "w_cq": g["wcq"].reshape(N_DEV, -1, D),
                     "w_co": g["wco"].reshape(N_DEV, -1, D), "w_ckv": g["wckv"]}
        names = list(parts)
        srcs = [parts[n] for n in names]
        *copies, token = _copy_start("exchange_start_" + group, srcs, [lax.empty(s.shape, s.dtype) for s in srcs],
                                     False)
        flying[group] = (names, copies)
        return token

    small = {n: (w[n][0] if n == "b_s" else w[n]) for n in tiny_names}
    small["w_s"] = w["w_s"][0]

    sq, dx0, gs = _local_step(x[0], mem[0], loss_target[0], small, weights, emit)

    sm_parts = [_pack_tiny(gs, sq)]
    sm_snd, sm_rcv, sm_src, sm_land, sm_token = _copy_start("tiny_start", sm_parts, _place_own(sm_parts, True), True)

    grad, delta, new_m, new_v = {}, {}, {}, {}

    def update(group, after):
        names, (snd, rcv, srcs, lands) = flying[group]
        owns, slots = _copy_wait("exchange_wait_" + group, srcs, lands, snd, rcv, after, False, with_srcs=True)
        for n, sl, own in zip(names, slots, owns):
            g, d, m2, v2 = _adamw_big("adamw_" + n, sl, local(n, w[n]), local(n, mo[n]), local(n, vo[n]), own=own)
            grad[n], delta[n], new_m[n], new_v[n] = (
                (t.T if n in TRANSPOSED else t).reshape(w[n].shape) for t in (g, d, m2, v2))
        return d

    last = update("ffn2", sm_token)
    last = update("mid", last)
    last = update("ffn1_dn", last)
    last = update("ffn1_up", last)
    ws_snd, ws_rcv, ws_src, ws_land = flying["w_s"]
    ws_all, = _copy_wait("w_s_wait", ws_src, ws_land, ws_snd, ws_rcv, last, True)
    tiny_all, = _copy_wait("tiny_wait", sm_src, sm_land, sm_snd, sm_rcv, ws_all, True)
    ws_shape = w["w_s"].shape
    for store, t in zip((grad, delta, new_m, new_v), _adamw_big(
            "adamw_w_s", ws_all, *[a["w_s"].reshape(-1, LANES) for a in (w, mo, vo)])):
        store["w_s"] = t.reshape(ws_shape)
    stores, loss_row = _adamw_tiny(tiny_all, *[{n: (a[n][0] if n == "b_s" else a[n]) for n in tiny_names}
                                               for a in (w, mo, vo)])
    for store, t in zip((grad, delta, new_m, new_v), stores):
        store.update({n: v.reshape(w[n].shape) for n, v in t.items()})
    loss = loss_row[0, 0] * (0.5 / D)

    return (loss, dx0[None], *[grad[n] for n in WEIGHTS], *[delta[n] for n in WEIGHTS],
            *[new_m[n] for n in WEIGHTS], *[new_v[n] for n in WEIGHTS])
```

```python
import functools

import jax
import jax.numpy as jnp
from jax import lax
from jax.experimental import pallas as pl
from jax.experimental.pallas import tpu as pltpu

F32 = jnp.float32
BF = jnp.bfloat16
S = jax.ShapeDtypeStruct

N_DEV = 8
D_MODEL = 1024
FOX_HEADS, FOX_HD = 8, 64
FOX_W = 512
GMLP_G, GMLP_GD = 8, 64
GMLP_W = 512
CHUNK = 128
CA_HEADS, CA_HD = 4, 256
N_FFN_BLK = 4
ZW = 2688
Z_Q, Z_K, Z_V, Z_U, Z_G, Z_F = 0, 512, 1024, 1536, 2048, 2560
EPS = 1e-6
NEG = -1e30
LANES = 128

ADAM_LR, ADAM_B1, ADAM_B2, ADAM_EPS, ADAM_WD, ADAM_STEP = 0.001, 0.9, 0.999, 1e-08, 0.01, 10

VMEM_LIMIT = 52 * 2 ** 20


def _cp(n_axes):
    return pltpu.CompilerParams(dimension_semantics=("arbitrary",) * n_axes, vmem_limit_bytes=VMEM_LIMIT)


def _nn(a, b):
    return jnp.dot(a, b, preferred_element_type=F32)


def _nt(a, b):
    return lax.dot_general(a, b, (((1,), (1,)), ((), ())), preferred_element_type=F32)


def _tn(a, b):
    return lax.dot_general(a, b, (((0,), (0,)), ((), ())), preferred_element_type=F32)


def _hi(a, b):
    return jnp.dot(a, b, precision=lax.Precision.HIGHEST, preferred_element_type=F32)


def _rstd(x):
    return lax.rsqrt(jnp.mean(x * x, axis=-1, keepdims=True) + EPS)


def _norm_bwd(dy, x, g, r=None):
    r = _rstd(x) if r is None else r
    xh = x * r
    dxh = dy * g
    dx = r * (dxh - xh * jnp.mean(dxh * xh, axis=-1, keepdims=True))
    return dx, dy * xh


def _acc_rows(ref, first, val):
    srow = jnp.sum(val, axis=0, keepdims=True)

    @pl.when(first)
    def _():
        ref[...] = srow

    @pl.when(jnp.logical_not(first))
    def _():
        ref[...] += srow


def _gelu(x):
    c = 0.7978845608028654
    return 0.5 * x * (1.0 + jnp.tanh(c * (x + 0.044715 * x * x * x)))


def _gelu_grad(x):
    c = 0.7978845608028654
    t = jnp.tanh(c * (x + 0.044715 * x * x * x))
    return 0.5 * (1.0 + t) + 0.5 * x * (1.0 - t * t) * c * (1.0 + 3 * 0.044715 * x * x)


def _tile(n, pref):
    return pref if n % pref == 0 else n


def _ffn_up(name, x, g, wup):
    T, D = x.shape
    FB = wup.shape[-2]
    tm = _tile(T, 1024)

    def body(x_ref, g_ref, w_ref, a_ref, h_ref):
        @pl.when(pl.program_id(1) == 0)
        def _():
            xf = x_ref[...]
            h_ref[...] = (xf * _rstd(xf) * g_ref[...]).astype(BF)

        hb = h_ref[...]
        gg = _nt(hb, w_ref[0])
        uu = _nt(hb, w_ref[1])
        a_ref[...] = (gg * jax.nn.sigmoid(gg) * uu).astype(BF)

    return pl.pallas_call(
        body, name=name, grid=(T // tm, N_FFN_BLK),
        in_specs=[pl.BlockSpec((tm, D), lambda i, j: (i, 0)),
                  pl.BlockSpec((1, D), lambda i, j: (0, 0)),
                  pl.BlockSpec((2, None, FB, D), lambda i, j: (0, j, 0, 0))],
        out_specs=[pl.BlockSpec((None, tm, FB), lambda i, j: (j, i, 0)),
                   pl.BlockSpec((tm, D), lambda i, j: (i, 0))],
        out_shape=[S((N_FFN_BLK, T, FB), BF), S((T, D), BF)],
        compiler_params=_cp(2))(x, g, wup)


def _ffn_down(name, a, wdn, x):
    _, T, FB = a.shape
    D = x.shape[1]
    tm = _tile(T, 512)

    def body(a_ref, w_ref, x_ref, o_ref):
        p = _nn(a_ref[0], w_ref[0])
        for j in range(1, N_FFN_BLK):
            p = p + _nn(a_ref[j], w_ref[j])
        o_ref[...] = x_ref[...] + 0.5 * p

    return pl.pallas_call(
        body, name=name, grid=(T // tm,),
        in_specs=[pl.BlockSpec((N_FFN_BLK, tm, FB), lambda i: (0, i, 0)),
                  pl.BlockSpec((N_FFN_BLK, FB, D), lambda i: (0, 0, 0)),
                  pl.BlockSpec((tm, D), lambda i: (i, 0))],
        out_specs=pl.BlockSpec((tm, D), lambda i: (i, 0)),
        out_shape=S((T, D), F32),
        compiler_params=_cp(1))(a, wdn, x)


def _ffn_down_loss(name, a, wdn, x, target):
    _, T, FB = a.shape
    D = x.shape[1]
    tm = _tile(T, 512)

    def body(a_ref, w_ref, x_ref, t_ref, d_ref, db_ref, loss_ref):
        i = pl.program_id(0)
        p = _nn(a_ref[0], w_ref[0])
        for j in range(1, N_FFN_BLK):
            p = p + _nn(a_ref[j], w_ref[j])
        diff = (x_ref[...] + 0.5 * p) - t_ref[...]
        dy = diff * (1.0 / D)
        d_ref[...] = dy
        db_ref[...] = dy.astype(BF)
        sq = jnp.zeros((8, LANES), F32) + jnp.sum(diff * diff)

        @pl.when(i == 0)
        def _():
            loss_ref[...] = sq

        @pl.when(i > 0)
        def _():
            loss_ref[...] += sq

    row = pl.BlockSpec((tm, D), lambda i: (i, 0))
    return pl.pallas_call(
        body, name=name, grid=(T // tm,),
        in_specs=[pl.BlockSpec((N_FFN_BLK, tm, FB), lambda i: (0, i, 0)),
                  pl.BlockSpec((N_FFN_BLK, FB, D), lambda i: (0, 0, 0)), row, row],
        out_specs=[row, row, pl.BlockSpec((8, LANES), lambda i: (0, 0))],
        out_shape=[S((T, D), F32), S((T, D), BF), S((8, LANES), F32)],
        compiler_params=_cp(1))(a, wdn, x, target)


def _ffn_bwd_act(name, dyb, h, wup, wdn):
    T, D = h.shape
    FB = wup.shape[-2]
    tm = _tile(T, 1024)

    def body(d_ref, h_ref, wu_ref, wd_ref, o_ref):
        da = 0.5 * _nt(d_ref[...], wd_ref[...])
        hb = h_ref[...]
        gg = _nt(hb, wu_ref[0])
        uu = _nt(hb, wu_ref[1])
        sg = jax.nn.sigmoid(gg)
        o_ref[0] = (da * uu * (sg * (1.0 + gg * (1.0 - sg)))).astype(BF)
        o_ref[1] = (da * (gg * sg)).astype(BF)

    return pl.pallas_call(
        body, name=name, grid=(T // tm, N_FFN_BLK),
        in_specs=[pl.BlockSpec((tm, D), lambda i, j: (i, 0)),
                  pl.BlockSpec((tm, D), lambda i, j: (i, 0)),
                  pl.BlockSpec((2, None, FB, D), lambda i, j: (0, j, 0, 0)),
                  pl.BlockSpec((None, FB, D), lambda i, j: (j, 0, 0))],
        out_specs=pl.BlockSpec((2, None, tm, FB), lambda i, j: (0, j, i, 0)),
        out_shape=S((2, N_FFN_BLK, T, FB), BF),
        compiler_params=_cp(2))(dyb, h, wup, wdn)


def _ffn_dx(name, dgu, wup, x, g, dy):
    T, D = x.shape
    FB = wup.shape[-2]
    tm = _tile(T, 512)

    def body(d_ref, w_ref, x_ref, g_ref, dy_ref, dx_ref, dg_ref):
        p = None
        for j in range(N_FFN_BLK):
            for half in range(2):
                t = _nn(d_ref[half, j], w_ref[half, j])
                p = t if p is None else p + t
        dx, dgr = _norm_bwd(p, x_ref[...], g_ref[...])
        dx_ref[...] = dx + dy_ref[...]
        _acc_rows(dg_ref, pl.program_id(0) == 0, dgr)

    return pl.pallas_call(
        body, name=name, grid=(T // tm,),
        in_specs=[pl.BlockSpec((2, N_FFN_BLK, tm, FB), lambda i: (0, 0, i, 0)),
                  pl.BlockSpec((2, N_FFN_BLK, FB, D), lambda i: (0, 0, 0, 0), pipeline_mode=pl.Buffered(1)),
                  pl.BlockSpec((tm, D), lambda i: (i, 0)),
                  pl.BlockSpec((1, D), lambda i: (0, 0)),
                  pl.BlockSpec((tm, D), lambda i: (i, 0))],
        out_specs=[pl.BlockSpec((tm, D), lambda i: (i, 0)),
                   pl.BlockSpec((1, D), lambda i: (0, 0))],
        out_shape=[S((T, D), F32), S((1, D), F32)],
        compiler_params=_cp(1))(dgu, wup, x, g, dy)


def _tn_matmul(name, a, a_spec, b, out_shape, out_spec, n_blocks, scale=1.0, after=None):
    extra = [] if after is None else [after]

    def body(a_ref, b_ref, *rest):
        o_ref = rest[-1]
        o_ref[...] = (_tn(a_ref[...], b_ref[...]) * scale).astype(o_ref.dtype)

    return pl.pallas_call(
        body, name=name, grid=(n_blocks,),
        in_specs=[a_spec, pl.BlockSpec(b.shape, lambda j: (0, 0), pipeline_mode=pl.Buffered(1))]
        + [pl.BlockSpec((8, LANES), lambda j: (0, 0)) for _ in extra],
        out_specs=out_spec, out_shape=out_shape, compiler_params=_cp(1))(a, b, *extra)


def _ffn_dwup(name, h, dgu, after=None):
    T, D = h.shape
    FB = dgu.shape[-1]
    return _tn_matmul(
        name + "_dwup", dgu.reshape(2 * N_FFN_BLK, T, FB), pl.BlockSpec((None, T, FB), lambda j: (j, 0, 0)), h,
        S((2 * N_FFN_BLK, FB, D), BF), pl.BlockSpec((None, FB, D), lambda j: (j, 0, 0)), 2 * N_FFN_BLK,
        after=after)


def _ffn_dwdn(name, a, dyb):
    _, T, FB = a.shape
    D = dyb.shape[1]
    return _tn_matmul(
        name + "_dwdn", a, pl.BlockSpec((None, T, FB), lambda j: (j, 0, 0)), dyb,
        S((N_FFN_BLK, FB, D), BF), pl.BlockSpec((None, FB, D), lambda j: (j, 0, 0)), N_FFN_BLK, scale=0.5)


def _tri(n, lower):
    r = lax.broadcasted_iota(jnp.int32, (n, n), 0)
    c = lax.broadcasted_iota(jnp.int32, (n, n), 1)
    return (r >= c) if lower else (r <= c)


def _spatial_mix(vgn_b, ws_ref, bst, tm):
    tril = _tri(CHUNK, True)
    wms = [jnp.where(tril, ws_ref[g], 0.0).astype(BF) for g in range(GMLP_G)]
    rows = []
    for c in range(tm // CHUNK):
        cols = []
        for g in range(GMLP_G):
            vs = vgn_b[c * CHUNK:(c + 1) * CHUNK, g * GMLP_GD:(g + 1) * GMLP_GD]
            cols.append(_nn(wms[g], vs) + bst[:, g:g + 1])
        rows.append(jnp.concatenate(cols, axis=1))
    return jnp.concatenate(rows, axis=0), wms


HB = 128
AUG_W = FOX_HEADS * HB
COL_A, COL_B, COL_C = 64, 67, 70
RS_Q, RS_K, RS_V, RS_O = 0, 8, 16, 17


def _piece_matrix(col):
    r = jnp.arange(LANES)
    dst = jnp.where(r < 3 * FOX_HEADS, (r % FOX_HEADS) * HB + col + r // FOX_HEADS, -1)
    return (jnp.arange(AUG_W)[None, :] == dst[:, None]).astype(BF)


def _ones_row(cols):
    c = jnp.arange(AUG_W) % HB
    hit = functools.reduce(jnp.logical_or, [(c >= a) & (c < a + 3) for a in cols])
    return hit.astype(F32)[None, :]


def _pieces(x):
    lane = lax.broadcasted_iota(jnp.int32, x.shape, 1)
    x = jnp.where(lane < FOX_HEADS, x, 0.0)
    hi = x.astype(BF).astype(F32)
    r1 = x - hi
    mid = r1.astype(BF).astype(F32)
    lo = (r1 - mid).astype(BF).astype(F32)
    return (hi + pltpu.roll(mid, FOX_HEADS, 1) + pltpu.roll(lo, 2 * FOX_HEADS, 1)).astype(BF)


def _mix_prep(x, g_mix, wz, bf128, g_q, g_k, g_sgu, w_s, b_st, g_go):
    T, D = x.shape
    tm = _tile(T, 512)
    pc_q, pc_k = _piece_matrix(COL_A), _piece_matrix(COL_B)
    one_q, one_k, one_v = _ones_row([COL_B]), _ones_row([COL_A, COL_C]), _ones_row([COL_A])

    def body(x_ref, gm_ref, wz_ref, bf_ref, gq_ref, gk_ref, gs_ref, ws_ref, bst_ref, go_ref, pq_ref, pk_ref, oq_ref,
             ok_ref, ov_ref, z_ref, h_ref, q_ref, k_ref, v_ref, y_ref, rs_ref, carry_ref):
        i = pl.program_id(0)

        @pl.when(i == 0)
        def _():
            carry_ref[...] = jnp.zeros_like(carry_ref)

        xf = x_ref[...]
        hb = (xf * _rstd(xf) * gm_ref[...]).astype(BF)
        h_ref[...] = hb
        z_ref[...] = _nt(hb, wz_ref[...])

        fl = z_ref[:, Z_F:Z_F + LANES] + bf_ref[...]
        logf = jnp.minimum(fl, 0.0) - jnp.log1p(jnp.exp(-jnp.abs(fl)))
        csum = _hi(_tri(tm, True).astype(F32), logf) + carry_ref[...]
        carry_ref[...] = csum[tm - 1:tm, :]
        ext_q = (_nn(_pieces(csum), pq_ref[...]) + oq_ref[...]).astype(BF)
        ext_k = (_nn(_pieces(-csum), pk_ref[...]) + ok_ref[...]).astype(BF)
        ext_v = jnp.broadcast_to(ov_ref[...], (tm, AUG_W)).astype(BF)

        rs_ref[...] = jnp.zeros_like(rs_ref)
        for h in range(FOX_HEADS):
            lo, hi = slice(h * HB, h * HB + FOX_HD), slice(h * HB + FOX_HD, (h + 1) * HB)
            qh = z_ref[:, Z_Q + h * FOX_HD:Z_Q + (h + 1) * FOX_HD]
            kh = z_ref[:, Z_K + h * FOX_HD:Z_K + (h + 1) * FOX_HD]
            rq, rk = _rstd(qh), _rstd(kh)
            rs_ref[:, RS_Q + h:RS_Q + h + 1] = rq
            rs_ref[:, RS_K + h:RS_K + h + 1] = rk
            q_ref[:, lo] = (qh * rq * gq_ref[...] * 0.125).astype(BF)
            k_ref[:, lo] = (kh * rk * gk_ref[...]).astype(BF)
            v_ref[:, lo] = z_ref[:, Z_V + h * FOX_HD:Z_V + (h + 1) * FOX_HD].astype(BF)
            q_ref[:, hi] = ext_q[:, hi]
            k_ref[:, hi] = ext_k[:, hi]
            v_ref[:, hi] = ext_v[:, hi]

        u = _gelu(z_ref[:, Z_U:Z_U + GMLP_W])
        vg = _gelu(z_ref[:, Z_G:Z_G + GMLP_W])
        rv = _rstd(vg)
        vgn = (vg * rv * gs_ref[...]).astype(BF)
        mixed, _ = _spatial_mix(vgn, ws_ref, bst_ref[...], tm)
        sgu = u * mixed
        ro = _rstd(sgu)
        y_ref[...] = (sgu * ro * go_ref[...]).astype(BF)
        rs_ref[:, RS_V:RS_V + 1] = rv
        rs_ref[:, RS_O:RS_O + 1] = ro

    row = lambda i: (i, 0)
    fix2 = lambda i: (0, 0)
    return pl.pallas_call(
        body, name="mix_prep", grid=(T // tm,),
        in_specs=[pl.BlockSpec((tm, D), row), pl.BlockSpec((1, D), fix2),
                  pl.BlockSpec((ZW, D), fix2, pipeline_mode=pl.Buffered(1)),
                  pl.BlockSpec((1, LANES), fix2), pl.BlockSpec((1, FOX_HD), fix2), pl.BlockSpec((1, FOX_HD), fix2),
                  pl.BlockSpec((1, GMLP_W), fix2), pl.BlockSpec((GMLP_G, CHUNK, CHUNK), lambda i: (0, 0, 0)),
                  pl.BlockSpec((CHUNK, GMLP_G), fix2), pl.BlockSpec((1, GMLP_W), fix2),
                  pl.BlockSpec((LANES, AUG_W), fix2),
                  pl.BlockSpec((LANES, AUG_W), fix2), pl.BlockSpec((1, AUG_W), fix2), pl.BlockSpec((1, AUG_W), fix2),
                  pl.BlockSpec((1, AUG_W), fix2)],
        out_specs=[pl.BlockSpec((tm, ZW), row), pl.BlockSpec((tm, D), row),
                   pl.BlockSpec((tm, AUG_W), row), pl.BlockSpec((tm, AUG_W), row), pl.BlockSpec((tm, AUG_W), row),
                   pl.BlockSpec((tm, GMLP_W), row), pl.BlockSpec((tm, LANES), row)],
        out_shape=[S((T, ZW), F32), S((T, D), BF), S((T, AUG_W), BF), S((T, AUG_W), BF), S((T, AUG_W), BF),
                   S((T, GMLP_W), BF), S((T, LANES), F32)],
        scratch_shapes=[pltpu.VMEM((1, LANES), F32)],
        compiler_params=_cp(1))(x, g_mix, wz, bf128, g_q, g_k, g_sgu, w_s, b_st, g_go, pc_q, pc_k, one_q, one_k,
                                one_v)


def _fox_fwd(q, k, v):
    T = q.shape[0]
    tq = _tile(T, 1024)
    nq = T // tq

    def body(q_ref, k_ref, v_ref, o_ref, lse_ref, m_sc, acc_sc):
        i, j = pl.program_id(0), pl.program_id(1)

        @pl.when(j == 0)
        def _():
            m_sc[...] = jnp.full(m_sc.shape, NEG, F32)
            acc_sc[...] = jnp.zeros_like(acc_sc)

        def step(masked):
            mask = _tri(tq, True) if masked else None
            for h in range(FOX_HEADS):
                hb = slice(h * HB, (h + 1) * HB)
                s = _nt(q_ref[:, hb], k_ref[:, hb])
                if masked:
                    s = jnp.where(mask, s, NEG)
                m_prev = m_sc[h]
                m_new = jnp.maximum(m_prev, jnp.broadcast_to(jnp.max(s, axis=1, keepdims=True), (tq, HB)))
                p = jnp.exp(s - jnp.tile(m_new, (1, tq // HB))).astype(BF)
                acc_sc[:, hb] = jnp.exp(m_prev - m_new) * acc_sc[:, hb] + _nn(p, v_ref[:, hb])
                m_sc[h] = m_new

        @pl.when(j < i)
        def _():
            step(False)

        @pl.when(j == i)
        def _():
            step(True)
            lse_ref[...] = jnp.zeros_like(lse_ref)
            for h in range(FOX_HEADS):
                l = acc_sc[:, h * HB + COL_A:h * HB + COL_A + 1]
                o_ref[:, h * FOX_HD:(h + 1) * FOX_HD] = acc_sc[:, h * HB:h * HB + FOX_HD] / l
                lse_ref[:, h:h + 1] = m_sc[h][:, 0:1] + jnp.log(l)

    qi = lambda i, j: (i, 0)
    kj = lambda i, j: (jnp.minimum(i, j), 0)
    return pl.pallas_call(
        body, name="fox_fwd", grid=(nq, nq),
        in_specs=[pl.BlockSpec((tq, AUG_W), qi), pl.BlockSpec((tq, AUG_W), kj), pl.BlockSpec((tq, AUG_W), kj)],
        out_specs=[pl.BlockSpec((tq, FOX_W), qi), pl.BlockSpec((tq, LANES), qi)],
        out_shape=[S((T, FOX_W), F32), S((T, LANES), F32)],
        scratch_shapes=[pltpu.VMEM((FOX_HEADS, tq, HB), F32), pltpu.VMEM((tq, AUG_W), F32)],
        compiler_params=_cp(2))(q, k, v)


def _fox_bwd(q, k, v, dob):
    T = q.shape[0]
    tq = _tile(T, 512)
    nq = T // tq
    n_sweeps = 1
    half = AUG_W // n_sweeps
    hpg = FOX_HEADS // n_sweeps

    pairs = [(j, i) for j in range(nq) for i in range(j, nq)]
    jt = jnp.asarray([p[0] for p in pairs], jnp.int32)
    it = jnp.asarray([p[1] for p in pairs], jnp.int32)

    def body(jt_ref, it_ref, q_ref, k_ref, v_ref, do_ref, dq_ref, dk_ref, dv_ref, dq_sc):
        t = pl.program_id(1)
        j, i = jt_ref[t], it_ref[t]

        @pl.when(t == 0)
        def _():
            dq_sc[...] = jnp.zeros_like(dq_sc)

        @pl.when(i == j)
        def _():
            dk_ref[...] = jnp.zeros_like(dk_ref)
            dv_ref[...] = jnp.zeros_like(dv_ref)

        def step(masked):
            rows = pl.ds(pl.multiple_of(i * tq, tq), tq)
            mask = _tri(tq, True) if masked else None
            for h in range(hpg):
                hb = slice(h * HB, (h + 1) * HB)
                qh, kh, vh, doh = q_ref[:, hb], k_ref[:, hb], v_ref[:, hb], do_ref[:, hb]
                s = _nt(qh, kh)
                if masked:
                    s = jnp.where(mask, s, NEG)
                p = jnp.exp(s)
                dsb = (p * _nt(doh, vh)).astype(BF)
                dv_ref[:, hb] += _tn(p.astype(BF), doh)
                dk_ref[:, hb] += _tn(dsb, qh)
                dq_sc[rows, hb] += _nn(dsb, kh)

        @pl.when(i > j)
        def _():
            step(False)

        @pl.when(i == j)
        def _():
            step(True)
            dq_ref[...] = dq_sc[pl.ds(pl.multiple_of(j * tq, tq), tq), :]

    qi = pl.BlockSpec((tq, half), lambda g, t, jt_ref, it_ref: (it_ref[t], g))
    kj = pl.BlockSpec((tq, half), lambda g, t, jt_ref, it_ref: (jt_ref[t], g))
    return pl.pallas_call(
        body, name="fox_bwd",
        grid_spec=pltpu.PrefetchScalarGridSpec(
            num_scalar_prefetch=2, grid=(n_sweeps, len(pairs)), in_specs=[qi, kj, kj, qi], out_specs=[kj, kj, kj],
            scratch_shapes=[pltpu.VMEM((T, half), F32)]),
        out_shape=[S((T, AUG_W), F32), S((T, AUG_W), F32), S((T, AUG_W), F32)],
        compiler_params=_cp(2))(jt, it, q, k, v, dob)


def _mix_out(attn, yg, g_fo, wout, x):
    T, D = x.shape
    tm = _tile(T, 1024)

    def body(a_ref, y_ref, g_ref, w_ref, x_ref, o_ref):
        at = a_ref[...]
        yf = (at * _rstd(at) * g_ref[...]).astype(BF)
        o_ref[...] = x_ref[...] + _nn(yf, w_ref[:FOX_W, :]) + _nn(y_ref[...], w_ref[FOX_W:, :])

    row = lambda i: (i, 0)
    return pl.pallas_call(
        body, name="mix_out", grid=(T // tm,),
        in_specs=[pl.BlockSpec((tm, FOX_W), row), pl.BlockSpec((tm, GMLP_W), row),
                  pl.BlockSpec((1, FOX_W), lambda i: (0, 0)), pl.BlockSpec((D, D), lambda i: (0, 0)),
                  pl.BlockSpec((tm, D), row)],
        out_specs=pl.BlockSpec((tm, D), row),
        out_shape=S((T, D), F32),
        compiler_params=_cp(1))(attn, yg, g_fo, wout, x)


def _mix_out_bwd(dx, attn, yg, g_fo, wout, qf, lse):
    T, D = dx.shape
    tm = _tile(T, 512)
    n = T // tm
    pc_l, pc_d = _piece_matrix(COL_C), _piece_matrix(COL_A)

    def body(dx_ref, a_ref, y_ref, g_ref, w_ref, qf_ref, lse_ref, pl_ref, pd_ref,
             qb_ref, dob_ref, dyg_ref, dw_ref, dg_ref, acc_ref, dsum_ref):
        i = pl.program_id(0)
        dxb = dx_ref[...].astype(BF)
        at = a_ref[...]
        yf = (at * _rstd(at) * g_ref[...]).astype(BF)
        dy = _nt(dxb, w_ref[...])
        p_top = _tn(yf, dxb)
        p_bot = _tn(y_ref[...], dxb)

        @pl.when(i == 0)
        def _():
            acc_ref[:FOX_W, :] = p_top
            acc_ref[FOX_W:, :] = p_bot

        @pl.when(i > 0)
        def _():
            acc_ref[:FOX_W, :] += p_top
            acc_ref[FOX_W:, :] += p_bot

        @pl.when(i == n - 1)
        def _():
            dw_ref[...] = acc_ref[...].astype(BF)

        dat, dgr = _norm_bwd(dy[:, :FOX_W], at, g_ref[...])
        _acc_rows(dg_ref, i == 0, dgr)
        dyg_ref[...] = dy[:, FOX_W:]
        prod = dat * at
        dsum_ref[...] = jnp.zeros_like(dsum_ref)
        for h in range(FOX_HEADS):
            dsum_ref[:, h:h + 1] = jnp.sum(prod[:, h * FOX_HD:(h + 1) * FOX_HD], axis=1, keepdims=True)
        ext_d = _nn(_pieces(-dsum_ref[...]), pd_ref[...]).astype(BF)
        ext_l = _nn(_pieces(-lse_ref[...]), pl_ref[...])
        datb = dat.astype(BF)
        for h in range(FOX_HEADS):
            lo, hi = slice(h * HB, h * HB + FOX_HD), slice(h * HB + FOX_HD, (h + 1) * HB)
            dob_ref[:, lo] = datb[:, h * FOX_HD:(h + 1) * FOX_HD]
            dob_ref[:, hi] = ext_d[:, hi]
            qb_ref[:, lo] = qf_ref[:, lo]
            qb_ref[:, hi] = (qf_ref[:, hi].astype(F32) + ext_l[:, hi]).astype(BF)

    row = lambda i: (i, 0)
    fix = lambda i: (0, 0)
    return pl.pallas_call(
        body, name="mix_out_bwd", grid=(n,),
        in_specs=[pl.BlockSpec((tm, D), row), pl.BlockSpec((tm, FOX_W), row), pl.BlockSpec((tm, GMLP_W), row),
                  pl.BlockSpec((1, FOX_W), fix), pl.BlockSpec((D, D), fix), pl.BlockSpec((tm, AUG_W), row),
                  pl.BlockSpec((tm, LANES), row), pl.BlockSpec((LANES, AUG_W), fix),
                  pl.BlockSpec((LANES, AUG_W), fix)],
        out_specs=[pl.BlockSpec((tm, AUG_W), row), pl.BlockSpec((tm, AUG_W), row), pl.BlockSpec((tm, GMLP_W), row),
                   pl.BlockSpec((D, D), fix), pl.BlockSpec((1, FOX_W), fix)],
        out_shape=[S((T, AUG_W), BF), S((T, AUG_W), BF), S((T, GMLP_W), F32), S((D, D), BF), S((1, FOX_W), F32)],
        scratch_shapes=[pltpu.VMEM((D, D), F32), pltpu.VMEM((tm, LANES), F32)],
        compiler_params=_cp(1))(dx, attn, yg, g_fo, wout, qf, lse, pc_l, pc_d)


def _mix_prep_bwd(z, dq, dk, dv, dyg, rs, bf128, g_q, g_k, g_sgu, w_s, b_st, g_go):
    T = z.shape[0]
    tm = _tile(T, 512)
    n = T // tm

    def body(z_ref, dq_ref, dk_ref, dv_ref, dyg_ref, rs_ref, bf_ref, gq_ref, gk_ref, gs_ref, ws_ref,
             bst_ref, go_ref, dz_ref, dgq_ref, dgk_ref, dgs_ref, dgo_ref, dws_ref, dbst_ref, dbf_ref, carry_ref):
        i = pl.program_id(0)
        first = i == 0
        rs = rs_ref[...]

        @pl.when(first)
        def _():
            carry_ref[...] = jnp.zeros_like(carry_ref)

        lane = lax.broadcasted_iota(jnp.int32, (tm, LANES), 1)
        dc = jnp.zeros((tm, LANES), F32)
        gq_rows, gk_rows = [], []
        for h in range(FOX_HEADS):
            hp = slice(h * HB, h * HB + FOX_HD)
            dqh, gqr = _norm_bwd(dq_ref[:, hp] * 0.125, z_ref[:, Z_Q + h * FOX_HD:Z_Q + (h + 1) * FOX_HD], gq_ref[...],
                                 rs[:, RS_Q + h:RS_Q + h + 1])
            dkh, gkr = _norm_bwd(dk_ref[:, hp], z_ref[:, Z_K + h * FOX_HD:Z_K + (h + 1) * FOX_HD], gk_ref[...],
                                 rs[:, RS_K + h:RS_K + h + 1])
            dz_ref[:, Z_Q + h * FOX_HD:Z_Q + (h + 1) * FOX_HD] = dqh.astype(BF)
            dz_ref[:, Z_K + h * FOX_HD:Z_K + (h + 1) * FOX_HD] = dkh.astype(BF)
            dz_ref[:, Z_V + h * FOX_HD:Z_V + (h + 1) * FOX_HD] = dv_ref[:, hp].astype(BF)
            dch = dq_ref[:, h * HB + COL_A:h * HB + COL_A + 1] - dk_ref[:, h * HB + COL_B:h * HB + COL_B + 1]
            dc = jnp.where(lane == h, dch, dc)
            gq_rows.append(gqr)
            gk_rows.append(gkr)
        _acc_rows(dgq_ref, first, functools.reduce(lambda a, b: a + b, gq_rows))
        _acc_rows(dgk_ref, first, functools.reduce(lambda a, b: a + b, gk_rows))

        dlogf = _hi(_tri(tm, False).astype(F32), dc) + carry_ref[...]
        carry_ref[...] = dlogf[0:1, :]
        fl = z_ref[:, Z_F:Z_F + LANES] + bf_ref[...]
        lane = lax.broadcasted_iota(jnp.int32, (tm, LANES), 1)
        df = jnp.where(lane < FOX_HEADS, dlogf * jax.nn.sigmoid(-fl), 0.0)
        dz_ref[:, Z_F:Z_F + LANES] = df.astype(BF)
        _acc_rows(dbf_ref, first, df)

        u_pre = z_ref[:, Z_U:Z_U + GMLP_W]
        vg_pre = z_ref[:, Z_G:Z_G + GMLP_W]
        u = _gelu(u_pre)
        vg = _gelu(vg_pre)
        rv = rs[:, RS_V:RS_V + 1]
        vgn = (vg * rv * gs_ref[...]).astype(BF)
        bst = bst_ref[...]
        mixed, wms = _spatial_mix(vgn, ws_ref, bst, tm)
        sgu = u * mixed
        dsgu, gor = _norm_bwd(dyg_ref[...], sgu, go_ref[...], rs[:, RS_O:RS_O + 1])
        _acc_rows(dgo_ref, first, gor)
        du = dsgu * mixed
        dmixed = dsgu * u
        dmb = dmixed.astype(BF)
        tril = _tri(CHUNK, True)
        dvgn_rows = []
        dws = [None] * GMLP_G
        dbs = [None] * GMLP_G
        for c in range(tm // CHUNK):
            cs = slice(c * CHUNK, (c + 1) * CHUNK)
            cols = []
            for g in range(GMLP_G):
                gs = slice(g * GMLP_GD, (g + 1) * GMLP_GD)
                dmc = dmb[cs, gs]
                pw = _nt(dmc, vgn[cs, gs])
                pb = jnp.sum(dmixed[cs, gs], axis=1, keepdims=True)
                dws[g] = pw if dws[g] is None else dws[g] + pw
                dbs[g] = pb if dbs[g] is None else dbs[g] + pb
                cols.append(_tn(wms[g], dmc))
            dvgn_rows.append(jnp.concatenate(cols, axis=1))
        dvgn = jnp.concatenate(dvgn_rows, axis=0)
        dbs_t = jnp.concatenate(dbs, axis=1)
        for g in range(GMLP_G):
            dwg = jnp.where(tril, dws[g], 0.0)

            @pl.when(first)
            def _():
                dws_ref[g] = dwg

            @pl.when(jnp.logical_not(first))
            def _():
                dws_ref[g] += dwg

        @pl.when(first)
        def _():
            dbst_ref[...] = dbs_t

        @pl.when(jnp.logical_not(first))
        def _():
            dbst_ref[...] += dbs_t

        dvg, gsr = _norm_bwd(dvgn, vg, gs_ref[...], rv)
        _acc_rows(dgs_ref, first, gsr)
        dz_ref[:, Z_U:Z_U + GMLP_W] = (du * _gelu_grad(u_pre)).astype(BF)
        dz_ref[:, Z_G:Z_G + GMLP_W] = (dvg * _gelu_grad(vg_pre)).astype(BF)

    rev = lambda i: (n - 1 - i, 0)
    fix = lambda i: (0, 0)
    fix3 = lambda i: (0, 0, 0)
    return pl.pallas_call(
        body, name="mix_prep_bwd", grid=(n,),
        in_specs=[pl.BlockSpec((tm, ZW), rev), pl.BlockSpec((tm, AUG_W), rev), pl.BlockSpec((tm, AUG_W), rev),
                  pl.BlockSpec((tm, AUG_W), rev), pl.BlockSpec((tm, GMLP_W), rev), pl.BlockSpec((tm, LANES), rev),
                  pl.BlockSpec((1, LANES), fix), pl.BlockSpec((1, FOX_HD), fix), pl.BlockSpec((1, FOX_HD), fix),
                  pl.BlockSpec((1, GMLP_W), fix), pl.BlockSpec((GMLP_G, CHUNK, CHUNK), fix3),
                  pl.BlockSpec((CHUNK, GMLP_G), fix), pl.BlockSpec((1, GMLP_W), fix)],
        out_specs=[pl.BlockSpec((tm, ZW), rev), pl.BlockSpec((1, FOX_HD), fix), pl.BlockSpec((1, FOX_HD), fix),
                   pl.BlockSpec((1, GMLP_W), fix), pl.BlockSpec((1, GMLP_W), fix),
                   pl.BlockSpec((GMLP_G, CHUNK, CHUNK), fix3), pl.BlockSpec((CHUNK, GMLP_G), fix),
                   pl.BlockSpec((1, LANES), fix)],
        out_shape=[S((T, ZW), BF), S((1, FOX_HD), F32), S((1, FOX_HD), F32), S((1, GMLP_W), F32), S((1, GMLP_W), F32),
                   S((GMLP_G, CHUNK, CHUNK), F32), S((CHUNK, GMLP_G), F32), S((1, LANES), F32)],
        scratch_shapes=[pltpu.VMEM((1, LANES), F32)],
        compiler_params=_cp(1))(z, dq, dk, dv, dyg, rs, bf128, g_q, g_k, g_sgu, w_s, b_st, g_go)


def _mix_proj_bwd(dz, wz, x, g, dy):
    T, D = x.shape
    tm = _tile(T, 512)

    def body(dz_ref, w_ref, x_ref, g_ref, dy_ref, dx_ref, dxb_ref, dg_ref):
        dh = _nn(dz_ref[...], w_ref[...])
        dx, dgr = _norm_bwd(dh, x_ref[...], g_ref[...])
        dx = dx + dy_ref[...]
        dx_ref[...] = dx
        dxb_ref[...] = dx.astype(BF)
        _acc_rows(dg_ref, pl.program_id(0) == 0, dgr)

    row = lambda i: (i, 0)
    fix = lambda i: (0, 0)
    return pl.pallas_call(
        body, name="mix_proj_bwd", grid=(T // tm,),
        in_specs=[pl.BlockSpec((tm, ZW), row), pl.BlockSpec((ZW, D), fix), pl.BlockSpec((tm, D), row),
                  pl.BlockSpec((1, D), fix), pl.BlockSpec((tm, D), row)],
        out_specs=[pl.BlockSpec((tm, D), row), pl.BlockSpec((tm, D), row), pl.BlockSpec((1, D), fix)],
        out_shape=[S((T, D), F32), S((T, D), BF), S((1, D), F32)],
        compiler_params=_cp(1))(dz, wz, x, g, dy)


def _ca_kv(mem, g_mem, wckv, g_ck):
    M, D = mem.shape

    def body(m_ref, g_ref, w_ref, gk_ref, mn_ref, kr_ref, kn_ref, v_ref):
        mf = m_ref[...]
        mn = (mf * _rstd(mf) * g_ref[...]).astype(BF)
        mn_ref[...] = mn
        for h in range(CA_HEADS):
            kr = _nn(mn, w_ref[h])
            kr_ref[h] = kr
            kn_ref[h] = (kr * _rstd(kr) * gk_ref[...]).astype(BF)
            v_ref[h] = _nn(mn, w_ref[CA_HEADS + h]).astype(BF)

    hd = (CA_HEADS, M, CA_HD)
    return pl.pallas_call(
        body, name="ca_kv", out_shape=[S((M, D), BF), S(hd, F32), S(hd, BF), S(hd, BF)],
        compiler_params=pltpu.CompilerParams(vmem_limit_bytes=VMEM_LIMIT))(mem, g_mem, wckv, g_ck)


def _ca_tile_fwd(xt, gca, wcq, gcq, kn_ref, v_ref):
    hb = (xt * _rstd(xt) * gca).astype(BF)
    qc = _nn(hb, wcq)
    qr, qn, ps = [], [], []
    for h in range(CA_HEADS):
        qh = qc[:, h * CA_HD:(h + 1) * CA_HD]
        qnh = (qh * _rstd(qh) * gcq * 0.0625).astype(BF)
        s = _nt(qnh, kn_ref[h])
        e = jnp.exp(s - jnp.max(s, axis=1, keepdims=True))
        ps.append(e / jnp.sum(e, axis=1, keepdims=True))
        qr.append(qh)
        qn.append(qnh)
    return hb, qr, qn, ps


def _ca_fwd(x, g_ca, wcq, g_cq, kn, vv, wco):
    T, D = x.shape
    M = kn.shape[1]
    tm = _tile(T, 1024)

    def body(x_ref, gca_ref, wcq_ref, gcq_ref, kn_ref, v_ref, wco_ref, o_ref, ob_sc):
        xt = x_ref[...]
        _, _, _, ps = _ca_tile_fwd(xt, gca_ref[...], wcq_ref[...], gcq_ref[...], kn_ref, v_ref)
        for h in range(CA_HEADS):
            ob_sc[:, h * CA_HD:(h + 1) * CA_HD] = _nn(ps[h].astype(BF), v_ref[h]).astype(BF)
        o_ref[...] = xt + _nn(ob_sc[...], wco_ref[...])

    row = lambda i: (i, 0)
    fix = lambda i: (0, 0)
    fix3 = lambda i: (0, 0, 0)
    return pl.pallas_call(
        body, name="ca_fwd", grid=(T // tm,),
        in_specs=[pl.BlockSpec((tm, D), row), pl.BlockSpec((1, D), fix), pl.BlockSpec((D, D), fix),
                  pl.BlockSpec((1, CA_HD), fix), pl.BlockSpec((CA_HEADS, M, CA_HD), fix3),
                  pl.BlockSpec((CA_HEADS, M, CA_HD), fix3), pl.BlockSpec((D, D), fix)],
        out_specs=pl.BlockSpec((tm, D), row), out_shape=S((T, D), F32),
        scratch_shapes=[pltpu.VMEM((tm, D), BF)],
        compiler_params=_cp(1))(x, g_ca, wcq, g_cq, kn, vv, wco)


def _ca_bwd(x, dy, g_ca, wcq, g_cq, kn, vv, wco):
    T, D = x.shape
    M = kn.shape[1]
    tm = _tile(T, 512)
    n = T // tm

    def body(x_ref, dy_ref, gca_ref, wcq_ref, gcq_ref, kn_ref, v_ref, wco_ref,
             dx_ref, dwq_ref, dwo_ref, dkn_ref, dv_ref, dgcq_ref, dgca_ref, aq_sc, ao_sc, ob_sc, dq_sc):
        i = pl.program_id(0)
        first = i == 0
        xt = x_ref[...]
        dyt = dy_ref[...]
        dyb = dyt.astype(BF)
        hb, qr, qn, ps = _ca_tile_fwd(xt, gca_ref[...], wcq_ref[...], gcq_ref[...], kn_ref, v_ref)
        do = _nt(dyb, wco_ref[...])
        gcq_rows = None
        for h in range(CA_HEADS):
            hs = slice(h * CA_HD, (h + 1) * CA_HD)
            p = ps[h]
            pb = p.astype(BF)
            ob_sc[:, hs] = _nn(pb, v_ref[h]).astype(BF)
            doh = do[:, hs].astype(BF)
            dp = _nt(doh, v_ref[h])
            ds = (p * (dp - jnp.sum(dp * p, axis=1, keepdims=True))).astype(BF)
            dvh = _tn(pb, doh)
            dkh = _tn(ds, qn[h])

            @pl.when(first)
            def _():
                dv_ref[h] = dvh
                dkn_ref[h] = dkh

            @pl.when(jnp.logical_not(first))
            def _():
                dv_ref[h] += dvh
                dkn_ref[h] += dkh

            dqn = _nn(ds, kn_ref[h]) * 0.0625
            dqh, gr = _norm_bwd(dqn, qr[h], gcq_ref[...])
            gcq_rows = gr if gcq_rows is None else gcq_rows + gr
            dq_sc[:, hs] = dqh.astype(BF)
        _acc_rows(dgcq_ref, first, gcq_rows)
        dqb = dq_sc[...]
        p_o = _tn(ob_sc[...], dyb)
        p_q = _tn(hb, dqb)

        @pl.when(first)
        def _():
            ao_sc[...] = p_o
            aq_sc[...] = p_q

        @pl.when(jnp.logical_not(first))
        def _():
            ao_sc[...] += p_o
            aq_sc[...] += p_q

        @pl.when(i == n - 1)
        def _():
            dwo_ref[...] = ao_sc[...].astype(BF)
            dwq_ref[...] = aq_sc[...].astype(BF)

        dh = _nt(dqb, wcq_ref[...])
        dx, gar = _norm_bwd(dh, xt, gca_ref[...])
        dx_ref[...] = dx + dyt
        _acc_rows(dgca_ref, first, gar)

    row = lambda i: (i, 0)
    fix = lambda i: (0, 0)
    fix3 = lambda i: (0, 0, 0)
    hd = (CA_HEADS, M, CA_HD)
    return pl.pallas_call(
        body, name="ca_bwd", grid=(n,),
        in_specs=[pl.BlockSpec((tm, D), row), pl.BlockSpec((tm, D), row), pl.BlockSpec((1, D), fix),
                  pl.BlockSpec((D, D), fix), pl.BlockSpec((1, CA_HD), fix), pl.BlockSpec(hd, fix3),
                  pl.BlockSpec(hd, fix3), pl.BlockSpec((D, D), fix)],
        out_specs=[pl.BlockSpec((tm, D), row), pl.BlockSpec((D, D), fix), pl.BlockSpec((D, D), fix),
                   pl.BlockSpec(hd, fix3), pl.BlockSpec(hd, fix3), pl.BlockSpec((1, CA_HD), fix),
                   pl.BlockSpec((1, D), fix)],
        out_shape=[S((T, D), F32), S((D, D), BF), S((D, D), BF), S(hd, F32), S(hd, F32), S((1, CA_HD), F32),
                   S((1, D), F32)],
        scratch_shapes=[pltpu.VMEM((D, D), F32), pltpu.VMEM((D, D), F32), pltpu.VMEM((tm, D), BF),
                        pltpu.VMEM((tm, D), BF)],
        compiler_params=_cp(1))(x, dy, g_ca, wcq, g_cq, kn, vv, wco)


def _ca_kv_bwd(mem, g_mem, mn, kraw, dkn, dvv, wckv, g_ck):
    M, D = mem.shape

    def body(m_ref, g_ref, mn_ref, kr_ref, dkn_ref, dv_ref, w_ref, gk_ref, dw_ref, dgk_ref, dgm_ref):
        mn = mn_ref[...]
        dmn = jnp.zeros((M, D), F32)
        gk_rows = None
        for h in range(CA_HEADS):
            dkr, gr = _norm_bwd(dkn_ref[h], kr_ref[h], gk_ref[...])
            gk_rows = gr if gk_rows is None else gk_rows + gr
            dkb = dkr.astype(BF)
            dvb = dv_ref[h].astype(BF)
            dw_ref[h] = _tn(mn, dkb).astype(BF)
            dw_ref[CA_HEADS + h] = _tn(mn, dvb).astype(BF)
            dmn = dmn + _nt(dkb, w_ref[h]) + _nt(dvb, w_ref[CA_HEADS + h])
        dgk_ref[...] = jnp.sum(gk_rows, axis=0, keepdims=True)
        mf = m_ref[...]
        dgm_ref[...] = jnp.sum(dmn * (mf * _rstd(mf)), axis=0, keepdims=True)

    return pl.pallas_call(
        body, name="ca_kv_bwd",
        out_shape=[S((2 * CA_HEADS, D, CA_HD), BF), S((1, CA_HD), F32), S((1, D), F32)],
        compiler_params=pltpu.CompilerParams(vmem_limit_bytes=VMEM_LIMIT))(mem, g_mem, mn, kraw, dkn, dvv, wckv, g_ck)


def _after(g, token):
    return g if token is None else g + token[0:1, 0:1]


def _local_step(x, mem, target, small, weights, emit):
    T, D = x.shape
    p = small
    bf128 = jnp.pad(p["b_f"], ((0, 0), (0, LANES - FOX_HEADS)))
    b_st = p["b_s"].T

    wup1 = weights("ffn1_up", x)["wup1"]
    a1, h1 = _ffn_up("ffn1_up", x, p["g_ffn1"], wup1)
    wdn1 = weights("ffn1_dn", h1)["wdn1"]
    x1 = _ffn_down("ffn1_down", a1, wdn1, x)
    wm = weights("mix", x1)
    z, h2, qf, ka, va, yg, rs = _mix_prep(x1, p["g_mix"], wm["wz"], bf128, p["g_q"], p["g_k"], p["g_sgu"], p["w_s"],
                                          b_st, p["g_gmlp_o"])
    attn, lse = _fox_fwd(qf, ka, va)
    x2 = _mix_out(attn, yg, p["g_fox_o"], wm["wout"], x1)
    wc = weights("ca", x2)
    mn, kraw, ckn, cvv = _ca_kv(mem, p["g_mem"], wc["wckv"], p["g_ck"])
    x3 = _ca_fwd(x2, p["g_ca"], wc["wcq"], p["g_cq"], ckn, cvv, wc["wco"])
    w2 = weights("ffn2", x3)
    a2, h4 = _ffn_up("ffn2_up", x3, p["g_ffn2"], w2["wup2"])
    dy4, dy4b, sq = _ffn_down_loss("ffn2_down", a2, w2["wdn2"], x3, target)

    gs = {}
    dgu2 = _ffn_bwd_act("ffn2_bwd_act", dy4b, h4, w2["wup2"], w2["wdn2"])
    tok = emit("ffn2", {"wup2": _ffn_dwup("ffn2", h4, dgu2), "wdn2": _ffn_dwdn("ffn2", a2, dy4b)})
    dx3, gs["g_ffn2"] = _ffn_dx("ffn2_dx", dgu2, w2["wup2"], x3, _after(p["g_ffn2"], tok), dy4)

    dx2, dwcq, dwco, dckn, dcvv, gs["g_cq"], gs["g_ca"] = _ca_bwd(
        x2, dx3, p["g_ca"], wc["wcq"], p["g_cq"], ckn, cvv, wc["wco"])
    dwckv, gs["g_ck"], gs["g_mem"] = _ca_kv_bwd(mem, p["g_mem"], mn, kraw, dckn, dcvv, wc["wckv"], p["g_ck"])

    qb, dob, dyg, dwout, gs["g_fox_o"] = _mix_out_bwd(dx2, attn, yg, p["g_fox_o"], wm["wout"], qf, lse)
    dq, dk, dv = _fox_bwd(qb, ka, va, dob)
    dz, gs["g_q"], gs["g_k"], gs["g_sgu"], gs["g_gmlp_o"], gs["w_s"], dbst, dbf = _mix_prep_bwd(
        z, dq, dk, dv, dyg, rs, bf128, p["g_q"], p["g_k"], p["g_sgu"], p["w_s"], b_st, p["g_gmlp_o"])
    gs["b_s"] = dbst.T
    gs["b_f"] = dbf[:, :FOX_HEADS]
    tok_ws = emit("w_s", {"w_s": gs["w_s"]})
    zb = ZW // 3
    dwz = _tn_matmul("mix_dwz", dz, pl.BlockSpec((T, zb), lambda j: (0, j)), h2,
                     S((ZW, D), BF), pl.BlockSpec((zb, D), lambda j: (j, 0)), 3)
    tok = emit("mid", {"wcq": dwcq, "wco": dwco, "wckv": dwckv, "wout": dwout, "wz": dwz})
    dx1, dx1b, gs["g_mix"] = _mix_proj_bwd(dz, wm["wz"], x1, _after(_after(p["g_mix"], tok), tok_ws), dx2)

    dgu1 = _ffn_bwd_act("ffn1_bwd_act", dx1b, h1, wup1, wdn1)
    tok = emit("ffn1_dn", {"wdn1": _ffn_dwdn("ffn1", a1, dx1b)})
    tok = emit("ffn1_up", {"wup1": _ffn_dwup("ffn1", h1, dgu1, after=tok)})
    dx0, gs["g_ffn1"] = _ffn_dx("ffn1_dx", dgu1, wup1, x, _after(p["g_ffn1"], tok), dx1)
    return sq, dx0, gs


MESH = pl.DeviceIdType.MESH
HBM_SPEC = pl.BlockSpec(memory_space=pltpu.HBM)
N_PEER = N_DEV - 1


def _place():
    return lax.axis_index("x"), lax.axis_index("y"), lax.axis_index("c")


def _slot(px, py, pc):
    return 4 * px + 2 * py + pc


SEM_SPEC = pl.BlockSpec(memory_space=pltpu.SEMAPHORE)
ANY_SPEC = pl.BlockSpec(memory_space=pl.ANY)
DATAFLOW = pltpu.SideEffectType.DATAFLOW_SIDE_EFFECTING


def _hbm(a):
    return pltpu.with_memory_space_constraint(a, pltpu.HBM)


def _peer(x, y, c, r):
    return (1 - x if r & 4 else x, 1 - y if r & 2 else y, 1 - c if r & 1 else c)


def _place_own(srcs, whole):
    my = _slot(*_place())
    lands = []
    for s in srcs:
        blk = s[None] if whole else lax.dynamic_slice_in_dim(s, my, 1, 0)
        shape = (N_DEV,) + s.shape if whole else s.shape
        lands.append(lax.dynamic_update_slice_in_dim(lax.empty(shape, s.dtype), blk, my, 0))
    return lands


ALL_PEERS = tuple(range(1, N_DEV))
NEAR_PEERS = (1, 2, 4, 6)
SAME_CORE = (2, 4, 6)


def _copy_start(name, srcs, lands, whole, peers=None):
    n = len(srcs)
    peers = peers or [ALL_PEERS] * n

    def body(*refs):
        src, land = refs[:n], refs[n:2 * n]
        send, recv = refs[2 * n:3 * n], refs[3 * n:4 * n]
        token = refs[6 * n]
        x, y, c = _place()
        my = _slot(x, y, c)
        for a in range(n):
            for r in peers[a]:
                p = _peer(x, y, c, r)
                pltpu.make_async_remote_copy(
                    src_ref=src[a] if whole else src[a].at[_slot(*p)], dst_ref=land[a].at[my],
                    send_sem=send[a].at[r - 1], recv_sem=recv[a].at[r - 1], device_id=p, device_id_type=MESH).start()
        token[...] = jnp.zeros_like(token)

    out = pl.pallas_call(
        body, name=name,
        out_shape=([pltpu.SemaphoreType.DMA((N_PEER,))] * (2 * n)
                   + [pltpu.HBM(s.shape, s.dtype) for s in srcs] + [pltpu.HBM(s.shape, s.dtype) for s in lands]
                   + [S((8, LANES), F32)]),
        in_specs=[HBM_SPEC] * (2 * n),
        out_specs=[SEM_SPEC] * (2 * n) + [HBM_SPEC] * (2 * n) + [pl.BlockSpec(memory_space=pltpu.VMEM)],
        input_output_aliases={i: 2 * n + i for i in range(2 * n)},
        compiler_params=pltpu.CompilerParams(has_side_effects=DATAFLOW),
    )(*[_hbm(s) for s in srcs], *[_hbm(s) for s in lands])
    return out[:n], out[n:2 * n], out[2 * n:3 * n], out[3 * n:4 * n], out[4 * n]


def _copy_wait(name, srcs, lands, send, recv, after, whole, peers=None, with_srcs=False):
    n = len(srcs)
    peers = peers or [ALL_PEERS] * n

    def body(*refs):
        src, land = refs[:n], refs[n:2 * n]
        snd, rcv = refs[2 * n:3 * n], refs[3 * n:4 * n]
        x, y, c = _place()
        for a in range(n):
            for r in peers[a]:
                p = _peer(x, y, c, r)
                ps = _slot(*p)
                cp = pltpu.make_async_remote_copy(
                    src_ref=src[a] if whole else src[a].at[ps], dst_ref=land[a].at[ps],
                    send_sem=snd[a].at[r - 1], recv_sem=rcv[a].at[r - 1], device_id=p, device_id_type=MESH)
                cp.wait_send()
                cp.wait_recv()

    out = pl.pallas_call(
        body, name=name,
        out_shape=[pltpu.HBM(s.shape, s.dtype) for s in srcs] + [pltpu.HBM(s.shape, s.dtype) for s in lands],
        in_specs=[HBM_SPEC] * (2 * n) + [SEM_SPEC] * (2 * n) + [ANY_SPEC],
        out_specs=[HBM_SPEC] * (2 * n),
        input_output_aliases={i: i for i in range(2 * n)},
        compiler_params=pltpu.CompilerParams(has_side_effects=DATAFLOW),
    )(*srcs, *lands, *send, *recv, after)
    return (out[:n], out[n:]) if with_srcs else out[n:]


def _forward_start(name, lands):
    n = len(lands)

    def body(*refs):
        land = refs[:n]
        send, recv = refs[n:2 * n], refs[2 * n:3 * n]
        token = refs[4 * n]
        x, y, c = _place()
        for a in range(n):
            for r in SAME_CORE:
                blk = land[a].at[_slot(*_peer(x, y, c, r))]
                pltpu.make_async_remote_copy(
                    src_ref=blk, dst_ref=blk, send_sem=send[a].at[r - 1], recv_sem=recv[a].at[r - 1],
                    device_id=(x, y, 1 - c), device_id_type=MESH).start()
        token[...] = jnp.zeros_like(token)

    out = pl.pallas_call(
        body, name=name,
        out_shape=([pltpu.SemaphoreType.DMA((N_PEER,))] * (2 * n) + [pltpu.HBM(s.shape, s.dtype) for s in lands]
                   + [S((8, LANES), F32)]),
        in_specs=[HBM_SPEC] * n,
        out_specs=[SEM_SPEC] * (2 * n) + [HBM_SPEC] * n + [pl.BlockSpec(memory_space=pltpu.VMEM)],
        input_output_aliases={i: 2 * n + i for i in range(n)},
        compiler_params=pltpu.CompilerParams(has_side_effects=DATAFLOW),
    )(*[_hbm(s) for s in lands])
    return out[:n], out[n:2 * n], out[2 * n:3 * n], out[3 * n]


def _forward_wait(name, lands, send, recv, after):
    n = len(lands)

    def body(*refs):
        land = refs[:n]
        snd, rcv = refs[n:2 * n], refs[2 * n:3 * n]
        x, y, c = _place()
        for a in range(n):
            for r in SAME_CORE:
                cp = pltpu.make_async_remote_copy(
                    src_ref=land[a].at[_slot(*_peer(x, y, c, r))], dst_ref=land[a].at[_slot(*_peer(x, y, c, r | 1))],
                    send_sem=snd[a].at[r - 1], recv_sem=rcv[a].at[r - 1], device_id=(x, y, 1 - c),
                    device_id_type=MESH)
                cp.wait_send()
                cp.wait_recv()

    return pl.pallas_call(
        body, name=name,
        out_shape=[pltpu.HBM(s.shape, s.dtype) for s in lands],
        in_specs=[HBM_SPEC] * n + [SEM_SPEC] * (2 * n) + [ANY_SPEC],
        out_specs=[HBM_SPEC] * n,
        input_output_aliases={i: i for i in range(n)},
        compiler_params=pltpu.CompilerParams(has_side_effects=DATAFLOW),
    )(*lands, *send, *recv, after)


def _adamw(w, g, m, v):
    m2 = ADAM_B1 * m + (1.0 - ADAM_B1) * g
    v2 = ADAM_B2 * v + (1.0 - ADAM_B2) * (g * g)
    m_hat = m2 / (1.0 - ADAM_B1 ** ADAM_STEP)
    v_hat = v2 / (1.0 - ADAM_B2 ** ADAM_STEP)
    delta = -ADAM_LR * (m_hat / (jnp.sqrt(v_hat) + ADAM_EPS) + ADAM_WD * w)
    return delta, m2, v2


def _adamw_big(name, slots, w, m, v, own=None):
    R, C = w.shape
    tr = next((t for t in (256, 352) if R % t == 0), R)

    def finish(g, w_ref, m_ref, v_ref, g_ref, d_ref, m2_ref, v2_ref):
        d, m2, v2 = _adamw(w_ref[...], g, m_ref[...], v_ref[...])
        g_ref[...] = g
        d_ref[...] = d
        m2_ref[...] = m2
        v2_ref[...] = v2

    if own is None:
        def body(s_ref, *refs):
            g = s_ref[0].astype(F32)
            for k in range(1, N_DEV):
                g = g + s_ref[k].astype(F32)
            finish(g, *refs)

        row = pl.BlockSpec((tr, C), lambda i: (i, 0))
        return pl.pallas_call(
            body, name=name, grid=(R // tr,),
            in_specs=[pl.BlockSpec((N_DEV, tr, C), lambda i: (0, i, 0)), row, row, row],
            out_specs=[row] * 4, out_shape=[S((R, C), F32)] * 4,
            compiler_params=_cp(1))(slots, w, m, v)

    def body(my_ref, s_ref, own_ref, *refs):
        mine = own_ref[...]
        g = None
        for k in range(N_DEV):
            part = jnp.where(my_ref[0] == k, mine, s_ref[k]).astype(F32)
            g = part if g is None else g + part
        finish(g, *refs)

    row = pl.BlockSpec((tr, C), lambda i, my_ref: (i, 0))
    my = jnp.reshape(_slot(*_place()), (1,)).astype(jnp.int32)
    return pl.pallas_call(
        body, name=name,
        grid_spec=pltpu.PrefetchScalarGridSpec(
            num_scalar_prefetch=1, grid=(R // tr,),
            in_specs=[pl.BlockSpec((N_DEV, tr, C), lambda i, my_ref: (0, i, 0)),
                      pl.BlockSpec((None, tr, C), lambda i, my_ref: (my_ref[0], i, 0)), row, row, row],
            out_specs=[row] * 4),
        out_shape=[S((R, C), F32)] * 4, compiler_params=_cp(1))(my, slots, own, w, m, v)


TINY_ROWS = (("b_s", 8), ("g_ffn1", 8), ("g_mix", 8), ("g_ca", 8), ("g_mem", 8), ("g_ffn2", 8), ("g_sgu", 4),
             ("g_fox_o", 4), ("g_gmlp_o", 4), ("g_cq", 2), ("g_ck", 2), ("g_q", 1), ("g_k", 1), ("b_f", 1),
             ("loss", 1))
TINY_P = 72


def _tiny_pieces(width):
    return [(j, slice(j * LANES, min((j + 1) * LANES, width))) for j in range(-(-width // LANES))]


def _pack_tiny(grads, sq):
    names = [n for n, _ in TINY_ROWS if n != "loss"]

    def body(*refs):
        ins, sq_ref, o_ref = refs[:len(names)], refs[len(names)], refs[len(names) + 1]
        o_ref[...] = jnp.zeros_like(o_ref)
        at = 0
        for ref, (name, r) in zip(ins, TINY_ROWS):
            if name == "b_s":
                o_ref[at:at + r, :] = ref[...]
            else:
                for j, cols in _tiny_pieces(ref.shape[1]):
                    o_ref[at + j:at + j + 1, 0:cols.stop - cols.start] = ref[:, cols]
            at += r
        o_ref[at:at + 1, :] = sq_ref[0:1, :]

    return pl.pallas_call(body, name="tiny_pack", out_shape=S((TINY_P, LANES), F32))(
        *[grads[n] for n in names], sq)


def _adamw_tiny(slots, w, m, v):
    names = [n for n, _ in TINY_ROWS if n != "loss"]
    k = len(names)

    def body(s_ref, *refs):
        ins, outs, loss_ref = refs[:3 * k], refs[3 * k:7 * k], refs[7 * k]
        g_all = s_ref[0]
        for d in range(1, N_DEV):
            g_all = g_all + s_ref[d]
        at = 0
        for i, (name, r) in enumerate(TINY_ROWS[:k]):
            w_ref, m_ref, v_ref = ins[i], ins[k + i], ins[2 * k + i]
            o = outs[4 * i:4 * i + 4]
            if name == "b_s":
                pieces = [(slice(at, at + r), slice(0, LANES), (slice(None), slice(None)))]
            else:
                pieces = [(slice(at + j, at + j + 1), slice(0, c.stop - c.start), (slice(None), c))
                          for j, c in _tiny_pieces(w_ref.shape[1])]
            for rows, lanes, dst in pieces:
                g = g_all[rows, lanes]
                res = (g,) + _adamw(w_ref[dst], g, m_ref[dst], v_ref[dst])
                for ref, val in zip(o, res):
                    ref[dst] = val
            at += r
        loss_ref[...] = g_all[at:at + 1, :]

    shapes = [S(w[n].shape, F32) for n in names]
    out = pl.pallas_call(
        body, name="adamw_tiny", out_shape=[s for s in shapes for _ in range(4)] + [S((1, LANES), F32)],
    )(slots, *[w[n] for n in names], *[m[n] for n in names], *[v[n] for n in names])
    stores = ({}, {}, {}, {})
    for i, n in enumerate(names):
        for store, t in zip(stores, out[4 * i:4 * i + 4]):
            store[n] = t
    return stores, out[4 * k]


WEIGHTS =('g_ffn1', 'w_ffn1_in', 'w_ffn1_out', 'g_mix', 'w_in', 'b_f', 'g_q', 'g_k', 'g_sgu', 'w_s', 'b_s',
           'g_fox_o', 'g_gmlp_o', 'w_out', 'g_ca', 'g_mem', 'w_cq', 'w_ckv', 'g_cq', 'g_ck', 'w_co', 'g_ffn2',
           'w_ffn2_in', 'w_ffn2_out')
BIG = ('w_ffn1_in', 'w_ffn1_out', 'w_in', 'w_out', 'w_cq', 'w_ckv', 'w_co', 'w_ffn2_in', 'w_ffn2_out')
TRANSPOSED = ('w_ffn1_in', 'w_in', 'w_ffn2_in')
TWO_LEVEL = ('w_ffn1_in', 'w_in')
GATHER_GROUPS = {"ffn1_up": ("w_ffn1_in",), "ffn1_dn": ("w_ffn1_out",), "mix": ("w_in", "w_out"),
                 "ca": ("w_cq", "w_ckv", "w_co"), "ffn2": ("w_ffn2_in", "w_ffn2_out")}
QKV_W = 3 * FOX_W
UV_OFF = QKV_W + FOX_HEADS


def kernel(x, mem, g_ffn1, w_ffn1_in, w_ffn1_out, g_mix, w_in, b_f, g_q, g_k, g_sgu, w_s, b_s, g_fox_o, g_gmlp_o, w_out, g_ca, g_mem, w_cq, w_ckv, g_cq, g_ck, w_co, g_ffn2, w_ffn2_in, w_ffn2_out, loss_target, m_g_ffn1, m_w_ffn1_in, m_w_ffn1_out, m_g_mix, m_w_in, m_b_f, m_g_q, m_g_k, m_g_sgu, m_w_s, m_b_s, m_g_fox_o, m_g_gmlp_o, m_w_out, m_g_ca, m_g_mem, m_w_cq, m_w_ckv, m_g_cq, m_g_ck, m_w_co, m_g_ffn2, m_w_ffn2_in, m_w_ffn2_out, v_g_ffn1, v_w_ffn1_in, v_w_ffn1_out, v_g_mix, v_w_in, v_b_f, v_g_q, v_g_k, v_g_sgu, v_w_s, v_b_s, v_g_fox_o, v_g_gmlp_o, v_w_out, v_g_ca, v_g_mem, v_w_cq, v_w_ckv, v_g_cq, v_g_ck, v_w_co, v_g_ffn2, v_w_ffn2_in, v_w_ffn2_out):
    args = dict(locals())
    w = {n: args[n] for n in WEIGHTS}
    mo = {n: args["m_" + n] for n in WEIGHTS}
    vo = {n: args["v_" + n] for n in WEIGHTS}
    D = D_MODEL

    def local(n, a):
        return a[0].T if n in TRANSPOSED else a[0]

    g_peers = [NEAR_PEERS if n in TWO_LEVEL else ALL_PEERS for n in BIG]
    handles = {}

    def start_gather(name, names, arrays):
        snd, rcv, src, land, token = _copy_start(name, arrays, _place_own(arrays, True), True,
                                                 peers=[g_peers[BIG.index(n)] for n in names])
        handles.update({n: (src[i], land[i], snd[i], rcv[i]) for i, n in enumerate(names)})
        return token

    first = local(BIG[0], w[BIG[0]]).astype(BF)
    fb = first.shape[0]
    token_first = start_gather("gather_start_first", BIG[:1], [first])
    token_rest = start_gather("gather_start_rest", BIG[1:],
                              [(local(n, w[n]) + token_first[0:1, 0:1]).astype(BF) for n in BIG[1:]])

    tiny_names = [n for n, _ in TINY_ROWS if n != "loss"]

    def weights(group, after):
        names = GATHER_GROUPS[group]
        hs = [handles[n] for n in names]
        got = list(_copy_wait("gather_wait_" + group, [h[0] for h in hs], [h[1] for h in hs], [h[2] for h in hs],
                              [h[3] for h in hs], token_rest if group == "ffn1_up" else after, True,
                              peers=[g_peers[BIG.index(n)] for n in names]))
        passed = [i for i, n in enumerate(names) if n in TWO_LEVEL]
        if passed:
            f_snd, f_rcv, f_land, f_token = _forward_start("gather_pass_start_" + group, [got[i] for i in passed])
            for i, t in zip(passed, _forward_wait("gather_pass_wait_" + group, f_land, f_snd, f_rcv, f_token)):
                got[i] = t
        got = dict(zip(names, got))
        if group == "ffn1_up":
            return {"wup1": got["w_ffn1_in"].reshape(2, N_FFN_BLK, fb, D)}
        if group == "ffn1_dn":
            return {"wdn1": got["w_ffn1_out"].reshape(N_FFN_BLK, fb, D)}
        if group == "mix":
            full = got["w_in"].reshape(-1, D)
            wz = jnp.concatenate([full[:QKV_W], full[UV_OFF:], full[QKV_W:UV_OFF],
                                  jnp.zeros((LANES - FOX_HEADS, D), BF)], axis=0)
            return {"wz": wz, "wout": got["w_out"].reshape(D, D)}
        if group == "ca":
            return {"wcq": got["w_cq"].reshape(D, D), "wco": got["w_co"].reshape(D, D), "wckv": got["w_ckv"]}
        return {"wup2": got["w_ffn2_in"].reshape(2, N_FFN_BLK, fb, D),
                "wdn2": got["w_ffn2_out"].reshape(N_FFN_BLK, fb, D)}

    flying = {}

    def emit(group, g):
        if group == "w_s":
            part = [g["w_s"].reshape(-1, LANES)]
            *copies, token = _copy_start("w_s_start", part, _place_own(part, True), True)
            flying[group] = copies
            return token
        if group == "ffn2":
            parts = {"w_ffn2_in": g["wup2"], "w_ffn2_out": g["wdn2"].reshape(N_DEV, -1, D)}
        elif group == "ffn1_dn":
            parts = {"w_ffn1_out": g["wdn1"].reshape(N_DEV, -1, D)}
        elif group == "ffn1_up":
            parts = {"w_ffn1_in": g["wup1"]}
        else:
            gz = g["wz"]
            g_in = jnp.concatenate([gz[:QKV_W], gz[Z_F:Z_F + FOX_HEADS], gz[QKV_W:Z_F]], axis=0)
            parts = {"w_in": g_in.reshape(N_DEV, -1, D).astype(BF),
                     "w_out": g["wout"].reshape(N_DEV, -1, D), "w_cq": g["wcq"].reshape(N_DEV, -1, D),
                     "w_co": g["wco"].reshape(N_DEV, -1, D), "w_ckv": g["wckv"]}
        names = list(parts)
        srcs = [parts[n] for n in names]
        *copies, token = _copy_start("exchange_start_" + group, srcs, [lax.empty(s.shape, s.dtype) for s in srcs],
                                     False)
        flying[group] = (names, copies)
        return token

    small = {n: (w[n][0] if n == "b_s" else w[n]) for n in tiny_names}
    small["w_s"] = w["w_s"][0]

    sq, dx0, gs = _local_step(x[0], mem[0], loss_target[0], small, weights, emit)

    sm_parts = [_pack_tiny(gs, sq)]
    sm_snd, sm_rcv, sm_src, sm_land, sm_token = _copy_start("tiny_start", sm_parts, _place_own(sm_parts, True), True)

    grad, delta, new_m, new_v = {}, {}, {}, {}

    def update(group, after):
        names, (snd, rcv, srcs, lands) = flying[group]
        owns, slots = _copy_wait("exchange_wait_" + group, srcs, lands, snd, rcv, after, False, with_srcs=True)
        for n, sl, own in zip(names, slots, owns):
            g, d, m2, v2 = _adamw_big("adamw_" + n, sl, local(n, w[n]), local(n, mo[n]), local(n, vo[n]), own=own)
            grad[n], delta[n], new_m[n], new_v[n] = (
                (t.T if n in TRANSPOSED else t).reshape(w[n].shape) for t in (g, d, m2, v2))
        return d

    last = update("ffn2", sm_token)
    last = update("mid", last)
    last = update("ffn1_dn", last)
    last = update("ffn1_up", last)
    ws_snd, ws_rcv, ws_src, ws_land = flying["w_s"]
    ws_all, = _copy_wait("w_s_wait", ws_src, ws_land, ws_snd, ws_rcv, last, True)
    tiny_all, = _copy_wait("tiny_wait", sm_src, sm_land, sm_snd, sm_rcv, ws_all, True)
    ws_shape = w["w_s"].shape
    for store, t in zip((grad, delta, new_m, new_v), _adamw_big(
            "adamw_w_s", ws_all, *[a["w_s"].reshape(-1, LANES) for a in (w, mo, vo)])):
        store["w_s"] = t.reshape(ws_shape)
    stores, loss_row = _adamw_tiny(tiny_all, *[{n: (a[n][0] if n == "b_s" else a[n]) for n in tiny_names}
                                               for a in (w, mo, vo)])
    for store, t in zip((grad, delta, new_m, new_v), stores):
        store.update({n: v.reshape(w[n].shape) for n, v in t.items()})
    loss = loss_row[0, 0] * (0.5 / D)

    return (loss, dx0[None], *[grad[n] for n in WEIGHTS], *[delta[n] for n in WEIGHTS],
            *[new_m[n] for n in WEIGHTS], *[new_v[n] for n in WEIGHTS])
```

```python
import functools

import jax
import jax.numpy as jnp
from jax import lax
from jax.experimental import pallas as pl
from jax.experimental.pallas import tpu as pltpu

F32 = jnp.float32
BF = jnp.bfloat16
S = jax.ShapeDtypeStruct

N_DEV = 8
D_MODEL = 1024
FOX_HEADS, FOX_HD = 8, 64
FOX_W = 512
GMLP_G, GMLP_GD = 8, 64
GMLP_W = 512
CHUNK = 128
CA_HEADS, CA_HD = 4, 256
N_FFN_BLK = 4
ZW = 2688
Z_Q, Z_K, Z_V, Z_U, Z_G, Z_F = 0, 512, 1024, 1536, 2048, 2560
EPS = 1e-6
NEG = -1e30
LANES = 128

ADAM_LR, ADAM_B1, ADAM_B2, ADAM_EPS, ADAM_WD, ADAM_STEP = 0.001, 0.9, 0.999, 1e-08, 0.01, 10

VMEM_LIMIT = 52 * 2 ** 20


def _cp(n_axes):
    return pltpu.CompilerParams(dimension_semantics=("arbitrary",) * n_axes, vmem_limit_bytes=VMEM_LIMIT)


def _nn(a, b):
    return jnp.dot(a, b, preferred_element_type=F32)


def _nt(a, b):
    return lax.dot_general(a, b, (((1,), (1,)), ((), ())), preferred_element_type=F32)


def _tn(a, b):
    return lax.dot_general(a, b, (((0,), (0,)), ((), ())), preferred_element_type=F32)


def _hi(mask, x):
    mb = mask.astype(BF)
    hi = x.astype(BF)
    r1 = x - hi.astype(F32)
    mid = r1.astype(BF)
    lo = (r1 - mid.astype(F32)).astype(BF)
    return _nn(mb, hi) + _nn(mb, mid) + _nn(mb, lo)


def _rstd(x):
    return lax.rsqrt(jnp.mean(x * x, axis=-1, keepdims=True) + EPS)


def _norm_bwd(dy, x, g, r=None):
    r = _rstd(x) if r is None else r
    xh = x * r
    dxh = dy * g
    dx = r * (dxh - xh * jnp.mean(dxh * xh, axis=-1, keepdims=True))
    return dx, dy * xh


def _acc_rows(ref, first, val):
    srow = jnp.sum(val, axis=0, keepdims=True)

    @pl.when(first)
    def _():
        ref[...] = srow

    @pl.when(jnp.logical_not(first))
    def _():
        ref[...] += srow


def _gelu(x):
    c = 0.7978845608028654
    return 0.5 * x * (1.0 + jnp.tanh(c * (x + 0.044715 * x * x * x)))


def _gelu_grad(x):
    c = 0.7978845608028654
    t = jnp.tanh(c * (x + 0.044715 * x * x * x))
    return 0.5 * (1.0 + t) + 0.5 * x * (1.0 - t * t) * c * (1.0 + 3 * 0.044715 * x * x)


def _tile(n, pref):
    return pref if n % pref == 0 else n


def _ffn_up(name, x, g, wup):
    T, D = x.shape
    FB = wup.shape[-2]
    tm = _tile(T, 1024)

    def body(x_ref, g_ref, w_ref, a_ref, h_ref):
        @pl.when(pl.program_id(1) == 0)
        def _():
            xf = x_ref[...]
            h_ref[...] = (xf * _rstd(xf) * g_ref[...]).astype(BF)

        hb = h_ref[...]
        gg = _nt(hb, w_ref[0])
        uu = _nt(hb, w_ref[1])
        a_ref[...] = (gg * jax.nn.sigmoid(gg) * uu).astype(BF)

    return pl.pallas_call(
        body, name=name, grid=(T // tm, N_FFN_BLK),
        in_specs=[pl.BlockSpec((tm, D), lambda i, j: (i, 0)),
                  pl.BlockSpec((1, D), lambda i, j: (0, 0)),
                  pl.BlockSpec((2, None, FB, D), lambda i, j: (0, j, 0, 0))],
        out_specs=[pl.BlockSpec((None, tm, FB), lambda i, j: (j, i, 0)),
                   pl.BlockSpec((tm, D), lambda i, j: (i, 0))],
        out_shape=[S((N_FFN_BLK, T, FB), BF), S((T, D), BF)],
        compiler_params=_cp(2))(x, g, wup)


def _ffn_down(name, a, wdn, x):
    _, T, FB = a.shape
    D = x.shape[1]
    tm = _tile(T, 512)

    def body(a_ref, w_ref, x_ref, o_ref):
        p = _nn(a_ref[0], w_ref[0])
        for j in range(1, N_FFN_BLK):
            p = p + _nn(a_ref[j], w_ref[j])
        o_ref[...] = x_ref[...] + 0.5 * p

    return pl.pallas_call(
        body, name=name, grid=(T // tm,),
        in_specs=[pl.BlockSpec((N_FFN_BLK, tm, FB), lambda i: (0, i, 0)),
                  pl.BlockSpec((N_FFN_BLK, FB, D), lambda i: (0, 0, 0)),
                  pl.BlockSpec((tm, D), lambda i: (i, 0))],
        out_specs=pl.BlockSpec((tm, D), lambda i: (i, 0)),
        out_shape=S((T, D), F32),
        compiler_params=_cp(1))(a, wdn, x)


def _ffn_down_loss(name, a, wdn, x, target):
    _, T, FB = a.shape
    D = x.shape[1]
    tm = _tile(T, 512)

    def body(a_ref, w_ref, x_ref, t_ref, d_ref, db_ref, loss_ref):
        i = pl.program_id(0)
        p = _nn(a_ref[0], w_ref[0])
        for j in range(1, N_FFN_BLK):
            p = p + _nn(a_ref[j], w_ref[j])
        diff = (x_ref[...] + 0.5 * p) - t_ref[...]
        dy = diff * (1.0 / D)
        d_ref[...] = dy
        db_ref[...] = dy.astype(BF)
        sq = jnp.zeros((8, LANES), F32) + jnp.sum(diff * diff)

        @pl.when(i == 0)
        def _():
            loss_ref[...] = sq

        @pl.when(i > 0)
        def _():
            loss_ref[...] += sq

    row = pl.BlockSpec((tm, D), lambda i: (i, 0))
    return pl.pallas_call(
        body, name=name, grid=(T // tm,),
        in_specs=[pl.BlockSpec((N_FFN_BLK, tm, FB), lambda i: (0, i, 0)),
                  pl.BlockSpec((N_FFN_BLK, FB, D), lambda i: (0, 0, 0)), row, row],
        out_specs=[row, row, pl.BlockSpec((8, LANES), lambda i: (0, 0))],
        out_shape=[S((T, D), F32), S((T, D), BF), S((8, LANES), F32)],
        compiler_params=_cp(1))(a, wdn, x, target)


def _ffn_bwd_act(name, dyb, h, wup, wdn):
    T, D = h.shape
    FB = wup.shape[-2]
    tm = _tile(T, 1024)

    def body(d_ref, h_ref, wu_ref, wd_ref, o_ref):
        da = 0.5 * _nt(d_ref[...], wd_ref[...])
        hb = h_ref[...]
        gg = _nt(hb, wu_ref[0])
        uu = _nt(hb, wu_ref[1])
        sg = jax.nn.sigmoid(gg)
        o_ref[0] = (da * uu * (sg * (1.0 + gg * (1.0 - sg)))).astype(BF)
        o_ref[1] = (da * (gg * sg)).astype(BF)

    return pl.pallas_call(
        body, name=name, grid=(T // tm, N_FFN_BLK),
        in_specs=[pl.BlockSpec((tm, D), lambda i, j: (i, 0)),
                  pl.BlockSpec((tm, D), lambda i, j: (i, 0)),
                  pl.BlockSpec((2, None, FB, D), lambda i, j: (0, j, 0, 0)),
                  pl.BlockSpec((None, FB, D), lambda i, j: (j, 0, 0))],
        out_specs=pl.BlockSpec((2, None, tm, FB), lambda i, j: (0, j, i, 0)),
        out_shape=S((2, N_FFN_BLK, T, FB), BF),
        compiler_params=_cp(2))(dyb, h, wup, wdn)


def _ffn_dx(name, dgu, wup, x, g, dy):
    T, D = x.shape
    FB = wup.shape[-2]
    tm = _tile(T, 512)

    def body(d_ref, w_ref, x_ref, g_ref, dy_ref, dx_ref, dg_ref):
        p = None
        for j in range(N_FFN_BLK):
            for half in range(2):
                t = _nn(d_ref[half, j], w_ref[half, j])
                p = t if p is None else p + t
        dx, dgr = _norm_bwd(p, x_ref[...], g_ref[...])
        dx_ref[...] = dx + dy_ref[...]
        _acc_rows(dg_ref, pl.program_id(0) == 0, dgr)

    return pl.pallas_call(
        body, name=name, grid=(T // tm,),
        in_specs=[pl.BlockSpec((2, N_FFN_BLK, tm, FB), lambda i: (0, 0, i, 0)),
                  pl.BlockSpec((2, N_FFN_BLK, FB, D), lambda i: (0, 0, 0, 0), pipeline_mode=pl.Buffered(1)),
                  pl.BlockSpec((tm, D), lambda i: (i, 0)),
                  pl.BlockSpec((1, D), lambda i: (0, 0)),
                  pl.BlockSpec((tm, D), lambda i: (i, 0))],
        out_specs=[pl.BlockSpec((tm, D), lambda i: (i, 0)),
                   pl.BlockSpec((1, D), lambda i: (0, 0))],
        out_shape=[S((T, D), F32), S((1, D), F32)],
        compiler_params=_cp(1))(dgu, wup, x, g, dy)


def _tn_matmul(name, a, a_spec, b, out_shape, out_spec, n_blocks, scale=1.0, after=None):
    extra = [] if after is None else [after]

    def body(a_ref, b_ref, *rest):
        o_ref = rest[-1]
        o_ref[...] = (_tn(a_ref[...], b_ref[...]) * scale).astype(o_ref.dtype)

    return pl.pallas_call(
        body, name=name, grid=(n_blocks,),
        in_specs=[a_spec, pl.BlockSpec(b.shape, lambda j: (0, 0), pipeline_mode=pl.Buffered(1))]
        + [pl.BlockSpec((8, LANES), lambda j: (0, 0)) for _ in extra],
        out_specs=out_spec, out_shape=out_shape, compiler_params=_cp(1))(a, b, *extra)


def _ffn_dwup(name, h, dgu, after=None):
    T, D = h.shape
    FB = dgu.shape[-1]
    return _tn_matmul(
        name + "_dwup", dgu.reshape(2 * N_FFN_BLK, T, FB), pl.BlockSpec((None, T, FB), lambda j: (j, 0, 0)), h,
        S((2 * N_FFN_BLK, FB, D), BF), pl.BlockSpec((None, FB, D), lambda j: (j, 0, 0)), 2 * N_FFN_BLK,
        after=after)


def _ffn_dwdn(name, a, dyb):
    _, T, FB = a.shape
    D = dyb.shape[1]
    return _tn_matmul(
        name + "_dwdn", a, pl.BlockSpec((None, T, FB), lambda j: (j, 0, 0)), dyb,
        S((N_FFN_BLK, FB, D), BF), pl.BlockSpec((None, FB, D), lambda j: (j, 0, 0)), N_FFN_BLK, scale=0.5)


def _tri(n, lower):
    r = lax.broadcasted_iota(jnp.int32, (n, n), 0)
    c = lax.broadcasted_iota(jnp.int32, (n, n), 1)
    return (r >= c) if lower else (r <= c)


def _spatial_mix(vgn_b, ws_ref, bst, tm):
    tril = _tri(CHUNK, True)
    wms = [jnp.where(tril, ws_ref[g], 0.0).astype(BF) for g in range(GMLP_G)]
    rows = []
    for c in range(tm // CHUNK):
        cols = []
        for g in range(GMLP_G):
            vs = vgn_b[c * CHUNK:(c + 1) * CHUNK, g * GMLP_GD:(g + 1) * GMLP_GD]
            cols.append(_nn(wms[g], vs) + bst[:, g:g + 1])
        rows.append(jnp.concatenate(cols, axis=1))
    return jnp.concatenate(rows, axis=0), wms


HB = 128
AUG_W = FOX_HEADS * HB
COL_A, COL_B, COL_C = 64, 67, 70
RS_Q, RS_K, RS_V, RS_O = 0, 8, 16, 17


def _piece_matrix(col):
    r = jnp.arange(LANES)
    dst = jnp.where(r < 3 * FOX_HEADS, (r % FOX_HEADS) * HB + col + r // FOX_HEADS, -1)
    return (jnp.arange(AUG_W)[None, :] == dst[:, None]).astype(BF)


def _ones_row(cols):
    c = jnp.arange(AUG_W) % HB
    hit = functools.reduce(jnp.logical_or, [(c >= a) & (c < a + 3) for a in cols])
    return hit.astype(F32)[None, :]


def _pieces(x):
    lane = lax.broadcasted_iota(jnp.int32, x.shape, 1)
    x = jnp.where(lane < FOX_HEADS, x, 0.0)
    hi = x.astype(BF).astype(F32)
    r1 = x - hi
    mid = r1.astype(BF).astype(F32)
    lo = (r1 - mid).astype(BF).astype(F32)
    return (hi + pltpu.roll(mid, FOX_HEADS, 1) + pltpu.roll(lo, 2 * FOX_HEADS, 1)).astype(BF)


def _mix_prep(x, g_mix, wz, bf128, g_q, g_k, g_sgu, w_s, b_st, g_go):
    T, D = x.shape
    tm = _tile(T, 512)
    pc_q, pc_k = _piece_matrix(COL_A), _piece_matrix(COL_B)
    one_q, one_k, one_v = _ones_row([COL_B]), _ones_row([COL_A, COL_C]), _ones_row([COL_A])

    def body(x_ref, gm_ref, wz_ref, bf_ref, gq_ref, gk_ref, gs_ref, ws_ref, bst_ref, go_ref, pq_ref, pk_ref, oq_ref,
             ok_ref, ov_ref, z_ref, h_ref, q_ref, k_ref, v_ref, y_ref, rs_ref, carry_ref):
        i = pl.program_id(0)

        @pl.when(i == 0)
        def _():
            carry_ref[...] = jnp.zeros_like(carry_ref)

        xf = x_ref[...]
        hb = (xf * _rstd(xf) * gm_ref[...]).astype(BF)
        h_ref[...] = hb
        z_ref[...] = _nt(hb, wz_ref[...])

        fl = z_ref[:, Z_F:Z_F + LANES] + bf_ref[...]
        logf = jnp.minimum(fl, 0.0) - jnp.log1p(jnp.exp(-jnp.abs(fl)))
        csum = _hi(_tri(tm, True), logf) + carry_ref[...]
        carry_ref[...] = csum[tm - 1:tm, :]
        ext_q = (_nn(_pieces(csum), pq_ref[...]) + oq_ref[...]).astype(BF)
        ext_k = (_nn(_pieces(-csum), pk_ref[...]) + ok_ref[...]).astype(BF)
        ext_v = jnp.broadcast_to(ov_ref[...], (tm, AUG_W)).astype(BF)

        rs_ref[...] = jnp.zeros_like(rs_ref)
        for h in range(FOX_HEADS):
            lo, hi = slice(h * HB, h * HB + FOX_HD), slice(h * HB + FOX_HD, (h + 1) * HB)
            qh = z_ref[:, Z_Q + h * FOX_HD:Z_Q + (h + 1) * FOX_HD]
            kh = z_ref[:, Z_K + h * FOX_HD:Z_K + (h + 1) * FOX_HD]
            rq, rk = _rstd(qh), _rstd(kh)
            rs_ref[:, RS_Q + h:RS_Q + h + 1] = rq
            rs_ref[:, RS_K + h:RS_K + h + 1] = rk
            q_ref[:, lo] = (qh * rq * gq_ref[...] * 0.125).astype(BF)
            k_ref[:, lo] = (kh * rk * gk_ref[...]).astype(BF)
            v_ref[:, lo] = z_ref[:, Z_V + h * FOX_HD:Z_V + (h + 1) * FOX_HD].astype(BF)
            q_ref[:, hi] = ext_q[:, hi]
            k_ref[:, hi] = ext_k[:, hi]
            v_ref[:, hi] = ext_v[:, hi]

        u = _gelu(z_ref[:, Z_U:Z_U + GMLP_W])
        vg = _gelu(z_ref[:, Z_G:Z_G + GMLP_W])
        rv = _rstd(vg)
        vgn = (vg * rv * gs_ref[...]).astype(BF)
        mixed, _ = _spatial_mix(vgn, ws_ref, bst_ref[...], tm)
        sgu = u * mixed
        ro = _rstd(sgu)
        y_ref[...] = (sgu * ro * go_ref[...]).astype(BF)
        rs_ref[:, RS_V:RS_V + 1] = rv
        rs_ref[:, RS_O:RS_O + 1] = ro

    row = lambda i: (i, 0)
    fix2 = lambda i: (0, 0)
    return pl.pallas_call(
        body, name="mix_prep", grid=(T // tm,),
        in_specs=[pl.BlockSpec((tm, D), row), pl.BlockSpec((1, D), fix2),
                  pl.BlockSpec((ZW, D), fix2, pipeline_mode=pl.Buffered(1)),
                  pl.BlockSpec((1, LANES), fix2), pl.BlockSpec((1, FOX_HD), fix2), pl.BlockSpec((1, FOX_HD), fix2),
                  pl.BlockSpec((1, GMLP_W), fix2), pl.BlockSpec((GMLP_G, CHUNK, CHUNK), lambda i: (0, 0, 0)),
                  pl.BlockSpec((CHUNK, GMLP_G), fix2), pl.BlockSpec((1, GMLP_W), fix2),
                  pl.BlockSpec((LANES, AUG_W), fix2),
                  pl.BlockSpec((LANES, AUG_W), fix2), pl.BlockSpec((1, AUG_W), fix2), pl.BlockSpec((1, AUG_W), fix2),
                  pl.BlockSpec((1, AUG_W), fix2)],
        out_specs=[pl.BlockSpec((tm, ZW), row), pl.BlockSpec((tm, D), row),
                   pl.BlockSpec((tm, AUG_W), row), pl.BlockSpec((tm, AUG_W), row), pl.BlockSpec((tm, AUG_W), row),
                   pl.BlockSpec((tm, GMLP_W), row), pl.BlockSpec((tm, LANES), row)],
        out_shape=[S((T, ZW), F32), S((T, D), BF), S((T, AUG_W), BF), S((T, AUG_W), BF), S((T, AUG_W), BF),
                   S((T, GMLP_W), BF), S((T, LANES), F32)],
        scratch_shapes=[pltpu.VMEM((1, LANES), F32)],
        compiler_params=_cp(1))(x, g_mix, wz, bf128, g_q, g_k, g_sgu, w_s, b_st, g_go, pc_q, pc_k, one_q, one_k,
                                one_v)


def _fox_fwd(q, k, v):
    T = q.shape[0]
    tq = _tile(T, 1024)
    nq = T // tq

    def body(q_ref, k_ref, v_ref, o_ref, lse_ref, m_sc, acc_sc):
        i, j = pl.program_id(0), pl.program_id(1)

        @pl.when(j == 0)
        def _():
            m_sc[...] = jnp.full(m_sc.shape, NEG, F32)
            acc_sc[...] = jnp.zeros_like(acc_sc)

        def step(masked):
            mask = _tri(tq, True) if masked else None
            for h in range(FOX_HEADS):
                hb = slice(h * HB, (h + 1) * HB)
                s = _nt(q_ref[:, hb], k_ref[:, hb])
                if masked:
                    s = jnp.where(mask, s, NEG)
                m_prev = m_sc[h]
                m_new = jnp.maximum(m_prev, jnp.broadcast_to(jnp.max(s, axis=1, keepdims=True), (tq, HB)))
                p = jnp.exp(s - jnp.tile(m_new, (1, tq // HB))).astype(BF)
                acc_sc[:, hb] = jnp.exp(m_prev - m_new) * acc_sc[:, hb] + _nn(p, v_ref[:, hb])
                m_sc[h] = m_new

        @pl.when(j < i)
        def _():
            step(False)

        @pl.when(j == i)
        def _():
            step(True)
            lse_ref[...] = jnp.zeros_like(lse_ref)
            for h in range(FOX_HEADS):
                l = acc_sc[:, h * HB + COL_A:h * HB + COL_A + 1]
                o_ref[:, h * FOX_HD:(h + 1) * FOX_HD] = acc_sc[:, h * HB:h * HB + FOX_HD] / l
                lse_ref[:, h:h + 1] = m_sc[h][:, 0:1] + jnp.log(l)

    qi = lambda i, j: (i, 0)
    kj = lambda i, j: (jnp.minimum(i, j), 0)
    return pl.pallas_call(
        body, name="fox_fwd", grid=(nq, nq),
        in_specs=[pl.BlockSpec((tq, AUG_W), qi), pl.BlockSpec((tq, AUG_W), kj), pl.BlockSpec((tq, AUG_W), kj)],
        out_specs=[pl.BlockSpec((tq, FOX_W), qi), pl.BlockSpec((tq, LANES), qi)],
        out_shape=[S((T, FOX_W), F32), S((T, LANES), F32)],
        scratch_shapes=[pltpu.VMEM((FOX_HEADS, tq, HB), F32), pltpu.VMEM((tq, AUG_W), F32)],
        compiler_params=_cp(2))(q, k, v)


def _fox_bwd(q, k, v, dob):
    T = q.shape[0]
    tq = _tile(T, 512)
    nq = T // tq
    n_sweeps = 1
    half = AUG_W // n_sweeps
    hpg = FOX_HEADS // n_sweeps

    pairs = [(j, i) for j in range(nq) for i in range(j, nq)]
    jt = jnp.asarray([p[0] for p in pairs], jnp.int32)
    it = jnp.asarray([p[1] for p in pairs], jnp.int32)

    def body(jt_ref, it_ref, q_ref, k_ref, v_ref, do_ref, dq_ref, dk_ref, dv_ref, dq_sc):
        t = pl.program_id(1)
        j, i = jt_ref[t], it_ref[t]

        @pl.when(t == 0)
        def _():
            dq_sc[...] = jnp.zeros_like(dq_sc)

        @pl.when(i == j)
        def _():
            dk_ref[...] = jnp.zeros_like(dk_ref)
            dv_ref[...] = jnp.zeros_like(dv_ref)

        def step(masked):
            rows = pl.ds(pl.multiple_of(i * tq, tq), tq)
            mask = _tri(tq, True) if masked else None
            for h in range(hpg):
                hb = slice(h * HB, (h + 1) * HB)
                qh, kh, vh, doh = q_ref[:, hb], k_ref[:, hb], v_ref[:, hb], do_ref[:, hb]
                s = _nt(qh, kh)
                if masked:
                    s = jnp.where(mask, s, NEG)
                p = jnp.exp(s)
                dsb = (p * _nt(doh, vh)).astype(BF)
                dv_ref[:, hb] += _tn(p.astype(BF), doh)
                dk_ref[:, hb] += _tn(dsb, qh)
                dq_sc[rows, hb] += _nn(dsb, kh)

        @pl.when(i > j)
        def _():
            step(False)

        @pl.when(i == j)
        def _():
            step(True)
            dq_ref[...] = dq_sc[pl.ds(pl.multiple_of(j * tq, tq), tq), :]

    qi = pl.BlockSpec((tq, half), lambda g, t, jt_ref, it_ref: (it_ref[t], g))
    kj = pl.BlockSpec((tq, half), lambda g, t, jt_ref, it_ref: (jt_ref[t], g))
    return pl.pallas_call(
        body, name="fox_bwd",
        grid_spec=pltpu.PrefetchScalarGridSpec(
            num_scalar_prefetch=2, grid=(n_sweeps, len(pairs)), in_specs=[qi, kj, kj, qi], out_specs=[kj, kj, kj],
            scratch_shapes=[pltpu.VMEM((T, half), F32)]),
        out_shape=[S((T, AUG_W), F32), S((T, AUG_W), F32), S((T, AUG_W), F32)],
        compiler_params=_cp(2))(jt, it, q, k, v, dob)


def _mix_out(attn, yg, g_fo, wout, x):
    T, D = x.shape
    tm = _tile(T, 1024)

    def body(a_ref, y_ref, g_ref, w_ref, x_ref, o_ref):
        at = a_ref[...]
        yf = (at * _rstd(at) * g_ref[...]).astype(BF)
        o_ref[...] = x_ref[...] + _nn(yf, w_ref[:FOX_W, :]) + _nn(y_ref[...], w_ref[FOX_W:, :])

    row = lambda i: (i, 0)
    return pl.pallas_call(
        body, name="mix_out", grid=(T // tm,),
        in_specs=[pl.BlockSpec((tm, FOX_W), row), pl.BlockSpec((tm, GMLP_W), row),
                  pl.BlockSpec((1, FOX_W), lambda i: (0, 0)), pl.BlockSpec((D, D), lambda i: (0, 0)),
                  pl.BlockSpec((tm, D), row)],
        out_specs=pl.BlockSpec((tm, D), row),
        out_shape=S((T, D), F32),
        compiler_params=_cp(1))(attn, yg, g_fo, wout, x)


def _mix_out_bwd(dx, attn, yg, g_fo, wout, qf, lse):
    T, D = dx.shape
    tm = _tile(T, 512)
    n = T // tm
    pc_l, pc_d = _piece_matrix(COL_C), _piece_matrix(COL_A)

    def body(dx_ref, a_ref, y_ref, g_ref, w_ref, qf_ref, lse_ref, pl_ref, pd_ref,
             qb_ref, dob_ref, dyg_ref, dw_ref, dg_ref, acc_ref, dsum_ref):
        i = pl.program_id(0)
        dxb = dx_ref[...].astype(BF)
        at = a_ref[...]
        yf = (at * _rstd(at) * g_ref[...]).astype(BF)
        dy = _nt(dxb, w_ref[...])
        p_top = _tn(yf, dxb)
        p_bot = _tn(y_ref[...], dxb)

        @pl.when(i == 0)
        def _():
            acc_ref[:FOX_W, :] = p_top
            acc_ref[FOX_W:, :] = p_bot

        @pl.when(i > 0)
        def _():
            acc_ref[:FOX_W, :] += p_top
            acc_ref[FOX_W:, :] += p_bot

        @pl.when(i == n - 1)
        def _():
            dw_ref[...] = acc_ref[...].astype(BF)

        dat, dgr = _norm_bwd(dy[:, :FOX_W], at, g_ref[...])
        _acc_rows(dg_ref, i == 0, dgr)
        dyg_ref[...] = dy[:, FOX_W:]
        prod = dat * at
        dsum_ref[...] = jnp.zeros_like(dsum_ref)
        for h in range(FOX_HEADS):
            dsum_ref[:, h:h + 1] = jnp.sum(prod[:, h * FOX_HD:(h + 1) * FOX_HD], axis=1, keepdims=True)
        ext_d = _nn(_pieces(-dsum_ref[...]), pd_ref[...]).astype(BF)
        ext_l = _nn(_pieces(-lse_ref[...]), pl_ref[...])
        datb = dat.astype(BF)
        for h in range(FOX_HEADS):
            lo, hi = slice(h * HB, h * HB + FOX_HD), slice(h * HB + FOX_HD, (h + 1) * HB)
            dob_ref[:, lo] = datb[:, h * FOX_HD:(h + 1) * FOX_HD]
            dob_ref[:, hi] = ext_d[:, hi]
            qb_ref[:, lo] = qf_ref[:, lo]
            qb_ref[:, hi] = (qf_ref[:, hi].astype(F32) + ext_l[:, hi]).astype(BF)

    row = lambda i: (i, 0)
    fix = lambda i: (0, 0)
    return pl.pallas_call(
        body, name="mix_out_bwd", grid=(n,),
        in_specs=[pl.BlockSpec((tm, D), row), pl.BlockSpec((tm, FOX_W), row), pl.BlockSpec((tm, GMLP_W), row),
                  pl.BlockSpec((1, FOX_W), fix), pl.BlockSpec((D, D), fix), pl.BlockSpec((tm, AUG_W), row),
                  pl.BlockSpec((tm, LANES), row), pl.BlockSpec((LANES, AUG_W), fix),
                  pl.BlockSpec((LANES, AUG_W), fix)],
        out_specs=[pl.BlockSpec((tm, AUG_W), row), pl.BlockSpec((tm, AUG_W), row), pl.BlockSpec((tm, GMLP_W), row),
                   pl.BlockSpec((D, D), fix), pl.BlockSpec((1, FOX_W), fix)],
        out_shape=[S((T, AUG_W), BF), S((T, AUG_W), BF), S((T, GMLP_W), F32), S((D, D), BF), S((1, FOX_W), F32)],
        scratch_shapes=[pltpu.VMEM((D, D), F32), pltpu.VMEM((tm, LANES), F32)],
        compiler_params=_cp(1))(dx, attn, yg, g_fo, wout, qf, lse, pc_l, pc_d)


def _mix_prep_bwd(z, dq, dk, dv, dyg, rs, bf128, g_q, g_k, g_sgu, w_s, b_st, g_go):
    T = z.shape[0]
    tm = _tile(T, 512)
    n = T // tm

    def body(z_ref, dq_ref, dk_ref, dv_ref, dyg_ref, rs_ref, bf_ref, gq_ref, gk_ref, gs_ref, ws_ref,
             bst_ref, go_ref, dz_ref, dgq_ref, dgk_ref, dgs_ref, dgo_ref, dws_ref, dbst_ref, dbf_ref, carry_ref):
        i = pl.program_id(0)
        first = i == 0
        rs = rs_ref[...]

        @pl.when(first)
        def _():
            carry_ref[...] = jnp.zeros_like(carry_ref)

        lane = lax.broadcasted_iota(jnp.int32, (tm, LANES), 1)
        dc = jnp.zeros((tm, LANES), F32)
        gq_rows, gk_rows = [], []
        for h in range(FOX_HEADS):
            hp = slice(h * HB, h * HB + FOX_HD)
            dqh, gqr = _norm_bwd(dq_ref[:, hp] * 0.125, z_ref[:, Z_Q + h * FOX_HD:Z_Q + (h + 1) * FOX_HD], gq_ref[...],
                                 rs[:, RS_Q + h:RS_Q + h + 1])
            dkh, gkr = _norm_bwd(dk_ref[:, hp], z_ref[:, Z_K + h * FOX_HD:Z_K + (h + 1) * FOX_HD], gk_ref[...],
                                 rs[:, RS_K + h:RS_K + h + 1])
            dz_ref[:, Z_Q + h * FOX_HD:Z_Q + (h + 1) * FOX_HD] = dqh.astype(BF)
            dz_ref[:, Z_K + h * FOX_HD:Z_K + (h + 1) * FOX_HD] = dkh.astype(BF)
            dz_ref[:, Z_V + h * FOX_HD:Z_V + (h + 1) * FOX_HD] = dv_ref[:, hp].astype(BF)
            dch = dq_ref[:, h * HB + COL_A:h * HB + COL_A + 1] - dk_ref[:, h * HB + COL_B:h * HB + COL_B + 1]
            dc = jnp.where(lane == h, dch, dc)
            gq_rows.append(gqr)
            gk_rows.append(gkr)
        _acc_rows(dgq_ref, first, functools.reduce(lambda a, b: a + b, gq_rows))
        _acc_rows(dgk_ref, first, functools.reduce(lambda a, b: a + b, gk_rows))

        dlogf = _hi(_tri(tm, False), dc) + carry_ref[...]
        carry_ref[...] = dlogf[0:1, :]
        fl = z_ref[:, Z_F:Z_F + LANES] + bf_ref[...]
        lane = lax.broadcasted_iota(jnp.int32, (tm, LANES), 1)
        df = jnp.where(lane < FOX_HEADS, dlogf * jax.nn.sigmoid(-fl), 0.0)
        dz_ref[:, Z_F:Z_F + LANES] = df.astype(BF)
        _acc_rows(dbf_ref, first, df)

        u_pre = z_ref[:, Z_U:Z_U + GMLP_W]
        vg_pre = z_ref[:, Z_G:Z_G + GMLP_W]
        u = _gelu(u_pre)
        vg = _gelu(vg_pre)
        rv = rs[:, RS_V:RS_V + 1]
        vgn = (vg * rv * gs_ref[...]).astype(BF)
        bst = bst_ref[...]
        mixed, wms = _spatial_mix(vgn, ws_ref, bst, tm)
        sgu = u * mixed
        dsgu, gor = _norm_bwd(dyg_ref[...], sgu, go_ref[...], rs[:, RS_O:RS_O + 1])
        _acc_rows(dgo_ref, first, gor)
        du = dsgu * mixed
        dmixed = dsgu * u
        dmb = dmixed.astype(BF)
        tril = _tri(CHUNK, True)
        dvgn_rows = []
        dws = [None] * GMLP_G
        dbs = [None] * GMLP_G
        for c in range(tm // CHUNK):
            cs = slice(c * CHUNK, (c + 1) * CHUNK)
            cols = []
            for g in range(GMLP_G):
                gs = slice(g * GMLP_GD, (g + 1) * GMLP_GD)
                dmc = dmb[cs, gs]
                pw = _nt(dmc, vgn[cs, gs])
                pb = jnp.sum(dmixed[cs, gs], axis=1, keepdims=True)
                dws[g] = pw if dws[g] is None else dws[g] + pw
                dbs[g] = pb if dbs[g] is None else dbs[g] + pb
                cols.append(_tn(wms[g], dmc))
            dvgn_rows.append(jnp.concatenate(cols, axis=1))
        dvgn = jnp.concatenate(dvgn_rows, axis=0)
        dbs_t = jnp.concatenate(dbs, axis=1)
        for g in range(GMLP_G):
            dwg = jnp.where(tril, dws[g], 0.0)

            @pl.when(first)
            def _():
                dws_ref[g] = dwg

            @pl.when(jnp.logical_not(first))
            def _():
                dws_ref[g] += dwg

        @pl.when(first)
        def _():
            dbst_ref[...] = dbs_t

        @pl.when(jnp.logical_not(first))
        def _():
            dbst_ref[...] += dbs_t

        dvg, gsr = _norm_bwd(dvgn, vg, gs_ref[...], rv)
        _acc_rows(dgs_ref, first, gsr)
        dz_ref[:, Z_U:Z_U + GMLP_W] = (du * _gelu_grad(u_pre)).astype(BF)
        dz_ref[:, Z_G:Z_G + GMLP_W] = (dvg * _gelu_grad(vg_pre)).astype(BF)

    rev = lambda i: (n - 1 - i, 0)
    fix = lambda i: (0, 0)
    fix3 = lambda i: (0, 0, 0)
    return pl.pallas_call(
        body, name="mix_prep_bwd", grid=(n,),
        in_specs=[pl.BlockSpec((tm, ZW), rev), pl.BlockSpec((tm, AUG_W), rev), pl.BlockSpec((tm, AUG_W), rev),
                  pl.BlockSpec((tm, AUG_W), rev), pl.BlockSpec((tm, GMLP_W), rev), pl.BlockSpec((tm, LANES), rev),
                  pl.BlockSpec((1, LANES), fix), pl.BlockSpec((1, FOX_HD), fix), pl.BlockSpec((1, FOX_HD), fix),
                  pl.BlockSpec((1, GMLP_W), fix), pl.BlockSpec((GMLP_G, CHUNK, CHUNK), fix3),
                  pl.BlockSpec((CHUNK, GMLP_G), fix), pl.BlockSpec((1, GMLP_W), fix)],
        out_specs=[pl.BlockSpec((tm, ZW), rev), pl.BlockSpec((1, FOX_HD), fix), pl.BlockSpec((1, FOX_HD), fix),
                   pl.BlockSpec((1, GMLP_W), fix), pl.BlockSpec((1, GMLP_W), fix),
                   pl.BlockSpec((GMLP_G, CHUNK, CHUNK), fix3), pl.BlockSpec((CHUNK, GMLP_G), fix),
                   pl.BlockSpec((1, LANES), fix)],
        out_shape=[S((T, ZW), BF), S((1, FOX_HD), F32), S((1, FOX_HD), F32), S((1, GMLP_W), F32), S((1, GMLP_W), F32),
                   S((GMLP_G, CHUNK, CHUNK), F32), S((CHUNK, GMLP_G), F32), S((1, LANES), F32)],
        scratch_shapes=[pltpu.VMEM((1, LANES), F32)],
        compiler_params=_cp(1))(z, dq, dk, dv, dyg, rs, bf128, g_q, g_k, g_sgu, w_s, b_st, g_go)


def _mix_proj_bwd(dz, wz, x, g, dy):
    T, D = x.shape
    tm = _tile(T, 512)

    def body(dz_ref, w_ref, x_ref, g_ref, dy_ref, dx_ref, dxb_ref, dg_ref):
        dh = _nn(dz_ref[...], w_ref[...])
        dx, dgr = _norm_bwd(dh, x_ref[...], g_ref[...])
        dx = dx + dy_ref[...]
        dx_ref[...] = dx
        dxb_ref[...] = dx.astype(BF)
        _acc_rows(dg_ref, pl.program_id(0) == 0, dgr)

    row = lambda i: (i, 0)
    fix = lambda i: (0, 0)
    return pl.pallas_call(
        body, name="mix_proj_bwd", grid=(T // tm,),
        in_specs=[pl.BlockSpec((tm, ZW), row), pl.BlockSpec((ZW, D), fix), pl.BlockSpec((tm, D), row),
                  pl.BlockSpec((1, D), fix), pl.BlockSpec((tm, D), row)],
        out_specs=[pl.BlockSpec((tm, D), row), pl.BlockSpec((tm, D), row), pl.BlockSpec((1, D), fix)],
        out_shape=[S((T, D), F32), S((T, D), BF), S((1, D), F32)],
        compiler_params=_cp(1))(dz, wz, x, g, dy)


def _ca_kv(mem, g_mem, wckv, g_ck):
    M, D = mem.shape

    def body(m_ref, g_ref, w_ref, gk_ref, mn_ref, kr_ref, kn_ref, v_ref):
        mf = m_ref[...]
        mn = (mf * _rstd(mf) * g_ref[...]).astype(BF)
        mn_ref[...] = mn
        for h in range(CA_HEADS):
            kr = _nn(mn, w_ref[h])
            kr_ref[h] = kr
            kn_ref[h] = (kr * _rstd(kr) * gk_ref[...]).astype(BF)
            v_ref[h] = _nn(mn, w_ref[CA_HEADS + h]).astype(BF)

    hd = (CA_HEADS, M, CA_HD)
    return pl.pallas_call(
        body, name="ca_kv", out_shape=[S((M, D), BF), S(hd, F32), S(hd, BF), S(hd, BF)],
        compiler_params=pltpu.CompilerParams(vmem_limit_bytes=VMEM_LIMIT))(mem, g_mem, wckv, g_ck)


def _ca_tile_fwd(xt, gca, wcq, gcq, kn_ref, v_ref):
    hb = (xt * _rstd(xt) * gca).astype(BF)
    qc = _nn(hb, wcq)
    qr, qn, ps = [], [], []
    for h in range(CA_HEADS):
        qh = qc[:, h * CA_HD:(h + 1) * CA_HD]
        qnh = (qh * _rstd(qh) * gcq * 0.0625).astype(BF)
        s = _nt(qnh, kn_ref[h])
        e = jnp.exp(s - jnp.max(s, axis=1, keepdims=True))
        ps.append(e / jnp.sum(e, axis=1, keepdims=True))
        qr.append(qh)
        qn.append(qnh)
    return hb, qr, qn, ps


def _ca_fwd(x, g_ca, wcq, g_cq, kn, vv, wco):
    T, D = x.shape
    M = kn.shape[1]
    tm = _tile(T, 1024)

    def body(x_ref, gca_ref, wcq_ref, gcq_ref, kn_ref, v_ref, wco_ref, o_ref, ob_sc):
        xt = x_ref[...]
        _, _, _, ps = _ca_tile_fwd(xt, gca_ref[...], wcq_ref[...], gcq_ref[...], kn_ref, v_ref)
        for h in range(CA_HEADS):
            ob_sc[:, h * CA_HD:(h + 1) * CA_HD] = _nn(ps[h].astype(BF), v_ref[h]).astype(BF)
        o_ref[...] = xt + _nn(ob_sc[...], wco_ref[...])

    row = lambda i: (i, 0)
    fix = lambda i: (0, 0)
    fix3 = lambda i: (0, 0, 0)
    return pl.pallas_call(
        body, name="ca_fwd", grid=(T // tm,),
        in_specs=[pl.BlockSpec((tm, D), row), pl.BlockSpec((1, D), fix), pl.BlockSpec((D, D), fix),
                  pl.BlockSpec((1, CA_HD), fix), pl.BlockSpec((CA_HEADS, M, CA_HD), fix3),
                  pl.BlockSpec((CA_HEADS, M, CA_HD), fix3), pl.BlockSpec((D, D), fix)],
        out_specs=pl.BlockSpec((tm, D), row), out_shape=S((T, D), F32),
        scratch_shapes=[pltpu.VMEM((tm, D), BF)],
        compiler_params=_cp(1))(x, g_ca, wcq, g_cq, kn, vv, wco)


def _ca_bwd(x, dy, g_ca, wcq, g_cq, kn, vv, wco):
    T, D = x.shape
    M = kn.shape[1]
    tm = _tile(T, 512)
    n = T // tm

    def body(x_ref, dy_ref, gca_ref, wcq_ref, gcq_ref, kn_ref, v_ref, wco_ref,
             dx_ref, dwq_ref, dwo_ref, dkn_ref, dv_ref, dgcq_ref, dgca_ref, aq_sc, ao_sc, ob_sc, dq_sc):
        i = pl.program_id(0)
        first = i == 0
        xt = x_ref[...]
        dyt = dy_ref[...]
        dyb = dyt.astype(BF)
        hb, qr, qn, ps = _ca_tile_fwd(xt, gca_ref[...], wcq_ref[...], gcq_ref[...], kn_ref, v_ref)
        do = _nt(dyb, wco_ref[...])
        gcq_rows = None
        for h in range(CA_HEADS):
            hs = slice(h * CA_HD, (h + 1) * CA_HD)
            p = ps[h]
            pb = p.astype(BF)
            ob_sc[:, hs] = _nn(pb, v_ref[h]).astype(BF)
            doh = do[:, hs].astype(BF)
            dp = _nt(doh, v_ref[h])
            ds = (p * (dp - jnp.sum(dp * p, axis=1, keepdims=True))).astype(BF)
            dvh = _tn(pb, doh)
            dkh = _tn(ds, qn[h])

            @pl.when(first)
            def _():
                dv_ref[h] = dvh
                dkn_ref[h] = dkh

            @pl.when(jnp.logical_not(first))
            def _():
                dv_ref[h] += dvh
                dkn_ref[h] += dkh

            dqn = _nn(ds, kn_ref[h]) * 0.0625
            dqh, gr = _norm_bwd(dqn, qr[h], gcq_ref[...])
            gcq_rows = gr if gcq_rows is None else gcq_rows + gr
            dq_sc[:, hs] = dqh.astype(BF)
        _acc_rows(dgcq_ref, first, gcq_rows)
        dqb = dq_sc[...]
        p_o = _tn(ob_sc[...], dyb)
        p_q = _tn(hb, dqb)

        @pl.when(first)
        def _():
            ao_sc[...] = p_o
            aq_sc[...] = p_q

        @pl.when(jnp.logical_not(first))
        def _():
            ao_sc[...] += p_o
            aq_sc[...] += p_q

        @pl.when(i == n - 1)
        def _():
            dwo_ref[...] = ao_sc[...].astype(BF)
            dwq_ref[...] = aq_sc[...].astype(BF)

        dh = _nt(dqb, wcq_ref[...])
        dx, gar = _norm_bwd(dh, xt, gca_ref[...])
        dx_ref[...] = dx + dyt
        _acc_rows(dgca_ref, first, gar)

    row = lambda i: (i, 0)
    fix = lambda i: (0, 0)
    fix3 = lambda i: (0, 0, 0)
    hd = (CA_HEADS, M, CA_HD)
    return pl.pallas_call(
        body, name="ca_bwd", grid=(n,),
        in_specs=[pl.BlockSpec((tm, D), row), pl.BlockSpec((tm, D), row), pl.BlockSpec((1, D), fix),
                  pl.BlockSpec((D, D), fix), pl.BlockSpec((1, CA_HD), fix), pl.BlockSpec(hd, fix3),
                  pl.BlockSpec(hd, fix3), pl.BlockSpec((D, D), fix)],
        out_specs=[pl.BlockSpec((tm, D), row), pl.BlockSpec((D, D), fix), pl.BlockSpec((D, D), fix),
                   pl.BlockSpec(hd, fix3), pl.BlockSpec(hd, fix3), pl.BlockSpec((1, CA_HD), fix),
                   pl.BlockSpec((1, D), fix)],
        out_shape=[S((T, D), F32), S((D, D), BF), S((D, D), BF), S(hd, F32), S(hd, F32), S((1, CA_HD), F32),
                   S((1, D), F32)],
        scratch_shapes=[pltpu.VMEM((D, D), F32), pltpu.VMEM((D, D), F32), pltpu.VMEM((tm, D), BF),
                        pltpu.VMEM((tm, D), BF)],
        compiler_params=_cp(1))(x, dy, g_ca, wcq, g_cq, kn, vv, wco)


def _ca_kv_bwd(mem, g_mem, mn, kraw, dkn, dvv, wckv, g_ck):
    M, D = mem.shape

    def body(m_ref, g_ref, mn_ref, kr_ref, dkn_ref, dv_ref, w_ref, gk_ref, dw_ref, dgk_ref, dgm_ref):
        mn = mn_ref[...]
        dmn = jnp.zeros((M, D), F32)
        gk_rows = None
        for h in range(CA_HEADS):
            dkr, gr = _norm_bwd(dkn_ref[h], kr_ref[h], gk_ref[...])
            gk_rows = gr if gk_rows is None else gk_rows + gr
            dkb = dkr.astype(BF)
            dvb = dv_ref[h].astype(BF)
            dw_ref[h] = _tn(mn, dkb).astype(BF)
            dw_ref[CA_HEADS + h] = _tn(mn, dvb).astype(BF)
            dmn = dmn + _nt(dkb, w_ref[h]) + _nt(dvb, w_ref[CA_HEADS + h])
        dgk_ref[...] = jnp.sum(gk_rows, axis=0, keepdims=True)
        mf = m_ref[...]
        dgm_ref[...] = jnp.sum(dmn * (mf * _rstd(mf)), axis=0, keepdims=True)

    return pl.pallas_call(
        body, name="ca_kv_bwd",
        out_shape=[S((2 * CA_HEADS, D, CA_HD), BF), S((1, CA_HD), F32), S((1, D), F32)],
        compiler_params=pltpu.CompilerParams(vmem_limit_bytes=VMEM_LIMIT))(mem, g_mem, mn, kraw, dkn, dvv, wckv, g_ck)


def _after(g, token):
    return g if token is None else g + token[0:1, 0:1]


def _local_step(x, mem, target, small, weights, emit):
    T, D = x.shape
    p = small
    bf128 = jnp.pad(p["b_f"], ((0, 0), (0, LANES - FOX_HEADS)))
    b_st = p["b_s"].T

    wup1 = weights("ffn1_up", x)["wup1"]
    a1, h1 = _ffn_up("ffn1_up", x, p["g_ffn1"], wup1)
    wdn1 = weights("ffn1_dn", h1)["wdn1"]
    x1 = _ffn_down("ffn1_down", a1, wdn1, x)
    wm = weights("mix", x1)
    z, h2, qf, ka, va, yg, rs = _mix_prep(x1, p["g_mix"], wm["wz"], bf128, p["g_q"], p["g_k"], p["g_sgu"], p["w_s"],
                                          b_st, p["g_gmlp_o"])
    attn, lse = _fox_fwd(qf, ka, va)
    x2 = _mix_out(attn, yg, p["g_fox_o"], wm["wout"], x1)
    wc = weights("ca", x2)
    mn, kraw, ckn, cvv = _ca_kv(mem, p["g_mem"], wc["wckv"], p["g_ck"])
    x3 = _ca_fwd(x2, p["g_ca"], wc["wcq"], p["g_cq"], ckn, cvv, wc["wco"])
    w2 = weights("ffn2", x3)
    a2, h4 = _ffn_up("ffn2_up", x3, p["g_ffn2"], w2["wup2"])
    dy4, dy4b, sq = _ffn_down_loss("ffn2_down", a2, w2["wdn2"], x3, target)

    gs = {}
    dgu2 = _ffn_bwd_act("ffn2_bwd_act", dy4b, h4, w2["wup2"], w2["wdn2"])
    tok = emit("ffn2", {"wup2": _ffn_dwup("ffn2", h4, dgu2), "wdn2": _ffn_dwdn("ffn2", a2, dy4b)})
    dx3, gs["g_ffn2"] = _ffn_dx("ffn2_dx", dgu2, w2["wup2"], x3, _after(p["g_ffn2"], tok), dy4)

    dx2, dwcq, dwco, dckn, dcvv, gs["g_cq"], gs["g_ca"] = _ca_bwd(
        x2, dx3, p["g_ca"], wc["wcq"], p["g_cq"], ckn, cvv, wc["wco"])
    dwckv, gs["g_ck"], gs["g_mem"] = _ca_kv_bwd(mem, p["g_mem"], mn, kraw, dckn, dcvv, wc["wckv"], p["g_ck"])

    qb, dob, dyg, dwout, gs["g_fox_o"] = _mix_out_bwd(dx2, attn, yg, p["g_fox_o"], wm["wout"], qf, lse)
    dq, dk, dv = _fox_bwd(qb, ka, va, dob)
    dz, gs["g_q"], gs["g_k"], gs["g_sgu"], gs["g_gmlp_o"], gs["w_s"], dbst, dbf = _mix_prep_bwd(
        z, dq, dk, dv, dyg, rs, bf128, p["g_q"], p["g_k"], p["g_sgu"], p["w_s"], b_st, p["g_gmlp_o"])
    gs["b_s"] = dbst.T
    gs["b_f"] = dbf[:, :FOX_HEADS]
    tok_ws = emit("w_s", {"w_s": gs["w_s"]})
    zb = ZW // 3
    dwz = _tn_matmul("mix_dwz", dz, pl.BlockSpec((T, zb), lambda j: (0, j)), h2,
                     S((ZW, D), BF), pl.BlockSpec((zb, D), lambda j: (j, 0)), 3)
    tok = emit("mid", {"wcq": dwcq, "wco": dwco, "wckv": dwckv, "wout": dwout, "wz": dwz})
    dx1, dx1b, gs["g_mix"] = _mix_proj_bwd(dz, wm["wz"], x1, _after(_after(p["g_mix"], tok), tok_ws), dx2)

    dgu1 = _ffn_bwd_act("ffn1_bwd_act", dx1b, h1, wup1, wdn1)
    tok = emit("ffn1_dn", {"wdn1": _ffn_dwdn("ffn1", a1, dx1b)})
    tok = emit("ffn1_up", {"wup1": _ffn_dwup("ffn1", h1, dgu1, after=tok)})
    dx0, gs["g_ffn1"] = _ffn_dx("ffn1_dx", dgu1, wup1, x, _after(p["g_ffn1"], tok), dx1)
    return sq, dx0, gs


MESH = pl.DeviceIdType.MESH
HBM_SPEC = pl.BlockSpec(memory_space=pltpu.HBM)
N_PEER = N_DEV - 1


def _place():
    return lax.axis_index("x"), lax.axis_index("y"), lax.axis_index("c")


def _slot(px, py, pc):
    return 4 * px + 2 * py + pc


SEM_SPEC = pl.BlockSpec(memory_space=pltpu.SEMAPHORE)
ANY_SPEC = pl.BlockSpec(memory_space=pl.ANY)
DATAFLOW = pltpu.SideEffectType.DATAFLOW_SIDE_EFFECTING


def _hbm(a):
    return pltpu.with_memory_space_constraint(a, pltpu.HBM)


def _peer(x, y, c, r):
    return (1 - x if r & 4 else x, 1 - y if r & 2 else y, 1 - c if r & 1 else c)


def _place_own(srcs, whole):
    my = _slot(*_place())
    lands = []
    for s in srcs:
        blk = s[None] if whole else lax.dynamic_slice_in_dim(s, my, 1, 0)
        shape = (N_DEV,) + s.shape if whole else s.shape
        lands.append(lax.dynamic_update_slice_in_dim(lax.empty(shape, s.dtype), blk, my, 0))
    return lands


ALL_PEERS = tuple(range(1, N_DEV))
NEAR_PEERS = (1, 2, 4, 6)
SAME_CORE = (2, 4, 6)


def _copy_start(name, srcs, lands, whole, peers=None):
    n = len(srcs)
    peers = peers or [ALL_PEERS] * n

    def body(*refs):
        src, land = refs[:n], refs[n:2 * n]
        send, recv = refs[2 * n:3 * n], refs[3 * n:4 * n]
        token = refs[6 * n]
        x, y, c = _place()
        my = _slot(x, y, c)
        for a in range(n):
            for r in peers[a]:
                p = _peer(x, y, c, r)
                pltpu.make_async_remote_copy(
                    src_ref=src[a] if whole else src[a].at[_slot(*p)], dst_ref=land[a].at[my],
                    send_sem=send[a].at[r - 1], recv_sem=recv[a].at[r - 1], device_id=p, device_id_type=MESH).start()
        token[...] = jnp.zeros_like(token)

    out = pl.pallas_call(
        body, name=name,
        out_shape=([pltpu.SemaphoreType.DMA((N_PEER,))] * (2 * n)
                   + [pltpu.HBM(s.shape, s.dtype) for s in srcs] + [pltpu.HBM(s.shape, s.dtype) for s in lands]
                   + [S((8, LANES), F32)]),
        in_specs=[HBM_SPEC] * (2 * n),
        out_specs=[SEM_SPEC] * (2 * n) + [HBM_SPEC] * (2 * n) + [pl.BlockSpec(memory_space=pltpu.VMEM)],
        input_output_aliases={i: 2 * n + i for i in range(2 * n)},
        compiler_params=pltpu.CompilerParams(has_side_effects=DATAFLOW),
    )(*[_hbm(s) for s in srcs], *[_hbm(s) for s in lands])
    return out[:n], out[n:2 * n], out[2 * n:3 * n], out[3 * n:4 * n], out[4 * n]


def _copy_wait(name, srcs, lands, send, recv, after, whole, peers=None, with_srcs=False):
    n = len(srcs)
    peers = peers or [ALL_PEERS] * n

    def body(*refs):
        src, land = refs[:n], refs[n:2 * n]
        snd, rcv = refs[2 * n:3 * n], refs[3 * n:4 * n]
        x, y, c = _place()
        for a in range(n):
            for r in peers[a]:
                p = _peer(x, y, c, r)
                ps = _slot(*p)
                cp = pltpu.make_async_remote_copy(
                    src_ref=src[a] if whole else src[a].at[ps], dst_ref=land[a].at[ps],
                    send_sem=snd[a].at[r - 1], recv_sem=rcv[a].at[r - 1], device_id=p, device_id_type=MESH)
                cp.wait_send()
                cp.wait_recv()

    out = pl.pallas_call(
        body, name=name,
        out_shape=[pltpu.HBM(s.shape, s.dtype) for s in srcs] + [pltpu.HBM(s.shape, s.dtype) for s in lands],
        in_specs=[HBM_SPEC] * (2 * n) + [SEM_SPEC] * (2 * n) + [ANY_SPEC],
        out_specs=[HBM_SPEC] * (2 * n),
        input_output_aliases={i: i for i in range(2 * n)},
        compiler_params=pltpu.CompilerParams(has_side_effects=DATAFLOW),
    )(*srcs, *lands, *send, *recv, after)
    return (out[:n], out[n:]) if with_srcs else out[n:]


def _forward_start(name, lands):
    n = len(lands)

    def body(*refs):
        land = refs[:n]
        send, recv = refs[n:2 * n], refs[2 * n:3 * n]
        token = refs[4 * n]
        x, y, c = _place()
        for a in range(n):
            for r in SAME_CORE:
                blk = land[a].at[_slot(*_peer(x, y, c, r))]
                pltpu.make_async_remote_copy(
                    src_ref=blk, dst_ref=blk, send_sem=send[a].at[r - 1], recv_sem=recv[a].at[r - 1],
                    device_id=(x, y, 1 - c), device_id_type=MESH).start()
        token[...] = jnp.zeros_like(token)

    out = pl.pallas_call(
        body, name=name,
        out_shape=([pltpu.SemaphoreType.DMA((N_PEER,))] * (2 * n) + [pltpu.HBM(s.shape, s.dtype) for s in lands]
                   + [S((8, LANES), F32)]),
        in_specs=[HBM_SPEC] * n,
        out_specs=[SEM_SPEC] * (2 * n) + [HBM_SPEC] * n + [pl.BlockSpec(memory_space=pltpu.VMEM)],
        input_output_aliases={i: 2 * n + i for i in range(n)},
        compiler_params=pltpu.CompilerParams(has_side_effects=DATAFLOW),
    )(*[_hbm(s) for s in lands])
    return out[:n], out[n:2 * n], out[2 * n:3 * n], out[3 * n]


def _forward_wait(name, lands, send, recv, after):
    n = len(lands)

    def body(*refs):
        land = refs[:n]
        snd, rcv = refs[n:2 * n], refs[2 * n:3 * n]
        x, y, c = _place()
        for a in range(n):
            for r in SAME_CORE:
                cp = pltpu.make_async_remote_copy(
                    src_ref=land[a].at[_slot(*_peer(x, y, c, r))], dst_ref=land[a].at[_slot(*_peer(x, y, c, r | 1))],
                    send_sem=snd[a].at[r - 1], recv_sem=rcv[a].at[r - 1], device_id=(x, y, 1 - c),
                    device_id_type=MESH)
                cp.wait_send()
                cp.wait_recv()

    return pl.pallas_call(
        body, name=name,
        out_shape=[pltpu.HBM(s.shape, s.dtype) for s in lands],
        in_specs=[HBM_SPEC] * n + [SEM_SPEC] * (2 * n) + [ANY_SPEC],
        out_specs=[HBM_SPEC] * n,
        input_output_aliases={i: i for i in range(n)},
        compiler_params=pltpu.CompilerParams(has_side_effects=DATAFLOW),
    )(*lands, *send, *recv, after)


def _adamw(w, g, m, v):
    m2 = ADAM_B1 * m + (1.0 - ADAM_B1) * g
    v2 = ADAM_B2 * v + (1.0 - ADAM_B2) * (g * g)
    m_hat = m2 / (1.0 - ADAM_B1 ** ADAM_STEP)
    v_hat = v2 / (1.0 - ADAM_B2 ** ADAM_STEP)
    delta = -ADAM_LR * (m_hat / (jnp.sqrt(v_hat) + ADAM_EPS) + ADAM_WD * w)
    return delta, m2, v2


def _adamw_big(name, slots, w, m, v, own=None):
    R, C = w.shape
    tr = next((t for t in (256, 352) if R % t == 0), R)

    def finish(g, w_ref, m_ref, v_ref, g_ref, d_ref, m2_ref, v2_ref):
        d, m2, v2 = _adamw(w_ref[...], g, m_ref[...], v_ref[...])
        g_ref[...] = g
        d_ref[...] = d
        m2_ref[...] = m2
        v2_ref[...] = v2

    if own is None:
        def body(s_ref, *refs):
            g = s_ref[0].astype(F32)
            for k in range(1, N_DEV):
                g = g + s_ref[k].astype(F32)
            finish(g, *refs)

        row = pl.BlockSpec((tr, C), lambda i: (i, 0))
        return pl.pallas_call(
            body, name=name, grid=(R // tr,),
            in_specs=[pl.BlockSpec((N_DEV, tr, C), lambda i: (0, i, 0)), row, row, row],
            out_specs=[row] * 4, out_shape=[S((R, C), F32)] * 4,
            compiler_params=_cp(1))(slots, w, m, v)

    def body(my_ref, s_ref, own_ref, *refs):
        mine = own_ref[...]
        g = None
        for k in range(N_DEV):
            part = jnp.where(my_ref[0] == k, mine, s_ref[k]).astype(F32)
            g = part if g is None else g + part
        finish(g, *refs)

    row = pl.BlockSpec((tr, C), lambda i, my_ref: (i, 0))
    my = jnp.reshape(_slot(*_place()), (1,)).astype(jnp.int32)
    return pl.pallas_call(
        body, name=name,
        grid_spec=pltpu.PrefetchScalarGridSpec(
            num_scalar_prefetch=1, grid=(R // tr,),
            in_specs=[pl.BlockSpec((N_DEV, tr, C), lambda i, my_ref: (0, i, 0)),
                      pl.BlockSpec((None, tr, C), lambda i, my_ref: (my_ref[0], i, 0)), row, row, row],
            out_specs=[row] * 4),
        out_shape=[S((R, C), F32)] * 4, compiler_params=_cp(1))(my, slots, own, w, m, v)


TINY_ROWS = (("b_s", 8), ("g_ffn1", 8), ("g_mix", 8), ("g_ca", 8), ("g_mem", 8), ("g_ffn2", 8), ("g_sgu", 4),
             ("g_fox_o", 4), ("g_gmlp_o", 4), ("g_cq", 2), ("g_ck", 2), ("g_q", 1), ("g_k", 1), ("b_f", 1),
             ("loss", 1))
TINY_P = 72


def _tiny_pieces(width):
    return [(j, slice(j * LANES, min((j + 1) * LANES, width))) for j in range(-(-width // LANES))]


def _pack_tiny(grads, sq):
    names = [n for n, _ in TINY_ROWS if n != "loss"]

    def body(*refs):
        ins, sq_ref, o_ref = refs[:len(names)], refs[len(names)], refs[len(names) + 1]
        o_ref[...] = jnp.zeros_like(o_ref)
        at = 0
        for ref, (name, r) in zip(ins, TINY_ROWS):
            if name == "b_s":
                o_ref[at:at + r, :] = ref[...]
            else:
                for j, cols in _tiny_pieces(ref.shape[1]):
                    o_ref[at + j:at + j + 1, 0:cols.stop - cols.start] = ref[:, cols]
            at += r
        o_ref[at:at + 1, :] = sq_ref[0:1, :]

    return pl.pallas_call(body, name="tiny_pack", out_shape=S((TINY_P, LANES), F32))(
        *[grads[n] for n in names], sq)


def _adamw_tiny(slots, w, m, v):
    names = [n for n, _ in TINY_ROWS if n != "loss"]
    k = len(names)

    def body(s_ref, *refs):
        ins, outs, loss_ref = refs[:3 * k], refs[3 * k:7 * k], refs[7 * k]
        g_all = s_ref[0]
        for d in range(1, N_DEV):
            g_all = g_all + s_ref[d]
        at = 0
        for i, (name, r) in enumerate(TINY_ROWS[:k]):
            w_ref, m_ref, v_ref = ins[i], ins[k + i], ins[2 * k + i]
            o = outs[4 * i:4 * i + 4]
            if name == "b_s":
                pieces = [(slice(at, at + r), slice(0, LANES), (slice(None), slice(None)))]
            else:
                pieces = [(slice(at + j, at + j + 1), slice(0, c.stop - c.start), (slice(None), c))
                          for j, c in _tiny_pieces(w_ref.shape[1])]
            for rows, lanes, dst in pieces:
                g = g_all[rows, lanes]
                res = (g,) + _adamw(w_ref[dst], g, m_ref[dst], v_ref[dst])
                for ref, val in zip(o, res):
                    ref[dst] = val
            at += r
        loss_ref[...] = g_all[at:at + 1, :]

    shapes = [S(w[n].shape, F32) for n in names]
    out = pl.pallas_call(
        body, name="adamw_tiny", out_shape=[s for s in shapes for _ in range(4)] + [S((1, LANES), F32)],
    )(slots, *[w[n] for n in names], *[m[n] for n in names], *[v[n] for n in names])
    stores = ({}, {}, {}, {})
    for i, n in enumerate(names):
        for store, t in zip(stores, out[4 * i:4 * i + 4]):
            store[n] = t
    return stores, out[4 * k]


WEIGHTS =('g_ffn1', 'w_ffn1_in', 'w_ffn1_out', 'g_mix', 'w_in', 'b_f', 'g_q', 'g_k', 'g_sgu', 'w_s', 'b_s',
           'g_fox_o', 'g_gmlp_o', 'w_out', 'g_ca', 'g_mem', 'w_cq', 'w_ckv', 'g_cq', 'g_ck', 'w_co', 'g_ffn2',
           'w_ffn2_in', 'w_ffn2_out')
BIG = ('w_ffn1_in', 'w_ffn1_out', 'w_in', 'w_out', 'w_cq', 'w_ckv', 'w_co', 'w_ffn2_in', 'w_ffn2_out')
TRANSPOSED = ('w_ffn1_in', 'w_in', 'w_ffn2_in')
TWO_LEVEL = ('w_ffn1_in', 'w_in')
GATHER_GROUPS = {"ffn1_up": ("w_ffn1_in",), "ffn1_dn": ("w_ffn1_out",), "mix": ("w_in", "w_out"),
                 "ca": ("w_cq", "w_ckv", "w_co"), "ffn2": ("w_ffn2_in", "w_ffn2_out")}
QKV_W = 3 * FOX_W
UV_OFF = QKV_W + FOX_HEADS


def kernel(x, mem, g_ffn1, w_ffn1_in, w_ffn1_out, g_mix, w_in, b_f, g_q, g_k, g_sgu, w_s, b_s, g_fox_o, g_gmlp_o, w_out, g_ca, g_mem, w_cq, w_ckv, g_cq, g_ck, w_co, g_ffn2, w_ffn2_in, w_ffn2_out, loss_target, m_g_ffn1, m_w_ffn1_in, m_w_ffn1_out, m_g_mix, m_w_in, m_b_f, m_g_q, m_g_k, m_g_sgu, m_w_s, m_b_s, m_g_fox_o, m_g_gmlp_o, m_w_out, m_g_ca, m_g_mem, m_w_cq, m_w_ckv, m_g_cq, m_g_ck, m_w_co, m_g_ffn2, m_w_ffn2_in, m_w_ffn2_out, v_g_ffn1, v_w_ffn1_in, v_w_ffn1_out, v_g_mix, v_w_in, v_b_f, v_g_q, v_g_k, v_g_sgu, v_w_s, v_b_s, v_g_fox_o, v_g_gmlp_o, v_w_out, v_g_ca, v_g_mem, v_w_cq, v_w_ckv, v_g_cq, v_g_ck, v_w_co, v_g_ffn2, v_w_ffn2_in, v_w_ffn2_out):
    args = dict(locals())
    w = {n: args[n] for n in WEIGHTS}
    mo = {n: args["m_" + n] for n in WEIGHTS}
    vo = {n: args["v_" + n] for n in WEIGHTS}
    D = D_MODEL

    def local(n, a):
        return a[0].T if n in TRANSPOSED else a[0]

    g_peers = [NEAR_PEERS if n in TWO_LEVEL else ALL_PEERS for n in BIG]
    handles = {}

    def start_gather(name, names, arrays):
        snd, rcv, src, land, token = _copy_start(name, arrays, _place_own(arrays, True), True,
                                                 peers=[g_peers[BIG.index(n)] for n in names])
        handles.update({n: (src[i], land[i], snd[i], rcv[i]) for i, n in enumerate(names)})
        return token

    first = local(BIG[0], w[BIG[0]]).astype(BF)
    fb = first.shape[0]
    token_first = start_gather("gather_start_first", BIG[:1], [first])
    token_rest = start_gather("gather_start_rest", BIG[1:],
                              [(local(n, w[n]) + token_first[0:1, 0:1]).astype(BF) for n in BIG[1:]])

    tiny_names = [n for n, _ in TINY_ROWS if n != "loss"]

    def weights(group, after):
        names = GATHER_GROUPS[group]
        hs = [handles[n] for n in names]
        got = list(_copy_wait("gather_wait_" + group, [h[0] for h in hs], [h[1] for h in hs], [h[2] for h in hs],
                              [h[3] for h in hs], token_rest if group == "ffn1_up" else after, True,
                              peers=[g_peers[BIG.index(n)] for n in names]))
        passed = [i for i, n in enumerate(names) if n in TWO_LEVEL]
        if passed:
            f_snd, f_rcv, f_land, f_token = _forward_start("gather_pass_start_" + group, [got[i] for i in passed])
            for i, t in zip(passed, _forward_wait("gather_pass_wait_" + group, f_land, f_snd, f_rcv, f_token)):
                got[i] = t
        got = dict(zip(names, got))
        if group == "ffn1_up":
            return {"wup1": got["w_ffn1_in"].reshape(2, N_FFN_BLK, fb, D)}
        if group == "ffn1_dn":
            return {"wdn1": got["w_ffn1_out"].reshape(N_FFN_BLK, fb, D)}
        if group == "mix":
            full = got["w_in"].reshape(-1, D)
            wz = jnp.concatenate([full[:QKV_W], full[UV_OFF:], full[QKV_W:UV_OFF],
                                  jnp.zeros((LANES - FOX_HEADS, D), BF)], axis=0)
            return {"wz": wz, "wout": got["w_out"].reshape(D, D)}
        if group == "ca":
            return {"wcq": got["w_cq"].reshape(D, D), "wco": got["w_co"].reshape(D, D), "wckv": got["w_ckv"]}
        return {"wup2": got["w_ffn2_in"].reshape(2, N_FFN_BLK, fb, D),
                "wdn2": got["w_ffn2_out"].reshape(N_FFN_BLK, fb, D)}

    flying = {}

    def emit(group, g):
        if group == "w_s":
            part = [g["w_s"].reshape(-1, LANES)]
            *copies, token = _copy_start("w_s_start", part, _place_own(part, True), True)
            flying[group] = copies
            return token
        if group == "ffn2":
            parts = {"w_ffn2_in": g["wup2"], "w_ffn2_out": g["wdn2"].reshape(N_DEV, -1, D)}
        elif group == "ffn1_dn":
            parts = {"w_ffn1_out": g["wdn1"].reshape(N_DEV, -1, D)}
        elif group == "ffn1_up":
            parts = {"w_ffn1_in": g["wup1"]}
        else:
            gz = g["wz"]
            g_in = jnp.concatenate([gz[:QKV_W], gz[Z_F:Z_F + FOX_HEADS], gz[QKV_W:Z_F]], axis=0)
            parts = {"w_in": g_in.reshape(N_DEV, -1, D).astype(BF),
                     "w_out": g["wout"].reshape(N_DEV, -1, D), "w_cq": g["wcq"].reshape(N_DEV, -1, D),
                     "w_co": g["wco"].reshape(N_DEV, -1, D), "w_ckv": g["wckv"]}
        names = list(parts)
        srcs = [parts[n] for n in names]
        *copies, token = _copy_start("exchange_start_" + group, srcs, [lax.empty(s.shape, s.dtype) for s in srcs],
                                     False)
        flying[group] = (names, copies)
        return token

    small = {n: (w[n][0] if n == "b_s" else w[n]) for n in tiny_names}
    small["w_s"] = w["w_s"][0]

    sq, dx0, gs = _local_step(x[0], mem[0], loss_target[0], small, weights, emit)

    sm_parts = [_pack_tiny(gs, sq)]
    sm_snd, sm_rcv, sm_src, sm_land, sm_token = _copy_start("tiny_start", sm_parts, _place_own(sm_parts, True), True)

    grad, delta, new_m, new_v = {}, {}, {}, {}

    def update(group, after):
        names, (snd, rcv, srcs, lands) = flying[group]
        owns, slots = _copy_wait("exchange_wait_" + group, srcs, lands, snd, rcv, after, False, with_srcs=True)
        for n, sl, own in zip(names, slots, owns):
            g, d, m2, v2 = _adamw_big("adamw_" + n, sl, local(n, w[n]), local(n, mo[n]), local(n, vo[n]), own=own)
            grad[n], delta[n], new_m[n], new_v[n] = (
                (t.T if n in TRANSPOSED else t).reshape(w[n].shape) for t in (g, d, m2, v2))
        return d

    last = update("ffn2", sm_token)
    last = update("mid", last)
    last = update("ffn1_dn", last)
    last = update("ffn1_up", last)
    ws_snd, ws_rcv, ws_src, ws_land = flying["w_s"]
    ws_all, = _copy_wait("w_s_wait", ws_src, ws_land, ws_snd, ws_rcv, last, True)
    tiny_all, = _copy_wait("tiny_wait", sm_src, sm_land, sm_snd, sm_rcv, ws_all, True)
    ws_shape = w["w_s"].shape
    for store, t in zip((grad, delta, new_m, new_v), _adamw_big(
            "adamw_w_s", ws_all, *[a["w_s"].reshape(-1, LANES) for a in (w, mo, vo)])):
        store["w_s"] = t.reshape(ws_shape)
    stores, loss_row = _adamw_tiny(tiny_all, *[{n: (a[n][0] if n == "b_s" else a[n]) for n in tiny_names}
                                               for a in (w, mo, vo)])
    for store, t in zip((grad, delta, new_m, new_v), stores):
        store.update({n: v.reshape(w[n].shape) for n, v in t.items()})
    loss = loss_row[0, 0] * (0.5 / D)

    return (loss, dx0[None], *[grad[n] for n in WEIGHTS], *[delta[n] for n in WEIGHTS],
            *[new_m[n] for n in WEIGHTS], *[new_v[n] for n in WEIGHTS])
```

```python
import functools

import jax
import jax.numpy as jnp
from jax import lax
from jax.experimental import pallas as pl
from jax.experimental.pallas import tpu as pltpu

F32 = jnp.float32
BF = jnp.bfloat16
S = jax.ShapeDtypeStruct

N_DEV = 8
D_MODEL = 1024
FOX_HEADS, FOX_HD = 8, 64
FOX_W = 512
GMLP_G, GMLP_GD = 8, 64
GMLP_W = 512
CHUNK = 128
CA_HEADS, CA_HD = 4, 256
N_FFN_BLK = 4
ZW = 2688
Z_Q, Z_K, Z_V, Z_U, Z_G, Z_F = 0, 512, 1024, 1536, 2048, 2560
EPS = 1e-6
NEG = -1e30
LANES = 128

ADAM_LR, ADAM_B1, ADAM_B2, ADAM_EPS, ADAM_WD, ADAM_STEP = 0.001, 0.9, 0.999, 1e-08, 0.01, 10

VMEM_LIMIT = 52 * 2 ** 20


def _cp(n_axes):
    return pltpu.CompilerParams(dimension_semantics=("arbitrary",) * n_axes, vmem_limit_bytes=VMEM_LIMIT)


def _nn(a, b):
    return jnp.dot(a, b, preferred_element_type=F32)


def _nt(a, b):
    return lax.dot_general(a, b, (((1,), (1,)), ((), ())), preferred_element_type=F32)


def _tn(a, b):
    return lax.dot_general(a, b, (((0,), (0,)), ((), ())), preferred_element_type=F32)


def _hi(mask, x):
    return jnp.dot(mask.astype(F32), x, precision=lax.Precision.HIGHEST, preferred_element_type=F32)


def _hi3(mask, x):
    mb = mask.astype(BF)
    hi = x.astype(BF)
    r1 = x - hi.astype(F32)
    mid = r1.astype(BF)
    lo = (r1 - mid.astype(F32)).astype(BF)
    return _nn(mb, hi) + _nn(mb, mid) + _nn(mb, lo)


def _rstd(x):
    return lax.rsqrt(jnp.mean(x * x, axis=-1, keepdims=True) + EPS)


def _norm_bwd(dy, x, g, r=None):
    r = _rstd(x) if r is None else r
    xh = x * r
    dxh = dy * g
    dx = r * (dxh - xh * jnp.mean(dxh * xh, axis=-1, keepdims=True))
    return dx, dy * xh


def _acc_rows(ref, first, val):
    srow = jnp.sum(val, axis=0, keepdims=True)

    @pl.when(first)
    def _():
        ref[...] = srow

    @pl.when(jnp.logical_not(first))
    def _():
        ref[...] += srow


def _gelu(x):
    c = 0.7978845608028654
    return 0.5 * x * (1.0 + jnp.tanh(c * (x + 0.044715 * x * x * x)))


def _gelu_grad(x):
    c = 0.7978845608028654
    t = jnp.tanh(c * (x + 0.044715 * x * x * x))
    return 0.5 * (1.0 + t) + 0.5 * x * (1.0 - t * t) * c * (1.0 + 3 * 0.044715 * x * x)


def _tile(n, pref):
    return pref if n % pref == 0 else n


def _ffn_up(name, x, g, wup):
    T, D = x.shape
    FB = wup.shape[-2]
    tm = _tile(T, 1024)

    def body(x_ref, g_ref, w_ref, a_ref, h_ref):
        @pl.when(pl.program_id(1) == 0)
        def _():
            xf = x_ref[...]
            h_ref[...] = (xf * _rstd(xf) * g_ref[...]).astype(BF)

        hb = h_ref[...]
        gg = _nt(hb, w_ref[0])
        uu = _nt(hb, w_ref[1])
        a_ref[...] = (gg * jax.nn.sigmoid(gg) * uu).astype(BF)

    return pl.pallas_call(
        body, name=name, grid=(T // tm, N_FFN_BLK),
        in_specs=[pl.BlockSpec((tm, D), lambda i, j: (i, 0)),
                  pl.BlockSpec((1, D), lambda i, j: (0, 0)),
                  pl.BlockSpec((2, None, FB, D), lambda i, j: (0, j, 0, 0))],
        out_specs=[pl.BlockSpec((None, tm, FB), lambda i, j: (j, i, 0)),
                   pl.BlockSpec((tm, D), lambda i, j: (i, 0))],
        out_shape=[S((N_FFN_BLK, T, FB), BF), S((T, D), BF)],
        compiler_params=_cp(2))(x, g, wup)


def _ffn_down(name, a, wdn, x):
    _, T, FB = a.shape
    D = x.shape[1]
    tm = _tile(T, 512)

    def body(a_ref, w_ref, x_ref, o_ref):
        p = _nn(a_ref[0], w_ref[0])
        for j in range(1, N_FFN_BLK):
            p = p + _nn(a_ref[j], w_ref[j])
        o_ref[...] = x_ref[...] + 0.5 * p

    return pl.pallas_call(
        body, name=name, grid=(T // tm,),
        in_specs=[pl.BlockSpec((N_FFN_BLK, tm, FB), lambda i: (0, i, 0)),
                  pl.BlockSpec((N_FFN_BLK, FB, D), lambda i: (0, 0, 0)),
                  pl.BlockSpec((tm, D), lambda i: (i, 0))],
        out_specs=pl.BlockSpec((tm, D), lambda i: (i, 0)),
        out_shape=S((T, D), F32),
        compiler_params=_cp(1))(a, wdn, x)


def _ffn_down_loss(name, a, wdn, x, target):
    _, T, FB = a.shape
    D = x.shape[1]
    tm = _tile(T, 512)

    def body(a_ref, w_ref, x_ref, t_ref, d_ref, db_ref, loss_ref):
        i = pl.program_id(0)
        p = _nn(a_ref[0], w_ref[0])
        for j in range(1, N_FFN_BLK):
            p = p + _nn(a_ref[j], w_ref[j])
        diff = (x_ref[...] + 0.5 * p) - t_ref[...]
        dy = diff * (1.0 / D)
        d_ref[...] = dy
        db_ref[...] = dy.astype(BF)
        sq = jnp.zeros((8, LANES), F32) + jnp.sum(diff * diff)

        @pl.when(i == 0)
        def _():
            loss_ref[...] = sq

        @pl.when(i > 0)
        def _():
            loss_ref[...] += sq

    row = pl.BlockSpec((tm, D), lambda i: (i, 0))
    return pl.pallas_call(
        body, name=name, grid=(T // tm,),
        in_specs=[pl.BlockSpec((N_FFN_BLK, tm, FB), lambda i: (0, i, 0)),
                  pl.BlockSpec((N_FFN_BLK, FB, D), lambda i: (0, 0, 0)), row, row],
        out_specs=[row, row, pl.BlockSpec((8, LANES), lambda i: (0, 0))],
        out_shape=[S((T, D), F32), S((T, D), BF), S((8, LANES), F32)],
        compiler_params=_cp(1))(a, wdn, x, target)


def _ffn_bwd_act(name, dyb, h, wup, wdn):
    T, D = h.shape
    FB = wup.shape[-2]
    tm = _tile(T, 1024)

    def body(d_ref, h_ref, wu_ref, wd_ref, o_ref):
        da = 0.5 * _nt(d_ref[...], wd_ref[...])
        hb = h_ref[...]
        gg = _nt(hb, wu_ref[0])
        uu = _nt(hb, wu_ref[1])
        sg = jax.nn.sigmoid(gg)
        o_ref[0] = (da * uu * (sg * (1.0 + gg * (1.0 - sg)))).astype(BF)
        o_ref[1] = (da * (gg * sg)).astype(BF)

    return pl.pallas_call(
        body, name=name, grid=(T // tm, N_FFN_BLK),
        in_specs=[pl.BlockSpec((tm, D), lambda i, j: (i, 0)),
                  pl.BlockSpec((tm, D), lambda i, j: (i, 0)),
                  pl.BlockSpec((2, None, FB, D), lambda i, j: (0, j, 0, 0)),
                  pl.BlockSpec((None, FB, D), lambda i, j: (j, 0, 0))],
        out_specs=pl.BlockSpec((2, None, tm, FB), lambda i, j: (0, j, i, 0)),
        out_shape=S((2, N_FFN_BLK, T, FB), BF),
        compiler_params=_cp(2))(dyb, h, wup, wdn)


def _ffn_dx(name, dgu, wup, x, g, dy):
    T, D = x.shape
    FB = wup.shape[-2]
    tm = _tile(T, 512)

    def body(d_ref, w_ref, x_ref, g_ref, dy_ref, dx_ref, dg_ref):
        p = None
        for j in range(N_FFN_BLK):
            for half in range(2):
                t = _nn(d_ref[half, j], w_ref[half, j])
                p = t if p is None else p + t
        dx, dgr = _norm_bwd(p, x_ref[...], g_ref[...])
        dx_ref[...] = dx + dy_ref[...]
        _acc_rows(dg_ref, pl.program_id(0) == 0, dgr)

    return pl.pallas_call(
        body, name=name, grid=(T // tm,),
        in_specs=[pl.BlockSpec((2, N_FFN_BLK, tm, FB), lambda i: (0, 0, i, 0)),
                  pl.BlockSpec((2, N_FFN_BLK, FB, D), lambda i: (0, 0, 0, 0), pipeline_mode=pl.Buffered(1)),
                  pl.BlockSpec((tm, D), lambda i: (i, 0)),
                  pl.BlockSpec((1, D), lambda i: (0, 0)),
                  pl.BlockSpec((tm, D), lambda i: (i, 0))],
        out_specs=[pl.BlockSpec((tm, D), lambda i: (i, 0)),
                   pl.BlockSpec((1, D), lambda i: (0, 0))],
        out_shape=[S((T, D), F32), S((1, D), F32)],
        compiler_params=_cp(1))(dgu, wup, x, g, dy)


def _tn_matmul(name, a, a_spec, b, out_shape, out_spec, n_blocks, scale=1.0, after=None):
    extra = [] if after is None else [after]

    def body(a_ref, b_ref, *rest):
        o_ref = rest[-1]
        o_ref[...] = (_tn(a_ref[...], b_ref[...]) * scale).astype(o_ref.dtype)

    return pl.pallas_call(
        body, name=name, grid=(n_blocks,),
        in_specs=[a_spec, pl.BlockSpec(b.shape, lambda j: (0, 0), pipeline_mode=pl.Buffered(1))]
        + [pl.BlockSpec((8, LANES), lambda j: (0, 0)) for _ in extra],
        out_specs=out_spec, out_shape=out_shape, compiler_params=_cp(1))(a, b, *extra)


def _ffn_dwup(name, h, dgu, after=None):
    T, D = h.shape
    FB = dgu.shape[-1]
    return _tn_matmul(
        name + "_dwup", dgu.reshape(2 * N_FFN_BLK, T, FB), pl.BlockSpec((None, T, FB), lambda j: (j, 0, 0)), h,
        S((2 * N_FFN_BLK, FB, D), BF), pl.BlockSpec((None, FB, D), lambda j: (j, 0, 0)), 2 * N_FFN_BLK,
        after=after)


def _ffn_dwdn(name, a, dyb):
    _, T, FB = a.shape
    D = dyb.shape[1]
    return _tn_matmul(
        name + "_dwdn", a, pl.BlockSpec((None, T, FB), lambda j: (j, 0, 0)), dyb,
        S((N_FFN_BLK, FB, D), BF), pl.BlockSpec((None, FB, D), lambda j: (j, 0, 0)), N_FFN_BLK, scale=0.5)


def _tri(n, lower):
    r = lax.broadcasted_iota(jnp.int32, (n, n), 0)
    c = lax.broadcasted_iota(jnp.int32, (n, n), 1)
    return (r >= c) if lower else (r <= c)


def _spatial_mix(vgn_b, ws_ref, bst, tm):
    tril = _tri(CHUNK, True)
    wms = [jnp.where(tril, ws_ref[g], 0.0).astype(BF) for g in range(GMLP_G)]
    rows = []
    for c in range(tm // CHUNK):
        cols = []
        for g in range(GMLP_G):
            vs = vgn_b[c * CHUNK:(c + 1) * CHUNK, g * GMLP_GD:(g + 1) * GMLP_GD]
            cols.append(_nn(wms[g], vs) + bst[:, g:g + 1])
        rows.append(jnp.concatenate(cols, axis=1))
    return jnp.concatenate(rows, axis=0), wms


HB = 128
AUG_W = FOX_HEADS * HB
COL_A, COL_B, COL_C = 64, 67, 70
RS_Q, RS_K, RS_V, RS_O = 0, 8, 16, 17


def _piece_matrix(col):
    r = jnp.arange(LANES)
    dst = jnp.where(r < 3 * FOX_HEADS, (r % FOX_HEADS) * HB + col + r // FOX_HEADS, -1)
    return (jnp.arange(AUG_W)[None, :] == dst[:, None]).astype(BF)


def _ones_row(cols):
    c = jnp.arange(AUG_W) % HB
    hit = functools.reduce(jnp.logical_or, [(c >= a) & (c < a + 3) for a in cols])
    return hit.astype(F32)[None, :]


def _pieces(x):
    lane = lax.broadcasted_iota(jnp.int32, x.shape, 1)
    x = jnp.where(lane < FOX_HEADS, x, 0.0)
    hi = x.astype(BF).astype(F32)
    r1 = x - hi
    mid = r1.astype(BF).astype(F32)
    lo = (r1 - mid).astype(BF).astype(F32)
    return (hi + pltpu.roll(mid, FOX_HEADS, 1) + pltpu.roll(lo, 2 * FOX_HEADS, 1)).astype(BF)


def _mix_prep(x, g_mix, wz, bf128, g_q, g_k, g_sgu, w_s, b_st, g_go):
    T, D = x.shape
    tm = _tile(T, 512)
    pc_q, pc_k = _piece_matrix(COL_A), _piece_matrix(COL_B)
    one_q, one_k, one_v = _ones_row([COL_B]), _ones_row([COL_A, COL_C]), _ones_row([COL_A])

    def body(x_ref, gm_ref, wz_ref, bf_ref, gq_ref, gk_ref, gs_ref, ws_ref, bst_ref, go_ref, pq_ref, pk_ref, oq_ref,
             ok_ref, ov_ref, z_ref, h_ref, q_ref, k_ref, v_ref, y_ref, rs_ref, carry_ref):
        i = pl.program_id(0)

        @pl.when(i == 0)
        def _():
            carry_ref[...] = jnp.zeros_like(carry_ref)

        xf = x_ref[...]
        hb = (xf * _rstd(xf) * gm_ref[...]).astype(BF)
        h_ref[...] = hb
        z_ref[...] = _nt(hb, wz_ref[...])

        fl = z_ref[:, Z_F:Z_F + LANES] + bf_ref[...]
        logf = jnp.minimum(fl, 0.0) - jnp.log1p(jnp.exp(-jnp.abs(fl)))
        csum = _hi(_tri(tm, True), logf) + carry_ref[...]
        carry_ref[...] = csum[tm - 1:tm, :]
        ext_q = (_nn(_pieces(csum), pq_ref[...]) + oq_ref[...]).astype(BF)
        ext_k = (_nn(_pieces(-csum), pk_ref[...]) + ok_ref[...]).astype(BF)
        ext_v = jnp.broadcast_to(ov_ref[...], (tm, AUG_W)).astype(BF)

        rs_ref[...] = jnp.zeros_like(rs_ref)
        for h in range(FOX_HEADS):
            lo, hi = slice(h * HB, h * HB + FOX_HD), slice(h * HB + FOX_HD, (h + 1) * HB)
            qh = z_ref[:, Z_Q + h * FOX_HD:Z_Q + (h + 1) * FOX_HD]
            kh = z_ref[:, Z_K + h * FOX_HD:Z_K + (h + 1) * FOX_HD]
            rq, rk = _rstd(qh), _rstd(kh)
            rs_ref[:, RS_Q + h:RS_Q + h + 1] = rq
            rs_ref[:, RS_K + h:RS_K + h + 1] = rk
            q_ref[:, lo] = (qh * rq * gq_ref[...] * 0.125).astype(BF)
            k_ref[:, lo] = (kh * rk * gk_ref[...]).astype(BF)
            v_ref[:, lo] = z_ref[:, Z_V + h * FOX_HD:Z_V + (h + 1) * FOX_HD].astype(BF)
            q_ref[:, hi] = ext_q[:, hi]
            k_ref[:, hi] = ext_k[:, hi]
            v_ref[:, hi] = ext_v[:, hi]

        u = _gelu(z_ref[:, Z_U:Z_U + GMLP_W])
        vg = _gelu(z_ref[:, Z_G:Z_G + GMLP_W])
        rv = _rstd(vg)
        vgn = (vg * rv * gs_ref[...]).astype(BF)
        mixed, _ = _spatial_mix(vgn, ws_ref, bst_ref[...], tm)
        sgu = u * mixed
        ro = _rstd(sgu)
        y_ref[...] = (sgu * ro * go_ref[...]).astype(BF)
        rs_ref[:, RS_V:RS_V + 1] = rv
        rs_ref[:, RS_O:RS_O + 1] = ro

    row = lambda i: (i, 0)
    fix2 = lambda i: (0, 0)
    return pl.pallas_call(
        body, name="mix_prep", grid=(T // tm,),
        in_specs=[pl.BlockSpec((tm, D), row), pl.BlockSpec((1, D), fix2),
                  pl.BlockSpec((ZW, D), fix2, pipeline_mode=pl.Buffered(1)),
                  pl.BlockSpec((1, LANES), fix2), pl.BlockSpec((1, FOX_HD), fix2), pl.BlockSpec((1, FOX_HD), fix2),
                  pl.BlockSpec((1, GMLP_W), fix2), pl.BlockSpec((GMLP_G, CHUNK, CHUNK), lambda i: (0, 0, 0)),
                  pl.BlockSpec((CHUNK, GMLP_G), fix2), pl.BlockSpec((1, GMLP_W), fix2),
                  pl.BlockSpec((LANES, AUG_W), fix2),
                  pl.BlockSpec((LANES, AUG_W), fix2), pl.BlockSpec((1, AUG_W), fix2), pl.BlockSpec((1, AUG_W), fix2),
                  pl.BlockSpec((1, AUG_W), fix2)],
        out_specs=[pl.BlockSpec((tm, ZW), row), pl.BlockSpec((tm, D), row),
                   pl.BlockSpec((tm, AUG_W), row), pl.BlockSpec((tm, AUG_W), row), pl.BlockSpec((tm, AUG_W), row),
                   pl.BlockSpec((tm, GMLP_W), row), pl.BlockSpec((tm, LANES), row)],
        out_shape=[S((T, ZW), F32), S((T, D), BF), S((T, AUG_W), BF), S((T, AUG_W), BF), S((T, AUG_W), BF),
                   S((T, GMLP_W), BF), S((T, LANES), F32)],
        scratch_shapes=[pltpu.VMEM((1, LANES), F32)],
        compiler_params=_cp(1))(x, g_mix, wz, bf128, g_q, g_k, g_sgu, w_s, b_st, g_go, pc_q, pc_k, one_q, one_k,
                                one_v)


def _fox_fwd(q, k, v):
    T = q.shape[0]
    tq = _tile(T, 1024)
    nq = T // tq

    def body(q_ref, k_ref, v_ref, o_ref, lse_ref, m_sc, acc_sc):
        i, j = pl.program_id(0), pl.program_id(1)

        @pl.when(j == 0)
        def _():
            m_sc[...] = jnp.full(m_sc.shape, NEG, F32)
            acc_sc[...] = jnp.zeros_like(acc_sc)

        def step(masked):
            mask = _tri(tq, True) if masked else None
            for h in range(FOX_HEADS):
                hb = slice(h * HB, (h + 1) * HB)
                s = _nt(q_ref[:, hb], k_ref[:, hb])
                if masked:
                    s = jnp.where(mask, s, NEG)
                m_prev = m_sc[h]
                m_new = jnp.maximum(m_prev, jnp.broadcast_to(jnp.max(s, axis=1, keepdims=True), (tq, HB)))
                p = jnp.exp(s - jnp.tile(m_new, (1, tq // HB))).astype(BF)
                acc_sc[:, hb] = jnp.exp(m_prev - m_new) * acc_sc[:, hb] + _nn(p, v_ref[:, hb])
                m_sc[h] = m_new

        @pl.when(j < i)
        def _():
            step(False)

        @pl.when(j == i)
        def _():
            step(True)
            lse_ref[...] = jnp.zeros_like(lse_ref)
            for h in range(FOX_HEADS):
                l = acc_sc[:, h * HB + COL_A:h * HB + COL_A + 1]
                o_ref[:, h * FOX_HD:(h + 1) * FOX_HD] = acc_sc[:, h * HB:h * HB + FOX_HD] / l
                lse_ref[:, h:h + 1] = m_sc[h][:, 0:1] + jnp.log(l)

    qi = lambda i, j: (i, 0)
    kj = lambda i, j: (jnp.minimum(i, j), 0)
    return pl.pallas_call(
        body, name="fox_fwd", grid=(nq, nq),
        in_specs=[pl.BlockSpec((tq, AUG_W), qi), pl.BlockSpec((tq, AUG_W), kj), pl.BlockSpec((tq, AUG_W), kj)],
        out_specs=[pl.BlockSpec((tq, FOX_W), qi), pl.BlockSpec((tq, LANES), qi)],
        out_shape=[S((T, FOX_W), F32), S((T, LANES), F32)],
        scratch_shapes=[pltpu.VMEM((FOX_HEADS, tq, HB), F32), pltpu.VMEM((tq, AUG_W), F32)],
        compiler_params=_cp(2))(q, k, v)


def _fox_bwd(q, k, v, dob):
    T = q.shape[0]
    tq = _tile(T, 512)
    nq = T // tq
    n_sweeps = 1
    half = AUG_W // n_sweeps
    hpg = FOX_HEADS // n_sweeps

    pairs = [(j, i) for j in range(nq) for i in range(j, nq)]
    jt = jnp.asarray([p[0] for p in pairs], jnp.int32)
    it = jnp.asarray([p[1] for p in pairs], jnp.int32)

    def body(jt_ref, it_ref, q_ref, k_ref, v_ref, do_ref, dq_ref, dk_ref, dv_ref, dq_sc):
        t = pl.program_id(1)
        j, i = jt_ref[t], it_ref[t]

        @pl.when(t == 0)
        def _():
            dq_sc[...] = jnp.zeros_like(dq_sc)

        @pl.when(i == j)
        def _():
            dk_ref[...] = jnp.zeros_like(dk_ref)
            dv_ref[...] = jnp.zeros_like(dv_ref)

        def step(masked):
            rows = pl.ds(pl.multiple_of(i * tq, tq), tq)
            mask = _tri(tq, True) if masked else None
            for h in range(hpg):
                hb = slice(h * HB, (h + 1) * HB)
                qh, kh, vh, doh = q_ref[:, hb], k_ref[:, hb], v_ref[:, hb], do_ref[:, hb]
                s = _nt(qh, kh)
                if masked:
                    s = jnp.where(mask, s, NEG)
                p = jnp.exp(s)
                dsb = (p * _nt(doh, vh)).astype(BF)
                dv_ref[:, hb] += _tn(p.astype(BF), doh)
                dk_ref[:, hb] += _tn(dsb, qh)
                dq_sc[rows, hb] += _nn(dsb, kh)

        @pl.when(i > j)
        def _():
            step(False)

        @pl.when(i == j)
        def _():
            step(True)
            dq_ref[...] = dq_sc[pl.ds(pl.multiple_of(j * tq, tq), tq), :]

    qi = pl.BlockSpec((tq, half), lambda g, t, jt_ref, it_ref: (it_ref[t], g))
    kj = pl.BlockSpec((tq, half), lambda g, t, jt_ref, it_ref: (jt_ref[t], g))
    return pl.pallas_call(
        body, name="fox_bwd",
        grid_spec=pltpu.PrefetchScalarGridSpec(
            num_scalar_prefetch=2, grid=(n_sweeps, len(pairs)), in_specs=[qi, kj, kj, qi], out_specs=[kj, kj, kj],
            scratch_shapes=[pltpu.VMEM((T, half), F32)]),
        out_shape=[S((T, AUG_W), F32), S((T, AUG_W), F32), S((T, AUG_W), F32)],
        compiler_params=_cp(2))(jt, it, q, k, v, dob)


def _mix_out(attn, yg, g_fo, wout, x):
    T, D = x.shape
    tm = _tile(T, 1024)

    def body(a_ref, y_ref, g_ref, w_ref, x_ref, o_ref):
        at = a_ref[...]
        yf = (at * _rstd(at) * g_ref[...]).astype(BF)
        o_ref[...] = x_ref[...] + _nn(yf, w_ref[:FOX_W, :]) + _nn(y_ref[...], w_ref[FOX_W:, :])

    row = lambda i: (i, 0)
    return pl.pallas_call(
        body, name="mix_out", grid=(T // tm,),
        in_specs=[pl.BlockSpec((tm, FOX_W), row), pl.BlockSpec((tm, GMLP_W), row),
                  pl.BlockSpec((1, FOX_W), lambda i: (0, 0)), pl.BlockSpec((D, D), lambda i: (0, 0)),
                  pl.BlockSpec((tm, D), row)],
        out_specs=pl.BlockSpec((tm, D), row),
        out_shape=S((T, D), F32),
        compiler_params=_cp(1))(attn, yg, g_fo, wout, x)


def _mix_out_bwd(dx, attn, yg, g_fo, wout, qf, lse):
    T, D = dx.shape
    tm = _tile(T, 512)
    n = T // tm
    pc_l, pc_d = _piece_matrix(COL_C), _piece_matrix(COL_A)

    def body(dx_ref, a_ref, y_ref, g_ref, w_ref, qf_ref, lse_ref, pl_ref, pd_ref,
             qb_ref, dob_ref, dyg_ref, dw_ref, dg_ref, acc_ref, dsum_ref):
        i = pl.program_id(0)
        dxb = dx_ref[...].astype(BF)
        at = a_ref[...]
        yf = (at * _rstd(at) * g_ref[...]).astype(BF)
        dy = _nt(dxb, w_ref[...])
        p_top = _tn(yf, dxb)
        p_bot = _tn(y_ref[...], dxb)

        @pl.when(i == 0)
        def _():
            acc_ref[:FOX_W, :] = p_top
            acc_ref[FOX_W:, :] = p_bot

        @pl.when(i > 0)
        def _():
            acc_ref[:FOX_W, :] += p_top
            acc_ref[FOX_W:, :] += p_bot

        @pl.when(i == n - 1)
        def _():
            dw_ref[...] = acc_ref[...].astype(BF)

        dat, dgr = _norm_bwd(dy[:, :FOX_W], at, g_ref[...])
        _acc_rows(dg_ref, i == 0, dgr)
        dyg_ref[...] = dy[:, FOX_W:]
        prod = dat * at
        dsum_ref[...] = jnp.zeros_like(dsum_ref)
        for h in range(FOX_HEADS):
            dsum_ref[:, h:h + 1] = jnp.sum(prod[:, h * FOX_HD:(h + 1) * FOX_HD], axis=1, keepdims=True)
        ext_d = _nn(_pieces(-dsum_ref[...]), pd_ref[...]).astype(BF)
        ext_l = _nn(_pieces(-lse_ref[...]), pl_ref[...])
        datb = dat.astype(BF)
        for h in range(FOX_HEADS):
            lo, hi = slice(h * HB, h * HB + FOX_HD), slice(h * HB + FOX_HD, (h + 1) * HB)
            dob_ref[:, lo] = datb[:, h * FOX_HD:(h + 1) * FOX_HD]
            dob_ref[:, hi] = ext_d[:, hi]
            qb_ref[:, lo] = qf_ref[:, lo]
            qb_ref[:, hi] = (qf_ref[:, hi].astype(F32) + ext_l[:, hi]).astype(BF)

    row = lambda i: (i, 0)
    fix = lambda i: (0, 0)
    return pl.pallas_call(
        body, name="mix_out_bwd", grid=(n,),
        in_specs=[pl.BlockSpec((tm, D), row), pl.BlockSpec((tm, FOX_W), row), pl.BlockSpec((tm, GMLP_W), row),
                  pl.BlockSpec((1, FOX_W), fix), pl.BlockSpec((D, D), fix), pl.BlockSpec((tm, AUG_W), row),
                  pl.BlockSpec((tm, LANES), row), pl.BlockSpec((LANES, AUG_W), fix),
                  pl.BlockSpec((LANES, AUG_W), fix)],
        out_specs=[pl.BlockSpec((tm, AUG_W), row), pl.BlockSpec((tm, AUG_W), row), pl.BlockSpec((tm, GMLP_W), row),
                   pl.BlockSpec((D, D), fix), pl.BlockSpec((1, FOX_W), fix)],
        out_shape=[S((T, AUG_W), BF), S((T, AUG_W), BF), S((T, GMLP_W), F32), S((D, D), BF), S((1, FOX_W), F32)],
        scratch_shapes=[pltpu.VMEM((D, D), F32), pltpu.VMEM((tm, LANES), F32)],
        compiler_params=_cp(1))(dx, attn, yg, g_fo, wout, qf, lse, pc_l, pc_d)


def _mix_prep_bwd(z, dq, dk, dv, dyg, rs, bf128, g_q, g_k, g_sgu, w_s, b_st, g_go):
    T = z.shape[0]
    tm = _tile(T, 512)
    n = T // tm

    def body(z_ref, dq_ref, dk_ref, dv_ref, dyg_ref, rs_ref, bf_ref, gq_ref, gk_ref, gs_ref, ws_ref,
             bst_ref, go_ref, dz_ref, dgq_ref, dgk_ref, dgs_ref, dgo_ref, dws_ref, dbst_ref, dbf_ref, carry_ref):
        i = pl.program_id(0)
        first = i == 0
        rs = rs_ref[...]

        @pl.when(first)
        def _():
            carry_ref[...] = jnp.zeros_like(carry_ref)

        lane = lax.broadcasted_iota(jnp.int32, (tm, LANES), 1)
        dc = jnp.zeros((tm, LANES), F32)
        gq_rows, gk_rows = [], []
        for h in range(FOX_HEADS):
            hp = slice(h * HB, h * HB + FOX_HD)
            dqh, gqr = _norm_bwd(dq_ref[:, hp] * 0.125, z_ref[:, Z_Q + h * FOX_HD:Z_Q + (h + 1) * FOX_HD], gq_ref[...],
                                 rs[:, RS_Q + h:RS_Q + h + 1])
            dkh, gkr = _norm_bwd(dk_ref[:, hp], z_ref[:, Z_K + h * FOX_HD:Z_K + (h + 1) * FOX_HD], gk_ref[...],
                                 rs[:, RS_K + h:RS_K + h + 1])
            dz_ref[:, Z_Q + h * FOX_HD:Z_Q + (h + 1) * FOX_HD] = dqh.astype(BF)
            dz_ref[:, Z_K + h * FOX_HD:Z_K + (h + 1) * FOX_HD] = dkh.astype(BF)
            dz_ref[:, Z_V + h * FOX_HD:Z_V + (h + 1) * FOX_HD] = dv_ref[:, hp].astype(BF)
            dch = dq_ref[:, h * HB + COL_A:h * HB + COL_A + 1] - dk_ref[:, h * HB + COL_B:h * HB + COL_B + 1]
            dc = jnp.where(lane == h, dch, dc)
            gq_rows.append(gqr)
            gk_rows.append(gkr)
        _acc_rows(dgq_ref, first, functools.reduce(lambda a, b: a + b, gq_rows))
        _acc_rows(dgk_ref, first, functools.reduce(lambda a, b: a + b, gk_rows))

        dlogf = _hi3(_tri(tm, False), dc) + carry_ref[...]
        carry_ref[...] = dlogf[0:1, :]
        fl = z_ref[:, Z_F:Z_F + LANES] + bf_ref[...]
        lane = lax.broadcasted_iota(jnp.int32, (tm, LANES), 1)
        df = jnp.where(lane < FOX_HEADS, dlogf * jax.nn.sigmoid(-fl), 0.0)
        dz_ref[:, Z_F:Z_F + LANES] = df.astype(BF)
        _acc_rows(dbf_ref, first, df)

        u_pre = z_ref[:, Z_U:Z_U + GMLP_W]
        vg_pre = z_ref[:, Z_G:Z_G + GMLP_W]
        u = _gelu(u_pre)
        vg = _gelu(vg_pre)
        rv = rs[:, RS_V:RS_V + 1]
        vgn = (vg * rv * gs_ref[...]).astype(BF)
        bst = bst_ref[...]
        mixed, wms = _spatial_mix(vgn, ws_ref, bst, tm)
        sgu = u * mixed
        dsgu, gor = _norm_bwd(dyg_ref[...], sgu, go_ref[...], rs[:, RS_O:RS_O + 1])
        _acc_rows(dgo_ref, first, gor)
        du = dsgu * mixed
        dmixed = dsgu * u
        dmb = dmixed.astype(BF)
        tril = _tri(CHUNK, True)
        dvgn_rows = []
        dws = [None] * GMLP_G
        dbs = [None] * GMLP_G
        for c in range(tm // CHUNK):
            cs = slice(c * CHUNK, (c + 1) * CHUNK)
            cols = []
            for g in range(GMLP_G):
                gs = slice(g * GMLP_GD, (g + 1) * GMLP_GD)
                dmc = dmb[cs, gs]
                pw = _nt(dmc, vgn[cs, gs])
                pb = jnp.sum(dmixed[cs, gs], axis=1, keepdims=True)
                dws[g] = pw if dws[g] is None else dws[g] + pw
                dbs[g] = pb if dbs[g] is None else dbs[g] + pb
                cols.append(_tn(wms[g], dmc))
            dvgn_rows.append(jnp.concatenate(cols, axis=1))
        dvgn = jnp.concatenate(dvgn_rows, axis=0)
        dbs_t = jnp.concatenate(dbs, axis=1)
        for g in range(GMLP_G):
            dwg = jnp.where(tril, dws[g], 0.0)

            @pl.when(first)
            def _():
                dws_ref[g] = dwg

            @pl.when(jnp.logical_not(first))
            def _():
                dws_ref[g] += dwg

        @pl.when(first)
        def _():
            dbst_ref[...] = dbs_t

        @pl.when(jnp.logical_not(first))
        def _():
            dbst_ref[...] += dbs_t

        dvg, gsr = _norm_bwd(dvgn, vg, gs_ref[...], rv)
        _acc_rows(dgs_ref, first, gsr)
        dz_ref[:, Z_U:Z_U + GMLP_W] = (du * _gelu_grad(u_pre)).astype(BF)
        dz_ref[:, Z_G:Z_G + GMLP_W] = (dvg * _gelu_grad(vg_pre)).astype(BF)

    rev = lambda i: (n - 1 - i, 0)
    fix = lambda i: (0, 0)
    fix3 = lambda i: (0, 0, 0)
    return pl.pallas_call(
        body, name="mix_prep_bwd", grid=(n,),
        in_specs=[pl.BlockSpec((tm, ZW), rev), pl.BlockSpec((tm, AUG_W), rev), pl.BlockSpec((tm, AUG_W), rev),
                  pl.BlockSpec((tm, AUG_W), rev), pl.BlockSpec((tm, GMLP_W), rev), pl.BlockSpec((tm, LANES), rev),
                  pl.BlockSpec((1, LANES), fix), pl.BlockSpec((1, FOX_HD), fix), pl.BlockSpec((1, FOX_HD), fix),
                  pl.BlockSpec((1, GMLP_W), fix), pl.BlockSpec((GMLP_G, CHUNK, CHUNK), fix3),
                  pl.BlockSpec((CHUNK, GMLP_G), fix), pl.BlockSpec((1, GMLP_W), fix)],
        out_specs=[pl.BlockSpec((tm, ZW), rev), pl.BlockSpec((1, FOX_HD), fix), pl.BlockSpec((1, FOX_HD), fix),
                   pl.BlockSpec((1, GMLP_W), fix), pl.BlockSpec((1, GMLP_W), fix),
                   pl.BlockSpec((GMLP_G, CHUNK, CHUNK), fix3), pl.BlockSpec((CHUNK, GMLP_G), fix),
                   pl.BlockSpec((1, LANES), fix)],
        out_shape=[S((T, ZW), BF), S((1, FOX_HD), F32), S((1, FOX_HD), F32), S((1, GMLP_W), F32), S((1, GMLP_W), F32),
                   S((GMLP_G, CHUNK, CHUNK), F32), S((CHUNK, GMLP_G), F32), S((1, LANES), F32)],
        scratch_shapes=[pltpu.VMEM((1, LANES), F32)],
        compiler_params=_cp(1))(z, dq, dk, dv, dyg, rs, bf128, g_q, g_k, g_sgu, w_s, b_st, g_go)


def _mix_proj_bwd(dz, wz, x, g, dy):
    T, D = x.shape
    tm = _tile(T, 512)

    def body(dz_ref, w_ref, x_ref, g_ref, dy_ref, dx_ref, dxb_ref, dg_ref):
        dh = _nn(dz_ref[...], w_ref[...])
        dx, dgr = _norm_bwd(dh, x_ref[...], g_ref[...])
        dx = dx + dy_ref[...]
        dx_ref[...] = dx
        dxb_ref[...] = dx.astype(BF)
        _acc_rows(dg_ref, pl.program_id(0) == 0, dgr)

    row = lambda i: (i, 0)
    fix = lambda i: (0, 0)
    return pl.pallas_call(
        body, name="mix_proj_bwd", grid=(T // tm,),
        in_specs=[pl.BlockSpec((tm, ZW), row), pl.BlockSpec((ZW, D), fix), pl.BlockSpec((tm, D), row),
                  pl.BlockSpec((1, D), fix), pl.BlockSpec((tm, D), row)],
        out_specs=[pl.BlockSpec((tm, D), row), pl.BlockSpec((tm, D), row), pl.BlockSpec((1, D), fix)],
        out_shape=[S((T, D), F32), S((T, D), BF), S((1, D), F32)],
        compiler_params=_cp(1))(dz, wz, x, g, dy)


def _ca_kv(mem, g_mem, wckv, g_ck):
    M, D = mem.shape

    def body(m_ref, g_ref, w_ref, gk_ref, mn_ref, kr_ref, kn_ref, v_ref):
        mf = m_ref[...]
        mn = (mf * _rstd(mf) * g_ref[...]).astype(BF)
        mn_ref[...] = mn
        for h in range(CA_HEADS):
            kr = _nn(mn, w_ref[h])
            kr_ref[h] = kr
            kn_ref[h] = (kr * _rstd(kr) * gk_ref[...]).astype(BF)
            v_ref[h] = _nn(mn, w_ref[CA_HEADS + h]).astype(BF)

    hd = (CA_HEADS, M, CA_HD)
    return pl.pallas_call(
        body, name="ca_kv", out_shape=[S((M, D), BF), S(hd, F32), S(hd, BF), S(hd, BF)],
        compiler_params=pltpu.CompilerParams(vmem_limit_bytes=VMEM_LIMIT))(mem, g_mem, wckv, g_ck)


def _ca_tile_fwd(xt, gca, wcq, gcq, kn_ref, v_ref):
    hb = (xt * _rstd(xt) * gca).astype(BF)
    qc = _nn(hb, wcq)
    qr, qn, ps = [], [], []
    for h in range(CA_HEADS):
        qh = qc[:, h * CA_HD:(h + 1) * CA_HD]
        qnh = (qh * _rstd(qh) * gcq * 0.0625).astype(BF)
        s = _nt(qnh, kn_ref[h])
        e = jnp.exp(s - jnp.max(s, axis=1, keepdims=True))
        ps.append(e / jnp.sum(e, axis=1, keepdims=True))
        qr.append(qh)
        qn.append(qnh)
    return hb, qr, qn, ps


def _ca_fwd(x, g_ca, wcq, g_cq, kn, vv, wco):
    T, D = x.shape
    M = kn.shape[1]
    tm = _tile(T, 1024)

    def body(x_ref, gca_ref, wcq_ref, gcq_ref, kn_ref, v_ref, wco_ref, o_ref, ob_sc):
        xt = x_ref[...]
        _, _, _, ps = _ca_tile_fwd(xt, gca_ref[...], wcq_ref[...], gcq_ref[...], kn_ref, v_ref)
        for h in range(CA_HEADS):
            ob_sc[:, h * CA_HD:(h + 1) * CA_HD] = _nn(ps[h].astype(BF), v_ref[h]).astype(BF)
        o_ref[...] = xt + _nn(ob_sc[...], wco_ref[...])

    row = lambda i: (i, 0)
    fix = lambda i: (0, 0)
    fix3 = lambda i: (0, 0, 0)
    return pl.pallas_call(
        body, name="ca_fwd", grid=(T // tm,),
        in_specs=[pl.BlockSpec((tm, D), row), pl.BlockSpec((1, D), fix), pl.BlockSpec((D, D), fix),
                  pl.BlockSpec((1, CA_HD), fix), pl.BlockSpec((CA_HEADS, M, CA_HD), fix3),
                  pl.BlockSpec((CA_HEADS, M, CA_HD), fix3), pl.BlockSpec((D, D), fix)],
        out_specs=pl.BlockSpec((tm, D), row), out_shape=S((T, D), F32),
        scratch_shapes=[pltpu.VMEM((tm, D), BF)],
        compiler_params=_cp(1))(x, g_ca, wcq, g_cq, kn, vv, wco)


def _ca_bwd(x, dy, g_ca, wcq, g_cq, kn, vv, wco):
    T, D = x.shape
    M = kn.shape[1]
    tm = _tile(T, 512)
    n = T // tm

    def body(x_ref, dy_ref, gca_ref, wcq_ref, gcq_ref, kn_ref, v_ref, wco_ref,
             dx_ref, dwq_ref, dwo_ref, dkn_ref, dv_ref, dgcq_ref, dgca_ref, aq_sc, ao_sc, ob_sc, dq_sc):
        i = pl.program_id(0)
        first = i == 0
        xt = x_ref[...]
        dyt = dy_ref[...]
        dyb = dyt.astype(BF)
        hb, qr, qn, ps = _ca_tile_fwd(xt, gca_ref[...], wcq_ref[...], gcq_ref[...], kn_ref, v_ref)
        do = _nt(dyb, wco_ref[...])
        gcq_rows = None
        for h in range(CA_HEADS):
            hs = slice(h * CA_HD, (h + 1) * CA_HD)
            p = ps[h]
            pb = p.astype(BF)
            ob_sc[:, hs] = _nn(pb, v_ref[h]).astype(BF)
            doh = do[:, hs].astype(BF)
            dp = _nt(doh, v_ref[h])
            ds = (p * (dp - jnp.sum(dp * p, axis=1, keepdims=True))).astype(BF)
            dvh = _tn(pb, doh)
            dkh = _tn(ds, qn[h])

            @pl.when(first)
            def _():
                dv_ref[h] = dvh
                dkn_ref[h] = dkh

            @pl.when(jnp.logical_not(first))
            def _():
                dv_ref[h] += dvh
                dkn_ref[h] += dkh

            dqn = _nn(ds, kn_ref[h]) * 0.0625
            dqh, gr = _norm_bwd(dqn, qr[h], gcq_ref[...])
            gcq_rows = gr if gcq_rows is None else gcq_rows + gr
            dq_sc[:, hs] = dqh.astype(BF)
        _acc_rows(dgcq_ref, first, gcq_rows)
        dqb = dq_sc[...]
        p_o = _tn(ob_sc[...], dyb)
        p_q = _tn(hb, dqb)

        @pl.when(first)
        def _():
            ao_sc[...] = p_o
            aq_sc[...] = p_q

        @pl.when(jnp.logical_not(first))
        def _():
            ao_sc[...] += p_o
            aq_sc[...] += p_q

        @pl.when(i == n - 1)
        def _():
            dwo_ref[...] = ao_sc[...].astype(BF)
            dwq_ref[...] = aq_sc[...].astype(BF)

        dh = _nt(dqb, wcq_ref[...])
        dx, gar = _norm_bwd(dh, xt, gca_ref[...])
        dx_ref[...] = dx + dyt
        _acc_rows(dgca_ref, first, gar)

    row = lambda i: (i, 0)
    fix = lambda i: (0, 0)
    fix3 = lambda i: (0, 0, 0)
    hd = (CA_HEADS, M, CA_HD)
    return pl.pallas_call(
        body, name="ca_bwd", grid=(n,),
        in_specs=[pl.BlockSpec((tm, D), row), pl.BlockSpec((tm, D), row), pl.BlockSpec((1, D), fix),
                  pl.BlockSpec((D, D), fix), pl.BlockSpec((1, CA_HD), fix), pl.BlockSpec(hd, fix3),
                  pl.BlockSpec(hd, fix3), pl.BlockSpec((D, D), fix)],
        out_specs=[pl.BlockSpec((tm, D), row), pl.BlockSpec((D, D), fix), pl.BlockSpec((D, D), fix),
                   pl.BlockSpec(hd, fix3), pl.BlockSpec(hd, fix3), pl.BlockSpec((1, CA_HD), fix),
                   pl.BlockSpec((1, D), fix)],
        out_shape=[S((T, D), F32), S((D, D), BF), S((D, D), BF), S(hd, F32), S(hd, F32), S((1, CA_HD), F32),
                   S((1, D), F32)],
        scratch_shapes=[pltpu.VMEM((D, D), F32), pltpu.VMEM((D, D), F32), pltpu.VMEM((tm, D), BF),
                        pltpu.VMEM((tm, D), BF)],
        compiler_params=_cp(1))(x, dy, g_ca, wcq, g_cq, kn, vv, wco)


def _ca_kv_bwd(mem, g_mem, mn, kraw, dkn, dvv, wckv, g_ck):
    M, D = mem.shape

    def body(m_ref, g_ref, mn_ref, kr_ref, dkn_ref, dv_ref, w_ref, gk_ref, dw_ref, dgk_ref, dgm_ref):
        mn = mn_ref[...]
        dmn = jnp.zeros((M, D), F32)
        gk_rows = None
        for h in range(CA_HEADS):
            dkr, gr = _norm_bwd(dkn_ref[h], kr_ref[h], gk_ref[...])
            gk_rows = gr if gk_rows is None else gk_rows + gr
            dkb = dkr.astype(BF)
            dvb = dv_ref[h].astype(BF)
            dw_ref[h] = _tn(mn, dkb).astype(BF)
            dw_ref[CA_HEADS + h] = _tn(mn, dvb).astype(BF)
            dmn = dmn + _nt(dkb, w_ref[h]) + _nt(dvb, w_ref[CA_HEADS + h])
        dgk_ref[...] = jnp.sum(gk_rows, axis=0, keepdims=True)
        mf = m_ref[...]
        dgm_ref[...] = jnp.sum(dmn * (mf * _rstd(mf)), axis=0, keepdims=True)

    return pl.pallas_call(
        body, name="ca_kv_bwd",
        out_shape=[S((2 * CA_HEADS, D, CA_HD), BF), S((1, CA_HD), F32), S((1, D), F32)],
        compiler_params=pltpu.CompilerParams(vmem_limit_bytes=VMEM_LIMIT))(mem, g_mem, mn, kraw, dkn, dvv, wckv, g_ck)


def _after(g, token):
    return g if token is None else g + token[0:1, 0:1]


def _local_step(x, mem, target, small, weights, emit):
    T, D = x.shape
    p = small
    bf128 = jnp.pad(p["b_f"], ((0, 0), (0, LANES - FOX_HEADS)))
    b_st = p["b_s"].T

    wup1 = weights("ffn1_up", x)["wup1"]
    a1, h1 = _ffn_up("ffn1_up", x, p["g_ffn1"], wup1)
    wdn1 = weights("ffn1_dn", h1)["wdn1"]
    x1 = _ffn_down("ffn1_down", a1, wdn1, x)
    wm = weights("mix", x1)
    z, h2, qf, ka, va, yg, rs = _mix_prep(x1, p["g_mix"], wm["wz"], bf128, p["g_q"], p["g_k"], p["g_sgu"], p["w_s"],
                                          b_st, p["g_gmlp_o"])
    attn, lse = _fox_fwd(qf, ka, va)
    x2 = _mix_out(attn, yg, p["g_fox_o"], wm["wout"], x1)
    wc = weights("ca", x2)
    mn, kraw, ckn, cvv = _ca_kv(mem, p["g_mem"], wc["wckv"], p["g_ck"])
    x3 = _ca_fwd(x2, p["g_ca"], wc["wcq"], p["g_cq"], ckn, cvv, wc["wco"])
    w2 = weights("ffn2", x3)
    a2, h4 = _ffn_up("ffn2_up", x3, p["g_ffn2"], w2["wup2"])
    dy4, dy4b, sq = _ffn_down_loss("ffn2_down", a2, w2["wdn2"], x3, target)

    gs = {}
    dgu2 = _ffn_bwd_act("ffn2_bwd_act", dy4b, h4, w2["wup2"], w2["wdn2"])
    tok = emit("ffn2", {"wup2": _ffn_dwup("ffn2", h4, dgu2), "wdn2": _ffn_dwdn("ffn2", a2, dy4b)})
    dx3, gs["g_ffn2"] = _ffn_dx("ffn2_dx", dgu2, w2["wup2"], x3, _after(p["g_ffn2"], tok), dy4)

    dx2, dwcq, dwco, dckn, dcvv, gs["g_cq"], gs["g_ca"] = _ca_bwd(
        x2, dx3, p["g_ca"], wc["wcq"], p["g_cq"], ckn, cvv, wc["wco"])
    dwckv, gs["g_ck"], gs["g_mem"] = _ca_kv_bwd(mem, p["g_mem"], mn, kraw, dckn, dcvv, wc["wckv"], p["g_ck"])

    qb, dob, dyg, dwout, gs["g_fox_o"] = _mix_out_bwd(dx2, attn, yg, p["g_fox_o"], wm["wout"], qf, lse)
    dq, dk, dv = _fox_bwd(qb, ka, va, dob)
    dz, gs["g_q"], gs["g_k"], gs["g_sgu"], gs["g_gmlp_o"], gs["w_s"], dbst, dbf = _mix_prep_bwd(
        z, dq, dk, dv, dyg, rs, bf128, p["g_q"], p["g_k"], p["g_sgu"], p["w_s"], b_st, p["g_gmlp_o"])
    gs["b_s"] = dbst.T
    gs["b_f"] = dbf[:, :FOX_HEADS]
    tok_ws = emit("w_s", {"w_s": gs["w_s"]})
    zb = ZW // 3
    dwz = _tn_matmul("mix_dwz", dz, pl.BlockSpec((T, zb), lambda j: (0, j)), h2,
                     S((ZW, D), BF), pl.BlockSpec((zb, D), lambda j: (j, 0)), 3)
    tok = emit("mid", {"wcq": dwcq, "wco": dwco, "wckv": dwckv, "wout": dwout, "wz": dwz})
    dx1, dx1b, gs["g_mix"] = _mix_proj_bwd(dz, wm["wz"], x1, _after(_after(p["g_mix"], tok), tok_ws), dx2)

    dgu1 = _ffn_bwd_act("ffn1_bwd_act", dx1b, h1, wup1, wdn1)
    tok = emit("ffn1_dn", {"wdn1": _ffn_dwdn("ffn1", a1, dx1b)})
    tok = emit("ffn1_up", {"wup1": _ffn_dwup("ffn1", h1, dgu1, after=tok)})
    dx0, gs["g_ffn1"] = _ffn_dx("ffn1_dx", dgu1, wup1, x, _after(p["g_ffn1"], tok), dx1)
    return sq, dx0, gs


MESH = pl.DeviceIdType.MESH
HBM_SPEC = pl.BlockSpec(memory_space=pltpu.HBM)
N_PEER = N_DEV - 1


def _place():
    return lax.axis_index("x"), lax.axis_index("y"), lax.axis_index("c")


def _slot(px, py, pc):
    return 4 * px + 2 * py + pc


SEM_SPEC = pl.BlockSpec(memory_space=pltpu.SEMAPHORE)
ANY_SPEC = pl.BlockSpec(memory_space=pl.ANY)
DATAFLOW = pltpu.SideEffectType.DATAFLOW_SIDE_EFFECTING


def _hbm(a):
    return pltpu.with_memory_space_constraint(a, pltpu.HBM)


def _peer(x, y, c, r):
    return (1 - x if r & 4 else x, 1 - y if r & 2 else y, 1 - c if r & 1 else c)


def _place_own(srcs, whole):
    my = _slot(*_place())
    lands = []
    for s in srcs:
        blk = s[None] if whole else lax.dynamic_slice_in_dim(s, my, 1, 0)
        shape = (N_DEV,) + s.shape if whole else s.shape
        lands.append(lax.dynamic_update_slice_in_dim(lax.empty(shape, s.dtype), blk, my, 0))
    return lands


ALL_PEERS = tuple(range(1, N_DEV))
NEAR_PEERS = (1, 2, 4, 6)
SAME_CORE = (2, 4, 6)


def _copy_start(name, srcs, lands, whole, peers=None):
    n = len(srcs)
    peers = peers or [ALL_PEERS] * n

    def body(*refs):
        src, land = refs[:n], refs[n:2 * n]
        send, recv = refs[2 * n:3 * n], refs[3 * n:4 * n]
        token = refs[6 * n]
        x, y, c = _place()
        my = _slot(x, y, c)
        for a in range(n):
            for r in peers[a]:
                p = _peer(x, y, c, r)
                pltpu.make_async_remote_copy(
                    src_ref=src[a] if whole else src[a].at[_slot(*p)], dst_ref=land[a].at[my],
                    send_sem=send[a].at[r - 1], recv_sem=recv[a].at[r - 1], device_id=p, device_id_type=MESH).start()
        token[...] = jnp.zeros_like(token)

    out = pl.pallas_call(
        body, name=name,
        out_shape=([pltpu.SemaphoreType.DMA((N_PEER,))] * (2 * n)
                   + [pltpu.HBM(s.shape, s.dtype) for s in srcs] + [pltpu.HBM(s.shape, s.dtype) for s in lands]
                   + [S((8, LANES), F32)]),
        in_specs=[HBM_SPEC] * (2 * n),
        out_specs=[SEM_SPEC] * (2 * n) + [HBM_SPEC] * (2 * n) + [pl.BlockSpec(memory_space=pltpu.VMEM)],
        input_output_aliases={i: 2 * n + i for i in range(2 * n)},
        compiler_params=pltpu.CompilerParams(has_side_effects=DATAFLOW),
    )(*[_hbm(s) for s in srcs], *[_hbm(s) for s in lands])
    return out[:n], out[n:2 * n], out[2 * n:3 * n], out[3 * n:4 * n], out[4 * n]


def _copy_wait(name, srcs, lands, send, recv, after, whole, peers=None, with_srcs=False):
    n = len(srcs)
    peers = peers or [ALL_PEERS] * n

    def body(*refs):
        src, land = refs[:n], refs[n:2 * n]
        snd, rcv = refs[2 * n:3 * n], refs[3 * n:4 * n]
        x, y, c = _place()
        for a in range(n):
            for r in peers[a]:
                p = _peer(x, y, c, r)
                ps = _slot(*p)
                cp = pltpu.make_async_remote_copy(
                    src_ref=src[a] if whole else src[a].at[ps], dst_ref=land[a].at[ps],
                    send_sem=snd[a].at[r - 1], recv_sem=rcv[a].at[r - 1], device_id=p, device_id_type=MESH)
                cp.wait_send()
                cp.wait_recv()

    out = pl.pallas_call(
        body, name=name,
        out_shape=[pltpu.HBM(s.shape, s.dtype) for s in srcs] + [pltpu.HBM(s.shape, s.dtype) for s in lands],
        in_specs=[HBM_SPEC] * (2 * n) + [SEM_SPEC] * (2 * n) + [ANY_SPEC],
        out_specs=[HBM_SPEC] * (2 * n),
        input_output_aliases={i: i for i in range(2 * n)},
        compiler_params=pltpu.CompilerParams(has_side_effects=DATAFLOW),
    )(*srcs, *lands, *send, *recv, after)
    return (out[:n], out[n:]) if with_srcs else out[n:]


def _forward_start(name, lands):
    n = len(lands)

    def body(*refs):
        land = refs[:n]
        send, recv = refs[n:2 * n], refs[2 * n:3 * n]
        token = refs[4 * n]
        x, y, c = _place()
        for a in range(n):
            for r in SAME_CORE:
                blk = land[a].at[_slot(*_peer(x, y, c, r))]
                pltpu.make_async_remote_copy(
                    src_ref=blk, dst_ref=blk, send_sem=send[a].at[r - 1], recv_sem=recv[a].at[r - 1],
                    device_id=(x, y, 1 - c), device_id_type=MESH).start()
        token[...] = jnp.zeros_like(token)

    out = pl.pallas_call(
        body, name=name,
        out_shape=([pltpu.SemaphoreType.DMA((N_PEER,))] * (2 * n) + [pltpu.HBM(s.shape, s.dtype) for s in lands]
                   + [S((8, LANES), F32)]),
        in_specs=[HBM_SPEC] * n,
        out_specs=[SEM_SPEC] * (2 * n) + [HBM_SPEC] * n + [pl.BlockSpec(memory_space=pltpu.VMEM)],
        input_output_aliases={i: 2 * n + i for i in range(n)},
        compiler_params=pltpu.CompilerParams(has_side_effects=DATAFLOW),
    )(*[_hbm(s) for s in lands])
    return out[:n], out[n:2 * n], out[2 * n:3 * n], out[3 * n]


def _forward_wait(name, lands, send, recv, after):
    n = len(lands)

    def body(*refs):
        land = refs[:n]
        snd, rcv = refs[n:2 * n], refs[2 * n:3 * n]
        x, y, c = _place()
        for a in range(n):
            for r in SAME_CORE:
                cp = pltpu.make_async_remote_copy(
                    src_ref=land[a].at[_slot(*_peer(x, y, c, r))], dst_ref=land[a].at[_slot(*_peer(x, y, c, r | 1))],
                    send_sem=snd[a].at[r - 1], recv_sem=rcv[a].at[r - 1], device_id=(x, y, 1 - c),
                    device_id_type=MESH)
                cp.wait_send()
                cp.wait_recv()

    return pl.pallas_call(
        body, name=name,
        out_shape=[pltpu.HBM(s.shape, s.dtype) for s in lands],
        in_specs=[HBM_SPEC] * n + [SEM_SPEC] * (2 * n) + [ANY_SPEC],
        out_specs=[HBM_SPEC] * n,
        input_output_aliases={i: i for i in range(n)},
        compiler_params=pltpu.CompilerParams(has_side_effects=DATAFLOW),
    )(*lands, *send, *recv, after)


def _adamw(w, g, m, v):
    m2 = ADAM_B1 * m + (1.0 - ADAM_B1) * g
    v2 = ADAM_B2 * v + (1.0 - ADAM_B2) * (g * g)
    m_hat = m2 / (1.0 - ADAM_B1 ** ADAM_STEP)
    v_hat = v2 / (1.0 - ADAM_B2 ** ADAM_STEP)
    delta = -ADAM_LR * (m_hat / (jnp.sqrt(v_hat) + ADAM_EPS) + ADAM_WD * w)
    return delta, m2, v2


def _adamw_big(name, slots, w, m, v, own=None):
    R, C = w.shape
    tr = next((t for t in (256, 352) if R % t == 0), R)

    def finish(g, w_ref, m_ref, v_ref, g_ref, d_ref, m2_ref, v2_ref):
        d, m2, v2 = _adamw(w_ref[...], g, m_ref[...], v_ref[...])
        g_ref[...] = g
        d_ref[...] = d
        m2_ref[...] = m2
        v2_ref[...] = v2

    if own is None:
        def body(s_ref, *refs):
            g = s_ref[0].astype(F32)
            for k in range(1, N_DEV):
                g = g + s_ref[k].astype(F32)
            finish(g, *refs)

        row = pl.BlockSpec((tr, C), lambda i: (i, 0))
        return pl.pallas_call(
            body, name=name, grid=(R // tr,),
            in_specs=[pl.BlockSpec((N_DEV, tr, C), lambda i: (0, i, 0)), row, row, row],
            out_specs=[row] * 4, out_shape=[S((R, C), F32)] * 4,
            compiler_params=_cp(1))(slots, w, m, v)

    def body(my_ref, s_ref, own_ref, *refs):
        mine = own_ref[...]
        g = None
        for k in range(N_DEV):
            part = jnp.where(my_ref[0] == k, mine, s_ref[k]).astype(F32)
            g = part if g is None else g + part
        finish(g, *refs)

    row = pl.BlockSpec((tr, C), lambda i, my_ref: (i, 0))
    my = jnp.reshape(_slot(*_place()), (1,)).astype(jnp.int32)
    return pl.pallas_call(
        body, name=name,
        grid_spec=pltpu.PrefetchScalarGridSpec(
            num_scalar_prefetch=1, grid=(R // tr,),
            in_specs=[pl.BlockSpec((N_DEV, tr, C), lambda i, my_ref: (0, i, 0)),
                      pl.BlockSpec((None, tr, C), lambda i, my_ref: (my_ref[0], i, 0)), row, row, row],
            out_specs=[row] * 4),
        out_shape=[S((R, C), F32)] * 4, compiler_params=_cp(1))(my, slots, own, w, m, v)


TINY_ROWS = (("b_s", 8), ("g_ffn1", 8), ("g_mix", 8), ("g_ca", 8), ("g_mem", 8), ("g_ffn2", 8), ("g_sgu", 4),
             ("g_fox_o", 4), ("g_gmlp_o", 4), ("g_cq", 2), ("g_ck", 2), ("g_q", 1), ("g_k", 1), ("b_f", 1),
             ("loss", 1))
TINY_P = 72


def _tiny_pieces(width):
    return [(j, slice(j * LANES, min((j + 1) * LANES, width))) for j in range(-(-width // LANES))]


def _pack_tiny(grads, sq):
    names = [n for n, _ in TINY_ROWS if n != "loss"]

    def body(*refs):
        ins, sq_ref, o_ref = refs[:len(names)], refs[len(names)], refs[len(names) + 1]
        o_ref[...] = jnp.zeros_like(o_ref)
        at = 0
        for ref, (name, r) in zip(ins, TINY_ROWS):
            if name == "b_s":
                o_ref[at:at + r, :] = ref[...]
            else:
                for j, cols in _tiny_pieces(ref.shape[1]):
                    o_ref[at + j:at + j + 1, 0:cols.stop - cols.start] = ref[:, cols]
            at += r
        o_ref[at:at + 1, :] = sq_ref[0:1, :]

    return pl.pallas_call(body, name="tiny_pack", out_shape=S((TINY_P, LANES), F32))(
        *[grads[n] for n in names], sq)


def _adamw_tiny(slots, w, m, v):
    names = [n for n, _ in TINY_ROWS if n != "loss"]
    k = len(names)

    def body(s_ref, *refs):
        ins, outs, loss_ref = refs[:3 * k], refs[3 * k:7 * k], refs[7 * k]
        g_all = s_ref[0]
        for d in range(1, N_DEV):
            g_all = g_all + s_ref[d]
        at = 0
        for i, (name, r) in enumerate(TINY_ROWS[:k]):
            w_ref, m_ref, v_ref = ins[i], ins[k + i], ins[2 * k + i]
            o = outs[4 * i:4 * i + 4]
            if name == "b_s":
                pieces = [(slice(at, at + r), slice(0, LANES), (slice(None), slice(None)))]
            else:
                pieces = [(slice(at + j, at + j + 1), slice(0, c.stop - c.start), (slice(None), c))
                          for j, c in _tiny_pieces(w_ref.shape[1])]
            for rows, lanes, dst in pieces:
                g = g_all[rows, lanes]
                res = (g,) + _adamw(w_ref[dst], g, m_ref[dst], v_ref[dst])
                for ref, val in zip(o, res):
                    ref[dst] = val
            at += r
        loss_ref[...] = g_all[at:at + 1, :]

    shapes = [S(w[n].shape, F32) for n in names]
    out = pl.pallas_call(
        body, name="adamw_tiny", out_shape=[s for s in shapes for _ in range(4)] + [S((1, LANES), F32)],
    )(slots, *[w[n] for n in names], *[m[n] for n in names], *[v[n] for n in names])
    stores = ({}, {}, {}, {})
    for i, n in enumerate(names):
        for store, t in zip(stores, out[4 * i:4 * i + 4]):
            store[n] = t
    return stores, out[4 * k]


WEIGHTS =('g_ffn1', 'w_ffn1_in', 'w_ffn1_out', 'g_mix', 'w_in', 'b_f', 'g_q', 'g_k', 'g_sgu', 'w_s', 'b_s',
           'g_fox_o', 'g_gmlp_o', 'w_out', 'g_ca', 'g_mem', 'w_cq', 'w_ckv', 'g_cq', 'g_ck', 'w_co', 'g_ffn2',
           'w_ffn2_in', 'w_ffn2_out')
BIG = ('w_ffn1_in', 'w_ffn1_out', 'w_in', 'w_out', 'w_cq', 'w_ckv', 'w_co', 'w_ffn2_in', 'w_ffn2_out')
TRANSPOSED = ('w_ffn1_in', 'w_in', 'w_ffn2_in')
TWO_LEVEL = ('w_ffn1_in', 'w_in')
GATHER_GROUPS = {"ffn1_up": ("w_ffn1_in",), "ffn1_dn": ("w_ffn1_out",), "mix": ("w_in", "w_out"),
                 "ca": ("w_cq", "w_ckv", "w_co"), "ffn2": ("w_ffn2_in", "w_ffn2_out")}
QKV_W = 3 * FOX_W
UV_OFF = QKV_W + FOX_HEADS


def kernel(x, mem, g_ffn1, w_ffn1_in, w_ffn1_out, g_mix, w_in, b_f, g_q, g_k, g_sgu, w_s, b_s, g_fox_o, g_gmlp_o, w_out, g_ca, g_mem, w_cq, w_ckv, g_cq, g_ck, w_co, g_ffn2, w_ffn2_in, w_ffn2_out, loss_target, m_g_ffn1, m_w_ffn1_in, m_w_ffn1_out, m_g_mix, m_w_in, m_b_f, m_g_q, m_g_k, m_g_sgu, m_w_s, m_b_s, m_g_fox_o, m_g_gmlp_o, m_w_out, m_g_ca, m_g_mem, m_w_cq, m_w_ckv, m_g_cq, m_g_ck, m_w_co, m_g_ffn2, m_w_ffn2_in, m_w_ffn2_out, v_g_ffn1, v_w_ffn1_in, v_w_ffn1_out, v_g_mix, v_w_in, v_b_f, v_g_q, v_g_k, v_g_sgu, v_w_s, v_b_s, v_g_fox_o, v_g_gmlp_o, v_w_out, v_g_ca, v_g_mem, v_w_cq, v_w_ckv, v_g_cq, v_g_ck, v_w_co, v_g_ffn2, v_w_ffn2_in, v_w_ffn2_out):
    args = dict(locals())
    w = {n: args[n] for n in WEIGHTS}
    mo = {n: args["m_" + n] for n in WEIGHTS}
    vo = {n: args["v_" + n] for n in WEIGHTS}
    D = D_MODEL

    def local(n, a):
        return a[0].T if n in TRANSPOSED else a[0]

    g_peers = [NEAR_PEERS if n in TWO_LEVEL else ALL_PEERS for n in BIG]
    handles = {}

    def start_gather(name, names, arrays):
        snd, rcv, src, land, token = _copy_start(name, arrays, _place_own(arrays, True), True,
                                                 peers=[g_peers[BIG.index(n)] for n in names])
        handles.update({n: (src[i], land[i], snd[i], rcv[i]) for i, n in enumerate(names)})
        return token

    first = local(BIG[0], w[BIG[0]]).astype(BF)
    fb = first.shape[0]
    token_first = start_gather("gather_start_first", BIG[:1], [first])
    token_rest = start_gather("gather_start_rest", BIG[1:],
                              [(local(n, w[n]) + token_first[0:1, 0:1]).astype(BF) for n in BIG[1:]])

    tiny_names = [n for n, _ in TINY_ROWS if n != "loss"]

    def weights(group, after):
        names = GATHER_GROUPS[group]
        hs = [handles[n] for n in names]
        got = list(_copy_wait("gather_wait_" + group, [h[0] for h in hs], [h[1] for h in hs], [h[2] for h in hs],
                              [h[3] for h in hs], token_rest if group == "ffn1_up" else after, True,
                              peers=[g_peers[BIG.index(n)] for n in names]))
        passed = [i for i, n in enumerate(names) if n in TWO_LEVEL]
        if passed:
            f_snd, f_rcv, f_land, f_token = _forward_start("gather_pass_start_" + group, [got[i] for i in passed])
            for i, t in zip(passed, _forward_wait("gather_pass_wait_" + group, f_land, f_snd, f_rcv, f_token)):
                got[i] = t
        got = dict(zip(names, got))
        if group == "ffn1_up":
            return {"wup1": got["w_ffn1_in"].reshape(2, N_FFN_BLK, fb, D)}
        if group == "ffn1_dn":
            return {"wdn1": got["w_ffn1_out"].reshape(N_FFN_BLK, fb, D)}
        if group == "mix":
            full = got["w_in"].reshape(-1, D)
            wz = jnp.concatenate([full[:QKV_W], full[UV_OFF:], full[QKV_W:UV_OFF],
                                  jnp.zeros((LANES - FOX_HEADS, D), BF)], axis=0)
            return {"wz": wz, "wout": got["w_out"].reshape(D, D)}
        if group == "ca":
            return {"wcq": got["w_cq"].reshape(D, D), "wco": got["w_co"].reshape(D, D), "wckv": got["w_ckv"]}
        return {"wup2": got["w_ffn2_in"].reshape(2, N_FFN_BLK, fb, D),
                "wdn2": got["w_ffn2_out"].reshape(N_FFN_BLK, fb, D)}

    flying = {}

    def emit(group, g):
        if group == "w_s":
            part = [g["w_s"].reshape(-1, LANES)]
            *copies, token = _copy_start("w_s_start", part, _place_own(part, True), True)
            flying[group] = copies
            return token
        if group == "ffn2":
            parts = {"w_ffn2_in": g["wup2"], "w_ffn2_out": g["wdn2"].reshape(N_DEV, -1, D)}
        elif group == "ffn1_dn":
            parts = {"w_ffn1_out": g["wdn1"].reshape(N_DEV, -1, D)}
        elif group == "ffn1_up":
            parts = {"w_ffn1_in": g["wup1"]}
        else:
            gz = g["wz"]
            g_in = jnp.concatenate([gz[:QKV_W], gz[Z_F:Z_F + FOX_HEADS], gz[QKV_W:Z_F]], axis=0)
            parts = {"w_in": g_in.reshape(N_DEV, -1, D).astype(BF),
                     "w_out": g["wout"].reshape(N_DEV, -1, D), "w_cq": g["wcq"].reshape(N_DEV, -1, D),
                     "w_co": g["wco"].reshape(N_DEV, -1, D), "w_ckv": g["wckv"]}
        names = list(parts)
        srcs = [parts[n] for n in names]
        *copies, token = _copy_start("exchange_start_" + group, srcs, [lax.empty(s.shape, s.dtype) for s in srcs],
                                     False)
        flying[group] = (names, copies)
        return token

    small = {n: (w[n][0] if n == "b_s" else w[n]) for n in tiny_names}
    small["w_s"] = w["w_s"][0]

    sq, dx0, gs = _local_step(x[0], mem[0], loss_target[0], small, weights, emit)

    sm_parts = [_pack_tiny(gs, sq)]
    sm_snd, sm_rcv, sm_src, sm_land, sm_token = _copy_start("tiny_start", sm_parts, _place_own(sm_parts, True), True)

    grad, delta, new_m, new_v = {}, {}, {}, {}

    def update(group, after):
        names, (snd, rcv, srcs, lands) = flying[group]
        owns, slots = _copy_wait("exchange_wait_" + group, srcs, lands, snd, rcv, after, False, with_srcs=True)
        for n, sl, own in zip(names, slots, owns):
            g, d, m2, v2 = _adamw_big("adamw_" + n, sl, local(n, w[n]), local(n, mo[n]), local(n, vo[n]), own=own)
            grad[n], delta[n], new_m[n], new_v[n] = (
                (t.T if n in TRANSPOSED else t).reshape(w[n].shape) for t in (g, d, m2, v2))
        return d

    last = update("ffn2", sm_token)
    last = update("mid", last)
    last = update("ffn1_dn", last)
    last = update("ffn1_up", last)
    ws_snd, ws_rcv, ws_src, ws_land = flying["w_s"]
    ws_all, = _copy_wait("w_s_wait", ws_src, ws_land, ws_snd, ws_rcv, last, True)
    tiny_all, = _copy_wait("tiny_wait", sm_src, sm_land, sm_snd, sm_rcv, ws_all, True)
    ws_shape = w["w_s"].shape
    for store, t in zip((grad, delta, new_m, new_v), _adamw_big(
            "adamw_w_s", ws_all, *[a["w_s"].reshape(-1, LANES) for a in (w, mo, vo)])):
        store["w_s"] = t.reshape(ws_shape)
    stores, loss_row = _adamw_tiny(tiny_all, *[{n: (a[n][0] if n == "b_s" else a[n]) for n in tiny_names}
                                               for a in (w, mo, vo)])
    for store, t in zip((grad, delta, new_m, new_v), stores):
        store.update({n: v.reshape(w[n].shape) for n, v in t.items()})
    loss = loss_row[0, 0] * (0.5 / D)

    return (loss, dx0[None], *[grad[n] for n in WEIGHTS], *[delta[n] for n in WEIGHTS],
            *[new_m[n] for n in WEIGHTS], *[new_v[n] for n in WEIGHTS])
```

```python
import functools

import jax
import jax.numpy as jnp
from jax import lax
from jax.experimental import pallas as pl
from jax.experimental.pallas import tpu as pltpu

F32 = jnp.float32
BF = jnp.bfloat16
S = jax.ShapeDtypeStruct

N_DEV = 8
D_MODEL = 1024
FOX_HEADS, FOX_HD = 8, 64
FOX_W = 512
GMLP_G, GMLP_GD = 8, 64
GMLP_W = 512
CHUNK = 128
CA_HEADS, CA_HD = 4, 256
N_FFN_BLK = 4
ZW = 2688
Z_Q, Z_K, Z_V, Z_U, Z_G, Z_F = 0, 512, 1024, 1536, 2048, 2560
EPS = 1e-6
NEG = -1e30
LANES = 128

ADAM_LR, ADAM_B1, ADAM_B2, ADAM_EPS, ADAM_WD, ADAM_STEP = 0.001, 0.9, 0.999, 1e-08, 0.01, 10

VMEM_LIMIT = 52 * 2 ** 20


def _cp(n_axes):
    return pltpu.CompilerParams(dimension_semantics=("arbitrary",) * n_axes, vmem_limit_bytes=VMEM_LIMIT)


def _nn(a, b):
    return jnp.dot(a, b, preferred_element_type=F32)


def _nt(a, b):
    return lax.dot_general(a, b, (((1,), (1,)), ((), ())), preferred_element_type=F32)


def _tn(a, b):
    return lax.dot_general(a, b, (((0,), (0,)), ((), ())), preferred_element_type=F32)


def _hi(mask, x):
    return jnp.dot(mask.astype(F32), x, precision=lax.Precision.HIGHEST, preferred_element_type=F32)


def _hi3(mask, x):
    mb = mask.astype(BF)
    hi = x.astype(BF)
    r1 = x - hi.astype(F32)
    mid = r1.astype(BF)
    lo = (r1 - mid.astype(F32)).astype(BF)
    return _nn(mb, hi) + _nn(mb, mid) + _nn(mb, lo)


def _rstd(x):
    return lax.rsqrt(jnp.mean(x * x, axis=-1, keepdims=True) + EPS)


def _norm_bwd(dy, x, g, r=None):
    r = _rstd(x) if r is None else r
    xh = x * r
    dxh = dy * g
    dx = r * (dxh - xh * jnp.mean(dxh * xh, axis=-1, keepdims=True))
    return dx, dy * xh


def _acc_rows(ref, first, val):
    srow = jnp.sum(val, axis=0, keepdims=True)

    @pl.when(first)
    def _():
        ref[...] = srow

    @pl.when(jnp.logical_not(first))
    def _():
        ref[...] += srow


def _gelu(x):
    c = 0.7978845608028654
    return 0.5 * x * (1.0 + jnp.tanh(c * (x + 0.044715 * x * x * x)))


def _gelu_grad(x):
    c = 0.7978845608028654
    t = jnp.tanh(c * (x + 0.044715 * x * x * x))
    return 0.5 * (1.0 + t) + 0.5 * x * (1.0 - t * t) * c * (1.0 + 3 * 0.044715 * x * x)


def _tile(n, pref):
    return pref if n % pref == 0 else n


def _ffn_up(name, x, g, wup):
    T, D = x.shape
    FB = wup.shape[-2]
    tm = _tile(T, 1024)

    def body(x_ref, g_ref, w_ref, a_ref, h_ref):
        @pl.when(pl.program_id(1) == 0)
        def _():
            xf = x_ref[...]
            h_ref[...] = (xf * _rstd(xf) * g_ref[...]).astype(BF)

        hb = h_ref[...]
        gg = _nt(hb, w_ref[0])
        uu = _nt(hb, w_ref[1])
        a_ref[...] = (gg * jax.nn.sigmoid(gg) * uu).astype(BF)

    return pl.pallas_call(
        body, name=name, grid=(T // tm, N_FFN_BLK),
        in_specs=[pl.BlockSpec((tm, D), lambda i, j: (i, 0)),
                  pl.BlockSpec((1, D), lambda i, j: (0, 0)),
                  pl.BlockSpec((2, None, FB, D), lambda i, j: (0, j, 0, 0))],
        out_specs=[pl.BlockSpec((None, tm, FB), lambda i, j: (j, i, 0)),
                   pl.BlockSpec((tm, D), lambda i, j: (i, 0))],
        out_shape=[S((N_FFN_BLK, T, FB), BF), S((T, D), BF)],
        compiler_params=_cp(2))(x, g, wup)


def _ffn_down(name, a, wdn, x):
    _, T, FB = a.shape
    D = x.shape[1]
    tm = _tile(T, 512)

    def body(a_ref, w_ref, x_ref, o_ref):
        p = _nn(a_ref[0], w_ref[0])
        for j in range(1, N_FFN_BLK):
            p = p + _nn(a_ref[j], w_ref[j])
        o_ref[...] = x_ref[...] + 0.5 * p

    return pl.pallas_call(
        body, name=name, grid=(T // tm,),
        in_specs=[pl.BlockSpec((N_FFN_BLK, tm, FB), lambda i: (0, i, 0)),
                  pl.BlockSpec((N_FFN_BLK, FB, D), lambda i: (0, 0, 0)),
                  pl.BlockSpec((tm, D), lambda i: (i, 0))],
        out_specs=pl.BlockSpec((tm, D), lambda i: (i, 0)),
        out_shape=S((T, D), F32),
        compiler_params=_cp(1))(a, wdn, x)


def _ffn_down_loss(name, a, wdn, x, target):
    _, T, FB = a.shape
    D = x.shape[1]
    tm = _tile(T, 512)

    def body(a_ref, w_ref, x_ref, t_ref, d_ref, db_ref, loss_ref):
        i = pl.program_id(0)
        p = _nn(a_ref[0], w_ref[0])
        for j in range(1, N_FFN_BLK):
            p = p + _nn(a_ref[j], w_ref[j])
        diff = (x_ref[...] + 0.5 * p) - t_ref[...]
        dy = diff * (1.0 / D)
        d_ref[...] = dy
        db_ref[...] = dy.astype(BF)
        sq = jnp.zeros((8, LANES), F32) + jnp.sum(diff * diff)

        @pl.when(i == 0)
        def _():
            loss_ref[...] = sq

        @pl.when(i > 0)
        def _():
            loss_ref[...] += sq

    row = pl.BlockSpec((tm, D), lambda i: (i, 0))
    return pl.pallas_call(
        body, name=name, grid=(T // tm,),
        in_specs=[pl.BlockSpec((N_FFN_BLK, tm, FB), lambda i: (0, i, 0)),
                  pl.BlockSpec((N_FFN_BLK, FB, D), lambda i: (0, 0, 0)), row, row],
        out_specs=[row, row, pl.BlockSpec((8, LANES), lambda i: (0, 0))],
        out_shape=[S((T, D), F32), S((T, D), BF), S((8, LANES), F32)],
        compiler_params=_cp(1))(a, wdn, x, target)


def _ffn_bwd_act(name, dyb, h, wup, wdn):
    T, D = h.shape
    FB = wup.shape[-2]
    tm = _tile(T, 1024)

    def body(d_ref, h_ref, wu_ref, wd_ref, o_ref):
        da = 0.5 * _nt(d_ref[...], wd_ref[...])
        hb = h_ref[...]
        gg = _nt(hb, wu_ref[0])
        uu = _nt(hb, wu_ref[1])
        sg = jax.nn.sigmoid(gg)
        o_ref[0] = (da * uu * (sg * (1.0 + gg * (1.0 - sg)))).astype(BF)
        o_ref[1] = (da * (gg * sg)).astype(BF)

    return pl.pallas_call(
        body, name=name, grid=(T // tm, N_FFN_BLK),
        in_specs=[pl.BlockSpec((tm, D), lambda i, j: (i, 0)),
                  pl.BlockSpec((tm, D), lambda i, j: (i, 0)),
                  pl.BlockSpec((2, None, FB, D), lambda i, j: (0, j, 0, 0)),
                  pl.BlockSpec((None, FB, D), lambda i, j: (j, 0, 0))],
        out_specs=pl.BlockSpec((2, None, tm, FB), lambda i, j: (0, j, i, 0)),
        out_shape=S((2, N_FFN_BLK, T, FB), BF),
        compiler_params=_cp(2))(dyb, h, wup, wdn)


def _ffn_dx(name, dgu, wup, x, g, dy):
    T, D = x.shape
    FB = wup.shape[-2]
    tm = _tile(T, 512)

    def body(d_ref, w_ref, x_ref, g_ref, dy_ref, dx_ref, dg_ref):
        p = None
        for j in range(N_FFN_BLK):
            for half in range(2):
                t = _nn(d_ref[half, j], w_ref[half, j])
                p = t if p is None else p + t
        dx, dgr = _norm_bwd(p, x_ref[...], g_ref[...])
        dx_ref[...] = dx + dy_ref[...]
        _acc_rows(dg_ref, pl.program_id(0) == 0, dgr)

    return pl.pallas_call(
        body, name=name, grid=(T // tm,),
        in_specs=[pl.BlockSpec((2, N_FFN_BLK, tm, FB), lambda i: (0, 0, i, 0)),
                  pl.BlockSpec((2, N_FFN_BLK, FB, D), lambda i: (0, 0, 0, 0), pipeline_mode=pl.Buffered(1)),
                  pl.BlockSpec((tm, D), lambda i: (i, 0)),
                  pl.BlockSpec((1, D), lambda i: (0, 0)),
                  pl.BlockSpec((tm, D), lambda i: (i, 0))],
        out_specs=[pl.BlockSpec((tm, D), lambda i: (i, 0)),
                   pl.BlockSpec((1, D), lambda i: (0, 0))],
        out_shape=[S((T, D), F32), S((1, D), F32)],
        compiler_params=_cp(1))(dgu, wup, x, g, dy)


def _tn_matmul(name, a, a_spec, b, out_shape, out_spec, n_blocks, scale=1.0, after=None):
    extra = [] if after is None else [after]

    def body(a_ref, b_ref, *rest):
        o_ref = rest[-1]
        o_ref[...] = (_tn(a_ref[...], b_ref[...]) * scale).astype(o_ref.dtype)

    return pl.pallas_call(
        body, name=name, grid=(n_blocks,),
        in_specs=[a_spec, pl.BlockSpec(b.shape, lambda j: (0, 0), pipeline_mode=pl.Buffered(1))]
        + [pl.BlockSpec((8, LANES), lambda j: (0, 0)) for _ in extra],
        out_specs=out_spec, out_shape=out_shape, compiler_params=_cp(1))(a, b, *extra)


def _ffn_dwup(name, h, dgu, after=None):
    T, D = h.shape
    FB = dgu.shape[-1]
    return _tn_matmul(
        name + "_dwup", dgu.reshape(2 * N_FFN_BLK, T, FB), pl.BlockSpec((None, T, FB), lambda j: (j, 0, 0)), h,
        S((2 * N_FFN_BLK, FB, D), BF), pl.BlockSpec((None, FB, D), lambda j: (j, 0, 0)), 2 * N_FFN_BLK,
        after=after)


def _ffn_dwdn(name, a, dyb):
    _, T, FB = a.shape
    D = dyb.shape[1]
    return _tn_matmul(
        name + "_dwdn", a, pl.BlockSpec((None, T, FB), lambda j: (j, 0, 0)), dyb,
        S((N_FFN_BLK, FB, D), BF), pl.BlockSpec((None, FB, D), lambda j: (j, 0, 0)), N_FFN_BLK, scale=0.5)


def _tri(n, lower):
    r = lax.broadcasted_iota(jnp.int32, (n, n), 0)
    c = lax.broadcasted_iota(jnp.int32, (n, n), 1)
    return (r >= c) if lower else (r <= c)


def _spatial_mix(vgn_b, ws_ref, bst, tm):
    tril = _tri(CHUNK, True)
    wms = [jnp.where(tril, ws_ref[g], 0.0).astype(BF) for g in range(GMLP_G)]
    rows = []
    for c in range(tm // CHUNK):
        cols = []
        for g in range(GMLP_G):
            vs = vgn_b[c * CHUNK:(c + 1) * CHUNK, g * GMLP_GD:(g + 1) * GMLP_GD]
            cols.append(_nn(wms[g], vs) + bst[:, g:g + 1])
        rows.append(jnp.concatenate(cols, axis=1))
    return jnp.concatenate(rows, axis=0), wms


HB = 128
AUG_W = FOX_HEADS * HB
COL_A, COL_B, COL_C = 64, 67, 70
RS_Q, RS_K, RS_V, RS_O = 0, 8, 16, 17


def _piece_matrix(col):
    r = jnp.arange(LANES)
    dst = jnp.where(r < 3 * FOX_HEADS, (r % FOX_HEADS) * HB + col + r // FOX_HEADS, -1)
    return (jnp.arange(AUG_W)[None, :] == dst[:, None]).astype(BF)


def _ones_row(cols):
    c = jnp.arange(AUG_W) % HB
    hit = functools.reduce(jnp.logical_or, [(c >= a) & (c < a + 3) for a in cols])
    return hit.astype(F32)[None, :]


def _pieces(x):
    lane = lax.broadcasted_iota(jnp.int32, x.shape, 1)
    x = jnp.where(lane < FOX_HEADS, x, 0.0)
    hi = x.astype(BF).astype(F32)
    r1 = x - hi
    mid = r1.astype(BF).astype(F32)
    lo = (r1 - mid).astype(BF).astype(F32)
    return (hi + pltpu.roll(mid, FOX_HEADS, 1) + pltpu.roll(lo, 2 * FOX_HEADS, 1)).astype(BF)


def _mix_prep(x, g_mix, wz, bf128, g_q, g_k, g_sgu, w_s, b_st, g_go):
    T, D = x.shape
    tm = _tile(T, 512)
    pc_q, pc_k = _piece_matrix(COL_A), _piece_matrix(COL_B)
    one_q, one_k, one_v = _ones_row([COL_B]), _ones_row([COL_A, COL_C]), _ones_row([COL_A])

    def body(x_ref, gm_ref, wz_ref, bf_ref, gq_ref, gk_ref, gs_ref, ws_ref, bst_ref, go_ref, pq_ref, pk_ref, oq_ref,
             ok_ref, ov_ref, z_ref, h_ref, q_ref, k_ref, v_ref, y_ref, rs_ref, carry_ref):
        i = pl.program_id(0)

        @pl.when(i == 0)
        def _():
            carry_ref[...] = jnp.zeros_like(carry_ref)

        xf = x_ref[...]
        hb = (xf * _rstd(xf) * gm_ref[...]).astype(BF)
        h_ref[...] = hb
        z_ref[...] = _nt(hb, wz_ref[...])

        fl = z_ref[:, Z_F:Z_F + LANES] + bf_ref[...]
        logf = jnp.minimum(fl, 0.0) - jnp.log1p(jnp.exp(-jnp.abs(fl)))
        csum = _hi(_tri(tm, True), logf) + carry_ref[...]
        carry_ref[...] = csum[tm - 1:tm, :]
        ext_q = (_nn(_pieces(csum), pq_ref[...]) + oq_ref[...]).astype(BF)
        ext_k = (_nn(_pieces(-csum), pk_ref[...]) + ok_ref[...]).astype(BF)
        ext_v = jnp.broadcast_to(ov_ref[...], (tm, AUG_W)).astype(BF)

        rs_ref[...] = jnp.zeros_like(rs_ref)
        for h in range(FOX_HEADS):
            lo, hi = slice(h * HB, h * HB + FOX_HD), slice(h * HB + FOX_HD, (h + 1) * HB)
            qh = z_ref[:, Z_Q + h * FOX_HD:Z_Q + (h + 1) * FOX_HD]
            kh = z_ref[:, Z_K + h * FOX_HD:Z_K + (h + 1) * FOX_HD]
            rq, rk = _rstd(qh), _rstd(kh)
            rs_ref[:, RS_Q + h:RS_Q + h + 1] = rq
            rs_ref[:, RS_K + h:RS_K + h + 1] = rk
            q_ref[:, lo] = (qh * rq * gq_ref[...] * 0.125).astype(BF)
            k_ref[:, lo] = (kh * rk * gk_ref[...]).astype(BF)
            v_ref[:, lo] = z_ref[:, Z_V + h * FOX_HD:Z_V + (h + 1) * FOX_HD].astype(BF)
            q_ref[:, hi] = ext_q[:, hi]
            k_ref[:, hi] = ext_k[:, hi]
            v_ref[:, hi] = ext_v[:, hi]

        u = _gelu(z_ref[:, Z_U:Z_U + GMLP_W])
        vg = _gelu(z_ref[:, Z_G:Z_G + GMLP_W])
        rv = _rstd(vg)
        vgn = (vg * rv * gs_ref[...]).astype(BF)
        mixed, _ = _spatial_mix(vgn, ws_ref, bst_ref[...], tm)
        sgu = u * mixed
        ro = _rstd(sgu)
        y_ref[...] = (sgu * ro * go_ref[...]).astype(BF)
        rs_ref[:, RS_V:RS_V + 1] = rv
        rs_ref[:, RS_O:RS_O + 1] = ro

    row = lambda i: (i, 0)
    fix2 = lambda i: (0, 0)
    return pl.pallas_call(
        body, name="mix_prep", grid=(T // tm,),
        in_specs=[pl.BlockSpec((tm, D), row), pl.BlockSpec((1, D), fix2),
                  pl.BlockSpec((ZW, D), fix2, pipeline_mode=pl.Buffered(1)),
                  pl.BlockSpec((1, LANES), fix2), pl.BlockSpec((1, FOX_HD), fix2), pl.BlockSpec((1, FOX_HD), fix2),
                  pl.BlockSpec((1, GMLP_W), fix2), pl.BlockSpec((GMLP_G, CHUNK, CHUNK), lambda i: (0, 0, 0)),
                  pl.BlockSpec((CHUNK, GMLP_G), fix2), pl.BlockSpec((1, GMLP_W), fix2),
                  pl.BlockSpec((LANES, AUG_W), fix2),
                  pl.BlockSpec((LANES, AUG_W), fix2), pl.BlockSpec((1, AUG_W), fix2), pl.BlockSpec((1, AUG_W), fix2),
                  pl.BlockSpec((1, AUG_W), fix2)],
        out_specs=[pl.BlockSpec((tm, ZW), row), pl.BlockSpec((tm, D), row),
                   pl.BlockSpec((tm, AUG_W), row), pl.BlockSpec((tm, AUG_W), row), pl.BlockSpec((tm, AUG_W), row),
                   pl.BlockSpec((tm, GMLP_W), row), pl.BlockSpec((tm, LANES), row)],
        out_shape=[S((T, ZW), F32), S((T, D), BF), S((T, AUG_W), BF), S((T, AUG_W), BF), S((T, AUG_W), BF),
                   S((T, GMLP_W), BF), S((T, LANES), F32)],
        scratch_shapes=[pltpu.VMEM((1, LANES), F32)],
        compiler_params=_cp(1))(x, g_mix, wz, bf128, g_q, g_k, g_sgu, w_s, b_st, g_go, pc_q, pc_k, one_q, one_k,
                                one_v)


def _fox_fwd(q, k, v):
    T = q.shape[0]
    tq = _tile(T, 1024)
    nq = T // tq

    def body(q_ref, k_ref, v_ref, o_ref, lse_ref, m_sc, acc_sc):
        i, j = pl.program_id(0), pl.program_id(1)

        @pl.when(j == 0)
        def _():
            m_sc[...] = jnp.full(m_sc.shape, NEG, F32)
            acc_sc[...] = jnp.zeros_like(acc_sc)

        def step(masked):
            mask = _tri(tq, True) if masked else None
            for h in range(FOX_HEADS):
                hb = slice(h * HB, (h + 1) * HB)
                s = _nt(q_ref[:, hb], k_ref[:, hb])
                if masked:
                    s = jnp.where(mask, s, NEG)
                m_prev = m_sc[h]
                m_new = jnp.maximum(m_prev, jnp.broadcast_to(jnp.max(s, axis=1, keepdims=True), (tq, HB)))
                p = jnp.exp(s - jnp.tile(m_new, (1, tq // HB))).astype(BF)
                acc_sc[:, hb] = jnp.exp(m_prev - m_new) * acc_sc[:, hb] + _nn(p, v_ref[:, hb])
                m_sc[h] = m_new

        @pl.when(j < i)
        def _():
            step(False)

        @pl.when(j == i)
        def _():
            step(True)
            lse_ref[...] = jnp.zeros_like(lse_ref)
            for h in range(FOX_HEADS):
                l = acc_sc[:, h * HB + COL_A:h * HB + COL_A + 1]
                o_ref[:, h * FOX_HD:(h + 1) * FOX_HD] = acc_sc[:, h * HB:h * HB + FOX_HD] / l
                lse_ref[:, h:h + 1] = m_sc[h][:, 0:1] + jnp.log(l)

    qi = lambda i, j: (i, 0)
    kj = lambda i, j: (jnp.minimum(i, j), 0)
    return pl.pallas_call(
        body, name="fox_fwd", grid=(nq, nq),
        in_specs=[pl.BlockSpec((tq, AUG_W), qi), pl.BlockSpec((tq, AUG_W), kj), pl.BlockSpec((tq, AUG_W), kj)],
        out_specs=[pl.BlockSpec((tq, FOX_W), qi), pl.BlockSpec((tq, LANES), qi)],
        out_shape=[S((T, FOX_W), F32), S((T, LANES), F32)],
        scratch_shapes=[pltpu.VMEM((FOX_HEADS, tq, HB), F32), pltpu.VMEM((tq, AUG_W), F32)],
        compiler_params=_cp(2))(q, k, v)


def _fox_bwd(q, k, v, dob):
    T = q.shape[0]
    tq = _tile(T, 512)
    nq = T // tq
    n_sweeps = 1
    half = AUG_W // n_sweeps
    hpg = FOX_HEADS // n_sweeps

    pairs = [(j, i) for j in range(nq) for i in range(j, nq)]
    jt = jnp.asarray([p[0] for p in pairs], jnp.int32)
    it = jnp.asarray([p[1] for p in pairs], jnp.int32)

    def body(jt_ref, it_ref, q_ref, k_ref, v_ref, do_ref, dq_ref, dk_ref, dv_ref, dq_sc):
        t = pl.program_id(1)
        j, i = jt_ref[t], it_ref[t]

        @pl.when(t == 0)
        def _():
            dq_sc[...] = jnp.zeros_like(dq_sc)

        @pl.when(i == j)
        def _():
            dk_ref[...] = jnp.zeros_like(dk_ref)
            dv_ref[...] = jnp.zeros_like(dv_ref)

        def step(masked):
            rows = pl.ds(pl.multiple_of(i * tq, tq), tq)
            mask = _tri(tq, True) if masked else None
            for h in range(hpg):
                hb = slice(h * HB, (h + 1) * HB)
                qh, kh, vh, doh = q_ref[:, hb], k_ref[:, hb], v_ref[:, hb], do_ref[:, hb]
                s = _nt(qh, kh)
                if masked:
                    s = jnp.where(mask, s, NEG)
                p = jnp.exp(s)
                dsb = (p * _nt(doh, vh)).astype(BF)
                dv_ref[:, hb] += _tn(p.astype(BF), doh)
                dk_ref[:, hb] += _tn(dsb, qh)
                dq_sc[rows, hb] += _nn(dsb, kh)

        @pl.when(i > j)
        def _():
            step(False)

        @pl.when(i == j)
        def _():
            step(True)
            dq_ref[...] = dq_sc[pl.ds(pl.multiple_of(j * tq, tq), tq), :]

    qi = pl.BlockSpec((tq, half), lambda g, t, jt_ref, it_ref: (it_ref[t], g))
    kj = pl.BlockSpec((tq, half), lambda g, t, jt_ref, it_ref: (jt_ref[t], g))
    return pl.pallas_call(
        body, name="fox_bwd",
        grid_spec=pltpu.PrefetchScalarGridSpec(
            num_scalar_prefetch=2, grid=(n_sweeps, len(pairs)), in_specs=[qi, kj, kj, qi], out_specs=[kj, kj, kj],
            scratch_shapes=[pltpu.VMEM((T, half), F32)]),
        out_shape=[S((T, AUG_W), F32), S((T, AUG_W), F32), S((T, AUG_W), F32)],
        compiler_params=_cp(2))(jt, it, q, k, v, dob)


def _mix_out(attn, yg, g_fo, wout, x):
    T, D = x.shape
    tm = _tile(T, 1024)

    def body(a_ref, y_ref, g_ref, w_ref, x_ref, o_ref):
        at = a_ref[...]
        yf = (at * _rstd(at) * g_ref[...]).astype(BF)
        o_ref[...] = x_ref[...] + _nn(yf, w_ref[:FOX_W, :]) + _nn(y_ref[...], w_ref[FOX_W:, :])

    row = lambda i: (i, 0)
    return pl.pallas_call(
        body, name="mix_out", grid=(T // tm,),
        in_specs=[pl.BlockSpec((tm, FOX_W), row), pl.BlockSpec((tm, GMLP_W), row),
                  pl.BlockSpec((1, FOX_W), lambda i: (0, 0)), pl.BlockSpec((D, D), lambda i: (0, 0)),
                  pl.BlockSpec((tm, D), row)],
        out_specs=pl.BlockSpec((tm, D), row),
        out_shape=S((T, D), F32),
        compiler_params=_cp(1))(attn, yg, g_fo, wout, x)


def _mix_out_bwd(dx, attn, yg, g_fo, wout, qf, lse):
    T, D = dx.shape
    tm = _tile(T, 512)
    n = T // tm
    pc_l, pc_d = _piece_matrix(COL_C), _piece_matrix(COL_A)

    def body(dx_ref, a_ref, y_ref, g_ref, w_ref, qf_ref, lse_ref, pl_ref, pd_ref,
             qb_ref, dob_ref, dyg_ref, dw_ref, dg_ref, acc_ref, dsum_ref):
        i = pl.program_id(0)
        dxb = dx_ref[...].astype(BF)
        at = a_ref[...]
        yf = (at * _rstd(at) * g_ref[...]).astype(BF)
        dy = _nt(dxb, w_ref[...])
        p_top = _tn(yf, dxb)
        p_bot = _tn(y_ref[...], dxb)

        @pl.when(i == 0)
        def _():
            acc_ref[:FOX_W, :] = p_top
            acc_ref[FOX_W:, :] = p_bot

        @pl.when(i > 0)
        def _():
            acc_ref[:FOX_W, :] += p_top
            acc_ref[FOX_W:, :] += p_bot

        @pl.when(i == n - 1)
        def _():
            dw_ref[...] = acc_ref[...].astype(BF)

        dat, dgr = _norm_bwd(dy[:, :FOX_W], at, g_ref[...])
        _acc_rows(dg_ref, i == 0, dgr)
        dyg_ref[...] = dy[:, FOX_W:]
        prod = dat * at
        dsum_ref[...] = jnp.zeros_like(dsum_ref)
        for h in range(FOX_HEADS):
            dsum_ref[:, h:h + 1] = jnp.sum(prod[:, h * FOX_HD:(h + 1) * FOX_HD], axis=1, keepdims=True)
        ext_d = _nn(_pieces(-dsum_ref[...]), pd_ref[...]).astype(BF)
        ext_l = _nn(_pieces(-lse_ref[...]), pl_ref[...])
        datb = dat.astype(BF)
        for h in range(FOX_HEADS):
            lo, hi = slice(h * HB, h * HB + FOX_HD), slice(h * HB + FOX_HD, (h + 1) * HB)
            dob_ref[:, lo] = datb[:, h * FOX_HD:(h + 1) * FOX_HD]
            dob_ref[:, hi] = ext_d[:, hi]
            qb_ref[:, lo] = qf_ref[:, lo]
            qb_ref[:, hi] = (qf_ref[:, hi].astype(F32) + ext_l[:, hi]).astype(BF)

    row = lambda i: (i, 0)
    fix = lambda i: (0, 0)
    return pl.pallas_call(
        body, name="mix_out_bwd", grid=(n,),
        in_specs=[pl.BlockSpec((tm, D), row), pl.BlockSpec((tm, FOX_W), row), pl.BlockSpec((tm, GMLP_W), row),
                  pl.BlockSpec((1, FOX_W), fix), pl.BlockSpec((D, D), fix), pl.BlockSpec((tm, AUG_W), row),
                  pl.BlockSpec((tm, LANES), row), pl.BlockSpec((LANES, AUG_W), fix),
                  pl.BlockSpec((LANES, AUG_W), fix)],
        out_specs=[pl.BlockSpec((tm, AUG_W), row), pl.BlockSpec((tm, AUG_W), row), pl.BlockSpec((tm, GMLP_W), row),
                   pl.BlockSpec((D, D), fix), pl.BlockSpec((1, FOX_W), fix)],
        out_shape=[S((T, AUG_W), BF), S((T, AUG_W), BF), S((T, GMLP_W), F32), S((D, D), BF), S((1, FOX_W), F32)],
        scratch_shapes=[pltpu.VMEM((D, D), F32), pltpu.VMEM((tm, LANES), F32)],
        compiler_params=_cp(1))(dx, attn, yg, g_fo, wout, qf, lse, pc_l, pc_d)


def _mix_prep_bwd(z, dq, dk, dv, dyg, rs, bf128, g_q, g_k, g_sgu, w_s, b_st, g_go):
    T = z.shape[0]
    tm = _tile(T, 512)
    n = T // tm

    def body(z_ref, dq_ref, dk_ref, dv_ref, dyg_ref, rs_ref, bf_ref, gq_ref, gk_ref, gs_ref, ws_ref,
             bst_ref, go_ref, dz_ref, dgq_ref, dgk_ref, dgs_ref, dgo_ref, dws_ref, dbst_ref, dbf_ref, carry_ref):
        i = pl.program_id(0)
        first = i == 0
        rs = rs_ref[...]

        @pl.when(first)
        def _():
            carry_ref[...] = jnp.zeros_like(carry_ref)

        lane = lax.broadcasted_iota(jnp.int32, (tm, LANES), 1)
        dc = jnp.zeros((tm, LANES), F32)
        gq_rows, gk_rows = [], []
        for h in range(FOX_HEADS):
            hp = slice(h * HB, h * HB + FOX_HD)
            dqh, gqr = _norm_bwd(dq_ref[:, hp] * 0.125, z_ref[:, Z_Q + h * FOX_HD:Z_Q + (h + 1) * FOX_HD], gq_ref[...],
                                 rs[:, RS_Q + h:RS_Q + h + 1])
            dkh, gkr = _norm_bwd(dk_ref[:, hp], z_ref[:, Z_K + h * FOX_HD:Z_K + (h + 1) * FOX_HD], gk_ref[...],
                                 rs[:, RS_K + h:RS_K + h + 1])
            dz_ref[:, Z_Q + h * FOX_HD:Z_Q + (h + 1) * FOX_HD] = dqh.astype(BF)
            dz_ref[:, Z_K + h * FOX_HD:Z_K + (h + 1) * FOX_HD] = dkh.astype(BF)
            dz_ref[:, Z_V + h * FOX_HD:Z_V + (h + 1) * FOX_HD] = dv_ref[:, hp].astype(BF)
            dch = dq_ref[:, h * HB + COL_A:h * HB + COL_A + 1] - dk_ref[:, h * HB + COL_B:h * HB + COL_B + 1]
            dc = jnp.where(lane == h, dch, dc)
            gq_rows.append(gqr)
            gk_rows.append(gkr)
        _acc_rows(dgq_ref, first, functools.reduce(lambda a, b: a + b, gq_rows))
        _acc_rows(dgk_ref, first, functools.reduce(lambda a, b: a + b, gk_rows))

        dlogf = _hi3(_tri(tm, False), dc) + carry_ref[...]
        carry_ref[...] = dlogf[0:1, :]
        fl = z_ref[:, Z_F:Z_F + LANES] + bf_ref[...]
        lane = lax.broadcasted_iota(jnp.int32, (tm, LANES), 1)
        df = jnp.where(lane < FOX_HEADS, dlogf * jax.nn.sigmoid(-fl), 0.0)
        dz_ref[:, Z_F:Z_F + LANES] = df.astype(BF)
        _acc_rows(dbf_ref, first, df)

        u_pre = z_ref[:, Z_U:Z_U + GMLP_W]
        vg_pre = z_ref[:, Z_G:Z_G + GMLP_W]
        u = _gelu(u_pre)
        vg = _gelu(vg_pre)
        rv = rs[:, RS_V:RS_V + 1]
        vgn = (vg * rv * gs_ref[...]).astype(BF)
        bst = bst_ref[...]
        mixed, wms = _spatial_mix(vgn, ws_ref, bst, tm)
        sgu = u * mixed
        dsgu, gor = _norm_bwd(dyg_ref[...], sgu, go_ref[...], rs[:, RS_O:RS_O + 1])
        _acc_rows(dgo_ref, first, gor)
        du = dsgu * mixed
        dmixed = dsgu * u
        dmb = dmixed.astype(BF)
        tril = _tri(CHUNK, True)
        dvgn_rows = []
        dws = [None] * GMLP_G
        dbs = [None] * GMLP_G
        for c in range(tm // CHUNK):
            cs = slice(c * CHUNK, (c + 1) * CHUNK)
            cols = []
            for g in range(GMLP_G):
                gs = slice(g * GMLP_GD, (g + 1) * GMLP_GD)
                dmc = dmb[cs, gs]
                pw = _nt(dmc, vgn[cs, gs])
                pb = jnp.sum(dmixed[cs, gs], axis=1, keepdims=True)
                dws[g] = pw if dws[g] is None else dws[g] + pw
                dbs[g] = pb if dbs[g] is None else dbs[g] + pb
                cols.append(_tn(wms[g], dmc))
            dvgn_rows.append(jnp.concatenate(cols, axis=1))
        dvgn = jnp.concatenate(dvgn_rows, axis=0)
        dbs_t = jnp.concatenate(dbs, axis=1)
        for g in range(GMLP_G):
            dwg = jnp.where(tril, dws[g], 0.0)

            @pl.when(first)
            def _():
                dws_ref[g] = dwg

            @pl.when(jnp.logical_not(first))
            def _():
                dws_ref[g] += dwg

        @pl.when(first)
        def _():
            dbst_ref[...] = dbs_t

        @pl.when(jnp.logical_not(first))
        def _():
            dbst_ref[...] += dbs_t

        dvg, gsr = _norm_bwd(dvgn, vg, gs_ref[...], rv)
        _acc_rows(dgs_ref, first, gsr)
        dz_ref[:, Z_U:Z_U + GMLP_W] = (du * _gelu_grad(u_pre)).astype(BF)
        dz_ref[:, Z_G:Z_G + GMLP_W] = (dvg * _gelu_grad(vg_pre)).astype(BF)

    rev = lambda i: (n - 1 - i, 0)
    fix = lambda i: (0, 0)
    fix3 = lambda i: (0, 0, 0)
    return pl.pallas_call(
        body, name="mix_prep_bwd", grid=(n,),
        in_specs=[pl.BlockSpec((tm, ZW), rev), pl.BlockSpec((tm, AUG_W), rev), pl.BlockSpec((tm, AUG_W), rev),
                  pl.BlockSpec((tm, AUG_W), rev), pl.BlockSpec((tm, GMLP_W), rev), pl.BlockSpec((tm, LANES), rev),
                  pl.BlockSpec((1, LANES), fix), pl.BlockSpec((1, FOX_HD), fix), pl.BlockSpec((1, FOX_HD), fix),
                  pl.BlockSpec((1, GMLP_W), fix), pl.BlockSpec((GMLP_G, CHUNK, CHUNK), fix3),
                  pl.BlockSpec((CHUNK, GMLP_G), fix), pl.BlockSpec((1, GMLP_W), fix)],
        out_specs=[pl.BlockSpec((tm, ZW), rev), pl.BlockSpec((1, FOX_HD), fix), pl.BlockSpec((1, FOX_HD), fix),
                   pl.BlockSpec((1, GMLP_W), fix), pl.BlockSpec((1, GMLP_W), fix),
                   pl.BlockSpec((GMLP_G, CHUNK, CHUNK), fix3), pl.BlockSpec((CHUNK, GMLP_G), fix),
                   pl.BlockSpec((1, LANES), fix)],
        out_shape=[S((T, ZW), BF), S((1, FOX_HD), F32), S((1, FOX_HD), F32), S((1, GMLP_W), F32), S((1, GMLP_W), F32),
                   S((GMLP_G, CHUNK, CHUNK), F32), S((CHUNK, GMLP_G), F32), S((1, LANES), F32)],
        scratch_shapes=[pltpu.VMEM((1, LANES), F32)],
        compiler_params=_cp(1))(z, dq, dk, dv, dyg, rs, bf128, g_q, g_k, g_sgu, w_s, b_st, g_go)


def _mix_proj_bwd(dz, wz, x, g, dy):
    T, D = x.shape
    tm = _tile(T, 512)

    def body(dz_ref, w_ref, x_ref, g_ref, dy_ref, dx_ref, dxb_ref, dg_ref):
        dh = _nn(dz_ref[...], w_ref[...])
        dx, dgr = _norm_bwd(dh, x_ref[...], g_ref[...])
        dx = dx + dy_ref[...]
        dx_ref[...] = dx
        dxb_ref[...] = dx.astype(BF)
        _acc_rows(dg_ref, pl.program_id(0) == 0, dgr)

    row = lambda i: (i, 0)
    fix = lambda i: (0, 0)
    return pl.pallas_call(
        body, name="mix_proj_bwd", grid=(T // tm,),
        in_specs=[pl.BlockSpec((tm, ZW), row), pl.BlockSpec((ZW, D), fix), pl.BlockSpec((tm, D), row),
                  pl.BlockSpec((1, D), fix), pl.BlockSpec((tm, D), row)],
        out_specs=[pl.BlockSpec((tm, D), row), pl.BlockSpec((tm, D), row), pl.BlockSpec((1, D), fix)],
        out_shape=[S((T, D), F32), S((T, D), BF), S((1, D), F32)],
        compiler_params=_cp(1))(dz, wz, x, g, dy)


def _ca_kv(mem, g_mem, wckv, g_ck):
    M, D = mem.shape

    def body(m_ref, g_ref, w_ref, gk_ref, mn_ref, kr_ref, kn_ref, v_ref):
        mf = m_ref[...]
        mn = (mf * _rstd(mf) * g_ref[...]).astype(BF)
        mn_ref[...] = mn
        for h in range(CA_HEADS):
            kr = _nn(mn, w_ref[h])
            kr_ref[h] = kr
            kn_ref[h] = (kr * _rstd(kr) * gk_ref[...]).astype(BF)
            v_ref[h] = _nn(mn, w_ref[CA_HEADS + h]).astype(BF)

    hd = (CA_HEADS, M, CA_HD)
    return pl.pallas_call(
        body, name="ca_kv", out_shape=[S((M, D), BF), S(hd, F32), S(hd, BF), S(hd, BF)],
        compiler_params=pltpu.CompilerParams(vmem_limit_bytes=VMEM_LIMIT))(mem, g_mem, wckv, g_ck)


def _ca_tile_fwd(xt, gca, wcq, gcq, kn_ref, v_ref):
    hb = (xt * _rstd(xt) * gca).astype(BF)
    qc = _nn(hb, wcq)
    qr, qn, ps = [], [], []
    for h in range(CA_HEADS):
        qh = qc[:, h * CA_HD:(h + 1) * CA_HD]
        qnh = (qh * _rstd(qh) * gcq * 0.0625).astype(BF)
        s = _nt(qnh, kn_ref[h])
        e = jnp.exp(s - jnp.max(s, axis=1, keepdims=True))
        ps.append(e / jnp.sum(e, axis=1, keepdims=True))
        qr.append(qh)
        qn.append(qnh)
    return hb, qr, qn, ps


def _ca_fwd(x, g_ca, wcq, g_cq, kn, vv, wco):
    T, D = x.shape
    M = kn.shape[1]
    tm = _tile(T, 1024)

    def body(x_ref, gca_ref, wcq_ref, gcq_ref, kn_ref, v_ref, wco_ref, o_ref, ob_sc):
        xt = x_ref[...]
        _, _, _, ps = _ca_tile_fwd(xt, gca_ref[...], wcq_ref[...], gcq_ref[...], kn_ref, v_ref)
        for h in range(CA_HEADS):
            ob_sc[:, h * CA_HD:(h + 1) * CA_HD] = _nn(ps[h].astype(BF), v_ref[h]).astype(BF)
        o_ref[...] = xt + _nn(ob_sc[...], wco_ref[...])

    row = lambda i: (i, 0)
    fix = lambda i: (0, 0)
    fix3 = lambda i: (0, 0, 0)
    return pl.pallas_call(
        body, name="ca_fwd", grid=(T // tm,),
        in_specs=[pl.BlockSpec((tm, D), row), pl.BlockSpec((1, D), fix), pl.BlockSpec((D, D), fix),
                  pl.BlockSpec((1, CA_HD), fix), pl.BlockSpec((CA_HEADS, M, CA_HD), fix3),
                  pl.BlockSpec((CA_HEADS, M, CA_HD), fix3), pl.BlockSpec((D, D), fix)],
        out_specs=pl.BlockSpec((tm, D), row), out_shape=S((T, D), F32),
        scratch_shapes=[pltpu.VMEM((tm, D), BF)],
        compiler_params=_cp(1))(x, g_ca, wcq, g_cq, kn, vv, wco)


def _ca_bwd(x, dy, g_ca, wcq, g_cq, kn, vv, wco):
    T, D = x.shape
    M = kn.shape[1]
    tm = _tile(T, 512)
    n = T // tm

    def body(x_ref, dy_ref, gca_ref, wcq_ref, gcq_ref, kn_ref, v_ref, wco_ref,
             dx_ref, dwq_ref, dwo_ref, dkn_ref, dv_ref, dgcq_ref, dgca_ref, aq_sc, ao_sc, ob_sc, dq_sc):
        i = pl.program_id(0)
        first = i == 0
        xt = x_ref[...]
        dyt = dy_ref[...]
        dyb = dyt.astype(BF)
        hb, qr, qn, ps = _ca_tile_fwd(xt, gca_ref[...], wcq_ref[...], gcq_ref[...], kn_ref, v_ref)
        do = _nt(dyb, wco_ref[...])
        gcq_rows = None
        for h in range(CA_HEADS):
            hs = slice(h * CA_HD, (h + 1) * CA_HD)
            p = ps[h]
            pb = p.astype(BF)
            ob_sc[:, hs] = _nn(pb, v_ref[h]).astype(BF)
            doh = do[:, hs].astype(BF)
            dp = _nt(doh, v_ref[h])
            ds = (p * (dp - jnp.sum(dp * p, axis=1, keepdims=True))).astype(BF)
            dvh = _tn(pb, doh)
            dkh = _tn(ds, qn[h])

            @pl.when(first)
            def _():
                dv_ref[h] = dvh
                dkn_ref[h] = dkh

            @pl.when(jnp.logical_not(first))
            def _():
                dv_ref[h] += dvh
                dkn_ref[h] += dkh

            dqn = _nn(ds, kn_ref[h]) * 0.0625
            dqh, gr = _norm_bwd(dqn, qr[h], gcq_ref[...])
            gcq_rows = gr if gcq_rows is None else gcq_rows + gr
            dq_sc[:, hs] = dqh.astype(BF)
        _acc_rows(dgcq_ref, first, gcq_rows)
        dqb = dq_sc[...]
        p_o = _tn(ob_sc[...], dyb)
        p_q = _tn(hb, dqb)

        @pl.when(first)
        def _():
            ao_sc[...] = p_o
            aq_sc[...] = p_q

        @pl.when(jnp.logical_not(first))
        def _():
            ao_sc[...] += p_o
            aq_sc[...] += p_q

        @pl.when(i == n - 1)
        def _():
            dwo_ref[...] = ao_sc[...].astype(BF)
            dwq_ref[...] = aq_sc[...].astype(BF)

        dh = _nt(dqb, wcq_ref[...])
        dx, gar = _norm_bwd(dh, xt, gca_ref[...])
        dx_ref[...] = dx + dyt
        _acc_rows(dgca_ref, first, gar)

    row = lambda i: (i, 0)
    fix = lambda i: (0, 0)
    fix3 = lambda i: (0, 0, 0)
    hd = (CA_HEADS, M, CA_HD)
    return pl.pallas_call(
        body, name="ca_bwd", grid=(n,),
        in_specs=[pl.BlockSpec((tm, D), row), pl.BlockSpec((tm, D), row), pl.BlockSpec((1, D), fix),
                  pl.BlockSpec((D, D), fix), pl.BlockSpec((1, CA_HD), fix), pl.BlockSpec(hd, fix3),
                  pl.BlockSpec(hd, fix3), pl.BlockSpec((D, D), fix)],
        out_specs=[pl.BlockSpec((tm, D), row), pl.BlockSpec((D, D), fix), pl.BlockSpec((D, D), fix),
                   pl.BlockSpec(hd, fix3), pl.BlockSpec(hd, fix3), pl.BlockSpec((1, CA_HD), fix),
                   pl.BlockSpec((1, D), fix)],
        out_shape=[S((T, D), F32), S((D, D), BF), S((D, D), BF), S(hd, F32), S(hd, F32), S((1, CA_HD), F32),
                   S((1, D), F32)],
        scratch_shapes=[pltpu.VMEM((D, D), F32), pltpu.VMEM((D, D), F32), pltpu.VMEM((tm, D), BF),
                        pltpu.VMEM((tm, D), BF)],
        compiler_params=_cp(1))(x, dy, g_ca, wcq, g_cq, kn, vv, wco)


def _ca_kv_bwd(mem, g_mem, mn, kraw, dkn, dvv, wckv, g_ck):
    M, D = mem.shape

    def body(m_ref, g_ref, mn_ref, kr_ref, dkn_ref, dv_ref, w_ref, gk_ref, dw_ref, dgk_ref, dgm_ref):
        mn = mn_ref[...]
        dmn = jnp.zeros((M, D), F32)
        gk_rows = None
        for h in range(CA_HEADS):
            dkr, gr = _norm_bwd(dkn_ref[h], kr_ref[h], gk_ref[...])
            gk_rows = gr if gk_rows is None else gk_rows + gr
            dkb = dkr.astype(BF)
            dvb = dv_ref[h].astype(BF)
            dw_ref[h] = _tn(mn, dkb).astype(BF)
            dw_ref[CA_HEADS + h] = _tn(mn, dvb).astype(BF)
            dmn = dmn + _nt(dkb, w_ref[h]) + _nt(dvb, w_ref[CA_HEADS + h])
        dgk_ref[...] = jnp.sum(gk_rows, axis=0, keepdims=True)
        mf = m_ref[...]
        dgm_ref[...] = jnp.sum(dmn * (mf * _rstd(mf)), axis=0, keepdims=True)

    return pl.pallas_call(
        body, name="ca_kv_bwd",
        out_shape=[S((2 * CA_HEADS, D, CA_HD), BF), S((1, CA_HD), F32), S((1, D), F32)],
        compiler_params=pltpu.CompilerParams(vmem_limit_bytes=VMEM_LIMIT))(mem, g_mem, mn, kraw, dkn, dvv, wckv, g_ck)


def _after(g, token):
    return g if token is None else g + token[0:1, 0:1]


def _local_step(x, mem, target, small, weights, emit):
    T, D = x.shape
    p = small
    bf128 = jnp.pad(p["b_f"], ((0, 0), (0, LANES - FOX_HEADS)))
    b_st = p["b_s"].T

    wup1 = weights("ffn1_up", x)["wup1"]
    a1, h1 = _ffn_up("ffn1_up", x, p["g_ffn1"], wup1)
    wdn1 = weights("ffn1_dn", h1)["wdn1"]
    x1 = _ffn_down("ffn1_down", a1, wdn1, x)
    wm = weights("mix", x1)
    z, h2, qf, ka, va, yg, rs = _mix_prep(x1, p["g_mix"], wm["wz"], bf128, p["g_q"], p["g_k"], p["g_sgu"], p["w_s"],
                                          b_st, p["g_gmlp_o"])
    attn, lse = _fox_fwd(qf, ka, va)
    x2 = _mix_out(attn, yg, p["g_fox_o"], wm["wout"], x1)
    wc = weights("ca", x2)
    mn, kraw, ckn, cvv = _ca_kv(mem, p["g_mem"], wc["wckv"], p["g_ck"])
    x3 = _ca_fwd(x2, p["g_ca"], wc["wcq"], p["g_cq"], ckn, cvv, wc["wco"])
    w2 = weights("ffn2", x3)
    a2, h4 = _ffn_up("ffn2_up", x3, p["g_ffn2"], w2["wup2"])
    dy4, dy4b, sq = _ffn_down_loss("ffn2_down", a2, w2["wdn2"], x3, target)

    gs = {}
    dgu2 = _ffn_bwd_act("ffn2_bwd_act", dy4b, h4, w2["wup2"], w2["wdn2"])
    tok = emit("ffn2", {"wup2": _ffn_dwup("ffn2", h4, dgu2), "wdn2": _ffn_dwdn("ffn2", a2, dy4b)})
    dx3, gs["g_ffn2"] = _ffn_dx("ffn2_dx", dgu2, w2["wup2"], x3, _after(p["g_ffn2"], tok), dy4)

    dx2, dwcq, dwco, dckn, dcvv, gs["g_cq"], gs["g_ca"] = _ca_bwd(
        x2, dx3, p["g_ca"], wc["wcq"], p["g_cq"], ckn, cvv, wc["wco"])
    dwckv, gs["g_ck"], gs["g_mem"] = _ca_kv_bwd(mem, p["g_mem"], mn, kraw, dckn, dcvv, wc["wckv"], p["g_ck"])

    qb, dob, dyg, dwout, gs["g_fox_o"] = _mix_out_bwd(dx2, attn, yg, p["g_fox_o"], wm["wout"], qf, lse)
    dq, dk, dv = _fox_bwd(qb, ka, va, dob)
    dz, gs["g_q"], gs["g_k"], gs["g_sgu"], gs["g_gmlp_o"], gs["w_s"], dbst, dbf = _mix_prep_bwd(
        z, dq, dk, dv, dyg, rs, bf128, p["g_q"], p["g_k"], p["g_sgu"], p["w_s"], b_st, p["g_gmlp_o"])
    gs["b_s"] = dbst.T
    gs["b_f"] = dbf[:, :FOX_HEADS]
    tok_ws = emit("w_s", {"w_s": gs["w_s"]})
    zb = ZW // 3
    dwz = _tn_matmul("mix_dwz", dz, pl.BlockSpec((T, zb), lambda j: (0, j)), h2,
                     S((ZW, D), BF), pl.BlockSpec((zb, D), lambda j: (j, 0)), 3)
    tok = emit("mid", {"wcq": dwcq, "wco": dwco, "wckv": dwckv, "wout": dwout, "wz": dwz})
    dx1, dx1b, gs["g_mix"] = _mix_proj_bwd(dz, wm["wz"], x1, _after(_after(p["g_mix"], tok), tok_ws), dx2)

    dgu1 = _ffn_bwd_act("ffn1_bwd_act", dx1b, h1, wup1, wdn1)
    tok = emit("ffn1_dn", {"wdn1": _ffn_dwdn("ffn1", a1, dx1b)})
    tok = emit("ffn1_up", {"wup1": _ffn_dwup("ffn1", h1, dgu1, after=tok)})
    dx0, gs["g_ffn1"] = _ffn_dx("ffn1_dx", dgu1, wup1, x, _after(p["g_ffn1"], tok), dx1)
    return sq, dx0, gs


MESH = pl.DeviceIdType.MESH
HBM_SPEC = pl.BlockSpec(memory_space=pltpu.HBM)
N_PEER = N_DEV - 1


def _place():
    return lax.axis_index("x"), lax.axis_index("y"), lax.axis_index("c")


def _slot(px, py, pc):
    return 4 * px + 2 * py + pc


SEM_SPEC = pl.BlockSpec(memory_space=pltpu.SEMAPHORE)
ANY_SPEC = pl.BlockSpec(memory_space=pl.ANY)
DATAFLOW = pltpu.SideEffectType.DATAFLOW_SIDE_EFFECTING


def _hbm(a):
    return pltpu.with_memory_space_constraint(a, pltpu.HBM)


def _peer(x, y, c, r):
    return (1 - x if r & 4 else x, 1 - y if r & 2 else y, 1 - c if r & 1 else c)


def _place_own(srcs, whole):
    my = _slot(*_place())
    lands = []
    for s in srcs:
        blk = s[None] if whole else lax.dynamic_slice_in_dim(s, my, 1, 0)
        shape = (N_DEV,) + s.shape if whole else s.shape
        lands.append(lax.dynamic_update_slice_in_dim(lax.empty(shape, s.dtype), blk, my, 0))
    return lands


ALL_PEERS = tuple(range(1, N_DEV))
NEAR_PEERS = (1, 2, 4, 6)
SAME_CORE = (2, 4, 6)


def _copy_start(name, srcs, lands, whole, peers=None):
    n = len(srcs)
    peers = peers or [ALL_PEERS] * n
    wh = list(whole) if isinstance(whole, (list, tuple)) else [whole] * n

    def body(*refs):
        src, land = refs[:n], refs[n:2 * n]
        send, recv = refs[2 * n:3 * n], refs[3 * n:4 * n]
        token = refs[6 * n]
        x, y, c = _place()
        my = _slot(x, y, c)
        for a in range(n):
            for r in peers[a]:
                p = _peer(x, y, c, r)
                pltpu.make_async_remote_copy(
                    src_ref=src[a] if wh[a] else src[a].at[_slot(*p)], dst_ref=land[a].at[my],
                    send_sem=send[a].at[r - 1], recv_sem=recv[a].at[r - 1], device_id=p, device_id_type=MESH).start()
        token[...] = jnp.zeros_like(token)

    out = pl.pallas_call(
        body, name=name,
        out_shape=([pltpu.SemaphoreType.DMA((N_PEER,))] * (2 * n)
                   + [pltpu.HBM(s.shape, s.dtype) for s in srcs] + [pltpu.HBM(s.shape, s.dtype) for s in lands]
                   + [S((8, LANES), F32)]),
        in_specs=[HBM_SPEC] * (2 * n),
        out_specs=[SEM_SPEC] * (2 * n) + [HBM_SPEC] * (2 * n) + [pl.BlockSpec(memory_space=pltpu.VMEM)],
        input_output_aliases={i: 2 * n + i for i in range(2 * n)},
        compiler_params=pltpu.CompilerParams(has_side_effects=DATAFLOW),
    )(*[_hbm(s) for s in srcs], *[_hbm(s) for s in lands])
    return out[:n], out[n:2 * n], out[2 * n:3 * n], out[3 * n:4 * n], out[4 * n]


def _copy_wait(name, srcs, lands, send, recv, after, whole, peers=None, with_srcs=False):
    n = len(srcs)
    peers = peers or [ALL_PEERS] * n
    wh = list(whole) if isinstance(whole, (list, tuple)) else [whole] * n

    def body(*refs):
        src, land = refs[:n], refs[n:2 * n]
        snd, rcv = refs[2 * n:3 * n], refs[3 * n:4 * n]
        x, y, c = _place()
        for a in range(n):
            for r in peers[a]:
                p = _peer(x, y, c, r)
                ps = _slot(*p)
                cp = pltpu.make_async_remote_copy(
                    src_ref=src[a] if wh[a] else src[a].at[ps], dst_ref=land[a].at[ps],
                    send_sem=snd[a].at[r - 1], recv_sem=rcv[a].at[r - 1], device_id=p, device_id_type=MESH)
                cp.wait_send()
                cp.wait_recv()

    out = pl.pallas_call(
        body, name=name,
        out_shape=[pltpu.HBM(s.shape, s.dtype) for s in srcs] + [pltpu.HBM(s.shape, s.dtype) for s in lands],
        in_specs=[HBM_SPEC] * (2 * n) + [SEM_SPEC] * (2 * n) + [ANY_SPEC],
        out_specs=[HBM_SPEC] * (2 * n),
        input_output_aliases={i: i for i in range(2 * n)},
        compiler_params=pltpu.CompilerParams(has_side_effects=DATAFLOW),
    )(*srcs, *lands, *send, *recv, after)
    return (out[:n], out[n:]) if with_srcs else out[n:]


def _forward_start(name, lands):
    n = len(lands)

    def body(*refs):
        land = refs[:n]
        send, recv = refs[n:2 * n], refs[2 * n:3 * n]
        token = refs[4 * n]
        x, y, c = _place()
        for a in range(n):
            for r in SAME_CORE:
                blk = land[a].at[_slot(*_peer(x, y, c, r))]
                pltpu.make_async_remote_copy(
                    src_ref=blk, dst_ref=blk, send_sem=send[a].at[r - 1], recv_sem=recv[a].at[r - 1],
                    device_id=(x, y, 1 - c), device_id_type=MESH).start()
        token[...] = jnp.zeros_like(token)

    out = pl.pallas_call(
        body, name=name,
        out_shape=([pltpu.SemaphoreType.DMA((N_PEER,))] * (2 * n) + [pltpu.HBM(s.shape, s.dtype) for s in lands]
                   + [S((8, LANES), F32)]),
        in_specs=[HBM_SPEC] * n,
        out_specs=[SEM_SPEC] * (2 * n) + [HBM_SPEC] * n + [pl.BlockSpec(memory_space=pltpu.VMEM)],
        input_output_aliases={i: 2 * n + i for i in range(n)},
        compiler_params=pltpu.CompilerParams(has_side_effects=DATAFLOW),
    )(*[_hbm(s) for s in lands])
    return out[:n], out[n:2 * n], out[2 * n:3 * n], out[3 * n]


def _forward_wait(name, lands, send, recv, after):
    n = len(lands)

    def body(*refs):
        land = refs[:n]
        snd, rcv = refs[n:2 * n], refs[2 * n:3 * n]
        x, y, c = _place()
        for a in range(n):
            for r in SAME_CORE:
                cp = pltpu.make_async_remote_copy(
                    src_ref=land[a].at[_slot(*_peer(x, y, c, r))], dst_ref=land[a].at[_slot(*_peer(x, y, c, r | 1))],
                    send_sem=snd[a].at[r - 1], recv_sem=rcv[a].at[r - 1], device_id=(x, y, 1 - c),
                    device_id_type=MESH)
                cp.wait_send()
                cp.wait_recv()

    return pl.pallas_call(
        body, name=name,
        out_shape=[pltpu.HBM(s.shape, s.dtype) for s in lands],
        in_specs=[HBM_SPEC] * n + [SEM_SPEC] * (2 * n) + [ANY_SPEC],
        out_specs=[HBM_SPEC] * n,
        input_output_aliases={i: i for i in range(n)},
        compiler_params=pltpu.CompilerParams(has_side_effects=DATAFLOW),
    )(*lands, *send, *recv, after)


def _adamw(w, g, m, v):
    m2 = ADAM_B1 * m + (1.0 - ADAM_B1) * g
    v2 = ADAM_B2 * v + (1.0 - ADAM_B2) * (g * g)
    m_hat = m2 / (1.0 - ADAM_B1 ** ADAM_STEP)
    v_hat = v2 / (1.0 - ADAM_B2 ** ADAM_STEP)
    delta = -ADAM_LR * (m_hat / (jnp.sqrt(v_hat) + ADAM_EPS) + ADAM_WD * w)
    return delta, m2, v2


def _adamw_big(name, slots, w, m, v, own=None):
    R, C = w.shape
    tr = next((t for t in (256, 352) if R % t == 0), R)

    def finish(g, w_ref, m_ref, v_ref, g_ref, d_ref, m2_ref, v2_ref):
        d, m2, v2 = _adamw(w_ref[...], g, m_ref[...], v_ref[...])
        g_ref[...] = g
        d_ref[...] = d
        m2_ref[...] = m2
        v2_ref[...] = v2

    if own is None:
        def body(s_ref, *refs):
            g = s_ref[0].astype(F32)
            for k in range(1, N_DEV):
                g = g + s_ref[k].astype(F32)
            finish(g, *refs)

        row = pl.BlockSpec((tr, C), lambda i: (i, 0))
        return pl.pallas_call(
            body, name=name, grid=(R // tr,),
            in_specs=[pl.BlockSpec((N_DEV, tr, C), lambda i: (0, i, 0)), row, row, row],
            out_specs=[row] * 4, out_shape=[S((R, C), F32)] * 4,
            compiler_params=_cp(1))(slots, w, m, v)

    def body(my_ref, s_ref, own_ref, *refs):
        mine = own_ref[...]
        g = None
        for k in range(N_DEV):
            part = jnp.where(my_ref[0] == k, mine, s_ref[k]).astype(F32)
            g = part if g is None else g + part
        finish(g, *refs)

    row = pl.BlockSpec((tr, C), lambda i, my_ref: (i, 0))
    my = jnp.reshape(_slot(*_place()), (1,)).astype(jnp.int32)
    return pl.pallas_call(
        body, name=name,
        grid_spec=pltpu.PrefetchScalarGridSpec(
            num_scalar_prefetch=1, grid=(R // tr,),
            in_specs=[pl.BlockSpec((N_DEV, tr, C), lambda i, my_ref: (0, i, 0)),
                      pl.BlockSpec((None, tr, C), lambda i, my_ref: (my_ref[0], i, 0)), row, row, row],
            out_specs=[row] * 4),
        out_shape=[S((R, C), F32)] * 4, compiler_params=_cp(1))(my, slots, own, w, m, v)


TINY_ROWS = (("b_s", 8), ("g_ffn1", 8), ("g_mix", 8), ("g_ca", 8), ("g_mem", 8), ("g_ffn2", 8), ("g_sgu", 4),
             ("g_fox_o", 4), ("g_gmlp_o", 4), ("g_cq", 2), ("g_ck", 2), ("g_q", 1), ("g_k", 1), ("b_f", 1),
             ("loss", 1))
TINY_P = 72


def _tiny_pieces(width):
    return [(j, slice(j * LANES, min((j + 1) * LANES, width))) for j in range(-(-width // LANES))]


def _pack_tiny(grads, sq):
    names = [n for n, _ in TINY_ROWS if n != "loss"]

    def body(*refs):
        ins, sq_ref, o_ref = refs[:len(names)], refs[len(names)], refs[len(names) + 1]
        o_ref[...] = jnp.zeros_like(o_ref)
        at = 0
        for ref, (name, r) in zip(ins, TINY_ROWS):
            if name == "b_s":
                o_ref[at:at + r, :] = ref[...]
            else:
                for j, cols in _tiny_pieces(ref.shape[1]):
                    o_ref[at + j:at + j + 1, 0:cols.stop - cols.start] = ref[:, cols]
            at += r
        o_ref[at:at + 1, :] = sq_ref[0:1, :]

    return pl.pallas_call(body, name="tiny_pack", out_shape=S((TINY_P, LANES), F32))(
        *[grads[n] for n in names], sq)


def _adamw_tiny(slots, w, m, v):
    names = [n for n, _ in TINY_ROWS if n != "loss"]
    k = len(names)

    def body(s_ref, *refs):
        ins, outs, loss_ref = refs[:3 * k], refs[3 * k:7 * k], refs[7 * k]
        g_all = s_ref[0]
        for d in range(1, N_DEV):
            g_all = g_all + s_ref[d]
        at = 0
        for i, (name, r) in enumerate(TINY_ROWS[:k]):
            w_ref, m_ref, v_ref = ins[i], ins[k + i], ins[2 * k + i]
            o = outs[4 * i:4 * i + 4]
            if name == "b_s":
                pieces = [(slice(at, at + r), slice(0, LANES), (slice(None), slice(None)))]
            else:
                pieces = [(slice(at + j, at + j + 1), slice(0, c.stop - c.start), (slice(None), c))
                          for j, c in _tiny_pieces(w_ref.shape[1])]
            for rows, lanes, dst in pieces:
                g = g_all[rows, lanes]
                res = (g,) + _adamw(w_ref[dst], g, m_ref[dst], v_ref[dst])
                for ref, val in zip(o, res):
                    ref[dst] = val
            at += r
        loss_ref[...] = g_all[at:at + 1, :]

    shapes = [S(w[n].shape, F32) for n in names]
    out = pl.pallas_call(
        body, name="adamw_tiny", out_shape=[s for s in shapes for _ in range(4)] + [S((1, LANES), F32)],
    )(slots, *[w[n] for n in names], *[m[n] for n in names], *[v[n] for n in names])
    stores = ({}, {}, {}, {})
    for i, n in enumerate(names):
        for store, t in zip(stores, out[4 * i:4 * i + 4]):
            store[n] = t
    return stores, out[4 * k]


WEIGHTS =('g_ffn1', 'w_ffn1_in', 'w_ffn1_out', 'g_mix', 'w_in', 'b_f', 'g_q', 'g_k', 'g_sgu', 'w_s', 'b_s',
           'g_fox_o', 'g_gmlp_o', 'w_out', 'g_ca', 'g_mem', 'w_cq', 'w_ckv', 'g_cq', 'g_ck', 'w_co', 'g_ffn2',
           'w_ffn2_in', 'w_ffn2_out')
BIG = ('w_ffn1_in', 'w_ffn1_out', 'w_in', 'w_out', 'w_cq', 'w_ckv', 'w_co', 'w_ffn2_in', 'w_ffn2_out')
TRANSPOSED = ('w_ffn1_in', 'w_in', 'w_ffn2_in')
TWO_LEVEL = ('w_ffn1_in', 'w_in')
GATHER_GROUPS = {"ffn1_up": ("w_ffn1_in",), "ffn1_dn": ("w_ffn1_out",), "mix": ("w_in", "w_out"),
                 "ca": ("w_cq", "w_ckv", "w_co"), "ffn2": ("w_ffn2_in", "w_ffn2_out")}
QKV_W = 3 * FOX_W
UV_OFF = QKV_W + FOX_HEADS


def kernel(x, mem, g_ffn1, w_ffn1_in, w_ffn1_out, g_mix, w_in, b_f, g_q, g_k, g_sgu, w_s, b_s, g_fox_o, g_gmlp_o, w_out, g_ca, g_mem, w_cq, w_ckv, g_cq, g_ck, w_co, g_ffn2, w_ffn2_in, w_ffn2_out, loss_target, m_g_ffn1, m_w_ffn1_in, m_w_ffn1_out, m_g_mix, m_w_in, m_b_f, m_g_q, m_g_k, m_g_sgu, m_w_s, m_b_s, m_g_fox_o, m_g_gmlp_o, m_w_out, m_g_ca, m_g_mem, m_w_cq, m_w_ckv, m_g_cq, m_g_ck, m_w_co, m_g_ffn2, m_w_ffn2_in, m_w_ffn2_out, v_g_ffn1, v_w_ffn1_in, v_w_ffn1_out, v_g_mix, v_w_in, v_b_f, v_g_q, v_g_k, v_g_sgu, v_w_s, v_b_s, v_g_fox_o, v_g_gmlp_o, v_w_out, v_g_ca, v_g_mem, v_w_cq, v_w_ckv, v_g_cq, v_g_ck, v_w_co, v_g_ffn2, v_w_ffn2_in, v_w_ffn2_out):
    args = dict(locals())
    w = {n: args[n] for n in WEIGHTS}
    mo = {n: args["m_" + n] for n in WEIGHTS}
    vo = {n: args["v_" + n] for n in WEIGHTS}
    D = D_MODEL

    def local(n, a):
        return a[0].T if n in TRANSPOSED else a[0]

    g_peers = [NEAR_PEERS if n in TWO_LEVEL else ALL_PEERS for n in BIG]
    handles = {}

    def start_gather(name, names, arrays):
        snd, rcv, src, land, token = _copy_start(name, arrays, _place_own(arrays, True), True,
                                                 peers=[g_peers[BIG.index(n)] for n in names])
        handles.update({n: (src[i], land[i], snd[i], rcv[i]) for i, n in enumerate(names)})
        return token

    first = local(BIG[0], w[BIG[0]]).astype(BF)
    fb = first.shape[0]
    token_first = start_gather("gather_start_first", BIG[:1], [first])
    token_rest = start_gather("gather_start_rest", BIG[1:],
                              [(local(n, w[n]) + token_first[0:1, 0:1]).astype(BF) for n in BIG[1:]])

    tiny_names = [n for n, _ in TINY_ROWS if n != "loss"]

    def weights(group, after):
        names = GATHER_GROUPS[group]
        hs = [handles[n] for n in names]
        got = list(_copy_wait("gather_wait_" + group, [h[0] for h in hs], [h[1] for h in hs], [h[2] for h in hs],
                              [h[3] for h in hs], token_rest if group == "ffn1_up" else after, True,
                              peers=[g_peers[BIG.index(n)] for n in names]))
        passed = [i for i, n in enumerate(names) if n in TWO_LEVEL]
        if passed:
            f_snd, f_rcv, f_land, f_token = _forward_start("gather_pass_start_" + group, [got[i] for i in passed])
            for i, t in zip(passed, _forward_wait("gather_pass_wait_" + group, f_land, f_snd, f_rcv, f_token)):
                got[i] = t
        got = dict(zip(names, got))
        if group == "ffn1_up":
            return {"wup1": got["w_ffn1_in"].reshape(2, N_FFN_BLK, fb, D)}
        if group == "ffn1_dn":
            return {"wdn1": got["w_ffn1_out"].reshape(N_FFN_BLK, fb, D)}
        if group == "mix":
            full = got["w_in"].reshape(-1, D)
            wz = jnp.concatenate([full[:QKV_W], full[UV_OFF:], full[QKV_W:UV_OFF],
                                  jnp.zeros((LANES - FOX_HEADS, D), BF)], axis=0)
            return {"wz": wz, "wout": got["w_out"].reshape(D, D)}
        if group == "ca":
            return {"wcq": got["w_cq"].reshape(D, D), "wco": got["w_co"].reshape(D, D), "wckv": got["w_ckv"]}
        return {"wup2": got["w_ffn2_in"].reshape(2, N_FFN_BLK, fb, D),
                "wdn2": got["w_ffn2_out"].reshape(N_FFN_BLK, fb, D)}

    flying = {}

    def emit(group, g):
        if group == "w_s":
            flying[group] = g["w_s"].reshape(-1, LANES)
            return None
        if group == "ffn2":
            parts = {"w_ffn2_in": g["wup2"], "w_ffn2_out": g["wdn2"].reshape(N_DEV, -1, D)}
        elif group == "ffn1_dn":
            parts = {"w_ffn1_out": g["wdn1"].reshape(N_DEV, -1, D)}
        elif group == "ffn1_up":
            parts = {"w_ffn1_in": g["wup1"]}
        else:
            gz = g["wz"]
            g_in = jnp.concatenate([gz[:QKV_W], gz[Z_F:Z_F + FOX_HEADS], gz[QKV_W:Z_F]], axis=0)
            parts = {"w_in": g_in.reshape(N_DEV, -1, D).astype(BF),
                     "w_out": g["wout"].reshape(N_DEV, -1, D), "w_cq": g["wcq"].reshape(N_DEV, -1, D),
                     "w_co": g["wco"].reshape(N_DEV, -1, D), "w_ckv": g["wckv"]}
        names = list(parts)
        srcs = [parts[n] for n in names]
        lands = [lax.empty(s.shape, s.dtype) for s in srcs]
        whole = [False] * len(srcs)
        if group == "mid":
            ws_part = flying.pop("w_s")
            names, srcs, whole = names + ["w_s"], srcs + [ws_part], whole + [True]
            lands += _place_own([ws_part], True)
        *copies, token = _copy_start("exchange_start_" + group, srcs, lands, whole)
        flying[group] = (names, copies, whole)
        return token

    small = {n: (w[n][0] if n == "b_s" else w[n]) for n in tiny_names}
    small["w_s"] = w["w_s"][0]

    sq, dx0, gs = _local_step(x[0], mem[0], loss_target[0], small, weights, emit)

    sm_parts = [_pack_tiny(gs, sq)]
    sm_snd, sm_rcv, sm_src, sm_land, sm_token = _copy_start("tiny_start", sm_parts, _place_own(sm_parts, True), True)

    grad, delta, new_m, new_v = {}, {}, {}, {}

    def update(group, after):
        names, (snd, rcv, srcs, lands), whole = flying[group]
        owns, slots = _copy_wait("exchange_wait_" + group, srcs, lands, snd, rcv, after, whole, with_srcs=True)
        for n, sl, own in zip(names, slots, owns):
            if n == "w_s":
                g, d, m2, v2 = _adamw_big("adamw_w_s", sl, *[a[n].reshape(-1, LANES) for a in (w, mo, vo)])
            else:
                g, d, m2, v2 = _adamw_big("adamw_" + n, sl, local(n, w[n]), local(n, mo[n]), local(n, vo[n]),
                                          own=own)
            grad[n], delta[n], new_m[n], new_v[n] = (
                (t.T if n in TRANSPOSED else t).reshape(w[n].shape) for t in (g, d, m2, v2))
        return d

    last = update("ffn2", sm_token)
    last = update("mid", last)
    last = update("ffn1_dn", last)
    last = update("ffn1_up", last)
    tiny_all, = _copy_wait("tiny_wait", sm_src, sm_land, sm_snd, sm_rcv, last, True)
    stores, loss_row = _adamw_tiny(tiny_all, *[{n: (a[n][0] if n == "b_s" else a[n]) for n in tiny_names}
                                               for a in (w, mo, vo)])
    for store, t in zip((grad, delta, new_m, new_v), stores):
        store.update({n: v.reshape(w[n].shape) for n, v in t.items()})
    loss = loss_row[0, 0] * (0.5 / D)

    return (loss, dx0[None], *[grad[n] for n in WEIGHTS], *[delta[n] for n in WEIGHTS],
            *[new_m[n] for n in WEIGHTS], *[new_v[n] for n in WEIGHTS])
```

```python
import functools

import jax
import jax.numpy as jnp
from jax import lax
from jax.experimental import pallas as pl
from jax.experimental.pallas import tpu as pltpu

F32 = jnp.float32
BF = jnp.bfloat16
S = jax.ShapeDtypeStruct

N_DEV = 8
D_MODEL = 1024
FOX_HEADS, FOX_HD = 8, 64
FOX_W = 512
GMLP_G, GMLP_GD = 8, 64
GMLP_W = 512
CHUNK = 128
CA_HEADS, CA_HD = 4, 256
N_FFN_BLK = 4
ZW = 2688
Z_Q, Z_K, Z_V, Z_U, Z_G, Z_F = 0, 512, 1024, 1536, 2048, 2560
EPS = 1e-6
NEG = -1e30
LANES = 128

ADAM_LR, ADAM_B1, ADAM_B2, ADAM_EPS, ADAM_WD, ADAM_STEP = 0.001, 0.9, 0.999, 1e-08, 0.01, 10

VMEM_LIMIT = 52 * 2 ** 20


def _cp(n_axes):
    return pltpu.CompilerParams(dimension_semantics=("arbitrary",) * n_axes, vmem_limit_bytes=VMEM_LIMIT)


def _nn(a, b):
    return jnp.dot(a, b, preferred_element_type=F32)


def _nt(a, b):
    return lax.dot_general(a, b, (((1,), (1,)), ((), ())), preferred_element_type=F32)


def _tn(a, b):
    return lax.dot_general(a, b, (((0,), (0,)), ((), ())), preferred_element_type=F32)


def _hi(mask, x):
    return jnp.dot(mask.astype(F32), x, precision=lax.Precision.HIGHEST, preferred_element_type=F32)


def _hi3(mask, x):
    mb = mask.astype(BF)
    hi = x.astype(BF)
    r1 = x - hi.astype(F32)
    mid = r1.astype(BF)
    lo = (r1 - mid.astype(F32)).astype(BF)
    return _nn(mb, hi) + _nn(mb, mid) + _nn(mb, lo)


def _rstd(x):
    return lax.rsqrt(jnp.mean(x * x, axis=-1, keepdims=True) + EPS)


def _norm_bwd(dy, x, g, r=None):
    r = _rstd(x) if r is None else r
    xh = x * r
    dxh = dy * g
    dx = r * (dxh - xh * jnp.mean(dxh * xh, axis=-1, keepdims=True))
    return dx, dy * xh


def _acc_rows(ref, first, val):
    srow = jnp.sum(val, axis=0, keepdims=True)

    @pl.when(first)
    def _():
        ref[...] = srow

    @pl.when(jnp.logical_not(first))
    def _():
        ref[...] += srow


def _gelu(x):
    c = 0.7978845608028654
    return 0.5 * x * (1.0 + jnp.tanh(c * (x + 0.044715 * x * x * x)))


def _gelu_grad(x):
    c = 0.7978845608028654
    t = jnp.tanh(c * (x + 0.044715 * x * x * x))
    return 0.5 * (1.0 + t) + 0.5 * x * (1.0 - t * t) * c * (1.0 + 3 * 0.044715 * x * x)


def _tile(n, pref):
    return pref if n % pref == 0 else n


def _ffn_up(name, x, g, wup):
    T, D = x.shape
    FB = wup.shape[-2]
    tm = _tile(T, 1024)

    def body(x_ref, g_ref, w_ref, a_ref, h_ref):
        @pl.when(pl.program_id(1) == 0)
        def _():
            xf = x_ref[...]
            h_ref[...] = (xf * _rstd(xf) * g_ref[...]).astype(BF)

        hb = h_ref[...]
        gg = _nt(hb, w_ref[0])
        uu = _nt(hb, w_ref[1])
        a_ref[...] = (gg * jax.nn.sigmoid(gg) * uu).astype(BF)

    return pl.pallas_call(
        body, name=name, grid=(T // tm, N_FFN_BLK),
        in_specs=[pl.BlockSpec((tm, D), lambda i, j: (i, 0)),
                  pl.BlockSpec((1, D), lambda i, j: (0, 0)),
                  pl.BlockSpec((2, None, FB, D), lambda i, j: (0, j, 0, 0))],
        out_specs=[pl.BlockSpec((None, tm, FB), lambda i, j: (j, i, 0)),
                   pl.BlockSpec((tm, D), lambda i, j: (i, 0))],
        out_shape=[S((N_FFN_BLK, T, FB), BF), S((T, D), BF)],
        compiler_params=_cp(2))(x, g, wup)


def _ffn_down(name, a, wdn, x):
    _, T, FB = a.shape
    D = x.shape[1]
    tm = _tile(T, 512)

    def body(a_ref, w_ref, x_ref, o_ref):
        p = _nn(a_ref[0], w_ref[0])
        for j in range(1, N_FFN_BLK):
            p = p + _nn(a_ref[j], w_ref[j])
        o_ref[...] = x_ref[...] + 0.5 * p

    return pl.pallas_call(
        body, name=name, grid=(T // tm,),
        in_specs=[pl.BlockSpec((N_FFN_BLK, tm, FB), lambda i: (0, i, 0)),
                  pl.BlockSpec((N_FFN_BLK, FB, D), lambda i: (0, 0, 0)),
                  pl.BlockSpec((tm, D), lambda i: (i, 0))],
        out_specs=pl.BlockSpec((tm, D), lambda i: (i, 0)),
        out_shape=S((T, D), F32),
        compiler_params=_cp(1))(a, wdn, x)


def _ffn_down_loss(name, a, wdn, x, target):
    _, T, FB = a.shape
    D = x.shape[1]
    tm = _tile(T, 512)

    def body(a_ref, w_ref, x_ref, t_ref, d_ref, db_ref, loss_ref):
        i = pl.program_id(0)
        p = _nn(a_ref[0], w_ref[0])
        for j in range(1, N_FFN_BLK):
            p = p + _nn(a_ref[j], w_ref[j])
        diff = (x_ref[...] + 0.5 * p) - t_ref[...]
        dy = diff * (1.0 / D)
        d_ref[...] = dy
        db_ref[...] = dy.astype(BF)
        sq = jnp.zeros((8, LANES), F32) + jnp.sum(diff * diff)

        @pl.when(i == 0)
        def _():
            loss_ref[...] = sq

        @pl.when(i > 0)
        def _():
            loss_ref[...] += sq

    row = pl.BlockSpec((tm, D), lambda i: (i, 0))
    return pl.pallas_call(
        body, name=name, grid=(T // tm,),
        in_specs=[pl.BlockSpec((N_FFN_BLK, tm, FB), lambda i: (0, i, 0)),
                  pl.BlockSpec((N_FFN_BLK, FB, D), lambda i: (0, 0, 0)), row, row],
        out_specs=[row, row, pl.BlockSpec((8, LANES), lambda i: (0, 0))],
        out_shape=[S((T, D), F32), S((T, D), BF), S((8, LANES), F32)],
        compiler_params=_cp(1))(a, wdn, x, target)


def _ffn_bwd_act(name, dyb, h, wup, wdn):
    T, D = h.shape
    FB = wup.shape[-2]
    tm = _tile(T, 1024)

    def body(d_ref, h_ref, wu_ref, wd_ref, o_ref):
        da = 0.5 * _nt(d_ref[...], wd_ref[...])
        hb = h_ref[...]
        gg = _nt(hb, wu_ref[0])
        uu = _nt(hb, wu_ref[1])
        sg = jax.nn.sigmoid(gg)
        o_ref[0] = (da * uu * (sg * (1.0 + gg * (1.0 - sg)))).astype(BF)
        o_ref[1] = (da * (gg * sg)).astype(BF)

    return pl.pallas_call(
        body, name=name, grid=(T // tm, N_FFN_BLK),
        in_specs=[pl.BlockSpec((tm, D), lambda i, j: (i, 0)),
                  pl.BlockSpec((tm, D), lambda i, j: (i, 0)),
                  pl.BlockSpec((2, None, FB, D), lambda i, j: (0, j, 0, 0)),
                  pl.BlockSpec((None, FB, D), lambda i, j: (j, 0, 0))],
        out_specs=pl.BlockSpec((2, None, tm, FB), lambda i, j: (0, j, i, 0)),
        out_shape=S((2, N_FFN_BLK, T, FB), BF),
        compiler_params=_cp(2))(dyb, h, wup, wdn)


def _ffn_dx(name, dgu, wup, x, g, dy):
    T, D = x.shape
    FB = wup.shape[-2]
    tm = _tile(T, 512)

    def body(d_ref, w_ref, x_ref, g_ref, dy_ref, dx_ref, dg_ref):
        p = None
        for j in range(N_FFN_BLK):
            for half in range(2):
                t = _nn(d_ref[half, j], w_ref[half, j])
                p = t if p is None else p + t
        dx, dgr = _norm_bwd(p, x_ref[...], g_ref[...])
        dx_ref[...] = dx + dy_ref[...]
        _acc_rows(dg_ref, pl.program_id(0) == 0, dgr)

    return pl.pallas_call(
        body, name=name, grid=(T // tm,),
        in_specs=[pl.BlockSpec((2, N_FFN_BLK, tm, FB), lambda i: (0, 0, i, 0)),
                  pl.BlockSpec((2, N_FFN_BLK, FB, D), lambda i: (0, 0, 0, 0), pipeline_mode=pl.Buffered(1)),
                  pl.BlockSpec((tm, D), lambda i: (i, 0)),
                  pl.BlockSpec((1, D), lambda i: (0, 0)),
                  pl.BlockSpec((tm, D), lambda i: (i, 0))],
        out_specs=[pl.BlockSpec((tm, D), lambda i: (i, 0)),
                   pl.BlockSpec((1, D), lambda i: (0, 0))],
        out_shape=[S((T, D), F32), S((1, D), F32)],
        compiler_params=_cp(1))(dgu, wup, x, g, dy)


def _tn_matmul(name, a, a_spec, b, out_shape, out_spec, n_blocks, scale=1.0, after=None):
    extra = [] if after is None else [after]

    def body(a_ref, b_ref, *rest):
        o_ref = rest[-1]
        o_ref[...] = (_tn(a_ref[...], b_ref[...]) * scale).astype(o_ref.dtype)

    return pl.pallas_call(
        body, name=name, grid=(n_blocks,),
        in_specs=[a_spec, pl.BlockSpec(b.shape, lambda j: (0, 0), pipeline_mode=pl.Buffered(1))]
        + [pl.BlockSpec((8, LANES), lambda j: (0, 0)) for _ in extra],
        out_specs=out_spec, out_shape=out_shape, compiler_params=_cp(1))(a, b, *extra)


def _ffn_dwup(name, h, dgu, after=None):
    T, D = h.shape
    FB = dgu.shape[-1]
    return _tn_matmul(
        name + "_dwup", dgu.reshape(2 * N_FFN_BLK, T, FB), pl.BlockSpec((None, T, FB), lambda j: (j, 0, 0)), h,
        S((2 * N_FFN_BLK, FB, D), BF), pl.BlockSpec((None, FB, D), lambda j: (j, 0, 0)), 2 * N_FFN_BLK,
        after=after)


def _ffn_dwdn(name, a, dyb):
    _, T, FB = a.shape
    D = dyb.shape[1]
    return _tn_matmul(
        name + "_dwdn", a, pl.BlockSpec((None, T, FB), lambda j: (j, 0, 0)), dyb,
        S((N_FFN_BLK, FB, D), BF), pl.BlockSpec((None, FB, D), lambda j: (j, 0, 0)), N_FFN_BLK, scale=0.5)


def _tri(n, lower):
    r = lax.broadcasted_iota(jnp.int32, (n, n), 0)
    c = lax.broadcasted_iota(jnp.int32, (n, n), 1)
    return (r >= c) if lower else (r <= c)


def _spatial_mix(vgn_b, ws_ref, bst, tm):
    tril = _tri(CHUNK, True)
    wms = [jnp.where(tril, ws_ref[g], 0.0).astype(BF) for g in range(GMLP_G)]
    rows = []
    for c in range(tm // CHUNK):
        cols = []
        for g in range(GMLP_G):
            vs = vgn_b[c * CHUNK:(c + 1) * CHUNK, g * GMLP_GD:(g + 1) * GMLP_GD]
            cols.append(_nn(wms[g], vs) + bst[:, g:g + 1])
        rows.append(jnp.concatenate(cols, axis=1))
    return jnp.concatenate(rows, axis=0), wms


HB = 128
AUG_W = FOX_HEADS * HB
COL_A, COL_B, COL_C = 64, 67, 70
RS_Q, RS_K, RS_V, RS_O = 0, 8, 16, 17


def _piece_matrix(col):
    r = jnp.arange(LANES)
    dst = jnp.where(r < 3 * FOX_HEADS, (r % FOX_HEADS) * HB + col + r // FOX_HEADS, -1)
    return (jnp.arange(AUG_W)[None, :] == dst[:, None]).astype(BF)


def _ones_row(cols):
    c = jnp.arange(AUG_W) % HB
    hit = functools.reduce(jnp.logical_or, [(c >= a) & (c < a + 3) for a in cols])
    return hit.astype(F32)[None, :]


def _pieces(x):
    lane = lax.broadcasted_iota(jnp.int32, x.shape, 1)
    x = jnp.where(lane < FOX_HEADS, x, 0.0)
    hi = x.astype(BF).astype(F32)
    r1 = x - hi
    mid = r1.astype(BF).astype(F32)
    lo = (r1 - mid).astype(BF).astype(F32)
    return (hi + pltpu.roll(mid, FOX_HEADS, 1) + pltpu.roll(lo, 2 * FOX_HEADS, 1)).astype(BF)


def _mix_prep(x, g_mix, wz, bf128, g_q, g_k, g_sgu, w_s, b_st, g_go):
    T, D = x.shape
    tm = _tile(T, 512)
    pc_q, pc_k = _piece_matrix(COL_A), _piece_matrix(COL_B)
    one_q, one_k, one_v = _ones_row([COL_B]), _ones_row([COL_A, COL_C]), _ones_row([COL_A])

    def body(x_ref, gm_ref, wz_ref, bf_ref, gq_ref, gk_ref, gs_ref, ws_ref, bst_ref, go_ref, pq_ref, pk_ref, oq_ref,
             ok_ref, ov_ref, z_ref, h_ref, q_ref, k_ref, v_ref, y_ref, rs_ref, carry_ref):
        i = pl.program_id(0)

        @pl.when(i == 0)
        def _():
            carry_ref[...] = jnp.zeros_like(carry_ref)

        xf = x_ref[...]
        hb = (xf * _rstd(xf) * gm_ref[...]).astype(BF)
        h_ref[...] = hb
        z_ref[...] = _nt(hb, wz_ref[...])

        fl = z_ref[:, Z_F:Z_F + LANES] + bf_ref[...]
        logf = jnp.minimum(fl, 0.0) - jnp.log1p(jnp.exp(-jnp.abs(fl)))
        csum = _hi(_tri(tm, True), logf) + carry_ref[...]
        carry_ref[...] = csum[tm - 1:tm, :]
        ext_q = (_nn(_pieces(csum), pq_ref[...]) + oq_ref[...]).astype(BF)
        ext_k = (_nn(_pieces(-csum), pk_ref[...]) + ok_ref[...]).astype(BF)
        ext_v = jnp.broadcast_to(ov_ref[...], (tm, AUG_W)).astype(BF)

        rs_ref[...] = jnp.zeros_like(rs_ref)
        for h in range(FOX_HEADS):
            lo, hi = slice(h * HB, h * HB + FOX_HD), slice(h * HB + FOX_HD, (h + 1) * HB)
            qh = z_ref[:, Z_Q + h * FOX_HD:Z_Q + (h + 1) * FOX_HD]
            kh = z_ref[:, Z_K + h * FOX_HD:Z_K + (h + 1) * FOX_HD]
            rq, rk = _rstd(qh), _rstd(kh)
            rs_ref[:, RS_Q + h:RS_Q + h + 1] = rq
            rs_ref[:, RS_K + h:RS_K + h + 1] = rk
            q_ref[:, lo] = (qh * rq * gq_ref[...] * 0.125).astype(BF)
            k_ref[:, lo] = (kh * rk * gk_ref[...]).astype(BF)
            v_ref[:, lo] = z_ref[:, Z_V + h * FOX_HD:Z_V + (h + 1) * FOX_HD].astype(BF)
            q_ref[:, hi] = ext_q[:, hi]
            k_ref[:, hi] = ext_k[:, hi]
            v_ref[:, hi] = ext_v[:, hi]

        u = _gelu(z_ref[:, Z_U:Z_U + GMLP_W])
        vg = _gelu(z_ref[:, Z_G:Z_G + GMLP_W])
        rv = _rstd(vg)
        vgn = (vg * rv * gs_ref[...]).astype(BF)
        mixed, _ = _spatial_mix(vgn, ws_ref, bst_ref[...], tm)
        sgu = u * mixed
        ro = _rstd(sgu)
        y_ref[...] = (sgu * ro * go_ref[...]).astype(BF)
        rs_ref[:, RS_V:RS_V + 1] = rv
        rs_ref[:, RS_O:RS_O + 1] = ro

    row = lambda i: (i, 0)
    fix2 = lambda i: (0, 0)
    return pl.pallas_call(
        body, name="mix_prep", grid=(T // tm,),
        in_specs=[pl.BlockSpec((tm, D), row), pl.BlockSpec((1, D), fix2),
                  pl.BlockSpec((ZW, D), fix2, pipeline_mode=pl.Buffered(1)),
                  pl.BlockSpec((1, LANES), fix2), pl.BlockSpec((1, FOX_HD), fix2), pl.BlockSpec((1, FOX_HD), fix2),
                  pl.BlockSpec((1, GMLP_W), fix2), pl.BlockSpec((GMLP_G, CHUNK, CHUNK), lambda i: (0, 0, 0)),
                  pl.BlockSpec((CHUNK, GMLP_G), fix2), pl.BlockSpec((1, GMLP_W), fix2),
                  pl.BlockSpec((LANES, AUG_W), fix2),
                  pl.BlockSpec((LANES, AUG_W), fix2), pl.BlockSpec((1, AUG_W), fix2), pl.BlockSpec((1, AUG_W), fix2),
                  pl.BlockSpec((1, AUG_W), fix2)],
        out_specs=[pl.BlockSpec((tm, ZW), row), pl.BlockSpec((tm, D), row),
                   pl.BlockSpec((tm, AUG_W), row), pl.BlockSpec((tm, AUG_W), row), pl.BlockSpec((tm, AUG_W), row),
                   pl.BlockSpec((tm, GMLP_W), row), pl.BlockSpec((tm, LANES), row)],
        out_shape=[S((T, ZW), F32), S((T, D), BF), S((T, AUG_W), BF), S((T, AUG_W), BF), S((T, AUG_W), BF),
                   S((T, GMLP_W), BF), S((T, LANES), F32)],
        scratch_shapes=[pltpu.VMEM((1, LANES), F32)],
        compiler_params=_cp(1))(x, g_mix, wz, bf128, g_q, g_k, g_sgu, w_s, b_st, g_go, pc_q, pc_k, one_q, one_k,
                                one_v)


def _fox_fwd(q, k, v):
    T = q.shape[0]
    tq = _tile(T, 1024)
    nq = T // tq

    def body(q_ref, k_ref, v_ref, o_ref, lse_ref, m_sc, acc_sc):
        i, j = pl.program_id(0), pl.program_id(1)

        @pl.when(j == 0)
        def _():
            m_sc[...] = jnp.full(m_sc.shape, NEG, F32)
            acc_sc[...] = jnp.zeros_like(acc_sc)

        def step(masked):
            mask = _tri(tq, True) if masked else None
            for h in range(FOX_HEADS):
                hb = slice(h * HB, (h + 1) * HB)
                s = _nt(q_ref[:, hb], k_ref[:, hb])
                if masked:
                    s = jnp.where(mask, s, NEG)
                m_prev = m_sc[h]
                m_new = jnp.maximum(m_prev, jnp.broadcast_to(jnp.max(s, axis=1, keepdims=True), (tq, HB)))
                p = jnp.exp(s - jnp.tile(m_new, (1, tq // HB))).astype(BF)
                acc_sc[:, hb] = jnp.exp(m_prev - m_new) * acc_sc[:, hb] + _nn(p, v_ref[:, hb])
                m_sc[h] = m_new

        @pl.when(j < i)
        def _():
            step(False)

        @pl.when(j == i)
        def _():
            step(True)
            lse_ref[...] = jnp.zeros_like(lse_ref)
            for h in range(FOX_HEADS):
                l = acc_sc[:, h * HB + COL_A:h * HB + COL_A + 1]
                o_ref[:, h * FOX_HD:(h + 1) * FOX_HD] = acc_sc[:, h * HB:h * HB + FOX_HD] / l
                lse_ref[:, h:h + 1] = m_sc[h][:, 0:1] + jnp.log(l)

    qi = lambda i, j: (i, 0)
    kj = lambda i, j: (jnp.minimum(i, j), 0)
    return pl.pallas_call(
        body, name="fox_fwd", grid=(nq, nq),
        in_specs=[pl.BlockSpec((tq, AUG_W), qi), pl.BlockSpec((tq, AUG_W), kj), pl.BlockSpec((tq, AUG_W), kj)],
        out_specs=[pl.BlockSpec((tq, FOX_W), qi), pl.BlockSpec((tq, LANES), qi)],
        out_shape=[S((T, FOX_W), F32), S((T, LANES), F32)],
        scratch_shapes=[pltpu.VMEM((FOX_HEADS, tq, HB), F32), pltpu.VMEM((tq, AUG_W), F32)],
        compiler_params=_cp(2))(q, k, v)


def _fox_bwd(q, k, v, dob):
    T = q.shape[0]
    tq = _tile(T, 512)
    nq = T // tq
    n_sweeps = 1
    half = AUG_W // n_sweeps
    hpg = FOX_HEADS // n_sweeps

    pairs = [(j, i) for j in range(nq) for i in range(j, nq)]
    jt = jnp.asarray([p[0] for p in pairs], jnp.int32)
    it = jnp.asarray([p[1] for p in pairs], jnp.int32)

    def body(jt_ref, it_ref, q_ref, k_ref, v_ref, do_ref, dq_ref, dk_ref, dv_ref, dq_sc):
        t = pl.program_id(1)
        j, i = jt_ref[t], it_ref[t]

        @pl.when(t == 0)
        def _():
            dq_sc[...] = jnp.zeros_like(dq_sc)

        @pl.when(i == j)
        def _():
            dk_ref[...] = jnp.zeros_like(dk_ref)
            dv_ref[...] = jnp.zeros_like(dv_ref)

        def step(masked):
            rows = pl.ds(pl.multiple_of(i * tq, tq), tq)
            mask = _tri(tq, True) if masked else None
            for h in range(hpg):
                hb = slice(h * HB, (h + 1) * HB)
                qh, kh, vh, doh = q_ref[:, hb], k_ref[:, hb], v_ref[:, hb], do_ref[:, hb]
                s = _nt(qh, kh)
                if masked:
                    s = jnp.where(mask, s, NEG)
                p = jnp.exp(s)
                dsb = (p * _nt(doh, vh)).astype(BF)
                dv_ref[:, hb] += _tn(p.astype(BF), doh)
                dk_ref[:, hb] += _tn(dsb, qh)
                dq_sc[rows, hb] += _nn(dsb, kh)

        @pl.when(i > j)
        def _():
            step(False)

        @pl.when(i == j)
        def _():
            step(True)
            dq_ref[...] = dq_sc[pl.ds(pl.multiple_of(j * tq, tq), tq), :]

    qi = pl.BlockSpec((tq, half), lambda g, t, jt_ref, it_ref: (it_ref[t], g))
    kj = pl.BlockSpec((tq, half), lambda g, t, jt_ref, it_ref: (jt_ref[t], g))
    return pl.pallas_call(
        body, name="fox_bwd",
        grid_spec=pltpu.PrefetchScalarGridSpec(
            num_scalar_prefetch=2, grid=(n_sweeps, len(pairs)), in_specs=[qi, kj, kj, qi], out_specs=[kj, kj, kj],
            scratch_shapes=[pltpu.VMEM((T, half), F32)]),
        out_shape=[S((T, AUG_W), F32), S((T, AUG_W), F32), S((T, AUG_W), F32)],
        compiler_params=_cp(2))(jt, it, q, k, v, dob)


def _mix_out(attn, yg, g_fo, wout, x):
    T, D = x.shape
    tm = _tile(T, 1024)

    def body(a_ref, y_ref, g_ref, w_ref, x_ref, o_ref):
        at = a_ref[...]
        yf = (at * _rstd(at) * g_ref[...]).astype(BF)
        o_ref[...] = x_ref[...] + _nn(yf, w_ref[:FOX_W, :]) + _nn(y_ref[...], w_ref[FOX_W:, :])

    row = lambda i: (i, 0)
    return pl.pallas_call(
        body, name="mix_out", grid=(T // tm,),
        in_specs=[pl.BlockSpec((tm, FOX_W), row), pl.BlockSpec((tm, GMLP_W), row),
                  pl.BlockSpec((1, FOX_W), lambda i: (0, 0)), pl.BlockSpec((D, D), lambda i: (0, 0)),
                  pl.BlockSpec((tm, D), row)],
        out_specs=pl.BlockSpec((tm, D), row),
        out_shape=S((T, D), F32),
        compiler_params=_cp(1))(attn, yg, g_fo, wout, x)


def _mix_out_bwd(dx, attn, yg, g_fo, wout, qf, lse):
    T, D = dx.shape
    tm = _tile(T, 512)
    n = T // tm
    pc_l, pc_d = _piece_matrix(COL_C), _piece_matrix(COL_A)

    def body(dx_ref, a_ref, y_ref, g_ref, w_ref, qf_ref, lse_ref, pl_ref, pd_ref,
             qb_ref, dob_ref, dyg_ref, dw_ref, dg_ref, acc_ref, dsum_ref):
        i = pl.program_id(0)
        dxb = dx_ref[...].astype(BF)
        at = a_ref[...]
        yf = (at * _rstd(at) * g_ref[...]).astype(BF)
        dy = _nt(dxb, w_ref[...])
        p_top = _tn(yf, dxb)
        p_bot = _tn(y_ref[...], dxb)

        @pl.when(i == 0)
        def _():
            acc_ref[:FOX_W, :] = p_top
            acc_ref[FOX_W:, :] = p_bot

        @pl.when(i > 0)
        def _():
            acc_ref[:FOX_W, :] += p_top
            acc_ref[FOX_W:, :] += p_bot

        @pl.when(i == n - 1)
        def _():
            dw_ref[...] = acc_ref[...].astype(BF)

        dat, dgr = _norm_bwd(dy[:, :FOX_W], at, g_ref[...])
        _acc_rows(dg_ref, i == 0, dgr)
        dyg_ref[...] = dy[:, FOX_W:]
        prod = dat * at
        dsum_ref[...] = jnp.zeros_like(dsum_ref)
        for h in range(FOX_HEADS):
            dsum_ref[:, h:h + 1] = jnp.sum(prod[:, h * FOX_HD:(h + 1) * FOX_HD], axis=1, keepdims=True)
        ext_d = _nn(_pieces(-dsum_ref[...]), pd_ref[...]).astype(BF)
        ext_l = _nn(_pieces(-lse_ref[...]), pl_ref[...])
        datb = dat.astype(BF)
        for h in range(FOX_HEADS):
            lo, hi = slice(h * HB, h * HB + FOX_HD), slice(h * HB + FOX_HD, (h + 1) * HB)
            dob_ref[:, lo] = datb[:, h * FOX_HD:(h + 1) * FOX_HD]
            dob_ref[:, hi] = ext_d[:, hi]
            qb_ref[:, lo] = qf_ref[:, lo]
            qb_ref[:, hi] = (qf_ref[:, hi].astype(F32) + ext_l[:, hi]).astype(BF)

    row = lambda i: (i, 0)
    fix = lambda i: (0, 0)
    return pl.pallas_call(
        body, name="mix_out_bwd", grid=(n,),
        in_specs=[pl.BlockSpec((tm, D), row), pl.BlockSpec((tm, FOX_W), row), pl.BlockSpec((tm, GMLP_W), row),
                  pl.BlockSpec((1, FOX_W), fix), pl.BlockSpec((D, D), fix), pl.BlockSpec((tm, AUG_W), row),
                  pl.BlockSpec((tm, LANES), row), pl.BlockSpec((LANES, AUG_W), fix),
                  pl.BlockSpec((LANES, AUG_W), fix)],
        out_specs=[pl.BlockSpec((tm, AUG_W), row), pl.BlockSpec((tm, AUG_W), row), pl.BlockSpec((tm, GMLP_W), row),
                   pl.BlockSpec((D, D), fix), pl.BlockSpec((1, FOX_W), fix)],
        out_shape=[S((T, AUG_W), BF), S((T, AUG_W), BF), S((T, GMLP_W), F32), S((D, D), BF), S((1, FOX_W), F32)],
        scratch_shapes=[pltpu.VMEM((D, D), F32), pltpu.VMEM((tm, LANES), F32)],
        compiler_params=_cp(1))(dx, attn, yg, g_fo, wout, qf, lse, pc_l, pc_d)


def _mix_prep_bwd(z, dq, dk, dv, dyg, rs, bf128, g_q, g_k, g_sgu, w_s, b_st, g_go):
    T = z.shape[0]
    tm = _tile(T, 512)
    n = T // tm

    def body(z_ref, dq_ref, dk_ref, dv_ref, dyg_ref, rs_ref, bf_ref, gq_ref, gk_ref, gs_ref, ws_ref,
             bst_ref, go_ref, dz_ref, dgq_ref, dgk_ref, dgs_ref, dgo_ref, dws_ref, dbst_ref, dbf_ref, carry_ref):
        i = pl.program_id(0)
        first = i == 0
        rs = rs_ref[...]

        @pl.when(first)
        def _():
            carry_ref[...] = jnp.zeros_like(carry_ref)

        lane = lax.broadcasted_iota(jnp.int32, (tm, LANES), 1)
        dc = jnp.zeros((tm, LANES), F32)
        gq_rows, gk_rows = [], []
        for h in range(FOX_HEADS):
            hp = slice(h * HB, h * HB + FOX_HD)
            dqh, gqr = _norm_bwd(dq_ref[:, hp] * 0.125, z_ref[:, Z_Q + h * FOX_HD:Z_Q + (h + 1) * FOX_HD], gq_ref[...],
                                 rs[:, RS_Q + h:RS_Q + h + 1])
            dkh, gkr = _norm_bwd(dk_ref[:, hp], z_ref[:, Z_K + h * FOX_HD:Z_K + (h + 1) * FOX_HD], gk_ref[...],
                                 rs[:, RS_K + h:RS_K + h + 1])
            dz_ref[:, Z_Q + h * FOX_HD:Z_Q + (h + 1) * FOX_HD] = dqh.astype(BF)
            dz_ref[:, Z_K + h * FOX_HD:Z_K + (h + 1) * FOX_HD] = dkh.astype(BF)
            dz_ref[:, Z_V + h * FOX_HD:Z_V + (h + 1) * FOX_HD] = dv_ref[:, hp].astype(BF)
            dch = dq_ref[:, h * HB + COL_A:h * HB + COL_A + 1] - dk_ref[:, h * HB + COL_B:h * HB + COL_B + 1]
            dc = jnp.where(lane == h, dch, dc)
            gq_rows.append(gqr)
            gk_rows.append(gkr)
        _acc_rows(dgq_ref, first, functools.reduce(lambda a, b: a + b, gq_rows))
        _acc_rows(dgk_ref, first, functools.reduce(lambda a, b: a + b, gk_rows))

        dlogf = _hi3(_tri(tm, False), dc) + carry_ref[...]
        carry_ref[...] = dlogf[0:1, :]
        fl = z_ref[:, Z_F:Z_F + LANES] + bf_ref[...]
        lane = lax.broadcasted_iota(jnp.int32, (tm, LANES), 1)
        df = jnp.where(lane < FOX_HEADS, dlogf * jax.nn.sigmoid(-fl), 0.0)
        dz_ref[:, Z_F:Z_F + LANES] = df.astype(BF)
        _acc_rows(dbf_ref, first, df)

        u_pre = z_ref[:, Z_U:Z_U + GMLP_W]
        vg_pre = z_ref[:, Z_G:Z_G + GMLP_W]
        u = _gelu(u_pre)
        vg = _gelu(vg_pre)
        rv = rs[:, RS_V:RS_V + 1]
        vgn = (vg * rv * gs_ref[...]).astype(BF)
        bst = bst_ref[...]
        mixed, wms = _spatial_mix(vgn, ws_ref, bst, tm)
        sgu = u * mixed
        dsgu, gor = _norm_bwd(dyg_ref[...], sgu, go_ref[...], rs[:, RS_O:RS_O + 1])
        _acc_rows(dgo_ref, first, gor)
        du = dsgu * mixed
        dmixed = dsgu * u
        dmb = dmixed.astype(BF)
        tril = _tri(CHUNK, True)
        dvgn_rows = []
        dws = [None] * GMLP_G
        dbs = [None] * GMLP_G
        for c in range(tm // CHUNK):
            cs = slice(c * CHUNK, (c + 1) * CHUNK)
            cols = []
            for g in range(GMLP_G):
                gs = slice(g * GMLP_GD, (g + 1) * GMLP_GD)
                dmc = dmb[cs, gs]
                pw = _nt(dmc, vgn[cs, gs])
                pb = jnp.sum(dmixed[cs, gs], axis=1, keepdims=True)
                dws[g] = pw if dws[g] is None else dws[g] + pw
                dbs[g] = pb if dbs[g] is None else dbs[g] + pb
                cols.append(_tn(wms[g], dmc))
            dvgn_rows.append(jnp.concatenate(cols, axis=1))
        dvgn = jnp.concatenate(dvgn_rows, axis=0)
        dbs_t = jnp.concatenate(dbs, axis=1)
        for g in range(GMLP_G):
            dwg = jnp.where(tril, dws[g], 0.0)

            @pl.when(first)
            def _():
                dws_ref[g] = dwg

            @pl.when(jnp.logical_not(first))
            def _():
                dws_ref[g] += dwg

        @pl.when(first)
        def _():
            dbst_ref[...] = dbs_t

        @pl.when(jnp.logical_not(first))
        def _():
            dbst_ref[...] += dbs_t

        dvg, gsr = _norm_bwd(dvgn, vg, gs_ref[...], rv)
        _acc_rows(dgs_ref, first, gsr)
        dz_ref[:, Z_U:Z_U + GMLP_W] = (du * _gelu_grad(u_pre)).astype(BF)
        dz_ref[:, Z_G:Z_G + GMLP_W] = (dvg * _gelu_grad(vg_pre)).astype(BF)

    rev = lambda i: (n - 1 - i, 0)
    fix = lambda i: (0, 0)
    fix3 = lambda i: (0, 0, 0)
    return pl.pallas_call(
        body, name="mix_prep_bwd", grid=(n,),
        in_specs=[pl.BlockSpec((tm, ZW), rev), pl.BlockSpec((tm, AUG_W), rev), pl.BlockSpec((tm, AUG_W), rev),
                  pl.BlockSpec((tm, AUG_W), rev), pl.BlockSpec((tm, GMLP_W), rev), pl.BlockSpec((tm, LANES), rev),
                  pl.BlockSpec((1, LANES), fix), pl.BlockSpec((1, FOX_HD), fix), pl.BlockSpec((1, FOX_HD), fix),
                  pl.BlockSpec((1, GMLP_W), fix), pl.BlockSpec((GMLP_G, CHUNK, CHUNK), fix3),
                  pl.BlockSpec((CHUNK, GMLP_G), fix), pl.BlockSpec((1, GMLP_W), fix)],
        out_specs=[pl.BlockSpec((tm, ZW), rev), pl.BlockSpec((1, FOX_HD), fix), pl.BlockSpec((1, FOX_HD), fix),
                   pl.BlockSpec((1, GMLP_W), fix), pl.BlockSpec((1, GMLP_W), fix),
                   pl.BlockSpec((GMLP_G, CHUNK, CHUNK), fix3), pl.BlockSpec((CHUNK, GMLP_G), fix),
                   pl.BlockSpec((1, LANES), fix)],
        out_shape=[S((T, ZW), BF), S((1, FOX_HD), F32), S((1, FOX_HD), F32), S((1, GMLP_W), F32), S((1, GMLP_W), F32),
                   S((GMLP_G, CHUNK, CHUNK), F32), S((CHUNK, GMLP_G), F32), S((1, LANES), F32)],
        scratch_shapes=[pltpu.VMEM((1, LANES), F32)],
        compiler_params=_cp(1))(z, dq, dk, dv, dyg, rs, bf128, g_q, g_k, g_sgu, w_s, b_st, g_go)


def _mix_proj_bwd(dz, wz, x, g, dy):
    T, D = x.shape
    tm = _tile(T, 512)

    def body(dz_ref, w_ref, x_ref, g_ref, dy_ref, dx_ref, dxb_ref, dg_ref):
        dh = _nn(dz_ref[...], w_ref[...])
        dx, dgr = _norm_bwd(dh, x_ref[...], g_ref[...])
        dx = dx + dy_ref[...]
        dx_ref[...] = dx
        dxb_ref[...] = dx.astype(BF)
        _acc_rows(dg_ref, pl.program_id(0) == 0, dgr)

    row = lambda i: (i, 0)
    fix = lambda i: (0, 0)
    return pl.pallas_call(
        body, name="mix_proj_bwd", grid=(T // tm,),
        in_specs=[pl.BlockSpec((tm, ZW), row), pl.BlockSpec((ZW, D), fix), pl.BlockSpec((tm, D), row),
                  pl.BlockSpec((1, D), fix), pl.BlockSpec((tm, D), row)],
        out_specs=[pl.BlockSpec((tm, D), row), pl.BlockSpec((tm, D), row), pl.BlockSpec((1, D), fix)],
        out_shape=[S((T, D), F32), S((T, D), BF), S((1, D), F32)],
        compiler_params=_cp(1))(dz, wz, x, g, dy)


def _ca_kv(mem, g_mem, wckv, g_ck):
    M, D = mem.shape

    def body(m_ref, g_ref, w_ref, gk_ref, mn_ref, kr_ref, kn_ref, v_ref):
        mf = m_ref[...]
        mn = (mf * _rstd(mf) * g_ref[...]).astype(BF)
        mn_ref[...] = mn
        for h in range(CA_HEADS):
            kr = _nn(mn, w_ref[h])
            kr_ref[h] = kr
            kn_ref[h] = (kr * _rstd(kr) * gk_ref[...]).astype(BF)
            v_ref[h] = _nn(mn, w_ref[CA_HEADS + h]).astype(BF)

    hd = (CA_HEADS, M, CA_HD)
    return pl.pallas_call(
        body, name="ca_kv", out_shape=[S((M, D), BF), S(hd, F32), S(hd, BF), S(hd, BF)],
        compiler_params=pltpu.CompilerParams(vmem_limit_bytes=VMEM_LIMIT))(mem, g_mem, wckv, g_ck)


def _ca_tile_fwd(xt, gca, wcq, gcq, kn_ref, v_ref):
    hb = (xt * _rstd(xt) * gca).astype(BF)
    qc = _nn(hb, wcq)
    qr, qn, ps = [], [], []
    for h in range(CA_HEADS):
        qh = qc[:, h * CA_HD:(h + 1) * CA_HD]
        qnh = (qh * _rstd(qh) * gcq * 0.0625).astype(BF)
        s = _nt(qnh, kn_ref[h])
        e = jnp.exp(s - jnp.max(s, axis=1, keepdims=True))
        ps.append(e / jnp.sum(e, axis=1, keepdims=True))
        qr.append(qh)
        qn.append(qnh)
    return hb, qr, qn, ps


def _ca_fwd(x, g_ca, wcq, g_cq, kn, vv, wco):
    T, D = x.shape
    M = kn.shape[1]
    tm = _tile(T, 1024)

    def body(x_ref, gca_ref, wcq_ref, gcq_ref, kn_ref, v_ref, wco_ref, o_ref, ob_sc):
        xt = x_ref[...]
        _, _, _, ps = _ca_tile_fwd(xt, gca_ref[...], wcq_ref[...], gcq_ref[...], kn_ref, v_ref)
        for h in range(CA_HEADS):
            ob_sc[:, h * CA_HD:(h + 1) * CA_HD] = _nn(ps[h].astype(BF), v_ref[h]).astype(BF)
        o_ref[...] = xt + _nn(ob_sc[...], wco_ref[...])

    row = lambda i: (i, 0)
    fix = lambda i: (0, 0)
    fix3 = lambda i: (0, 0, 0)
    return pl.pallas_call(
        body, name="ca_fwd", grid=(T // tm,),
        in_specs=[pl.BlockSpec((tm, D), row), pl.BlockSpec((1, D), fix), pl.BlockSpec((D, D), fix),
                  pl.BlockSpec((1, CA_HD), fix), pl.BlockSpec((CA_HEADS, M, CA_HD), fix3),
                  pl.BlockSpec((CA_HEADS, M, CA_HD), fix3), pl.BlockSpec((D, D), fix)],
        out_specs=pl.BlockSpec((tm, D), row), out_shape=S((T, D), F32),
        scratch_shapes=[pltpu.VMEM((tm, D), BF)],
        compiler_params=_cp(1))(x, g_ca, wcq, g_cq, kn, vv, wco)


def _ca_bwd(x, dy, g_ca, wcq, g_cq, kn, vv, wco):
    T, D = x.shape
    M = kn.shape[1]
    tm = _tile(T, 512)
    n = T // tm

    def body(x_ref, dy_ref, gca_ref, wcq_ref, gcq_ref, kn_ref, v_ref, wco_ref,
             dx_ref, dwq_ref, dwo_ref, dkn_ref, dv_ref, dgcq_ref, dgca_ref, aq_sc, ao_sc, ob_sc, dq_sc):
        i = pl.program_id(0)
        first = i == 0
        xt = x_ref[...]
        dyt = dy_ref[...]
        dyb = dyt.astype(BF)
        hb, qr, qn, ps = _ca_tile_fwd(xt, gca_ref[...], wcq_ref[...], gcq_ref[...], kn_ref, v_ref)
        do = _nt(dyb, wco_ref[...])
        gcq_rows = None
        for h in range(CA_HEADS):
            hs = slice(h * CA_HD, (h + 1) * CA_HD)
            p = ps[h]
            pb = p.astype(BF)
            ob_sc[:, hs] = _nn(pb, v_ref[h]).astype(BF)
            doh = do[:, hs].astype(BF)
            dp = _nt(doh, v_ref[h])
            ds = (p * (dp - jnp.sum(dp * p, axis=1, keepdims=True))).astype(BF)
            dvh = _tn(pb, doh)
            dkh = _tn(ds, qn[h])

            @pl.when(first)
            def _():
                dv_ref[h] = dvh
                dkn_ref[h] = dkh

            @pl.when(jnp.logical_not(first))
            def _():
                dv_ref[h] += dvh
                dkn_ref[h] += dkh

            dqn = _nn(ds, kn_ref[h]) * 0.0625
            dqh, gr = _norm_bwd(dqn, qr[h], gcq_ref[...])
            gcq_rows = gr if gcq_rows is None else gcq_rows + gr
            dq_sc[:, hs] = dqh.astype(BF)
        _acc_rows(dgcq_ref, first, gcq_rows)
        dqb = dq_sc[...]
        p_o = _tn(ob_sc[...], dyb)
        p_q = _tn(hb, dqb)

        @pl.when(first)
        def _():
            ao_sc[...] = p_o
            aq_sc[...] = p_q

        @pl.when(jnp.logical_not(first))
        def _():
            ao_sc[...] += p_o
            aq_sc[...] += p_q

        @pl.when(i == n - 1)
        def _():
            dwo_ref[...] = ao_sc[...].astype(BF)
            dwq_ref[...] = aq_sc[...].astype(BF)

        dh = _nt(dqb, wcq_ref[...])
        dx, gar = _norm_bwd(dh, xt, gca_ref[...])
        dx_ref[...] = dx + dyt
        _acc_rows(dgca_ref, first, gar)

    row = lambda i: (i, 0)
    fix = lambda i: (0, 0)
    fix3 = lambda i: (0, 0, 0)
    hd = (CA_HEADS, M, CA_HD)
    return pl.pallas_call(
        body, name="ca_bwd", grid=(n,),
        in_specs=[pl.BlockSpec((tm, D), row), pl.BlockSpec((tm, D), row), pl.BlockSpec((1, D), fix),
                  pl.BlockSpec((D, D), fix), pl.BlockSpec((1, CA_HD), fix), pl.BlockSpec(hd, fix3),
                  pl.BlockSpec(hd, fix3), pl.BlockSpec((D, D), fix)],
        out_specs=[pl.BlockSpec((tm, D), row), pl.BlockSpec((D, D), fix), pl.BlockSpec((D, D), fix),
                   pl.BlockSpec(hd, fix3), pl.BlockSpec(hd, fix3), pl.BlockSpec((1, CA_HD), fix),
                   pl.BlockSpec((1, D), fix)],
        out_shape=[S((T, D), F32), S((D, D), BF), S((D, D), BF), S(hd, F32), S(hd, F32), S((1, CA_HD), F32),
                   S((1, D), F32)],
        scratch_shapes=[pltpu.VMEM((D, D), F32), pltpu.VMEM((D, D), F32), pltpu.VMEM((tm, D), BF),
                        pltpu.VMEM((tm, D), BF)],
        compiler_params=_cp(1))(x, dy, g_ca, wcq, g_cq, kn, vv, wco)


def _ca_kv_bwd(mem, g_mem, mn, kraw, dkn, dvv, wckv, g_ck):
    M, D = mem.shape

    def body(m_ref, g_ref, mn_ref, kr_ref, dkn_ref, dv_ref, w_ref, gk_ref, dw_ref, dgk_ref, dgm_ref):
        mn = mn_ref[...]
        dmn = jnp.zeros((M, D), F32)
        gk_rows = None
        for h in range(CA_HEADS):
            dkr, gr = _norm_bwd(dkn_ref[h], kr_ref[h], gk_ref[...])
            gk_rows = gr if gk_rows is None else gk_rows + gr
            dkb = dkr.astype(BF)
            dvb = dv_ref[h].astype(BF)
            dw_ref[h] = _tn(mn, dkb).astype(BF)
            dw_ref[CA_HEADS + h] = _tn(mn, dvb).astype(BF)
            dmn = dmn + _nt(dkb, w_ref[h]) + _nt(dvb, w_ref[CA_HEADS + h])
        dgk_ref[...] = jnp.sum(gk_rows, axis=0, keepdims=True)
        mf = m_ref[...]
        dgm_ref[...] = jnp.sum(dmn * (mf * _rstd(mf)), axis=0, keepdims=True)

    return pl.pallas_call(
        body, name="ca_kv_bwd",
        out_shape=[S((2 * CA_HEADS, D, CA_HD), BF), S((1, CA_HD), F32), S((1, D), F32)],
        compiler_params=pltpu.CompilerParams(vmem_limit_bytes=VMEM_LIMIT))(mem, g_mem, mn, kraw, dkn, dvv, wckv, g_ck)


def _after(g, token):
    return g if token is None else g + token[0:1, 0:1]


def _local_step(x, mem, target, small, weights, emit):
    T, D = x.shape
    p = small
    bf128 = jnp.pad(p["b_f"], ((0, 0), (0, LANES - FOX_HEADS)))
    b_st = p["b_s"].T

    wup1 = weights("ffn1_up", x)["wup1"]
    a1, h1 = _ffn_up("ffn1_up", x, p["g_ffn1"], wup1)
    wdn1 = weights("ffn1_dn", h1)["wdn1"]
    x1 = _ffn_down("ffn1_down", a1, wdn1, x)
    wm = weights("mix", x1)
    z, h2, qf, ka, va, yg, rs = _mix_prep(x1, p["g_mix"], wm["wz"], bf128, p["g_q"], p["g_k"], p["g_sgu"], p["w_s"],
                                          b_st, p["g_gmlp_o"])
    attn, lse = _fox_fwd(qf, ka, va)
    x2 = _mix_out(attn, yg, p["g_fox_o"], wm["wout"], x1)
    wc = weights("ca", x2)
    mn, kraw, ckn, cvv = _ca_kv(mem, p["g_mem"], wc["wckv"], p["g_ck"])
    x3 = _ca_fwd(x2, p["g_ca"], wc["wcq"], p["g_cq"], ckn, cvv, wc["wco"])
    w2 = weights("ffn2", x3)
    a2, h4 = _ffn_up("ffn2_up", x3, p["g_ffn2"], w2["wup2"])
    dy4, dy4b, sq = _ffn_down_loss("ffn2_down", a2, w2["wdn2"], x3, target)

    gs = {}
    dgu2 = _ffn_bwd_act("ffn2_bwd_act", dy4b, h4, w2["wup2"], w2["wdn2"])
    tok = emit("ffn2", {"wup2": _ffn_dwup("ffn2", h4, dgu2), "wdn2": _ffn_dwdn("ffn2", a2, dy4b)})
    dx3, gs["g_ffn2"] = _ffn_dx("ffn2_dx", dgu2, w2["wup2"], x3, _after(p["g_ffn2"], tok), dy4)

    dx2, dwcq, dwco, dckn, dcvv, gs["g_cq"], gs["g_ca"] = _ca_bwd(
        x2, dx3, p["g_ca"], wc["wcq"], p["g_cq"], ckn, cvv, wc["wco"])
    dwckv, gs["g_ck"], gs["g_mem"] = _ca_kv_bwd(mem, p["g_mem"], mn, kraw, dckn, dcvv, wc["wckv"], p["g_ck"])

    qb, dob, dyg, dwout, gs["g_fox_o"] = _mix_out_bwd(dx2, attn, yg, p["g_fox_o"], wm["wout"], qf, lse)
    dq, dk, dv = _fox_bwd(qb, ka, va, dob)
    dz, gs["g_q"], gs["g_k"], gs["g_sgu"], gs["g_gmlp_o"], gs["w_s"], dbst, dbf = _mix_prep_bwd(
        z, dq, dk, dv, dyg, rs, bf128, p["g_q"], p["g_k"], p["g_sgu"], p["w_s"], b_st, p["g_gmlp_o"])
    gs["b_s"] = dbst.T
    gs["b_f"] = dbf[:, :FOX_HEADS]
    tok_ws = emit("w_s", {"w_s": gs["w_s"]})
    zb = ZW // 3
    dwz = _tn_matmul("mix_dwz", dz, pl.BlockSpec((T, zb), lambda j: (0, j)), h2,
                     S((ZW, D), BF), pl.BlockSpec((zb, D), lambda j: (j, 0)), 3)
    tok = emit("mid", {"wcq": dwcq, "wco": dwco, "wckv": dwckv, "wout": dwout, "wz": dwz})
    dx1, dx1b, gs["g_mix"] = _mix_proj_bwd(dz, wm["wz"], x1, _after(_after(p["g_mix"], tok), tok_ws), dx2)

    dgu1 = _ffn_bwd_act("ffn1_bwd_act", dx1b, h1, wup1, wdn1)
    tok = emit("ffn1_dn", {"wdn1": _ffn_dwdn("ffn1", a1, dx1b)})
    tok = emit("ffn1_up", {"wup1": _ffn_dwup("ffn1", h1, dgu1, after=tok)})
    dx0, gs["g_ffn1"] = _ffn_dx("ffn1_dx", dgu1, wup1, x, _after(p["g_ffn1"], tok), dx1)
    return sq, dx0, gs


MESH = pl.DeviceIdType.MESH
HBM_SPEC = pl.BlockSpec(memory_space=pltpu.HBM)
N_PEER = N_DEV - 1


def _place():
    return lax.axis_index("x"), lax.axis_index("y"), lax.axis_index("c")


def _slot(px, py, pc):
    return 4 * px + 2 * py + pc


SEM_SPEC = pl.BlockSpec(memory_space=pltpu.SEMAPHORE)
ANY_SPEC = pl.BlockSpec(memory_space=pl.ANY)
DATAFLOW = pltpu.SideEffectType.DATAFLOW_SIDE_EFFECTING


def _hbm(a):
    return pltpu.with_memory_space_constraint(a, pltpu.HBM)


def _peer(x, y, c, r):
    return (1 - x if r & 4 else x, 1 - y if r & 2 else y, 1 - c if r & 1 else c)


def _place_own(srcs, whole):
    my = _slot(*_place())
    lands = []
    for s in srcs:
        blk = s[None] if whole else lax.dynamic_slice_in_dim(s, my, 1, 0)
        shape = (N_DEV,) + s.shape if whole else s.shape
        lands.append(lax.dynamic_update_slice_in_dim(lax.empty(shape, s.dtype), blk, my, 0))
    return lands


ALL_PEERS = tuple(range(1, N_DEV))
NEAR_PEERS = (1, 2, 4, 6)
SAME_CORE = (2, 4, 6)


def _copy_start(name, srcs, lands, whole, peers=None):
    n = len(srcs)
    peers = peers or [ALL_PEERS] * n
    wh = list(whole) if isinstance(whole, (list, tuple)) else [whole] * n

    def body(*refs):
        src, land = refs[:n], refs[n:2 * n]
        send, recv = refs[2 * n:3 * n], refs[3 * n:4 * n]
        token = refs[6 * n]
        x, y, c = _place()
        my = _slot(x, y, c)
        for a in range(n):
            for r in peers[a]:
                p = _peer(x, y, c, r)
                pltpu.make_async_remote_copy(
                    src_ref=src[a] if wh[a] else src[a].at[_slot(*p)], dst_ref=land[a].at[my],
                    send_sem=send[a].at[r - 1], recv_sem=recv[a].at[r - 1], device_id=p, device_id_type=MESH).start()
        token[...] = jnp.zeros_like(token)

    out = pl.pallas_call(
        body, name=name,
        out_shape=([pltpu.SemaphoreType.DMA((N_PEER,))] * (2 * n)
                   + [pltpu.HBM(s.shape, s.dtype) for s in srcs] + [pltpu.HBM(s.shape, s.dtype) for s in lands]
                   + [S((8, LANES), F32)]),
        in_specs=[HBM_SPEC] * (2 * n),
        out_specs=[SEM_SPEC] * (2 * n) + [HBM_SPEC] * (2 * n) + [pl.BlockSpec(memory_space=pltpu.VMEM)],
        input_output_aliases={i: 2 * n + i for i in range(2 * n)},
        compiler_params=pltpu.CompilerParams(has_side_effects=DATAFLOW),
    )(*[_hbm(s) for s in srcs], *[_hbm(s) for s in lands])
    return out[:n], out[n:2 * n], out[2 * n:3 * n], out[3 * n:4 * n], out[4 * n]


def _copy_wait(name, srcs, lands, send, recv, after, whole, peers=None, with_srcs=False):
    n = len(srcs)
    peers = peers or [ALL_PEERS] * n
    wh = list(whole) if isinstance(whole, (list, tuple)) else [whole] * n

    def body(*refs):
        src, land = refs[:n], refs[n:2 * n]
        snd, rcv = refs[2 * n:3 * n], refs[3 * n:4 * n]
        x, y, c = _place()
        for a in range(n):
            for r in peers[a]:
                p = _peer(x, y, c, r)
                ps = _slot(*p)
                cp = pltpu.make_async_remote_copy(
                    src_ref=src[a] if wh[a] else src[a].at[ps], dst_ref=land[a].at[ps],
                    send_sem=snd[a].at[r - 1], recv_sem=rcv[a].at[r - 1], device_id=p, device_id_type=MESH)
                cp.wait_send()
                cp.wait_recv()

    out = pl.pallas_call(
        body, name=name,
        out_shape=[pltpu.HBM(s.shape, s.dtype) for s in srcs] + [pltpu.HBM(s.shape, s.dtype) for s in lands],
        in_specs=[HBM_SPEC] * (2 * n) + [SEM_SPEC] * (2 * n) + [ANY_SPEC],
        out_specs=[HBM_SPEC] * (2 * n),
        input_output_aliases={i: i for i in range(2 * n)},
        compiler_params=pltpu.CompilerParams(has_side_effects=DATAFLOW),
    )(*srcs, *lands, *send, *recv, after)
    return (out[:n], out[n:]) if with_srcs else out[n:]


def _forward_start(name, lands):
    n = len(lands)

    def body(*refs):
        land = refs[:n]
        send, recv = refs[n:2 * n], refs[2 * n:3 * n]
        token = refs[4 * n]
        x, y, c = _place()
        for a in range(n):
            for r in SAME_CORE:
                blk = land[a].at[_slot(*_peer(x, y, c, r))]
                pltpu.make_async_remote_copy(
                    src_ref=blk, dst_ref=blk, send_sem=send[a].at[r - 1], recv_sem=recv[a].at[r - 1],
                    device_id=(x, y, 1 - c), device_id_type=MESH).start()
        token[...] = jnp.zeros_like(token)

    out = pl.pallas_call(
        body, name=name,
        out_shape=([pltpu.SemaphoreType.DMA((N_PEER,))] * (2 * n) + [pltpu.HBM(s.shape, s.dtype) for s in lands]
                   + [S((8, LANES), F32)]),
        in_specs=[HBM_SPEC] * n,
        out_specs=[SEM_SPEC] * (2 * n) + [HBM_SPEC] * n + [pl.BlockSpec(memory_space=pltpu.VMEM)],
        input_output_aliases={i: 2 * n + i for i in range(n)},
        compiler_params=pltpu.CompilerParams(has_side_effects=DATAFLOW),
    )(*[_hbm(s) for s in lands])
    return out[:n], out[n:2 * n], out[2 * n:3 * n], out[3 * n]


def _forward_wait(name, lands, send, recv, after):
    n = len(lands)

    def body(*refs):
        land = refs[:n]
        snd, rcv = refs[n:2 * n], refs[2 * n:3 * n]
        x, y, c = _place()
        for a in range(n):
            for r in SAME_CORE:
                cp = pltpu.make_async_remote_copy(
                    src_ref=land[a].at[_slot(*_peer(x, y, c, r))], dst_ref=land[a].at[_slot(*_peer(x, y, c, r | 1))],
                    send_sem=snd[a].at[r - 1], recv_sem=rcv[a].at[r - 1], device_id=(x, y, 1 - c),
                    device_id_type=MESH)
                cp.wait_send()
                cp.wait_recv()

    return pl.pallas_call(
        body, name=name,
        out_shape=[pltpu.HBM(s.shape, s.dtype) for s in lands],
        in_specs=[HBM_SPEC] * n + [SEM_SPEC] * (2 * n) + [ANY_SPEC],
        out_specs=[HBM_SPEC] * n,
        input_output_aliases={i: i for i in range(n)},
        compiler_params=pltpu.CompilerParams(has_side_effects=DATAFLOW),
    )(*lands, *send, *recv, after)


def _adamw(w, g, m, v):
    m2 = ADAM_B1 * m + (1.0 - ADAM_B1) * g
    v2 = ADAM_B2 * v + (1.0 - ADAM_B2) * (g * g)
    m_hat = m2 / (1.0 - ADAM_B1 ** ADAM_STEP)
    v_hat = v2 / (1.0 - ADAM_B2 ** ADAM_STEP)
    delta = -ADAM_LR * (m_hat / (jnp.sqrt(v_hat) + ADAM_EPS) + ADAM_WD * w)
    return delta, m2, v2


def _adamw_big(name, slots, w, m, v, own=None):
    R, C = w.shape
    tr = next((t for t in (128, 176, 64) if R % t == 0 and R // t >= 2), R)

    def finish(g, w_ref, m_ref, v_ref, g_ref, d_ref, m2_ref, v2_ref):
        d, m2, v2 = _adamw(w_ref[...], g, m_ref[...], v_ref[...])
        g_ref[...] = g
        d_ref[...] = d
        m2_ref[...] = m2
        v2_ref[...] = v2

    if own is None:
        def body(s_ref, *refs):
            g = s_ref[0].astype(F32)
            for k in range(1, N_DEV):
                g = g + s_ref[k].astype(F32)
            finish(g, *refs)

        row = pl.BlockSpec((tr, C), lambda i: (i, 0))
        return pl.pallas_call(
            body, name=name, grid=(R // tr,),
            in_specs=[pl.BlockSpec((N_DEV, tr, C), lambda i: (0, i, 0)), row, row, row],
            out_specs=[row] * 4, out_shape=[S((R, C), F32)] * 4,
            compiler_params=_cp(1))(slots, w, m, v)

    def body(my_ref, s_ref, own_ref, *refs):
        mine = own_ref[...]
        g = None
        for k in range(N_DEV):
            part = jnp.where(my_ref[0] == k, mine, s_ref[k]).astype(F32)
            g = part if g is None else g + part
        finish(g, *refs)

    row = pl.BlockSpec((tr, C), lambda i, my_ref: (i, 0))
    my = jnp.reshape(_slot(*_place()), (1,)).astype(jnp.int32)
    return pl.pallas_call(
        body, name=name,
        grid_spec=pltpu.PrefetchScalarGridSpec(
            num_scalar_prefetch=1, grid=(R // tr,),
            in_specs=[pl.BlockSpec((N_DEV, tr, C), lambda i, my_ref: (0, i, 0)),
                      pl.BlockSpec((None, tr, C), lambda i, my_ref: (my_ref[0], i, 0)), row, row, row],
            out_specs=[row] * 4),
        out_shape=[S((R, C), F32)] * 4, compiler_params=_cp(1))(my, slots, own, w, m, v)


TINY_ROWS = (("b_s", 8), ("g_ffn1", 8), ("g_mix", 8), ("g_ca", 8), ("g_mem", 8), ("g_ffn2", 8), ("g_sgu", 4),
             ("g_fox_o", 4), ("g_gmlp_o", 4), ("g_cq", 2), ("g_ck", 2), ("g_q", 1), ("g_k", 1), ("b_f", 1),
             ("loss", 1))
TINY_P = 72


def _tiny_pieces(width):
    return [(j, slice(j * LANES, min((j + 1) * LANES, width))) for j in range(-(-width // LANES))]


def _pack_tiny(grads, sq):
    names = [n for n, _ in TINY_ROWS if n != "loss"]

    def body(*refs):
        ins, sq_ref, o_ref = refs[:len(names)], refs[len(names)], refs[len(names) + 1]
        o_ref[...] = jnp.zeros_like(o_ref)
        at = 0
        for ref, (name, r) in zip(ins, TINY_ROWS):
            if name == "b_s":
                o_ref[at:at + r, :] = ref[...]
            else:
                for j, cols in _tiny_pieces(ref.shape[1]):
                    o_ref[at + j:at + j + 1, 0:cols.stop - cols.start] = ref[:, cols]
            at += r
        o_ref[at:at + 1, :] = sq_ref[0:1, :]

    return pl.pallas_call(body, name="tiny_pack", out_shape=S((TINY_P, LANES), F32))(
        *[grads[n] for n in names], sq)


def _adamw_tiny(slots, w, m, v):
    names = [n for n, _ in TINY_ROWS if n != "loss"]
    k = len(names)

    def body(s_ref, *refs):
        ins, outs, loss_ref = refs[:3 * k], refs[3 * k:7 * k], refs[7 * k]
        g_all = s_ref[0]
        for d in range(1, N_DEV):
            g_all = g_all + s_ref[d]
        at = 0
        for i, (name, r) in enumerate(TINY_ROWS[:k]):
            w_ref, m_ref, v_ref = ins[i], ins[k + i], ins[2 * k + i]
            o = outs[4 * i:4 * i + 4]
            if name == "b_s":
                pieces = [(slice(at, at + r), slice(0, LANES), (slice(None), slice(None)))]
            else:
                pieces = [(slice(at + j, at + j + 1), slice(0, c.stop - c.start), (slice(None), c))
                          for j, c in _tiny_pieces(w_ref.shape[1])]
            for rows, lanes, dst in pieces:
                g = g_all[rows, lanes]
                res = (g,) + _adamw(w_ref[dst], g, m_ref[dst], v_ref[dst])
                for ref, val in zip(o, res):
                    ref[dst] = val
            at += r
        loss_ref[...] = g_all[at:at + 1, :]

    shapes = [S(w[n].shape, F32) for n in names]
    out = pl.pallas_call(
        body, name="adamw_tiny", out_shape=[s for s in shapes for _ in range(4)] + [S((1, LANES), F32)],
    )(slots, *[w[n] for n in names], *[m[n] for n in names], *[v[n] for n in names])
    stores = ({}, {}, {}, {})
    for i, n in enumerate(names):
        for store, t in zip(stores, out[4 * i:4 * i + 4]):
            store[n] = t
    return stores, out[4 * k]


WEIGHTS =('g_ffn1', 'w_ffn1_in', 'w_ffn1_out', 'g_mix', 'w_in', 'b_f', 'g_q', 'g_k', 'g_sgu', 'w_s', 'b_s',
           'g_fox_o', 'g_gmlp_o', 'w_out', 'g_ca', 'g_mem', 'w_cq', 'w_ckv', 'g_cq', 'g_ck', 'w_co', 'g_ffn2',
           'w_ffn2_in', 'w_ffn2_out')
BIG = ('w_ffn1_in', 'w_ffn1_out', 'w_in', 'w_out', 'w_cq', 'w_ckv', 'w_co', 'w_ffn2_in', 'w_ffn2_out')
TRANSPOSED = ('w_ffn1_in', 'w_in', 'w_ffn2_in')
TWO_LEVEL = ('w_ffn1_in', 'w_in')
GATHER_GROUPS = {"ffn1_up": ("w_ffn1_in",), "ffn1_dn": ("w_ffn1_out",), "mix": ("w_in", "w_out"),
                 "ca": ("w_cq", "w_ckv", "w_co"), "ffn2": ("w_ffn2_in", "w_ffn2_out")}
QKV_W = 3 * FOX_W
UV_OFF = QKV_W + FOX_HEADS


def kernel(x, mem, g_ffn1, w_ffn1_in, w_ffn1_out, g_mix, w_in, b_f, g_q, g_k, g_sgu, w_s, b_s, g_fox_o, g_gmlp_o, w_out, g_ca, g_mem, w_cq, w_ckv, g_cq, g_ck, w_co, g_ffn2, w_ffn2_in, w_ffn2_out, loss_target, m_g_ffn1, m_w_ffn1_in, m_w_ffn1_out, m_g_mix, m_w_in, m_b_f, m_g_q, m_g_k, m_g_sgu, m_w_s, m_b_s, m_g_fox_o, m_g_gmlp_o, m_w_out, m_g_ca, m_g_mem, m_w_cq, m_w_ckv, m_g_cq, m_g_ck, m_w_co, m_g_ffn2, m_w_ffn2_in, m_w_ffn2_out, v_g_ffn1, v_w_ffn1_in, v_w_ffn1_out, v_g_mix, v_w_in, v_b_f, v_g_q, v_g_k, v_g_sgu, v_w_s, v_b_s, v_g_fox_o, v_g_gmlp_o, v_w_out, v_g_ca, v_g_mem, v_w_cq, v_w_ckv, v_g_cq, v_g_ck, v_w_co, v_g_ffn2, v_w_ffn2_in, v_w_ffn2_out):
    args = dict(locals())
    w = {n: args[n] for n in WEIGHTS}
    mo = {n: args["m_" + n] for n in WEIGHTS}
    vo = {n: args["v_" + n] for n in WEIGHTS}
    D = D_MODEL

    def local(n, a):
        return a[0].T if n in TRANSPOSED else a[0]

    g_peers = [NEAR_PEERS if n in TWO_LEVEL else ALL_PEERS for n in BIG]
    handles = {}

    def start_gather(name, names, arrays):
        snd, rcv, src, land, token = _copy_start(name, arrays, _place_own(arrays, True), True,
                                                 peers=[g_peers[BIG.index(n)] for n in names])
        handles.update({n: (src[i], land[i], snd[i], rcv[i]) for i, n in enumerate(names)})
        return token

    first = local(BIG[0], w[BIG[0]]).astype(BF)
    fb = first.shape[0]
    token_first = start_gather("gather_start_first", BIG[:1], [first])
    token_rest = start_gather("gather_start_rest", BIG[1:],
                              [(local(n, w[n]) + token_first[0:1, 0:1]).astype(BF) for n in BIG[1:]])

    tiny_names = [n for n, _ in TINY_ROWS if n != "loss"]

    def weights(group, after):
        names = GATHER_GROUPS[group]
        hs = [handles[n] for n in names]
        got = list(_copy_wait("gather_wait_" + group, [h[0] for h in hs], [h[1] for h in hs], [h[2] for h in hs],
                              [h[3] for h in hs], token_rest if group == "ffn1_up" else after, True,
                              peers=[g_peers[BIG.index(n)] for n in names]))
        passed = [i for i, n in enumerate(names) if n in TWO_LEVEL]
        if passed:
            f_snd, f_rcv, f_land, f_token = _forward_start("gather_pass_start_" + group, [got[i] for i in passed])
            for i, t in zip(passed, _forward_wait("gather_pass_wait_" + group, f_land, f_snd, f_rcv, f_token)):
                got[i] = t
        got = dict(zip(names, got))
        if group == "ffn1_up":
            return {"wup1": got["w_ffn1_in"].reshape(2, N_FFN_BLK, fb, D)}
        if group == "ffn1_dn":
            return {"wdn1": got["w_ffn1_out"].reshape(N_FFN_BLK, fb, D)}
        if group == "mix":
            full = got["w_in"].reshape(-1, D)
            wz = jnp.concatenate([full[:QKV_W], full[UV_OFF:], full[QKV_W:UV_OFF],
                                  jnp.zeros((LANES - FOX_HEADS, D), BF)], axis=0)
            return {"wz": wz, "wout": got["w_out"].reshape(D, D)}
        if group == "ca":
            return {"wcq": got["w_cq"].reshape(D, D), "wco": got["w_co"].reshape(D, D), "wckv": got["w_ckv"]}
        return {"wup2": got["w_ffn2_in"].reshape(2, N_FFN_BLK, fb, D),
                "wdn2": got["w_ffn2_out"].reshape(N_FFN_BLK, fb, D)}

    flying = {}

    def emit(group, g):
        if group == "w_s":
            flying[group] = g["w_s"].reshape(-1, LANES)
            return None
        if group == "ffn2":
            parts = {"w_ffn2_in": g["wup2"], "w_ffn2_out": g["wdn2"].reshape(N_DEV, -1, D)}
        elif group == "ffn1_dn":
            parts = {"w_ffn1_out": g["wdn1"].reshape(N_DEV, -1, D)}
        elif group == "ffn1_up":
            parts = {"w_ffn1_in": g["wup1"]}
        else:
            gz = g["wz"]
            g_in = jnp.concatenate([gz[:QKV_W], gz[Z_F:Z_F + FOX_HEADS], gz[QKV_W:Z_F]], axis=0)
            parts = {"w_in": g_in.reshape(N_DEV, -1, D).astype(BF),
                     "w_out": g["wout"].reshape(N_DEV, -1, D), "w_cq": g["wcq"].reshape(N_DEV, -1, D),
                     "w_co": g["wco"].reshape(N_DEV, -1, D), "w_ckv": g["wckv"]}
        names = list(parts)
        srcs = [parts[n] for n in names]
        lands = [lax.empty(s.shape, s.dtype) for s in srcs]
        whole = [False] * len(srcs)
        if group == "mid":
            ws_part = flying.pop("w_s")
            names, srcs, whole = names + ["w_s"], srcs + [ws_part], whole + [True]
            lands += _place_own([ws_part], True)
        *copies, token = _copy_start("exchange_start_" + group, srcs, lands, whole)
        flying[group] = (names, copies, whole)
        return token

    small = {n: (w[n][0] if n == "b_s" else w[n]) for n in tiny_names}
    small["w_s"] = w["w_s"][0]

    sq, dx0, gs = _local_step(x[0], mem[0], loss_target[0], small, weights, emit)

    sm_parts = [_pack_tiny(gs, sq)]
    sm_snd, sm_rcv, sm_src, sm_land, sm_token = _copy_start("tiny_start", sm_parts, _place_own(sm_parts, True), True)

    grad, delta, new_m, new_v = {}, {}, {}, {}

    def update(group, after):
        names, (snd, rcv, srcs, lands), whole = flying[group]
        owns, slots = _copy_wait("exchange_wait_" + group, srcs, lands, snd, rcv, after, whole, with_srcs=True)
        for n, sl, own in zip(names, slots, owns):
            if n == "w_s":
                g, d, m2, v2 = _adamw_big("adamw_w_s", sl, *[a[n].reshape(-1, LANES) for a in (w, mo, vo)])
            else:
                g, d, m2, v2 = _adamw_big("adamw_" + n, sl, local(n, w[n]), local(n, mo[n]), local(n, vo[n]),
                                          own=own)
            grad[n], delta[n], new_m[n], new_v[n] = (
                (t.T if n in TRANSPOSED else t).reshape(w[n].shape) for t in (g, d, m2, v2))
        return d

    last = update("ffn2", sm_token)
    last = update("mid", last)
    last = update("ffn1_dn", last)
    last = update("ffn1_up", last)
    tiny_all, = _copy_wait("tiny_wait", sm_src, sm_land, sm_snd, sm_rcv, last, True)
    stores, loss_row = _adamw_tiny(tiny_all, *[{n: (a[n][0] if n == "b_s" else a[n]) for n in tiny_names}
                                               for a in (w, mo, vo)])
    for store, t in zip((grad, delta, new_m, new_v), stores):
        store.update({n: v.reshape(w[n].shape) for n, v in t.items()})
    loss = loss_row[0, 0] * (0.5 / D)

    return (loss, dx0[None], *[grad[n] for n in WEIGHTS], *[delta[n] for n in WEIGHTS],
            *[new_m[n] for n in WEIGHTS], *[new_v[n] for n in WEIGHTS])
```

```python
import functools

import jax
import jax.numpy as jnp
from jax import lax
from jax.experimental import pallas as pl
from jax.experimental.pallas import tpu as pltpu

F32 = jnp.float32
BF = jnp.bfloat16
S = jax.ShapeDtypeStruct

N_DEV = 8
D_MODEL = 1024
FOX_HEADS, FOX_HD = 8, 64
FOX_W = 512
GMLP_G, GMLP_GD = 8, 64
GMLP_W = 512
CHUNK = 128
CA_HEADS, CA_HD = 4, 256
N_FFN_BLK = 4
ZW = 2688
Z_Q, Z_K, Z_V, Z_U, Z_G, Z_F = 0, 512, 1024, 1536, 2048, 2560
EPS = 1e-6
NEG = -1e30
LANES = 128

ADAM_LR, ADAM_B1, ADAM_B2, ADAM_EPS, ADAM_WD, ADAM_STEP = 0.001, 0.9, 0.999, 1e-08, 0.01, 10

VMEM_LIMIT = 52 * 2 ** 20


def _cp(n_axes):
    return pltpu.CompilerParams(dimension_semantics=("arbitrary",) * n_axes, vmem_limit_bytes=VMEM_LIMIT)


def _nn(a, b):
    return jnp.dot(a, b, preferred_element_type=F32)


def _nt(a, b):
    return lax.dot_general(a, b, (((1,), (1,)), ((), ())), preferred_element_type=F32)


def _tn(a, b):
    return lax.dot_general(a, b, (((0,), (0,)), ((), ())), preferred_element_type=F32)


def _hi(mask, x):
    return jnp.dot(mask.astype(F32), x, precision=lax.Precision.HIGHEST, preferred_element_type=F32)


def _hi3(mask, x):
    mb = mask.astype(BF)
    hi = x.astype(BF)
    r1 = x - hi.astype(F32)
    mid = r1.astype(BF)
    lo = (r1 - mid.astype(F32)).astype(BF)
    return _nn(mb, hi) + _nn(mb, mid) + _nn(mb, lo)


def _rstd(x):
    return lax.rsqrt(jnp.mean(x * x, axis=-1, keepdims=True) + EPS)


def _norm_bwd(dy, x, g, r=None):
    r = _rstd(x) if r is None else r
    xh = x * r
    dxh = dy * g
    dx = r * (dxh - xh * jnp.mean(dxh * xh, axis=-1, keepdims=True))
    return dx, dy * xh


def _acc_rows(ref, first, val):
    srow = jnp.sum(val, axis=0, keepdims=True)

    @pl.when(first)
    def _():
        ref[...] = srow

    @pl.when(jnp.logical_not(first))
    def _():
        ref[...] += srow


def _gelu(x):
    c = 0.7978845608028654
    return 0.5 * x * (1.0 + jnp.tanh(c * (x + 0.044715 * x * x * x)))


def _gelu_grad(x):
    c = 0.7978845608028654
    t = jnp.tanh(c * (x + 0.044715 * x * x * x))
    return 0.5 * (1.0 + t) + 0.5 * x * (1.0 - t * t) * c * (1.0 + 3 * 0.044715 * x * x)


def _tile(n, pref):
    return pref if n % pref == 0 else n


def _rms_cast(name, x, g, after):
    T, D = x.shape
    tm = _tile(T, 1024)

    def body(x_ref, g_ref, t_ref, h_ref):
        xf = x_ref[...]
        h_ref[...] = (xf * _rstd(xf) * g_ref[...]).astype(BF)

    return pl.pallas_call(
        body, name=name, grid=(T // tm,),
        in_specs=[pl.BlockSpec((tm, D), lambda i: (i, 0)), pl.BlockSpec((1, D), lambda i: (0, 0)),
                  pl.BlockSpec((8, LANES), lambda i: (0, 0))],
        out_specs=pl.BlockSpec((tm, D), lambda i: (i, 0)), out_shape=S((T, D), BF),
        compiler_params=_cp(1))(x, g, after)


def _ffn_up_from_h(name, h, wup):
    T, D = h.shape
    FB = wup.shape[-2]
    tm = _tile(T, 1024)

    def body(h_ref, w_ref, a_ref):
        hb = h_ref[...]
        gg = _nt(hb, w_ref[0])
        uu = _nt(hb, w_ref[1])
        a_ref[...] = (gg * jax.nn.sigmoid(gg) * uu).astype(BF)

    return pl.pallas_call(
        body, name=name, grid=(T // tm, N_FFN_BLK),
        in_specs=[pl.BlockSpec((tm, D), lambda i, j: (i, 0)),
                  pl.BlockSpec((2, None, FB, D), lambda i, j: (0, j, 0, 0))],
        out_specs=pl.BlockSpec((None, tm, FB), lambda i, j: (j, i, 0)),
        out_shape=S((N_FFN_BLK, T, FB), BF),
        compiler_params=_cp(2))(h, wup)


def _ffn_up(name, x, g, wup):
    T, D = x.shape
    FB = wup.shape[-2]
    tm = _tile(T, 1024)

    def body(x_ref, g_ref, w_ref, a_ref, h_ref):
        @pl.when(pl.program_id(1) == 0)
        def _():
            xf = x_ref[...]
            h_ref[...] = (xf * _rstd(xf) * g_ref[...]).astype(BF)

        hb = h_ref[...]
        gg = _nt(hb, w_ref[0])
        uu = _nt(hb, w_ref[1])
        a_ref[...] = (gg * jax.nn.sigmoid(gg) * uu).astype(BF)

    return pl.pallas_call(
        body, name=name, grid=(T // tm, N_FFN_BLK),
        in_specs=[pl.BlockSpec((tm, D), lambda i, j: (i, 0)),
                  pl.BlockSpec((1, D), lambda i, j: (0, 0)),
                  pl.BlockSpec((2, None, FB, D), lambda i, j: (0, j, 0, 0))],
        out_specs=[pl.BlockSpec((None, tm, FB), lambda i, j: (j, i, 0)),
                   pl.BlockSpec((tm, D), lambda i, j: (i, 0))],
        out_shape=[S((N_FFN_BLK, T, FB), BF), S((T, D), BF)],
        compiler_params=_cp(2))(x, g, wup)


def _ffn_down(name, a, wdn, x):
    _, T, FB = a.shape
    D = x.shape[1]
    tm = _tile(T, 512)

    def body(a_ref, w_ref, x_ref, o_ref):
        p = _nn(a_ref[0], w_ref[0])
        for j in range(1, N_FFN_BLK):
            p = p + _nn(a_ref[j], w_ref[j])
        o_ref[...] = x_ref[...] + 0.5 * p

    return pl.pallas_call(
        body, name=name, grid=(T // tm,),
        in_specs=[pl.BlockSpec((N_FFN_BLK, tm, FB), lambda i: (0, i, 0)),
                  pl.BlockSpec((N_FFN_BLK, FB, D), lambda i: (0, 0, 0)),
                  pl.BlockSpec((tm, D), lambda i: (i, 0))],
        out_specs=pl.BlockSpec((tm, D), lambda i: (i, 0)),
        out_shape=S((T, D), F32),
        compiler_params=_cp(1))(a, wdn, x)


def _ffn_down_loss(name, a, wdn, x, target):
    _, T, FB = a.shape
    D = x.shape[1]
    tm = _tile(T, 512)

    def body(a_ref, w_ref, x_ref, t_ref, d_ref, db_ref, loss_ref):
        i = pl.program_id(0)
        p = _nn(a_ref[0], w_ref[0])
        for j in range(1, N_FFN_BLK):
            p = p + _nn(a_ref[j], w_ref[j])
        diff = (x_ref[...] + 0.5 * p) - t_ref[...]
        dy = diff * (1.0 / D)
        d_ref[...] = dy
        db_ref[...] = dy.astype(BF)
        sq = jnp.zeros((8, LANES), F32) + jnp.sum(diff * diff)

        @pl.when(i == 0)
        def _():
            loss_ref[...] = sq

        @pl.when(i > 0)
        def _():
            loss_ref[...] += sq

    row = pl.BlockSpec((tm, D), lambda i: (i, 0))
    return pl.pallas_call(
        body, name=name, grid=(T // tm,),
        in_specs=[pl.BlockSpec((N_FFN_BLK, tm, FB), lambda i: (0, i, 0)),
                  pl.BlockSpec((N_FFN_BLK, FB, D), lambda i: (0, 0, 0)), row, row],
        out_specs=[row, row, pl.BlockSpec((8, LANES), lambda i: (0, 0))],
        out_shape=[S((T, D), F32), S((T, D), BF), S((8, LANES), F32)],
        compiler_params=_cp(1))(a, wdn, x, target)


def _ffn_bwd_act(name, dyb, h, wup, wdn):
    T, D = h.shape
    FB = wup.shape[-2]
    tm = _tile(T, 1024)

    def body(d_ref, h_ref, wu_ref, wd_ref, o_ref):
        da = 0.5 * _nt(d_ref[...], wd_ref[...])
        hb = h_ref[...]
        gg = _nt(hb, wu_ref[0])
        uu = _nt(hb, wu_ref[1])
        sg = jax.nn.sigmoid(gg)
        o_ref[0] = (da * uu * (sg * (1.0 + gg * (1.0 - sg)))).astype(BF)
        o_ref[1] = (da * (gg * sg)).astype(BF)

    return pl.pallas_call(
        body, name=name, grid=(T // tm, N_FFN_BLK),
        in_specs=[pl.BlockSpec((tm, D), lambda i, j: (i, 0)),
                  pl.BlockSpec((tm, D), lambda i, j: (i, 0)),
                  pl.BlockSpec((2, None, FB, D), lambda i, j: (0, j, 0, 0)),
                  pl.BlockSpec((None, FB, D), lambda i, j: (j, 0, 0))],
        out_specs=pl.BlockSpec((2, None, tm, FB), lambda i, j: (0, j, i, 0)),
        out_shape=S((2, N_FFN_BLK, T, FB), BF),
        compiler_params=_cp(2))(dyb, h, wup, wdn)


def _ffn_dx(name, dgu, wup, x, g, dy):
    T, D = x.shape
    FB = wup.shape[-2]
    tm = _tile(T, 512)

    def body(d_ref, w_ref, x_ref, g_ref, dy_ref, dx_ref, dg_ref):
        p = None
        for j in range(N_FFN_BLK):
            for half in range(2):
                t = _nn(d_ref[half, j], w_ref[half, j])
                p = t if p is None else p + t
        dx, dgr = _norm_bwd(p, x_ref[...], g_ref[...])
        dx_ref[...] = dx + dy_ref[...]
        _acc_rows(dg_ref, pl.program_id(0) == 0, dgr)

    return pl.pallas_call(
        body, name=name, grid=(T // tm,),
        in_specs=[pl.BlockSpec((2, N_FFN_BLK, tm, FB), lambda i: (0, 0, i, 0)),
                  pl.BlockSpec((2, N_FFN_BLK, FB, D), lambda i: (0, 0, 0, 0), pipeline_mode=pl.Buffered(1)),
                  pl.BlockSpec((tm, D), lambda i: (i, 0)),
                  pl.BlockSpec((1, D), lambda i: (0, 0)),
                  pl.BlockSpec((tm, D), lambda i: (i, 0))],
        out_specs=[pl.BlockSpec((tm, D), lambda i: (i, 0)),
                   pl.BlockSpec((1, D), lambda i: (0, 0))],
        out_shape=[S((T, D), F32), S((1, D), F32)],
        compiler_params=_cp(1))(dgu, wup, x, g, dy)


def _tn_matmul(name, a, a_spec, b, out_shape, out_spec, n_blocks, scale=1.0, after=None):
    extra = [] if after is None else [after]

    def body(a_ref, b_ref, *rest):
        o_ref = rest[-1]
        o_ref[...] = (_tn(a_ref[...], b_ref[...]) * scale).astype(o_ref.dtype)

    return pl.pallas_call(
        body, name=name, grid=(n_blocks,),
        in_specs=[a_spec, pl.BlockSpec(b.shape, lambda j: (0, 0), pipeline_mode=pl.Buffered(1))]
        + [pl.BlockSpec((8, LANES), lambda j: (0, 0)) for _ in extra],
        out_specs=out_spec, out_shape=out_shape, compiler_params=_cp(1))(a, b, *extra)


def _ffn_dwup(name, h, dgu, after=None):
    T, D = h.shape
    FB = dgu.shape[-1]
    return _tn_matmul(
        name + "_dwup", dgu.reshape(2 * N_FFN_BLK, T, FB), pl.BlockSpec((None, T, FB), lambda j: (j, 0, 0)), h,
        S((2 * N_FFN_BLK, FB, D), BF), pl.BlockSpec((None, FB, D), lambda j: (j, 0, 0)), 2 * N_FFN_BLK,
        after=after)


def _ffn_dwdn(name, a, dyb):
    _, T, FB = a.shape
    D = dyb.shape[1]
    return _tn_matmul(
        name + "_dwdn", a, pl.BlockSpec((None, T, FB), lambda j: (j, 0, 0)), dyb,
        S((N_FFN_BLK, FB, D), BF), pl.BlockSpec((None, FB, D), lambda j: (j, 0, 0)), N_FFN_BLK, scale=0.5)


def _tri(n, lower):
    r = lax.broadcasted_iota(jnp.int32, (n, n), 0)
    c = lax.broadcasted_iota(jnp.int32, (n, n), 1)
    return (r >= c) if lower else (r <= c)


def _spatial_mix(vgn_b, ws_ref, bst, tm):
    tril = _tri(CHUNK, True)
    wms = [jnp.where(tril, ws_ref[g], 0.0).astype(BF) for g in range(GMLP_G)]
    rows = []
    for c in range(tm // CHUNK):
        cols = []
        for g in range(GMLP_G):
            vs = vgn_b[c * CHUNK:(c + 1) * CHUNK, g * GMLP_GD:(g + 1) * GMLP_GD]
            cols.append(_nn(wms[g], vs) + bst[:, g:g + 1])
        rows.append(jnp.concatenate(cols, axis=1))
    return jnp.concatenate(rows, axis=0), wms


HB = 128
AUG_W = FOX_HEADS * HB
COL_A, COL_B, COL_C = 64, 67, 70
RS_Q, RS_K, RS_V, RS_O = 0, 8, 16, 17


def _piece_matrix(col):
    r = jnp.arange(LANES)
    dst = jnp.where(r < 3 * FOX_HEADS, (r % FOX_HEADS) * HB + col + r // FOX_HEADS, -1)
    return (jnp.arange(AUG_W)[None, :] == dst[:, None]).astype(BF)


def _ones_row(cols):
    c = jnp.arange(AUG_W) % HB
    hit = functools.reduce(jnp.logical_or, [(c >= a) & (c < a + 3) for a in cols])
    return hit.astype(F32)[None, :]


def _pieces(x):
    lane = lax.broadcasted_iota(jnp.int32, x.shape, 1)
    x = jnp.where(lane < FOX_HEADS, x, 0.0)
    hi = x.astype(BF).astype(F32)
    r1 = x - hi
    mid = r1.astype(BF).astype(F32)
    lo = (r1 - mid).astype(BF).astype(F32)
    return (hi + pltpu.roll(mid, FOX_HEADS, 1) + pltpu.roll(lo, 2 * FOX_HEADS, 1)).astype(BF)


def _mix_prep(x, g_mix, wz, bf128, g_q, g_k, g_sgu, w_s, b_st, g_go):
    T, D = x.shape
    tm = _tile(T, 512)
    pc_q, pc_k = _piece_matrix(COL_A), _piece_matrix(COL_B)
    one_q, one_k, one_v = _ones_row([COL_B]), _ones_row([COL_A, COL_C]), _ones_row([COL_A])

    def body(x_ref, gm_ref, wz_ref, bf_ref, gq_ref, gk_ref, gs_ref, ws_ref, bst_ref, go_ref, pq_ref, pk_ref, oq_ref,
             ok_ref, ov_ref, z_ref, h_ref, q_ref, k_ref, v_ref, y_ref, rs_ref, carry_ref):
        i = pl.program_id(0)

        @pl.when(i == 0)
        def _():
            carry_ref[...] = jnp.zeros_like(carry_ref)

        xf = x_ref[...]
        hb = (xf * _rstd(xf) * gm_ref[...]).astype(BF)
        h_ref[...] = hb
        z_ref[...] = _nt(hb, wz_ref[...])

        fl = z_ref[:, Z_F:Z_F + LANES] + bf_ref[...]
        logf = jnp.minimum(fl, 0.0) - jnp.log1p(jnp.exp(-jnp.abs(fl)))
        csum = _hi(_tri(tm, True), logf) + carry_ref[...]
        carry_ref[...] = csum[tm - 1:tm, :]
        ext_q = (_nn(_pieces(csum), pq_ref[...]) + oq_ref[...]).astype(BF)
        ext_k = (_nn(_pieces(-csum), pk_ref[...]) + ok_ref[...]).astype(BF)
        ext_v = jnp.broadcast_to(ov_ref[...], (tm, AUG_W)).astype(BF)

        rs_ref[...] = jnp.zeros_like(rs_ref)
        for h in range(FOX_HEADS):
            lo, hi = slice(h * HB, h * HB + FOX_HD), slice(h * HB + FOX_HD, (h + 1) * HB)
            qh = z_ref[:, Z_Q + h * FOX_HD:Z_Q + (h + 1) * FOX_HD]
            kh = z_ref[:, Z_K + h * FOX_HD:Z_K + (h + 1) * FOX_HD]
            rq, rk = _rstd(qh), _rstd(kh)
            rs_ref[:, RS_Q + h:RS_Q + h + 1] = rq
            rs_ref[:, RS_K + h:RS_K + h + 1] = rk
            q_ref[:, lo] = (qh * rq * gq_ref[...] * 0.125).astype(BF)
            k_ref[:, lo] = (kh * rk * gk_ref[...]).astype(BF)
            v_ref[:, lo] = z_ref[:, Z_V + h * FOX_HD:Z_V + (h + 1) * FOX_HD].astype(BF)
            q_ref[:, hi] = ext_q[:, hi]
            k_ref[:, hi] = ext_k[:, hi]
            v_ref[:, hi] = ext_v[:, hi]

        u = _gelu(z_ref[:, Z_U:Z_U + GMLP_W])
        vg = _gelu(z_ref[:, Z_G:Z_G + GMLP_W])
        rv = _rstd(vg)
        vgn = (vg * rv * gs_ref[...]).astype(BF)
        mixed, _ = _spatial_mix(vgn, ws_ref, bst_ref[...], tm)
        sgu = u * mixed
        ro = _rstd(sgu)
        y_ref[...] = (sgu * ro * go_ref[...]).astype(BF)
        rs_ref[:, RS_V:RS_V + 1] = rv
        rs_ref[:, RS_O:RS_O + 1] = ro

    row = lambda i: (i, 0)
    fix2 = lambda i: (0, 0)
    return pl.pallas_call(
        body, name="mix_prep", grid=(T // tm,),
        in_specs=[pl.BlockSpec((tm, D), row), pl.BlockSpec((1, D), fix2),
                  pl.BlockSpec((ZW, D), fix2, pipeline_mode=pl.Buffered(1)),
                  pl.BlockSpec((1, LANES), fix2), pl.BlockSpec((1, FOX_HD), fix2), pl.BlockSpec((1, FOX_HD), fix2),
                  pl.BlockSpec((1, GMLP_W), fix2), pl.BlockSpec((GMLP_G, CHUNK, CHUNK), lambda i: (0, 0, 0)),
                  pl.BlockSpec((CHUNK, GMLP_G), fix2), pl.BlockSpec((1, GMLP_W), fix2),
                  pl.BlockSpec((LANES, AUG_W), fix2),
                  pl.BlockSpec((LANES, AUG_W), fix2), pl.BlockSpec((1, AUG_W), fix2), pl.BlockSpec((1, AUG_W), fix2),
                  pl.BlockSpec((1, AUG_W), fix2)],
        out_specs=[pl.BlockSpec((tm, ZW), row), pl.BlockSpec((tm, D), row),
                   pl.BlockSpec((tm, AUG_W), row), pl.BlockSpec((tm, AUG_W), row), pl.BlockSpec((tm, AUG_W), row),
                   pl.BlockSpec((tm, GMLP_W), row), pl.BlockSpec((tm, LANES), row)],
        out_shape=[S((T, ZW), F32), S((T, D), BF), S((T, AUG_W), BF), S((T, AUG_W), BF), S((T, AUG_W), BF),
                   S((T, GMLP_W), BF), S((T, LANES), F32)],
        scratch_shapes=[pltpu.VMEM((1, LANES), F32)],
        compiler_params=_cp(1))(x, g_mix, wz, bf128, g_q, g_k, g_sgu, w_s, b_st, g_go, pc_q, pc_k, one_q, one_k,
                                one_v)


def _fox_fwd(q, k, v):
    T = q.shape[0]
    tq = _tile(T, 1024)
    nq = T // tq

    def body(q_ref, k_ref, v_ref, o_ref, lse_ref, m_sc, acc_sc):
        i, j = pl.program_id(0), pl.program_id(1)

        @pl.when(j == 0)
        def _():
            m_sc[...] = jnp.full(m_sc.shape, NEG, F32)
            acc_sc[...] = jnp.zeros_like(acc_sc)

        def step(masked):
            mask = _tri(tq, True) if masked else None
            for h in range(FOX_HEADS):
                hb = slice(h * HB, (h + 1) * HB)
                s = _nt(q_ref[:, hb], k_ref[:, hb])
                if masked:
                    s = jnp.where(mask, s, NEG)
                m_prev = m_sc[h]
                m_new = jnp.maximum(m_prev, jnp.broadcast_to(jnp.max(s, axis=1, keepdims=True), (tq, HB)))
                p = jnp.exp(s - jnp.tile(m_new, (1, tq // HB))).astype(BF)
                acc_sc[:, hb] = jnp.exp(m_prev - m_new) * acc_sc[:, hb] + _nn(p, v_ref[:, hb])
                m_sc[h] = m_new

        @pl.when(j < i)
        def _():
            step(False)

        @pl.when(j == i)
        def _():
            step(True)
            lse_ref[...] = jnp.zeros_like(lse_ref)
            for h in range(FOX_HEADS):
                l = acc_sc[:, h * HB + COL_A:h * HB + COL_A + 1]
                o_ref[:, h * FOX_HD:(h + 1) * FOX_HD] = acc_sc[:, h * HB:h * HB + FOX_HD] / l
                lse_ref[:, h:h + 1] = m_sc[h][:, 0:1] + jnp.log(l)

    qi = lambda i, j: (i, 0)
    kj = lambda i, j: (jnp.minimum(i, j), 0)
    return pl.pallas_call(
        body, name="fox_fwd", grid=(nq, nq),
        in_specs=[pl.BlockSpec((tq, AUG_W), qi), pl.BlockSpec((tq, AUG_W), kj), pl.BlockSpec((tq, AUG_W), kj)],
        out_specs=[pl.BlockSpec((tq, FOX_W), qi), pl.BlockSpec((tq, LANES), qi)],
        out_shape=[S((T, FOX_W), F32), S((T, LANES), F32)],
        scratch_shapes=[pltpu.VMEM((FOX_HEADS, tq, HB), F32), pltpu.VMEM((tq, AUG_W), F32)],
        compiler_params=_cp(2))(q, k, v)


def _fox_bwd(q, k, v, dob):
    T = q.shape[0]
    tq = _tile(T, 512)
    nq = T // tq
    n_sweeps = 1
    half = AUG_W // n_sweeps
    hpg = FOX_HEADS // n_sweeps

    pairs = [(j, i) for j in range(nq) for i in range(j, nq)]
    jt = jnp.asarray([p[0] for p in pairs], jnp.int32)
    it = jnp.asarray([p[1] for p in pairs], jnp.int32)

    def body(jt_ref, it_ref, q_ref, k_ref, v_ref, do_ref, dq_ref, dk_ref, dv_ref, dq_sc):
        t = pl.program_id(1)
        j, i = jt_ref[t], it_ref[t]

        @pl.when(t == 0)
        def _():
            dq_sc[...] = jnp.zeros_like(dq_sc)

        @pl.when(i == j)
        def _():
            dk_ref[...] = jnp.zeros_like(dk_ref)
            dv_ref[...] = jnp.zeros_like(dv_ref)

        def step(masked):
            rows = pl.ds(pl.multiple_of(i * tq, tq), tq)
            mask = _tri(tq, True) if masked else None
            for h in range(hpg):
                hb = slice(h * HB, (h + 1) * HB)
                qh, kh, vh, doh = q_ref[:, hb], k_ref[:, hb], v_ref[:, hb], do_ref[:, hb]
                s = _nt(qh, kh)
                if masked:
                    s = jnp.where(mask, s, NEG)
                p = jnp.exp(s)
                dsb = (p * _nt(doh, vh)).astype(BF)
                dv_ref[:, hb] += _tn(p.astype(BF), doh)
                dk_ref[:, hb] += _tn(dsb, qh)
                dq_sc[rows, hb] += _nn(dsb, kh)

        @pl.when(i > j)
        def _():
            step(False)

        @pl.when(i == j)
        def _():
            step(True)
            dq_ref[...] = dq_sc[pl.ds(pl.multiple_of(j * tq, tq), tq), :]

    qi = pl.BlockSpec((tq, half), lambda g, t, jt_ref, it_ref: (it_ref[t], g))
    kj = pl.BlockSpec((tq, half), lambda g, t, jt_ref, it_ref: (jt_ref[t], g))
    return pl.pallas_call(
        body, name="fox_bwd",
        grid_spec=pltpu.PrefetchScalarGridSpec(
            num_scalar_prefetch=2, grid=(n_sweeps, len(pairs)), in_specs=[qi, kj, kj, qi], out_specs=[kj, kj, kj],
            scratch_shapes=[pltpu.VMEM((T, half), F32)]),
        out_shape=[S((T, AUG_W), F32), S((T, AUG_W), F32), S((T, AUG_W), F32)],
        compiler_params=_cp(2))(jt, it, q, k, v, dob)


def _mix_out(attn, yg, g_fo, wout, x):
    T, D = x.shape
    tm = _tile(T, 1024)

    def body(a_ref, y_ref, g_ref, w_ref, x_ref, o_ref):
        at = a_ref[...]
        yf = (at * _rstd(at) * g_ref[...]).astype(BF)
        o_ref[...] = x_ref[...] + _nn(yf, w_ref[:FOX_W, :]) + _nn(y_ref[...], w_ref[FOX_W:, :])

    row = lambda i: (i, 0)
    return pl.pallas_call(
        body, name="mix_out", grid=(T // tm,),
        in_specs=[pl.BlockSpec((tm, FOX_W), row), pl.BlockSpec((tm, GMLP_W), row),
                  pl.BlockSpec((1, FOX_W), lambda i: (0, 0)), pl.BlockSpec((D, D), lambda i: (0, 0)),
                  pl.BlockSpec((tm, D), row)],
        out_specs=pl.BlockSpec((tm, D), row),
        out_shape=S((T, D), F32),
        compiler_params=_cp(1))(attn, yg, g_fo, wout, x)


def _mix_out_bwd(dx, attn, yg, g_fo, wout, qf, lse):
    T, D = dx.shape
    tm = _tile(T, 512)
    n = T // tm
    pc_l, pc_d = _piece_matrix(COL_C), _piece_matrix(COL_A)

    def body(dx_ref, a_ref, y_ref, g_ref, w_ref, qf_ref, lse_ref, pl_ref, pd_ref,
             qb_ref, dob_ref, dyg_ref, dw_ref, dg_ref, acc_ref, dsum_ref):
        i = pl.program_id(0)
        dxb = dx_ref[...].astype(BF)
        at = a_ref[...]
        yf = (at * _rstd(at) * g_ref[...]).astype(BF)
        dy = _nt(dxb, w_ref[...])
        p_top = _tn(yf, dxb)
        p_bot = _tn(y_ref[...], dxb)

        @pl.when(i == 0)
        def _():
            acc_ref[:FOX_W, :] = p_top
            acc_ref[FOX_W:, :] = p_bot

        @pl.when(i > 0)
        def _():
            acc_ref[:FOX_W, :] += p_top
            acc_ref[FOX_W:, :] += p_bot

        @pl.when(i == n - 1)
        def _():
            dw_ref[...] = acc_ref[...].astype(BF)

        dat, dgr = _norm_bwd(dy[:, :FOX_W], at, g_ref[...])
        _acc_rows(dg_ref, i == 0, dgr)
        dyg_ref[...] = dy[:, FOX_W:]
        prod = dat * at
        dsum_ref[...] = jnp.zeros_like(dsum_ref)
        for h in range(FOX_HEADS):
            dsum_ref[:, h:h + 1] = jnp.sum(prod[:, h * FOX_HD:(h + 1) * FOX_HD], axis=1, keepdims=True)
        ext_d = _nn(_pieces(-dsum_ref[...]), pd_ref[...]).astype(BF)
        ext_l = _nn(_pieces(-lse_ref[...]), pl_ref[...])
        datb = dat.astype(BF)
        for h in range(FOX_HEADS):
            lo, hi = slice(h * HB, h * HB + FOX_HD), slice(h * HB + FOX_HD, (h + 1) * HB)
            dob_ref[:, lo] = datb[:, h * FOX_HD:(h + 1) * FOX_HD]
            dob_ref[:, hi] = ext_d[:, hi]
            qb_ref[:, lo] = qf_ref[:, lo]
            qb_ref[:, hi] = (qf_ref[:, hi].astype(F32) + ext_l[:, hi]).astype(BF)

    row = lambda i: (i, 0)
    fix = lambda i: (0, 0)
    return pl.pallas_call(
        body, name="mix_out_bwd", grid=(n,),
        in_specs=[pl.BlockSpec((tm, D), row), pl.BlockSpec((tm, FOX_W), row), pl.BlockSpec((tm, GMLP_W), row),
                  pl.BlockSpec((1, FOX_W), fix), pl.BlockSpec((D, D), fix), pl.BlockSpec((tm, AUG_W), row),
                  pl.BlockSpec((tm, LANES), row), pl.BlockSpec((LANES, AUG_W), fix),
                  pl.BlockSpec((LANES, AUG_W), fix)],
        out_specs=[pl.BlockSpec((tm, AUG_W), row), pl.BlockSpec((tm, AUG_W), row), pl.BlockSpec((tm, GMLP_W), row),
                   pl.BlockSpec((D, D), fix), pl.BlockSpec((1, FOX_W), fix)],
        out_shape=[S((T, AUG_W), BF), S((T, AUG_W), BF), S((T, GMLP_W), F32), S((D, D), BF), S((1, FOX_W), F32)],
        scratch_shapes=[pltpu.VMEM((D, D), F32), pltpu.VMEM((tm, LANES), F32)],
        compiler_params=_cp(1))(dx, attn, yg, g_fo, wout, qf, lse, pc_l, pc_d)


def _mix_prep_bwd(z, dq, dk, dv, dyg, rs, bf128, g_q, g_k, g_sgu, w_s, b_st, g_go):
    T = z.shape[0]
    tm = _tile(T, 512)
    n = T // tm

    def body(z_ref, dq_ref, dk_ref, dv_ref, dyg_ref, rs_ref, bf_ref, gq_ref, gk_ref, gs_ref, ws_ref,
             bst_ref, go_ref, dz_ref, dgq_ref, dgk_ref, dgs_ref, dgo_ref, dws_ref, dbst_ref, dbf_ref, carry_ref):
        i = pl.program_id(0)
        first = i == 0
        rs = rs_ref[...]

        @pl.when(first)
        def _():
            carry_ref[...] = jnp.zeros_like(carry_ref)

        lane = lax.broadcasted_iota(jnp.int32, (tm, LANES), 1)
        dc = jnp.zeros((tm, LANES), F32)
        gq_rows, gk_rows = [], []
        for h in range(FOX_HEADS):
            hp = slice(h * HB, h * HB + FOX_HD)
            dqh, gqr = _norm_bwd(dq_ref[:, hp] * 0.125, z_ref[:, Z_Q + h * FOX_HD:Z_Q + (h + 1) * FOX_HD], gq_ref[...],
                                 rs[:, RS_Q + h:RS_Q + h + 1])
            dkh, gkr = _norm_bwd(dk_ref[:, hp], z_ref[:, Z_K + h * FOX_HD:Z_K + (h + 1) * FOX_HD], gk_ref[...],
                                 rs[:, RS_K + h:RS_K + h + 1])
            dz_ref[:, Z_Q + h * FOX_HD:Z_Q + (h + 1) * FOX_HD] = dqh.astype(BF)
            dz_ref[:, Z_K + h * FOX_HD:Z_K + (h + 1) * FOX_HD] = dkh.astype(BF)
            dz_ref[:, Z_V + h * FOX_HD:Z_V + (h + 1) * FOX_HD] = dv_ref[:, hp].astype(BF)
            dch = dq_ref[:, h * HB + COL_A:h * HB + COL_A + 1] - dk_ref[:, h * HB + COL_B:h * HB + COL_B + 1]
            dc = jnp.where(lane == h, dch, dc)
            gq_rows.append(gqr)
            gk_rows.append(gkr)
        _acc_rows(dgq_ref, first, functools.reduce(lambda a, b: a + b, gq_rows))
        _acc_rows(dgk_ref, first, functools.reduce(lambda a, b: a + b, gk_rows))

        dlogf = _hi3(_tri(tm, False), dc) + carry_ref[...]
        carry_ref[...] = dlogf[0:1, :]
        fl = z_ref[:, Z_F:Z_F + LANES] + bf_ref[...]
        lane = lax.broadcasted_iota(jnp.int32, (tm, LANES), 1)
        df = jnp.where(lane < FOX_HEADS, dlogf * jax.nn.sigmoid(-fl), 0.0)
        dz_ref[:, Z_F:Z_F + LANES] = df.astype(BF)
        _acc_rows(dbf_ref, first, df)

        u_pre = z_ref[:, Z_U:Z_U + GMLP_W]
        vg_pre = z_ref[:, Z_G:Z_G + GMLP_W]
        u = _gelu(u_pre)
        vg = _gelu(vg_pre)
        rv = rs[:, RS_V:RS_V + 1]
        vgn = (vg * rv * gs_ref[...]).astype(BF)
        bst = bst_ref[...]
        mixed, wms = _spatial_mix(vgn, ws_ref, bst, tm)
        sgu = u * mixed
        dsgu, gor = _norm_bwd(dyg_ref[...], sgu, go_ref[...], rs[:, RS_O:RS_O + 1])
        _acc_rows(dgo_ref, first, gor)
        du = dsgu * mixed
        dmixed = dsgu * u
        dmb = dmixed.astype(BF)
        tril = _tri(CHUNK, True)
        dvgn_rows = []
        dws = [None] * GMLP_G
        dbs = [None] * GMLP_G
        for c in range(tm // CHUNK):
            cs = slice(c * CHUNK, (c + 1) * CHUNK)
            cols = []
            for g in range(GMLP_G):
                gs = slice(g * GMLP_GD, (g + 1) * GMLP_GD)
                dmc = dmb[cs, gs]
                pw = _nt(dmc, vgn[cs, gs])
                pb = jnp.sum(dmixed[cs, gs], axis=1, keepdims=True)
                dws[g] = pw if dws[g] is None else dws[g] + pw
                dbs[g] = pb if dbs[g] is None else dbs[g] + pb
                cols.append(_tn(wms[g], dmc))
            dvgn_rows.append(jnp.concatenate(cols, axis=1))
        dvgn = jnp.concatenate(dvgn_rows, axis=0)
        dbs_t = jnp.concatenate(dbs, axis=1)
        for g in range(GMLP_G):
            dwg = jnp.where(tril, dws[g], 0.0)

            @pl.when(first)
            def _():
                dws_ref[g] = dwg

            @pl.when(jnp.logical_not(first))
            def _():
                dws_ref[g] += dwg

        @pl.when(first)
        def _():
            dbst_ref[...] = dbs_t

        @pl.when(jnp.logical_not(first))
        def _():
            dbst_ref[...] += dbs_t

        dvg, gsr = _norm_bwd(dvgn, vg, gs_ref[...], rv)
        _acc_rows(dgs_ref, first, gsr)
        dz_ref[:, Z_U:Z_U + GMLP_W] = (du * _gelu_grad(u_pre)).astype(BF)
        dz_ref[:, Z_G:Z_G + GMLP_W] = (dvg * _gelu_grad(vg_pre)).astype(BF)

    rev = lambda i: (n - 1 - i, 0)
    fix = lambda i: (0, 0)
    fix3 = lambda i: (0, 0, 0)
    return pl.pallas_call(
        body, name="mix_prep_bwd", grid=(n,),
        in_specs=[pl.BlockSpec((tm, ZW), rev), pl.BlockSpec((tm, AUG_W), rev), pl.BlockSpec((tm, AUG_W), rev),
                  pl.BlockSpec((tm, AUG_W), rev), pl.BlockSpec((tm, GMLP_W), rev), pl.BlockSpec((tm, LANES), rev),
                  pl.BlockSpec((1, LANES), fix), pl.BlockSpec((1, FOX_HD), fix), pl.BlockSpec((1, FOX_HD), fix),
                  pl.BlockSpec((1, GMLP_W), fix), pl.BlockSpec((GMLP_G, CHUNK, CHUNK), fix3),
                  pl.BlockSpec((CHUNK, GMLP_G), fix), pl.BlockSpec((1, GMLP_W), fix)],
        out_specs=[pl.BlockSpec((tm, ZW), rev), pl.BlockSpec((1, FOX_HD), fix), pl.BlockSpec((1, FOX_HD), fix),
                   pl.BlockSpec((1, GMLP_W), fix), pl.BlockSpec((1, GMLP_W), fix),
                   pl.BlockSpec((GMLP_G, CHUNK, CHUNK), fix3), pl.BlockSpec((CHUNK, GMLP_G), fix),
                   pl.BlockSpec((1, LANES), fix)],
        out_shape=[S((T, ZW), BF), S((1, FOX_HD), F32), S((1, FOX_HD), F32), S((1, GMLP_W), F32), S((1, GMLP_W), F32),
                   S((GMLP_G, CHUNK, CHUNK), F32), S((CHUNK, GMLP_G), F32), S((1, LANES), F32)],
        scratch_shapes=[pltpu.VMEM((1, LANES), F32)],
        compiler_params=_cp(1))(z, dq, dk, dv, dyg, rs, bf128, g_q, g_k, g_sgu, w_s, b_st, g_go)


def _mix_proj_bwd(dz, wz, x, g, dy):
    T, D = x.shape
    tm = _tile(T, 512)

    def body(dz_ref, w_ref, x_ref, g_ref, dy_ref, dx_ref, dxb_ref, dg_ref):
        dh = _nn(dz_ref[...], w_ref[...])
        dx, dgr = _norm_bwd(dh, x_ref[...], g_ref[...])
        dx = dx + dy_ref[...]
        dx_ref[...] = dx
        dxb_ref[...] = dx.astype(BF)
        _acc_rows(dg_ref, pl.program_id(0) == 0, dgr)

    row = lambda i: (i, 0)
    fix = lambda i: (0, 0)
    return pl.pallas_call(
        body, name="mix_proj_bwd", grid=(T // tm,),
        in_specs=[pl.BlockSpec((tm, ZW), row), pl.BlockSpec((ZW, D), fix), pl.BlockSpec((tm, D), row),
                  pl.BlockSpec((1, D), fix), pl.BlockSpec((tm, D), row)],
        out_specs=[pl.BlockSpec((tm, D), row), pl.BlockSpec((tm, D), row), pl.BlockSpec((1, D), fix)],
        out_shape=[S((T, D), F32), S((T, D), BF), S((1, D), F32)],
        compiler_params=_cp(1))(dz, wz, x, g, dy)


def _ca_kv(mem, g_mem, wckv, g_ck):
    M, D = mem.shape

    def body(m_ref, g_ref, w_ref, gk_ref, mn_ref, kr_ref, kn_ref, v_ref):
        mf = m_ref[...]
        mn = (mf * _rstd(mf) * g_ref[...]).astype(BF)
        mn_ref[...] = mn
        for h in range(CA_HEADS):
            kr = _nn(mn, w_ref[h])
            kr_ref[h] = kr
            kn_ref[h] = (kr * _rstd(kr) * gk_ref[...]).astype(BF)
            v_ref[h] = _nn(mn, w_ref[CA_HEADS + h]).astype(BF)

    hd = (CA_HEADS, M, CA_HD)
    return pl.pallas_call(
        body, name="ca_kv", out_shape=[S((M, D), BF), S(hd, F32), S(hd, BF), S(hd, BF)],
        compiler_params=pltpu.CompilerParams(vmem_limit_bytes=VMEM_LIMIT))(mem, g_mem, wckv, g_ck)


def _ca_tile_fwd(xt, gca, wcq, gcq, kn_ref, v_ref):
    hb = (xt * _rstd(xt) * gca).astype(BF)
    qc = _nn(hb, wcq)
    qr, qn, ps = [], [], []
    for h in range(CA_HEADS):
        qh = qc[:, h * CA_HD:(h + 1) * CA_HD]
        qnh = (qh * _rstd(qh) * gcq * 0.0625).astype(BF)
        s = _nt(qnh, kn_ref[h])
        e = jnp.exp(s - jnp.max(s, axis=1, keepdims=True))
        ps.append(e / jnp.sum(e, axis=1, keepdims=True))
        qr.append(qh)
        qn.append(qnh)
    return hb, qr, qn, ps


def _ca_fwd(x, g_ca, wcq, g_cq, kn, vv, wco):
    T, D = x.shape
    M = kn.shape[1]
    tm = _tile(T, 1024)

    def body(x_ref, gca_ref, wcq_ref, gcq_ref, kn_ref, v_ref, wco_ref, o_ref, ob_sc):
        xt = x_ref[...]
        _, _, _, ps = _ca_tile_fwd(xt, gca_ref[...], wcq_ref[...], gcq_ref[...], kn_ref, v_ref)
        for h in range(CA_HEADS):
            ob_sc[:, h * CA_HD:(h + 1) * CA_HD] = _nn(ps[h].astype(BF), v_ref[h]).astype(BF)
        o_ref[...] = xt + _nn(ob_sc[...], wco_ref[...])

    row = lambda i: (i, 0)
    fix = lambda i: (0, 0)
    fix3 = lambda i: (0, 0, 0)
    return pl.pallas_call(
        body, name="ca_fwd", grid=(T // tm,),
        in_specs=[pl.BlockSpec((tm, D), row), pl.BlockSpec((1, D), fix), pl.BlockSpec((D, D), fix),
                  pl.BlockSpec((1, CA_HD), fix), pl.BlockSpec((CA_HEADS, M, CA_HD), fix3),
                  pl.BlockSpec((CA_HEADS, M, CA_HD), fix3), pl.BlockSpec((D, D), fix)],
        out_specs=pl.BlockSpec((tm, D), row), out_shape=S((T, D), F32),
        scratch_shapes=[pltpu.VMEM((tm, D), BF)],
        compiler_params=_cp(1))(x, g_ca, wcq, g_cq, kn, vv, wco)


def _ca_bwd(x, dy, g_ca, wcq, g_cq, kn, vv, wco):
    T, D = x.shape
    M = kn.shape[1]
    tm = _tile(T, 512)
    n = T // tm

    def body(x_ref, dy_ref, gca_ref, wcq_ref, gcq_ref, kn_ref, v_ref, wco_ref,
             dx_ref, dwq_ref, dwo_ref, dkn_ref, dv_ref, dgcq_ref, dgca_ref, aq_sc, ao_sc, ob_sc, dq_sc):
        i = pl.program_id(0)
        first = i == 0
        xt = x_ref[...]
        dyt = dy_ref[...]
        dyb = dyt.astype(BF)
        hb, qr, qn, ps = _ca_tile_fwd(xt, gca_ref[...], wcq_ref[...], gcq_ref[...], kn_ref, v_ref)
        do = _nt(dyb, wco_ref[...])
        gcq_rows = None
        for h in range(CA_HEADS):
            hs = slice(h * CA_HD, (h + 1) * CA_HD)
            p = ps[h]
            pb = p.astype(BF)
            ob_sc[:, hs] = _nn(pb, v_ref[h]).astype(BF)
            doh = do[:, hs].astype(BF)
            dp = _nt(doh, v_ref[h])
            ds = (p * (dp - jnp.sum(dp * p, axis=1, keepdims=True))).astype(BF)
            dvh = _tn(pb, doh)
            dkh = _tn(ds, qn[h])

            @pl.when(first)
            def _():
                dv_ref[h] = dvh
                dkn_ref[h] = dkh

            @pl.when(jnp.logical_not(first))
            def _():
                dv_ref[h] += dvh
                dkn_ref[h] += dkh

            dqn = _nn(ds, kn_ref[h]) * 0.0625
            dqh, gr = _norm_bwd(dqn, qr[h], gcq_ref[...])
            gcq_rows = gr if gcq_rows is None else gcq_rows + gr
            dq_sc[:, hs] = dqh.astype(BF)
        _acc_rows(dgcq_ref, first, gcq_rows)
        dqb = dq_sc[...]
        p_o = _tn(ob_sc[...], dyb)
        p_q = _tn(hb, dqb)

        @pl.when(first)
        def _():
            ao_sc[...] = p_o
            aq_sc[...] = p_q

        @pl.when(jnp.logical_not(first))
        def _():
            ao_sc[...] += p_o
            aq_sc[...] += p_q

        @pl.when(i == n - 1)
        def _():
            dwo_ref[...] = ao_sc[...].astype(BF)
            dwq_ref[...] = aq_sc[...].astype(BF)

        dh = _nt(dqb, wcq_ref[...])
        dx, gar = _norm_bwd(dh, xt, gca_ref[...])
        dx_ref[...] = dx + dyt
        _acc_rows(dgca_ref, first, gar)

    row = lambda i: (i, 0)
    fix = lambda i: (0, 0)
    fix3 = lambda i: (0, 0, 0)
    hd = (CA_HEADS, M, CA_HD)
    return pl.pallas_call(
        body, name="ca_bwd", grid=(n,),
        in_specs=[pl.BlockSpec((tm, D), row), pl.BlockSpec((tm, D), row), pl.BlockSpec((1, D), fix),
                  pl.BlockSpec((D, D), fix), pl.BlockSpec((1, CA_HD), fix), pl.BlockSpec(hd, fix3),
                  pl.BlockSpec(hd, fix3), pl.BlockSpec((D, D), fix)],
        out_specs=[pl.BlockSpec((tm, D), row), pl.BlockSpec((D, D), fix), pl.BlockSpec((D, D), fix),
                   pl.BlockSpec(hd, fix3), pl.BlockSpec(hd, fix3), pl.BlockSpec((1, CA_HD), fix),
                   pl.BlockSpec((1, D), fix)],
        out_shape=[S((T, D), F32), S((D, D), BF), S((D, D), BF), S(hd, F32), S(hd, F32), S((1, CA_HD), F32),
                   S((1, D), F32)],
        scratch_shapes=[pltpu.VMEM((D, D), F32), pltpu.VMEM((D, D), F32), pltpu.VMEM((tm, D), BF),
                        pltpu.VMEM((tm, D), BF)],
        compiler_params=_cp(1))(x, dy, g_ca, wcq, g_cq, kn, vv, wco)


def _ca_kv_bwd(mem, g_mem, mn, kraw, dkn, dvv, wckv, g_ck):
    M, D = mem.shape

    def body(m_ref, g_ref, mn_ref, kr_ref, dkn_ref, dv_ref, w_ref, gk_ref, dw_ref, dgk_ref, dgm_ref):
        mn = mn_ref[...]
        dmn = jnp.zeros((M, D), F32)
        gk_rows = None
        for h in range(CA_HEADS):
            dkr, gr = _norm_bwd(dkn_ref[h], kr_ref[h], gk_ref[...])
            gk_rows = gr if gk_rows is None else gk_rows + gr
            dkb = dkr.astype(BF)
            dvb = dv_ref[h].astype(BF)
            dw_ref[h] = _tn(mn, dkb).astype(BF)
            dw_ref[CA_HEADS + h] = _tn(mn, dvb).astype(BF)
            dmn = dmn + _nt(dkb, w_ref[h]) + _nt(dvb, w_ref[CA_HEADS + h])
        dgk_ref[...] = jnp.sum(gk_rows, axis=0, keepdims=True)
        mf = m_ref[...]
        dgm_ref[...] = jnp.sum(dmn * (mf * _rstd(mf)), axis=0, keepdims=True)

    return pl.pallas_call(
        body, name="ca_kv_bwd",
        out_shape=[S((2 * CA_HEADS, D, CA_HD), BF), S((1, CA_HD), F32), S((1, D), F32)],
        compiler_params=pltpu.CompilerParams(vmem_limit_bytes=VMEM_LIMIT))(mem, g_mem, mn, kraw, dkn, dvv, wckv, g_ck)


def _after(g, token):
    return g if token is None else g + token[0:1, 0:1]


def _local_step(x, mem, target, small, weights, emit, h1=None):
    T, D = x.shape
    p = small
    bf128 = jnp.pad(p["b_f"], ((0, 0), (0, LANES - FOX_HEADS)))
    b_st = p["b_s"].T

    wup1 = weights("ffn1_up", x if h1 is None else h1)["wup1"]
    if h1 is None:
        a1, h1 = _ffn_up("ffn1_up", x, p["g_ffn1"], wup1)
    else:
        a1 = _ffn_up_from_h("ffn1_up", h1, wup1)
    wdn1 = weights("ffn1_dn", h1)["wdn1"]
    x1 = _ffn_down("ffn1_down", a1, wdn1, x)
    wm = weights("mix", x1)
    z, h2, qf, ka, va, yg, rs = _mix_prep(x1, p["g_mix"], wm["wz"], bf128, p["g_q"], p["g_k"], p["g_sgu"], p["w_s"],
                                          b_st, p["g_gmlp_o"])
    attn, lse = _fox_fwd(qf, ka, va)
    x2 = _mix_out(attn, yg, p["g_fox_o"], wm["wout"], x1)
    wc = weights("ca", x2)
    mn, kraw, ckn, cvv = _ca_kv(mem, p["g_mem"], wc["wckv"], p["g_ck"])
    x3 = _ca_fwd(x2, p["g_ca"], wc["wcq"], p["g_cq"], ckn, cvv, wc["wco"])
    w2 = weights("ffn2", x3)
    a2, h4 = _ffn_up("ffn2_up", x3, p["g_ffn2"], w2["wup2"])
    dy4, dy4b, sq = _ffn_down_loss("ffn2_down", a2, w2["wdn2"], x3, target)

    gs = {}
    dgu2 = _ffn_bwd_act("ffn2_bwd_act", dy4b, h4, w2["wup2"], w2["wdn2"])
    tok = emit("ffn2", {"wup2": _ffn_dwup("ffn2", h4, dgu2), "wdn2": _ffn_dwdn("ffn2", a2, dy4b)})
    dx3, gs["g_ffn2"] = _ffn_dx("ffn2_dx", dgu2, w2["wup2"], x3, _after(p["g_ffn2"], tok), dy4)

    dx2, dwcq, dwco, dckn, dcvv, gs["g_cq"], gs["g_ca"] = _ca_bwd(
        x2, dx3, p["g_ca"], wc["wcq"], p["g_cq"], ckn, cvv, wc["wco"])
    dwckv, gs["g_ck"], gs["g_mem"] = _ca_kv_bwd(mem, p["g_mem"], mn, kraw, dckn, dcvv, wc["wckv"], p["g_ck"])

    qb, dob, dyg, dwout, gs["g_fox_o"] = _mix_out_bwd(dx2, attn, yg, p["g_fox_o"], wm["wout"], qf, lse)
    dq, dk, dv = _fox_bwd(qb, ka, va, dob)
    dz, gs["g_q"], gs["g_k"], gs["g_sgu"], gs["g_gmlp_o"], gs["w_s"], dbst, dbf = _mix_prep_bwd(
        z, dq, dk, dv, dyg, rs, bf128, p["g_q"], p["g_k"], p["g_sgu"], p["w_s"], b_st, p["g_gmlp_o"])
    gs["b_s"] = dbst.T
    gs["b_f"] = dbf[:, :FOX_HEADS]
    tok_ws = emit("w_s", {"w_s": gs["w_s"]})
    zb = ZW // 3
    dwz = _tn_matmul("mix_dwz", dz, pl.BlockSpec((T, zb), lambda j: (0, j)), h2,
                     S((ZW, D), BF), pl.BlockSpec((zb, D), lambda j: (j, 0)), 3)
    tok = emit("mid", {"wcq": dwcq, "wco": dwco, "wckv": dwckv, "wout": dwout, "wz": dwz})
    dx1, dx1b, gs["g_mix"] = _mix_proj_bwd(dz, wm["wz"], x1, _after(_after(p["g_mix"], tok), tok_ws), dx2)

    dgu1 = _ffn_bwd_act("ffn1_bwd_act", dx1b, h1, wup1, wdn1)
    tok = emit("ffn1_dn", {"wdn1": _ffn_dwdn("ffn1", a1, dx1b)})
    tok = emit("ffn1_up", {"wup1": _ffn_dwup("ffn1", h1, dgu1, after=tok)})
    dx0, gs["g_ffn1"] = _ffn_dx("ffn1_dx", dgu1, wup1, x, _after(p["g_ffn1"], tok), dx1)
    return sq, dx0, gs


MESH = pl.DeviceIdType.MESH
HBM_SPEC = pl.BlockSpec(memory_space=pltpu.HBM)
N_PEER = N_DEV - 1


def _place():
    return lax.axis_index("x"), lax.axis_index("y"), lax.axis_index("c")


def _slot(px, py, pc):
    return 4 * px + 2 * py + pc


SEM_SPEC = pl.BlockSpec(memory_space=pltpu.SEMAPHORE)
ANY_SPEC = pl.BlockSpec(memory_space=pl.ANY)
DATAFLOW = pltpu.SideEffectType.DATAFLOW_SIDE_EFFECTING


def _hbm(a):
    return pltpu.with_memory_space_constraint(a, pltpu.HBM)


def _peer(x, y, c, r):
    return (1 - x if r & 4 else x, 1 - y if r & 2 else y, 1 - c if r & 1 else c)


def _place_own(srcs, whole):
    my = _slot(*_place())
    lands = []
    for s in srcs:
        blk = s[None] if whole else lax.dynamic_slice_in_dim(s, my, 1, 0)
        shape = (N_DEV,) + s.shape if whole else s.shape
        lands.append(lax.dynamic_update_slice_in_dim(lax.empty(shape, s.dtype), blk, my, 0))
    return lands


ALL_PEERS = tuple(range(1, N_DEV))
NEAR_PEERS = (1, 2, 4, 6)
SAME_CORE = (2, 4, 6)


def _copy_start(name, srcs, lands, whole, peers=None):
    n = len(srcs)
    peers = peers or [ALL_PEERS] * n
    wh = list(whole) if isinstance(whole, (list, tuple)) else [whole] * n

    def body(*refs):
        src, land = refs[:n], refs[n:2 * n]
        send, recv = refs[2 * n:3 * n], refs[3 * n:4 * n]
        token = refs[6 * n]
        x, y, c = _place()
        my = _slot(x, y, c)
        for a in range(n):
            for r in peers[a]:
                p = _peer(x, y, c, r)
                pltpu.make_async_remote_copy(
                    src_ref=src[a] if wh[a] else src[a].at[_slot(*p)], dst_ref=land[a].at[my],
                    send_sem=send[a].at[r - 1], recv_sem=recv[a].at[r - 1], device_id=p, device_id_type=MESH).start()
        token[...] = jnp.zeros_like(token)

    out = pl.pallas_call(
        body, name=name,
        out_shape=([pltpu.SemaphoreType.DMA((N_PEER,))] * (2 * n)
                   + [pltpu.HBM(s.shape, s.dtype) for s in srcs] + [pltpu.HBM(s.shape, s.dtype) for s in lands]
                   + [S((8, LANES), F32)]),
        in_specs=[HBM_SPEC] * (2 * n),
        out_specs=[SEM_SPEC] * (2 * n) + [HBM_SPEC] * (2 * n) + [pl.BlockSpec(memory_space=pltpu.VMEM)],
        input_output_aliases={i: 2 * n + i for i in range(2 * n)},
        compiler_params=pltpu.CompilerParams(has_side_effects=DATAFLOW),
    )(*[_hbm(s) for s in srcs], *[_hbm(s) for s in lands])
    return out[:n], out[n:2 * n], out[2 * n:3 * n], out[3 * n:4 * n], out[4 * n]


def _copy_wait(name, srcs, lands, send, recv, after, whole, peers=None, with_srcs=False):
    n = len(srcs)
    peers = peers or [ALL_PEERS] * n
    wh = list(whole) if isinstance(whole, (list, tuple)) else [whole] * n

    def body(*refs):
        src, land = refs[:n], refs[n:2 * n]
        snd, rcv = refs[2 * n:3 * n], refs[3 * n:4 * n]
        x, y, c = _place()
        for a in range(n):
            for r in peers[a]:
                p = _peer(x, y, c, r)
                ps = _slot(*p)
                cp = pltpu.make_async_remote_copy(
                    src_ref=src[a] if wh[a] else src[a].at[ps], dst_ref=land[a].at[ps],
                    send_sem=snd[a].at[r - 1], recv_sem=rcv[a].at[r - 1], device_id=p, device_id_type=MESH)
                cp.wait_send()
                cp.wait_recv()

    out = pl.pallas_call(
        body, name=name,
        out_shape=[pltpu.HBM(s.shape, s.dtype) for s in srcs] + [pltpu.HBM(s.shape, s.dtype) for s in lands],
        in_specs=[HBM_SPEC] * (2 * n) + [SEM_SPEC] * (2 * n) + [ANY_SPEC],
        out_specs=[HBM_SPEC] * (2 * n),
        input_output_aliases={i: i for i in range(2 * n)},
        compiler_params=pltpu.CompilerParams(has_side_effects=DATAFLOW),
    )(*srcs, *lands, *send, *recv, after)
    return (out[:n], out[n:]) if with_srcs else out[n:]


def _forward_start(name, lands):
    n = len(lands)

    def body(*refs):
        land = refs[:n]
        send, recv = refs[n:2 * n], refs[2 * n:3 * n]
        token = refs[4 * n]
        x, y, c = _place()
        for a in range(n):
            for r in SAME_CORE:
                blk = land[a].at[_slot(*_peer(x, y, c, r))]
                pltpu.make_async_remote_copy(
                    src_ref=blk, dst_ref=blk, send_sem=send[a].at[r - 1], recv_sem=recv[a].at[r - 1],
                    device_id=(x, y, 1 - c), device_id_type=MESH).start()
        token[...] = jnp.zeros_like(token)

    out = pl.pallas_call(
        body, name=name,
        out_shape=([pltpu.SemaphoreType.DMA((N_PEER,))] * (2 * n) + [pltpu.HBM(s.shape, s.dtype) for s in lands]
                   + [S((8, LANES), F32)]),
        in_specs=[HBM_SPEC] * n,
        out_specs=[SEM_SPEC] * (2 * n) + [HBM_SPEC] * n + [pl.BlockSpec(memory_space=pltpu.VMEM)],
        input_output_aliases={i: 2 * n + i for i in range(n)},
        compiler_params=pltpu.CompilerParams(has_side_effects=DATAFLOW),
    )(*[_hbm(s) for s in lands])
    return out[:n], out[n:2 * n], out[2 * n:3 * n], out[3 * n]


def _forward_wait(name, lands, send, recv, after):
    n = len(lands)

    def body(*refs):
        land = refs[:n]
        snd, rcv = refs[n:2 * n], refs[2 * n:3 * n]
        x, y, c = _place()
        for a in range(n):
            for r in SAME_CORE:
                cp = pltpu.make_async_remote_copy(
                    src_ref=land[a].at[_slot(*_peer(x, y, c, r))], dst_ref=land[a].at[_slot(*_peer(x, y, c, r | 1))],
                    send_sem=snd[a].at[r - 1], recv_sem=rcv[a].at[r - 1], device_id=(x, y, 1 - c),
                    device_id_type=MESH)
                cp.wait_send()
                cp.wait_recv()

    return pl.pallas_call(
        body, name=name,
        out_shape=[pltpu.HBM(s.shape, s.dtype) for s in lands],
        in_specs=[HBM_SPEC] * n + [SEM_SPEC] * (2 * n) + [ANY_SPEC],
        out_specs=[HBM_SPEC] * n,
        input_output_aliases={i: i for i in range(n)},
        compiler_params=pltpu.CompilerParams(has_side_effects=DATAFLOW),
    )(*lands, *send, *recv, after)


def _adamw(w, g, m, v):
    m2 = ADAM_B1 * m + (1.0 - ADAM_B1) * g
    v2 = ADAM_B2 * v + (1.0 - ADAM_B2) * (g * g)
    m_hat = m2 / (1.0 - ADAM_B1 ** ADAM_STEP)
    v_hat = v2 / (1.0 - ADAM_B2 ** ADAM_STEP)
    delta = -ADAM_LR * (m_hat / (jnp.sqrt(v_hat) + ADAM_EPS) + ADAM_WD * w)
    return delta, m2, v2


def _adamw_big(name, slots, w, m, v, own=None):
    R, C = w.shape
    tr = next((t for t in (128, 176, 64) if R % t == 0 and R // t >= 2), R)

    def finish(g, w_ref, m_ref, v_ref, g_ref, d_ref, m2_ref, v2_ref):
        d, m2, v2 = _adamw(w_ref[...], g, m_ref[...], v_ref[...])
        g_ref[...] = g
        d_ref[...] = d
        m2_ref[...] = m2
        v2_ref[...] = v2

    if own is None:
        def body(s_ref, *refs):
            g = s_ref[0].astype(F32)
            for k in range(1, N_DEV):
                g = g + s_ref[k].astype(F32)
            finish(g, *refs)

        row = pl.BlockSpec((tr, C), lambda i: (i, 0))
        return pl.pallas_call(
            body, name=name, grid=(R // tr,),
            in_specs=[pl.BlockSpec((N_DEV, tr, C), lambda i: (0, i, 0)), row, row, row],
            out_specs=[row] * 4, out_shape=[S((R, C), F32)] * 4,
            compiler_params=_cp(1))(slots, w, m, v)

    def body(my_ref, s_ref, own_ref, *refs):
        mine = own_ref[...]
        g = None
        for k in range(N_DEV):
            part = jnp.where(my_ref[0] == k, mine, s_ref[k]).astype(F32)
            g = part if g is None else g + part
        finish(g, *refs)

    row = pl.BlockSpec((tr, C), lambda i, my_ref: (i, 0))
    my = jnp.reshape(_slot(*_place()), (1,)).astype(jnp.int32)
    return pl.pallas_call(
        body, name=name,
        grid_spec=pltpu.PrefetchScalarGridSpec(
            num_scalar_prefetch=1, grid=(R // tr,),
            in_specs=[pl.BlockSpec((N_DEV, tr, C), lambda i, my_ref: (0, i, 0)),
                      pl.BlockSpec((None, tr, C), lambda i, my_ref: (my_ref[0], i, 0)), row, row, row],
            out_specs=[row] * 4),
        out_shape=[S((R, C), F32)] * 4, compiler_params=_cp(1))(my, slots, own, w, m, v)


TINY_ROWS = (("b_s", 8), ("g_ffn1", 8), ("g_mix", 8), ("g_ca", 8), ("g_mem", 8), ("g_ffn2", 8), ("g_sgu", 4),
             ("g_fox_o", 4), ("g_gmlp_o", 4), ("g_cq", 2), ("g_ck", 2), ("g_q", 1), ("g_k", 1), ("b_f", 1),
             ("loss", 1))
TINY_P = 72


def _tiny_pieces(width):
    return [(j, slice(j * LANES, min((j + 1) * LANES, width))) for j in range(-(-width // LANES))]


def _pack_tiny(grads, sq):
    names = [n for n, _ in TINY_ROWS if n != "loss"]

    def body(*refs):
        ins, sq_ref, o_ref = refs[:len(names)], refs[len(names)], refs[len(names) + 1]
        o_ref[...] = jnp.zeros_like(o_ref)
        at = 0
        for ref, (name, r) in zip(ins, TINY_ROWS):
            if name == "b_s":
                o_ref[at:at + r, :] = ref[...]
            else:
                for j, cols in _tiny_pieces(ref.shape[1]):
                    o_ref[at + j:at + j + 1, 0:cols.stop - cols.start] = ref[:, cols]
            at += r
        o_ref[at:at + 1, :] = sq_ref[0:1, :]

    return pl.pallas_call(body, name="tiny_pack", out_shape=S((TINY_P, LANES), F32))(
        *[grads[n] for n in names], sq)


def _adamw_tiny(slots, w, m, v):
    names = [n for n, _ in TINY_ROWS if n != "loss"]
    k = len(names)

    def body(s_ref, *refs):
        ins, outs, loss_ref = refs[:3 * k], refs[3 * k:7 * k], refs[7 * k]
        g_all = s_ref[0]
        for d in range(1, N_DEV):
            g_all = g_all + s_ref[d]
        at = 0
        for i, (name, r) in enumerate(TINY_ROWS[:k]):
            w_ref, m_ref, v_ref = ins[i], ins[k + i], ins[2 * k + i]
            o = outs[4 * i:4 * i + 4]
            if name == "b_s":
                pieces = [(slice(at, at + r), slice(0, LANES), (slice(None), slice(None)))]
            else:
                pieces = [(slice(at + j, at + j + 1), slice(0, c.stop - c.start), (slice(None), c))
                          for j, c in _tiny_pieces(w_ref.shape[1])]
            for rows, lanes, dst in pieces:
                g = g_all[rows, lanes]
                res = (g,) + _adamw(w_ref[dst], g, m_ref[dst], v_ref[dst])
                for ref, val in zip(o, res):
                    ref[dst] = val
            at += r
        loss_ref[...] = g_all[at:at + 1, :]

    shapes = [S(w[n].shape, F32) for n in names]
    out = pl.pallas_call(
        body, name="adamw_tiny", out_shape=[s for s in shapes for _ in range(4)] + [S((1, LANES), F32)],
    )(slots, *[w[n] for n in names], *[m[n] for n in names], *[v[n] for n in names])
    stores = ({}, {}, {}, {})
    for i, n in enumerate(names):
        for store, t in zip(stores, out[4 * i:4 * i + 4]):
            store[n] = t
    return stores, out[4 * k]


WEIGHTS =('g_ffn1', 'w_ffn1_in', 'w_ffn1_out', 'g_mix', 'w_in', 'b_f', 'g_q', 'g_k', 'g_sgu', 'w_s', 'b_s',
           'g_fox_o', 'g_gmlp_o', 'w_out', 'g_ca', 'g_mem', 'w_cq', 'w_ckv', 'g_cq', 'g_ck', 'w_co', 'g_ffn2',
           'w_ffn2_in', 'w_ffn2_out')
BIG = ('w_ffn1_in', 'w_ffn1_out', 'w_in', 'w_out', 'w_cq', 'w_ckv', 'w_co', 'w_ffn2_in', 'w_ffn2_out')
TRANSPOSED = ('w_ffn1_in', 'w_in', 'w_ffn2_in')
TWO_LEVEL = ('w_ffn1_in', 'w_in')
GATHER_GROUPS = {"ffn1_up": ("w_ffn1_in",), "ffn1_dn": ("w_ffn1_out",), "mix": ("w_in", "w_out"),
                 "ca": ("w_cq", "w_ckv", "w_co"), "ffn2": ("w_ffn2_in", "w_ffn2_out")}
QKV_W = 3 * FOX_W
UV_OFF = QKV_W + FOX_HEADS


def kernel(x, mem, g_ffn1, w_ffn1_in, w_ffn1_out, g_mix, w_in, b_f, g_q, g_k, g_sgu, w_s, b_s, g_fox_o, g_gmlp_o, w_out, g_ca, g_mem, w_cq, w_ckv, g_cq, g_ck, w_co, g_ffn2, w_ffn2_in, w_ffn2_out, loss_target, m_g_ffn1, m_w_ffn1_in, m_w_ffn1_out, m_g_mix, m_w_in, m_b_f, m_g_q, m_g_k, m_g_sgu, m_w_s, m_b_s, m_g_fox_o, m_g_gmlp_o, m_w_out, m_g_ca, m_g_mem, m_w_cq, m_w_ckv, m_g_cq, m_g_ck, m_w_co, m_g_ffn2, m_w_ffn2_in, m_w_ffn2_out, v_g_ffn1, v_w_ffn1_in, v_w_ffn1_out, v_g_mix, v_w_in, v_b_f, v_g_q, v_g_k, v_g_sgu, v_w_s, v_b_s, v_g_fox_o, v_g_gmlp_o, v_w_out, v_g_ca, v_g_mem, v_w_cq, v_w_ckv, v_g_cq, v_g_ck, v_w_co, v_g_ffn2, v_w_ffn2_in, v_w_ffn2_out):
    args = dict(locals())
    w = {n: args[n] for n in WEIGHTS}
    mo = {n: args["m_" + n] for n in WEIGHTS}
    vo = {n: args["v_" + n] for n in WEIGHTS}
    D = D_MODEL

    def local(n, a):
        return a[0].T if n in TRANSPOSED else a[0]

    g_peers = [NEAR_PEERS if n in TWO_LEVEL else ALL_PEERS for n in BIG]
    handles = {}

    def start_gather(name, names, arrays):
        snd, rcv, src, land, token = _copy_start(name, arrays, _place_own(arrays, True), True,
                                                 peers=[g_peers[BIG.index(n)] for n in names])
        handles.update({n: (src[i], land[i], snd[i], rcv[i]) for i, n in enumerate(names)})
        return token

    first = local(BIG[0], w[BIG[0]]).astype(BF)
    fb = first.shape[0]
    token_first = start_gather("gather_start_first", BIG[:1], [first])
    token_rest = start_gather("gather_start_rest", BIG[1:],
                              [(local(n, w[n]) + token_first[0:1, 0:1]).astype(BF) for n in BIG[1:]])

    tiny_names = [n for n, _ in TINY_ROWS if n != "loss"]

    def weights(group, after):
        names = GATHER_GROUPS[group]
        hs = [handles[n] for n in names]
        got = list(_copy_wait("gather_wait_" + group, [h[0] for h in hs], [h[1] for h in hs], [h[2] for h in hs],
                              [h[3] for h in hs], after, True,
                              peers=[g_peers[BIG.index(n)] for n in names]))
        passed = [i for i, n in enumerate(names) if n in TWO_LEVEL]
        if passed:
            f_snd, f_rcv, f_land, f_token = _forward_start("gather_pass_start_" + group, [got[i] for i in passed])
            for i, t in zip(passed, _forward_wait("gather_pass_wait_" + group, f_land, f_snd, f_rcv, f_token)):
                got[i] = t
        got = dict(zip(names, got))
        if group == "ffn1_up":
            return {"wup1": got["w_ffn1_in"].reshape(2, N_FFN_BLK, fb, D)}
        if group == "ffn1_dn":
            return {"wdn1": got["w_ffn1_out"].reshape(N_FFN_BLK, fb, D)}
        if group == "mix":
            full = got["w_in"].reshape(-1, D)
            wz = jnp.concatenate([full[:QKV_W], full[UV_OFF:], full[QKV_W:UV_OFF],
                                  jnp.zeros((LANES - FOX_HEADS, D), BF)], axis=0)
            return {"wz": wz, "wout": got["w_out"].reshape(D, D)}
        if group == "ca":
            return {"wcq": got["w_cq"].reshape(D, D), "wco": got["w_co"].reshape(D, D), "wckv": got["w_ckv"]}
        return {"wup2": got["w_ffn2_in"].reshape(2, N_FFN_BLK, fb, D),
                "wdn2": got["w_ffn2_out"].reshape(N_FFN_BLK, fb, D)}

    flying = {}

    def emit(group, g):
        if group == "w_s":
            flying[group] = g["w_s"].reshape(-1, LANES)
            return None
        if group == "ffn2":
            parts = {"w_ffn2_in": g["wup2"], "w_ffn2_out": g["wdn2"].reshape(N_DEV, -1, D)}
        elif group == "ffn1_dn":
            parts = {"w_ffn1_out": g["wdn1"].reshape(N_DEV, -1, D)}
        elif group == "ffn1_up":
            parts = {"w_ffn1_in": g["wup1"]}
        else:
            gz = g["wz"]
            g_in = jnp.concatenate([gz[:QKV_W], gz[Z_F:Z_F + FOX_HEADS], gz[QKV_W:Z_F]], axis=0)
            parts = {"w_in": g_in.reshape(N_DEV, -1, D).astype(BF),
                     "w_out": g["wout"].reshape(N_DEV, -1, D), "w_cq": g["wcq"].reshape(N_DEV, -1, D),
                     "w_co": g["wco"].reshape(N_DEV, -1, D), "w_ckv": g["wckv"]}
        names = list(parts)
        srcs = [parts[n] for n in names]
        lands = [lax.empty(s.shape, s.dtype) for s in srcs]
        whole = [False] * len(srcs)
        if group == "mid":
            ws_part = flying.pop("w_s")
            names, srcs, whole = names + ["w_s"], srcs + [ws_part], whole + [True]
            lands += _place_own([ws_part], True)
        *copies, token = _copy_start("exchange_start_" + group, srcs, lands, whole)
        flying[group] = (names, copies, whole)
        return token

    small = {n: (w[n][0] if n == "b_s" else w[n]) for n in tiny_names}
    small["w_s"] = w["w_s"][0]

    h1 = _rms_cast("ffn1_norm", x[0], w["g_ffn1"], token_rest)
    sq, dx0, gs = _local_step(x[0], mem[0], loss_target[0], small, weights, emit, h1=h1)

    sm_parts = [_pack_tiny(gs, sq)]
    sm_snd, sm_rcv, sm_src, sm_land, sm_token = _copy_start("tiny_start", sm_parts, _place_own(sm_parts, True), True)

    grad, delta, new_m, new_v = {}, {}, {}, {}

    def update(group, after):
        names, (snd, rcv, srcs, lands), whole = flying[group]
        owns, slots = _copy_wait("exchange_wait_" + group, srcs, lands, snd, rcv, after, whole, with_srcs=True)
        for n, sl, own in zip(names, slots, owns):
            if n == "w_s":
                g, d, m2, v2 = _adamw_big("adamw_w_s", sl, *[a[n].reshape(-1, LANES) for a in (w, mo, vo)])
            else:
                g, d, m2, v2 = _adamw_big("adamw_" + n, sl, local(n, w[n]), local(n, mo[n]), local(n, vo[n]),
                                          own=own)
            grad[n], delta[n], new_m[n], new_v[n] = (
                (t.T if n in TRANSPOSED else t).reshape(w[n].shape) for t in (g, d, m2, v2))
        return d

    last = update("ffn2", sm_token)
    last = update("mid", last)
    last = update("ffn1_dn", last)
    last = update("ffn1_up", last)
    tiny_all, = _copy_wait("tiny_wait", sm_src, sm_land, sm_snd, sm_rcv, last, True)
    stores, loss_row = _adamw_tiny(tiny_all, *[{n: (a[n][0] if n == "b_s" else a[n]) for n in tiny_names}
                                               for a in (w, mo, vo)])
    for store, t in zip((grad, delta, new_m, new_v), stores):
        store.update({n: v.reshape(w[n].shape) for n, v in t.items()})
    loss = loss_row[0, 0] * (0.5 / D)

    return (loss, dx0[None], *[grad[n] for n in WEIGHTS], *[delta[n] for n in WEIGHTS],
            *[new_m[n] for n in WEIGHTS], *[new_v[n] for n in WEIGHTS])
```

```python
import functools

import jax
import jax.numpy as jnp
from jax import lax
from jax.experimental import pallas as pl
from jax.experimental.pallas import tpu as pltpu

F32 = jnp.float32
BF = jnp.bfloat16
S = jax.ShapeDtypeStruct

N_DEV = 8
D_MODEL = 1024
FOX_HEADS, FOX_HD = 8, 64
FOX_W = 512
GMLP_G, GMLP_GD = 8, 64
GMLP_W = 512
CHUNK = 128
CA_HEADS, CA_HD = 4, 256
N_FFN_BLK = 4
ZW = 2688
Z_Q, Z_K, Z_V, Z_U, Z_G, Z_F = 0, 512, 1024, 1536, 2048, 2560
EPS = 1e-6
NEG = -1e30
LANES = 128

ADAM_LR, ADAM_B1, ADAM_B2, ADAM_EPS, ADAM_WD, ADAM_STEP = 0.001, 0.9, 0.999, 1e-08, 0.01, 10

VMEM_LIMIT = 52 * 2 ** 20


def _cp(n_axes):
    return pltpu.CompilerParams(dimension_semantics=("arbitrary",) * n_axes, vmem_limit_bytes=VMEM_LIMIT)


def _nn(a, b):
    return jnp.dot(a, b, preferred_element_type=F32)


def _nt(a, b):
    return lax.dot_general(a, b, (((1,), (1,)), ((), ())), preferred_element_type=F32)


def _tn(a, b):
    return lax.dot_general(a, b, (((0,), (0,)), ((), ())), preferred_element_type=F32)


def _hi(mask, x):
    return jnp.dot(mask.astype(F32), x, precision=lax.Precision.HIGHEST, preferred_element_type=F32)


def _hi3(mask, x):
    mb = mask.astype(BF)
    hi = x.astype(BF)
    r1 = x - hi.astype(F32)
    mid = r1.astype(BF)
    lo = (r1 - mid.astype(F32)).astype(BF)
    return _nn(mb, hi) + _nn(mb, mid) + _nn(mb, lo)


def _rstd(x):
    return lax.rsqrt(jnp.mean(x * x, axis=-1, keepdims=True) + EPS)


def _norm_bwd(dy, x, g, r=None):
    r = _rstd(x) if r is None else r
    xh = x * r
    dxh = dy * g
    dx = r * (dxh - xh * jnp.mean(dxh * xh, axis=-1, keepdims=True))
    return dx, dy * xh


def _acc_rows(ref, first, val):
    srow = jnp.sum(val, axis=0, keepdims=True)

    @pl.when(first)
    def _():
        ref[...] = srow

    @pl.when(jnp.logical_not(first))
    def _():
        ref[...] += srow


def _gelu(x):
    c = 0.7978845608028654
    return 0.5 * x * (1.0 + jnp.tanh(c * (x + 0.044715 * x * x * x)))


def _gelu_grad(x):
    c = 0.7978845608028654
    t = jnp.tanh(c * (x + 0.044715 * x * x * x))
    return 0.5 * (1.0 + t) + 0.5 * x * (1.0 - t * t) * c * (1.0 + 3 * 0.044715 * x * x)


def _tile(n, pref):
    return pref if n % pref == 0 else n


def _rms_cast(name, x, g, after):
    T, D = x.shape
    tm = _tile(T, 1024)

    def body(x_ref, g_ref, t_ref, h_ref):
        xf = x_ref[...]
        h_ref[...] = (xf * _rstd(xf) * g_ref[...]).astype(BF)

    return pl.pallas_call(
        body, name=name, grid=(T // tm,),
        in_specs=[pl.BlockSpec((tm, D), lambda i: (i, 0)), pl.BlockSpec((1, D), lambda i: (0, 0)),
                  pl.BlockSpec((8, LANES), lambda i: (0, 0))],
        out_specs=pl.BlockSpec((tm, D), lambda i: (i, 0)), out_shape=S((T, D), BF),
        compiler_params=_cp(1))(x, g, after)


def _ffn_up_from_h(name, h, wup):
    T, D = h.shape
    FB = wup.shape[-2]
    tm = _tile(T, 1024)

    def body(h_ref, w_ref, a_ref):
        hb = h_ref[...]
        gg = _nt(hb, w_ref[0])
        uu = _nt(hb, w_ref[1])
        a_ref[...] = (gg * jax.nn.sigmoid(gg) * uu).astype(BF)

    return pl.pallas_call(
        body, name=name, grid=(T // tm, N_FFN_BLK),
        in_specs=[pl.BlockSpec((tm, D), lambda i, j: (i, 0)),
                  pl.BlockSpec((2, None, FB, D), lambda i, j: (0, j, 0, 0))],
        out_specs=pl.BlockSpec((None, tm, FB), lambda i, j: (j, i, 0)),
        out_shape=S((N_FFN_BLK, T, FB), BF),
        compiler_params=_cp(2))(h, wup)


def _ffn_up(name, x, g, wup):
    T, D = x.shape
    FB = wup.shape[-2]
    tm = _tile(T, 1024)

    def body(x_ref, g_ref, w_ref, a_ref, h_ref):
        @pl.when(pl.program_id(1) == 0)
        def _():
            xf = x_ref[...]
            h_ref[...] = (xf * _rstd(xf) * g_ref[...]).astype(BF)

        hb = h_ref[...]
        gg = _nt(hb, w_ref[0])
        uu = _nt(hb, w_ref[1])
        a_ref[...] = (gg * jax.nn.sigmoid(gg) * uu).astype(BF)

    return pl.pallas_call(
        body, name=name, grid=(T // tm, N_FFN_BLK),
        in_specs=[pl.BlockSpec((tm, D), lambda i, j: (i, 0)),
                  pl.BlockSpec((1, D), lambda i, j: (0, 0)),
                  pl.BlockSpec((2, None, FB, D), lambda i, j: (0, j, 0, 0))],
        out_specs=[pl.BlockSpec((None, tm, FB), lambda i, j: (j, i, 0)),
                   pl.BlockSpec((tm, D), lambda i, j: (i, 0))],
        out_shape=[S((N_FFN_BLK, T, FB), BF), S((T, D), BF)],
        compiler_params=_cp(2))(x, g, wup)


def _ffn_down(name, a, wdn, x):
    _, T, FB = a.shape
    D = x.shape[1]
    tm = _tile(T, 512)

    def body(a_ref, w_ref, x_ref, o_ref):
        p = _nn(a_ref[0], w_ref[0])
        for j in range(1, N_FFN_BLK):
            p = p + _nn(a_ref[j], w_ref[j])
        o_ref[...] = x_ref[...] + 0.5 * p

    return pl.pallas_call(
        body, name=name, grid=(T // tm,),
        in_specs=[pl.BlockSpec((N_FFN_BLK, tm, FB), lambda i: (0, i, 0)),
                  pl.BlockSpec((N_FFN_BLK, FB, D), lambda i: (0, 0, 0)),
                  pl.BlockSpec((tm, D), lambda i: (i, 0))],
        out_specs=pl.BlockSpec((tm, D), lambda i: (i, 0)),
        out_shape=S((T, D), F32),
        compiler_params=_cp(1))(a, wdn, x)


def _ffn_down_loss(name, a, wdn, x, target):
    _, T, FB = a.shape
    D = x.shape[1]
    tm = _tile(T, 512)

    def body(a_ref, w_ref, x_ref, t_ref, d_ref, db_ref, loss_ref):
        i = pl.program_id(0)
        p = _nn(a_ref[0], w_ref[0])
        for j in range(1, N_FFN_BLK):
            p = p + _nn(a_ref[j], w_ref[j])
        diff = (x_ref[...] + 0.5 * p) - t_ref[...]
        dy = diff * (1.0 / D)
        d_ref[...] = dy
        db_ref[...] = dy.astype(BF)
        sq = jnp.zeros((8, LANES), F32) + jnp.sum(diff * diff)

        @pl.when(i == 0)
        def _():
            loss_ref[...] = sq

        @pl.when(i > 0)
        def _():
            loss_ref[...] += sq

    row = pl.BlockSpec((tm, D), lambda i: (i, 0))
    return pl.pallas_call(
        body, name=name, grid=(T // tm,),
        in_specs=[pl.BlockSpec((N_FFN_BLK, tm, FB), lambda i: (0, i, 0)),
                  pl.BlockSpec((N_FFN_BLK, FB, D), lambda i: (0, 0, 0)), row, row],
        out_specs=[row, row, pl.BlockSpec((8, LANES), lambda i: (0, 0))],
        out_shape=[S((T, D), F32), S((T, D), BF), S((8, LANES), F32)],
        compiler_params=_cp(1))(a, wdn, x, target)


def _ffn_bwd_act(name, dyb, h, wup, wdn):
    T, D = h.shape
    FB = wup.shape[-2]
    tm = _tile(T, 1024)

    def body(d_ref, h_ref, wu_ref, wd_ref, o_ref):
        da = 0.5 * _nt(d_ref[...], wd_ref[...])
        hb = h_ref[...]
        gg = _nt(hb, wu_ref[0])
        uu = _nt(hb, wu_ref[1])
        sg = jax.nn.sigmoid(gg)
        o_ref[0] = (da * uu * (sg * (1.0 + gg * (1.0 - sg)))).astype(BF)
        o_ref[1] = (da * (gg * sg)).astype(BF)

    return pl.pallas_call(
        body, name=name, grid=(T // tm, N_FFN_BLK),
        in_specs=[pl.BlockSpec((tm, D), lambda i, j: (i, 0)),
                  pl.BlockSpec((tm, D), lambda i, j: (i, 0)),
                  pl.BlockSpec((2, None, FB, D), lambda i, j: (0, j, 0, 0)),
                  pl.BlockSpec((None, FB, D), lambda i, j: (j, 0, 0))],
        out_specs=pl.BlockSpec((2, None, tm, FB), lambda i, j: (0, j, i, 0)),
        out_shape=S((2, N_FFN_BLK, T, FB), BF),
        compiler_params=_cp(2))(dyb, h, wup, wdn)


def _ffn_dx(name, dgu, wup, x, g, dy):
    T, D = x.shape
    FB = wup.shape[-2]
    tm = _tile(T, 512)

    def body(d_ref, w_ref, x_ref, g_ref, dy_ref, dx_ref, dg_ref):
        p = None
        for j in range(N_FFN_BLK):
            for half in range(2):
                t = _nn(d_ref[half, j], w_ref[half, j])
                p = t if p is None else p + t
        dx, dgr = _norm_bwd(p, x_ref[...], g_ref[...])
        dx_ref[...] = dx + dy_ref[...]
        _acc_rows(dg_ref, pl.program_id(0) == 0, dgr)

    return pl.pallas_call(
        body, name=name, grid=(T // tm,),
        in_specs=[pl.BlockSpec((2, N_FFN_BLK, tm, FB), lambda i: (0, 0, i, 0)),
                  pl.BlockSpec((2, N_FFN_BLK, FB, D), lambda i: (0, 0, 0, 0), pipeline_mode=pl.Buffered(1)),
                  pl.BlockSpec((tm, D), lambda i: (i, 0)),
                  pl.BlockSpec((1, D), lambda i: (0, 0)),
                  pl.BlockSpec((tm, D), lambda i: (i, 0))],
        out_specs=[pl.BlockSpec((tm, D), lambda i: (i, 0)),
                   pl.BlockSpec((1, D), lambda i: (0, 0))],
        out_shape=[S((T, D), F32), S((1, D), F32)],
        compiler_params=_cp(1))(dgu, wup, x, g, dy)


def _tn_matmul(name, a, a_spec, b, out_shape, out_spec, n_blocks, scale=1.0, after=None):
    extra = [] if after is None else [after]

    def body(a_ref, b_ref, *rest):
        o_ref = rest[-1]
        o_ref[...] = (_tn(a_ref[...], b_ref[...]) * scale).astype(o_ref.dtype)

    return pl.pallas_call(
        body, name=name, grid=(n_blocks,),
        in_specs=[a_spec, pl.BlockSpec(b.shape, lambda j: (0, 0), pipeline_mode=pl.Buffered(1))]
        + [pl.BlockSpec((8, LANES), lambda j: (0, 0)) for _ in extra],
        out_specs=out_spec, out_shape=out_shape, compiler_params=_cp(1))(a, b, *extra)


def _ffn_dwup(name, h, dgu, after=None):
    T, D = h.shape
    FB = dgu.shape[-1]
    return _tn_matmul(
        name + "_dwup", dgu.reshape(2 * N_FFN_BLK, T, FB), pl.BlockSpec((None, T, FB), lambda j: (j, 0, 0)), h,
        S((2 * N_FFN_BLK, FB, D), BF), pl.BlockSpec((None, FB, D), lambda j: (j, 0, 0)), 2 * N_FFN_BLK,
        after=after)


def _ffn_dwdn(name, a, dyb):
    _, T, FB = a.shape
    D = dyb.shape[1]
    return _tn_matmul(
        name + "_dwdn", a, pl.BlockSpec((None, T, FB), lambda j: (j, 0, 0)), dyb,
        S((N_FFN_BLK, FB, D), BF), pl.BlockSpec((None, FB, D), lambda j: (j, 0, 0)), N_FFN_BLK, scale=0.5)


def _tri(n, lower):
    r = lax.broadcasted_iota(jnp.int32, (n, n), 0)
    c = lax.broadcasted_iota(jnp.int32, (n, n), 1)
    return (r >= c) if lower else (r <= c)


def _spatial_mix(vgn_b, ws_ref, bst, tm):
    tril = _tri(CHUNK, True)
    wms = [jnp.where(tril, ws_ref[g], 0.0).astype(BF) for g in range(GMLP_G)]
    rows = []
    for c in range(tm // CHUNK):
        cols = []
        for g in range(GMLP_G):
            vs = vgn_b[c * CHUNK:(c + 1) * CHUNK, g * GMLP_GD:(g + 1) * GMLP_GD]
            cols.append(_nn(wms[g], vs) + bst[:, g:g + 1])
        rows.append(jnp.concatenate(cols, axis=1))
    return jnp.concatenate(rows, axis=0), wms


HB = 128
AUG_W = FOX_HEADS * HB
COL_A, COL_B, COL_C = 64, 67, 70
RS_Q, RS_K, RS_V, RS_O = 0, 8, 16, 17


def _piece_matrix(col):
    r = jnp.arange(LANES)
    dst = jnp.where(r < 3 * FOX_HEADS, (r % FOX_HEADS) * HB + col + r // FOX_HEADS, -1)
    return (jnp.arange(AUG_W)[None, :] == dst[:, None]).astype(BF)


def _ones_row(cols):
    c = jnp.arange(AUG_W) % HB
    hit = functools.reduce(jnp.logical_or, [(c >= a) & (c < a + 3) for a in cols])
    return hit.astype(F32)[None, :]


def _pieces(x):
    lane = lax.broadcasted_iota(jnp.int32, x.shape, 1)
    x = jnp.where(lane < FOX_HEADS, x, 0.0)
    hi = x.astype(BF).astype(F32)
    r1 = x - hi
    mid = r1.astype(BF).astype(F32)
    lo = (r1 - mid).astype(BF).astype(F32)
    return (hi + pltpu.roll(mid, FOX_HEADS, 1) + pltpu.roll(lo, 2 * FOX_HEADS, 1)).astype(BF)


def _mix_prep(x, g_mix, wz, bf128, g_q, g_k, g_sgu, w_s, b_st, g_go):
    T, D = x.shape
    tm = _tile(T, 512)
    pc_q, pc_k = _piece_matrix(COL_A), _piece_matrix(COL_B)
    one_q, one_k, one_v = _ones_row([COL_B]), _ones_row([COL_A, COL_C]), _ones_row([COL_A])

    def body(x_ref, gm_ref, wz_ref, bf_ref, gq_ref, gk_ref, gs_ref, ws_ref, bst_ref, go_ref, pq_ref, pk_ref, oq_ref,
             ok_ref, ov_ref, z_ref, h_ref, q_ref, k_ref, v_ref, y_ref, rs_ref, carry_ref):
        i = pl.program_id(0)

        @pl.when(i == 0)
        def _():
            carry_ref[...] = jnp.zeros_like(carry_ref)

        xf = x_ref[...]
        hb = (xf * _rstd(xf) * gm_ref[...]).astype(BF)
        h_ref[...] = hb
        z_ref[...] = _nt(hb, wz_ref[...])

        fl = z_ref[:, Z_F:Z_F + LANES] + bf_ref[...]
        logf = jnp.minimum(fl, 0.0) - jnp.log1p(jnp.exp(-jnp.abs(fl)))
        csum = _hi(_tri(tm, True), logf) + carry_ref[...]
        carry_ref[...] = csum[tm - 1:tm, :]
        ext_q = (_nn(_pieces(csum), pq_ref[...]) + oq_ref[...]).astype(BF)
        ext_k = (_nn(_pieces(-csum), pk_ref[...]) + ok_ref[...]).astype(BF)
        ext_v = jnp.broadcast_to(ov_ref[...], (tm, AUG_W)).astype(BF)

        rs_ref[...] = jnp.zeros_like(rs_ref)
        for h in range(FOX_HEADS):
            lo, hi = slice(h * HB, h * HB + FOX_HD), slice(h * HB + FOX_HD, (h + 1) * HB)
            qh = z_ref[:, Z_Q + h * FOX_HD:Z_Q + (h + 1) * FOX_HD]
            kh = z_ref[:, Z_K + h * FOX_HD:Z_K + (h + 1) * FOX_HD]
            rq, rk = _rstd(qh), _rstd(kh)
            rs_ref[:, RS_Q + h:RS_Q + h + 1] = rq
            rs_ref[:, RS_K + h:RS_K + h + 1] = rk
            q_ref[:, lo] = (qh * rq * gq_ref[...] * 0.125).astype(BF)
            k_ref[:, lo] = (kh * rk * gk_ref[...]).astype(BF)
            v_ref[:, lo] = z_ref[:, Z_V + h * FOX_HD:Z_V + (h + 1) * FOX_HD].astype(BF)
            q_ref[:, hi] = ext_q[:, hi]
            k_ref[:, hi] = ext_k[:, hi]
            v_ref[:, hi] = ext_v[:, hi]

        u = _gelu(z_ref[:, Z_U:Z_U + GMLP_W])
        vg = _gelu(z_ref[:, Z_G:Z_G + GMLP_W])
        rv = _rstd(vg)
        vgn = (vg * rv * gs_ref[...]).astype(BF)
        mixed, _ = _spatial_mix(vgn, ws_ref, bst_ref[...], tm)
        sgu = u * mixed
        ro = _rstd(sgu)
        y_ref[...] = (sgu * ro * go_ref[...]).astype(BF)
        rs_ref[:, RS_V:RS_V + 1] = rv
        rs_ref[:, RS_O:RS_O + 1] = ro

    row = lambda i: (i, 0)
    fix2 = lambda i: (0, 0)
    return pl.pallas_call(
        body, name="mix_prep", grid=(T // tm,),
        in_specs=[pl.BlockSpec((tm, D), row), pl.BlockSpec((1, D), fix2),
                  pl.BlockSpec((ZW, D), fix2, pipeline_mode=pl.Buffered(1)),
                  pl.BlockSpec((1, LANES), fix2), pl.BlockSpec((1, FOX_HD), fix2), pl.BlockSpec((1, FOX_HD), fix2),
                  pl.BlockSpec((1, GMLP_W), fix2), pl.BlockSpec((GMLP_G, CHUNK, CHUNK), lambda i: (0, 0, 0)),
                  pl.BlockSpec((CHUNK, GMLP_G), fix2), pl.BlockSpec((1, GMLP_W), fix2),
                  pl.BlockSpec((LANES, AUG_W), fix2),
                  pl.BlockSpec((LANES, AUG_W), fix2), pl.BlockSpec((1, AUG_W), fix2), pl.BlockSpec((1, AUG_W), fix2),
                  pl.BlockSpec((1, AUG_W), fix2)],
        out_specs=[pl.BlockSpec((tm, ZW), row), pl.BlockSpec((tm, D), row),
                   pl.BlockSpec((tm, AUG_W), row), pl.BlockSpec((tm, AUG_W), row), pl.BlockSpec((tm, AUG_W), row),
                   pl.BlockSpec((tm, GMLP_W), row), pl.BlockSpec((tm, LANES), row)],
        out_shape=[S((T, ZW), F32), S((T, D), BF), S((T, AUG_W), BF), S((T, AUG_W), BF), S((T, AUG_W), BF),
                   S((T, GMLP_W), BF), S((T, LANES), F32)],
        scratch_shapes=[pltpu.VMEM((1, LANES), F32)],
        compiler_params=_cp(1))(x, g_mix, wz, bf128, g_q, g_k, g_sgu, w_s, b_st, g_go, pc_q, pc_k, one_q, one_k,
                                one_v)


def _fox_fwd(q, k, v):
    T = q.shape[0]
    tq = _tile(T, 1024)
    nq = T // tq

    def body(q_ref, k_ref, v_ref, o_ref, lse_ref, m_sc, acc_sc):
        i, j = pl.program_id(0), pl.program_id(1)

        @pl.when(j == 0)
        def _():
            m_sc[...] = jnp.full(m_sc.shape, NEG, F32)
            acc_sc[...] = jnp.zeros_like(acc_sc)

        def step(masked):
            mask = _tri(tq, True) if masked else None
            for h in range(FOX_HEADS):
                hb = slice(h * HB, (h + 1) * HB)
                s = _nt(q_ref[:, hb], k_ref[:, hb])
                if masked:
                    s = jnp.where(mask, s, NEG)
                m_prev = m_sc[h]
                m_new = jnp.maximum(m_prev, jnp.broadcast_to(jnp.max(s, axis=1, keepdims=True), (tq, HB)))
                p = jnp.exp(s - jnp.tile(m_new, (1, tq // HB))).astype(BF)
                acc_sc[:, hb] = jnp.exp(m_prev - m_new) * acc_sc[:, hb] + _nn(p, v_ref[:, hb])
                m_sc[h] = m_new

        @pl.when(j < i)
        def _():
            step(False)

        @pl.when(j == i)
        def _():
            step(True)
            lse_ref[...] = jnp.zeros_like(lse_ref)
            for h in range(FOX_HEADS):
                l = acc_sc[:, h * HB + COL_A:h * HB + COL_A + 1]
                o_ref[:, h * FOX_HD:(h + 1) * FOX_HD] = acc_sc[:, h * HB:h * HB + FOX_HD] / l
                lse_ref[:, h:h + 1] = m_sc[h][:, 0:1] + jnp.log(l)

    qi = lambda i, j: (i, 0)
    kj = lambda i, j: (jnp.minimum(i, j), 0)
    return pl.pallas_call(
        body, name="fox_fwd", grid=(nq, nq),
        in_specs=[pl.BlockSpec((tq, AUG_W), qi), pl.BlockSpec((tq, AUG_W), kj), pl.BlockSpec((tq, AUG_W), kj)],
        out_specs=[pl.BlockSpec((tq, FOX_W), qi), pl.BlockSpec((tq, LANES), qi)],
        out_shape=[S((T, FOX_W), F32), S((T, LANES), F32)],
        scratch_shapes=[pltpu.VMEM((FOX_HEADS, tq, HB), F32), pltpu.VMEM((tq, AUG_W), F32)],
        compiler_params=_cp(2))(q, k, v)


def _fox_bwd(q, k, v, dob):
    T = q.shape[0]
    tq = _tile(T, 512)
    nq = T // tq
    n_sweeps = 1
    half = AUG_W // n_sweeps
    hpg = FOX_HEADS // n_sweeps

    pairs = [(j, i) for j in range(nq) for i in range(j, nq)]
    jt = jnp.asarray([p[0] for p in pairs], jnp.int32)
    it = jnp.asarray([p[1] for p in pairs], jnp.int32)

    def body(jt_ref, it_ref, q_ref, k_ref, v_ref, do_ref, dq_ref, dk_ref, dv_ref, dq_sc):
        t = pl.program_id(1)
        j, i = jt_ref[t], it_ref[t]

        @pl.when(t == 0)
        def _():
            dq_sc[...] = jnp.zeros_like(dq_sc)

        @pl.when(i == j)
        def _():
            dk_ref[...] = jnp.zeros_like(dk_ref)
            dv_ref[...] = jnp.zeros_like(dv_ref)

        def step(masked):
            rows = pl.ds(pl.multiple_of(i * tq, tq), tq)
            mask = _tri(tq, True) if masked else None
            for h in range(hpg):
                hb = slice(h * HB, (h + 1) * HB)
                qh, kh, vh, doh = q_ref[:, hb], k_ref[:, hb], v_ref[:, hb], do_ref[:, hb]
                s = _nt(qh, kh)
                if masked:
                    s = jnp.where(mask, s, NEG)
                p = jnp.exp(s)
                dsb = (p * _nt(doh, vh)).astype(BF)
                dv_ref[:, hb] += _tn(p.astype(BF), doh)
                dk_ref[:, hb] += _tn(dsb, qh)
                dq_sc[rows, hb] += _nn(dsb, kh)

        @pl.when(i > j)
        def _():
            step(False)

        @pl.when(i == j)
        def _():
            step(True)
            dq_ref[...] = dq_sc[pl.ds(pl.multiple_of(j * tq, tq), tq), :]

    qi = pl.BlockSpec((tq, half), lambda g, t, jt_ref, it_ref: (it_ref[t], g))
    kj = pl.BlockSpec((tq, half), lambda g, t, jt_ref, it_ref: (jt_ref[t], g))
    return pl.pallas_call(
        body, name="fox_bwd",
        grid_spec=pltpu.PrefetchScalarGridSpec(
            num_scalar_prefetch=2, grid=(n_sweeps, len(pairs)), in_specs=[qi, kj, kj, qi], out_specs=[kj, kj, kj],
            scratch_shapes=[pltpu.VMEM((T, half), F32)]),
        out_shape=[S((T, AUG_W), F32), S((T, AUG_W), F32), S((T, AUG_W), F32)],
        compiler_params=_cp(2))(jt, it, q, k, v, dob)


def _mix_out(attn, yg, g_fo, wout, x):
    T, D = x.shape
    tm = _tile(T, 1024)

    def body(a_ref, y_ref, g_ref, w_ref, x_ref, o_ref):
        at = a_ref[...]
        yf = (at * _rstd(at) * g_ref[...]).astype(BF)
        o_ref[...] = x_ref[...] + _nn(yf, w_ref[:FOX_W, :]) + _nn(y_ref[...], w_ref[FOX_W:, :])

    row = lambda i: (i, 0)
    return pl.pallas_call(
        body, name="mix_out", grid=(T // tm,),
        in_specs=[pl.BlockSpec((tm, FOX_W), row), pl.BlockSpec((tm, GMLP_W), row),
                  pl.BlockSpec((1, FOX_W), lambda i: (0, 0)), pl.BlockSpec((D, D), lambda i: (0, 0)),
                  pl.BlockSpec((tm, D), row)],
        out_specs=pl.BlockSpec((tm, D), row),
        out_shape=S((T, D), F32),
        compiler_params=_cp(1))(attn, yg, g_fo, wout, x)


def _mix_out_bwd(dx, attn, yg, g_fo, wout, qf, lse):
    T, D = dx.shape
    tm = _tile(T, 512)
    n = T // tm
    pc_l, pc_d = _piece_matrix(COL_C), _piece_matrix(COL_A)

    def body(dx_ref, a_ref, y_ref, g_ref, w_ref, qf_ref, lse_ref, pl_ref, pd_ref,
             qb_ref, dob_ref, dyg_ref, dw_ref, dg_ref, acc_ref, dsum_ref):
        i = pl.program_id(0)
        dxb = dx_ref[...].astype(BF)
        at = a_ref[...]
        yf = (at * _rstd(at) * g_ref[...]).astype(BF)
        dy = _nt(dxb, w_ref[...])
        p_top = _tn(yf, dxb)
        p_bot = _tn(y_ref[...], dxb)

        @pl.when(i == 0)
        def _():
            acc_ref[:FOX_W, :] = p_top
            acc_ref[FOX_W:, :] = p_bot

        @pl.when(i > 0)
        def _():
            acc_ref[:FOX_W, :] += p_top
            acc_ref[FOX_W:, :] += p_bot

        @pl.when(i == n - 1)
        def _():
            dw_ref[...] = acc_ref[...].astype(BF)

        dat, dgr = _norm_bwd(dy[:, :FOX_W], at, g_ref[...])
        _acc_rows(dg_ref, i == 0, dgr)
        dyg_ref[...] = dy[:, FOX_W:]
        prod = dat * at
        dsum_ref[...] = jnp.zeros_like(dsum_ref)
        for h in range(FOX_HEADS):
            dsum_ref[:, h:h + 1] = jnp.sum(prod[:, h * FOX_HD:(h + 1) * FOX_HD], axis=1, keepdims=True)
        ext_d = _nn(_pieces(-dsum_ref[...]), pd_ref[...]).astype(BF)
        ext_l = _nn(_pieces(-lse_ref[...]), pl_ref[...])
        datb = dat.astype(BF)
        for h in range(FOX_HEADS):
            lo, hi = slice(h * HB, h * HB + FOX_HD), slice(h * HB + FOX_HD, (h + 1) * HB)
            dob_ref[:, lo] = datb[:, h * FOX_HD:(h + 1) * FOX_HD]
            dob_ref[:, hi] = ext_d[:, hi]
            qb_ref[:, lo] = qf_ref[:, lo]
            qb_ref[:, hi] = (qf_ref[:, hi].astype(F32) + ext_l[:, hi]).astype(BF)

    row = lambda i: (i, 0)
    fix = lambda i: (0, 0)
    return pl.pallas_call(
        body, name="mix_out_bwd", grid=(n,),
        in_specs=[pl.BlockSpec((tm, D), row), pl.BlockSpec((tm, FOX_W), row), pl.BlockSpec((tm, GMLP_W), row),
                  pl.BlockSpec((1, FOX_W), fix), pl.BlockSpec((D, D), fix), pl.BlockSpec((tm, AUG_W), row),
                  pl.BlockSpec((tm, LANES), row), pl.BlockSpec((LANES, AUG_W), fix),
                  pl.BlockSpec((LANES, AUG_W), fix)],
        out_specs=[pl.BlockSpec((tm, AUG_W), row), pl.BlockSpec((tm, AUG_W), row), pl.BlockSpec((tm, GMLP_W), row),
                   pl.BlockSpec((D, D), fix), pl.BlockSpec((1, FOX_W), fix)],
        out_shape=[S((T, AUG_W), BF), S((T, AUG_W), BF), S((T, GMLP_W), F32), S((D, D), BF), S((1, FOX_W), F32)],
        scratch_shapes=[pltpu.VMEM((D, D), F32), pltpu.VMEM((tm, LANES), F32)],
        compiler_params=_cp(1))(dx, attn, yg, g_fo, wout, qf, lse, pc_l, pc_d)


def _mix_prep_bwd(z, dq, dk, dv, dyg, rs, bf128, g_q, g_k, g_sgu, w_s, b_st, g_go):
    T = z.shape[0]
    tm = _tile(T, 512)
    n = T // tm

    def body(z_ref, dq_ref, dk_ref, dv_ref, dyg_ref, rs_ref, bf_ref, gq_ref, gk_ref, gs_ref, ws_ref,
             bst_ref, go_ref, dz_ref, dgq_ref, dgk_ref, dgs_ref, dgo_ref, dws_ref, dbst_ref, dbf_ref, carry_ref):
        i = pl.program_id(0)
        first = i == 0
        rs = rs_ref[...]

        @pl.when(first)
        def _():
            carry_ref[...] = jnp.zeros_like(carry_ref)

        lane = lax.broadcasted_iota(jnp.int32, (tm, LANES), 1)
        dc = jnp.zeros((tm, LANES), F32)
        gq_rows, gk_rows = [], []
        for h in range(FOX_HEADS):
            hp = slice(h * HB, h * HB + FOX_HD)
            dqh, gqr = _norm_bwd(dq_ref[:, hp] * 0.125, z_ref[:, Z_Q + h * FOX_HD:Z_Q + (h + 1) * FOX_HD], gq_ref[...],
                                 rs[:, RS_Q + h:RS_Q + h + 1])
            dkh, gkr = _norm_bwd(dk_ref[:, hp], z_ref[:, Z_K + h * FOX_HD:Z_K + (h + 1) * FOX_HD], gk_ref[...],
                                 rs[:, RS_K + h:RS_K + h + 1])
            dz_ref[:, Z_Q + h * FOX_HD:Z_Q + (h + 1) * FOX_HD] = dqh.astype(BF)
            dz_ref[:, Z_K + h * FOX_HD:Z_K + (h + 1) * FOX_HD] = dkh.astype(BF)
            dz_ref[:, Z_V + h * FOX_HD:Z_V + (h + 1) * FOX_HD] = dv_ref[:, hp].astype(BF)
            dch = dq_ref[:, h * HB + COL_A:h * HB + COL_A + 1] - dk_ref[:, h * HB + COL_B:h * HB + COL_B + 1]
            dc = jnp.where(lane == h, dch, dc)
            gq_rows.append(gqr)
            gk_rows.append(gkr)
        _acc_rows(dgq_ref, first, functools.reduce(lambda a, b: a + b, gq_rows))
        _acc_rows(dgk_ref, first, functools.reduce(lambda a, b: a + b, gk_rows))

        dlogf = _hi3(_tri(tm, False), dc) + carry_ref[...]
        carry_ref[...] = dlogf[0:1, :]
        fl = z_ref[:, Z_F:Z_F + LANES] + bf_ref[...]
        lane = lax.broadcasted_iota(jnp.int32, (tm, LANES), 1)
        df = jnp.where(lane < FOX_HEADS, dlogf * jax.nn.sigmoid(-fl), 0.0)
        dz_ref[:, Z_F:Z_F + LANES] = df.astype(BF)
        _acc_rows(dbf_ref, first, df)

        u_pre = z_ref[:, Z_U:Z_U + GMLP_W]
        vg_pre = z_ref[:, Z_G:Z_G + GMLP_W]
        u = _gelu(u_pre)
        vg = _gelu(vg_pre)
        rv = rs[:, RS_V:RS_V + 1]
        vgn = (vg * rv * gs_ref[...]).astype(BF)
        bst = bst_ref[...]
        mixed, wms = _spatial_mix(vgn, ws_ref, bst, tm)
        sgu = u * mixed
        dsgu, gor = _norm_bwd(dyg_ref[...], sgu, go_ref[...], rs[:, RS_O:RS_O + 1])
        _acc_rows(dgo_ref, first, gor)
        du = dsgu * mixed
        dmixed = dsgu * u
        dmb = dmixed.astype(BF)
        tril = _tri(CHUNK, True)
        dvgn_rows = []
        dws = [None] * GMLP_G
        dbs = [None] * GMLP_G
        for c in range(tm // CHUNK):
            cs = slice(c * CHUNK, (c + 1) * CHUNK)
            cols = []
            for g in range(GMLP_G):
                gs = slice(g * GMLP_GD, (g + 1) * GMLP_GD)
                dmc = dmb[cs, gs]
                pw = _nt(dmc, vgn[cs, gs])
                pb = jnp.sum(dmixed[cs, gs], axis=1, keepdims=True)
                dws[g] = pw if dws[g] is None else dws[g] + pw
                dbs[g] = pb if dbs[g] is None else dbs[g] + pb
                cols.append(_tn(wms[g], dmc))
            dvgn_rows.append(jnp.concatenate(cols, axis=1))
        dvgn = jnp.concatenate(dvgn_rows, axis=0)
        dbs_t = jnp.concatenate(dbs, axis=1)
        for g in range(GMLP_G):
            dwg = jnp.where(tril, dws[g], 0.0)

            @pl.when(first)
            def _():
                dws_ref[g] = dwg

            @pl.when(jnp.logical_not(first))
            def _():
                dws_ref[g] += dwg

        @pl.when(first)
        def _():
            dbst_ref[...] = dbs_t

        @pl.when(jnp.logical_not(first))
        def _():
            dbst_ref[...] += dbs_t

        dvg, gsr = _norm_bwd(dvgn, vg, gs_ref[...], rv)
        _acc_rows(dgs_ref, first, gsr)
        dz_ref[:, Z_U:Z_U + GMLP_W] = (du * _gelu_grad(u_pre)).astype(BF)
        dz_ref[:, Z_G:Z_G + GMLP_W] = (dvg * _gelu_grad(vg_pre)).astype(BF)

    rev = lambda i: (n - 1 - i, 0)
    fix = lambda i: (0, 0)
    fix3 = lambda i: (0, 0, 0)
    return pl.pallas_call(
        body, name="mix_prep_bwd", grid=(n,),
        in_specs=[pl.BlockSpec((tm, ZW), rev), pl.BlockSpec((tm, AUG_W), rev), pl.BlockSpec((tm, AUG_W), rev),
                  pl.BlockSpec((tm, AUG_W), rev), pl.BlockSpec((tm, GMLP_W), rev), pl.BlockSpec((tm, LANES), rev),
                  pl.BlockSpec((1, LANES), fix), pl.BlockSpec((1, FOX_HD), fix), pl.BlockSpec((1, FOX_HD), fix),
                  pl.BlockSpec((1, GMLP_W), fix), pl.BlockSpec((GMLP_G, CHUNK, CHUNK), fix3),
                  pl.BlockSpec((CHUNK, GMLP_G), fix), pl.BlockSpec((1, GMLP_W), fix)],
        out_specs=[pl.BlockSpec((tm, ZW), rev), pl.BlockSpec((1, FOX_HD), fix), pl.BlockSpec((1, FOX_HD), fix),
                   pl.BlockSpec((1, GMLP_W), fix), pl.BlockSpec((1, GMLP_W), fix),
                   pl.BlockSpec((GMLP_G, CHUNK, CHUNK), fix3), pl.BlockSpec((CHUNK, GMLP_G), fix),
                   pl.BlockSpec((1, LANES), fix)],
        out_shape=[S((T, ZW), BF), S((1, FOX_HD), F32), S((1, FOX_HD), F32), S((1, GMLP_W), F32), S((1, GMLP_W), F32),
                   S((GMLP_G, CHUNK, CHUNK), F32), S((CHUNK, GMLP_G), F32), S((1, LANES), F32)],
        scratch_shapes=[pltpu.VMEM((1, LANES), F32)],
        compiler_params=_cp(1))(z, dq, dk, dv, dyg, rs, bf128, g_q, g_k, g_sgu, w_s, b_st, g_go)


def _mix_proj_bwd(dz, wz, x, g, dy):
    T, D = x.shape
    tm = _tile(T, 512)

    def body(dz_ref, w_ref, x_ref, g_ref, dy_ref, dx_ref, dxb_ref, dg_ref):
        dh = _nn(dz_ref[...], w_ref[...])
        dx, dgr = _norm_bwd(dh, x_ref[...], g_ref[...])
        dx = dx + dy_ref[...]
        dx_ref[...] = dx
        dxb_ref[...] = dx.astype(BF)
        _acc_rows(dg_ref, pl.program_id(0) == 0, dgr)

    row = lambda i: (i, 0)
    fix = lambda i: (0, 0)
    return pl.pallas_call(
        body, name="mix_proj_bwd", grid=(T // tm,),
        in_specs=[pl.BlockSpec((tm, ZW), row), pl.BlockSpec((ZW, D), fix), pl.BlockSpec((tm, D), row),
                  pl.BlockSpec((1, D), fix), pl.BlockSpec((tm, D), row)],
        out_specs=[pl.BlockSpec((tm, D), row), pl.BlockSpec((tm, D), row), pl.BlockSpec((1, D), fix)],
        out_shape=[S((T, D), F32), S((T, D), BF), S((1, D), F32)],
        compiler_params=_cp(1))(dz, wz, x, g, dy)


def _ca_kv(mem, g_mem, wckv, g_ck):
    M, D = mem.shape

    def body(m_ref, g_ref, w_ref, gk_ref, mn_ref, kr_ref, kn_ref, v_ref):
        mf = m_ref[...]
        mn = (mf * _rstd(mf) * g_ref[...]).astype(BF)
        mn_ref[...] = mn
        for h in range(CA_HEADS):
            kr = _nn(mn, w_ref[h])
            kr_ref[h] = kr
            kn_ref[h] = (kr * _rstd(kr) * gk_ref[...]).astype(BF)
            v_ref[h] = _nn(mn, w_ref[CA_HEADS + h]).astype(BF)

    hd = (CA_HEADS, M, CA_HD)
    return pl.pallas_call(
        body, name="ca_kv", out_shape=[S((M, D), BF), S(hd, F32), S(hd, BF), S(hd, BF)],
        compiler_params=pltpu.CompilerParams(vmem_limit_bytes=VMEM_LIMIT))(mem, g_mem, wckv, g_ck)


def _ca_tile_fwd(xt, gca, wcq, gcq, kn_ref, v_ref):
    hb = (xt * _rstd(xt) * gca).astype(BF)
    qc = _nn(hb, wcq)
    qr, qn, ps = [], [], []
    for h in range(CA_HEADS):
        qh = qc[:, h * CA_HD:(h + 1) * CA_HD]
        qnh = (qh * _rstd(qh) * gcq * 0.0625).astype(BF)
        s = _nt(qnh, kn_ref[h])
        e = jnp.exp(s - jnp.max(s, axis=1, keepdims=True))
        ps.append(e / jnp.sum(e, axis=1, keepdims=True))
        qr.append(qh)
        qn.append(qnh)
    return hb, qr, qn, ps


def _ca_fwd(x, g_ca, wcq, g_cq, kn, vv, wco):
    T, D = x.shape
    M = kn.shape[1]
    tm = _tile(T, 1024)

    def body(x_ref, gca_ref, wcq_ref, gcq_ref, kn_ref, v_ref, wco_ref, o_ref, ob_sc):
        xt = x_ref[...]
        _, _, _, ps = _ca_tile_fwd(xt, gca_ref[...], wcq_ref[...], gcq_ref[...], kn_ref, v_ref)
        for h in range(CA_HEADS):
            ob_sc[:, h * CA_HD:(h + 1) * CA_HD] = _nn(ps[h].astype(BF), v_ref[h]).astype(BF)
        o_ref[...] = xt + _nn(ob_sc[...], wco_ref[...])

    row = lambda i: (i, 0)
    fix = lambda i: (0, 0)
    fix3 = lambda i: (0, 0, 0)
    return pl.pallas_call(
        body, name="ca_fwd", grid=(T // tm,),
        in_specs=[pl.BlockSpec((tm, D), row), pl.BlockSpec((1, D), fix), pl.BlockSpec((D, D), fix),
                  pl.BlockSpec((1, CA_HD), fix), pl.BlockSpec((CA_HEADS, M, CA_HD), fix3),
                  pl.BlockSpec((CA_HEADS, M, CA_HD), fix3), pl.BlockSpec((D, D), fix)],
        out_specs=pl.BlockSpec((tm, D), row), out_shape=S((T, D), F32),
        scratch_shapes=[pltpu.VMEM((tm, D), BF)],
        compiler_params=_cp(1))(x, g_ca, wcq, g_cq, kn, vv, wco)


def _ca_bwd(x, dy, g_ca, wcq, g_cq, kn, vv, wco):
    T, D = x.shape
    M = kn.shape[1]
    tm = _tile(T, 512)
    n = T // tm

    def body(x_ref, dy_ref, gca_ref, wcq_ref, gcq_ref, kn_ref, v_ref, wco_ref,
             dx_ref, dwq_ref, dwo_ref, dkn_ref, dv_ref, dgcq_ref, dgca_ref, aq_sc, ao_sc, ob_sc, dq_sc):
        i = pl.program_id(0)
        first = i == 0
        xt = x_ref[...]
        dyt = dy_ref[...]
        dyb = dyt.astype(BF)
        hb, qr, qn, ps = _ca_tile_fwd(xt, gca_ref[...], wcq_ref[...], gcq_ref[...], kn_ref, v_ref)
        do = _nt(dyb, wco_ref[...])
        gcq_rows = None
        for h in range(CA_HEADS):
            hs = slice(h * CA_HD, (h + 1) * CA_HD)
            p = ps[h]
            pb = p.astype(BF)
            ob_sc[:, hs] = _nn(pb, v_ref[h]).astype(BF)
            doh = do[:, hs].astype(BF)
            dp = _nt(doh, v_ref[h])
            ds = (p * (dp - jnp.sum(dp * p, axis=1, keepdims=True))).astype(BF)
            dvh = _tn(pb, doh)
            dkh = _tn(ds, qn[h])

            @pl.when(first)
            def _():
                dv_ref[h] = dvh
                dkn_ref[h] = dkh

            @pl.when(jnp.logical_not(first))
            def _():
                dv_ref[h] += dvh
                dkn_ref[h] += dkh

            dqn = _nn(ds, kn_ref[h]) * 0.0625
            dqh, gr = _norm_bwd(dqn, qr[h], gcq_ref[...])
            gcq_rows = gr if gcq_rows is None else gcq_rows + gr
            dq_sc[:, hs] = dqh.astype(BF)
        _acc_rows(dgcq_ref, first, gcq_rows)
        dqb = dq_sc[...]
        p_o = _tn(ob_sc[...], dyb)
        p_q = _tn(hb, dqb)

        @pl.when(first)
        def _():
            ao_sc[...] = p_o
            aq_sc[...] = p_q

        @pl.when(jnp.logical_not(first))
        def _():
            ao_sc[...] += p_o
            aq_sc[...] += p_q

        @pl.when(i == n - 1)
        def _():
            dwo_ref[...] = ao_sc[...].astype(BF)
            dwq_ref[...] = aq_sc[...].astype(BF)

        dh = _nt(dqb, wcq_ref[...])
        dx, gar = _norm_bwd(dh, xt, gca_ref[...])
        dx_ref[...] = dx + dyt
        _acc_rows(dgca_ref, first, gar)

    row = lambda i: (i, 0)
    fix = lambda i: (0, 0)
    fix3 = lambda i: (0, 0, 0)
    hd = (CA_HEADS, M, CA_HD)
    return pl.pallas_call(
        body, name="ca_bwd", grid=(n,),
        in_specs=[pl.BlockSpec((tm, D), row), pl.BlockSpec((tm, D), row), pl.BlockSpec((1, D), fix),
                  pl.BlockSpec((D, D), fix), pl.BlockSpec((1, CA_HD), fix), pl.BlockSpec(hd, fix3),
                  pl.BlockSpec(hd, fix3), pl.BlockSpec((D, D), fix)],
        out_specs=[pl.BlockSpec((tm, D), row), pl.BlockSpec((D, D), fix), pl.BlockSpec((D, D), fix),
                   pl.BlockSpec(hd, fix3), pl.BlockSpec(hd, fix3), pl.BlockSpec((1, CA_HD), fix),
                   pl.BlockSpec((1, D), fix)],
        out_shape=[S((T, D), F32), S((D, D), BF), S((D, D), BF), S(hd, F32), S(hd, F32), S((1, CA_HD), F32),
                   S((1, D), F32)],
        scratch_shapes=[pltpu.VMEM((D, D), F32), pltpu.VMEM((D, D), F32), pltpu.VMEM((tm, D), BF),
                        pltpu.VMEM((tm, D), BF)],
        compiler_params=_cp(1))(x, dy, g_ca, wcq, g_cq, kn, vv, wco)


def _ca_kv_bwd(mem, g_mem, mn, kraw, dkn, dvv, wckv, g_ck):
    M, D = mem.shape

    def body(m_ref, g_ref, mn_ref, kr_ref, dkn_ref, dv_ref, w_ref, gk_ref, dw_ref, dgk_ref, dgm_ref):
        mn = mn_ref[...]
        dmn = jnp.zeros((M, D), F32)
        gk_rows = None
        for h in range(CA_HEADS):
            dkr, gr = _norm_bwd(dkn_ref[h], kr_ref[h], gk_ref[...])
            gk_rows = gr if gk_rows is None else gk_rows + gr
            dkb = dkr.astype(BF)
            dvb = dv_ref[h].astype(BF)
            dw_ref[h] = _tn(mn, dkb).astype(BF)
            dw_ref[CA_HEADS + h] = _tn(mn, dvb).astype(BF)
            dmn = dmn + _nt(dkb, w_ref[h]) + _nt(dvb, w_ref[CA_HEADS + h])
        dgk_ref[...] = jnp.sum(gk_rows, axis=0, keepdims=True)
        mf = m_ref[...]
        dgm_ref[...] = jnp.sum(dmn * (mf * _rstd(mf)), axis=0, keepdims=True)

    return pl.pallas_call(
        body, name="ca_kv_bwd",
        out_shape=[S((2 * CA_HEADS, D, CA_HD), BF), S((1, CA_HD), F32), S((1, D), F32)],
        compiler_params=pltpu.CompilerParams(vmem_limit_bytes=VMEM_LIMIT))(mem, g_mem, mn, kraw, dkn, dvv, wckv, g_ck)


def _after(g, token):
    return g if token is None else g + token[0:1, 0:1]


def _local_step(x, mem, target, small, weights, emit, h1=None):
    T, D = x.shape
    p = small
    bf128 = jnp.pad(p["b_f"], ((0, 0), (0, LANES - FOX_HEADS)))
    b_st = p["b_s"].T

    wup1 = weights("ffn1_up", x if h1 is None else h1)["wup1"]
    if h1 is None:
        a1, h1 = _ffn_up("ffn1_up", x, p["g_ffn1"], wup1)
    else:
        a1 = _ffn_up_from_h("ffn1_up", h1, wup1)
    wdn1 = weights("ffn1_dn", a1)["wdn1"]
    x1 = _ffn_down("ffn1_down", a1, wdn1, x)
    wm = weights("mix", x1)
    z, h2, qf, ka, va, yg, rs = _mix_prep(x1, p["g_mix"], wm["wz"], bf128, p["g_q"], p["g_k"], p["g_sgu"], p["w_s"],
                                          b_st, p["g_gmlp_o"])
    attn, lse = _fox_fwd(qf, ka, va)
    x2 = _mix_out(attn, yg, p["g_fox_o"], wm["wout"], x1)
    wc = weights("ca", x2)
    mn, kraw, ckn, cvv = _ca_kv(mem, p["g_mem"], wc["wckv"], p["g_ck"])
    x3 = _ca_fwd(x2, p["g_ca"], wc["wcq"], p["g_cq"], ckn, cvv, wc["wco"])
    w2 = weights("ffn2", x3)
    a2, h4 = _ffn_up("ffn2_up", x3, p["g_ffn2"], w2["wup2"])
    dy4, dy4b, sq = _ffn_down_loss("ffn2_down", a2, w2["wdn2"], x3, target)

    gs = {}
    dgu2 = _ffn_bwd_act("ffn2_bwd_act", dy4b, h4, w2["wup2"], w2["wdn2"])
    tok = emit("ffn2", {"wup2": _ffn_dwup("ffn2", h4, dgu2), "wdn2": _ffn_dwdn("ffn2", a2, dy4b)})
    dx3, gs["g_ffn2"] = _ffn_dx("ffn2_dx", dgu2, w2["wup2"], x3, _after(p["g_ffn2"], tok), dy4)

    dx2, dwcq, dwco, dckn, dcvv, gs["g_cq"], gs["g_ca"] = _ca_bwd(
        x2, dx3, p["g_ca"], wc["wcq"], p["g_cq"], ckn, cvv, wc["wco"])
    dwckv, gs["g_ck"], gs["g_mem"] = _ca_kv_bwd(mem, p["g_mem"], mn, kraw, dckn, dcvv, wc["wckv"], p["g_ck"])

    qb, dob, dyg, dwout, gs["g_fox_o"] = _mix_out_bwd(dx2, attn, yg, p["g_fox_o"], wm["wout"], qf, lse)
    dq, dk, dv = _fox_bwd(qb, ka, va, dob)
    dz, gs["g_q"], gs["g_k"], gs["g_sgu"], gs["g_gmlp_o"], gs["w_s"], dbst, dbf = _mix_prep_bwd(
        z, dq, dk, dv, dyg, rs, bf128, p["g_q"], p["g_k"], p["g_sgu"], p["w_s"], b_st, p["g_gmlp_o"])
    gs["b_s"] = dbst.T
    gs["b_f"] = dbf[:, :FOX_HEADS]
    tok_ws = emit("w_s", {"w_s": gs["w_s"]})
    zb = ZW // 3
    dwz = _tn_matmul("mix_dwz", dz, pl.BlockSpec((T, zb), lambda j: (0, j)), h2,
                     S((ZW, D), BF), pl.BlockSpec((zb, D), lambda j: (j, 0)), 3)
    tok = emit("mid", {"wcq": dwcq, "wco": dwco, "wckv": dwckv, "wout": dwout, "wz": dwz})
    dx1, dx1b, gs["g_mix"] = _mix_proj_bwd(dz, wm["wz"], x1, _after(_after(p["g_mix"], tok), tok_ws), dx2)

    dgu1 = _ffn_bwd_act("ffn1_bwd_act", dx1b, h1, wup1, wdn1)
    tok = emit("ffn1_dn", {"wdn1": _ffn_dwdn("ffn1", a1, dx1b)})
    tok = emit("ffn1_up", {"wup1": _ffn_dwup("ffn1", h1, dgu1, after=tok)})
    dx0, gs["g_ffn1"] = _ffn_dx("ffn1_dx", dgu1, wup1, x, _after(p["g_ffn1"], tok), dx1)
    return sq, dx0, gs


MESH = pl.DeviceIdType.MESH
HBM_SPEC = pl.BlockSpec(memory_space=pltpu.HBM)
N_PEER = N_DEV - 1


def _place():
    return lax.axis_index("x"), lax.axis_index("y"), lax.axis_index("c")


def _slot(px, py, pc):
    return 4 * px + 2 * py + pc


SEM_SPEC = pl.BlockSpec(memory_space=pltpu.SEMAPHORE)
ANY_SPEC = pl.BlockSpec(memory_space=pl.ANY)
DATAFLOW = pltpu.SideEffectType.DATAFLOW_SIDE_EFFECTING


def _hbm(a):
    return pltpu.with_memory_space_constraint(a, pltpu.HBM)


def _peer(x, y, c, r):
    return (1 - x if r & 4 else x, 1 - y if r & 2 else y, 1 - c if r & 1 else c)


def _place_own(srcs, whole):
    my = _slot(*_place())
    lands = []
    for s in srcs:
        blk = s[None] if whole else lax.dynamic_slice_in_dim(s, my, 1, 0)
        shape = (N_DEV,) + s.shape if whole else s.shape
        lands.append(lax.dynamic_update_slice_in_dim(lax.empty(shape, s.dtype), blk, my, 0))
    return lands


ALL_PEERS = tuple(range(1, N_DEV))
NEAR_PEERS = (1, 2, 4, 6)
SAME_CORE = (2, 4, 6)


def _copy_start(name, srcs, lands, whole, peers=None):
    n = len(srcs)
    peers = peers or [ALL_PEERS] * n
    wh = list(whole) if isinstance(whole, (list, tuple)) else [whole] * n

    def body(*refs):
        src, land = refs[:n], refs[n:2 * n]
        send, recv = refs[2 * n:3 * n], refs[3 * n:4 * n]
        token = refs[6 * n]
        x, y, c = _place()
        my = _slot(x, y, c)
        for a in range(n):
            for r in peers[a]:
                p = _peer(x, y, c, r)
                pltpu.make_async_remote_copy(
                    src_ref=src[a] if wh[a] else src[a].at[_slot(*p)], dst_ref=land[a].at[my],
                    send_sem=send[a].at[r - 1], recv_sem=recv[a].at[r - 1], device_id=p, device_id_type=MESH).start()
        token[...] = jnp.zeros_like(token)

    out = pl.pallas_call(
        body, name=name,
        out_shape=([pltpu.SemaphoreType.DMA((N_PEER,))] * (2 * n)
                   + [pltpu.HBM(s.shape, s.dtype) for s in srcs] + [pltpu.HBM(s.shape, s.dtype) for s in lands]
                   + [S((8, LANES), F32)]),
        in_specs=[HBM_SPEC] * (2 * n),
        out_specs=[SEM_SPEC] * (2 * n) + [HBM_SPEC] * (2 * n) + [pl.BlockSpec(memory_space=pltpu.VMEM)],
        input_output_aliases={i: 2 * n + i for i in range(2 * n)},
        compiler_params=pltpu.CompilerParams(has_side_effects=DATAFLOW),
    )(*[_hbm(s) for s in srcs], *[_hbm(s) for s in lands])
    return out[:n], out[n:2 * n], out[2 * n:3 * n], out[3 * n:4 * n], out[4 * n]


def _copy_wait(name, srcs, lands, send, recv, after, whole, peers=None, with_srcs=False):
    n = len(srcs)
    peers = peers or [ALL_PEERS] * n
    wh = list(whole) if isinstance(whole, (list, tuple)) else [whole] * n

    def body(*refs):
        src, land = refs[:n], refs[n:2 * n]
        snd, rcv = refs[2 * n:3 * n], refs[3 * n:4 * n]
        x, y, c = _place()
        for a in range(n):
            for r in peers[a]:
                p = _peer(x, y, c, r)
                ps = _slot(*p)
                cp = pltpu.make_async_remote_copy(
                    src_ref=src[a] if wh[a] else src[a].at[ps], dst_ref=land[a].at[ps],
                    send_sem=snd[a].at[r - 1], recv_sem=rcv[a].at[r - 1], device_id=p, device_id_type=MESH)
                cp.wait_send()
                cp.wait_recv()

    out = pl.pallas_call(
        body, name=name,
        out_shape=[pltpu.HBM(s.shape, s.dtype) for s in srcs] + [pltpu.HBM(s.shape, s.dtype) for s in lands],
        in_specs=[HBM_SPEC] * (2 * n) + [SEM_SPEC] * (2 * n) + [ANY_SPEC],
        out_specs=[HBM_SPEC] * (2 * n),
        input_output_aliases={i: i for i in range(2 * n)},
        compiler_params=pltpu.CompilerParams(has_side_effects=DATAFLOW),
    )(*srcs, *lands, *send, *recv, after)
    return (out[:n], out[n:]) if with_srcs else out[n:]


def _forward_start(name, lands):
    n = len(lands)

    def body(*refs):
        land = refs[:n]
        send, recv = refs[n:2 * n], refs[2 * n:3 * n]
        token = refs[4 * n]
        x, y, c = _place()
        for a in range(n):
            for r in SAME_CORE:
                blk = land[a].at[_slot(*_peer(x, y, c, r))]
                pltpu.make_async_remote_copy(
                    src_ref=blk, dst_ref=blk, send_sem=send[a].at[r - 1], recv_sem=recv[a].at[r - 1],
                    device_id=(x, y, 1 - c), device_id_type=MESH).start()
        token[...] = jnp.zeros_like(token)

    out = pl.pallas_call(
        body, name=name,
        out_shape=([pltpu.SemaphoreType.DMA((N_PEER,))] * (2 * n) + [pltpu.HBM(s.shape, s.dtype) for s in lands]
                   + [S((8, LANES), F32)]),
        in_specs=[HBM_SPEC] * n,
        out_specs=[SEM_SPEC] * (2 * n) + [HBM_SPEC] * n + [pl.BlockSpec(memory_space=pltpu.VMEM)],
        input_output_aliases={i: 2 * n + i for i in range(n)},
        compiler_params=pltpu.CompilerParams(has_side_effects=DATAFLOW),
    )(*[_hbm(s) for s in lands])
    return out[:n], out[n:2 * n], out[2 * n:3 * n], out[3 * n]


def _forward_wait(name, lands, send, recv, after):
    n = len(lands)

    def body(*refs):
        land = refs[:n]
        snd, rcv = refs[n:2 * n], refs[2 * n:3 * n]
        x, y, c = _place()
        for a in range(n):
            for r in SAME_CORE:
                cp = pltpu.make_async_remote_copy(
                    src_ref=land[a].at[_slot(*_peer(x, y, c, r))], dst_ref=land[a].at[_slot(*_peer(x, y, c, r | 1))],
                    send_sem=snd[a].at[r - 1], recv_sem=rcv[a].at[r - 1], device_id=(x, y, 1 - c),
                    device_id_type=MESH)
                cp.wait_send()
                cp.wait_recv()

    return pl.pallas_call(
        body, name=name,
        out_shape=[pltpu.HBM(s.shape, s.dtype) for s in lands],
        in_specs=[HBM_SPEC] * n + [SEM_SPEC] * (2 * n) + [ANY_SPEC],
        out_specs=[HBM_SPEC] * n,
        input_output_aliases={i: i for i in range(n)},
        compiler_params=pltpu.CompilerParams(has_side_effects=DATAFLOW),
    )(*lands, *send, *recv, after)


def _adamw(w, g, m, v):
    m2 = ADAM_B1 * m + (1.0 - ADAM_B1) * g
    v2 = ADAM_B2 * v + (1.0 - ADAM_B2) * (g * g)
    m_hat = m2 / (1.0 - ADAM_B1 ** ADAM_STEP)
    v_hat = v2 / (1.0 - ADAM_B2 ** ADAM_STEP)
    delta = -ADAM_LR * (m_hat / (jnp.sqrt(v_hat) + ADAM_EPS) + ADAM_WD * w)
    return delta, m2, v2


def _adamw_big(name, slots, w, m, v, own=None):
    R, C = w.shape
    tr = next((t for t in (128, 176, 64) if R % t == 0 and R // t >= 2), R)

    def finish(g, w_ref, m_ref, v_ref, g_ref, d_ref, m2_ref, v2_ref):
        d, m2, v2 = _adamw(w_ref[...], g, m_ref[...], v_ref[...])
        g_ref[...] = g
        d_ref[...] = d
        m2_ref[...] = m2
        v2_ref[...] = v2

    if own is None:
        def body(s_ref, *refs):
            g = s_ref[0].astype(F32)
            for k in range(1, N_DEV):
                g = g + s_ref[k].astype(F32)
            finish(g, *refs)

        row = pl.BlockSpec((tr, C), lambda i: (i, 0))
        return pl.pallas_call(
            body, name=name, grid=(R // tr,),
            in_specs=[pl.BlockSpec((N_DEV, tr, C), lambda i: (0, i, 0)), row, row, row],
            out_specs=[row] * 4, out_shape=[S((R, C), F32)] * 4,
            compiler_params=_cp(1))(slots, w, m, v)

    def body(my_ref, s_ref, own_ref, *refs):
        mine = own_ref[...]
        g = None
        for k in range(N_DEV):
            part = jnp.where(my_ref[0] == k, mine, s_ref[k]).astype(F32)
            g = part if g is None else g + part
        finish(g, *refs)

    row = pl.BlockSpec((tr, C), lambda i, my_ref: (i, 0))
    my = jnp.reshape(_slot(*_place()), (1,)).astype(jnp.int32)
    return pl.pallas_call(
        body, name=name,
        grid_spec=pltpu.PrefetchScalarGridSpec(
            num_scalar_prefetch=1, grid=(R // tr,),
            in_specs=[pl.BlockSpec((N_DEV, tr, C), lambda i, my_ref: (0, i, 0)),
                      pl.BlockSpec((None, tr, C), lambda i, my_ref: (my_ref[0], i, 0)), row, row, row],
            out_specs=[row] * 4),
        out_shape=[S((R, C), F32)] * 4, compiler_params=_cp(1))(my, slots, own, w, m, v)


TINY_ROWS = (("b_s", 8), ("g_ffn1", 8), ("g_mix", 8), ("g_ca", 8), ("g_mem", 8), ("g_ffn2", 8), ("g_sgu", 4),
             ("g_fox_o", 4), ("g_gmlp_o", 4), ("g_cq", 2), ("g_ck", 2), ("g_q", 1), ("g_k", 1), ("b_f", 1),
             ("loss", 1))
TINY_P = 72


def _tiny_pieces(width):
    return [(j, slice(j * LANES, min((j + 1) * LANES, width))) for j in range(-(-width // LANES))]


def _pack_tiny(grads, sq):
    names = [n for n, _ in TINY_ROWS if n != "loss"]

    def body(*refs):
        ins, sq_ref, o_ref = refs[:len(names)], refs[len(names)], refs[len(names) + 1]
        o_ref[...] = jnp.zeros_like(o_ref)
        at = 0
        for ref, (name, r) in zip(ins, TINY_ROWS):
            if name == "b_s":
                o_ref[at:at + r, :] = ref[...]
            else:
                for j, cols in _tiny_pieces(ref.shape[1]):
                    o_ref[at + j:at + j + 1, 0:cols.stop - cols.start] = ref[:, cols]
            at += r
        o_ref[at:at + 1, :] = sq_ref[0:1, :]

    return pl.pallas_call(body, name="tiny_pack", out_shape=S((TINY_P, LANES), F32))(
        *[grads[n] for n in names], sq)


def _adamw_tiny(slots, w, m, v):
    names = [n for n, _ in TINY_ROWS if n != "loss"]
    k = len(names)

    def body(s_ref, *refs):
        ins, outs, loss_ref = refs[:3 * k], refs[3 * k:7 * k], refs[7 * k]
        g_all = s_ref[0]
        for d in range(1, N_DEV):
            g_all = g_all + s_ref[d]
        at = 0
        for i, (name, r) in enumerate(TINY_ROWS[:k]):
            w_ref, m_ref, v_ref = ins[i], ins[k + i], ins[2 * k + i]
            o = outs[4 * i:4 * i + 4]
            if name == "b_s":
                pieces = [(slice(at, at + r), slice(0, LANES), (slice(None), slice(None)))]
            else:
                pieces = [(slice(at + j, at + j + 1), slice(0, c.stop - c.start), (slice(None), c))
                          for j, c in _tiny_pieces(w_ref.shape[1])]
            for rows, lanes, dst in pieces:
                g = g_all[rows, lanes]
                res = (g,) + _adamw(w_ref[dst], g, m_ref[dst], v_ref[dst])
                for ref, val in zip(o, res):
                    ref[dst] = val
            at += r
        loss_ref[...] = g_all[at:at + 1, :]

    shapes = [S(w[n].shape, F32) for n in names]
    out = pl.pallas_call(
        body, name="adamw_tiny", out_shape=[s for s in shapes for _ in range(4)] + [S((1, LANES), F32)],
    )(slots, *[w[n] for n in names], *[m[n] for n in names], *[v[n] for n in names])
    stores = ({}, {}, {}, {})
    for i, n in enumerate(names):
        for store, t in zip(stores, out[4 * i:4 * i + 4]):
            store[n] = t
    return stores, out[4 * k]


WEIGHTS =('g_ffn1', 'w_ffn1_in', 'w_ffn1_out', 'g_mix', 'w_in', 'b_f', 'g_q', 'g_k', 'g_sgu', 'w_s', 'b_s',
           'g_fox_o', 'g_gmlp_o', 'w_out', 'g_ca', 'g_mem', 'w_cq', 'w_ckv', 'g_cq', 'g_ck', 'w_co', 'g_ffn2',
           'w_ffn2_in', 'w_ffn2_out')
BIG = ('w_ffn1_in', 'w_ffn1_out', 'w_in', 'w_out', 'w_cq', 'w_ckv', 'w_co', 'w_ffn2_in', 'w_ffn2_out')
TRANSPOSED = ('w_ffn1_in', 'w_in', 'w_ffn2_in')
TWO_LEVEL = ('w_ffn1_in', 'w_in')
GATHER_GROUPS = {"ffn1_up": ("w_ffn1_in",), "ffn1_dn": ("w_ffn1_out",), "mix": ("w_in", "w_out"),
                 "ca": ("w_cq", "w_ckv", "w_co"), "ffn2": ("w_ffn2_in", "w_ffn2_out")}
QKV_W = 3 * FOX_W
UV_OFF = QKV_W + FOX_HEADS


def kernel(x, mem, g_ffn1, w_ffn1_in, w_ffn1_out, g_mix, w_in, b_f, g_q, g_k, g_sgu, w_s, b_s, g_fox_o, g_gmlp_o, w_out, g_ca, g_mem, w_cq, w_ckv, g_cq, g_ck, w_co, g_ffn2, w_ffn2_in, w_ffn2_out, loss_target, m_g_ffn1, m_w_ffn1_in, m_w_ffn1_out, m_g_mix, m_w_in, m_b_f, m_g_q, m_g_k, m_g_sgu, m_w_s, m_b_s, m_g_fox_o, m_g_gmlp_o, m_w_out, m_g_ca, m_g_mem, m_w_cq, m_w_ckv, m_g_cq, m_g_ck, m_w_co, m_g_ffn2, m_w_ffn2_in, m_w_ffn2_out, v_g_ffn1, v_w_ffn1_in, v_w_ffn1_out, v_g_mix, v_w_in, v_b_f, v_g_q, v_g_k, v_g_sgu, v_w_s, v_b_s, v_g_fox_o, v_g_gmlp_o, v_w_out, v_g_ca, v_g_mem, v_w_cq, v_w_ckv, v_g_cq, v_g_ck, v_w_co, v_g_ffn2, v_w_ffn2_in, v_w_ffn2_out):
    args = dict(locals())
    w = {n: args[n] for n in WEIGHTS}
    mo = {n: args["m_" + n] for n in WEIGHTS}
    vo = {n: args["v_" + n] for n in WEIGHTS}
    D = D_MODEL

    def local(n, a):
        return a[0].T if n in TRANSPOSED else a[0]

    g_peers = [NEAR_PEERS if n in TWO_LEVEL else ALL_PEERS for n in BIG]
    handles = {}

    def start_gather(name, names, arrays):
        snd, rcv, src, land, token = _copy_start(name, arrays, _place_own(arrays, True), True,
                                                 peers=[g_peers[BIG.index(n)] for n in names])
        handles.update({n: (src[i], land[i], snd[i], rcv[i]) for i, n in enumerate(names)})
        return token

    first = local(BIG[0], w[BIG[0]]).astype(BF)
    fb = first.shape[0]
    token_first = start_gather("gather_start_first", BIG[:1], [first])
    token_rest = start_gather("gather_start_rest", BIG[1:],
                              [(local(n, w[n]) + token_first[0:1, 0:1]).astype(BF) for n in BIG[1:]])

    tiny_names = [n for n, _ in TINY_ROWS if n != "loss"]

    def weights(group, after):
        names = GATHER_GROUPS[group]
        hs = [handles[n] for n in names]
        got = list(_copy_wait("gather_wait_" + group, [h[0] for h in hs], [h[1] for h in hs], [h[2] for h in hs],
                              [h[3] for h in hs], after, True,
                              peers=[g_peers[BIG.index(n)] for n in names]))
        passed = [i for i, n in enumerate(names) if n in TWO_LEVEL]
        if passed:
            f_snd, f_rcv, f_land, f_token = _forward_start("gather_pass_start_" + group, [got[i] for i in passed])
            for i, t in zip(passed, _forward_wait("gather_pass_wait_" + group, f_land, f_snd, f_rcv, f_token)):
                got[i] = t
        got = dict(zip(names, got))
        if group == "ffn1_up":
            return {"wup1": got["w_ffn1_in"].reshape(2, N_FFN_BLK, fb, D)}
        if group == "ffn1_dn":
            return {"wdn1": got["w_ffn1_out"].reshape(N_FFN_BLK, fb, D)}
        if group == "mix":
            full = got["w_in"].reshape(-1, D)
            wz = jnp.concatenate([full[:QKV_W], full[UV_OFF:], full[QKV_W:UV_OFF],
                                  jnp.zeros((LANES - FOX_HEADS, D), BF)], axis=0)
            return {"wz": wz, "wout": got["w_out"].reshape(D, D)}
        if group == "ca":
            return {"wcq": got["w_cq"].reshape(D, D), "wco": got["w_co"].reshape(D, D), "wckv": got["w_ckv"]}
        return {"wup2": got["w_ffn2_in"].reshape(2, N_FFN_BLK, fb, D),
                "wdn2": got["w_ffn2_out"].reshape(N_FFN_BLK, fb, D)}

    flying = {}

    def emit(group, g):
        if group == "w_s":
            flying[group] = g["w_s"].reshape(-1, LANES)
            return None
        if group == "ffn2":
            parts = {"w_ffn2_in": g["wup2"], "w_ffn2_out": g["wdn2"].reshape(N_DEV, -1, D)}
        elif group == "ffn1_dn":
            parts = {"w_ffn1_out": g["wdn1"].reshape(N_DEV, -1, D)}
        elif group == "ffn1_up":
            parts = {"w_ffn1_in": g["wup1"]}
        else:
            gz = g["wz"]
            g_in = jnp.concatenate([gz[:QKV_W], gz[Z_F:Z_F + FOX_HEADS], gz[QKV_W:Z_F]], axis=0)
            parts = {"w_in": g_in.reshape(N_DEV, -1, D).astype(BF),
                     "w_out": g["wout"].reshape(N_DEV, -1, D), "w_cq": g["wcq"].reshape(N_DEV, -1, D),
                     "w_co": g["wco"].reshape(N_DEV, -1, D), "w_ckv": g["wckv"]}
        names = list(parts)
        srcs = [parts[n] for n in names]
        lands = [lax.empty(s.shape, s.dtype) for s in srcs]
        whole = [False] * len(srcs)
        if group == "mid":
            ws_part = flying.pop("w_s")
            names, srcs, whole = names + ["w_s"], srcs + [ws_part], whole + [True]
            lands += _place_own([ws_part], True)
        *copies, token = _copy_start("exchange_start_" + group, srcs, lands, whole)
        flying[group] = (names, copies, whole)
        return token

    small = {n: (w[n][0] if n == "b_s" else w[n]) for n in tiny_names}
    small["w_s"] = w["w_s"][0]

    h1 = _rms_cast("ffn1_norm", x[0], w["g_ffn1"], token_rest)
    sq, dx0, gs = _local_step(x[0], mem[0], loss_target[0], small, weights, emit, h1=h1)

    sm_parts = [_pack_tiny(gs, sq)]
    sm_snd, sm_rcv, sm_src, sm_land, sm_token = _copy_start("tiny_start", sm_parts, _place_own(sm_parts, True), True)

    grad, delta, new_m, new_v = {}, {}, {}, {}

    def update(group, after):
        names, (snd, rcv, srcs, lands), whole = flying[group]
        owns, slots = _copy_wait("exchange_wait_" + group, srcs, lands, snd, rcv, after, whole, with_srcs=True)
        for n, sl, own in zip(names, slots, owns):
            if n == "w_s":
                g, d, m2, v2 = _adamw_big("adamw_w_s", sl, *[a[n].reshape(-1, LANES) for a in (w, mo, vo)])
            else:
                g, d, m2, v2 = _adamw_big("adamw_" + n, sl, local(n, w[n]), local(n, mo[n]), local(n, vo[n]),
                                          own=own)
            grad[n], delta[n], new_m[n], new_v[n] = (
                (t.T if n in TRANSPOSED else t).reshape(w[n].shape) for t in (g, d, m2, v2))
        return d

    last = update("ffn2", sm_token)
    last = update("mid", last)
    last = update("ffn1_dn", last)
    last = update("ffn1_up", last)
    tiny_all, = _copy_wait("tiny_wait", sm_src, sm_land, sm_snd, sm_rcv, last, True)
    stores, loss_row = _adamw_tiny(tiny_all, *[{n: (a[n][0] if n == "b_s" else a[n]) for n in tiny_names}
                                               for a in (w, mo, vo)])
    for store, t in zip((grad, delta, new_m, new_v), stores):
        store.update({n: v.reshape(w[n].shape) for n, v in t.items()})
    loss = loss_row[0, 0] * (0.5 / D)

    return (loss, dx0[None], *[grad[n] for n in WEIGHTS], *[delta[n] for n in WEIGHTS],
            *[new_m[n] for n in WEIGHTS], *[new_v[n] for n in WEIGHTS])
```

```python
import functools

import jax
import jax.numpy as jnp
from jax import lax
from jax.experimental import pallas as pl
from jax.experimental.pallas import tpu as pltpu

F32 = jnp.float32
BF = jnp.bfloat16
S = jax.ShapeDtypeStruct

N_DEV = 8
D_MODEL = 1024
FOX_HEADS, FOX_HD = 8, 64
FOX_W = 512
GMLP_G, GMLP_GD = 8, 64
GMLP_W = 512
CHUNK = 128
CA_HEADS, CA_HD = 4, 256
N_FFN_BLK = 4
ZW = 2688
Z_Q, Z_K, Z_V, Z_U, Z_G, Z_F = 0, 512, 1024, 1536, 2048, 2560
EPS = 1e-6
NEG = -1e30
LANES = 128

ADAM_LR, ADAM_B1, ADAM_B2, ADAM_EPS, ADAM_WD, ADAM_STEP = 0.001, 0.9, 0.999, 1e-08, 0.01, 10

VMEM_LIMIT = 52 * 2 ** 20


def _cp(n_axes):
    return pltpu.CompilerParams(dimension_semantics=("arbitrary",) * n_axes, vmem_limit_bytes=VMEM_LIMIT)


def _nn(a, b):
    return jnp.dot(a, b, preferred_element_type=F32)


def _nt(a, b):
    return lax.dot_general(a, b, (((1,), (1,)), ((), ())), preferred_element_type=F32)


def _tn(a, b):
    return lax.dot_general(a, b, (((0,), (0,)), ((), ())), preferred_element_type=F32)


def _hi(mask, x):
    return jnp.dot(mask.astype(F32), x, precision=lax.Precision.HIGHEST, preferred_element_type=F32)


def _hi3(mask, x):
    mb = mask.astype(BF)
    hi = x.astype(BF)
    r1 = x - hi.astype(F32)
    mid = r1.astype(BF)
    lo = (r1 - mid.astype(F32)).astype(BF)
    return _nn(mb, hi) + _nn(mb, mid) + _nn(mb, lo)


def _rstd(x):
    return lax.rsqrt(jnp.mean(x * x, axis=-1, keepdims=True) + EPS)


def _norm_bwd(dy, x, g, r=None):
    r = _rstd(x) if r is None else r
    xh = x * r
    dxh = dy * g
    dx = r * (dxh - xh * jnp.mean(dxh * xh, axis=-1, keepdims=True))
    return dx, dy * xh


def _acc_rows(ref, first, val):
    srow = jnp.sum(val, axis=0, keepdims=True)

    @pl.when(first)
    def _():
        ref[...] = srow

    @pl.when(jnp.logical_not(first))
    def _():
        ref[...] += srow


def _gelu(x):
    c = 0.7978845608028654
    return 0.5 * x * (1.0 + jnp.tanh(c * (x + 0.044715 * x * x * x)))


def _gelu_grad(x):
    c = 0.7978845608028654
    t = jnp.tanh(c * (x + 0.044715 * x * x * x))
    return 0.5 * (1.0 + t) + 0.5 * x * (1.0 - t * t) * c * (1.0 + 3 * 0.044715 * x * x)


def _tile(n, pref):
    return pref if n % pref == 0 else n


def _rms_cast(name, x, g, after):
    T, D = x.shape
    tm = _tile(T, 1024)

    def body(x_ref, g_ref, t_ref, h_ref):
        xf = x_ref[...]
        h_ref[...] = (xf * _rstd(xf) * g_ref[...]).astype(BF)

    return pl.pallas_call(
        body, name=name, grid=(T // tm,),
        in_specs=[pl.BlockSpec((tm, D), lambda i: (i, 0)), pl.BlockSpec((1, D), lambda i: (0, 0)),
                  pl.BlockSpec((8, LANES), lambda i: (0, 0))],
        out_specs=pl.BlockSpec((tm, D), lambda i: (i, 0)), out_shape=S((T, D), BF),
        compiler_params=_cp(1))(x, g, after)


def _ffn_up_from_h(name, h, wup):
    T, D = h.shape
    FB = wup.shape[-2]
    tm = _tile(T, 1024)

    def body(h_ref, w_ref, a_ref):
        hb = h_ref[...]
        gg = _nt(hb, w_ref[0])
        uu = _nt(hb, w_ref[1])
        a_ref[...] = (gg * jax.nn.sigmoid(gg) * uu).astype(BF)

    return pl.pallas_call(
        body, name=name, grid=(T // tm, N_FFN_BLK),
        in_specs=[pl.BlockSpec((tm, D), lambda i, j: (i, 0)),
                  pl.BlockSpec((2, None, FB, D), lambda i, j: (0, j, 0, 0))],
        out_specs=pl.BlockSpec((None, tm, FB), lambda i, j: (j, i, 0)),
        out_shape=S((N_FFN_BLK, T, FB), BF),
        compiler_params=_cp(2))(h, wup)


def _ffn_up(name, x, g, wup):
    T, D = x.shape
    FB = wup.shape[-2]
    tm = _tile(T, 1024)

    def body(x_ref, g_ref, w_ref, a_ref, h_ref):
        @pl.when(pl.program_id(1) == 0)
        def _():
            xf = x_ref[...]
            h_ref[...] = (xf * _rstd(xf) * g_ref[...]).astype(BF)

        hb = h_ref[...]
        gg = _nt(hb, w_ref[0])
        uu = _nt(hb, w_ref[1])
        a_ref[...] = (gg * jax.nn.sigmoid(gg) * uu).astype(BF)

    return pl.pallas_call(
        body, name=name, grid=(T // tm, N_FFN_BLK),
        in_specs=[pl.BlockSpec((tm, D), lambda i, j: (i, 0)),
                  pl.BlockSpec((1, D), lambda i, j: (0, 0)),
                  pl.BlockSpec((2, None, FB, D), lambda i, j: (0, j, 0, 0))],
        out_specs=[pl.BlockSpec((None, tm, FB), lambda i, j: (j, i, 0)),
                   pl.BlockSpec((tm, D), lambda i, j: (i, 0))],
        out_shape=[S((N_FFN_BLK, T, FB), BF), S((T, D), BF)],
        compiler_params=_cp(2))(x, g, wup)


def _ffn_down(name, a, wdn, x):
    _, T, FB = a.shape
    D = x.shape[1]
    tm = _tile(T, 512)

    def body(a_ref, w_ref, x_ref, o_ref):
        p = _nn(a_ref[0], w_ref[0])
        for j in range(1, N_FFN_BLK):
            p = p + _nn(a_ref[j], w_ref[j])
        o_ref[...] = x_ref[...] + 0.5 * p

    return pl.pallas_call(
        body, name=name, grid=(T // tm,),
        in_specs=[pl.BlockSpec((N_FFN_BLK, tm, FB), lambda i: (0, i, 0)),
                  pl.BlockSpec((N_FFN_BLK, FB, D), lambda i: (0, 0, 0)),
                  pl.BlockSpec((tm, D), lambda i: (i, 0))],
        out_specs=pl.BlockSpec((tm, D), lambda i: (i, 0)),
        out_shape=S((T, D), F32),
        compiler_params=_cp(1))(a, wdn, x)


def _ffn_down_loss(name, a, wdn, x, target):
    _, T, FB = a.shape
    D = x.shape[1]
    tm = _tile(T, 512)

    def body(a_ref, w_ref, x_ref, t_ref, d_ref, db_ref, loss_ref):
        i = pl.program_id(0)
        p = _nn(a_ref[0], w_ref[0])
        for j in range(1, N_FFN_BLK):
            p = p + _nn(a_ref[j], w_ref[j])
        diff = (x_ref[...] + 0.5 * p) - t_ref[...]
        dy = diff * (1.0 / D)
        d_ref[...] = dy
        db_ref[...] = dy.astype(BF)
        sq = jnp.zeros((8, LANES), F32) + jnp.sum(diff * diff)

        @pl.when(i == 0)
        def _():
            loss_ref[...] = sq

        @pl.when(i > 0)
        def _():
            loss_ref[...] += sq

    row = pl.BlockSpec((tm, D), lambda i: (i, 0))
    return pl.pallas_call(
        body, name=name, grid=(T // tm,),
        in_specs=[pl.BlockSpec((N_FFN_BLK, tm, FB), lambda i: (0, i, 0)),
                  pl.BlockSpec((N_FFN_BLK, FB, D), lambda i: (0, 0, 0)), row, row],
        out_specs=[row, row, pl.BlockSpec((8, LANES), lambda i: (0, 0))],
        out_shape=[S((T, D), F32), S((T, D), BF), S((8, LANES), F32)],
        compiler_params=_cp(1))(a, wdn, x, target)


def _ffn_bwd_act(name, dyb, h, wup, wdn):
    T, D = h.shape
    FB = wup.shape[-2]
    tm = _tile(T, 1024)

    def body(d_ref, h_ref, wu_ref, wd_ref, o_ref):
        da = 0.5 * _nt(d_ref[...], wd_ref[...])
        hb = h_ref[...]
        gg = _nt(hb, wu_ref[0])
        uu = _nt(hb, wu_ref[1])
        sg = jax.nn.sigmoid(gg)
        o_ref[0] = (da * uu * (sg * (1.0 + gg * (1.0 - sg)))).astype(BF)
        o_ref[1] = (da * (gg * sg)).astype(BF)

    return pl.pallas_call(
        body, name=name, grid=(T // tm, N_FFN_BLK),
        in_specs=[pl.BlockSpec((tm, D), lambda i, j: (i, 0)),
                  pl.BlockSpec((tm, D), lambda i, j: (i, 0)),
                  pl.BlockSpec((2, None, FB, D), lambda i, j: (0, j, 0, 0)),
                  pl.BlockSpec((None, FB, D), lambda i, j: (j, 0, 0))],
        out_specs=pl.BlockSpec((2, None, tm, FB), lambda i, j: (0, j, i, 0)),
        out_shape=S((2, N_FFN_BLK, T, FB), BF),
        compiler_params=_cp(2))(dyb, h, wup, wdn)


def _ffn_dx(name, dgu, wup, x, g, dy):
    T, D = x.shape
    FB = wup.shape[-2]
    tm = _tile(T, 512)

    def body(d_ref, w_ref, x_ref, g_ref, dy_ref, dx_ref, dg_ref):
        p = None
        for j in range(N_FFN_BLK):
            for half in range(2):
                t = _nn(d_ref[half, j], w_ref[half, j])
                p = t if p is None else p + t
        dx, dgr = _norm_bwd(p, x_ref[...], g_ref[...])
        dx_ref[...] = dx + dy_ref[...]
        _acc_rows(dg_ref, pl.program_id(0) == 0, dgr)

    return pl.pallas_call(
        body, name=name, grid=(T // tm,),
        in_specs=[pl.BlockSpec((2, N_FFN_BLK, tm, FB), lambda i: (0, 0, i, 0)),
                  pl.BlockSpec((2, N_FFN_BLK, FB, D), lambda i: (0, 0, 0, 0), pipeline_mode=pl.Buffered(1)),
                  pl.BlockSpec((tm, D), lambda i: (i, 0)),
                  pl.BlockSpec((1, D), lambda i: (0, 0)),
                  pl.BlockSpec((tm, D), lambda i: (i, 0))],
        out_specs=[pl.BlockSpec((tm, D), lambda i: (i, 0)),
                   pl.BlockSpec((1, D), lambda i: (0, 0))],
        out_shape=[S((T, D), F32), S((1, D), F32)],
        compiler_params=_cp(1))(dgu, wup, x, g, dy)


def _tn_matmul(name, a, a_spec, b, out_shape, out_spec, n_blocks, scale=1.0, after=None):
    extra = [] if after is None else [after]

    def body(a_ref, b_ref, *rest):
        o_ref = rest[-1]
        o_ref[...] = (_tn(a_ref[...], b_ref[...]) * scale).astype(o_ref.dtype)

    return pl.pallas_call(
        body, name=name, grid=(n_blocks,),
        in_specs=[a_spec, pl.BlockSpec(b.shape, lambda j: (0, 0), pipeline_mode=pl.Buffered(1))]
        + [pl.BlockSpec((8, LANES), lambda j: (0, 0)) for _ in extra],
        out_specs=out_spec, out_shape=out_shape, compiler_params=_cp(1))(a, b, *extra)


def _ffn_dwup(name, h, dgu, after=None):
    T, D = h.shape
    FB = dgu.shape[-1]
    return _tn_matmul(
        name + "_dwup", dgu.reshape(2 * N_FFN_BLK, T, FB), pl.BlockSpec((None, T, FB), lambda j: (j, 0, 0)), h,
        S((2 * N_FFN_BLK, FB, D), BF), pl.BlockSpec((None, FB, D), lambda j: (j, 0, 0)), 2 * N_FFN_BLK,
        after=after)


def _ffn_dwdn(name, a, dyb):
    _, T, FB = a.shape
    D = dyb.shape[1]
    return _tn_matmul(
        name + "_dwdn", a, pl.BlockSpec((None, T, FB), lambda j: (j, 0, 0)), dyb,
        S((N_FFN_BLK, FB, D), BF), pl.BlockSpec((None, FB, D), lambda j: (j, 0, 0)), N_FFN_BLK, scale=0.5)


def _tri(n, lower):
    r = lax.broadcasted_iota(jnp.int32, (n, n), 0)
    c = lax.broadcasted_iota(jnp.int32, (n, n), 1)
    return (r >= c) if lower else (r <= c)


def _spatial_mix(vgn_b, ws_ref, bst, tm):
    tril = _tri(CHUNK, True)
    wms = [jnp.where(tril, ws_ref[g], 0.0).astype(BF) for g in range(GMLP_G)]
    rows = []
    for c in range(tm // CHUNK):
        cols = []
        for g in range(GMLP_G):
            vs = vgn_b[c * CHUNK:(c + 1) * CHUNK, g * GMLP_GD:(g + 1) * GMLP_GD]
            cols.append(_nn(wms[g], vs) + bst[:, g:g + 1])
        rows.append(jnp.concatenate(cols, axis=1))
    return jnp.concatenate(rows, axis=0), wms


HB = 128
AUG_W = FOX_HEADS * HB
COL_A, COL_B, COL_C = 64, 67, 70
RS_Q, RS_K, RS_V, RS_O = 0, 8, 16, 17


def _piece_matrix(col):
    r = jnp.arange(LANES)
    dst = jnp.where(r < 3 * FOX_HEADS, (r % FOX_HEADS) * HB + col + r // FOX_HEADS, -1)
    return (jnp.arange(AUG_W)[None, :] == dst[:, None]).astype(BF)


def _ones_row(cols):
    c = jnp.arange(AUG_W) % HB
    hit = functools.reduce(jnp.logical_or, [(c >= a) & (c < a + 3) for a in cols])
    return hit.astype(F32)[None, :]


def _pieces(x):
    lane = lax.broadcasted_iota(jnp.int32, x.shape, 1)
    x = jnp.where(lane < FOX_HEADS, x, 0.0)
    hi = x.astype(BF).astype(F32)
    r1 = x - hi
    mid = r1.astype(BF).astype(F32)
    lo = (r1 - mid).astype(BF).astype(F32)
    return (hi + pltpu.roll(mid, FOX_HEADS, 1) + pltpu.roll(lo, 2 * FOX_HEADS, 1)).astype(BF)


def _mix_prep(x, g_mix, wz, bf128, g_q, g_k, g_sgu, w_s, b_st, g_go):
    T, D = x.shape
    tm = _tile(T, 512)
    pc_q, pc_k = _piece_matrix(COL_A), _piece_matrix(COL_B)
    one_q, one_k, one_v = _ones_row([COL_B]), _ones_row([COL_A, COL_C]), _ones_row([COL_A])

    def body(x_ref, gm_ref, wz_ref, bf_ref, gq_ref, gk_ref, gs_ref, ws_ref, bst_ref, go_ref, pq_ref, pk_ref, oq_ref,
             ok_ref, ov_ref, z_ref, h_ref, q_ref, k_ref, v_ref, y_ref, rs_ref, carry_ref):
        i = pl.program_id(0)

        @pl.when(i == 0)
        def _():
            carry_ref[...] = jnp.zeros_like(carry_ref)

        xf = x_ref[...]
        hb = (xf * _rstd(xf) * gm_ref[...]).astype(BF)
        h_ref[...] = hb
        z_ref[...] = _nt(hb, wz_ref[...])

        fl = z_ref[:, Z_F:Z_F + LANES] + bf_ref[...]
        logf = jnp.minimum(fl, 0.0) - jnp.log1p(jnp.exp(-jnp.abs(fl)))
        csum = _hi(_tri(tm, True), logf) + carry_ref[...]
        carry_ref[...] = csum[tm - 1:tm, :]
        ext_q = (_nn(_pieces(csum), pq_ref[...]) + oq_ref[...]).astype(BF)
        ext_k = (_nn(_pieces(-csum), pk_ref[...]) + ok_ref[...]).astype(BF)
        ext_v = jnp.broadcast_to(ov_ref[...], (tm, AUG_W)).astype(BF)

        rs_ref[...] = jnp.zeros_like(rs_ref)
        for h in range(FOX_HEADS):
            lo, hi = slice(h * HB, h * HB + FOX_HD), slice(h * HB + FOX_HD, (h + 1) * HB)
            qh = z_ref[:, Z_Q + h * FOX_HD:Z_Q + (h + 1) * FOX_HD]
            kh = z_ref[:, Z_K + h * FOX_HD:Z_K + (h + 1) * FOX_HD]
            rq, rk = _rstd(qh), _rstd(kh)
            rs_ref[:, RS_Q + h:RS_Q + h + 1] = rq
            rs_ref[:, RS_K + h:RS_K + h + 1] = rk
            q_ref[:, lo] = (qh * rq * gq_ref[...] * 0.125).astype(BF)
            k_ref[:, lo] = (kh * rk * gk_ref[...]).astype(BF)
            v_ref[:, lo] = z_ref[:, Z_V + h * FOX_HD:Z_V + (h + 1) * FOX_HD].astype(BF)
            q_ref[:, hi] = ext_q[:, hi]
            k_ref[:, hi] = ext_k[:, hi]
            v_ref[:, hi] = ext_v[:, hi]

        u = _gelu(z_ref[:, Z_U:Z_U + GMLP_W])
        vg = _gelu(z_ref[:, Z_G:Z_G + GMLP_W])
        rv = _rstd(vg)
        vgn = (vg * rv * gs_ref[...]).astype(BF)
        mixed, _ = _spatial_mix(vgn, ws_ref, bst_ref[...], tm)
        sgu = u * mixed
        ro = _rstd(sgu)
        y_ref[...] = (sgu * ro * go_ref[...]).astype(BF)
        rs_ref[:, RS_V:RS_V + 1] = rv
        rs_ref[:, RS_O:RS_O + 1] = ro

    row = lambda i: (i, 0)
    fix2 = lambda i: (0, 0)
    return pl.pallas_call(
        body, name="mix_prep", grid=(T // tm,),
        in_specs=[pl.BlockSpec((tm, D), row), pl.BlockSpec((1, D), fix2),
                  pl.BlockSpec((ZW, D), fix2, pipeline_mode=pl.Buffered(1)),
                  pl.BlockSpec((1, LANES), fix2), pl.BlockSpec((1, FOX_HD), fix2), pl.BlockSpec((1, FOX_HD), fix2),
                  pl.BlockSpec((1, GMLP_W), fix2), pl.BlockSpec((GMLP_G, CHUNK, CHUNK), lambda i: (0, 0, 0)),
                  pl.BlockSpec((CHUNK, GMLP_G), fix2), pl.BlockSpec((1, GMLP_W), fix2),
                  pl.BlockSpec((LANES, AUG_W), fix2),
                  pl.BlockSpec((LANES, AUG_W), fix2), pl.BlockSpec((1, AUG_W), fix2), pl.BlockSpec((1, AUG_W), fix2),
                  pl.BlockSpec((1, AUG_W), fix2)],
        out_specs=[pl.BlockSpec((tm, ZW), row), pl.BlockSpec((tm, D), row),
                   pl.BlockSpec((tm, AUG_W), row), pl.BlockSpec((tm, AUG_W), row), pl.BlockSpec((tm, AUG_W), row),
                   pl.BlockSpec((tm, GMLP_W), row), pl.BlockSpec((tm, LANES), row)],
        out_shape=[S((T, ZW), F32), S((T, D), BF), S((T, AUG_W), BF), S((T, AUG_W), BF), S((T, AUG_W), BF),
                   S((T, GMLP_W), BF), S((T, LANES), F32)],
        scratch_shapes=[pltpu.VMEM((1, LANES), F32)],
        compiler_params=_cp(1))(x, g_mix, wz, bf128, g_q, g_k, g_sgu, w_s, b_st, g_go, pc_q, pc_k, one_q, one_k,
                                one_v)


def _fox_fwd(q, k, v):
    T = q.shape[0]
    tq = _tile(T, 1024)
    nq = T // tq

    pairs = [(i, j) for i in range(nq) for j in range(i + 1)]
    it = jnp.asarray([p[0] for p in pairs], jnp.int32)
    jt = jnp.asarray([p[1] for p in pairs], jnp.int32)

    def body(it_ref, jt_ref, q_ref, k_ref, v_ref, o_ref, lse_ref, m_sc, acc_sc):
        t = pl.program_id(0)
        i, j = it_ref[t], jt_ref[t]

        @pl.when(j == 0)
        def _():
            m_sc[...] = jnp.full(m_sc.shape, NEG, F32)
            acc_sc[...] = jnp.zeros_like(acc_sc)

        def step(masked):
            mask = _tri(tq, True) if masked else None
            for h in range(FOX_HEADS):
                hb = slice(h * HB, (h + 1) * HB)
                s = _nt(q_ref[:, hb], k_ref[:, hb])
                if masked:
                    s = jnp.where(mask, s, NEG)
                m_prev = m_sc[h]
                m_new = jnp.maximum(m_prev, jnp.broadcast_to(jnp.max(s, axis=1, keepdims=True), (tq, HB)))
                p = jnp.exp(s - jnp.tile(m_new, (1, tq // HB))).astype(BF)
                acc_sc[:, hb] = jnp.exp(m_prev - m_new) * acc_sc[:, hb] + _nn(p, v_ref[:, hb])
                m_sc[h] = m_new

        @pl.when(j < i)
        def _():
            step(False)

        @pl.when(j == i)
        def _():
            step(True)
            lse_ref[...] = jnp.zeros_like(lse_ref)
            for h in range(FOX_HEADS):
                l = acc_sc[:, h * HB + COL_A:h * HB + COL_A + 1]
                o_ref[:, h * FOX_HD:(h + 1) * FOX_HD] = acc_sc[:, h * HB:h * HB + FOX_HD] / l
                lse_ref[:, h:h + 1] = m_sc[h][:, 0:1] + jnp.log(l)

    qi = lambda t, it_ref, jt_ref: (it_ref[t], 0)
    kj = lambda t, it_ref, jt_ref: (jt_ref[t], 0)
    return pl.pallas_call(
        body, name="fox_fwd",
        grid_spec=pltpu.PrefetchScalarGridSpec(
            num_scalar_prefetch=2, grid=(len(pairs),),
            in_specs=[pl.BlockSpec((tq, AUG_W), qi), pl.BlockSpec((tq, AUG_W), kj), pl.BlockSpec((tq, AUG_W), kj)],
            out_specs=[pl.BlockSpec((tq, FOX_W), qi), pl.BlockSpec((tq, LANES), qi)],
            scratch_shapes=[pltpu.VMEM((FOX_HEADS, tq, HB), F32), pltpu.VMEM((tq, AUG_W), F32)]),
        out_shape=[S((T, FOX_W), F32), S((T, LANES), F32)],
        compiler_params=_cp(1))(it, jt, q, k, v)


def _fox_bwd(q, k, v, dob):
    T = q.shape[0]
    tq = _tile(T, 512)
    nq = T // tq
    n_sweeps = 1
    half = AUG_W // n_sweeps
    hpg = FOX_HEADS // n_sweeps

    pairs = [(j, i) for j in range(nq) for i in range(j, nq)]
    jt = jnp.asarray([p[0] for p in pairs], jnp.int32)
    it = jnp.asarray([p[1] for p in pairs], jnp.int32)

    def body(jt_ref, it_ref, q_ref, k_ref, v_ref, do_ref, dq_ref, dk_ref, dv_ref, dq_sc):
        t = pl.program_id(1)
        j, i = jt_ref[t], it_ref[t]

        @pl.when(t == 0)
        def _():
            dq_sc[...] = jnp.zeros_like(dq_sc)

        @pl.when(i == j)
        def _():
            dk_ref[...] = jnp.zeros_like(dk_ref)
            dv_ref[...] = jnp.zeros_like(dv_ref)

        def step(masked):
            rows = pl.ds(pl.multiple_of(i * tq, tq), tq)
            mask = _tri(tq, True) if masked else None
            for h in range(hpg):
                hb = slice(h * HB, (h + 1) * HB)
                qh, kh, vh, doh = q_ref[:, hb], k_ref[:, hb], v_ref[:, hb], do_ref[:, hb]
                s = _nt(qh, kh)
                if masked:
                    s = jnp.where(mask, s, NEG)
                p = jnp.exp(s)
                dsb = (p * _nt(doh, vh)).astype(BF)
                dv_ref[:, hb] += _tn(p.astype(BF), doh)
                dk_ref[:, hb] += _tn(dsb, qh)
                dq_sc[rows, hb] += _nn(dsb, kh)

        @pl.when(i > j)
        def _():
            step(False)

        @pl.when(i == j)
        def _():
            step(True)
            dq_ref[...] = dq_sc[pl.ds(pl.multiple_of(j * tq, tq), tq), :]

    qi = pl.BlockSpec((tq, half), lambda g, t, jt_ref, it_ref: (it_ref[t], g))
    kj = pl.BlockSpec((tq, half), lambda g, t, jt_ref, it_ref: (jt_ref[t], g))
    return pl.pallas_call(
        body, name="fox_bwd",
        grid_spec=pltpu.PrefetchScalarGridSpec(
            num_scalar_prefetch=2, grid=(n_sweeps, len(pairs)), in_specs=[qi, kj, kj, qi], out_specs=[kj, kj, kj],
            scratch_shapes=[pltpu.VMEM((T, half), F32)]),
        out_shape=[S((T, AUG_W), F32), S((T, AUG_W), F32), S((T, AUG_W), F32)],
        compiler_params=_cp(2))(jt, it, q, k, v, dob)


def _mix_out(attn, yg, g_fo, wout, x):
    T, D = x.shape
    tm = _tile(T, 1024)

    def body(a_ref, y_ref, g_ref, w_ref, x_ref, o_ref):
        at = a_ref[...]
        yf = (at * _rstd(at) * g_ref[...]).astype(BF)
        o_ref[...] = x_ref[...] + _nn(yf, w_ref[:FOX_W, :]) + _nn(y_ref[...], w_ref[FOX_W:, :])

    row = lambda i: (i, 0)
    return pl.pallas_call(
        body, name="mix_out", grid=(T // tm,),
        in_specs=[pl.BlockSpec((tm, FOX_W), row), pl.BlockSpec((tm, GMLP_W), row),
                  pl.BlockSpec((1, FOX_W), lambda i: (0, 0)), pl.BlockSpec((D, D), lambda i: (0, 0)),
                  pl.BlockSpec((tm, D), row)],
        out_specs=pl.BlockSpec((tm, D), row),
        out_shape=S((T, D), F32),
        compiler_params=_cp(1))(attn, yg, g_fo, wout, x)


def _mix_out_bwd(dx, attn, yg, g_fo, wout, qf, lse):
    T, D = dx.shape
    tm = _tile(T, 512)
    n = T // tm
    pc_l, pc_d = _piece_matrix(COL_C), _piece_matrix(COL_A)

    def body(dx_ref, a_ref, y_ref, g_ref, w_ref, qf_ref, lse_ref, pl_ref, pd_ref,
             qb_ref, dob_ref, dyg_ref, dw_ref, dg_ref, acc_ref, dsum_ref):
        i = pl.program_id(0)
        dxb = dx_ref[...].astype(BF)
        at = a_ref[...]
        yf = (at * _rstd(at) * g_ref[...]).astype(BF)
        dy = _nt(dxb, w_ref[...])
        p_top = _tn(yf, dxb)
        p_bot = _tn(y_ref[...], dxb)

        @pl.when(i == 0)
        def _():
            acc_ref[:FOX_W, :] = p_top
            acc_ref[FOX_W:, :] = p_bot

        @pl.when(i > 0)
        def _():
            acc_ref[:FOX_W, :] += p_top
            acc_ref[FOX_W:, :] += p_bot

        @pl.when(i == n - 1)
        def _():
            dw_ref[...] = acc_ref[...].astype(BF)

        dat, dgr = _norm_bwd(dy[:, :FOX_W], at, g_ref[...])
        _acc_rows(dg_ref, i == 0, dgr)
        dyg_ref[...] = dy[:, FOX_W:]
        prod = dat * at
        dsum_ref[...] = jnp.zeros_like(dsum_ref)
        for h in range(FOX_HEADS):
            dsum_ref[:, h:h + 1] = jnp.sum(prod[:, h * FOX_HD:(h + 1) * FOX_HD], axis=1, keepdims=True)
        ext_d = _nn(_pieces(-dsum_ref[...]), pd_ref[...]).astype(BF)
        ext_l = _nn(_pieces(-lse_ref[...]), pl_ref[...])
        datb = dat.astype(BF)
        for h in range(FOX_HEADS):
            lo, hi = slice(h * HB, h * HB + FOX_HD), slice(h * HB + FOX_HD, (h + 1) * HB)
            dob_ref[:, lo] = datb[:, h * FOX_HD:(h + 1) * FOX_HD]
            dob_ref[:, hi] = ext_d[:, hi]
            qb_ref[:, lo] = qf_ref[:, lo]
            qb_ref[:, hi] = (qf_ref[:, hi].astype(F32) + ext_l[:, hi]).astype(BF)

    row = lambda i: (i, 0)
    fix = lambda i: (0, 0)
    return pl.pallas_call(
        body, name="mix_out_bwd", grid=(n,),
        in_specs=[pl.BlockSpec((tm, D), row), pl.BlockSpec((tm, FOX_W), row), pl.BlockSpec((tm, GMLP_W), row),
                  pl.BlockSpec((1, FOX_W), fix), pl.BlockSpec((D, D), fix), pl.BlockSpec((tm, AUG_W), row),
                  pl.BlockSpec((tm, LANES), row), pl.BlockSpec((LANES, AUG_W), fix),
                  pl.BlockSpec((LANES, AUG_W), fix)],
        out_specs=[pl.BlockSpec((tm, AUG_W), row), pl.BlockSpec((tm, AUG_W), row), pl.BlockSpec((tm, GMLP_W), row),
                   pl.BlockSpec((D, D), fix), pl.BlockSpec((1, FOX_W), fix)],
        out_shape=[S((T, AUG_W), BF), S((T, AUG_W), BF), S((T, GMLP_W), F32), S((D, D), BF), S((1, FOX_W), F32)],
        scratch_shapes=[pltpu.VMEM((D, D), F32), pltpu.VMEM((tm, LANES), F32)],
        compiler_params=_cp(1))(dx, attn, yg, g_fo, wout, qf, lse, pc_l, pc_d)


def _mix_prep_bwd(z, dq, dk, dv, dyg, rs, bf128, g_q, g_k, g_sgu, w_s, b_st, g_go):
    T = z.shape[0]
    tm = _tile(T, 512)
    n = T // tm

    def body(z_ref, dq_ref, dk_ref, dv_ref, dyg_ref, rs_ref, bf_ref, gq_ref, gk_ref, gs_ref, ws_ref,
             bst_ref, go_ref, dz_ref, dgq_ref, dgk_ref, dgs_ref, dgo_ref, dws_ref, dbst_ref, dbf_ref, carry_ref):
        i = pl.program_id(0)
        first = i == 0
        rs = rs_ref[...]

        @pl.when(first)
        def _():
            carry_ref[...] = jnp.zeros_like(carry_ref)

        lane = lax.broadcasted_iota(jnp.int32, (tm, LANES), 1)
        dc = jnp.zeros((tm, LANES), F32)
        gq_rows, gk_rows = [], []
        for h in range(FOX_HEADS):
            hp = slice(h * HB, h * HB + FOX_HD)
            dqh, gqr = _norm_bwd(dq_ref[:, hp] * 0.125, z_ref[:, Z_Q + h * FOX_HD:Z_Q + (h + 1) * FOX_HD], gq_ref[...],
                                 rs[:, RS_Q + h:RS_Q + h + 1])
            dkh, gkr = _norm_bwd(dk_ref[:, hp], z_ref[:, Z_K + h * FOX_HD:Z_K + (h + 1) * FOX_HD], gk_ref[...],
                                 rs[:, RS_K + h:RS_K + h + 1])
            dz_ref[:, Z_Q + h * FOX_HD:Z_Q + (h + 1) * FOX_HD] = dqh.astype(BF)
            dz_ref[:, Z_K + h * FOX_HD:Z_K + (h + 1) * FOX_HD] = dkh.astype(BF)
            dz_ref[:, Z_V + h * FOX_HD:Z_V + (h + 1) * FOX_HD] = dv_ref[:, hp].astype(BF)
            dch = dq_ref[:, h * HB + COL_A:h * HB + COL_A + 1] - dk_ref[:, h * HB + COL_B:h * HB + COL_B + 1]
            dc = jnp.where(lane == h, dch, dc)
            gq_rows.append(gqr)
            gk_rows.append(gkr)
        _acc_rows(dgq_ref, first, functools.reduce(lambda a, b: a + b, gq_rows))
        _acc_rows(dgk_ref, first, functools.reduce(lambda a, b: a + b, gk_rows))

        dlogf = _hi3(_tri(tm, False), dc) + carry_ref[...]
        carry_ref[...] = dlogf[0:1, :]
        fl = z_ref[:, Z_F:Z_F + LANES] + bf_ref[...]
        lane = lax.broadcasted_iota(jnp.int32, (tm, LANES), 1)
        df = jnp.where(lane < FOX_HEADS, dlogf * jax.nn.sigmoid(-fl), 0.0)
        dz_ref[:, Z_F:Z_F + LANES] = df.astype(BF)
        _acc_rows(dbf_ref, first, df)

        u_pre = z_ref[:, Z_U:Z_U + GMLP_W]
        vg_pre = z_ref[:, Z_G:Z_G + GMLP_W]
        u = _gelu(u_pre)
        vg = _gelu(vg_pre)
        rv = rs[:, RS_V:RS_V + 1]
        vgn = (vg * rv * gs_ref[...]).astype(BF)
        bst = bst_ref[...]
        mixed, wms = _spatial_mix(vgn, ws_ref, bst, tm)
        sgu = u * mixed
        dsgu, gor = _norm_bwd(dyg_ref[...], sgu, go_ref[...], rs[:, RS_O:RS_O + 1])
        _acc_rows(dgo_ref, first, gor)
        du = dsgu * mixed
        dmixed = dsgu * u
        dmb = dmixed.astype(BF)
        tril = _tri(CHUNK, True)
        dvgn_rows = []
        dws = [None] * GMLP_G
        dbs = [None] * GMLP_G
        for c in range(tm // CHUNK):
            cs = slice(c * CHUNK, (c + 1) * CHUNK)
            cols = []
            for g in range(GMLP_G):
                gs = slice(g * GMLP_GD, (g + 1) * GMLP_GD)
                dmc = dmb[cs, gs]
                pw = _nt(dmc, vgn[cs, gs])
                pb = jnp.sum(dmixed[cs, gs], axis=1, keepdims=True)
                dws[g] = pw if dws[g] is None else dws[g] + pw
                dbs[g] = pb if dbs[g] is None else dbs[g] + pb
                cols.append(_tn(wms[g], dmc))
            dvgn_rows.append(jnp.concatenate(cols, axis=1))
        dvgn = jnp.concatenate(dvgn_rows, axis=0)
        dbs_t = jnp.concatenate(dbs, axis=1)
        for g in range(GMLP_G):
            dwg = jnp.where(tril, dws[g], 0.0)

            @pl.when(first)
            def _():
                dws_ref[g] = dwg

            @pl.when(jnp.logical_not(first))
            def _():
                dws_ref[g] += dwg

        @pl.when(first)
        def _():
            dbst_ref[...] = dbs_t

        @pl.when(jnp.logical_not(first))
        def _():
            dbst_ref[...] += dbs_t

        dvg, gsr = _norm_bwd(dvgn, vg, gs_ref[...], rv)
        _acc_rows(dgs_ref, first, gsr)
        dz_ref[:, Z_U:Z_U + GMLP_W] = (du * _gelu_grad(u_pre)).astype(BF)
        dz_ref[:, Z_G:Z_G + GMLP_W] = (dvg * _gelu_grad(vg_pre)).astype(BF)

    rev = lambda i: (n - 1 - i, 0)
    fix = lambda i: (0, 0)
    fix3 = lambda i: (0, 0, 0)
    return pl.pallas_call(
        body, name="mix_prep_bwd", grid=(n,),
        in_specs=[pl.BlockSpec((tm, ZW), rev), pl.BlockSpec((tm, AUG_W), rev), pl.BlockSpec((tm, AUG_W), rev),
                  pl.BlockSpec((tm, AUG_W), rev), pl.BlockSpec((tm, GMLP_W), rev), pl.BlockSpec((tm, LANES), rev),
                  pl.BlockSpec((1, LANES), fix), pl.BlockSpec((1, FOX_HD), fix), pl.BlockSpec((1, FOX_HD), fix),
                  pl.BlockSpec((1, GMLP_W), fix), pl.BlockSpec((GMLP_G, CHUNK, CHUNK), fix3),
                  pl.BlockSpec((CHUNK, GMLP_G), fix), pl.BlockSpec((1, GMLP_W), fix)],
        out_specs=[pl.BlockSpec((tm, ZW), rev), pl.BlockSpec((1, FOX_HD), fix), pl.BlockSpec((1, FOX_HD), fix),
                   pl.BlockSpec((1, GMLP_W), fix), pl.BlockSpec((1, GMLP_W), fix),
                   pl.BlockSpec((GMLP_G, CHUNK, CHUNK), fix3), pl.BlockSpec((CHUNK, GMLP_G), fix),
                   pl.BlockSpec((1, LANES), fix)],
        out_shape=[S((T, ZW), BF), S((1, FOX_HD), F32), S((1, FOX_HD), F32), S((1, GMLP_W), F32), S((1, GMLP_W), F32),
                   S((GMLP_G, CHUNK, CHUNK), F32), S((CHUNK, GMLP_G), F32), S((1, LANES), F32)],
        scratch_shapes=[pltpu.VMEM((1, LANES), F32)],
        compiler_params=_cp(1))(z, dq, dk, dv, dyg, rs, bf128, g_q, g_k, g_sgu, w_s, b_st, g_go)


def _mix_proj_bwd(dz, wz, x, g, dy):
    T, D = x.shape
    tm = _tile(T, 512)

    def body(dz_ref, w_ref, x_ref, g_ref, dy_ref, dx_ref, dxb_ref, dg_ref):
        dh = _nn(dz_ref[...], w_ref[...])
        dx, dgr = _norm_bwd(dh, x_ref[...], g_ref[...])
        dx = dx + dy_ref[...]
        dx_ref[...] = dx
        dxb_ref[...] = dx.astype(BF)
        _acc_rows(dg_ref, pl.program_id(0) == 0, dgr)

    row = lambda i: (i, 0)
    fix = lambda i: (0, 0)
    return pl.pallas_call(
        body, name="mix_proj_bwd", grid=(T // tm,),
        in_specs=[pl.BlockSpec((tm, ZW), row), pl.BlockSpec((ZW, D), fix), pl.BlockSpec((tm, D), row),
                  pl.BlockSpec((1, D), fix), pl.BlockSpec((tm, D), row)],
        out_specs=[pl.BlockSpec((tm, D), row), pl.BlockSpec((tm, D), row), pl.BlockSpec((1, D), fix)],
        out_shape=[S((T, D), F32), S((T, D), BF), S((1, D), F32)],
        compiler_params=_cp(1))(dz, wz, x, g, dy)


def _ca_kv(mem, g_mem, wckv, g_ck):
    M, D = mem.shape

    def body(m_ref, g_ref, w_ref, gk_ref, mn_ref, kr_ref, kn_ref, v_ref):
        mf = m_ref[...]
        mn = (mf * _rstd(mf) * g_ref[...]).astype(BF)
        mn_ref[...] = mn
        for h in range(CA_HEADS):
            kr = _nn(mn, w_ref[h])
            kr_ref[h] = kr
            kn_ref[h] = (kr * _rstd(kr) * gk_ref[...]).astype(BF)
            v_ref[h] = _nn(mn, w_ref[CA_HEADS + h]).astype(BF)

    hd = (CA_HEADS, M, CA_HD)
    return pl.pallas_call(
        body, name="ca_kv", out_shape=[S((M, D), BF), S(hd, F32), S(hd, BF), S(hd, BF)],
        compiler_params=pltpu.CompilerParams(vmem_limit_bytes=VMEM_LIMIT))(mem, g_mem, wckv, g_ck)


def _ca_tile_fwd(xt, gca, wcq, gcq, kn_ref, v_ref):
    hb = (xt * _rstd(xt) * gca).astype(BF)
    qc = _nn(hb, wcq)
    qr, qn, ps = [], [], []
    for h in range(CA_HEADS):
        qh = qc[:, h * CA_HD:(h + 1) * CA_HD]
        qnh = (qh * _rstd(qh) * gcq * 0.0625).astype(BF)
        s = _nt(qnh, kn_ref[h])
        e = jnp.exp(s - jnp.max(s, axis=1, keepdims=True))
        ps.append(e / jnp.sum(e, axis=1, keepdims=True))
        qr.append(qh)
        qn.append(qnh)
    return hb, qr, qn, ps


def _ca_fwd(x, g_ca, wcq, g_cq, kn, vv, wco):
    T, D = x.shape
    M = kn.shape[1]
    tm = _tile(T, 1024)

    def body(x_ref, gca_ref, wcq_ref, gcq_ref, kn_ref, v_ref, wco_ref, o_ref, ob_sc):
        xt = x_ref[...]
        _, _, _, ps = _ca_tile_fwd(xt, gca_ref[...], wcq_ref[...], gcq_ref[...], kn_ref, v_ref)
        for h in range(CA_HEADS):
            ob_sc[:, h * CA_HD:(h + 1) * CA_HD] = _nn(ps[h].astype(BF), v_ref[h]).astype(BF)
        o_ref[...] = xt + _nn(ob_sc[...], wco_ref[...])

    row = lambda i: (i, 0)
    fix = lambda i: (0, 0)
    fix3 = lambda i: (0, 0, 0)
    return pl.pallas_call(
        body, name="ca_fwd", grid=(T // tm,),
        in_specs=[pl.BlockSpec((tm, D), row), pl.BlockSpec((1, D), fix), pl.BlockSpec((D, D), fix),
                  pl.BlockSpec((1, CA_HD), fix), pl.BlockSpec((CA_HEADS, M, CA_HD), fix3),
                  pl.BlockSpec((CA_HEADS, M, CA_HD), fix3), pl.BlockSpec((D, D), fix)],
        out_specs=pl.BlockSpec((tm, D), row), out_shape=S((T, D), F32),
        scratch_shapes=[pltpu.VMEM((tm, D), BF)],
        compiler_params=_cp(1))(x, g_ca, wcq, g_cq, kn, vv, wco)


def _ca_bwd(x, dy, g_ca, wcq, g_cq, kn, vv, wco):
    T, D = x.shape
    M = kn.shape[1]
    tm = _tile(T, 512)
    n = T // tm

    def body(x_ref, dy_ref, gca_ref, wcq_ref, gcq_ref, kn_ref, v_ref, wco_ref,
             dx_ref, dwq_ref, dwo_ref, dkn_ref, dv_ref, dgcq_ref, dgca_ref, aq_sc, ao_sc, ob_sc, dq_sc):
        i = pl.program_id(0)
        first = i == 0
        xt = x_ref[...]
        dyt = dy_ref[...]
        dyb = dyt.astype(BF)
        hb, qr, qn, ps = _ca_tile_fwd(xt, gca_ref[...], wcq_ref[...], gcq_ref[...], kn_ref, v_ref)
        do = _nt(dyb, wco_ref[...])
        gcq_rows = None
        for h in range(CA_HEADS):
            hs = slice(h * CA_HD, (h + 1) * CA_HD)
            p = ps[h]
            pb = p.astype(BF)
            ob_sc[:, hs] = _nn(pb, v_ref[h]).astype(BF)
            doh = do[:, hs].astype(BF)
            dp = _nt(doh, v_ref[h])
            ds = (p * (dp - jnp.sum(dp * p, axis=1, keepdims=True))).astype(BF)
            dvh = _tn(pb, doh)
            dkh = _tn(ds, qn[h])

            @pl.when(first)
            def _():
                dv_ref[h] = dvh
                dkn_ref[h] = dkh

            @pl.when(jnp.logical_not(first))
            def _():
                dv_ref[h] += dvh
                dkn_ref[h] += dkh

            dqn = _nn(ds, kn_ref[h]) * 0.0625
            dqh, gr = _norm_bwd(dqn, qr[h], gcq_ref[...])
            gcq_rows = gr if gcq_rows is None else gcq_rows + gr
            dq_sc[:, hs] = dqh.astype(BF)
        _acc_rows(dgcq_ref, first, gcq_rows)
        dqb = dq_sc[...]
        p_o = _tn(ob_sc[...], dyb)
        p_q = _tn(hb, dqb)

        @pl.when(first)
        def _():
            ao_sc[...] = p_o
            aq_sc[...] = p_q

        @pl.when(jnp.logical_not(first))
        def _():
            ao_sc[...] += p_o
            aq_sc[...] += p_q

        @pl.when(i == n - 1)
        def _():
            dwo_ref[...] = ao_sc[...].astype(BF)
            dwq_ref[...] = aq_sc[...].astype(BF)

        dh = _nt(dqb, wcq_ref[...])
        dx, gar = _norm_bwd(dh, xt, gca_ref[...])
        dx_ref[...] = dx + dyt
        _acc_rows(dgca_ref, first, gar)

    row = lambda i: (i, 0)
    fix = lambda i: (0, 0)
    fix3 = lambda i: (0, 0, 0)
    hd = (CA_HEADS, M, CA_HD)
    return pl.pallas_call(
        body, name="ca_bwd", grid=(n,),
        in_specs=[pl.BlockSpec((tm, D), row), pl.BlockSpec((tm, D), row), pl.BlockSpec((1, D), fix),
                  pl.BlockSpec((D, D), fix), pl.BlockSpec((1, CA_HD), fix), pl.BlockSpec(hd, fix3),
                  pl.BlockSpec(hd, fix3), pl.BlockSpec((D, D), fix)],
        out_specs=[pl.BlockSpec((tm, D), row), pl.BlockSpec((D, D), fix), pl.BlockSpec((D, D), fix),
                   pl.BlockSpec(hd, fix3), pl.BlockSpec(hd, fix3), pl.BlockSpec((1, CA_HD), fix),
                   pl.BlockSpec((1, D), fix)],
        out_shape=[S((T, D), F32), S((D, D), BF), S((D, D), BF), S(hd, F32), S(hd, F32), S((1, CA_HD), F32),
                   S((1, D), F32)],
        scratch_shapes=[pltpu.VMEM((D, D), F32), pltpu.VMEM((D, D), F32), pltpu.VMEM((tm, D), BF),
                        pltpu.VMEM((tm, D), BF)],
        compiler_params=_cp(1))(x, dy, g_ca, wcq, g_cq, kn, vv, wco)


def _ca_kv_bwd(mem, g_mem, mn, kraw, dkn, dvv, wckv, g_ck):
    M, D = mem.shape

    def body(m_ref, g_ref, mn_ref, kr_ref, dkn_ref, dv_ref, w_ref, gk_ref, dw_ref, dgk_ref, dgm_ref):
        mn = mn_ref[...]
        dmn = jnp.zeros((M, D), F32)
        gk_rows = None
        for h in range(CA_HEADS):
            dkr, gr = _norm_bwd(dkn_ref[h], kr_ref[h], gk_ref[...])
            gk_rows = gr if gk_rows is None else gk_rows + gr
            dkb = dkr.astype(BF)
            dvb = dv_ref[h].astype(BF)
            dw_ref[h] = _tn(mn, dkb).astype(BF)
            dw_ref[CA_HEADS + h] = _tn(mn, dvb).astype(BF)
            dmn = dmn + _nt(dkb, w_ref[h]) + _nt(dvb, w_ref[CA_HEADS + h])
        dgk_ref[...] = jnp.sum(gk_rows, axis=0, keepdims=True)
        mf = m_ref[...]
        dgm_ref[...] = jnp.sum(dmn * (mf * _rstd(mf)), axis=0, keepdims=True)

    return pl.pallas_call(
        body, name="ca_kv_bwd",
        out_shape=[S((2 * CA_HEADS, D, CA_HD), BF), S((1, CA_HD), F32), S((1, D), F32)],
        compiler_params=pltpu.CompilerParams(vmem_limit_bytes=VMEM_LIMIT))(mem, g_mem, mn, kraw, dkn, dvv, wckv, g_ck)


def _after(g, token):
    return g if token is None else g + token[0:1, 0:1]


def _local_step(x, mem, target, small, weights, emit, h1=None):
    T, D = x.shape
    p = small
    bf128 = jnp.pad(p["b_f"], ((0, 0), (0, LANES - FOX_HEADS)))
    b_st = p["b_s"].T

    wup1 = weights("ffn1_up", x if h1 is None else h1)["wup1"]
    if h1 is None:
        a1, h1 = _ffn_up("ffn1_up", x, p["g_ffn1"], wup1)
    else:
        a1 = _ffn_up_from_h("ffn1_up", h1, wup1)
    wdn1 = weights("ffn1_dn", a1)["wdn1"]
    x1 = _ffn_down("ffn1_down", a1, wdn1, x)
    wm = weights("mix", x1)
    z, h2, qf, ka, va, yg, rs = _mix_prep(x1, p["g_mix"], wm["wz"], bf128, p["g_q"], p["g_k"], p["g_sgu"], p["w_s"],
                                          b_st, p["g_gmlp_o"])
    attn, lse = _fox_fwd(qf, ka, va)
    x2 = _mix_out(attn, yg, p["g_fox_o"], wm["wout"], x1)
    wc = weights("ca", x2)
    mn, kraw, ckn, cvv = _ca_kv(mem, p["g_mem"], wc["wckv"], p["g_ck"])
    x3 = _ca_fwd(x2, p["g_ca"], wc["wcq"], p["g_cq"], ckn, cvv, wc["wco"])
    w2 = weights("ffn2", x3)
    a2, h4 = _ffn_up("ffn2_up", x3, p["g_ffn2"], w2["wup2"])
    dy4, dy4b, sq = _ffn_down_loss("ffn2_down", a2, w2["wdn2"], x3, target)

    gs = {}
    dgu2 = _ffn_bwd_act("ffn2_bwd_act", dy4b, h4, w2["wup2"], w2["wdn2"])
    tok = emit("ffn2", {"wup2": _ffn_dwup("ffn2", h4, dgu2), "wdn2": _ffn_dwdn("ffn2", a2, dy4b)})
    dx3, gs["g_ffn2"] = _ffn_dx("ffn2_dx", dgu2, w2["wup2"], x3, _after(p["g_ffn2"], tok), dy4)

    dx2, dwcq, dwco, dckn, dcvv, gs["g_cq"], gs["g_ca"] = _ca_bwd(
        x2, dx3, p["g_ca"], wc["wcq"], p["g_cq"], ckn, cvv, wc["wco"])
    dwckv, gs["g_ck"], gs["g_mem"] = _ca_kv_bwd(mem, p["g_mem"], mn, kraw, dckn, dcvv, wc["wckv"], p["g_ck"])

    qb, dob, dyg, dwout, gs["g_fox_o"] = _mix_out_bwd(dx2, attn, yg, p["g_fox_o"], wm["wout"], qf, lse)
    dq, dk, dv = _fox_bwd(qb, ka, va, dob)
    dz, gs["g_q"], gs["g_k"], gs["g_sgu"], gs["g_gmlp_o"], gs["w_s"], dbst, dbf = _mix_prep_bwd(
        z, dq, dk, dv, dyg, rs, bf128, p["g_q"], p["g_k"], p["g_sgu"], p["w_s"], b_st, p["g_gmlp_o"])
    gs["b_s"] = dbst.T
    gs["b_f"] = dbf[:, :FOX_HEADS]
    tok_ws = emit("w_s", {"w_s": gs["w_s"]})
    zb = ZW // 3
    dwz = _tn_matmul("mix_dwz", dz, pl.BlockSpec((T, zb), lambda j: (0, j)), h2,
                     S((ZW, D), BF), pl.BlockSpec((zb, D), lambda j: (j, 0)), 3)
    tok = emit("mid", {"wcq": dwcq, "wco": dwco, "wckv": dwckv, "wout": dwout, "wz": dwz})
    dx1, dx1b, gs["g_mix"] = _mix_proj_bwd(dz, wm["wz"], x1, _after(_after(p["g_mix"], tok), tok_ws), dx2)

    dgu1 = _ffn_bwd_act("ffn1_bwd_act", dx1b, h1, wup1, wdn1)
    tok = emit("ffn1_dn", {"wdn1": _ffn_dwdn("ffn1", a1, dx1b)})
    tok = emit("ffn1_up", {"wup1": _ffn_dwup("ffn1", h1, dgu1, after=tok)})
    dx0, gs["g_ffn1"] = _ffn_dx("ffn1_dx", dgu1, wup1, x, _after(p["g_ffn1"], tok), dx1)
    return sq, dx0, gs


MESH = pl.DeviceIdType.MESH
HBM_SPEC = pl.BlockSpec(memory_space=pltpu.HBM)
N_PEER = N_DEV - 1


def _place():
    return lax.axis_index("x"), lax.axis_index("y"), lax.axis_index("c")


def _slot(px, py, pc):
    return 4 * px + 2 * py + pc


SEM_SPEC = pl.BlockSpec(memory_space=pltpu.SEMAPHORE)
ANY_SPEC = pl.BlockSpec(memory_space=pl.ANY)
DATAFLOW = pltpu.SideEffectType.DATAFLOW_SIDE_EFFECTING


def _hbm(a):
    return pltpu.with_memory_space_constraint(a, pltpu.HBM)


def _peer(x, y, c, r):
    return (1 - x if r & 4 else x, 1 - y if r & 2 else y, 1 - c if r & 1 else c)


def _place_own(srcs, whole):
    my = _slot(*_place())
    lands = []
    for s in srcs:
        blk = s[None] if whole else lax.dynamic_slice_in_dim(s, my, 1, 0)
        shape = (N_DEV,) + s.shape if whole else s.shape
        lands.append(lax.dynamic_update_slice_in_dim(lax.empty(shape, s.dtype), blk, my, 0))
    return lands


ALL_PEERS = tuple(range(1, N_DEV))
NEAR_PEERS = (1, 2, 4, 6)
SAME_CORE = (2, 4, 6)


def _copy_start(name, srcs, lands, whole, peers=None):
    n = len(srcs)
    peers = peers or [ALL_PEERS] * n
    wh = list(whole) if isinstance(whole, (list, tuple)) else [whole] * n

    def body(*refs):
        src, land = refs[:n], refs[n:2 * n]
        send, recv = refs[2 * n:3 * n], refs[3 * n:4 * n]
        token = refs[6 * n]
        x, y, c = _place()
        my = _slot(x, y, c)
        for a in range(n):
            for r in peers[a]:
                p = _peer(x, y, c, r)
                pltpu.make_async_remote_copy(
                    src_ref=src[a] if wh[a] else src[a].at[_slot(*p)], dst_ref=land[a].at[my],
                    send_sem=send[a].at[r - 1], recv_sem=recv[a].at[r - 1], device_id=p, device_id_type=MESH).start()
        token[...] = jnp.zeros_like(token)

    out = pl.pallas_call(
        body, name=name,
        out_shape=([pltpu.SemaphoreType.DMA((N_PEER,))] * (2 * n)
                   + [pltpu.HBM(s.shape, s.dtype) for s in srcs] + [pltpu.HBM(s.shape, s.dtype) for s in lands]
                   + [S((8, LANES), F32)]),
        in_specs=[HBM_SPEC] * (2 * n),
        out_specs=[SEM_SPEC] * (2 * n) + [HBM_SPEC] * (2 * n) + [pl.BlockSpec(memory_space=pltpu.VMEM)],
        input_output_aliases={i: 2 * n + i for i in range(2 * n)},
        compiler_params=pltpu.CompilerParams(has_side_effects=DATAFLOW),
    )(*[_hbm(s) for s in srcs], *[_hbm(s) for s in lands])
    return out[:n], out[n:2 * n], out[2 * n:3 * n], out[3 * n:4 * n], out[4 * n]


def _copy_wait(name, srcs, lands, send, recv, after, whole, peers=None, with_srcs=False):
    n = len(srcs)
    peers = peers or [ALL_PEERS] * n
    wh = list(whole) if isinstance(whole, (list, tuple)) else [whole] * n

    def body(*refs):
        src, land = refs[:n], refs[n:2 * n]
        snd, rcv = refs[2 * n:3 * n], refs[3 * n:4 * n]
        x, y, c = _place()
        for a in range(n):
            for r in peers[a]:
                p = _peer(x, y, c, r)
                ps = _slot(*p)
                cp = pltpu.make_async_remote_copy(
                    src_ref=src[a] if wh[a] else src[a].at[ps], dst_ref=land[a].at[ps],
                    send_sem=snd[a].at[r - 1], recv_sem=rcv[a].at[r - 1], device_id=p, device_id_type=MESH)
                cp.wait_send()
                cp.wait_recv()

    out = pl.pallas_call(
        body, name=name,
        out_shape=[pltpu.HBM(s.shape, s.dtype) for s in srcs] + [pltpu.HBM(s.shape, s.dtype) for s in lands],
        in_specs=[HBM_SPEC] * (2 * n) + [SEM_SPEC] * (2 * n) + [ANY_SPEC],
        out_specs=[HBM_SPEC] * (2 * n),
        input_output_aliases={i: i for i in range(2 * n)},
        compiler_params=pltpu.CompilerParams(has_side_effects=DATAFLOW),
    )(*srcs, *lands, *send, *recv, after)
    return (out[:n], out[n:]) if with_srcs else out[n:]


def _forward_start(name, lands):
    n = len(lands)

    def body(*refs):
        land = refs[:n]
        send, recv = refs[n:2 * n], refs[2 * n:3 * n]
        token = refs[4 * n]
        x, y, c = _place()
        for a in range(n):
            for r in SAME_CORE:
                blk = land[a].at[_slot(*_peer(x, y, c, r))]
                pltpu.make_async_remote_copy(
                    src_ref=blk, dst_ref=blk, send_sem=send[a].at[r - 1], recv_sem=recv[a].at[r - 1],
                    device_id=(x, y, 1 - c), device_id_type=MESH).start()
        token[...] = jnp.zeros_like(token)

    out = pl.pallas_call(
        body, name=name,
        out_shape=([pltpu.SemaphoreType.DMA((N_PEER,))] * (2 * n) + [pltpu.HBM(s.shape, s.dtype) for s in lands]
                   + [S((8, LANES), F32)]),
        in_specs=[HBM_SPEC] * n,
        out_specs=[SEM_SPEC] * (2 * n) + [HBM_SPEC] * n + [pl.BlockSpec(memory_space=pltpu.VMEM)],
        input_output_aliases={i: 2 * n + i for i in range(n)},
        compiler_params=pltpu.CompilerParams(has_side_effects=DATAFLOW),
    )(*[_hbm(s) for s in lands])
    return out[:n], out[n:2 * n], out[2 * n:3 * n], out[3 * n]


def _forward_wait(name, lands, send, recv, after):
    n = len(lands)

    def body(*refs):
        land = refs[:n]
        snd, rcv = refs[n:2 * n], refs[2 * n:3 * n]
        x, y, c = _place()
        for a in range(n):
            for r in SAME_CORE:
                cp = pltpu.make_async_remote_copy(
                    src_ref=land[a].at[_slot(*_peer(x, y, c, r))], dst_ref=land[a].at[_slot(*_peer(x, y, c, r | 1))],
                    send_sem=snd[a].at[r - 1], recv_sem=rcv[a].at[r - 1], device_id=(x, y, 1 - c),
                    device_id_type=MESH)
                cp.wait_send()
                cp.wait_recv()

    return pl.pallas_call(
        body, name=name,
        out_shape=[pltpu.HBM(s.shape, s.dtype) for s in lands],
        in_specs=[HBM_SPEC] * n + [SEM_SPEC] * (2 * n) + [ANY_SPEC],
        out_specs=[HBM_SPEC] * n,
        input_output_aliases={i: i for i in range(n)},
        compiler_params=pltpu.CompilerParams(has_side_effects=DATAFLOW),
    )(*lands, *send, *recv, after)


def _adamw(w, g, m, v):
    m2 = ADAM_B1 * m + (1.0 - ADAM_B1) * g
    v2 = ADAM_B2 * v + (1.0 - ADAM_B2) * (g * g)
    m_hat = m2 / (1.0 - ADAM_B1 ** ADAM_STEP)
    v_hat = v2 / (1.0 - ADAM_B2 ** ADAM_STEP)
    delta = -ADAM_LR * (m_hat / (jnp.sqrt(v_hat) + ADAM_EPS) + ADAM_WD * w)
    return delta, m2, v2


def _adamw_big(name, slots, w, m, v, own=None):
    R, C = w.shape
    tr = next((t for t in (128, 176, 64) if R % t == 0 and R // t >= 2), R)

    def finish(g, w_ref, m_ref, v_ref, g_ref, d_ref, m2_ref, v2_ref):
        d, m2, v2 = _adamw(w_ref[...], g, m_ref[...], v_ref[...])
        g_ref[...] = g
        d_ref[...] = d
        m2_ref[...] = m2
        v2_ref[...] = v2

    if own is None:
        def body(s_ref, *refs):
            g = s_ref[0].astype(F32)
            for k in range(1, N_DEV):
                g = g + s_ref[k].astype(F32)
            finish(g, *refs)

        row = pl.BlockSpec((tr, C), lambda i: (i, 0))
        return pl.pallas_call(
            body, name=name, grid=(R // tr,),
            in_specs=[pl.BlockSpec((N_DEV, tr, C), lambda i: (0, i, 0)), row, row, row],
            out_specs=[row] * 4, out_shape=[S((R, C), F32)] * 4,
            compiler_params=_cp(1))(slots, w, m, v)

    def body(my_ref, s_ref, own_ref, *refs):
        mine = own_ref[...]
        g = None
        for k in range(N_DEV):
            part = jnp.where(my_ref[0] == k, mine, s_ref[k]).astype(F32)
            g = part if g is None else g + part
        finish(g, *refs)

    row = pl.BlockSpec((tr, C), lambda i, my_ref: (i, 0))
    my = jnp.reshape(_slot(*_place()), (1,)).astype(jnp.int32)
    return pl.pallas_call(
        body, name=name,
        grid_spec=pltpu.PrefetchScalarGridSpec(
            num_scalar_prefetch=1, grid=(R // tr,),
            in_specs=[pl.BlockSpec((N_DEV, tr, C), lambda i, my_ref: (0, i, 0)),
                      pl.BlockSpec((None, tr, C), lambda i, my_ref: (my_ref[0], i, 0)), row, row, row],
            out_specs=[row] * 4),
        out_shape=[S((R, C), F32)] * 4, compiler_params=_cp(1))(my, slots, own, w, m, v)


TINY_ROWS = (("b_s", 8), ("g_ffn1", 8), ("g_mix", 8), ("g_ca", 8), ("g_mem", 8), ("g_ffn2", 8), ("g_sgu", 4),
             ("g_fox_o", 4), ("g_gmlp_o", 4), ("g_cq", 2), ("g_ck", 2), ("g_q", 1), ("g_k", 1), ("b_f", 1),
             ("loss", 1))
TINY_P = 72


def _tiny_pieces(width):
    return [(j, slice(j * LANES, min((j + 1) * LANES, width))) for j in range(-(-width // LANES))]


def _pack_tiny(grads, sq):
    names = [n for n, _ in TINY_ROWS if n != "loss"]

    def body(*refs):
        ins, sq_ref, o_ref = refs[:len(names)], refs[len(names)], refs[len(names) + 1]
        o_ref[...] = jnp.zeros_like(o_ref)
        at = 0
        for ref, (name, r) in zip(ins, TINY_ROWS):
            if name == "b_s":
                o_ref[at:at + r, :] = ref[...]
            else:
                for j, cols in _tiny_pieces(ref.shape[1]):
                    o_ref[at + j:at + j + 1, 0:cols.stop - cols.start] = ref[:, cols]
            at += r
        o_ref[at:at + 1, :] = sq_ref[0:1, :]

    return pl.pallas_call(body, name="tiny_pack", out_shape=S((TINY_P, LANES), F32))(
        *[grads[n] for n in names], sq)


def _adamw_tiny(slots, w, m, v):
    names = [n for n, _ in TINY_ROWS if n != "loss"]
    k = len(names)

    def body(s_ref, *refs):
        ins, outs, loss_ref = refs[:3 * k], refs[3 * k:7 * k], refs[7 * k]
        g_all = s_ref[0]
        for d in range(1, N_DEV):
            g_all = g_all + s_ref[d]
        at = 0
        for i, (name, r) in enumerate(TINY_ROWS[:k]):
            w_ref, m_ref, v_ref = ins[i], ins[k + i], ins[2 * k + i]
            o = outs[4 * i:4 * i + 4]
            if name == "b_s":
                pieces = [(slice(at, at + r), slice(0, LANES), (slice(None), slice(None)))]
            else:
                pieces = [(slice(at + j, at + j + 1), slice(0, c.stop - c.start), (slice(None), c))
                          for j, c in _tiny_pieces(w_ref.shape[1])]
            for rows, lanes, dst in pieces:
                g = g_all[rows, lanes]
                res = (g,) + _adamw(w_ref[dst], g, m_ref[dst], v_ref[dst])
                for ref, val in zip(o, res):
                    ref[dst] = val
            at += r
        loss_ref[...] = g_all[at:at + 1, :]

    shapes = [S(w[n].shape, F32) for n in names]
    out = pl.pallas_call(
        body, name="adamw_tiny", out_shape=[s for s in shapes for _ in range(4)] + [S((1, LANES), F32)],
    )(slots, *[w[n] for n in names], *[m[n] for n in names], *[v[n] for n in names])
    stores = ({}, {}, {}, {})
    for i, n in enumerate(names):
        for store, t in zip(stores, out[4 * i:4 * i + 4]):
            store[n] = t
    return stores, out[4 * k]


WEIGHTS =('g_ffn1', 'w_ffn1_in', 'w_ffn1_out', 'g_mix', 'w_in', 'b_f', 'g_q', 'g_k', 'g_sgu', 'w_s', 'b_s',
           'g_fox_o', 'g_gmlp_o', 'w_out', 'g_ca', 'g_mem', 'w_cq', 'w_ckv', 'g_cq', 'g_ck', 'w_co', 'g_ffn2',
           'w_ffn2_in', 'w_ffn2_out')
BIG = ('w_ffn1_in', 'w_ffn1_out', 'w_in', 'w_out', 'w_cq', 'w_ckv', 'w_co', 'w_ffn2_in', 'w_ffn2_out')
TRANSPOSED = ('w_ffn1_in', 'w_in', 'w_ffn2_in')
TWO_LEVEL = ('w_ffn1_in', 'w_in')
GATHER_GROUPS = {"ffn1_up": ("w_ffn1_in",), "ffn1_dn": ("w_ffn1_out",), "mix": ("w_in", "w_out"),
                 "ca": ("w_cq", "w_ckv", "w_co"), "ffn2": ("w_ffn2_in", "w_ffn2_out")}
QKV_W = 3 * FOX_W
UV_OFF = QKV_W + FOX_HEADS


def kernel(x, mem, g_ffn1, w_ffn1_in, w_ffn1_out, g_mix, w_in, b_f, g_q, g_k, g_sgu, w_s, b_s, g_fox_o, g_gmlp_o, w_out, g_ca, g_mem, w_cq, w_ckv, g_cq, g_ck, w_co, g_ffn2, w_ffn2_in, w_ffn2_out, loss_target, m_g_ffn1, m_w_ffn1_in, m_w_ffn1_out, m_g_mix, m_w_in, m_b_f, m_g_q, m_g_k, m_g_sgu, m_w_s, m_b_s, m_g_fox_o, m_g_gmlp_o, m_w_out, m_g_ca, m_g_mem, m_w_cq, m_w_ckv, m_g_cq, m_g_ck, m_w_co, m_g_ffn2, m_w_ffn2_in, m_w_ffn2_out, v_g_ffn1, v_w_ffn1_in, v_w_ffn1_out, v_g_mix, v_w_in, v_b_f, v_g_q, v_g_k, v_g_sgu, v_w_s, v_b_s, v_g_fox_o, v_g_gmlp_o, v_w_out, v_g_ca, v_g_mem, v_w_cq, v_w_ckv, v_g_cq, v_g_ck, v_w_co, v_g_ffn2, v_w_ffn2_in, v_w_ffn2_out):
    args = dict(locals())
    w = {n: args[n] for n in WEIGHTS}
    mo = {n: args["m_" + n] for n in WEIGHTS}
    vo = {n: args["v_" + n] for n in WEIGHTS}
    D = D_MODEL

    def local(n, a):
        return a[0].T if n in TRANSPOSED else a[0]

    g_peers = [NEAR_PEERS if n in TWO_LEVEL else ALL_PEERS for n in BIG]
    handles = {}

    def start_gather(name, names, arrays):
        snd, rcv, src, land, token = _copy_start(name, arrays, _place_own(arrays, True), True,
                                                 peers=[g_peers[BIG.index(n)] for n in names])
        handles.update({n: (src[i], land[i], snd[i], rcv[i]) for i, n in enumerate(names)})
        return token

    first = local(BIG[0], w[BIG[0]]).astype(BF)
    fb = first.shape[0]
    token_first = start_gather("gather_start_first", BIG[:1], [first])
    token_rest = start_gather("gather_start_rest", BIG[1:],
                              [(local(n, w[n]) + token_first[0:1, 0:1]).astype(BF) for n in BIG[1:]])

    tiny_names = [n for n, _ in TINY_ROWS if n != "loss"]

    def weights(group, after):
        names = GATHER_GROUPS[group]
        hs = [handles[n] for n in names]
        got = list(_copy_wait("gather_wait_" + group, [h[0] for h in hs], [h[1] for h in hs], [h[2] for h in hs],
                              [h[3] for h in hs], after, True,
                              peers=[g_peers[BIG.index(n)] for n in names]))
        passed = [i for i, n in enumerate(names) if n in TWO_LEVEL]
        if passed:
            f_snd, f_rcv, f_land, f_token = _forward_start("gather_pass_start_" + group, [got[i] for i in passed])
            for i, t in zip(passed, _forward_wait("gather_pass_wait_" + group, f_land, f_snd, f_rcv, f_token)):
                got[i] = t
        got = dict(zip(names, got))
        if group == "ffn1_up":
            return {"wup1": got["w_ffn1_in"].reshape(2, N_FFN_BLK, fb, D)}
        if group == "ffn1_dn":
            return {"wdn1": got["w_ffn1_out"].reshape(N_FFN_BLK, fb, D)}
        if group == "mix":
            full = got["w_in"].reshape(-1, D)
            wz = jnp.concatenate([full[:QKV_W], full[UV_OFF:], full[QKV_W:UV_OFF],
                                  jnp.zeros((LANES - FOX_HEADS, D), BF)], axis=0)
            return {"wz": wz, "wout": got["w_out"].reshape(D, D)}
        if group == "ca":
            return {"wcq": got["w_cq"].reshape(D, D), "wco": got["w_co"].reshape(D, D), "wckv": got["w_ckv"]}
        return {"wup2": got["w_ffn2_in"].reshape(2, N_FFN_BLK, fb, D),
                "wdn2": got["w_ffn2_out"].reshape(N_FFN_BLK, fb, D)}

    flying = {}

    def emit(group, g):
        if group == "w_s":
            flying[group] = g["w_s"].reshape(-1, LANES)
            return None
        if group == "ffn2":
            parts = {"w_ffn2_in": g["wup2"], "w_ffn2_out": g["wdn2"].reshape(N_DEV, -1, D)}
        elif group == "ffn1_dn":
            parts = {"w_ffn1_out": g["wdn1"].reshape(N_DEV, -1, D)}
        elif group == "ffn1_up":
            parts = {"w_ffn1_in": g["wup1"]}
        else:
            gz = g["wz"]
            g_in = jnp.concatenate([gz[:QKV_W], gz[Z_F:Z_F + FOX_HEADS], gz[QKV_W:Z_F]], axis=0)
            parts = {"w_in": g_in.reshape(N_DEV, -1, D).astype(BF),
                     "w_out": g["wout"].reshape(N_DEV, -1, D), "w_cq": g["wcq"].reshape(N_DEV, -1, D),
                     "w_co": g["wco"].reshape(N_DEV, -1, D), "w_ckv": g["wckv"]}
        names = list(parts)
        srcs = [parts[n] for n in names]
        lands = [lax.empty(s.shape, s.dtype) for s in srcs]
        whole = [False] * len(srcs)
        if group == "mid":
            ws_part = flying.pop("w_s")
            names, srcs, whole = names + ["w_s"], srcs + [ws_part], whole + [True]
            lands += _place_own([ws_part], True)
        *copies, token = _copy_start("exchange_start_" + group, srcs, lands, whole)
        flying[group] = (names, copies, whole)
        return token

    small = {n: (w[n][0] if n == "b_s" else w[n]) for n in tiny_names}
    small["w_s"] = w["w_s"][0]

    h1 = _rms_cast("ffn1_norm", x[0], w["g_ffn1"], token_rest)
    sq, dx0, gs = _local_step(x[0], mem[0], loss_target[0], small, weights, emit, h1=h1)

    sm_parts = [_pack_tiny(gs, sq)]
    sm_snd, sm_rcv, sm_src, sm_land, sm_token = _copy_start("tiny_start", sm_parts, _place_own(sm_parts, True), True)

    grad, delta, new_m, new_v = {}, {}, {}, {}

    def update(group, after):
        names, (snd, rcv, srcs, lands), whole = flying[group]
        owns, slots = _copy_wait("exchange_wait_" + group, srcs, lands, snd, rcv, after, whole, with_srcs=True)
        for n, sl, own in zip(names, slots, owns):
            if n == "w_s":
                g, d, m2, v2 = _adamw_big("adamw_w_s", sl, *[a[n].reshape(-1, LANES) for a in (w, mo, vo)])
            else:
                g, d, m2, v2 = _adamw_big("adamw_" + n, sl, local(n, w[n]), local(n, mo[n]), local(n, vo[n]),
                                          own=own)
            grad[n], delta[n], new_m[n], new_v[n] = (
                (t.T if n in TRANSPOSED else t).reshape(w[n].shape) for t in (g, d, m2, v2))
        return d

    last = update("ffn2", sm_token)
    last = update("mid", last)
    last = update("ffn1_dn", last)
    last = update("ffn1_up", last)
    tiny_all, = _copy_wait("tiny_wait", sm_src, sm_land, sm_snd, sm_rcv, last, True)
    stores, loss_row = _adamw_tiny(tiny_all, *[{n: (a[n][0] if n == "b_s" else a[n]) for n in tiny_names}
                                               for a in (w, mo, vo)])
    for store, t in zip((grad, delta, new_m, new_v), stores):
        store.update({n: v.reshape(w[n].shape) for n, v in t.items()})
    loss = loss_row[0, 0] * (0.5 / D)

    return (loss, dx0[None], *[grad[n] for n in WEIGHTS], *[delta[n] for n in WEIGHTS],
            *[new_m[n] for n in WEIGHTS], *[new_v[n] for n in WEIGHTS])
```

```python
import functools

import jax
import jax.numpy as jnp
from jax import lax
from jax.experimental import pallas as pl
from jax.experimental.pallas import tpu as pltpu

F32 = jnp.float32
BF = jnp.bfloat16
S = jax.ShapeDtypeStruct

N_DEV = 8
D_MODEL = 1024
FOX_HEADS, FOX_HD = 8, 64
FOX_W = 512
GMLP_G, GMLP_GD = 8, 64
GMLP_W = 512
CHUNK = 128
CA_HEADS, CA_HD = 4, 256
N_FFN_BLK = 4
ZW = 2688
Z_Q, Z_K, Z_V, Z_U, Z_G, Z_F = 0, 512, 1024, 1536, 2048, 2560
EPS = 1e-6
NEG = -1e30
LANES = 128

ADAM_LR, ADAM_B1, ADAM_B2, ADAM_EPS, ADAM_WD, ADAM_STEP = 0.001, 0.9, 0.999, 1e-08, 0.01, 10

VMEM_LIMIT = 52 * 2 ** 20


def _cp(n_axes):
    return pltpu.CompilerParams(dimension_semantics=("arbitrary",) * n_axes, vmem_limit_bytes=VMEM_LIMIT)


def _nn(a, b):
    return jnp.dot(a, b, preferred_element_type=F32)


def _nt(a, b):
    return lax.dot_general(a, b, (((1,), (1,)), ((), ())), preferred_element_type=F32)


def _tn(a, b):
    return lax.dot_general(a, b, (((0,), (0,)), ((), ())), preferred_element_type=F32)


def _hi(mask, x):
    return jnp.dot(mask.astype(F32), x, precision=lax.Precision.HIGHEST, preferred_element_type=F32)


def _hi3(mask, x):
    mb = mask.astype(BF)
    hi = x.astype(BF)
    r1 = x - hi.astype(F32)
    mid = r1.astype(BF)
    lo = (r1 - mid.astype(F32)).astype(BF)
    return _nn(mb, hi) + _nn(mb, mid) + _nn(mb, lo)


def _rstd(x):
    return lax.rsqrt(jnp.mean(x * x, axis=-1, keepdims=True) + EPS)


def _norm_bwd(dy, x, g, r=None):
    r = _rstd(x) if r is None else r
    xh = x * r
    dxh = dy * g
    dx = r * (dxh - xh * jnp.mean(dxh * xh, axis=-1, keepdims=True))
    return dx, dy * xh


def _acc_rows(ref, first, val):
    srow = jnp.sum(val, axis=0, keepdims=True)

    @pl.when(first)
    def _():
        ref[...] = srow

    @pl.when(jnp.logical_not(first))
    def _():
        ref[...] += srow


def _gelu(x):
    c = 0.7978845608028654
    return 0.5 * x * (1.0 + jnp.tanh(c * (x + 0.044715 * x * x * x)))


def _gelu_grad(x):
    c = 0.7978845608028654
    t = jnp.tanh(c * (x + 0.044715 * x * x * x))
    return 0.5 * (1.0 + t) + 0.5 * x * (1.0 - t * t) * c * (1.0 + 3 * 0.044715 * x * x)


def _tile(n, pref):
    return pref if n % pref == 0 else n


def _rms_cast(name, x, g, after):
    T, D = x.shape
    tm = _tile(T, 1024)

    def body(x_ref, g_ref, t_ref, h_ref):
        xf = x_ref[...]
        h_ref[...] = (xf * _rstd(xf) * g_ref[...]).astype(BF)

    return pl.pallas_call(
        body, name=name, grid=(T // tm,),
        in_specs=[pl.BlockSpec((tm, D), lambda i: (i, 0)), pl.BlockSpec((1, D), lambda i: (0, 0)),
                  pl.BlockSpec((8, LANES), lambda i: (0, 0))],
        out_specs=pl.BlockSpec((tm, D), lambda i: (i, 0)), out_shape=S((T, D), BF),
        compiler_params=_cp(1))(x, g, after)


def _ffn_up_from_h(name, h, wup):
    T, D = h.shape
    FB = wup.shape[-2]
    tm = _tile(T, 1024)

    def body(h_ref, w_ref, a_ref):
        hb = h_ref[...]
        gg = _nt(hb, w_ref[0])
        uu = _nt(hb, w_ref[1])
        a_ref[...] = (gg * jax.nn.sigmoid(gg) * uu).astype(BF)

    return pl.pallas_call(
        body, name=name, grid=(T // tm, N_FFN_BLK),
        in_specs=[pl.BlockSpec((tm, D), lambda i, j: (i, 0)),
                  pl.BlockSpec((2, None, FB, D), lambda i, j: (0, j, 0, 0))],
        out_specs=pl.BlockSpec((None, tm, FB), lambda i, j: (j, i, 0)),
        out_shape=S((N_FFN_BLK, T, FB), BF),
        compiler_params=_cp(2))(h, wup)


def _ffn_up(name, x, g, wup):
    T, D = x.shape
    FB = wup.shape[-2]
    tm = _tile(T, 1024)

    def body(x_ref, g_ref, w_ref, a_ref, h_ref):
        @pl.when(pl.program_id(1) == 0)
        def _():
            xf = x_ref[...]
            h_ref[...] = (xf * _rstd(xf) * g_ref[...]).astype(BF)

        hb = h_ref[...]
        gg = _nt(hb, w_ref[0])
        uu = _nt(hb, w_ref[1])
        a_ref[...] = (gg * jax.nn.sigmoid(gg) * uu).astype(BF)

    return pl.pallas_call(
        body, name=name, grid=(T // tm, N_FFN_BLK),
        in_specs=[pl.BlockSpec((tm, D), lambda i, j: (i, 0)),
                  pl.BlockSpec((1, D), lambda i, j: (0, 0)),
                  pl.BlockSpec((2, None, FB, D), lambda i, j: (0, j, 0, 0))],
        out_specs=[pl.BlockSpec((None, tm, FB), lambda i, j: (j, i, 0)),
                   pl.BlockSpec((tm, D), lambda i, j: (i, 0))],
        out_shape=[S((N_FFN_BLK, T, FB), BF), S((T, D), BF)],
        compiler_params=_cp(2))(x, g, wup)


def _ffn_down(name, a, wdn, x):
    _, T, FB = a.shape
    D = x.shape[1]
    tm = _tile(T, 512)

    def body(a_ref, w_ref, x_ref, o_ref):
        p = _nn(a_ref[0], w_ref[0])
        for j in range(1, N_FFN_BLK):
            p = p + _nn(a_ref[j], w_ref[j])
        o_ref[...] = x_ref[...] + 0.5 * p

    return pl.pallas_call(
        body, name=name, grid=(T // tm,),
        in_specs=[pl.BlockSpec((N_FFN_BLK, tm, FB), lambda i: (0, i, 0)),
                  pl.BlockSpec((N_FFN_BLK, FB, D), lambda i: (0, 0, 0)),
                  pl.BlockSpec((tm, D), lambda i: (i, 0))],
        out_specs=pl.BlockSpec((tm, D), lambda i: (i, 0)),
        out_shape=S((T, D), F32),
        compiler_params=_cp(1))(a, wdn, x)


def _ffn_down_loss(name, a, wdn, x, target):
    _, T, FB = a.shape
    D = x.shape[1]
    tm = _tile(T, 512)

    def body(a_ref, w_ref, x_ref, t_ref, d_ref, db_ref, loss_ref):
        i = pl.program_id(0)
        p = _nn(a_ref[0], w_ref[0])
        for j in range(1, N_FFN_BLK):
            p = p + _nn(a_ref[j], w_ref[j])
        diff = (x_ref[...] + 0.5 * p) - t_ref[...]
        dy = diff * (1.0 / D)
        d_ref[...] = dy
        db_ref[...] = dy.astype(BF)
        sq = jnp.zeros((8, LANES), F32) + jnp.sum(diff * diff)

        @pl.when(i == 0)
        def _():
            loss_ref[...] = sq

        @pl.when(i > 0)
        def _():
            loss_ref[...] += sq

    row = pl.BlockSpec((tm, D), lambda i: (i, 0))
    return pl.pallas_call(
        body, name=name, grid=(T // tm,),
        in_specs=[pl.BlockSpec((N_FFN_BLK, tm, FB), lambda i: (0, i, 0)),
                  pl.BlockSpec((N_FFN_BLK, FB, D), lambda i: (0, 0, 0)), row, row],
        out_specs=[row, row, pl.BlockSpec((8, LANES), lambda i: (0, 0))],
        out_shape=[S((T, D), F32), S((T, D), BF), S((8, LANES), F32)],
        compiler_params=_cp(1))(a, wdn, x, target)


def _ffn_bwd_act(name, dyb, h, wup, wdn):
    T, D = h.shape
    FB = wup.shape[-2]
    tm = _tile(T, 1024)

    def body(d_ref, h_ref, wu_ref, wd_ref, o_ref):
        da = 0.5 * _nt(d_ref[...], wd_ref[...])
        hb = h_ref[...]
        gg = _nt(hb, wu_ref[0])
        uu = _nt(hb, wu_ref[1])
        sg = jax.nn.sigmoid(gg)
        o_ref[0] = (da * uu * (sg * (1.0 + gg * (1.0 - sg)))).astype(BF)
        o_ref[1] = (da * (gg * sg)).astype(BF)

    return pl.pallas_call(
        body, name=name, grid=(T // tm, N_FFN_BLK),
        in_specs=[pl.BlockSpec((tm, D), lambda i, j: (i, 0)),
                  pl.BlockSpec((tm, D), lambda i, j: (i, 0)),
                  pl.BlockSpec((2, None, FB, D), lambda i, j: (0, j, 0, 0)),
                  pl.BlockSpec((None, FB, D), lambda i, j: (j, 0, 0))],
        out_specs=pl.BlockSpec((2, None, tm, FB), lambda i, j: (0, j, i, 0)),
        out_shape=S((2, N_FFN_BLK, T, FB), BF),
        compiler_params=_cp(2))(dyb, h, wup, wdn)


def _ffn_dx(name, dgu, wup, x, g, dy):
    T, D = x.shape
    FB = wup.shape[-2]
    tm = _tile(T, 512)

    def body(d_ref, w_ref, x_ref, g_ref, dy_ref, dx_ref, dg_ref):
        p = None
        for j in range(N_FFN_BLK):
            for half in range(2):
                t = _nn(d_ref[half, j], w_ref[half, j])
                p = t if p is None else p + t
        dx, dgr = _norm_bwd(p, x_ref[...], g_ref[...])
        dx_ref[...] = dx + dy_ref[...]
        _acc_rows(dg_ref, pl.program_id(0) == 0, dgr)

    return pl.pallas_call(
        body, name=name, grid=(T // tm,),
        in_specs=[pl.BlockSpec((2, N_FFN_BLK, tm, FB), lambda i: (0, 0, i, 0)),
                  pl.BlockSpec((2, N_FFN_BLK, FB, D), lambda i: (0, 0, 0, 0), pipeline_mode=pl.Buffered(1)),
                  pl.BlockSpec((tm, D), lambda i: (i, 0)),
                  pl.BlockSpec((1, D), lambda i: (0, 0)),
                  pl.BlockSpec((tm, D), lambda i: (i, 0))],
        out_specs=[pl.BlockSpec((tm, D), lambda i: (i, 0)),
                   pl.BlockSpec((1, D), lambda i: (0, 0))],
        out_shape=[S((T, D), F32), S((1, D), F32)],
        compiler_params=_cp(1))(dgu, wup, x, g, dy)


def _tn_matmul(name, a, a_spec, b, out_shape, out_spec, n_blocks, scale=1.0, after=None):
    extra = [] if after is None else [after]

    def body(a_ref, b_ref, *rest):
        o_ref = rest[-1]
        o_ref[...] = (_tn(a_ref[...], b_ref[...]) * scale).astype(o_ref.dtype)

    return pl.pallas_call(
        body, name=name, grid=(n_blocks,),
        in_specs=[a_spec, pl.BlockSpec(b.shape, lambda j: (0, 0), pipeline_mode=pl.Buffered(1))]
        + [pl.BlockSpec((8, LANES), lambda j: (0, 0)) for _ in extra],
        out_specs=out_spec, out_shape=out_shape, compiler_params=_cp(1))(a, b, *extra)


def _ffn_dwup(name, h, dgu, after=None):
    T, D = h.shape
    FB = dgu.shape[-1]
    return _tn_matmul(
        name + "_dwup", dgu.reshape(2 * N_FFN_BLK, T, FB), pl.BlockSpec((None, T, FB), lambda j: (j, 0, 0)), h,
        S((2 * N_FFN_BLK, FB, D), BF), pl.BlockSpec((None, FB, D), lambda j: (j, 0, 0)), 2 * N_FFN_BLK,
        after=after)


def _ffn_dwdn(name, a, dyb):
    _, T, FB = a.shape
    D = dyb.shape[1]
    return _tn_matmul(
        name + "_dwdn", a, pl.BlockSpec((None, T, FB), lambda j: (j, 0, 0)), dyb,
        S((N_FFN_BLK, FB, D), BF), pl.BlockSpec((None, FB, D), lambda j: (j, 0, 0)), N_FFN_BLK, scale=0.5)


def _tri(n, lower):
    r = lax.broadcasted_iota(jnp.int32, (n, n), 0)
    c = lax.broadcasted_iota(jnp.int32, (n, n), 1)
    return (r >= c) if lower else (r <= c)


def _spatial_mix(vgn_b, ws_ref, bst, tm):
    tril = _tri(CHUNK, True)
    wms = [jnp.where(tril, ws_ref[g], 0.0).astype(BF) for g in range(GMLP_G)]
    rows = []
    for c in range(tm // CHUNK):
        cols = []
        for g in range(GMLP_G):
            vs = vgn_b[c * CHUNK:(c + 1) * CHUNK, g * GMLP_GD:(g + 1) * GMLP_GD]
            cols.append(_nn(wms[g], vs) + bst[:, g:g + 1])
        rows.append(jnp.concatenate(cols, axis=1))
    return jnp.concatenate(rows, axis=0), wms


HB = 128
AUG_W = FOX_HEADS * HB
COL_A, COL_B, COL_C = 64, 67, 70
RS_Q, RS_K, RS_V, RS_O = 0, 8, 16, 17


def _piece_matrix(col):
    r = jnp.arange(LANES)
    dst = jnp.where(r < 3 * FOX_HEADS, (r % FOX_HEADS) * HB + col + r // FOX_HEADS, -1)
    return (jnp.arange(AUG_W)[None, :] == dst[:, None]).astype(BF)


def _ones_row(cols):
    c = jnp.arange(AUG_W) % HB
    hit = functools.reduce(jnp.logical_or, [(c >= a) & (c < a + 3) for a in cols])
    return hit.astype(F32)[None, :]


def _pieces(x):
    lane = lax.broadcasted_iota(jnp.int32, x.shape, 1)
    x = jnp.where(lane < FOX_HEADS, x, 0.0)
    hi = x.astype(BF).astype(F32)
    r1 = x - hi
    mid = r1.astype(BF).astype(F32)
    lo = (r1 - mid).astype(BF).astype(F32)
    return (hi + pltpu.roll(mid, FOX_HEADS, 1) + pltpu.roll(lo, 2 * FOX_HEADS, 1)).astype(BF)


def _mix_prep(x, g_mix, wz, bf128, g_q, g_k, g_sgu, w_s, b_st, g_go):
    T, D = x.shape
    tm = _tile(T, 512)
    pc_q, pc_k = _piece_matrix(COL_A), _piece_matrix(COL_B)
    one_q, one_k, one_v = _ones_row([COL_B]), _ones_row([COL_A, COL_C]), _ones_row([COL_A])

    def body(x_ref, gm_ref, wz_ref, bf_ref, gq_ref, gk_ref, gs_ref, ws_ref, bst_ref, go_ref, pq_ref, pk_ref, oq_ref,
             ok_ref, ov_ref, z_ref, h_ref, q_ref, k_ref, v_ref, y_ref, rs_ref, carry_ref):
        i = pl.program_id(0)

        @pl.when(i == 0)
        def _():
            carry_ref[...] = jnp.zeros_like(carry_ref)

        xf = x_ref[...]
        hb = (xf * _rstd(xf) * gm_ref[...]).astype(BF)
        h_ref[...] = hb
        z_ref[...] = _nt(hb, wz_ref[...])

        fl = z_ref[:, Z_F:Z_F + LANES] + bf_ref[...]
        logf = jnp.minimum(fl, 0.0) - jnp.log1p(jnp.exp(-jnp.abs(fl)))
        csum = _hi(_tri(tm, True), logf) + carry_ref[...]
        carry_ref[...] = csum[tm - 1:tm, :]
        ext_q = (_nn(_pieces(csum), pq_ref[...]) + oq_ref[...]).astype(BF)
        ext_k = (_nn(_pieces(-csum), pk_ref[...]) + ok_ref[...]).astype(BF)
        ext_v = jnp.broadcast_to(ov_ref[...], (tm, AUG_W)).astype(BF)

        rs_ref[...] = jnp.zeros_like(rs_ref)
        for h in range(FOX_HEADS):
            lo, hi = slice(h * HB, h * HB + FOX_HD), slice(h * HB + FOX_HD, (h + 1) * HB)
            qh = z_ref[:, Z_Q + h * FOX_HD:Z_Q + (h + 1) * FOX_HD]
            kh = z_ref[:, Z_K + h * FOX_HD:Z_K + (h + 1) * FOX_HD]
            rq, rk = _rstd(qh), _rstd(kh)
            rs_ref[:, RS_Q + h:RS_Q + h + 1] = rq
            rs_ref[:, RS_K + h:RS_K + h + 1] = rk
            q_ref[:, lo] = (qh * rq * gq_ref[...] * 0.125).astype(BF)
            k_ref[:, lo] = (kh * rk * gk_ref[...]).astype(BF)
            v_ref[:, lo] = z_ref[:, Z_V + h * FOX_HD:Z_V + (h + 1) * FOX_HD].astype(BF)
            q_ref[:, hi] = ext_q[:, hi]
            k_ref[:, hi] = ext_k[:, hi]
            v_ref[:, hi] = ext_v[:, hi]

        u = _gelu(z_ref[:, Z_U:Z_U + GMLP_W])
        vg = _gelu(z_ref[:, Z_G:Z_G + GMLP_W])
        rv = _rstd(vg)
        vgn = (vg * rv * gs_ref[...]).astype(BF)
        mixed, _ = _spatial_mix(vgn, ws_ref, bst_ref[...], tm)
        sgu = u * mixed
        ro = _rstd(sgu)
        y_ref[...] = (sgu * ro * go_ref[...]).astype(BF)
        rs_ref[:, RS_V:RS_V + 1] = rv
        rs_ref[:, RS_O:RS_O + 1] = ro

    row = lambda i: (i, 0)
    fix2 = lambda i: (0, 0)
    return pl.pallas_call(
        body, name="mix_prep", grid=(T // tm,),
        in_specs=[pl.BlockSpec((tm, D), row), pl.BlockSpec((1, D), fix2),
                  pl.BlockSpec((ZW, D), fix2, pipeline_mode=pl.Buffered(1)),
                  pl.BlockSpec((1, LANES), fix2), pl.BlockSpec((1, FOX_HD), fix2), pl.BlockSpec((1, FOX_HD), fix2),
                  pl.BlockSpec((1, GMLP_W), fix2), pl.BlockSpec((GMLP_G, CHUNK, CHUNK), lambda i: (0, 0, 0)),
                  pl.BlockSpec((CHUNK, GMLP_G), fix2), pl.BlockSpec((1, GMLP_W), fix2),
                  pl.BlockSpec((LANES, AUG_W), fix2),
                  pl.BlockSpec((LANES, AUG_W), fix2), pl.BlockSpec((1, AUG_W), fix2), pl.BlockSpec((1, AUG_W), fix2),
                  pl.BlockSpec((1, AUG_W), fix2)],
        out_specs=[pl.BlockSpec((tm, ZW), row), pl.BlockSpec((tm, D), row),
                   pl.BlockSpec((tm, AUG_W), row), pl.BlockSpec((tm, AUG_W), row), pl.BlockSpec((tm, AUG_W), row),
                   pl.BlockSpec((tm, GMLP_W), row), pl.BlockSpec((tm, LANES), row)],
        out_shape=[S((T, ZW), F32), S((T, D), BF), S((T, AUG_W), BF), S((T, AUG_W), BF), S((T, AUG_W), BF),
                   S((T, GMLP_W), BF), S((T, LANES), F32)],
        scratch_shapes=[pltpu.VMEM((1, LANES), F32)],
        compiler_params=_cp(1))(x, g_mix, wz, bf128, g_q, g_k, g_sgu, w_s, b_st, g_go, pc_q, pc_k, one_q, one_k,
                                one_v)


def _fox_fwd(q, k, v):
    T = q.shape[0]
    tq = _tile(T, 1024)
    nq = T // tq

    pairs = [(i, j) for i in range(nq) for j in range(i + 1)]
    it = jnp.asarray([p[0] for p in pairs], jnp.int32)
    jt = jnp.asarray([p[1] for p in pairs], jnp.int32)

    def body(it_ref, jt_ref, q_ref, k_ref, v_ref, o_ref, lse_ref, m_sc, acc_sc):
        t = pl.program_id(0)
        i, j = it_ref[t], jt_ref[t]

        @pl.when(j == 0)
        def _():
            m_sc[...] = jnp.full(m_sc.shape, NEG, F32)
            acc_sc[...] = jnp.zeros_like(acc_sc)

        def step(masked):
            mask = _tri(tq, True) if masked else None
            for h in range(FOX_HEADS):
                hb = slice(h * HB, (h + 1) * HB)
                s = _nt(q_ref[:, hb], k_ref[:, hb])
                if masked:
                    s = jnp.where(mask, s, NEG)
                m_prev = m_sc[h]
                m_new = jnp.maximum(m_prev, jnp.broadcast_to(jnp.max(s, axis=1, keepdims=True), (tq, HB)))
                p = jnp.exp(s - jnp.tile(m_new, (1, tq // HB))).astype(BF)
                acc_sc[:, hb] = jnp.exp(m_prev - m_new) * acc_sc[:, hb] + _nn(p, v_ref[:, hb])
                m_sc[h] = m_new

        @pl.when(j < i)
        def _():
            step(False)

        @pl.when(j == i)
        def _():
            step(True)
            lse_ref[...] = jnp.zeros_like(lse_ref)
            for h in range(FOX_HEADS):
                l = acc_sc[:, h * HB + COL_A:h * HB + COL_A + 1]
                o_ref[:, h * FOX_HD:(h + 1) * FOX_HD] = acc_sc[:, h * HB:h * HB + FOX_HD] / l
                lse_ref[:, h:h + 1] = m_sc[h][:, 0:1] + jnp.log(l)

    qi = lambda t, it_ref, jt_ref: (it_ref[t], 0)
    kj = lambda t, it_ref, jt_ref: (jt_ref[t], 0)
    return pl.pallas_call(
        body, name="fox_fwd",
        grid_spec=pltpu.PrefetchScalarGridSpec(
            num_scalar_prefetch=2, grid=(len(pairs),),
            in_specs=[pl.BlockSpec((tq, AUG_W), qi), pl.BlockSpec((tq, AUG_W), kj), pl.BlockSpec((tq, AUG_W), kj)],
            out_specs=[pl.BlockSpec((tq, FOX_W), qi), pl.BlockSpec((tq, LANES), qi)],
            scratch_shapes=[pltpu.VMEM((FOX_HEADS, tq, HB), F32), pltpu.VMEM((tq, AUG_W), F32)]),
        out_shape=[S((T, FOX_W), F32), S((T, LANES), F32)],
        compiler_params=_cp(1))(it, jt, q, k, v)


def _fox_bwd(q, k, v, dob):
    T = q.shape[0]
    tq = _tile(T, 1024)
    nq = T // tq
    n_sweeps = 2
    half = AUG_W // n_sweeps
    hpg = FOX_HEADS // n_sweeps

    pairs = [(j, i) for j in range(nq) for i in range(j, nq)]
    jt = jnp.asarray([p[0] for p in pairs], jnp.int32)
    it = jnp.asarray([p[1] for p in pairs], jnp.int32)

    def body(jt_ref, it_ref, q_ref, k_ref, v_ref, do_ref, dq_ref, dk_ref, dv_ref, dq_sc):
        t = pl.program_id(1)
        j, i = jt_ref[t], it_ref[t]

        @pl.when(t == 0)
        def _():
            dq_sc[...] = jnp.zeros_like(dq_sc)

        @pl.when(i == j)
        def _():
            dk_ref[...] = jnp.zeros_like(dk_ref)
            dv_ref[...] = jnp.zeros_like(dv_ref)

        def step(masked):
            rows = pl.ds(pl.multiple_of(i * tq, tq), tq)
            mask = _tri(tq, True) if masked else None
            for h in range(hpg):
                hb = slice(h * HB, (h + 1) * HB)
                qh, kh, vh, doh = q_ref[:, hb], k_ref[:, hb], v_ref[:, hb], do_ref[:, hb]
                s = _nt(qh, kh)
                if masked:
                    s = jnp.where(mask, s, NEG)
                p = jnp.exp(s)
                dsb = (p * _nt(doh, vh)).astype(BF)
                dv_ref[:, hb] += _tn(p.astype(BF), doh)
                dk_ref[:, hb] += _tn(dsb, qh)
                dq_sc[rows, hb] += _nn(dsb, kh)

        @pl.when(i > j)
        def _():
            step(False)

        @pl.when(i == j)
        def _():
            step(True)
            dq_ref[...] = dq_sc[pl.ds(pl.multiple_of(j * tq, tq), tq), :]

    qi = pl.BlockSpec((tq, half), lambda g, t, jt_ref, it_ref: (it_ref[t], g))
    kj = pl.BlockSpec((tq, half), lambda g, t, jt_ref, it_ref: (jt_ref[t], g))
    return pl.pallas_call(
        body, name="fox_bwd",
        grid_spec=pltpu.PrefetchScalarGridSpec(
            num_scalar_prefetch=2, grid=(n_sweeps, len(pairs)), in_specs=[qi, kj, kj, qi], out_specs=[kj, kj, kj],
            scratch_shapes=[pltpu.VMEM((T, half), F32)]),
        out_shape=[S((T, AUG_W), F32), S((T, AUG_W), F32), S((T, AUG_W), F32)],
        compiler_params=_cp(2))(jt, it, q, k, v, dob)


def _mix_out(attn, yg, g_fo, wout, x):
    T, D = x.shape
    tm = _tile(T, 1024)

    def body(a_ref, y_ref, g_ref, w_ref, x_ref, o_ref):
        at = a_ref[...]
        yf = (at * _rstd(at) * g_ref[...]).astype(BF)
        o_ref[...] = x_ref[...] + _nn(yf, w_ref[:FOX_W, :]) + _nn(y_ref[...], w_ref[FOX_W:, :])

    row = lambda i: (i, 0)
    return pl.pallas_call(
        body, name="mix_out", grid=(T // tm,),
        in_specs=[pl.BlockSpec((tm, FOX_W), row), pl.BlockSpec((tm, GMLP_W), row),
                  pl.BlockSpec((1, FOX_W), lambda i: (0, 0)), pl.BlockSpec((D, D), lambda i: (0, 0)),
                  pl.BlockSpec((tm, D), row)],
        out_specs=pl.BlockSpec((tm, D), row),
        out_shape=S((T, D), F32),
        compiler_params=_cp(1))(attn, yg, g_fo, wout, x)


def _mix_out_bwd(dx, attn, yg, g_fo, wout, qf, lse):
    T, D = dx.shape
    tm = _tile(T, 512)
    n = T // tm
    pc_l, pc_d = _piece_matrix(COL_C), _piece_matrix(COL_A)

    def body(dx_ref, a_ref, y_ref, g_ref, w_ref, qf_ref, lse_ref, pl_ref, pd_ref,
             qb_ref, dob_ref, dyg_ref, dw_ref, dg_ref, acc_ref, dsum_ref):
        i = pl.program_id(0)
        dxb = dx_ref[...].astype(BF)
        at = a_ref[...]
        yf = (at * _rstd(at) * g_ref[...]).astype(BF)
        dy = _nt(dxb, w_ref[...])
        p_top = _tn(yf, dxb)
        p_bot = _tn(y_ref[...], dxb)

        @pl.when(i == 0)
        def _():
            acc_ref[:FOX_W, :] = p_top
            acc_ref[FOX_W:, :] = p_bot

        @pl.when(i > 0)
        def _():
            acc_ref[:FOX_W, :] += p_top
            acc_ref[FOX_W:, :] += p_bot

        @pl.when(i == n - 1)
        def _():
            dw_ref[...] = acc_ref[...].astype(BF)

        dat, dgr = _norm_bwd(dy[:, :FOX_W], at, g_ref[...])
        _acc_rows(dg_ref, i == 0, dgr)
        dyg_ref[...] = dy[:, FOX_W:]
        prod = dat * at
        dsum_ref[...] = jnp.zeros_like(dsum_ref)
        for h in range(FOX_HEADS):
            dsum_ref[:, h:h + 1] = jnp.sum(prod[:, h * FOX_HD:(h + 1) * FOX_HD], axis=1, keepdims=True)
        ext_d = _nn(_pieces(-dsum_ref[...]), pd_ref[...]).astype(BF)
        ext_l = _nn(_pieces(-lse_ref[...]), pl_ref[...])
        datb = dat.astype(BF)
        for h in range(FOX_HEADS):
            lo, hi = slice(h * HB, h * HB + FOX_HD), slice(h * HB + FOX_HD, (h + 1) * HB)
            dob_ref[:, lo] = datb[:, h * FOX_HD:(h + 1) * FOX_HD]
            dob_ref[:, hi] = ext_d[:, hi]
            qb_ref[:, lo] = qf_ref[:, lo]
            qb_ref[:, hi] = (qf_ref[:, hi].astype(F32) + ext_l[:, hi]).astype(BF)

    row = lambda i: (i, 0)
    fix = lambda i: (0, 0)
    return pl.pallas_call(
        body, name="mix_out_bwd", grid=(n,),
        in_specs=[pl.BlockSpec((tm, D), row), pl.BlockSpec((tm, FOX_W), row), pl.BlockSpec((tm, GMLP_W), row),
                  pl.BlockSpec((1, FOX_W), fix), pl.BlockSpec((D, D), fix), pl.BlockSpec((tm, AUG_W), row),
                  pl.BlockSpec((tm, LANES), row), pl.BlockSpec((LANES, AUG_W), fix),
                  pl.BlockSpec((LANES, AUG_W), fix)],
        out_specs=[pl.BlockSpec((tm, AUG_W), row), pl.BlockSpec((tm, AUG_W), row), pl.BlockSpec((tm, GMLP_W), row),
                   pl.BlockSpec((D, D), fix), pl.BlockSpec((1, FOX_W), fix)],
        out_shape=[S((T, AUG_W), BF), S((T, AUG_W), BF), S((T, GMLP_W), F32), S((D, D), BF), S((1, FOX_W), F32)],
        scratch_shapes=[pltpu.VMEM((D, D), F32), pltpu.VMEM((tm, LANES), F32)],
        compiler_params=_cp(1))(dx, attn, yg, g_fo, wout, qf, lse, pc_l, pc_d)


def _mix_prep_bwd(z, dq, dk, dv, dyg, rs, bf128, g_q, g_k, g_sgu, w_s, b_st, g_go):
    T = z.shape[0]
    tm = _tile(T, 512)
    n = T // tm

    def body(z_ref, dq_ref, dk_ref, dv_ref, dyg_ref, rs_ref, bf_ref, gq_ref, gk_ref, gs_ref, ws_ref,
             bst_ref, go_ref, dz_ref, dgq_ref, dgk_ref, dgs_ref, dgo_ref, dws_ref, dbst_ref, dbf_ref, carry_ref):
        i = pl.program_id(0)
        first = i == 0
        rs = rs_ref[...]

        @pl.when(first)
        def _():
            carry_ref[...] = jnp.zeros_like(carry_ref)

        lane = lax.broadcasted_iota(jnp.int32, (tm, LANES), 1)
        dc = jnp.zeros((tm, LANES), F32)
        gq_rows, gk_rows = [], []
        for h in range(FOX_HEADS):
            hp = slice(h * HB, h * HB + FOX_HD)
            dqh, gqr = _norm_bwd(dq_ref[:, hp] * 0.125, z_ref[:, Z_Q + h * FOX_HD:Z_Q + (h + 1) * FOX_HD], gq_ref[...],
                                 rs[:, RS_Q + h:RS_Q + h + 1])
            dkh, gkr = _norm_bwd(dk_ref[:, hp], z_ref[:, Z_K + h * FOX_HD:Z_K + (h + 1) * FOX_HD], gk_ref[...],
                                 rs[:, RS_K + h:RS_K + h + 1])
            dz_ref[:, Z_Q + h * FOX_HD:Z_Q + (h + 1) * FOX_HD] = dqh.astype(BF)
            dz_ref[:, Z_K + h * FOX_HD:Z_K + (h + 1) * FOX_HD] = dkh.astype(BF)
            dz_ref[:, Z_V + h * FOX_HD:Z_V + (h + 1) * FOX_HD] = dv_ref[:, hp].astype(BF)
            dch = dq_ref[:, h * HB + COL_A:h * HB + COL_A + 1] - dk_ref[:, h * HB + COL_B:h * HB + COL_B + 1]
            dc = jnp.where(lane == h, dch, dc)
            gq_rows.append(gqr)
            gk_rows.append(gkr)
        _acc_rows(dgq_ref, first, functools.reduce(lambda a, b: a + b, gq_rows))
        _acc_rows(dgk_ref, first, functools.reduce(lambda a, b: a + b, gk_rows))

        dlogf = _hi3(_tri(tm, False), dc) + carry_ref[...]
        carry_ref[...] = dlogf[0:1, :]
        fl = z_ref[:, Z_F:Z_F + LANES] + bf_ref[...]
        lane = lax.broadcasted_iota(jnp.int32, (tm, LANES), 1)
        df = jnp.where(lane < FOX_HEADS, dlogf * jax.nn.sigmoid(-fl), 0.0)
        dz_ref[:, Z_F:Z_F + LANES] = df.astype(BF)
        _acc_rows(dbf_ref, first, df)

        u_pre = z_ref[:, Z_U:Z_U + GMLP_W]
        vg_pre = z_ref[:, Z_G:Z_G + GMLP_W]
        u = _gelu(u_pre)
        vg = _gelu(vg_pre)
        rv = rs[:, RS_V:RS_V + 1]
        vgn = (vg * rv * gs_ref[...]).astype(BF)
        bst = bst_ref[...]
        mixed, wms = _spatial_mix(vgn, ws_ref, bst, tm)
        sgu = u * mixed
        dsgu, gor = _norm_bwd(dyg_ref[...], sgu, go_ref[...], rs[:, RS_O:RS_O + 1])
        _acc_rows(dgo_ref, first, gor)
        du = dsgu * mixed
        dmixed = dsgu * u
        dmb = dmixed.astype(BF)
        tril = _tri(CHUNK, True)
        dvgn_rows = []
        dws = [None] * GMLP_G
        dbs = [None] * GMLP_G
        for c in range(tm // CHUNK):
            cs = slice(c * CHUNK, (c + 1) * CHUNK)
            cols = []
            for g in range(GMLP_G):
                gs = slice(g * GMLP_GD, (g + 1) * GMLP_GD)
                dmc = dmb[cs, gs]
                pw = _nt(dmc, vgn[cs, gs])
                pb = jnp.sum(dmixed[cs, gs], axis=1, keepdims=True)
                dws[g] = pw if dws[g] is None else dws[g] + pw
                dbs[g] = pb if dbs[g] is None else dbs[g] + pb
                cols.append(_tn(wms[g], dmc))
            dvgn_rows.append(jnp.concatenate(cols, axis=1))
        dvgn = jnp.concatenate(dvgn_rows, axis=0)
        dbs_t = jnp.concatenate(dbs, axis=1)
        for g in range(GMLP_G):
            dwg = jnp.where(tril, dws[g], 0.0)

            @pl.when(first)
            def _():
                dws_ref[g] = dwg

            @pl.when(jnp.logical_not(first))
            def _():
                dws_ref[g] += dwg

        @pl.when(first)
        def _():
            dbst_ref[...] = dbs_t

        @pl.when(jnp.logical_not(first))
        def _():
            dbst_ref[...] += dbs_t

        dvg, gsr = _norm_bwd(dvgn, vg, gs_ref[...], rv)
        _acc_rows(dgs_ref, first, gsr)
        dz_ref[:, Z_U:Z_U + GMLP_W] = (du * _gelu_grad(u_pre)).astype(BF)
        dz_ref[:, Z_G:Z_G + GMLP_W] = (dvg * _gelu_grad(vg_pre)).astype(BF)

    rev = lambda i: (n - 1 - i, 0)
    fix = lambda i: (0, 0)
    fix3 = lambda i: (0, 0, 0)
    return pl.pallas_call(
        body, name="mix_prep_bwd", grid=(n,),
        in_specs=[pl.BlockSpec((tm, ZW), rev), pl.BlockSpec((tm, AUG_W), rev), pl.BlockSpec((tm, AUG_W), rev),
                  pl.BlockSpec((tm, AUG_W), rev), pl.BlockSpec((tm, GMLP_W), rev), pl.BlockSpec((tm, LANES), rev),
                  pl.BlockSpec((1, LANES), fix), pl.BlockSpec((1, FOX_HD), fix), pl.BlockSpec((1, FOX_HD), fix),
                  pl.BlockSpec((1, GMLP_W), fix), pl.BlockSpec((GMLP_G, CHUNK, CHUNK), fix3),
                  pl.BlockSpec((CHUNK, GMLP_G), fix), pl.BlockSpec((1, GMLP_W), fix)],
        out_specs=[pl.BlockSpec((tm, ZW), rev), pl.BlockSpec((1, FOX_HD), fix), pl.BlockSpec((1, FOX_HD), fix),
                   pl.BlockSpec((1, GMLP_W), fix), pl.BlockSpec((1, GMLP_W), fix),
                   pl.BlockSpec((GMLP_G, CHUNK, CHUNK), fix3), pl.BlockSpec((CHUNK, GMLP_G), fix),
                   pl.BlockSpec((1, LANES), fix)],
        out_shape=[S((T, ZW), BF), S((1, FOX_HD), F32), S((1, FOX_HD), F32), S((1, GMLP_W), F32), S((1, GMLP_W), F32),
                   S((GMLP_G, CHUNK, CHUNK), F32), S((CHUNK, GMLP_G), F32), S((1, LANES), F32)],
        scratch_shapes=[pltpu.VMEM((1, LANES), F32)],
        compiler_params=_cp(1))(z, dq, dk, dv, dyg, rs, bf128, g_q, g_k, g_sgu, w_s, b_st, g_go)


def _mix_proj_bwd(dz, wz, x, g, dy):
    T, D = x.shape
    tm = _tile(T, 512)

    def body(dz_ref, w_ref, x_ref, g_ref, dy_ref, dx_ref, dxb_ref, dg_ref):
        dh = _nn(dz_ref[...], w_ref[...])
        dx, dgr = _norm_bwd(dh, x_ref[...], g_ref[...])
        dx = dx + dy_ref[...]
        dx_ref[...] = dx
        dxb_ref[...] = dx.astype(BF)
        _acc_rows(dg_ref, pl.program_id(0) == 0, dgr)

    row = lambda i: (i, 0)
    fix = lambda i: (0, 0)
    return pl.pallas_call(
        body, name="mix_proj_bwd", grid=(T // tm,),
        in_specs=[pl.BlockSpec((tm, ZW), row), pl.BlockSpec((ZW, D), fix), pl.BlockSpec((tm, D), row),
                  pl.BlockSpec((1, D), fix), pl.BlockSpec((tm, D), row)],
        out_specs=[pl.BlockSpec((tm, D), row), pl.BlockSpec((tm, D), row), pl.BlockSpec((1, D), fix)],
        out_shape=[S((T, D), F32), S((T, D), BF), S((1, D), F32)],
        compiler_params=_cp(1))(dz, wz, x, g, dy)


def _ca_kv(mem, g_mem, wckv, g_ck):
    M, D = mem.shape

    def body(m_ref, g_ref, w_ref, gk_ref, mn_ref, kr_ref, kn_ref, v_ref):
        mf = m_ref[...]
        mn = (mf * _rstd(mf) * g_ref[...]).astype(BF)
        mn_ref[...] = mn
        for h in range(CA_HEADS):
            kr = _nn(mn, w_ref[h])
            kr_ref[h] = kr
            kn_ref[h] = (kr * _rstd(kr) * gk_ref[...]).astype(BF)
            v_ref[h] = _nn(mn, w_ref[CA_HEADS + h]).astype(BF)

    hd = (CA_HEADS, M, CA_HD)
    return pl.pallas_call(
        body, name="ca_kv", out_shape=[S((M, D), BF), S(hd, F32), S(hd, BF), S(hd, BF)],
        compiler_params=pltpu.CompilerParams(vmem_limit_bytes=VMEM_LIMIT))(mem, g_mem, wckv, g_ck)


def _ca_tile_fwd(xt, gca, wcq, gcq, kn_ref, v_ref):
    hb = (xt * _rstd(xt) * gca).astype(BF)
    qc = _nn(hb, wcq)
    qr, qn, ps = [], [], []
    for h in range(CA_HEADS):
        qh = qc[:, h * CA_HD:(h + 1) * CA_HD]
        qnh = (qh * _rstd(qh) * gcq * 0.0625).astype(BF)
        s = _nt(qnh, kn_ref[h])
        e = jnp.exp(s - jnp.max(s, axis=1, keepdims=True))
        ps.append(e / jnp.sum(e, axis=1, keepdims=True))
        qr.append(qh)
        qn.append(qnh)
    return hb, qr, qn, ps


def _ca_fwd(x, g_ca, wcq, g_cq, kn, vv, wco):
    T, D = x.shape
    M = kn.shape[1]
    tm = _tile(T, 1024)

    def body(x_ref, gca_ref, wcq_ref, gcq_ref, kn_ref, v_ref, wco_ref, o_ref, ob_sc):
        xt = x_ref[...]
        _, _, _, ps = _ca_tile_fwd(xt, gca_ref[...], wcq_ref[...], gcq_ref[...], kn_ref, v_ref)
        for h in range(CA_HEADS):
            ob_sc[:, h * CA_HD:(h + 1) * CA_HD] = _nn(ps[h].astype(BF), v_ref[h]).astype(BF)
        o_ref[...] = xt + _nn(ob_sc[...], wco_ref[...])

    row = lambda i: (i, 0)
    fix = lambda i: (0, 0)
    fix3 = lambda i: (0, 0, 0)
    return pl.pallas_call(
        body, name="ca_fwd", grid=(T // tm,),
        in_specs=[pl.BlockSpec((tm, D), row), pl.BlockSpec((1, D), fix), pl.BlockSpec((D, D), fix),
                  pl.BlockSpec((1, CA_HD), fix), pl.BlockSpec((CA_HEADS, M, CA_HD), fix3),
                  pl.BlockSpec((CA_HEADS, M, CA_HD), fix3), pl.BlockSpec((D, D), fix)],
        out_specs=pl.BlockSpec((tm, D), row), out_shape=S((T, D), F32),
        scratch_shapes=[pltpu.VMEM((tm, D), BF)],
        compiler_params=_cp(1))(x, g_ca, wcq, g_cq, kn, vv, wco)


def _ca_bwd(x, dy, g_ca, wcq, g_cq, kn, vv, wco):
    T, D = x.shape
    M = kn.shape[1]
    tm = _tile(T, 512)
    n = T // tm

    def body(x_ref, dy_ref, gca_ref, wcq_ref, gcq_ref, kn_ref, v_ref, wco_ref,
             dx_ref, dwq_ref, dwo_ref, dkn_ref, dv_ref, dgcq_ref, dgca_ref, aq_sc, ao_sc, ob_sc, dq_sc):
        i = pl.program_id(0)
        first = i == 0
        xt = x_ref[...]
        dyt = dy_ref[...]
        dyb = dyt.astype(BF)
        hb, qr, qn, ps = _ca_tile_fwd(xt, gca_ref[...], wcq_ref[...], gcq_ref[...], kn_ref, v_ref)
        do = _nt(dyb, wco_ref[...])
        gcq_rows = None
        for h in range(CA_HEADS):
            hs = slice(h * CA_HD, (h + 1) * CA_HD)
            p = ps[h]
            pb = p.astype(BF)
            ob_sc[:, hs] = _nn(pb, v_ref[h]).astype(BF)
            doh = do[:, hs].astype(BF)
            dp = _nt(doh, v_ref[h])
            ds = (p * (dp - jnp.sum(dp * p, axis=1, keepdims=True))).astype(BF)
            dvh = _tn(pb, doh)
            dkh = _tn(ds, qn[h])

            @pl.when(first)
            def _():
                dv_ref[h] = dvh
                dkn_ref[h] = dkh

            @pl.when(jnp.logical_not(first))
            def _():
                dv_ref[h] += dvh
                dkn_ref[h] += dkh

            dqn = _nn(ds, kn_ref[h]) * 0.0625
            dqh, gr = _norm_bwd(dqn, qr[h], gcq_ref[...])
            gcq_rows = gr if gcq_rows is None else gcq_rows + gr
            dq_sc[:, hs] = dqh.astype(BF)
        _acc_rows(dgcq_ref, first, gcq_rows)
        dqb = dq_sc[...]
        p_o = _tn(ob_sc[...], dyb)
        p_q = _tn(hb, dqb)

        @pl.when(first)
        def _():
            ao_sc[...] = p_o
            aq_sc[...] = p_q

        @pl.when(jnp.logical_not(first))
        def _():
            ao_sc[...] += p_o
            aq_sc[...] += p_q

        @pl.when(i == n - 1)
        def _():
            dwo_ref[...] = ao_sc[...].astype(BF)
            dwq_ref[...] = aq_sc[...].astype(BF)

        dh = _nt(dqb, wcq_ref[...])
        dx, gar = _norm_bwd(dh, xt, gca_ref[...])
        dx_ref[...] = dx + dyt
        _acc_rows(dgca_ref, first, gar)

    row = lambda i: (i, 0)
    fix = lambda i: (0, 0)
    fix3 = lambda i: (0, 0, 0)
    hd = (CA_HEADS, M, CA_HD)
    return pl.pallas_call(
        body, name="ca_bwd", grid=(n,),
        in_specs=[pl.BlockSpec((tm, D), row), pl.BlockSpec((tm, D), row), pl.BlockSpec((1, D), fix),
                  pl.BlockSpec((D, D), fix), pl.BlockSpec((1, CA_HD), fix), pl.BlockSpec(hd, fix3),
                  pl.BlockSpec(hd, fix3), pl.BlockSpec((D, D), fix)],
        out_specs=[pl.BlockSpec((tm, D), row), pl.BlockSpec((D, D), fix), pl.BlockSpec((D, D), fix),
                   pl.BlockSpec(hd, fix3), pl.BlockSpec(hd, fix3), pl.BlockSpec((1, CA_HD), fix),
                   pl.BlockSpec((1, D), fix)],
        out_shape=[S((T, D), F32), S((D, D), BF), S((D, D), BF), S(hd, F32), S(hd, F32), S((1, CA_HD), F32),
                   S((1, D), F32)],
        scratch_shapes=[pltpu.VMEM((D, D), F32), pltpu.VMEM((D, D), F32), pltpu.VMEM((tm, D), BF),
                        pltpu.VMEM((tm, D), BF)],
        compiler_params=_cp(1))(x, dy, g_ca, wcq, g_cq, kn, vv, wco)


def _ca_kv_bwd(mem, g_mem, mn, kraw, dkn, dvv, wckv, g_ck):
    M, D = mem.shape

    def body(m_ref, g_ref, mn_ref, kr_ref, dkn_ref, dv_ref, w_ref, gk_ref, dw_ref, dgk_ref, dgm_ref):
        mn = mn_ref[...]
        dmn = jnp.zeros((M, D), F32)
        gk_rows = None
        for h in range(CA_HEADS):
            dkr, gr = _norm_bwd(dkn_ref[h], kr_ref[h], gk_ref[...])
            gk_rows = gr if gk_rows is None else gk_rows + gr
            dkb = dkr.astype(BF)
            dvb = dv_ref[h].astype(BF)
            dw_ref[h] = _tn(mn, dkb).astype(BF)
            dw_ref[CA_HEADS + h] = _tn(mn, dvb).astype(BF)
            dmn = dmn + _nt(dkb, w_ref[h]) + _nt(dvb, w_ref[CA_HEADS + h])
        dgk_ref[...] = jnp.sum(gk_rows, axis=0, keepdims=True)
        mf = m_ref[...]
        dgm_ref[...] = jnp.sum(dmn * (mf * _rstd(mf)), axis=0, keepdims=True)

    return pl.pallas_call(
        body, name="ca_kv_bwd",
        out_shape=[S((2 * CA_HEADS, D, CA_HD), BF), S((1, CA_HD), F32), S((1, D), F32)],
        compiler_params=pltpu.CompilerParams(vmem_limit_bytes=VMEM_LIMIT))(mem, g_mem, mn, kraw, dkn, dvv, wckv, g_ck)


def _after(g, token):
    return g if token is None else g + token[0:1, 0:1]


def _local_step(x, mem, target, small, weights, emit, h1=None):
    T, D = x.shape
    p = small
    bf128 = jnp.pad(p["b_f"], ((0, 0), (0, LANES - FOX_HEADS)))
    b_st = p["b_s"].T

    wup1 = weights("ffn1_up", x if h1 is None else h1)["wup1"]
    if h1 is None:
        a1, h1 = _ffn_up("ffn1_up", x, p["g_ffn1"], wup1)
    else:
        a1 = _ffn_up_from_h("ffn1_up", h1, wup1)
    wdn1 = weights("ffn1_dn", a1)["wdn1"]
    x1 = _ffn_down("ffn1_down", a1, wdn1, x)
    wm = weights("mix", x1)
    z, h2, qf, ka, va, yg, rs = _mix_prep(x1, p["g_mix"], wm["wz"], bf128, p["g_q"], p["g_k"], p["g_sgu"], p["w_s"],
                                          b_st, p["g_gmlp_o"])
    attn, lse = _fox_fwd(qf, ka, va)
    x2 = _mix_out(attn, yg, p["g_fox_o"], wm["wout"], x1)
    wc = weights("ca", x2)
    mn, kraw, ckn, cvv = _ca_kv(mem, p["g_mem"], wc["wckv"], p["g_ck"])
    x3 = _ca_fwd(x2, p["g_ca"], wc["wcq"], p["g_cq"], ckn, cvv, wc["wco"])
    w2 = weights("ffn2", x3)
    a2, h4 = _ffn_up("ffn2_up", x3, p["g_ffn2"], w2["wup2"])
    dy4, dy4b, sq = _ffn_down_loss("ffn2_down", a2, w2["wdn2"], x3, target)

    gs = {}
    dgu2 = _ffn_bwd_act("ffn2_bwd_act", dy4b, h4, w2["wup2"], w2["wdn2"])
    tok = emit("ffn2", {"wup2": _ffn_dwup("ffn2", h4, dgu2), "wdn2": _ffn_dwdn("ffn2", a2, dy4b)})
    dx3, gs["g_ffn2"] = _ffn_dx("ffn2_dx", dgu2, w2["wup2"], x3, _after(p["g_ffn2"], tok), dy4)

    dx2, dwcq, dwco, dckn, dcvv, gs["g_cq"], gs["g_ca"] = _ca_bwd(
        x2, dx3, p["g_ca"], wc["wcq"], p["g_cq"], ckn, cvv, wc["wco"])
    dwckv, gs["g_ck"], gs["g_mem"] = _ca_kv_bwd(mem, p["g_mem"], mn, kraw, dckn, dcvv, wc["wckv"], p["g_ck"])

    qb, dob, dyg, dwout, gs["g_fox_o"] = _mix_out_bwd(dx2, attn, yg, p["g_fox_o"], wm["wout"], qf, lse)
    dq, dk, dv = _fox_bwd(qb, ka, va, dob)
    dz, gs["g_q"], gs["g_k"], gs["g_sgu"], gs["g_gmlp_o"], gs["w_s"], dbst, dbf = _mix_prep_bwd(
        z, dq, dk, dv, dyg, rs, bf128, p["g_q"], p["g_k"], p["g_sgu"], p["w_s"], b_st, p["g_gmlp_o"])
    gs["b_s"] = dbst.T
    gs["b_f"] = dbf[:, :FOX_HEADS]
    tok_ws = emit("w_s", {"w_s": gs["w_s"]})
    zb = ZW // 3
    dwz = _tn_matmul("mix_dwz", dz, pl.BlockSpec((T, zb), lambda j: (0, j)), h2,
                     S((ZW, D), BF), pl.BlockSpec((zb, D), lambda j: (j, 0)), 3)
    tok = emit("mid", {"wcq": dwcq, "wco": dwco, "wckv": dwckv, "wout": dwout, "wz": dwz})
    dx1, dx1b, gs["g_mix"] = _mix_proj_bwd(dz, wm["wz"], x1, _after(_after(p["g_mix"], tok), tok_ws), dx2)

    dgu1 = _ffn_bwd_act("ffn1_bwd_act", dx1b, h1, wup1, wdn1)
    tok = emit("ffn1_dn", {"wdn1": _ffn_dwdn("ffn1", a1, dx1b)})
    tok = emit("ffn1_up", {"wup1": _ffn_dwup("ffn1", h1, dgu1, after=tok)})
    dx0, gs["g_ffn1"] = _ffn_dx("ffn1_dx", dgu1, wup1, x, _after(p["g_ffn1"], tok), dx1)
    return sq, dx0, gs


MESH = pl.DeviceIdType.MESH
HBM_SPEC = pl.BlockSpec(memory_space=pltpu.HBM)
N_PEER = N_DEV - 1


def _place():
    return lax.axis_index("x"), lax.axis_index("y"), lax.axis_index("c")


def _slot(px, py, pc):
    return 4 * px + 2 * py + pc


SEM_SPEC = pl.BlockSpec(memory_space=pltpu.SEMAPHORE)
ANY_SPEC = pl.BlockSpec(memory_space=pl.ANY)
DATAFLOW = pltpu.SideEffectType.DATAFLOW_SIDE_EFFECTING


def _hbm(a):
    return pltpu.with_memory_space_constraint(a, pltpu.HBM)


def _peer(x, y, c, r):
    return (1 - x if r & 4 else x, 1 - y if r & 2 else y, 1 - c if r & 1 else c)


def _place_own(srcs, whole):
    my = _slot(*_place())
    lands = []
    for s in srcs:
        blk = s[None] if whole else lax.dynamic_slice_in_dim(s, my, 1, 0)
        shape = (N_DEV,) + s.shape if whole else s.shape
        lands.append(lax.dynamic_update_slice_in_dim(lax.empty(shape, s.dtype), blk, my, 0))
    return lands


ALL_PEERS = tuple(range(1, N_DEV))
NEAR_PEERS = (1, 2, 4, 6)
SAME_CORE = (2, 4, 6)


def _copy_start(name, srcs, lands, whole, peers=None):
    n = len(srcs)
    peers = peers or [ALL_PEERS] * n
    wh = list(whole) if isinstance(whole, (list, tuple)) else [whole] * n

    def body(*refs):
        src, land = refs[:n], refs[n:2 * n]
        send, recv = refs[2 * n:3 * n], refs[3 * n:4 * n]
        token = refs[6 * n]
        x, y, c = _place()
        my = _slot(x, y, c)
        for a in range(n):
            for r in peers[a]:
                p = _peer(x, y, c, r)
                pltpu.make_async_remote_copy(
                    src_ref=src[a] if wh[a] else src[a].at[_slot(*p)], dst_ref=land[a].at[my],
                    send_sem=send[a].at[r - 1], recv_sem=recv[a].at[r - 1], device_id=p, device_id_type=MESH).start()
        token[...] = jnp.zeros_like(token)

    out = pl.pallas_call(
        body, name=name,
        out_shape=([pltpu.SemaphoreType.DMA((N_PEER,))] * (2 * n)
                   + [pltpu.HBM(s.shape, s.dtype) for s in srcs] + [pltpu.HBM(s.shape, s.dtype) for s in lands]
                   + [S((8, LANES), F32)]),
        in_specs=[HBM_SPEC] * (2 * n),
        out_specs=[SEM_SPEC] * (2 * n) + [HBM_SPEC] * (2 * n) + [pl.BlockSpec(memory_space=pltpu.VMEM)],
        input_output_aliases={i: 2 * n + i for i in range(2 * n)},
        compiler_params=pltpu.CompilerParams(has_side_effects=DATAFLOW),
    )(*[_hbm(s) for s in srcs], *[_hbm(s) for s in lands])
    return out[:n], out[n:2 * n], out[2 * n:3 * n], out[3 * n:4 * n], out[4 * n]


def _copy_wait(name, srcs, lands, send, recv, after, whole, peers=None, with_srcs=False):
    n = len(srcs)
    peers = peers or [ALL_PEERS] * n
    wh = list(whole) if isinstance(whole, (list, tuple)) else [whole] * n

    def body(*refs):
        src, land = refs[:n], refs[n:2 * n]
        snd, rcv = refs[2 * n:3 * n], refs[3 * n:4 * n]
        x, y, c = _place()
        for a in range(n):
            for r in peers[a]:
                p = _peer(x, y, c, r)
                ps = _slot(*p)
                cp = pltpu.make_async_remote_copy(
                    src_ref=src[a] if wh[a] else src[a].at[ps], dst_ref=land[a].at[ps],
                    send_sem=snd[a].at[r - 1], recv_sem=rcv[a].at[r - 1], device_id=p, device_id_type=MESH)
                cp.wait_send()
                cp.wait_recv()

    out = pl.pallas_call(
        body, name=name,
        out_shape=[pltpu.HBM(s.shape, s.dtype) for s in srcs] + [pltpu.HBM(s.shape, s.dtype) for s in lands],
        in_specs=[HBM_SPEC] * (2 * n) + [SEM_SPEC] * (2 * n) + [ANY_SPEC],
        out_specs=[HBM_SPEC] * (2 * n),
        input_output_aliases={i: i for i in range(2 * n)},
        compiler_params=pltpu.CompilerParams(has_side_effects=DATAFLOW),
    )(*srcs, *lands, *send, *recv, after)
    return (out[:n], out[n:]) if with_srcs else out[n:]


def _forward_start(name, lands):
    n = len(lands)

    def body(*refs):
        land = refs[:n]
        send, recv = refs[n:2 * n], refs[2 * n:3 * n]
        token = refs[4 * n]
        x, y, c = _place()
        for a in range(n):
            for r in SAME_CORE:
                blk = land[a].at[_slot(*_peer(x, y, c, r))]
                pltpu.make_async_remote_copy(
                    src_ref=blk, dst_ref=blk, send_sem=send[a].at[r - 1], recv_sem=recv[a].at[r - 1],
                    device_id=(x, y, 1 - c), device_id_type=MESH).start()
        token[...] = jnp.zeros_like(token)

    out = pl.pallas_call(
        body, name=name,
        out_shape=([pltpu.SemaphoreType.DMA((N_PEER,))] * (2 * n) + [pltpu.HBM(s.shape, s.dtype) for s in lands]
                   + [S((8, LANES), F32)]),
        in_specs=[HBM_SPEC] * n,
        out_specs=[SEM_SPEC] * (2 * n) + [HBM_SPEC] * n + [pl.BlockSpec(memory_space=pltpu.VMEM)],
        input_output_aliases={i: 2 * n + i for i in range(n)},
        compiler_params=pltpu.CompilerParams(has_side_effects=DATAFLOW),
    )(*[_hbm(s) for s in lands])
    return out[:n], out[n:2 * n], out[2 * n:3 * n], out[3 * n]


def _forward_wait(name, lands, send, recv, after):
    n = len(lands)

    def body(*refs):
        land = refs[:n]
        snd, rcv = refs[n:2 * n], refs[2 * n:3 * n]
        x, y, c = _place()
        for a in range(n):
            for r in SAME_CORE:
                cp = pltpu.make_async_remote_copy(
                    src_ref=land[a].at[_slot(*_peer(x, y, c, r))], dst_ref=land[a].at[_slot(*_peer(x, y, c, r | 1))],
                    send_sem=snd[a].at[r - 1], recv_sem=rcv[a].at[r - 1], device_id=(x, y, 1 - c),
                    device_id_type=MESH)
                cp.wait_send()
                cp.wait_recv()

    return pl.pallas_call(
        body, name=name,
        out_shape=[pltpu.HBM(s.shape, s.dtype) for s in lands],
        in_specs=[HBM_SPEC] * n + [SEM_SPEC] * (2 * n) + [ANY_SPEC],
        out_specs=[HBM_SPEC] * n,
        input_output_aliases={i: i for i in range(n)},
        compiler_params=pltpu.CompilerParams(has_side_effects=DATAFLOW),
    )(*lands, *send, *recv, after)


def _adamw(w, g, m, v):
    m2 = ADAM_B1 * m + (1.0 - ADAM_B1) * g
    v2 = ADAM_B2 * v + (1.0 - ADAM_B2) * (g * g)
    m_hat = m2 / (1.0 - ADAM_B1 ** ADAM_STEP)
    v_hat = v2 / (1.0 - ADAM_B2 ** ADAM_STEP)
    delta = -ADAM_LR * (m_hat / (jnp.sqrt(v_hat) + ADAM_EPS) + ADAM_WD * w)
    return delta, m2, v2


def _adamw_big(name, slots, w, m, v, own=None):
    R, C = w.shape
    tr = next((t for t in (128, 176, 64) if R % t == 0 and R // t >= 2), R)

    def finish(g, w_ref, m_ref, v_ref, g_ref, d_ref, m2_ref, v2_ref):
        d, m2, v2 = _adamw(w_ref[...], g, m_ref[...], v_ref[...])
        g_ref[...] = g
        d_ref[...] = d
        m2_ref[...] = m2
        v2_ref[...] = v2

    if own is None:
        def body(s_ref, *refs):
            g = s_ref[0].astype(F32)
            for k in range(1, N_DEV):
                g = g + s_ref[k].astype(F32)
            finish(g, *refs)

        row = pl.BlockSpec((tr, C), lambda i: (i, 0))
        return pl.pallas_call(
            body, name=name, grid=(R // tr,),
            in_specs=[pl.BlockSpec((N_DEV, tr, C), lambda i: (0, i, 0)), row, row, row],
            out_specs=[row] * 4, out_shape=[S((R, C), F32)] * 4,
            compiler_params=_cp(1))(slots, w, m, v)

    def body(my_ref, s_ref, own_ref, *refs):
        mine = own_ref[...]
        g = None
        for k in range(N_DEV):
            part = jnp.where(my_ref[0] == k, mine, s_ref[k]).astype(F32)
            g = part if g is None else g + part
        finish(g, *refs)

    row = pl.BlockSpec((tr, C), lambda i, my_ref: (i, 0))
    my = jnp.reshape(_slot(*_place()), (1,)).astype(jnp.int32)
    return pl.pallas_call(
        body, name=name,
        grid_spec=pltpu.PrefetchScalarGridSpec(
            num_scalar_prefetch=1, grid=(R // tr,),
            in_specs=[pl.BlockSpec((N_DEV, tr, C), lambda i, my_ref: (0, i, 0)),
                      pl.BlockSpec((None, tr, C), lambda i, my_ref: (my_ref[0], i, 0)), row, row, row],
            out_specs=[row] * 4),
        out_shape=[S((R, C), F32)] * 4, compiler_params=_cp(1))(my, slots, own, w, m, v)


TINY_ROWS = (("b_s", 8), ("g_ffn1", 8), ("g_mix", 8), ("g_ca", 8), ("g_mem", 8), ("g_ffn2", 8), ("g_sgu", 4),
             ("g_fox_o", 4), ("g_gmlp_o", 4), ("g_cq", 2), ("g_ck", 2), ("g_q", 1), ("g_k", 1), ("b_f", 1),
             ("loss", 1))
TINY_P = 72


def _tiny_pieces(width):
    return [(j, slice(j * LANES, min((j + 1) * LANES, width))) for j in range(-(-width // LANES))]


def _pack_tiny(grads, sq):
    names = [n for n, _ in TINY_ROWS if n != "loss"]

    def body(*refs):
        ins, sq_ref, o_ref = refs[:len(names)], refs[len(names)], refs[len(names) + 1]
        o_ref[...] = jnp.zeros_like(o_ref)
        at = 0
        for ref, (name, r) in zip(ins, TINY_ROWS):
            if name == "b_s":
                o_ref[at:at + r, :] = ref[...]
            else:
                for j, cols in _tiny_pieces(ref.shape[1]):
                    o_ref[at + j:at + j + 1, 0:cols.stop - cols.start] = ref[:, cols]
            at += r
        o_ref[at:at + 1, :] = sq_ref[0:1, :]

    return pl.pallas_call(body, name="tiny_pack", out_shape=S((TINY_P, LANES), F32))(
        *[grads[n] for n in names], sq)


def _adamw_tiny(slots, w, m, v):
    names = [n for n, _ in TINY_ROWS if n != "loss"]
    k = len(names)

    def body(s_ref, *refs):
        ins, outs, loss_ref = refs[:3 * k], refs[3 * k:7 * k], refs[7 * k]
        g_all = s_ref[0]
        for d in range(1, N_DEV):
            g_all = g_all + s_ref[d]
        at = 0
        for i, (name, r) in enumerate(TINY_ROWS[:k]):
            w_ref, m_ref, v_ref = ins[i], ins[k + i], ins[2 * k + i]
            o = outs[4 * i:4 * i + 4]
            if name == "b_s":
                pieces = [(slice(at, at + r), slice(0, LANES), (slice(None), slice(None)))]
            else:
                pieces = [(slice(at + j, at + j + 1), slice(0, c.stop - c.start), (slice(None), c))
                          for j, c in _tiny_pieces(w_ref.shape[1])]
            for rows, lanes, dst in pieces:
                g = g_all[rows, lanes]
                res = (g,) + _adamw(w_ref[dst], g, m_ref[dst], v_ref[dst])
                for ref, val in zip(o, res):
                    ref[dst] = val
            at += r
        loss_ref[...] = g_all[at:at + 1, :]

    shapes = [S(w[n].shape, F32) for n in names]
    out = pl.pallas_call(
        body, name="adamw_tiny", out_shape=[s for s in shapes for _ in range(4)] + [S((1, LANES), F32)],
    )(slots, *[w[n] for n in names], *[m[n] for n in names], *[v[n] for n in names])
    stores = ({}, {}, {}, {})
    for i, n in enumerate(names):
        for store, t in zip(stores, out[4 * i:4 * i + 4]):
            store[n] = t
    return stores, out[4 * k]


WEIGHTS =('g_ffn1', 'w_ffn1_in', 'w_ffn1_out', 'g_mix', 'w_in', 'b_f', 'g_q', 'g_k', 'g_sgu', 'w_s', 'b_s',
           'g_fox_o', 'g_gmlp_o', 'w_out', 'g_ca', 'g_mem', 'w_cq', 'w_ckv', 'g_cq', 'g_ck', 'w_co', 'g_ffn2',
           'w_ffn2_in', 'w_ffn2_out')
BIG = ('w_ffn1_in', 'w_ffn1_out', 'w_in', 'w_out', 'w_cq', 'w_ckv', 'w_co', 'w_ffn2_in', 'w_ffn2_out')
TRANSPOSED = ('w_ffn1_in', 'w_in', 'w_ffn2_in')
TWO_LEVEL = ('w_ffn1_in', 'w_in')
GATHER_GROUPS = {"ffn1_up": ("w_ffn1_in",), "ffn1_dn": ("w_ffn1_out",), "mix": ("w_in", "w_out"),
                 "ca": ("w_cq", "w_ckv", "w_co"), "ffn2": ("w_ffn2_in", "w_ffn2_out")}
QKV_W = 3 * FOX_W
UV_OFF = QKV_W + FOX_HEADS


def kernel(x, mem, g_ffn1, w_ffn1_in, w_ffn1_out, g_mix, w_in, b_f, g_q, g_k, g_sgu, w_s, b_s, g_fox_o, g_gmlp_o, w_out, g_ca, g_mem, w_cq, w_ckv, g_cq, g_ck, w_co, g_ffn2, w_ffn2_in, w_ffn2_out, loss_target, m_g_ffn1, m_w_ffn1_in, m_w_ffn1_out, m_g_mix, m_w_in, m_b_f, m_g_q, m_g_k, m_g_sgu, m_w_s, m_b_s, m_g_fox_o, m_g_gmlp_o, m_w_out, m_g_ca, m_g_mem, m_w_cq, m_w_ckv, m_g_cq, m_g_ck, m_w_co, m_g_ffn2, m_w_ffn2_in, m_w_ffn2_out, v_g_ffn1, v_w_ffn1_in, v_w_ffn1_out, v_g_mix, v_w_in, v_b_f, v_g_q, v_g_k, v_g_sgu, v_w_s, v_b_s, v_g_fox_o, v_g_gmlp_o, v_w_out, v_g_ca, v_g_mem, v_w_cq, v_w_ckv, v_g_cq, v_g_ck, v_w_co, v_g_ffn2, v_w_ffn2_in, v_w_ffn2_out):
    args = dict(locals())
    w = {n: args[n] for n in WEIGHTS}
    mo = {n: args["m_" + n] for n in WEIGHTS}
    vo = {n: args["v_" + n] for n in WEIGHTS}
    D = D_MODEL

    def local(n, a):
        return a[0].T if n in TRANSPOSED else a[0]

    g_peers = [NEAR_PEERS if n in TWO_LEVEL else ALL_PEERS for n in BIG]
    handles = {}

    def start_gather(name, names, arrays):
        snd, rcv, src, land, token = _copy_start(name, arrays, _place_own(arrays, True), True,
                                                 peers=[g_peers[BIG.index(n)] for n in names])
        handles.update({n: (src[i], land[i], snd[i], rcv[i]) for i, n in enumerate(names)})
        return token

    first = local(BIG[0], w[BIG[0]]).astype(BF)
    fb = first.shape[0]
    token_first = start_gather("gather_start_first", BIG[:1], [first])
    token_rest = start_gather("gather_start_rest", BIG[1:],
                              [(local(n, w[n]) + token_first[0:1, 0:1]).astype(BF) for n in BIG[1:]])

    tiny_names = [n for n, _ in TINY_ROWS if n != "loss"]

    def weights(group, after):
        names = GATHER_GROUPS[group]
        hs = [handles[n] for n in names]
        got = list(_copy_wait("gather_wait_" + group, [h[0] for h in hs], [h[1] for h in hs], [h[2] for h in hs],
                              [h[3] for h in hs], after, True,
                              peers=[g_peers[BIG.index(n)] for n in names]))
        passed = [i for i, n in enumerate(names) if n in TWO_LEVEL]
        if passed:
            f_snd, f_rcv, f_land, f_token = _forward_start("gather_pass_start_" + group, [got[i] for i in passed])
            for i, t in zip(passed, _forward_wait("gather_pass_wait_" + group, f_land, f_snd, f_rcv, f_token)):
                got[i] = t
        got = dict(zip(names, got))
        if group == "ffn1_up":
            return {"wup1": got["w_ffn1_in"].reshape(2, N_FFN_BLK, fb, D)}
        if group == "ffn1_dn":
            return {"wdn1": got["w_ffn1_out"].reshape(N_FFN_BLK, fb, D)}
        if group == "mix":
            full = got["w_in"].reshape(-1, D)
            wz = jnp.concatenate([full[:QKV_W], full[UV_OFF:], full[QKV_W:UV_OFF],
                                  jnp.zeros((LANES - FOX_HEADS, D), BF)], axis=0)
            return {"wz": wz, "wout": got["w_out"].reshape(D, D)}
        if group == "ca":
            return {"wcq": got["w_cq"].reshape(D, D), "wco": got["w_co"].reshape(D, D), "wckv": got["w_ckv"]}
        return {"wup2": got["w_ffn2_in"].reshape(2, N_FFN_BLK, fb, D),
                "wdn2": got["w_ffn2_out"].reshape(N_FFN_BLK, fb, D)}

    flying = {}

    def emit(group, g):
        if group == "w_s":
            flying[group] = g["w_s"].reshape(-1, LANES)
            return None
        if group == "ffn2":
            parts = {"w_ffn2_in": g["wup2"], "w_ffn2_out": g["wdn2"].reshape(N_DEV, -1, D)}
        elif group == "ffn1_dn":
            parts = {"w_ffn1_out": g["wdn1"].reshape(N_DEV, -1, D)}
        elif group == "ffn1_up":
            parts = {"w_ffn1_in": g["wup1"]}
        else:
            gz = g["wz"]
            g_in = jnp.concatenate([gz[:QKV_W], gz[Z_F:Z_F + FOX_HEADS], gz[QKV_W:Z_F]], axis=0)
            parts = {"w_in": g_in.reshape(N_DEV, -1, D).astype(BF),
                     "w_out": g["wout"].reshape(N_DEV, -1, D), "w_cq": g["wcq"].reshape(N_DEV, -1, D),
                     "w_co": g["wco"].reshape(N_DEV, -1, D), "w_ckv": g["wckv"]}
        names = list(parts)
        srcs = [parts[n] for n in names]
        lands = [lax.empty(s.shape, s.dtype) for s in srcs]
        whole = [False] * len(srcs)
        if group == "mid":
            ws_part = flying.pop("w_s")
            names, srcs, whole = names + ["w_s"], srcs + [ws_part], whole + [True]
            lands += _place_own([ws_part], True)
        *copies, token = _copy_start("exchange_start_" + group, srcs, lands, whole)
        flying[group] = (names, copies, whole)
        return token

    small = {n: (w[n][0] if n == "b_s" else w[n]) for n in tiny_names}
    small["w_s"] = w["w_s"][0]

    h1 = _rms_cast("ffn1_norm", x[0], w["g_ffn1"], token_rest)
    sq, dx0, gs = _local_step(x[0], mem[0], loss_target[0], small, weights, emit, h1=h1)

    sm_parts = [_pack_tiny(gs, sq)]
    sm_snd, sm_rcv, sm_src, sm_land, sm_token = _copy_start("tiny_start", sm_parts, _place_own(sm_parts, True), True)

    grad, delta, new_m, new_v = {}, {}, {}, {}

    def update(group, after):
        names, (snd, rcv, srcs, lands), whole = flying[group]
        owns, slots = _copy_wait("exchange_wait_" + group, srcs, lands, snd, rcv, after, whole, with_srcs=True)
        for n, sl, own in zip(names, slots, owns):
            if n == "w_s":
                g, d, m2, v2 = _adamw_big("adamw_w_s", sl, *[a[n].reshape(-1, LANES) for a in (w, mo, vo)])
            else:
                g, d, m2, v2 = _adamw_big("adamw_" + n, sl, local(n, w[n]), local(n, mo[n]), local(n, vo[n]),
                                          own=own)
            grad[n], delta[n], new_m[n], new_v[n] = (
                (t.T if n in TRANSPOSED else t).reshape(w[n].shape) for t in (g, d, m2, v2))
        return d

    last = update("ffn2", sm_token)
    last = update("mid", last)
    last = update("ffn1_dn", last)
    last = update("ffn1_up", last)
    tiny_all, = _copy_wait("tiny_wait", sm_src, sm_land, sm_snd, sm_rcv, last, True)
    stores, loss_row = _adamw_tiny(tiny_all, *[{n: (a[n][0] if n == "b_s" else a[n]) for n in tiny_names}
                                               for a in (w, mo, vo)])
    for store, t in zip((grad, delta, new_m, new_v), stores):
        store.update({n: v.reshape(w[n].shape) for n, v in t.items()})
    loss = loss_row[0, 0] * (0.5 / D)

    return (loss, dx0[None], *[grad[n] for n in WEIGHTS], *[delta[n] for n in WEIGHTS],
            *[new_m[n] for n in WEIGHTS], *[new_v[n] for n in WEIGHTS])
```
